```python
import jax, jax.numpy as jnp
from jax import lax
import numpy as np

D_MODEL = 2048
BATCH = 8
SEQ = 2048
DEPTH = 1

HEAD_DIM = 128
N_HEADS_SB = 8
N_HEADS_FOX = 8
D_SB = N_HEADS_SB * HEAD_DIM
D_FOX = N_HEADS_FOX * HEAD_DIM
D_FF = -(-8 * D_MODEL // (3 * 256)) * 256
Q_BLOCK = 128
RMS_EPS = 1e-6
SPLIT_SIZES = (D_SB, D_SB, D_SB, D_FOX, D_FOX, D_FOX, N_HEADS_FOX, D_MODEL, D_MODEL)
SPLIT_POINTS = (D_SB, 2 * D_SB, 3 * D_SB,
                3 * D_SB + D_FOX, 3 * D_SB + 2 * D_FOX, 3 * D_SB + 3 * D_FOX,
                3 * D_SB + 3 * D_FOX + N_HEADS_FOX,
                3 * D_SB + 3 * D_FOX + N_HEADS_FOX + D_MODEL)
D_IN = 3 * D_SB + 3 * D_FOX + N_HEADS_FOX + 2 * D_MODEL

kernel_name = "hybrid_stickbreak_forgetting_gated_block"


def rms_norm(x, g):
    xf = x.astype(jnp.float32)
    y = xf * lax.rsqrt(jnp.mean(xf * xf, axis=-1, keepdims=True) + RMS_EPS)
    return (y * g.astype(jnp.float32)).astype(x.dtype)


def split_heads(t, n_heads):
    b, s, _ = t.shape
    return t.reshape(b, s, n_heads, HEAD_DIM).transpose(0, 2, 1, 3)


def merge_heads(t):
    b, h, s, d = t.shape
    return t.transpose(0, 2, 1, 3).reshape(b, s, h * d)


def stick_breaking_block(q_blk, k, v, q_start):
    tq, tk = q_blk.shape[2], k.shape[2]
    z = jnp.einsum("bhqd,bhkd->bhqk", q_blk, k).astype(jnp.float32) * (HEAD_DIM ** -0.5)
    t_idx = q_start + jnp.arange(tq)[:, None]
    s_idx = jnp.arange(tk)[None, :]
    mask = s_idx < t_idx
    log_keep = jnp.where(mask, jax.nn.log_sigmoid(-z), 0.0)
    between = lax.cumsum(log_keep, axis=3, reverse=True) - log_keep
    w = jnp.where(mask, jnp.exp(jax.nn.log_sigmoid(z) + between), 0.0)
    return jnp.einsum("bhqk,bhkd->bhqd", w.astype(v.dtype), v)


def forgetting_block(q_blk, k, v, cum_q, cum_k, q_start):
    tq, tk = q_blk.shape[2], k.shape[2]
    logits = jnp.einsum("bhqd,bhkd->bhqk", q_blk, k).astype(jnp.float32) * (HEAD_DIM ** -0.5)
    logits = logits + cum_q[..., :, None] - cum_k[..., None, :]
    t_idx = q_start + jnp.arange(tq)[:, None]
    s_idx = jnp.arange(tk)[None, :]
    logits = jnp.where(s_idx <= t_idx, logits, -jnp.inf)
    p = jax.nn.softmax(logits, axis=-1)
    return jnp.einsum("bhqk,bhkd->bhqd", p.astype(v.dtype), v)


def token_mixer(u, w_in, b_forget, w_branch_sb, w_branch_fox, w_out):
    s = u.shape[1]
    proj = u @ w_in
    q_sb, k_sb, v_sb, q_fx, k_fx, v_fx, f_logit, g_sb, g_fx = jnp.split(proj, SPLIT_POINTS, axis=-1)
    q_sb, k_sb, v_sb = (split_heads(t, N_HEADS_SB) for t in (q_sb, k_sb, v_sb))
    q_fx, k_fx, v_fx = (split_heads(t, N_HEADS_FOX) for t in (q_fx, k_fx, v_fx))
    log_f = jax.nn.log_sigmoid((f_logit + b_forget).astype(jnp.float32))
    cum = lax.cumsum(log_f.transpose(0, 2, 1), axis=2)

    outs_sb, outs_fx = [], []
    for i in range(s // Q_BLOCK):
        q0, q1 = i * Q_BLOCK, (i + 1) * Q_BLOCK
        outs_sb.append(stick_breaking_block(q_sb[:, :, q0:q1], k_sb[:, :, :q1], v_sb[:, :, :q1], q0))
        outs_fx.append(forgetting_block(q_fx[:, :, q0:q1], k_fx[:, :, :q1], v_fx[:, :, :q1],
                                        cum[:, :, q0:q1], cum[:, :, :q1], q0))
    o_sb = merge_heads(jnp.concatenate(outs_sb, axis=2))
    o_fx = merge_heads(jnp.concatenate(outs_fx, axis=2))

    merged = jax.nn.sigmoid(g_sb) * (o_sb @ w_branch_sb) + jax.nn.sigmoid(g_fx) * (o_fx @ w_branch_fox)
    return merged @ w_out


def swiglu(u, w_gate, w_up, w_down):
    return (jax.nn.silu(u @ w_gate) * (u @ w_up)) @ w_down


def _fwd_setup_inputs(seed: int = 0) -> dict:
    key = jax.random.key(seed)
    ks = jax.random.split(key, 14)
    f32 = jnp.float32

    def dense(k, fan_in, fan_out):
        return jax.random.normal(k, (DEPTH, fan_in, fan_out), f32) * (fan_in ** -0.5)

    def gain(k):
        return 1.0 + 0.02 * jax.random.normal(k, (DEPTH, D_MODEL), f32)

    return {
        "x": jax.random.normal(ks[0], (BATCH, SEQ, D_MODEL), f32),
        "norm_mix_pre": gain(ks[1]),
        "norm_mix_post": gain(ks[2]),
        "w_in": dense(ks[3], D_MODEL, D_IN),
        "b_forget": 3.0 + 0.1 * jax.random.normal(ks[4], (DEPTH, N_HEADS_FOX), f32),
        "w_branch_sb": dense(ks[5], D_SB, D_MODEL),
        "w_branch_fox": dense(ks[6], D_FOX, D_MODEL),
        "w_out": dense(ks[7], D_MODEL, D_MODEL),
        "norm_ffn_pre": gain(ks[8]),
        "norm_ffn_post": gain(ks[9]),
        "w_ffn_gate": dense(ks[10], D_MODEL, D_FF),
        "w_ffn_up": dense(ks[11], D_MODEL, D_FF),
        "w_ffn_down": dense(ks[12], D_FF, D_MODEL),
    }


def _fwd_reference(x, norm_mix_pre, norm_mix_post, w_in, b_forget, w_branch_sb, w_branch_fox, w_out,
              norm_ffn_pre, norm_ffn_post, w_ffn_gate, w_ffn_up, w_ffn_down):
    h = x
    for l in range(DEPTH):
        mix = token_mixer(rms_norm(h, norm_mix_pre[l]), w_in[l], b_forget[l],
                          w_branch_sb[l], w_branch_fox[l], w_out[l])
        h = h + rms_norm(mix, norm_mix_post[l])
        ff = swiglu(rms_norm(h, norm_ffn_pre[l]), w_ffn_gate[l], w_ffn_up[l], w_ffn_down[l])
        h = h + rms_norm(ff, norm_ffn_post[l])
    return h


import jax as _jax
import jax.numpy as _jnp

TWIN_FORMAT = 'train_step'
FWD_PARAMS = ['x', 'norm_mix_pre', 'norm_mix_post', 'w_in', 'b_forget', 'w_branch_sb', 'w_branch_fox', 'w_out', 'norm_ffn_pre', 'norm_ffn_post', 'w_ffn_gate', 'w_ffn_up', 'w_ffn_down']
TWIN_WEIGHTS = ['norm_mix_pre', 'norm_mix_post', 'w_in', 'b_forget', 'w_branch_sb', 'w_branch_fox', 'w_out', 'norm_ffn_pre', 'norm_ffn_post', 'w_ffn_gate', 'w_ffn_up', 'w_ffn_down']
TWIN_DIFF_INPUT = 'x'
TWIN_INPUTS = ['x', 'norm_mix_pre', 'norm_mix_post', 'w_in', 'b_forget', 'w_branch_sb', 'w_branch_fox', 'w_out', 'norm_ffn_pre', 'norm_ffn_post', 'w_ffn_gate', 'w_ffn_up', 'w_ffn_down', 'loss_target', 'm_norm_mix_pre', 'm_norm_mix_post', 'm_w_in', 'm_b_forget', 'm_w_branch_sb', 'm_w_branch_fox', 'm_w_out', 'm_norm_ffn_pre', 'm_norm_ffn_post', 'm_w_ffn_gate', 'm_w_ffn_up', 'm_w_ffn_down', 'v_norm_mix_pre', 'v_norm_mix_post', 'v_w_in', 'v_b_forget', 'v_w_branch_sb', 'v_w_branch_fox', 'v_w_out', 'v_norm_ffn_pre', 'v_norm_ffn_post', 'v_w_ffn_gate', 'v_w_ffn_up', 'v_w_ffn_down']
TWIN_OUTPUTS = ['loss', 'grad_x', 'grad_norm_mix_pre', 'grad_norm_mix_post', 'grad_w_in', 'grad_b_forget', 'grad_w_branch_sb', 'grad_w_branch_fox', 'grad_w_out', 'grad_norm_ffn_pre', 'grad_norm_ffn_post', 'grad_w_ffn_gate', 'grad_w_ffn_up', 'grad_w_ffn_down', 'delta_norm_mix_pre', 'delta_norm_mix_post', 'delta_w_in', 'delta_b_forget', 'delta_w_branch_sb', 'delta_w_branch_fox', 'delta_w_out', 'delta_norm_ffn_pre', 'delta_norm_ffn_post', 'delta_w_ffn_gate', 'delta_w_ffn_up', 'delta_w_ffn_down', 'new_m_norm_mix_pre', 'new_m_norm_mix_post', 'new_m_w_in', 'new_m_b_forget', 'new_m_w_branch_sb', 'new_m_w_branch_fox', 'new_m_w_out', 'new_m_norm_ffn_pre', 'new_m_norm_ffn_post', 'new_m_w_ffn_gate', 'new_m_w_ffn_up', 'new_m_w_ffn_down', 'new_v_norm_mix_pre', 'new_v_norm_mix_post', 'new_v_w_in', 'new_v_b_forget', 'new_v_w_branch_sb', 'new_v_w_branch_fox', 'new_v_w_out', 'new_v_norm_ffn_pre', 'new_v_norm_ffn_post', 'new_v_w_ffn_gate', 'new_v_w_ffn_up', 'new_v_w_ffn_down']
TWIN_LEAF_KINDS = {'loss': 'loss', 'grad_x': 'grad_x', 'grad_norm_mix_pre': 'grad_w', 'grad_norm_mix_post': 'grad_w', 'grad_w_in': 'grad_w', 'grad_b_forget': 'grad_w', 'grad_w_branch_sb': 'grad_w', 'grad_w_branch_fox': 'grad_w', 'grad_w_out': 'grad_w', 'grad_norm_ffn_pre': 'grad_w', 'grad_norm_ffn_post': 'grad_w', 'grad_w_ffn_gate': 'grad_w', 'grad_w_ffn_up': 'grad_w', 'grad_w_ffn_down': 'grad_w', 'delta_norm_mix_pre': 'delta_w', 'delta_norm_mix_post': 'delta_w', 'delta_w_in': 'delta_w', 'delta_b_forget': 'delta_w', 'delta_w_branch_sb': 'delta_w', 'delta_w_branch_fox': 'delta_w', 'delta_w_out': 'delta_w', 'delta_norm_ffn_pre': 'delta_w', 'delta_norm_ffn_post': 'delta_w', 'delta_w_ffn_gate': 'delta_w', 'delta_w_ffn_up': 'delta_w', 'delta_w_ffn_down': 'delta_w', 'new_m_norm_mix_pre': 'new_m', 'new_m_norm_mix_post': 'new_m', 'new_m_w_in': 'new_m', 'new_m_b_forget': 'new_m', 'new_m_w_branch_sb': 'new_m', 'new_m_w_branch_fox': 'new_m', 'new_m_w_out': 'new_m', 'new_m_norm_ffn_pre': 'new_m', 'new_m_norm_ffn_post': 'new_m', 'new_m_w_ffn_gate': 'new_m', 'new_m_w_ffn_up': 'new_m', 'new_m_w_ffn_down': 'new_m', 'new_v_norm_mix_pre': 'new_v', 'new_v_norm_mix_post': 'new_v', 'new_v_w_in': 'new_v', 'new_v_b_forget': 'new_v', 'new_v_w_branch_sb': 'new_v', 'new_v_w_branch_fox': 'new_v', 'new_v_w_out': 'new_v', 'new_v_norm_ffn_pre': 'new_v', 'new_v_norm_ffn_post': 'new_v', 'new_v_w_ffn_gate': 'new_v', 'new_v_w_ffn_up': 'new_v', 'new_v_w_ffn_down': 'new_v'}


def _forward(args):
    return _fwd_reference(*[args[k] for k in FWD_PARAMS])


def _output_shape():
    out = _jax.eval_shape(lambda: _forward(_fwd_setup_inputs(0)))
    return out.shape, out.dtype

N_MICROBATCH = 1
ADAM_LR = 0.001
ADAM_B1 = 0.9
ADAM_B2 = 0.999
ADAM_EPS = 1e-08
ADAM_WD = 0.01
ADAM_STEP = 10
PER_EXAMPLE_BATCH_AXIS = {'x': 0, 'loss_target': 0}
SHARED_INPUTS = []
_WEIGHT_DTYPES = {'norm_mix_pre': _jnp.float32, 'norm_mix_post': _jnp.float32, 'w_in': _jnp.float32, 'b_forget': _jnp.float32, 'w_branch_sb': _jnp.float32, 'w_branch_fox': _jnp.float32, 'w_out': _jnp.float32, 'norm_ffn_pre': _jnp.float32, 'norm_ffn_post': _jnp.float32, 'w_ffn_gate': _jnp.float32, 'w_ffn_up': _jnp.float32, 'w_ffn_down': _jnp.float32}
MOMENT_SCALE = {'norm_mix_pre': 2.138483e-01, 'norm_mix_post': 8.024681e+00, 'w_in': 9.344925e-02, 'b_forget': 4.924917e-01, 'w_branch_sb': 1.515454e-01, 'w_branch_fox': 6.821900e-02, 'w_out': 1.697844e-01, 'norm_ffn_pre': 1.705905e-01, 'norm_ffn_post': 7.986423e+00, 'w_ffn_gate': 6.805894e-02, 'w_ffn_up': 8.039049e-02, 'w_ffn_down': 1.330867e-01}


def _to_microbatches(a, axis):
    t = _jnp.moveaxis(a, axis, 0)
    t = t.reshape((N_MICROBATCH, t.shape[0] // N_MICROBATCH) + t.shape[1:])
    return _jnp.moveaxis(t, 1, axis + 1)


def setup_inputs(seed: int = 0) -> dict:
    inp = _fwd_setup_inputs(seed)
    key = _jax.random.fold_in(_jax.random.key(seed), 7919)
    shape, _ = _output_shape()
    out = dict(inp)
    out["loss_target"] = _jax.random.normal(_jax.random.fold_in(key, 0), shape, _jnp.float32)
    for i, name in enumerate(TWIN_WEIGHTS):
        w = inp[name].astype(_jnp.float32)
        if MOMENT_SCALE is None:
            s = _jnp.sqrt(_jnp.mean(_jnp.square(w)) + 1e-30)
        else:
            s = MOMENT_SCALE[name]
        km, kv = _jax.random.split(_jax.random.fold_in(key, i + 1))
        out[name] = w
        out["m_" + name] = s * _jax.random.normal(km, w.shape, _jnp.float32)
        out["v_" + name] = (s * s) * _jax.random.uniform(kv, w.shape, _jnp.float32, 0.5, 1.5)
    if N_MICROBATCH > 1:
        for name, axis in PER_EXAMPLE_BATCH_AXIS.items():
            out[name] = _to_microbatches(out[name], axis)
    return {'x': out['x'], 'norm_mix_pre': out['norm_mix_pre'], 'norm_mix_post': out['norm_mix_post'], 'w_in': out['w_in'], 'b_forget': out['b_forget'], 'w_branch_sb': out['w_branch_sb'], 'w_branch_fox': out['w_branch_fox'], 'w_out': out['w_out'], 'norm_ffn_pre': out['norm_ffn_pre'], 'norm_ffn_post': out['norm_ffn_post'], 'w_ffn_gate': out['w_ffn_gate'], 'w_ffn_up': out['w_ffn_up'], 'w_ffn_down': out['w_ffn_down'], 'loss_target': out['loss_target'], 'm_norm_mix_pre': out['m_norm_mix_pre'], 'm_norm_mix_post': out['m_norm_mix_post'], 'm_w_in': out['m_w_in'], 'm_b_forget': out['m_b_forget'], 'm_w_branch_sb': out['m_w_branch_sb'], 'm_w_branch_fox': out['m_w_branch_fox'], 'm_w_out': out['m_w_out'], 'm_norm_ffn_pre': out['m_norm_ffn_pre'], 'm_norm_ffn_post': out['m_norm_ffn_post'], 'm_w_ffn_gate': out['m_w_ffn_gate'], 'm_w_ffn_up': out['m_w_ffn_up'], 'm_w_ffn_down': out['m_w_ffn_down'], 'v_norm_mix_pre': out['v_norm_mix_pre'], 'v_norm_mix_post': out['v_norm_mix_post'], 'v_w_in': out['v_w_in'], 'v_b_forget': out['v_b_forget'], 'v_w_branch_sb': out['v_w_branch_sb'], 'v_w_branch_fox': out['v_w_branch_fox'], 'v_w_out': out['v_w_out'], 'v_norm_ffn_pre': out['v_norm_ffn_pre'], 'v_norm_ffn_post': out['v_norm_ffn_post'], 'v_w_ffn_gate': out['v_w_ffn_gate'], 'v_w_ffn_up': out['v_w_ffn_up'], 'v_w_ffn_down': out['v_w_ffn_down']}


def _loss(weights, diff, rest, loss_target):
    with _jax.named_scope("forward"):
        args = {**rest, TWIN_DIFF_INPUT: diff, **{k: w.astype(_WEIGHT_DTYPES[k]) for k, w in weights.items()}}
        y = _forward(args)
    with _jax.named_scope("loss_head"):
        err = _jnp.square(y.astype(_jnp.float32) - loss_target)
        return 0.5 * _jnp.sum(_jnp.mean(err, axis=-1)) if err.ndim else 0.5 * err


def _adamw(w, g, m, v):
    m = ADAM_B1 * m + (1.0 - ADAM_B1) * g
    v = ADAM_B2 * v + (1.0 - ADAM_B2) * _jnp.square(g)
    m_hat = m / (1.0 - ADAM_B1 ** ADAM_STEP)
    v_hat = v / (1.0 - ADAM_B2 ** ADAM_STEP)
    delta = -ADAM_LR * (m_hat / (_jnp.sqrt(v_hat) + ADAM_EPS) + ADAM_WD * w)
    return delta, m, v


def reference(x, norm_mix_pre, norm_mix_post, w_in, b_forget, w_branch_sb, w_branch_fox, w_out, norm_ffn_pre, norm_ffn_post, w_ffn_gate, w_ffn_up, w_ffn_down, loss_target, m_norm_mix_pre, m_norm_mix_post, m_w_in, m_b_forget, m_w_branch_sb, m_w_branch_fox, m_w_out, m_norm_ffn_pre, m_norm_ffn_post, m_w_ffn_gate, m_w_ffn_up, m_w_ffn_down, v_norm_mix_pre, v_norm_mix_post, v_w_in, v_b_forget, v_w_branch_sb, v_w_branch_fox, v_w_out, v_norm_ffn_pre, v_norm_ffn_post, v_w_ffn_gate, v_w_ffn_up, v_w_ffn_down):
    given = dict(x=x, norm_mix_pre=norm_mix_pre, norm_mix_post=norm_mix_post, w_in=w_in, b_forget=b_forget, w_branch_sb=w_branch_sb, w_branch_fox=w_branch_fox, w_out=w_out, norm_ffn_pre=norm_ffn_pre, norm_ffn_post=norm_ffn_post, w_ffn_gate=w_ffn_gate, w_ffn_up=w_ffn_up, w_ffn_down=w_ffn_down, loss_target=loss_target, m_norm_mix_pre=m_norm_mix_pre, m_norm_mix_post=m_norm_mix_post, m_w_in=m_w_in, m_b_forget=m_b_forget, m_w_branch_sb=m_w_branch_sb, m_w_branch_fox=m_w_branch_fox, m_w_out=m_w_out, m_norm_ffn_pre=m_norm_ffn_pre, m_norm_ffn_post=m_norm_ffn_post, m_w_ffn_gate=m_w_ffn_gate, m_w_ffn_up=m_w_ffn_up, m_w_ffn_down=m_w_ffn_down, v_norm_mix_pre=v_norm_mix_pre, v_norm_mix_post=v_norm_mix_post, v_w_in=v_w_in, v_b_forget=v_b_forget, v_w_branch_sb=v_w_branch_sb, v_w_branch_fox=v_w_branch_fox, v_w_out=v_w_out, v_norm_ffn_pre=v_norm_ffn_pre, v_norm_ffn_post=v_norm_ffn_post, v_w_ffn_gate=v_w_ffn_gate, v_w_ffn_up=v_w_ffn_up, v_w_ffn_down=v_w_ffn_down)
    weights = {n: given[n] for n in TWIN_WEIGHTS}
    shared = {n: given[n] for n in SHARED_INPUTS}
    per_example = {n: given[n] for n in ['x']}
    grad_fn = _jax.value_and_grad(_loss, argnums=(0, 1))

    def one_microbatch(ex, loss_target):
        ex = dict(ex)
        diff = ex.pop(TWIN_DIFF_INPUT)
        return grad_fn(weights, diff, {**shared, **ex}, loss_target)

    if N_MICROBATCH == 1:
        loss, (grad_w, grad_x) = one_microbatch(per_example, given["loss_target"])
    else:
        def body(carry, xs):
            loss_sum, grad_sum = carry
            l_k, (gw_k, gx_k) = one_microbatch(xs[0], xs[1])
            with _jax.named_scope("update"):
                return (loss_sum + l_k, _jax.tree.map(_jnp.add, grad_sum, gw_k)), gx_k

        init = (_jnp.zeros((), _jnp.float32), _jax.tree.map(_jnp.zeros_like, weights))
        (loss, grad_w), grad_x = _jax.lax.scan(body, init, (per_example, given["loss_target"]))
    with _jax.named_scope("update"):
        delta_w, new_m, new_v = {}, {}, {}
        for n in TWIN_WEIGHTS:
            delta_w[n], new_m[n], new_v[n] = _adamw(weights[n], grad_w[n], given["m_" + n], given["v_" + n])
    return (loss, grad_x, *[grad_w[n] for n in TWIN_WEIGHTS], *[delta_w[n] for n in TWIN_WEIGHTS],
            *[new_m[n] for n in TWIN_WEIGHTS], *[new_v[n] for n in TWIN_WEIGHTS])
```

```python
import jax
import jax.numpy as jnp
from jax import lax
from jax.experimental import pallas as pl
from jax.experimental.pallas import tpu as pltpu

F32 = jnp.float32
BF16 = jnp.bfloat16
MESH = pl.DeviceIdType.MESH
ANY = pl.BlockSpec(memory_space=pl.ANY)

N_DEV = 8
HEAD_DIM = 128
RMS_EPS = 1e-6
F_PAD = 512
LANES = 128
ATT_TQ = 256
ATT_TK = 256
NEG_BIG = -1e30
VMEM_LIMIT = 56 * 1024 * 1024

ADAM_LR = 0.001
ADAM_B1 = 0.9
ADAM_B2 = 0.999
ADAM_EPS = 1e-08
ADAM_WD = 0.01
ADAM_STEP = 10

_DIMS = {"nn": ((1,), (0,)), "nt": ((1,), (1,)), "tn": ((0,), (0,))}


def _params(sem):
    return pltpu.CompilerParams(dimension_semantics=sem, vmem_limit_bytes=VMEM_LIMIT)


def _dot(a, b, mode="nn"):
    return lax.dot_general(a.astype(BF16), b.astype(BF16), (_DIMS[mode], ((), ())), preferred_element_type=F32)


def _tile(n, pref):
    if n <= pref:
        return n
    t = (pref // LANES) * LANES
    while n % t:
        t -= LANES
    return t


def _split2(v):
    hi = v.astype(BF16)
    return hi, (v - hi.astype(F32)).astype(BF16)


def _split3(v):
    a = v.astype(BF16)
    r = v - a.astype(F32)
    b = r.astype(BF16)
    return a, b, (r - b.astype(F32)).astype(BF16)


def _tri(n, cmp):
    r = lax.broadcasted_iota(jnp.int32, (n, n), 0)
    c = lax.broadcasted_iota(jnp.int32, (n, n), 1)
    return jnp.where(cmp(r, c), 1.0, 0.0).astype(BF16)


def _lane_pick(v, h):
    lane = lax.broadcasted_iota(jnp.int32, v.shape, 1)
    return jnp.sum(jnp.where(lane == h, v, 0.0), axis=1, keepdims=True)


def _lane_put(ref, rows, h, col):
    old = ref[rows, :]
    lane = lax.broadcasted_iota(jnp.int32, old.shape, 1)
    ref[rows, :] = jnp.where(lane == h, col, old)


def _sigmoid(z):
    return 1.0 / (1.0 + jnp.exp(-z))


def _log_sigmoid(z):
    return jnp.minimum(z, 0.0) - jnp.log(1.0 + jnp.exp(-jnp.abs(z)))


def _sds(shape, dtype):
    return jax.ShapeDtypeStruct(shape, dtype)


def _matmul(name, mode, pairs, grid, acc_shape, out_shape, out_specs, extras=(), epilogue=None, init=None):
    n_p, n_e = len(pairs), len(extras)
    nk = grid[-1]
    single = not isinstance(out_shape, (list, tuple))
    n_i = 0 if init is None else 1

    def body(*refs):
        ab = refs[:2 * n_p]
        ex = refs[2 * n_p:2 * n_p + n_e]
        ini = refs[2 * n_p + n_e:2 * n_p + n_e + n_i]
        outs = refs[2 * n_p + n_e + n_i:-1]
        acc = refs[-1]
        k = pl.program_id(len(grid) - 1)

        @pl.when(k == 0)
        def _():
            acc[...] = jnp.zeros_like(acc) if init is None else ini[0][...].astype(F32)

        t = _dot(ab[0][...], ab[1][...], mode)
        for p in range(1, n_p):
            t = t + _dot(ab[2 * p][...], ab[2 * p + 1][...], mode)
        acc[...] += t

        @pl.when(k == nk - 1)
        def _():
            if epilogue is None:
                outs[0][...] = acc[...].astype(outs[0].dtype)
            else:
                epilogue(acc[...], ex, outs)

    in_specs = [s for (_, sa, _, sb) in pairs for s in (sa, sb)] + [s for (_, s) in extras]
    args = [v for (a, _, b, _) in pairs for v in (a, b)] + [e for (e, _) in extras]
    if init is not None:
        in_specs.append(init[1])
        args.append(init[0])
    return pl.pallas_call(
        body, name=name, grid=grid, in_specs=in_specs,
        out_specs=out_specs if single else list(out_specs),
        out_shape=out_shape if single else list(out_shape),
        scratch_shapes=[pltpu.VMEM(acc_shape, F32)],
        compiler_params=_params(("parallel",) * (len(grid) - 1) + ("arbitrary",)),
    )(*args)


def _mm_plain(name, mode, a, b, out_dtype, *, n_off=0, n=None, k_off=0, tm=1024, tn=512, tk=512, init=None):
    if mode == "nn":
        (m, kk), nn_ = a.shape, b.shape[1]
    elif mode == "nt":
        (m, kk), nn_ = a.shape, b.shape[0]
    else:
        (kk, m), nn_ = a.shape, b.shape[1]
    n = nn_ if n is None else n
    tm, tn, tk = _tile(m, tm), _tile(n, tn), _tile(kk, tk)
    assert n_off % tn == 0 and k_off % tk == 0
    off, koff = n_off // tn, k_off // tk
    a_spec = {"nn": pl.BlockSpec((tm, tk), lambda i, j, k: (i, k)),
              "nt": pl.BlockSpec((tm, tk), lambda i, j, k: (i, k)),
              "tn": pl.BlockSpec((tk, tm), lambda i, j, k: (k, i))}[mode]
    b_spec = {"nn": pl.BlockSpec((tk, tn), lambda i, j, k: (k, j + off)),
              "nt": pl.BlockSpec((tn, tk), lambda i, j, k: (j, k + koff)),
              "tn": pl.BlockSpec((tk, tn), lambda i, j, k: (k, j))}[mode]
    o_spec = pl.BlockSpec((tm, tn), lambda i, j, k: (i, j))
    if init is not None:
        init = (init, o_spec)
    return _matmul(name, mode, [(a, a_spec, b, b_spec)], (m // tm, n // tn, kk // tk), (tm, tn),
                   _sds((m, n), out_dtype), o_spec, init=init)


def _rows_call(name, body, ins, outs, s, tr=256):
    def spec(v, per_row):
        if per_row:
            return pl.BlockSpec((tr, v.shape[1]), lambda i: (i, 0))
        return pl.BlockSpec(v.shape, lambda i: (0, 0))
    return pl.pallas_call(
        body, name=name, grid=(s // tr,),
        in_specs=[spec(v, p) for v, p in ins], out_specs=[spec(v, p) for v, p in outs],
        out_shape=[_sds(v.shape, v.dtype) for v, _ in outs],
        compiler_params=_params(("arbitrary",)),
    )(*[v for v, _ in ins])


def _rsq(v):
    return lax.rsqrt(jnp.mean(v * v, axis=-1, keepdims=True) + RMS_EPS)


def _norm_bwd(dy, v, r, g):
    vh = v * r
    t = dy * g
    dv = r * (t - vh * jnp.mean(t * vh, axis=-1, keepdims=True))
    return dv, jnp.sum(dy * vh, axis=0, keepdims=True)


def _accum(ref, val):
    @pl.when(pl.program_id(0) == 0)
    def _():
        ref[...] = jnp.zeros_like(ref)
    ref[...] += val


def _pre_norm(x, g):
    def body(x_ref, g_ref, u_ref):
        v = x_ref[...]
        u_ref[...] = (v * _rsq(v) * g_ref[...]).astype(BF16)
    s, d = x.shape
    return _rows_call("pre_norm", body, [(x, True), (g, False)], [(_sds((s, d), BF16), True)], s)[0]


def _mid_norms(x, mix, g_post, g_pre):
    def body(x_ref, mix_ref, gp_ref, gn_ref, h_ref, u_ref):
        mv = mix_ref[...]
        h = x_ref[...] + mv * _rsq(mv) * gp_ref[...]
        h_ref[...] = h
        u_ref[...] = (h * _rsq(h) * gn_ref[...]).astype(BF16)
    s, d = x.shape
    return _rows_call("mid_norms", body, [(x, True), (mix, True), (g_post, False), (g_pre, False)],
                      [(_sds((s, d), F32), True), (_sds((s, d), BF16), True)], s)


def _loss_head(h1, ff, target, g):
    s, d = h1.shape

    def body(h_ref, ff_ref, t_ref, g_ref, loss_ref, dy_ref, dff_ref, dg_ref):
        fv = ff_ref[...]
        r = _rsq(fv)
        err = h_ref[...] + fv * r * g_ref[...] - t_ref[...]
        part = 0.5 * jnp.sum(jnp.mean(err * err, axis=-1, keepdims=True), axis=0, keepdims=True)
        _accum(loss_ref, jnp.broadcast_to(part, loss_ref.shape))
        dy = err * (1.0 / d)
        dy_ref[...] = dy
        dff, dg = _norm_bwd(dy, fv, r, g_ref[...])
        dff_ref[...] = dff.astype(BF16)
        _accum(dg_ref, dg)

    return _rows_call("loss_head", body, [(h1, True), (ff, True), (target, True), (g, False)],
                      [(_sds((1, LANES), F32), False), (_sds((s, d), F32), True),
                       (_sds((s, d), BF16), True), (_sds((1, d), F32), False)], s)


def _mid_norms_bwd(dy, du2, h1, mix, g_pre, g_post):
    s, d = dy.shape

    def body(dy_ref, du_ref, h_ref, mix_ref, gn_ref, gp_ref, dh_ref, dmix_ref, dgn_ref, dgp_ref):
        h = h_ref[...]
        dh, dgn = _norm_bwd(du_ref[...], h, _rsq(h), gn_ref[...])
        dh = dh + dy_ref[...]
        dh_ref[...] = dh
        _accum(dgn_ref, dgn)
        mv = mix_ref[...]
        dmix, dgp = _norm_bwd(dh, mv, _rsq(mv), gp_ref[...])
        dmix_ref[...] = dmix.astype(BF16)
        _accum(dgp_ref, dgp)

    return _rows_call("mid_norms_bwd", body,
                      [(dy, True), (du2, True), (h1, True), (mix, True), (g_pre, False), (g_post, False)],
                      [(_sds((s, d), F32), True), (_sds((s, d), BF16), True),
                       (_sds((1, d), F32), False), (_sds((1, d), F32), False)], s)


def _pre_norm_bwd(dh1, du, x, g):
    s, d = x.shape

    def body(dh_ref, du_ref, x_ref, g_ref, dx_ref, dg_ref):
        v = x_ref[...]
        dv, dg = _norm_bwd(du_ref[...], v, _rsq(v), g_ref[...])
        dx_ref[...] = dh_ref[...] + dv
        _accum(dg_ref, dg)

    return _rows_call("pre_norm_bwd", body, [(dh1, True), (du, True), (x, True), (g, False)],
                      [(_sds((s, d), F32), True), (_sds((1, d), F32), False)], s)


def _forget_fwd(gf, b_pad, f_blk):
    s = gf.shape[0]
    tb = ATT_TK
    nb = s // tb

    def body(f_ref, b_ref, col_ref, row_ref):
        incl = _tri(tb, lambda r, c: c <= r)
        carry = jnp.zeros((1, LANES), F32)
        for i in range(nb):
            lf = _log_sigmoid(f_ref[pl.ds(i * tb, tb), :] + b_ref[...])
            parts = _split3(lf)
            cum = carry + _dot(incl, parts[0]) + _dot(incl, parts[1]) + _dot(incl, parts[2])
            col_ref[pl.ds(i * tb, tb), :] = cum
            row_ref[i] = cum.T
            carry = carry + jnp.sum(lf, axis=0, keepdims=True)

    return pl.pallas_call(
        body, name="forget_fwd", grid=(1,),
        in_specs=[pl.BlockSpec((s, LANES), lambda i: (0, f_blk)), pl.BlockSpec((1, LANES), lambda i: (0, 0))],
        out_specs=[pl.BlockSpec((s, LANES), lambda i: (0, 0)), pl.BlockSpec((nb, LANES, tb), lambda i: (0, 0, 0))],
        out_shape=[_sds((s, LANES), F32), _sds((nb, LANES, tb), F32)],
        compiler_params=_params(("arbitrary",)),
    )(gf, b_pad)


def _forget_bwd(dgf, dcum, gf, b_pad, f_blk):
    s = gf.shape[0]
    tb = ATT_TK
    nb = s // tb
    sec = dgf.shape[1] // F_PAD - 1

    def body(dgf_hbm, dc_ref, f_ref, b_ref, out_ref, db_ref):
        del dgf_hbm
        incl = _tri(tb, lambda r, c: c >= r)
        carry = jnp.zeros((1, LANES), F32)
        db = jnp.zeros((1, LANES), F32)
        out_ref[...] = jnp.zeros_like(out_ref)
        for i in reversed(range(nb)):
            dc = dc_ref[pl.ds(i * tb, tb), :]
            parts = _split3(dc)
            dlf = carry + _dot(incl, parts[0]) + _dot(incl, parts[1]) + _dot(incl, parts[2])
            z = f_ref[pl.ds(i * tb, tb), :] + b_ref[...]
            df = dlf * _sigmoid(-z)
            out_ref[pl.ds(i * tb, tb), pl.ds(0, LANES)] = df.astype(BF16)
            db = db + jnp.sum(df, axis=0, keepdims=True)
            carry = carry + jnp.sum(dc, axis=0, keepdims=True)
        db_ref[...] = db

    return pl.pallas_call(
        body, name="forget_bwd", grid=(1,),
        in_specs=[ANY, pl.BlockSpec((s, LANES), lambda i: (0, 0)),
                  pl.BlockSpec((s, LANES), lambda i: (0, f_blk)), pl.BlockSpec((1, LANES), lambda i: (0, 0))],
        out_specs=[pl.BlockSpec((s, F_PAD), lambda i: (0, sec)), pl.BlockSpec((1, LANES), lambda i: (0, 0))],
        out_shape=[_sds(dgf.shape, BF16), _sds((1, LANES), F32)],
        input_output_aliases={0: 0},
        compiler_params=_params(("arbitrary",)),
    )(dgf, dcum, gf, b_pad)


def _rel_index():
    r = lax.broadcasted_iota(jnp.int32, (ATT_TQ, ATT_TK), 0)
    c = lax.broadcasted_iota(jnp.int32, (ATT_TQ, ATT_TK), 1)
    return r - c


def _qkv_specs(hb0, s):
    return [pl.BlockSpec((ATT_TQ, HEAD_DIM), lambda h, i: (i, 3 * (hb0 + h))),
            pl.BlockSpec((s, HEAD_DIM), lambda h, i: (0, 3 * (hb0 + h) + 1)),
            pl.BlockSpec((s, HEAD_DIM), lambda h, i: (0, 3 * (hb0 + h) + 2))]


def _sb_fwd(qkv, n_heads):
    s = qkv.shape[0]
    scale = HEAD_DIM ** -0.5
    tq, tk = ATT_TQ, ATT_TK

    def body(q_ref, k_ref, v_ref, o_ref, tot_ref):
        h, i = pl.program_id(0), pl.program_id(1)

        @pl.when((h == 0) & (i == 0))
        def _():
            tot_ref[...] = jnp.zeros_like(tot_ref)

        q = q_ref[...]
        rel = _rel_index()
        upper = _tri(tk, lambda r, c: r > c)

        def step(n, carry):
            c, acc = carry
            kj = i - n
            rows = pl.ds(pl.multiple_of(kj * tk, tk), tk)
            z = _dot(q, k_ref[rows, :], "nt") * scale
            mask = rel > (kj - i) * tk
            lsz = _log_sigmoid(z)
            lk = jnp.where(mask, lsz - z, 0.0)
            hi, lo = _split2(lk)
            between = c + _dot(hi, upper) + _dot(lo, upper)
            w = jnp.where(mask, jnp.exp(lsz + between), 0.0)
            acc = acc + _dot(w, v_ref[rows, :])
            return c + jnp.sum(lk, axis=1, keepdims=True), acc

        c, acc = lax.fori_loop(0, i + 1, step, (jnp.zeros((tq, 1), F32), jnp.zeros((tq, HEAD_DIM), F32)))
        o_ref[...] = acc.astype(BF16)
        _lane_put(tot_ref, pl.ds(pl.multiple_of(i * tq, tq), tq), h, c)

    return pl.pallas_call(
        body, name="sb_fwd", grid=(n_heads, s // tq),
        in_specs=_qkv_specs(0, s),
        out_specs=[pl.BlockSpec((tq, HEAD_DIM), lambda h, i: (i, h)), pl.BlockSpec((s, LANES), lambda h, i: (0, 0))],
        out_shape=[_sds((s, n_heads * HEAD_DIM), BF16), _sds((s, LANES), F32)],
        compiler_params=_params(("arbitrary", "arbitrary")),
    )(qkv, qkv, qkv)


def _sb_bwd(qkv, do, tot, n_heads):
    s = qkv.shape[0]
    scale = HEAD_DIM ** -0.5
    tq, tk = ATT_TQ, ATT_TK
    nq = s // tq
    hd = HEAD_DIM

    def body(q_ref, k_ref, v_ref, do_ref, tot_ref, out_ref, dk_acc, dv_acc):
        h, i = pl.program_id(0), pl.program_id(1)

        @pl.when(i == 0)
        def _():
            dk_acc[...] = jnp.zeros_like(dk_acc)
            dv_acc[...] = jnp.zeros_like(dv_acc)

        q = q_ref[...]
        dout = do_ref[...]
        total = _lane_pick(tot_ref[...], h)
        rel = _rel_index()
        incl = _tri(tk, lambda r, c: r <= c)
        excl = _tri(tk, lambda r, c: r < c)

        def step(kj, carry):
            p_l, p_e, dq = carry
            rows = pl.ds(pl.multiple_of(kj * tk, tk), tk)
            k_t = k_ref[rows, :]
            z = _dot(q, k_t, "nt") * scale
            mask = rel > (kj - i) * tk
            lsz = _log_sigmoid(z)
            lk = jnp.where(mask, lsz - z, 0.0)
            hi, lo = _split2(lk)
            between = total - (p_l + _dot(hi, incl) + _dot(lo, incl))
            w = jnp.where(mask, jnp.exp(lsz + between), 0.0)
            e = _dot(dout, v_ref[rows, :], "nt") * w
            hi, lo = _split2(e)
            e_before = p_e + _dot(hi, excl) + _dot(lo, excl)
            sg = jnp.exp(lsz)
            dz = (jnp.where(mask, e * (1.0 - sg) - e_before * sg, 0.0) * scale).astype(BF16)
            dq = dq + _dot(dz, k_t)
            dk_acc[rows, :] += _dot(dz, q, "tn")
            dv_acc[rows, :] += _dot(w, dout, "tn")
            return p_l + jnp.sum(lk, axis=1, keepdims=True), p_e + jnp.sum(e, axis=1, keepdims=True), dq

        zero = jnp.zeros((tq, 1), F32)
        _, _, dq = lax.fori_loop(0, i + 1, step, (zero, zero, jnp.zeros((tq, hd), F32)))
        out_ref[pl.ds(pl.multiple_of(i * tq, tq), tq), pl.ds(0, hd)] = dq.astype(BF16)

        @pl.when(i == nq - 1)
        def _():
            out_ref[:, pl.ds(hd, hd)] = dk_acc[...].astype(BF16)
            out_ref[:, pl.ds(2 * hd, hd)] = dv_acc[...].astype(BF16)

    return pl.pallas_call(
        body, name="sb_bwd", grid=(n_heads, nq),
        in_specs=_qkv_specs(0, s) + [pl.BlockSpec((tq, hd), lambda h, i: (i, h)),
                                     pl.BlockSpec((tq, LANES), lambda h, i: (i, 0))],
        out_specs=pl.BlockSpec((s, 3 * hd), lambda h, i: (0, h)),
        out_shape=_sds(qkv.shape, BF16),
        scratch_shapes=[pltpu.VMEM((s, hd), F32), pltpu.VMEM((s, hd), F32)],
        compiler_params=_params(("arbitrary", "arbitrary")),
    )(qkv, qkv, qkv, do, tot)


def _fox_fwd(qkv, cum_col, cum_row, n_heads, hb0):
    s = qkv.shape[0]
    scale = HEAD_DIM ** -0.5
    tq, tk = ATT_TQ, ATT_TK

    def body(q_ref, k_ref, v_ref, cc_ref, cr_ref, o_ref, o32_ref, lse_ref):
        h, i = pl.program_id(0), pl.program_id(1)

        @pl.when((h == 0) & (i == 0))
        def _():
            lse_ref[...] = jnp.zeros_like(lse_ref)

        q = q_ref[...]
        cq = _lane_pick(cc_ref[...], h)
        rel = _rel_index()

        def step(kj, carry):
            m, l, acc = carry
            rows = pl.ds(pl.multiple_of(kj * tk, tk), tk)
            ck = cr_ref[kj, pl.ds(h, 1), :]
            sc = _dot(q, k_ref[rows, :], "nt") * scale + cq - ck
            sc = jnp.where(rel >= (kj - i) * tk, sc, NEG_BIG)
            m_new = jnp.maximum(m, jnp.max(sc, axis=1, keepdims=True))
            p = jnp.exp(sc - m_new)
            alpha = jnp.exp(m - m_new)
            hi, lo = _split2(p)
            v_t = v_ref[rows, :]
            return (m_new, alpha * l + jnp.sum(p, axis=1, keepdims=True), alpha * acc + _dot(hi, v_t) + _dot(lo, v_t))

        m, l, acc = lax.fori_loop(0, i + 1, step, (jnp.full((tq, 1), NEG_BIG, F32), jnp.zeros((tq, 1), F32),
                                                   jnp.zeros((tq, HEAD_DIM), F32)))
        o = acc / l
        o_ref[...] = o.astype(BF16)
        o32_ref[...] = o
        _lane_put(lse_ref, pl.ds(pl.multiple_of(i * tq, tq), tq), h, m + jnp.log(l))

    nb = cum_row.shape[0]
    return pl.pallas_call(
        body, name="fox_fwd", grid=(n_heads, s // tq),
        in_specs=_qkv_specs(hb0, s) + [pl.BlockSpec((tq, LANES), lambda h, i: (i, 0)),
                                       pl.BlockSpec((nb, 8, tk), lambda h, i: (0, 0, 0))],
        out_specs=[pl.BlockSpec((tq, HEAD_DIM), lambda h, i: (i, h)), pl.BlockSpec((tq, HEAD_DIM), lambda h, i: (i, h)),
                   pl.BlockSpec((s, LANES), lambda h, i: (0, 0))],
        out_shape=[_sds((s, n_heads * HEAD_DIM), BF16), _sds((s, n_heads * HEAD_DIM), F32), _sds((s, LANES), F32)],
        compiler_params=_params(("arbitrary", "arbitrary")),
    )(qkv, qkv, qkv, cum_col, cum_row)


def _fox_bwd(dqkv, qkv, do, o, lse, cum_col, cum_row, n_heads, hb0):
    s = qkv.shape[0]
    scale = HEAD_DIM ** -0.5
    tq, tk = ATT_TQ, ATT_TK
    nq = s // tq
    hd = HEAD_DIM

    def body(dqkv_hbm, q_ref, k_ref, v_ref, do_ref, o_ref, lse_ref, cc_ref, cr_ref, out_ref, dc_ref,
             dk_acc, dv_acc, col_acc):
        del dqkv_hbm
        h, i = pl.program_id(0), pl.program_id(1)

        @pl.when((h == 0) & (i == 0))
        def _():
            dc_ref[...] = jnp.zeros_like(dc_ref)

        @pl.when(i == 0)
        def _():
            dk_acc[...] = jnp.zeros_like(dk_acc)
            dv_acc[...] = jnp.zeros_like(dv_acc)
            col_acc[...] = jnp.zeros_like(col_acc)

        q = q_ref[...]
        dout = do_ref[...]
        delta = jnp.sum(dout.astype(F32) * o_ref[...], axis=1, keepdims=True)
        lse_q = _lane_pick(lse_ref[...], h)
        cq = _lane_pick(cc_ref[...], h)
        rel = _rel_index()

        def step(kj, carry):
            dq, row_sum = carry
            rows = pl.ds(pl.multiple_of(kj * tk, tk), tk)
            k_t = k_ref[rows, :]
            ck = cr_ref[kj, pl.ds(h, 1), :]
            sc = _dot(q, k_t, "nt") * scale + cq - ck
            p = jnp.where(rel >= (kj - i) * tk, jnp.exp(sc - lse_q), 0.0)
            ds_f = p * (_dot(dout, v_ref[rows, :], "nt") - delta)
            col_acc[kj] += jnp.broadcast_to(jnp.sum(ds_f, axis=0, keepdims=True), (8, tk))
            ds = (ds_f * scale).astype(BF16)
            dk_acc[rows, :] += _dot(ds, q, "tn")
            dv_acc[rows, :] += _dot(p, dout, "tn")
            return dq + _dot(ds, k_t), row_sum + jnp.sum(ds_f, axis=1, keepdims=True)

        dq, row_sum = lax.fori_loop(0, i + 1, step, (jnp.zeros((tq, hd), F32), jnp.zeros((tq, 1), F32)))
        q_rows = pl.ds(pl.multiple_of(i * tq, tq), tq)
        out_ref[q_rows, pl.ds(0, hd)] = dq.astype(BF16)
        _lane_put(dc_ref, q_rows, h, row_sum)

        @pl.when(i == nq - 1)
        def _():
            out_ref[:, pl.ds(hd, hd)] = dk_acc[...].astype(BF16)
            out_ref[:, pl.ds(2 * hd, hd)] = dv_acc[...].astype(BF16)
            lane = lax.broadcasted_iota(jnp.int32, (tk, LANES), 1)
            for kj in range(nb):
                col = jnp.broadcast_to(col_acc[kj][0:1, :], (LANES, tk)).T
                old = dc_ref[pl.ds(kj * tk, tk), :]
                dc_ref[pl.ds(kj * tk, tk), :] = jnp.where(lane == h, old - col, old)

    nb = cum_row.shape[0]
    return pl.pallas_call(
        body, name="fox_bwd", grid=(n_heads, nq),
        in_specs=[ANY] + _qkv_specs(hb0, s) + [
            pl.BlockSpec((tq, hd), lambda h, i: (i, h)), pl.BlockSpec((tq, hd), lambda h, i: (i, h)),
            pl.BlockSpec((tq, LANES), lambda h, i: (i, 0)), pl.BlockSpec((tq, LANES), lambda h, i: (i, 0)),
            pl.BlockSpec((nb, 8, tk), lambda h, i: (0, 0, 0))],
        out_specs=[pl.BlockSpec((s, 3 * hd), lambda h, i: (0, hb0 + h)), pl.BlockSpec((s, LANES), lambda h, i: (0, 0))],
        out_shape=[_sds(dqkv.shape, BF16), _sds((s, LANES), F32)],
        scratch_shapes=[pltpu.VMEM((s, hd), F32), pltpu.VMEM((s, hd), F32), pltpu.VMEM((s // tk, 8, tk), F32)],
        input_output_aliases={0: 0},
        compiler_params=_params(("arbitrary", "arbitrary")),
    )(dqkv, qkv, qkv, qkv, do, o, lse, cum_col, cum_row)


def _branch_merge(o_sb, o_fx, w_sb, w_fx, gf, tm=1024):
    s = o_sb.shape[0]
    cs = w_sb.shape[2]
    tm = _tile(s, tm)

    def body(osb_ref, ofx_ref, wsb_ref, wfx_ref, g_ref, merged_ref, asb_ref, afx_ref):
        a_sb = _dot(osb_ref[...], wsb_ref[...])
        a_fx = _dot(ofx_ref[...], wfx_ref[...])
        g = g_ref[...]
        merged_ref[...] = (_sigmoid(g[:, :cs]) * a_sb + _sigmoid(g[:, cs:]) * a_fx).astype(BF16)
        asb_ref[...] = a_sb.astype(BF16)
        afx_ref[...] = a_fx.astype(BF16)

    blk = pl.BlockSpec((tm, cs), lambda i, j: (i, j))
    out = _sds((s, N_DEV * cs), BF16)
    return pl.pallas_call(
        body, name="branch_merge", grid=(s // tm, N_DEV),
        in_specs=[pl.BlockSpec((tm, o_sb.shape[1]), lambda i, j: (i, 0)),
                  pl.BlockSpec((tm, o_fx.shape[1]), lambda i, j: (i, 0)),
                  pl.BlockSpec((None,) + w_sb.shape[1:], lambda i, j: (j, 0, 0)),
                  pl.BlockSpec((None,) + w_fx.shape[1:], lambda i, j: (j, 0, 0)),
                  pl.BlockSpec((tm, 2 * cs), lambda i, j: (i, j))],
        out_specs=[blk, blk, blk], out_shape=[out, out, out],
        compiler_params=_params(("parallel", "arbitrary")),
    )(o_sb, o_fx, w_sb, w_fx, gf)


def _merge_bwd(dmix, w_out, gf, a_sb, a_fx, tm=1024, tk=512):
    s, d = dmix.shape
    cs = d // N_DEV
    tm, tk = _tile(s, tm), _tile(d, tk)

    def epilogue(acc, ex, outs):
        g, a_sb, a_fx = ex[0][...], ex[1][...].astype(F32), ex[2][...].astype(F32)
        s_sb, s_fx = _sigmoid(g[:, :cs]), _sigmoid(g[:, cs:])
        outs[0][...] = (acc * s_sb).astype(BF16)
        outs[1][...] = (acc * s_fx).astype(BF16)
        outs[2][...] = jnp.concatenate([acc * a_sb * s_sb * (1.0 - s_sb), acc * a_fx * s_fx * (1.0 - s_fx)],
                                       axis=1).astype(BF16)

    blk = pl.BlockSpec((tm, cs), lambda i, j, k: (i, j))
    wide = pl.BlockSpec((tm, 2 * cs), lambda i, j, k: (i, j))
    return _matmul(
        "merge_bwd", "nt",
        [(dmix, pl.BlockSpec((tm, tk), lambda i, j, k: (i, k)), w_out, pl.BlockSpec((cs, tk), lambda i, j, k: (j, k)))],
        (s // tm, N_DEV, d // tk), (tm, cs),
        [_sds((s, d), BF16), _sds((s, d), BF16), _sds(gf.shape, BF16)], [blk, blk, wide],
        extras=[(gf, wide), (a_sb, blk), (a_fx, blk)], epilogue=epilogue)


def _ffn_up(u2, w_gate, w_up, tm=1024):
    s, d = u2.shape
    fs = w_gate.shape[2]
    tm = _tile(s, tm)

    def body(u_ref, wg_ref, wu_ref, gate_ref, up_ref, act_ref):
        u = u_ref[...]
        gate = _dot(u, wg_ref[...])
        up = _dot(u, wu_ref[...])
        gate_ref[...] = gate
        up_ref[...] = up
        act_ref[...] = (gate * _sigmoid(gate) * up).astype(BF16)

    w_spec = pl.BlockSpec((None, d, fs), lambda i, j: (j, 0, 0))
    o_spec = pl.BlockSpec((None, tm, fs), lambda i, j: (j, i, 0))
    return pl.pallas_call(
        body, name="ffn_up", grid=(s // tm, N_DEV),
        in_specs=[pl.BlockSpec((tm, d), lambda i, j: (i, 0)), w_spec, w_spec],
        out_specs=[o_spec, o_spec, o_spec],
        out_shape=[_sds((N_DEV, s, fs), F32), _sds((N_DEV, s, fs), F32), _sds((N_DEV, s, fs), BF16)],
        compiler_params=_params(("parallel", "arbitrary")),
    )(u2, w_gate, w_up)


def _ffn_down_bwd(dff, w_down, gate, up, tm=1024):
    s, d = dff.shape
    fs = w_down.shape[1]
    tm = _tile(s, tm)

    def body(dff_ref, wd_ref, gate_ref, up_ref, dgate_ref, dup_ref):
        dact = _dot(dff_ref[...], wd_ref[...], "nt")
        gate = gate_ref[...]
        sg = _sigmoid(gate)
        dup_ref[...] = (dact * gate * sg).astype(BF16)
        dgate_ref[...] = (dact * up_ref[...] * sg * (1.0 + gate * (1.0 - sg))).astype(BF16)

    a_spec = pl.BlockSpec((None, tm, fs), lambda i, j: (j, i, 0))
    return pl.pallas_call(
        body, name="ffn_down_bwd", grid=(s // tm, N_DEV),
        in_specs=[pl.BlockSpec((tm, d), lambda i, j: (i, 0)), pl.BlockSpec((None, fs, d), lambda i, j: (j, 0, 0)),
                  a_spec, a_spec],
        out_specs=[a_spec, a_spec],
        out_shape=[_sds((N_DEV, s, fs), BF16), _sds((N_DEV, s, fs), BF16)],
        compiler_params=_params(("parallel", "arbitrary")),
    )(dff, w_down, gate, up)


def _mesh_place():
    x, y, c = lax.axis_index("x"), lax.axis_index("y"), lax.axis_index("c")
    peers = []
    for d in range(1, N_DEV):
        px = 1 - x if d & 4 else x
        py = 1 - y if d & 2 else y
        pc = 1 - c if d & 1 else c
        peers.append((d, (px, py, pc), 4 * px + 2 * py + pc))
    return 4 * x + 2 * y + c, peers


def _exchange(name, arrays, scatter):
    n = len(arrays)

    def body(*refs):
        ins, outs = refs[:n], refs[n:2 * n]
        send, recv, local = refs[2 * n:]
        me, peers = _mesh_place()

        def src(a, p):
            return ins[a].at[p] if scatter else ins[a]

        own = [pltpu.make_async_copy(src(a, me), outs[a].at[me], local.at[a]) for a in range(n)]
        for cp in own:
            cp.start()
        sent = []
        for a in range(n):
            for d, dev, flat in peers:
                cp = pltpu.make_async_remote_copy(src_ref=src(a, flat), dst_ref=outs[a].at[me], send_sem=send.at[a, d],
                                                  recv_sem=recv.at[a, d], device_id=dev, device_id_type=MESH)
                cp.start()
                sent.append(cp)
        for a in range(n):
            for d, dev, flat in peers:
                pltpu.make_async_remote_copy(src_ref=src(a, flat), dst_ref=outs[a].at[flat], send_sem=send.at[a, d],
                                             recv_sem=recv.at[a, d], device_id=dev, device_id_type=MESH).wait_recv()
        for cp in sent:
            cp.wait_send()
        for cp in own:
            cp.wait()

    def out_shape(a):
        return _sds(a.shape if scatter else (N_DEV,) + a.shape, a.dtype)

    return pl.pallas_call(
        body, name=name, in_specs=[ANY] * n, out_specs=[ANY] * n, out_shape=[out_shape(a) for a in arrays],
        scratch_shapes=[pltpu.SemaphoreType.DMA((n, N_DEV)), pltpu.SemaphoreType.DMA((n, N_DEV)),
                        pltpu.SemaphoreType.DMA((n,))],
    )(*arrays)


def _adamw(g, w, m, v):
    m = ADAM_B1 * m + (1.0 - ADAM_B1) * g
    v = ADAM_B2 * v + (1.0 - ADAM_B2) * (g * g)
    m_hat = m / (1.0 - ADAM_B1 ** ADAM_STEP)
    v_hat = v / (1.0 - ADAM_B2 ** ADAM_STEP)
    delta = -ADAM_LR * (m_hat / (jnp.sqrt(v_hat) + ADAM_EPS) + ADAM_WD * w)
    return delta, m, v


def _update(name, parts, w, m, v, block_bytes=1 << 20):
    r, c = w.shape
    tr = max(8, min(r, (block_bytes // (4 * c)) // 8 * 8))
    while r % tr:
        tr -= 8

    def body(p_ref, w_ref, m_ref, v_ref, g_ref, d_ref, nm_ref, nv_ref):
        g = p_ref[0].astype(F32)
        for p in range(1, N_DEV):
            g = g + p_ref[p].astype(F32)
        g_ref[...] = g
        d_ref[...], nm_ref[...], nv_ref[...] = _adamw(g, w_ref[...], m_ref[...], v_ref[...])

    blk = pl.BlockSpec((tr, c), lambda i: (i, 0))
    return pl.pallas_call(
        body, name=name, grid=(r // tr,),
        in_specs=[pl.BlockSpec((N_DEV, tr, c), lambda i: (0, i, 0)), blk, blk, blk],
        out_specs=[blk] * 4, out_shape=[_sds((r, c), F32)] * 4,
        compiler_params=_params(("parallel",)),
    )(parts, w, m, v)


def _small_update(part, w, m, v):
    n = part.shape[1]

    def body(p_ref, w_ref, m_ref, v_ref, g_ref, d_ref, nm_ref, nv_ref, buf, send, recv):
        me, peers = _mesh_place()
        buf[me] = p_ref[...]
        sent = []
        for d, dev, flat in peers:
            cp = pltpu.make_async_remote_copy(src_ref=p_ref, dst_ref=buf.at[me], send_sem=send.at[d],
                                              recv_sem=recv.at[d], device_id=dev, device_id_type=MESH)
            cp.start()
            sent.append(cp)
        for d, dev, flat in peers:
            pltpu.make_async_remote_copy(src_ref=p_ref, dst_ref=buf.at[flat], send_sem=send.at[d],
                                         recv_sem=recv.at[d], device_id=dev, device_id_type=MESH).wait_recv()
        for cp in sent:
            cp.wait_send()
        g = buf[0]
        for p in range(1, N_DEV):
            g = g + buf[p]
        g_ref[...] = g
        d_ref[...], nm_ref[...], nv_ref[...] = _adamw(g, w_ref[...], m_ref[...], v_ref[...])

    vm = pl.BlockSpec(memory_space=pltpu.VMEM)
    return pl.pallas_call(
        body, name="small_update", in_specs=[vm] * 4, out_specs=[vm] * 4, out_shape=[_sds((1, n), F32)] * 4,
        scratch_shapes=[pltpu.VMEM((N_DEV, 1, n), F32), pltpu.SemaphoreType.DMA((N_DEV,)),
                        pltpu.SemaphoreType.DMA((N_DEV,))],
    )(part, w, m, v)


def _w_in_reorder(g_in, d_sb, d_fox, n_f, d):
    full = jnp.transpose(g_in, (1, 0, 2)).reshape(d, -1)
    cs = d // N_DEV

    def heads(sec, width):
        return sec.reshape(d, 3, width // HEAD_DIM, HEAD_DIM).transpose(0, 2, 1, 3).reshape(d, 3 * width)

    o1 = 3 * d_sb
    o2 = o1 + 3 * d_fox
    o3 = o2 + n_f
    gates = jnp.stack([full[:, o3:o3 + d].reshape(d, N_DEV, cs), full[:, o3 + d:].reshape(d, N_DEV, cs)], axis=2)
    f_sec = jnp.pad(full[:, o2:o3], ((0, 0), (0, F_PAD - n_f)))
    return jnp.concatenate([heads(full[:, :o1], d_sb), heads(full[:, o1:o2], d_fox), gates.reshape(d, 2 * d), f_sec], axis=1)


def _w_in_restore(dwq, dwgf, d_sb, d_fox, n_f, d):
    cs = d // N_DEV

    def heads(sec, width):
        return sec.reshape(d, width // HEAD_DIM, 3, HEAD_DIM).transpose(0, 2, 1, 3).reshape(d, 3 * width)

    gates = dwgf[:, :2 * d].reshape(d, N_DEV, 2, cs)
    full = jnp.concatenate([heads(dwq[:, :3 * d_sb], d_sb), heads(dwq[:, 3 * d_sb:], d_fox),
                            dwgf[:, 2 * d:2 * d + n_f], gates[:, :, 0].reshape(d, d), gates[:, :, 1].reshape(d, d)], axis=1)
    return jnp.transpose(full.reshape(d, N_DEV, -1), (1, 0, 2))


def kernel(x, norm_mix_pre, norm_mix_post, w_in, b_forget, w_branch_sb, w_branch_fox, w_out, norm_ffn_pre, norm_ffn_post, w_ffn_gate, w_ffn_up, w_ffn_down, loss_target, m_norm_mix_pre, m_norm_mix_post, m_w_in, m_b_forget, m_w_branch_sb, m_w_branch_fox, m_w_out, m_norm_ffn_pre, m_norm_ffn_post, m_w_ffn_gate, m_w_ffn_up, m_w_ffn_down, v_norm_mix_pre, v_norm_mix_post, v_w_in, v_b_forget, v_w_branch_sb, v_w_branch_fox, v_w_out, v_norm_ffn_pre, v_norm_ffn_post, v_w_ffn_gate, v_w_ffn_up, v_w_ffn_down):
    xs, target = x[0], loss_target[0]
    s, d = xs.shape
    d_sb, d_fox = w_branch_sb.shape[1], w_branch_fox.shape[1]
    h_sb, h_fox = d_sb // HEAD_DIM, d_fox // HEAD_DIM
    n_f = b_forget.shape[1]
    fs = w_ffn_gate.shape[2]
    cs = d // N_DEV
    n_qkv = 3 * (d_sb + d_fox)
    n_gf = 2 * d + F_PAD
    f_blk = 2 * d // LANES
    big = (w_in, w_branch_sb, w_branch_fox, w_out, w_ffn_gate, w_ffn_up, w_ffn_down)
    big_m = (m_w_in, m_w_branch_sb, m_w_branch_fox, m_w_out, m_w_ffn_gate, m_w_ffn_up, m_w_ffn_down)
    big_v = (v_w_in, v_w_branch_sb, v_w_branch_fox, v_w_out, v_w_ffn_gate, v_w_ffn_up, v_w_ffn_down)

    g_in, g_sb, g_fx, g_out, g_gate, g_up, g_down = _exchange(
        "gather_weights", [w[0].astype(BF16) for w in big], scatter=False)
    w_cat = _w_in_reorder(g_in, d_sb, d_fox, n_f, d)
    w_out_full = g_out.reshape(d, d)
    b_pad = jnp.pad(b_forget, ((0, 0), (0, LANES - n_f)))

    u = _pre_norm(xs, norm_mix_pre)
    qkv = _mm_plain("proj_qkv", "nn", u, w_cat, BF16, n=n_qkv)
    gf = _mm_plain("proj_gates", "nn", u, w_cat, F32, n_off=n_qkv, n=n_gf)
    cum_col, cum_row = _forget_fwd(gf, b_pad, f_blk)
    o_sb, tot = _sb_fwd(qkv, h_sb)
    o_fx, o_fx32, lse = _fox_fwd(qkv, cum_col, cum_row, h_fox, h_sb)
    merged, a_sb, a_fx = _branch_merge(o_sb, o_fx, g_sb, g_fx, gf)
    mix = _mm_plain("out_proj", "nn", merged, w_out_full, F32)
    h1, u2 = _mid_norms(xs, mix, norm_mix_post, norm_ffn_pre)
    gate, up, act = _ffn_up(u2, g_gate, g_up)
    tm, tn = _tile(s, 1024), _tile(d, 512)
    ff = _matmul("ffn_down", "nn",
                 [(act, pl.BlockSpec((None, tm, fs), lambda i, j, k: (k, i, 0)),
                   g_down, pl.BlockSpec((None, fs, tn), lambda i, j, k: (k, 0, j)))],
                 (s // tm, d // tn, N_DEV), (tm, tn), _sds((s, d), F32), pl.BlockSpec((tm, tn), lambda i, j, k: (i, j)))
    loss_part, dy, dff, dg_ffn_post = _loss_head(h1, ff, target, norm_ffn_post)

    dgate, dup = _ffn_down_bwd(dff, g_down, gate, up)
    tk = _tile(s, 512)
    dw_down = _matmul("dw_down", "tn",
                      [(act, pl.BlockSpec((None, tk, fs), lambda j, n, k: (j, k, 0)),
                        dff, pl.BlockSpec((tk, tn), lambda j, n, k: (k, n)))],
                      (N_DEV, d // tn, s // tk), (fs, tn), _sds((N_DEV, fs, d), BF16),
                      pl.BlockSpec((None, fs, tn), lambda j, n, k: (j, 0, n)))

    def dw_up(name, dact):
        return _matmul(name, "tn",
                       [(u2, pl.BlockSpec((tk, tn), lambda j, i, k: (k, i)),
                         dact, pl.BlockSpec((None, tk, fs), lambda j, i, k: (j, k, 0)))],
                       (N_DEV, d // tn, s // tk), (tn, fs), _sds((N_DEV, d, fs), BF16),
                       pl.BlockSpec((None, tn, fs), lambda j, i, k: (j, i, 0)))

    dw_gate, dw_upw = dw_up("dw_gate", dgate), dw_up("dw_up", dup)
    a_spec = pl.BlockSpec((None, tm, fs), lambda i, j, k: (k, i, 0))
    b_spec = pl.BlockSpec((None, tn, fs), lambda i, j, k: (k, j, 0))
    du2 = _matmul("du2", "nt", [(dgate, a_spec, g_gate, b_spec), (dup, a_spec, g_up, b_spec)],
                  (s // tm, d // tn, N_DEV), (tm, tn), _sds((s, d), F32), pl.BlockSpec((tm, tn), lambda i, j, k: (i, j)))
    dh1, dmix, dg_ffn_pre, dg_mix_post = _mid_norms_bwd(dy, du2, h1, mix, norm_ffn_pre, norm_mix_post)

    da_sb, da_fx, dgf = _merge_bwd(dmix, w_out_full, gf, a_sb, a_fx)
    dw_out = _mm_plain("dw_out", "tn", merged, dmix, BF16).reshape(N_DEV, cs, d)

    def branch_bwd(tag, da, w_b, o_b, width):
        tb = _tile(width, 512)
        do = _matmul("do_" + tag, "nt",
                     [(da, pl.BlockSpec((tm, cs), lambda i, j, k: (i, k)),
                       w_b, pl.BlockSpec((None, tb, cs), lambda i, j, k: (k, j, 0)))],
                     (s // tm, width // tb, N_DEV), (tm, tb), _sds((s, width), BF16),
                     pl.BlockSpec((tm, tb), lambda i, j, k: (i, j)))
        dw = _matmul("dw_" + tag, "tn",
                     [(o_b, pl.BlockSpec((tk, tb), lambda j, i, k: (k, i)),
                       da, pl.BlockSpec((tk, cs), lambda j, i, k: (k, j)))],
                     (N_DEV, width // tb, s // tk), (tb, cs), _sds((N_DEV, width, cs), BF16),
                     pl.BlockSpec((None, tb, cs), lambda j, i, k: (j, i, 0)))
        return do, dw

    do_sb, dw_sb = branch_bwd("sb", da_sb, g_sb, o_sb, d_sb)
    do_fx, dw_fx = branch_bwd("fox", da_fx, g_fx, o_fx, d_fox)

    dqkv = _sb_bwd(qkv, do_sb, tot, h_sb)
    dqkv, dcum = _fox_bwd(dqkv, qkv, do_fx, o_fx32, lse, cum_col, cum_row, h_fox, h_sb)
    dgf, db_part = _forget_bwd(dgf, dcum, gf, b_pad, f_blk)
    du = _mm_plain("du_qkv", "nt", dqkv, w_cat, F32)
    du = _mm_plain("du_gates", "nt", dgf, w_cat, F32, k_off=n_qkv, init=du)
    dx, dg_mix_pre = _pre_norm_bwd(dh1, du, xs, norm_mix_pre)
    dw_in = _w_in_restore(_mm_plain("dw_qkv", "tn", u, dqkv, BF16), _mm_plain("dw_gates", "tn", u, dgf, BF16),
                          d_sb, d_fox, n_f, d)

    parts = _exchange("scatter_grads", [dw_in, dw_sb, dw_fx, dw_out, dw_gate, dw_upw, dw_down], scatter=True)
    names = ("w_in", "w_branch_sb", "w_branch_fox", "w_out", "w_ffn_gate", "w_ffn_up", "w_ffn_down")
    upd = {nm: [o[None] for o in _update("update_" + nm, p, w[0], m[0], v[0])]
           for nm, p, w, m, v in zip(names, parts, big, big_m, big_v)}

    small = ((norm_mix_pre, m_norm_mix_pre, v_norm_mix_pre), (norm_mix_post, m_norm_mix_post, v_norm_mix_post),
             (norm_ffn_pre, m_norm_ffn_pre, v_norm_ffn_pre), (norm_ffn_post, m_norm_ffn_post, v_norm_ffn_post))
    pad_f = ((0, 0), (0, LANES - n_f))
    cat = lambda i: jnp.concatenate([t[i] for t in small] + [jnp.pad((b_forget, m_b_forget, v_b_forget)[i], pad_f)], axis=1)
    sm = _small_update(jnp.concatenate([dg_mix_pre, dg_mix_post, dg_ffn_pre, dg_ffn_post, db_part], axis=1),
                       cat(0), cat(1), cat(2))
    for i, nm in enumerate(("norm_mix_pre", "norm_mix_post", "norm_ffn_pre", "norm_ffn_post")):
        upd[nm] = [o[:, i * d:(i + 1) * d] for o in sm]
    upd["b_forget"] = [o[:, 4 * d:4 * d + n_f] for o in sm]

    loss = lax.psum(loss_part[0, 0], ("x", "y", "c"))
    order = ("norm_mix_pre", "norm_mix_post", "w_in", "b_forget", "w_branch_sb", "w_branch_fox", "w_out",
             "norm_ffn_pre", "norm_ffn_post", "w_ffn_gate", "w_ffn_up", "w_ffn_down")
    return (loss, dx[None]) + tuple(upd[nm][i] for i in range(4) for nm in order)
```

```python
import jax
import jax.numpy as jnp
from jax import lax
from jax.experimental import pallas as pl
from jax.experimental.pallas import tpu as pltpu

F32 = jnp.float32
BF16 = jnp.bfloat16
MESH = pl.DeviceIdType.MESH
ANY = pl.BlockSpec(memory_space=pl.ANY)
HBM = pl.BlockSpec(memory_space=pltpu.HBM)
SEM = pl.BlockSpec(memory_space=pltpu.SEMAPHORE)
EFFECT = pltpu.SideEffectType.DATAFLOW_SIDE_EFFECTING

N_DEV = 8
HEAD_DIM = 128
RMS_EPS = 1e-6
F_PAD = 512
LANES = 128
ATT_TQ = 256
ATT_TK = 256
NEG_BIG = -1e30
VMEM_LIMIT = 56 * 1024 * 1024

ADAM_LR = 0.001
ADAM_B1 = 0.9
ADAM_B2 = 0.999
ADAM_EPS = 1e-08
ADAM_WD = 0.01
ADAM_STEP = 10

_DIMS = {"nn": ((1,), (0,)), "nt": ((1,), (1,)), "tn": ((0,), (0,))}


def _params(sem):
    return pltpu.CompilerParams(dimension_semantics=sem, vmem_limit_bytes=VMEM_LIMIT)


def _dot(a, b, mode="nn"):
    return lax.dot_general(a.astype(BF16), b.astype(BF16), (_DIMS[mode], ((), ())), preferred_element_type=F32)


def _tile(n, pref):
    if n <= pref:
        return n
    t = (pref // LANES) * LANES
    while n % t:
        t -= LANES
    return t


def _split2(v):
    hi = v.astype(BF16)
    return hi, (v - hi.astype(F32)).astype(BF16)


def _split3(v):
    a = v.astype(BF16)
    r = v - a.astype(F32)
    b = r.astype(BF16)
    return a, b, (r - b.astype(F32)).astype(BF16)


def _tri(n, cmp):
    r = lax.broadcasted_iota(jnp.int32, (n, n), 0)
    c = lax.broadcasted_iota(jnp.int32, (n, n), 1)
    return jnp.where(cmp(r, c), 1.0, 0.0).astype(BF16)


def _lane_pick(v, h):
    lane = lax.broadcasted_iota(jnp.int32, v.shape, 1)
    return jnp.sum(jnp.where(lane == h, v, 0.0), axis=1, keepdims=True)


def _lane_put(ref, rows, h, col):
    old = ref[rows, :]
    lane = lax.broadcasted_iota(jnp.int32, old.shape, 1)
    ref[rows, :] = jnp.where(lane == h, col, old)


def _sigmoid(z):
    return 1.0 / (1.0 + jnp.exp(-z))


def _log_sigmoid(z):
    return jnp.minimum(z, 0.0) - jnp.log(1.0 + jnp.exp(-jnp.abs(z)))


def _sds(shape, dtype):
    return jax.ShapeDtypeStruct(shape, dtype)


def _matmul(name, mode, pairs, grid, acc_shape, out_shape, out_specs, extras=(), epilogue=None, init=None):
    n_p, n_e = len(pairs), len(extras)
    nk = grid[-1]
    single = not isinstance(out_shape, (list, tuple))
    n_i = 0 if init is None else 1

    def body(*refs):
        ab = refs[:2 * n_p]
        ex = refs[2 * n_p:2 * n_p + n_e]
        ini = refs[2 * n_p + n_e:2 * n_p + n_e + n_i]
        outs = refs[2 * n_p + n_e + n_i:-1]
        acc = refs[-1]
        k = pl.program_id(len(grid) - 1)

        @pl.when(k == 0)
        def _():
            acc[...] = jnp.zeros_like(acc) if init is None else ini[0][...].astype(F32)

        t = _dot(ab[0][...], ab[1][...], mode)
        for p in range(1, n_p):
            t = t + _dot(ab[2 * p][...], ab[2 * p + 1][...], mode)
        acc[...] += t

        @pl.when(k == nk - 1)
        def _():
            if epilogue is None:
                outs[0][...] = acc[...].astype(outs[0].dtype)
            else:
                epilogue(acc[...], ex, outs)

    in_specs = [s for (_, sa, _, sb) in pairs for s in (sa, sb)] + [s for (_, s) in extras]
    args = [v for (a, _, b, _) in pairs for v in (a, b)] + [e for (e, _) in extras]
    if init is not None:
        in_specs.append(init[1])
        args.append(init[0])
    return pl.pallas_call(
        body, name=name, grid=grid, in_specs=in_specs,
        out_specs=out_specs if single else list(out_specs),
        out_shape=out_shape if single else list(out_shape),
        scratch_shapes=[pltpu.VMEM(acc_shape, F32)],
        compiler_params=_params(("parallel",) * (len(grid) - 1) + ("arbitrary",)),
    )(*args)


def _mm_plain(name, mode, a, b, out_dtype, *, n_off=0, n=None, k_off=0, tm=1024, tn=512, tk=512, init=None):
    if mode == "nn":
        (m, kk), nn_ = a.shape, b.shape[1]
    elif mode == "nt":
        (m, kk), nn_ = a.shape, b.shape[0]
    else:
        (kk, m), nn_ = a.shape, b.shape[1]
    n = nn_ if n is None else n
    tm, tn, tk = _tile(m, tm), _tile(n, tn), _tile(kk, tk)
    assert n_off % tn == 0 and k_off % tk == 0
    off, koff = n_off // tn, k_off // tk
    a_spec = {"nn": pl.BlockSpec((tm, tk), lambda i, j, k: (i, k)),
              "nt": pl.BlockSpec((tm, tk), lambda i, j, k: (i, k)),
              "tn": pl.BlockSpec((tk, tm), lambda i, j, k: (k, i))}[mode]
    b_spec = {"nn": pl.BlockSpec((tk, tn), lambda i, j, k: (k, j + off)),
              "nt": pl.BlockSpec((tn, tk), lambda i, j, k: (j, k + koff)),
              "tn": pl.BlockSpec((tk, tn), lambda i, j, k: (k, j))}[mode]
    o_spec = pl.BlockSpec((tm, tn), lambda i, j, k: (i, j))
    if init is not None:
        init = (init, o_spec)
    return _matmul(name, mode, [(a, a_spec, b, b_spec)], (m // tm, n // tn, kk // tk), (tm, tn),
                   _sds((m, n), out_dtype), o_spec, init=init)


def _rows_call(name, body, ins, outs, s, tr=256):
    def spec(v, per_row):
        if per_row:
            return pl.BlockSpec((tr, v.shape[1]), lambda i: (i, 0))
        return pl.BlockSpec(v.shape, lambda i: (0, 0))
    return pl.pallas_call(
        body, name=name, grid=(s // tr,),
        in_specs=[spec(v, p) for v, p in ins], out_specs=[spec(v, p) for v, p in outs],
        out_shape=[_sds(v.shape, v.dtype) for v, _ in outs],
        compiler_params=_params(("arbitrary",)),
    )(*[v for v, _ in ins])


def _rsq(v):
    return lax.rsqrt(jnp.mean(v * v, axis=-1, keepdims=True) + RMS_EPS)


def _norm_bwd(dy, v, r, g):
    vh = v * r
    t = dy * g
    dv = r * (t - vh * jnp.mean(t * vh, axis=-1, keepdims=True))
    return dv, jnp.sum(dy * vh, axis=0, keepdims=True)


def _accum(ref, val):
    @pl.when(pl.program_id(0) == 0)
    def _():
        ref[...] = jnp.zeros_like(ref)
    ref[...] += val


def _pre_norm(x, g):
    def body(x_ref, g_ref, u_ref):
        v = x_ref[...]
        u_ref[...] = (v * _rsq(v) * g_ref[...]).astype(BF16)
    s, d = x.shape
    return _rows_call("pre_norm", body, [(x, True), (g, False)], [(_sds((s, d), BF16), True)], s)[0]


def _mid_norms(x, mix, g_post, g_pre):
    def body(x_ref, mix_ref, gp_ref, gn_ref, h_ref, u_ref):
        mv = mix_ref[...]
        h = x_ref[...] + mv * _rsq(mv) * gp_ref[...]
        h_ref[...] = h
        u_ref[...] = (h * _rsq(h) * gn_ref[...]).astype(BF16)
    s, d = x.shape
    return _rows_call("mid_norms", body, [(x, True), (mix, True), (g_post, False), (g_pre, False)],
                      [(_sds((s, d), F32), True), (_sds((s, d), BF16), True)], s)


def _loss_head(h1, ff, target, g):
    s, d = h1.shape

    def body(h_ref, ff_ref, t_ref, g_ref, loss_ref, dy_ref, dff_ref, dg_ref):
        fv = ff_ref[...]
        r = _rsq(fv)
        err = h_ref[...] + fv * r * g_ref[...] - t_ref[...]
        part = 0.5 * jnp.sum(jnp.mean(err * err, axis=-1, keepdims=True), axis=0, keepdims=True)
        _accum(loss_ref, jnp.broadcast_to(part, loss_ref.shape))
        dy = err * (1.0 / d)
        dy_ref[...] = dy
        dff, dg = _norm_bwd(dy, fv, r, g_ref[...])
        dff_ref[...] = dff.astype(BF16)
        _accum(dg_ref, dg)

    return _rows_call("loss_head", body, [(h1, True), (ff, True), (target, True), (g, False)],
                      [(_sds((1, LANES), F32), False), (_sds((s, d), F32), True),
                       (_sds((s, d), BF16), True), (_sds((1, d), F32), False)], s)


def _mid_norms_bwd(dy, du2, h1, mix, g_pre, g_post):
    s, d = dy.shape

    def body(dy_ref, du_ref, h_ref, mix_ref, gn_ref, gp_ref, dh_ref, dmix_ref, dgn_ref, dgp_ref):
        h = h_ref[...]
        dh, dgn = _norm_bwd(du_ref[...], h, _rsq(h), gn_ref[...])
        dh = dh + dy_ref[...]
        dh_ref[...] = dh
        _accum(dgn_ref, dgn)
        mv = mix_ref[...]
        dmix, dgp = _norm_bwd(dh, mv, _rsq(mv), gp_ref[...])
        dmix_ref[...] = dmix.astype(BF16)
        _accum(dgp_ref, dgp)

    return _rows_call("mid_norms_bwd", body,
                      [(dy, True), (du2, True), (h1, True), (mix, True), (g_pre, False), (g_post, False)],
                      [(_sds((s, d), F32), True), (_sds((s, d), BF16), True),
                       (_sds((1, d), F32), False), (_sds((1, d), F32), False)], s)


def _pre_norm_bwd(dh1, du, x, g):
    s, d = x.shape

    def body(dh_ref, du_ref, x_ref, g_ref, dx_ref, dg_ref):
        v = x_ref[...]
        dv, dg = _norm_bwd(du_ref[...], v, _rsq(v), g_ref[...])
        dx_ref[...] = dh_ref[...] + dv
        _accum(dg_ref, dg)

    return _rows_call("pre_norm_bwd", body, [(dh1, True), (du, True), (x, True), (g, False)],
                      [(_sds((s, d), F32), True), (_sds((1, d), F32), False)], s)


def _forget_fwd(gf, b_pad, f_blk):
    s = gf.shape[0]
    tb = ATT_TK
    nb = s // tb

    def body(f_ref, b_ref, col_ref, row_ref):
        incl = _tri(tb, lambda r, c: c <= r)
        carry = jnp.zeros((1, LANES), F32)
        for i in range(nb):
            lf = _log_sigmoid(f_ref[pl.ds(i * tb, tb), :] + b_ref[...])
            parts = _split3(lf)
            cum = carry + _dot(incl, parts[0]) + _dot(incl, parts[1]) + _dot(incl, parts[2])
            col_ref[pl.ds(i * tb, tb), :] = cum
            row_ref[i] = cum.T
            carry = carry + jnp.sum(lf, axis=0, keepdims=True)

    return pl.pallas_call(
        body, name="forget_fwd", grid=(1,),
        in_specs=[pl.BlockSpec((s, LANES), lambda i: (0, f_blk)), pl.BlockSpec((1, LANES), lambda i: (0, 0))],
        out_specs=[pl.BlockSpec((s, LANES), lambda i: (0, 0)), pl.BlockSpec((nb, LANES, tb), lambda i: (0, 0, 0))],
        out_shape=[_sds((s, LANES), F32), _sds((nb, LANES, tb), F32)],
        compiler_params=_params(("arbitrary",)),
    )(gf, b_pad)


def _forget_bwd(dgf, dcum, gf, b_pad, f_blk):
    s = gf.shape[0]
    tb = ATT_TK
    nb = s // tb
    sec = dgf.shape[1] // F_PAD - 1

    def body(dgf_hbm, dc_ref, f_ref, b_ref, out_ref, db_ref):
        del dgf_hbm
        incl = _tri(tb, lambda r, c: c >= r)
        carry = jnp.zeros((1, LANES), F32)
        db = jnp.zeros((1, LANES), F32)
        out_ref[...] = jnp.zeros_like(out_ref)
        for i in reversed(range(nb)):
            dc = dc_ref[pl.ds(i * tb, tb), :]
            parts = _split3(dc)
            dlf = carry + _dot(incl, parts[0]) + _dot(incl, parts[1]) + _dot(incl, parts[2])
            z = f_ref[pl.ds(i * tb, tb), :] + b_ref[...]
            df = dlf * _sigmoid(-z)
            out_ref[pl.ds(i * tb, tb), pl.ds(0, LANES)] = df.astype(BF16)
            db = db + jnp.sum(df, axis=0, keepdims=True)
            carry = carry + jnp.sum(dc, axis=0, keepdims=True)
        db_ref[...] = db

    return pl.pallas_call(
        body, name="forget_bwd", grid=(1,),
        in_specs=[ANY, pl.BlockSpec((s, LANES), lambda i: (0, 0)),
                  pl.BlockSpec((s, LANES), lambda i: (0, f_blk)), pl.BlockSpec((1, LANES), lambda i: (0, 0))],
        out_specs=[pl.BlockSpec((s, F_PAD), lambda i: (0, sec)), pl.BlockSpec((1, LANES), lambda i: (0, 0))],
        out_shape=[_sds(dgf.shape, BF16), _sds((1, LANES), F32)],
        input_output_aliases={0: 0},
        compiler_params=_params(("arbitrary",)),
    )(dgf, dcum, gf, b_pad)


def _rel_index():
    r = lax.broadcasted_iota(jnp.int32, (ATT_TQ, ATT_TK), 0)
    c = lax.broadcasted_iota(jnp.int32, (ATT_TQ, ATT_TK), 1)
    return r - c


def _qkv_specs(hb0, s):
    return [pl.BlockSpec((ATT_TQ, HEAD_DIM), lambda h, i: (i, 3 * (hb0 + h))),
            pl.BlockSpec((s, HEAD_DIM), lambda h, i: (0, 3 * (hb0 + h) + 1)),
            pl.BlockSpec((s, HEAD_DIM), lambda h, i: (0, 3 * (hb0 + h) + 2))]


def _sb_fwd(qkv, n_heads):
    s = qkv.shape[0]
    scale = HEAD_DIM ** -0.5
    tq, tk = ATT_TQ, ATT_TK

    def body(q_ref, k_ref, v_ref, o_ref, tot_ref):
        h, i = pl.program_id(0), pl.program_id(1)

        @pl.when((h == 0) & (i == 0))
        def _():
            tot_ref[...] = jnp.zeros_like(tot_ref)

        q = q_ref[...]
        rel = _rel_index()
        upper = _tri(tk, lambda r, c: r > c)

        def step(n, carry):
            c, acc = carry
            kj = i - n
            rows = pl.ds(pl.multiple_of(kj * tk, tk), tk)
            z = _dot(q, k_ref[rows, :], "nt") * scale
            mask = rel > (kj - i) * tk
            lsz = _log_sigmoid(z)
            lk = jnp.where(mask, lsz - z, 0.0)
            hi, lo = _split2(lk)
            between = c + _dot(hi, upper) + _dot(lo, upper)
            w = jnp.where(mask, jnp.exp(lsz + between), 0.0)
            acc = acc + _dot(w, v_ref[rows, :])
            return c + jnp.sum(lk, axis=1, keepdims=True), acc

        c, acc = lax.fori_loop(0, i + 1, step, (jnp.zeros((tq, 1), F32), jnp.zeros((tq, HEAD_DIM), F32)))
        o_ref[...] = acc.astype(BF16)
        _lane_put(tot_ref, pl.ds(pl.multiple_of(i * tq, tq), tq), h, c)

    return pl.pallas_call(
        body, name="sb_fwd", grid=(n_heads, s // tq),
        in_specs=_qkv_specs(0, s),
        out_specs=[pl.BlockSpec((tq, HEAD_DIM), lambda h, i: (i, h)), pl.BlockSpec((s, LANES), lambda h, i: (0, 0))],
        out_shape=[_sds((s, n_heads * HEAD_DIM), BF16), _sds((s, LANES), F32)],
        compiler_params=_params(("arbitrary", "arbitrary")),
    )(qkv, qkv, qkv)


def _sb_bwd(qkv, do, tot, n_heads):
    s = qkv.shape[0]
    scale = HEAD_DIM ** -0.5
    tq, tk = ATT_TQ, ATT_TK
    nq = s // tq
    hd = HEAD_DIM

    def body(q_ref, k_ref, v_ref, do_ref, tot_ref, out_ref, dk_acc, dv_acc):
        h, i = pl.program_id(0), pl.program_id(1)

        @pl.when(i == 0)
        def _():
            dk_acc[...] = jnp.zeros_like(dk_acc)
            dv_acc[...] = jnp.zeros_like(dv_acc)

        q = q_ref[...]
        dout = do_ref[...]
        total = _lane_pick(tot_ref[...], h)
        rel = _rel_index()
        incl = _tri(tk, lambda r, c: r <= c)
        excl = _tri(tk, lambda r, c: r < c)

        def step(kj, carry):
            p_l, p_e, dq = carry
            rows = pl.ds(pl.multiple_of(kj * tk, tk), tk)
            k_t = k_ref[rows, :]
            z = _dot(q, k_t, "nt") * scale
            mask = rel > (kj - i) * tk
            lsz = _log_sigmoid(z)
            lk = jnp.where(mask, lsz - z, 0.0)
            hi, lo = _split2(lk)
            between = total - (p_l + _dot(hi, incl) + _dot(lo, incl))
            w = jnp.where(mask, jnp.exp(lsz + between), 0.0)
            e = _dot(dout, v_ref[rows, :], "nt") * w
            hi, lo = _split2(e)
            e_before = p_e + _dot(hi, excl) + _dot(lo, excl)
            sg = jnp.exp(lsz)
            dz = (jnp.where(mask, e * (1.0 - sg) - e_before * sg, 0.0) * scale).astype(BF16)
            dq = dq + _dot(dz, k_t)
            dk_acc[rows, :] += _dot(dz, q, "tn")
            dv_acc[rows, :] += _dot(w, dout, "tn")
            return p_l + jnp.sum(lk, axis=1, keepdims=True), p_e + jnp.sum(e, axis=1, keepdims=True), dq

        zero = jnp.zeros((tq, 1), F32)
        _, _, dq = lax.fori_loop(0, i + 1, step, (zero, zero, jnp.zeros((tq, hd), F32)))
        out_ref[pl.ds(pl.multiple_of(i * tq, tq), tq), pl.ds(0, hd)] = dq.astype(BF16)

        @pl.when(i == nq - 1)
        def _():
            out_ref[:, pl.ds(hd, hd)] = dk_acc[...].astype(BF16)
            out_ref[:, pl.ds(2 * hd, hd)] = dv_acc[...].astype(BF16)

    return pl.pallas_call(
        body, name="sb_bwd", grid=(n_heads, nq),
        in_specs=_qkv_specs(0, s) + [pl.BlockSpec((tq, hd), lambda h, i: (i, h)),
                                     pl.BlockSpec((tq, LANES), lambda h, i: (i, 0))],
        out_specs=pl.BlockSpec((s, 3 * hd), lambda h, i: (0, h)),
        out_shape=_sds(qkv.shape, BF16),
        scratch_shapes=[pltpu.VMEM((s, hd), F32), pltpu.VMEM((s, hd), F32)],
        compiler_params=_params(("arbitrary", "arbitrary")),
    )(qkv, qkv, qkv, do, tot)


def _fox_fwd(qkv, cum_col, cum_row, n_heads, hb0):
    s = qkv.shape[0]
    scale = HEAD_DIM ** -0.5
    tq, tk = ATT_TQ, ATT_TK

    def body(q_ref, k_ref, v_ref, cc_ref, cr_ref, o_ref, o32_ref, lse_ref):
        h, i = pl.program_id(0), pl.program_id(1)

        @pl.when((h == 0) & (i == 0))
        def _():
            lse_ref[...] = jnp.zeros_like(lse_ref)

        q = q_ref[...]
        cq = _lane_pick(cc_ref[...], h)
        rel = _rel_index()

        def step(kj, carry):
            m, l, acc = carry
            rows = pl.ds(pl.multiple_of(kj * tk, tk), tk)
            ck = cr_ref[kj, pl.ds(h, 1), :]
            sc = _dot(q, k_ref[rows, :], "nt") * scale + cq - ck
            sc = jnp.where(rel >= (kj - i) * tk, sc, NEG_BIG)
            m_new = jnp.maximum(m, jnp.max(sc, axis=1, keepdims=True))
            p = jnp.exp(sc - m_new)
            alpha = jnp.exp(m - m_new)
            hi, lo = _split2(p)
            v_t = v_ref[rows, :]
            return (m_new, alpha * l + jnp.sum(p, axis=1, keepdims=True), alpha * acc + _dot(hi, v_t) + _dot(lo, v_t))

        m, l, acc = lax.fori_loop(0, i + 1, step, (jnp.full((tq, 1), NEG_BIG, F32), jnp.zeros((tq, 1), F32),
                                                   jnp.zeros((tq, HEAD_DIM), F32)))
        o = acc / l
        o_ref[...] = o.astype(BF16)
        o32_ref[...] = o
        _lane_put(lse_ref, pl.ds(pl.multiple_of(i * tq, tq), tq), h, m + jnp.log(l))

    nb = cum_row.shape[0]
    return pl.pallas_call(
        body, name="fox_fwd", grid=(n_heads, s // tq),
        in_specs=_qkv_specs(hb0, s) + [pl.BlockSpec((tq, LANES), lambda h, i: (i, 0)),
                                       pl.BlockSpec((nb, 8, tk), lambda h, i: (0, 0, 0))],
        out_specs=[pl.BlockSpec((tq, HEAD_DIM), lambda h, i: (i, h)), pl.BlockSpec((tq, HEAD_DIM), lambda h, i: (i, h)),
                   pl.BlockSpec((s, LANES), lambda h, i: (0, 0))],
        out_shape=[_sds((s, n_heads * HEAD_DIM), BF16), _sds((s, n_heads * HEAD_DIM), F32), _sds((s, LANES), F32)],
        compiler_params=_params(("arbitrary", "arbitrary")),
    )(qkv, qkv, qkv, cum_col, cum_row)


def _fox_bwd(dqkv, qkv, do, o, lse, cum_col, cum_row, n_heads, hb0):
    s = qkv.shape[0]
    scale = HEAD_DIM ** -0.5
    tq, tk = ATT_TQ, ATT_TK
    nq = s // tq
    hd = HEAD_DIM

    def body(dqkv_hbm, q_ref, k_ref, v_ref, do_ref, o_ref, lse_ref, cc_ref, cr_ref, out_ref, dc_ref,
             dk_acc, dv_acc, col_acc):
        del dqkv_hbm
        h, i = pl.program_id(0), pl.program_id(1)

        @pl.when((h == 0) & (i == 0))
        def _():
            dc_ref[...] = jnp.zeros_like(dc_ref)

        @pl.when(i == 0)
        def _():
            dk_acc[...] = jnp.zeros_like(dk_acc)
            dv_acc[...] = jnp.zeros_like(dv_acc)
            col_acc[...] = jnp.zeros_like(col_acc)

        q = q_ref[...]
        dout = do_ref[...]
        delta = jnp.sum(dout.astype(F32) * o_ref[...], axis=1, keepdims=True)
        lse_q = _lane_pick(lse_ref[...], h)
        cq = _lane_pick(cc_ref[...], h)
        rel = _rel_index()

        def step(kj, carry):
            dq, row_sum = carry
            rows = pl.ds(pl.multiple_of(kj * tk, tk), tk)
            k_t = k_ref[rows, :]
            ck = cr_ref[kj, pl.ds(h, 1), :]
            sc = _dot(q, k_t, "nt") * scale + cq - ck
            p = jnp.where(rel >= (kj - i) * tk, jnp.exp(sc - lse_q), 0.0)
            ds_f = p * (_dot(dout, v_ref[rows, :], "nt") - delta)
            col_acc[kj] += jnp.broadcast_to(jnp.sum(ds_f, axis=0, keepdims=True), (8, tk))
            ds = (ds_f * scale).astype(BF16)
            dk_acc[rows, :] += _dot(ds, q, "tn")
            dv_acc[rows, :] += _dot(p, dout, "tn")
            return dq + _dot(ds, k_t), row_sum + jnp.sum(ds_f, axis=1, keepdims=True)

        dq, row_sum = lax.fori_loop(0, i + 1, step, (jnp.zeros((tq, hd), F32), jnp.zeros((tq, 1), F32)))
        q_rows = pl.ds(pl.multiple_of(i * tq, tq), tq)
        out_ref[q_rows, pl.ds(0, hd)] = dq.astype(BF16)
        _lane_put(dc_ref, q_rows, h, row_sum)

        @pl.when(i == nq - 1)
        def _():
            out_ref[:, pl.ds(hd, hd)] = dk_acc[...].astype(BF16)
            out_ref[:, pl.ds(2 * hd, hd)] = dv_acc[...].astype(BF16)
            lane = lax.broadcasted_iota(jnp.int32, (tk, LANES), 1)
            for kj in range(nb):
                col = jnp.broadcast_to(col_acc[kj][0:1, :], (LANES, tk)).T
                old = dc_ref[pl.ds(kj * tk, tk), :]
                dc_ref[pl.ds(kj * tk, tk), :] = jnp.where(lane == h, old - col, old)

    nb = cum_row.shape[0]
    return pl.pallas_call(
        body, name="fox_bwd", grid=(n_heads, nq),
        in_specs=[ANY] + _qkv_specs(hb0, s) + [
            pl.BlockSpec((tq, hd), lambda h, i: (i, h)), pl.BlockSpec((tq, hd), lambda h, i: (i, h)),
            pl.BlockSpec((tq, LANES), lambda h, i: (i, 0)), pl.BlockSpec((tq, LANES), lambda h, i: (i, 0)),
            pl.BlockSpec((nb, 8, tk), lambda h, i: (0, 0, 0))],
        out_specs=[pl.BlockSpec((s, 3 * hd), lambda h, i: (0, hb0 + h)), pl.BlockSpec((s, LANES), lambda h, i: (0, 0))],
        out_shape=[_sds(dqkv.shape, BF16), _sds((s, LANES), F32)],
        scratch_shapes=[pltpu.VMEM((s, hd), F32), pltpu.VMEM((s, hd), F32), pltpu.VMEM((s // tk, 8, tk), F32)],
        input_output_aliases={0: 0},
        compiler_params=_params(("arbitrary", "arbitrary")),
    )(dqkv, qkv, qkv, qkv, do, o, lse, cum_col, cum_row)


def _branch_merge(o_sb, o_fx, w_sb, w_fx, gf, tm=1024):
    s = o_sb.shape[0]
    cs = w_sb.shape[2]
    tm = _tile(s, tm)

    def body(osb_ref, ofx_ref, wsb_ref, wfx_ref, g_ref, merged_ref, asb_ref, afx_ref):
        a_sb = _dot(osb_ref[...], wsb_ref[...])
        a_fx = _dot(ofx_ref[...], wfx_ref[...])
        g = g_ref[...]
        merged_ref[...] = (_sigmoid(g[:, :cs]) * a_sb + _sigmoid(g[:, cs:]) * a_fx).astype(BF16)
        asb_ref[...] = a_sb.astype(BF16)
        afx_ref[...] = a_fx.astype(BF16)

    blk = pl.BlockSpec((tm, cs), lambda i, j: (i, j))
    out = _sds((s, N_DEV * cs), BF16)
    return pl.pallas_call(
        body, name="branch_merge", grid=(s // tm, N_DEV),
        in_specs=[pl.BlockSpec((tm, o_sb.shape[1]), lambda i, j: (i, 0)),
                  pl.BlockSpec((tm, o_fx.shape[1]), lambda i, j: (i, 0)),
                  pl.BlockSpec((None,) + w_sb.shape[1:], lambda i, j: (j, 0, 0)),
                  pl.BlockSpec((None,) + w_fx.shape[1:], lambda i, j: (j, 0, 0)),
                  pl.BlockSpec((tm, 2 * cs), lambda i, j: (i, j))],
        out_specs=[blk, blk, blk], out_shape=[out, out, out],
        compiler_params=_params(("parallel", "arbitrary")),
    )(o_sb, o_fx, w_sb, w_fx, gf)


def _merge_bwd(dmix, w_out, gf, a_sb, a_fx, tm=1024, tk=512):
    s, d = dmix.shape
    cs = d // N_DEV
    tm, tk = _tile(s, tm), _tile(d, tk)

    def epilogue(acc, ex, outs):
        g, a_sb, a_fx = ex[0][...], ex[1][...].astype(F32), ex[2][...].astype(F32)
        s_sb, s_fx = _sigmoid(g[:, :cs]), _sigmoid(g[:, cs:])
        outs[0][...] = (acc * s_sb).astype(BF16)
        outs[1][...] = (acc * s_fx).astype(BF16)
        outs[2][...] = jnp.concatenate([acc * a_sb * s_sb * (1.0 - s_sb), acc * a_fx * s_fx * (1.0 - s_fx)],
                                       axis=1).astype(BF16)

    blk = pl.BlockSpec((tm, cs), lambda i, j, k: (i, j))
    wide = pl.BlockSpec((tm, 2 * cs), lambda i, j, k: (i, j))
    return _matmul(
        "merge_bwd", "nt",
        [(dmix, pl.BlockSpec((tm, tk), lambda i, j, k: (i, k)), w_out, pl.BlockSpec((cs, tk), lambda i, j, k: (j, k)))],
        (s // tm, N_DEV, d // tk), (tm, cs),
        [_sds((s, d), BF16), _sds((s, d), BF16), _sds(gf.shape, BF16)], [blk, blk, wide],
        extras=[(gf, wide), (a_sb, blk), (a_fx, blk)], epilogue=epilogue)


def _ffn_up(u2, w_gate, w_up, tm=1024):
    s, d = u2.shape
    fs = w_gate.shape[2]
    tm = _tile(s, tm)

    def body(u_ref, wg_ref, wu_ref, gate_ref, up_ref, act_ref):
        u = u_ref[...]
        gate = _dot(u, wg_ref[...])
        up = _dot(u, wu_ref[...])
        gate_ref[...] = gate
        up_ref[...] = up
        act_ref[...] = (gate * _sigmoid(gate) * up).astype(BF16)

    w_spec = pl.BlockSpec((None, d, fs), lambda i, j: (j, 0, 0))
    o_spec = pl.BlockSpec((None, tm, fs), lambda i, j: (j, i, 0))
    return pl.pallas_call(
        body, name="ffn_up", grid=(s // tm, N_DEV),
        in_specs=[pl.BlockSpec((tm, d), lambda i, j: (i, 0)), w_spec, w_spec],
        out_specs=[o_spec, o_spec, o_spec],
        out_shape=[_sds((N_DEV, s, fs), F32), _sds((N_DEV, s, fs), F32), _sds((N_DEV, s, fs), BF16)],
        compiler_params=_params(("parallel", "arbitrary")),
    )(u2, w_gate, w_up)


def _ffn_down_bwd(dff, w_down, gate, up, tm=1024):
    s, d = dff.shape
    fs = w_down.shape[1]
    tm = _tile(s, tm)

    def body(dff_ref, wd_ref, gate_ref, up_ref, dgate_ref, dup_ref):
        dact = _dot(dff_ref[...], wd_ref[...], "nt")
        gate = gate_ref[...]
        sg = _sigmoid(gate)
        dup_ref[...] = (dact * gate * sg).astype(BF16)
        dgate_ref[...] = (dact * up_ref[...] * sg * (1.0 + gate * (1.0 - sg))).astype(BF16)

    a_spec = pl.BlockSpec((None, tm, fs), lambda i, j: (j, i, 0))
    return pl.pallas_call(
        body, name="ffn_down_bwd", grid=(s // tm, N_DEV),
        in_specs=[pl.BlockSpec((tm, d), lambda i, j: (i, 0)), pl.BlockSpec((None, fs, d), lambda i, j: (j, 0, 0)),
                  a_spec, a_spec],
        out_specs=[a_spec, a_spec],
        out_shape=[_sds((N_DEV, s, fs), BF16), _sds((N_DEV, s, fs), BF16)],
        compiler_params=_params(("parallel", "arbitrary")),
    )(dff, w_down, gate, up)


def _mesh_place():
    x, y, c = lax.axis_index("x"), lax.axis_index("y"), lax.axis_index("c")
    peers = []
    for d in range(1, N_DEV):
        px = 1 - x if d & 4 else x
        py = 1 - y if d & 2 else y
        pc = 1 - c if d & 1 else c
        peers.append((d, (px, py, pc), 4 * px + 2 * py + pc))
    return 4 * x + 2 * y + c, peers


def _exchange(name, arrays, scatter):
    n = len(arrays)

    def body(*refs):
        ins, outs = refs[:n], refs[n:2 * n]
        send, recv, local = refs[2 * n:]
        me, peers = _mesh_place()

        def src(a, p):
            return ins[a].at[p] if scatter else ins[a]

        own = [pltpu.make_async_copy(src(a, me), outs[a].at[me], local.at[a]) for a in range(n)]
        for cp in own:
            cp.start()
        sent = []
        for a in range(n):
            for d, dev, flat in peers:
                cp = pltpu.make_async_remote_copy(src_ref=src(a, flat), dst_ref=outs[a].at[me], send_sem=send.at[a, d],
                                                  recv_sem=recv.at[a, d], device_id=dev, device_id_type=MESH)
                cp.start()
                sent.append(cp)
        for a in range(n):
            for d, dev, flat in peers:
                pltpu.make_async_remote_copy(src_ref=src(a, flat), dst_ref=outs[a].at[flat], send_sem=send.at[a, d],
                                             recv_sem=recv.at[a, d], device_id=dev, device_id_type=MESH).wait_recv()
        for cp in sent:
            cp.wait_send()
        for cp in own:
            cp.wait()

    def out_shape(a):
        return _sds(a.shape if scatter else (N_DEV,) + a.shape, a.dtype)

    return pl.pallas_call(
        body, name=name, in_specs=[ANY] * n, out_specs=[ANY] * n, out_shape=[out_shape(a) for a in arrays],
        scratch_shapes=[pltpu.SemaphoreType.DMA((n, N_DEV)), pltpu.SemaphoreType.DMA((n, N_DEV)),
                        pltpu.SemaphoreType.DMA((n,))],
    )(*arrays)


def _flat_me():
    return 4 * lax.axis_index("x") + 2 * lax.axis_index("y") + lax.axis_index("c")


def _after(value, token):
    return lax.optimization_barrier((value, token))[0]


def _in_hbm(a):
    return pltpu.with_memory_space_constraint(a, pltpu.HBM)


def _scatter_start(name, parts):
    n = len(parts)
    me = _flat_me()
    lands = [lax.dynamic_update_slice_in_dim(lax.empty(a.shape, a.dtype), lax.dynamic_slice_in_dim(a, me, 1, 0), me, 0)
             for a in parts]

    def body(*refs):
        ins, lnd = refs[:n], refs[n:2 * n]
        send, recv = refs[2 * n], refs[2 * n + 1]
        token = refs[-1]
        mine, peers = _mesh_place()
        for a in range(n):
            for d, dev, flat in peers:
                pltpu.make_async_remote_copy(src_ref=ins[a].at[flat], dst_ref=lnd[a].at[mine], send_sem=send.at[a * N_DEV + d],
                                             recv_sem=recv.at[a * N_DEV + d], device_id=dev, device_id_type=MESH).start()
        token[...] = jnp.zeros_like(token)

    res = pl.pallas_call(
        body, name=name,
        out_shape=[pltpu.SemaphoreType.DMA((n * N_DEV,)), pltpu.SemaphoreType.DMA((n * N_DEV,))]
        + [pltpu.HBM(a.shape, a.dtype) for a in parts] * 2 + [_sds((8, LANES), F32)],
        in_specs=[HBM] * (2 * n), out_specs=[SEM, SEM] + [HBM] * (2 * n) + [pl.BlockSpec(memory_space=pltpu.VMEM)],
        input_output_aliases={i: 2 + i for i in range(2 * n)},
        compiler_params=pltpu.CompilerParams(has_side_effects=EFFECT),
    )(*[_in_hbm(a) for a in parts], *[_in_hbm(a) for a in lands])
    return res[0], res[1], res[2:2 + n], res[2 + n:2 + 2 * n], res[-1]


def _scatter_wait(name, send, recv, parts, lands, after):
    n = len(parts)

    def body(*refs):
        ins, lnd = refs[:n], refs[n:2 * n]
        send_sem, recv_sem = refs[2 * n], refs[2 * n + 1]
        mine, peers = _mesh_place()
        for a in range(n):
            for d, dev, flat in peers:
                cp = pltpu.make_async_remote_copy(src_ref=ins[a].at[flat], dst_ref=lnd[a].at[flat],
                                                  send_sem=send_sem.at[a * N_DEV + d], recv_sem=recv_sem.at[a * N_DEV + d],
                                                  device_id=dev, device_id_type=MESH)
                cp.wait_send()
                cp.wait_recv()

    res = pl.pallas_call(
        body, name=name,
        out_shape=[pltpu.HBM(a.shape, a.dtype) for a in parts] * 2,
        in_specs=[HBM] * (2 * n) + [SEM, SEM, ANY], out_specs=[HBM] * (2 * n),
        input_output_aliases={i: i for i in range(2 * n)},
        compiler_params=pltpu.CompilerParams(has_side_effects=EFFECT),
    )(*parts, *lands, send, recv, after)
    return res[n:]


def _adamw(g, w, m, v):
    m = ADAM_B1 * m + (1.0 - ADAM_B1) * g
    v = ADAM_B2 * v + (1.0 - ADAM_B2) * (g * g)
    m_hat = m / (1.0 - ADAM_B1 ** ADAM_STEP)
    v_hat = v / (1.0 - ADAM_B2 ** ADAM_STEP)
    delta = -ADAM_LR * (m_hat / (jnp.sqrt(v_hat) + ADAM_EPS) + ADAM_WD * w)
    return delta, m, v


def _update(name, parts, w, m, v, block_bytes=1 << 20):
    r, c = w.shape
    tr = max(8, min(r, (block_bytes // (4 * c)) // 8 * 8))
    while r % tr:
        tr -= 8

    def body(p_ref, w_ref, m_ref, v_ref, g_ref, d_ref, nm_ref, nv_ref):
        g = p_ref[0].astype(F32)
        for p in range(1, N_DEV):
            g = g + p_ref[p].astype(F32)
        g_ref[...] = g
        d_ref[...], nm_ref[...], nv_ref[...] = _adamw(g, w_ref[...], m_ref[...], v_ref[...])

    blk = pl.BlockSpec((tr, c), lambda i: (i, 0))
    return pl.pallas_call(
        body, name=name, grid=(r // tr,),
        in_specs=[pl.BlockSpec((N_DEV, tr, c), lambda i: (0, i, 0)), blk, blk, blk],
        out_specs=[blk] * 4, out_shape=[_sds((r, c), F32)] * 4,
        compiler_params=_params(("parallel",)),
    )(parts, w, m, v)


def _small_update(part, w, m, v):
    n = part.shape[1]

    def body(p_ref, w_ref, m_ref, v_ref, g_ref, d_ref, nm_ref, nv_ref, buf, send, recv):
        me, peers = _mesh_place()
        buf[me] = p_ref[...]
        sent = []
        for d, dev, flat in peers:
            cp = pltpu.make_async_remote_copy(src_ref=p_ref, dst_ref=buf.at[me], send_sem=send.at[d],
                                              recv_sem=recv.at[d], device_id=dev, device_id_type=MESH)
            cp.start()
            sent.append(cp)
        for d, dev, flat in peers:
            pltpu.make_async_remote_copy(src_ref=p_ref, dst_ref=buf.at[flat], send_sem=send.at[d],
                                         recv_sem=recv.at[d], device_id=dev, device_id_type=MESH).wait_recv()
        for cp in sent:
            cp.wait_send()
        g = buf[0]
        for p in range(1, N_DEV):
            g = g + buf[p]
        g_ref[...] = g
        d_ref[...], nm_ref[...], nv_ref[...] = _adamw(g, w_ref[...], m_ref[...], v_ref[...])

    vm = pl.BlockSpec(memory_space=pltpu.VMEM)
    return pl.pallas_call(
        body, name="small_update", in_specs=[vm] * 4, out_specs=[vm] * 4, out_shape=[_sds((1, n), F32)] * 4,
        scratch_shapes=[pltpu.VMEM((N_DEV, 1, n), F32), pltpu.SemaphoreType.DMA((N_DEV,)),
                        pltpu.SemaphoreType.DMA((N_DEV,))],
    )(part, w, m, v)


def _w_in_reorder(g_in, d_sb, d_fox, n_f, d):
    full = jnp.transpose(g_in, (1, 0, 2)).reshape(d, -1)
    cs = d // N_DEV

    def heads(sec, width):
        return sec.reshape(d, 3, width // HEAD_DIM, HEAD_DIM).transpose(0, 2, 1, 3).reshape(d, 3 * width)

    o1 = 3 * d_sb
    o2 = o1 + 3 * d_fox
    o3 = o2 + n_f
    gates = jnp.stack([full[:, o3:o3 + d].reshape(d, N_DEV, cs), full[:, o3 + d:].reshape(d, N_DEV, cs)], axis=2)
    f_sec = jnp.pad(full[:, o2:o3], ((0, 0), (0, F_PAD - n_f)))
    return jnp.concatenate([heads(full[:, :o1], d_sb), heads(full[:, o1:o2], d_fox), gates.reshape(d, 2 * d), f_sec], axis=1)


def _w_in_restore(dwq, dwgf, d_sb, d_fox, n_f, d):
    cs = d // N_DEV

    def heads(sec, width):
        return sec.reshape(d, width // HEAD_DIM, 3, HEAD_DIM).transpose(0, 2, 1, 3).reshape(d, 3 * width)

    gates = dwgf[:, :2 * d].reshape(d, N_DEV, 2, cs)
    full = jnp.concatenate([heads(dwq[:, :3 * d_sb], d_sb), heads(dwq[:, 3 * d_sb:], d_fox),
                            dwgf[:, 2 * d:2 * d + n_f], gates[:, :, 0].reshape(d, d), gates[:, :, 1].reshape(d, d)], axis=1)
    return jnp.transpose(full.reshape(d, N_DEV, -1), (1, 0, 2))


def kernel(x, norm_mix_pre, norm_mix_post, w_in, b_forget, w_branch_sb, w_branch_fox, w_out, norm_ffn_pre, norm_ffn_post, w_ffn_gate, w_ffn_up, w_ffn_down, loss_target, m_norm_mix_pre, m_norm_mix_post, m_w_in, m_b_forget, m_w_branch_sb, m_w_branch_fox, m_w_out, m_norm_ffn_pre, m_norm_ffn_post, m_w_ffn_gate, m_w_ffn_up, m_w_ffn_down, v_norm_mix_pre, v_norm_mix_post, v_w_in, v_b_forget, v_w_branch_sb, v_w_branch_fox, v_w_out, v_norm_ffn_pre, v_norm_ffn_post, v_w_ffn_gate, v_w_ffn_up, v_w_ffn_down):
    xs, target = x[0], loss_target[0]
    s, d = xs.shape
    d_sb, d_fox = w_branch_sb.shape[1], w_branch_fox.shape[1]
    h_sb, h_fox = d_sb // HEAD_DIM, d_fox // HEAD_DIM
    n_f = b_forget.shape[1]
    fs = w_ffn_gate.shape[2]
    cs = d // N_DEV
    n_qkv = 3 * (d_sb + d_fox)
    n_gf = 2 * d + F_PAD
    f_blk = 2 * d // LANES
    big = (w_in, w_branch_sb, w_branch_fox, w_out, w_ffn_gate, w_ffn_up, w_ffn_down)
    big_m = (m_w_in, m_w_branch_sb, m_w_branch_fox, m_w_out, m_w_ffn_gate, m_w_ffn_up, m_w_ffn_down)
    big_v = (v_w_in, v_w_branch_sb, v_w_branch_fox, v_w_out, v_w_ffn_gate, v_w_ffn_up, v_w_ffn_down)

    g_in, g_sb, g_fx, g_out, g_gate, g_up, g_down = _exchange(
        "gather_weights", [w[0].astype(BF16) for w in big], scatter=False)
    w_cat = _w_in_reorder(g_in, d_sb, d_fox, n_f, d)
    w_out_full = g_out.reshape(d, d)
    b_pad = jnp.pad(b_forget, ((0, 0), (0, LANES - n_f)))

    u = _pre_norm(xs, norm_mix_pre)
    qkv = _mm_plain("proj_qkv", "nn", u, w_cat, BF16, n=n_qkv)
    gf = _mm_plain("proj_gates", "nn", u, w_cat, F32, n_off=n_qkv, n=n_gf)
    cum_col, cum_row = _forget_fwd(gf, b_pad, f_blk)
    o_sb, tot = _sb_fwd(qkv, h_sb)
    o_fx, o_fx32, lse = _fox_fwd(qkv, cum_col, cum_row, h_fox, h_sb)
    merged, a_sb, a_fx = _branch_merge(o_sb, o_fx, g_sb, g_fx, gf)
    mix = _mm_plain("out_proj", "nn", merged, w_out_full, F32)
    h1, u2 = _mid_norms(xs, mix, norm_mix_post, norm_ffn_pre)
    gate, up, act = _ffn_up(u2, g_gate, g_up)
    tm, tn = _tile(s, 1024), _tile(d, 512)
    ff = _matmul("ffn_down", "nn",
                 [(act, pl.BlockSpec((None, tm, fs), lambda i, j, k: (k, i, 0)),
                   g_down, pl.BlockSpec((None, fs, tn), lambda i, j, k: (k, 0, j)))],
                 (s // tm, d // tn, N_DEV), (tm, tn), _sds((s, d), F32), pl.BlockSpec((tm, tn), lambda i, j, k: (i, j)))
    loss_part, dy, dff, dg_ffn_post = _loss_head(h1, ff, target, norm_ffn_post)

    dgate, dup = _ffn_down_bwd(dff, g_down, gate, up)
    tk = _tile(s, 512)
    dw_down = _matmul("dw_down", "tn",
                      [(act, pl.BlockSpec((None, tk, fs), lambda j, n, k: (j, k, 0)),
                        dff, pl.BlockSpec((tk, tn), lambda j, n, k: (k, n)))],
                      (N_DEV, d // tn, s // tk), (fs, tn), _sds((N_DEV, fs, d), BF16),
                      pl.BlockSpec((None, fs, tn), lambda j, n, k: (j, 0, n)))

    def dw_up(name, dact):
        return _matmul(name, "tn",
                       [(u2, pl.BlockSpec((tk, tn), lambda j, i, k: (k, i)),
                         dact, pl.BlockSpec((None, tk, fs), lambda j, i, k: (j, k, 0)))],
                       (N_DEV, d // tn, s // tk), (tn, fs), _sds((N_DEV, d, fs), BF16),
                       pl.BlockSpec((None, tn, fs), lambda j, i, k: (j, i, 0)))

    dw_gate, dw_upw = dw_up("dw_gate", dgate), dw_up("dw_up", dup)
    rs_ffn = _scatter_start("scatter_ffn", [dw_gate, dw_upw, dw_down])
    a_spec = pl.BlockSpec((None, tm, fs), lambda i, j, k: (k, i, 0))
    b_spec = pl.BlockSpec((None, tn, fs), lambda i, j, k: (k, j, 0))
    du2 = _matmul("du2", "nt", [(_after(dgate, rs_ffn[4]), a_spec, g_gate, b_spec), (dup, a_spec, g_up, b_spec)],
                  (s // tm, d // tn, N_DEV), (tm, tn), _sds((s, d), F32), pl.BlockSpec((tm, tn), lambda i, j, k: (i, j)))
    dh1, dmix, dg_ffn_pre, dg_mix_post = _mid_norms_bwd(dy, du2, h1, mix, norm_ffn_pre, norm_mix_post)

    da_sb, da_fx, dgf = _merge_bwd(dmix, w_out_full, gf, a_sb, a_fx)
    dw_out = _mm_plain("dw_out", "tn", merged, dmix, BF16).reshape(N_DEV, cs, d)

    def branch_bwd(tag, da, w_b, o_b, width):
        tb = _tile(width, 512)
        do = _matmul("do_" + tag, "nt",
                     [(da, pl.BlockSpec((tm, cs), lambda i, j, k: (i, k)),
                       w_b, pl.BlockSpec((None, tb, cs), lambda i, j, k: (k, j, 0)))],
                     (s // tm, width // tb, N_DEV), (tm, tb), _sds((s, width), BF16),
                     pl.BlockSpec((tm, tb), lambda i, j, k: (i, j)))
        dw = _matmul("dw_" + tag, "tn",
                     [(o_b, pl.BlockSpec((tk, tb), lambda j, i, k: (k, i)),
                       da, pl.BlockSpec((tk, cs), lambda j, i, k: (k, j)))],
                     (N_DEV, width // tb, s // tk), (tb, cs), _sds((N_DEV, width, cs), BF16),
                     pl.BlockSpec((None, tb, cs), lambda j, i, k: (j, i, 0)))
        return do, dw

    do_sb, dw_sb = branch_bwd("sb", da_sb, g_sb, o_sb, d_sb)
    do_fx, dw_fx = branch_bwd("fox", da_fx, g_fx, o_fx, d_fox)

    rs_mid = _scatter_start("scatter_mid", [dw_sb, dw_fx, dw_out])

    dqkv = _sb_bwd(qkv, _after(do_sb, rs_mid[4]), tot, h_sb)
    dqkv, dcum = _fox_bwd(dqkv, qkv, do_fx, o_fx32, lse, cum_col, cum_row, h_fox, h_sb)
    dgf, db_part = _forget_bwd(dgf, dcum, gf, b_pad, f_blk)
    du = _mm_plain("du_qkv", "nt", dqkv, w_cat, F32)
    du = _mm_plain("du_gates", "nt", dgf, w_cat, F32, k_off=n_qkv, init=du)
    dw_in = _w_in_restore(_mm_plain("dw_qkv", "tn", u, dqkv, BF16), _mm_plain("dw_gates", "tn", u, dgf, BF16),
                          d_sb, d_fox, n_f, d)
    rs_in = _scatter_start("scatter_in", [dw_in])
    dx, dg_mix_pre = _pre_norm_bwd(dh1, _after(du, rs_in[4]), xs, norm_mix_pre)

    upd = {}

    def update_group(tag, rs, names, after):
        parts = _scatter_wait("scatter_" + tag + "_wait", *rs[:4], after=after)
        for nm, p in zip(names, parts):
            w, m, v = weights[nm]
            upd[nm] = [o[None] for o in _update("update_" + nm, p, w[0], m[0], v[0])]

    weights = dict(zip(("w_in", "w_branch_sb", "w_branch_fox", "w_out", "w_ffn_gate", "w_ffn_up", "w_ffn_down"),
                       zip(big, big_m, big_v)))
    update_group("ffn", rs_ffn, ("w_ffn_gate", "w_ffn_up", "w_ffn_down"), dx)
    update_group("mid", rs_mid, ("w_branch_sb", "w_branch_fox", "w_out"), upd["w_ffn_down"][0])
    update_group("in", rs_in, ("w_in",), upd["w_out"][0])

    small = ((norm_mix_pre, m_norm_mix_pre, v_norm_mix_pre), (norm_mix_post, m_norm_mix_post, v_norm_mix_post),
             (norm_ffn_pre, m_norm_ffn_pre, v_norm_ffn_pre), (norm_ffn_post, m_norm_ffn_post, v_norm_ffn_post))
    pad_f = ((0, 0), (0, LANES - n_f))
    cat = lambda i: jnp.concatenate([t[i] for t in small] + [jnp.pad((b_forget, m_b_forget, v_b_forget)[i], pad_f)], axis=1)
    sm = _small_update(jnp.concatenate([dg_mix_pre, dg_mix_post, dg_ffn_pre, dg_ffn_post, db_part], axis=1),
                       cat(0), cat(1), cat(2))
    for i, nm in enumerate(("norm_mix_pre", "norm_mix_post", "norm_ffn_pre", "norm_ffn_post")):
        upd[nm] = [o[:, i * d:(i + 1) * d] for o in sm]
    upd["b_forget"] = [o[:, 4 * d:4 * d + n_f] for o in sm]

    loss = lax.psum(loss_part[0, 0], ("x", "y", "c"))
    order = ("norm_mix_pre", "norm_mix_post", "w_in", "b_forget", "w_branch_sb", "w_branch_fox", "w_out",
             "norm_ffn_pre", "norm_ffn_post", "w_ffn_gate", "w_ffn_up", "w_ffn_down")
    return (loss, dx[None]) + tuple(upd[nm][i] for i in range(4) for nm in order)
```

```python
import jax
import jax.numpy as jnp
from jax import lax
from jax.experimental import pallas as pl
from jax.experimental.pallas import tpu as pltpu

F32 = jnp.float32
BF16 = jnp.bfloat16
MESH = pl.DeviceIdType.MESH
ANY = pl.BlockSpec(memory_space=pl.ANY)
HBM = pl.BlockSpec(memory_space=pltpu.HBM)
SEM = pl.BlockSpec(memory_space=pltpu.SEMAPHORE)
EFFECT = pltpu.SideEffectType.DATAFLOW_SIDE_EFFECTING

N_DEV = 8
HEAD_DIM = 128
RMS_EPS = 1e-6
F_PAD = 512
LANES = 128
ATT_TQ = 256
ATT_TK = 256
NEG_BIG = -1e30
VMEM_LIMIT = 56 * 1024 * 1024

ADAM_LR = 0.001
ADAM_B1 = 0.9
ADAM_B2 = 0.999
ADAM_EPS = 1e-08
ADAM_WD = 0.01
ADAM_STEP = 10

_DIMS = {"nn": ((1,), (0,)), "nt": ((1,), (1,)), "tn": ((0,), (0,))}


def _params(sem):
    return pltpu.CompilerParams(dimension_semantics=sem, vmem_limit_bytes=VMEM_LIMIT)


def _dot(a, b, mode="nn"):
    return lax.dot_general(a.astype(BF16), b.astype(BF16), (_DIMS[mode], ((), ())), preferred_element_type=F32)


def _tile(n, pref):
    if n <= pref:
        return n
    t = (pref // LANES) * LANES
    while n % t:
        t -= LANES
    return t


def _split2(v):
    hi = v.astype(BF16)
    return hi, (v - hi.astype(F32)).astype(BF16)


def _split3(v):
    a = v.astype(BF16)
    r = v - a.astype(F32)
    b = r.astype(BF16)
    return a, b, (r - b.astype(F32)).astype(BF16)


def _tri(n, cmp):
    r = lax.broadcasted_iota(jnp.int32, (n, n), 0)
    c = lax.broadcasted_iota(jnp.int32, (n, n), 1)
    return jnp.where(cmp(r, c), 1.0, 0.0).astype(BF16)


def _lane_pick(v, h):
    lane = lax.broadcasted_iota(jnp.int32, v.shape, 1)
    return jnp.sum(jnp.where(lane == h, v, 0.0), axis=1, keepdims=True)


def _lane_put(ref, rows, h, col):
    old = ref[rows, :]
    lane = lax.broadcasted_iota(jnp.int32, old.shape, 1)
    ref[rows, :] = jnp.where(lane == h, col, old)


def _sigmoid(z):
    return 1.0 / (1.0 + jnp.exp(-z))


def _log_sigmoid(z):
    return jnp.minimum(z, 0.0) - jnp.log(1.0 + jnp.exp(-jnp.abs(z)))


def _sds(shape, dtype):
    return jax.ShapeDtypeStruct(shape, dtype)


def _matmul(name, mode, pairs, grid, acc_shape, out_shape, out_specs, extras=(), epilogue=None, init=None, dep=None):
    n_p, n_e = len(pairs), len(extras)
    nk = grid[-1]
    single = not isinstance(out_shape, (list, tuple))
    n_i = 0 if init is None else 1
    n_d = 0 if dep is None else 1

    def body(*refs):
        ab = refs[:2 * n_p]
        ex = refs[2 * n_p:2 * n_p + n_e]
        ini = refs[2 * n_p + n_e:2 * n_p + n_e + n_i]
        outs = refs[2 * n_p + n_e + n_i + n_d:-1]
        acc = refs[-1]
        k = pl.program_id(len(grid) - 1)

        @pl.when(k == 0)
        def _():
            acc[...] = jnp.zeros_like(acc) if init is None else ini[0][...].astype(F32)

        t = _dot(ab[0][...], ab[1][...], mode)
        for p in range(1, n_p):
            t = t + _dot(ab[2 * p][...], ab[2 * p + 1][...], mode)
        acc[...] += t

        @pl.when(k == nk - 1)
        def _():
            if epilogue is None:
                outs[0][...] = acc[...].astype(outs[0].dtype)
            else:
                epilogue(acc[...], ex, outs)

    in_specs = [s for (_, sa, _, sb) in pairs for s in (sa, sb)] + [s for (_, s) in extras]
    args = [v for (a, _, b, _) in pairs for v in (a, b)] + [e for (e, _) in extras]
    if init is not None:
        in_specs.append(init[1])
        args.append(init[0])
    if dep is not None:
        in_specs.append(ANY)
        args.append(dep)
    return pl.pallas_call(
        body, name=name, grid=grid, in_specs=in_specs,
        out_specs=out_specs if single else list(out_specs),
        out_shape=out_shape if single else list(out_shape),
        scratch_shapes=[pltpu.VMEM(acc_shape, F32)],
        compiler_params=_params(("parallel",) * (len(grid) - 1) + ("arbitrary",)),
    )(*args)


def _mm_plain(name, mode, a, b, out_dtype, *, n_off=0, n=None, k_off=0, tm=1024, tn=512, tk=512, init=None):
    if mode == "nn":
        (m, kk), nn_ = a.shape, b.shape[1]
    elif mode == "nt":
        (m, kk), nn_ = a.shape, b.shape[0]
    else:
        (kk, m), nn_ = a.shape, b.shape[1]
    n = nn_ if n is None else n
    tm, tn, tk = _tile(m, tm), _tile(n, tn), _tile(kk, tk)
    assert n_off % tn == 0 and k_off % tk == 0
    off, koff = n_off // tn, k_off // tk
    a_spec = {"nn": pl.BlockSpec((tm, tk), lambda i, j, k: (i, k)),
              "nt": pl.BlockSpec((tm, tk), lambda i, j, k: (i, k)),
              "tn": pl.BlockSpec((tk, tm), lambda i, j, k: (k, i))}[mode]
    b_spec = {"nn": pl.BlockSpec((tk, tn), lambda i, j, k: (k, j + off)),
              "nt": pl.BlockSpec((tn, tk), lambda i, j, k: (j, k + koff)),
              "tn": pl.BlockSpec((tk, tn), lambda i, j, k: (k, j))}[mode]
    o_spec = pl.BlockSpec((tm, tn), lambda i, j, k: (i, j))
    if init is not None:
        init = (init, o_spec)
    return _matmul(name, mode, [(a, a_spec, b, b_spec)], (m // tm, n // tn, kk // tk), (tm, tn),
                   _sds((m, n), out_dtype), o_spec, init=init)


def _rows_call(name, body, ins, outs, s, tr=256, dep=None):
    def spec(v, per_row):
        if per_row:
            return pl.BlockSpec((tr, v.shape[1]), lambda i: (i, 0))
        return pl.BlockSpec(v.shape, lambda i: (0, 0))
    n_in = len(ins)
    deps = [] if dep is None else [dep]

    def with_dep(*refs):
        body(*refs[:n_in], *refs[n_in + len(deps):])

    return pl.pallas_call(
        with_dep, name=name, grid=(s // tr,),
        in_specs=[spec(v, p) for v, p in ins] + [ANY] * len(deps), out_specs=[spec(v, p) for v, p in outs],
        out_shape=[_sds(v.shape, v.dtype) for v, _ in outs],
        compiler_params=_params(("arbitrary",)),
    )(*[v for v, _ in ins], *deps)


def _rsq(v):
    return lax.rsqrt(jnp.mean(v * v, axis=-1, keepdims=True) + RMS_EPS)


def _norm_bwd(dy, v, r, g):
    vh = v * r
    t = dy * g
    dv = r * (t - vh * jnp.mean(t * vh, axis=-1, keepdims=True))
    return dv, jnp.sum(dy * vh, axis=0, keepdims=True)


def _accum(ref, val):
    @pl.when(pl.program_id(0) == 0)
    def _():
        ref[...] = jnp.zeros_like(ref)
    ref[...] += val


def _pre_norm(x, g, dep=None):
    def body(x_ref, g_ref, u_ref):
        v = x_ref[...]
        u_ref[...] = (v * _rsq(v) * g_ref[...]).astype(BF16)
    s, d = x.shape
    return _rows_call("pre_norm", body, [(x, True), (g, False)], [(_sds((s, d), BF16), True)], s, dep=dep)[0]


def _mid_norms(x, mix, g_post, g_pre):
    def body(x_ref, mix_ref, gp_ref, gn_ref, h_ref, u_ref):
        mv = mix_ref[...]
        h = x_ref[...] + mv * _rsq(mv) * gp_ref[...]
        h_ref[...] = h
        u_ref[...] = (h * _rsq(h) * gn_ref[...]).astype(BF16)
    s, d = x.shape
    return _rows_call("mid_norms", body, [(x, True), (mix, True), (g_post, False), (g_pre, False)],
                      [(_sds((s, d), F32), True), (_sds((s, d), BF16), True)], s)


def _loss_head(h1, ff, target, g):
    s, d = h1.shape

    def body(h_ref, ff_ref, t_ref, g_ref, loss_ref, dy_ref, dff_ref, dg_ref):
        fv = ff_ref[...]
        r = _rsq(fv)
        err = h_ref[...] + fv * r * g_ref[...] - t_ref[...]
        part = 0.5 * jnp.sum(jnp.mean(err * err, axis=-1, keepdims=True), axis=0, keepdims=True)
        _accum(loss_ref, jnp.broadcast_to(part, loss_ref.shape))
        dy = err * (1.0 / d)
        dy_ref[...] = dy
        dff, dg = _norm_bwd(dy, fv, r, g_ref[...])
        dff_ref[...] = dff.astype(BF16)
        _accum(dg_ref, dg)

    return _rows_call("loss_head", body, [(h1, True), (ff, True), (target, True), (g, False)],
                      [(_sds((1, LANES), F32), False), (_sds((s, d), F32), True),
                       (_sds((s, d), BF16), True), (_sds((1, d), F32), False)], s)


def _mid_norms_bwd(dy, du2, h1, mix, g_pre, g_post):
    s, d = dy.shape

    def body(dy_ref, du_ref, h_ref, mix_ref, gn_ref, gp_ref, dh_ref, dmix_ref, dgn_ref, dgp_ref):
        h = h_ref[...]
        dh, dgn = _norm_bwd(du_ref[...], h, _rsq(h), gn_ref[...])
        dh = dh + dy_ref[...]
        dh_ref[...] = dh
        _accum(dgn_ref, dgn)
        mv = mix_ref[...]
        dmix, dgp = _norm_bwd(dh, mv, _rsq(mv), gp_ref[...])
        dmix_ref[...] = dmix.astype(BF16)
        _accum(dgp_ref, dgp)

    return _rows_call("mid_norms_bwd", body,
                      [(dy, True), (du2, True), (h1, True), (mix, True), (g_pre, False), (g_post, False)],
                      [(_sds((s, d), F32), True), (_sds((s, d), BF16), True),
                       (_sds((1, d), F32), False), (_sds((1, d), F32), False)], s)


def _pre_norm_bwd(dh1, du, x, g, dep=None):
    s, d = x.shape

    def body(dh_ref, du_ref, x_ref, g_ref, dx_ref, dg_ref):
        v = x_ref[...]
        dv, dg = _norm_bwd(du_ref[...], v, _rsq(v), g_ref[...])
        dx_ref[...] = dh_ref[...] + dv
        _accum(dg_ref, dg)

    return _rows_call("pre_norm_bwd", body, [(dh1, True), (du, True), (x, True), (g, False)],
                      [(_sds((s, d), F32), True), (_sds((1, d), F32), False)], s, dep=dep)


def _forget_fwd(gf, b_pad, f_blk):
    s = gf.shape[0]
    tb = ATT_TK
    nb = s // tb

    def body(f_ref, b_ref, col_ref, row_ref):
        incl = _tri(tb, lambda r, c: c <= r)
        carry = jnp.zeros((1, LANES), F32)
        for i in range(nb):
            lf = _log_sigmoid(f_ref[pl.ds(i * tb, tb), :] + b_ref[...])
            parts = _split3(lf)
            cum = carry + _dot(incl, parts[0]) + _dot(incl, parts[1]) + _dot(incl, parts[2])
            col_ref[pl.ds(i * tb, tb), :] = cum
            row_ref[i] = cum.T
            carry = carry + jnp.sum(lf, axis=0, keepdims=True)

    return pl.pallas_call(
        body, name="forget_fwd", grid=(1,),
        in_specs=[pl.BlockSpec((s, LANES), lambda i: (0, f_blk)), pl.BlockSpec((1, LANES), lambda i: (0, 0))],
        out_specs=[pl.BlockSpec((s, LANES), lambda i: (0, 0)), pl.BlockSpec((nb, LANES, tb), lambda i: (0, 0, 0))],
        out_shape=[_sds((s, LANES), F32), _sds((nb, LANES, tb), F32)],
        compiler_params=_params(("arbitrary",)),
    )(gf, b_pad)


def _forget_bwd(dgf, dcum, gf, b_pad, f_blk):
    s = gf.shape[0]
    tb = ATT_TK
    nb = s // tb
    sec = dgf.shape[1] // F_PAD - 1

    def body(dgf_hbm, dc_ref, f_ref, b_ref, out_ref, db_ref):
        del dgf_hbm
        incl = _tri(tb, lambda r, c: c >= r)
        carry = jnp.zeros((1, LANES), F32)
        db = jnp.zeros((1, LANES), F32)
        out_ref[...] = jnp.zeros_like(out_ref)
        for i in reversed(range(nb)):
            dc = dc_ref[pl.ds(i * tb, tb), :]
            parts = _split3(dc)
            dlf = carry + _dot(incl, parts[0]) + _dot(incl, parts[1]) + _dot(incl, parts[2])
            z = f_ref[pl.ds(i * tb, tb), :] + b_ref[...]
            df = dlf * _sigmoid(-z)
            out_ref[pl.ds(i * tb, tb), pl.ds(0, LANES)] = df.astype(BF16)
            db = db + jnp.sum(df, axis=0, keepdims=True)
            carry = carry + jnp.sum(dc, axis=0, keepdims=True)
        db_ref[...] = db

    return pl.pallas_call(
        body, name="forget_bwd", grid=(1,),
        in_specs=[ANY, pl.BlockSpec((s, LANES), lambda i: (0, 0)),
                  pl.BlockSpec((s, LANES), lambda i: (0, f_blk)), pl.BlockSpec((1, LANES), lambda i: (0, 0))],
        out_specs=[pl.BlockSpec((s, F_PAD), lambda i: (0, sec)), pl.BlockSpec((1, LANES), lambda i: (0, 0))],
        out_shape=[_sds(dgf.shape, BF16), _sds((1, LANES), F32)],
        input_output_aliases={0: 0},
        compiler_params=_params(("arbitrary",)),
    )(dgf, dcum, gf, b_pad)


def _rel_index():
    r = lax.broadcasted_iota(jnp.int32, (ATT_TQ, ATT_TK), 0)
    c = lax.broadcasted_iota(jnp.int32, (ATT_TQ, ATT_TK), 1)
    return r - c


def _qkv_specs(hb0, s):
    return [pl.BlockSpec((ATT_TQ, HEAD_DIM), lambda h, i: (i, 3 * (hb0 + h))),
            pl.BlockSpec((s, HEAD_DIM), lambda h, i: (0, 3 * (hb0 + h) + 1)),
            pl.BlockSpec((s, HEAD_DIM), lambda h, i: (0, 3 * (hb0 + h) + 2))]


def _sb_fwd(qkv, n_heads):
    s = qkv.shape[0]
    scale = HEAD_DIM ** -0.5
    tq, tk = ATT_TQ, ATT_TK

    def body(q_ref, k_ref, v_ref, o_ref, tot_ref):
        h, i = pl.program_id(0), pl.program_id(1)

        @pl.when((h == 0) & (i == 0))
        def _():
            tot_ref[...] = jnp.zeros_like(tot_ref)

        q = q_ref[...]
        rel = _rel_index()
        upper = _tri(tk, lambda r, c: r > c)

        def step(n, carry):
            c, acc = carry
            kj = i - n
            rows = pl.ds(pl.multiple_of(kj * tk, tk), tk)
            z = _dot(q, k_ref[rows, :], "nt") * scale
            mask = rel > (kj - i) * tk
            lsz = _log_sigmoid(z)
            lk = jnp.where(mask, lsz - z, 0.0)
            hi, lo = _split2(lk)
            between = c + _dot(hi, upper) + _dot(lo, upper)
            w = jnp.where(mask, jnp.exp(lsz + between), 0.0)
            acc = acc + _dot(w, v_ref[rows, :])
            return c + jnp.sum(lk, axis=1, keepdims=True), acc

        c, acc = lax.fori_loop(0, i + 1, step, (jnp.zeros((tq, 1), F32), jnp.zeros((tq, HEAD_DIM), F32)))
        o_ref[...] = acc.astype(BF16)
        _lane_put(tot_ref, pl.ds(pl.multiple_of(i * tq, tq), tq), h, c)

    return pl.pallas_call(
        body, name="sb_fwd", grid=(n_heads, s // tq),
        in_specs=_qkv_specs(0, s),
        out_specs=[pl.BlockSpec((tq, HEAD_DIM), lambda h, i: (i, h)), pl.BlockSpec((s, LANES), lambda h, i: (0, 0))],
        out_shape=[_sds((s, n_heads * HEAD_DIM), BF16), _sds((s, LANES), F32)],
        compiler_params=_params(("arbitrary", "arbitrary")),
    )(qkv, qkv, qkv)


def _sb_bwd(qkv, do, tot, n_heads, dep):
    s = qkv.shape[0]
    scale = HEAD_DIM ** -0.5
    tq, tk = ATT_TQ, ATT_TK
    nq = s // tq
    hd = HEAD_DIM

    def body(q_ref, k_ref, v_ref, do_ref, tot_ref, dep_ref, out_ref, dk_acc, dv_acc):
        del dep_ref
        h, i = pl.program_id(0), pl.program_id(1)

        @pl.when(i == 0)
        def _():
            dk_acc[...] = jnp.zeros_like(dk_acc)
            dv_acc[...] = jnp.zeros_like(dv_acc)

        q = q_ref[...]
        dout = do_ref[...]
        total = _lane_pick(tot_ref[...], h)
        rel = _rel_index()
        incl = _tri(tk, lambda r, c: r <= c)
        excl = _tri(tk, lambda r, c: r < c)

        def step(kj, carry):
            p_l, p_e, dq = carry
            rows = pl.ds(pl.multiple_of(kj * tk, tk), tk)
            k_t = k_ref[rows, :]
            z = _dot(q, k_t, "nt") * scale
            mask = rel > (kj - i) * tk
            lsz = _log_sigmoid(z)
            lk = jnp.where(mask, lsz - z, 0.0)
            hi, lo = _split2(lk)
            between = total - (p_l + _dot(hi, incl) + _dot(lo, incl))
            w = jnp.where(mask, jnp.exp(lsz + between), 0.0)
            e = _dot(dout, v_ref[rows, :], "nt") * w
            hi, lo = _split2(e)
            e_before = p_e + _dot(hi, excl) + _dot(lo, excl)
            sg = jnp.exp(lsz)
            dz = (jnp.where(mask, e * (1.0 - sg) - e_before * sg, 0.0) * scale).astype(BF16)
            dq = dq + _dot(dz, k_t)
            dk_acc[rows, :] += _dot(dz, q, "tn")
            dv_acc[rows, :] += _dot(w, dout, "tn")
            return p_l + jnp.sum(lk, axis=1, keepdims=True), p_e + jnp.sum(e, axis=1, keepdims=True), dq

        zero = jnp.zeros((tq, 1), F32)
        _, _, dq = lax.fori_loop(0, i + 1, step, (zero, zero, jnp.zeros((tq, hd), F32)))
        out_ref[pl.ds(pl.multiple_of(i * tq, tq), tq), pl.ds(0, hd)] = dq.astype(BF16)

        @pl.when(i == nq - 1)
        def _():
            out_ref[:, pl.ds(hd, hd)] = dk_acc[...].astype(BF16)
            out_ref[:, pl.ds(2 * hd, hd)] = dv_acc[...].astype(BF16)

    return pl.pallas_call(
        body, name="sb_bwd", grid=(n_heads, nq),
        in_specs=_qkv_specs(0, s) + [pl.BlockSpec((tq, hd), lambda h, i: (i, h)),
                                     pl.BlockSpec((tq, LANES), lambda h, i: (i, 0)), ANY],
        out_specs=pl.BlockSpec((s, 3 * hd), lambda h, i: (0, h)),
        out_shape=_sds(qkv.shape, BF16),
        scratch_shapes=[pltpu.VMEM((s, hd), F32), pltpu.VMEM((s, hd), F32)],
        compiler_params=_params(("arbitrary", "arbitrary")),
    )(qkv, qkv, qkv, do, tot, dep)


def _fox_fwd(qkv, cum_col, cum_row, n_heads, hb0, dep):
    s = qkv.shape[0]
    scale = HEAD_DIM ** -0.5
    tq, tk = ATT_TQ, ATT_TK

    def body(q_ref, k_ref, v_ref, cc_ref, cr_ref, dep_ref, o_ref, o32_ref, lse_ref):
        del dep_ref
        h, i = pl.program_id(0), pl.program_id(1)

        @pl.when((h == 0) & (i == 0))
        def _():
            lse_ref[...] = jnp.zeros_like(lse_ref)

        q = q_ref[...]
        cq = _lane_pick(cc_ref[...], h)
        rel = _rel_index()

        def step(kj, carry):
            m, l, acc = carry
            rows = pl.ds(pl.multiple_of(kj * tk, tk), tk)
            ck = cr_ref[kj, pl.ds(h, 1), :]
            sc = _dot(q, k_ref[rows, :], "nt") * scale + cq - ck
            sc = jnp.where(rel >= (kj - i) * tk, sc, NEG_BIG)
            m_new = jnp.maximum(m, jnp.max(sc, axis=1, keepdims=True))
            p = jnp.exp(sc - m_new)
            alpha = jnp.exp(m - m_new)
            hi, lo = _split2(p)
            v_t = v_ref[rows, :]
            return (m_new, alpha * l + jnp.sum(p, axis=1, keepdims=True), alpha * acc + _dot(hi, v_t) + _dot(lo, v_t))

        m, l, acc = lax.fori_loop(0, i + 1, step, (jnp.full((tq, 1), NEG_BIG, F32), jnp.zeros((tq, 1), F32),
                                                   jnp.zeros((tq, HEAD_DIM), F32)))
        o = acc / l
        o_ref[...] = o.astype(BF16)
        o32_ref[...] = o
        _lane_put(lse_ref, pl.ds(pl.multiple_of(i * tq, tq), tq), h, m + jnp.log(l))

    nb = cum_row.shape[0]
    return pl.pallas_call(
        body, name="fox_fwd", grid=(n_heads, s // tq),
        in_specs=_qkv_specs(hb0, s) + [pl.BlockSpec((tq, LANES), lambda h, i: (i, 0)),
                                       pl.BlockSpec((nb, 8, tk), lambda h, i: (0, 0, 0)), ANY],
        out_specs=[pl.BlockSpec((tq, HEAD_DIM), lambda h, i: (i, h)), pl.BlockSpec((tq, HEAD_DIM), lambda h, i: (i, h)),
                   pl.BlockSpec((s, LANES), lambda h, i: (0, 0))],
        out_shape=[_sds((s, n_heads * HEAD_DIM), BF16), _sds((s, n_heads * HEAD_DIM), F32), _sds((s, LANES), F32)],
        compiler_params=_params(("arbitrary", "arbitrary")),
    )(qkv, qkv, qkv, cum_col, cum_row, dep)


def _fox_bwd(dqkv, qkv, do, o, lse, cum_col, cum_row, n_heads, hb0):
    s = qkv.shape[0]
    scale = HEAD_DIM ** -0.5
    tq, tk = ATT_TQ, ATT_TK
    nq = s // tq
    hd = HEAD_DIM

    def body(dqkv_hbm, q_ref, k_ref, v_ref, do_ref, o_ref, lse_ref, cc_ref, cr_ref, out_ref, dc_ref,
             dk_acc, dv_acc, col_acc):
        del dqkv_hbm
        h, i = pl.program_id(0), pl.program_id(1)

        @pl.when((h == 0) & (i == 0))
        def _():
            dc_ref[...] = jnp.zeros_like(dc_ref)

        @pl.when(i == 0)
        def _():
            dk_acc[...] = jnp.zeros_like(dk_acc)
            dv_acc[...] = jnp.zeros_like(dv_acc)
            col_acc[...] = jnp.zeros_like(col_acc)

        q = q_ref[...]
        dout = do_ref[...]
        delta = jnp.sum(dout.astype(F32) * o_ref[...], axis=1, keepdims=True)
        lse_q = _lane_pick(lse_ref[...], h)
        cq = _lane_pick(cc_ref[...], h)
        rel = _rel_index()

        def step(kj, carry):
            dq, row_sum = carry
            rows = pl.ds(pl.multiple_of(kj * tk, tk), tk)
            k_t = k_ref[rows, :]
            ck = cr_ref[kj, pl.ds(h, 1), :]
            sc = _dot(q, k_t, "nt") * scale + cq - ck
            p = jnp.where(rel >= (kj - i) * tk, jnp.exp(sc - lse_q), 0.0)
            ds_f = p * (_dot(dout, v_ref[rows, :], "nt") - delta)
            col_acc[kj] += jnp.broadcast_to(jnp.sum(ds_f, axis=0, keepdims=True), (8, tk))
            ds = (ds_f * scale).astype(BF16)
            dk_acc[rows, :] += _dot(ds, q, "tn")
            dv_acc[rows, :] += _dot(p, dout, "tn")
            return dq + _dot(ds, k_t), row_sum + jnp.sum(ds_f, axis=1, keepdims=True)

        dq, row_sum = lax.fori_loop(0, i + 1, step, (jnp.zeros((tq, hd), F32), jnp.zeros((tq, 1), F32)))
        q_rows = pl.ds(pl.multiple_of(i * tq, tq), tq)
        out_ref[q_rows, pl.ds(0, hd)] = dq.astype(BF16)
        _lane_put(dc_ref, q_rows, h, row_sum)

        @pl.when(i == nq - 1)
        def _():
            out_ref[:, pl.ds(hd, hd)] = dk_acc[...].astype(BF16)
            out_ref[:, pl.ds(2 * hd, hd)] = dv_acc[...].astype(BF16)
            lane = lax.broadcasted_iota(jnp.int32, (tk, LANES), 1)
            for kj in range(nb):
                col = jnp.broadcast_to(col_acc[kj][0:1, :], (LANES, tk)).T
                old = dc_ref[pl.ds(kj * tk, tk), :]
                dc_ref[pl.ds(kj * tk, tk), :] = jnp.where(lane == h, old - col, old)

    nb = cum_row.shape[0]
    return pl.pallas_call(
        body, name="fox_bwd", grid=(n_heads, nq),
        in_specs=[ANY] + _qkv_specs(hb0, s) + [
            pl.BlockSpec((tq, hd), lambda h, i: (i, h)), pl.BlockSpec((tq, hd), lambda h, i: (i, h)),
            pl.BlockSpec((tq, LANES), lambda h, i: (i, 0)), pl.BlockSpec((tq, LANES), lambda h, i: (i, 0)),
            pl.BlockSpec((nb, 8, tk), lambda h, i: (0, 0, 0))],
        out_specs=[pl.BlockSpec((s, 3 * hd), lambda h, i: (0, hb0 + h)), pl.BlockSpec((s, LANES), lambda h, i: (0, 0))],
        out_shape=[_sds(dqkv.shape, BF16), _sds((s, LANES), F32)],
        scratch_shapes=[pltpu.VMEM((s, hd), F32), pltpu.VMEM((s, hd), F32), pltpu.VMEM((s // tk, 8, tk), F32)],
        input_output_aliases={0: 0},
        compiler_params=_params(("arbitrary", "arbitrary")),
    )(dqkv, qkv, qkv, qkv, do, o, lse, cum_col, cum_row)


def _branch_merge(o_sb, o_fx, w_sb, w_fx, gf, dep, tm=1024):
    s = o_sb.shape[0]
    cs = w_sb.shape[2]
    tm = _tile(s, tm)

    def body(osb_ref, ofx_ref, wsb_ref, wfx_ref, g_ref, dep_ref, merged_ref, asb_ref, afx_ref):
        del dep_ref
        a_sb = _dot(osb_ref[...], wsb_ref[...])
        a_fx = _dot(ofx_ref[...], wfx_ref[...])
        g = g_ref[...]
        merged_ref[...] = (_sigmoid(g[:, :cs]) * a_sb + _sigmoid(g[:, cs:]) * a_fx).astype(BF16)
        asb_ref[...] = a_sb.astype(BF16)
        afx_ref[...] = a_fx.astype(BF16)

    blk = pl.BlockSpec((tm, cs), lambda i, j: (i, j))
    out = _sds((s, N_DEV * cs), BF16)
    return pl.pallas_call(
        body, name="branch_merge", grid=(s // tm, N_DEV),
        in_specs=[pl.BlockSpec((tm, o_sb.shape[1]), lambda i, j: (i, 0)),
                  pl.BlockSpec((tm, o_fx.shape[1]), lambda i, j: (i, 0)),
                  pl.BlockSpec((None,) + w_sb.shape[1:], lambda i, j: (j, 0, 0)),
                  pl.BlockSpec((None,) + w_fx.shape[1:], lambda i, j: (j, 0, 0)),
                  pl.BlockSpec((tm, 2 * cs), lambda i, j: (i, j)), ANY],
        out_specs=[blk, blk, blk], out_shape=[out, out, out],
        compiler_params=_params(("parallel", "arbitrary")),
    )(o_sb, o_fx, w_sb, w_fx, gf, dep)


def _merge_bwd(dmix, w_out, gf, a_sb, a_fx, tm=1024, tk=512):
    s, d = dmix.shape
    cs = d // N_DEV
    tm, tk = _tile(s, tm), _tile(d, tk)

    def epilogue(acc, ex, outs):
        g, a_sb, a_fx = ex[0][...], ex[1][...].astype(F32), ex[2][...].astype(F32)
        s_sb, s_fx = _sigmoid(g[:, :cs]), _sigmoid(g[:, cs:])
        outs[0][...] = (acc * s_sb).astype(BF16)
        outs[1][...] = (acc * s_fx).astype(BF16)
        outs[2][...] = jnp.concatenate([acc * a_sb * s_sb * (1.0 - s_sb), acc * a_fx * s_fx * (1.0 - s_fx)],
                                       axis=1).astype(BF16)

    blk = pl.BlockSpec((tm, cs), lambda i, j, k: (i, j))
    wide = pl.BlockSpec((tm, 2 * cs), lambda i, j, k: (i, j))
    return _matmul(
        "merge_bwd", "nt",
        [(dmix, pl.BlockSpec((tm, tk), lambda i, j, k: (i, k)), w_out, pl.BlockSpec((cs, tk), lambda i, j, k: (j, k)))],
        (s // tm, N_DEV, d // tk), (tm, cs),
        [_sds((s, d), BF16), _sds((s, d), BF16), _sds(gf.shape, BF16)], [blk, blk, wide],
        extras=[(gf, wide), (a_sb, blk), (a_fx, blk)], epilogue=epilogue)


def _ffn_up(u2, w_gate, w_up, tm=1024):
    s, d = u2.shape
    fs = w_gate.shape[2]
    tm = _tile(s, tm)

    def body(u_ref, wg_ref, wu_ref, gate_ref, up_ref, act_ref):
        u = u_ref[...]
        gate = _dot(u, wg_ref[...])
        up = _dot(u, wu_ref[...])
        gate_ref[...] = gate
        up_ref[...] = up
        act_ref[...] = (gate * _sigmoid(gate) * up).astype(BF16)

    w_spec = pl.BlockSpec((None, d, fs), lambda i, j: (j, 0, 0))
    o_spec = pl.BlockSpec((None, tm, fs), lambda i, j: (j, i, 0))
    return pl.pallas_call(
        body, name="ffn_up", grid=(s // tm, N_DEV),
        in_specs=[pl.BlockSpec((tm, d), lambda i, j: (i, 0)), w_spec, w_spec],
        out_specs=[o_spec, o_spec, o_spec],
        out_shape=[_sds((N_DEV, s, fs), F32), _sds((N_DEV, s, fs), F32), _sds((N_DEV, s, fs), BF16)],
        compiler_params=_params(("parallel", "arbitrary")),
    )(u2, w_gate, w_up)


def _ffn_down_bwd(dff, w_down, gate, up, tm=1024):
    s, d = dff.shape
    fs = w_down.shape[1]
    tm = _tile(s, tm)

    def body(dff_ref, wd_ref, gate_ref, up_ref, dgate_ref, dup_ref):
        dact = _dot(dff_ref[...], wd_ref[...], "nt")
        gate = gate_ref[...]
        sg = _sigmoid(gate)
        dup_ref[...] = (dact * gate * sg).astype(BF16)
        dgate_ref[...] = (dact * up_ref[...] * sg * (1.0 + gate * (1.0 - sg))).astype(BF16)

    a_spec = pl.BlockSpec((None, tm, fs), lambda i, j: (j, i, 0))
    return pl.pallas_call(
        body, name="ffn_down_bwd", grid=(s // tm, N_DEV),
        in_specs=[pl.BlockSpec((tm, d), lambda i, j: (i, 0)), pl.BlockSpec((None, fs, d), lambda i, j: (j, 0, 0)),
                  a_spec, a_spec],
        out_specs=[a_spec, a_spec],
        out_shape=[_sds((N_DEV, s, fs), BF16), _sds((N_DEV, s, fs), BF16)],
        compiler_params=_params(("parallel", "arbitrary")),
    )(dff, w_down, gate, up)


def _mesh_place():
    x, y, c = lax.axis_index("x"), lax.axis_index("y"), lax.axis_index("c")
    peers = []
    for d in range(1, N_DEV):
        px = 1 - x if d & 4 else x
        py = 1 - y if d & 2 else y
        pc = 1 - c if d & 1 else c
        peers.append((d, (px, py, pc), 4 * px + 2 * py + pc))
    return 4 * x + 2 * y + c, peers


def _flat_me():
    return 4 * lax.axis_index("x") + 2 * lax.axis_index("y") + lax.axis_index("c")


def _in_hbm(a):
    return pltpu.with_memory_space_constraint(a, pltpu.HBM)


def _scatter_start(name, parts):
    n = len(parts)
    me = _flat_me()
    lands = [lax.dynamic_update_slice_in_dim(lax.empty(a.shape, a.dtype), lax.dynamic_slice_in_dim(a, me, 1, 0), me, 0)
             for a in parts]

    def body(*refs):
        ins, lnd = refs[:n], refs[n:2 * n]
        send, recv = refs[2 * n], refs[2 * n + 1]
        token = refs[-1]
        mine, peers = _mesh_place()
        for a in range(n):
            for d, dev, flat in peers:
                pltpu.make_async_remote_copy(src_ref=ins[a].at[flat], dst_ref=lnd[a].at[mine], send_sem=send.at[a * N_DEV + d],
                                             recv_sem=recv.at[a * N_DEV + d], device_id=dev, device_id_type=MESH).start()
        token[...] = jnp.zeros_like(token)

    res = pl.pallas_call(
        body, name=name,
        out_shape=[pltpu.SemaphoreType.DMA((n * N_DEV,)), pltpu.SemaphoreType.DMA((n * N_DEV,))]
        + [pltpu.HBM(a.shape, a.dtype) for a in parts] * 2 + [_sds((8, LANES), F32)],
        in_specs=[HBM] * (2 * n), out_specs=[SEM, SEM] + [HBM] * (2 * n) + [pl.BlockSpec(memory_space=pltpu.VMEM)],
        input_output_aliases={i: 2 + i for i in range(2 * n)},
        compiler_params=pltpu.CompilerParams(has_side_effects=EFFECT),
    )(*[_in_hbm(a) for a in parts], *[_in_hbm(a) for a in lands])
    return res[0], res[1], res[2:2 + n], res[2 + n:2 + 2 * n], res[-1]


def _scatter_wait(name, send, recv, parts, lands, after):
    n = len(parts)

    def body(*refs):
        ins, lnd = refs[:n], refs[n:2 * n]
        send_sem, recv_sem = refs[2 * n], refs[2 * n + 1]
        mine, peers = _mesh_place()
        for a in range(n):
            for d, dev, flat in peers:
                cp = pltpu.make_async_remote_copy(src_ref=ins[a].at[flat], dst_ref=lnd[a].at[flat],
                                                  send_sem=send_sem.at[a * N_DEV + d], recv_sem=recv_sem.at[a * N_DEV + d],
                                                  device_id=dev, device_id_type=MESH)
                cp.wait_send()
                cp.wait_recv()

    res = pl.pallas_call(
        body, name=name,
        out_shape=[pltpu.HBM(a.shape, a.dtype) for a in parts] * 2,
        in_specs=[HBM] * (2 * n) + [SEM, SEM, ANY], out_specs=[HBM] * (2 * n),
        input_output_aliases={i: i for i in range(2 * n)},
        compiler_params=pltpu.CompilerParams(has_side_effects=EFFECT),
    )(*parts, *lands, send, recv, after)
    return res[n:]


def _gather_targets():
    x, y, c = lax.axis_index("x"), lax.axis_index("y"), lax.axis_index("c")
    chips = [(x, y), (1 - x, y), (x, 1 - y), (1 - x, 1 - y)]
    same = [((cx, cy, c), 4 * cx + 2 * cy + c) for cx, cy in chips]
    other = [((cx, cy, 1 - c), 4 * cx + 2 * cy + 1 - c) for cx, cy in chips]
    return same[0][1], [other[0]] + same[1:], [flat for _, flat in other[1:]], other[0][0]


def _gather_start(shards):
    n = len(shards)
    me = _flat_me()
    lands = [lax.dynamic_update_slice_in_dim(lax.empty((N_DEV,) + a.shape, a.dtype), a[None], me, 0) for a in shards]

    def body(*refs):
        lnd, send, recv, token = refs[:n], refs[n], refs[n + 1], refs[-1]
        mine, targets, _, _ = _gather_targets()
        for a in range(n):
            for t, (dev, _) in enumerate(targets):
                pltpu.make_async_remote_copy(src_ref=lnd[a].at[mine], dst_ref=lnd[a].at[mine], send_sem=send.at[4 * a + t],
                                             recv_sem=recv.at[4 * a + t], device_id=dev, device_id_type=MESH).start()
        token[...] = jnp.zeros_like(token)

    res = pl.pallas_call(
        body, name="gather_start",
        out_shape=[pltpu.SemaphoreType.DMA((4 * n,)), pltpu.SemaphoreType.DMA((4 * n,))]
        + [pltpu.HBM(a.shape, a.dtype) for a in lands] + [_sds((8, LANES), F32)],
        in_specs=[HBM] * n, out_specs=[SEM, SEM] + [HBM] * n + [pl.BlockSpec(memory_space=pltpu.VMEM)],
        input_output_aliases={i: 2 + i for i in range(n)},
        compiler_params=pltpu.CompilerParams(has_side_effects=EFFECT),
    )(*[_in_hbm(a) for a in lands])
    return res[0], res[1], list(res[2:2 + n]), res[-1]


def _gather_forward(name, lands, first, send, recv, after):
    n = len(lands)

    def body(*refs):
        lnd, send_sem, recv_sem = refs[:n], refs[n], refs[n + 1]
        send2, recv2, token = refs[-3], refs[-2], refs[-1]
        mine, targets, _, sibling = _gather_targets()
        for a in range(n):
            for t, (dev, flat) in enumerate(targets):
                cp = pltpu.make_async_remote_copy(src_ref=lnd[a].at[mine], dst_ref=lnd[a].at[flat],
                                                  send_sem=send_sem.at[4 * (first + a) + t],
                                                  recv_sem=recv_sem.at[4 * (first + a) + t], device_id=dev, device_id_type=MESH)
                cp.wait_send()
                if t:
                    cp.wait_recv()
                    pltpu.make_async_remote_copy(src_ref=lnd[a].at[flat], dst_ref=lnd[a].at[flat], send_sem=send2.at[3 * a + t - 1],
                                                 recv_sem=recv2.at[3 * a + t - 1], device_id=sibling, device_id_type=MESH).start()
        token[...] = jnp.zeros_like(token)

    res = pl.pallas_call(
        body, name=name,
        out_shape=[pltpu.HBM(a.shape, a.dtype) for a in lands]
        + [pltpu.SemaphoreType.DMA((3 * n,)), pltpu.SemaphoreType.DMA((3 * n,)), _sds((8, LANES), F32)],
        in_specs=[HBM] * n + [SEM, SEM, ANY], out_specs=[HBM] * n + [SEM, SEM, pl.BlockSpec(memory_space=pltpu.VMEM)],
        input_output_aliases={i: i for i in range(n)},
        compiler_params=pltpu.CompilerParams(has_side_effects=EFFECT),
    )(*lands, send, recv, after)
    return list(res[:n]), res[n], res[n + 1], res[-1]


def _gather_wait(name, lands, first, recv, send2, recv2, after):
    n = len(lands)

    def body(*refs):
        lnd, recv_sem, send2_sem, recv2_sem = refs[:n], refs[n], refs[n + 1], refs[n + 2]
        mine, targets, passed, sibling = _gather_targets()
        for a in range(n):
            dev, flat = targets[0]
            pltpu.make_async_remote_copy(src_ref=lnd[a].at[mine], dst_ref=lnd[a].at[flat], send_sem=send2_sem.at[3 * a],
                                         recv_sem=recv_sem.at[4 * (first + a)], device_id=dev, device_id_type=MESH).wait_recv()
            for t in range(3):
                cp = pltpu.make_async_remote_copy(src_ref=lnd[a].at[targets[t + 1][1]], dst_ref=lnd[a].at[passed[t]],
                                                  send_sem=send2_sem.at[3 * a + t], recv_sem=recv2_sem.at[3 * a + t],
                                                  device_id=sibling, device_id_type=MESH)
                cp.wait_send()
                cp.wait_recv()

    res = pl.pallas_call(
        body, name=name, out_shape=[pltpu.HBM(a.shape, a.dtype) for a in lands],
        in_specs=[HBM] * n + [SEM, SEM, SEM, ANY], out_specs=[HBM] * n,
        input_output_aliases={i: i for i in range(n)},
        compiler_params=pltpu.CompilerParams(has_side_effects=EFFECT),
    )(*lands, recv, send2, recv2, after)
    return list(res)


def _adamw(g, w, m, v):
    m = ADAM_B1 * m + (1.0 - ADAM_B1) * g
    v = ADAM_B2 * v + (1.0 - ADAM_B2) * (g * g)
    m_hat = m / (1.0 - ADAM_B1 ** ADAM_STEP)
    v_hat = v / (1.0 - ADAM_B2 ** ADAM_STEP)
    delta = -ADAM_LR * (m_hat / (jnp.sqrt(v_hat) + ADAM_EPS) + ADAM_WD * w)
    return delta, m, v


def _update(name, parts, w, m, v, block_bytes=1 << 20):
    r, c = w.shape
    tr = max(8, min(r, (block_bytes // (4 * c)) // 8 * 8))
    while r % tr:
        tr -= 8

    def body(p_ref, w_ref, m_ref, v_ref, g_ref, d_ref, nm_ref, nv_ref):
        g = p_ref[0].astype(F32)
        for p in range(1, N_DEV):
            g = g + p_ref[p].astype(F32)
        g_ref[...] = g
        d_ref[...], nm_ref[...], nv_ref[...] = _adamw(g, w_ref[...], m_ref[...], v_ref[...])

    blk = pl.BlockSpec((tr, c), lambda i: (i, 0))
    return pl.pallas_call(
        body, name=name, grid=(r // tr,),
        in_specs=[pl.BlockSpec((N_DEV, tr, c), lambda i: (0, i, 0)), blk, blk, blk],
        out_specs=[blk] * 4, out_shape=[_sds((r, c), F32)] * 4,
        compiler_params=_params(("parallel",)),
    )(parts, w, m, v)


def _small_update(part, w, m, v):
    n = part.shape[1]

    def body(p_ref, w_ref, m_ref, v_ref, g_ref, d_ref, nm_ref, nv_ref, buf, send, recv):
        me, peers = _mesh_place()
        buf[me] = p_ref[...]
        sent = []
        for d, dev, flat in peers:
            cp = pltpu.make_async_remote_copy(src_ref=p_ref, dst_ref=buf.at[me], send_sem=send.at[d],
                                              recv_sem=recv.at[d], device_id=dev, device_id_type=MESH)
            cp.start()
            sent.append(cp)
        for d, dev, flat in peers:
            pltpu.make_async_remote_copy(src_ref=p_ref, dst_ref=buf.at[flat], send_sem=send.at[d],
                                         recv_sem=recv.at[d], device_id=dev, device_id_type=MESH).wait_recv()
        for cp in sent:
            cp.wait_send()
        g = buf[0]
        for p in range(1, N_DEV):
            g = g + buf[p]
        g_ref[...] = g
        d_ref[...], nm_ref[...], nv_ref[...] = _adamw(g, w_ref[...], m_ref[...], v_ref[...])

    vm = pl.BlockSpec(memory_space=pltpu.VMEM)
    return pl.pallas_call(
        body, name="small_update", in_specs=[vm] * 4, out_specs=[vm] * 4, out_shape=[_sds((1, n), F32)] * 4,
        scratch_shapes=[pltpu.VMEM((N_DEV, 1, n), F32), pltpu.SemaphoreType.DMA((N_DEV,)),
                        pltpu.SemaphoreType.DMA((N_DEV,))],
    )(part, w, m, v)


def _w_in_reorder(g_in, d_sb, d_fox, n_f, d):
    full = jnp.transpose(g_in, (1, 0, 2)).reshape(d, -1)
    cs = d // N_DEV

    def heads(sec, width):
        return sec.reshape(d, 3, width // HEAD_DIM, HEAD_DIM).transpose(0, 2, 1, 3).reshape(d, 3 * width)

    o1 = 3 * d_sb
    o2 = o1 + 3 * d_fox
    o3 = o2 + n_f
    gates = jnp.stack([full[:, o3:o3 + d].reshape(d, N_DEV, cs), full[:, o3 + d:].reshape(d, N_DEV, cs)], axis=2)
    f_sec = jnp.pad(full[:, o2:o3], ((0, 0), (0, F_PAD - n_f)))
    return jnp.concatenate([heads(full[:, :o1], d_sb), heads(full[:, o1:o2], d_fox), gates.reshape(d, 2 * d), f_sec], axis=1)


def _w_in_restore(dwq, dwgf, d_sb, d_fox, n_f, d):
    cs = d // N_DEV

    def heads(sec, width):
        return sec.reshape(d, width // HEAD_DIM, 3, HEAD_DIM).transpose(0, 2, 1, 3).reshape(d, 3 * width)

    gates = dwgf[:, :2 * d].reshape(d, N_DEV, 2, cs)
    full = jnp.concatenate([heads(dwq[:, :3 * d_sb], d_sb), heads(dwq[:, 3 * d_sb:], d_fox),
                            dwgf[:, 2 * d:2 * d + n_f], gates[:, :, 0].reshape(d, d), gates[:, :, 1].reshape(d, d)], axis=1)
    return jnp.transpose(full.reshape(d, N_DEV, -1), (1, 0, 2))


def kernel(x, norm_mix_pre, norm_mix_post, w_in, b_forget, w_branch_sb, w_branch_fox, w_out, norm_ffn_pre, norm_ffn_post, w_ffn_gate, w_ffn_up, w_ffn_down, loss_target, m_norm_mix_pre, m_norm_mix_post, m_w_in, m_b_forget, m_w_branch_sb, m_w_branch_fox, m_w_out, m_norm_ffn_pre, m_norm_ffn_post, m_w_ffn_gate, m_w_ffn_up, m_w_ffn_down, v_norm_mix_pre, v_norm_mix_post, v_w_in, v_b_forget, v_w_branch_sb, v_w_branch_fox, v_w_out, v_norm_ffn_pre, v_norm_ffn_post, v_w_ffn_gate, v_w_ffn_up, v_w_ffn_down):
    xs, target = x[0], loss_target[0]
    s, d = xs.shape
    d_sb, d_fox = w_branch_sb.shape[1], w_branch_fox.shape[1]
    h_sb, h_fox = d_sb // HEAD_DIM, d_fox // HEAD_DIM
    n_f = b_forget.shape[1]
    fs = w_ffn_gate.shape[2]
    cs = d // N_DEV
    n_qkv = 3 * (d_sb + d_fox)
    n_gf = 2 * d + F_PAD
    f_blk = 2 * d // LANES
    big = (w_in, w_branch_sb, w_branch_fox, w_out, w_ffn_gate, w_ffn_up, w_ffn_down)
    big_m = (m_w_in, m_w_branch_sb, m_w_branch_fox, m_w_out, m_w_ffn_gate, m_w_ffn_up, m_w_ffn_down)
    big_v = (v_w_in, v_w_branch_sb, v_w_branch_fox, v_w_out, v_w_ffn_gate, v_w_ffn_up, v_w_ffn_down)

    send1, recv1, lands, token = _gather_start([w[0].astype(BF16) for w in big])
    b_pad = jnp.pad(b_forget, ((0, 0), (0, LANES - n_f)))

    u = _pre_norm(xs, norm_mix_pre, dep=token)
    l_in, send2, recv2, token = _gather_forward("gather_in_forward", lands[0:1], 0, send1, recv1, u)
    (g_in,) = _gather_wait("gather_in_wait", l_in, 0, recv1, send2, recv2, token)
    w_cat = _w_in_reorder(g_in, d_sb, d_fox, n_f, d)
    qkv = _mm_plain("proj_qkv", "nn", u, w_cat, BF16, n=n_qkv)
    gf = _mm_plain("proj_gates", "nn", u, w_cat, F32, n_off=n_qkv, n=n_gf)
    cum_col, cum_row = _forget_fwd(gf, b_pad, f_blk)
    o_sb, tot = _sb_fwd(qkv, h_sb)
    l_mid, send2, recv2, token = _gather_forward("gather_mid_forward", lands[1:4], 1, send1, recv1, o_sb)
    o_fx, o_fx32, lse = _fox_fwd(qkv, cum_col, cum_row, h_fox, h_sb, token)
    g_sb, g_fx, g_out = _gather_wait("gather_mid_wait", l_mid, 1, recv1, send2, recv2, o_fx)
    w_out_full = g_out.reshape(d, d)
    l_ffn, send2, recv2, token = _gather_forward("gather_ffn_forward", lands[4:7], 4, send1, recv1, o_fx)
    merged, a_sb, a_fx = _branch_merge(o_sb, o_fx, g_sb, g_fx, gf, token)
    mix = _mm_plain("out_proj", "nn", merged, w_out_full, F32)
    g_gate, g_up, g_down = _gather_wait("gather_ffn_wait", l_ffn, 4, recv1, send2, recv2, mix)
    h1, u2 = _mid_norms(xs, mix, norm_mix_post, norm_ffn_pre)
    gate, up, act = _ffn_up(u2, g_gate, g_up)
    tm, tn = _tile(s, 1024), _tile(d, 512)
    ff = _matmul("ffn_down", "nn",
                 [(act, pl.BlockSpec((None, tm, fs), lambda i, j, k: (k, i, 0)),
                   g_down, pl.BlockSpec((None, fs, tn), lambda i, j, k: (k, 0, j)))],
                 (s // tm, d // tn, N_DEV), (tm, tn), _sds((s, d), F32), pl.BlockSpec((tm, tn), lambda i, j, k: (i, j)))
    loss_part, dy, dff, dg_ffn_post = _loss_head(h1, ff, target, norm_ffn_post)

    dgate, dup = _ffn_down_bwd(dff, g_down, gate, up)
    tk = _tile(s, 512)
    dw_down = _matmul("dw_down", "tn",
                      [(act, pl.BlockSpec((None, tk, fs), lambda j, n, k: (j, k, 0)),
                        dff, pl.BlockSpec((tk, tn), lambda j, n, k: (k, n)))],
                      (N_DEV, d // tn, s // tk), (fs, tn), _sds((N_DEV, fs, d), BF16),
                      pl.BlockSpec((None, fs, tn), lambda j, n, k: (j, 0, n)))

    def dw_up(name, dact):
        return _matmul(name, "tn",
                       [(u2, pl.BlockSpec((tk, tn), lambda j, i, k: (k, i)),
                         dact, pl.BlockSpec((None, tk, fs), lambda j, i, k: (j, k, 0)))],
                       (N_DEV, d // tn, s // tk), (tn, fs), _sds((N_DEV, d, fs), BF16),
                       pl.BlockSpec((None, tn, fs), lambda j, i, k: (j, i, 0)))

    dw_gate, dw_upw = dw_up("dw_gate", dgate), dw_up("dw_up", dup)
    rs_ffn = _scatter_start("scatter_ffn", [dw_gate, dw_upw, dw_down])
    a_spec = pl.BlockSpec((None, tm, fs), lambda i, j, k: (k, i, 0))
    b_spec = pl.BlockSpec((None, tn, fs), lambda i, j, k: (k, j, 0))
    du2 = _matmul("du2", "nt", [(dgate, a_spec, g_gate, b_spec), (dup, a_spec, g_up, b_spec)],
                  (s // tm, d // tn, N_DEV), (tm, tn), _sds((s, d), F32), pl.BlockSpec((tm, tn), lambda i, j, k: (i, j)),
                  dep=rs_ffn[4])
    dh1, dmix, dg_ffn_pre, dg_mix_post = _mid_norms_bwd(dy, du2, h1, mix, norm_ffn_pre, norm_mix_post)

    da_sb, da_fx, dgf = _merge_bwd(dmix, w_out_full, gf, a_sb, a_fx)
    dw_out = _mm_plain("dw_out", "tn", merged, dmix, BF16).reshape(N_DEV, cs, d)

    def branch_bwd(tag, da, w_b, o_b, width):
        tb = _tile(width, 512)
        do = _matmul("do_" + tag, "nt",
                     [(da, pl.BlockSpec((tm, cs), lambda i, j, k: (i, k)),
                       w_b, pl.BlockSpec((None, tb, cs), lambda i, j, k: (k, j, 0)))],
                     (s // tm, width // tb, N_DEV), (tm, tb), _sds((s, width), BF16),
                     pl.BlockSpec((tm, tb), lambda i, j, k: (i, j)))
        dw = _matmul("dw_" + tag, "tn",
                     [(o_b, pl.BlockSpec((tk, tb), lambda j, i, k: (k, i)),
                       da, pl.BlockSpec((tk, cs), lambda j, i, k: (k, j)))],
                     (N_DEV, width // tb, s // tk), (tb, cs), _sds((N_DEV, width, cs), BF16),
                     pl.BlockSpec((None, tb, cs), lambda j, i, k: (j, i, 0)))
        return do, dw

    do_sb, dw_sb = branch_bwd("sb", da_sb, g_sb, o_sb, d_sb)
    do_fx, dw_fx = branch_bwd("fox", da_fx, g_fx, o_fx, d_fox)

    rs_mid = _scatter_start("scatter_mid", [dw_sb, dw_fx, dw_out])

    dqkv = _sb_bwd(qkv, do_sb, tot, h_sb, rs_mid[4])
    dqkv, dcum = _fox_bwd(dqkv, qkv, do_fx, o_fx32, lse, cum_col, cum_row, h_fox, h_sb)
    dgf, db_part = _forget_bwd(dgf, dcum, gf, b_pad, f_blk)
    du = _mm_plain("du_qkv", "nt", dqkv, w_cat, F32)
    du = _mm_plain("du_gates", "nt", dgf, w_cat, F32, k_off=n_qkv, init=du)
    dw_in = _w_in_restore(_mm_plain("dw_qkv", "tn", u, dqkv, BF16), _mm_plain("dw_gates", "tn", u, dgf, BF16),
                          d_sb, d_fox, n_f, d)
    rs_in = _scatter_start("scatter_in", [dw_in])
    dx, dg_mix_pre = _pre_norm_bwd(dh1, du, xs, norm_mix_pre, dep=rs_in[4])

    upd = {}

    def update_group(tag, rs, names, after):
        parts = _scatter_wait("scatter_" + tag + "_wait", *rs[:4], after=after)
        for nm, p in zip(names, parts):
            w, m, v = weights[nm]
            upd[nm] = [o[None] for o in _update("update_" + nm, p, w[0], m[0], v[0])]

    weights = dict(zip(("w_in", "w_branch_sb", "w_branch_fox", "w_out", "w_ffn_gate", "w_ffn_up", "w_ffn_down"),
                       zip(big, big_m, big_v)))
    update_group("ffn", rs_ffn, ("w_ffn_gate", "w_ffn_up", "w_ffn_down"), dx)
    update_group("mid", rs_mid, ("w_branch_sb", "w_branch_fox", "w_out"), upd["w_ffn_down"][0])
    update_group("in", rs_in, ("w_in",), upd["w_out"][0])

    small = ((norm_mix_pre, m_norm_mix_pre, v_norm_mix_pre), (norm_mix_post, m_norm_mix_post, v_norm_mix_post),
             (norm_ffn_pre, m_norm_ffn_pre, v_norm_ffn_pre), (norm_ffn_post, m_norm_ffn_post, v_norm_ffn_post))
    pad_f = ((0, 0), (0, LANES - n_f))
    cat = lambda i: jnp.concatenate([t[i] for t in small] + [jnp.pad((b_forget, m_b_forget, v_b_forget)[i], pad_f)], axis=1)
    sm = _small_update(jnp.concatenate([dg_mix_pre, dg_mix_post, dg_ffn_pre, dg_ffn_post, db_part], axis=1),
                       cat(0), cat(1), cat(2))
    for i, nm in enumerate(("norm_mix_pre", "norm_mix_post", "norm_ffn_pre", "norm_ffn_post")):
        upd[nm] = [o[:, i * d:(i + 1) * d] for o in sm]
    upd["b_forget"] = [o[:, 4 * d:4 * d + n_f] for o in sm]

    loss = lax.psum(loss_part[0, 0], ("x", "y", "c"))
    order = ("norm_mix_pre", "norm_mix_post", "w_in", "b_forget", "w_branch_sb", "w_branch_fox", "w_out",
             "norm_ffn_pre", "norm_ffn_post", "w_ffn_gate", "w_ffn_up", "w_ffn_down")
    return (loss, dx[None]) + tuple(upd[nm][i] for i in range(4) for nm in order)
```

```python
import jax
import jax.numpy as jnp
from jax import lax
from jax.experimental import pallas as pl
from jax.experimental.pallas import tpu as pltpu

F32 = jnp.float32
BF16 = jnp.bfloat16
MESH = pl.DeviceIdType.MESH
ANY = pl.BlockSpec(memory_space=pl.ANY)
HBM = pl.BlockSpec(memory_space=pltpu.HBM)
SEM = pl.BlockSpec(memory_space=pltpu.SEMAPHORE)
EFFECT = pltpu.SideEffectType.DATAFLOW_SIDE_EFFECTING

N_DEV = 8
HEAD_DIM = 128
RMS_EPS = 1e-6
F_PAD = 512
LANES = 128
ATT_TQ = 256
ATT_TK = 256
NEG_BIG = -1e30
VMEM_LIMIT = 56 * 1024 * 1024

ADAM_LR = 0.001
ADAM_B1 = 0.9
ADAM_B2 = 0.999
ADAM_EPS = 1e-08
ADAM_WD = 0.01
ADAM_STEP = 10

_DIMS = {"nn": ((1,), (0,)), "nt": ((1,), (1,)), "tn": ((0,), (0,))}


def _params(sem):
    return pltpu.CompilerParams(dimension_semantics=sem, vmem_limit_bytes=VMEM_LIMIT)


def _dot(a, b, mode="nn"):
    return lax.dot_general(a.astype(BF16), b.astype(BF16), (_DIMS[mode], ((), ())), preferred_element_type=F32)


def _tile(n, pref):
    if n <= pref:
        return n
    t = (pref // LANES) * LANES
    while n % t:
        t -= LANES
    return t


def _split2(v):
    hi = v.astype(BF16)
    return hi, (v - hi.astype(F32)).astype(BF16)


def _split3(v):
    a = v.astype(BF16)
    r = v - a.astype(F32)
    b = r.astype(BF16)
    return a, b, (r - b.astype(F32)).astype(BF16)


def _tri(n, cmp):
    r = lax.broadcasted_iota(jnp.int32, (n, n), 0)
    c = lax.broadcasted_iota(jnp.int32, (n, n), 1)
    return jnp.where(cmp(r, c), 1.0, 0.0).astype(BF16)


def _lane_pick(v, h):
    lane = lax.broadcasted_iota(jnp.int32, v.shape, 1)
    return jnp.sum(jnp.where(lane == h, v, 0.0), axis=1, keepdims=True)


def _lane_put(ref, rows, h, col):
    old = ref[rows, :]
    lane = lax.broadcasted_iota(jnp.int32, old.shape, 1)
    ref[rows, :] = jnp.where(lane == h, col, old)


def _sigmoid(z):
    return 1.0 / (1.0 + jnp.exp(-z))


def _log_sigmoid(z):
    return jnp.minimum(z, 0.0) - jnp.log(1.0 + jnp.exp(-jnp.abs(z)))


def _sds(shape, dtype):
    return jax.ShapeDtypeStruct(shape, dtype)


def _matmul(name, mode, pairs, grid, acc_shape, out_shape, out_specs, extras=(), epilogue=None, init=None, dep=None):
    n_p, n_e = len(pairs), len(extras)
    nk = grid[-1]
    single = not isinstance(out_shape, (list, tuple))
    n_i = 0 if init is None else 1
    n_d = 0 if dep is None else 1

    one_step = nk == 1 and init is None

    def body(*refs):
        ab = refs[:2 * n_p]
        ex = refs[2 * n_p:2 * n_p + n_e]
        ini = refs[2 * n_p + n_e:2 * n_p + n_e + n_i]
        outs = refs[2 * n_p + n_e + n_i + n_d:len(refs) - (0 if one_step else 1)]

        def finish(total):
            if epilogue is None:
                outs[0][...] = total.astype(outs[0].dtype)
            else:
                epilogue(total, ex, outs)

        t = _dot(ab[0][...], ab[1][...], mode)
        for p in range(1, n_p):
            t = t + _dot(ab[2 * p][...], ab[2 * p + 1][...], mode)
        if one_step:
            finish(t)
            return
        acc = refs[-1]
        k = pl.program_id(len(grid) - 1)

        @pl.when(k == 0)
        def _():
            acc[...] = t if init is None else ini[0][...].astype(F32) + t

        @pl.when(k > 0)
        def _():
            acc[...] += t

        @pl.when(k == nk - 1)
        def _():
            finish(acc[...])

    in_specs = [s for (_, sa, _, sb) in pairs for s in (sa, sb)] + [s for (_, s) in extras]
    args = [v for (a, _, b, _) in pairs for v in (a, b)] + [e for (e, _) in extras]
    if init is not None:
        in_specs.append(init[1])
        args.append(init[0])
    if dep is not None:
        in_specs.append(ANY)
        args.append(dep)
    return pl.pallas_call(
        body, name=name, grid=grid, in_specs=in_specs,
        out_specs=out_specs if single else list(out_specs),
        out_shape=out_shape if single else list(out_shape),
        scratch_shapes=[] if one_step else [pltpu.VMEM(acc_shape, F32)],
        compiler_params=_params(("parallel",) * (len(grid) - 1) + ("arbitrary",)),
    )(*args)


def _mm_plain(name, mode, a, b, out_dtype, *, n_off=0, n=None, k_off=0, tm=1024, tn=1536, tk=2048, init=None, dep=None):
    if mode == "nn":
        (m, kk), nn_ = a.shape, b.shape[1]
    elif mode == "nt":
        (m, kk), nn_ = a.shape, b.shape[0]
    else:
        (kk, m), nn_ = a.shape, b.shape[1]
    n = nn_ if n is None else n
    tm, tn, tk = _tile(m, tm), _tile(n, tn), _tile(kk, tk)
    while n_off % tn or n % tn:
        tn -= LANES
    while k_off % tk or kk % tk:
        tk -= LANES
    off, koff = n_off // tn, k_off // tk
    a_spec = {"nn": pl.BlockSpec((tm, tk), lambda i, j, k: (i, k)),
              "nt": pl.BlockSpec((tm, tk), lambda i, j, k: (i, k)),
              "tn": pl.BlockSpec((tk, tm), lambda i, j, k: (k, i))}[mode]
    b_spec = {"nn": pl.BlockSpec((tk, tn), lambda i, j, k: (k, j + off)),
              "nt": pl.BlockSpec((tn, tk), lambda i, j, k: (j, k + koff)),
              "tn": pl.BlockSpec((tk, tn), lambda i, j, k: (k, j))}[mode]
    o_spec = pl.BlockSpec((tm, tn), lambda i, j, k: (i, j))
    if init is not None:
        init = (init, o_spec)
    return _matmul(name, mode, [(a, a_spec, b, b_spec)], (m // tm, n // tn, kk // tk), (tm, tn),
                   _sds((m, n), out_dtype), o_spec, init=init, dep=dep)


def _rows_call(name, body, ins, outs, s, tr=256, dep=None):
    def spec(v, per_row):
        if per_row == "transposed":
            return pl.BlockSpec((v.shape[0], tr), lambda i: (0, i))
        if per_row:
            return pl.BlockSpec((tr, v.shape[1]), lambda i: (i, 0))
        return pl.BlockSpec(v.shape, lambda i: (0, 0))
    n_in = len(ins)
    deps = [] if dep is None else [dep]

    def with_dep(*refs):
        body(*refs[:n_in], *refs[n_in + len(deps):])

    return pl.pallas_call(
        with_dep, name=name, grid=(s // tr,),
        in_specs=[spec(v, p) for v, p in ins] + [ANY] * len(deps), out_specs=[spec(v, p) for v, p in outs],
        out_shape=[_sds(v.shape, v.dtype) for v, _ in outs],
        compiler_params=_params(("arbitrary",)),
    )(*[v for v, _ in ins], *deps)


def _rsq(v):
    return lax.rsqrt(jnp.mean(v * v, axis=-1, keepdims=True) + RMS_EPS)


def _norm_bwd(dy, v, r, g):
    vh = v * r
    t = dy * g
    dv = r * (t - vh * jnp.mean(t * vh, axis=-1, keepdims=True))
    return dv, jnp.sum(dy * vh, axis=0, keepdims=True)


def _accum(ref, val):
    @pl.when(pl.program_id(0) == 0)
    def _():
        ref[...] = jnp.zeros_like(ref)
    ref[...] += val


def _pre_norm(x, g, dep=None):
    def body(x_ref, g_ref, u_ref, ut_ref):
        v = x_ref[...]
        u = (v * _rsq(v) * g_ref[...]).astype(BF16)
        u_ref[...] = u
        ut_ref[...] = u.T
    s, d = x.shape
    return _rows_call("pre_norm", body, [(x, True), (g, False)],
                      [(_sds((s, d), BF16), True), (_sds((d, s), BF16), "transposed")], s, dep=dep)


def _mid_norms(x, mix, g_post, g_pre):
    def body(x_ref, mix_ref, gp_ref, gn_ref, h_ref, u_ref, ut_ref):
        mv = mix_ref[...]
        h = x_ref[...] + mv * _rsq(mv) * gp_ref[...]
        h_ref[...] = h
        u = (h * _rsq(h) * gn_ref[...]).astype(BF16)
        u_ref[...] = u
        ut_ref[...] = u.T
    s, d = x.shape
    return _rows_call("mid_norms", body, [(x, True), (mix, True), (g_post, False), (g_pre, False)],
                      [(_sds((s, d), F32), True), (_sds((s, d), BF16), True), (_sds((d, s), BF16), "transposed")], s)


def _loss_head(h1, ff, target, g):
    s, d = h1.shape

    def body(h_ref, ff_ref, t_ref, g_ref, loss_ref, dy_ref, dff_ref, dg_ref):
        fv = ff_ref[...]
        r = _rsq(fv)
        err = h_ref[...] + fv * r * g_ref[...] - t_ref[...]
        part = 0.5 * jnp.sum(jnp.mean(err * err, axis=-1, keepdims=True), axis=0, keepdims=True)
        _accum(loss_ref, jnp.broadcast_to(part, loss_ref.shape))
        dy = err * (1.0 / d)
        dy_ref[...] = dy
        dff, dg = _norm_bwd(dy, fv, r, g_ref[...])
        dff_ref[...] = dff.astype(BF16)
        _accum(dg_ref, dg)

    return _rows_call("loss_head", body, [(h1, True), (ff, True), (target, True), (g, False)],
                      [(_sds((1, LANES), F32), False), (_sds((s, d), F32), True),
                       (_sds((s, d), BF16), True), (_sds((1, d), F32), False)], s)


def _mid_norms_bwd(dy, du2, h1, mix, g_pre, g_post):
    s, d = dy.shape

    def body(dy_ref, du_ref, h_ref, mix_ref, gn_ref, gp_ref, dh_ref, dmix_ref, dgn_ref, dgp_ref):
        h = h_ref[...]
        dh, dgn = _norm_bwd(du_ref[...], h, _rsq(h), gn_ref[...])
        dh = dh + dy_ref[...]
        dh_ref[...] = dh
        _accum(dgn_ref, dgn)
        mv = mix_ref[...]
        dmix, dgp = _norm_bwd(dh, mv, _rsq(mv), gp_ref[...])
        dmix_ref[...] = dmix.astype(BF16)
        _accum(dgp_ref, dgp)

    return _rows_call("mid_norms_bwd", body,
                      [(dy, True), (du2, True), (h1, True), (mix, True), (g_pre, False), (g_post, False)],
                      [(_sds((s, d), F32), True), (_sds((s, d), BF16), True),
                       (_sds((1, d), F32), False), (_sds((1, d), F32), False)], s)


def _pre_norm_bwd(dh1, du, x, g, dep=None):
    s, d = x.shape

    def body(dh_ref, du_ref, x_ref, g_ref, dx_ref, dg_ref):
        v = x_ref[...]
        dv, dg = _norm_bwd(du_ref[...], v, _rsq(v), g_ref[...])
        dx_ref[...] = dh_ref[...] + dv
        _accum(dg_ref, dg)

    return _rows_call("pre_norm_bwd", body, [(dh1, True), (du, True), (x, True), (g, False)],
                      [(_sds((s, d), F32), True), (_sds((1, d), F32), False)], s, dep=dep)


def _forget_fwd(gf, b_pad, f_blk):
    s = gf.shape[0]
    tb = ATT_TK
    nb = s // tb

    def body(f_ref, b_ref, col_ref, row_ref):
        incl = _tri(tb, lambda r, c: c <= r)
        carry = jnp.zeros((1, LANES), F32)
        for i in range(nb):
            lf = _log_sigmoid(f_ref[pl.ds(i * tb, tb), :] + b_ref[...])
            parts = _split3(lf)
            cum = carry + _dot(incl, parts[0]) + _dot(incl, parts[1]) + _dot(incl, parts[2])
            col_ref[pl.ds(i * tb, tb), :] = cum
            row_ref[i] = cum.T
            carry = carry + jnp.sum(lf, axis=0, keepdims=True)

    return pl.pallas_call(
        body, name="forget_fwd", grid=(1,),
        in_specs=[pl.BlockSpec((s, LANES), lambda i: (0, f_blk)), pl.BlockSpec((1, LANES), lambda i: (0, 0))],
        out_specs=[pl.BlockSpec((s, LANES), lambda i: (0, 0)), pl.BlockSpec((nb, LANES, tb), lambda i: (0, 0, 0))],
        out_shape=[_sds((s, LANES), F32), _sds((nb, LANES, tb), F32)],
        compiler_params=_params(("arbitrary",)),
    )(gf, b_pad)


def _forget_bwd(dgf, dcum, gf, b_pad, f_blk):
    s = gf.shape[0]
    tb = ATT_TK
    nb = s // tb
    sec = dgf.shape[1] // F_PAD - 1

    def body(dgf_hbm, dc_ref, f_ref, b_ref, out_ref, db_ref):
        del dgf_hbm
        incl = _tri(tb, lambda r, c: c >= r)
        carry = jnp.zeros((1, LANES), F32)
        db = jnp.zeros((1, LANES), F32)
        out_ref[...] = jnp.zeros_like(out_ref)
        for i in reversed(range(nb)):
            dc = dc_ref[pl.ds(i * tb, tb), :]
            parts = _split3(dc)
            dlf = carry + _dot(incl, parts[0]) + _dot(incl, parts[1]) + _dot(incl, parts[2])
            z = f_ref[pl.ds(i * tb, tb), :] + b_ref[...]
            df = dlf * _sigmoid(-z)
            out_ref[pl.ds(i * tb, tb), pl.ds(0, LANES)] = df.astype(BF16)
            db = db + jnp.sum(df, axis=0, keepdims=True)
            carry = carry + jnp.sum(dc, axis=0, keepdims=True)
        db_ref[...] = db

    return pl.pallas_call(
        body, name="forget_bwd", grid=(1,),
        in_specs=[ANY, pl.BlockSpec((s, LANES), lambda i: (0, 0)),
                  pl.BlockSpec((s, LANES), lambda i: (0, f_blk)), pl.BlockSpec((1, LANES), lambda i: (0, 0))],
        out_specs=[pl.BlockSpec((s, F_PAD), lambda i: (0, sec)), pl.BlockSpec((1, LANES), lambda i: (0, 0))],
        out_shape=[_sds(dgf.shape, BF16), _sds((1, LANES), F32)],
        input_output_aliases={0: 0},
        compiler_params=_params(("arbitrary",)),
    )(dgf, dcum, gf, b_pad)


def _rel_index():
    r = lax.broadcasted_iota(jnp.int32, (ATT_TQ, ATT_TK), 0)
    c = lax.broadcasted_iota(jnp.int32, (ATT_TQ, ATT_TK), 1)
    return r - c


def _qkv_specs(hb0, s):
    return [pl.BlockSpec((ATT_TQ, HEAD_DIM), lambda h, i: (i, 3 * (hb0 + h))),
            pl.BlockSpec((s, HEAD_DIM), lambda h, i: (0, 3 * (hb0 + h) + 1)),
            pl.BlockSpec((s, HEAD_DIM), lambda h, i: (0, 3 * (hb0 + h) + 2))]


def _sb_fwd(qkv, n_heads):
    s = qkv.shape[0]
    scale = HEAD_DIM ** -0.5
    tq, tk = ATT_TQ, ATT_TK

    def body(q_ref, k_ref, v_ref, o_ref, ot_ref, tot_ref):
        h, i = pl.program_id(0), pl.program_id(1)

        @pl.when((h == 0) & (i == 0))
        def _():
            tot_ref[...] = jnp.zeros_like(tot_ref)

        q = q_ref[...]
        rel = _rel_index()
        upper = _tri(tk, lambda r, c: r > c)

        def step(n, carry):
            c, acc = carry
            kj = i - n
            rows = pl.ds(pl.multiple_of(kj * tk, tk), tk)
            z = _dot(q, k_ref[rows, :], "nt") * scale
            mask = rel > (kj - i) * tk
            lsz = _log_sigmoid(z)
            lk = jnp.where(mask, lsz - z, 0.0)
            hi, lo = _split2(lk)
            between = c + _dot(hi, upper) + _dot(lo, upper)
            w = jnp.where(mask, jnp.exp(lsz + between), 0.0)
            acc = acc + _dot(w, v_ref[rows, :])
            return c + jnp.sum(lk, axis=1, keepdims=True), acc

        c, acc = lax.fori_loop(0, i + 1, step, (jnp.zeros((tq, 1), F32), jnp.zeros((tq, HEAD_DIM), F32)))
        o = acc.astype(BF16)
        o_ref[...] = o
        ot_ref[...] = o.T
        _lane_put(tot_ref, pl.ds(pl.multiple_of(i * tq, tq), tq), h, c)

    return pl.pallas_call(
        body, name="sb_fwd", grid=(n_heads, s // tq),
        in_specs=_qkv_specs(0, s),
        out_specs=[pl.BlockSpec((tq, HEAD_DIM), lambda h, i: (i, h)), pl.BlockSpec((HEAD_DIM, tq), lambda h, i: (h, i)),
                   pl.BlockSpec((s, LANES), lambda h, i: (0, 0))],
        out_shape=[_sds((s, n_heads * HEAD_DIM), BF16), _sds((n_heads * HEAD_DIM, s), BF16), _sds((s, LANES), F32)],
        compiler_params=_params(("arbitrary", "arbitrary")),
    )(qkv, qkv, qkv)


def _sb_bwd(qkv, do, tot, n_heads, dep):
    s = qkv.shape[0]
    scale = HEAD_DIM ** -0.5
    tq, tk = ATT_TQ, ATT_TK
    nq = s // tq
    hd = HEAD_DIM

    def body(q_ref, k_ref, v_ref, do_ref, tot_ref, dep_ref, out_ref, dk_acc, dv_acc):
        del dep_ref
        h, i = pl.program_id(0), pl.program_id(1)

        @pl.when(i == 0)
        def _():
            dk_acc[...] = jnp.zeros_like(dk_acc)
            dv_acc[...] = jnp.zeros_like(dv_acc)

        q = q_ref[...]
        dout = do_ref[...]
        total = _lane_pick(tot_ref[...], h)
        rel = _rel_index()
        incl = _tri(tk, lambda r, c: r <= c)
        excl = _tri(tk, lambda r, c: r < c)

        def step(kj, carry):
            p_l, p_e, dq = carry
            rows = pl.ds(pl.multiple_of(kj * tk, tk), tk)
            k_t = k_ref[rows, :]
            z = _dot(q, k_t, "nt") * scale
            mask = rel > (kj - i) * tk
            lsz = _log_sigmoid(z)
            lk = jnp.where(mask, lsz - z, 0.0)
            hi, lo = _split2(lk)
            between = total - (p_l + _dot(hi, incl) + _dot(lo, incl))
            w = jnp.where(mask, jnp.exp(lsz + between), 0.0)
            e = _dot(dout, v_ref[rows, :], "nt") * w
            hi, lo = _split2(e)
            e_before = p_e + _dot(hi, excl) + _dot(lo, excl)
            sg = jnp.exp(lsz)
            dz = (jnp.where(mask, e * (1.0 - sg) - e_before * sg, 0.0) * scale).astype(BF16)
            dq = dq + _dot(dz, k_t)
            dk_acc[rows, :] += _dot(dz, q, "tn")
            dv_acc[rows, :] += _dot(w, dout, "tn")
            return p_l + jnp.sum(lk, axis=1, keepdims=True), p_e + jnp.sum(e, axis=1, keepdims=True), dq

        zero = jnp.zeros((tq, 1), F32)
        _, _, dq = lax.fori_loop(0, i + 1, step, (zero, zero, jnp.zeros((tq, hd), F32)))
        out_ref[pl.ds(pl.multiple_of(i * tq, tq), tq), pl.ds(0, hd)] = dq.astype(BF16)

        @pl.when(i == nq - 1)
        def _():
            out_ref[:, pl.ds(hd, hd)] = dk_acc[...].astype(BF16)
            out_ref[:, pl.ds(2 * hd, hd)] = dv_acc[...].astype(BF16)

    return pl.pallas_call(
        body, name="sb_bwd", grid=(n_heads, nq),
        in_specs=_qkv_specs(0, s) + [pl.BlockSpec((tq, hd), lambda h, i: (i, h)),
                                     pl.BlockSpec((tq, LANES), lambda h, i: (i, 0)), ANY],
        out_specs=pl.BlockSpec((s, 3 * hd), lambda h, i: (0, h)),
        out_shape=_sds(qkv.shape, BF16),
        scratch_shapes=[pltpu.VMEM((s, hd), F32), pltpu.VMEM((s, hd), F32)],
        compiler_params=_params(("arbitrary", "arbitrary")),
    )(qkv, qkv, qkv, do, tot, dep)


def _fox_fwd(qkv, cum_col, cum_row, n_heads, hb0, dep):
    s = qkv.shape[0]
    scale = HEAD_DIM ** -0.5
    tq, tk = ATT_TQ, ATT_TK

    def body(q_ref, k_ref, v_ref, cc_ref, cr_ref, dep_ref, o_ref, ot_ref, o32_ref, lse_ref):
        del dep_ref
        h, i = pl.program_id(0), pl.program_id(1)

        @pl.when((h == 0) & (i == 0))
        def _():
            lse_ref[...] = jnp.zeros_like(lse_ref)

        q = q_ref[...]
        cq = _lane_pick(cc_ref[...], h)
        rel = _rel_index()

        def step(kj, carry):
            m, l, acc = carry
            rows = pl.ds(pl.multiple_of(kj * tk, tk), tk)
            ck = cr_ref[kj, pl.ds(h, 1), :]
            sc = _dot(q, k_ref[rows, :], "nt") * scale + cq - ck
            sc = jnp.where(rel >= (kj - i) * tk, sc, NEG_BIG)
            m_new = jnp.maximum(m, jnp.max(sc, axis=1, keepdims=True))
            p = jnp.exp(sc - m_new)
            alpha = jnp.exp(m - m_new)
            hi, lo = _split2(p)
            v_t = v_ref[rows, :]
            return (m_new, alpha * l + jnp.sum(p, axis=1, keepdims=True), alpha * acc + _dot(hi, v_t) + _dot(lo, v_t))

        m, l, acc = lax.fori_loop(0, i + 1, step, (jnp.full((tq, 1), NEG_BIG, F32), jnp.zeros((tq, 1), F32),
                                                   jnp.zeros((tq, HEAD_DIM), F32)))
        o = acc / l
        o_ref[...] = o.astype(BF16)
        ot_ref[...] = o.astype(BF16).T
        o32_ref[...] = o
        _lane_put(lse_ref, pl.ds(pl.multiple_of(i * tq, tq), tq), h, m + jnp.log(l))

    nb = cum_row.shape[0]
    return pl.pallas_call(
        body, name="fox_fwd", grid=(n_heads, s // tq),
        in_specs=_qkv_specs(hb0, s) + [pl.BlockSpec((tq, LANES), lambda h, i: (i, 0)),
                                       pl.BlockSpec((nb, 8, tk), lambda h, i: (0, 0, 0)), ANY],
        out_specs=[pl.BlockSpec((tq, HEAD_DIM), lambda h, i: (i, h)), pl.BlockSpec((HEAD_DIM, tq), lambda h, i: (h, i)),
                   pl.BlockSpec((tq, HEAD_DIM), lambda h, i: (i, h)), pl.BlockSpec((s, LANES), lambda h, i: (0, 0))],
        out_shape=[_sds((s, n_heads * HEAD_DIM), BF16), _sds((n_heads * HEAD_DIM, s), BF16),
                   _sds((s, n_heads * HEAD_DIM), F32), _sds((s, LANES), F32)],
        compiler_params=_params(("arbitrary", "arbitrary")),
    )(qkv, qkv, qkv, cum_col, cum_row, dep)


def _fox_bwd(dqkv, qkv, do, o, lse, cum_col, cum_row, n_heads, hb0):
    s = qkv.shape[0]
    scale = HEAD_DIM ** -0.5
    tq, tk = ATT_TQ, ATT_TK
    nq = s // tq
    hd = HEAD_DIM

    def body(dqkv_hbm, q_ref, k_ref, v_ref, do_ref, o_ref, lse_ref, cc_ref, cr_ref, out_ref, dc_ref,
             dk_acc, dv_acc, col_acc):
        del dqkv_hbm
        h, i = pl.program_id(0), pl.program_id(1)

        @pl.when((h == 0) & (i == 0))
        def _():
            dc_ref[...] = jnp.zeros_like(dc_ref)

        @pl.when(i == 0)
        def _():
            dk_acc[...] = jnp.zeros_like(dk_acc)
            dv_acc[...] = jnp.zeros_like(dv_acc)
            col_acc[...] = jnp.zeros_like(col_acc)

        q = q_ref[...]
        dout = do_ref[...]
        delta = jnp.sum(dout.astype(F32) * o_ref[...], axis=1, keepdims=True)
        lse_q = _lane_pick(lse_ref[...], h)
        cq = _lane_pick(cc_ref[...], h)
        rel = _rel_index()

        def step(kj, carry):
            dq, row_sum = carry
            rows = pl.ds(pl.multiple_of(kj * tk, tk), tk)
            k_t = k_ref[rows, :]
            ck = cr_ref[kj, pl.ds(h, 1), :]
            sc = _dot(q, k_t, "nt") * scale + cq - ck
            p = jnp.where(rel >= (kj - i) * tk, jnp.exp(sc - lse_q), 0.0)
            ds_f = p * (_dot(dout, v_ref[rows, :], "nt") - delta)
            col_acc[kj] += jnp.broadcast_to(jnp.sum(ds_f, axis=0, keepdims=True), (8, tk))
            ds = (ds_f * scale).astype(BF16)
            dk_acc[rows, :] += _dot(ds, q, "tn")
            dv_acc[rows, :] += _dot(p, dout, "tn")
            return dq + _dot(ds, k_t), row_sum + jnp.sum(ds_f, axis=1, keepdims=True)

        dq, row_sum = lax.fori_loop(0, i + 1, step, (jnp.zeros((tq, hd), F32), jnp.zeros((tq, 1), F32)))
        q_rows = pl.ds(pl.multiple_of(i * tq, tq), tq)
        out_ref[q_rows, pl.ds(0, hd)] = dq.astype(BF16)
        _lane_put(dc_ref, q_rows, h, row_sum)

        @pl.when(i == nq - 1)
        def _():
            out_ref[:, pl.ds(hd, hd)] = dk_acc[...].astype(BF16)
            out_ref[:, pl.ds(2 * hd, hd)] = dv_acc[...].astype(BF16)
            lane = lax.broadcasted_iota(jnp.int32, (tk, LANES), 1)
            for kj in range(nb):
                col = jnp.broadcast_to(col_acc[kj][0:1, :], (LANES, tk)).T
                old = dc_ref[pl.ds(kj * tk, tk), :]
                dc_ref[pl.ds(kj * tk, tk), :] = jnp.where(lane == h, old - col, old)

    nb = cum_row.shape[0]
    return pl.pallas_call(
        body, name="fox_bwd", grid=(n_heads, nq),
        in_specs=[ANY] + _qkv_specs(hb0, s) + [
            pl.BlockSpec((tq, hd), lambda h, i: (i, h)), pl.BlockSpec((tq, hd), lambda h, i: (i, h)),
            pl.BlockSpec((tq, LANES), lambda h, i: (i, 0)), pl.BlockSpec((tq, LANES), lambda h, i: (i, 0)),
            pl.BlockSpec((nb, 8, tk), lambda h, i: (0, 0, 0))],
        out_specs=[pl.BlockSpec((s, 3 * hd), lambda h, i: (0, hb0 + h)), pl.BlockSpec((s, LANES), lambda h, i: (0, 0))],
        out_shape=[_sds(dqkv.shape, BF16), _sds((s, LANES), F32)],
        scratch_shapes=[pltpu.VMEM((s, hd), F32), pltpu.VMEM((s, hd), F32), pltpu.VMEM((s // tk, 8, tk), F32)],
        input_output_aliases={0: 0},
        compiler_params=_params(("arbitrary", "arbitrary")),
    )(dqkv, qkv, qkv, qkv, do, o, lse, cum_col, cum_row)


def _branch_merge(o_sb, o_fx, w_sb, w_fx, gf, dep, tm=1024):
    s = o_sb.shape[0]
    cs = w_sb.shape[2]
    tm = _tile(s, tm)

    def body(osb_ref, ofx_ref, wsb_ref, wfx_ref, g_ref, dep_ref, merged_ref, mt_ref, asb_ref, afx_ref):
        del dep_ref
        a_sb = _dot(osb_ref[...], wsb_ref[...])
        a_fx = _dot(ofx_ref[...], wfx_ref[...])
        g = g_ref[...]
        merged = (_sigmoid(g[:, :cs]) * a_sb + _sigmoid(g[:, cs:]) * a_fx).astype(BF16)
        merged_ref[...] = merged
        mt_ref[...] = merged.T
        asb_ref[...] = a_sb.astype(BF16)
        afx_ref[...] = a_fx.astype(BF16)

    blk = pl.BlockSpec((tm, cs), lambda i, j: (i, j))
    out = _sds((s, N_DEV * cs), BF16)
    return pl.pallas_call(
        body, name="branch_merge", grid=(s // tm, N_DEV),
        in_specs=[pl.BlockSpec((tm, o_sb.shape[1]), lambda i, j: (i, 0)),
                  pl.BlockSpec((tm, o_fx.shape[1]), lambda i, j: (i, 0)),
                  pl.BlockSpec((None,) + w_sb.shape[1:], lambda i, j: (j, 0, 0)),
                  pl.BlockSpec((None,) + w_fx.shape[1:], lambda i, j: (j, 0, 0)),
                  pl.BlockSpec((tm, 2 * cs), lambda i, j: (i, j)), ANY],
        out_specs=[blk, pl.BlockSpec((cs, tm), lambda i, j: (j, i)), blk, blk],
        out_shape=[out, _sds((N_DEV * cs, s), BF16), out, out],
        compiler_params=_params(("parallel", "arbitrary")),
    )(o_sb, o_fx, w_sb, w_fx, gf, dep)


def _merge_bwd(dmix, w_out, gf, a_sb, a_fx, tm=1024, tk=2048):
    s, d = dmix.shape
    cs = d // N_DEV
    tm, tk = _tile(s, tm), _tile(d, tk)

    def epilogue(acc, ex, outs):
        g, a_sb, a_fx = ex[0][...], ex[1][...].astype(F32), ex[2][...].astype(F32)
        s_sb, s_fx = _sigmoid(g[:, :cs]), _sigmoid(g[:, cs:])
        outs[0][...] = (acc * s_sb).astype(BF16)
        outs[1][...] = (acc * s_fx).astype(BF16)
        outs[2][...] = jnp.concatenate([acc * a_sb * s_sb * (1.0 - s_sb), acc * a_fx * s_fx * (1.0 - s_fx)],
                                       axis=1).astype(BF16)

    blk = pl.BlockSpec((tm, cs), lambda i, j, k: (i, j))
    wide = pl.BlockSpec((tm, 2 * cs), lambda i, j, k: (i, j))
    return _matmul(
        "merge_bwd", "nt",
        [(dmix, pl.BlockSpec((tm, tk), lambda i, j, k: (i, k)), w_out, pl.BlockSpec((cs, tk), lambda i, j, k: (j, k)))],
        (s // tm, N_DEV, d // tk), (tm, cs),
        [_sds((s, d), BF16), _sds((s, d), BF16), _sds(gf.shape, BF16)], [blk, blk, wide],
        extras=[(gf, wide), (a_sb, blk), (a_fx, blk)], epilogue=epilogue)


def _ffn_up(u2, w_gate, w_up, tm=1024):
    s, d = u2.shape
    fs = w_gate.shape[2]
    tm = _tile(s, tm)

    def body(u_ref, wg_ref, wu_ref, gate_ref, up_ref, act_ref, actt_ref):
        u = u_ref[...]
        gate = _dot(u, wg_ref[...])
        up = _dot(u, wu_ref[...])
        gate_ref[...] = gate
        up_ref[...] = up
        act = (gate * _sigmoid(gate) * up).astype(BF16)
        act_ref[...] = act
        actt_ref[...] = act.T

    w_spec = pl.BlockSpec((None, d, fs), lambda i, j: (j, 0, 0))
    o_spec = pl.BlockSpec((None, tm, fs), lambda i, j: (j, i, 0))
    return pl.pallas_call(
        body, name="ffn_up", grid=(s // tm, N_DEV),
        in_specs=[pl.BlockSpec((tm, d), lambda i, j: (i, 0)), w_spec, w_spec],
        out_specs=[o_spec, o_spec, o_spec, pl.BlockSpec((None, fs, tm), lambda i, j: (j, 0, i))],
        out_shape=[_sds((N_DEV, s, fs), F32), _sds((N_DEV, s, fs), F32), _sds((N_DEV, s, fs), BF16),
                   _sds((N_DEV, fs, s), BF16)],
        compiler_params=_params(("parallel", "arbitrary")),
    )(u2, w_gate, w_up)


def _ffn_down_bwd(dff, w_down, gate, up, tm=1024):
    s, d = dff.shape
    fs = w_down.shape[1]
    tm = _tile(s, tm)

    def body(dff_ref, wd_ref, gate_ref, up_ref, dgate_ref, dup_ref):
        dact = _dot(dff_ref[...], wd_ref[...], "nt")
        gate = gate_ref[...]
        sg = _sigmoid(gate)
        dup_ref[...] = (dact * gate * sg).astype(BF16)
        dgate_ref[...] = (dact * up_ref[...] * sg * (1.0 + gate * (1.0 - sg))).astype(BF16)

    a_spec = pl.BlockSpec((None, tm, fs), lambda i, j: (j, i, 0))
    return pl.pallas_call(
        body, name="ffn_down_bwd", grid=(s // tm, N_DEV),
        in_specs=[pl.BlockSpec((tm, d), lambda i, j: (i, 0)), pl.BlockSpec((None, fs, d), lambda i, j: (j, 0, 0)),
                  a_spec, a_spec],
        out_specs=[a_spec, a_spec],
        out_shape=[_sds((N_DEV, s, fs), BF16), _sds((N_DEV, s, fs), BF16)],
        compiler_params=_params(("parallel", "arbitrary")),
    )(dff, w_down, gate, up)


def _mesh_place():
    x, y, c = lax.axis_index("x"), lax.axis_index("y"), lax.axis_index("c")
    peers = []
    for d in range(1, N_DEV):
        px = 1 - x if d & 4 else x
        py = 1 - y if d & 2 else y
        pc = 1 - c if d & 1 else c
        peers.append((d, (px, py, pc), 4 * px + 2 * py + pc))
    return 4 * x + 2 * y + c, peers


def _flat_me():
    return 4 * lax.axis_index("x") + 2 * lax.axis_index("y") + lax.axis_index("c")


def _in_hbm(a):
    return pltpu.with_memory_space_constraint(a, pltpu.HBM)


def _scatter_start(name, parts):
    n = len(parts)
    me = _flat_me()
    lands = [lax.dynamic_update_slice_in_dim(lax.empty(a.shape, a.dtype), lax.dynamic_slice_in_dim(a, me, 1, 0), me, 0)
             for a in parts]

    def body(*refs):
        ins, lnd = refs[:n], refs[n:2 * n]
        send, recv = refs[2 * n], refs[2 * n + 1]
        token = refs[-1]
        mine, peers = _mesh_place()
        for a in range(n):
            for d, dev, flat in peers:
                pltpu.make_async_remote_copy(src_ref=ins[a].at[flat], dst_ref=lnd[a].at[mine], send_sem=send.at[a * N_DEV + d],
                                             recv_sem=recv.at[a * N_DEV + d], device_id=dev, device_id_type=MESH).start()
        token[...] = jnp.zeros_like(token)

    res = pl.pallas_call(
        body, name=name,
        out_shape=[pltpu.SemaphoreType.DMA((n * N_DEV,)), pltpu.SemaphoreType.DMA((n * N_DEV,))]
        + [pltpu.HBM(a.shape, a.dtype) for a in parts] * 2 + [_sds((8, LANES), F32)],
        in_specs=[HBM] * (2 * n), out_specs=[SEM, SEM] + [HBM] * (2 * n) + [pl.BlockSpec(memory_space=pltpu.VMEM)],
        input_output_aliases={i: 2 + i for i in range(2 * n)},
        compiler_params=pltpu.CompilerParams(has_side_effects=EFFECT),
    )(*[_in_hbm(a) for a in parts], *[_in_hbm(a) for a in lands])
    return res[0], res[1], res[2:2 + n], res[2 + n:2 + 2 * n], res[-1]


def _scatter_wait(name, send, recv, parts, lands, after):
    n = len(parts)

    def body(*refs):
        ins, lnd = refs[:n], refs[n:2 * n]
        send_sem, recv_sem = refs[2 * n], refs[2 * n + 1]
        mine, peers = _mesh_place()
        for a in range(n):
            for d, dev, flat in peers:
                cp = pltpu.make_async_remote_copy(src_ref=ins[a].at[flat], dst_ref=lnd[a].at[flat],
                                                  send_sem=send_sem.at[a * N_DEV + d], recv_sem=recv_sem.at[a * N_DEV + d],
                                                  device_id=dev, device_id_type=MESH)
                cp.wait_send()
                cp.wait_recv()

    res = pl.pallas_call(
        body, name=name,
        out_shape=[pltpu.HBM(a.shape, a.dtype) for a in parts] * 2,
        in_specs=[HBM] * (2 * n) + [SEM, SEM, ANY], out_specs=[HBM] * (2 * n),
        input_output_aliases={i: i for i in range(2 * n)},
        compiler_params=pltpu.CompilerParams(has_side_effects=EFFECT),
    )(*parts, *lands, send, recv, after)
    return res[n:]


def _gather_targets():
    x, y, c = lax.axis_index("x"), lax.axis_index("y"), lax.axis_index("c")
    chips = [(x, y), (1 - x, y), (x, 1 - y), (1 - x, 1 - y)]
    same = [((cx, cy, c), 4 * cx + 2 * cy + c) for cx, cy in chips]
    other = [((cx, cy, 1 - c), 4 * cx + 2 * cy + 1 - c) for cx, cy in chips]
    return same[0][1], [other[0]] + same[1:], [flat for _, flat in other[1:]], other[0][0]


def _gather_start(shards):
    n = len(shards)
    me = _flat_me()
    lands = [lax.dynamic_update_slice_in_dim(lax.empty((N_DEV,) + a.shape, a.dtype), a[None], me, 0) for a in shards]

    def body(*refs):
        lnd, send, recv, token = refs[:n], refs[n], refs[n + 1], refs[-1]
        mine, targets, _, _ = _gather_targets()
        for a in range(n):
            for t, (dev, _) in enumerate(targets):
                pltpu.make_async_remote_copy(src_ref=lnd[a].at[mine], dst_ref=lnd[a].at[mine], send_sem=send.at[4 * a + t],
                                             recv_sem=recv.at[4 * a + t], device_id=dev, device_id_type=MESH).start()
        token[...] = jnp.zeros_like(token)

    res = pl.pallas_call(
        body, name="gather_start",
        out_shape=[pltpu.SemaphoreType.DMA((4 * n,)), pltpu.SemaphoreType.DMA((4 * n,))]
        + [pltpu.HBM(a.shape, a.dtype) for a in lands] + [_sds((8, LANES), F32)],
        in_specs=[HBM] * n, out_specs=[SEM, SEM] + [HBM] * n + [pl.BlockSpec(memory_space=pltpu.VMEM)],
        input_output_aliases={i: 2 + i for i in range(n)},
        compiler_params=pltpu.CompilerParams(has_side_effects=EFFECT),
    )(*[_in_hbm(a) for a in lands])
    return res[0], res[1], list(res[2:2 + n]), res[-1]


def _gather_forward(name, lands, first, send, recv, after):
    n = len(lands)

    def body(*refs):
        lnd, send_sem, recv_sem = refs[:n], refs[n], refs[n + 1]
        send2, recv2, token = refs[-3], refs[-2], refs[-1]
        mine, targets, _, sibling = _gather_targets()
        for a in range(n):
            for t, (dev, flat) in enumerate(targets):
                cp = pltpu.make_async_remote_copy(src_ref=lnd[a].at[mine], dst_ref=lnd[a].at[flat],
                                                  send_sem=send_sem.at[4 * (first + a) + t],
                                                  recv_sem=recv_sem.at[4 * (first + a) + t], device_id=dev, device_id_type=MESH)
                cp.wait_send()
                if t:
                    cp.wait_recv()
                    pltpu.make_async_remote_copy(src_ref=lnd[a].at[flat], dst_ref=lnd[a].at[flat], send_sem=send2.at[3 * a + t - 1],
                                                 recv_sem=recv2.at[3 * a + t - 1], device_id=sibling, device_id_type=MESH).start()
        token[...] = jnp.zeros_like(token)

    res = pl.pallas_call(
        body, name=name,
        out_shape=[pltpu.HBM(a.shape, a.dtype) for a in lands]
        + [pltpu.SemaphoreType.DMA((3 * n,)), pltpu.SemaphoreType.DMA((3 * n,)), _sds((8, LANES), F32)],
        in_specs=[HBM] * n + [SEM, SEM, ANY], out_specs=[HBM] * n + [SEM, SEM, pl.BlockSpec(memory_space=pltpu.VMEM)],
        input_output_aliases={i: i for i in range(n)},
        compiler_params=pltpu.CompilerParams(has_side_effects=EFFECT),
    )(*lands, send, recv, after)
    return list(res[:n]), res[n], res[n + 1], res[-1]


def _gather_wait(name, lands, first, recv, send2, recv2, after):
    n = len(lands)

    def body(*refs):
        lnd, recv_sem, send2_sem, recv2_sem = refs[:n], refs[n], refs[n + 1], refs[n + 2]
        mine, targets, passed, sibling = _gather_targets()
        for a in range(n):
            dev, flat = targets[0]
            pltpu.make_async_remote_copy(src_ref=lnd[a].at[mine], dst_ref=lnd[a].at[flat], send_sem=send2_sem.at[3 * a],
                                         recv_sem=recv_sem.at[4 * (first + a)], device_id=dev, device_id_type=MESH).wait_recv()
            for t in range(3):
                cp = pltpu.make_async_remote_copy(src_ref=lnd[a].at[targets[t + 1][1]], dst_ref=lnd[a].at[passed[t]],
                                                  send_sem=send2_sem.at[3 * a + t], recv_sem=recv2_sem.at[3 * a + t],
                                                  device_id=sibling, device_id_type=MESH)
                cp.wait_send()
                cp.wait_recv()

    res = pl.pallas_call(
        body, name=name, out_shape=[pltpu.HBM(a.shape, a.dtype) for a in lands],
        in_specs=[HBM] * n + [SEM, SEM, SEM, ANY], out_specs=[HBM] * n,
        input_output_aliases={i: i for i in range(n)},
        compiler_params=pltpu.CompilerParams(has_side_effects=EFFECT),
    )(*lands, recv, send2, recv2, after)
    return list(res)


def _adamw(g, w, m, v):
    m = ADAM_B1 * m + (1.0 - ADAM_B1) * g
    v = ADAM_B2 * v + (1.0 - ADAM_B2) * (g * g)
    m_hat = m / (1.0 - ADAM_B1 ** ADAM_STEP)
    v_hat = v / (1.0 - ADAM_B2 ** ADAM_STEP)
    delta = -ADAM_LR * (m_hat / (jnp.sqrt(v_hat) + ADAM_EPS) + ADAM_WD * w)
    return delta, m, v


def _update(name, parts, w, m, v, block_bytes=1 << 20):
    r, c = w.shape
    tr = max(8, min(r, (block_bytes // (4 * c)) // 8 * 8))
    while r % tr:
        tr -= 8

    def body(p_ref, w_ref, m_ref, v_ref, g_ref, d_ref, nm_ref, nv_ref):
        g = p_ref[0].astype(F32)
        for p in range(1, N_DEV):
            g = g + p_ref[p].astype(F32)
        g_ref[...] = g
        d_ref[...], nm_ref[...], nv_ref[...] = _adamw(g, w_ref[...], m_ref[...], v_ref[...])

    blk = pl.BlockSpec((tr, c), lambda i: (i, 0))
    return pl.pallas_call(
        body, name=name, grid=(r // tr,),
        in_specs=[pl.BlockSpec((N_DEV, tr, c), lambda i: (0, i, 0)), blk, blk, blk],
        out_specs=[blk] * 4, out_shape=[_sds((r, c), F32)] * 4,
        compiler_params=_params(("parallel",)),
    )(parts, w, m, v)


def _small_update(part, w, m, v):
    n = part.shape[1]

    def body(p_ref, w_ref, m_ref, v_ref, g_ref, d_ref, nm_ref, nv_ref, buf, send, recv):
        me, peers = _mesh_place()
        buf[me] = p_ref[...]
        sent = []
        for d, dev, flat in peers:
            cp = pltpu.make_async_remote_copy(src_ref=p_ref, dst_ref=buf.at[me], send_sem=send.at[d],
                                              recv_sem=recv.at[d], device_id=dev, device_id_type=MESH)
            cp.start()
            sent.append(cp)
        for d, dev, flat in peers:
            pltpu.make_async_remote_copy(src_ref=p_ref, dst_ref=buf.at[flat], send_sem=send.at[d],
                                         recv_sem=recv.at[d], device_id=dev, device_id_type=MESH).wait_recv()
        for cp in sent:
            cp.wait_send()
        g = buf[0]
        for p in range(1, N_DEV):
            g = g + buf[p]
        g_ref[...] = g
        d_ref[...], nm_ref[...], nv_ref[...] = _adamw(g, w_ref[...], m_ref[...], v_ref[...])

    vm = pl.BlockSpec(memory_space=pltpu.VMEM)
    return pl.pallas_call(
        body, name="small_update", in_specs=[vm] * 4, out_specs=[vm] * 4, out_shape=[_sds((1, n), F32)] * 4,
        scratch_shapes=[pltpu.VMEM((N_DEV, 1, n), F32), pltpu.SemaphoreType.DMA((N_DEV,)),
                        pltpu.SemaphoreType.DMA((N_DEV,))],
    )(part, w, m, v)


def _w_in_reorder(g_in, d_sb, d_fox, n_f, d):
    full = jnp.transpose(g_in, (1, 0, 2)).reshape(d, -1)
    cs = d // N_DEV

    def heads(sec, width):
        return sec.reshape(d, 3, width // HEAD_DIM, HEAD_DIM).transpose(0, 2, 1, 3).reshape(d, 3 * width)

    o1 = 3 * d_sb
    o2 = o1 + 3 * d_fox
    o3 = o2 + n_f
    gates = jnp.stack([full[:, o3:o3 + d].reshape(d, N_DEV, cs), full[:, o3 + d:].reshape(d, N_DEV, cs)], axis=2)
    f_sec = jnp.pad(full[:, o2:o3], ((0, 0), (0, F_PAD - n_f)))
    return jnp.concatenate([heads(full[:, :o1], d_sb), heads(full[:, o1:o2], d_fox), gates.reshape(d, 2 * d), f_sec], axis=1)


def _w_in_restore(dwq, dwgf, d_sb, d_fox, n_f, d):
    cs = d // N_DEV

    def heads(sec, width):
        return sec.reshape(d, width // HEAD_DIM, 3, HEAD_DIM).transpose(0, 2, 1, 3).reshape(d, 3 * width)

    gates = dwgf[:, :2 * d].reshape(d, N_DEV, 2, cs)
    full = jnp.concatenate([heads(dwq[:, :3 * d_sb], d_sb), heads(dwq[:, 3 * d_sb:], d_fox),
                            dwgf[:, 2 * d:2 * d + n_f], gates[:, :, 0].reshape(d, d), gates[:, :, 1].reshape(d, d)], axis=1)
    return jnp.transpose(full.reshape(d, N_DEV, -1), (1, 0, 2))


def kernel(x, norm_mix_pre, norm_mix_post, w_in, b_forget, w_branch_sb, w_branch_fox, w_out, norm_ffn_pre, norm_ffn_post, w_ffn_gate, w_ffn_up, w_ffn_down, loss_target, m_norm_mix_pre, m_norm_mix_post, m_w_in, m_b_forget, m_w_branch_sb, m_w_branch_fox, m_w_out, m_norm_ffn_pre, m_norm_ffn_post, m_w_ffn_gate, m_w_ffn_up, m_w_ffn_down, v_norm_mix_pre, v_norm_mix_post, v_w_in, v_b_forget, v_w_branch_sb, v_w_branch_fox, v_w_out, v_norm_ffn_pre, v_norm_ffn_post, v_w_ffn_gate, v_w_ffn_up, v_w_ffn_down):
    xs, target = x[0], loss_target[0]
    s, d = xs.shape
    d_sb, d_fox = w_branch_sb.shape[1], w_branch_fox.shape[1]
    h_sb, h_fox = d_sb // HEAD_DIM, d_fox // HEAD_DIM
    n_f = b_forget.shape[1]
    fs = w_ffn_gate.shape[2]
    cs = d // N_DEV
    n_qkv = 3 * (d_sb + d_fox)
    n_gf = 2 * d + F_PAD
    f_blk = 2 * d // LANES
    big = (w_in, w_branch_sb, w_branch_fox, w_out, w_ffn_gate, w_ffn_up, w_ffn_down)
    big_m = (m_w_in, m_w_branch_sb, m_w_branch_fox, m_w_out, m_w_ffn_gate, m_w_ffn_up, m_w_ffn_down)
    big_v = (v_w_in, v_w_branch_sb, v_w_branch_fox, v_w_out, v_w_ffn_gate, v_w_ffn_up, v_w_ffn_down)

    send1, recv1, lands, token = _gather_start([w[0].astype(BF16) for w in big])
    b_pad = jnp.pad(b_forget, ((0, 0), (0, LANES - n_f)))

    u, u_t = _pre_norm(xs, norm_mix_pre, dep=token)
    l_in, send2, recv2, token = _gather_forward("gather_in_forward", lands[0:1], 0, send1, recv1, u)
    (g_in,) = _gather_wait("gather_in_wait", l_in, 0, recv1, send2, recv2, token)
    w_cat = _w_in_reorder(g_in, d_sb, d_fox, n_f, d)
    qkv = _mm_plain("proj_qkv", "nn", u, w_cat, BF16, n=n_qkv)
    gf = _mm_plain("proj_gates", "nn", u, w_cat, F32, n_off=n_qkv, n=n_gf)
    cum_col, cum_row = _forget_fwd(gf, b_pad, f_blk)
    o_sb, o_sb_t, tot = _sb_fwd(qkv, h_sb)
    l_mid, send2, recv2, token = _gather_forward("gather_mid_forward", lands[1:4], 1, send1, recv1, o_sb)
    o_fx, o_fx_t, o_fx32, lse = _fox_fwd(qkv, cum_col, cum_row, h_fox, h_sb, token)
    g_sb, g_fx, g_out = _gather_wait("gather_mid_wait", l_mid, 1, recv1, send2, recv2, o_fx)
    w_out_full = g_out.reshape(d, d)
    l_ffn, send2, recv2, token = _gather_forward("gather_ffn_forward", lands[4:7], 4, send1, recv1, o_fx)
    merged, merged_t, a_sb, a_fx = _branch_merge(o_sb, o_fx, g_sb, g_fx, gf, token)
    mix = _mm_plain("out_proj", "nn", merged, w_out_full, F32)
    g_gate, g_up, g_down = _gather_wait("gather_ffn_wait", l_ffn, 4, recv1, send2, recv2, mix)
    h1, u2, u2_t = _mid_norms(xs, mix, norm_mix_post, norm_ffn_pre)
    gate, up, act, act_t = _ffn_up(u2, g_gate, g_up)
    tm, tn = _tile(s, 1024), _tile(d, 1024)
    ff = _matmul("ffn_down", "nn",
                 [(act, pl.BlockSpec((None, tm, fs), lambda i, j, k: (k, i, 0)),
                   g_down, pl.BlockSpec((None, fs, tn), lambda i, j, k: (k, 0, j)))],
                 (s // tm, d // tn, N_DEV), (tm, tn), _sds((s, d), F32), pl.BlockSpec((tm, tn), lambda i, j, k: (i, j)))
    loss_part, dy, dff, dg_ffn_post = _loss_head(h1, ff, target, norm_ffn_post)

    dgate, dup = _ffn_down_bwd(dff, g_down, gate, up)
    dw_down = _matmul("dw_down", "nn",
                      [(act_t, pl.BlockSpec((None, fs, s), lambda j, n, k: (j, 0, 0)),
                        dff, pl.BlockSpec((s, tn), lambda j, n, k: (0, n)))],
                      (N_DEV, d // tn, 1), (fs, tn), _sds((N_DEV, fs, d), BF16),
                      pl.BlockSpec((None, fs, tn), lambda j, n, k: (j, 0, n)))

    def dw_up(name, dact):
        return _matmul(name, "nn",
                       [(u2_t, pl.BlockSpec((tn, s), lambda j, i, k: (i, 0)),
                         dact, pl.BlockSpec((None, s, fs), lambda j, i, k: (j, 0, 0)))],
                       (N_DEV, d // tn, 1), (tn, fs), _sds((N_DEV, d, fs), BF16),
                       pl.BlockSpec((None, tn, fs), lambda j, i, k: (j, i, 0)))

    dw_gate, dw_upw = dw_up("dw_gate", dgate), dw_up("dw_up", dup)
    rs_ffn = _scatter_start("scatter_ffn", [dw_gate, dw_upw, dw_down])
    a_spec = pl.BlockSpec((None, tm, fs), lambda i, j, k: (k, i, 0))
    b_spec = pl.BlockSpec((None, tn, fs), lambda i, j, k: (k, j, 0))
    du2 = _matmul("du2", "nt", [(dgate, a_spec, g_gate, b_spec), (dup, a_spec, g_up, b_spec)],
                  (s // tm, d // tn, N_DEV), (tm, tn), _sds((s, d), F32), pl.BlockSpec((tm, tn), lambda i, j, k: (i, j)),
                  dep=rs_ffn[4])
    dh1, dmix, dg_ffn_pre, dg_mix_post = _mid_norms_bwd(dy, du2, h1, mix, norm_ffn_pre, norm_mix_post)

    da_sb, da_fx, dgf = _merge_bwd(dmix, w_out_full, gf, a_sb, a_fx)
    dw_out = _mm_plain("dw_out", "nn", merged_t, dmix, BF16).reshape(N_DEV, cs, d)

    def branch_bwd(tag, da, w_b, o_t, width):
        tb = _tile(width, 1024)
        do = _matmul("do_" + tag, "nt",
                     [(da, pl.BlockSpec((tm, cs), lambda i, j, k: (i, k)),
                       w_b, pl.BlockSpec((None, tb, cs), lambda i, j, k: (k, j, 0)))],
                     (s // tm, width // tb, N_DEV), (tm, tb), _sds((s, width), BF16),
                     pl.BlockSpec((tm, tb), lambda i, j, k: (i, j)))
        dw = _matmul("dw_" + tag, "nn",
                     [(o_t, pl.BlockSpec((width, s), lambda j, i, k: (0, 0)),
                       da, pl.BlockSpec((s, cs), lambda j, i, k: (0, j)))],
                     (N_DEV, 1, 1), (width, cs), _sds((N_DEV, width, cs), BF16),
                     pl.BlockSpec((None, width, cs), lambda j, i, k: (j, 0, 0)))
        return do, dw

    do_sb, dw_sb = branch_bwd("sb", da_sb, g_sb, o_sb_t, d_sb)
    do_fx, dw_fx = branch_bwd("fox", da_fx, g_fx, o_fx_t, d_fox)

    rs_mid = _scatter_start("scatter_mid", [dw_sb, dw_fx, dw_out])

    dqkv = _sb_bwd(qkv, do_sb, tot, h_sb, rs_mid[4])
    dqkv, dcum = _fox_bwd(dqkv, qkv, do_fx, o_fx32, lse, cum_col, cum_row, h_fox, h_sb)
    dgf, db_part = _forget_bwd(dgf, dcum, gf, b_pad, f_blk)
    dw_in = _w_in_restore(_mm_plain("dw_qkv", "nn", u_t, dqkv, BF16), _mm_plain("dw_gates", "nn", u_t, dgf, BF16),
                          d_sb, d_fox, n_f, d)
    rs_in = _scatter_start("scatter_in", [dw_in])
    du = _mm_plain("du_qkv", "nt", dqkv, w_cat, F32, tn=1024, dep=rs_in[4])
    du = _mm_plain("du_gates", "nt", dgf, w_cat, F32, tn=1024, k_off=n_qkv, init=du)
    dx, dg_mix_pre = _pre_norm_bwd(dh1, du, xs, norm_mix_pre)

    upd = {}

    def update_group(tag, rs, names, after):
        parts = _scatter_wait("scatter_" + tag + "_wait", *rs[:4], after=after)
        for nm, p in zip(names, parts):
            w, m, v = weights[nm]
            upd[nm] = [o[None] for o in _update("update_" + nm, p, w[0], m[0], v[0])]

    weights = dict(zip(("w_in", "w_branch_sb", "w_branch_fox", "w_out", "w_ffn_gate", "w_ffn_up", "w_ffn_down"),
                       zip(big, big_m, big_v)))
    update_group("ffn", rs_ffn, ("w_ffn_gate", "w_ffn_up", "w_ffn_down"), dx)
    update_group("mid", rs_mid, ("w_branch_sb", "w_branch_fox", "w_out"), upd["w_ffn_down"][0])
    update_group("in", rs_in, ("w_in",), upd["w_out"][0])

    small = ((norm_mix_pre, m_norm_mix_pre, v_norm_mix_pre), (norm_mix_post, m_norm_mix_post, v_norm_mix_post),
             (norm_ffn_pre, m_norm_ffn_pre, v_norm_ffn_pre), (norm_ffn_post, m_norm_ffn_post, v_norm_ffn_post))
    pad_f = ((0, 0), (0, LANES - n_f))
    cat = lambda i: jnp.concatenate([t[i] for t in small] + [jnp.pad((b_forget, m_b_forget, v_b_forget)[i], pad_f)], axis=1)
    sm = _small_update(jnp.concatenate([dg_mix_pre, dg_mix_post, dg_ffn_pre, dg_ffn_post, db_part], axis=1),
                       cat(0), cat(1), cat(2))
    for i, nm in enumerate(("norm_mix_pre", "norm_mix_post", "norm_ffn_pre", "norm_ffn_post")):
        upd[nm] = [o[:, i * d:(i + 1) * d] for o in sm]
    upd["b_forget"] = [o[:, 4 * d:4 * d + n_f] for o in sm]

    loss = lax.psum(loss_part[0, 0], ("x", "y", "c"))
    order = ("norm_mix_pre", "norm_mix_post", "w_in", "b_forget", "w_branch_sb", "w_branch_fox", "w_out",
             "norm_ffn_pre", "norm_ffn_post", "w_ffn_gate", "w_ffn_up", "w_ffn_down")
    return (loss, dx[None]) + tuple(upd[nm][i] for i in range(4) for nm in order)
```

```python
import jax
import jax.numpy as jnp
from jax import lax
from jax.experimental import pallas as pl
from jax.experimental.pallas import tpu as pltpu

F32 = jnp.float32
BF16 = jnp.bfloat16
MESH = pl.DeviceIdType.MESH
ANY = pl.BlockSpec(memory_space=pl.ANY)
HBM = pl.BlockSpec(memory_space=pltpu.HBM)
SEM = pl.BlockSpec(memory_space=pltpu.SEMAPHORE)
EFFECT = pltpu.SideEffectType.DATAFLOW_SIDE_EFFECTING

N_DEV = 8
HEAD_DIM = 128
RMS_EPS = 1e-6
F_PAD = 512
LANES = 128
ATT_TQ = 256
ATT_TK = 256
NEG_BIG = -1e30
VMEM_LIMIT = 56 * 1024 * 1024

ADAM_LR = 0.001
ADAM_B1 = 0.9
ADAM_B2 = 0.999
ADAM_EPS = 1e-08
ADAM_WD = 0.01
ADAM_STEP = 10

_DIMS = {"nn": ((1,), (0,)), "nt": ((1,), (1,)), "tn": ((0,), (0,))}


def _params(sem):
    return pltpu.CompilerParams(dimension_semantics=sem, vmem_limit_bytes=VMEM_LIMIT)


def _dot(a, b, mode="nn"):
    return lax.dot_general(a.astype(BF16), b.astype(BF16), (_DIMS[mode], ((), ())), preferred_element_type=F32)


def _tile(n, pref):
    if n <= pref:
        return n
    t = (pref // LANES) * LANES
    while n % t:
        t -= LANES
    return t


def _split2(v):
    hi = v.astype(BF16)
    return hi, (v - hi.astype(F32)).astype(BF16)


def _split3(v):
    a = v.astype(BF16)
    r = v - a.astype(F32)
    b = r.astype(BF16)
    return a, b, (r - b.astype(F32)).astype(BF16)


def _tri(n, cmp):
    r = lax.broadcasted_iota(jnp.int32, (n, n), 0)
    c = lax.broadcasted_iota(jnp.int32, (n, n), 1)
    return jnp.where(cmp(r, c), 1.0, 0.0).astype(BF16)


def _lane_pick(v, h):
    lane = lax.broadcasted_iota(jnp.int32, v.shape, 1)
    return jnp.sum(jnp.where(lane == h, v, 0.0), axis=1, keepdims=True)


def _lane_put(ref, rows, h, col):
    old = ref[rows, :]
    lane = lax.broadcasted_iota(jnp.int32, old.shape, 1)
    ref[rows, :] = jnp.where(lane == h, col, old)


def _sigmoid(z):
    return 1.0 / (1.0 + jnp.exp(-z))


def _log_sigmoid(z):
    return jnp.minimum(z, 0.0) - jnp.log(1.0 + jnp.exp(-jnp.abs(z)))


def _sds(shape, dtype):
    return jax.ShapeDtypeStruct(shape, dtype)


def _matmul(name, mode, pairs, grid, acc_shape, out_shape, out_specs, extras=(), epilogue=None, init=None, dep=None):
    n_p, n_e = len(pairs), len(extras)
    nk = grid[-1]
    single = not isinstance(out_shape, (list, tuple))
    n_i = 0 if init is None else 1
    n_d = 0 if dep is None else 1

    one_step = nk == 1 and init is None

    def body(*refs):
        ab = refs[:2 * n_p]
        ex = refs[2 * n_p:2 * n_p + n_e]
        ini = refs[2 * n_p + n_e:2 * n_p + n_e + n_i]
        outs = refs[2 * n_p + n_e + n_i + n_d:len(refs) - (0 if one_step else 1)]

        def finish(total):
            if epilogue is None:
                outs[0][...] = total.astype(outs[0].dtype)
            else:
                epilogue(total, ex, outs)

        t = _dot(ab[0][...], ab[1][...], mode)
        for p in range(1, n_p):
            t = t + _dot(ab[2 * p][...], ab[2 * p + 1][...], mode)
        if one_step:
            finish(t)
            return
        acc = refs[-1]
        k = pl.program_id(len(grid) - 1)

        @pl.when(k == 0)
        def _():
            acc[...] = t if init is None else ini[0][...].astype(F32) + t

        @pl.when(k > 0)
        def _():
            acc[...] += t

        @pl.when(k == nk - 1)
        def _():
            finish(acc[...])

    in_specs = [s for (_, sa, _, sb) in pairs for s in (sa, sb)] + [s for (_, s) in extras]
    args = [v for (a, _, b, _) in pairs for v in (a, b)] + [e for (e, _) in extras]
    if init is not None:
        in_specs.append(init[1])
        args.append(init[0])
    if dep is not None:
        in_specs.append(ANY)
        args.append(dep)
    return pl.pallas_call(
        body, name=name, grid=grid, in_specs=in_specs,
        out_specs=out_specs if single else list(out_specs),
        out_shape=out_shape if single else list(out_shape),
        scratch_shapes=[] if one_step else [pltpu.VMEM(acc_shape, F32)],
        compiler_params=_params(("parallel",) * (len(grid) - 1) + ("arbitrary",)),
    )(*args)


def _mm_plain(name, mode, a, b, out_dtype, *, n_off=0, n=None, k_off=0, tm=1024, tn=1536, tk=2048, init=None, dep=None):
    if mode == "nn":
        (m, kk), nn_ = a.shape, b.shape[1]
    elif mode == "nt":
        (m, kk), nn_ = a.shape, b.shape[0]
    else:
        (kk, m), nn_ = a.shape, b.shape[1]
    n = nn_ if n is None else n
    tm, tn, tk = _tile(m, tm), _tile(n, tn), _tile(kk, tk)
    while n_off % tn or n % tn:
        tn -= LANES
    while k_off % tk or kk % tk:
        tk -= LANES
    off, koff = n_off // tn, k_off // tk
    a_spec = {"nn": pl.BlockSpec((tm, tk), lambda i, j, k: (i, k)),
              "nt": pl.BlockSpec((tm, tk), lambda i, j, k: (i, k)),
              "tn": pl.BlockSpec((tk, tm), lambda i, j, k: (k, i))}[mode]
    b_spec = {"nn": pl.BlockSpec((tk, tn), lambda i, j, k: (k, j + off)),
              "nt": pl.BlockSpec((tn, tk), lambda i, j, k: (j, k + koff)),
              "tn": pl.BlockSpec((tk, tn), lambda i, j, k: (k, j))}[mode]
    o_spec = pl.BlockSpec((tm, tn), lambda i, j, k: (i, j))
    if init is not None:
        init = (init, o_spec)
    return _matmul(name, mode, [(a, a_spec, b, b_spec)], (m // tm, n // tn, kk // tk), (tm, tn),
                   _sds((m, n), out_dtype), o_spec, init=init, dep=dep)


def _rows_call(name, body, ins, outs, s, tr=256, dep=None):
    def spec(v, per_row):
        if per_row == "transposed":
            return pl.BlockSpec((v.shape[0], tr), lambda i: (0, i))
        if per_row:
            return pl.BlockSpec((tr, v.shape[1]), lambda i: (i, 0))
        return pl.BlockSpec(v.shape, lambda i: (0, 0))
    n_in = len(ins)
    deps = [] if dep is None else [dep]

    def with_dep(*refs):
        body(*refs[:n_in], *refs[n_in + len(deps):])

    return pl.pallas_call(
        with_dep, name=name, grid=(s // tr,),
        in_specs=[spec(v, p) for v, p in ins] + [ANY] * len(deps), out_specs=[spec(v, p) for v, p in outs],
        out_shape=[_sds(v.shape, v.dtype) for v, _ in outs],
        compiler_params=_params(("arbitrary",)),
    )(*[v for v, _ in ins], *deps)


def _rsq(v):
    return lax.rsqrt(jnp.mean(v * v, axis=-1, keepdims=True) + RMS_EPS)


def _norm_bwd(dy, v, r, g):
    vh = v * r
    t = dy * g
    dv = r * (t - vh * jnp.mean(t * vh, axis=-1, keepdims=True))
    return dv, jnp.sum(dy * vh, axis=0, keepdims=True)


def _accum(ref, val):
    @pl.when(pl.program_id(0) == 0)
    def _():
        ref[...] = jnp.zeros_like(ref)
    ref[...] += val


def _pre_norm(x, g, dep=None):
    def body(x_ref, g_ref, u_ref, ut_ref):
        v = x_ref[...]
        u = (v * _rsq(v) * g_ref[...]).astype(BF16)
        u_ref[...] = u
        ut_ref[...] = u.T
    s, d = x.shape
    return _rows_call("pre_norm", body, [(x, True), (g, False)],
                      [(_sds((s, d), BF16), True), (_sds((d, s), BF16), "transposed")], s, dep=dep)


def _mid_norms(x, mix, g_post, g_pre):
    def body(x_ref, mix_ref, gp_ref, gn_ref, h_ref, u_ref, ut_ref):
        mv = mix_ref[...]
        h = x_ref[...] + mv * _rsq(mv) * gp_ref[...]
        h_ref[...] = h
        u = (h * _rsq(h) * gn_ref[...]).astype(BF16)
        u_ref[...] = u
        ut_ref[...] = u.T
    s, d = x.shape
    return _rows_call("mid_norms", body, [(x, True), (mix, True), (g_post, False), (g_pre, False)],
                      [(_sds((s, d), F32), True), (_sds((s, d), BF16), True), (_sds((d, s), BF16), "transposed")], s)


def _loss_head(h1, ff, target, g):
    s, d = h1.shape

    def body(h_ref, ff_ref, t_ref, g_ref, loss_ref, dy_ref, dff_ref, dg_ref):
        fv = ff_ref[...]
        r = _rsq(fv)
        err = h_ref[...] + fv * r * g_ref[...] - t_ref[...]
        part = 0.5 * jnp.sum(jnp.mean(err * err, axis=-1, keepdims=True), axis=0, keepdims=True)
        _accum(loss_ref, jnp.broadcast_to(part, loss_ref.shape))
        dy = err * (1.0 / d)
        dy_ref[...] = dy
        dff, dg = _norm_bwd(dy, fv, r, g_ref[...])
        dff_ref[...] = dff.astype(BF16)
        _accum(dg_ref, dg)

    return _rows_call("loss_head", body, [(h1, True), (ff, True), (target, True), (g, False)],
                      [(_sds((1, LANES), F32), False), (_sds((s, d), F32), True),
                       (_sds((s, d), BF16), True), (_sds((1, d), F32), False)], s)


def _mid_norms_bwd(dy, du2, h1, mix, g_pre, g_post):
    s, d = dy.shape

    def body(dy_ref, du_ref, h_ref, mix_ref, gn_ref, gp_ref, dh_ref, dmix_ref, dgn_ref, dgp_ref):
        h = h_ref[...]
        dh, dgn = _norm_bwd(du_ref[...], h, _rsq(h), gn_ref[...])
        dh = dh + dy_ref[...]
        dh_ref[...] = dh
        _accum(dgn_ref, dgn)
        mv = mix_ref[...]
        dmix, dgp = _norm_bwd(dh, mv, _rsq(mv), gp_ref[...])
        dmix_ref[...] = dmix.astype(BF16)
        _accum(dgp_ref, dgp)

    return _rows_call("mid_norms_bwd", body,
                      [(dy, True), (du2, True), (h1, True), (mix, True), (g_pre, False), (g_post, False)],
                      [(_sds((s, d), F32), True), (_sds((s, d), BF16), True),
                       (_sds((1, d), F32), False), (_sds((1, d), F32), False)], s)


def _pre_norm_bwd(dh1, du, x, g, dep=None):
    s, d = x.shape

    def body(dh_ref, du_ref, x_ref, g_ref, dx_ref, dg_ref):
        v = x_ref[...]
        dv, dg = _norm_bwd(du_ref[...], v, _rsq(v), g_ref[...])
        dx_ref[...] = dh_ref[...] + dv
        _accum(dg_ref, dg)

    return _rows_call("pre_norm_bwd", body, [(dh1, True), (du, True), (x, True), (g, False)],
                      [(_sds((s, d), F32), True), (_sds((1, d), F32), False)], s, dep=dep)


def _forget_fwd(gf, b_pad, f_blk):
    s = gf.shape[0]
    tb = ATT_TK
    nb = s // tb

    def body(f_ref, b_ref, col_ref, row_ref):
        incl = _tri(tb, lambda r, c: c <= r)
        carry = jnp.zeros((1, LANES), F32)
        for i in range(nb):
            lf = _log_sigmoid(f_ref[pl.ds(i * tb, tb), :] + b_ref[...])
            parts = _split3(lf)
            cum = carry + _dot(incl, parts[0]) + _dot(incl, parts[1]) + _dot(incl, parts[2])
            col_ref[pl.ds(i * tb, tb), :] = cum
            row_ref[i] = cum.T
            carry = carry + jnp.sum(lf, axis=0, keepdims=True)

    return pl.pallas_call(
        body, name="forget_fwd", grid=(1,),
        in_specs=[pl.BlockSpec((s, LANES), lambda i: (0, f_blk)), pl.BlockSpec((1, LANES), lambda i: (0, 0))],
        out_specs=[pl.BlockSpec((s, LANES), lambda i: (0, 0)), pl.BlockSpec((nb, LANES, tb), lambda i: (0, 0, 0))],
        out_shape=[_sds((s, LANES), F32), _sds((nb, LANES, tb), F32)],
        compiler_params=_params(("arbitrary",)),
    )(gf, b_pad)


def _forget_bwd(dgf, dcum, gf, b_pad, f_blk):
    s = gf.shape[0]
    tb = ATT_TK
    nb = s // tb
    sec = dgf.shape[1] // F_PAD - 1

    def body(dgf_hbm, dc_ref, f_ref, b_ref, out_ref, db_ref):
        del dgf_hbm
        incl = _tri(tb, lambda r, c: c >= r)
        carry = jnp.zeros((1, LANES), F32)
        db = jnp.zeros((1, LANES), F32)
        out_ref[...] = jnp.zeros_like(out_ref)
        for i in reversed(range(nb)):
            dc = dc_ref[pl.ds(i * tb, tb), :]
            parts = _split3(dc)
            dlf = carry + _dot(incl, parts[0]) + _dot(incl, parts[1]) + _dot(incl, parts[2])
            z = f_ref[pl.ds(i * tb, tb), :] + b_ref[...]
            df = dlf * _sigmoid(-z)
            out_ref[pl.ds(i * tb, tb), pl.ds(0, LANES)] = df.astype(BF16)
            db = db + jnp.sum(df, axis=0, keepdims=True)
            carry = carry + jnp.sum(dc, axis=0, keepdims=True)
        db_ref[...] = db

    return pl.pallas_call(
        body, name="forget_bwd", grid=(1,),
        in_specs=[ANY, pl.BlockSpec((s, LANES), lambda i: (0, 0)),
                  pl.BlockSpec((s, LANES), lambda i: (0, f_blk)), pl.BlockSpec((1, LANES), lambda i: (0, 0))],
        out_specs=[pl.BlockSpec((s, F_PAD), lambda i: (0, sec)), pl.BlockSpec((1, LANES), lambda i: (0, 0))],
        out_shape=[_sds(dgf.shape, BF16), _sds((1, LANES), F32)],
        input_output_aliases={0: 0},
        compiler_params=_params(("arbitrary",)),
    )(dgf, dcum, gf, b_pad)


def _rel_index():
    r = lax.broadcasted_iota(jnp.int32, (ATT_TQ, ATT_TK), 0)
    c = lax.broadcasted_iota(jnp.int32, (ATT_TQ, ATT_TK), 1)
    return r - c


def _qkv_specs(hb0, s):
    return [pl.BlockSpec((ATT_TQ, HEAD_DIM), lambda h, i: (i, 3 * (hb0 + h))),
            pl.BlockSpec((s, HEAD_DIM), lambda h, i: (0, 3 * (hb0 + h) + 1)),
            pl.BlockSpec((s, HEAD_DIM), lambda h, i: (0, 3 * (hb0 + h) + 2))]


def _sb_fwd(qkv, n_heads):
    s = qkv.shape[0]
    scale = HEAD_DIM ** -0.5
    tq, tk = ATT_TQ, ATT_TK

    def body(q_ref, k_ref, v_ref, o_ref, ot_ref, tot_ref):
        h, i = pl.program_id(0), pl.program_id(1)

        @pl.when((h == 0) & (i == 0))
        def _():
            tot_ref[...] = jnp.zeros_like(tot_ref)

        q = q_ref[...]
        rel = _rel_index()
        upper = _tri(tk, lambda r, c: r > c)

        def step(n, carry):
            c, acc = carry
            kj = i - n
            rows = pl.ds(pl.multiple_of(kj * tk, tk), tk)
            z = _dot(q, k_ref[rows, :], "nt") * scale
            mask = rel > (kj - i) * tk
            lsz = _log_sigmoid(z)
            lk = jnp.where(mask, lsz - z, 0.0)
            hi, lo = _split2(lk)
            between = c + _dot(hi, upper) + _dot(lo, upper)
            w = jnp.where(mask, jnp.exp(lsz + between), 0.0)
            acc = acc + _dot(w, v_ref[rows, :])
            return c + jnp.sum(lk, axis=1, keepdims=True), acc

        c, acc = lax.fori_loop(0, i + 1, step, (jnp.zeros((tq, 1), F32), jnp.zeros((tq, HEAD_DIM), F32)))
        o = acc.astype(BF16)
        o_ref[...] = o
        ot_ref[...] = o.T
        _lane_put(tot_ref, pl.ds(pl.multiple_of(i * tq, tq), tq), h, c)

    return pl.pallas_call(
        body, name="sb_fwd", grid=(n_heads, s // tq),
        in_specs=_qkv_specs(0, s),
        out_specs=[pl.BlockSpec((tq, HEAD_DIM), lambda h, i: (i, h)), pl.BlockSpec((HEAD_DIM, tq), lambda h, i: (h, i)),
                   pl.BlockSpec((s, LANES), lambda h, i: (0, 0))],
        out_shape=[_sds((s, n_heads * HEAD_DIM), BF16), _sds((n_heads * HEAD_DIM, s), BF16), _sds((s, LANES), F32)],
        compiler_params=_params(("arbitrary", "arbitrary")),
    )(qkv, qkv, qkv)


def _sb_bwd(qkv, do, tot, n_heads, dep):
    s = qkv.shape[0]
    scale = HEAD_DIM ** -0.5
    tq, tk = ATT_TQ, ATT_TK
    nq = s // tq
    hd = HEAD_DIM

    def body(q_ref, k_ref, v_ref, do_ref, tot_ref, dep_ref, out_ref, dk_acc, dv_acc):
        del dep_ref
        h, i = pl.program_id(0), pl.program_id(1)

        @pl.when(i == 0)
        def _():
            dk_acc[...] = jnp.zeros_like(dk_acc)
            dv_acc[...] = jnp.zeros_like(dv_acc)

        q = q_ref[...]
        dout = do_ref[...]
        total = _lane_pick(tot_ref[...], h)
        rel = _rel_index()
        incl = _tri(tk, lambda r, c: r <= c)
        excl = _tri(tk, lambda r, c: r < c)

        def step(kj, carry):
            p_l, p_e, dq = carry
            rows = pl.ds(pl.multiple_of(kj * tk, tk), tk)
            k_t = k_ref[rows, :]
            z = _dot(q, k_t, "nt") * scale
            mask = rel > (kj - i) * tk
            lsz = _log_sigmoid(z)
            lk = jnp.where(mask, lsz - z, 0.0)
            hi, lo = _split2(lk)
            between = total - (p_l + _dot(hi, incl) + _dot(lo, incl))
            w = jnp.where(mask, jnp.exp(lsz + between), 0.0)
            e = _dot(dout, v_ref[rows, :], "nt") * w
            hi, lo = _split2(e)
            e_before = p_e + _dot(hi, excl) + _dot(lo, excl)
            sg = jnp.exp(lsz)
            dz = (jnp.where(mask, e * (1.0 - sg) - e_before * sg, 0.0) * scale).astype(BF16)
            dq = dq + _dot(dz, k_t)
            dk_acc[rows, :] += _dot(dz, q, "tn")
            dv_acc[rows, :] += _dot(w, dout, "tn")
            return p_l + jnp.sum(lk, axis=1, keepdims=True), p_e + jnp.sum(e, axis=1, keepdims=True), dq

        zero = jnp.zeros((tq, 1), F32)
        _, _, dq = lax.fori_loop(0, i + 1, step, (zero, zero, jnp.zeros((tq, hd), F32)))
        out_ref[pl.ds(pl.multiple_of(i * tq, tq), tq), pl.ds(0, hd)] = dq.astype(BF16)

        @pl.when(i == nq - 1)
        def _():
            out_ref[:, pl.ds(hd, hd)] = dk_acc[...].astype(BF16)
            out_ref[:, pl.ds(2 * hd, hd)] = dv_acc[...].astype(BF16)

    return pl.pallas_call(
        body, name="sb_bwd", grid=(n_heads, nq),
        in_specs=_qkv_specs(0, s) + [pl.BlockSpec((tq, hd), lambda h, i: (i, h)),
                                     pl.BlockSpec((tq, LANES), lambda h, i: (i, 0)), ANY],
        out_specs=pl.BlockSpec((s, 3 * hd), lambda h, i: (0, h)),
        out_shape=_sds(qkv.shape, BF16),
        scratch_shapes=[pltpu.VMEM((s, hd), F32), pltpu.VMEM((s, hd), F32)],
        compiler_params=_params(("arbitrary", "arbitrary")),
    )(qkv, qkv, qkv, do, tot, dep)


def _fox_fwd(qkv, cum_col, cum_row, n_heads, hb0, dep):
    s = qkv.shape[0]
    scale = HEAD_DIM ** -0.5
    tq, tk = ATT_TQ, ATT_TK

    def body(q_ref, k_ref, v_ref, cc_ref, cr_ref, dep_ref, o_ref, ot_ref, o32_ref, lse_ref):
        del dep_ref
        h, i = pl.program_id(0), pl.program_id(1)

        @pl.when((h == 0) & (i == 0))
        def _():
            lse_ref[...] = jnp.zeros_like(lse_ref)

        q = q_ref[...]
        cq = _lane_pick(cc_ref[...], h)
        rel = _rel_index()

        def step(kj, carry):
            m, l, acc = carry
            rows = pl.ds(pl.multiple_of(kj * tk, tk), tk)
            ck = cr_ref[kj, pl.ds(h, 1), :]
            sc = _dot(q, k_ref[rows, :], "nt") * scale + cq - ck
            sc = jnp.where(rel >= (kj - i) * tk, sc, NEG_BIG)
            m_new = jnp.maximum(m, jnp.max(sc, axis=1, keepdims=True))
            p = jnp.exp(sc - m_new)
            alpha = jnp.exp(m - m_new)
            hi, lo = _split2(p)
            v_t = v_ref[rows, :]
            return (m_new, alpha * l + jnp.sum(p, axis=1, keepdims=True), alpha * acc + _dot(hi, v_t) + _dot(lo, v_t))

        m, l, acc = lax.fori_loop(0, i + 1, step, (jnp.full((tq, 1), NEG_BIG, F32), jnp.zeros((tq, 1), F32),
                                                   jnp.zeros((tq, HEAD_DIM), F32)))
        o = acc / l
        o_ref[...] = o.astype(BF16)
        ot_ref[...] = o.astype(BF16).T
        o32_ref[...] = o
        _lane_put(lse_ref, pl.ds(pl.multiple_of(i * tq, tq), tq), h, m + jnp.log(l))

    nb = cum_row.shape[0]
    return pl.pallas_call(
        body, name="fox_fwd", grid=(n_heads, s // tq),
        in_specs=_qkv_specs(hb0, s) + [pl.BlockSpec((tq, LANES), lambda h, i: (i, 0)),
                                       pl.BlockSpec((nb, 8, tk), lambda h, i: (0, 0, 0)), ANY],
        out_specs=[pl.BlockSpec((tq, HEAD_DIM), lambda h, i: (i, h)), pl.BlockSpec((HEAD_DIM, tq), lambda h, i: (h, i)),
                   pl.BlockSpec((tq, HEAD_DIM), lambda h, i: (i, h)), pl.BlockSpec((s, LANES), lambda h, i: (0, 0))],
        out_shape=[_sds((s, n_heads * HEAD_DIM), BF16), _sds((n_heads * HEAD_DIM, s), BF16),
                   _sds((s, n_heads * HEAD_DIM), F32), _sds((s, LANES), F32)],
        compiler_params=_params(("arbitrary", "arbitrary")),
    )(qkv, qkv, qkv, cum_col, cum_row, dep)


def _fox_bwd(dqkv, qkv, do, o, lse, cum_col, cum_row, n_heads, hb0):
    s = qkv.shape[0]
    scale = HEAD_DIM ** -0.5
    tq, tk = ATT_TQ, ATT_TK
    nq = s // tq
    hd = HEAD_DIM

    def body(dqkv_hbm, q_ref, k_ref, v_ref, do_ref, o_ref, lse_ref, cc_ref, cr_ref, out_ref, dc_ref,
             dk_acc, dv_acc, col_acc):
        del dqkv_hbm
        h, i = pl.program_id(0), pl.program_id(1)

        @pl.when((h == 0) & (i == 0))
        def _():
            dc_ref[...] = jnp.zeros_like(dc_ref)

        @pl.when(i == 0)
        def _():
            dk_acc[...] = jnp.zeros_like(dk_acc)
            dv_acc[...] = jnp.zeros_like(dv_acc)
            col_acc[...] = jnp.zeros_like(col_acc)

        q = q_ref[...]
        dout = do_ref[...]
        delta = jnp.sum(dout.astype(F32) * o_ref[...], axis=1, keepdims=True)
        lse_q = _lane_pick(lse_ref[...], h)
        cq = _lane_pick(cc_ref[...], h)
        rel = _rel_index()

        def step(kj, carry):
            dq, row_sum = carry
            rows = pl.ds(pl.multiple_of(kj * tk, tk), tk)
            k_t = k_ref[rows, :]
            ck = cr_ref[kj, pl.ds(h, 1), :]
            sc = _dot(q, k_t, "nt") * scale + cq - ck
            p = jnp.where(rel >= (kj - i) * tk, jnp.exp(sc - lse_q), 0.0)
            ds_f = p * (_dot(dout, v_ref[rows, :], "nt") - delta)
            col_acc[kj] += jnp.broadcast_to(jnp.sum(ds_f, axis=0, keepdims=True), (8, tk))
            ds = (ds_f * scale).astype(BF16)
            dk_acc[rows, :] += _dot(ds, q, "tn")
            dv_acc[rows, :] += _dot(p, dout, "tn")
            return dq + _dot(ds, k_t), row_sum + jnp.sum(ds_f, axis=1, keepdims=True)

        dq, row_sum = lax.fori_loop(0, i + 1, step, (jnp.zeros((tq, hd), F32), jnp.zeros((tq, 1), F32)))
        q_rows = pl.ds(pl.multiple_of(i * tq, tq), tq)
        out_ref[q_rows, pl.ds(0, hd)] = dq.astype(BF16)
        _lane_put(dc_ref, q_rows, h, row_sum)

        @pl.when(i == nq - 1)
        def _():
            out_ref[:, pl.ds(hd, hd)] = dk_acc[...].astype(BF16)
            out_ref[:, pl.ds(2 * hd, hd)] = dv_acc[...].astype(BF16)
            lane = lax.broadcasted_iota(jnp.int32, (tk, LANES), 1)
            for kj in range(nb):
                col = jnp.broadcast_to(col_acc[kj][0:1, :], (LANES, tk)).T
                old = dc_ref[pl.ds(kj * tk, tk), :]
                dc_ref[pl.ds(kj * tk, tk), :] = jnp.where(lane == h, old - col, old)

    nb = cum_row.shape[0]
    return pl.pallas_call(
        body, name="fox_bwd", grid=(n_heads, nq),
        in_specs=[ANY] + _qkv_specs(hb0, s) + [
            pl.BlockSpec((tq, hd), lambda h, i: (i, h)), pl.BlockSpec((tq, hd), lambda h, i: (i, h)),
            pl.BlockSpec((tq, LANES), lambda h, i: (i, 0)), pl.BlockSpec((tq, LANES), lambda h, i: (i, 0)),
            pl.BlockSpec((nb, 8, tk), lambda h, i: (0, 0, 0))],
        out_specs=[pl.BlockSpec((s, 3 * hd), lambda h, i: (0, hb0 + h)), pl.BlockSpec((s, LANES), lambda h, i: (0, 0))],
        out_shape=[_sds(dqkv.shape, BF16), _sds((s, LANES), F32)],
        scratch_shapes=[pltpu.VMEM((s, hd), F32), pltpu.VMEM((s, hd), F32), pltpu.VMEM((s // tk, 8, tk), F32)],
        input_output_aliases={0: 0},
        compiler_params=_params(("arbitrary", "arbitrary")),
    )(dqkv, qkv, qkv, qkv, do, o, lse, cum_col, cum_row)


def _branch_merge(o_sb, o_fx, w_sb, w_fx, gf, dep, tm=1024):
    s = o_sb.shape[0]
    cs = w_sb.shape[2]
    tm = _tile(s, tm)

    def body(osb_ref, ofx_ref, wsb_ref, wfx_ref, g_ref, dep_ref, merged_ref, mt_ref, asb_ref, afx_ref):
        del dep_ref
        a_sb = _dot(osb_ref[...], wsb_ref[...])
        a_fx = _dot(ofx_ref[...], wfx_ref[...])
        g = g_ref[...]
        merged = (_sigmoid(g[:, :cs]) * a_sb + _sigmoid(g[:, cs:]) * a_fx).astype(BF16)
        merged_ref[...] = merged
        mt_ref[...] = merged.T
        asb_ref[...] = a_sb.astype(BF16)
        afx_ref[...] = a_fx.astype(BF16)

    blk = pl.BlockSpec((tm, cs), lambda i, j: (i, j))
    out = _sds((s, N_DEV * cs), BF16)
    return pl.pallas_call(
        body, name="branch_merge", grid=(s // tm, N_DEV),
        in_specs=[pl.BlockSpec((tm, o_sb.shape[1]), lambda i, j: (i, 0)),
                  pl.BlockSpec((tm, o_fx.shape[1]), lambda i, j: (i, 0)),
                  pl.BlockSpec((None,) + w_sb.shape[1:], lambda i, j: (j, 0, 0)),
                  pl.BlockSpec((None,) + w_fx.shape[1:], lambda i, j: (j, 0, 0)),
                  pl.BlockSpec((tm, 2 * cs), lambda i, j: (i, j)), ANY],
        out_specs=[blk, pl.BlockSpec((cs, tm), lambda i, j: (j, i)), blk, blk],
        out_shape=[out, _sds((N_DEV * cs, s), BF16), out, out],
        compiler_params=_params(("parallel", "arbitrary")),
    )(o_sb, o_fx, w_sb, w_fx, gf, dep)


def _merge_bwd(dmix, w_out, gf, a_sb, a_fx, tm=1024, tk=2048):
    s, d = dmix.shape
    cs = d // N_DEV
    tm, tk = _tile(s, tm), _tile(d, tk)

    def epilogue(acc, ex, outs):
        g, a_sb, a_fx = ex[0][...], ex[1][...].astype(F32), ex[2][...].astype(F32)
        s_sb, s_fx = _sigmoid(g[:, :cs]), _sigmoid(g[:, cs:])
        outs[0][...] = (acc * s_sb).astype(BF16)
        outs[1][...] = (acc * s_fx).astype(BF16)
        outs[2][...] = jnp.concatenate([acc * a_sb * s_sb * (1.0 - s_sb), acc * a_fx * s_fx * (1.0 - s_fx)],
                                       axis=1).astype(BF16)

    blk = pl.BlockSpec((tm, cs), lambda i, j, k: (i, j))
    wide = pl.BlockSpec((tm, 2 * cs), lambda i, j, k: (i, j))
    return _matmul(
        "merge_bwd", "nt",
        [(dmix, pl.BlockSpec((tm, tk), lambda i, j, k: (i, k)), w_out, pl.BlockSpec((cs, tk), lambda i, j, k: (j, k)))],
        (s // tm, N_DEV, d // tk), (tm, cs),
        [_sds((s, d), BF16), _sds((s, d), BF16), _sds(gf.shape, BF16)], [blk, blk, wide],
        extras=[(gf, wide), (a_sb, blk), (a_fx, blk)], epilogue=epilogue)


def _ffn_up(u2, w_gate, w_up, tm=1024):
    s, d = u2.shape
    fs = w_gate.shape[2]
    tm = _tile(s, tm)

    def body(u_ref, wg_ref, wu_ref, gate_ref, up_ref, act_ref, actt_ref):
        u = u_ref[...]
        gate = _dot(u, wg_ref[...])
        up = _dot(u, wu_ref[...])
        gate_ref[...] = gate
        up_ref[...] = up
        act = (gate * _sigmoid(gate) * up).astype(BF16)
        act_ref[...] = act
        actt_ref[...] = act.T

    w_spec = pl.BlockSpec((None, d, fs), lambda i, j: (j, 0, 0))
    o_spec = pl.BlockSpec((None, tm, fs), lambda i, j: (j, i, 0))
    return pl.pallas_call(
        body, name="ffn_up", grid=(s // tm, N_DEV),
        in_specs=[pl.BlockSpec((tm, d), lambda i, j: (i, 0)), w_spec, w_spec],
        out_specs=[o_spec, o_spec, o_spec, pl.BlockSpec((None, fs, tm), lambda i, j: (j, 0, i))],
        out_shape=[_sds((N_DEV, s, fs), F32), _sds((N_DEV, s, fs), F32), _sds((N_DEV, s, fs), BF16),
                   _sds((N_DEV, fs, s), BF16)],
        compiler_params=_params(("parallel", "arbitrary")),
    )(u2, w_gate, w_up)


def _ffn_down_bwd(dff, w_down, gate, up, tm=1024):
    s, d = dff.shape
    fs = w_down.shape[1]
    tm = _tile(s, tm)

    def body(dff_ref, wd_ref, gate_ref, up_ref, dgate_ref, dup_ref):
        dact = _dot(dff_ref[...], wd_ref[...], "nt")
        gate = gate_ref[...]
        sg = _sigmoid(gate)
        dup_ref[...] = (dact * gate * sg).astype(BF16)
        dgate_ref[...] = (dact * up_ref[...] * sg * (1.0 + gate * (1.0 - sg))).astype(BF16)

    a_spec = pl.BlockSpec((None, tm, fs), lambda i, j: (j, i, 0))
    return pl.pallas_call(
        body, name="ffn_down_bwd", grid=(s // tm, N_DEV),
        in_specs=[pl.BlockSpec((tm, d), lambda i, j: (i, 0)), pl.BlockSpec((None, fs, d), lambda i, j: (j, 0, 0)),
                  a_spec, a_spec],
        out_specs=[a_spec, a_spec],
        out_shape=[_sds((N_DEV, s, fs), BF16), _sds((N_DEV, s, fs), BF16)],
        compiler_params=_params(("parallel", "arbitrary")),
    )(dff, w_down, gate, up)


def _mesh_place():
    x, y, c = lax.axis_index("x"), lax.axis_index("y"), lax.axis_index("c")
    peers = []
    for d in range(1, N_DEV):
        px = 1 - x if d & 4 else x
        py = 1 - y if d & 2 else y
        pc = 1 - c if d & 1 else c
        peers.append((d, (px, py, pc), 4 * px + 2 * py + pc))
    return 4 * x + 2 * y + c, peers


def _flat_me():
    return 4 * lax.axis_index("x") + 2 * lax.axis_index("y") + lax.axis_index("c")


def _in_hbm(a):
    return pltpu.with_memory_space_constraint(a, pltpu.HBM)


def _scatter_start(name, parts):
    n = len(parts)
    me = _flat_me()
    lands = [lax.dynamic_update_slice_in_dim(lax.empty(a.shape, a.dtype), lax.dynamic_slice_in_dim(a, me, 1, 0), me, 0)
             for a in parts]

    def body(*refs):
        ins, lnd = refs[:n], refs[n:2 * n]
        send, recv = refs[2 * n], refs[2 * n + 1]
        token = refs[-1]
        mine, peers = _mesh_place()
        for a in range(n):
            for d, dev, flat in peers:
                pltpu.make_async_remote_copy(src_ref=ins[a].at[flat], dst_ref=lnd[a].at[mine], send_sem=send.at[a * N_DEV + d],
                                             recv_sem=recv.at[a * N_DEV + d], device_id=dev, device_id_type=MESH).start()
        token[...] = jnp.zeros_like(token)

    res = pl.pallas_call(
        body, name=name,
        out_shape=[pltpu.SemaphoreType.DMA((n * N_DEV,)), pltpu.SemaphoreType.DMA((n * N_DEV,))]
        + [pltpu.HBM(a.shape, a.dtype) for a in parts] * 2 + [_sds((8, LANES), F32)],
        in_specs=[HBM] * (2 * n), out_specs=[SEM, SEM] + [HBM] * (2 * n) + [pl.BlockSpec(memory_space=pltpu.VMEM)],
        input_output_aliases={i: 2 + i for i in range(2 * n)},
        compiler_params=pltpu.CompilerParams(has_side_effects=EFFECT),
    )(*[_in_hbm(a) for a in parts], *[_in_hbm(a) for a in lands])
    return res[0], res[1], res[2:2 + n], res[2 + n:2 + 2 * n], res[-1]


def _scatter_wait(name, send, recv, parts, lands, after):
    n = len(parts)

    def body(*refs):
        ins, lnd = refs[:n], refs[n:2 * n]
        send_sem, recv_sem = refs[2 * n], refs[2 * n + 1]
        mine, peers = _mesh_place()
        for a in range(n):
            for d, dev, flat in peers:
                cp = pltpu.make_async_remote_copy(src_ref=ins[a].at[flat], dst_ref=lnd[a].at[flat],
                                                  send_sem=send_sem.at[a * N_DEV + d], recv_sem=recv_sem.at[a * N_DEV + d],
                                                  device_id=dev, device_id_type=MESH)
                cp.wait_send()
                cp.wait_recv()

    res = pl.pallas_call(
        body, name=name,
        out_shape=[pltpu.HBM(a.shape, a.dtype) for a in parts] * 2,
        in_specs=[HBM] * (2 * n) + [SEM, SEM, ANY], out_specs=[HBM] * (2 * n),
        input_output_aliases={i: i for i in range(2 * n)},
        compiler_params=pltpu.CompilerParams(has_side_effects=EFFECT),
    )(*parts, *lands, send, recv, after)
    return res[n:]


def _gather_targets():
    x, y, c = lax.axis_index("x"), lax.axis_index("y"), lax.axis_index("c")
    chips = [(x, y), (1 - x, y), (x, 1 - y), (1 - x, 1 - y)]
    same = [((cx, cy, c), 4 * cx + 2 * cy + c) for cx, cy in chips]
    other = [((cx, cy, 1 - c), 4 * cx + 2 * cy + 1 - c) for cx, cy in chips]
    return same[0][1], [other[0]] + same[1:], [flat for _, flat in other[1:]], other[0][0]


def _gather_start(shards):
    n = len(shards)
    me = _flat_me()
    lands = [lax.dynamic_update_slice_in_dim(lax.empty((N_DEV,) + a.shape, a.dtype), a[None], me, 0) for a in shards]

    def body(*refs):
        lnd, send, recv, token = refs[:n], refs[n], refs[n + 1], refs[-1]
        mine, targets, _, _ = _gather_targets()
        for a in range(n):
            for t, (dev, _) in enumerate(targets):
                pltpu.make_async_remote_copy(src_ref=lnd[a].at[mine], dst_ref=lnd[a].at[mine], send_sem=send.at[4 * a + t],
                                             recv_sem=recv.at[4 * a + t], device_id=dev, device_id_type=MESH).start()
        token[...] = jnp.zeros_like(token)

    res = pl.pallas_call(
        body, name="gather_start",
        out_shape=[pltpu.SemaphoreType.DMA((4 * n,)), pltpu.SemaphoreType.DMA((4 * n,))]
        + [pltpu.HBM(a.shape, a.dtype) for a in lands] + [_sds((8, LANES), F32)],
        in_specs=[HBM] * n, out_specs=[SEM, SEM] + [HBM] * n + [pl.BlockSpec(memory_space=pltpu.VMEM)],
        input_output_aliases={i: 2 + i for i in range(n)},
        compiler_params=pltpu.CompilerParams(has_side_effects=EFFECT),
    )(*[_in_hbm(a) for a in lands])
    return res[0], res[1], list(res[2:2 + n]), res[-1]


def _gather_forward(name, lands, first, send, recv, after):
    n = len(lands)

    def body(*refs):
        lnd, send_sem, recv_sem = refs[:n], refs[n], refs[n + 1]
        send2, recv2, token = refs[-3], refs[-2], refs[-1]
        mine, targets, _, sibling = _gather_targets()
        for a in range(n):
            for t, (dev, flat) in enumerate(targets):
                cp = pltpu.make_async_remote_copy(src_ref=lnd[a].at[mine], dst_ref=lnd[a].at[flat],
                                                  send_sem=send_sem.at[4 * (first + a) + t],
                                                  recv_sem=recv_sem.at[4 * (first + a) + t], device_id=dev, device_id_type=MESH)
                cp.wait_send()
                if t:
                    cp.wait_recv()
                    pltpu.make_async_remote_copy(src_ref=lnd[a].at[flat], dst_ref=lnd[a].at[flat], send_sem=send2.at[3 * a + t - 1],
                                                 recv_sem=recv2.at[3 * a + t - 1], device_id=sibling, device_id_type=MESH).start()
        token[...] = jnp.zeros_like(token)

    res = pl.pallas_call(
        body, name=name,
        out_shape=[pltpu.HBM(a.shape, a.dtype) for a in lands]
        + [pltpu.SemaphoreType.DMA((3 * n,)), pltpu.SemaphoreType.DMA((3 * n,)), _sds((8, LANES), F32)],
        in_specs=[HBM] * n + [SEM, SEM, ANY], out_specs=[HBM] * n + [SEM, SEM, pl.BlockSpec(memory_space=pltpu.VMEM)],
        input_output_aliases={i: i for i in range(n)},
        compiler_params=pltpu.CompilerParams(has_side_effects=EFFECT),
    )(*lands, send, recv, after)
    return list(res[:n]), res[n], res[n + 1], res[-1]


def _gather_wait(name, lands, first, recv, send2, recv2, after):
    n = len(lands)

    def body(*refs):
        lnd, recv_sem, send2_sem, recv2_sem = refs[:n], refs[n], refs[n + 1], refs[n + 2]
        mine, targets, passed, sibling = _gather_targets()
        for a in range(n):
            dev, flat = targets[0]
            pltpu.make_async_remote_copy(src_ref=lnd[a].at[mine], dst_ref=lnd[a].at[flat], send_sem=send2_sem.at[3 * a],
                                         recv_sem=recv_sem.at[4 * (first + a)], device_id=dev, device_id_type=MESH).wait_recv()
            for t in range(3):
                cp = pltpu.make_async_remote_copy(src_ref=lnd[a].at[targets[t + 1][1]], dst_ref=lnd[a].at[passed[t]],
                                                  send_sem=send2_sem.at[3 * a + t], recv_sem=recv2_sem.at[3 * a + t],
                                                  device_id=sibling, device_id_type=MESH)
                cp.wait_send()
                cp.wait_recv()

    res = pl.pallas_call(
        body, name=name, out_shape=[pltpu.HBM(a.shape, a.dtype) for a in lands],
        in_specs=[HBM] * n + [SEM, SEM, SEM, ANY], out_specs=[HBM] * n,
        input_output_aliases={i: i for i in range(n)},
        compiler_params=pltpu.CompilerParams(has_side_effects=EFFECT),
    )(*lands, recv, send2, recv2, after)
    return list(res)


def _adamw(g, w, m, v):
    m = ADAM_B1 * m + (1.0 - ADAM_B1) * g
    v = ADAM_B2 * v + (1.0 - ADAM_B2) * (g * g)
    m_hat = m / (1.0 - ADAM_B1 ** ADAM_STEP)
    v_hat = v / (1.0 - ADAM_B2 ** ADAM_STEP)
    delta = -ADAM_LR * (m_hat / (jnp.sqrt(v_hat) + ADAM_EPS) + ADAM_WD * w)
    return delta, m, v


def _update(name, parts, w, m, v, layout=None, block_bytes=1 << 20):
    _, r, c = w.shape
    cp = parts.shape[2]
    tr = max(8, min(r, (block_bytes // (4 * cp)) // 8 * 8))
    while r % tr:
        tr -= 8

    def body(p_ref, w_ref, m_ref, v_ref, g_ref, d_ref, nm_ref, nv_ref, *scratch):
        g = p_ref[0].astype(F32)
        for p in range(1, N_DEV):
            g = g + p_ref[p].astype(F32)
        if layout is not None:
            s1, s2, lg = layout.my_shifts()
            lane = lax.broadcasted_iota(jnp.int32, g.shape, 1)
            scratch[0][...] = jnp.where(lane < lg, pltpu.roll(g, cp - s1, 1), pltpu.roll(g, cp - s2, 1))
            g = scratch[0][:, 0:c]
        g_ref[...] = g
        d_ref[...], nm_ref[...], nv_ref[...] = _adamw(g, w_ref[...], m_ref[...], v_ref[...])

    blk = pl.BlockSpec((None, tr, c), lambda i: (0, i, 0))
    return pl.pallas_call(
        body, name=name, grid=(r // tr,),
        in_specs=[pl.BlockSpec((N_DEV, tr, cp), lambda i: (0, i, 0)), blk, blk, blk],
        out_specs=[blk] * 4, out_shape=[_sds((1, r, c), F32)] * 4,
        scratch_shapes=[] if layout is None else [pltpu.VMEM((tr, cp), F32)],
        compiler_params=_params(("parallel",)),
    )(parts, w, m, v)


def _small_update(part, w, m, v):
    n = part.shape[1]

    def body(p_ref, w_ref, m_ref, v_ref, g_ref, d_ref, nm_ref, nv_ref, buf, send, recv):
        me, peers = _mesh_place()
        buf[me] = p_ref[...]
        sent = []
        for d, dev, flat in peers:
            cp = pltpu.make_async_remote_copy(src_ref=p_ref, dst_ref=buf.at[me], send_sem=send.at[d],
                                              recv_sem=recv.at[d], device_id=dev, device_id_type=MESH)
            cp.start()
            sent.append(cp)
        for d, dev, flat in peers:
            pltpu.make_async_remote_copy(src_ref=p_ref, dst_ref=buf.at[flat], send_sem=send.at[d],
                                         recv_sem=recv.at[d], device_id=dev, device_id_type=MESH).wait_recv()
        for cp in sent:
            cp.wait_send()
        g = buf[0]
        for p in range(1, N_DEV):
            g = g + buf[p]
        g_ref[...] = g
        d_ref[...], nm_ref[...], nv_ref[...] = _adamw(g, w_ref[...], m_ref[...], v_ref[...])

    vm = pl.BlockSpec(memory_space=pltpu.VMEM)
    return pl.pallas_call(
        body, name="small_update", in_specs=[vm] * 4, out_specs=[vm] * 4, out_shape=[_sds((1, n), F32)] * 4,
        scratch_shapes=[pltpu.VMEM((N_DEV, 1, n), F32), pltpu.SemaphoreType.DMA((N_DEV,)),
                        pltpu.SemaphoreType.DMA((N_DEV,))],
    )(part, w, m, v)


class _WInLayout:
    def __init__(self, n8, n_f, d_sb, d_fox, d):
        assert n8 % LANES == 1 and n_f < LANES and d % (N_DEV * LANES) == 0
        self.n8, self.n_f, self.d = n8, n_f, d
        self.sp = n8 // LANES
        self.wp = (n8 + 2 * LANES - 2) // LANES * LANES
        self.n_qkv = 3 * (d_sb + d_fox)
        nq, dt, tc = self.n_qkv // LANES, d // LANES, d // N_DEV // LANES
        h_sb, h_fox = d_sb // HEAD_DIM, d_fox // HEAD_DIM
        self.sources = {}
        self.part_tile = {}
        for p in range(N_DEV):
            lg = min(max(self.n_qkv + n_f - n8 * p, 0), n8)
            s1, s2 = p, p + LANES - n_f
            spans = []
            if lg > 0:
                spans.append(("a", self.sp * p, s1 // LANES, (lg + s1 - 1) // LANES))
            if lg < n8:
                spans.append(("g", self.sp * p - 1 - nq, (lg + s2) // LANES, (n8 - 1 + s2) // LANES))
            for kind, base, first, last in spans:
                for i in range(first, last + 1):
                    assert (p, i) not in self.part_tile
                    self.part_tile[(p, i)] = (kind, base + i)
                    self.sources.setdefault((kind, base + i), []).append((p, i))
        self.cat_tiles = [("a", r * h_sb + h) for h in range(h_sb) for r in range(3)]
        self.cat_tiles += [("a", 3 * h_sb + r * h_fox + h) for h in range(h_fox) for r in range(3)]
        self.cat_tiles += [("g", which * dt + j * tc + half) for j in range(N_DEV) for which in (0, 1) for half in range(tc)]
        self.cat_tiles += [("a", nq)] + [None] * (F_PAD // LANES - 1)
        self.cat_index = {key: c for c, key in enumerate(self.cat_tiles) if key is not None}

    def my_shifts(self):
        me = _flat_me()
        return me, me + LANES - self.n_f, jnp.clip(self.n_qkv + self.n_f - self.n8 * me, 0, self.n8)


def _lane_tile(i):
    return pl.ds(i * LANES, LANES)


def _w_in_shift(w_in, lay, tr=256):
    _, d, n8 = w_in.shape

    def body(w_ref, o_ref, buf):
        buf[...] = jnp.zeros_like(buf)
        buf[:, 0:n8] = w_ref[...]
        v = buf[...]
        s1, s2, lg = lay.my_shifts()
        pos = lax.broadcasted_iota(jnp.int32, v.shape, 1)
        o_ref[...] = jnp.where(pos < lg + s1, pltpu.roll(v, s1, 1),
                               jnp.where(pos >= lg + s2, pltpu.roll(v, s2, 1), 0.0)).astype(BF16)

    return pl.pallas_call(
        body, name="w_in_shift", grid=(d // tr,),
        in_specs=[pl.BlockSpec((None, tr, n8), lambda i: (0, i, 0))],
        out_specs=pl.BlockSpec((tr, lay.wp), lambda i: (i, 0)), out_shape=_sds((d, lay.wp), BF16),
        scratch_shapes=[pltpu.VMEM((tr, lay.wp), F32)],
        compiler_params=_params(("parallel",)),
    )(w_in)


def _w_in_build(g_in, lay, tr=256):
    d = g_in.shape[1]
    width = len(lay.cat_tiles) * LANES

    def body(g_ref, o_ref):
        for c, key in enumerate(lay.cat_tiles):
            if key is None:
                o_ref[:, _lane_tile(c)] = jnp.zeros((tr, LANES), BF16)
                continue
            (p, i), *more = lay.sources[key]
            val = g_ref[p, :, _lane_tile(i)]
            for p2, i2 in more:
                val = val + g_ref[p2, :, _lane_tile(i2)]
            o_ref[:, _lane_tile(c)] = val

    return pl.pallas_call(
        body, name="w_in_build", grid=(d // tr,),
        in_specs=[pl.BlockSpec((N_DEV, tr, lay.wp), lambda i: (0, i, 0))],
        out_specs=pl.BlockSpec((tr, width), lambda i: (i, 0)), out_shape=_sds((d, width), BF16),
        compiler_params=_params(("parallel",)),
    )(g_in)


def _w_in_grad_parts(dwq, dwgf, lay, tr=256):
    d = dwq.shape[0]
    nq = lay.n_qkv // LANES

    def body(q_ref, g_ref, o_ref):
        for p in range(N_DEV):
            for i in range(lay.wp // LANES):
                key = lay.part_tile.get((p, i))
                if key is None:
                    o_ref[p, :, _lane_tile(i)] = jnp.zeros((tr, LANES), BF16)
                    continue
                c = lay.cat_index[key]
                o_ref[p, :, _lane_tile(i)] = q_ref[:, _lane_tile(c)] if c < nq else g_ref[:, _lane_tile(c - nq)]

    return pl.pallas_call(
        body, name="w_in_grad_parts", grid=(d // tr,),
        in_specs=[pl.BlockSpec((tr, dwq.shape[1]), lambda i: (i, 0)), pl.BlockSpec((tr, dwgf.shape[1]), lambda i: (i, 0))],
        out_specs=pl.BlockSpec((N_DEV, tr, lay.wp), lambda i: (0, i, 0)), out_shape=_sds((N_DEV, d, lay.wp), BF16),
        compiler_params=_params(("parallel",)),
    )(dwq, dwgf)


def kernel(x, norm_mix_pre, norm_mix_post, w_in, b_forget, w_branch_sb, w_branch_fox, w_out, norm_ffn_pre, norm_ffn_post, w_ffn_gate, w_ffn_up, w_ffn_down, loss_target, m_norm_mix_pre, m_norm_mix_post, m_w_in, m_b_forget, m_w_branch_sb, m_w_branch_fox, m_w_out, m_norm_ffn_pre, m_norm_ffn_post, m_w_ffn_gate, m_w_ffn_up, m_w_ffn_down, v_norm_mix_pre, v_norm_mix_post, v_w_in, v_b_forget, v_w_branch_sb, v_w_branch_fox, v_w_out, v_norm_ffn_pre, v_norm_ffn_post, v_w_ffn_gate, v_w_ffn_up, v_w_ffn_down):
    xs, target = x[0], loss_target[0]
    s, d = xs.shape
    d_sb, d_fox = w_branch_sb.shape[1], w_branch_fox.shape[1]
    h_sb, h_fox = d_sb // HEAD_DIM, d_fox // HEAD_DIM
    n_f = b_forget.shape[1]
    fs = w_ffn_gate.shape[2]
    cs = d // N_DEV
    n_qkv = 3 * (d_sb + d_fox)
    n_gf = 2 * d + F_PAD
    f_blk = 2 * d // LANES
    big = (w_in, w_branch_sb, w_branch_fox, w_out, w_ffn_gate, w_ffn_up, w_ffn_down)
    big_m = (m_w_in, m_w_branch_sb, m_w_branch_fox, m_w_out, m_w_ffn_gate, m_w_ffn_up, m_w_ffn_down)
    big_v = (v_w_in, v_w_branch_sb, v_w_branch_fox, v_w_out, v_w_ffn_gate, v_w_ffn_up, v_w_ffn_down)

    lay = _WInLayout(w_in.shape[2], n_f, d_sb, d_fox, d)
    send1, recv1, lands, token = _gather_start([_w_in_shift(w_in, lay)] + [w[0].astype(BF16) for w in big[1:]])
    b_pad = jnp.pad(b_forget, ((0, 0), (0, LANES - n_f)))

    u, u_t = _pre_norm(xs, norm_mix_pre, dep=token)
    l_in, send2, recv2, token = _gather_forward("gather_in_forward", lands[0:1], 0, send1, recv1, u)
    (g_in,) = _gather_wait("gather_in_wait", l_in, 0, recv1, send2, recv2, token)
    w_cat = _w_in_build(g_in, lay)
    qkv = _mm_plain("proj_qkv", "nn", u, w_cat, BF16, n=n_qkv)
    gf = _mm_plain("proj_gates", "nn", u, w_cat, F32, n_off=n_qkv, n=n_gf)
    cum_col, cum_row = _forget_fwd(gf, b_pad, f_blk)
    o_sb, o_sb_t, tot = _sb_fwd(qkv, h_sb)
    l_mid, send2, recv2, token = _gather_forward("gather_mid_forward", lands[1:4], 1, send1, recv1, o_sb)
    o_fx, o_fx_t, o_fx32, lse = _fox_fwd(qkv, cum_col, cum_row, h_fox, h_sb, token)
    g_sb, g_fx, g_out = _gather_wait("gather_mid_wait", l_mid, 1, recv1, send2, recv2, o_fx)
    w_out_full = g_out.reshape(d, d)
    l_ffn, send2, recv2, token = _gather_forward("gather_ffn_forward", lands[4:7], 4, send1, recv1, o_fx)
    merged, merged_t, a_sb, a_fx = _branch_merge(o_sb, o_fx, g_sb, g_fx, gf, token)
    mix = _mm_plain("out_proj", "nn", merged, w_out_full, F32)
    g_gate, g_up, g_down = _gather_wait("gather_ffn_wait", l_ffn, 4, recv1, send2, recv2, mix)
    h1, u2, u2_t = _mid_norms(xs, mix, norm_mix_post, norm_ffn_pre)
    gate, up, act, act_t = _ffn_up(u2, g_gate, g_up)
    tm, tn = _tile(s, 1024), _tile(d, 1024)
    ff = _matmul("ffn_down", "nn",
                 [(act, pl.BlockSpec((None, tm, fs), lambda i, j, k: (k, i, 0)),
                   g_down, pl.BlockSpec((None, fs, tn), lambda i, j, k: (k, 0, j)))],
                 (s // tm, d // tn, N_DEV), (tm, tn), _sds((s, d), F32), pl.BlockSpec((tm, tn), lambda i, j, k: (i, j)))
    loss_part, dy, dff, dg_ffn_post = _loss_head(h1, ff, target, norm_ffn_post)

    dgate, dup = _ffn_down_bwd(dff, g_down, gate, up)
    dw_down = _matmul("dw_down", "nn",
                      [(act_t, pl.BlockSpec((None, fs, s), lambda j, n, k: (j, 0, 0)),
                        dff, pl.BlockSpec((s, tn), lambda j, n, k: (0, n)))],
                      (N_DEV, d // tn, 1), (fs, tn), _sds((N_DEV, fs, d), BF16),
                      pl.BlockSpec((None, fs, tn), lambda j, n, k: (j, 0, n)))

    def dw_up(name, dact):
        return _matmul(name, "nn",
                       [(u2_t, pl.BlockSpec((tn, s), lambda j, i, k: (i, 0)),
                         dact, pl.BlockSpec((None, s, fs), lambda j, i, k: (j, 0, 0)))],
                       (N_DEV, d // tn, 1), (tn, fs), _sds((N_DEV, d, fs), BF16),
                       pl.BlockSpec((None, tn, fs), lambda j, i, k: (j, i, 0)))

    dw_gate, dw_upw = dw_up("dw_gate", dgate), dw_up("dw_up", dup)
    rs_ffn = _scatter_start("scatter_ffn", [dw_gate, dw_upw, dw_down])
    a_spec = pl.BlockSpec((None, tm, fs), lambda i, j, k: (k, i, 0))
    b_spec = pl.BlockSpec((None, tn, fs), lambda i, j, k: (k, j, 0))
    du2 = _matmul("du2", "nt", [(dgate, a_spec, g_gate, b_spec), (dup, a_spec, g_up, b_spec)],
                  (s // tm, d // tn, N_DEV), (tm, tn), _sds((s, d), F32), pl.BlockSpec((tm, tn), lambda i, j, k: (i, j)),
                  dep=rs_ffn[4])
    dh1, dmix, dg_ffn_pre, dg_mix_post = _mid_norms_bwd(dy, du2, h1, mix, norm_ffn_pre, norm_mix_post)

    da_sb, da_fx, dgf = _merge_bwd(dmix, w_out_full, gf, a_sb, a_fx)
    dw_out = _mm_plain("dw_out", "nn", merged_t, dmix, BF16).reshape(N_DEV, cs, d)

    def branch_bwd(tag, da, w_b, o_t, width):
        tb = _tile(width, 1024)
        do = _matmul("do_" + tag, "nt",
                     [(da, pl.BlockSpec((tm, cs), lambda i, j, k: (i, k)),
                       w_b, pl.BlockSpec((None, tb, cs), lambda i, j, k: (k, j, 0)))],
                     (s // tm, width // tb, N_DEV), (tm, tb), _sds((s, width), BF16),
                     pl.BlockSpec((tm, tb), lambda i, j, k: (i, j)))
        dw = _matmul("dw_" + tag, "nn",
                     [(o_t, pl.BlockSpec((width, s), lambda j, i, k: (0, 0)),
                       da, pl.BlockSpec((s, cs), lambda j, i, k: (0, j)))],
                     (N_DEV, 1, 1), (width, cs), _sds((N_DEV, width, cs), BF16),
                     pl.BlockSpec((None, width, cs), lambda j, i, k: (j, 0, 0)))
        return do, dw

    do_sb, dw_sb = branch_bwd("sb", da_sb, g_sb, o_sb_t, d_sb)
    do_fx, dw_fx = branch_bwd("fox", da_fx, g_fx, o_fx_t, d_fox)

    rs_mid = _scatter_start("scatter_mid", [dw_sb, dw_fx, dw_out])

    dqkv = _sb_bwd(qkv, do_sb, tot, h_sb, rs_mid[4])
    dqkv, dcum = _fox_bwd(dqkv, qkv, do_fx, o_fx32, lse, cum_col, cum_row, h_fox, h_sb)
    dgf, db_part = _forget_bwd(dgf, dcum, gf, b_pad, f_blk)
    dw_in = _w_in_grad_parts(_mm_plain("dw_qkv", "nn", u_t, dqkv, BF16), _mm_plain("dw_gates", "nn", u_t, dgf, BF16), lay)
    rs_in = _scatter_start("scatter_in", [dw_in])
    du = _mm_plain("du_qkv", "nt", dqkv, w_cat, F32, tn=1024, dep=rs_in[4])
    du = _mm_plain("du_gates", "nt", dgf, w_cat, F32, tn=1024, k_off=n_qkv, init=du)
    dx, dg_mix_pre = _pre_norm_bwd(dh1, du, xs, norm_mix_pre)

    upd = {}

    def update_group(tag, rs, names, after):
        parts = _scatter_wait("scatter_" + tag + "_wait", *rs[:4], after=after)
        for nm, p in zip(names, parts):
            w, m, v = weights[nm]
            upd[nm] = _update("update_" + nm, p, w, m, v, layout=lay if nm == "w_in" else None)

    weights = dict(zip(("w_in", "w_branch_sb", "w_branch_fox", "w_out", "w_ffn_gate", "w_ffn_up", "w_ffn_down"),
                       zip(big, big_m, big_v)))
    update_group("ffn", rs_ffn, ("w_ffn_gate", "w_ffn_up", "w_ffn_down"), dx)
    update_group("mid", rs_mid, ("w_branch_sb", "w_branch_fox", "w_out"), upd["w_ffn_down"][0])
    update_group("in", rs_in, ("w_in",), upd["w_out"][0])

    small = ((norm_mix_pre, m_norm_mix_pre, v_norm_mix_pre), (norm_mix_post, m_norm_mix_post, v_norm_mix_post),
             (norm_ffn_pre, m_norm_ffn_pre, v_norm_ffn_pre), (norm_ffn_post, m_norm_ffn_post, v_norm_ffn_post))
    pad_f = ((0, 0), (0, LANES - n_f))
    cat = lambda i: jnp.concatenate([t[i] for t in small] + [jnp.pad((b_forget, m_b_forget, v_b_forget)[i], pad_f)], axis=1)
    sm = _small_update(jnp.concatenate([dg_mix_pre, dg_mix_post, dg_ffn_pre, dg_ffn_post, db_part], axis=1),
                       cat(0), cat(1), cat(2))
    for i, nm in enumerate(("norm_mix_pre", "norm_mix_post", "norm_ffn_pre", "norm_ffn_post")):
        upd[nm] = [o[:, i * d:(i + 1) * d] for o in sm]
    upd["b_forget"] = [o[:, 4 * d:4 * d + n_f] for o in sm]

    loss = lax.psum(loss_part[0, 0], ("x", "y", "c"))
    order = ("norm_mix_pre", "norm_mix_post", "w_in", "b_forget", "w_branch_sb", "w_branch_fox", "w_out",
             "norm_ffn_pre", "norm_ffn_post", "w_ffn_gate", "w_ffn_up", "w_ffn_down")
    return (loss, dx[None]) + tuple(upd[nm][i] for i in range(4) for nm in order)
```

```python
import jax
import jax.numpy as jnp
from jax import lax
from jax.experimental import pallas as pl
from jax.experimental.pallas import tpu as pltpu

F32 = jnp.float32
BF16 = jnp.bfloat16
MESH = pl.DeviceIdType.MESH
ANY = pl.BlockSpec(memory_space=pl.ANY)
HBM = pl.BlockSpec(memory_space=pltpu.HBM)
SEM = pl.BlockSpec(memory_space=pltpu.SEMAPHORE)
EFFECT = pltpu.SideEffectType.DATAFLOW_SIDE_EFFECTING

N_DEV = 8
HEAD_DIM = 128
RMS_EPS = 1e-6
F_PAD = 512
LANES = 128
ATT_TQ = 256
ATT_TK = 256
ATT_CH = 32
NEG_BIG = -1e30
VMEM_LIMIT = 56 * 1024 * 1024

ADAM_LR = 0.001
ADAM_B1 = 0.9
ADAM_B2 = 0.999
ADAM_EPS = 1e-08
ADAM_WD = 0.01
ADAM_STEP = 10

_DIMS = {"nn": ((1,), (0,)), "nt": ((1,), (1,)), "tn": ((0,), (0,))}


def _params(sem):
    return pltpu.CompilerParams(dimension_semantics=sem, vmem_limit_bytes=VMEM_LIMIT)


def _dot(a, b, mode="nn"):
    return lax.dot_general(a.astype(BF16), b.astype(BF16), (_DIMS[mode], ((), ())), preferred_element_type=F32)


def _tile(n, pref):
    if n <= pref:
        return n
    t = (pref // LANES) * LANES
    while n % t:
        t -= LANES
    return t


def _split2(v):
    hi = v.astype(BF16)
    return hi, (v - hi.astype(F32)).astype(BF16)


def _split3(v):
    a = v.astype(BF16)
    r = v - a.astype(F32)
    b = r.astype(BF16)
    return a, b, (r - b.astype(F32)).astype(BF16)


def _tri(n, cmp):
    r = lax.broadcasted_iota(jnp.int32, (n, n), 0)
    c = lax.broadcasted_iota(jnp.int32, (n, n), 1)
    return jnp.where(cmp(r, c), 1.0, 0.0).astype(BF16)


def _lane_pick(v, h):
    lane = lax.broadcasted_iota(jnp.int32, v.shape, 1)
    return jnp.sum(jnp.where(lane == h, v, 0.0), axis=1, keepdims=True)


def _lane_put(ref, rows, h, col):
    old = ref[rows, :]
    lane = lax.broadcasted_iota(jnp.int32, old.shape, 1)
    ref[rows, :] = jnp.where(lane == h, col, old)


def _sigmoid(z):
    return 1.0 / (1.0 + jnp.exp(-z))


def _log_sigmoid(z):
    return jnp.minimum(z, 0.0) - jnp.log(1.0 + jnp.exp(-jnp.abs(z)))


def _sds(shape, dtype):
    return jax.ShapeDtypeStruct(shape, dtype)


def _matmul(name, mode, pairs, grid, acc_shape, out_shape, out_specs, extras=(), epilogue=None, init=None, dep=None):
    n_p, n_e = len(pairs), len(extras)
    nk = grid[-1]
    single = not isinstance(out_shape, (list, tuple))
    n_i = 0 if init is None else 1
    n_d = 0 if dep is None else 1

    one_step = nk == 1 and init is None

    def body(*refs):
        ab = refs[:2 * n_p]
        ex = refs[2 * n_p:2 * n_p + n_e]
        ini = refs[2 * n_p + n_e:2 * n_p + n_e + n_i]
        outs = refs[2 * n_p + n_e + n_i + n_d:len(refs) - (0 if one_step else 1)]

        def finish(total):
            if epilogue is None:
                outs[0][...] = total.astype(outs[0].dtype)
            else:
                epilogue(total, ex, outs)

        t = _dot(ab[0][...], ab[1][...], mode)
        for p in range(1, n_p):
            t = t + _dot(ab[2 * p][...], ab[2 * p + 1][...], mode)
        if one_step:
            finish(t)
            return
        acc = refs[-1]
        k = pl.program_id(len(grid) - 1)

        @pl.when(k == 0)
        def _():
            acc[...] = t if init is None else ini[0][...].astype(F32) + t

        @pl.when(k > 0)
        def _():
            acc[...] += t

        @pl.when(k == nk - 1)
        def _():
            finish(acc[...])

    in_specs = [s for (_, sa, _, sb) in pairs for s in (sa, sb)] + [s for (_, s) in extras]
    args = [v for (a, _, b, _) in pairs for v in (a, b)] + [e for (e, _) in extras]
    if init is not None:
        in_specs.append(init[1])
        args.append(init[0])
    if dep is not None:
        in_specs.append(ANY)
        args.append(dep)
    return pl.pallas_call(
        body, name=name, grid=grid, in_specs=in_specs,
        out_specs=out_specs if single else list(out_specs),
        out_shape=out_shape if single else list(out_shape),
        scratch_shapes=[] if one_step else [pltpu.VMEM(acc_shape, F32)],
        compiler_params=_params(("parallel",) * (len(grid) - 1) + ("arbitrary",)),
    )(*args)


def _mm_plain(name, mode, a, b, out_dtype, *, n_off=0, n=None, k_off=0, tm=1024, tn=1536, tk=2048, init=None, dep=None):
    if mode == "nn":
        (m, kk), nn_ = a.shape, b.shape[1]
    elif mode == "nt":
        (m, kk), nn_ = a.shape, b.shape[0]
    else:
        (kk, m), nn_ = a.shape, b.shape[1]
    n = nn_ if n is None else n
    tm, tn, tk = _tile(m, tm), _tile(n, tn), _tile(kk, tk)
    while n_off % tn or n % tn:
        tn -= LANES
    while k_off % tk or kk % tk:
        tk -= LANES
    off, koff = n_off // tn, k_off // tk
    a_spec = {"nn": pl.BlockSpec((tm, tk), lambda i, j, k: (i, k)),
              "nt": pl.BlockSpec((tm, tk), lambda i, j, k: (i, k)),
              "tn": pl.BlockSpec((tk, tm), lambda i, j, k: (k, i))}[mode]
    b_spec = {"nn": pl.BlockSpec((tk, tn), lambda i, j, k: (k, j + off)),
              "nt": pl.BlockSpec((tn, tk), lambda i, j, k: (j, k + koff)),
              "tn": pl.BlockSpec((tk, tn), lambda i, j, k: (k, j))}[mode]
    o_spec = pl.BlockSpec((tm, tn), lambda i, j, k: (i, j))
    if init is not None:
        init = (init, o_spec)
    return _matmul(name, mode, [(a, a_spec, b, b_spec)], (m // tm, n // tn, kk // tk), (tm, tn),
                   _sds((m, n), out_dtype), o_spec, init=init, dep=dep)


def _rows_call(name, body, ins, outs, s, tr=256, dep=None):
    def spec(v, per_row):
        if per_row == "transposed":
            return pl.BlockSpec((v.shape[0], tr), lambda i: (0, i))
        if per_row:
            return pl.BlockSpec((tr, v.shape[1]), lambda i: (i, 0))
        return pl.BlockSpec(v.shape, lambda i: (0, 0))
    n_in = len(ins)
    deps = [] if dep is None else [dep]

    def with_dep(*refs):
        body(*refs[:n_in], *refs[n_in + len(deps):])

    return pl.pallas_call(
        with_dep, name=name, grid=(s // tr,),
        in_specs=[spec(v, p) for v, p in ins] + [ANY] * len(deps), out_specs=[spec(v, p) for v, p in outs],
        out_shape=[_sds(v.shape, v.dtype) for v, _ in outs],
        compiler_params=_params(("arbitrary",)),
    )(*[v for v, _ in ins], *deps)


def _rsq(v):
    return lax.rsqrt(jnp.mean(v * v, axis=-1, keepdims=True) + RMS_EPS)


def _norm_bwd(dy, v, r, g):
    vh = v * r
    t = dy * g
    dv = r * (t - vh * jnp.mean(t * vh, axis=-1, keepdims=True))
    return dv, jnp.sum(dy * vh, axis=0, keepdims=True)


def _accum(ref, val):
    @pl.when(pl.program_id(0) == 0)
    def _():
        ref[...] = jnp.zeros_like(ref)
    ref[...] += val


def _pre_norm(x, g, dep=None):
    def body(x_ref, g_ref, u_ref, ut_ref):
        v = x_ref[...]
        u = (v * _rsq(v) * g_ref[...]).astype(BF16)
        u_ref[...] = u
        ut_ref[...] = u.T
    s, d = x.shape
    return _rows_call("pre_norm", body, [(x, True), (g, False)],
                      [(_sds((s, d), BF16), True), (_sds((d, s), BF16), "transposed")], s, dep=dep)


def _mid_norms(x, mix, g_post, g_pre):
    def body(x_ref, mix_ref, gp_ref, gn_ref, h_ref, u_ref, ut_ref):
        mv = mix_ref[...]
        h = x_ref[...] + mv * _rsq(mv) * gp_ref[...]
        h_ref[...] = h
        u = (h * _rsq(h) * gn_ref[...]).astype(BF16)
        u_ref[...] = u
        ut_ref[...] = u.T
    s, d = x.shape
    return _rows_call("mid_norms", body, [(x, True), (mix, True), (g_post, False), (g_pre, False)],
                      [(_sds((s, d), F32), True), (_sds((s, d), BF16), True), (_sds((d, s), BF16), "transposed")], s)


def _loss_head(h1, ff, target, g):
    s, d = h1.shape

    def body(h_ref, ff_ref, t_ref, g_ref, loss_ref, dy_ref, dff_ref, dg_ref):
        fv = ff_ref[...]
        r = _rsq(fv)
        err = h_ref[...] + fv * r * g_ref[...] - t_ref[...]
        part = 0.5 * jnp.sum(jnp.mean(err * err, axis=-1, keepdims=True), axis=0, keepdims=True)
        _accum(loss_ref, jnp.broadcast_to(part, loss_ref.shape))
        dy = err * (1.0 / d)
        dy_ref[...] = dy
        dff, dg = _norm_bwd(dy, fv, r, g_ref[...])
        dff_ref[...] = dff.astype(BF16)
        _accum(dg_ref, dg)

    return _rows_call("loss_head", body, [(h1, True), (ff, True), (target, True), (g, False)],
                      [(_sds((1, LANES), F32), False), (_sds((s, d), F32), True),
                       (_sds((s, d), BF16), True), (_sds((1, d), F32), False)], s)


def _mid_norms_bwd(dy, du2, h1, mix, g_pre, g_post):
    s, d = dy.shape

    def body(dy_ref, du_ref, h_ref, mix_ref, gn_ref, gp_ref, dh_ref, dmix_ref, dgn_ref, dgp_ref):
        h = h_ref[...]
        dh, dgn = _norm_bwd(du_ref[...], h, _rsq(h), gn_ref[...])
        dh = dh + dy_ref[...]
        dh_ref[...] = dh
        _accum(dgn_ref, dgn)
        mv = mix_ref[...]
        dmix, dgp = _norm_bwd(dh, mv, _rsq(mv), gp_ref[...])
        dmix_ref[...] = dmix.astype(BF16)
        _accum(dgp_ref, dgp)

    return _rows_call("mid_norms_bwd", body,
                      [(dy, True), (du2, True), (h1, True), (mix, True), (g_pre, False), (g_post, False)],
                      [(_sds((s, d), F32), True), (_sds((s, d), BF16), True),
                       (_sds((1, d), F32), False), (_sds((1, d), F32), False)], s)


def _pre_norm_bwd(dh1, du, x, g, dep=None):
    s, d = x.shape

    def body(dh_ref, du_ref, x_ref, g_ref, dx_ref, dg_ref):
        v = x_ref[...]
        dv, dg = _norm_bwd(du_ref[...], v, _rsq(v), g_ref[...])
        dx_ref[...] = dh_ref[...] + dv
        _accum(dg_ref, dg)

    return _rows_call("pre_norm_bwd", body, [(dh1, True), (du, True), (x, True), (g, False)],
                      [(_sds((s, d), F32), True), (_sds((1, d), F32), False)], s, dep=dep)


def _forget_fwd(gf, b_pad, f_blk):
    s = gf.shape[0]
    tb = ATT_TK
    nb = s // tb

    def body(f_ref, b_ref, col_ref, row_ref):
        incl = _tri(tb, lambda r, c: c <= r)
        carry = jnp.zeros((1, LANES), F32)
        for i in range(nb):
            lf = _log_sigmoid(f_ref[pl.ds(i * tb, tb), :] + b_ref[...])
            parts = _split3(lf)
            cum = carry + _dot(incl, parts[0]) + _dot(incl, parts[1]) + _dot(incl, parts[2])
            col_ref[pl.ds(i * tb, tb), :] = cum
            row_ref[i] = cum.T
            carry = carry + jnp.sum(lf, axis=0, keepdims=True)

    return pl.pallas_call(
        body, name="forget_fwd", grid=(1,),
        in_specs=[pl.BlockSpec((s, LANES), lambda i: (0, f_blk)), pl.BlockSpec((1, LANES), lambda i: (0, 0))],
        out_specs=[pl.BlockSpec((s, LANES), lambda i: (0, 0)), pl.BlockSpec((nb, LANES, tb), lambda i: (0, 0, 0))],
        out_shape=[_sds((s, LANES), F32), _sds((nb, LANES, tb), F32)],
        compiler_params=_params(("arbitrary",)),
    )(gf, b_pad)


def _forget_bwd(dgf, dcum, gf, b_pad, f_blk):
    s = gf.shape[0]
    tb = ATT_TK
    nb = s // tb
    sec = dgf.shape[1] // F_PAD - 1

    def body(dgf_hbm, dc_ref, f_ref, b_ref, out_ref, db_ref):
        del dgf_hbm
        incl = _tri(tb, lambda r, c: c >= r)
        carry = jnp.zeros((1, LANES), F32)
        db = jnp.zeros((1, LANES), F32)
        out_ref[...] = jnp.zeros_like(out_ref)
        for i in reversed(range(nb)):
            dc = dc_ref[pl.ds(i * tb, tb), :]
            parts = _split3(dc)
            dlf = carry + _dot(incl, parts[0]) + _dot(incl, parts[1]) + _dot(incl, parts[2])
            z = f_ref[pl.ds(i * tb, tb), :] + b_ref[...]
            df = dlf * _sigmoid(-z)
            out_ref[pl.ds(i * tb, tb), pl.ds(0, LANES)] = df.astype(BF16)
            db = db + jnp.sum(df, axis=0, keepdims=True)
            carry = carry + jnp.sum(dc, axis=0, keepdims=True)
        db_ref[...] = db

    return pl.pallas_call(
        body, name="forget_bwd", grid=(1,),
        in_specs=[ANY, pl.BlockSpec((s, LANES), lambda i: (0, 0)),
                  pl.BlockSpec((s, LANES), lambda i: (0, f_blk)), pl.BlockSpec((1, LANES), lambda i: (0, 0))],
        out_specs=[pl.BlockSpec((s, F_PAD), lambda i: (0, sec)), pl.BlockSpec((1, LANES), lambda i: (0, 0))],
        out_shape=[_sds(dgf.shape, BF16), _sds((1, LANES), F32)],
        input_output_aliases={0: 0},
        compiler_params=_params(("arbitrary",)),
    )(dgf, dcum, gf, b_pad)


def _row_chunks(fn):
    def body(c, carry):
        fn(pl.ds(pl.multiple_of(c * ATT_CH, ATT_CH), ATT_CH))
        return carry
    lax.fori_loop(0, ATT_TQ // ATT_CH, body, 0)


def _diag_mask(rows, strict):
    r = lax.broadcasted_iota(jnp.int32, (ATT_CH, ATT_TK), 0) + rows.start
    c = lax.broadcasted_iota(jnp.int32, (ATT_CH, ATT_TK), 1)
    return c < r if strict else c <= r


def _tile_scratch(dtype):
    return pltpu.VMEM((ATT_TQ, ATT_TK), dtype)


def _col_scratch():
    return pltpu.VMEM((ATT_TQ, 1), F32)


def _qkv_specs(hb0, s):
    return [pl.BlockSpec((ATT_TQ, HEAD_DIM), lambda h, i: (i, 3 * (hb0 + h))),
            pl.BlockSpec((s, HEAD_DIM), lambda h, i: (0, 3 * (hb0 + h) + 1)),
            pl.BlockSpec((s, HEAD_DIM), lambda h, i: (0, 3 * (hb0 + h) + 2))]


def _sb_fwd(qkv, n_heads):
    s = qkv.shape[0]
    scale = HEAD_DIM ** -0.5
    tq, tk = ATT_TQ, ATT_TK

    def body(q_ref, k_ref, v_ref, o_ref, ot_ref, tot_ref, z_s, p_s, hi_s, lo_s, w_s, c_s, rs_s, acc_s):
        h, i = pl.program_id(0), pl.program_id(1)

        @pl.when((h == 0) & (i == 0))
        def _():
            tot_ref[...] = jnp.zeros_like(tot_ref)

        q = q_ref[...]
        upper = _tri(tk, lambda r, c: r > c)
        c_s[...] = jnp.zeros_like(c_s)
        acc_s[...] = jnp.zeros_like(acc_s)

        def tile(kj, masked):
            keys = pl.ds(pl.multiple_of(kj * tk, tk), tk)
            z_s[...] = _dot(q, k_ref[keys, :], "nt")

            def log_keep(r):
                z = z_s[r, :] * scale
                lsz = _log_sigmoid(z)
                lk = lsz - z
                if masked:
                    lk = jnp.where(_diag_mask(r, True), lk, 0.0)
                z_s[r, :] = lsz
                hi_s[r, :], lo_s[r, :] = _split2(lk)
                rs_s[r, :] = jnp.sum(lk, axis=1, keepdims=True)

            _row_chunks(log_keep)
            p_s[...] = _dot(hi_s[...], upper) + _dot(lo_s[...], upper)

            def weights(r):
                w = jnp.exp(z_s[r, :] + p_s[r, :] + c_s[r, :])
                if masked:
                    w = jnp.where(_diag_mask(r, True), w, 0.0)
                w_s[r, :] = w.astype(BF16)

            _row_chunks(weights)
            acc_s[...] += _dot(w_s[...], v_ref[keys, :])
            c_s[...] += rs_s[...]

        tile(i, True)

        def below(n, carry):
            tile(i - 1 - n, False)
            return carry

        lax.fori_loop(0, i, below, 0)
        o = acc_s[...].astype(BF16)
        o_ref[...] = o
        ot_ref[...] = o.T
        _lane_put(tot_ref, pl.ds(pl.multiple_of(i * tq, tq), tq), h, c_s[...])

    return pl.pallas_call(
        body, name="sb_fwd", grid=(n_heads, s // tq),
        in_specs=_qkv_specs(0, s),
        out_specs=[pl.BlockSpec((tq, HEAD_DIM), lambda h, i: (i, h)), pl.BlockSpec((HEAD_DIM, tq), lambda h, i: (h, i)),
                   pl.BlockSpec((s, LANES), lambda h, i: (0, 0))],
        out_shape=[_sds((s, n_heads * HEAD_DIM), BF16), _sds((n_heads * HEAD_DIM, s), BF16), _sds((s, LANES), F32)],
        scratch_shapes=[_tile_scratch(F32), _tile_scratch(F32), _tile_scratch(BF16), _tile_scratch(BF16), _tile_scratch(BF16),
                        _col_scratch(), _col_scratch(), pltpu.VMEM((tq, HEAD_DIM), F32)],
        compiler_params=_params(("arbitrary", "arbitrary")),
    )(qkv, qkv, qkv)


def _sb_bwd(qkv, do, tot, n_heads, dep):
    s = qkv.shape[0]
    scale = HEAD_DIM ** -0.5
    tq, tk = ATT_TQ, ATT_TK
    nq = s // tq
    hd = HEAD_DIM

    def body(q_ref, k_ref, v_ref, do_ref, tot_ref, dep_ref, out_ref, dk_acc, dv_acc,
             z_s, d_s, p_s, hi_s, lo_s, w_s, tot_s, pl_s, pe_s, rl_s, re_s, dq_s):
        del dep_ref
        h, i = pl.program_id(0), pl.program_id(1)

        @pl.when(i == 0)
        def _():
            dk_acc[...] = jnp.zeros_like(dk_acc)
            dv_acc[...] = jnp.zeros_like(dv_acc)

        q = q_ref[...]
        dout = do_ref[...]
        incl = _tri(tk, lambda r, c: r <= c)
        excl = _tri(tk, lambda r, c: r < c)
        tot_s[...] = _lane_pick(tot_ref[...], h)
        pl_s[...] = jnp.zeros_like(pl_s)
        pe_s[...] = jnp.zeros_like(pe_s)
        dq_s[...] = jnp.zeros_like(dq_s)

        def tile(kj, masked):
            keys = pl.ds(pl.multiple_of(kj * tk, tk), tk)
            k_t = k_ref[keys, :]
            z_s[...] = _dot(q, k_t, "nt")
            d_s[...] = _dot(dout, v_ref[keys, :], "nt")

            def log_keep(r):
                z = z_s[r, :] * scale
                lsz = _log_sigmoid(z)
                lk = lsz - z
                if masked:
                    lk = jnp.where(_diag_mask(r, True), lk, 0.0)
                z_s[r, :] = lsz
                hi_s[r, :], lo_s[r, :] = _split2(lk)
                rl_s[r, :] = jnp.sum(lk, axis=1, keepdims=True)

            _row_chunks(log_keep)
            p_s[...] = _dot(hi_s[...], incl) + _dot(lo_s[...], incl)

            def weights(r):
                between = tot_s[r, :] - (pl_s[r, :] + p_s[r, :])
                w = jnp.exp(z_s[r, :] + between)
                if masked:
                    w = jnp.where(_diag_mask(r, True), w, 0.0)
                e = d_s[r, :] * w
                w_s[r, :] = w.astype(BF16)
                d_s[r, :] = e
                hi_s[r, :], lo_s[r, :] = _split2(e)
                re_s[r, :] = jnp.sum(e, axis=1, keepdims=True)

            _row_chunks(weights)
            p_s[...] = _dot(hi_s[...], excl) + _dot(lo_s[...], excl)

            def score_grad(r):
                sg = jnp.exp(z_s[r, :])
                dz = d_s[r, :] * (1.0 - sg) - (pe_s[r, :] + p_s[r, :]) * sg
                if masked:
                    dz = jnp.where(_diag_mask(r, True), dz, 0.0)
                hi_s[r, :] = (dz * scale).astype(BF16)

            _row_chunks(score_grad)
            dz = hi_s[...]
            dq_s[...] += _dot(dz, k_t)
            dk_acc[keys, :] += _dot(dz, q, "tn")
            dv_acc[keys, :] += _dot(w_s[...], dout, "tn")
            pl_s[...] += rl_s[...]
            pe_s[...] += re_s[...]

        def below(kj, carry):
            tile(kj, False)
            return carry

        lax.fori_loop(0, i, below, 0)
        tile(i, True)
        out_ref[pl.ds(pl.multiple_of(i * tq, tq), tq), pl.ds(0, hd)] = dq_s[...].astype(BF16)

        @pl.when(i == nq - 1)
        def _():
            out_ref[:, pl.ds(hd, hd)] = dk_acc[...].astype(BF16)
            out_ref[:, pl.ds(2 * hd, hd)] = dv_acc[...].astype(BF16)

    return pl.pallas_call(
        body, name="sb_bwd", grid=(n_heads, nq),
        in_specs=_qkv_specs(0, s) + [pl.BlockSpec((tq, hd), lambda h, i: (i, h)),
                                     pl.BlockSpec((tq, LANES), lambda h, i: (i, 0)), ANY],
        out_specs=pl.BlockSpec((s, 3 * hd), lambda h, i: (0, h)),
        out_shape=_sds(qkv.shape, BF16),
        scratch_shapes=[pltpu.VMEM((s, hd), F32), pltpu.VMEM((s, hd), F32),
                        _tile_scratch(F32), _tile_scratch(F32), _tile_scratch(F32), _tile_scratch(BF16), _tile_scratch(BF16),
                        _tile_scratch(BF16), _col_scratch(), _col_scratch(), _col_scratch(), _col_scratch(), _col_scratch(),
                        pltpu.VMEM((tq, hd), F32)],
        compiler_params=_params(("arbitrary", "arbitrary")),
    )(qkv, qkv, qkv, do, tot, dep)


def _fox_fwd(qkv, cum_col, cum_row, n_heads, hb0, dep):
    s = qkv.shape[0]
    scale = HEAD_DIM ** -0.5
    tq, tk = ATT_TQ, ATT_TK

    def body(q_ref, k_ref, v_ref, cc_ref, cr_ref, dep_ref, o_ref, ot_ref, o32_ref, lse_ref,
             s_s, hi_s, lo_s, cq_s, m_s, l_s, a_s, acc_s):
        del dep_ref
        h, i = pl.program_id(0), pl.program_id(1)

        @pl.when((h == 0) & (i == 0))
        def _():
            lse_ref[...] = jnp.zeros_like(lse_ref)

        q = q_ref[...]
        cq_s[...] = _lane_pick(cc_ref[...], h)
        m_s[...] = jnp.full(m_s.shape, NEG_BIG, F32)
        l_s[...] = jnp.zeros_like(l_s)
        acc_s[...] = jnp.zeros_like(acc_s)

        def tile(kj, masked):
            keys = pl.ds(pl.multiple_of(kj * tk, tk), tk)
            ck = cr_ref[kj, pl.ds(h, 1), :]
            s_s[...] = _dot(q, k_ref[keys, :], "nt")

            def softmax_step(r):
                sc = s_s[r, :] * scale + cq_s[r, :] - ck
                if masked:
                    sc = jnp.where(_diag_mask(r, False), sc, NEG_BIG)
                m_old = m_s[r, :]
                m_new = jnp.maximum(m_old, jnp.max(sc, axis=1, keepdims=True))
                p = jnp.exp(sc - m_new)
                alpha = jnp.exp(m_old - m_new)
                m_s[r, :] = m_new
                a_s[r, :] = alpha
                l_s[r, :] = alpha * l_s[r, :] + jnp.sum(p, axis=1, keepdims=True)
                hi_s[r, :], lo_s[r, :] = _split2(p)

            _row_chunks(softmax_step)
            v_t = v_ref[keys, :]
            acc_s[...] = a_s[...] * acc_s[...] + _dot(hi_s[...], v_t) + _dot(lo_s[...], v_t)

        def below(kj, carry):
            tile(kj, False)
            return carry

        lax.fori_loop(0, i, below, 0)
        tile(i, True)
        o = acc_s[...] / l_s[...]
        o_ref[...] = o.astype(BF16)
        ot_ref[...] = o.astype(BF16).T
        o32_ref[...] = o
        _lane_put(lse_ref, pl.ds(pl.multiple_of(i * tq, tq), tq), h, m_s[...] + jnp.log(l_s[...]))

    nb = cum_row.shape[0]
    return pl.pallas_call(
        body, name="fox_fwd", grid=(n_heads, s // tq),
        in_specs=_qkv_specs(hb0, s) + [pl.BlockSpec((tq, LANES), lambda h, i: (i, 0)),
                                       pl.BlockSpec((nb, 8, tk), lambda h, i: (0, 0, 0)), ANY],
        out_specs=[pl.BlockSpec((tq, HEAD_DIM), lambda h, i: (i, h)), pl.BlockSpec((HEAD_DIM, tq), lambda h, i: (h, i)),
                   pl.BlockSpec((tq, HEAD_DIM), lambda h, i: (i, h)), pl.BlockSpec((s, LANES), lambda h, i: (0, 0))],
        out_shape=[_sds((s, n_heads * HEAD_DIM), BF16), _sds((n_heads * HEAD_DIM, s), BF16),
                   _sds((s, n_heads * HEAD_DIM), F32), _sds((s, LANES), F32)],
        scratch_shapes=[_tile_scratch(F32), _tile_scratch(BF16), _tile_scratch(BF16), _col_scratch(), _col_scratch(),
                        _col_scratch(), _col_scratch(), pltpu.VMEM((tq, HEAD_DIM), F32)],
        compiler_params=_params(("arbitrary", "arbitrary")),
    )(qkv, qkv, qkv, cum_col, cum_row, dep)


def _fox_bwd(dqkv, qkv, do, o, lse, cum_col, cum_row, n_heads, hb0):
    s = qkv.shape[0]
    scale = HEAD_DIM ** -0.5
    tq, tk = ATT_TQ, ATT_TK
    nq = s // tq
    hd = HEAD_DIM

    def body(dqkv_hbm, q_ref, k_ref, v_ref, do_ref, o_ref, lse_ref, cc_ref, cr_ref, out_ref, dc_ref,
             dk_acc, dv_acc, col_acc, s_s, d_s, p_s, ds_s, cq_s, lse_s, delta_s, row_s, dq_s):
        del dqkv_hbm
        h, i = pl.program_id(0), pl.program_id(1)

        @pl.when((h == 0) & (i == 0))
        def _():
            dc_ref[...] = jnp.zeros_like(dc_ref)

        @pl.when(i == 0)
        def _():
            dk_acc[...] = jnp.zeros_like(dk_acc)
            dv_acc[...] = jnp.zeros_like(dv_acc)
            col_acc[...] = jnp.zeros_like(col_acc)

        q = q_ref[...]
        dout = do_ref[...]
        delta_s[...] = jnp.sum(dout.astype(F32) * o_ref[...], axis=1, keepdims=True)
        lse_s[...] = _lane_pick(lse_ref[...], h)
        cq_s[...] = _lane_pick(cc_ref[...], h)
        row_s[...] = jnp.zeros_like(row_s)
        dq_s[...] = jnp.zeros_like(dq_s)

        def tile(kj, masked):
            keys = pl.ds(pl.multiple_of(kj * tk, tk), tk)
            k_t = k_ref[keys, :]
            ck = cr_ref[kj, pl.ds(h, 1), :]
            s_s[...] = _dot(q, k_t, "nt")
            d_s[...] = _dot(dout, v_ref[keys, :], "nt")

            def score_grad(r):
                p = jnp.exp(s_s[r, :] * scale + cq_s[r, :] - ck - lse_s[r, :])
                if masked:
                    p = jnp.where(_diag_mask(r, False), p, 0.0)
                ds_f = p * (d_s[r, :] - delta_s[r, :])
                col_acc[kj] += jnp.broadcast_to(jnp.sum(ds_f, axis=0, keepdims=True), (8, tk))
                row_s[r, :] += jnp.sum(ds_f, axis=1, keepdims=True)
                p_s[r, :] = p.astype(BF16)
                ds_s[r, :] = (ds_f * scale).astype(BF16)

            _row_chunks(score_grad)
            ds = ds_s[...]
            dk_acc[keys, :] += _dot(ds, q, "tn")
            dv_acc[keys, :] += _dot(p_s[...], dout, "tn")
            dq_s[...] += _dot(ds, k_t)

        def below(kj, carry):
            tile(kj, False)
            return carry

        lax.fori_loop(0, i, below, 0)
        tile(i, True)
        q_rows = pl.ds(pl.multiple_of(i * tq, tq), tq)
        out_ref[q_rows, pl.ds(0, hd)] = dq_s[...].astype(BF16)
        _lane_put(dc_ref, q_rows, h, row_s[...])

        @pl.when(i == nq - 1)
        def _():
            out_ref[:, pl.ds(hd, hd)] = dk_acc[...].astype(BF16)
            out_ref[:, pl.ds(2 * hd, hd)] = dv_acc[...].astype(BF16)
            lane = lax.broadcasted_iota(jnp.int32, (tk, LANES), 1)
            for kj in range(nb):
                col = jnp.broadcast_to(col_acc[kj][0:1, :], (LANES, tk)).T
                old = dc_ref[pl.ds(kj * tk, tk), :]
                dc_ref[pl.ds(kj * tk, tk), :] = jnp.where(lane == h, old - col, old)

    nb = cum_row.shape[0]
    return pl.pallas_call(
        body, name="fox_bwd", grid=(n_heads, nq),
        in_specs=[ANY] + _qkv_specs(hb0, s) + [
            pl.BlockSpec((tq, hd), lambda h, i: (i, h)), pl.BlockSpec((tq, hd), lambda h, i: (i, h)),
            pl.BlockSpec((tq, LANES), lambda h, i: (i, 0)), pl.BlockSpec((tq, LANES), lambda h, i: (i, 0)),
            pl.BlockSpec((nb, 8, tk), lambda h, i: (0, 0, 0))],
        out_specs=[pl.BlockSpec((s, 3 * hd), lambda h, i: (0, hb0 + h)), pl.BlockSpec((s, LANES), lambda h, i: (0, 0))],
        out_shape=[_sds(dqkv.shape, BF16), _sds((s, LANES), F32)],
        scratch_shapes=[pltpu.VMEM((s, hd), F32), pltpu.VMEM((s, hd), F32), pltpu.VMEM((s // tk, 8, tk), F32),
                        _tile_scratch(F32), _tile_scratch(F32), _tile_scratch(BF16), _tile_scratch(BF16),
                        _col_scratch(), _col_scratch(), _col_scratch(), _col_scratch(), pltpu.VMEM((tq, hd), F32)],
        input_output_aliases={0: 0},
        compiler_params=_params(("arbitrary", "arbitrary")),
    )(dqkv, qkv, qkv, qkv, do, o, lse, cum_col, cum_row)


def _branch_merge(o_sb, o_fx, w_sb, w_fx, gf, dep, tm=1024):
    s = o_sb.shape[0]
    cs = w_sb.shape[2]
    tm = _tile(s, tm)

    def body(osb_ref, ofx_ref, wsb_ref, wfx_ref, g_ref, dep_ref, merged_ref, mt_ref, asb_ref, afx_ref):
        del dep_ref
        a_sb = _dot(osb_ref[...], wsb_ref[...])
        a_fx = _dot(ofx_ref[...], wfx_ref[...])
        g = g_ref[...]
        merged = (_sigmoid(g[:, :cs]) * a_sb + _sigmoid(g[:, cs:]) * a_fx).astype(BF16)
        merged_ref[...] = merged
        mt_ref[...] = merged.T
        asb_ref[...] = a_sb.astype(BF16)
        afx_ref[...] = a_fx.astype(BF16)

    blk = pl.BlockSpec((tm, cs), lambda i, j: (i, j))
    out = _sds((s, N_DEV * cs), BF16)
    return pl.pallas_call(
        body, name="branch_merge", grid=(s // tm, N_DEV),
        in_specs=[pl.BlockSpec((tm, o_sb.shape[1]), lambda i, j: (i, 0)),
                  pl.BlockSpec((tm, o_fx.shape[1]), lambda i, j: (i, 0)),
                  pl.BlockSpec((None,) + w_sb.shape[1:], lambda i, j: (j, 0, 0)),
                  pl.BlockSpec((None,) + w_fx.shape[1:], lambda i, j: (j, 0, 0)),
                  pl.BlockSpec((tm, 2 * cs), lambda i, j: (i, j)), ANY],
        out_specs=[blk, pl.BlockSpec((cs, tm), lambda i, j: (j, i)), blk, blk],
        out_shape=[out, _sds((N_DEV * cs, s), BF16), out, out],
        compiler_params=_params(("parallel", "arbitrary")),
    )(o_sb, o_fx, w_sb, w_fx, gf, dep)


def _merge_bwd(dmix, w_out, gf, a_sb, a_fx, tm=1024, tk=2048):
    s, d = dmix.shape
    cs = d // N_DEV
    tm, tk = _tile(s, tm), _tile(d, tk)

    def epilogue(acc, ex, outs):
        g, a_sb, a_fx = ex[0][...], ex[1][...].astype(F32), ex[2][...].astype(F32)
        s_sb, s_fx = _sigmoid(g[:, :cs]), _sigmoid(g[:, cs:])
        outs[0][...] = (acc * s_sb).astype(BF16)
        outs[1][...] = (acc * s_fx).astype(BF16)
        outs[2][...] = jnp.concatenate([acc * a_sb * s_sb * (1.0 - s_sb), acc * a_fx * s_fx * (1.0 - s_fx)],
                                       axis=1).astype(BF16)

    blk = pl.BlockSpec((tm, cs), lambda i, j, k: (i, j))
    wide = pl.BlockSpec((tm, 2 * cs), lambda i, j, k: (i, j))
    return _matmul(
        "merge_bwd", "nt",
        [(dmix, pl.BlockSpec((tm, tk), lambda i, j, k: (i, k)), w_out, pl.BlockSpec((cs, tk), lambda i, j, k: (j, k)))],
        (s // tm, N_DEV, d // tk), (tm, cs),
        [_sds((s, d), BF16), _sds((s, d), BF16), _sds(gf.shape, BF16)], [blk, blk, wide],
        extras=[(gf, wide), (a_sb, blk), (a_fx, blk)], epilogue=epilogue)


def _ffn_up(u2, w_gate, w_up, tm=1024):
    s, d = u2.shape
    fs = w_gate.shape[2]
    tm = _tile(s, tm)

    def body(u_ref, wg_ref, wu_ref, gate_ref, up_ref, act_ref, actt_ref):
        u = u_ref[...]
        gate = _dot(u, wg_ref[...])
        up = _dot(u, wu_ref[...])
        gate_ref[...] = gate
        up_ref[...] = up
        act = (gate * _sigmoid(gate) * up).astype(BF16)
        act_ref[...] = act
        actt_ref[...] = act.T

    w_spec = pl.BlockSpec((None, d, fs), lambda i, j: (j, 0, 0))
    o_spec = pl.BlockSpec((None, tm, fs), lambda i, j: (j, i, 0))
    return pl.pallas_call(
        body, name="ffn_up", grid=(s // tm, N_DEV),
        in_specs=[pl.BlockSpec((tm, d), lambda i, j: (i, 0)), w_spec, w_spec],
        out_specs=[o_spec, o_spec, o_spec, pl.BlockSpec((None, fs, tm), lambda i, j: (j, 0, i))],
        out_shape=[_sds((N_DEV, s, fs), F32), _sds((N_DEV, s, fs), F32), _sds((N_DEV, s, fs), BF16),
                   _sds((N_DEV, fs, s), BF16)],
        compiler_params=_params(("parallel", "arbitrary")),
    )(u2, w_gate, w_up)


def _ffn_down_bwd(dff, w_down, gate, up, tm=1024):
    s, d = dff.shape
    fs = w_down.shape[1]
    tm = _tile(s, tm)

    def body(dff_ref, wd_ref, gate_ref, up_ref, dgate_ref, dup_ref):
        dact = _dot(dff_ref[...], wd_ref[...], "nt")
        gate = gate_ref[...]
        sg = _sigmoid(gate)
        dup_ref[...] = (dact * gate * sg).astype(BF16)
        dgate_ref[...] = (dact * up_ref[...] * sg * (1.0 + gate * (1.0 - sg))).astype(BF16)

    a_spec = pl.BlockSpec((None, tm, fs), lambda i, j: (j, i, 0))
    return pl.pallas_call(
        body, name="ffn_down_bwd", grid=(s // tm, N_DEV),
        in_specs=[pl.BlockSpec((tm, d), lambda i, j: (i, 0)), pl.BlockSpec((None, fs, d), lambda i, j: (j, 0, 0)),
                  a_spec, a_spec],
        out_specs=[a_spec, a_spec],
        out_shape=[_sds((N_DEV, s, fs), BF16), _sds((N_DEV, s, fs), BF16)],
        compiler_params=_params(("parallel", "arbitrary")),
    )(dff, w_down, gate, up)


def _mesh_place():
    x, y, c = lax.axis_index("x"), lax.axis_index("y"), lax.axis_index("c")
    peers = []
    for d in range(1, N_DEV):
        px = 1 - x if d & 4 else x
        py = 1 - y if d & 2 else y
        pc = 1 - c if d & 1 else c
        peers.append((d, (px, py, pc), 4 * px + 2 * py + pc))
    return 4 * x + 2 * y + c, peers


def _flat_me():
    return 4 * lax.axis_index("x") + 2 * lax.axis_index("y") + lax.axis_index("c")


def _in_hbm(a):
    return pltpu.with_memory_space_constraint(a, pltpu.HBM)


def _scatter_start(name, parts):
    n = len(parts)
    me = _flat_me()
    lands = [lax.dynamic_update_slice_in_dim(lax.empty(a.shape, a.dtype), lax.dynamic_slice_in_dim(a, me, 1, 0), me, 0)
             for a in parts]

    def body(*refs):
        ins, lnd = refs[:n], refs[n:2 * n]
        send, recv = refs[2 * n], refs[2 * n + 1]
        token = refs[-1]
        mine, peers = _mesh_place()
        for a in range(n):
            for d, dev, flat in peers:
                pltpu.make_async_remote_copy(src_ref=ins[a].at[flat], dst_ref=lnd[a].at[mine], send_sem=send.at[a * N_DEV + d],
                                             recv_sem=recv.at[a * N_DEV + d], device_id=dev, device_id_type=MESH).start()
        token[...] = jnp.zeros_like(token)

    res = pl.pallas_call(
        body, name=name,
        out_shape=[pltpu.SemaphoreType.DMA((n * N_DEV,)), pltpu.SemaphoreType.DMA((n * N_DEV,))]
        + [pltpu.HBM(a.shape, a.dtype) for a in parts] * 2 + [_sds((8, LANES), F32)],
        in_specs=[HBM] * (2 * n), out_specs=[SEM, SEM] + [HBM] * (2 * n) + [pl.BlockSpec(memory_space=pltpu.VMEM)],
        input_output_aliases={i: 2 + i for i in range(2 * n)},
        compiler_params=pltpu.CompilerParams(has_side_effects=EFFECT),
    )(*[_in_hbm(a) for a in parts], *[_in_hbm(a) for a in lands])
    return res[0], res[1], res[2:2 + n], res[2 + n:2 + 2 * n], res[-1]


def _scatter_wait(name, send, recv, parts, lands, after):
    n = len(parts)

    def body(*refs):
        ins, lnd = refs[:n], refs[n:2 * n]
        send_sem, recv_sem = refs[2 * n], refs[2 * n + 1]
        mine, peers = _mesh_place()
        for a in range(n):
            for d, dev, flat in peers:
                cp = pltpu.make_async_remote_copy(src_ref=ins[a].at[flat], dst_ref=lnd[a].at[flat],
                                                  send_sem=send_sem.at[a * N_DEV + d], recv_sem=recv_sem.at[a * N_DEV + d],
                                                  device_id=dev, device_id_type=MESH)
                cp.wait_send()
                cp.wait_recv()

    res = pl.pallas_call(
        body, name=name,
        out_shape=[pltpu.HBM(a.shape, a.dtype) for a in parts] * 2,
        in_specs=[HBM] * (2 * n) + [SEM, SEM, ANY], out_specs=[HBM] * (2 * n),
        input_output_aliases={i: i for i in range(2 * n)},
        compiler_params=pltpu.CompilerParams(has_side_effects=EFFECT),
    )(*parts, *lands, send, recv, after)
    return res[n:]


def _gather_targets():
    x, y, c = lax.axis_index("x"), lax.axis_index("y"), lax.axis_index("c")
    chips = [(x, y), (1 - x, y), (x, 1 - y), (1 - x, 1 - y)]
    same = [((cx, cy, c), 4 * cx + 2 * cy + c) for cx, cy in chips]
    other = [((cx, cy, 1 - c), 4 * cx + 2 * cy + 1 - c) for cx, cy in chips]
    return same[0][1], [other[0]] + same[1:], [flat for _, flat in other[1:]], other[0][0]


def _gather_start(shards):
    n = len(shards)
    me = _flat_me()
    lands = [lax.dynamic_update_slice_in_dim(lax.empty((N_DEV,) + a.shape, a.dtype), a[None], me, 0) for a in shards]

    def body(*refs):
        lnd, send, recv, token = refs[:n], refs[n], refs[n + 1], refs[-1]
        mine, targets, _, _ = _gather_targets()
        for a in range(n):
            for t, (dev, _) in enumerate(targets):
                pltpu.make_async_remote_copy(src_ref=lnd[a].at[mine], dst_ref=lnd[a].at[mine], send_sem=send.at[4 * a + t],
                                             recv_sem=recv.at[4 * a + t], device_id=dev, device_id_type=MESH).start()
        token[...] = jnp.zeros_like(token)

    res = pl.pallas_call(
        body, name="gather_start",
        out_shape=[pltpu.SemaphoreType.DMA((4 * n,)), pltpu.SemaphoreType.DMA((4 * n,))]
        + [pltpu.HBM(a.shape, a.dtype) for a in lands] + [_sds((8, LANES), F32)],
        in_specs=[HBM] * n, out_specs=[SEM, SEM] + [HBM] * n + [pl.BlockSpec(memory_space=pltpu.VMEM)],
        input_output_aliases={i: 2 + i for i in range(n)},
        compiler_params=pltpu.CompilerParams(has_side_effects=EFFECT),
    )(*[_in_hbm(a) for a in lands])
    return res[0], res[1], list(res[2:2 + n]), res[-1]


def _gather_forward(name, lands, first, send, recv, after):
    n = len(lands)

    def body(*refs):
        lnd, send_sem, recv_sem = refs[:n], refs[n], refs[n + 1]
        send2, recv2, token = refs[-3], refs[-2], refs[-1]
        mine, targets, _, sibling = _gather_targets()
        for a in range(n):
            for t, (dev, flat) in enumerate(targets):
                cp = pltpu.make_async_remote_copy(src_ref=lnd[a].at[mine], dst_ref=lnd[a].at[flat],
                                                  send_sem=send_sem.at[4 * (first + a) + t],
                                                  recv_sem=recv_sem.at[4 * (first + a) + t], device_id=dev, device_id_type=MESH)
                cp.wait_send()
                if t:
                    cp.wait_recv()
                    pltpu.make_async_remote_copy(src_ref=lnd[a].at[flat], dst_ref=lnd[a].at[flat], send_sem=send2.at[3 * a + t - 1],
                                                 recv_sem=recv2.at[3 * a + t - 1], device_id=sibling, device_id_type=MESH).start()
        token[...] = jnp.zeros_like(token)

    res = pl.pallas_call(
        body, name=name,
        out_shape=[pltpu.HBM(a.shape, a.dtype) for a in lands]
        + [pltpu.SemaphoreType.DMA((3 * n,)), pltpu.SemaphoreType.DMA((3 * n,)), _sds((8, LANES), F32)],
        in_specs=[HBM] * n + [SEM, SEM, ANY], out_specs=[HBM] * n + [SEM, SEM, pl.BlockSpec(memory_space=pltpu.VMEM)],
        input_output_aliases={i: i for i in range(n)},
        compiler_params=pltpu.CompilerParams(has_side_effects=EFFECT),
    )(*lands, send, recv, after)
    return list(res[:n]), res[n], res[n + 1], res[-1]


def _gather_wait(name, lands, first, recv, send2, recv2, after):
    n = len(lands)

    def body(*refs):
        lnd, recv_sem, send2_sem, recv2_sem = refs[:n], refs[n], refs[n + 1], refs[n + 2]
        mine, targets, passed, sibling = _gather_targets()
        for a in range(n):
            dev, flat = targets[0]
            pltpu.make_async_remote_copy(src_ref=lnd[a].at[mine], dst_ref=lnd[a].at[flat], send_sem=send2_sem.at[3 * a],
                                         recv_sem=recv_sem.at[4 * (first + a)], device_id=dev, device_id_type=MESH).wait_recv()
            for t in range(3):
                cp = pltpu.make_async_remote_copy(src_ref=lnd[a].at[targets[t + 1][1]], dst_ref=lnd[a].at[passed[t]],
                                                  send_sem=send2_sem.at[3 * a + t], recv_sem=recv2_sem.at[3 * a + t],
                                                  device_id=sibling, device_id_type=MESH)
                cp.wait_send()
                cp.wait_recv()

    res = pl.pallas_call(
        body, name=name, out_shape=[pltpu.HBM(a.shape, a.dtype) for a in lands],
        in_specs=[HBM] * n + [SEM, SEM, SEM, ANY], out_specs=[HBM] * n,
        input_output_aliases={i: i for i in range(n)},
        compiler_params=pltpu.CompilerParams(has_side_effects=EFFECT),
    )(*lands, recv, send2, recv2, after)
    return list(res)


def _adamw(g, w, m, v):
    m = ADAM_B1 * m + (1.0 - ADAM_B1) * g
    v = ADAM_B2 * v + (1.0 - ADAM_B2) * (g * g)
    m_hat = m / (1.0 - ADAM_B1 ** ADAM_STEP)
    v_hat = v / (1.0 - ADAM_B2 ** ADAM_STEP)
    delta = -ADAM_LR * (m_hat / (jnp.sqrt(v_hat) + ADAM_EPS) + ADAM_WD * w)
    return delta, m, v


def _update(name, parts, w, m, v, layout=None, block_bytes=1 << 20):
    _, r, c = w.shape
    cp = parts.shape[2]
    tr = max(8, min(r, (block_bytes // (4 * cp)) // 8 * 8))
    while r % tr:
        tr -= 8

    def body(p_ref, w_ref, m_ref, v_ref, g_ref, d_ref, nm_ref, nv_ref, *scratch):
        g = p_ref[0].astype(F32)
        for p in range(1, N_DEV):
            g = g + p_ref[p].astype(F32)
        if layout is not None:
            s1, s2, lg = layout.my_shifts()
            lane = lax.broadcasted_iota(jnp.int32, g.shape, 1)
            scratch[0][...] = jnp.where(lane < lg, pltpu.roll(g, cp - s1, 1), pltpu.roll(g, cp - s2, 1))
            g = scratch[0][:, 0:c]
        g_ref[...] = g
        d_ref[...], nm_ref[...], nv_ref[...] = _adamw(g, w_ref[...], m_ref[...], v_ref[...])

    blk = pl.BlockSpec((None, tr, c), lambda i: (0, i, 0))
    return pl.pallas_call(
        body, name=name, grid=(r // tr,),
        in_specs=[pl.BlockSpec((N_DEV, tr, cp), lambda i: (0, i, 0)), blk, blk, blk],
        out_specs=[blk] * 4, out_shape=[_sds((1, r, c), F32)] * 4,
        scratch_shapes=[] if layout is None else [pltpu.VMEM((tr, cp), F32)],
        compiler_params=_params(("parallel",)),
    )(parts, w, m, v)


def _small_update(part, w, m, v):
    n = part.shape[1]

    def body(p_ref, w_ref, m_ref, v_ref, g_ref, d_ref, nm_ref, nv_ref, buf, send, recv):
        me, peers = _mesh_place()
        buf[me] = p_ref[...]
        sent = []
        for d, dev, flat in peers:
            cp = pltpu.make_async_remote_copy(src_ref=p_ref, dst_ref=buf.at[me], send_sem=send.at[d],
                                              recv_sem=recv.at[d], device_id=dev, device_id_type=MESH)
            cp.start()
            sent.append(cp)
        for d, dev, flat in peers:
            pltpu.make_async_remote_copy(src_ref=p_ref, dst_ref=buf.at[flat], send_sem=send.at[d],
                                         recv_sem=recv.at[d], device_id=dev, device_id_type=MESH).wait_recv()
        for cp in sent:
            cp.wait_send()
        g = buf[0]
        for p in range(1, N_DEV):
            g = g + buf[p]
        g_ref[...] = g
        d_ref[...], nm_ref[...], nv_ref[...] = _adamw(g, w_ref[...], m_ref[...], v_ref[...])

    vm = pl.BlockSpec(memory_space=pltpu.VMEM)
    return pl.pallas_call(
        body, name="small_update", in_specs=[vm] * 4, out_specs=[vm] * 4, out_shape=[_sds((1, n), F32)] * 4,
        scratch_shapes=[pltpu.VMEM((N_DEV, 1, n), F32), pltpu.SemaphoreType.DMA((N_DEV,)),
                        pltpu.SemaphoreType.DMA((N_DEV,))],
    )(part, w, m, v)


class _WInLayout:
    def __init__(self, n8, n_f, d_sb, d_fox, d):
        assert n8 % LANES == 1 and n_f < LANES and d % (N_DEV * LANES) == 0
        self.n8, self.n_f, self.d = n8, n_f, d
        self.sp = n8 // LANES
        self.wp = (n8 + 2 * LANES - 2) // LANES * LANES
        self.n_qkv = 3 * (d_sb + d_fox)
        nq, dt, tc = self.n_qkv // LANES, d // LANES, d // N_DEV // LANES
        h_sb, h_fox = d_sb // HEAD_DIM, d_fox // HEAD_DIM
        self.sources = {}
        self.part_tile = {}
        for p in range(N_DEV):
            lg = min(max(self.n_qkv + n_f - n8 * p, 0), n8)
            s1, s2 = p, p + LANES - n_f
            spans = []
            if lg > 0:
                spans.append(("a", self.sp * p, s1 // LANES, (lg + s1 - 1) // LANES))
            if lg < n8:
                spans.append(("g", self.sp * p - 1 - nq, (lg + s2) // LANES, (n8 - 1 + s2) // LANES))
            for kind, base, first, last in spans:
                for i in range(first, last + 1):
                    assert (p, i) not in self.part_tile
                    self.part_tile[(p, i)] = (kind, base + i)
                    self.sources.setdefault((kind, base + i), []).append((p, i))
        self.cat_tiles = [("a", r * h_sb + h) for h in range(h_sb) for r in range(3)]
        self.cat_tiles += [("a", 3 * h_sb + r * h_fox + h) for h in range(h_fox) for r in range(3)]
        self.cat_tiles += [("g", which * dt + j * tc + half) for j in range(N_DEV) for which in (0, 1) for half in range(tc)]
        self.cat_tiles += [("a", nq)] + [None] * (F_PAD // LANES - 1)
        self.cat_index = {key: c for c, key in enumerate(self.cat_tiles) if key is not None}

    def my_shifts(self):
        me = _flat_me()
        return me, me + LANES - self.n_f, jnp.clip(self.n_qkv + self.n_f - self.n8 * me, 0, self.n8)


def _lane_tile(i):
    return pl.ds(i * LANES, LANES)


def _w_in_shift(w_in, lay, tr=256):
    _, d, n8 = w_in.shape

    def body(w_ref, o_ref, buf):
        buf[...] = jnp.zeros_like(buf)
        buf[:, 0:n8] = w_ref[...]
        v = buf[...]
        s1, s2, lg = lay.my_shifts()
        pos = lax.broadcasted_iota(jnp.int32, v.shape, 1)
        o_ref[...] = jnp.where(pos < lg + s1, pltpu.roll(v, s1, 1),
                               jnp.where(pos >= lg + s2, pltpu.roll(v, s2, 1), 0.0)).astype(BF16)

    return pl.pallas_call(
        body, name="w_in_shift", grid=(d // tr,),
        in_specs=[pl.BlockSpec((None, tr, n8), lambda i: (0, i, 0))],
        out_specs=pl.BlockSpec((tr, lay.wp), lambda i: (i, 0)), out_shape=_sds((d, lay.wp), BF16),
        scratch_shapes=[pltpu.VMEM((tr, lay.wp), F32)],
        compiler_params=_params(("parallel",)),
    )(w_in)


def _w_in_build(g_in, lay, tr=256):
    d = g_in.shape[1]
    width = len(lay.cat_tiles) * LANES

    def body(g_ref, o_ref):
        for c, key in enumerate(lay.cat_tiles):
            if key is None:
                o_ref[:, _lane_tile(c)] = jnp.zeros((tr, LANES), BF16)
                continue
            (p, i), *more = lay.sources[key]
            val = g_ref[p, :, _lane_tile(i)]
            for p2, i2 in more:
                val = val + g_ref[p2, :, _lane_tile(i2)]
            o_ref[:, _lane_tile(c)] = val

    return pl.pallas_call(
        body, name="w_in_build", grid=(d // tr,),
        in_specs=[pl.BlockSpec((N_DEV, tr, lay.wp), lambda i: (0, i, 0))],
        out_specs=pl.BlockSpec((tr, width), lambda i: (i, 0)), out_shape=_sds((d, width), BF16),
        compiler_params=_params(("parallel",)),
    )(g_in)


def _w_in_grad_parts(dwq, dwgf, lay, tr=256):
    d = dwq.shape[0]
    nq = lay.n_qkv // LANES

    def body(q_ref, g_ref, o_ref):
        for p in range(N_DEV):
            for i in range(lay.wp // LANES):
                key = lay.part_tile.get((p, i))
                if key is None:
                    o_ref[p, :, _lane_tile(i)] = jnp.zeros((tr, LANES), BF16)
                    continue
                c = lay.cat_index[key]
                o_ref[p, :, _lane_tile(i)] = q_ref[:, _lane_tile(c)] if c < nq else g_ref[:, _lane_tile(c - nq)]

    return pl.pallas_call(
        body, name="w_in_grad_parts", grid=(d // tr,),
        in_specs=[pl.BlockSpec((tr, dwq.shape[1]), lambda i: (i, 0)), pl.BlockSpec((tr, dwgf.shape[1]), lambda i: (i, 0))],
        out_specs=pl.BlockSpec((N_DEV, tr, lay.wp), lambda i: (0, i, 0)), out_shape=_sds((N_DEV, d, lay.wp), BF16),
        compiler_params=_params(("parallel",)),
    )(dwq, dwgf)


def kernel(x, norm_mix_pre, norm_mix_post, w_in, b_forget, w_branch_sb, w_branch_fox, w_out, norm_ffn_pre, norm_ffn_post, w_ffn_gate, w_ffn_up, w_ffn_down, loss_target, m_norm_mix_pre, m_norm_mix_post, m_w_in, m_b_forget, m_w_branch_sb, m_w_branch_fox, m_w_out, m_norm_ffn_pre, m_norm_ffn_post, m_w_ffn_gate, m_w_ffn_up, m_w_ffn_down, v_norm_mix_pre, v_norm_mix_post, v_w_in, v_b_forget, v_w_branch_sb, v_w_branch_fox, v_w_out, v_norm_ffn_pre, v_norm_ffn_post, v_w_ffn_gate, v_w_ffn_up, v_w_ffn_down):
    xs, target = x[0], loss_target[0]
    s, d = xs.shape
    d_sb, d_fox = w_branch_sb.shape[1], w_branch_fox.shape[1]
    h_sb, h_fox = d_sb // HEAD_DIM, d_fox // HEAD_DIM
    n_f = b_forget.shape[1]
    fs = w_ffn_gate.shape[2]
    cs = d // N_DEV
    n_qkv = 3 * (d_sb + d_fox)
    n_gf = 2 * d + F_PAD
    f_blk = 2 * d // LANES
    big = (w_in, w_branch_sb, w_branch_fox, w_out, w_ffn_gate, w_ffn_up, w_ffn_down)
    big_m = (m_w_in, m_w_branch_sb, m_w_branch_fox, m_w_out, m_w_ffn_gate, m_w_ffn_up, m_w_ffn_down)
    big_v = (v_w_in, v_w_branch_sb, v_w_branch_fox, v_w_out, v_w_ffn_gate, v_w_ffn_up, v_w_ffn_down)

    lay = _WInLayout(w_in.shape[2], n_f, d_sb, d_fox, d)
    send1, recv1, lands, token = _gather_start([_w_in_shift(w_in, lay)] + [w[0].astype(BF16) for w in big[1:]])
    b_pad = jnp.pad(b_forget, ((0, 0), (0, LANES - n_f)))

    u, u_t = _pre_norm(xs, norm_mix_pre, dep=token)
    l_in, send2, recv2, token = _gather_forward("gather_in_forward", lands[0:1], 0, send1, recv1, u)
    (g_in,) = _gather_wait("gather_in_wait", l_in, 0, recv1, send2, recv2, token)
    w_cat = _w_in_build(g_in, lay)
    qkv = _mm_plain("proj_qkv", "nn", u, w_cat, BF16, n=n_qkv)
    gf = _mm_plain("proj_gates", "nn", u, w_cat, F32, n_off=n_qkv, n=n_gf)
    cum_col, cum_row = _forget_fwd(gf, b_pad, f_blk)
    o_sb, o_sb_t, tot = _sb_fwd(qkv, h_sb)
    l_mid, send2, recv2, token = _gather_forward("gather_mid_forward", lands[1:4], 1, send1, recv1, o_sb)
    o_fx, o_fx_t, o_fx32, lse = _fox_fwd(qkv, cum_col, cum_row, h_fox, h_sb, token)
    g_sb, g_fx, g_out = _gather_wait("gather_mid_wait", l_mid, 1, recv1, send2, recv2, o_fx)
    w_out_full = g_out.reshape(d, d)
    l_ffn, send2, recv2, token = _gather_forward("gather_ffn_forward", lands[4:7], 4, send1, recv1, o_fx)
    merged, merged_t, a_sb, a_fx = _branch_merge(o_sb, o_fx, g_sb, g_fx, gf, token)
    mix = _mm_plain("out_proj", "nn", merged, w_out_full, F32)
    g_gate, g_up, g_down = _gather_wait("gather_ffn_wait", l_ffn, 4, recv1, send2, recv2, mix)
    h1, u2, u2_t = _mid_norms(xs, mix, norm_mix_post, norm_ffn_pre)
    gate, up, act, act_t = _ffn_up(u2, g_gate, g_up)
    tm, tn = _tile(s, 1024), _tile(d, 1024)
    ff = _matmul("ffn_down", "nn",
                 [(act, pl.BlockSpec((None, tm, fs), lambda i, j, k: (k, i, 0)),
                   g_down, pl.BlockSpec((None, fs, tn), lambda i, j, k: (k, 0, j)))],
                 (s // tm, d // tn, N_DEV), (tm, tn), _sds((s, d), F32), pl.BlockSpec((tm, tn), lambda i, j, k: (i, j)))
    loss_part, dy, dff, dg_ffn_post = _loss_head(h1, ff, target, norm_ffn_post)

    dgate, dup = _ffn_down_bwd(dff, g_down, gate, up)
    dw_down = _matmul("dw_down", "nn",
                      [(act_t, pl.BlockSpec((None, fs, s), lambda j, n, k: (j, 0, 0)),
                        dff, pl.BlockSpec((s, tn), lambda j, n, k: (0, n)))],
                      (N_DEV, d // tn, 1), (fs, tn), _sds((N_DEV, fs, d), BF16),
                      pl.BlockSpec((None, fs, tn), lambda j, n, k: (j, 0, n)))

    def dw_up(name, dact):
        return _matmul(name, "nn",
                       [(u2_t, pl.BlockSpec((tn, s), lambda j, i, k: (i, 0)),
                         dact, pl.BlockSpec((None, s, fs), lambda j, i, k: (j, 0, 0)))],
                       (N_DEV, d // tn, 1), (tn, fs), _sds((N_DEV, d, fs), BF16),
                       pl.BlockSpec((None, tn, fs), lambda j, i, k: (j, i, 0)))

    dw_gate, dw_upw = dw_up("dw_gate", dgate), dw_up("dw_up", dup)
    rs_ffn = _scatter_start("scatter_ffn", [dw_gate, dw_upw, dw_down])
    a_spec = pl.BlockSpec((None, tm, fs), lambda i, j, k: (k, i, 0))
    b_spec = pl.BlockSpec((None, tn, fs), lambda i, j, k: (k, j, 0))
    du2 = _matmul("du2", "nt", [(dgate, a_spec, g_gate, b_spec), (dup, a_spec, g_up, b_spec)],
                  (s // tm, d // tn, N_DEV), (tm, tn), _sds((s, d), F32), pl.BlockSpec((tm, tn), lambda i, j, k: (i, j)),
                  dep=rs_ffn[4])
    dh1, dmix, dg_ffn_pre, dg_mix_post = _mid_norms_bwd(dy, du2, h1, mix, norm_ffn_pre, norm_mix_post)

    da_sb, da_fx, dgf = _merge_bwd(dmix, w_out_full, gf, a_sb, a_fx)
    dw_out = _mm_plain("dw_out", "nn", merged_t, dmix, BF16).reshape(N_DEV, cs, d)

    def branch_bwd(tag, da, w_b, o_t, width):
        tb = _tile(width, 1024)
        do = _matmul("do_" + tag, "nt",
                     [(da, pl.BlockSpec((tm, cs), lambda i, j, k: (i, k)),
                       w_b, pl.BlockSpec((None, tb, cs), lambda i, j, k: (k, j, 0)))],
                     (s // tm, width // tb, N_DEV), (tm, tb), _sds((s, width), BF16),
                     pl.BlockSpec((tm, tb), lambda i, j, k: (i, j)))
        dw = _matmul("dw_" + tag, "nn",
                     [(o_t, pl.BlockSpec((width, s), lambda j, i, k: (0, 0)),
                       da, pl.BlockSpec((s, cs), lambda j, i, k: (0, j)))],
                     (N_DEV, 1, 1), (width, cs), _sds((N_DEV, width, cs), BF16),
                     pl.BlockSpec((None, width, cs), lambda j, i, k: (j, 0, 0)))
        return do, dw

    do_sb, dw_sb = branch_bwd("sb", da_sb, g_sb, o_sb_t, d_sb)
    do_fx, dw_fx = branch_bwd("fox", da_fx, g_fx, o_fx_t, d_fox)

    rs_mid = _scatter_start("scatter_mid", [dw_sb, dw_fx, dw_out])

    dqkv = _sb_bwd(qkv, do_sb, tot, h_sb, rs_mid[4])
    dqkv, dcum = _fox_bwd(dqkv, qkv, do_fx, o_fx32, lse, cum_col, cum_row, h_fox, h_sb)
    dgf, db_part = _forget_bwd(dgf, dcum, gf, b_pad, f_blk)
    dw_in = _w_in_grad_parts(_mm_plain("dw_qkv", "nn", u_t, dqkv, BF16), _mm_plain("dw_gates", "nn", u_t, dgf, BF16), lay)
    rs_in = _scatter_start("scatter_in", [dw_in])
    du = _mm_plain("du_qkv", "nt", dqkv, w_cat, F32, tn=1024, dep=rs_in[4])
    du = _mm_plain("du_gates", "nt", dgf, w_cat, F32, tn=1024, k_off=n_qkv, init=du)
    dx, dg_mix_pre = _pre_norm_bwd(dh1, du, xs, norm_mix_pre)

    upd = {}

    def update_group(tag, rs, names, after):
        parts = _scatter_wait("scatter_" + tag + "_wait", *rs[:4], after=after)
        for nm, p in zip(names, parts):
            w, m, v = weights[nm]
            upd[nm] = _update("update_" + nm, p, w, m, v, layout=lay if nm == "w_in" else None)

    weights = dict(zip(("w_in", "w_branch_sb", "w_branch_fox", "w_out", "w_ffn_gate", "w_ffn_up", "w_ffn_down"),
                       zip(big, big_m, big_v)))
    update_group("ffn", rs_ffn, ("w_ffn_gate", "w_ffn_up", "w_ffn_down"), dx)
    update_group("mid", rs_mid, ("w_branch_sb", "w_branch_fox", "w_out"), upd["w_ffn_down"][0])
    update_group("in", rs_in, ("w_in",), upd["w_out"][0])

    small = ((norm_mix_pre, m_norm_mix_pre, v_norm_mix_pre), (norm_mix_post, m_norm_mix_post, v_norm_mix_post),
             (norm_ffn_pre, m_norm_ffn_pre, v_norm_ffn_pre), (norm_ffn_post, m_norm_ffn_post, v_norm_ffn_post))
    pad_f = ((0, 0), (0, LANES - n_f))
    cat = lambda i: jnp.concatenate([t[i] for t in small] + [jnp.pad((b_forget, m_b_forget, v_b_forget)[i], pad_f)], axis=1)
    sm = _small_update(jnp.concatenate([dg_mix_pre, dg_mix_post, dg_ffn_pre, dg_ffn_post, db_part], axis=1),
                       cat(0), cat(1), cat(2))
    for i, nm in enumerate(("norm_mix_pre", "norm_mix_post", "norm_ffn_pre", "norm_ffn_post")):
        upd[nm] = [o[:, i * d:(i + 1) * d] for o in sm]
    upd["b_forget"] = [o[:, 4 * d:4 * d + n_f] for o in sm]

    loss = lax.psum(loss_part[0, 0], ("x", "y", "c"))
    order = ("norm_mix_pre", "norm_mix_post", "w_in", "b_forget", "w_branch_sb", "w_branch_fox", "w_out",
             "norm_ffn_pre", "norm_ffn_post", "w_ffn_gate", "w_ffn_up", "w_ffn_down")
    return (loss, dx[None]) + tuple(upd[nm][i] for i in range(4) for nm in order)
```

```python
import jax
import jax.numpy as jnp
from jax import lax
from jax.experimental import pallas as pl
from jax.experimental.pallas import tpu as pltpu

F32 = jnp.float32
BF16 = jnp.bfloat16
MESH = pl.DeviceIdType.MESH
ANY = pl.BlockSpec(memory_space=pl.ANY)
HBM = pl.BlockSpec(memory_space=pltpu.HBM)
SEM = pl.BlockSpec(memory_space=pltpu.SEMAPHORE)
EFFECT = pltpu.SideEffectType.DATAFLOW_SIDE_EFFECTING

N_DEV = 8
HEAD_DIM = 128
RMS_EPS = 1e-6
F_PAD = 512
LANES = 128
ATT_TQ = 256
ATT_TK = 256
NEG_BIG = -1e30
VMEM_LIMIT = 56 * 1024 * 1024

ADAM_LR = 0.001
ADAM_B1 = 0.9
ADAM_B2 = 0.999
ADAM_EPS = 1e-08
ADAM_WD = 0.01
ADAM_STEP = 10

_DIMS = {"nn": ((1,), (0,)), "nt": ((1,), (1,)), "tn": ((0,), (0,))}


def _params(sem):
    return pltpu.CompilerParams(dimension_semantics=sem, vmem_limit_bytes=VMEM_LIMIT)


def _dot(a, b, mode="nn"):
    return lax.dot_general(a.astype(BF16), b.astype(BF16), (_DIMS[mode], ((), ())), preferred_element_type=F32)


def _tile(n, pref):
    if n <= pref:
        return n
    t = (pref // LANES) * LANES
    while n % t:
        t -= LANES
    return t


def _split2(v):
    hi = v.astype(BF16)
    return hi, (v - hi.astype(F32)).astype(BF16)


def _split3(v):
    a = v.astype(BF16)
    r = v - a.astype(F32)
    b = r.astype(BF16)
    return a, b, (r - b.astype(F32)).astype(BF16)


def _tri(n, cmp):
    r = lax.broadcasted_iota(jnp.int32, (n, n), 0)
    c = lax.broadcasted_iota(jnp.int32, (n, n), 1)
    return jnp.where(cmp(r, c), 1.0, 0.0).astype(BF16)


def _lane_pick(v, h):
    lane = lax.broadcasted_iota(jnp.int32, v.shape, 1)
    return jnp.sum(jnp.where(lane == h, v, 0.0), axis=1, keepdims=True)


def _lane_put(ref, rows, h, col):
    old = ref[rows, :]
    lane = lax.broadcasted_iota(jnp.int32, old.shape, 1)
    ref[rows, :] = jnp.where(lane == h, col, old)


def _sigmoid(z):
    return 1.0 / (1.0 + jnp.exp(-z))


def _log_sigmoid(z):
    return jnp.minimum(z, 0.0) - jnp.log(1.0 + jnp.exp(-jnp.abs(z)))


def _sds(shape, dtype):
    return jax.ShapeDtypeStruct(shape, dtype)


def _matmul(name, mode, pairs, grid, acc_shape, out_shape, out_specs, extras=(), epilogue=None, init=None, dep=None):
    n_p, n_e = len(pairs), len(extras)
    nk = grid[-1]
    single = not isinstance(out_shape, (list, tuple))
    n_i = 0 if init is None else 1
    n_d = 0 if dep is None else 1

    one_step = nk == 1 and init is None

    def body(*refs):
        ab = refs[:2 * n_p]
        ex = refs[2 * n_p:2 * n_p + n_e]
        ini = refs[2 * n_p + n_e:2 * n_p + n_e + n_i]
        outs = refs[2 * n_p + n_e + n_i + n_d:len(refs) - (0 if one_step else 1)]

        def finish(total):
            if epilogue is None:
                outs[0][...] = total.astype(outs[0].dtype)
            else:
                epilogue(total, ex, outs)

        t = _dot(ab[0][...], ab[1][...], mode)
        for p in range(1, n_p):
            t = t + _dot(ab[2 * p][...], ab[2 * p + 1][...], mode)
        if one_step:
            finish(t)
            return
        acc = refs[-1]
        k = pl.program_id(len(grid) - 1)

        @pl.when(k == 0)
        def _():
            acc[...] = t if init is None else ini[0][...].astype(F32) + t

        @pl.when(k > 0)
        def _():
            acc[...] += t

        @pl.when(k == nk - 1)
        def _():
            finish(acc[...])

    in_specs = [s for (_, sa, _, sb) in pairs for s in (sa, sb)] + [s for (_, s) in extras]
    args = [v for (a, _, b, _) in pairs for v in (a, b)] + [e for (e, _) in extras]
    if init is not None:
        in_specs.append(init[1])
        args.append(init[0])
    if dep is not None:
        in_specs.append(ANY)
        args.append(dep)
    return pl.pallas_call(
        body, name=name, grid=grid, in_specs=in_specs,
        out_specs=out_specs if single else list(out_specs),
        out_shape=out_shape if single else list(out_shape),
        scratch_shapes=[] if one_step else [pltpu.VMEM(acc_shape, F32)],
        compiler_params=_params(("parallel",) * (len(grid) - 1) + ("arbitrary",)),
    )(*args)


def _mm_plain(name, mode, a, b, out_dtype, *, n_off=0, n=None, k_off=0, tm=1024, tn=1536, tk=2048, init=None, dep=None):
    if mode == "nn":
        (m, kk), nn_ = a.shape, b.shape[1]
    elif mode == "nt":
        (m, kk), nn_ = a.shape, b.shape[0]
    else:
        (kk, m), nn_ = a.shape, b.shape[1]
    n = nn_ if n is None else n
    tm, tn, tk = _tile(m, tm), _tile(n, tn), _tile(kk, tk)
    while n_off % tn or n % tn:
        tn -= LANES
    while k_off % tk or kk % tk:
        tk -= LANES
    off, koff = n_off // tn, k_off // tk
    a_spec = {"nn": pl.BlockSpec((tm, tk), lambda i, j, k: (i, k)),
              "nt": pl.BlockSpec((tm, tk), lambda i, j, k: (i, k)),
              "tn": pl.BlockSpec((tk, tm), lambda i, j, k: (k, i))}[mode]
    b_spec = {"nn": pl.BlockSpec((tk, tn), lambda i, j, k: (k, j + off)),
              "nt": pl.BlockSpec((tn, tk), lambda i, j, k: (j, k + koff)),
              "tn": pl.BlockSpec((tk, tn), lambda i, j, k: (k, j))}[mode]
    o_spec = pl.BlockSpec((tm, tn), lambda i, j, k: (i, j))
    if init is not None:
        init = (init, o_spec)
    return _matmul(name, mode, [(a, a_spec, b, b_spec)], (m // tm, n // tn, kk // tk), (tm, tn),
                   _sds((m, n), out_dtype), o_spec, init=init, dep=dep)


def _rows_call(name, body, ins, outs, s, tr=256, dep=None):
    def spec(v, per_row):
        if per_row == "transposed":
            return pl.BlockSpec((v.shape[0], tr), lambda i: (0, i))
        if per_row:
            return pl.BlockSpec((tr, v.shape[1]), lambda i: (i, 0))
        return pl.BlockSpec(v.shape, lambda i: (0, 0))
    n_in = len(ins)
    deps = [] if dep is None else [dep]

    def with_dep(*refs):
        body(*refs[:n_in], *refs[n_in + len(deps):])

    return pl.pallas_call(
        with_dep, name=name, grid=(s // tr,),
        in_specs=[spec(v, p) for v, p in ins] + [ANY] * len(deps), out_specs=[spec(v, p) for v, p in outs],
        out_shape=[_sds(v.shape, v.dtype) for v, _ in outs],
        compiler_params=_params(("arbitrary",)),
    )(*[v for v, _ in ins], *deps)


def _rsq(v):
    return lax.rsqrt(jnp.mean(v * v, axis=-1, keepdims=True) + RMS_EPS)


def _norm_bwd(dy, v, r, g):
    vh = v * r
    t = dy * g
    dv = r * (t - vh * jnp.mean(t * vh, axis=-1, keepdims=True))
    return dv, jnp.sum(dy * vh, axis=0, keepdims=True)


def _accum(ref, val):
    @pl.when(pl.program_id(0) == 0)
    def _():
        ref[...] = jnp.zeros_like(ref)
    ref[...] += val


def _pre_norm(x, g, dep=None):
    def body(x_ref, g_ref, u_ref, ut_ref):
        v = x_ref[...]
        u = (v * _rsq(v) * g_ref[...]).astype(BF16)
        u_ref[...] = u
        ut_ref[...] = u.T
    s, d = x.shape
    return _rows_call("pre_norm", body, [(x, True), (g, False)],
                      [(_sds((s, d), BF16), True), (_sds((d, s), BF16), "transposed")], s, dep=dep)


def _mid_norms(x, mix, g_post, g_pre):
    def body(x_ref, mix_ref, gp_ref, gn_ref, h_ref, u_ref, ut_ref):
        mv = mix_ref[...]
        h = x_ref[...] + mv * _rsq(mv) * gp_ref[...]
        h_ref[...] = h
        u = (h * _rsq(h) * gn_ref[...]).astype(BF16)
        u_ref[...] = u
        ut_ref[...] = u.T
    s, d = x.shape
    return _rows_call("mid_norms", body, [(x, True), (mix, True), (g_post, False), (g_pre, False)],
                      [(_sds((s, d), F32), True), (_sds((s, d), BF16), True), (_sds((d, s), BF16), "transposed")], s)


def _loss_head(h1, ff, target, g):
    s, d = h1.shape

    def body(h_ref, ff_ref, t_ref, g_ref, loss_ref, dy_ref, dff_ref, dg_ref):
        fv = ff_ref[...]
        r = _rsq(fv)
        err = h_ref[...] + fv * r * g_ref[...] - t_ref[...]
        part = 0.5 * jnp.sum(jnp.mean(err * err, axis=-1, keepdims=True), axis=0, keepdims=True)
        _accum(loss_ref, jnp.broadcast_to(part, loss_ref.shape))
        dy = err * (1.0 / d)
        dy_ref[...] = dy
        dff, dg = _norm_bwd(dy, fv, r, g_ref[...])
        dff_ref[...] = dff.astype(BF16)
        _accum(dg_ref, dg)

    return _rows_call("loss_head", body, [(h1, True), (ff, True), (target, True), (g, False)],
                      [(_sds((1, LANES), F32), False), (_sds((s, d), F32), True),
                       (_sds((s, d), BF16), True), (_sds((1, d), F32), False)], s)


def _mid_norms_bwd(dy, du2, h1, mix, g_pre, g_post):
    s, d = dy.shape

    def body(dy_ref, du_ref, h_ref, mix_ref, gn_ref, gp_ref, dh_ref, dmix_ref, dgn_ref, dgp_ref):
        h = h_ref[...]
        dh, dgn = _norm_bwd(du_ref[...], h, _rsq(h), gn_ref[...])
        dh = dh + dy_ref[...]
        dh_ref[...] = dh
        _accum(dgn_ref, dgn)
        mv = mix_ref[...]
        dmix, dgp = _norm_bwd(dh, mv, _rsq(mv), gp_ref[...])
        dmix_ref[...] = dmix.astype(BF16)
        _accum(dgp_ref, dgp)

    return _rows_call("mid_norms_bwd", body,
                      [(dy, True), (du2, True), (h1, True), (mix, True), (g_pre, False), (g_post, False)],
                      [(_sds((s, d), F32), True), (_sds((s, d), BF16), True),
                       (_sds((1, d), F32), False), (_sds((1, d), F32), False)], s)


def _pre_norm_bwd(dh1, du, x, g, dep=None):
    s, d = x.shape

    def body(dh_ref, du_ref, x_ref, g_ref, dx_ref, dg_ref):
        v = x_ref[...]
        dv, dg = _norm_bwd(du_ref[...], v, _rsq(v), g_ref[...])
        dx_ref[...] = dh_ref[...] + dv
        _accum(dg_ref, dg)

    return _rows_call("pre_norm_bwd", body, [(dh1, True), (du, True), (x, True), (g, False)],
                      [(_sds((s, d), F32), True), (_sds((1, d), F32), False)], s, dep=dep)


def _forget_fwd(gf, b_pad, f_blk):
    s = gf.shape[0]
    tb = ATT_TK
    nb = s // tb

    def body(f_ref, b_ref, col_ref, row_ref):
        incl = _tri(tb, lambda r, c: c <= r)
        carry = jnp.zeros((1, LANES), F32)
        for i in range(nb):
            lf = _log_sigmoid(f_ref[pl.ds(i * tb, tb), :] + b_ref[...])
            parts = _split3(lf)
            cum = carry + _dot(incl, parts[0]) + _dot(incl, parts[1]) + _dot(incl, parts[2])
            col_ref[pl.ds(i * tb, tb), :] = cum
            row_ref[i] = cum.T
            carry = carry + jnp.sum(lf, axis=0, keepdims=True)

    return pl.pallas_call(
        body, name="forget_fwd", grid=(1,),
        in_specs=[pl.BlockSpec((s, LANES), lambda i: (0, f_blk)), pl.BlockSpec((1, LANES), lambda i: (0, 0))],
        out_specs=[pl.BlockSpec((s, LANES), lambda i: (0, 0)), pl.BlockSpec((nb, LANES, tb), lambda i: (0, 0, 0))],
        out_shape=[_sds((s, LANES), F32), _sds((nb, LANES, tb), F32)],
        compiler_params=_params(("arbitrary",)),
    )(gf, b_pad)


def _forget_bwd(dgf, dcum, gf, b_pad, f_blk):
    s = gf.shape[0]
    tb = ATT_TK
    nb = s // tb
    sec = dgf.shape[1] // F_PAD - 1

    def body(dgf_hbm, dc_ref, f_ref, b_ref, out_ref, db_ref):
        del dgf_hbm
        incl = _tri(tb, lambda r, c: c >= r)
        carry = jnp.zeros((1, LANES), F32)
        db = jnp.zeros((1, LANES), F32)
        out_ref[...] = jnp.zeros_like(out_ref)
        for i in reversed(range(nb)):
            dc = dc_ref[pl.ds(i * tb, tb), :]
            parts = _split3(dc)
            dlf = carry + _dot(incl, parts[0]) + _dot(incl, parts[1]) + _dot(incl, parts[2])
            z = f_ref[pl.ds(i * tb, tb), :] + b_ref[...]
            df = dlf * _sigmoid(-z)
            out_ref[pl.ds(i * tb, tb), pl.ds(0, LANES)] = df.astype(BF16)
            db = db + jnp.sum(df, axis=0, keepdims=True)
            carry = carry + jnp.sum(dc, axis=0, keepdims=True)
        db_ref[...] = db

    return pl.pallas_call(
        body, name="forget_bwd", grid=(1,),
        in_specs=[ANY, pl.BlockSpec((s, LANES), lambda i: (0, 0)),
                  pl.BlockSpec((s, LANES), lambda i: (0, f_blk)), pl.BlockSpec((1, LANES), lambda i: (0, 0))],
        out_specs=[pl.BlockSpec((s, F_PAD), lambda i: (0, sec)), pl.BlockSpec((1, LANES), lambda i: (0, 0))],
        out_shape=[_sds(dgf.shape, BF16), _sds((1, LANES), F32)],
        input_output_aliases={0: 0},
        compiler_params=_params(("arbitrary",)),
    )(dgf, dcum, gf, b_pad)


def _diag_mask(strict):
    r = lax.broadcasted_iota(jnp.int32, (ATT_TQ, ATT_TK), 0)
    c = lax.broadcasted_iota(jnp.int32, (ATT_TQ, ATT_TK), 1)
    return c < r if strict else c <= r


def _qkv_specs(hb0, s):
    return [pl.BlockSpec((ATT_TQ, HEAD_DIM), lambda h, i: (i, 3 * (hb0 + h))),
            pl.BlockSpec((s, HEAD_DIM), lambda h, i: (0, 3 * (hb0 + h) + 1)),
            pl.BlockSpec((s, HEAD_DIM), lambda h, i: (0, 3 * (hb0 + h) + 2))]


def _sb_fwd(qkv, n_heads):
    s = qkv.shape[0]
    scale = HEAD_DIM ** -0.5
    tq, tk = ATT_TQ, ATT_TK

    def body(q_ref, k_ref, v_ref, o_ref, ot_ref, tot_ref):
        h, i = pl.program_id(0), pl.program_id(1)

        @pl.when((h == 0) & (i == 0))
        def _():
            tot_ref[...] = jnp.zeros_like(tot_ref)

        q = q_ref[...]
        upper = _tri(tk, lambda r, c: r > c)

        def tile(kj, carry, mask):
            c, acc = carry
            rows = pl.ds(pl.multiple_of(kj * tk, tk), tk)
            z = _dot(q, k_ref[rows, :], "nt") * scale
            lsz = _log_sigmoid(z)
            lk = lsz - z if mask is None else jnp.where(mask, lsz - z, 0.0)
            hi, lo = _split2(lk)
            w = jnp.exp(lsz + (c + _dot(hi, upper) + _dot(lo, upper)))
            if mask is not None:
                w = jnp.where(mask, w, 0.0)
            return c + jnp.sum(lk, axis=1, keepdims=True), acc + _dot(w, v_ref[rows, :])

        carry = tile(i, (jnp.zeros((tq, 1), F32), jnp.zeros((tq, HEAD_DIM), F32)), _diag_mask(True))
        c, acc = lax.fori_loop(0, i, lambda n, cr: tile(i - 1 - n, cr, None), carry)
        o = acc.astype(BF16)
        o_ref[...] = o
        ot_ref[...] = o.T
        _lane_put(tot_ref, pl.ds(pl.multiple_of(i * tq, tq), tq), h, c)

    return pl.pallas_call(
        body, name="sb_fwd", grid=(n_heads, s // tq),
        in_specs=_qkv_specs(0, s),
        out_specs=[pl.BlockSpec((tq, HEAD_DIM), lambda h, i: (i, h)), pl.BlockSpec((HEAD_DIM, tq), lambda h, i: (h, i)),
                   pl.BlockSpec((s, LANES), lambda h, i: (0, 0))],
        out_shape=[_sds((s, n_heads * HEAD_DIM), BF16), _sds((n_heads * HEAD_DIM, s), BF16), _sds((s, LANES), F32)],
        compiler_params=_params(("arbitrary", "arbitrary")),
    )(qkv, qkv, qkv)


def _sb_bwd(qkv, do, tot, n_heads, dep):
    s = qkv.shape[0]
    scale = HEAD_DIM ** -0.5
    tq, tk = ATT_TQ, ATT_TK
    nq = s // tq
    hd = HEAD_DIM

    def body(q_ref, k_ref, v_ref, do_ref, tot_ref, dep_ref, out_ref, dk_acc, dv_acc):
        del dep_ref
        h, i = pl.program_id(0), pl.program_id(1)

        @pl.when(i == 0)
        def _():
            dk_acc[...] = jnp.zeros_like(dk_acc)
            dv_acc[...] = jnp.zeros_like(dv_acc)

        q = q_ref[...]
        dout = do_ref[...]
        total = _lane_pick(tot_ref[...], h)
        incl = _tri(tk, lambda r, c: r <= c)
        excl = _tri(tk, lambda r, c: r < c)

        def tile(kj, carry, mask):
            p_l, p_e, dq = carry
            rows = pl.ds(pl.multiple_of(kj * tk, tk), tk)
            k_t = k_ref[rows, :]
            z = _dot(q, k_t, "nt") * scale
            lsz = _log_sigmoid(z)
            lk = lsz - z if mask is None else jnp.where(mask, lsz - z, 0.0)
            hi, lo = _split2(lk)
            w = jnp.exp(lsz + (total - (p_l + _dot(hi, incl) + _dot(lo, incl))))
            if mask is not None:
                w = jnp.where(mask, w, 0.0)
            e = _dot(dout, v_ref[rows, :], "nt") * w
            hi, lo = _split2(e)
            e_before = p_e + _dot(hi, excl) + _dot(lo, excl)
            sg = jnp.exp(lsz)
            dz = e * (1.0 - sg) - e_before * sg
            if mask is not None:
                dz = jnp.where(mask, dz, 0.0)
            dz = (dz * scale).astype(BF16)
            dk_acc[rows, :] += _dot(dz, q, "tn")
            dv_acc[rows, :] += _dot(w, dout, "tn")
            return p_l + jnp.sum(lk, axis=1, keepdims=True), p_e + jnp.sum(e, axis=1, keepdims=True), dq + _dot(dz, k_t)

        zero = jnp.zeros((tq, 1), F32)
        carry = lax.fori_loop(0, i, lambda kj, cr: tile(kj, cr, None), (zero, zero, jnp.zeros((tq, hd), F32)))
        _, _, dq = tile(i, carry, _diag_mask(True))
        out_ref[pl.ds(pl.multiple_of(i * tq, tq), tq), pl.ds(0, hd)] = dq.astype(BF16)

        @pl.when(i == nq - 1)
        def _():
            out_ref[:, pl.ds(hd, hd)] = dk_acc[...].astype(BF16)
            out_ref[:, pl.ds(2 * hd, hd)] = dv_acc[...].astype(BF16)

    return pl.pallas_call(
        body, name="sb_bwd", grid=(n_heads, nq),
        in_specs=_qkv_specs(0, s) + [pl.BlockSpec((tq, hd), lambda h, i: (i, h)),
                                     pl.BlockSpec((tq, LANES), lambda h, i: (i, 0)), ANY],
        out_specs=pl.BlockSpec((s, 3 * hd), lambda h, i: (0, h)),
        out_shape=_sds(qkv.shape, BF16),
        scratch_shapes=[pltpu.VMEM((s, hd), F32), pltpu.VMEM((s, hd), F32)],
        compiler_params=_params(("arbitrary", "arbitrary")),
    )(qkv, qkv, qkv, do, tot, dep)


def _fox_fwd(qkv, cum_col, cum_row, n_heads, hb0, dep):
    s = qkv.shape[0]
    scale = HEAD_DIM ** -0.5
    tq, tk = ATT_TQ, ATT_TK

    def body(q_ref, k_ref, v_ref, cc_ref, cr_ref, dep_ref, o_ref, ot_ref, o32_ref, lse_ref):
        del dep_ref
        h, i = pl.program_id(0), pl.program_id(1)

        @pl.when((h == 0) & (i == 0))
        def _():
            lse_ref[...] = jnp.zeros_like(lse_ref)

        q = q_ref[...]
        cq = _lane_pick(cc_ref[...], h)

        def tile(kj, carry, mask):
            m, l, acc = carry
            rows = pl.ds(pl.multiple_of(kj * tk, tk), tk)
            sc = _dot(q, k_ref[rows, :], "nt") * scale + cq - cr_ref[kj, pl.ds(h, 1), :]
            if mask is not None:
                sc = jnp.where(mask, sc, NEG_BIG)
            m_new = jnp.maximum(m, jnp.max(sc, axis=1, keepdims=True))
            p = jnp.exp(sc - m_new)
            alpha = jnp.exp(m - m_new)
            hi, lo = _split2(p)
            v_t = v_ref[rows, :]
            return (m_new, alpha * l + jnp.sum(p, axis=1, keepdims=True), alpha * acc + _dot(hi, v_t) + _dot(lo, v_t))

        carry = (jnp.full((tq, 1), NEG_BIG, F32), jnp.zeros((tq, 1), F32), jnp.zeros((tq, HEAD_DIM), F32))
        carry = lax.fori_loop(0, i, lambda kj, cr: tile(kj, cr, None), carry)
        m, l, acc = tile(i, carry, _diag_mask(False))
        o = acc / l
        o_ref[...] = o.astype(BF16)
        ot_ref[...] = o.astype(BF16).T
        o32_ref[...] = o
        _lane_put(lse_ref, pl.ds(pl.multiple_of(i * tq, tq), tq), h, m + jnp.log(l))

    nb = cum_row.shape[0]
    return pl.pallas_call(
        body, name="fox_fwd", grid=(n_heads, s // tq),
        in_specs=_qkv_specs(hb0, s) + [pl.BlockSpec((tq, LANES), lambda h, i: (i, 0)),
                                       pl.BlockSpec((nb, 8, tk), lambda h, i: (0, 0, 0)), ANY],
        out_specs=[pl.BlockSpec((tq, HEAD_DIM), lambda h, i: (i, h)), pl.BlockSpec((HEAD_DIM, tq), lambda h, i: (h, i)),
                   pl.BlockSpec((tq, HEAD_DIM), lambda h, i: (i, h)), pl.BlockSpec((s, LANES), lambda h, i: (0, 0))],
        out_shape=[_sds((s, n_heads * HEAD_DIM), BF16), _sds((n_heads * HEAD_DIM, s), BF16),
                   _sds((s, n_heads * HEAD_DIM), F32), _sds((s, LANES), F32)],
        compiler_params=_params(("arbitrary", "arbitrary")),
    )(qkv, qkv, qkv, cum_col, cum_row, dep)


def _fox_bwd(dqkv, qkv, do, o, lse, cum_col, cum_row, n_heads, hb0):
    s = qkv.shape[0]
    scale = HEAD_DIM ** -0.5
    tq, tk = ATT_TQ, ATT_TK
    nq = s // tq
    hd = HEAD_DIM

    def body(dqkv_hbm, q_ref, k_ref, v_ref, do_ref, o_ref, lse_ref, cc_ref, cr_ref, out_ref, dc_ref,
             dk_acc, dv_acc, col_acc):
        del dqkv_hbm
        h, i = pl.program_id(0), pl.program_id(1)

        @pl.when((h == 0) & (i == 0))
        def _():
            dc_ref[...] = jnp.zeros_like(dc_ref)

        @pl.when(i == 0)
        def _():
            dk_acc[...] = jnp.zeros_like(dk_acc)
            dv_acc[...] = jnp.zeros_like(dv_acc)
            col_acc[...] = jnp.zeros_like(col_acc)

        q = q_ref[...]
        dout = do_ref[...]
        delta = jnp.sum(dout.astype(F32) * o_ref[...], axis=1, keepdims=True)
        shift = _lane_pick(cc_ref[...], h) - _lane_pick(lse_ref[...], h)

        def tile(kj, carry, mask):
            dq, row_sum = carry
            rows = pl.ds(pl.multiple_of(kj * tk, tk), tk)
            k_t = k_ref[rows, :]
            p = jnp.exp(_dot(q, k_t, "nt") * scale + shift - cr_ref[kj, pl.ds(h, 1), :])
            if mask is not None:
                p = jnp.where(mask, p, 0.0)
            ds_f = p * (_dot(dout, v_ref[rows, :], "nt") - delta)
            col_acc[kj] += jnp.broadcast_to(jnp.sum(ds_f, axis=0, keepdims=True), (8, tk))
            ds = (ds_f * scale).astype(BF16)
            dk_acc[rows, :] += _dot(ds, q, "tn")
            dv_acc[rows, :] += _dot(p, dout, "tn")
            return dq + _dot(ds, k_t), row_sum + jnp.sum(ds_f, axis=1, keepdims=True)

        carry = lax.fori_loop(0, i, lambda kj, cr: tile(kj, cr, None), (jnp.zeros((tq, hd), F32), jnp.zeros((tq, 1), F32)))
        dq, row_sum = tile(i, carry, _diag_mask(False))
        q_rows = pl.ds(pl.multiple_of(i * tq, tq), tq)
        out_ref[q_rows, pl.ds(0, hd)] = dq.astype(BF16)
        _lane_put(dc_ref, q_rows, h, row_sum)

        @pl.when(i == nq - 1)
        def _():
            out_ref[:, pl.ds(hd, hd)] = dk_acc[...].astype(BF16)
            out_ref[:, pl.ds(2 * hd, hd)] = dv_acc[...].astype(BF16)
            lane = lax.broadcasted_iota(jnp.int32, (tk, LANES), 1)
            for kj in range(nb):
                col = jnp.broadcast_to(col_acc[kj][0:1, :], (LANES, tk)).T
                old = dc_ref[pl.ds(kj * tk, tk), :]
                dc_ref[pl.ds(kj * tk, tk), :] = jnp.where(lane == h, old - col, old)

    nb = cum_row.shape[0]
    return pl.pallas_call(
        body, name="fox_bwd", grid=(n_heads, nq),
        in_specs=[ANY] + _qkv_specs(hb0, s) + [
            pl.BlockSpec((tq, hd), lambda h, i: (i, h)), pl.BlockSpec((tq, hd), lambda h, i: (i, h)),
            pl.BlockSpec((tq, LANES), lambda h, i: (i, 0)), pl.BlockSpec((tq, LANES), lambda h, i: (i, 0)),
            pl.BlockSpec((nb, 8, tk), lambda h, i: (0, 0, 0))],
        out_specs=[pl.BlockSpec((s, 3 * hd), lambda h, i: (0, hb0 + h)), pl.BlockSpec((s, LANES), lambda h, i: (0, 0))],
        out_shape=[_sds(dqkv.shape, BF16), _sds((s, LANES), F32)],
        scratch_shapes=[pltpu.VMEM((s, hd), F32), pltpu.VMEM((s, hd), F32), pltpu.VMEM((s // tk, 8, tk), F32)],
        input_output_aliases={0: 0},
        compiler_params=_params(("arbitrary", "arbitrary")),
    )(dqkv, qkv, qkv, qkv, do, o, lse, cum_col, cum_row)


def _branch_merge(o_sb, o_fx, w_sb, w_fx, gf, dep, tm=1024):
    s = o_sb.shape[0]
    cs = w_sb.shape[2]
    tm = _tile(s, tm)

    def body(osb_ref, ofx_ref, wsb_ref, wfx_ref, g_ref, dep_ref, merged_ref, mt_ref, asb_ref, afx_ref):
        del dep_ref
        a_sb = _dot(osb_ref[...], wsb_ref[...])
        a_fx = _dot(ofx_ref[...], wfx_ref[...])
        g = g_ref[...]
        merged = (_sigmoid(g[:, :cs]) * a_sb + _sigmoid(g[:, cs:]) * a_fx).astype(BF16)
        merged_ref[...] = merged
        mt_ref[...] = merged.T
        asb_ref[...] = a_sb.astype(BF16)
        afx_ref[...] = a_fx.astype(BF16)

    blk = pl.BlockSpec((tm, cs), lambda i, j: (i, j))
    out = _sds((s, N_DEV * cs), BF16)
    return pl.pallas_call(
        body, name="branch_merge", grid=(s // tm, N_DEV),
        in_specs=[pl.BlockSpec((tm, o_sb.shape[1]), lambda i, j: (i, 0)),
                  pl.BlockSpec((tm, o_fx.shape[1]), lambda i, j: (i, 0)),
                  pl.BlockSpec((None,) + w_sb.shape[1:], lambda i, j: (j, 0, 0)),
                  pl.BlockSpec((None,) + w_fx.shape[1:], lambda i, j: (j, 0, 0)),
                  pl.BlockSpec((tm, 2 * cs), lambda i, j: (i, j)), ANY],
        out_specs=[blk, pl.BlockSpec((cs, tm), lambda i, j: (j, i)), blk, blk],
        out_shape=[out, _sds((N_DEV * cs, s), BF16), out, out],
        compiler_params=_params(("parallel", "arbitrary")),
    )(o_sb, o_fx, w_sb, w_fx, gf, dep)


def _merge_bwd(dmix, w_out, gf, a_sb, a_fx, tm=1024, tk=2048):
    s, d = dmix.shape
    cs = d // N_DEV
    tm, tk = _tile(s, tm), _tile(d, tk)

    def epilogue(acc, ex, outs):
        g, a_sb, a_fx = ex[0][...], ex[1][...].astype(F32), ex[2][...].astype(F32)
        s_sb, s_fx = _sigmoid(g[:, :cs]), _sigmoid(g[:, cs:])
        outs[0][...] = (acc * s_sb).astype(BF16)
        outs[1][...] = (acc * s_fx).astype(BF16)
        outs[2][...] = jnp.concatenate([acc * a_sb * s_sb * (1.0 - s_sb), acc * a_fx * s_fx * (1.0 - s_fx)],
                                       axis=1).astype(BF16)

    blk = pl.BlockSpec((tm, cs), lambda i, j, k: (i, j))
    wide = pl.BlockSpec((tm, 2 * cs), lambda i, j, k: (i, j))
    return _matmul(
        "merge_bwd", "nt",
        [(dmix, pl.BlockSpec((tm, tk), lambda i, j, k: (i, k)), w_out, pl.BlockSpec((cs, tk), lambda i, j, k: (j, k)))],
        (s // tm, N_DEV, d // tk), (tm, cs),
        [_sds((s, d), BF16), _sds((s, d), BF16), _sds(gf.shape, BF16)], [blk, blk, wide],
        extras=[(gf, wide), (a_sb, blk), (a_fx, blk)], epilogue=epilogue)


def _ffn_up(u2, w_gate, w_up, tm=1024):
    s, d = u2.shape
    fs = w_gate.shape[2]
    tm = _tile(s, tm)

    def body(u_ref, wg_ref, wu_ref, gate_ref, up_ref, act_ref, actt_ref):
        u = u_ref[...]
        gate = _dot(u, wg_ref[...])
        up = _dot(u, wu_ref[...])
        gate_ref[...] = gate
        up_ref[...] = up
        act = (gate * _sigmoid(gate) * up).astype(BF16)
        act_ref[...] = act
        actt_ref[...] = act.T

    w_spec = pl.BlockSpec((None, d, fs), lambda i, j: (j, 0, 0))
    o_spec = pl.BlockSpec((None, tm, fs), lambda i, j: (j, i, 0))
    return pl.pallas_call(
        body, name="ffn_up", grid=(s // tm, N_DEV),
        in_specs=[pl.BlockSpec((tm, d), lambda i, j: (i, 0)), w_spec, w_spec],
        out_specs=[o_spec, o_spec, o_spec, pl.BlockSpec((None, fs, tm), lambda i, j: (j, 0, i))],
        out_shape=[_sds((N_DEV, s, fs), F32), _sds((N_DEV, s, fs), F32), _sds((N_DEV, s, fs), BF16),
                   _sds((N_DEV, fs, s), BF16)],
        compiler_params=_params(("parallel", "arbitrary")),
    )(u2, w_gate, w_up)


def _ffn_down_bwd(dff, w_down, gate, up, tm=1024):
    s, d = dff.shape
    fs = w_down.shape[1]
    tm = _tile(s, tm)

    def body(dff_ref, wd_ref, gate_ref, up_ref, dgate_ref, dup_ref):
        dact = _dot(dff_ref[...], wd_ref[...], "nt")
        gate = gate_ref[...]
        sg = _sigmoid(gate)
        dup_ref[...] = (dact * gate * sg).astype(BF16)
        dgate_ref[...] = (dact * up_ref[...] * sg * (1.0 + gate * (1.0 - sg))).astype(BF16)

    a_spec = pl.BlockSpec((None, tm, fs), lambda i, j: (j, i, 0))
    return pl.pallas_call(
        body, name="ffn_down_bwd", grid=(s // tm, N_DEV),
        in_specs=[pl.BlockSpec((tm, d), lambda i, j: (i, 0)), pl.BlockSpec((None, fs, d), lambda i, j: (j, 0, 0)),
                  a_spec, a_spec],
        out_specs=[a_spec, a_spec],
        out_shape=[_sds((N_DEV, s, fs), BF16), _sds((N_DEV, s, fs), BF16)],
        compiler_params=_params(("parallel", "arbitrary")),
    )(dff, w_down, gate, up)


def _mesh_place():
    x, y, c = lax.axis_index("x"), lax.axis_index("y"), lax.axis_index("c")
    peers = []
    for d in range(1, N_DEV):
        px = 1 - x if d & 4 else x
        py = 1 - y if d & 2 else y
        pc = 1 - c if d & 1 else c
        peers.append((d, (px, py, pc), 4 * px + 2 * py + pc))
    return 4 * x + 2 * y + c, peers


def _flat_me():
    return 4 * lax.axis_index("x") + 2 * lax.axis_index("y") + lax.axis_index("c")


def _in_hbm(a):
    return pltpu.with_memory_space_constraint(a, pltpu.HBM)


def _scatter_start(name, parts):
    n = len(parts)
    me = _flat_me()
    lands = [lax.dynamic_update_slice_in_dim(lax.empty(a.shape, a.dtype), lax.dynamic_slice_in_dim(a, me, 1, 0), me, 0)
             for a in parts]

    def body(*refs):
        ins, lnd = refs[:n], refs[n:2 * n]
        send, recv = refs[2 * n], refs[2 * n + 1]
        token = refs[-1]
        mine, peers = _mesh_place()
        for a in range(n):
            for d, dev, flat in peers:
                pltpu.make_async_remote_copy(src_ref=ins[a].at[flat], dst_ref=lnd[a].at[mine], send_sem=send.at[a * N_DEV + d],
                                             recv_sem=recv.at[a * N_DEV + d], device_id=dev, device_id_type=MESH).start()
        token[...] = jnp.zeros_like(token)

    res = pl.pallas_call(
        body, name=name,
        out_shape=[pltpu.SemaphoreType.DMA((n * N_DEV,)), pltpu.SemaphoreType.DMA((n * N_DEV,))]
        + [pltpu.HBM(a.shape, a.dtype) for a in parts] * 2 + [_sds((8, LANES), F32)],
        in_specs=[HBM] * (2 * n), out_specs=[SEM, SEM] + [HBM] * (2 * n) + [pl.BlockSpec(memory_space=pltpu.VMEM)],
        input_output_aliases={i: 2 + i for i in range(2 * n)},
        compiler_params=pltpu.CompilerParams(has_side_effects=EFFECT),
    )(*[_in_hbm(a) for a in parts], *[_in_hbm(a) for a in lands])
    return res[0], res[1], res[2:2 + n], res[2 + n:2 + 2 * n], res[-1]


def _scatter_wait(name, send, recv, parts, lands, after):
    n = len(parts)

    def body(*refs):
        ins, lnd = refs[:n], refs[n:2 * n]
        send_sem, recv_sem = refs[2 * n], refs[2 * n + 1]
        mine, peers = _mesh_place()
        for a in range(n):
            for d, dev, flat in peers:
                cp = pltpu.make_async_remote_copy(src_ref=ins[a].at[flat], dst_ref=lnd[a].at[flat],
                                                  send_sem=send_sem.at[a * N_DEV + d], recv_sem=recv_sem.at[a * N_DEV + d],
                                                  device_id=dev, device_id_type=MESH)
                cp.wait_send()
                cp.wait_recv()

    res = pl.pallas_call(
        body, name=name,
        out_shape=[pltpu.HBM(a.shape, a.dtype) for a in parts] * 2,
        in_specs=[HBM] * (2 * n) + [SEM, SEM, ANY], out_specs=[HBM] * (2 * n),
        input_output_aliases={i: i for i in range(2 * n)},
        compiler_params=pltpu.CompilerParams(has_side_effects=EFFECT),
    )(*parts, *lands, send, recv, after)
    return res[n:]


def _gather_targets():
    x, y, c = lax.axis_index("x"), lax.axis_index("y"), lax.axis_index("c")
    chips = [(x, y), (1 - x, y), (x, 1 - y), (1 - x, 1 - y)]
    same = [((cx, cy, c), 4 * cx + 2 * cy + c) for cx, cy in chips]
    other = [((cx, cy, 1 - c), 4 * cx + 2 * cy + 1 - c) for cx, cy in chips]
    return same[0][1], [other[0]] + same[1:], [flat for _, flat in other[1:]], other[0][0]


def _gather_start(shards):
    n = len(shards)
    me = _flat_me()
    lands = [lax.dynamic_update_slice_in_dim(lax.empty((N_DEV,) + a.shape, a.dtype), a[None], me, 0) for a in shards]

    def body(*refs):
        lnd, send, recv, token = refs[:n], refs[n], refs[n + 1], refs[-1]
        mine, targets, _, _ = _gather_targets()
        for a in range(n):
            for t, (dev, _) in enumerate(targets):
                pltpu.make_async_remote_copy(src_ref=lnd[a].at[mine], dst_ref=lnd[a].at[mine], send_sem=send.at[4 * a + t],
                                             recv_sem=recv.at[4 * a + t], device_id=dev, device_id_type=MESH).start()
        token[...] = jnp.zeros_like(token)

    res = pl.pallas_call(
        body, name="gather_start",
        out_shape=[pltpu.SemaphoreType.DMA((4 * n,)), pltpu.SemaphoreType.DMA((4 * n,))]
        + [pltpu.HBM(a.shape, a.dtype) for a in lands] + [_sds((8, LANES), F32)],
        in_specs=[HBM] * n, out_specs=[SEM, SEM] + [HBM] * n + [pl.BlockSpec(memory_space=pltpu.VMEM)],
        input_output_aliases={i: 2 + i for i in range(n)},
        compiler_params=pltpu.CompilerParams(has_side_effects=EFFECT),
    )(*[_in_hbm(a) for a in lands])
    return res[0], res[1], list(res[2:2 + n]), res[-1]


def _gather_forward(name, lands, first, send, recv, after):
    n = len(lands)

    def body(*refs):
        lnd, send_sem, recv_sem = refs[:n], refs[n], refs[n + 1]
        send2, recv2, token = refs[-3], refs[-2], refs[-1]
        mine, targets, _, sibling = _gather_targets()
        for a in range(n):
            for t, (dev, flat) in enumerate(targets):
                cp = pltpu.make_async_remote_copy(src_ref=lnd[a].at[mine], dst_ref=lnd[a].at[flat],
                                                  send_sem=send_sem.at[4 * (first + a) + t],
                                                  recv_sem=recv_sem.at[4 * (first + a) + t], device_id=dev, device_id_type=MESH)
                cp.wait_send()
                if t:
                    cp.wait_recv()
                    pltpu.make_async_remote_copy(src_ref=lnd[a].at[flat], dst_ref=lnd[a].at[flat], send_sem=send2.at[3 * a + t - 1],
                                                 recv_sem=recv2.at[3 * a + t - 1], device_id=sibling, device_id_type=MESH).start()
        token[...] = jnp.zeros_like(token)

    res = pl.pallas_call(
        body, name=name,
        out_shape=[pltpu.HBM(a.shape, a.dtype) for a in lands]
        + [pltpu.SemaphoreType.DMA((3 * n,)), pltpu.SemaphoreType.DMA((3 * n,)), _sds((8, LANES), F32)],
        in_specs=[HBM] * n + [SEM, SEM, ANY], out_specs=[HBM] * n + [SEM, SEM, pl.BlockSpec(memory_space=pltpu.VMEM)],
        input_output_aliases={i: i for i in range(n)},
        compiler_params=pltpu.CompilerParams(has_side_effects=EFFECT),
    )(*lands, send, recv, after)
    return list(res[:n]), res[n], res[n + 1], res[-1]


def _gather_wait(name, lands, first, recv, send2, recv2, after):
    n = len(lands)

    def body(*refs):
        lnd, recv_sem, send2_sem, recv2_sem = refs[:n], refs[n], refs[n + 1], refs[n + 2]
        mine, targets, passed, sibling = _gather_targets()
        for a in range(n):
            dev, flat = targets[0]
            pltpu.make_async_remote_copy(src_ref=lnd[a].at[mine], dst_ref=lnd[a].at[flat], send_sem=send2_sem.at[3 * a],
                                         recv_sem=recv_sem.at[4 * (first + a)], device_id=dev, device_id_type=MESH).wait_recv()
            for t in range(3):
                cp = pltpu.make_async_remote_copy(src_ref=lnd[a].at[targets[t + 1][1]], dst_ref=lnd[a].at[passed[t]],
                                                  send_sem=send2_sem.at[3 * a + t], recv_sem=recv2_sem.at[3 * a + t],
                                                  device_id=sibling, device_id_type=MESH)
                cp.wait_send()
                cp.wait_recv()

    res = pl.pallas_call(
        body, name=name, out_shape=[pltpu.HBM(a.shape, a.dtype) for a in lands],
        in_specs=[HBM] * n + [SEM, SEM, SEM, ANY], out_specs=[HBM] * n,
        input_output_aliases={i: i for i in range(n)},
        compiler_params=pltpu.CompilerParams(has_side_effects=EFFECT),
    )(*lands, recv, send2, recv2, after)
    return list(res)


def _adamw(g, w, m, v):
    m = ADAM_B1 * m + (1.0 - ADAM_B1) * g
    v = ADAM_B2 * v + (1.0 - ADAM_B2) * (g * g)
    m_hat = m / (1.0 - ADAM_B1 ** ADAM_STEP)
    v_hat = v / (1.0 - ADAM_B2 ** ADAM_STEP)
    delta = -ADAM_LR * (m_hat / (jnp.sqrt(v_hat) + ADAM_EPS) + ADAM_WD * w)
    return delta, m, v


def _update(name, parts, w, m, v, layout=None, block_bytes=1 << 20):
    _, r, c = w.shape
    cp = parts.shape[2]
    tr = max(8, min(r, (block_bytes // (4 * cp)) // 8 * 8))
    while r % tr:
        tr -= 8

    def body(p_ref, w_ref, m_ref, v_ref, g_ref, d_ref, nm_ref, nv_ref, *scratch):
        g = p_ref[0].astype(F32)
        for p in range(1, N_DEV):
            g = g + p_ref[p].astype(F32)
        if layout is not None:
            s1, s2, lg = layout.my_shifts()
            lane = lax.broadcasted_iota(jnp.int32, g.shape, 1)
            scratch[0][...] = jnp.where(lane < lg, pltpu.roll(g, cp - s1, 1), pltpu.roll(g, cp - s2, 1))
            g = scratch[0][:, 0:c]
        g_ref[...] = g
        d_ref[...], nm_ref[...], nv_ref[...] = _adamw(g, w_ref[...], m_ref[...], v_ref[...])

    blk = pl.BlockSpec((None, tr, c), lambda i: (0, i, 0))
    return pl.pallas_call(
        body, name=name, grid=(r // tr,),
        in_specs=[pl.BlockSpec((N_DEV, tr, cp), lambda i: (0, i, 0)), blk, blk, blk],
        out_specs=[blk] * 4, out_shape=[_sds((1, r, c), F32)] * 4,
        scratch_shapes=[] if layout is None else [pltpu.VMEM((tr, cp), F32)],
        compiler_params=_params(("parallel",)),
    )(parts, w, m, v)


def _small_update(part, w, m, v):
    n = part.shape[1]

    def body(p_ref, w_ref, m_ref, v_ref, g_ref, d_ref, nm_ref, nv_ref, buf, send, recv):
        me, peers = _mesh_place()
        buf[me] = p_ref[...]
        sent = []
        for d, dev, flat in peers:
            cp = pltpu.make_async_remote_copy(src_ref=p_ref, dst_ref=buf.at[me], send_sem=send.at[d],
                                              recv_sem=recv.at[d], device_id=dev, device_id_type=MESH)
            cp.start()
            sent.append(cp)
        for d, dev, flat in peers:
            pltpu.make_async_remote_copy(src_ref=p_ref, dst_ref=buf.at[flat], send_sem=send.at[d],
                                         recv_sem=recv.at[d], device_id=dev, device_id_type=MESH).wait_recv()
        for cp in sent:
            cp.wait_send()
        g = buf[0]
        for p in range(1, N_DEV):
            g = g + buf[p]
        g_ref[...] = g
        d_ref[...], nm_ref[...], nv_ref[...] = _adamw(g, w_ref[...], m_ref[...], v_ref[...])

    vm = pl.BlockSpec(memory_space=pltpu.VMEM)
    return pl.pallas_call(
        body, name="small_update", in_specs=[vm] * 4, out_specs=[vm] * 4, out_shape=[_sds((1, n), F32)] * 4,
        scratch_shapes=[pltpu.VMEM((N_DEV, 1, n), F32), pltpu.SemaphoreType.DMA((N_DEV,)),
                        pltpu.SemaphoreType.DMA((N_DEV,))],
    )(part, w, m, v)


class _WInLayout:
    def __init__(self, n8, n_f, d_sb, d_fox, d):
        assert n8 % LANES == 1 and n_f < LANES and d % (N_DEV * LANES) == 0
        self.n8, self.n_f, self.d = n8, n_f, d
        self.sp = n8 // LANES
        self.wp = (n8 + 2 * LANES - 2) // LANES * LANES
        self.n_qkv = 3 * (d_sb + d_fox)
        nq, dt, tc = self.n_qkv // LANES, d // LANES, d // N_DEV // LANES
        h_sb, h_fox = d_sb // HEAD_DIM, d_fox // HEAD_DIM
        self.sources = {}
        self.part_tile = {}
        for p in range(N_DEV):
            lg = min(max(self.n_qkv + n_f - n8 * p, 0), n8)
            s1, s2 = p, p + LANES - n_f
            spans = []
            if lg > 0:
                spans.append(("a", self.sp * p, s1 // LANES, (lg + s1 - 1) // LANES))
            if lg < n8:
                spans.append(("g", self.sp * p - 1 - nq, (lg + s2) // LANES, (n8 - 1 + s2) // LANES))
            for kind, base, first, last in spans:
                for i in range(first, last + 1):
                    assert (p, i) not in self.part_tile
                    self.part_tile[(p, i)] = (kind, base + i)
                    self.sources.setdefault((kind, base + i), []).append((p, i))
        self.cat_tiles = [("a", r * h_sb + h) for h in range(h_sb) for r in range(3)]
        self.cat_tiles += [("a", 3 * h_sb + r * h_fox + h) for h in range(h_fox) for r in range(3)]
        self.cat_tiles += [("g", which * dt + j * tc + half) for j in range(N_DEV) for which in (0, 1) for half in range(tc)]
        self.cat_tiles += [("a", nq)] + [None] * (F_PAD // LANES - 1)
        self.cat_index = {key: c for c, key in enumerate(self.cat_tiles) if key is not None}

    def my_shifts(self):
        me = _flat_me()
        return me, me + LANES - self.n_f, jnp.clip(self.n_qkv + self.n_f - self.n8 * me, 0, self.n8)


def _lane_tile(i):
    return pl.ds(i * LANES, LANES)


def _w_in_shift(w_in, lay, tr=256):
    _, d, n8 = w_in.shape

    def body(w_ref, o_ref, buf):
        buf[...] = jnp.zeros_like(buf)
        buf[:, 0:n8] = w_ref[...]
        v = buf[...]
        s1, s2, lg = lay.my_shifts()
        pos = lax.broadcasted_iota(jnp.int32, v.shape, 1)
        o_ref[...] = jnp.where(pos < lg + s1, pltpu.roll(v, s1, 1),
                               jnp.where(pos >= lg + s2, pltpu.roll(v, s2, 1), 0.0)).astype(BF16)

    return pl.pallas_call(
        body, name="w_in_shift", grid=(d // tr,),
        in_specs=[pl.BlockSpec((None, tr, n8), lambda i: (0, i, 0))],
        out_specs=pl.BlockSpec((tr, lay.wp), lambda i: (i, 0)), out_shape=_sds((d, lay.wp), BF16),
        scratch_shapes=[pltpu.VMEM((tr, lay.wp), F32)],
        compiler_params=_params(("parallel",)),
    )(w_in)


def _w_in_build(g_in, lay, tr=256):
    d = g_in.shape[1]
    width = len(lay.cat_tiles) * LANES

    def body(g_ref, o_ref):
        for c, key in enumerate(lay.cat_tiles):
            if key is None:
                o_ref[:, _lane_tile(c)] = jnp.zeros((tr, LANES), BF16)
                continue
            (p, i), *more = lay.sources[key]
            val = g_ref[p, :, _lane_tile(i)]
            for p2, i2 in more:
                val = val + g_ref[p2, :, _lane_tile(i2)]
            o_ref[:, _lane_tile(c)] = val

    return pl.pallas_call(
        body, name="w_in_build", grid=(d // tr,),
        in_specs=[pl.BlockSpec((N_DEV, tr, lay.wp), lambda i: (0, i, 0))],
        out_specs=pl.BlockSpec((tr, width), lambda i: (i, 0)), out_shape=_sds((d, width), BF16),
        compiler_params=_params(("parallel",)),
    )(g_in)


def _w_in_grad_parts(dwq, dwgf, lay, tr=256):
    d = dwq.shape[0]
    nq = lay.n_qkv // LANES

    def body(q_ref, g_ref, o_ref):
        for p in range(N_DEV):
            for i in range(lay.wp // LANES):
                key = lay.part_tile.get((p, i))
                if key is None:
                    o_ref[p, :, _lane_tile(i)] = jnp.zeros((tr, LANES), BF16)
                    continue
                c = lay.cat_index[key]
                o_ref[p, :, _lane_tile(i)] = q_ref[:, _lane_tile(c)] if c < nq else g_ref[:, _lane_tile(c - nq)]

    return pl.pallas_call(
        body, name="w_in_grad_parts", grid=(d // tr,),
        in_specs=[pl.BlockSpec((tr, dwq.shape[1]), lambda i: (i, 0)), pl.BlockSpec((tr, dwgf.shape[1]), lambda i: (i, 0))],
        out_specs=pl.BlockSpec((N_DEV, tr, lay.wp), lambda i: (0, i, 0)), out_shape=_sds((N_DEV, d, lay.wp), BF16),
        compiler_params=_params(("parallel",)),
    )(dwq, dwgf)


def kernel(x, norm_mix_pre, norm_mix_post, w_in, b_forget, w_branch_sb, w_branch_fox, w_out, norm_ffn_pre, norm_ffn_post, w_ffn_gate, w_ffn_up, w_ffn_down, loss_target, m_norm_mix_pre, m_norm_mix_post, m_w_in, m_b_forget, m_w_branch_sb, m_w_branch_fox, m_w_out, m_norm_ffn_pre, m_norm_ffn_post, m_w_ffn_gate, m_w_ffn_up, m_w_ffn_down, v_norm_mix_pre, v_norm_mix_post, v_w_in, v_b_forget, v_w_branch_sb, v_w_branch_fox, v_w_out, v_norm_ffn_pre, v_norm_ffn_post, v_w_ffn_gate, v_w_ffn_up, v_w_ffn_down):
    xs, target = x[0], loss_target[0]
    s, d = xs.shape
    d_sb, d_fox = w_branch_sb.shape[1], w_branch_fox.shape[1]
    h_sb, h_fox = d_sb // HEAD_DIM, d_fox // HEAD_DIM
    n_f = b_forget.shape[1]
    fs = w_ffn_gate.shape[2]
    cs = d // N_DEV
    n_qkv = 3 * (d_sb + d_fox)
    n_gf = 2 * d + F_PAD
    f_blk = 2 * d // LANES
    big = (w_in, w_branch_sb, w_branch_fox, w_out, w_ffn_gate, w_ffn_up, w_ffn_down)
    big_m = (m_w_in, m_w_branch_sb, m_w_branch_fox, m_w_out, m_w_ffn_gate, m_w_ffn_up, m_w_ffn_down)
    big_v = (v_w_in, v_w_branch_sb, v_w_branch_fox, v_w_out, v_w_ffn_gate, v_w_ffn_up, v_w_ffn_down)

    lay = _WInLayout(w_in.shape[2], n_f, d_sb, d_fox, d)
    send1, recv1, lands, token = _gather_start([_w_in_shift(w_in, lay)] + [w[0].astype(BF16) for w in big[1:]])
    b_pad = jnp.pad(b_forget, ((0, 0), (0, LANES - n_f)))

    u, u_t = _pre_norm(xs, norm_mix_pre, dep=token)
    l_in, send2, recv2, token = _gather_forward("gather_in_forward", lands[0:1], 0, send1, recv1, u)
    (g_in,) = _gather_wait("gather_in_wait", l_in, 0, recv1, send2, recv2, token)
    w_cat = _w_in_build(g_in, lay)
    qkv = _mm_plain("proj_qkv", "nn", u, w_cat, BF16, n=n_qkv)
    gf = _mm_plain("proj_gates", "nn", u, w_cat, F32, n_off=n_qkv, n=n_gf)
    cum_col, cum_row = _forget_fwd(gf, b_pad, f_blk)
    o_sb, o_sb_t, tot = _sb_fwd(qkv, h_sb)
    l_mid, send2, recv2, token = _gather_forward("gather_mid_forward", lands[1:4], 1, send1, recv1, o_sb)
    o_fx, o_fx_t, o_fx32, lse = _fox_fwd(qkv, cum_col, cum_row, h_fox, h_sb, token)
    g_sb, g_fx, g_out = _gather_wait("gather_mid_wait", l_mid, 1, recv1, send2, recv2, o_fx)
    w_out_full = g_out.reshape(d, d)
    l_ffn, send2, recv2, token = _gather_forward("gather_ffn_forward", lands[4:7], 4, send1, recv1, o_fx)
    merged, merged_t, a_sb, a_fx = _branch_merge(o_sb, o_fx, g_sb, g_fx, gf, token)
    mix = _mm_plain("out_proj", "nn", merged, w_out_full, F32)
    g_gate, g_up, g_down = _gather_wait("gather_ffn_wait", l_ffn, 4, recv1, send2, recv2, mix)
    h1, u2, u2_t = _mid_norms(xs, mix, norm_mix_post, norm_ffn_pre)
    gate, up, act, act_t = _ffn_up(u2, g_gate, g_up)
    tm, tn = _tile(s, 1024), _tile(d, 1024)
    ff = _matmul("ffn_down", "nn",
                 [(act, pl.BlockSpec((None, tm, fs), lambda i, j, k: (k, i, 0)),
                   g_down, pl.BlockSpec((None, fs, tn), lambda i, j, k: (k, 0, j)))],
                 (s // tm, d // tn, N_DEV), (tm, tn), _sds((s, d), F32), pl.BlockSpec((tm, tn), lambda i, j, k: (i, j)))
    loss_part, dy, dff, dg_ffn_post = _loss_head(h1, ff, target, norm_ffn_post)

    dgate, dup = _ffn_down_bwd(dff, g_down, gate, up)
    dw_down = _matmul("dw_down", "nn",
                      [(act_t, pl.BlockSpec((None, fs, s), lambda j, n, k: (j, 0, 0)),
                        dff, pl.BlockSpec((s, tn), lambda j, n, k: (0, n)))],
                      (N_DEV, d // tn, 1), (fs, tn), _sds((N_DEV, fs, d), BF16),
                      pl.BlockSpec((None, fs, tn), lambda j, n, k: (j, 0, n)))

    def dw_up(name, dact):
        return _matmul(name, "nn",
                       [(u2_t, pl.BlockSpec((tn, s), lambda j, i, k: (i, 0)),
                         dact, pl.BlockSpec((None, s, fs), lambda j, i, k: (j, 0, 0)))],
                       (N_DEV, d // tn, 1), (tn, fs), _sds((N_DEV, d, fs), BF16),
                       pl.BlockSpec((None, tn, fs), lambda j, i, k: (j, i, 0)))

    dw_gate, dw_upw = dw_up("dw_gate", dgate), dw_up("dw_up", dup)
    rs_ffn = _scatter_start("scatter_ffn", [dw_gate, dw_upw, dw_down])
    a_spec = pl.BlockSpec((None, tm, fs), lambda i, j, k: (k, i, 0))
    b_spec = pl.BlockSpec((None, tn, fs), lambda i, j, k: (k, j, 0))
    du2 = _matmul("du2", "nt", [(dgate, a_spec, g_gate, b_spec), (dup, a_spec, g_up, b_spec)],
                  (s // tm, d // tn, N_DEV), (tm, tn), _sds((s, d), F32), pl.BlockSpec((tm, tn), lambda i, j, k: (i, j)),
                  dep=rs_ffn[4])
    dh1, dmix, dg_ffn_pre, dg_mix_post = _mid_norms_bwd(dy, du2, h1, mix, norm_ffn_pre, norm_mix_post)

    da_sb, da_fx, dgf = _merge_bwd(dmix, w_out_full, gf, a_sb, a_fx)
    dw_out = _mm_plain("dw_out", "nn", merged_t, dmix, BF16).reshape(N_DEV, cs, d)

    def branch_bwd(tag, da, w_b, o_t, width):
        tb = _tile(width, 1024)
        do = _matmul("do_" + tag, "nt",
                     [(da, pl.BlockSpec((tm, cs), lambda i, j, k: (i, k)),
                       w_b, pl.BlockSpec((None, tb, cs), lambda i, j, k: (k, j, 0)))],
                     (s // tm, width // tb, N_DEV), (tm, tb), _sds((s, width), BF16),
                     pl.BlockSpec((tm, tb), lambda i, j, k: (i, j)))
        dw = _matmul("dw_" + tag, "nn",
                     [(o_t, pl.BlockSpec((width, s), lambda j, i, k: (0, 0)),
                       da, pl.BlockSpec((s, cs), lambda j, i, k: (0, j)))],
                     (N_DEV, 1, 1), (width, cs), _sds((N_DEV, width, cs), BF16),
                     pl.BlockSpec((None, width, cs), lambda j, i, k: (j, 0, 0)))
        return do, dw

    do_sb, dw_sb = branch_bwd("sb", da_sb, g_sb, o_sb_t, d_sb)
    do_fx, dw_fx = branch_bwd("fox", da_fx, g_fx, o_fx_t, d_fox)

    rs_mid = _scatter_start("scatter_mid", [dw_sb, dw_fx, dw_out])

    dqkv = _sb_bwd(qkv, do_sb, tot, h_sb, rs_mid[4])
    dqkv, dcum = _fox_bwd(dqkv, qkv, do_fx, o_fx32, lse, cum_col, cum_row, h_fox, h_sb)
    dgf, db_part = _forget_bwd(dgf, dcum, gf, b_pad, f_blk)
    dw_in = _w_in_grad_parts(_mm_plain("dw_qkv", "nn", u_t, dqkv, BF16), _mm_plain("dw_gates", "nn", u_t, dgf, BF16), lay)
    rs_in = _scatter_start("scatter_in", [dw_in])
    du = _mm_plain("du_qkv", "nt", dqkv, w_cat, F32, tn=1024, dep=rs_in[4])
    du = _mm_plain("du_gates", "nt", dgf, w_cat, F32, tn=1024, k_off=n_qkv, init=du)
    dx, dg_mix_pre = _pre_norm_bwd(dh1, du, xs, norm_mix_pre)

    upd = {}

    def update_group(tag, rs, names, after):
        parts = _scatter_wait("scatter_" + tag + "_wait", *rs[:4], after=after)
        for nm, p in zip(names, parts):
            w, m, v = weights[nm]
            upd[nm] = _update("update_" + nm, p, w, m, v, layout=lay if nm == "w_in" else None)

    weights = dict(zip(("w_in", "w_branch_sb", "w_branch_fox", "w_out", "w_ffn_gate", "w_ffn_up", "w_ffn_down"),
                       zip(big, big_m, big_v)))
    update_group("ffn", rs_ffn, ("w_ffn_gate", "w_ffn_up", "w_ffn_down"), dx)
    update_group("mid", rs_mid, ("w_branch_sb", "w_branch_fox", "w_out"), upd["w_ffn_down"][0])
    update_group("in", rs_in, ("w_in",), upd["w_out"][0])

    small = ((norm_mix_pre, m_norm_mix_pre, v_norm_mix_pre), (norm_mix_post, m_norm_mix_post, v_norm_mix_post),
             (norm_ffn_pre, m_norm_ffn_pre, v_norm_ffn_pre), (norm_ffn_post, m_norm_ffn_post, v_norm_ffn_post))
    pad_f = ((0, 0), (0, LANES - n_f))
    cat = lambda i: jnp.concatenate([t[i] for t in small] + [jnp.pad((b_forget, m_b_forget, v_b_forget)[i], pad_f)], axis=1)
    sm = _small_update(jnp.concatenate([dg_mix_pre, dg_mix_post, dg_ffn_pre, dg_ffn_post, db_part], axis=1),
                       cat(0), cat(1), cat(2))
    for i, nm in enumerate(("norm_mix_pre", "norm_mix_post", "norm_ffn_pre", "norm_ffn_post")):
        upd[nm] = [o[:, i * d:(i + 1) * d] for o in sm]
    upd["b_forget"] = [o[:, 4 * d:4 * d + n_f] for o in sm]

    loss = lax.psum(loss_part[0, 0], ("x", "y", "c"))
    order = ("norm_mix_pre", "norm_mix_post", "w_in", "b_forget", "w_branch_sb", "w_branch_fox", "w_out",
             "norm_ffn_pre", "norm_ffn_post", "w_ffn_gate", "w_ffn_up", "w_ffn_down")
    return (loss, dx[None]) + tuple(upd[nm][i] for i in range(4) for nm in order)
```

```python
import jax
import jax.numpy as jnp
from jax import lax
from jax.experimental import pallas as pl
from jax.experimental.pallas import tpu as pltpu

F32 = jnp.float32
BF16 = jnp.bfloat16
MESH = pl.DeviceIdType.MESH
ANY = pl.BlockSpec(memory_space=pl.ANY)
HBM = pl.BlockSpec(memory_space=pltpu.HBM)
SEM = pl.BlockSpec(memory_space=pltpu.SEMAPHORE)
EFFECT = pltpu.SideEffectType.DATAFLOW_SIDE_EFFECTING

N_DEV = 8
HEAD_DIM = 128
RMS_EPS = 1e-6
F_PAD = 512
LANES = 128
ATT_TQ = 256
ATT_TK = 256
ATT_HP = 2
NEG_BIG = -1e30
VMEM_LIMIT = 56 * 1024 * 1024

ADAM_LR = 0.001
ADAM_B1 = 0.9
ADAM_B2 = 0.999
ADAM_EPS = 1e-08
ADAM_WD = 0.01
ADAM_STEP = 10

_DIMS = {"nn": ((1,), (0,)), "nt": ((1,), (1,)), "tn": ((0,), (0,))}


def _params(sem):
    return pltpu.CompilerParams(dimension_semantics=sem, vmem_limit_bytes=VMEM_LIMIT)


def _dot(a, b, mode="nn"):
    return lax.dot_general(a.astype(BF16), b.astype(BF16), (_DIMS[mode], ((), ())), preferred_element_type=F32)


def _tile(n, pref):
    if n <= pref:
        return n
    t = (pref // LANES) * LANES
    while n % t:
        t -= LANES
    return t


def _split2(v):
    hi = v.astype(BF16)
    return hi, (v - hi.astype(F32)).astype(BF16)


def _split3(v):
    a = v.astype(BF16)
    r = v - a.astype(F32)
    b = r.astype(BF16)
    return a, b, (r - b.astype(F32)).astype(BF16)


def _tri(n, cmp):
    r = lax.broadcasted_iota(jnp.int32, (n, n), 0)
    c = lax.broadcasted_iota(jnp.int32, (n, n), 1)
    return jnp.where(cmp(r, c), 1.0, 0.0).astype(BF16)


def _lane_pick(v, h):
    lane = lax.broadcasted_iota(jnp.int32, v.shape, 1)
    return jnp.sum(jnp.where(lane == h, v, 0.0), axis=1, keepdims=True)


def _lane_put(ref, rows, h, col):
    old = ref[rows, :]
    lane = lax.broadcasted_iota(jnp.int32, old.shape, 1)
    ref[rows, :] = jnp.where(lane == h, col, old)


def _sigmoid(z):
    return 1.0 / (1.0 + jnp.exp(-z))


def _log_sigmoid(z):
    return jnp.minimum(z, 0.0) - jnp.log(1.0 + jnp.exp(-jnp.abs(z)))


def _sds(shape, dtype):
    return jax.ShapeDtypeStruct(shape, dtype)


def _matmul(name, mode, pairs, grid, acc_shape, out_shape, out_specs, extras=(), epilogue=None, init=None, dep=None):
    n_p, n_e = len(pairs), len(extras)
    nk = grid[-1]
    single = not isinstance(out_shape, (list, tuple))
    n_i = 0 if init is None else 1
    n_d = 0 if dep is None else 1

    one_step = nk == 1 and init is None

    def body(*refs):
        ab = refs[:2 * n_p]
        ex = refs[2 * n_p:2 * n_p + n_e]
        ini = refs[2 * n_p + n_e:2 * n_p + n_e + n_i]
        outs = refs[2 * n_p + n_e + n_i + n_d:len(refs) - (0 if one_step else 1)]

        def finish(total):
            if epilogue is None:
                outs[0][...] = total.astype(outs[0].dtype)
            else:
                epilogue(total, ex, outs)

        t = _dot(ab[0][...], ab[1][...], mode)
        for p in range(1, n_p):
            t = t + _dot(ab[2 * p][...], ab[2 * p + 1][...], mode)
        if one_step:
            finish(t)
            return
        acc = refs[-1]
        k = pl.program_id(len(grid) - 1)

        @pl.when(k == 0)
        def _():
            acc[...] = t if init is None else ini[0][...].astype(F32) + t

        @pl.when(k > 0)
        def _():
            acc[...] += t

        @pl.when(k == nk - 1)
        def _():
            finish(acc[...])

    in_specs = [s for (_, sa, _, sb) in pairs for s in (sa, sb)] + [s for (_, s) in extras]
    args = [v for (a, _, b, _) in pairs for v in (a, b)] + [e for (e, _) in extras]
    if init is not None:
        in_specs.append(init[1])
        args.append(init[0])
    if dep is not None:
        in_specs.append(ANY)
        args.append(dep)
    return pl.pallas_call(
        body, name=name, grid=grid, in_specs=in_specs,
        out_specs=out_specs if single else list(out_specs),
        out_shape=out_shape if single else list(out_shape),
        scratch_shapes=[] if one_step else [pltpu.VMEM(acc_shape, F32)],
        compiler_params=_params(("parallel",) * (len(grid) - 1) + ("arbitrary",)),
    )(*args)


def _mm_plain(name, mode, a, b, out_dtype, *, n_off=0, n=None, k_off=0, tm=1024, tn=1536, tk=2048, init=None, dep=None):
    if mode == "nn":
        (m, kk), nn_ = a.shape, b.shape[1]
    elif mode == "nt":
        (m, kk), nn_ = a.shape, b.shape[0]
    else:
        (kk, m), nn_ = a.shape, b.shape[1]
    n = nn_ if n is None else n
    tm, tn, tk = _tile(m, tm), _tile(n, tn), _tile(kk, tk)
    while n_off % tn or n % tn:
        tn -= LANES
    while k_off % tk or kk % tk:
        tk -= LANES
    off, koff = n_off // tn, k_off // tk
    a_spec = {"nn": pl.BlockSpec((tm, tk), lambda i, j, k: (i, k)),
              "nt": pl.BlockSpec((tm, tk), lambda i, j, k: (i, k)),
              "tn": pl.BlockSpec((tk, tm), lambda i, j, k: (k, i))}[mode]
    b_spec = {"nn": pl.BlockSpec((tk, tn), lambda i, j, k: (k, j + off)),
              "nt": pl.BlockSpec((tn, tk), lambda i, j, k: (j, k + koff)),
              "tn": pl.BlockSpec((tk, tn), lambda i, j, k: (k, j))}[mode]
    o_spec = pl.BlockSpec((tm, tn), lambda i, j, k: (i, j))
    if init is not None:
        init = (init, o_spec)
    return _matmul(name, mode, [(a, a_spec, b, b_spec)], (m // tm, n // tn, kk // tk), (tm, tn),
                   _sds((m, n), out_dtype), o_spec, init=init, dep=dep)


def _rows_call(name, body, ins, outs, s, tr=256, dep=None):
    def spec(v, per_row):
        if per_row == "transposed":
            return pl.BlockSpec((v.shape[0], tr), lambda i: (0, i))
        if per_row:
            return pl.BlockSpec((tr, v.shape[1]), lambda i: (i, 0))
        return pl.BlockSpec(v.shape, lambda i: (0, 0))
    n_in = len(ins)
    deps = [] if dep is None else [dep]

    def with_dep(*refs):
        body(*refs[:n_in], *refs[n_in + len(deps):])

    return pl.pallas_call(
        with_dep, name=name, grid=(s // tr,),
        in_specs=[spec(v, p) for v, p in ins] + [ANY] * len(deps), out_specs=[spec(v, p) for v, p in outs],
        out_shape=[_sds(v.shape, v.dtype) for v, _ in outs],
        compiler_params=_params(("arbitrary",)),
    )(*[v for v, _ in ins], *deps)


def _rsq(v):
    return lax.rsqrt(jnp.mean(v * v, axis=-1, keepdims=True) + RMS_EPS)


def _norm_bwd(dy, v, r, g):
    vh = v * r
    t = dy * g
    dv = r * (t - vh * jnp.mean(t * vh, axis=-1, keepdims=True))
    return dv, jnp.sum(dy * vh, axis=0, keepdims=True)


def _accum(ref, val):
    @pl.when(pl.program_id(0) == 0)
    def _():
        ref[...] = jnp.zeros_like(ref)
    ref[...] += val


def _pre_norm(x, g, dep=None):
    def body(x_ref, g_ref, u_ref, ut_ref):
        v = x_ref[...]
        u = (v * _rsq(v) * g_ref[...]).astype(BF16)
        u_ref[...] = u
        ut_ref[...] = u.T
    s, d = x.shape
    return _rows_call("pre_norm", body, [(x, True), (g, False)],
                      [(_sds((s, d), BF16), True), (_sds((d, s), BF16), "transposed")], s, dep=dep)


def _mid_norms(x, mix, g_post, g_pre):
    def body(x_ref, mix_ref, gp_ref, gn_ref, h_ref, u_ref, ut_ref):
        mv = mix_ref[...]
        h = x_ref[...] + mv * _rsq(mv) * gp_ref[...]
        h_ref[...] = h
        u = (h * _rsq(h) * gn_ref[...]).astype(BF16)
        u_ref[...] = u
        ut_ref[...] = u.T
    s, d = x.shape
    return _rows_call("mid_norms", body, [(x, True), (mix, True), (g_post, False), (g_pre, False)],
                      [(_sds((s, d), F32), True), (_sds((s, d), BF16), True), (_sds((d, s), BF16), "transposed")], s)


def _loss_head(h1, ff, target, g):
    s, d = h1.shape

    def body(h_ref, ff_ref, t_ref, g_ref, loss_ref, dy_ref, dff_ref, dg_ref):
        fv = ff_ref[...]
        r = _rsq(fv)
        err = h_ref[...] + fv * r * g_ref[...] - t_ref[...]
        part = 0.5 * jnp.sum(jnp.mean(err * err, axis=-1, keepdims=True), axis=0, keepdims=True)
        _accum(loss_ref, jnp.broadcast_to(part, loss_ref.shape))
        dy = err * (1.0 / d)
        dy_ref[...] = dy
        dff, dg = _norm_bwd(dy, fv, r, g_ref[...])
        dff_ref[...] = dff.astype(BF16)
        _accum(dg_ref, dg)

    return _rows_call("loss_head", body, [(h1, True), (ff, True), (target, True), (g, False)],
                      [(_sds((1, LANES), F32), False), (_sds((s, d), F32), True),
                       (_sds((s, d), BF16), True), (_sds((1, d), F32), False)], s)


def _mid_norms_bwd(dy, du2, h1, mix, g_pre, g_post):
    s, d = dy.shape

    def body(dy_ref, du_ref, h_ref, mix_ref, gn_ref, gp_ref, dh_ref, dmix_ref, dgn_ref, dgp_ref):
        h = h_ref[...]
        dh, dgn = _norm_bwd(du_ref[...], h, _rsq(h), gn_ref[...])
        dh = dh + dy_ref[...]
        dh_ref[...] = dh
        _accum(dgn_ref, dgn)
        mv = mix_ref[...]
        dmix, dgp = _norm_bwd(dh, mv, _rsq(mv), gp_ref[...])
        dmix_ref[...] = dmix.astype(BF16)
        _accum(dgp_ref, dgp)

    return _rows_call("mid_norms_bwd", body,
                      [(dy, True), (du2, True), (h1, True), (mix, True), (g_pre, False), (g_post, False)],
                      [(_sds((s, d), F32), True), (_sds((s, d), BF16), True),
                       (_sds((1, d), F32), False), (_sds((1, d), F32), False)], s)


def _pre_norm_bwd(dh1, du, x, g, dep=None):
    s, d = x.shape

    def body(dh_ref, du_ref, x_ref, g_ref, dx_ref, dg_ref):
        v = x_ref[...]
        dv, dg = _norm_bwd(du_ref[...], v, _rsq(v), g_ref[...])
        dx_ref[...] = dh_ref[...] + dv
        _accum(dg_ref, dg)

    return _rows_call("pre_norm_bwd", body, [(dh1, True), (du, True), (x, True), (g, False)],
                      [(_sds((s, d), F32), True), (_sds((1, d), F32), False)], s, dep=dep)


def _forget_fwd(gf, b_pad, f_blk):
    s = gf.shape[0]
    tb = ATT_TK
    nb = s // tb

    def body(f_ref, b_ref, col_ref, row_ref):
        incl = _tri(tb, lambda r, c: c <= r)
        carry = jnp.zeros((1, LANES), F32)
        for i in range(nb):
            lf = _log_sigmoid(f_ref[pl.ds(i * tb, tb), :] + b_ref[...])
            parts = _split3(lf)
            cum = carry + _dot(incl, parts[0]) + _dot(incl, parts[1]) + _dot(incl, parts[2])
            col_ref[pl.ds(i * tb, tb), :] = cum
            row_ref[i] = cum.T
            carry = carry + jnp.sum(lf, axis=0, keepdims=True)

    return pl.pallas_call(
        body, name="forget_fwd", grid=(1,),
        in_specs=[pl.BlockSpec((s, LANES), lambda i: (0, f_blk)), pl.BlockSpec((1, LANES), lambda i: (0, 0))],
        out_specs=[pl.BlockSpec((s, LANES), lambda i: (0, 0)), pl.BlockSpec((nb, LANES, tb), lambda i: (0, 0, 0))],
        out_shape=[_sds((s, LANES), F32), _sds((nb, LANES, tb), F32)],
        compiler_params=_params(("arbitrary",)),
    )(gf, b_pad)


def _forget_bwd(dgf, dcum, gf, b_pad, f_blk):
    s = gf.shape[0]
    tb = ATT_TK
    nb = s // tb
    sec = dgf.shape[1] // F_PAD - 1

    def body(dgf_hbm, dc_ref, f_ref, b_ref, out_ref, db_ref):
        del dgf_hbm
        incl = _tri(tb, lambda r, c: c >= r)
        carry = jnp.zeros((1, LANES), F32)
        db = jnp.zeros((1, LANES), F32)
        out_ref[...] = jnp.zeros_like(out_ref)
        for i in reversed(range(nb)):
            dc = dc_ref[pl.ds(i * tb, tb), :]
            parts = _split3(dc)
            dlf = carry + _dot(incl, parts[0]) + _dot(incl, parts[1]) + _dot(incl, parts[2])
            z = f_ref[pl.ds(i * tb, tb), :] + b_ref[...]
            df = dlf * _sigmoid(-z)
            out_ref[pl.ds(i * tb, tb), pl.ds(0, LANES)] = df.astype(BF16)
            db = db + jnp.sum(df, axis=0, keepdims=True)
            carry = carry + jnp.sum(dc, axis=0, keepdims=True)
        db_ref[...] = db

    return pl.pallas_call(
        body, name="forget_bwd", grid=(1,),
        in_specs=[ANY, pl.BlockSpec((s, LANES), lambda i: (0, 0)),
                  pl.BlockSpec((s, LANES), lambda i: (0, f_blk)), pl.BlockSpec((1, LANES), lambda i: (0, 0))],
        out_specs=[pl.BlockSpec((s, F_PAD), lambda i: (0, sec)), pl.BlockSpec((1, LANES), lambda i: (0, 0))],
        out_shape=[_sds(dgf.shape, BF16), _sds((1, LANES), F32)],
        input_output_aliases={0: 0},
        compiler_params=_params(("arbitrary",)),
    )(dgf, dcum, gf, b_pad)


def _diag_mask(strict):
    r = lax.broadcasted_iota(jnp.int32, (ATT_TQ, ATT_TK), 0)
    c = lax.broadcasted_iota(jnp.int32, (ATT_TQ, ATT_TK), 1)
    return c < r if strict else c <= r


def _qkv_specs(hb0, s):
    specs = []
    for j in range(ATT_HP):
        def col(g, j=j):
            return 3 * (hb0 + ATT_HP * g + j)
        specs += [pl.BlockSpec((ATT_TQ, HEAD_DIM), lambda g, i, col=col: (i, col(g))),
                  pl.BlockSpec((s, HEAD_DIM), lambda g, i, col=col: (0, col(g) + 1)),
                  pl.BlockSpec((s, HEAD_DIM), lambda g, i, col=col: (0, col(g) + 2))]
    return specs


def _head_cols(j):
    return pl.ds(j * HEAD_DIM, HEAD_DIM)


def _sb_fwd(qkv, n_heads):
    s = qkv.shape[0]
    scale = HEAD_DIM ** -0.5
    tq, tk = ATT_TQ, ATT_TK
    heads = range(ATT_HP)

    def body(*refs):
        qkv_refs, (o_ref, ot_ref, tot_ref) = refs[:3 * ATT_HP], refs[3 * ATT_HP:]
        g, i = pl.program_id(0), pl.program_id(1)

        @pl.when((g == 0) & (i == 0))
        def _():
            tot_ref[...] = jnp.zeros_like(tot_ref)

        qs = [qkv_refs[3 * j][...] for j in heads]
        upper = _tri(tk, lambda r, c: r > c)

        def tile(kj, carry, mask):
            rows = pl.ds(pl.multiple_of(kj * tk, tk), tk)
            z = [_dot(qs[j], qkv_refs[3 * j + 1][rows, :], "nt") * scale for j in heads]
            lsz = [_log_sigmoid(z[j]) for j in heads]
            lk = [lsz[j] - z[j] if mask is None else jnp.where(mask, lsz[j] - z[j], 0.0) for j in heads]
            parts = [_split2(lk[j]) for j in heads]
            above = [carry[j][0] + _dot(parts[j][0], upper) + _dot(parts[j][1], upper) for j in heads]
            w = [jnp.exp(lsz[j] + above[j]) for j in heads]
            if mask is not None:
                w = [jnp.where(mask, w[j], 0.0) for j in heads]
            return tuple((carry[j][0] + jnp.sum(lk[j], axis=1, keepdims=True),
                          carry[j][1] + _dot(w[j], qkv_refs[3 * j + 2][rows, :])) for j in heads)

        carry = tile(i, tuple((jnp.zeros((tq, 1), F32), jnp.zeros((tq, HEAD_DIM), F32)) for _ in heads), _diag_mask(True))
        carry = lax.fori_loop(0, i, lambda n, cr: tile(i - 1 - n, cr, None), carry)
        q_rows = pl.ds(pl.multiple_of(i * tq, tq), tq)
        for j in heads:
            c, acc = carry[j]
            o = acc.astype(BF16)
            o_ref[:, _head_cols(j)] = o
            ot_ref[_head_cols(j), :] = o.T
            _lane_put(tot_ref, q_rows, ATT_HP * g + j, c)

    wide = ATT_HP * HEAD_DIM
    return pl.pallas_call(
        body, name="sb_fwd", grid=(n_heads // ATT_HP, s // tq),
        in_specs=_qkv_specs(0, s),
        out_specs=[pl.BlockSpec((tq, wide), lambda g, i: (i, g)), pl.BlockSpec((wide, tq), lambda g, i: (g, i)),
                   pl.BlockSpec((s, LANES), lambda g, i: (0, 0))],
        out_shape=[_sds((s, n_heads * HEAD_DIM), BF16), _sds((n_heads * HEAD_DIM, s), BF16), _sds((s, LANES), F32)],
        compiler_params=_params(("arbitrary", "arbitrary")),
    )(*[qkv] * (3 * ATT_HP))


def _sb_bwd(qkv, do, tot, n_heads, dep):
    s = qkv.shape[0]
    scale = HEAD_DIM ** -0.5
    tq, tk = ATT_TQ, ATT_TK
    nq = s // tq
    hd = HEAD_DIM

    heads = range(ATT_HP)

    def body(*refs):
        qkv_refs = refs[:3 * ATT_HP]
        do_ref, tot_ref, _, out_ref, dk_acc, dv_acc = refs[3 * ATT_HP:]
        g, i = pl.program_id(0), pl.program_id(1)

        @pl.when(i == 0)
        def _():
            dk_acc[...] = jnp.zeros_like(dk_acc)
            dv_acc[...] = jnp.zeros_like(dv_acc)

        qs = [qkv_refs[3 * j][...] for j in heads]
        douts = [do_ref[:, _head_cols(j)] for j in heads]
        totals = [_lane_pick(tot_ref[...], ATT_HP * g + j) for j in heads]
        incl = _tri(tk, lambda r, c: r <= c)
        excl = _tri(tk, lambda r, c: r < c)

        def tile(kj, carry, mask):
            rows = pl.ds(pl.multiple_of(kj * tk, tk), tk)
            k_t = [qkv_refs[3 * j + 1][rows, :] for j in heads]
            z = [_dot(qs[j], k_t[j], "nt") * scale for j in heads]
            dw = [_dot(douts[j], qkv_refs[3 * j + 2][rows, :], "nt") for j in heads]
            lsz = [_log_sigmoid(z[j]) for j in heads]
            lk = [lsz[j] - z[j] if mask is None else jnp.where(mask, lsz[j] - z[j], 0.0) for j in heads]
            parts = [_split2(lk[j]) for j in heads]
            below = [carry[j][0] + _dot(parts[j][0], incl) + _dot(parts[j][1], incl) for j in heads]
            w = [jnp.exp(lsz[j] + (totals[j] - below[j])) for j in heads]
            if mask is not None:
                w = [jnp.where(mask, w[j], 0.0) for j in heads]
            e = [dw[j] * w[j] for j in heads]
            parts = [_split2(e[j]) for j in heads]
            e_before = [carry[j][1] + _dot(parts[j][0], excl) + _dot(parts[j][1], excl) for j in heads]
            sg = [jnp.exp(lsz[j]) for j in heads]
            dz = [e[j] * (1.0 - sg[j]) - e_before[j] * sg[j] for j in heads]
            if mask is not None:
                dz = [jnp.where(mask, dz[j], 0.0) for j in heads]
            dz = [(dz[j] * scale).astype(BF16) for j in heads]
            for j in heads:
                dk_acc[j, rows, :] += _dot(dz[j], qs[j], "tn")
                dv_acc[j, rows, :] += _dot(w[j], douts[j], "tn")
            return tuple((carry[j][0] + jnp.sum(lk[j], axis=1, keepdims=True),
                          carry[j][1] + jnp.sum(e[j], axis=1, keepdims=True),
                          carry[j][2] + _dot(dz[j], k_t[j])) for j in heads)

        zero = jnp.zeros((tq, 1), F32)
        carry = lax.fori_loop(0, i, lambda kj, cr: tile(kj, cr, None),
                              tuple((zero, zero, jnp.zeros((tq, hd), F32)) for _ in heads))
        carry = tile(i, carry, _diag_mask(True))
        for j in heads:
            out_ref[pl.ds(pl.multiple_of(i * tq, tq), tq), pl.ds(3 * j * hd, hd)] = carry[j][2].astype(BF16)

        @pl.when(i == nq - 1)
        def _():
            for j in heads:
                out_ref[:, pl.ds((3 * j + 1) * hd, hd)] = dk_acc[j].astype(BF16)
                out_ref[:, pl.ds((3 * j + 2) * hd, hd)] = dv_acc[j].astype(BF16)

    wide = ATT_HP * hd
    return pl.pallas_call(
        body, name="sb_bwd", grid=(n_heads // ATT_HP, nq),
        in_specs=_qkv_specs(0, s) + [pl.BlockSpec((tq, wide), lambda g, i: (i, g)),
                                     pl.BlockSpec((tq, LANES), lambda g, i: (i, 0)), ANY],
        out_specs=pl.BlockSpec((s, 3 * wide), lambda g, i: (0, g)),
        out_shape=_sds(qkv.shape, BF16),
        scratch_shapes=[pltpu.VMEM((ATT_HP, s, hd), F32), pltpu.VMEM((ATT_HP, s, hd), F32)],
        compiler_params=_params(("arbitrary", "arbitrary")),
    )(*[qkv] * (3 * ATT_HP), do, tot, dep)


def _fox_fwd(qkv, cum_col, cum_row, n_heads, hb0, dep):
    s = qkv.shape[0]
    scale = HEAD_DIM ** -0.5
    tq, tk = ATT_TQ, ATT_TK

    heads = range(ATT_HP)

    def body(*refs):
        qkv_refs = refs[:3 * ATT_HP]
        cc_ref, cr_ref, _, o_ref, ot_ref, o32_ref, lse_ref = refs[3 * ATT_HP:]
        g, i = pl.program_id(0), pl.program_id(1)

        @pl.when((g == 0) & (i == 0))
        def _():
            lse_ref[...] = jnp.zeros_like(lse_ref)

        qs = [qkv_refs[3 * j][...] for j in heads]
        cqs = [_lane_pick(cc_ref[...], ATT_HP * g + j) for j in heads]

        def tile(kj, carry, mask):
            rows = pl.ds(pl.multiple_of(kj * tk, tk), tk)
            sc = [_dot(qs[j], qkv_refs[3 * j + 1][rows, :], "nt") * scale + cqs[j]
                  - cr_ref[kj, pl.ds(ATT_HP * g + j, 1), :] for j in heads]
            if mask is not None:
                sc = [jnp.where(mask, sc[j], NEG_BIG) for j in heads]
            m_new = [jnp.maximum(carry[j][0], jnp.max(sc[j], axis=1, keepdims=True)) for j in heads]
            p = [jnp.exp(sc[j] - m_new[j]) for j in heads]
            alpha = [jnp.exp(carry[j][0] - m_new[j]) for j in heads]
            parts = [_split2(p[j]) for j in heads]
            v_t = [qkv_refs[3 * j + 2][rows, :] for j in heads]
            pv = [_dot(parts[j][0], v_t[j]) + _dot(parts[j][1], v_t[j]) for j in heads]
            return tuple((m_new[j], alpha[j] * carry[j][1] + jnp.sum(p[j], axis=1, keepdims=True),
                          alpha[j] * carry[j][2] + pv[j]) for j in heads)

        carry = tuple((jnp.full((tq, 1), NEG_BIG, F32), jnp.zeros((tq, 1), F32), jnp.zeros((tq, HEAD_DIM), F32))
                      for _ in heads)
        carry = lax.fori_loop(0, i, lambda kj, cr: tile(kj, cr, None), carry)
        carry = tile(i, carry, _diag_mask(False))
        q_rows = pl.ds(pl.multiple_of(i * tq, tq), tq)
        for j in heads:
            m, l, acc = carry[j]
            o = acc / l
            o_ref[:, _head_cols(j)] = o.astype(BF16)
            ot_ref[_head_cols(j), :] = o.astype(BF16).T
            o32_ref[:, _head_cols(j)] = o
            _lane_put(lse_ref, q_rows, ATT_HP * g + j, m + jnp.log(l))

    nb = cum_row.shape[0]
    wide = ATT_HP * HEAD_DIM
    return pl.pallas_call(
        body, name="fox_fwd", grid=(n_heads // ATT_HP, s // tq),
        in_specs=_qkv_specs(hb0, s) + [pl.BlockSpec((tq, LANES), lambda g, i: (i, 0)),
                                       pl.BlockSpec((nb, 8, tk), lambda g, i: (0, 0, 0)), ANY],
        out_specs=[pl.BlockSpec((tq, wide), lambda g, i: (i, g)), pl.BlockSpec((wide, tq), lambda g, i: (g, i)),
                   pl.BlockSpec((tq, wide), lambda g, i: (i, g)), pl.BlockSpec((s, LANES), lambda g, i: (0, 0))],
        out_shape=[_sds((s, n_heads * HEAD_DIM), BF16), _sds((n_heads * HEAD_DIM, s), BF16),
                   _sds((s, n_heads * HEAD_DIM), F32), _sds((s, LANES), F32)],
        compiler_params=_params(("arbitrary", "arbitrary")),
    )(*[qkv] * (3 * ATT_HP), cum_col, cum_row, dep)


def _fox_bwd(dqkv, qkv, do, o, lse, cum_col, cum_row, n_heads, hb0):
    s = qkv.shape[0]
    scale = HEAD_DIM ** -0.5
    tq, tk = ATT_TQ, ATT_TK
    nq = s // tq
    hd = HEAD_DIM

    heads = range(ATT_HP)
    assert hb0 % ATT_HP == 0

    def body(*refs):
        qkv_refs = refs[1:1 + 3 * ATT_HP]
        do_ref, o_ref, lse_ref, cc_ref, cr_ref, out_ref, dc_ref, dk_acc, dv_acc, col_acc = refs[1 + 3 * ATT_HP:]
        g, i = pl.program_id(0), pl.program_id(1)

        @pl.when((g == 0) & (i == 0))
        def _():
            dc_ref[...] = jnp.zeros_like(dc_ref)

        @pl.when(i == 0)
        def _():
            dk_acc[...] = jnp.zeros_like(dk_acc)
            dv_acc[...] = jnp.zeros_like(dv_acc)
            col_acc[...] = jnp.zeros_like(col_acc)

        qs = [qkv_refs[3 * j][...] for j in heads]
        douts = [do_ref[:, _head_cols(j)] for j in heads]
        deltas = [jnp.sum(douts[j].astype(F32) * o_ref[:, _head_cols(j)], axis=1, keepdims=True) for j in heads]
        shifts = [_lane_pick(cc_ref[...], ATT_HP * g + j) - _lane_pick(lse_ref[...], ATT_HP * g + j) for j in heads]

        def tile(kj, carry, mask):
            rows = pl.ds(pl.multiple_of(kj * tk, tk), tk)
            k_t = [qkv_refs[3 * j + 1][rows, :] for j in heads]
            sc = [_dot(qs[j], k_t[j], "nt") * scale + shifts[j] - cr_ref[kj, pl.ds(ATT_HP * g + j, 1), :] for j in heads]
            dp = [_dot(douts[j], qkv_refs[3 * j + 2][rows, :], "nt") for j in heads]
            p = [jnp.exp(sc[j]) for j in heads]
            if mask is not None:
                p = [jnp.where(mask, p[j], 0.0) for j in heads]
            ds_f = [p[j] * (dp[j] - deltas[j]) for j in heads]
            ds = [(ds_f[j] * scale).astype(BF16) for j in heads]
            for j in heads:
                col_acc[j, kj] += jnp.broadcast_to(jnp.sum(ds_f[j], axis=0, keepdims=True), (8, tk))
                dk_acc[j, rows, :] += _dot(ds[j], qs[j], "tn")
                dv_acc[j, rows, :] += _dot(p[j], douts[j], "tn")
            return tuple((carry[j][0] + _dot(ds[j], k_t[j]), carry[j][1] + jnp.sum(ds_f[j], axis=1, keepdims=True))
                         for j in heads)

        carry = lax.fori_loop(0, i, lambda kj, cr: tile(kj, cr, None),
                              tuple((jnp.zeros((tq, hd), F32), jnp.zeros((tq, 1), F32)) for _ in heads))
        carry = tile(i, carry, _diag_mask(False))
        q_rows = pl.ds(pl.multiple_of(i * tq, tq), tq)
        for j in heads:
            out_ref[q_rows, pl.ds(3 * j * hd, hd)] = carry[j][0].astype(BF16)
            _lane_put(dc_ref, q_rows, ATT_HP * g + j, carry[j][1])

        @pl.when(i == nq - 1)
        def _():
            lane = lax.broadcasted_iota(jnp.int32, (tk, LANES), 1)
            for j in heads:
                out_ref[:, pl.ds((3 * j + 1) * hd, hd)] = dk_acc[j].astype(BF16)
                out_ref[:, pl.ds((3 * j + 2) * hd, hd)] = dv_acc[j].astype(BF16)
                for kj in range(nb):
                    col = jnp.broadcast_to(col_acc[j, kj][0:1, :], (LANES, tk)).T
                    old = dc_ref[pl.ds(kj * tk, tk), :]
                    dc_ref[pl.ds(kj * tk, tk), :] = jnp.where(lane == ATT_HP * g + j, old - col, old)

    nb = cum_row.shape[0]
    wide = ATT_HP * hd
    return pl.pallas_call(
        body, name="fox_bwd", grid=(n_heads // ATT_HP, nq),
        in_specs=[ANY] + _qkv_specs(hb0, s) + [
            pl.BlockSpec((tq, wide), lambda g, i: (i, g)), pl.BlockSpec((tq, wide), lambda g, i: (i, g)),
            pl.BlockSpec((tq, LANES), lambda g, i: (i, 0)), pl.BlockSpec((tq, LANES), lambda g, i: (i, 0)),
            pl.BlockSpec((nb, 8, tk), lambda g, i: (0, 0, 0))],
        out_specs=[pl.BlockSpec((s, 3 * wide), lambda g, i: (0, hb0 // ATT_HP + g)),
                   pl.BlockSpec((s, LANES), lambda g, i: (0, 0))],
        out_shape=[_sds(dqkv.shape, BF16), _sds((s, LANES), F32)],
        scratch_shapes=[pltpu.VMEM((ATT_HP, s, hd), F32), pltpu.VMEM((ATT_HP, s, hd), F32),
                        pltpu.VMEM((ATT_HP, s // tk, 8, tk), F32)],
        input_output_aliases={0: 0},
        compiler_params=_params(("arbitrary", "arbitrary")),
    )(dqkv, *[qkv] * (3 * ATT_HP), do, o, lse, cum_col, cum_row)


def _branch_merge(o_sb, o_fx, w_sb, w_fx, gf, dep, tm=1024):
    s = o_sb.shape[0]
    cs = w_sb.shape[2]
    tm = _tile(s, tm)

    def body(osb_ref, ofx_ref, wsb_ref, wfx_ref, g_ref, dep_ref, merged_ref, mt_ref, asb_ref, afx_ref):
        del dep_ref
        a_sb = _dot(osb_ref[...], wsb_ref[...])
        a_fx = _dot(ofx_ref[...], wfx_ref[...])
        g = g_ref[...]
        merged = (_sigmoid(g[:, :cs]) * a_sb + _sigmoid(g[:, cs:]) * a_fx).astype(BF16)
        merged_ref[...] = merged
        mt_ref[...] = merged.T
        asb_ref[...] = a_sb.astype(BF16)
        afx_ref[...] = a_fx.astype(BF16)

    blk = pl.BlockSpec((tm, cs), lambda i, j: (i, j))
    out = _sds((s, N_DEV * cs), BF16)
    return pl.pallas_call(
        body, name="branch_merge", grid=(s // tm, N_DEV),
        in_specs=[pl.BlockSpec((tm, o_sb.shape[1]), lambda i, j: (i, 0)),
                  pl.BlockSpec((tm, o_fx.shape[1]), lambda i, j: (i, 0)),
                  pl.BlockSpec((None,) + w_sb.shape[1:], lambda i, j: (j, 0, 0)),
                  pl.BlockSpec((None,) + w_fx.shape[1:], lambda i, j: (j, 0, 0)),
                  pl.BlockSpec((tm, 2 * cs), lambda i, j: (i, j)), ANY],
        out_specs=[blk, pl.BlockSpec((cs, tm), lambda i, j: (j, i)), blk, blk],
        out_shape=[out, _sds((N_DEV * cs, s), BF16), out, out],
        compiler_params=_params(("parallel", "arbitrary")),
    )(o_sb, o_fx, w_sb, w_fx, gf, dep)


def _merge_bwd(dmix, w_out, gf, a_sb, a_fx, tm=1024, tk=2048):
    s, d = dmix.shape
    cs = d // N_DEV
    tm, tk = _tile(s, tm), _tile(d, tk)

    def epilogue(acc, ex, outs):
        g, a_sb, a_fx = ex[0][...], ex[1][...].astype(F32), ex[2][...].astype(F32)
        s_sb, s_fx = _sigmoid(g[:, :cs]), _sigmoid(g[:, cs:])
        outs[0][...] = (acc * s_sb).astype(BF16)
        outs[1][...] = (acc * s_fx).astype(BF16)
        outs[2][...] = jnp.concatenate([acc * a_sb * s_sb * (1.0 - s_sb), acc * a_fx * s_fx * (1.0 - s_fx)],
                                       axis=1).astype(BF16)

    blk = pl.BlockSpec((tm, cs), lambda i, j, k: (i, j))
    wide = pl.BlockSpec((tm, 2 * cs), lambda i, j, k: (i, j))
    return _matmul(
        "merge_bwd", "nt",
        [(dmix, pl.BlockSpec((tm, tk), lambda i, j, k: (i, k)), w_out, pl.BlockSpec((cs, tk), lambda i, j, k: (j, k)))],
        (s // tm, N_DEV, d // tk), (tm, cs),
        [_sds((s, d), BF16), _sds((s, d), BF16), _sds(gf.shape, BF16)], [blk, blk, wide],
        extras=[(gf, wide), (a_sb, blk), (a_fx, blk)], epilogue=epilogue)


def _ffn_up(u2, w_gate, w_up, tm=1024):
    s, d = u2.shape
    fs = w_gate.shape[2]
    tm = _tile(s, tm)

    def body(u_ref, wg_ref, wu_ref, gate_ref, up_ref, act_ref, actt_ref):
        u = u_ref[...]
        gate = _dot(u, wg_ref[...])
        up = _dot(u, wu_ref[...])
        gate_ref[...] = gate
        up_ref[...] = up
        act = (gate * _sigmoid(gate) * up).astype(BF16)
        act_ref[...] = act
        actt_ref[...] = act.T

    w_spec = pl.BlockSpec((None, d, fs), lambda i, j: (j, 0, 0))
    o_spec = pl.BlockSpec((None, tm, fs), lambda i, j: (j, i, 0))
    return pl.pallas_call(
        body, name="ffn_up", grid=(s // tm, N_DEV),
        in_specs=[pl.BlockSpec((tm, d), lambda i, j: (i, 0)), w_spec, w_spec],
        out_specs=[o_spec, o_spec, o_spec, pl.BlockSpec((None, fs, tm), lambda i, j: (j, 0, i))],
        out_shape=[_sds((N_DEV, s, fs), F32), _sds((N_DEV, s, fs), F32), _sds((N_DEV, s, fs), BF16),
                   _sds((N_DEV, fs, s), BF16)],
        compiler_params=_params(("parallel", "arbitrary")),
    )(u2, w_gate, w_up)


def _ffn_down_bwd(dff, w_down, gate, up, tm=1024):
    s, d = dff.shape
    fs = w_down.shape[1]
    tm = _tile(s, tm)

    def body(dff_ref, wd_ref, gate_ref, up_ref, dgate_ref, dup_ref):
        dact = _dot(dff_ref[...], wd_ref[...], "nt")
        gate = gate_ref[...]
        sg = _sigmoid(gate)
        dup_ref[...] = (dact * gate * sg).astype(BF16)
        dgate_ref[...] = (dact * up_ref[...] * sg * (1.0 + gate * (1.0 - sg))).astype(BF16)

    a_spec = pl.BlockSpec((None, tm, fs), lambda i, j: (j, i, 0))
    return pl.pallas_call(
        body, name="ffn_down_bwd", grid=(s // tm, N_DEV),
        in_specs=[pl.BlockSpec((tm, d), lambda i, j: (i, 0)), pl.BlockSpec((None, fs, d), lambda i, j: (j, 0, 0)),
                  a_spec, a_spec],
        out_specs=[a_spec, a_spec],
        out_shape=[_sds((N_DEV, s, fs), BF16), _sds((N_DEV, s, fs), BF16)],
        compiler_params=_params(("parallel", "arbitrary")),
    )(dff, w_down, gate, up)


def _mesh_place():
    x, y, c = lax.axis_index("x"), lax.axis_index("y"), lax.axis_index("c")
    peers = []
    for d in range(1, N_DEV):
        px = 1 - x if d & 4 else x
        py = 1 - y if d & 2 else y
        pc = 1 - c if d & 1 else c
        peers.append((d, (px, py, pc), 4 * px + 2 * py + pc))
    return 4 * x + 2 * y + c, peers


def _flat_me():
    return 4 * lax.axis_index("x") + 2 * lax.axis_index("y") + lax.axis_index("c")


def _in_hbm(a):
    return pltpu.with_memory_space_constraint(a, pltpu.HBM)


def _scatter_start(name, parts):
    n = len(parts)
    me = _flat_me()
    lands = [lax.dynamic_update_slice_in_dim(lax.empty(a.shape, a.dtype), lax.dynamic_slice_in_dim(a, me, 1, 0), me, 0)
             for a in parts]

    def body(*refs):
        ins, lnd = refs[:n], refs[n:2 * n]
        send, recv = refs[2 * n], refs[2 * n + 1]
        token = refs[-1]
        mine, peers = _mesh_place()
        for a in range(n):
            for d, dev, flat in peers:
                pltpu.make_async_remote_copy(src_ref=ins[a].at[flat], dst_ref=lnd[a].at[mine], send_sem=send.at[a * N_DEV + d],
                                             recv_sem=recv.at[a * N_DEV + d], device_id=dev, device_id_type=MESH).start()
        token[...] = jnp.zeros_like(token)

    res = pl.pallas_call(
        body, name=name,
        out_shape=[pltpu.SemaphoreType.DMA((n * N_DEV,)), pltpu.SemaphoreType.DMA((n * N_DEV,))]
        + [pltpu.HBM(a.shape, a.dtype) for a in parts] * 2 + [_sds((8, LANES), F32)],
        in_specs=[HBM] * (2 * n), out_specs=[SEM, SEM] + [HBM] * (2 * n) + [pl.BlockSpec(memory_space=pltpu.VMEM)],
        input_output_aliases={i: 2 + i for i in range(2 * n)},
        compiler_params=pltpu.CompilerParams(has_side_effects=EFFECT),
    )(*[_in_hbm(a) for a in parts], *[_in_hbm(a) for a in lands])
    return res[0], res[1], res[2:2 + n], res[2 + n:2 + 2 * n], res[-1]


def _scatter_wait(name, send, recv, parts, lands, after):
    n = len(parts)

    def body(*refs):
        ins, lnd = refs[:n], refs[n:2 * n]
        send_sem, recv_sem = refs[2 * n], refs[2 * n + 1]
        mine, peers = _mesh_place()
        for a in range(n):
            for d, dev, flat in peers:
                cp = pltpu.make_async_remote_copy(src_ref=ins[a].at[flat], dst_ref=lnd[a].at[flat],
                                                  send_sem=send_sem.at[a * N_DEV + d], recv_sem=recv_sem.at[a * N_DEV + d],
                                                  device_id=dev, device_id_type=MESH)
                cp.wait_send()
                cp.wait_recv()

    res = pl.pallas_call(
        body, name=name,
        out_shape=[pltpu.HBM(a.shape, a.dtype) for a in parts] * 2,
        in_specs=[HBM] * (2 * n) + [SEM, SEM, ANY], out_specs=[HBM] * (2 * n),
        input_output_aliases={i: i for i in range(2 * n)},
        compiler_params=pltpu.CompilerParams(has_side_effects=EFFECT),
    )(*parts, *lands, send, recv, after)
    return res[n:]


def _gather_targets():
    x, y, c = lax.axis_index("x"), lax.axis_index("y"), lax.axis_index("c")
    chips = [(x, y), (1 - x, y), (x, 1 - y), (1 - x, 1 - y)]
    same = [((cx, cy, c), 4 * cx + 2 * cy + c) for cx, cy in chips]
    other = [((cx, cy, 1 - c), 4 * cx + 2 * cy + 1 - c) for cx, cy in chips]
    return same[0][1], [other[0]] + same[1:], [flat for _, flat in other[1:]], other[0][0]


def _gather_start(shards):
    n = len(shards)
    me = _flat_me()
    lands = [lax.dynamic_update_slice_in_dim(lax.empty((N_DEV,) + a.shape, a.dtype), a[None], me, 0) for a in shards]

    def body(*refs):
        lnd, send, recv, token = refs[:n], refs[n], refs[n + 1], refs[-1]
        mine, targets, _, _ = _gather_targets()
        for a in range(n):
            for t, (dev, _) in enumerate(targets):
                pltpu.make_async_remote_copy(src_ref=lnd[a].at[mine], dst_ref=lnd[a].at[mine], send_sem=send.at[4 * a + t],
                                             recv_sem=recv.at[4 * a + t], device_id=dev, device_id_type=MESH).start()
        token[...] = jnp.zeros_like(token)

    res = pl.pallas_call(
        body, name="gather_start",
        out_shape=[pltpu.SemaphoreType.DMA((4 * n,)), pltpu.SemaphoreType.DMA((4 * n,))]
        + [pltpu.HBM(a.shape, a.dtype) for a in lands] + [_sds((8, LANES), F32)],
        in_specs=[HBM] * n, out_specs=[SEM, SEM] + [HBM] * n + [pl.BlockSpec(memory_space=pltpu.VMEM)],
        input_output_aliases={i: 2 + i for i in range(n)},
        compiler_params=pltpu.CompilerParams(has_side_effects=EFFECT),
    )(*[_in_hbm(a) for a in lands])
    return res[0], res[1], list(res[2:2 + n]), res[-1]


def _gather_forward(name, lands, first, send, recv, after):
    n = len(lands)

    def body(*refs):
        lnd, send_sem, recv_sem = refs[:n], refs[n], refs[n + 1]
        send2, recv2, token = refs[-3], refs[-2], refs[-1]
        mine, targets, _, sibling = _gather_targets()
        for a in range(n):
            for t, (dev, flat) in enumerate(targets):
                cp = pltpu.make_async_remote_copy(src_ref=lnd[a].at[mine], dst_ref=lnd[a].at[flat],
                                                  send_sem=send_sem.at[4 * (first + a) + t],
                                                  recv_sem=recv_sem.at[4 * (first + a) + t], device_id=dev, device_id_type=MESH)
                cp.wait_send()
                if t:
                    cp.wait_recv()
                    pltpu.make_async_remote_copy(src_ref=lnd[a].at[flat], dst_ref=lnd[a].at[flat], send_sem=send2.at[3 * a + t - 1],
                                                 recv_sem=recv2.at[3 * a + t - 1], device_id=sibling, device_id_type=MESH).start()
        token[...] = jnp.zeros_like(token)

    res = pl.pallas_call(
        body, name=name,
        out_shape=[pltpu.HBM(a.shape, a.dtype) for a in lands]
        + [pltpu.SemaphoreType.DMA((3 * n,)), pltpu.SemaphoreType.DMA((3 * n,)), _sds((8, LANES), F32)],
        in_specs=[HBM] * n + [SEM, SEM, ANY], out_specs=[HBM] * n + [SEM, SEM, pl.BlockSpec(memory_space=pltpu.VMEM)],
        input_output_aliases={i: i for i in range(n)},
        compiler_params=pltpu.CompilerParams(has_side_effects=EFFECT),
    )(*lands, send, recv, after)
    return list(res[:n]), res[n], res[n + 1], res[-1]


def _gather_wait(name, lands, first, recv, send2, recv2, after):
    n = len(lands)

    def body(*refs):
        lnd, recv_sem, send2_sem, recv2_sem = refs[:n], refs[n], refs[n + 1], refs[n + 2]
        mine, targets, passed, sibling = _gather_targets()
        for a in range(n):
            dev, flat = targets[0]
            pltpu.make_async_remote_copy(src_ref=lnd[a].at[mine], dst_ref=lnd[a].at[flat], send_sem=send2_sem.at[3 * a],
                                         recv_sem=recv_sem.at[4 * (first + a)], device_id=dev, device_id_type=MESH).wait_recv()
            for t in range(3):
                cp = pltpu.make_async_remote_copy(src_ref=lnd[a].at[targets[t + 1][1]], dst_ref=lnd[a].at[passed[t]],
                                                  send_sem=send2_sem.at[3 * a + t], recv_sem=recv2_sem.at[3 * a + t],
                                                  device_id=sibling, device_id_type=MESH)
                cp.wait_send()
                cp.wait_recv()

    res = pl.pallas_call(
        body, name=name, out_shape=[pltpu.HBM(a.shape, a.dtype) for a in lands],
        in_specs=[HBM] * n + [SEM, SEM, SEM, ANY], out_specs=[HBM] * n,
        input_output_aliases={i: i for i in range(n)},
        compiler_params=pltpu.CompilerParams(has_side_effects=EFFECT),
    )(*lands, recv, send2, recv2, after)
    return list(res)


def _adamw(g, w, m, v):
    m = ADAM_B1 * m + (1.0 - ADAM_B1) * g
    v = ADAM_B2 * v + (1.0 - ADAM_B2) * (g * g)
    m_hat = m / (1.0 - ADAM_B1 ** ADAM_STEP)
    v_hat = v / (1.0 - ADAM_B2 ** ADAM_STEP)
    delta = -ADAM_LR * (m_hat / (jnp.sqrt(v_hat) + ADAM_EPS) + ADAM_WD * w)
    return delta, m, v


def _update(name, parts, w, m, v, layout=None, block_bytes=1 << 20):
    _, r, c = w.shape
    cp = parts.shape[2]
    tr = max(8, min(r, (block_bytes // (4 * cp)) // 8 * 8))
    while r % tr:
        tr -= 8

    def body(p_ref, w_ref, m_ref, v_ref, g_ref, d_ref, nm_ref, nv_ref, *scratch):
        g = p_ref[0].astype(F32)
        for p in range(1, N_DEV):
            g = g + p_ref[p].astype(F32)
        if layout is not None:
            s1, s2, lg = layout.my_shifts()
            lane = lax.broadcasted_iota(jnp.int32, g.shape, 1)
            scratch[0][...] = jnp.where(lane < lg, pltpu.roll(g, cp - s1, 1), pltpu.roll(g, cp - s2, 1))
            g = scratch[0][:, 0:c]
        g_ref[...] = g
        d_ref[...], nm_ref[...], nv_ref[...] = _adamw(g, w_ref[...], m_ref[...], v_ref[...])

    blk = pl.BlockSpec((None, tr, c), lambda i: (0, i, 0))
    return pl.pallas_call(
        body, name=name, grid=(r // tr,),
        in_specs=[pl.BlockSpec((N_DEV, tr, cp), lambda i: (0, i, 0)), blk, blk, blk],
        out_specs=[blk] * 4, out_shape=[_sds((1, r, c), F32)] * 4,
        scratch_shapes=[] if layout is None else [pltpu.VMEM((tr, cp), F32)],
        compiler_params=_params(("parallel",)),
    )(parts, w, m, v)


def _small_update(part, w, m, v):
    n = part.shape[1]

    def body(p_ref, w_ref, m_ref, v_ref, g_ref, d_ref, nm_ref, nv_ref, buf, send, recv):
        me, peers = _mesh_place()
        buf[me] = p_ref[...]
        sent = []
        for d, dev, flat in peers:
            cp = pltpu.make_async_remote_copy(src_ref=p_ref, dst_ref=buf.at[me], send_sem=send.at[d],
                                              recv_sem=recv.at[d], device_id=dev, device_id_type=MESH)
            cp.start()
            sent.append(cp)
        for d, dev, flat in peers:
            pltpu.make_async_remote_copy(src_ref=p_ref, dst_ref=buf.at[flat], send_sem=send.at[d],
                                         recv_sem=recv.at[d], device_id=dev, device_id_type=MESH).wait_recv()
        for cp in sent:
            cp.wait_send()
        g = buf[0]
        for p in range(1, N_DEV):
            g = g + buf[p]
        g_ref[...] = g
        d_ref[...], nm_ref[...], nv_ref[...] = _adamw(g, w_ref[...], m_ref[...], v_ref[...])

    vm = pl.BlockSpec(memory_space=pltpu.VMEM)
    return pl.pallas_call(
        body, name="small_update", in_specs=[vm] * 4, out_specs=[vm] * 4, out_shape=[_sds((1, n), F32)] * 4,
        scratch_shapes=[pltpu.VMEM((N_DEV, 1, n), F32), pltpu.SemaphoreType.DMA((N_DEV,)),
                        pltpu.SemaphoreType.DMA((N_DEV,))],
    )(part, w, m, v)


class _WInLayout:
    def __init__(self, n8, n_f, d_sb, d_fox, d):
        assert n8 % LANES == 1 and n_f < LANES and d % (N_DEV * LANES) == 0
        self.n8, self.n_f, self.d = n8, n_f, d
        self.sp = n8 // LANES
        self.wp = (n8 + 2 * LANES - 2) // LANES * LANES
        self.n_qkv = 3 * (d_sb + d_fox)
        nq, dt, tc = self.n_qkv // LANES, d // LANES, d // N_DEV // LANES
        h_sb, h_fox = d_sb // HEAD_DIM, d_fox // HEAD_DIM
        self.sources = {}
        self.part_tile = {}
        for p in range(N_DEV):
            lg = min(max(self.n_qkv + n_f - n8 * p, 0), n8)
            s1, s2 = p, p + LANES - n_f
            spans = []
            if lg > 0:
                spans.append(("a", self.sp * p, s1 // LANES, (lg + s1 - 1) // LANES))
            if lg < n8:
                spans.append(("g", self.sp * p - 1 - nq, (lg + s2) // LANES, (n8 - 1 + s2) // LANES))
            for kind, base, first, last in spans:
                for i in range(first, last + 1):
                    assert (p, i) not in self.part_tile
                    self.part_tile[(p, i)] = (kind, base + i)
                    self.sources.setdefault((kind, base + i), []).append((p, i))
        self.cat_tiles = [("a", r * h_sb + h) for h in range(h_sb) for r in range(3)]
        self.cat_tiles += [("a", 3 * h_sb + r * h_fox + h) for h in range(h_fox) for r in range(3)]
        self.cat_tiles += [("g", which * dt + j * tc + half) for j in range(N_DEV) for which in (0, 1) for half in range(tc)]
        self.cat_tiles += [("a", nq)] + [None] * (F_PAD // LANES - 1)
        self.cat_index = {key: c for c, key in enumerate(self.cat_tiles) if key is not None}

    def my_shifts(self):
        me = _flat_me()
        return me, me + LANES - self.n_f, jnp.clip(self.n_qkv + self.n_f - self.n8 * me, 0, self.n8)


def _lane_tile(i):
    return pl.ds(i * LANES, LANES)


def _w_in_shift(w_in, lay, tr=256):
    _, d, n8 = w_in.shape

    def body(w_ref, o_ref, buf):
        buf[...] = jnp.zeros_like(buf)
        buf[:, 0:n8] = w_ref[...]
        v = buf[...]
        s1, s2, lg = lay.my_shifts()
        pos = lax.broadcasted_iota(jnp.int32, v.shape, 1)
        o_ref[...] = jnp.where(pos < lg + s1, pltpu.roll(v, s1, 1),
                               jnp.where(pos >= lg + s2, pltpu.roll(v, s2, 1), 0.0)).astype(BF16)

    return pl.pallas_call(
        body, name="w_in_shift", grid=(d // tr,),
        in_specs=[pl.BlockSpec((None, tr, n8), lambda i: (0, i, 0))],
        out_specs=pl.BlockSpec((tr, lay.wp), lambda i: (i, 0)), out_shape=_sds((d, lay.wp), BF16),
        scratch_shapes=[pltpu.VMEM((tr, lay.wp), F32)],
        compiler_params=_params(("parallel",)),
    )(w_in)


def _w_in_build(g_in, lay, tr=256):
    d = g_in.shape[1]
    width = len(lay.cat_tiles) * LANES

    def body(g_ref, o_ref):
        for c, key in enumerate(lay.cat_tiles):
            if key is None:
                o_ref[:, _lane_tile(c)] = jnp.zeros((tr, LANES), BF16)
                continue
            (p, i), *more = lay.sources[key]
            val = g_ref[p, :, _lane_tile(i)]
            for p2, i2 in more:
                val = val + g_ref[p2, :, _lane_tile(i2)]
            o_ref[:, _lane_tile(c)] = val

    return pl.pallas_call(
        body, name="w_in_build", grid=(d // tr,),
        in_specs=[pl.BlockSpec((N_DEV, tr, lay.wp), lambda i: (0, i, 0))],
        out_specs=pl.BlockSpec((tr, width), lambda i: (i, 0)), out_shape=_sds((d, width), BF16),
        compiler_params=_params(("parallel",)),
    )(g_in)


def _w_in_grad_parts(dwq, dwgf, lay, tr=256):
    d = dwq.shape[0]
    nq = lay.n_qkv // LANES

    def body(q_ref, g_ref, o_ref):
        for p in range(N_DEV):
            for i in range(lay.wp // LANES):
                key = lay.part_tile.get((p, i))
                if key is None:
                    o_ref[p, :, _lane_tile(i)] = jnp.zeros((tr, LANES), BF16)
                    continue
                c = lay.cat_index[key]
                o_ref[p, :, _lane_tile(i)] = q_ref[:, _lane_tile(c)] if c < nq else g_ref[:, _lane_tile(c - nq)]

    return pl.pallas_call(
        body, name="w_in_grad_parts", grid=(d // tr,),
        in_specs=[pl.BlockSpec((tr, dwq.shape[1]), lambda i: (i, 0)), pl.BlockSpec((tr, dwgf.shape[1]), lambda i: (i, 0))],
        out_specs=pl.BlockSpec((N_DEV, tr, lay.wp), lambda i: (0, i, 0)), out_shape=_sds((N_DEV, d, lay.wp), BF16),
        compiler_params=_params(("parallel",)),
    )(dwq, dwgf)


def kernel(x, norm_mix_pre, norm_mix_post, w_in, b_forget, w_branch_sb, w_branch_fox, w_out, norm_ffn_pre, norm_ffn_post, w_ffn_gate, w_ffn_up, w_ffn_down, loss_target, m_norm_mix_pre, m_norm_mix_post, m_w_in, m_b_forget, m_w_branch_sb, m_w_branch_fox, m_w_out, m_norm_ffn_pre, m_norm_ffn_post, m_w_ffn_gate, m_w_ffn_up, m_w_ffn_down, v_norm_mix_pre, v_norm_mix_post, v_w_in, v_b_forget, v_w_branch_sb, v_w_branch_fox, v_w_out, v_norm_ffn_pre, v_norm_ffn_post, v_w_ffn_gate, v_w_ffn_up, v_w_ffn_down):
    xs, target = x[0], loss_target[0]
    s, d = xs.shape
    d_sb, d_fox = w_branch_sb.shape[1], w_branch_fox.shape[1]
    h_sb, h_fox = d_sb // HEAD_DIM, d_fox // HEAD_DIM
    n_f = b_forget.shape[1]
    fs = w_ffn_gate.shape[2]
    cs = d // N_DEV
    n_qkv = 3 * (d_sb + d_fox)
    n_gf = 2 * d + F_PAD
    f_blk = 2 * d // LANES
    big = (w_in, w_branch_sb, w_branch_fox, w_out, w_ffn_gate, w_ffn_up, w_ffn_down)
    big_m = (m_w_in, m_w_branch_sb, m_w_branch_fox, m_w_out, m_w_ffn_gate, m_w_ffn_up, m_w_ffn_down)
    big_v = (v_w_in, v_w_branch_sb, v_w_branch_fox, v_w_out, v_w_ffn_gate, v_w_ffn_up, v_w_ffn_down)

    lay = _WInLayout(w_in.shape[2], n_f, d_sb, d_fox, d)
    send1, recv1, lands, token = _gather_start([_w_in_shift(w_in, lay)] + [w[0].astype(BF16) for w in big[1:]])
    b_pad = jnp.pad(b_forget, ((0, 0), (0, LANES - n_f)))

    u, u_t = _pre_norm(xs, norm_mix_pre, dep=token)
    l_in, send2, recv2, token = _gather_forward("gather_in_forward", lands[0:1], 0, send1, recv1, u)
    (g_in,) = _gather_wait("gather_in_wait", l_in, 0, recv1, send2, recv2, token)
    w_cat = _w_in_build(g_in, lay)
    qkv = _mm_plain("proj_qkv", "nn", u, w_cat, BF16, n=n_qkv)
    gf = _mm_plain("proj_gates", "nn", u, w_cat, F32, n_off=n_qkv, n=n_gf)
    cum_col, cum_row = _forget_fwd(gf, b_pad, f_blk)
    o_sb, o_sb_t, tot = _sb_fwd(qkv, h_sb)
    l_mid, send2, recv2, token = _gather_forward("gather_mid_forward", lands[1:4], 1, send1, recv1, o_sb)
    o_fx, o_fx_t, o_fx32, lse = _fox_fwd(qkv, cum_col, cum_row, h_fox, h_sb, token)
    g_sb, g_fx, g_out = _gather_wait("gather_mid_wait", l_mid, 1, recv1, send2, recv2, o_fx)
    w_out_full = g_out.reshape(d, d)
    l_ffn, send2, recv2, token = _gather_forward("gather_ffn_forward", lands[4:7], 4, send1, recv1, o_fx)
    merged, merged_t, a_sb, a_fx = _branch_merge(o_sb, o_fx, g_sb, g_fx, gf, token)
    mix = _mm_plain("out_proj", "nn", merged, w_out_full, F32)
    g_gate, g_up, g_down = _gather_wait("gather_ffn_wait", l_ffn, 4, recv1, send2, recv2, mix)
    h1, u2, u2_t = _mid_norms(xs, mix, norm_mix_post, norm_ffn_pre)
    gate, up, act, act_t = _ffn_up(u2, g_gate, g_up)
    tm, tn = _tile(s, 1024), _tile(d, 1024)
    ff = _matmul("ffn_down", "nn",
                 [(act, pl.BlockSpec((None, tm, fs), lambda i, j, k: (k, i, 0)),
                   g_down, pl.BlockSpec((None, fs, tn), lambda i, j, k: (k, 0, j)))],
                 (s // tm, d // tn, N_DEV), (tm, tn), _sds((s, d), F32), pl.BlockSpec((tm, tn), lambda i, j, k: (i, j)))
    loss_part, dy, dff, dg_ffn_post = _loss_head(h1, ff, target, norm_ffn_post)

    dgate, dup = _ffn_down_bwd(dff, g_down, gate, up)
    dw_down = _matmul("dw_down", "nn",
                      [(act_t, pl.BlockSpec((None, fs, s), lambda j, n, k: (j, 0, 0)),
                        dff, pl.BlockSpec((s, tn), lambda j, n, k: (0, n)))],
                      (N_DEV, d // tn, 1), (fs, tn), _sds((N_DEV, fs, d), BF16),
                      pl.BlockSpec((None, fs, tn), lambda j, n, k: (j, 0, n)))

    def dw_up(name, dact):
        return _matmul(name, "nn",
                       [(u2_t, pl.BlockSpec((tn, s), lambda j, i, k: (i, 0)),
                         dact, pl.BlockSpec((None, s, fs), lambda j, i, k: (j, 0, 0)))],
                       (N_DEV, d // tn, 1), (tn, fs), _sds((N_DEV, d, fs), BF16),
                       pl.BlockSpec((None, tn, fs), lambda j, i, k: (j, i, 0)))

    dw_gate, dw_upw = dw_up("dw_gate", dgate), dw_up("dw_up", dup)
    rs_ffn = _scatter_start("scatter_ffn", [dw_gate, dw_upw, dw_down])
    a_spec = pl.BlockSpec((None, tm, fs), lambda i, j, k: (k, i, 0))
    b_spec = pl.BlockSpec((None, tn, fs), lambda i, j, k: (k, j, 0))
    du2 = _matmul("du2", "nt", [(dgate, a_spec, g_gate, b_spec), (dup, a_spec, g_up, b_spec)],
                  (s // tm, d // tn, N_DEV), (tm, tn), _sds((s, d), F32), pl.BlockSpec((tm, tn), lambda i, j, k: (i, j)),
                  dep=rs_ffn[4])
    dh1, dmix, dg_ffn_pre, dg_mix_post = _mid_norms_bwd(dy, du2, h1, mix, norm_ffn_pre, norm_mix_post)

    da_sb, da_fx, dgf = _merge_bwd(dmix, w_out_full, gf, a_sb, a_fx)
    dw_out = _mm_plain("dw_out", "nn", merged_t, dmix, BF16).reshape(N_DEV, cs, d)

    def branch_bwd(tag, da, w_b, o_t, width):
        tb = _tile(width, 1024)
        do = _matmul("do_" + tag, "nt",
                     [(da, pl.BlockSpec((tm, cs), lambda i, j, k: (i, k)),
                       w_b, pl.BlockSpec((None, tb, cs), lambda i, j, k: (k, j, 0)))],
                     (s // tm, width // tb, N_DEV), (tm, tb), _sds((s, width), BF16),
                     pl.BlockSpec((tm, tb), lambda i, j, k: (i, j)))
        dw = _matmul("dw_" + tag, "nn",
                     [(o_t, pl.BlockSpec((width, s), lambda j, i, k: (0, 0)),
                       da, pl.BlockSpec((s, cs), lambda j, i, k: (0, j)))],
                     (N_DEV, 1, 1), (width, cs), _sds((N_DEV, width, cs), BF16),
                     pl.BlockSpec((None, width, cs), lambda j, i, k: (j, 0, 0)))
        return do, dw

    do_sb, dw_sb = branch_bwd("sb", da_sb, g_sb, o_sb_t, d_sb)
    do_fx, dw_fx = branch_bwd("fox", da_fx, g_fx, o_fx_t, d_fox)

    rs_mid = _scatter_start("scatter_mid", [dw_sb, dw_fx, dw_out])

    dqkv = _sb_bwd(qkv, do_sb, tot, h_sb, rs_mid[4])
    dqkv, dcum = _fox_bwd(dqkv, qkv, do_fx, o_fx32, lse, cum_col, cum_row, h_fox, h_sb)
    dgf, db_part = _forget_bwd(dgf, dcum, gf, b_pad, f_blk)
    dw_in = _w_in_grad_parts(_mm_plain("dw_qkv", "nn", u_t, dqkv, BF16), _mm_plain("dw_gates", "nn", u_t, dgf, BF16), lay)
    rs_in = _scatter_start("scatter_in", [dw_in])
    du = _mm_plain("du_qkv", "nt", dqkv, w_cat, F32, tn=1024, dep=rs_in[4])
    du = _mm_plain("du_gates", "nt", dgf, w_cat, F32, tn=1024, k_off=n_qkv, init=du)
    dx, dg_mix_pre = _pre_norm_bwd(dh1, du, xs, norm_mix_pre)

    upd = {}

    def update_group(tag, rs, names, after):
        parts = _scatter_wait("scatter_" + tag + "_wait", *rs[:4], after=after)
        for nm, p in zip(names, parts):
            w, m, v = weights[nm]
            upd[nm] = _update("update_" + nm, p, w, m, v, layout=lay if nm == "w_in" else None)

    weights = dict(zip(("w_in", "w_branch_sb", "w_branch_fox", "w_out", "w_ffn_gate", "w_ffn_up", "w_ffn_down"),
                       zip(big, big_m, big_v)))
    update_group("ffn", rs_ffn, ("w_ffn_gate", "w_ffn_up", "w_ffn_down"), dx)
    update_group("mid", rs_mid, ("w_branch_sb", "w_branch_fox", "w_out"), upd["w_ffn_down"][0])
    update_group("in", rs_in, ("w_in",), upd["w_out"][0])

    small = ((norm_mix_pre, m_norm_mix_pre, v_norm_mix_pre), (norm_mix_post, m_norm_mix_post, v_norm_mix_post),
             (norm_ffn_pre, m_norm_ffn_pre, v_norm_ffn_pre), (norm_ffn_post, m_norm_ffn_post, v_norm_ffn_post))
    pad_f = ((0, 0), (0, LANES - n_f))
    cat = lambda i: jnp.concatenate([t[i] for t in small] + [jnp.pad((b_forget, m_b_forget, v_b_forget)[i], pad_f)], axis=1)
    sm = _small_update(jnp.concatenate([dg_mix_pre, dg_mix_post, dg_ffn_pre, dg_ffn_post, db_part], axis=1),
                       cat(0), cat(1), cat(2))
    for i, nm in enumerate(("norm_mix_pre", "norm_mix_post", "norm_ffn_pre", "norm_ffn_post")):
        upd[nm] = [o[:, i * d:(i + 1) * d] for o in sm]
    upd["b_forget"] = [o[:, 4 * d:4 * d + n_f] for o in sm]

    loss = lax.psum(loss_part[0, 0], ("x", "y", "c"))
    order = ("norm_mix_pre", "norm_mix_post", "w_in", "b_forget", "w_branch_sb", "w_branch_fox", "w_out",
             "norm_ffn_pre", "norm_ffn_post", "w_ffn_gate", "w_ffn_up", "w_ffn_down")
    return (loss, dx[None]) + tuple(upd[nm][i] for i in range(4) for nm in order)
```

```python
import jax
import jax.numpy as jnp
from jax import lax
from jax.experimental import pallas as pl
from jax.experimental.pallas import tpu as pltpu

F32 = jnp.float32
BF16 = jnp.bfloat16
MESH = pl.DeviceIdType.MESH
ANY = pl.BlockSpec(memory_space=pl.ANY)
HBM = pl.BlockSpec(memory_space=pltpu.HBM)
SEM = pl.BlockSpec(memory_space=pltpu.SEMAPHORE)
EFFECT = pltpu.SideEffectType.DATAFLOW_SIDE_EFFECTING

N_DEV = 8
HEAD_DIM = 128
RMS_EPS = 1e-6
F_PAD = 512
LANES = 128
ATT_TQ = 256
ATT_TK = 256
ATT_HP = 2
NEG_BIG = -1e30
VMEM_LIMIT = 56 * 1024 * 1024

ADAM_LR = 0.001
ADAM_B1 = 0.9
ADAM_B2 = 0.999
ADAM_EPS = 1e-08
ADAM_WD = 0.01
ADAM_STEP = 10

_DIMS = {"nn": ((1,), (0,)), "nt": ((1,), (1,)), "tn": ((0,), (0,))}


def _params(sem):
    return pltpu.CompilerParams(dimension_semantics=sem, vmem_limit_bytes=VMEM_LIMIT)


def _dot(a, b, mode="nn"):
    return lax.dot_general(a.astype(BF16), b.astype(BF16), (_DIMS[mode], ((), ())), preferred_element_type=F32)


def _tile(n, pref):
    if n <= pref:
        return n
    t = (pref // LANES) * LANES
    while n % t:
        t -= LANES
    return t


def _split2(v):
    hi = v.astype(BF16)
    return hi, (v - hi.astype(F32)).astype(BF16)


def _split3(v):
    a = v.astype(BF16)
    r = v - a.astype(F32)
    b = r.astype(BF16)
    return a, b, (r - b.astype(F32)).astype(BF16)


def _tri(n, cmp):
    r = lax.broadcasted_iota(jnp.int32, (n, n), 0)
    c = lax.broadcasted_iota(jnp.int32, (n, n), 1)
    return jnp.where(cmp(r, c), 1.0, 0.0).astype(BF16)


def _lane_pick(v, h):
    lane = lax.broadcasted_iota(jnp.int32, v.shape, 1)
    return jnp.sum(jnp.where(lane == h, v, 0.0), axis=1, keepdims=True)


def _lane_put(ref, rows, h, col):
    old = ref[rows, :]
    lane = lax.broadcasted_iota(jnp.int32, old.shape, 1)
    ref[rows, :] = jnp.where(lane == h, col, old)


def _sigmoid(z):
    return 1.0 / (1.0 + jnp.exp(-z))


def _log_sigmoid(z):
    return jnp.minimum(z, 0.0) - jnp.log(1.0 + jnp.exp(-jnp.abs(z)))


def _sds(shape, dtype):
    return jax.ShapeDtypeStruct(shape, dtype)


def _matmul(name, mode, pairs, grid, acc_shape, out_shape, out_specs, extras=(), epilogue=None, init=None, dep=None):
    n_p, n_e = len(pairs), len(extras)
    nk = grid[-1]
    single = not isinstance(out_shape, (list, tuple))
    n_i = 0 if init is None else 1
    n_d = 0 if dep is None else 1

    one_step = nk == 1 and init is None

    def body(*refs):
        ab = refs[:2 * n_p]
        ex = refs[2 * n_p:2 * n_p + n_e]
        ini = refs[2 * n_p + n_e:2 * n_p + n_e + n_i]
        outs = refs[2 * n_p + n_e + n_i + n_d:len(refs) - (0 if one_step else 1)]

        def finish(total):
            if epilogue is None:
                outs[0][...] = total.astype(outs[0].dtype)
            else:
                epilogue(total, ex, outs)

        t = _dot(ab[0][...], ab[1][...], mode)
        for p in range(1, n_p):
            t = t + _dot(ab[2 * p][...], ab[2 * p + 1][...], mode)
        if one_step:
            finish(t)
            return
        acc = refs[-1]
        k = pl.program_id(len(grid) - 1)

        @pl.when(k == 0)
        def _():
            acc[...] = t if init is None else ini[0][...].astype(F32) + t

        @pl.when(k > 0)
        def _():
            acc[...] += t

        @pl.when(k == nk - 1)
        def _():
            finish(acc[...])

    in_specs = [s for (_, sa, _, sb) in pairs for s in (sa, sb)] + [s for (_, s) in extras]
    args = [v for (a, _, b, _) in pairs for v in (a, b)] + [e for (e, _) in extras]
    if init is not None:
        in_specs.append(init[1])
        args.append(init[0])
    if dep is not None:
        in_specs.append(ANY)
        args.append(dep)
    return pl.pallas_call(
        body, name=name, grid=grid, in_specs=in_specs,
        out_specs=out_specs if single else list(out_specs),
        out_shape=out_shape if single else list(out_shape),
        scratch_shapes=[] if one_step else [pltpu.VMEM(acc_shape, F32)],
        compiler_params=_params(("parallel",) * (len(grid) - 1) + ("arbitrary",)),
    )(*args)


def _mm_plain(name, mode, a, b, out_dtype, *, n_off=0, n=None, k_off=0, tm=1024, tn=1536, tk=2048, init=None, dep=None):
    if mode == "nn":
        (m, kk), nn_ = a.shape, b.shape[1]
    elif mode == "nt":
        (m, kk), nn_ = a.shape, b.shape[0]
    else:
        (kk, m), nn_ = a.shape, b.shape[1]
    n = nn_ if n is None else n
    tm, tn, tk = _tile(m, tm), _tile(n, tn), _tile(kk, tk)
    while n_off % tn or n % tn:
        tn -= LANES
    while k_off % tk or kk % tk:
        tk -= LANES
    off, koff = n_off // tn, k_off // tk
    a_spec = {"nn": pl.BlockSpec((tm, tk), lambda i, j, k: (i, k)),
              "nt": pl.BlockSpec((tm, tk), lambda i, j, k: (i, k)),
              "tn": pl.BlockSpec((tk, tm), lambda i, j, k: (k, i))}[mode]
    b_spec = {"nn": pl.BlockSpec((tk, tn), lambda i, j, k: (k, j + off)),
              "nt": pl.BlockSpec((tn, tk), lambda i, j, k: (j, k + koff)),
              "tn": pl.BlockSpec((tk, tn), lambda i, j, k: (k, j))}[mode]
    o_spec = pl.BlockSpec((tm, tn), lambda i, j, k: (i, j))
    if init is not None:
        init = (init, o_spec)
    return _matmul(name, mode, [(a, a_spec, b, b_spec)], (m // tm, n // tn, kk // tk), (tm, tn),
                   _sds((m, n), out_dtype), o_spec, init=init, dep=dep)


def _rows_call(name, body, ins, outs, s, tr=256, dep=None):
    def spec(v, per_row):
        if per_row == "transposed":
            return pl.BlockSpec((v.shape[0], tr), lambda i: (0, i))
        if per_row:
            return pl.BlockSpec((tr, v.shape[1]), lambda i: (i, 0))
        return pl.BlockSpec(v.shape, lambda i: (0, 0))
    n_in = len(ins)
    deps = [] if dep is None else [dep]

    def with_dep(*refs):
        body(*refs[:n_in], *refs[n_in + len(deps):])

    return pl.pallas_call(
        with_dep, name=name, grid=(s // tr,),
        in_specs=[spec(v, p) for v, p in ins] + [ANY] * len(deps), out_specs=[spec(v, p) for v, p in outs],
        out_shape=[_sds(v.shape, v.dtype) for v, _ in outs],
        compiler_params=_params(("arbitrary",)),
    )(*[v for v, _ in ins], *deps)


def _rsq(v):
    return lax.rsqrt(jnp.mean(v * v, axis=-1, keepdims=True) + RMS_EPS)


def _norm_bwd(dy, v, r, g):
    vh = v * r
    t = dy * g
    dv = r * (t - vh * jnp.mean(t * vh, axis=-1, keepdims=True))
    return dv, jnp.sum(dy * vh, axis=0, keepdims=True)


def _accum(ref, val):
    @pl.when(pl.program_id(0) == 0)
    def _():
        ref[...] = jnp.zeros_like(ref)
    ref[...] += val


def _pre_norm(x, g, dep=None):
    def body(x_ref, g_ref, u_ref, ut_ref):
        v = x_ref[...]
        u = (v * _rsq(v) * g_ref[...]).astype(BF16)
        u_ref[...] = u
        ut_ref[...] = u.T
    s, d = x.shape
    return _rows_call("pre_norm", body, [(x, True), (g, False)],
                      [(_sds((s, d), BF16), True), (_sds((d, s), BF16), "transposed")], s, dep=dep)


def _mid_norms(x, mix, g_post, g_pre):
    def body(x_ref, mix_ref, gp_ref, gn_ref, h_ref, u_ref, ut_ref):
        mv = mix_ref[...]
        h = x_ref[...] + mv * _rsq(mv) * gp_ref[...]
        h_ref[...] = h
        u = (h * _rsq(h) * gn_ref[...]).astype(BF16)
        u_ref[...] = u
        ut_ref[...] = u.T
    s, d = x.shape
    return _rows_call("mid_norms", body, [(x, True), (mix, True), (g_post, False), (g_pre, False)],
                      [(_sds((s, d), F32), True), (_sds((s, d), BF16), True), (_sds((d, s), BF16), "transposed")], s)


def _loss_head(h1, ff, target, g):
    s, d = h1.shape

    def body(h_ref, ff_ref, t_ref, g_ref, loss_ref, dy_ref, dff_ref, dg_ref):
        fv = ff_ref[...]
        r = _rsq(fv)
        err = h_ref[...] + fv * r * g_ref[...] - t_ref[...]
        part = 0.5 * jnp.sum(jnp.mean(err * err, axis=-1, keepdims=True), axis=0, keepdims=True)
        _accum(loss_ref, jnp.broadcast_to(part, loss_ref.shape))
        dy = err * (1.0 / d)
        dy_ref[...] = dy
        dff, dg = _norm_bwd(dy, fv, r, g_ref[...])
        dff_ref[...] = dff.astype(BF16)
        _accum(dg_ref, dg)

    return _rows_call("loss_head", body, [(h1, True), (ff, True), (target, True), (g, False)],
                      [(_sds((1, LANES), F32), False), (_sds((s, d), F32), True),
                       (_sds((s, d), BF16), True), (_sds((1, d), F32), False)], s)


def _mid_norms_bwd(dy, du2, h1, mix, g_pre, g_post):
    s, d = dy.shape

    def body(dy_ref, du_ref, h_ref, mix_ref, gn_ref, gp_ref, dh_ref, dmix_ref, dgn_ref, dgp_ref):
        h = h_ref[...]
        dh, dgn = _norm_bwd(du_ref[...], h, _rsq(h), gn_ref[...])
        dh = dh + dy_ref[...]
        dh_ref[...] = dh
        _accum(dgn_ref, dgn)
        mv = mix_ref[...]
        dmix, dgp = _norm_bwd(dh, mv, _rsq(mv), gp_ref[...])
        dmix_ref[...] = dmix.astype(BF16)
        _accum(dgp_ref, dgp)

    return _rows_call("mid_norms_bwd", body,
                      [(dy, True), (du2, True), (h1, True), (mix, True), (g_pre, False), (g_post, False)],
                      [(_sds((s, d), F32), True), (_sds((s, d), BF16), True),
                       (_sds((1, d), F32), False), (_sds((1, d), F32), False)], s)


def _pre_norm_bwd(dh1, du, x, g, dep=None):
    s, d = x.shape

    def body(dh_ref, du_ref, x_ref, g_ref, dx_ref, dg_ref):
        v = x_ref[...]
        dv, dg = _norm_bwd(du_ref[...], v, _rsq(v), g_ref[...])
        dx_ref[...] = dh_ref[...] + dv
        _accum(dg_ref, dg)

    return _rows_call("pre_norm_bwd", body, [(dh1, True), (du, True), (x, True), (g, False)],
                      [(_sds((s, d), F32), True), (_sds((1, d), F32), False)], s, dep=dep)


def _forget_fwd(gf, b_pad, f_blk):
    s = gf.shape[0]
    tb = ATT_TK
    nb = s // tb

    def body(f_ref, b_ref, col_ref, row_ref):
        incl = _tri(tb, lambda r, c: c <= r)
        carry = jnp.zeros((1, LANES), F32)
        for i in range(nb):
            lf = _log_sigmoid(f_ref[pl.ds(i * tb, tb), :] + b_ref[...])
            parts = _split3(lf)
            cum = carry + _dot(incl, parts[0]) + _dot(incl, parts[1]) + _dot(incl, parts[2])
            col_ref[pl.ds(i * tb, tb), :] = cum
            row_ref[i] = cum.T
            carry = carry + jnp.sum(lf, axis=0, keepdims=True)

    return pl.pallas_call(
        body, name="forget_fwd", grid=(1,),
        in_specs=[pl.BlockSpec((s, LANES), lambda i: (0, f_blk)), pl.BlockSpec((1, LANES), lambda i: (0, 0))],
        out_specs=[pl.BlockSpec((s, LANES), lambda i: (0, 0)), pl.BlockSpec((nb, LANES, tb), lambda i: (0, 0, 0))],
        out_shape=[_sds((s, LANES), F32), _sds((nb, LANES, tb), F32)],
        compiler_params=_params(("arbitrary",)),
    )(gf, b_pad)


def _forget_bwd(dgf, dcum, gf, b_pad, f_blk):
    s = gf.shape[0]
    tb = ATT_TK
    nb = s // tb
    sec = dgf.shape[1] // F_PAD - 1

    def body(dgf_hbm, dc_ref, f_ref, b_ref, out_ref, db_ref):
        del dgf_hbm
        incl = _tri(tb, lambda r, c: c >= r)
        carry = jnp.zeros((1, LANES), F32)
        db = jnp.zeros((1, LANES), F32)
        out_ref[...] = jnp.zeros_like(out_ref)
        for i in reversed(range(nb)):
            dc = dc_ref[pl.ds(i * tb, tb), :]
            parts = _split3(dc)
            dlf = carry + _dot(incl, parts[0]) + _dot(incl, parts[1]) + _dot(incl, parts[2])
            z = f_ref[pl.ds(i * tb, tb), :] + b_ref[...]
            df = dlf * _sigmoid(-z)
            out_ref[pl.ds(i * tb, tb), pl.ds(0, LANES)] = df.astype(BF16)
            db = db + jnp.sum(df, axis=0, keepdims=True)
            carry = carry + jnp.sum(dc, axis=0, keepdims=True)
        db_ref[...] = db

    return pl.pallas_call(
        body, name="forget_bwd", grid=(1,),
        in_specs=[ANY, pl.BlockSpec((s, LANES), lambda i: (0, 0)),
                  pl.BlockSpec((s, LANES), lambda i: (0, f_blk)), pl.BlockSpec((1, LANES), lambda i: (0, 0))],
        out_specs=[pl.BlockSpec((s, F_PAD), lambda i: (0, sec)), pl.BlockSpec((1, LANES), lambda i: (0, 0))],
        out_shape=[_sds(dgf.shape, BF16), _sds((1, LANES), F32)],
        input_output_aliases={0: 0},
        compiler_params=_params(("arbitrary",)),
    )(dgf, dcum, gf, b_pad)


def _diag_mask(strict):
    r = lax.broadcasted_iota(jnp.int32, (ATT_TQ, ATT_TK), 0)
    c = lax.broadcasted_iota(jnp.int32, (ATT_TQ, ATT_TK), 1)
    return c < r if strict else c <= r


def _qkv_specs(hb0, s):
    specs = []
    for j in range(ATT_HP):
        def col(g, j=j):
            return 3 * (hb0 + ATT_HP * g + j)
        specs += [pl.BlockSpec((ATT_TQ, HEAD_DIM), lambda g, i, col=col: (i, col(g))),
                  pl.BlockSpec((s, HEAD_DIM), lambda g, i, col=col: (0, col(g) + 1)),
                  pl.BlockSpec((s, HEAD_DIM), lambda g, i, col=col: (0, col(g) + 2))]
    return specs


def _head_cols(j):
    return pl.ds(j * HEAD_DIM, HEAD_DIM)


def _sb_fwd(qkv, n_heads):
    s = qkv.shape[0]
    scale = HEAD_DIM ** -0.5
    tq, tk = ATT_TQ, ATT_TK
    heads = range(ATT_HP)

    def body(*refs):
        qkv_refs, (o_ref, ot_ref, tot_ref) = refs[:3 * ATT_HP], refs[3 * ATT_HP:]
        g, i = pl.program_id(0), pl.program_id(1)

        @pl.when((g == 0) & (i == 0))
        def _():
            tot_ref[...] = jnp.zeros_like(tot_ref)

        qs = [qkv_refs[3 * j][...] for j in heads]
        upper = _tri(tk, lambda r, c: r > c)

        def tile(kj, carry, mask):
            rows = pl.ds(pl.multiple_of(kj * tk, tk), tk)
            z = [_dot(qs[j], qkv_refs[3 * j + 1][rows, :], "nt") * scale for j in heads]
            lsz = [_log_sigmoid(z[j]) for j in heads]
            lk = [lsz[j] - z[j] if mask is None else jnp.where(mask, lsz[j] - z[j], 0.0) for j in heads]
            parts = [_split2(lk[j]) for j in heads]
            above = [carry[j][0] + _dot(parts[j][0], upper) + _dot(parts[j][1], upper) for j in heads]
            w = [jnp.exp(lsz[j] + above[j]) for j in heads]
            if mask is not None:
                w = [jnp.where(mask, w[j], 0.0) for j in heads]
            return tuple((carry[j][0] + jnp.sum(lk[j], axis=1, keepdims=True),
                          carry[j][1] + _dot(w[j], qkv_refs[3 * j + 2][rows, :])) for j in heads)

        carry = tile(i, tuple((jnp.zeros((tq, 1), F32), jnp.zeros((tq, HEAD_DIM), F32)) for _ in heads), _diag_mask(True))
        carry = lax.fori_loop(0, i, lambda n, cr: tile(i - 1 - n, cr, None), carry)
        q_rows = pl.ds(pl.multiple_of(i * tq, tq), tq)
        for j in heads:
            c, acc = carry[j]
            o = acc.astype(BF16)
            o_ref[:, _head_cols(j)] = o
            ot_ref[_head_cols(j), :] = o.T
            _lane_put(tot_ref, q_rows, ATT_HP * g + j, c)

    wide = ATT_HP * HEAD_DIM
    return pl.pallas_call(
        body, name="sb_fwd", grid=(n_heads // ATT_HP, s // tq),
        in_specs=_qkv_specs(0, s),
        out_specs=[pl.BlockSpec((tq, wide), lambda g, i: (i, g)), pl.BlockSpec((wide, tq), lambda g, i: (g, i)),
                   pl.BlockSpec((s, LANES), lambda g, i: (0, 0))],
        out_shape=[_sds((s, n_heads * HEAD_DIM), BF16), _sds((n_heads * HEAD_DIM, s), BF16), _sds((s, LANES), F32)],
        compiler_params=_params(("arbitrary", "arbitrary")),
    )(*[qkv] * (3 * ATT_HP))


def _sb_bwd(qkv, do, tot, n_heads, dep):
    s = qkv.shape[0]
    scale = HEAD_DIM ** -0.5
    tq, tk = ATT_TQ, ATT_TK
    nq = s // tq
    hd = HEAD_DIM

    heads = range(ATT_HP)

    def body(*refs):
        qkv_refs = refs[:3 * ATT_HP]
        do_ref, tot_ref, _, out_ref, dk_acc, dv_acc = refs[3 * ATT_HP:]
        g, i = pl.program_id(0), pl.program_id(1)

        @pl.when(i == 0)
        def _():
            dk_acc[...] = jnp.zeros_like(dk_acc)
            dv_acc[...] = jnp.zeros_like(dv_acc)

        qs = [qkv_refs[3 * j][...] for j in heads]
        douts = [do_ref[:, _head_cols(j)] for j in heads]
        totals = [_lane_pick(tot_ref[...], ATT_HP * g + j) for j in heads]
        incl = _tri(tk, lambda r, c: r <= c)
        excl = _tri(tk, lambda r, c: r < c)

        def tile(kj, carry, mask):
            rows = pl.ds(pl.multiple_of(kj * tk, tk), tk)
            k_t = [qkv_refs[3 * j + 1][rows, :] for j in heads]
            z = [_dot(qs[j], k_t[j], "nt") * scale for j in heads]
            dw = [_dot(douts[j], qkv_refs[3 * j + 2][rows, :], "nt") for j in heads]
            lsz = [_log_sigmoid(z[j]) for j in heads]
            lk = [lsz[j] - z[j] if mask is None else jnp.where(mask, lsz[j] - z[j], 0.0) for j in heads]
            parts = [_split2(lk[j]) for j in heads]
            below = [carry[j][0] + _dot(parts[j][0], incl) + _dot(parts[j][1], incl) for j in heads]
            w = [jnp.exp(lsz[j] + (totals[j] - below[j])) for j in heads]
            if mask is not None:
                w = [jnp.where(mask, w[j], 0.0) for j in heads]
            e = [dw[j] * w[j] for j in heads]
            parts = [_split2(e[j]) for j in heads]
            e_before = [carry[j][1] + _dot(parts[j][0], excl) + _dot(parts[j][1], excl) for j in heads]
            sg = [jnp.exp(lsz[j]) for j in heads]
            dz = [e[j] * (1.0 - sg[j]) - e_before[j] * sg[j] for j in heads]
            if mask is not None:
                dz = [jnp.where(mask, dz[j], 0.0) for j in heads]
            dz = [(dz[j] * scale).astype(BF16) for j in heads]
            for j in heads:
                dk_acc[j, rows, :] += _dot(dz[j], qs[j], "tn")
                dv_acc[j, rows, :] += _dot(w[j], douts[j], "tn")
            return tuple((carry[j][0] + jnp.sum(lk[j], axis=1, keepdims=True),
                          carry[j][1] + jnp.sum(e[j], axis=1, keepdims=True),
                          carry[j][2] + _dot(dz[j], k_t[j])) for j in heads)

        zero = jnp.zeros((tq, 1), F32)
        carry = lax.fori_loop(0, i, lambda kj, cr: tile(kj, cr, None),
                              tuple((zero, zero, jnp.zeros((tq, hd), F32)) for _ in heads))
        carry = tile(i, carry, _diag_mask(True))
        for j in heads:
            out_ref[pl.ds(pl.multiple_of(i * tq, tq), tq), pl.ds(3 * j * hd, hd)] = carry[j][2].astype(BF16)

        @pl.when(i == nq - 1)
        def _():
            for j in heads:
                out_ref[:, pl.ds((3 * j + 1) * hd, hd)] = dk_acc[j].astype(BF16)
                out_ref[:, pl.ds((3 * j + 2) * hd, hd)] = dv_acc[j].astype(BF16)

    wide = ATT_HP * hd
    return pl.pallas_call(
        body, name="sb_bwd", grid=(n_heads // ATT_HP, nq),
        in_specs=_qkv_specs(0, s) + [pl.BlockSpec((tq, wide), lambda g, i: (i, g)),
                                     pl.BlockSpec((tq, LANES), lambda g, i: (i, 0)), ANY],
        out_specs=pl.BlockSpec((s, 3 * wide), lambda g, i: (0, g)),
        out_shape=_sds(qkv.shape, BF16),
        scratch_shapes=[pltpu.VMEM((ATT_HP, s, hd), F32), pltpu.VMEM((ATT_HP, s, hd), F32)],
        compiler_params=_params(("arbitrary", "arbitrary")),
    )(*[qkv] * (3 * ATT_HP), do, tot, dep)


def _fox_fwd(qkv, cum_col, cum_row, n_heads, hb0, dep):
    s = qkv.shape[0]
    scale = HEAD_DIM ** -0.5
    tq, tk = ATT_TQ, ATT_TK

    heads = range(ATT_HP)

    def body(*refs):
        qkv_refs = refs[:3 * ATT_HP]
        cc_ref, cr_ref, _, o_ref, ot_ref, o32_ref, lse_ref = refs[3 * ATT_HP:]
        g, i = pl.program_id(0), pl.program_id(1)

        @pl.when((g == 0) & (i == 0))
        def _():
            lse_ref[...] = jnp.zeros_like(lse_ref)

        qs = [qkv_refs[3 * j][...] for j in heads]
        cqs = [_lane_pick(cc_ref[...], ATT_HP * g + j) for j in heads]

        def tile(kj, carry, mask):
            rows = pl.ds(pl.multiple_of(kj * tk, tk), tk)
            sc = [_dot(qs[j], qkv_refs[3 * j + 1][rows, :], "nt") * scale + cqs[j]
                  - cr_ref[kj, pl.ds(ATT_HP * g + j, 1), :] for j in heads]
            if mask is not None:
                sc = [jnp.where(mask, sc[j], NEG_BIG) for j in heads]
            m_new = [jnp.maximum(carry[j][0], jnp.max(sc[j], axis=1, keepdims=True)) for j in heads]
            p = [jnp.exp(sc[j] - m_new[j]) for j in heads]
            alpha = [jnp.exp(carry[j][0] - m_new[j]) for j in heads]
            parts = [_split2(p[j]) for j in heads]
            v_t = [qkv_refs[3 * j + 2][rows, :] for j in heads]
            pv = [_dot(parts[j][0], v_t[j]) + _dot(parts[j][1], v_t[j]) for j in heads]
            return tuple((m_new[j], alpha[j] * carry[j][1] + jnp.sum(p[j], axis=1, keepdims=True),
                          alpha[j] * carry[j][2] + pv[j]) for j in heads)

        carry = tuple((jnp.full((tq, 1), NEG_BIG, F32), jnp.zeros((tq, 1), F32), jnp.zeros((tq, HEAD_DIM), F32))
                      for _ in heads)
        carry = lax.fori_loop(0, i, lambda kj, cr: tile(kj, cr, None), carry)
        carry = tile(i, carry, _diag_mask(False))
        q_rows = pl.ds(pl.multiple_of(i * tq, tq), tq)
        for j in heads:
            m, l, acc = carry[j]
            o = acc / l
            o_ref[:, _head_cols(j)] = o.astype(BF16)
            ot_ref[_head_cols(j), :] = o.astype(BF16).T
            o32_ref[:, _head_cols(j)] = o
            _lane_put(lse_ref, q_rows, ATT_HP * g + j, m + jnp.log(l))

    nb = cum_row.shape[0]
    wide = ATT_HP * HEAD_DIM
    return pl.pallas_call(
        body, name="fox_fwd", grid=(n_heads // ATT_HP, s // tq),
        in_specs=_qkv_specs(hb0, s) + [pl.BlockSpec((tq, LANES), lambda g, i: (i, 0)),
                                       pl.BlockSpec((nb, 8, tk), lambda g, i: (0, 0, 0)), ANY],
        out_specs=[pl.BlockSpec((tq, wide), lambda g, i: (i, g)), pl.BlockSpec((wide, tq), lambda g, i: (g, i)),
                   pl.BlockSpec((tq, wide), lambda g, i: (i, g)), pl.BlockSpec((s, LANES), lambda g, i: (0, 0))],
        out_shape=[_sds((s, n_heads * HEAD_DIM), BF16), _sds((n_heads * HEAD_DIM, s), BF16),
                   _sds((s, n_heads * HEAD_DIM), F32), _sds((s, LANES), F32)],
        compiler_params=_params(("arbitrary", "arbitrary")),
    )(*[qkv] * (3 * ATT_HP), cum_col, cum_row, dep)


def _fox_bwd(dqkv, qkv, do, o, lse, cum_col, cum_row, n_heads, hb0, dep):
    s = qkv.shape[0]
    scale = HEAD_DIM ** -0.5
    tq, tk = ATT_TQ, ATT_TK
    nq = s // tq
    hd = HEAD_DIM

    heads = range(ATT_HP)
    assert hb0 % ATT_HP == 0

    def body(*refs):
        qkv_refs = refs[1:1 + 3 * ATT_HP]
        do_ref, o_ref, lse_ref, cc_ref, cr_ref, _, out_ref, dc_ref, dk_acc, dv_acc, col_acc = refs[1 + 3 * ATT_HP:]
        g, i = pl.program_id(0), pl.program_id(1)

        @pl.when((g == 0) & (i == 0))
        def _():
            dc_ref[...] = jnp.zeros_like(dc_ref)

        @pl.when(i == 0)
        def _():
            dk_acc[...] = jnp.zeros_like(dk_acc)
            dv_acc[...] = jnp.zeros_like(dv_acc)
            col_acc[...] = jnp.zeros_like(col_acc)

        qs = [qkv_refs[3 * j][...] for j in heads]
        douts = [do_ref[:, _head_cols(j)] for j in heads]
        deltas = [jnp.sum(douts[j].astype(F32) * o_ref[:, _head_cols(j)], axis=1, keepdims=True) for j in heads]
        shifts = [_lane_pick(cc_ref[...], ATT_HP * g + j) - _lane_pick(lse_ref[...], ATT_HP * g + j) for j in heads]

        def tile(kj, carry, mask):
            rows = pl.ds(pl.multiple_of(kj * tk, tk), tk)
            k_t = [qkv_refs[3 * j + 1][rows, :] for j in heads]
            sc = [_dot(qs[j], k_t[j], "nt") * scale + shifts[j] - cr_ref[kj, pl.ds(ATT_HP * g + j, 1), :] for j in heads]
            dp = [_dot(douts[j], qkv_refs[3 * j + 2][rows, :], "nt") for j in heads]
            p = [jnp.exp(sc[j]) for j in heads]
            if mask is not None:
                p = [jnp.where(mask, p[j], 0.0) for j in heads]
            ds_f = [p[j] * (dp[j] - deltas[j]) for j in heads]
            ds = [(ds_f[j] * scale).astype(BF16) for j in heads]
            for j in heads:
                col_acc[j, kj] += jnp.broadcast_to(jnp.sum(ds_f[j], axis=0, keepdims=True), (8, tk))
                dk_acc[j, rows, :] += _dot(ds[j], qs[j], "tn")
                dv_acc[j, rows, :] += _dot(p[j], douts[j], "tn")
            return tuple((carry[j][0] + _dot(ds[j], k_t[j]), carry[j][1] + jnp.sum(ds_f[j], axis=1, keepdims=True))
                         for j in heads)

        carry = lax.fori_loop(0, i, lambda kj, cr: tile(kj, cr, None),
                              tuple((jnp.zeros((tq, hd), F32), jnp.zeros((tq, 1), F32)) for _ in heads))
        carry = tile(i, carry, _diag_mask(False))
        q_rows = pl.ds(pl.multiple_of(i * tq, tq), tq)
        for j in heads:
            out_ref[q_rows, pl.ds(3 * j * hd, hd)] = carry[j][0].astype(BF16)
            _lane_put(dc_ref, q_rows, ATT_HP * g + j, carry[j][1])

        @pl.when(i == nq - 1)
        def _():
            lane = lax.broadcasted_iota(jnp.int32, (tk, LANES), 1)
            for j in heads:
                out_ref[:, pl.ds((3 * j + 1) * hd, hd)] = dk_acc[j].astype(BF16)
                out_ref[:, pl.ds((3 * j + 2) * hd, hd)] = dv_acc[j].astype(BF16)
                for kj in range(nb):
                    col = jnp.broadcast_to(col_acc[j, kj][0:1, :], (LANES, tk)).T
                    old = dc_ref[pl.ds(kj * tk, tk), :]
                    dc_ref[pl.ds(kj * tk, tk), :] = jnp.where(lane == ATT_HP * g + j, old - col, old)

    nb = cum_row.shape[0]
    wide = ATT_HP * hd
    return pl.pallas_call(
        body, name="fox_bwd", grid=(n_heads // ATT_HP, nq),
        in_specs=[ANY] + _qkv_specs(hb0, s) + [
            pl.BlockSpec((tq, wide), lambda g, i: (i, g)), pl.BlockSpec((tq, wide), lambda g, i: (i, g)),
            pl.BlockSpec((tq, LANES), lambda g, i: (i, 0)), pl.BlockSpec((tq, LANES), lambda g, i: (i, 0)),
            pl.BlockSpec((nb, 8, tk), lambda g, i: (0, 0, 0)), ANY],
        out_specs=[pl.BlockSpec((s, 3 * wide), lambda g, i: (0, hb0 // ATT_HP + g)),
                   pl.BlockSpec((s, LANES), lambda g, i: (0, 0))],
        out_shape=[_sds(dqkv.shape, BF16), _sds((s, LANES), F32)],
        scratch_shapes=[pltpu.VMEM((ATT_HP, s, hd), F32), pltpu.VMEM((ATT_HP, s, hd), F32),
                        pltpu.VMEM((ATT_HP, s // tk, 8, tk), F32)],
        input_output_aliases={0: 0},
        compiler_params=_params(("arbitrary", "arbitrary")),
    )(dqkv, *[qkv] * (3 * ATT_HP), do, o, lse, cum_col, cum_row, dep)


def _branch_merge(o_sb, o_fx, w_sb, w_fx, gf, dep, tm=1024):
    s = o_sb.shape[0]
    cs = w_sb.shape[2]
    tm = _tile(s, tm)

    def body(osb_ref, ofx_ref, wsb_ref, wfx_ref, g_ref, dep_ref, merged_ref, mt_ref, asb_ref, afx_ref):
        del dep_ref
        a_sb = _dot(osb_ref[...], wsb_ref[...])
        a_fx = _dot(ofx_ref[...], wfx_ref[...])
        g = g_ref[...]
        merged = (_sigmoid(g[:, :cs]) * a_sb + _sigmoid(g[:, cs:]) * a_fx).astype(BF16)
        merged_ref[...] = merged
        mt_ref[...] = merged.T
        asb_ref[...] = a_sb.astype(BF16)
        afx_ref[...] = a_fx.astype(BF16)

    blk = pl.BlockSpec((tm, cs), lambda i, j: (i, j))
    out = _sds((s, N_DEV * cs), BF16)
    return pl.pallas_call(
        body, name="branch_merge", grid=(s // tm, N_DEV),
        in_specs=[pl.BlockSpec((tm, o_sb.shape[1]), lambda i, j: (i, 0)),
                  pl.BlockSpec((tm, o_fx.shape[1]), lambda i, j: (i, 0)),
                  pl.BlockSpec((None,) + w_sb.shape[1:], lambda i, j: (j, 0, 0)),
                  pl.BlockSpec((None,) + w_fx.shape[1:], lambda i, j: (j, 0, 0)),
                  pl.BlockSpec((tm, 2 * cs), lambda i, j: (i, j)), ANY],
        out_specs=[blk, pl.BlockSpec((cs, tm), lambda i, j: (j, i)), blk, blk],
        out_shape=[out, _sds((N_DEV * cs, s), BF16), out, out],
        compiler_params=_params(("parallel", "arbitrary")),
    )(o_sb, o_fx, w_sb, w_fx, gf, dep)


def _merge_bwd(dmix, w_out, gf, a_sb, a_fx, tm=1024, tk=2048, dep=None):
    s, d = dmix.shape
    cs = d // N_DEV
    tm, tk = _tile(s, tm), _tile(d, tk)

    def epilogue(acc, ex, outs):
        g, a_sb, a_fx = ex[0][...], ex[1][...].astype(F32), ex[2][...].astype(F32)
        s_sb, s_fx = _sigmoid(g[:, :cs]), _sigmoid(g[:, cs:])
        outs[0][...] = (acc * s_sb).astype(BF16)
        outs[1][...] = (acc * s_fx).astype(BF16)
        outs[2][...] = jnp.concatenate([acc * a_sb * s_sb * (1.0 - s_sb), acc * a_fx * s_fx * (1.0 - s_fx)],
                                       axis=1).astype(BF16)

    blk = pl.BlockSpec((tm, cs), lambda i, j, k: (i, j))
    wide = pl.BlockSpec((tm, 2 * cs), lambda i, j, k: (i, j))
    return _matmul(
        "merge_bwd", "nt",
        [(dmix, pl.BlockSpec((tm, tk), lambda i, j, k: (i, k)), w_out, pl.BlockSpec((cs, tk), lambda i, j, k: (j, k)))],
        (s // tm, N_DEV, d // tk), (tm, cs),
        [_sds((s, d), BF16), _sds((s, d), BF16), _sds(gf.shape, BF16)], [blk, blk, wide],
        extras=[(gf, wide), (a_sb, blk), (a_fx, blk)], epilogue=epilogue, dep=dep)


def _ffn_up(u2, w_gate, w_up, tm=1024):
    s, d = u2.shape
    fs = w_gate.shape[2]
    tm = _tile(s, tm)

    def body(u_ref, wg_ref, wu_ref, gate_ref, up_ref, act_ref, actt_ref):
        u = u_ref[...]
        gate = _dot(u, wg_ref[...])
        up = _dot(u, wu_ref[...])
        gate_ref[...] = gate
        up_ref[...] = up
        act = (gate * _sigmoid(gate) * up).astype(BF16)
        act_ref[...] = act
        actt_ref[...] = act.T

    w_spec = pl.BlockSpec((None, d, fs), lambda i, j: (j, 0, 0))
    o_spec = pl.BlockSpec((None, tm, fs), lambda i, j: (j, i, 0))
    return pl.pallas_call(
        body, name="ffn_up", grid=(s // tm, N_DEV),
        in_specs=[pl.BlockSpec((tm, d), lambda i, j: (i, 0)), w_spec, w_spec],
        out_specs=[o_spec, o_spec, o_spec, pl.BlockSpec((None, fs, tm), lambda i, j: (j, 0, i))],
        out_shape=[_sds((N_DEV, s, fs), F32), _sds((N_DEV, s, fs), F32), _sds((N_DEV, s, fs), BF16),
                   _sds((N_DEV, fs, s), BF16)],
        compiler_params=_params(("parallel", "arbitrary")),
    )(u2, w_gate, w_up)


def _ffn_down_bwd(dff, w_down, gate, up, tm=1024):
    s, d = dff.shape
    fs = w_down.shape[1]
    tm = _tile(s, tm)

    def body(dff_ref, wd_ref, gate_ref, up_ref, dgate_ref, dup_ref):
        dact = _dot(dff_ref[...], wd_ref[...], "nt")
        gate = gate_ref[...]
        sg = _sigmoid(gate)
        dup_ref[...] = (dact * gate * sg).astype(BF16)
        dgate_ref[...] = (dact * up_ref[...] * sg * (1.0 + gate * (1.0 - sg))).astype(BF16)

    a_spec = pl.BlockSpec((None, tm, fs), lambda i, j: (j, i, 0))
    return pl.pallas_call(
        body, name="ffn_down_bwd", grid=(s // tm, N_DEV),
        in_specs=[pl.BlockSpec((tm, d), lambda i, j: (i, 0)), pl.BlockSpec((None, fs, d), lambda i, j: (j, 0, 0)),
                  a_spec, a_spec],
        out_specs=[a_spec, a_spec],
        out_shape=[_sds((N_DEV, s, fs), BF16), _sds((N_DEV, s, fs), BF16)],
        compiler_params=_params(("parallel", "arbitrary")),
    )(dff, w_down, gate, up)


def _mesh_place():
    x, y, c = lax.axis_index("x"), lax.axis_index("y"), lax.axis_index("c")
    peers = []
    for d in range(1, N_DEV):
        px = 1 - x if d & 4 else x
        py = 1 - y if d & 2 else y
        pc = 1 - c if d & 1 else c
        peers.append((d, (px, py, pc), 4 * px + 2 * py + pc))
    return 4 * x + 2 * y + c, peers


def _flat_me():
    return 4 * lax.axis_index("x") + 2 * lax.axis_index("y") + lax.axis_index("c")


def _in_hbm(a):
    return pltpu.with_memory_space_constraint(a, pltpu.HBM)


def _pair_plan():
    x, y, c = lax.axis_index("x"), lax.axis_index("y"), lax.axis_index("c")
    return [(2 * q + (1 - c), q, q, (x, y, 1 - c)) for q in range(4)]


def _chip_plan():
    x, y, c = lax.axis_index("x"), lax.axis_index("y"), lax.axis_index("c")
    plan = []
    for fx, fy in ((1, 0), (0, 1), (1, 1)):
        cx, cy = (1 - x if fx else x), (1 - y if fy else y)
        plan.append((2 * cx + cy, 2 * x + y, 2 * cx + cy, (cx, cy, c)))
    return plan


def _split_start(name, srcs, lands, plan, k):
    n = len(srcs)

    def body(*refs):
        ins, lnd = refs[:n], refs[n:2 * n]
        send, recv, token = refs[2 * n], refs[2 * n + 1], refs[-1]
        copies = plan()
        for a in range(n):
            for t, (src, dst, _, dev) in enumerate(copies):
                pltpu.make_async_remote_copy(src_ref=ins[a].at[src], dst_ref=lnd[a].at[dst], send_sem=send.at[k * a + t],
                                             recv_sem=recv.at[k * a + t], device_id=dev, device_id_type=MESH).start()
        token[...] = jnp.zeros_like(token)

    res = pl.pallas_call(
        body, name=name,
        out_shape=[pltpu.SemaphoreType.DMA((n * k,)), pltpu.SemaphoreType.DMA((n * k,))]
        + [pltpu.HBM(a.shape, a.dtype) for a in list(srcs) + list(lands)] + [_sds((8, LANES), F32)],
        in_specs=[HBM] * (2 * n), out_specs=[SEM, SEM] + [HBM] * (2 * n) + [pl.BlockSpec(memory_space=pltpu.VMEM)],
        input_output_aliases={i: 2 + i for i in range(2 * n)},
        compiler_params=pltpu.CompilerParams(has_side_effects=EFFECT),
    )(*[_in_hbm(a) for a in srcs], *[_in_hbm(a) for a in lands])
    return res[0], res[1], res[2:2 + n], res[2 + n:2 + 2 * n], res[-1]


def _split_wait(name, send, recv, srcs, lands, plan, k, after):
    n = len(srcs)

    def body(*refs):
        ins, lnd = refs[:n], refs[n:2 * n]
        send_sem, recv_sem = refs[2 * n], refs[2 * n + 1]
        copies = plan()
        for a in range(n):
            for t, (src, _, dst, dev) in enumerate(copies):
                cp = pltpu.make_async_remote_copy(src_ref=ins[a].at[src], dst_ref=lnd[a].at[dst], send_sem=send_sem.at[k * a + t],
                                                  recv_sem=recv_sem.at[k * a + t], device_id=dev, device_id_type=MESH)
                cp.wait_send()
                cp.wait_recv()

    res = pl.pallas_call(
        body, name=name,
        out_shape=[pltpu.HBM(a.shape, a.dtype) for a in list(srcs) + list(lands)],
        in_specs=[HBM] * (2 * n) + [SEM, SEM, ANY], out_specs=[HBM] * (2 * n),
        input_output_aliases={i: i for i in range(2 * n)},
        compiler_params=pltpu.CompilerParams(has_side_effects=EFFECT),
    )(*srcs, *lands, send, recv, after)
    return res[n:]


def _pair_add(name, parts, land):
    _, r, cols = parts.shape
    tr = max(16, min(r, ((1 << 20) // (2 * cols)) // 16 * 16))
    while r % tr:
        tr -= 16

    def body(c_ref, p_ref, l_ref, o_ref):
        del c_ref
        o_ref[...] = (p_ref[...].astype(F32) + l_ref[...].astype(F32)).astype(BF16)

    blk = pl.BlockSpec((None, tr, cols), lambda q, i, c_ref: (q, i, 0))
    return pl.pallas_call(
        body, name=name,
        grid_spec=pltpu.PrefetchScalarGridSpec(
            num_scalar_prefetch=1, grid=(4, r // tr),
            in_specs=[pl.BlockSpec((None, tr, cols), lambda q, i, c_ref: (2 * q + c_ref[0], i, 0)), blk], out_specs=blk),
        out_shape=_sds((4, r, cols), BF16),
        compiler_params=_params(("parallel", "parallel")),
    )(jnp.reshape(lax.axis_index("c"), (1,)).astype(jnp.int32), parts, land)


def _scatter_pairs(tag, parts):
    lands = [lax.empty((4,) + a.shape[1:], a.dtype) for a in parts]
    return _split_start("pair_" + tag, parts, lands, _pair_plan, 4)


def _scatter_chips(tag, started, after):
    send, recv, parts, lands, _ = started
    lands = _split_wait("pair_" + tag + "_wait", send, recv, parts, lands, _pair_plan, 4, after)
    sums = [_pair_add("pair_" + tag + "_add%d" % a, p, l) for a, (p, l) in enumerate(zip(parts, lands))]
    chip = 2 * lax.axis_index("x") + lax.axis_index("y")
    final = [lax.dynamic_update_slice_in_dim(lax.empty(v.shape, v.dtype), lax.dynamic_slice_in_dim(v, chip, 1, 0), chip, 0)
             for v in sums]
    return _split_start("chips_" + tag, sums, final, _chip_plan, 3)


def _scatter_end(tag, started, after):
    send, recv, sums, final, _ = started
    return _split_wait("chips_" + tag + "_wait", send, recv, sums, final, _chip_plan, 3, after)


def _gather_targets():
    x, y, c = lax.axis_index("x"), lax.axis_index("y"), lax.axis_index("c")
    chips = [(x, y), (1 - x, y), (x, 1 - y), (1 - x, 1 - y)]
    same = [((cx, cy, c), 4 * cx + 2 * cy + c) for cx, cy in chips]
    other = [((cx, cy, 1 - c), 4 * cx + 2 * cy + 1 - c) for cx, cy in chips]
    return same[0][1], [other[0]] + same[1:], [flat for _, flat in other[1:]], other[0][0]


def _gather_start(shards):
    n = len(shards)
    me = _flat_me()
    lands = [lax.dynamic_update_slice_in_dim(lax.empty((N_DEV,) + a.shape, a.dtype), a[None], me, 0) for a in shards]

    def body(*refs):
        lnd, send, recv, token = refs[:n], refs[n], refs[n + 1], refs[-1]
        mine, targets, _, _ = _gather_targets()
        for a in range(n):
            for t, (dev, _) in enumerate(targets):
                pltpu.make_async_remote_copy(src_ref=lnd[a].at[mine], dst_ref=lnd[a].at[mine], send_sem=send.at[4 * a + t],
                                             recv_sem=recv.at[4 * a + t], device_id=dev, device_id_type=MESH).start()
        token[...] = jnp.zeros_like(token)

    res = pl.pallas_call(
        body, name="gather_start",
        out_shape=[pltpu.SemaphoreType.DMA((4 * n,)), pltpu.SemaphoreType.DMA((4 * n,))]
        + [pltpu.HBM(a.shape, a.dtype) for a in lands] + [_sds((8, LANES), F32)],
        in_specs=[HBM] * n, out_specs=[SEM, SEM] + [HBM] * n + [pl.BlockSpec(memory_space=pltpu.VMEM)],
        input_output_aliases={i: 2 + i for i in range(n)},
        compiler_params=pltpu.CompilerParams(has_side_effects=EFFECT),
    )(*[_in_hbm(a) for a in lands])
    return res[0], res[1], list(res[2:2 + n]), res[-1]


def _gather_forward(name, lands, first, send, recv, after):
    n = len(lands)

    def body(*refs):
        lnd, send_sem, recv_sem = refs[:n], refs[n], refs[n + 1]
        send2, recv2, token = refs[-3], refs[-2], refs[-1]
        mine, targets, _, sibling = _gather_targets()
        for a in range(n):
            for t, (dev, flat) in enumerate(targets):
                cp = pltpu.make_async_remote_copy(src_ref=lnd[a].at[mine], dst_ref=lnd[a].at[flat],
                                                  send_sem=send_sem.at[4 * (first + a) + t],
                                                  recv_sem=recv_sem.at[4 * (first + a) + t], device_id=dev, device_id_type=MESH)
                cp.wait_send()
                if t:
                    cp.wait_recv()
                    pltpu.make_async_remote_copy(src_ref=lnd[a].at[flat], dst_ref=lnd[a].at[flat], send_sem=send2.at[3 * a + t - 1],
                                                 recv_sem=recv2.at[3 * a + t - 1], device_id=sibling, device_id_type=MESH).start()
        token[...] = jnp.zeros_like(token)

    res = pl.pallas_call(
        body, name=name,
        out_shape=[pltpu.HBM(a.shape, a.dtype) for a in lands]
        + [pltpu.SemaphoreType.DMA((3 * n,)), pltpu.SemaphoreType.DMA((3 * n,)), _sds((8, LANES), F32)],
        in_specs=[HBM] * n + [SEM, SEM, ANY], out_specs=[HBM] * n + [SEM, SEM, pl.BlockSpec(memory_space=pltpu.VMEM)],
        input_output_aliases={i: i for i in range(n)},
        compiler_params=pltpu.CompilerParams(has_side_effects=EFFECT),
    )(*lands, send, recv, after)
    return list(res[:n]), res[n], res[n + 1], res[-1]


def _gather_wait(name, lands, first, recv, send2, recv2, after):
    n = len(lands)

    def body(*refs):
        lnd, recv_sem, send2_sem, recv2_sem = refs[:n], refs[n], refs[n + 1], refs[n + 2]
        mine, targets, passed, sibling = _gather_targets()
        for a in range(n):
            dev, flat = targets[0]
            pltpu.make_async_remote_copy(src_ref=lnd[a].at[mine], dst_ref=lnd[a].at[flat], send_sem=send2_sem.at[3 * a],
                                         recv_sem=recv_sem.at[4 * (first + a)], device_id=dev, device_id_type=MESH).wait_recv()
            for t in range(3):
                cp = pltpu.make_async_remote_copy(src_ref=lnd[a].at[targets[t + 1][1]], dst_ref=lnd[a].at[passed[t]],
                                                  send_sem=send2_sem.at[3 * a + t], recv_sem=recv2_sem.at[3 * a + t],
                                                  device_id=sibling, device_id_type=MESH)
                cp.wait_send()
                cp.wait_recv()

    res = pl.pallas_call(
        body, name=name, out_shape=[pltpu.HBM(a.shape, a.dtype) for a in lands],
        in_specs=[HBM] * n + [SEM, SEM, SEM, ANY], out_specs=[HBM] * n,
        input_output_aliases={i: i for i in range(n)},
        compiler_params=pltpu.CompilerParams(has_side_effects=EFFECT),
    )(*lands, recv, send2, recv2, after)
    return list(res)


def _adamw(g, w, m, v):
    m = ADAM_B1 * m + (1.0 - ADAM_B1) * g
    v = ADAM_B2 * v + (1.0 - ADAM_B2) * (g * g)
    m_hat = m / (1.0 - ADAM_B1 ** ADAM_STEP)
    v_hat = v / (1.0 - ADAM_B2 ** ADAM_STEP)
    delta = -ADAM_LR * (m_hat / (jnp.sqrt(v_hat) + ADAM_EPS) + ADAM_WD * w)
    return delta, m, v


def _update(name, parts, w, m, v, layout=None, block_bytes=1 << 20):
    _, r, c = w.shape
    n_slots, _, cp = parts.shape
    tr = max(8, min(r, (block_bytes // (4 * cp)) // 8 * 8))
    while r % tr:
        tr -= 8

    def body(p_ref, w_ref, m_ref, v_ref, g_ref, d_ref, nm_ref, nv_ref, *scratch):
        g = p_ref[0].astype(F32)
        for p in range(1, n_slots):
            g = g + p_ref[p].astype(F32)
        if layout is not None:
            s1, s2, lg = layout.my_shifts()
            lane = lax.broadcasted_iota(jnp.int32, g.shape, 1)
            scratch[0][...] = jnp.where(lane < lg, pltpu.roll(g, cp - s1, 1), pltpu.roll(g, cp - s2, 1))
            g = scratch[0][:, 0:c]
        g_ref[...] = g
        d_ref[...], nm_ref[...], nv_ref[...] = _adamw(g, w_ref[...], m_ref[...], v_ref[...])

    blk = pl.BlockSpec((None, tr, c), lambda i: (0, i, 0))
    return pl.pallas_call(
        body, name=name, grid=(r // tr,),
        in_specs=[pl.BlockSpec((n_slots, tr, cp), lambda i: (0, i, 0)), blk, blk, blk],
        out_specs=[blk] * 4, out_shape=[_sds((1, r, c), F32)] * 4,
        scratch_shapes=[] if layout is None else [pltpu.VMEM((tr, cp), F32)],
        compiler_params=_params(("parallel",)),
    )(parts, w, m, v)


def _small_update(part, w, m, v):
    n = part.shape[1]

    def body(p_ref, w_ref, m_ref, v_ref, g_ref, d_ref, nm_ref, nv_ref, buf, send, recv):
        me, peers = _mesh_place()
        buf[me] = p_ref[...]
        sent = []
        for d, dev, flat in peers:
            cp = pltpu.make_async_remote_copy(src_ref=p_ref, dst_ref=buf.at[me], send_sem=send.at[d],
                                              recv_sem=recv.at[d], device_id=dev, device_id_type=MESH)
            cp.start()
            sent.append(cp)
        for d, dev, flat in peers:
            pltpu.make_async_remote_copy(src_ref=p_ref, dst_ref=buf.at[flat], send_sem=send.at[d],
                                         recv_sem=recv.at[d], device_id=dev, device_id_type=MESH).wait_recv()
        for cp in sent:
            cp.wait_send()
        g = buf[0]
        for p in range(1, N_DEV):
            g = g + buf[p]
        g_ref[...] = g
        d_ref[...], nm_ref[...], nv_ref[...] = _adamw(g, w_ref[...], m_ref[...], v_ref[...])

    vm = pl.BlockSpec(memory_space=pltpu.VMEM)
    return pl.pallas_call(
        body, name="small_update", in_specs=[vm] * 4, out_specs=[vm] * 4, out_shape=[_sds((1, n), F32)] * 4,
        scratch_shapes=[pltpu.VMEM((N_DEV, 1, n), F32), pltpu.SemaphoreType.DMA((N_DEV,)),
                        pltpu.SemaphoreType.DMA((N_DEV,))],
    )(part, w, m, v)


class _WInLayout:
    def __init__(self, n8, n_f, d_sb, d_fox, d):
        assert n8 % LANES == 1 and n_f < LANES and d % (N_DEV * LANES) == 0
        self.n8, self.n_f, self.d = n8, n_f, d
        self.sp = n8 // LANES
        self.wp = (n8 + 2 * LANES - 2) // LANES * LANES
        self.n_qkv = 3 * (d_sb + d_fox)
        nq, dt, tc = self.n_qkv // LANES, d // LANES, d // N_DEV // LANES
        h_sb, h_fox = d_sb // HEAD_DIM, d_fox // HEAD_DIM
        self.sources = {}
        self.part_tile = {}
        for p in range(N_DEV):
            lg = min(max(self.n_qkv + n_f - n8 * p, 0), n8)
            s1, s2 = p, p + LANES - n_f
            spans = []
            if lg > 0:
                spans.append(("a", self.sp * p, s1 // LANES, (lg + s1 - 1) // LANES))
            if lg < n8:
                spans.append(("g", self.sp * p - 1 - nq, (lg + s2) // LANES, (n8 - 1 + s2) // LANES))
            for kind, base, first, last in spans:
                for i in range(first, last + 1):
                    assert (p, i) not in self.part_tile
                    self.part_tile[(p, i)] = (kind, base + i)
                    self.sources.setdefault((kind, base + i), []).append((p, i))
        self.cat_tiles = [("a", r * h_sb + h) for h in range(h_sb) for r in range(3)]
        self.cat_tiles += [("a", 3 * h_sb + r * h_fox + h) for h in range(h_fox) for r in range(3)]
        self.cat_tiles += [("g", which * dt + j * tc + half) for j in range(N_DEV) for which in (0, 1) for half in range(tc)]
        self.cat_tiles += [("a", nq)] + [None] * (F_PAD // LANES - 1)
        self.cat_index = {key: c for c, key in enumerate(self.cat_tiles) if key is not None}

    def my_shifts(self):
        me = _flat_me()
        return me, me + LANES - self.n_f, jnp.clip(self.n_qkv + self.n_f - self.n8 * me, 0, self.n8)


def _lane_tile(i):
    return pl.ds(i * LANES, LANES)


def _w_in_shift(w_in, lay, tr=256):
    _, d, n8 = w_in.shape

    def body(w_ref, o_ref, buf):
        buf[...] = jnp.zeros_like(buf)
        buf[:, 0:n8] = w_ref[...]
        v = buf[...]
        s1, s2, lg = lay.my_shifts()
        pos = lax.broadcasted_iota(jnp.int32, v.shape, 1)
        o_ref[...] = jnp.where(pos < lg + s1, pltpu.roll(v, s1, 1),
                               jnp.where(pos >= lg + s2, pltpu.roll(v, s2, 1), 0.0)).astype(BF16)

    return pl.pallas_call(
        body, name="w_in_shift", grid=(d // tr,),
        in_specs=[pl.BlockSpec((None, tr, n8), lambda i: (0, i, 0))],
        out_specs=pl.BlockSpec((tr, lay.wp), lambda i: (i, 0)), out_shape=_sds((d, lay.wp), BF16),
        scratch_shapes=[pltpu.VMEM((tr, lay.wp), F32)],
        compiler_params=_params(("parallel",)),
    )(w_in)


def _w_in_build(g_in, lay, tr=256):
    d = g_in.shape[1]
    width = len(lay.cat_tiles) * LANES

    def body(g_ref, o_ref):
        for c, key in enumerate(lay.cat_tiles):
            if key is None:
                o_ref[:, _lane_tile(c)] = jnp.zeros((tr, LANES), BF16)
                continue
            (p, i), *more = lay.sources[key]
            val = g_ref[p, :, _lane_tile(i)]
            for p2, i2 in more:
                val = val + g_ref[p2, :, _lane_tile(i2)]
            o_ref[:, _lane_tile(c)] = val

    return pl.pallas_call(
        body, name="w_in_build", grid=(d // tr,),
        in_specs=[pl.BlockSpec((N_DEV, tr, lay.wp), lambda i: (0, i, 0))],
        out_specs=pl.BlockSpec((tr, width), lambda i: (i, 0)), out_shape=_sds((d, width), BF16),
        compiler_params=_params(("parallel",)),
    )(g_in)


def _w_in_grad_parts(dwq, dwgf, lay, tr=256):
    d = dwq.shape[0]
    nq = lay.n_qkv // LANES

    def body(q_ref, g_ref, o_ref):
        for p in range(N_DEV):
            for i in range(lay.wp // LANES):
                key = lay.part_tile.get((p, i))
                if key is None:
                    o_ref[p, :, _lane_tile(i)] = jnp.zeros((tr, LANES), BF16)
                    continue
                c = lay.cat_index[key]
                o_ref[p, :, _lane_tile(i)] = q_ref[:, _lane_tile(c)] if c < nq else g_ref[:, _lane_tile(c - nq)]

    return pl.pallas_call(
        body, name="w_in_grad_parts", grid=(d // tr,),
        in_specs=[pl.BlockSpec((tr, dwq.shape[1]), lambda i: (i, 0)), pl.BlockSpec((tr, dwgf.shape[1]), lambda i: (i, 0))],
        out_specs=pl.BlockSpec((N_DEV, tr, lay.wp), lambda i: (0, i, 0)), out_shape=_sds((N_DEV, d, lay.wp), BF16),
        compiler_params=_params(("parallel",)),
    )(dwq, dwgf)


def kernel(x, norm_mix_pre, norm_mix_post, w_in, b_forget, w_branch_sb, w_branch_fox, w_out, norm_ffn_pre, norm_ffn_post, w_ffn_gate, w_ffn_up, w_ffn_down, loss_target, m_norm_mix_pre, m_norm_mix_post, m_w_in, m_b_forget, m_w_branch_sb, m_w_branch_fox, m_w_out, m_norm_ffn_pre, m_norm_ffn_post, m_w_ffn_gate, m_w_ffn_up, m_w_ffn_down, v_norm_mix_pre, v_norm_mix_post, v_w_in, v_b_forget, v_w_branch_sb, v_w_branch_fox, v_w_out, v_norm_ffn_pre, v_norm_ffn_post, v_w_ffn_gate, v_w_ffn_up, v_w_ffn_down):
    xs, target = x[0], loss_target[0]
    s, d = xs.shape
    d_sb, d_fox = w_branch_sb.shape[1], w_branch_fox.shape[1]
    h_sb, h_fox = d_sb // HEAD_DIM, d_fox // HEAD_DIM
    n_f = b_forget.shape[1]
    fs = w_ffn_gate.shape[2]
    cs = d // N_DEV
    n_qkv = 3 * (d_sb + d_fox)
    n_gf = 2 * d + F_PAD
    f_blk = 2 * d // LANES
    big = (w_in, w_branch_sb, w_branch_fox, w_out, w_ffn_gate, w_ffn_up, w_ffn_down)
    big_m = (m_w_in, m_w_branch_sb, m_w_branch_fox, m_w_out, m_w_ffn_gate, m_w_ffn_up, m_w_ffn_down)
    big_v = (v_w_in, v_w_branch_sb, v_w_branch_fox, v_w_out, v_w_ffn_gate, v_w_ffn_up, v_w_ffn_down)

    lay = _WInLayout(w_in.shape[2], n_f, d_sb, d_fox, d)
    send1, recv1, lands, token = _gather_start([_w_in_shift(w_in, lay)] + [w[0].astype(BF16) for w in big[1:]])
    b_pad = jnp.pad(b_forget, ((0, 0), (0, LANES - n_f)))

    u, u_t = _pre_norm(xs, norm_mix_pre, dep=token)
    l_in, send2, recv2, token = _gather_forward("gather_in_forward", lands[0:1], 0, send1, recv1, u)
    (g_in,) = _gather_wait("gather_in_wait", l_in, 0, recv1, send2, recv2, token)
    w_cat = _w_in_build(g_in, lay)
    qkv = _mm_plain("proj_qkv", "nn", u, w_cat, BF16, n=n_qkv)
    gf = _mm_plain("proj_gates", "nn", u, w_cat, F32, n_off=n_qkv, n=n_gf)
    cum_col, cum_row = _forget_fwd(gf, b_pad, f_blk)
    o_sb, o_sb_t, tot = _sb_fwd(qkv, h_sb)
    l_mid, send2, recv2, token = _gather_forward("gather_mid_forward", lands[1:4], 1, send1, recv1, o_sb)
    o_fx, o_fx_t, o_fx32, lse = _fox_fwd(qkv, cum_col, cum_row, h_fox, h_sb, token)
    g_sb, g_fx, g_out = _gather_wait("gather_mid_wait", l_mid, 1, recv1, send2, recv2, o_fx)
    w_out_full = g_out.reshape(d, d)
    l_ffn, send2, recv2, token = _gather_forward("gather_ffn_forward", lands[4:7], 4, send1, recv1, o_fx)
    merged, merged_t, a_sb, a_fx = _branch_merge(o_sb, o_fx, g_sb, g_fx, gf, token)
    mix = _mm_plain("out_proj", "nn", merged, w_out_full, F32)
    g_gate, g_up, g_down = _gather_wait("gather_ffn_wait", l_ffn, 4, recv1, send2, recv2, mix)
    h1, u2, u2_t = _mid_norms(xs, mix, norm_mix_post, norm_ffn_pre)
    gate, up, act, act_t = _ffn_up(u2, g_gate, g_up)
    tm, tn = _tile(s, 1024), _tile(d, 1024)
    ff = _matmul("ffn_down", "nn",
                 [(act, pl.BlockSpec((None, tm, fs), lambda i, j, k: (k, i, 0)),
                   g_down, pl.BlockSpec((None, fs, tn), lambda i, j, k: (k, 0, j)))],
                 (s // tm, d // tn, N_DEV), (tm, tn), _sds((s, d), F32), pl.BlockSpec((tm, tn), lambda i, j, k: (i, j)))
    loss_part, dy, dff, dg_ffn_post = _loss_head(h1, ff, target, norm_ffn_post)

    dgate, dup = _ffn_down_bwd(dff, g_down, gate, up)
    dw_down = _matmul("dw_down", "nn",
                      [(act_t, pl.BlockSpec((None, fs, s), lambda j, n, k: (j, 0, 0)),
                        dff, pl.BlockSpec((s, tn), lambda j, n, k: (0, n)))],
                      (N_DEV, d // tn, 1), (fs, tn), _sds((N_DEV, fs, d), BF16),
                      pl.BlockSpec((None, fs, tn), lambda j, n, k: (j, 0, n)))

    def dw_up(name, dact):
        return _matmul(name, "nn",
                       [(u2_t, pl.BlockSpec((tn, s), lambda j, i, k: (i, 0)),
                         dact, pl.BlockSpec((None, s, fs), lambda j, i, k: (j, 0, 0)))],
                       (N_DEV, d // tn, 1), (tn, fs), _sds((N_DEV, d, fs), BF16),
                       pl.BlockSpec((None, tn, fs), lambda j, i, k: (j, i, 0)))

    dw_gate, dw_upw = dw_up("dw_gate", dgate), dw_up("dw_up", dup)
    rs_ffn = _scatter_pairs("ffn", [dw_gate, dw_upw, dw_down])
    a_spec = pl.BlockSpec((None, tm, fs), lambda i, j, k: (k, i, 0))
    b_spec = pl.BlockSpec((None, tn, fs), lambda i, j, k: (k, j, 0))
    du2 = _matmul("du2", "nt", [(dgate, a_spec, g_gate, b_spec), (dup, a_spec, g_up, b_spec)],
                  (s // tm, d // tn, N_DEV), (tm, tn), _sds((s, d), F32), pl.BlockSpec((tm, tn), lambda i, j, k: (i, j)),
                  dep=rs_ffn[4])
    rs_ffn = _scatter_chips("ffn", rs_ffn, du2)
    dh1, dmix, dg_ffn_pre, dg_mix_post = _mid_norms_bwd(dy, du2, h1, mix, norm_ffn_pre, norm_mix_post)

    da_sb, da_fx, dgf = _merge_bwd(dmix, w_out_full, gf, a_sb, a_fx, dep=rs_ffn[4])
    dw_out = _mm_plain("dw_out", "nn", merged_t, dmix, BF16).reshape(N_DEV, cs, d)

    def branch_bwd(tag, da, w_b, o_t, width):
        tb = _tile(width, 1024)
        do = _matmul("do_" + tag, "nt",
                     [(da, pl.BlockSpec((tm, cs), lambda i, j, k: (i, k)),
                       w_b, pl.BlockSpec((None, tb, cs), lambda i, j, k: (k, j, 0)))],
                     (s // tm, width // tb, N_DEV), (tm, tb), _sds((s, width), BF16),
                     pl.BlockSpec((tm, tb), lambda i, j, k: (i, j)))
        dw = _matmul("dw_" + tag, "nn",
                     [(o_t, pl.BlockSpec((width, s), lambda j, i, k: (0, 0)),
                       da, pl.BlockSpec((s, cs), lambda j, i, k: (0, j)))],
                     (N_DEV, 1, 1), (width, cs), _sds((N_DEV, width, cs), BF16),
                     pl.BlockSpec((None, width, cs), lambda j, i, k: (j, 0, 0)))
        return do, dw

    do_sb, dw_sb = branch_bwd("sb", da_sb, g_sb, o_sb_t, d_sb)
    do_fx, dw_fx = branch_bwd("fox", da_fx, g_fx, o_fx_t, d_fox)

    rs_mid = _scatter_pairs("mid", [dw_sb, dw_fx, dw_out])

    dqkv = _sb_bwd(qkv, do_sb, tot, h_sb, rs_mid[4])
    rs_mid = _scatter_chips("mid", rs_mid, dqkv)
    dqkv, dcum = _fox_bwd(dqkv, qkv, do_fx, o_fx32, lse, cum_col, cum_row, h_fox, h_sb, rs_mid[4])
    dgf, db_part = _forget_bwd(dgf, dcum, gf, b_pad, f_blk)
    dw_in = _w_in_grad_parts(_mm_plain("dw_qkv", "nn", u_t, dqkv, BF16), _mm_plain("dw_gates", "nn", u_t, dgf, BF16), lay)
    rs_in = _scatter_pairs("in", [dw_in])
    du = _mm_plain("du_qkv", "nt", dqkv, w_cat, F32, tn=1024, dep=rs_in[4])
    rs_in = _scatter_chips("in", rs_in, du)
    du = _mm_plain("du_gates", "nt", dgf, w_cat, F32, tn=1024, k_off=n_qkv, init=du, dep=rs_in[4])
    dx, dg_mix_pre = _pre_norm_bwd(dh1, du, xs, norm_mix_pre)

    upd = {}

    def update_group(tag, rs, names, after):
        parts = _scatter_end(tag, rs, after)
        for nm, p in zip(names, parts):
            w, m, v = weights[nm]
            upd[nm] = _update("update_" + nm, p, w, m, v, layout=lay if nm == "w_in" else None)

    weights = dict(zip(("w_in", "w_branch_sb", "w_branch_fox", "w_out", "w_ffn_gate", "w_ffn_up", "w_ffn_down"),
                       zip(big, big_m, big_v)))
    update_group("ffn", rs_ffn, ("w_ffn_gate", "w_ffn_up", "w_ffn_down"), dx)
    update_group("mid", rs_mid, ("w_branch_sb", "w_branch_fox", "w_out"), upd["w_ffn_down"][0])
    update_group("in", rs_in, ("w_in",), upd["w_out"][0])

    small = ((norm_mix_pre, m_norm_mix_pre, v_norm_mix_pre), (norm_mix_post, m_norm_mix_post, v_norm_mix_post),
             (norm_ffn_pre, m_norm_ffn_pre, v_norm_ffn_pre), (norm_ffn_post, m_norm_ffn_post, v_norm_ffn_post))
    pad_f = ((0, 0), (0, LANES - n_f))
    cat = lambda i: jnp.concatenate([t[i] for t in small] + [jnp.pad((b_forget, m_b_forget, v_b_forget)[i], pad_f)], axis=1)
    sm = _small_update(jnp.concatenate([dg_mix_pre, dg_mix_post, dg_ffn_pre, dg_ffn_post, db_part], axis=1),
                       cat(0), cat(1), cat(2))
    for i, nm in enumerate(("norm_mix_pre", "norm_mix_post", "norm_ffn_pre", "norm_ffn_post")):
        upd[nm] = [o[:, i * d:(i + 1) * d] for o in sm]
    upd["b_forget"] = [o[:, 4 * d:4 * d + n_f] for o in sm]

    loss = lax.psum(loss_part[0, 0], ("x", "y", "c"))
    order = ("norm_mix_pre", "norm_mix_post", "w_in", "b_forget", "w_branch_sb", "w_branch_fox", "w_out",
             "norm_ffn_pre", "norm_ffn_post", "w_ffn_gate", "w_ffn_up", "w_ffn_down")
    return (loss, dx[None]) + tuple(upd[nm][i] for i in range(4) for nm in order)
```

```python
import jax
import jax.numpy as jnp
from jax import lax
from jax.experimental import pallas as pl
from jax.experimental.pallas import tpu as pltpu

F32 = jnp.float32
BF16 = jnp.bfloat16
MESH = pl.DeviceIdType.MESH
ANY = pl.BlockSpec(memory_space=pl.ANY)
HBM = pl.BlockSpec(memory_space=pltpu.HBM)
SEM = pl.BlockSpec(memory_space=pltpu.SEMAPHORE)
EFFECT = pltpu.SideEffectType.DATAFLOW_SIDE_EFFECTING

N_DEV = 8
HEAD_DIM = 128
RMS_EPS = 1e-6
F_PAD = 512
LANES = 128
ATT_TQ = 256
ATT_TK = 256
ATT_HP = 2
NEG_BIG = -1e30
VMEM_LIMIT = 56 * 1024 * 1024

ADAM_LR = 0.001
ADAM_B1 = 0.9
ADAM_B2 = 0.999
ADAM_EPS = 1e-08
ADAM_WD = 0.01
ADAM_STEP = 10

_DIMS = {"nn": ((1,), (0,)), "nt": ((1,), (1,)), "tn": ((0,), (0,))}


def _params(sem):
    return pltpu.CompilerParams(dimension_semantics=sem, vmem_limit_bytes=VMEM_LIMIT)


def _dot(a, b, mode="nn"):
    return lax.dot_general(a.astype(BF16), b.astype(BF16), (_DIMS[mode], ((), ())), preferred_element_type=F32)


def _tile(n, pref):
    if n <= pref:
        return n
    t = (pref // LANES) * LANES
    while n % t:
        t -= LANES
    return t


def _split2(v):
    hi = v.astype(BF16)
    return hi, (v - hi.astype(F32)).astype(BF16)


def _split3(v):
    a = v.astype(BF16)
    r = v - a.astype(F32)
    b = r.astype(BF16)
    return a, b, (r - b.astype(F32)).astype(BF16)


def _tri(n, cmp):
    r = lax.broadcasted_iota(jnp.int32, (n, n), 0)
    c = lax.broadcasted_iota(jnp.int32, (n, n), 1)
    return jnp.where(cmp(r, c), 1.0, 0.0).astype(BF16)


def _lane_pick(v, h):
    lane = lax.broadcasted_iota(jnp.int32, v.shape, 1)
    return jnp.sum(jnp.where(lane == h, v, 0.0), axis=1, keepdims=True)


def _lane_put(ref, rows, h, col):
    old = ref[rows, :]
    lane = lax.broadcasted_iota(jnp.int32, old.shape, 1)
    ref[rows, :] = jnp.where(lane == h, col, old)


def _sigmoid(z):
    return 1.0 / (1.0 + jnp.exp(-z))


def _log_sigmoid(z):
    return jnp.minimum(z, 0.0) - jnp.log(1.0 + jnp.exp(-jnp.abs(z)))


def _sds(shape, dtype):
    return jax.ShapeDtypeStruct(shape, dtype)


def _matmul(name, mode, pairs, grid, acc_shape, out_shape, out_specs, extras=(), epilogue=None, init=None, dep=None):
    n_p, n_e = len(pairs), len(extras)
    nk = grid[-1]
    single = not isinstance(out_shape, (list, tuple))
    n_i = 0 if init is None else 1
    n_d = 0 if dep is None else 1

    one_step = nk == 1 and init is None

    def body(*refs):
        ab = refs[:2 * n_p]
        ex = refs[2 * n_p:2 * n_p + n_e]
        ini = refs[2 * n_p + n_e:2 * n_p + n_e + n_i]
        outs = refs[2 * n_p + n_e + n_i + n_d:len(refs) - (0 if one_step else 1)]

        def finish(total):
            if epilogue is None:
                outs[0][...] = total.astype(outs[0].dtype)
            else:
                epilogue(total, ex, outs)

        t = _dot(ab[0][...], ab[1][...], mode)
        for p in range(1, n_p):
            t = t + _dot(ab[2 * p][...], ab[2 * p + 1][...], mode)
        if one_step:
            finish(t)
            return
        acc = refs[-1]
        k = pl.program_id(len(grid) - 1)

        @pl.when(k == 0)
        def _():
            acc[...] = t if init is None else ini[0][...].astype(F32) + t

        @pl.when(k > 0)
        def _():
            acc[...] += t

        @pl.when(k == nk - 1)
        def _():
            finish(acc[...])

    in_specs = [s for (_, sa, _, sb) in pairs for s in (sa, sb)] + [s for (_, s) in extras]
    args = [v for (a, _, b, _) in pairs for v in (a, b)] + [e for (e, _) in extras]
    if init is not None:
        in_specs.append(init[1])
        args.append(init[0])
    if dep is not None:
        in_specs.append(ANY)
        args.append(dep)
    return pl.pallas_call(
        body, name=name, grid=grid, in_specs=in_specs,
        out_specs=out_specs if single else list(out_specs),
        out_shape=out_shape if single else list(out_shape),
        scratch_shapes=[] if one_step else [pltpu.VMEM(acc_shape, F32)],
        compiler_params=_params(("parallel",) * (len(grid) - 1) + ("arbitrary",)),
    )(*args)


def _mm_plain(name, mode, a, b, out_dtype, *, n_off=0, n=None, k_off=0, tm=1024, tn=1536, tk=2048, init=None, dep=None):
    if mode == "nn":
        (m, kk), nn_ = a.shape, b.shape[1]
    elif mode == "nt":
        (m, kk), nn_ = a.shape, b.shape[0]
    else:
        (kk, m), nn_ = a.shape, b.shape[1]
    n = nn_ if n is None else n
    tm, tn, tk = _tile(m, tm), _tile(n, tn), _tile(kk, tk)
    while n_off % tn or n % tn:
        tn -= LANES
    while k_off % tk or kk % tk:
        tk -= LANES
    off, koff = n_off // tn, k_off // tk
    a_spec = {"nn": pl.BlockSpec((tm, tk), lambda i, j, k: (i, k)),
              "nt": pl.BlockSpec((tm, tk), lambda i, j, k: (i, k)),
              "tn": pl.BlockSpec((tk, tm), lambda i, j, k: (k, i))}[mode]
    b_spec = {"nn": pl.BlockSpec((tk, tn), lambda i, j, k: (k, j + off)),
              "nt": pl.BlockSpec((tn, tk), lambda i, j, k: (j, k + koff)),
              "tn": pl.BlockSpec((tk, tn), lambda i, j, k: (k, j))}[mode]
    o_spec = pl.BlockSpec((tm, tn), lambda i, j, k: (i, j))
    if init is not None:
        init = (init, o_spec)
    return _matmul(name, mode, [(a, a_spec, b, b_spec)], (m // tm, n // tn, kk // tk), (tm, tn),
                   _sds((m, n), out_dtype), o_spec, init=init, dep=dep)


def _rows_call(name, body, ins, outs, s, tr=256, dep=None):
    def spec(v, per_row):
        if per_row == "transposed":
            return pl.BlockSpec((v.shape[0], tr), lambda i: (0, i))
        if per_row:
            return pl.BlockSpec((tr, v.shape[1]), lambda i: (i, 0))
        return pl.BlockSpec(v.shape, lambda i: (0, 0))
    n_in = len(ins)
    deps = [] if dep is None else [dep]

    def with_dep(*refs):
        body(*refs[:n_in], *refs[n_in + len(deps):])

    return pl.pallas_call(
        with_dep, name=name, grid=(s // tr,),
        in_specs=[spec(v, p) for v, p in ins] + [ANY] * len(deps), out_specs=[spec(v, p) for v, p in outs],
        out_shape=[_sds(v.shape, v.dtype) for v, _ in outs],
        compiler_params=_params(("arbitrary",)),
    )(*[v for v, _ in ins], *deps)


def _rsq(v):
    return lax.rsqrt(jnp.mean(v * v, axis=-1, keepdims=True) + RMS_EPS)


def _norm_bwd(dy, v, r, g):
    vh = v * r
    t = dy * g
    dv = r * (t - vh * jnp.mean(t * vh, axis=-1, keepdims=True))
    return dv, jnp.sum(dy * vh, axis=0, keepdims=True)


def _accum(ref, val):
    @pl.when(pl.program_id(0) == 0)
    def _():
        ref[...] = jnp.zeros_like(ref)
    ref[...] += val


def _pre_norm(x, g, dep=None):
    def body(x_ref, g_ref, u_ref, ut_ref):
        v = x_ref[...]
        u = (v * _rsq(v) * g_ref[...]).astype(BF16)
        u_ref[...] = u
        ut_ref[...] = u.T
    s, d = x.shape
    return _rows_call("pre_norm", body, [(x, True), (g, False)],
                      [(_sds((s, d), BF16), True), (_sds((d, s), BF16), "transposed")], s, dep=dep)


def _mid_norms(x, mix, g_post, g_pre):
    def body(x_ref, mix_ref, gp_ref, gn_ref, h_ref, u_ref, ut_ref):
        mv = mix_ref[...]
        h = x_ref[...] + mv * _rsq(mv) * gp_ref[...]
        h_ref[...] = h
        u = (h * _rsq(h) * gn_ref[...]).astype(BF16)
        u_ref[...] = u
        ut_ref[...] = u.T
    s, d = x.shape
    return _rows_call("mid_norms", body, [(x, True), (mix, True), (g_post, False), (g_pre, False)],
                      [(_sds((s, d), F32), True), (_sds((s, d), BF16), True), (_sds((d, s), BF16), "transposed")], s)


def _loss_head(h1, ff, target, g):
    s, d = h1.shape

    def body(h_ref, ff_ref, t_ref, g_ref, loss_ref, dy_ref, dff_ref, dg_ref):
        fv = ff_ref[...]
        r = _rsq(fv)
        err = h_ref[...] + fv * r * g_ref[...] - t_ref[...]
        part = 0.5 * jnp.sum(jnp.mean(err * err, axis=-1, keepdims=True), axis=0, keepdims=True)
        _accum(loss_ref, jnp.broadcast_to(part, loss_ref.shape))
        dy = err * (1.0 / d)
        dy_ref[...] = dy
        dff, dg = _norm_bwd(dy, fv, r, g_ref[...])
        dff_ref[...] = dff.astype(BF16)
        _accum(dg_ref, dg)

    return _rows_call("loss_head", body, [(h1, True), (ff, True), (target, True), (g, False)],
                      [(_sds((1, LANES), F32), False), (_sds((s, d), F32), True),
                       (_sds((s, d), BF16), True), (_sds((1, d), F32), False)], s)


def _mid_norms_bwd(dy, du2, h1, mix, g_pre, g_post):
    s, d = dy.shape

    def body(dy_ref, du_ref, h_ref, mix_ref, gn_ref, gp_ref, dh_ref, dmix_ref, dgn_ref, dgp_ref):
        h = h_ref[...]
        dh, dgn = _norm_bwd(du_ref[...], h, _rsq(h), gn_ref[...])
        dh = dh + dy_ref[...]
        dh_ref[...] = dh
        _accum(dgn_ref, dgn)
        mv = mix_ref[...]
        dmix, dgp = _norm_bwd(dh, mv, _rsq(mv), gp_ref[...])
        dmix_ref[...] = dmix.astype(BF16)
        _accum(dgp_ref, dgp)

    return _rows_call("mid_norms_bwd", body,
                      [(dy, True), (du2, True), (h1, True), (mix, True), (g_pre, False), (g_post, False)],
                      [(_sds((s, d), F32), True), (_sds((s, d), BF16), True),
                       (_sds((1, d), F32), False), (_sds((1, d), F32), False)], s)


def _pre_norm_bwd(dh1, du, x, g, dep=None):
    s, d = x.shape

    def body(dh_ref, du_ref, x_ref, g_ref, dx_ref, dg_ref):
        v = x_ref[...]
        dv, dg = _norm_bwd(du_ref[...], v, _rsq(v), g_ref[...])
        dx_ref[...] = dh_ref[...] + dv
        _accum(dg_ref, dg)

    return _rows_call("pre_norm_bwd", body, [(dh1, True), (du, True), (x, True), (g, False)],
                      [(_sds((s, d), F32), True), (_sds((1, d), F32), False)], s, dep=dep)


def _forget_fwd(gf, b_pad, f_blk):
    s = gf.shape[0]
    tb = ATT_TK
    nb = s // tb

    def body(f_ref, b_ref, col_ref, row_ref):
        incl = _tri(tb, lambda r, c: c <= r)
        carry = jnp.zeros((1, LANES), F32)
        for i in range(nb):
            lf = _log_sigmoid(f_ref[pl.ds(i * tb, tb), :] + b_ref[...])
            parts = _split3(lf)
            cum = carry + _dot(incl, parts[0]) + _dot(incl, parts[1]) + _dot(incl, parts[2])
            col_ref[pl.ds(i * tb, tb), :] = cum
            row_ref[i] = cum.T
            carry = carry + jnp.sum(lf, axis=0, keepdims=True)

    return pl.pallas_call(
        body, name="forget_fwd", grid=(1,),
        in_specs=[pl.BlockSpec((s, LANES), lambda i: (0, f_blk)), pl.BlockSpec((1, LANES), lambda i: (0, 0))],
        out_specs=[pl.BlockSpec((s, LANES), lambda i: (0, 0)), pl.BlockSpec((nb, LANES, tb), lambda i: (0, 0, 0))],
        out_shape=[_sds((s, LANES), F32), _sds((nb, LANES, tb), F32)],
        compiler_params=_params(("arbitrary",)),
    )(gf, b_pad)


def _forget_bwd(dgf, dcum, gf, b_pad, f_blk):
    s = gf.shape[0]
    tb = ATT_TK
    nb = s // tb
    sec = dgf.shape[1] // F_PAD - 1

    def body(dgf_hbm, dc_ref, f_ref, b_ref, out_ref, db_ref):
        del dgf_hbm
        incl = _tri(tb, lambda r, c: c >= r)
        carry = jnp.zeros((1, LANES), F32)
        db = jnp.zeros((1, LANES), F32)
        out_ref[...] = jnp.zeros_like(out_ref)
        for i in reversed(range(nb)):
            dc = dc_ref[pl.ds(i * tb, tb), :]
            parts = _split3(dc)
            dlf = carry + _dot(incl, parts[0]) + _dot(incl, parts[1]) + _dot(incl, parts[2])
            z = f_ref[pl.ds(i * tb, tb), :] + b_ref[...]
            df = dlf * _sigmoid(-z)
            out_ref[pl.ds(i * tb, tb), pl.ds(0, LANES)] = df.astype(BF16)
            db = db + jnp.sum(df, axis=0, keepdims=True)
            carry = carry + jnp.sum(dc, axis=0, keepdims=True)
        db_ref[...] = db

    return pl.pallas_call(
        body, name="forget_bwd", grid=(1,),
        in_specs=[ANY, pl.BlockSpec((s, LANES), lambda i: (0, 0)),
                  pl.BlockSpec((s, LANES), lambda i: (0, f_blk)), pl.BlockSpec((1, LANES), lambda i: (0, 0))],
        out_specs=[pl.BlockSpec((s, F_PAD), lambda i: (0, sec)), pl.BlockSpec((1, LANES), lambda i: (0, 0))],
        out_shape=[_sds(dgf.shape, BF16), _sds((1, LANES), F32)],
        input_output_aliases={0: 0},
        compiler_params=_params(("arbitrary",)),
    )(dgf, dcum, gf, b_pad)


def _diag_mask(strict):
    r = lax.broadcasted_iota(jnp.int32, (ATT_TQ, ATT_TK), 0)
    c = lax.broadcasted_iota(jnp.int32, (ATT_TQ, ATT_TK), 1)
    return c < r if strict else c <= r


def _qkv_specs(hb0, s):
    specs = []
    for j in range(ATT_HP):
        def col(g, j=j):
            return 3 * (hb0 + ATT_HP * g + j)
        specs += [pl.BlockSpec((ATT_TQ, HEAD_DIM), lambda g, i, col=col: (i, col(g))),
                  pl.BlockSpec((s, HEAD_DIM), lambda g, i, col=col: (0, col(g) + 1)),
                  pl.BlockSpec((s, HEAD_DIM), lambda g, i, col=col: (0, col(g) + 2))]
    return specs


def _head_cols(j):
    return pl.ds(j * HEAD_DIM, HEAD_DIM)


def _sb_fwd(qkv, n_heads):
    s = qkv.shape[0]
    scale = HEAD_DIM ** -0.5
    tq, tk = ATT_TQ, ATT_TK
    heads = range(ATT_HP)

    def body(*refs):
        qkv_refs, (o_ref, ot_ref, tot_ref) = refs[:3 * ATT_HP], refs[3 * ATT_HP:]
        g, i = pl.program_id(0), pl.program_id(1)

        @pl.when((g == 0) & (i == 0))
        def _():
            tot_ref[...] = jnp.zeros_like(tot_ref)

        qs = [qkv_refs[3 * j][...] for j in heads]
        upper = _tri(tk, lambda r, c: r > c)

        def tile(kj, carry, mask):
            rows = pl.ds(pl.multiple_of(kj * tk, tk), tk)
            z = [_dot(qs[j], qkv_refs[3 * j + 1][rows, :], "nt") * scale for j in heads]
            lsz = [_log_sigmoid(z[j]) for j in heads]
            lk = [lsz[j] - z[j] if mask is None else jnp.where(mask, lsz[j] - z[j], 0.0) for j in heads]
            parts = [_split2(lk[j]) for j in heads]
            above = [carry[j][0] + _dot(parts[j][0], upper) + _dot(parts[j][1], upper) for j in heads]
            w = [jnp.exp(lsz[j] + above[j]) for j in heads]
            if mask is not None:
                w = [jnp.where(mask, w[j], 0.0) for j in heads]
            return tuple((carry[j][0] + jnp.sum(lk[j], axis=1, keepdims=True),
                          carry[j][1] + _dot(w[j], qkv_refs[3 * j + 2][rows, :])) for j in heads)

        carry = tile(i, tuple((jnp.zeros((tq, 1), F32), jnp.zeros((tq, HEAD_DIM), F32)) for _ in heads), _diag_mask(True))
        carry = lax.fori_loop(0, i, lambda n, cr: tile(i - 1 - n, cr, None), carry)
        q_rows = pl.ds(pl.multiple_of(i * tq, tq), tq)
        for j in heads:
            c, acc = carry[j]
            o = acc.astype(BF16)
            o_ref[:, _head_cols(j)] = o
            ot_ref[_head_cols(j), :] = o.T
            _lane_put(tot_ref, q_rows, ATT_HP * g + j, c)

    wide = ATT_HP * HEAD_DIM
    return pl.pallas_call(
        body, name="sb_fwd", grid=(n_heads // ATT_HP, s // tq),
        in_specs=_qkv_specs(0, s),
        out_specs=[pl.BlockSpec((tq, wide), lambda g, i: (i, g)), pl.BlockSpec((wide, tq), lambda g, i: (g, i)),
                   pl.BlockSpec((s, LANES), lambda g, i: (0, 0))],
        out_shape=[_sds((s, n_heads * HEAD_DIM), BF16), _sds((n_heads * HEAD_DIM, s), BF16), _sds((s, LANES), F32)],
        compiler_params=_params(("arbitrary", "arbitrary")),
    )(*[qkv] * (3 * ATT_HP))


def _sb_bwd(qkv, do, tot, n_heads, dep):
    s = qkv.shape[0]
    scale = HEAD_DIM ** -0.5
    tq, tk = ATT_TQ, ATT_TK
    nq = s // tq
    hd = HEAD_DIM

    heads = range(ATT_HP)

    def body(*refs):
        qkv_refs = refs[:3 * ATT_HP]
        do_ref, tot_ref, _, out_ref, dk_acc, dv_acc = refs[3 * ATT_HP:]
        g, i = pl.program_id(0), pl.program_id(1)

        @pl.when(i == 0)
        def _():
            dk_acc[...] = jnp.zeros_like(dk_acc)
            dv_acc[...] = jnp.zeros_like(dv_acc)

        qs = [qkv_refs[3 * j][...] for j in heads]
        douts = [do_ref[:, _head_cols(j)] for j in heads]
        totals = [_lane_pick(tot_ref[...], ATT_HP * g + j) for j in heads]
        incl = _tri(tk, lambda r, c: r <= c)
        excl = _tri(tk, lambda r, c: r < c)

        def tile(kj, carry, mask):
            rows = pl.ds(pl.multiple_of(kj * tk, tk), tk)
            k_t = [qkv_refs[3 * j + 1][rows, :] for j in heads]
            z = [_dot(qs[j], k_t[j], "nt") * scale for j in heads]
            dw = [_dot(douts[j], qkv_refs[3 * j + 2][rows, :], "nt") for j in heads]
            lsz = [_log_sigmoid(z[j]) for j in heads]
            lk = [lsz[j] - z[j] if mask is None else jnp.where(mask, lsz[j] - z[j], 0.0) for j in heads]
            parts = [_split2(lk[j]) for j in heads]
            below = [carry[j][0] + _dot(parts[j][0], incl) + _dot(parts[j][1], incl) for j in heads]
            w = [jnp.exp(lsz[j] + (totals[j] - below[j])) for j in heads]
            if mask is not None:
                w = [jnp.where(mask, w[j], 0.0) for j in heads]
            e = [dw[j] * w[j] for j in heads]
            parts = [_split2(e[j]) for j in heads]
            e_before = [carry[j][1] + _dot(parts[j][0], excl) + _dot(parts[j][1], excl) for j in heads]
            sg = [jnp.exp(lsz[j]) for j in heads]
            dz = [e[j] * (1.0 - sg[j]) - e_before[j] * sg[j] for j in heads]
            if mask is not None:
                dz = [jnp.where(mask, dz[j], 0.0) for j in heads]
            dz = [(dz[j] * scale).astype(BF16) for j in heads]
            for j in heads:
                dk_acc[j, rows, :] += _dot(dz[j], qs[j], "tn")
                dv_acc[j, rows, :] += _dot(w[j], douts[j], "tn")
            return tuple((carry[j][0] + jnp.sum(lk[j], axis=1, keepdims=True),
                          carry[j][1] + jnp.sum(e[j], axis=1, keepdims=True),
                          carry[j][2] + _dot(dz[j], k_t[j])) for j in heads)

        zero = jnp.zeros((tq, 1), F32)
        carry = lax.fori_loop(0, i, lambda kj, cr: tile(kj, cr, None),
                              tuple((zero, zero, jnp.zeros((tq, hd), F32)) for _ in heads))
        carry = tile(i, carry, _diag_mask(True))
        for j in heads:
            out_ref[pl.ds(pl.multiple_of(i * tq, tq), tq), pl.ds(3 * j * hd, hd)] = carry[j][2].astype(BF16)

        @pl.when(i == nq - 1)
        def _():
            for j in heads:
                out_ref[:, pl.ds((3 * j + 1) * hd, hd)] = dk_acc[j].astype(BF16)
                out_ref[:, pl.ds((3 * j + 2) * hd, hd)] = dv_acc[j].astype(BF16)

    wide = ATT_HP * hd
    return pl.pallas_call(
        body, name="sb_bwd", grid=(n_heads // ATT_HP, nq),
        in_specs=_qkv_specs(0, s) + [pl.BlockSpec((tq, wide), lambda g, i: (i, g)),
                                     pl.BlockSpec((tq, LANES), lambda g, i: (i, 0)), ANY],
        out_specs=pl.BlockSpec((s, 3 * wide), lambda g, i: (0, g)),
        out_shape=_sds(qkv.shape, BF16),
        scratch_shapes=[pltpu.VMEM((ATT_HP, s, hd), F32), pltpu.VMEM((ATT_HP, s, hd), F32)],
        compiler_params=_params(("arbitrary", "arbitrary")),
    )(*[qkv] * (3 * ATT_HP), do, tot, dep)


def _fox_fwd(qkv, cum_col, cum_row, n_heads, hb0, dep):
    s = qkv.shape[0]
    scale = HEAD_DIM ** -0.5
    tq, tk = ATT_TQ, ATT_TK

    heads = range(ATT_HP)

    def body(*refs):
        qkv_refs = refs[:3 * ATT_HP]
        cc_ref, cr_ref, _, o_ref, ot_ref, o32_ref, lse_ref = refs[3 * ATT_HP:]
        g, i = pl.program_id(0), pl.program_id(1)

        @pl.when((g == 0) & (i == 0))
        def _():
            lse_ref[...] = jnp.zeros_like(lse_ref)

        qs = [qkv_refs[3 * j][...] for j in heads]
        cqs = [_lane_pick(cc_ref[...], ATT_HP * g + j) for j in heads]

        def tile(kj, carry, mask):
            rows = pl.ds(pl.multiple_of(kj * tk, tk), tk)
            sc = [_dot(qs[j], qkv_refs[3 * j + 1][rows, :], "nt") * scale + cqs[j]
                  - cr_ref[kj, pl.ds(ATT_HP * g + j, 1), :] for j in heads]
            if mask is not None:
                sc = [jnp.where(mask, sc[j], NEG_BIG) for j in heads]
            m_new = [jnp.maximum(carry[j][0], jnp.max(sc[j], axis=1, keepdims=True)) for j in heads]
            p = [jnp.exp(sc[j] - m_new[j]) for j in heads]
            alpha = [jnp.exp(carry[j][0] - m_new[j]) for j in heads]
            parts = [_split2(p[j]) for j in heads]
            v_t = [qkv_refs[3 * j + 2][rows, :] for j in heads]
            pv = [_dot(parts[j][0], v_t[j]) + _dot(parts[j][1], v_t[j]) for j in heads]
            return tuple((m_new[j], alpha[j] * carry[j][1] + jnp.sum(p[j], axis=1, keepdims=True),
                          alpha[j] * carry[j][2] + pv[j]) for j in heads)

        carry = tuple((jnp.full((tq, 1), NEG_BIG, F32), jnp.zeros((tq, 1), F32), jnp.zeros((tq, HEAD_DIM), F32))
                      for _ in heads)
        carry = lax.fori_loop(0, i, lambda kj, cr: tile(kj, cr, None), carry)
        carry = tile(i, carry, _diag_mask(False))
        q_rows = pl.ds(pl.multiple_of(i * tq, tq), tq)
        for j in heads:
            m, l, acc = carry[j]
            o = acc / l
            o_ref[:, _head_cols(j)] = o.astype(BF16)
            ot_ref[_head_cols(j), :] = o.astype(BF16).T
            o32_ref[:, _head_cols(j)] = o
            _lane_put(lse_ref, q_rows, ATT_HP * g + j, m + jnp.log(l))

    nb = cum_row.shape[0]
    wide = ATT_HP * HEAD_DIM
    return pl.pallas_call(
        body, name="fox_fwd", grid=(n_heads // ATT_HP, s // tq),
        in_specs=_qkv_specs(hb0, s) + [pl.BlockSpec((tq, LANES), lambda g, i: (i, 0)),
                                       pl.BlockSpec((nb, 8, tk), lambda g, i: (0, 0, 0)), ANY],
        out_specs=[pl.BlockSpec((tq, wide), lambda g, i: (i, g)), pl.BlockSpec((wide, tq), lambda g, i: (g, i)),
                   pl.BlockSpec((tq, wide), lambda g, i: (i, g)), pl.BlockSpec((s, LANES), lambda g, i: (0, 0))],
        out_shape=[_sds((s, n_heads * HEAD_DIM), BF16), _sds((n_heads * HEAD_DIM, s), BF16),
                   _sds((s, n_heads * HEAD_DIM), F32), _sds((s, LANES), F32)],
        compiler_params=_params(("arbitrary", "arbitrary")),
    )(*[qkv] * (3 * ATT_HP), cum_col, cum_row, dep)


def _fox_bwd(dqkv, qkv, do, o, lse, cum_col, cum_row, n_heads, hb0, dep):
    s = qkv.shape[0]
    scale = HEAD_DIM ** -0.5
    tq, tk = ATT_TQ, ATT_TK
    nq = s // tq
    hd = HEAD_DIM

    heads = range(ATT_HP)
    assert hb0 % ATT_HP == 0

    def body(*refs):
        qkv_refs = refs[1:1 + 3 * ATT_HP]
        do_ref, o_ref, lse_ref, cc_ref, cr_ref, _, out_ref, dc_ref, dk_acc, dv_acc, col_acc = refs[1 + 3 * ATT_HP:]
        g, i = pl.program_id(0), pl.program_id(1)

        @pl.when((g == 0) & (i == 0))
        def _():
            dc_ref[...] = jnp.zeros_like(dc_ref)

        @pl.when(i == 0)
        def _():
            dk_acc[...] = jnp.zeros_like(dk_acc)
            dv_acc[...] = jnp.zeros_like(dv_acc)
            col_acc[...] = jnp.zeros_like(col_acc)

        qs = [qkv_refs[3 * j][...] for j in heads]
        douts = [do_ref[:, _head_cols(j)] for j in heads]
        deltas = [jnp.sum(douts[j].astype(F32) * o_ref[:, _head_cols(j)], axis=1, keepdims=True) for j in heads]
        shifts = [_lane_pick(cc_ref[...], ATT_HP * g + j) - _lane_pick(lse_ref[...], ATT_HP * g + j) for j in heads]

        def tile(kj, carry, mask):
            rows = pl.ds(pl.multiple_of(kj * tk, tk), tk)
            k_t = [qkv_refs[3 * j + 1][rows, :] for j in heads]
            sc = [_dot(qs[j], k_t[j], "nt") * scale + shifts[j] - cr_ref[kj, pl.ds(ATT_HP * g + j, 1), :] for j in heads]
            dp = [_dot(douts[j], qkv_refs[3 * j + 2][rows, :], "nt") for j in heads]
            p = [jnp.exp(sc[j]) for j in heads]
            if mask is not None:
                p = [jnp.where(mask, p[j], 0.0) for j in heads]
            ds_f = [p[j] * (dp[j] - deltas[j]) for j in heads]
            ds = [(ds_f[j] * scale).astype(BF16) for j in heads]
            for j in heads:
                col_acc[j, kj] += jnp.broadcast_to(jnp.sum(ds_f[j], axis=0, keepdims=True), (8, tk))
                dk_acc[j, rows, :] += _dot(ds[j], qs[j], "tn")
                dv_acc[j, rows, :] += _dot(p[j], douts[j], "tn")
            return tuple((carry[j][0] + _dot(ds[j], k_t[j]), carry[j][1] + jnp.sum(ds_f[j], axis=1, keepdims=True))
                         for j in heads)

        carry = lax.fori_loop(0, i, lambda kj, cr: tile(kj, cr, None),
                              tuple((jnp.zeros((tq, hd), F32), jnp.zeros((tq, 1), F32)) for _ in heads))
        carry = tile(i, carry, _diag_mask(False))
        q_rows = pl.ds(pl.multiple_of(i * tq, tq), tq)
        for j in heads:
            out_ref[q_rows, pl.ds(3 * j * hd, hd)] = carry[j][0].astype(BF16)
            _lane_put(dc_ref, q_rows, ATT_HP * g + j, carry[j][1])

        @pl.when(i == nq - 1)
        def _():
            lane = lax.broadcasted_iota(jnp.int32, (tk, LANES), 1)
            for j in heads:
                out_ref[:, pl.ds((3 * j + 1) * hd, hd)] = dk_acc[j].astype(BF16)
                out_ref[:, pl.ds((3 * j + 2) * hd, hd)] = dv_acc[j].astype(BF16)
                for kj in range(nb):
                    col = jnp.broadcast_to(col_acc[j, kj][0:1, :], (LANES, tk)).T
                    old = dc_ref[pl.ds(kj * tk, tk), :]
                    dc_ref[pl.ds(kj * tk, tk), :] = jnp.where(lane == ATT_HP * g + j, old - col, old)

    nb = cum_row.shape[0]
    wide = ATT_HP * hd
    return pl.pallas_call(
        body, name="fox_bwd", grid=(n_heads // ATT_HP, nq),
        in_specs=[ANY] + _qkv_specs(hb0, s) + [
            pl.BlockSpec((tq, wide), lambda g, i: (i, g)), pl.BlockSpec((tq, wide), lambda g, i: (i, g)),
            pl.BlockSpec((tq, LANES), lambda g, i: (i, 0)), pl.BlockSpec((tq, LANES), lambda g, i: (i, 0)),
            pl.BlockSpec((nb, 8, tk), lambda g, i: (0, 0, 0)), ANY],
        out_specs=[pl.BlockSpec((s, 3 * wide), lambda g, i: (0, hb0 // ATT_HP + g)),
                   pl.BlockSpec((s, LANES), lambda g, i: (0, 0))],
        out_shape=[_sds(dqkv.shape, BF16), _sds((s, LANES), F32)],
        scratch_shapes=[pltpu.VMEM((ATT_HP, s, hd), F32), pltpu.VMEM((ATT_HP, s, hd), F32),
                        pltpu.VMEM((ATT_HP, s // tk, 8, tk), F32)],
        input_output_aliases={0: 0},
        compiler_params=_params(("arbitrary", "arbitrary")),
    )(dqkv, *[qkv] * (3 * ATT_HP), do, o, lse, cum_col, cum_row, dep)


def _branch_merge(o_sb, o_fx, w_sb, w_fx, gf, dep, tm=1024):
    s = o_sb.shape[0]
    cs = w_sb.shape[2]
    tm = _tile(s, tm)

    def body(osb_ref, ofx_ref, wsb_ref, wfx_ref, g_ref, dep_ref, merged_ref, mt_ref, asb_ref, afx_ref):
        del dep_ref
        a_sb = _dot(osb_ref[...], wsb_ref[...])
        a_fx = _dot(ofx_ref[...], wfx_ref[...])
        g = g_ref[...]
        merged = (_sigmoid(g[:, :cs]) * a_sb + _sigmoid(g[:, cs:]) * a_fx).astype(BF16)
        merged_ref[...] = merged
        mt_ref[...] = merged.T
        asb_ref[...] = a_sb.astype(BF16)
        afx_ref[...] = a_fx.astype(BF16)

    blk = pl.BlockSpec((tm, cs), lambda i, j: (i, j))
    out = _sds((s, N_DEV * cs), BF16)
    return pl.pallas_call(
        body, name="branch_merge", grid=(s // tm, N_DEV),
        in_specs=[pl.BlockSpec((tm, o_sb.shape[1]), lambda i, j: (i, 0)),
                  pl.BlockSpec((tm, o_fx.shape[1]), lambda i, j: (i, 0)),
                  pl.BlockSpec((None,) + w_sb.shape[1:], lambda i, j: (j, 0, 0)),
                  pl.BlockSpec((None,) + w_fx.shape[1:], lambda i, j: (j, 0, 0)),
                  pl.BlockSpec((tm, 2 * cs), lambda i, j: (i, j)), ANY],
        out_specs=[blk, pl.BlockSpec((cs, tm), lambda i, j: (j, i)), blk, blk],
        out_shape=[out, _sds((N_DEV * cs, s), BF16), out, out],
        compiler_params=_params(("parallel", "arbitrary")),
    )(o_sb, o_fx, w_sb, w_fx, gf, dep)


def _merge_bwd(dmix, w_out, gf, a_sb, a_fx, tm=1024, tk=2048, dep=None):
    s, d = dmix.shape
    cs = d // N_DEV
    tm, tk = _tile(s, tm), _tile(d, tk)

    def epilogue(acc, ex, outs):
        g, a_sb, a_fx = ex[0][...], ex[1][...].astype(F32), ex[2][...].astype(F32)
        s_sb, s_fx = _sigmoid(g[:, :cs]), _sigmoid(g[:, cs:])
        outs[0][...] = (acc * s_sb).astype(BF16)
        outs[1][...] = (acc * s_fx).astype(BF16)
        outs[2][...] = jnp.concatenate([acc * a_sb * s_sb * (1.0 - s_sb), acc * a_fx * s_fx * (1.0 - s_fx)],
                                       axis=1).astype(BF16)

    blk = pl.BlockSpec((tm, cs), lambda i, j, k: (i, j))
    wide = pl.BlockSpec((tm, 2 * cs), lambda i, j, k: (i, j))
    return _matmul(
        "merge_bwd", "nt",
        [(dmix, pl.BlockSpec((tm, tk), lambda i, j, k: (i, k)), w_out, pl.BlockSpec((cs, tk), lambda i, j, k: (j, k)))],
        (s // tm, N_DEV, d // tk), (tm, cs),
        [_sds((s, d), BF16), _sds((s, d), BF16), _sds(gf.shape, BF16)], [blk, blk, wide],
        extras=[(gf, wide), (a_sb, blk), (a_fx, blk)], epilogue=epilogue, dep=dep)


def _ffn_up(u2, w_gate, w_up, tm=1024):
    s, d = u2.shape
    fs = w_gate.shape[2]
    tm = _tile(s, tm)

    def body(u_ref, wg_ref, wu_ref, gate_ref, up_ref, act_ref, actt_ref):
        u = u_ref[...]
        gate = _dot(u, wg_ref[...])
        up = _dot(u, wu_ref[...])
        gate_ref[...] = gate
        up_ref[...] = up
        act = (gate * _sigmoid(gate) * up).astype(BF16)
        act_ref[...] = act
        actt_ref[...] = act.T

    w_spec = pl.BlockSpec((None, d, fs), lambda i, j: (j, 0, 0))
    o_spec = pl.BlockSpec((None, tm, fs), lambda i, j: (j, i, 0))
    return pl.pallas_call(
        body, name="ffn_up", grid=(s // tm, N_DEV),
        in_specs=[pl.BlockSpec((tm, d), lambda i, j: (i, 0)), w_spec, w_spec],
        out_specs=[o_spec, o_spec, o_spec, pl.BlockSpec((None, fs, tm), lambda i, j: (j, 0, i))],
        out_shape=[_sds((N_DEV, s, fs), F32), _sds((N_DEV, s, fs), F32), _sds((N_DEV, s, fs), BF16),
                   _sds((N_DEV, fs, s), BF16)],
        compiler_params=_params(("parallel", "arbitrary")),
    )(u2, w_gate, w_up)


def _ffn_down_bwd(dff, w_down, gate, up, tm=1024):
    s, d = dff.shape
    fs = w_down.shape[1]
    tm = _tile(s, tm)

    def body(dff_ref, wd_ref, gate_ref, up_ref, dgate_ref, dup_ref):
        dact = _dot(dff_ref[...], wd_ref[...], "nt")
        gate = gate_ref[...]
        sg = _sigmoid(gate)
        dup_ref[...] = (dact * gate * sg).astype(BF16)
        dgate_ref[...] = (dact * up_ref[...] * sg * (1.0 + gate * (1.0 - sg))).astype(BF16)

    a_spec = pl.BlockSpec((None, tm, fs), lambda i, j: (j, i, 0))
    return pl.pallas_call(
        body, name="ffn_down_bwd", grid=(s // tm, N_DEV),
        in_specs=[pl.BlockSpec((tm, d), lambda i, j: (i, 0)), pl.BlockSpec((None, fs, d), lambda i, j: (j, 0, 0)),
                  a_spec, a_spec],
        out_specs=[a_spec, a_spec],
        out_shape=[_sds((N_DEV, s, fs), BF16), _sds((N_DEV, s, fs), BF16)],
        compiler_params=_params(("parallel", "arbitrary")),
    )(dff, w_down, gate, up)


def _mesh_place():
    x, y, c = lax.axis_index("x"), lax.axis_index("y"), lax.axis_index("c")
    peers = []
    for d in range(1, N_DEV):
        px = 1 - x if d & 4 else x
        py = 1 - y if d & 2 else y
        pc = 1 - c if d & 1 else c
        peers.append((d, (px, py, pc), 4 * px + 2 * py + pc))
    return 4 * x + 2 * y + c, peers


def _flat_me():
    return 4 * lax.axis_index("x") + 2 * lax.axis_index("y") + lax.axis_index("c")


def _in_hbm(a):
    return pltpu.with_memory_space_constraint(a, pltpu.HBM)


def _pair_plan():
    x, y, c = lax.axis_index("x"), lax.axis_index("y"), lax.axis_index("c")
    return [(2 * q + (1 - c), q, q, (x, y, 1 - c)) for q in range(4)]


def _chip_plan():
    x, y, c = lax.axis_index("x"), lax.axis_index("y"), lax.axis_index("c")
    plan = []
    for fx, fy in ((1, 0), (0, 1), (1, 1)):
        cx, cy = (1 - x if fx else x), (1 - y if fy else y)
        plan.append((2 * cx + cy, 2 * x + y, 2 * cx + cy, (cx, cy, c)))
    return plan


def _split_start(name, srcs, lands, plan, k):
    n = len(srcs)

    def body(*refs):
        ins, lnd = refs[:n], refs[n:2 * n]
        send, recv, token = refs[2 * n], refs[2 * n + 1], refs[-1]
        copies = plan()
        for a in range(n):
            for t, (src, dst, _, dev) in enumerate(copies):
                pltpu.make_async_remote_copy(src_ref=ins[a].at[src], dst_ref=lnd[a].at[dst], send_sem=send.at[k * a + t],
                                             recv_sem=recv.at[k * a + t], device_id=dev, device_id_type=MESH).start()
        token[...] = jnp.zeros_like(token)

    res = pl.pallas_call(
        body, name=name,
        out_shape=[pltpu.SemaphoreType.DMA((n * k,)), pltpu.SemaphoreType.DMA((n * k,))]
        + [pltpu.HBM(a.shape, a.dtype) for a in list(srcs) + list(lands)] + [_sds((8, LANES), F32)],
        in_specs=[HBM] * (2 * n), out_specs=[SEM, SEM] + [HBM] * (2 * n) + [pl.BlockSpec(memory_space=pltpu.VMEM)],
        input_output_aliases={i: 2 + i for i in range(2 * n)},
        compiler_params=pltpu.CompilerParams(has_side_effects=EFFECT),
    )(*[_in_hbm(a) for a in srcs], *[_in_hbm(a) for a in lands])
    return res[0], res[1], res[2:2 + n], res[2 + n:2 + 2 * n], res[-1]


def _split_wait(name, send, recv, srcs, lands, plan, k, after):
    n = len(srcs)

    def body(*refs):
        ins, lnd = refs[:n], refs[n:2 * n]
        send_sem, recv_sem = refs[2 * n], refs[2 * n + 1]
        copies = plan()
        for a in range(n):
            for t, (src, _, dst, dev) in enumerate(copies):
                cp = pltpu.make_async_remote_copy(src_ref=ins[a].at[src], dst_ref=lnd[a].at[dst], send_sem=send_sem.at[k * a + t],
                                                  recv_sem=recv_sem.at[k * a + t], device_id=dev, device_id_type=MESH)
                cp.wait_send()
                cp.wait_recv()

    res = pl.pallas_call(
        body, name=name,
        out_shape=[pltpu.HBM(a.shape, a.dtype) for a in list(srcs) + list(lands)],
        in_specs=[HBM] * (2 * n) + [SEM, SEM] + [ANY] * len(after), out_specs=[HBM] * (2 * n),
        input_output_aliases={i: i for i in range(2 * n)},
        compiler_params=pltpu.CompilerParams(has_side_effects=EFFECT),
    )(*srcs, *lands, send, recv, *after)
    return res[:n], res[n:]


def _pair_add(name, parts, land):
    _, r, cols = parts.shape
    tr = max(16, min(r, ((1 << 20) // (2 * cols)) // 16 * 16))
    while r % tr:
        tr -= 16

    def body(c_ref, p_ref, l_ref, o_ref):
        del c_ref
        o_ref[...] = (p_ref[...].astype(F32) + l_ref[...].astype(F32)).astype(BF16)

    blk = pl.BlockSpec((None, tr, cols), lambda q, i, c_ref: (q, i, 0))
    return pl.pallas_call(
        body, name=name,
        grid_spec=pltpu.PrefetchScalarGridSpec(
            num_scalar_prefetch=1, grid=(4, r // tr),
            in_specs=[pl.BlockSpec((None, tr, cols), lambda q, i, c_ref: (2 * q + c_ref[0], i, 0)), blk], out_specs=blk),
        out_shape=_sds((4, r, cols), BF16),
        compiler_params=_params(("parallel", "parallel")),
    )(jnp.reshape(lax.axis_index("c"), (1,)).astype(jnp.int32), parts, land)


def _scatter_pairs(tag, parts):
    lands = [lax.empty((4,) + a.shape[1:], a.dtype) for a in parts]
    return _split_start("pair_" + tag, parts, lands, _pair_plan, 4)


def _scatter_chips(tag, started, after):
    send, recv, parts, lands, _ = started
    parts, lands = _split_wait("pair_" + tag + "_wait", send, recv, parts, lands, _pair_plan, 4, [after])
    sums = [_pair_add("pair_" + tag + "_add%d" % a, p, l) for a, (p, l) in enumerate(zip(parts, lands))]
    chip = 2 * lax.axis_index("x") + lax.axis_index("y")
    final = [lax.dynamic_update_slice_in_dim(lax.empty(v.shape, v.dtype), lax.dynamic_slice_in_dim(v, chip, 1, 0), chip, 0)
             for v in sums]
    return _split_start("chips_" + tag, sums, final, _chip_plan, 3)


def _scatter_end(tag, started, after):
    send, recv, sums, final, _ = started
    return _split_wait("chips_" + tag + "_wait", send, recv, sums, final, _chip_plan, 3, after)[1]


def _gather_targets():
    x, y, c = lax.axis_index("x"), lax.axis_index("y"), lax.axis_index("c")
    chips = [(x, y), (1 - x, y), (x, 1 - y), (1 - x, 1 - y)]
    same = [((cx, cy, c), 4 * cx + 2 * cy + c) for cx, cy in chips]
    other = [((cx, cy, 1 - c), 4 * cx + 2 * cy + 1 - c) for cx, cy in chips]
    return same[0][1], [other[0]] + same[1:], [flat for _, flat in other[1:]], other[0][0]


def _gather_start(shards):
    n = len(shards)
    me = _flat_me()
    lands = [lax.dynamic_update_slice_in_dim(lax.empty((N_DEV,) + a.shape, a.dtype), a[None], me, 0) for a in shards]

    def body(*refs):
        lnd, send, recv, token = refs[:n], refs[n], refs[n + 1], refs[-1]
        mine, targets, _, _ = _gather_targets()
        for a in range(n):
            for t, (dev, _) in enumerate(targets):
                pltpu.make_async_remote_copy(src_ref=lnd[a].at[mine], dst_ref=lnd[a].at[mine], send_sem=send.at[4 * a + t],
                                             recv_sem=recv.at[4 * a + t], device_id=dev, device_id_type=MESH).start()
        token[...] = jnp.zeros_like(token)

    res = pl.pallas_call(
        body, name="gather_start",
        out_shape=[pltpu.SemaphoreType.DMA((4 * n,)), pltpu.SemaphoreType.DMA((4 * n,))]
        + [pltpu.HBM(a.shape, a.dtype) for a in lands] + [_sds((8, LANES), F32)],
        in_specs=[HBM] * n, out_specs=[SEM, SEM] + [HBM] * n + [pl.BlockSpec(memory_space=pltpu.VMEM)],
        input_output_aliases={i: 2 + i for i in range(n)},
        compiler_params=pltpu.CompilerParams(has_side_effects=EFFECT),
    )(*[_in_hbm(a) for a in lands])
    return res[0], res[1], list(res[2:2 + n]), res[-1]


def _gather_forward(name, lands, first, send, recv, after):
    n = len(lands)

    def body(*refs):
        lnd, send_sem, recv_sem = refs[:n], refs[n], refs[n + 1]
        send2, recv2, token = refs[-3], refs[-2], refs[-1]
        mine, targets, _, sibling = _gather_targets()
        for a in range(n):
            for t, (dev, flat) in enumerate(targets):
                cp = pltpu.make_async_remote_copy(src_ref=lnd[a].at[mine], dst_ref=lnd[a].at[flat],
                                                  send_sem=send_sem.at[4 * (first + a) + t],
                                                  recv_sem=recv_sem.at[4 * (first + a) + t], device_id=dev, device_id_type=MESH)
                cp.wait_send()
                if t:
                    cp.wait_recv()
                    pltpu.make_async_remote_copy(src_ref=lnd[a].at[flat], dst_ref=lnd[a].at[flat], send_sem=send2.at[3 * a + t - 1],
                                                 recv_sem=recv2.at[3 * a + t - 1], device_id=sibling, device_id_type=MESH).start()
        token[...] = jnp.zeros_like(token)

    res = pl.pallas_call(
        body, name=name,
        out_shape=[pltpu.HBM(a.shape, a.dtype) for a in lands]
        + [pltpu.SemaphoreType.DMA((3 * n,)), pltpu.SemaphoreType.DMA((3 * n,)), _sds((8, LANES), F32)],
        in_specs=[HBM] * n + [SEM, SEM, ANY], out_specs=[HBM] * n + [SEM, SEM, pl.BlockSpec(memory_space=pltpu.VMEM)],
        input_output_aliases={i: i for i in range(n)},
        compiler_params=pltpu.CompilerParams(has_side_effects=EFFECT),
    )(*lands, send, recv, after)
    return list(res[:n]), res[n], res[n + 1], res[-1]


def _gather_wait(name, lands, first, recv, send2, recv2, after):
    n = len(lands)

    def body(*refs):
        lnd, recv_sem, send2_sem, recv2_sem = refs[:n], refs[n], refs[n + 1], refs[n + 2]
        mine, targets, passed, sibling = _gather_targets()
        for a in range(n):
            dev, flat = targets[0]
            pltpu.make_async_remote_copy(src_ref=lnd[a].at[mine], dst_ref=lnd[a].at[flat], send_sem=send2_sem.at[3 * a],
                                         recv_sem=recv_sem.at[4 * (first + a)], device_id=dev, device_id_type=MESH).wait_recv()
            for t in range(3):
                cp = pltpu.make_async_remote_copy(src_ref=lnd[a].at[targets[t + 1][1]], dst_ref=lnd[a].at[passed[t]],
                                                  send_sem=send2_sem.at[3 * a + t], recv_sem=recv2_sem.at[3 * a + t],
                                                  device_id=sibling, device_id_type=MESH)
                cp.wait_send()
                cp.wait_recv()

    res = pl.pallas_call(
        body, name=name, out_shape=[pltpu.HBM(a.shape, a.dtype) for a in lands],
        in_specs=[HBM] * n + [SEM, SEM, SEM, ANY], out_specs=[HBM] * n,
        input_output_aliases={i: i for i in range(n)},
        compiler_params=pltpu.CompilerParams(has_side_effects=EFFECT),
    )(*lands, recv, send2, recv2, after)
    return list(res)


def _adamw(g, w, m, v):
    m = ADAM_B1 * m + (1.0 - ADAM_B1) * g
    v = ADAM_B2 * v + (1.0 - ADAM_B2) * (g * g)
    m_hat = m / (1.0 - ADAM_B1 ** ADAM_STEP)
    v_hat = v / (1.0 - ADAM_B2 ** ADAM_STEP)
    delta = -ADAM_LR * (m_hat / (jnp.sqrt(v_hat) + ADAM_EPS) + ADAM_WD * w)
    return delta, m, v


def _update(name, parts, w, m, v, layout=None, block_bytes=1 << 20):
    _, r, c = w.shape
    n_slots, _, cp = parts.shape
    tr = max(8, min(r, (block_bytes // (4 * cp)) // 8 * 8))
    while r % tr:
        tr -= 8

    def body(p_ref, w_ref, m_ref, v_ref, g_ref, d_ref, nm_ref, nv_ref, *scratch):
        g = p_ref[0].astype(F32)
        for p in range(1, n_slots):
            g = g + p_ref[p].astype(F32)
        if layout is not None:
            s1, s2, lg = layout.my_shifts()
            lane = lax.broadcasted_iota(jnp.int32, g.shape, 1)
            scratch[0][...] = jnp.where(lane < lg, pltpu.roll(g, cp - s1, 1), pltpu.roll(g, cp - s2, 1))
            g = scratch[0][:, 0:c]
        g_ref[...] = g
        d_ref[...], nm_ref[...], nv_ref[...] = _adamw(g, w_ref[...], m_ref[...], v_ref[...])

    blk = pl.BlockSpec((None, tr, c), lambda i: (0, i, 0))
    return pl.pallas_call(
        body, name=name, grid=(r // tr,),
        in_specs=[pl.BlockSpec((n_slots, tr, cp), lambda i: (0, i, 0)), blk, blk, blk],
        out_specs=[blk] * 4, out_shape=[_sds((1, r, c), F32)] * 4,
        scratch_shapes=[] if layout is None else [pltpu.VMEM((tr, cp), F32)],
        compiler_params=_params(("parallel",)),
    )(parts, w, m, v)


def _small_update(part, w, m, v):
    n = part.shape[1]

    def body(p_ref, w_ref, m_ref, v_ref, g_ref, d_ref, nm_ref, nv_ref, buf, send, recv):
        me, peers = _mesh_place()
        buf[me] = p_ref[...]
        sent = []
        for d, dev, flat in peers:
            cp = pltpu.make_async_remote_copy(src_ref=p_ref, dst_ref=buf.at[me], send_sem=send.at[d],
                                              recv_sem=recv.at[d], device_id=dev, device_id_type=MESH)
            cp.start()
            sent.append(cp)
        for d, dev, flat in peers:
            pltpu.make_async_remote_copy(src_ref=p_ref, dst_ref=buf.at[flat], send_sem=send.at[d],
                                         recv_sem=recv.at[d], device_id=dev, device_id_type=MESH).wait_recv()
        for cp in sent:
            cp.wait_send()
        g = buf[0]
        for p in range(1, N_DEV):
            g = g + buf[p]
        g_ref[...] = g
        d_ref[...], nm_ref[...], nv_ref[...] = _adamw(g, w_ref[...], m_ref[...], v_ref[...])

    vm = pl.BlockSpec(memory_space=pltpu.VMEM)
    return pl.pallas_call(
        body, name="small_update", in_specs=[vm] * 4, out_specs=[vm] * 4, out_shape=[_sds((1, n), F32)] * 4,
        scratch_shapes=[pltpu.VMEM((N_DEV, 1, n), F32), pltpu.SemaphoreType.DMA((N_DEV,)),
                        pltpu.SemaphoreType.DMA((N_DEV,))],
    )(part, w, m, v)


class _WInLayout:
    def __init__(self, n8, n_f, d_sb, d_fox, d):
        assert n8 % LANES == 1 and n_f < LANES and d % (N_DEV * LANES) == 0
        self.n8, self.n_f, self.d = n8, n_f, d
        self.sp = n8 // LANES
        self.wp = (n8 + 2 * LANES - 2) // LANES * LANES
        self.n_qkv = 3 * (d_sb + d_fox)
        nq, dt, tc = self.n_qkv // LANES, d // LANES, d // N_DEV // LANES
        h_sb, h_fox = d_sb // HEAD_DIM, d_fox // HEAD_DIM
        self.sources = {}
        self.part_tile = {}
        for p in range(N_DEV):
            lg = min(max(self.n_qkv + n_f - n8 * p, 0), n8)
            s1, s2 = p, p + LANES - n_f
            spans = []
            if lg > 0:
                spans.append(("a", self.sp * p, s1 // LANES, (lg + s1 - 1) // LANES))
            if lg < n8:
                spans.append(("g", self.sp * p - 1 - nq, (lg + s2) // LANES, (n8 - 1 + s2) // LANES))
            for kind, base, first, last in spans:
                for i in range(first, last + 1):
                    assert (p, i) not in self.part_tile
                    self.part_tile[(p, i)] = (kind, base + i)
                    self.sources.setdefault((kind, base + i), []).append((p, i))
        self.cat_tiles = [("a", r * h_sb + h) for h in range(h_sb) for r in range(3)]
        self.cat_tiles += [("a", 3 * h_sb + r * h_fox + h) for h in range(h_fox) for r in range(3)]
        self.cat_tiles += [("g", which * dt + j * tc + half) for j in range(N_DEV) for which in (0, 1) for half in range(tc)]
        self.cat_tiles += [("a", nq)] + [None] * (F_PAD // LANES - 1)
        self.cat_index = {key: c for c, key in enumerate(self.cat_tiles) if key is not None}

    def my_shifts(self):
        me = _flat_me()
        return me, me + LANES - self.n_f, jnp.clip(self.n_qkv + self.n_f - self.n8 * me, 0, self.n8)


def _lane_tile(i):
    return pl.ds(i * LANES, LANES)


def _w_in_shift(w_in, lay, tr=256):
    _, d, n8 = w_in.shape

    def body(w_ref, o_ref, buf):
        buf[...] = jnp.zeros_like(buf)
        buf[:, 0:n8] = w_ref[...]
        v = buf[...]
        s1, s2, lg = lay.my_shifts()
        pos = lax.broadcasted_iota(jnp.int32, v.shape, 1)
        o_ref[...] = jnp.where(pos < lg + s1, pltpu.roll(v, s1, 1),
                               jnp.where(pos >= lg + s2, pltpu.roll(v, s2, 1), 0.0)).astype(BF16)

    return pl.pallas_call(
        body, name="w_in_shift", grid=(d // tr,),
        in_specs=[pl.BlockSpec((None, tr, n8), lambda i: (0, i, 0))],
        out_specs=pl.BlockSpec((tr, lay.wp), lambda i: (i, 0)), out_shape=_sds((d, lay.wp), BF16),
        scratch_shapes=[pltpu.VMEM((tr, lay.wp), F32)],
        compiler_params=_params(("parallel",)),
    )(w_in)


def _w_in_build(g_in, lay, tr=256):
    d = g_in.shape[1]
    width = len(lay.cat_tiles) * LANES

    def body(g_ref, o_ref):
        for c, key in enumerate(lay.cat_tiles):
            if key is None:
                o_ref[:, _lane_tile(c)] = jnp.zeros((tr, LANES), BF16)
                continue
            (p, i), *more = lay.sources[key]
            val = g_ref[p, :, _lane_tile(i)]
            for p2, i2 in more:
                val = val + g_ref[p2, :, _lane_tile(i2)]
            o_ref[:, _lane_tile(c)] = val

    return pl.pallas_call(
        body, name="w_in_build", grid=(d // tr,),
        in_specs=[pl.BlockSpec((N_DEV, tr, lay.wp), lambda i: (0, i, 0))],
        out_specs=pl.BlockSpec((tr, width), lambda i: (i, 0)), out_shape=_sds((d, width), BF16),
        compiler_params=_params(("parallel",)),
    )(g_in)


def _w_in_grad_parts(dwq, dwgf, lay, tr=256):
    d = dwq.shape[0]
    nq = lay.n_qkv // LANES

    def body(q_ref, g_ref, o_ref):
        for p in range(N_DEV):
            for i in range(lay.wp // LANES):
                key = lay.part_tile.get((p, i))
                if key is None:
                    o_ref[p, :, _lane_tile(i)] = jnp.zeros((tr, LANES), BF16)
                    continue
                c = lay.cat_index[key]
                o_ref[p, :, _lane_tile(i)] = q_ref[:, _lane_tile(c)] if c < nq else g_ref[:, _lane_tile(c - nq)]

    return pl.pallas_call(
        body, name="w_in_grad_parts", grid=(d // tr,),
        in_specs=[pl.BlockSpec((tr, dwq.shape[1]), lambda i: (i, 0)), pl.BlockSpec((tr, dwgf.shape[1]), lambda i: (i, 0))],
        out_specs=pl.BlockSpec((N_DEV, tr, lay.wp), lambda i: (0, i, 0)), out_shape=_sds((N_DEV, d, lay.wp), BF16),
        compiler_params=_params(("parallel",)),
    )(dwq, dwgf)


def kernel(x, norm_mix_pre, norm_mix_post, w_in, b_forget, w_branch_sb, w_branch_fox, w_out, norm_ffn_pre, norm_ffn_post, w_ffn_gate, w_ffn_up, w_ffn_down, loss_target, m_norm_mix_pre, m_norm_mix_post, m_w_in, m_b_forget, m_w_branch_sb, m_w_branch_fox, m_w_out, m_norm_ffn_pre, m_norm_ffn_post, m_w_ffn_gate, m_w_ffn_up, m_w_ffn_down, v_norm_mix_pre, v_norm_mix_post, v_w_in, v_b_forget, v_w_branch_sb, v_w_branch_fox, v_w_out, v_norm_ffn_pre, v_norm_ffn_post, v_w_ffn_gate, v_w_ffn_up, v_w_ffn_down):
    xs, target = x[0], loss_target[0]
    s, d = xs.shape
    d_sb, d_fox = w_branch_sb.shape[1], w_branch_fox.shape[1]
    h_sb, h_fox = d_sb // HEAD_DIM, d_fox // HEAD_DIM
    n_f = b_forget.shape[1]
    fs = w_ffn_gate.shape[2]
    cs = d // N_DEV
    n_qkv = 3 * (d_sb + d_fox)
    n_gf = 2 * d + F_PAD
    f_blk = 2 * d // LANES
    big = (w_in, w_branch_sb, w_branch_fox, w_out, w_ffn_gate, w_ffn_up, w_ffn_down)
    big_m = (m_w_in, m_w_branch_sb, m_w_branch_fox, m_w_out, m_w_ffn_gate, m_w_ffn_up, m_w_ffn_down)
    big_v = (v_w_in, v_w_branch_sb, v_w_branch_fox, v_w_out, v_w_ffn_gate, v_w_ffn_up, v_w_ffn_down)

    lay = _WInLayout(w_in.shape[2], n_f, d_sb, d_fox, d)
    send1, recv1, lands, token = _gather_start([_w_in_shift(w_in, lay)] + [w[0].astype(BF16) for w in big[1:]])
    b_pad = jnp.pad(b_forget, ((0, 0), (0, LANES - n_f)))

    u, u_t = _pre_norm(xs, norm_mix_pre, dep=token)
    l_in, send2, recv2, token = _gather_forward("gather_in_forward", lands[0:1], 0, send1, recv1, u)
    (g_in,) = _gather_wait("gather_in_wait", l_in, 0, recv1, send2, recv2, token)
    w_cat = _w_in_build(g_in, lay)
    qkv = _mm_plain("proj_qkv", "nn", u, w_cat, BF16, n=n_qkv)
    gf = _mm_plain("proj_gates", "nn", u, w_cat, F32, n_off=n_qkv, n=n_gf)
    cum_col, cum_row = _forget_fwd(gf, b_pad, f_blk)
    o_sb, o_sb_t, tot = _sb_fwd(qkv, h_sb)
    l_mid, send2, recv2, token = _gather_forward("gather_mid_forward", lands[1:4], 1, send1, recv1, o_sb)
    o_fx, o_fx_t, o_fx32, lse = _fox_fwd(qkv, cum_col, cum_row, h_fox, h_sb, token)
    g_sb, g_fx, g_out = _gather_wait("gather_mid_wait", l_mid, 1, recv1, send2, recv2, o_fx)
    w_out_full = g_out.reshape(d, d)
    merged, merged_t, a_sb, a_fx = _branch_merge(o_sb, o_fx, g_sb, g_fx, gf, o_fx)
    l_ffn, send2, recv2, token = _gather_forward("gather_ffn_forward", lands[4:7], 4, send1, recv1, merged)
    mix = _mm_plain("out_proj", "nn", merged, w_out_full, F32, dep=token)
    h1, u2, u2_t = _mid_norms(xs, mix, norm_mix_post, norm_ffn_pre)
    g_gate, g_up, g_down = _gather_wait("gather_ffn_wait", l_ffn, 4, recv1, send2, recv2, u2)
    gate, up, act, act_t = _ffn_up(u2, g_gate, g_up)
    tm, tn = _tile(s, 1024), _tile(d, 1024)
    ff = _matmul("ffn_down", "nn",
                 [(act, pl.BlockSpec((None, tm, fs), lambda i, j, k: (k, i, 0)),
                   g_down, pl.BlockSpec((None, fs, tn), lambda i, j, k: (k, 0, j)))],
                 (s // tm, d // tn, N_DEV), (tm, tn), _sds((s, d), F32), pl.BlockSpec((tm, tn), lambda i, j, k: (i, j)))
    loss_part, dy, dff, dg_ffn_post = _loss_head(h1, ff, target, norm_ffn_post)

    dgate, dup = _ffn_down_bwd(dff, g_down, gate, up)
    dw_down = _matmul("dw_down", "nn",
                      [(act_t, pl.BlockSpec((None, fs, s), lambda j, n, k: (j, 0, 0)),
                        dff, pl.BlockSpec((s, tn), lambda j, n, k: (0, n)))],
                      (N_DEV, d // tn, 1), (fs, tn), _sds((N_DEV, fs, d), BF16),
                      pl.BlockSpec((None, fs, tn), lambda j, n, k: (j, 0, n)))

    def dw_up(name, dact):
        return _matmul(name, "nn",
                       [(u2_t, pl.BlockSpec((tn, s), lambda j, i, k: (i, 0)),
                         dact, pl.BlockSpec((None, s, fs), lambda j, i, k: (j, 0, 0)))],
                       (N_DEV, d // tn, 1), (tn, fs), _sds((N_DEV, d, fs), BF16),
                       pl.BlockSpec((None, tn, fs), lambda j, i, k: (j, i, 0)))

    dw_gate, dw_upw = dw_up("dw_gate", dgate), dw_up("dw_up", dup)
    rs_ffn = _scatter_pairs("ffn", [dw_gate, dw_upw, dw_down])
    a_spec = pl.BlockSpec((None, tm, fs), lambda i, j, k: (k, i, 0))
    b_spec = pl.BlockSpec((None, tn, fs), lambda i, j, k: (k, j, 0))
    du2 = _matmul("du2", "nt", [(dgate, a_spec, g_gate, b_spec), (dup, a_spec, g_up, b_spec)],
                  (s // tm, d // tn, N_DEV), (tm, tn), _sds((s, d), F32), pl.BlockSpec((tm, tn), lambda i, j, k: (i, j)),
                  dep=rs_ffn[4])
    rs_ffn = _scatter_chips("ffn", rs_ffn, du2)
    dh1, dmix, dg_ffn_pre, dg_mix_post = _mid_norms_bwd(dy, du2, h1, mix, norm_ffn_pre, norm_mix_post)

    da_sb, da_fx, dgf = _merge_bwd(dmix, w_out_full, gf, a_sb, a_fx, dep=rs_ffn[4])
    dw_out = _mm_plain("dw_out", "nn", merged_t, dmix, BF16).reshape(N_DEV, cs, d)

    def branch_bwd(tag, da, w_b, o_t, width):
        tb = _tile(width, 1024)
        do = _matmul("do_" + tag, "nt",
                     [(da, pl.BlockSpec((tm, cs), lambda i, j, k: (i, k)),
                       w_b, pl.BlockSpec((None, tb, cs), lambda i, j, k: (k, j, 0)))],
                     (s // tm, width // tb, N_DEV), (tm, tb), _sds((s, width), BF16),
                     pl.BlockSpec((tm, tb), lambda i, j, k: (i, j)))
        dw = _matmul("dw_" + tag, "nn",
                     [(o_t, pl.BlockSpec((width, s), lambda j, i, k: (0, 0)),
                       da, pl.BlockSpec((s, cs), lambda j, i, k: (0, j)))],
                     (N_DEV, 1, 1), (width, cs), _sds((N_DEV, width, cs), BF16),
                     pl.BlockSpec((None, width, cs), lambda j, i, k: (j, 0, 0)))
        return do, dw

    do_sb, dw_sb = branch_bwd("sb", da_sb, g_sb, o_sb_t, d_sb)
    do_fx, dw_fx = branch_bwd("fox", da_fx, g_fx, o_fx_t, d_fox)

    rs_mid = _scatter_pairs("mid", [dw_sb, dw_fx, dw_out])

    dqkv = _sb_bwd(qkv, do_sb, tot, h_sb, rs_mid[4])
    rs_mid = _scatter_chips("mid", rs_mid, dqkv)
    dqkv, dcum = _fox_bwd(dqkv, qkv, do_fx, o_fx32, lse, cum_col, cum_row, h_fox, h_sb, rs_mid[4])
    dgf, db_part = _forget_bwd(dgf, dcum, gf, b_pad, f_blk)
    dw_in = _w_in_grad_parts(_mm_plain("dw_qkv", "nn", u_t, dqkv, BF16), _mm_plain("dw_gates", "nn", u_t, dgf, BF16), lay)
    rs_in = _scatter_pairs("in", [dw_in])
    du = _mm_plain("du_qkv", "nt", dqkv, w_cat, F32, tn=1024, dep=rs_in[4])
    rs_in = _scatter_chips("in", rs_in, du)
    du = _mm_plain("du_gates", "nt", dgf, w_cat, F32, tn=1024, k_off=n_qkv, init=du, dep=rs_in[4])
    dx, dg_mix_pre = _pre_norm_bwd(dh1, du, xs, norm_mix_pre)

    upd = {}

    def update_group(tag, rs, names, after):
        parts = _scatter_end(tag, rs, after)
        for nm, p in zip(names, parts):
            w, m, v = weights[nm]
            upd[nm] = _update("update_" + nm, p, w, m, v, layout=lay if nm == "w_in" else None)

    weights = dict(zip(("w_in", "w_branch_sb", "w_branch_fox", "w_out", "w_ffn_gate", "w_ffn_up", "w_ffn_down"),
                       zip(big, big_m, big_v)))
    update_group("ffn", rs_ffn, ("w_ffn_gate", "w_ffn_up", "w_ffn_down"), [dx])
    update_group("mid", rs_mid, ("w_branch_sb", "w_branch_fox", "w_out"), [upd[nm][3] for nm in ("w_ffn_gate", "w_ffn_up", "w_ffn_down")])
    update_group("in", rs_in, ("w_in",), [upd[nm][3] for nm in ("w_branch_sb", "w_branch_fox", "w_out")])

    small = ((norm_mix_pre, m_norm_mix_pre, v_norm_mix_pre), (norm_mix_post, m_norm_mix_post, v_norm_mix_post),
             (norm_ffn_pre, m_norm_ffn_pre, v_norm_ffn_pre), (norm_ffn_post, m_norm_ffn_post, v_norm_ffn_post))
    pad_f = ((0, 0), (0, LANES - n_f))
    cat = lambda i: jnp.concatenate([t[i] for t in small] + [jnp.pad((b_forget, m_b_forget, v_b_forget)[i], pad_f)], axis=1)
    sm = _small_update(jnp.concatenate([dg_mix_pre, dg_mix_post, dg_ffn_pre, dg_ffn_post, db_part], axis=1),
                       cat(0), cat(1), cat(2))
    for i, nm in enumerate(("norm_mix_pre", "norm_mix_post", "norm_ffn_pre", "norm_ffn_post")):
        upd[nm] = [o[:, i * d:(i + 1) * d] for o in sm]
    upd["b_forget"] = [o[:, 4 * d:4 * d + n_f] for o in sm]

    loss = lax.psum(loss_part[0, 0], ("x", "y", "c"))
    order = ("norm_mix_pre", "norm_mix_post", "w_in", "b_forget", "w_branch_sb", "w_branch_fox", "w_out",
             "norm_ffn_pre", "norm_ffn_post", "w_ffn_gate", "w_ffn_up", "w_ffn_down")
    return (loss, dx[None]) + tuple(upd[nm][i] for i in range(4) for nm in order)
```

```python
import jax
import jax.numpy as jnp
from jax import lax
from jax.experimental import pallas as pl
from jax.experimental.pallas import tpu as pltpu

F32 = jnp.float32
BF16 = jnp.bfloat16
MESH = pl.DeviceIdType.MESH
ANY = pl.BlockSpec(memory_space=pl.ANY)
HBM = pl.BlockSpec(memory_space=pltpu.HBM)
SEM = pl.BlockSpec(memory_space=pltpu.SEMAPHORE)
EFFECT = pltpu.SideEffectType.DATAFLOW_SIDE_EFFECTING

N_DEV = 8
HEAD_DIM = 128
RMS_EPS = 1e-6
F_PAD = 512
LANES = 128
ATT_TQ = 256
ATT_TK = 256
ATT_HP = 2
NEG_BIG = -1e30
VMEM_LIMIT = 56 * 1024 * 1024

ADAM_LR = 0.001
ADAM_B1 = 0.9
ADAM_B2 = 0.999
ADAM_EPS = 1e-08
ADAM_WD = 0.01
ADAM_STEP = 10

_DIMS = {"nn": ((1,), (0,)), "nt": ((1,), (1,)), "tn": ((0,), (0,))}


def _params(sem):
    return pltpu.CompilerParams(dimension_semantics=sem, vmem_limit_bytes=VMEM_LIMIT)


def _dot(a, b, mode="nn"):
    return lax.dot_general(a.astype(BF16), b.astype(BF16), (_DIMS[mode], ((), ())), preferred_element_type=F32)


def _tile(n, pref):
    if n <= pref:
        return n
    t = (pref // LANES) * LANES
    while n % t:
        t -= LANES
    return t


def _split2(v):
    hi = v.astype(BF16)
    return hi, (v - hi.astype(F32)).astype(BF16)


def _split3(v):
    a = v.astype(BF16)
    r = v - a.astype(F32)
    b = r.astype(BF16)
    return a, b, (r - b.astype(F32)).astype(BF16)


def _tri(n, cmp):
    r = lax.broadcasted_iota(jnp.int32, (n, n), 0)
    c = lax.broadcasted_iota(jnp.int32, (n, n), 1)
    return jnp.where(cmp(r, c), 1.0, 0.0).astype(BF16)


def _lane_pick(v, h):
    lane = lax.broadcasted_iota(jnp.int32, v.shape, 1)
    return jnp.sum(jnp.where(lane == h, v, 0.0), axis=1, keepdims=True)


def _lane_put(ref, rows, h, col):
    old = ref[rows, :]
    lane = lax.broadcasted_iota(jnp.int32, old.shape, 1)
    ref[rows, :] = jnp.where(lane == h, col, old)


def _sigmoid(z):
    return 1.0 / (1.0 + jnp.exp(-z))


def _log_sigmoid(z):
    return jnp.minimum(z, 0.0) - jnp.log(1.0 + jnp.exp(-jnp.abs(z)))


def _sds(shape, dtype):
    return jax.ShapeDtypeStruct(shape, dtype)


def _matmul(name, mode, pairs, grid, acc_shape, out_shape, out_specs, extras=(), epilogue=None, init=None, dep=None):
    n_p, n_e = len(pairs), len(extras)
    nk = grid[-1]
    single = not isinstance(out_shape, (list, tuple))
    n_i = 0 if init is None else 1
    n_d = 0 if dep is None else 1

    one_step = nk == 1 and init is None

    def body(*refs):
        ab = refs[:2 * n_p]
        ex = refs[2 * n_p:2 * n_p + n_e]
        ini = refs[2 * n_p + n_e:2 * n_p + n_e + n_i]
        outs = refs[2 * n_p + n_e + n_i + n_d:len(refs) - (0 if one_step else 1)]

        def finish(total):
            if epilogue is None:
                outs[0][...] = total.astype(outs[0].dtype)
            else:
                epilogue(total, ex, outs)

        t = _dot(ab[0][...], ab[1][...], mode)
        for p in range(1, n_p):
            t = t + _dot(ab[2 * p][...], ab[2 * p + 1][...], mode)
        if one_step:
            finish(t)
            return
        acc = refs[-1]
        k = pl.program_id(len(grid) - 1)

        @pl.when(k == 0)
        def _():
            acc[...] = t if init is None else ini[0][...].astype(F32) + t

        @pl.when(k > 0)
        def _():
            acc[...] += t

        @pl.when(k == nk - 1)
        def _():
            finish(acc[...])

    in_specs = [s for (_, sa, _, sb) in pairs for s in (sa, sb)] + [s for (_, s) in extras]
    args = [v for (a, _, b, _) in pairs for v in (a, b)] + [e for (e, _) in extras]
    if init is not None:
        in_specs.append(init[1])
        args.append(init[0])
    if dep is not None:
        in_specs.append(ANY)
        args.append(dep)
    return pl.pallas_call(
        body, name=name, grid=grid, in_specs=in_specs,
        out_specs=out_specs if single else list(out_specs),
        out_shape=out_shape if single else list(out_shape),
        scratch_shapes=[] if one_step else [pltpu.VMEM(acc_shape, F32)],
        compiler_params=_params(("parallel",) * (len(grid) - 1) + ("arbitrary",)),
    )(*args)


def _mm_plain(name, mode, a, b, out_dtype, *, n_off=0, n=None, k_off=0, tm=1024, tn=1536, tk=2048, init=None, dep=None):
    if mode == "nn":
        (m, kk), nn_ = a.shape, b.shape[1]
    elif mode == "nt":
        (m, kk), nn_ = a.shape, b.shape[0]
    else:
        (kk, m), nn_ = a.shape, b.shape[1]
    n = nn_ if n is None else n
    tm, tn, tk = _tile(m, tm), _tile(n, tn), _tile(kk, tk)
    while n_off % tn or n % tn:
        tn -= LANES
    while k_off % tk or kk % tk:
        tk -= LANES
    off, koff = n_off // tn, k_off // tk
    a_spec = {"nn": pl.BlockSpec((tm, tk), lambda i, j, k: (i, k)),
              "nt": pl.BlockSpec((tm, tk), lambda i, j, k: (i, k)),
              "tn": pl.BlockSpec((tk, tm), lambda i, j, k: (k, i))}[mode]
    b_spec = {"nn": pl.BlockSpec((tk, tn), lambda i, j, k: (k, j + off)),
              "nt": pl.BlockSpec((tn, tk), lambda i, j, k: (j, k + koff)),
              "tn": pl.BlockSpec((tk, tn), lambda i, j, k: (k, j))}[mode]
    o_spec = pl.BlockSpec((tm, tn), lambda i, j, k: (i, j))
    if init is not None:
        init = (init, o_spec)
    return _matmul(name, mode, [(a, a_spec, b, b_spec)], (m // tm, n // tn, kk // tk), (tm, tn),
                   _sds((m, n), out_dtype), o_spec, init=init, dep=dep)


def _rows_call(name, body, ins, outs, s, tr=256, dep=None):
    def spec(v, per_row):
        if per_row == "transposed":
            return pl.BlockSpec((v.shape[0], tr), lambda i: (0, i))
        if per_row:
            return pl.BlockSpec((tr, v.shape[1]), lambda i: (i, 0))
        return pl.BlockSpec(v.shape, lambda i: (0, 0))
    n_in = len(ins)
    deps = [] if dep is None else [dep]

    def with_dep(*refs):
        body(*refs[:n_in], *refs[n_in + len(deps):])

    return pl.pallas_call(
        with_dep, name=name, grid=(s // tr,),
        in_specs=[spec(v, p) for v, p in ins] + [ANY] * len(deps), out_specs=[spec(v, p) for v, p in outs],
        out_shape=[_sds(v.shape, v.dtype) for v, _ in outs],
        compiler_params=_params(("arbitrary",)),
    )(*[v for v, _ in ins], *deps)


def _rsq(v):
    return lax.rsqrt(jnp.mean(v * v, axis=-1, keepdims=True) + RMS_EPS)


def _norm_bwd(dy, v, r, g):
    vh = v * r
    t = dy * g
    dv = r * (t - vh * jnp.mean(t * vh, axis=-1, keepdims=True))
    return dv, jnp.sum(dy * vh, axis=0, keepdims=True)


def _accum(ref, val):
    @pl.when(pl.program_id(0) == 0)
    def _():
        ref[...] = jnp.zeros_like(ref)
    ref[...] += val


def _pre_norm(x, g, dep=None):
    def body(x_ref, g_ref, u_ref, ut_ref):
        v = x_ref[...]
        u = (v * _rsq(v) * g_ref[...]).astype(BF16)
        u_ref[...] = u
        ut_ref[...] = u.T
    s, d = x.shape
    return _rows_call("pre_norm", body, [(x, True), (g, False)],
                      [(_sds((s, d), BF16), True), (_sds((d, s), BF16), "transposed")], s, dep=dep)


def _mid_norms(x, mix, g_post, g_pre):
    def body(x_ref, mix_ref, gp_ref, gn_ref, h_ref, u_ref, ut_ref):
        mv = mix_ref[...]
        h = x_ref[...] + mv * _rsq(mv) * gp_ref[...]
        h_ref[...] = h
        u = (h * _rsq(h) * gn_ref[...]).astype(BF16)
        u_ref[...] = u
        ut_ref[...] = u.T
    s, d = x.shape
    return _rows_call("mid_norms", body, [(x, True), (mix, True), (g_post, False), (g_pre, False)],
                      [(_sds((s, d), F32), True), (_sds((s, d), BF16), True), (_sds((d, s), BF16), "transposed")], s)


def _loss_head(h1, ff, target, g):
    s, d = h1.shape

    def body(h_ref, ff_ref, t_ref, g_ref, loss_ref, dy_ref, dff_ref, dg_ref):
        fv = ff_ref[...]
        r = _rsq(fv)
        err = h_ref[...] + fv * r * g_ref[...] - t_ref[...]
        part = 0.5 * jnp.sum(jnp.mean(err * err, axis=-1, keepdims=True), axis=0, keepdims=True)
        _accum(loss_ref, jnp.broadcast_to(part, loss_ref.shape))
        dy = err * (1.0 / d)
        dy_ref[...] = dy
        dff, dg = _norm_bwd(dy, fv, r, g_ref[...])
        dff_ref[...] = dff.astype(BF16)
        _accum(dg_ref, dg)

    return _rows_call("loss_head", body, [(h1, True), (ff, True), (target, True), (g, False)],
                      [(_sds((1, LANES), F32), False), (_sds((s, d), F32), True),
                       (_sds((s, d), BF16), True), (_sds((1, d), F32), False)], s)


def _mid_norms_bwd(dy, du2, h1, mix, g_pre, g_post):
    s, d = dy.shape

    def body(dy_ref, du_ref, h_ref, mix_ref, gn_ref, gp_ref, dh_ref, dmix_ref, dgn_ref, dgp_ref):
        h = h_ref[...]
        dh, dgn = _norm_bwd(du_ref[...], h, _rsq(h), gn_ref[...])
        dh = dh + dy_ref[...]
        dh_ref[...] = dh
        _accum(dgn_ref, dgn)
        mv = mix_ref[...]
        dmix, dgp = _norm_bwd(dh, mv, _rsq(mv), gp_ref[...])
        dmix_ref[...] = dmix.astype(BF16)
        _accum(dgp_ref, dgp)

    return _rows_call("mid_norms_bwd", body,
                      [(dy, True), (du2, True), (h1, True), (mix, True), (g_pre, False), (g_post, False)],
                      [(_sds((s, d), F32), True), (_sds((s, d), BF16), True),
                       (_sds((1, d), F32), False), (_sds((1, d), F32), False)], s)


def _pre_norm_bwd(dh1, du, x, g, dep=None):
    s, d = x.shape

    def body(dh_ref, du_ref, x_ref, g_ref, dx_ref, dg_ref):
        v = x_ref[...]
        dv, dg = _norm_bwd(du_ref[...], v, _rsq(v), g_ref[...])
        dx_ref[...] = dh_ref[...] + dv
        _accum(dg_ref, dg)

    return _rows_call("pre_norm_bwd", body, [(dh1, True), (du, True), (x, True), (g, False)],
                      [(_sds((s, d), F32), True), (_sds((1, d), F32), False)], s, dep=dep)


def _forget_fwd(gf, b_pad, f_blk):
    s = gf.shape[0]
    tb = ATT_TK
    nb = s // tb

    def body(f_ref, b_ref, col_ref, row_ref):
        incl = _tri(tb, lambda r, c: c <= r)
        carry = jnp.zeros((1, LANES), F32)
        for i in range(nb):
            lf = _log_sigmoid(f_ref[pl.ds(i * tb, tb), :] + b_ref[...])
            parts = _split3(lf)
            cum = carry + _dot(incl, parts[0]) + _dot(incl, parts[1]) + _dot(incl, parts[2])
            col_ref[pl.ds(i * tb, tb), :] = cum
            row_ref[i] = cum.T
            carry = carry + jnp.sum(lf, axis=0, keepdims=True)

    return pl.pallas_call(
        body, name="forget_fwd", grid=(1,),
        in_specs=[pl.BlockSpec((s, LANES), lambda i: (0, f_blk)), pl.BlockSpec((1, LANES), lambda i: (0, 0))],
        out_specs=[pl.BlockSpec((s, LANES), lambda i: (0, 0)), pl.BlockSpec((nb, LANES, tb), lambda i: (0, 0, 0))],
        out_shape=[_sds((s, LANES), F32), _sds((nb, LANES, tb), F32)],
        compiler_params=_params(("arbitrary",)),
    )(gf, b_pad)


def _forget_bwd(dgf, dcum, gf, b_pad, f_blk):
    s = gf.shape[0]
    tb = ATT_TK
    nb = s // tb
    sec = dgf.shape[1] // F_PAD - 1

    def body(dgf_hbm, dc_ref, f_ref, b_ref, out_ref, db_ref):
        del dgf_hbm
        incl = _tri(tb, lambda r, c: c >= r)
        carry = jnp.zeros((1, LANES), F32)
        db = jnp.zeros((1, LANES), F32)
        out_ref[...] = jnp.zeros_like(out_ref)
        for i in reversed(range(nb)):
            dc = dc_ref[pl.ds(i * tb, tb), :]
            parts = _split3(dc)
            dlf = carry + _dot(incl, parts[0]) + _dot(incl, parts[1]) + _dot(incl, parts[2])
            z = f_ref[pl.ds(i * tb, tb), :] + b_ref[...]
            df = dlf * _sigmoid(-z)
            out_ref[pl.ds(i * tb, tb), pl.ds(0, LANES)] = df.astype(BF16)
            db = db + jnp.sum(df, axis=0, keepdims=True)
            carry = carry + jnp.sum(dc, axis=0, keepdims=True)
        db_ref[...] = db

    return pl.pallas_call(
        body, name="forget_bwd", grid=(1,),
        in_specs=[ANY, pl.BlockSpec((s, LANES), lambda i: (0, 0)),
                  pl.BlockSpec((s, LANES), lambda i: (0, f_blk)), pl.BlockSpec((1, LANES), lambda i: (0, 0))],
        out_specs=[pl.BlockSpec((s, F_PAD), lambda i: (0, sec)), pl.BlockSpec((1, LANES), lambda i: (0, 0))],
        out_shape=[_sds(dgf.shape, BF16), _sds((1, LANES), F32)],
        input_output_aliases={0: 0},
        compiler_params=_params(("arbitrary",)),
    )(dgf, dcum, gf, b_pad)


def _diag_mask(strict):
    r = lax.broadcasted_iota(jnp.int32, (ATT_TQ, ATT_TK), 0)
    c = lax.broadcasted_iota(jnp.int32, (ATT_TQ, ATT_TK), 1)
    return c < r if strict else c <= r


def _qkv_specs(hb0, s):
    specs = []
    for j in range(ATT_HP):
        def col(g, j=j):
            return 3 * (hb0 + ATT_HP * g + j)
        specs += [pl.BlockSpec((ATT_TQ, HEAD_DIM), lambda g, i, col=col: (i, col(g))),
                  pl.BlockSpec((s, HEAD_DIM), lambda g, i, col=col: (0, col(g) + 1)),
                  pl.BlockSpec((s, HEAD_DIM), lambda g, i, col=col: (0, col(g) + 2))]
    return specs


def _head_cols(j):
    return pl.ds(j * HEAD_DIM, HEAD_DIM)


def _sb_fwd(qkv, n_heads):
    s = qkv.shape[0]
    scale = HEAD_DIM ** -0.5
    tq, tk = ATT_TQ, ATT_TK
    heads = range(ATT_HP)

    def body(*refs):
        qkv_refs, (o_ref, ot_ref, tot_ref) = refs[:3 * ATT_HP], refs[3 * ATT_HP:]
        g, i = pl.program_id(0), pl.program_id(1)

        @pl.when((g == 0) & (i == 0))
        def _():
            tot_ref[...] = jnp.zeros_like(tot_ref)

        qs = [qkv_refs[3 * j][...] for j in heads]
        upper = _tri(tk, lambda r, c: r > c)

        def tile(kj, carry, mask):
            rows = pl.ds(pl.multiple_of(kj * tk, tk), tk)
            z = [_dot(qs[j], qkv_refs[3 * j + 1][rows, :], "nt") * scale for j in heads]
            lsz = [_log_sigmoid(z[j]) for j in heads]
            lk = [lsz[j] - z[j] if mask is None else jnp.where(mask, lsz[j] - z[j], 0.0) for j in heads]
            parts = [_split2(lk[j]) for j in heads]
            above = [carry[j][0] + _dot(parts[j][0], upper) + _dot(parts[j][1], upper) for j in heads]
            w = [jnp.exp(lsz[j] + above[j]) for j in heads]
            if mask is not None:
                w = [jnp.where(mask, w[j], 0.0) for j in heads]
            return tuple((carry[j][0] + jnp.sum(lk[j], axis=1, keepdims=True),
                          carry[j][1] + _dot(w[j], qkv_refs[3 * j + 2][rows, :])) for j in heads)

        carry = tile(i, tuple((jnp.zeros((tq, 1), F32), jnp.zeros((tq, HEAD_DIM), F32)) for _ in heads), _diag_mask(True))
        carry = lax.fori_loop(0, i, lambda n, cr: tile(i - 1 - n, cr, None), carry)
        q_rows = pl.ds(pl.multiple_of(i * tq, tq), tq)
        for j in heads:
            c, acc = carry[j]
            o = acc.astype(BF16)
            o_ref[:, _head_cols(j)] = o
            ot_ref[_head_cols(j), :] = o.T
            _lane_put(tot_ref, q_rows, ATT_HP * g + j, c)

    wide = ATT_HP * HEAD_DIM
    return pl.pallas_call(
        body, name="sb_fwd", grid=(n_heads // ATT_HP, s // tq),
        in_specs=_qkv_specs(0, s),
        out_specs=[pl.BlockSpec((tq, wide), lambda g, i: (i, g)), pl.BlockSpec((wide, tq), lambda g, i: (g, i)),
                   pl.BlockSpec((s, LANES), lambda g, i: (0, 0))],
        out_shape=[_sds((s, n_heads * HEAD_DIM), BF16), _sds((n_heads * HEAD_DIM, s), BF16), _sds((s, LANES), F32)],
        compiler_params=_params(("arbitrary", "arbitrary")),
    )(*[qkv] * (3 * ATT_HP))


def _sb_bwd(qkv, do, tot, n_heads, dep):
    s = qkv.shape[0]
    scale = HEAD_DIM ** -0.5
    tq, tk = ATT_TQ, ATT_TK
    nq = s // tq
    hd = HEAD_DIM

    heads = range(ATT_HP)

    def body(*refs):
        qkv_refs = refs[:3 * ATT_HP]
        do_ref, tot_ref, _, out_ref, dk_acc, dv_acc = refs[3 * ATT_HP:]
        g, i = pl.program_id(0), pl.program_id(1)

        @pl.when(i == 0)
        def _():
            dk_acc[...] = jnp.zeros_like(dk_acc)
            dv_acc[...] = jnp.zeros_like(dv_acc)

        qs = [qkv_refs[3 * j][...] for j in heads]
        douts = [do_ref[:, _head_cols(j)] for j in heads]
        totals = [_lane_pick(tot_ref[...], ATT_HP * g + j) for j in heads]
        incl = _tri(tk, lambda r, c: r <= c)
        excl = _tri(tk, lambda r, c: r < c)

        def tile(kj, carry, mask):
            rows = pl.ds(pl.multiple_of(kj * tk, tk), tk)
            k_t = [qkv_refs[3 * j + 1][rows, :] for j in heads]
            z = [_dot(qs[j], k_t[j], "nt") * scale for j in heads]
            dw = [_dot(douts[j], qkv_refs[3 * j + 2][rows, :], "nt") for j in heads]
            lsz = [_log_sigmoid(z[j]) for j in heads]
            lk = [lsz[j] - z[j] if mask is None else jnp.where(mask, lsz[j] - z[j], 0.0) for j in heads]
            parts = [_split2(lk[j]) for j in heads]
            below = [carry[j][0] + _dot(parts[j][0], incl) + _dot(parts[j][1], incl) for j in heads]
            w = [jnp.exp(lsz[j] + (totals[j] - below[j])) for j in heads]
            if mask is not None:
                w = [jnp.where(mask, w[j], 0.0) for j in heads]
            e = [dw[j] * w[j] for j in heads]
            parts = [_split2(e[j]) for j in heads]
            e_before = [carry[j][1] + _dot(parts[j][0], excl) + _dot(parts[j][1], excl) for j in heads]
            sg = [jnp.exp(lsz[j]) for j in heads]
            dz = [e[j] * (1.0 - sg[j]) - e_before[j] * sg[j] for j in heads]
            if mask is not None:
                dz = [jnp.where(mask, dz[j], 0.0) for j in heads]
            dz = [(dz[j] * scale).astype(BF16) for j in heads]
            for j in heads:
                dk_acc[j, rows, :] += _dot(dz[j], qs[j], "tn")
                dv_acc[j, rows, :] += _dot(w[j], douts[j], "tn")
            return tuple((carry[j][0] + jnp.sum(lk[j], axis=1, keepdims=True),
                          carry[j][1] + jnp.sum(e[j], axis=1, keepdims=True),
                          carry[j][2] + _dot(dz[j], k_t[j])) for j in heads)

        zero = jnp.zeros((tq, 1), F32)
        carry = lax.fori_loop(0, i, lambda kj, cr: tile(kj, cr, None),
                              tuple((zero, zero, jnp.zeros((tq, hd), F32)) for _ in heads))
        carry = tile(i, carry, _diag_mask(True))
        for j in heads:
            out_ref[pl.ds(pl.multiple_of(i * tq, tq), tq), pl.ds(3 * j * hd, hd)] = carry[j][2].astype(BF16)

        @pl.when(i == nq - 1)
        def _():
            for j in heads:
                out_ref[:, pl.ds((3 * j + 1) * hd, hd)] = dk_acc[j].astype(BF16)
                out_ref[:, pl.ds((3 * j + 2) * hd, hd)] = dv_acc[j].astype(BF16)

    wide = ATT_HP * hd
    return pl.pallas_call(
        body, name="sb_bwd", grid=(n_heads // ATT_HP, nq),
        in_specs=_qkv_specs(0, s) + [pl.BlockSpec((tq, wide), lambda g, i: (i, g)),
                                     pl.BlockSpec((tq, LANES), lambda g, i: (i, 0)), ANY],
        out_specs=pl.BlockSpec((s, 3 * wide), lambda g, i: (0, g)),
        out_shape=_sds(qkv.shape, BF16),
        scratch_shapes=[pltpu.VMEM((ATT_HP, s, hd), F32), pltpu.VMEM((ATT_HP, s, hd), F32)],
        compiler_params=_params(("arbitrary", "arbitrary")),
    )(*[qkv] * (3 * ATT_HP), do, tot, dep)


def _fox_fwd(qkv, cum_col, cum_row, n_heads, hb0, dep):
    s = qkv.shape[0]
    scale = HEAD_DIM ** -0.5
    tq, tk = ATT_TQ, ATT_TK

    heads = range(ATT_HP)

    def body(*refs):
        qkv_refs = refs[:3 * ATT_HP]
        cc_ref, cr_ref, _, o_ref, ot_ref, o32_ref, lse_ref = refs[3 * ATT_HP:]
        g, i = pl.program_id(0), pl.program_id(1)

        @pl.when((g == 0) & (i == 0))
        def _():
            lse_ref[...] = jnp.zeros_like(lse_ref)

        qs = [qkv_refs[3 * j][...] for j in heads]
        cqs = [_lane_pick(cc_ref[...], ATT_HP * g + j) for j in heads]

        def tile(kj, carry, mask):
            rows = pl.ds(pl.multiple_of(kj * tk, tk), tk)
            sc = [_dot(qs[j], qkv_refs[3 * j + 1][rows, :], "nt") * scale + cqs[j]
                  - cr_ref[kj, pl.ds(ATT_HP * g + j, 1), :] for j in heads]
            if mask is not None:
                sc = [jnp.where(mask, sc[j], NEG_BIG) for j in heads]
            m_new = [jnp.maximum(carry[j][0], jnp.max(sc[j], axis=1, keepdims=True)) for j in heads]
            p = [jnp.exp(sc[j] - m_new[j]) for j in heads]
            alpha = [jnp.exp(carry[j][0] - m_new[j]) for j in heads]
            parts = [_split2(p[j]) for j in heads]
            v_t = [qkv_refs[3 * j + 2][rows, :] for j in heads]
            pv = [_dot(parts[j][0], v_t[j]) + _dot(parts[j][1], v_t[j]) for j in heads]
            return tuple((m_new[j], alpha[j] * carry[j][1] + jnp.sum(p[j], axis=1, keepdims=True),
                          alpha[j] * carry[j][2] + pv[j]) for j in heads)

        carry = tuple((jnp.full((tq, 1), NEG_BIG, F32), jnp.zeros((tq, 1), F32), jnp.zeros((tq, HEAD_DIM), F32))
                      for _ in heads)
        carry = lax.fori_loop(0, i, lambda kj, cr: tile(kj, cr, None), carry)
        carry = tile(i, carry, _diag_mask(False))
        q_rows = pl.ds(pl.multiple_of(i * tq, tq), tq)
        for j in heads:
            m, l, acc = carry[j]
            o = acc / l
            o_ref[:, _head_cols(j)] = o.astype(BF16)
            ot_ref[_head_cols(j), :] = o.astype(BF16).T
            o32_ref[:, _head_cols(j)] = o
            _lane_put(lse_ref, q_rows, ATT_HP * g + j, m + jnp.log(l))

    nb = cum_row.shape[0]
    wide = ATT_HP * HEAD_DIM
    return pl.pallas_call(
        body, name="fox_fwd", grid=(n_heads // ATT_HP, s // tq),
        in_specs=_qkv_specs(hb0, s) + [pl.BlockSpec((tq, LANES), lambda g, i: (i, 0)),
                                       pl.BlockSpec((nb, 8, tk), lambda g, i: (0, 0, 0)), ANY],
        out_specs=[pl.BlockSpec((tq, wide), lambda g, i: (i, g)), pl.BlockSpec((wide, tq), lambda g, i: (g, i)),
                   pl.BlockSpec((tq, wide), lambda g, i: (i, g)), pl.BlockSpec((s, LANES), lambda g, i: (0, 0))],
        out_shape=[_sds((s, n_heads * HEAD_DIM), BF16), _sds((n_heads * HEAD_DIM, s), BF16),
                   _sds((s, n_heads * HEAD_DIM), F32), _sds((s, LANES), F32)],
        compiler_params=_params(("arbitrary", "arbitrary")),
    )(*[qkv] * (3 * ATT_HP), cum_col, cum_row, dep)


def _fox_bwd(dqkv, qkv, do, o, lse, cum_col, cum_row, n_heads, hb0, dep):
    s = qkv.shape[0]
    scale = HEAD_DIM ** -0.5
    tq, tk = ATT_TQ, ATT_TK
    nq = s // tq
    hd = HEAD_DIM

    heads = range(ATT_HP)
    assert hb0 % ATT_HP == 0

    def body(*refs):
        qkv_refs = refs[1:1 + 3 * ATT_HP]
        do_ref, o_ref, lse_ref, cc_ref, cr_ref, _, out_ref, dc_ref, dk_acc, dv_acc, col_acc = refs[1 + 3 * ATT_HP:]
        g, i = pl.program_id(0), pl.program_id(1)

        @pl.when((g == 0) & (i == 0))
        def _():
            dc_ref[...] = jnp.zeros_like(dc_ref)

        @pl.when(i == 0)
        def _():
            dk_acc[...] = jnp.zeros_like(dk_acc)
            dv_acc[...] = jnp.zeros_like(dv_acc)
            col_acc[...] = jnp.zeros_like(col_acc)

        qs = [qkv_refs[3 * j][...] for j in heads]
        douts = [do_ref[:, _head_cols(j)] for j in heads]
        deltas = [jnp.sum(douts[j].astype(F32) * o_ref[:, _head_cols(j)], axis=1, keepdims=True) for j in heads]
        shifts = [_lane_pick(cc_ref[...], ATT_HP * g + j) - _lane_pick(lse_ref[...], ATT_HP * g + j) for j in heads]

        def tile(kj, carry, mask):
            rows = pl.ds(pl.multiple_of(kj * tk, tk), tk)
            k_t = [qkv_refs[3 * j + 1][rows, :] for j in heads]
            sc = [_dot(qs[j], k_t[j], "nt") * scale + shifts[j] - cr_ref[kj, pl.ds(ATT_HP * g + j, 1), :] for j in heads]
            dp = [_dot(douts[j], qkv_refs[3 * j + 2][rows, :], "nt") for j in heads]
            p = [jnp.exp(sc[j]) for j in heads]
            if mask is not None:
                p = [jnp.where(mask, p[j], 0.0) for j in heads]
            ds_f = [p[j] * (dp[j] - deltas[j]) for j in heads]
            ds = [(ds_f[j] * scale).astype(BF16) for j in heads]
            for j in heads:
                col_acc[j, kj] += jnp.broadcast_to(jnp.sum(ds_f[j], axis=0, keepdims=True), (8, tk))
                dk_acc[j, rows, :] += _dot(ds[j], qs[j], "tn")
                dv_acc[j, rows, :] += _dot(p[j], douts[j], "tn")
            return tuple((carry[j][0] + _dot(ds[j], k_t[j]), carry[j][1] + jnp.sum(ds_f[j], axis=1, keepdims=True))
                         for j in heads)

        carry = lax.fori_loop(0, i, lambda kj, cr: tile(kj, cr, None),
                              tuple((jnp.zeros((tq, hd), F32), jnp.zeros((tq, 1), F32)) for _ in heads))
        carry = tile(i, carry, _diag_mask(False))
        q_rows = pl.ds(pl.multiple_of(i * tq, tq), tq)
        for j in heads:
            out_ref[q_rows, pl.ds(3 * j * hd, hd)] = carry[j][0].astype(BF16)
            _lane_put(dc_ref, q_rows, ATT_HP * g + j, carry[j][1])

        @pl.when(i == nq - 1)
        def _():
            lane = lax.broadcasted_iota(jnp.int32, (tk, LANES), 1)
            for j in heads:
                out_ref[:, pl.ds((3 * j + 1) * hd, hd)] = dk_acc[j].astype(BF16)
                out_ref[:, pl.ds((3 * j + 2) * hd, hd)] = dv_acc[j].astype(BF16)
                for kj in range(nb):
                    col = jnp.broadcast_to(col_acc[j, kj][0:1, :], (LANES, tk)).T
                    old = dc_ref[pl.ds(kj * tk, tk), :]
                    dc_ref[pl.ds(kj * tk, tk), :] = jnp.where(lane == ATT_HP * g + j, old - col, old)

    nb = cum_row.shape[0]
    wide = ATT_HP * hd
    return pl.pallas_call(
        body, name="fox_bwd", grid=(n_heads // ATT_HP, nq),
        in_specs=[ANY] + _qkv_specs(hb0, s) + [
            pl.BlockSpec((tq, wide), lambda g, i: (i, g)), pl.BlockSpec((tq, wide), lambda g, i: (i, g)),
            pl.BlockSpec((tq, LANES), lambda g, i: (i, 0)), pl.BlockSpec((tq, LANES), lambda g, i: (i, 0)),
            pl.BlockSpec((nb, 8, tk), lambda g, i: (0, 0, 0)), ANY],
        out_specs=[pl.BlockSpec((s, 3 * wide), lambda g, i: (0, hb0 // ATT_HP + g)),
                   pl.BlockSpec((s, LANES), lambda g, i: (0, 0))],
        out_shape=[_sds(dqkv.shape, BF16), _sds((s, LANES), F32)],
        scratch_shapes=[pltpu.VMEM((ATT_HP, s, hd), F32), pltpu.VMEM((ATT_HP, s, hd), F32),
                        pltpu.VMEM((ATT_HP, s // tk, 8, tk), F32)],
        input_output_aliases={0: 0},
        compiler_params=_params(("arbitrary", "arbitrary")),
    )(dqkv, *[qkv] * (3 * ATT_HP), do, o, lse, cum_col, cum_row, dep)


def _branch_merge(o_sb, o_fx, w_sb, w_fx, gf, dep, tm=1024):
    s = o_sb.shape[0]
    cs = w_sb.shape[2]
    tm = _tile(s, tm)

    def body(osb_ref, ofx_ref, wsb_ref, wfx_ref, g_ref, dep_ref, merged_ref, mt_ref, asb_ref, afx_ref):
        del dep_ref
        a_sb = _dot(osb_ref[...], wsb_ref[...])
        a_fx = _dot(ofx_ref[...], wfx_ref[...])
        g = g_ref[...]
        merged = (_sigmoid(g[:, :cs]) * a_sb + _sigmoid(g[:, cs:]) * a_fx).astype(BF16)
        merged_ref[...] = merged
        mt_ref[...] = merged.T
        asb_ref[...] = a_sb.astype(BF16)
        afx_ref[...] = a_fx.astype(BF16)

    blk = pl.BlockSpec((tm, cs), lambda i, j: (i, j))
    out = _sds((s, N_DEV * cs), BF16)
    return pl.pallas_call(
        body, name="branch_merge", grid=(s // tm, N_DEV),
        in_specs=[pl.BlockSpec((tm, o_sb.shape[1]), lambda i, j: (i, 0)),
                  pl.BlockSpec((tm, o_fx.shape[1]), lambda i, j: (i, 0)),
                  pl.BlockSpec((None,) + w_sb.shape[1:], lambda i, j: (j, 0, 0)),
                  pl.BlockSpec((None,) + w_fx.shape[1:], lambda i, j: (j, 0, 0)),
                  pl.BlockSpec((tm, 2 * cs), lambda i, j: (i, j)), ANY],
        out_specs=[blk, pl.BlockSpec((cs, tm), lambda i, j: (j, i)), blk, blk],
        out_shape=[out, _sds((N_DEV * cs, s), BF16), out, out],
        compiler_params=_params(("parallel", "arbitrary")),
    )(o_sb, o_fx, w_sb, w_fx, gf, dep)


def _merge_bwd(dmix, w_out, gf, a_sb, a_fx, tm=1024, tk=2048, dep=None):
    s, d = dmix.shape
    cs = d // N_DEV
    tm, tk = _tile(s, tm), _tile(d, tk)

    def epilogue(acc, ex, outs):
        g, a_sb, a_fx = ex[0][...], ex[1][...].astype(F32), ex[2][...].astype(F32)
        s_sb, s_fx = _sigmoid(g[:, :cs]), _sigmoid(g[:, cs:])
        outs[0][...] = (acc * s_sb).astype(BF16)
        outs[1][...] = (acc * s_fx).astype(BF16)
        outs[2][...] = jnp.concatenate([acc * a_sb * s_sb * (1.0 - s_sb), acc * a_fx * s_fx * (1.0 - s_fx)],
                                       axis=1).astype(BF16)

    blk = pl.BlockSpec((tm, cs), lambda i, j, k: (i, j))
    wide = pl.BlockSpec((tm, 2 * cs), lambda i, j, k: (i, j))
    return _matmul(
        "merge_bwd", "nt",
        [(dmix, pl.BlockSpec((tm, tk), lambda i, j, k: (i, k)), w_out, pl.BlockSpec((cs, tk), lambda i, j, k: (j, k)))],
        (s // tm, N_DEV, d // tk), (tm, cs),
        [_sds((s, d), BF16), _sds((s, d), BF16), _sds(gf.shape, BF16)], [blk, blk, wide],
        extras=[(gf, wide), (a_sb, blk), (a_fx, blk)], epilogue=epilogue, dep=dep)


def _ffn_up(u2, w_gate, w_up, tm=1024):
    s, d = u2.shape
    fs = w_gate.shape[2]
    tm = _tile(s, tm)

    def body(u_ref, wg_ref, wu_ref, gate_ref, up_ref, act_ref, actt_ref):
        u = u_ref[...]
        gate = _dot(u, wg_ref[...])
        up = _dot(u, wu_ref[...])
        gate_ref[...] = gate
        up_ref[...] = up
        act = (gate * _sigmoid(gate) * up).astype(BF16)
        act_ref[...] = act
        actt_ref[...] = act.T

    w_spec = pl.BlockSpec((None, d, fs), lambda i, j: (j, 0, 0))
    o_spec = pl.BlockSpec((None, tm, fs), lambda i, j: (j, i, 0))
    return pl.pallas_call(
        body, name="ffn_up", grid=(s // tm, N_DEV),
        in_specs=[pl.BlockSpec((tm, d), lambda i, j: (i, 0)), w_spec, w_spec],
        out_specs=[o_spec, o_spec, o_spec, pl.BlockSpec((None, fs, tm), lambda i, j: (j, 0, i))],
        out_shape=[_sds((N_DEV, s, fs), F32), _sds((N_DEV, s, fs), F32), _sds((N_DEV, s, fs), BF16),
                   _sds((N_DEV, fs, s), BF16)],
        compiler_params=_params(("parallel", "arbitrary")),
    )(u2, w_gate, w_up)


def _ffn_down_bwd(dff, w_down, gate, up, tm=1024):
    s, d = dff.shape
    fs = w_down.shape[1]
    tm = _tile(s, tm)

    def body(dff_ref, wd_ref, gate_ref, up_ref, dgate_ref, dup_ref):
        dact = _dot(dff_ref[...], wd_ref[...], "nt")
        gate = gate_ref[...]
        sg = _sigmoid(gate)
        dup_ref[...] = (dact * gate * sg).astype(BF16)
        dgate_ref[...] = (dact * up_ref[...] * sg * (1.0 + gate * (1.0 - sg))).astype(BF16)

    a_spec = pl.BlockSpec((None, tm, fs), lambda i, j: (j, i, 0))
    return pl.pallas_call(
        body, name="ffn_down_bwd", grid=(s // tm, N_DEV),
        in_specs=[pl.BlockSpec((tm, d), lambda i, j: (i, 0)), pl.BlockSpec((None, fs, d), lambda i, j: (j, 0, 0)),
                  a_spec, a_spec],
        out_specs=[a_spec, a_spec],
        out_shape=[_sds((N_DEV, s, fs), BF16), _sds((N_DEV, s, fs), BF16)],
        compiler_params=_params(("parallel", "arbitrary")),
    )(dff, w_down, gate, up)


def _mesh_place():
    x, y, c = lax.axis_index("x"), lax.axis_index("y"), lax.axis_index("c")
    peers = []
    for d in range(1, N_DEV):
        px = 1 - x if d & 4 else x
        py = 1 - y if d & 2 else y
        pc = 1 - c if d & 1 else c
        peers.append((d, (px, py, pc), 4 * px + 2 * py + pc))
    return 4 * x + 2 * y + c, peers


def _flat_me():
    return 4 * lax.axis_index("x") + 2 * lax.axis_index("y") + lax.axis_index("c")


def _in_hbm(a):
    return pltpu.with_memory_space_constraint(a, pltpu.HBM)


def _pair_plan():
    x, y, c = lax.axis_index("x"), lax.axis_index("y"), lax.axis_index("c")
    return [(2 * q + (1 - c), q, q, (x, y, 1 - c)) for q in range(4)]


def _chip_plan():
    x, y, c = lax.axis_index("x"), lax.axis_index("y"), lax.axis_index("c")
    plan = []
    for fx, fy in ((1, 0), (0, 1), (1, 1)):
        cx, cy = (1 - x if fx else x), (1 - y if fy else y)
        plan.append((2 * cx + cy, 2 * x + y, 2 * cx + cy, (cx, cy, c)))
    return plan


def _split_start(name, srcs, lands, plan, k):
    n = len(srcs)

    def body(*refs):
        ins, lnd = refs[:n], refs[n:2 * n]
        send, recv, token = refs[2 * n], refs[2 * n + 1], refs[-1]
        copies = plan()
        for a in range(n):
            for t, (src, dst, _, dev) in enumerate(copies):
                pltpu.make_async_remote_copy(src_ref=ins[a].at[src], dst_ref=lnd[a].at[dst], send_sem=send.at[k * a + t],
                                             recv_sem=recv.at[k * a + t], device_id=dev, device_id_type=MESH).start()
        token[...] = jnp.zeros_like(token)

    res = pl.pallas_call(
        body, name=name,
        out_shape=[pltpu.SemaphoreType.DMA((n * k,)), pltpu.SemaphoreType.DMA((n * k,))]
        + [pltpu.HBM(a.shape, a.dtype) for a in list(srcs) + list(lands)] + [_sds((8, LANES), F32)],
        in_specs=[HBM] * (2 * n), out_specs=[SEM, SEM] + [HBM] * (2 * n) + [pl.BlockSpec(memory_space=pltpu.VMEM)],
        input_output_aliases={i: 2 + i for i in range(2 * n)},
        compiler_params=pltpu.CompilerParams(has_side_effects=EFFECT),
    )(*[_in_hbm(a) for a in srcs], *[_in_hbm(a) for a in lands])
    return res[0], res[1], res[2:2 + n], res[2 + n:2 + 2 * n], res[-1]


def _split_wait(name, send, recv, srcs, lands, plan, k, after):
    n = len(srcs)

    def body(*refs):
        ins, lnd = refs[:n], refs[n:2 * n]
        send_sem, recv_sem = refs[2 * n], refs[2 * n + 1]
        copies = plan()
        for a in range(n):
            for t, (src, _, dst, dev) in enumerate(copies):
                cp = pltpu.make_async_remote_copy(src_ref=ins[a].at[src], dst_ref=lnd[a].at[dst], send_sem=send_sem.at[k * a + t],
                                                  recv_sem=recv_sem.at[k * a + t], device_id=dev, device_id_type=MESH)
                cp.wait_send()
                cp.wait_recv()

    res = pl.pallas_call(
        body, name=name,
        out_shape=[pltpu.HBM(a.shape, a.dtype) for a in list(srcs) + list(lands)],
        in_specs=[HBM] * (2 * n) + [SEM, SEM] + [ANY] * len(after), out_specs=[HBM] * (2 * n),
        input_output_aliases={i: i for i in range(2 * n)},
        compiler_params=pltpu.CompilerParams(has_side_effects=EFFECT),
    )(*srcs, *lands, send, recv, *after)
    return res[:n], res[n:]


def _pair_add(name, parts, land):
    _, r, cols = parts.shape
    tr = max(16, min(r, ((1 << 20) // (2 * cols)) // 16 * 16))
    while r % tr:
        tr -= 16

    def body(c_ref, p_ref, l_ref, o_ref):
        del c_ref
        o_ref[...] = (p_ref[...].astype(F32) + l_ref[...].astype(F32)).astype(BF16)

    blk = pl.BlockSpec((None, tr, cols), lambda q, i, c_ref: (q, i, 0))
    return pl.pallas_call(
        body, name=name,
        grid_spec=pltpu.PrefetchScalarGridSpec(
            num_scalar_prefetch=1, grid=(4, r // tr),
            in_specs=[pl.BlockSpec((None, tr, cols), lambda q, i, c_ref: (2 * q + c_ref[0], i, 0)), blk], out_specs=blk),
        out_shape=_sds((4, r, cols), BF16),
        compiler_params=_params(("parallel", "parallel")),
    )(jnp.reshape(lax.axis_index("c"), (1,)).astype(jnp.int32), parts, land)


def _scatter_pairs(tag, parts):
    lands = [lax.empty((4,) + a.shape[1:], a.dtype) for a in parts]
    return _split_start("pair_" + tag, parts, lands, _pair_plan, 4)


def _scatter_chips(tag, started, after):
    send, recv, parts, lands, _ = started
    parts, lands = _split_wait("pair_" + tag + "_wait", send, recv, parts, lands, _pair_plan, 4, [after])
    sums = [_pair_add("pair_" + tag + "_add%d" % a, p, l) for a, (p, l) in enumerate(zip(parts, lands))]
    chip = 2 * lax.axis_index("x") + lax.axis_index("y")
    final = [lax.dynamic_update_slice_in_dim(lax.empty(v.shape, v.dtype), lax.dynamic_slice_in_dim(v, chip, 1, 0), chip, 0)
             for v in sums]
    return _split_start("chips_" + tag, sums, final, _chip_plan, 3)


def _scatter_end(tag, started, after):
    send, recv, sums, final, _ = started
    return _split_wait("chips_" + tag + "_wait", send, recv, sums, final, _chip_plan, 3, after)[1]


def _gather_targets():
    x, y, c = lax.axis_index("x"), lax.axis_index("y"), lax.axis_index("c")
    chips = [(x, y), (1 - x, y), (x, 1 - y), (1 - x, 1 - y)]
    same = [((cx, cy, c), 4 * cx + 2 * cy + c) for cx, cy in chips]
    other = [((cx, cy, 1 - c), 4 * cx + 2 * cy + 1 - c) for cx, cy in chips]
    return same[0][1], [other[0]] + same[1:], [flat for _, flat in other[1:]], other[0][0]


def _gather_start(shards):
    n = len(shards)
    me = _flat_me()
    lands = [lax.dynamic_update_slice_in_dim(lax.empty((N_DEV,) + a.shape, a.dtype), a[None], me, 0) for a in shards]

    def body(*refs):
        lnd, send, recv, token = refs[:n], refs[n], refs[n + 1], refs[-1]
        mine, targets, _, _ = _gather_targets()
        for a in range(n):
            for t, (dev, _) in enumerate(targets):
                pltpu.make_async_remote_copy(src_ref=lnd[a].at[mine], dst_ref=lnd[a].at[mine], send_sem=send.at[4 * a + t],
                                             recv_sem=recv.at[4 * a + t], device_id=dev, device_id_type=MESH).start()
        token[...] = jnp.zeros_like(token)

    res = pl.pallas_call(
        body, name="gather_start",
        out_shape=[pltpu.SemaphoreType.DMA((4 * n,)), pltpu.SemaphoreType.DMA((4 * n,))]
        + [pltpu.HBM(a.shape, a.dtype) for a in lands] + [_sds((8, LANES), F32)],
        in_specs=[HBM] * n, out_specs=[SEM, SEM] + [HBM] * n + [pl.BlockSpec(memory_space=pltpu.VMEM)],
        input_output_aliases={i: 2 + i for i in range(n)},
        compiler_params=pltpu.CompilerParams(has_side_effects=EFFECT),
    )(*[_in_hbm(a) for a in lands])
    return res[0], res[1], list(res[2:2 + n]), res[-1]


def _gather_forward(name, lands, first, send, recv, after):
    n = len(lands)

    def body(*refs):
        lnd, send_sem, recv_sem = refs[:n], refs[n], refs[n + 1]
        send2, recv2, token = refs[-3], refs[-2], refs[-1]
        mine, targets, _, sibling = _gather_targets()
        for a in range(n):
            for t, (dev, flat) in enumerate(targets):
                cp = pltpu.make_async_remote_copy(src_ref=lnd[a].at[mine], dst_ref=lnd[a].at[flat],
                                                  send_sem=send_sem.at[4 * (first + a) + t],
                                                  recv_sem=recv_sem.at[4 * (first + a) + t], device_id=dev, device_id_type=MESH)
                cp.wait_send()
                if t:
                    cp.wait_recv()
                    pltpu.make_async_remote_copy(src_ref=lnd[a].at[flat], dst_ref=lnd[a].at[flat], send_sem=send2.at[3 * a + t - 1],
                                                 recv_sem=recv2.at[3 * a + t - 1], device_id=sibling, device_id_type=MESH).start()
        token[...] = jnp.zeros_like(token)

    res = pl.pallas_call(
        body, name=name,
        out_shape=[pltpu.HBM(a.shape, a.dtype) for a in lands]
        + [pltpu.SemaphoreType.DMA((3 * n,)), pltpu.SemaphoreType.DMA((3 * n,)), _sds((8, LANES), F32)],
        in_specs=[HBM] * n + [SEM, SEM] + [ANY] * len(after),
        out_specs=[HBM] * n + [SEM, SEM, pl.BlockSpec(memory_space=pltpu.VMEM)],
        input_output_aliases={i: i for i in range(n)},
        compiler_params=pltpu.CompilerParams(has_side_effects=EFFECT),
    )(*lands, send, recv, *after)
    return list(res[:n]), res[n], res[n + 1], res[-1]


def _gather_wait(name, lands, first, recv, send2, recv2, after):
    n = len(lands)

    def body(*refs):
        lnd, recv_sem, send2_sem, recv2_sem = refs[:n], refs[n], refs[n + 1], refs[n + 2]
        mine, targets, passed, sibling = _gather_targets()
        for a in range(n):
            dev, flat = targets[0]
            pltpu.make_async_remote_copy(src_ref=lnd[a].at[mine], dst_ref=lnd[a].at[flat], send_sem=send2_sem.at[3 * a],
                                         recv_sem=recv_sem.at[4 * (first + a)], device_id=dev, device_id_type=MESH).wait_recv()
            for t in range(3):
                cp = pltpu.make_async_remote_copy(src_ref=lnd[a].at[targets[t + 1][1]], dst_ref=lnd[a].at[passed[t]],
                                                  send_sem=send2_sem.at[3 * a + t], recv_sem=recv2_sem.at[3 * a + t],
                                                  device_id=sibling, device_id_type=MESH)
                cp.wait_send()
                cp.wait_recv()

    res = pl.pallas_call(
        body, name=name, out_shape=[pltpu.HBM(a.shape, a.dtype) for a in lands],
        in_specs=[HBM] * n + [SEM, SEM, SEM, ANY], out_specs=[HBM] * n,
        input_output_aliases={i: i for i in range(n)},
        compiler_params=pltpu.CompilerParams(has_side_effects=EFFECT),
    )(*lands, recv, send2, recv2, after)
    return list(res)


def _adamw_decay(w, m, v):
    return ADAM_WD * w, ADAM_B1 * m, ADAM_B2 * v


def _adamw_finish(g, wd_w, m1, v1):
    m = m1 + (1.0 - ADAM_B1) * g
    v = v1 + (1.0 - ADAM_B2) * (g * g)
    m_hat = m / (1.0 - ADAM_B1 ** ADAM_STEP)
    v_hat = v / (1.0 - ADAM_B2 ** ADAM_STEP)
    delta = -ADAM_LR * (m_hat / (jnp.sqrt(v_hat) + ADAM_EPS) + wd_w)
    return delta, m, v


def _adamw(g, w, m, v):
    return _adamw_finish(g, *_adamw_decay(w, m, v))


def _update_prep(name, w, m, v, block_bytes=1 << 20):
    _, r, c = w.shape
    tr = max(8, min(r, (block_bytes // (4 * c)) // 8 * 8))
    while r % tr:
        tr -= 8

    def body(w_ref, m_ref, v_ref, ow_ref, om_ref, ov_ref):
        ow_ref[...], om_ref[...], ov_ref[...] = _adamw_decay(w_ref[...], m_ref[...], v_ref[...])

    blk = pl.BlockSpec((None, tr, c), lambda i: (0, i, 0))
    return pl.pallas_call(
        body, name=name, grid=(r // tr,), in_specs=[blk] * 3, out_specs=[blk] * 3, out_shape=[_sds((1, r, c), F32)] * 3,
        compiler_params=_params(("parallel",)),
    )(w, m, v)


def _update(name, parts, w, m, v, layout=None, decayed=False, block_bytes=1 << 20):
    _, r, c = w.shape
    n_slots, _, cp = parts.shape
    tr = max(8, min(r, (block_bytes // (4 * cp)) // 8 * 8))
    while r % tr:
        tr -= 8

    def body(p_ref, w_ref, m_ref, v_ref, g_ref, d_ref, nm_ref, nv_ref, *scratch):
        g = p_ref[0].astype(F32)
        for p in range(1, n_slots):
            g = g + p_ref[p].astype(F32)
        if layout is not None:
            s1, s2, lg = layout.my_shifts()
            lane = lax.broadcasted_iota(jnp.int32, g.shape, 1)
            scratch[0][...] = jnp.where(lane < lg, pltpu.roll(g, cp - s1, 1), pltpu.roll(g, cp - s2, 1))
            g = scratch[0][:, 0:c]
        g_ref[...] = g
        step = _adamw_finish if decayed else _adamw
        d_ref[...], nm_ref[...], nv_ref[...] = step(g, w_ref[...], m_ref[...], v_ref[...])

    blk = pl.BlockSpec((None, tr, c), lambda i: (0, i, 0))
    return pl.pallas_call(
        body, name=name, grid=(r // tr,),
        in_specs=[pl.BlockSpec((n_slots, tr, cp), lambda i: (0, i, 0)), blk, blk, blk],
        out_specs=[blk] * 4, out_shape=[_sds((1, r, c), F32)] * 4,
        scratch_shapes=[] if layout is None else [pltpu.VMEM((tr, cp), F32)],
        compiler_params=_params(("parallel",)),
    )(parts, w, m, v)


def _small_update(part, w, m, v):
    n = part.shape[1]

    def body(p_ref, w_ref, m_ref, v_ref, g_ref, d_ref, nm_ref, nv_ref, buf, send, recv):
        me, peers = _mesh_place()
        buf[me] = p_ref[...]
        sent = []
        for d, dev, flat in peers:
            cp = pltpu.make_async_remote_copy(src_ref=p_ref, dst_ref=buf.at[me], send_sem=send.at[d],
                                              recv_sem=recv.at[d], device_id=dev, device_id_type=MESH)
            cp.start()
            sent.append(cp)
        for d, dev, flat in peers:
            pltpu.make_async_remote_copy(src_ref=p_ref, dst_ref=buf.at[flat], send_sem=send.at[d],
                                         recv_sem=recv.at[d], device_id=dev, device_id_type=MESH).wait_recv()
        for cp in sent:
            cp.wait_send()
        g = buf[0]
        for p in range(1, N_DEV):
            g = g + buf[p]
        g_ref[...] = g
        d_ref[...], nm_ref[...], nv_ref[...] = _adamw(g, w_ref[...], m_ref[...], v_ref[...])

    vm = pl.BlockSpec(memory_space=pltpu.VMEM)
    return pl.pallas_call(
        body, name="small_update", in_specs=[vm] * 4, out_specs=[vm] * 4, out_shape=[_sds((1, n), F32)] * 4,
        scratch_shapes=[pltpu.VMEM((N_DEV, 1, n), F32), pltpu.SemaphoreType.DMA((N_DEV,)),
                        pltpu.SemaphoreType.DMA((N_DEV,))],
    )(part, w, m, v)


class _WInLayout:
    def __init__(self, n8, n_f, d_sb, d_fox, d):
        assert n8 % LANES == 1 and n_f < LANES and d % (N_DEV * LANES) == 0
        self.n8, self.n_f, self.d = n8, n_f, d
        self.sp = n8 // LANES
        self.wp = (n8 + 2 * LANES - 2) // LANES * LANES
        self.n_qkv = 3 * (d_sb + d_fox)
        nq, dt, tc = self.n_qkv // LANES, d // LANES, d // N_DEV // LANES
        h_sb, h_fox = d_sb // HEAD_DIM, d_fox // HEAD_DIM
        self.sources = {}
        self.part_tile = {}
        for p in range(N_DEV):
            lg = min(max(self.n_qkv + n_f - n8 * p, 0), n8)
            s1, s2 = p, p + LANES - n_f
            spans = []
            if lg > 0:
                spans.append(("a", self.sp * p, s1 // LANES, (lg + s1 - 1) // LANES))
            if lg < n8:
                spans.append(("g", self.sp * p - 1 - nq, (lg + s2) // LANES, (n8 - 1 + s2) // LANES))
            for kind, base, first, last in spans:
                for i in range(first, last + 1):
                    assert (p, i) not in self.part_tile
                    self.part_tile[(p, i)] = (kind, base + i)
                    self.sources.setdefault((kind, base + i), []).append((p, i))
        self.cat_tiles = [("a", r * h_sb + h) for h in range(h_sb) for r in range(3)]
        self.cat_tiles += [("a", 3 * h_sb + r * h_fox + h) for h in range(h_fox) for r in range(3)]
        self.cat_tiles += [("g", which * dt + j * tc + half) for j in range(N_DEV) for which in (0, 1) for half in range(tc)]
        self.cat_tiles += [("a", nq)] + [None] * (F_PAD // LANES - 1)
        self.cat_index = {key: c for c, key in enumerate(self.cat_tiles) if key is not None}

    def my_shifts(self):
        me = _flat_me()
        return me, me + LANES - self.n_f, jnp.clip(self.n_qkv + self.n_f - self.n8 * me, 0, self.n8)


def _lane_tile(i):
    return pl.ds(i * LANES, LANES)


def _w_in_shift(w_in, lay, tr=256):
    _, d, n8 = w_in.shape

    def body(w_ref, o_ref, buf):
        buf[...] = jnp.zeros_like(buf)
        buf[:, 0:n8] = w_ref[...]
        v = buf[...]
        s1, s2, lg = lay.my_shifts()
        pos = lax.broadcasted_iota(jnp.int32, v.shape, 1)
        o_ref[...] = jnp.where(pos < lg + s1, pltpu.roll(v, s1, 1),
                               jnp.where(pos >= lg + s2, pltpu.roll(v, s2, 1), 0.0)).astype(BF16)

    return pl.pallas_call(
        body, name="w_in_shift", grid=(d // tr,),
        in_specs=[pl.BlockSpec((None, tr, n8), lambda i: (0, i, 0))],
        out_specs=pl.BlockSpec((tr, lay.wp), lambda i: (i, 0)), out_shape=_sds((d, lay.wp), BF16),
        scratch_shapes=[pltpu.VMEM((tr, lay.wp), F32)],
        compiler_params=_params(("parallel",)),
    )(w_in)


def _w_in_build(g_in, lay, tr=256):
    d = g_in.shape[1]
    width = len(lay.cat_tiles) * LANES

    def body(g_ref, o_ref):
        for c, key in enumerate(lay.cat_tiles):
            if key is None:
                o_ref[:, _lane_tile(c)] = jnp.zeros((tr, LANES), BF16)
                continue
            (p, i), *more = lay.sources[key]
            val = g_ref[p, :, _lane_tile(i)]
            for p2, i2 in more:
                val = val + g_ref[p2, :, _lane_tile(i2)]
            o_ref[:, _lane_tile(c)] = val

    return pl.pallas_call(
        body, name="w_in_build", grid=(d // tr,),
        in_specs=[pl.BlockSpec((N_DEV, tr, lay.wp), lambda i: (0, i, 0))],
        out_specs=pl.BlockSpec((tr, width), lambda i: (i, 0)), out_shape=_sds((d, width), BF16),
        compiler_params=_params(("parallel",)),
    )(g_in)


def _w_in_grad_parts(dwq, dwgf, lay, tr=256):
    d = dwq.shape[0]
    nq = lay.n_qkv // LANES

    def body(q_ref, g_ref, o_ref):
        for p in range(N_DEV):
            for i in range(lay.wp // LANES):
                key = lay.part_tile.get((p, i))
                if key is None:
                    o_ref[p, :, _lane_tile(i)] = jnp.zeros((tr, LANES), BF16)
                    continue
                c = lay.cat_index[key]
                o_ref[p, :, _lane_tile(i)] = q_ref[:, _lane_tile(c)] if c < nq else g_ref[:, _lane_tile(c - nq)]

    return pl.pallas_call(
        body, name="w_in_grad_parts", grid=(d // tr,),
        in_specs=[pl.BlockSpec((tr, dwq.shape[1]), lambda i: (i, 0)), pl.BlockSpec((tr, dwgf.shape[1]), lambda i: (i, 0))],
        out_specs=pl.BlockSpec((N_DEV, tr, lay.wp), lambda i: (0, i, 0)), out_shape=_sds((N_DEV, d, lay.wp), BF16),
        compiler_params=_params(("parallel",)),
    )(dwq, dwgf)


def kernel(x, norm_mix_pre, norm_mix_post, w_in, b_forget, w_branch_sb, w_branch_fox, w_out, norm_ffn_pre, norm_ffn_post, w_ffn_gate, w_ffn_up, w_ffn_down, loss_target, m_norm_mix_pre, m_norm_mix_post, m_w_in, m_b_forget, m_w_branch_sb, m_w_branch_fox, m_w_out, m_norm_ffn_pre, m_norm_ffn_post, m_w_ffn_gate, m_w_ffn_up, m_w_ffn_down, v_norm_mix_pre, v_norm_mix_post, v_w_in, v_b_forget, v_w_branch_sb, v_w_branch_fox, v_w_out, v_norm_ffn_pre, v_norm_ffn_post, v_w_ffn_gate, v_w_ffn_up, v_w_ffn_down):
    xs, target = x[0], loss_target[0]
    s, d = xs.shape
    d_sb, d_fox = w_branch_sb.shape[1], w_branch_fox.shape[1]
    h_sb, h_fox = d_sb // HEAD_DIM, d_fox // HEAD_DIM
    n_f = b_forget.shape[1]
    fs = w_ffn_gate.shape[2]
    cs = d // N_DEV
    n_qkv = 3 * (d_sb + d_fox)
    n_gf = 2 * d + F_PAD
    f_blk = 2 * d // LANES
    big = (w_in, w_branch_sb, w_branch_fox, w_out, w_ffn_gate, w_ffn_up, w_ffn_down)
    big_m = (m_w_in, m_w_branch_sb, m_w_branch_fox, m_w_out, m_w_ffn_gate, m_w_ffn_up, m_w_ffn_down)
    big_v = (v_w_in, v_w_branch_sb, v_w_branch_fox, v_w_out, v_w_ffn_gate, v_w_ffn_up, v_w_ffn_down)

    lay = _WInLayout(w_in.shape[2], n_f, d_sb, d_fox, d)
    send1, recv1, lands, token = _gather_start([_w_in_shift(w_in, lay)] + [w[0].astype(BF16) for w in big[1:]])
    b_pad = jnp.pad(b_forget, ((0, 0), (0, LANES - n_f)))

    u, u_t = _pre_norm(xs, norm_mix_pre, dep=token)
    weights = dict(zip(("w_in", "w_branch_sb", "w_branch_fox", "w_out", "w_ffn_gate", "w_ffn_up", "w_ffn_down"),
                       zip(big, big_m, big_v)))
    decayed = {nm: _update_prep("decay_" + nm, *weights[nm]) for nm in ("w_in", "w_ffn_gate", "w_ffn_up")}
    l_in, send2, recv2, token = _gather_forward("gather_in_forward", lands[0:1], 0, send1, recv1,
                                                [u] + [t[2] for t in decayed.values()])
    (g_in,) = _gather_wait("gather_in_wait", l_in, 0, recv1, send2, recv2, token)
    w_cat = _w_in_build(g_in, lay)
    qkv = _mm_plain("proj_qkv", "nn", u, w_cat, BF16, n=n_qkv)
    gf = _mm_plain("proj_gates", "nn", u, w_cat, F32, n_off=n_qkv, n=n_gf)
    cum_col, cum_row = _forget_fwd(gf, b_pad, f_blk)
    o_sb, o_sb_t, tot = _sb_fwd(qkv, h_sb)
    l_mid, send2, recv2, token = _gather_forward("gather_mid_forward", lands[1:4], 1, send1, recv1, [o_sb])
    o_fx, o_fx_t, o_fx32, lse = _fox_fwd(qkv, cum_col, cum_row, h_fox, h_sb, token)
    g_sb, g_fx, g_out = _gather_wait("gather_mid_wait", l_mid, 1, recv1, send2, recv2, o_fx)
    w_out_full = g_out.reshape(d, d)
    merged, merged_t, a_sb, a_fx = _branch_merge(o_sb, o_fx, g_sb, g_fx, gf, o_fx)
    l_ffn, send2, recv2, token = _gather_forward("gather_ffn_forward", lands[4:7], 4, send1, recv1, [merged])
    mix = _mm_plain("out_proj", "nn", merged, w_out_full, F32, dep=token)
    h1, u2, u2_t = _mid_norms(xs, mix, norm_mix_post, norm_ffn_pre)
    g_gate, g_up, g_down = _gather_wait("gather_ffn_wait", l_ffn, 4, recv1, send2, recv2, u2)
    gate, up, act, act_t = _ffn_up(u2, g_gate, g_up)
    tm, tn = _tile(s, 1024), _tile(d, 1024)
    ff = _matmul("ffn_down", "nn",
                 [(act, pl.BlockSpec((None, tm, fs), lambda i, j, k: (k, i, 0)),
                   g_down, pl.BlockSpec((None, fs, tn), lambda i, j, k: (k, 0, j)))],
                 (s // tm, d // tn, N_DEV), (tm, tn), _sds((s, d), F32), pl.BlockSpec((tm, tn), lambda i, j, k: (i, j)))
    loss_part, dy, dff, dg_ffn_post = _loss_head(h1, ff, target, norm_ffn_post)

    dgate, dup = _ffn_down_bwd(dff, g_down, gate, up)
    dw_down = _matmul("dw_down", "nn",
                      [(act_t, pl.BlockSpec((None, fs, s), lambda j, n, k: (j, 0, 0)),
                        dff, pl.BlockSpec((s, tn), lambda j, n, k: (0, n)))],
                      (N_DEV, d // tn, 1), (fs, tn), _sds((N_DEV, fs, d), BF16),
                      pl.BlockSpec((None, fs, tn), lambda j, n, k: (j, 0, n)))

    def dw_up(name, dact):
        return _matmul(name, "nn",
                       [(u2_t, pl.BlockSpec((tn, s), lambda j, i, k: (i, 0)),
                         dact, pl.BlockSpec((None, s, fs), lambda j, i, k: (j, 0, 0)))],
                       (N_DEV, d // tn, 1), (tn, fs), _sds((N_DEV, d, fs), BF16),
                       pl.BlockSpec((None, tn, fs), lambda j, i, k: (j, i, 0)))

    dw_gate, dw_upw = dw_up("dw_gate", dgate), dw_up("dw_up", dup)
    rs_ffn = _scatter_pairs("ffn", [dw_gate, dw_upw, dw_down])
    a_spec = pl.BlockSpec((None, tm, fs), lambda i, j, k: (k, i, 0))
    b_spec = pl.BlockSpec((None, tn, fs), lambda i, j, k: (k, j, 0))
    du2 = _matmul("du2", "nt", [(dgate, a_spec, g_gate, b_spec), (dup, a_spec, g_up, b_spec)],
                  (s // tm, d // tn, N_DEV), (tm, tn), _sds((s, d), F32), pl.BlockSpec((tm, tn), lambda i, j, k: (i, j)),
                  dep=rs_ffn[4])
    rs_ffn = _scatter_chips("ffn", rs_ffn, du2)
    dh1, dmix, dg_ffn_pre, dg_mix_post = _mid_norms_bwd(dy, du2, h1, mix, norm_ffn_pre, norm_mix_post)

    da_sb, da_fx, dgf = _merge_bwd(dmix, w_out_full, gf, a_sb, a_fx, dep=rs_ffn[4])
    dw_out = _mm_plain("dw_out", "nn", merged_t, dmix, BF16).reshape(N_DEV, cs, d)

    def branch_bwd(tag, da, w_b, o_t, width):
        tb = _tile(width, 1024)
        do = _matmul("do_" + tag, "nt",
                     [(da, pl.BlockSpec((tm, cs), lambda i, j, k: (i, k)),
                       w_b, pl.BlockSpec((None, tb, cs), lambda i, j, k: (k, j, 0)))],
                     (s // tm, width // tb, N_DEV), (tm, tb), _sds((s, width), BF16),
                     pl.BlockSpec((tm, tb), lambda i, j, k: (i, j)))
        dw = _matmul("dw_" + tag, "nn",
                     [(o_t, pl.BlockSpec((width, s), lambda j, i, k: (0, 0)),
                       da, pl.BlockSpec((s, cs), lambda j, i, k: (0, j)))],
                     (N_DEV, 1, 1), (width, cs), _sds((N_DEV, width, cs), BF16),
                     pl.BlockSpec((None, width, cs), lambda j, i, k: (j, 0, 0)))
        return do, dw

    do_sb, dw_sb = branch_bwd("sb", da_sb, g_sb, o_sb_t, d_sb)
    do_fx, dw_fx = branch_bwd("fox", da_fx, g_fx, o_fx_t, d_fox)

    rs_mid = _scatter_pairs("mid", [dw_sb, dw_fx, dw_out])

    dqkv = _sb_bwd(qkv, do_sb, tot, h_sb, rs_mid[4])
    rs_mid = _scatter_chips("mid", rs_mid, dqkv)
    dqkv, dcum = _fox_bwd(dqkv, qkv, do_fx, o_fx32, lse, cum_col, cum_row, h_fox, h_sb, rs_mid[4])
    dgf, db_part = _forget_bwd(dgf, dcum, gf, b_pad, f_blk)
    dw_in = _w_in_grad_parts(_mm_plain("dw_qkv", "nn", u_t, dqkv, BF16), _mm_plain("dw_gates", "nn", u_t, dgf, BF16), lay)
    rs_in = _scatter_pairs("in", [dw_in])
    du = _mm_plain("du_qkv", "nt", dqkv, w_cat, F32, tn=1024, dep=rs_in[4])
    rs_in = _scatter_chips("in", rs_in, du)
    du = _mm_plain("du_gates", "nt", dgf, w_cat, F32, tn=1024, k_off=n_qkv, init=du, dep=rs_in[4])
    dx, dg_mix_pre = _pre_norm_bwd(dh1, du, xs, norm_mix_pre)

    upd = {}

    def update_group(tag, rs, names, after):
        parts = _scatter_end(tag, rs, after)
        for nm, p in zip(names, parts):
            w, m, v = decayed.get(nm, weights[nm])
            upd[nm] = _update("update_" + nm, p, w, m, v, layout=lay if nm == "w_in" else None, decayed=nm in decayed)

    update_group("ffn", rs_ffn, ("w_ffn_gate", "w_ffn_up", "w_ffn_down"), [dx])
    update_group("mid", rs_mid, ("w_branch_sb", "w_branch_fox", "w_out"), [upd[nm][3] for nm in ("w_ffn_gate", "w_ffn_up", "w_ffn_down")])
    update_group("in", rs_in, ("w_in",), [upd[nm][3] for nm in ("w_branch_sb", "w_branch_fox", "w_out")])

    small = ((norm_mix_pre, m_norm_mix_pre, v_norm_mix_pre), (norm_mix_post, m_norm_mix_post, v_norm_mix_post),
             (norm_ffn_pre, m_norm_ffn_pre, v_norm_ffn_pre), (norm_ffn_post, m_norm_ffn_post, v_norm_ffn_post))
    pad_f = ((0, 0), (0, LANES - n_f))
    cat = lambda i: jnp.concatenate([t[i] for t in small] + [jnp.pad((b_forget, m_b_forget, v_b_forget)[i], pad_f)], axis=1)
    sm = _small_update(jnp.concatenate([dg_mix_pre, dg_mix_post, dg_ffn_pre, dg_ffn_post, db_part], axis=1),
                       cat(0), cat(1), cat(2))
    for i, nm in enumerate(("norm_mix_pre", "norm_mix_post", "norm_ffn_pre", "norm_ffn_post")):
        upd[nm] = [o[:, i * d:(i + 1) * d] for o in sm]
    upd["b_forget"] = [o[:, 4 * d:4 * d + n_f] for o in sm]

    loss = lax.psum(loss_part[0, 0], ("x", "y", "c"))
    order = ("norm_mix_pre", "norm_mix_post", "w_in", "b_forget", "w_branch_sb", "w_branch_fox", "w_out",
             "norm_ffn_pre", "norm_ffn_post", "w_ffn_gate", "w_ffn_up", "w_ffn_down")
    return (loss, dx[None]) + tuple(upd[nm][i] for i in range(4) for nm in order)
```

```python
import jax
import jax.numpy as jnp
from jax import lax
from jax.experimental import pallas as pl
from jax.experimental.pallas import tpu as pltpu

F32 = jnp.float32
BF16 = jnp.bfloat16
MESH = pl.DeviceIdType.MESH
ANY = pl.BlockSpec(memory_space=pl.ANY)
HBM = pl.BlockSpec(memory_space=pltpu.HBM)
SEM = pl.BlockSpec(memory_space=pltpu.SEMAPHORE)
EFFECT = pltpu.SideEffectType.DATAFLOW_SIDE_EFFECTING

N_DEV = 8
HEAD_DIM = 128
RMS_EPS = 1e-6
F_PAD = 512
LANES = 128
ATT_TQ = 256
ATT_TK = 256
ATT_HP = 4
NEG_BIG = -1e30
VMEM_LIMIT = 56 * 1024 * 1024

ADAM_LR = 0.001
ADAM_B1 = 0.9
ADAM_B2 = 0.999
ADAM_EPS = 1e-08
ADAM_WD = 0.01
ADAM_STEP = 10

_DIMS = {"nn": ((1,), (0,)), "nt": ((1,), (1,)), "tn": ((0,), (0,))}


def _params(sem):
    return pltpu.CompilerParams(dimension_semantics=sem, vmem_limit_bytes=VMEM_LIMIT)


def _dot(a, b, mode="nn"):
    return lax.dot_general(a.astype(BF16), b.astype(BF16), (_DIMS[mode], ((), ())), preferred_element_type=F32)


def _tile(n, pref):
    if n <= pref:
        return n
    t = (pref // LANES) * LANES
    while n % t:
        t -= LANES
    return t


def _split2(v):
    hi = v.astype(BF16)
    return hi, (v - hi.astype(F32)).astype(BF16)


def _split3(v):
    a = v.astype(BF16)
    r = v - a.astype(F32)
    b = r.astype(BF16)
    return a, b, (r - b.astype(F32)).astype(BF16)


def _tri(n, cmp):
    r = lax.broadcasted_iota(jnp.int32, (n, n), 0)
    c = lax.broadcasted_iota(jnp.int32, (n, n), 1)
    return jnp.where(cmp(r, c), 1.0, 0.0).astype(BF16)


def _lane_pick(v, h):
    lane = lax.broadcasted_iota(jnp.int32, v.shape, 1)
    return jnp.sum(jnp.where(lane == h, v, 0.0), axis=1, keepdims=True)


def _lane_put(ref, rows, h, col):
    old = ref[rows, :]
    lane = lax.broadcasted_iota(jnp.int32, old.shape, 1)
    ref[rows, :] = jnp.where(lane == h, col, old)


def _sigmoid(z):
    return 1.0 / (1.0 + jnp.exp(-z))


def _log_sigmoid(z):
    return jnp.minimum(z, 0.0) - jnp.log(1.0 + jnp.exp(-jnp.abs(z)))


def _sds(shape, dtype):
    return jax.ShapeDtypeStruct(shape, dtype)


def _matmul(name, mode, pairs, grid, acc_shape, out_shape, out_specs, extras=(), epilogue=None, init=None, dep=None):
    n_p, n_e = len(pairs), len(extras)
    nk = grid[-1]
    single = not isinstance(out_shape, (list, tuple))
    n_i = 0 if init is None else 1
    n_d = 0 if dep is None else 1

    one_step = nk == 1 and init is None

    def body(*refs):
        ab = refs[:2 * n_p]
        ex = refs[2 * n_p:2 * n_p + n_e]
        ini = refs[2 * n_p + n_e:2 * n_p + n_e + n_i]
        outs = refs[2 * n_p + n_e + n_i + n_d:len(refs) - (0 if one_step else 1)]

        def finish(total):
            if epilogue is None:
                outs[0][...] = total.astype(outs[0].dtype)
            else:
                epilogue(total, ex, outs)

        t = _dot(ab[0][...], ab[1][...], mode)
        for p in range(1, n_p):
            t = t + _dot(ab[2 * p][...], ab[2 * p + 1][...], mode)
        if one_step:
            finish(t)
            return
        acc = refs[-1]
        k = pl.program_id(len(grid) - 1)

        @pl.when(k == 0)
        def _():
            acc[...] = t if init is None else ini[0][...].astype(F32) + t

        @pl.when(k > 0)
        def _():
            acc[...] += t

        @pl.when(k == nk - 1)
        def _():
            finish(acc[...])

    in_specs = [s for (_, sa, _, sb) in pairs for s in (sa, sb)] + [s for (_, s) in extras]
    args = [v for (a, _, b, _) in pairs for v in (a, b)] + [e for (e, _) in extras]
    if init is not None:
        in_specs.append(init[1])
        args.append(init[0])
    if dep is not None:
        in_specs.append(ANY)
        args.append(dep)
    return pl.pallas_call(
        body, name=name, grid=grid, in_specs=in_specs,
        out_specs=out_specs if single else list(out_specs),
        out_shape=out_shape if single else list(out_shape),
        scratch_shapes=[] if one_step else [pltpu.VMEM(acc_shape, F32)],
        compiler_params=_params(("parallel",) * (len(grid) - 1) + ("arbitrary",)),
    )(*args)


def _mm_plain(name, mode, a, b, out_dtype, *, n_off=0, n=None, k_off=0, tm=1024, tn=1536, tk=2048, init=None, dep=None):
    if mode == "nn":
        (m, kk), nn_ = a.shape, b.shape[1]
    elif mode == "nt":
        (m, kk), nn_ = a.shape, b.shape[0]
    else:
        (kk, m), nn_ = a.shape, b.shape[1]
    n = nn_ if n is None else n
    tm, tn, tk = _tile(m, tm), _tile(n, tn), _tile(kk, tk)
    while n_off % tn or n % tn:
        tn -= LANES
    while k_off % tk or kk % tk:
        tk -= LANES
    off, koff = n_off // tn, k_off // tk
    a_spec = {"nn": pl.BlockSpec((tm, tk), lambda i, j, k: (i, k)),
              "nt": pl.BlockSpec((tm, tk), lambda i, j, k: (i, k)),
              "tn": pl.BlockSpec((tk, tm), lambda i, j, k: (k, i))}[mode]
    b_spec = {"nn": pl.BlockSpec((tk, tn), lambda i, j, k: (k, j + off)),
              "nt": pl.BlockSpec((tn, tk), lambda i, j, k: (j, k + koff)),
              "tn": pl.BlockSpec((tk, tn), lambda i, j, k: (k, j))}[mode]
    o_spec = pl.BlockSpec((tm, tn), lambda i, j, k: (i, j))
    if init is not None:
        init = (init, o_spec)
    return _matmul(name, mode, [(a, a_spec, b, b_spec)], (m // tm, n // tn, kk // tk), (tm, tn),
                   _sds((m, n), out_dtype), o_spec, init=init, dep=dep)


def _rows_call(name, body, ins, outs, s, tr=256, dep=None):
    def spec(v, per_row):
        if per_row == "transposed":
            return pl.BlockSpec((v.shape[0], tr), lambda i: (0, i))
        if per_row:
            return pl.BlockSpec((tr, v.shape[1]), lambda i: (i, 0))
        return pl.BlockSpec(v.shape, lambda i: (0, 0))
    n_in = len(ins)
    deps = [] if dep is None else [dep]

    def with_dep(*refs):
        body(*refs[:n_in], *refs[n_in + len(deps):])

    return pl.pallas_call(
        with_dep, name=name, grid=(s // tr,),
        in_specs=[spec(v, p) for v, p in ins] + [ANY] * len(deps), out_specs=[spec(v, p) for v, p in outs],
        out_shape=[_sds(v.shape, v.dtype) for v, _ in outs],
        compiler_params=_params(("arbitrary",)),
    )(*[v for v, _ in ins], *deps)


def _rsq(v):
    return lax.rsqrt(jnp.mean(v * v, axis=-1, keepdims=True) + RMS_EPS)


def _norm_bwd(dy, v, r, g):
    vh = v * r
    t = dy * g
    dv = r * (t - vh * jnp.mean(t * vh, axis=-1, keepdims=True))
    return dv, jnp.sum(dy * vh, axis=0, keepdims=True)


def _accum(ref, val):
    @pl.when(pl.program_id(0) == 0)
    def _():
        ref[...] = jnp.zeros_like(ref)
    ref[...] += val


def _pre_norm(x, g, dep=None):
    def body(x_ref, g_ref, u_ref, ut_ref):
        v = x_ref[...]
        u = (v * _rsq(v) * g_ref[...]).astype(BF16)
        u_ref[...] = u
        ut_ref[...] = u.T
    s, d = x.shape
    return _rows_call("pre_norm", body, [(x, True), (g, False)],
                      [(_sds((s, d), BF16), True), (_sds((d, s), BF16), "transposed")], s, dep=dep)


def _mid_norms(x, mix, g_post, g_pre):
    def body(x_ref, mix_ref, gp_ref, gn_ref, h_ref, u_ref, ut_ref):
        mv = mix_ref[...]
        h = x_ref[...] + mv * _rsq(mv) * gp_ref[...]
        h_ref[...] = h
        u = (h * _rsq(h) * gn_ref[...]).astype(BF16)
        u_ref[...] = u
        ut_ref[...] = u.T
    s, d = x.shape
    return _rows_call("mid_norms", body, [(x, True), (mix, True), (g_post, False), (g_pre, False)],
                      [(_sds((s, d), F32), True), (_sds((s, d), BF16), True), (_sds((d, s), BF16), "transposed")], s)


def _loss_head(h1, ff, target, g):
    s, d = h1.shape

    def body(h_ref, ff_ref, t_ref, g_ref, loss_ref, dy_ref, dff_ref, dg_ref):
        fv = ff_ref[...]
        r = _rsq(fv)
        err = h_ref[...] + fv * r * g_ref[...] - t_ref[...]
        part = 0.5 * jnp.sum(jnp.mean(err * err, axis=-1, keepdims=True), axis=0, keepdims=True)
        _accum(loss_ref, jnp.broadcast_to(part, loss_ref.shape))
        dy = err * (1.0 / d)
        dy_ref[...] = dy
        dff, dg = _norm_bwd(dy, fv, r, g_ref[...])
        dff_ref[...] = dff.astype(BF16)
        _accum(dg_ref, dg)

    return _rows_call("loss_head", body, [(h1, True), (ff, True), (target, True), (g, False)],
                      [(_sds((1, LANES), F32), False), (_sds((s, d), F32), True),
                       (_sds((s, d), BF16), True), (_sds((1, d), F32), False)], s)


def _mid_norms_bwd(dy, du2, h1, mix, g_pre, g_post):
    s, d = dy.shape

    def body(dy_ref, du_ref, h_ref, mix_ref, gn_ref, gp_ref, dh_ref, dmix_ref, dgn_ref, dgp_ref):
        h = h_ref[...]
        dh, dgn = _norm_bwd(du_ref[...], h, _rsq(h), gn_ref[...])
        dh = dh + dy_ref[...]
        dh_ref[...] = dh
        _accum(dgn_ref, dgn)
        mv = mix_ref[...]
        dmix, dgp = _norm_bwd(dh, mv, _rsq(mv), gp_ref[...])
        dmix_ref[...] = dmix.astype(BF16)
        _accum(dgp_ref, dgp)

    return _rows_call("mid_norms_bwd", body,
                      [(dy, True), (du2, True), (h1, True), (mix, True), (g_pre, False), (g_post, False)],
                      [(_sds((s, d), F32), True), (_sds((s, d), BF16), True),
                       (_sds((1, d), F32), False), (_sds((1, d), F32), False)], s)


def _pre_norm_bwd(dh1, du, x, g, dep=None):
    s, d = x.shape

    def body(dh_ref, du_ref, x_ref, g_ref, dx_ref, dg_ref):
        v = x_ref[...]
        dv, dg = _norm_bwd(du_ref[...], v, _rsq(v), g_ref[...])
        dx_ref[...] = dh_ref[...] + dv
        _accum(dg_ref, dg)

    return _rows_call("pre_norm_bwd", body, [(dh1, True), (du, True), (x, True), (g, False)],
                      [(_sds((s, d), F32), True), (_sds((1, d), F32), False)], s, dep=dep)


def _forget_fwd(gf, b_pad, f_blk):
    s = gf.shape[0]
    tb = ATT_TK
    nb = s // tb

    def body(f_ref, b_ref, col_ref, row_ref):
        incl = _tri(tb, lambda r, c: c <= r)
        carry = jnp.zeros((1, LANES), F32)
        for i in range(nb):
            lf = _log_sigmoid(f_ref[pl.ds(i * tb, tb), :] + b_ref[...])
            parts = _split3(lf)
            cum = carry + _dot(incl, parts[0]) + _dot(incl, parts[1]) + _dot(incl, parts[2])
            col_ref[pl.ds(i * tb, tb), :] = cum
            row_ref[i] = cum.T
            carry = carry + jnp.sum(lf, axis=0, keepdims=True)

    return pl.pallas_call(
        body, name="forget_fwd", grid=(1,),
        in_specs=[pl.BlockSpec((s, LANES), lambda i: (0, f_blk)), pl.BlockSpec((1, LANES), lambda i: (0, 0))],
        out_specs=[pl.BlockSpec((s, LANES), lambda i: (0, 0)), pl.BlockSpec((nb, LANES, tb), lambda i: (0, 0, 0))],
        out_shape=[_sds((s, LANES), F32), _sds((nb, LANES, tb), F32)],
        compiler_params=_params(("arbitrary",)),
    )(gf, b_pad)


def _forget_bwd(dgf, dcum, gf, b_pad, f_blk):
    s = gf.shape[0]
    tb = ATT_TK
    nb = s // tb
    sec = dgf.shape[1] // F_PAD - 1

    def body(dgf_hbm, dc_ref, f_ref, b_ref, out_ref, db_ref):
        del dgf_hbm
        incl = _tri(tb, lambda r, c: c >= r)
        carry = jnp.zeros((1, LANES), F32)
        db = jnp.zeros((1, LANES), F32)
        out_ref[...] = jnp.zeros_like(out_ref)
        for i in reversed(range(nb)):
            dc = dc_ref[pl.ds(i * tb, tb), :]
            parts = _split3(dc)
            dlf = carry + _dot(incl, parts[0]) + _dot(incl, parts[1]) + _dot(incl, parts[2])
            z = f_ref[pl.ds(i * tb, tb), :] + b_ref[...]
            df = dlf * _sigmoid(-z)
            out_ref[pl.ds(i * tb, tb), pl.ds(0, LANES)] = df.astype(BF16)
            db = db + jnp.sum(df, axis=0, keepdims=True)
            carry = carry + jnp.sum(dc, axis=0, keepdims=True)
        db_ref[...] = db

    return pl.pallas_call(
        body, name="forget_bwd", grid=(1,),
        in_specs=[ANY, pl.BlockSpec((s, LANES), lambda i: (0, 0)),
                  pl.BlockSpec((s, LANES), lambda i: (0, f_blk)), pl.BlockSpec((1, LANES), lambda i: (0, 0))],
        out_specs=[pl.BlockSpec((s, F_PAD), lambda i: (0, sec)), pl.BlockSpec((1, LANES), lambda i: (0, 0))],
        out_shape=[_sds(dgf.shape, BF16), _sds((1, LANES), F32)],
        input_output_aliases={0: 0},
        compiler_params=_params(("arbitrary",)),
    )(dgf, dcum, gf, b_pad)


def _diag_mask(strict):
    r = lax.broadcasted_iota(jnp.int32, (ATT_TQ, ATT_TK), 0)
    c = lax.broadcasted_iota(jnp.int32, (ATT_TQ, ATT_TK), 1)
    return c < r if strict else c <= r


def _qkv_specs(hb0, s):
    specs = []
    for j in range(ATT_HP):
        def col(g, j=j):
            return 3 * (hb0 + ATT_HP * g + j)
        specs += [pl.BlockSpec((ATT_TQ, HEAD_DIM), lambda g, i, col=col: (i, col(g))),
                  pl.BlockSpec((s, HEAD_DIM), lambda g, i, col=col: (0, col(g) + 1)),
                  pl.BlockSpec((s, HEAD_DIM), lambda g, i, col=col: (0, col(g) + 2))]
    return specs


def _head_cols(j):
    return pl.ds(j * HEAD_DIM, HEAD_DIM)


def _sb_fwd(qkv, n_heads):
    s = qkv.shape[0]
    scale = HEAD_DIM ** -0.5
    tq, tk = ATT_TQ, ATT_TK
    heads = range(ATT_HP)

    def body(*refs):
        qkv_refs, (o_ref, ot_ref, tot_ref) = refs[:3 * ATT_HP], refs[3 * ATT_HP:]
        g, i = pl.program_id(0), pl.program_id(1)

        @pl.when((g == 0) & (i == 0))
        def _():
            tot_ref[...] = jnp.zeros_like(tot_ref)

        qs = [qkv_refs[3 * j][...] for j in heads]
        upper = _tri(tk, lambda r, c: r > c)

        def tile(kj, carry, mask):
            rows = pl.ds(pl.multiple_of(kj * tk, tk), tk)
            z = [_dot(qs[j], qkv_refs[3 * j + 1][rows, :], "nt") * scale for j in heads]
            lsz = [_log_sigmoid(z[j]) for j in heads]
            lk = [lsz[j] - z[j] if mask is None else jnp.where(mask, lsz[j] - z[j], 0.0) for j in heads]
            parts = [_split2(lk[j]) for j in heads]
            above = [carry[j][0] + _dot(parts[j][0], upper) + _dot(parts[j][1], upper) for j in heads]
            w = [jnp.exp(lsz[j] + above[j]) for j in heads]
            if mask is not None:
                w = [jnp.where(mask, w[j], 0.0) for j in heads]
            return tuple((carry[j][0] + jnp.sum(lk[j], axis=1, keepdims=True),
                          carry[j][1] + _dot(w[j], qkv_refs[3 * j + 2][rows, :])) for j in heads)

        carry = tile(i, tuple((jnp.zeros((tq, 1), F32), jnp.zeros((tq, HEAD_DIM), F32)) for _ in heads), _diag_mask(True))
        carry = lax.fori_loop(0, i, lambda n, cr: tile(i - 1 - n, cr, None), carry)
        q_rows = pl.ds(pl.multiple_of(i * tq, tq), tq)
        for j in heads:
            c, acc = carry[j]
            o = acc.astype(BF16)
            o_ref[:, _head_cols(j)] = o
            ot_ref[_head_cols(j), :] = o.T
            _lane_put(tot_ref, q_rows, ATT_HP * g + j, c)

    wide = ATT_HP * HEAD_DIM
    return pl.pallas_call(
        body, name="sb_fwd", grid=(n_heads // ATT_HP, s // tq),
        in_specs=_qkv_specs(0, s),
        out_specs=[pl.BlockSpec((tq, wide), lambda g, i: (i, g)), pl.BlockSpec((wide, tq), lambda g, i: (g, i)),
                   pl.BlockSpec((s, LANES), lambda g, i: (0, 0))],
        out_shape=[_sds((s, n_heads * HEAD_DIM), BF16), _sds((n_heads * HEAD_DIM, s), BF16), _sds((s, LANES), F32)],
        compiler_params=_params(("arbitrary", "arbitrary")),
    )(*[qkv] * (3 * ATT_HP))


def _sb_bwd(qkv, do, tot, n_heads, dep):
    s = qkv.shape[0]
    scale = HEAD_DIM ** -0.5
    tq, tk = ATT_TQ, ATT_TK
    nq = s // tq
    hd = HEAD_DIM

    heads = range(ATT_HP)

    def body(*refs):
        qkv_refs = refs[:3 * ATT_HP]
        do_ref, tot_ref, _, out_ref, dk_acc, dv_acc = refs[3 * ATT_HP:]
        g, i = pl.program_id(0), pl.program_id(1)

        @pl.when(i == 0)
        def _():
            dk_acc[...] = jnp.zeros_like(dk_acc)
            dv_acc[...] = jnp.zeros_like(dv_acc)

        qs = [qkv_refs[3 * j][...] for j in heads]
        douts = [do_ref[:, _head_cols(j)] for j in heads]
        totals = [_lane_pick(tot_ref[...], ATT_HP * g + j) for j in heads]
        incl = _tri(tk, lambda r, c: r <= c)
        excl = _tri(tk, lambda r, c: r < c)

        def tile(kj, carry, mask):
            rows = pl.ds(pl.multiple_of(kj * tk, tk), tk)
            k_t = [qkv_refs[3 * j + 1][rows, :] for j in heads]
            z = [_dot(qs[j], k_t[j], "nt") * scale for j in heads]
            dw = [_dot(douts[j], qkv_refs[3 * j + 2][rows, :], "nt") for j in heads]
            lsz = [_log_sigmoid(z[j]) for j in heads]
            lk = [lsz[j] - z[j] if mask is None else jnp.where(mask, lsz[j] - z[j], 0.0) for j in heads]
            parts = [_split2(lk[j]) for j in heads]
            below = [carry[j][0] + _dot(parts[j][0], incl) + _dot(parts[j][1], incl) for j in heads]
            w = [jnp.exp(lsz[j] + (totals[j] - below[j])) for j in heads]
            if mask is not None:
                w = [jnp.where(mask, w[j], 0.0) for j in heads]
            e = [dw[j] * w[j] for j in heads]
            parts = [_split2(e[j]) for j in heads]
            e_before = [carry[j][1] + _dot(parts[j][0], excl) + _dot(parts[j][1], excl) for j in heads]
            sg = [jnp.exp(lsz[j]) for j in heads]
            dz = [e[j] * (1.0 - sg[j]) - e_before[j] * sg[j] for j in heads]
            if mask is not None:
                dz = [jnp.where(mask, dz[j], 0.0) for j in heads]
            dz = [(dz[j] * scale).astype(BF16) for j in heads]
            for j in heads:
                dk_acc[j, rows, :] += _dot(dz[j], qs[j], "tn")
                dv_acc[j, rows, :] += _dot(w[j], douts[j], "tn")
            return tuple((carry[j][0] + jnp.sum(lk[j], axis=1, keepdims=True),
                          carry[j][1] + jnp.sum(e[j], axis=1, keepdims=True),
                          carry[j][2] + _dot(dz[j], k_t[j])) for j in heads)

        zero = jnp.zeros((tq, 1), F32)
        carry = lax.fori_loop(0, i, lambda kj, cr: tile(kj, cr, None),
                              tuple((zero, zero, jnp.zeros((tq, hd), F32)) for _ in heads))
        carry = tile(i, carry, _diag_mask(True))
        for j in heads:
            out_ref[pl.ds(pl.multiple_of(i * tq, tq), tq), pl.ds(3 * j * hd, hd)] = carry[j][2].astype(BF16)

        @pl.when(i == nq - 1)
        def _():
            for j in heads:
                out_ref[:, pl.ds((3 * j + 1) * hd, hd)] = dk_acc[j].astype(BF16)
                out_ref[:, pl.ds((3 * j + 2) * hd, hd)] = dv_acc[j].astype(BF16)

    wide = ATT_HP * hd
    return pl.pallas_call(
        body, name="sb_bwd", grid=(n_heads // ATT_HP, nq),
        in_specs=_qkv_specs(0, s) + [pl.BlockSpec((tq, wide), lambda g, i: (i, g)),
                                     pl.BlockSpec((tq, LANES), lambda g, i: (i, 0)), ANY],
        out_specs=pl.BlockSpec((s, 3 * wide), lambda g, i: (0, g)),
        out_shape=_sds(qkv.shape, BF16),
        scratch_shapes=[pltpu.VMEM((ATT_HP, s, hd), F32), pltpu.VMEM((ATT_HP, s, hd), F32)],
        compiler_params=_params(("arbitrary", "arbitrary")),
    )(*[qkv] * (3 * ATT_HP), do, tot, dep)


def _fox_fwd(qkv, cum_col, cum_row, n_heads, hb0, dep):
    s = qkv.shape[0]
    scale = HEAD_DIM ** -0.5
    tq, tk = ATT_TQ, ATT_TK

    heads = range(ATT_HP)

    def body(*refs):
        qkv_refs = refs[:3 * ATT_HP]
        cc_ref, cr_ref, _, o_ref, ot_ref, o32_ref, lse_ref = refs[3 * ATT_HP:]
        g, i = pl.program_id(0), pl.program_id(1)

        @pl.when((g == 0) & (i == 0))
        def _():
            lse_ref[...] = jnp.zeros_like(lse_ref)

        qs = [qkv_refs[3 * j][...] for j in heads]
        cqs = [_lane_pick(cc_ref[...], ATT_HP * g + j) for j in heads]

        def tile(kj, carry, mask):
            rows = pl.ds(pl.multiple_of(kj * tk, tk), tk)
            sc = [_dot(qs[j], qkv_refs[3 * j + 1][rows, :], "nt") * scale + cqs[j]
                  - cr_ref[kj, pl.ds(ATT_HP * g + j, 1), :] for j in heads]
            if mask is not None:
                sc = [jnp.where(mask, sc[j], NEG_BIG) for j in heads]
            m_new = [jnp.maximum(carry[j][0], jnp.max(sc[j], axis=1, keepdims=True)) for j in heads]
            p = [jnp.exp(sc[j] - m_new[j]) for j in heads]
            alpha = [jnp.exp(carry[j][0] - m_new[j]) for j in heads]
            parts = [_split2(p[j]) for j in heads]
            v_t = [qkv_refs[3 * j + 2][rows, :] for j in heads]
            pv = [_dot(parts[j][0], v_t[j]) + _dot(parts[j][1], v_t[j]) for j in heads]
            return tuple((m_new[j], alpha[j] * carry[j][1] + jnp.sum(p[j], axis=1, keepdims=True),
                          alpha[j] * carry[j][2] + pv[j]) for j in heads)

        carry = tuple((jnp.full((tq, 1), NEG_BIG, F32), jnp.zeros((tq, 1), F32), jnp.zeros((tq, HEAD_DIM), F32))
                      for _ in heads)
        carry = lax.fori_loop(0, i, lambda kj, cr: tile(kj, cr, None), carry)
        carry = tile(i, carry, _diag_mask(False))
        q_rows = pl.ds(pl.multiple_of(i * tq, tq), tq)
        for j in heads:
            m, l, acc = carry[j]
            o = acc / l
            o_ref[:, _head_cols(j)] = o.astype(BF16)
            ot_ref[_head_cols(j), :] = o.astype(BF16).T
            o32_ref[:, _head_cols(j)] = o
            _lane_put(lse_ref, q_rows, ATT_HP * g + j, m + jnp.log(l))

    nb = cum_row.shape[0]
    wide = ATT_HP * HEAD_DIM
    return pl.pallas_call(
        body, name="fox_fwd", grid=(n_heads // ATT_HP, s // tq),
        in_specs=_qkv_specs(hb0, s) + [pl.BlockSpec((tq, LANES), lambda g, i: (i, 0)),
                                       pl.BlockSpec((nb, 8, tk), lambda g, i: (0, 0, 0)), ANY],
        out_specs=[pl.BlockSpec((tq, wide), lambda g, i: (i, g)), pl.BlockSpec((wide, tq), lambda g, i: (g, i)),
                   pl.BlockSpec((tq, wide), lambda g, i: (i, g)), pl.BlockSpec((s, LANES), lambda g, i: (0, 0))],
        out_shape=[_sds((s, n_heads * HEAD_DIM), BF16), _sds((n_heads * HEAD_DIM, s), BF16),
                   _sds((s, n_heads * HEAD_DIM), F32), _sds((s, LANES), F32)],
        compiler_params=_params(("arbitrary", "arbitrary")),
    )(*[qkv] * (3 * ATT_HP), cum_col, cum_row, dep)


def _fox_bwd(dqkv, qkv, do, o, lse, cum_col, cum_row, n_heads, hb0, dep):
    s = qkv.shape[0]
    scale = HEAD_DIM ** -0.5
    tq, tk = ATT_TQ, ATT_TK
    nq = s // tq
    hd = HEAD_DIM

    heads = range(ATT_HP)
    assert hb0 % ATT_HP == 0

    def body(*refs):
        qkv_refs = refs[1:1 + 3 * ATT_HP]
        do_ref, o_ref, lse_ref, cc_ref, cr_ref, _, out_ref, dc_ref, dk_acc, dv_acc, col_acc = refs[1 + 3 * ATT_HP:]
        g, i = pl.program_id(0), pl.program_id(1)

        @pl.when((g == 0) & (i == 0))
        def _():
            dc_ref[...] = jnp.zeros_like(dc_ref)

        @pl.when(i == 0)
        def _():
            dk_acc[...] = jnp.zeros_like(dk_acc)
            dv_acc[...] = jnp.zeros_like(dv_acc)
            col_acc[...] = jnp.zeros_like(col_acc)

        qs = [qkv_refs[3 * j][...] for j in heads]
        douts = [do_ref[:, _head_cols(j)] for j in heads]
        deltas = [jnp.sum(douts[j].astype(F32) * o_ref[:, _head_cols(j)], axis=1, keepdims=True) for j in heads]
        shifts = [_lane_pick(cc_ref[...], ATT_HP * g + j) - _lane_pick(lse_ref[...], ATT_HP * g + j) for j in heads]

        def tile(kj, carry, mask):
            rows = pl.ds(pl.multiple_of(kj * tk, tk), tk)
            k_t = [qkv_refs[3 * j + 1][rows, :] for j in heads]
            sc = [_dot(qs[j], k_t[j], "nt") * scale + shifts[j] - cr_ref[kj, pl.ds(ATT_HP * g + j, 1), :] for j in heads]
            dp = [_dot(douts[j], qkv_refs[3 * j + 2][rows, :], "nt") for j in heads]
            p = [jnp.exp(sc[j]) for j in heads]
            if mask is not None:
                p = [jnp.where(mask, p[j], 0.0) for j in heads]
            ds_f = [p[j] * (dp[j] - deltas[j]) for j in heads]
            ds = [(ds_f[j] * scale).astype(BF16) for j in heads]
            for j in heads:
                col_acc[j, kj] += jnp.broadcast_to(jnp.sum(ds_f[j], axis=0, keepdims=True), (8, tk))
                dk_acc[j, rows, :] += _dot(ds[j], qs[j], "tn")
                dv_acc[j, rows, :] += _dot(p[j], douts[j], "tn")
            return tuple((carry[j][0] + _dot(ds[j], k_t[j]), carry[j][1] + jnp.sum(ds_f[j], axis=1, keepdims=True))
                         for j in heads)

        carry = lax.fori_loop(0, i, lambda kj, cr: tile(kj, cr, None),
                              tuple((jnp.zeros((tq, hd), F32), jnp.zeros((tq, 1), F32)) for _ in heads))
        carry = tile(i, carry, _diag_mask(False))
        q_rows = pl.ds(pl.multiple_of(i * tq, tq), tq)
        for j in heads:
            out_ref[q_rows, pl.ds(3 * j * hd, hd)] = carry[j][0].astype(BF16)
            _lane_put(dc_ref, q_rows, ATT_HP * g + j, carry[j][1])

        @pl.when(i == nq - 1)
        def _():
            lane = lax.broadcasted_iota(jnp.int32, (tk, LANES), 1)
            for j in heads:
                out_ref[:, pl.ds((3 * j + 1) * hd, hd)] = dk_acc[j].astype(BF16)
                out_ref[:, pl.ds((3 * j + 2) * hd, hd)] = dv_acc[j].astype(BF16)
                for kj in range(nb):
                    col = jnp.broadcast_to(col_acc[j, kj][0:1, :], (LANES, tk)).T
                    old = dc_ref[pl.ds(kj * tk, tk), :]
                    dc_ref[pl.ds(kj * tk, tk), :] = jnp.where(lane == ATT_HP * g + j, old - col, old)

    nb = cum_row.shape[0]
    wide = ATT_HP * hd
    return pl.pallas_call(
        body, name="fox_bwd", grid=(n_heads // ATT_HP, nq),
        in_specs=[ANY] + _qkv_specs(hb0, s) + [
            pl.BlockSpec((tq, wide), lambda g, i: (i, g)), pl.BlockSpec((tq, wide), lambda g, i: (i, g)),
            pl.BlockSpec((tq, LANES), lambda g, i: (i, 0)), pl.BlockSpec((tq, LANES), lambda g, i: (i, 0)),
            pl.BlockSpec((nb, 8, tk), lambda g, i: (0, 0, 0)), ANY],
        out_specs=[pl.BlockSpec((s, 3 * wide), lambda g, i: (0, hb0 // ATT_HP + g)),
                   pl.BlockSpec((s, LANES), lambda g, i: (0, 0))],
        out_shape=[_sds(dqkv.shape, BF16), _sds((s, LANES), F32)],
        scratch_shapes=[pltpu.VMEM((ATT_HP, s, hd), F32), pltpu.VMEM((ATT_HP, s, hd), F32),
                        pltpu.VMEM((ATT_HP, s // tk, 8, tk), F32)],
        input_output_aliases={0: 0},
        compiler_params=_params(("arbitrary", "arbitrary")),
    )(dqkv, *[qkv] * (3 * ATT_HP), do, o, lse, cum_col, cum_row, dep)


def _branch_merge(o_sb, o_fx, w_sb, w_fx, gf, dep, tm=1024):
    s = o_sb.shape[0]
    cs = w_sb.shape[2]
    tm = _tile(s, tm)

    def body(osb_ref, ofx_ref, wsb_ref, wfx_ref, g_ref, dep_ref, merged_ref, mt_ref, asb_ref, afx_ref):
        del dep_ref
        a_sb = _dot(osb_ref[...], wsb_ref[...])
        a_fx = _dot(ofx_ref[...], wfx_ref[...])
        g = g_ref[...]
        merged = (_sigmoid(g[:, :cs]) * a_sb + _sigmoid(g[:, cs:]) * a_fx).astype(BF16)
        merged_ref[...] = merged
        mt_ref[...] = merged.T
        asb_ref[...] = a_sb.astype(BF16)
        afx_ref[...] = a_fx.astype(BF16)

    blk = pl.BlockSpec((tm, cs), lambda i, j: (i, j))
    out = _sds((s, N_DEV * cs), BF16)
    return pl.pallas_call(
        body, name="branch_merge", grid=(s // tm, N_DEV),
        in_specs=[pl.BlockSpec((tm, o_sb.shape[1]), lambda i, j: (i, 0)),
                  pl.BlockSpec((tm, o_fx.shape[1]), lambda i, j: (i, 0)),
                  pl.BlockSpec((None,) + w_sb.shape[1:], lambda i, j: (j, 0, 0)),
                  pl.BlockSpec((None,) + w_fx.shape[1:], lambda i, j: (j, 0, 0)),
                  pl.BlockSpec((tm, 2 * cs), lambda i, j: (i, j)), ANY],
        out_specs=[blk, pl.BlockSpec((cs, tm), lambda i, j: (j, i)), blk, blk],
        out_shape=[out, _sds((N_DEV * cs, s), BF16), out, out],
        compiler_params=_params(("parallel", "arbitrary")),
    )(o_sb, o_fx, w_sb, w_fx, gf, dep)


def _merge_bwd(dmix, w_out, gf, a_sb, a_fx, tm=1024, tk=2048, dep=None):
    s, d = dmix.shape
    cs = d // N_DEV
    tm, tk = _tile(s, tm), _tile(d, tk)

    def epilogue(acc, ex, outs):
        g, a_sb, a_fx = ex[0][...], ex[1][...].astype(F32), ex[2][...].astype(F32)
        s_sb, s_fx = _sigmoid(g[:, :cs]), _sigmoid(g[:, cs:])
        outs[0][...] = (acc * s_sb).astype(BF16)
        outs[1][...] = (acc * s_fx).astype(BF16)
        outs[2][...] = jnp.concatenate([acc * a_sb * s_sb * (1.0 - s_sb), acc * a_fx * s_fx * (1.0 - s_fx)],
                                       axis=1).astype(BF16)

    blk = pl.BlockSpec((tm, cs), lambda i, j, k: (i, j))
    wide = pl.BlockSpec((tm, 2 * cs), lambda i, j, k: (i, j))
    return _matmul(
        "merge_bwd", "nt",
        [(dmix, pl.BlockSpec((tm, tk), lambda i, j, k: (i, k)), w_out, pl.BlockSpec((cs, tk), lambda i, j, k: (j, k)))],
        (s // tm, N_DEV, d // tk), (tm, cs),
        [_sds((s, d), BF16), _sds((s, d), BF16), _sds(gf.shape, BF16)], [blk, blk, wide],
        extras=[(gf, wide), (a_sb, blk), (a_fx, blk)], epilogue=epilogue, dep=dep)


def _ffn_up(u2, w_gate, w_up, tm=1024):
    s, d = u2.shape
    fs = w_gate.shape[2]
    tm = _tile(s, tm)

    def body(u_ref, wg_ref, wu_ref, gate_ref, up_ref, act_ref, actt_ref):
        u = u_ref[...]
        gate = _dot(u, wg_ref[...])
        up = _dot(u, wu_ref[...])
        gate_ref[...] = gate
        up_ref[...] = up
        act = (gate * _sigmoid(gate) * up).astype(BF16)
        act_ref[...] = act
        actt_ref[...] = act.T

    w_spec = pl.BlockSpec((None, d, fs), lambda i, j: (j, 0, 0))
    o_spec = pl.BlockSpec((None, tm, fs), lambda i, j: (j, i, 0))
    return pl.pallas_call(
        body, name="ffn_up", grid=(s // tm, N_DEV),
        in_specs=[pl.BlockSpec((tm, d), lambda i, j: (i, 0)), w_spec, w_spec],
        out_specs=[o_spec, o_spec, o_spec, pl.BlockSpec((None, fs, tm), lambda i, j: (j, 0, i))],
        out_shape=[_sds((N_DEV, s, fs), F32), _sds((N_DEV, s, fs), F32), _sds((N_DEV, s, fs), BF16),
                   _sds((N_DEV, fs, s), BF16)],
        compiler_params=_params(("parallel", "arbitrary")),
    )(u2, w_gate, w_up)


def _ffn_down_bwd(dff, w_down, gate, up, tm=1024):
    s, d = dff.shape
    fs = w_down.shape[1]
    tm = _tile(s, tm)

    def body(dff_ref, wd_ref, gate_ref, up_ref, dgate_ref, dup_ref):
        dact = _dot(dff_ref[...], wd_ref[...], "nt")
        gate = gate_ref[...]
        sg = _sigmoid(gate)
        dup_ref[...] = (dact * gate * sg).astype(BF16)
        dgate_ref[...] = (dact * up_ref[...] * sg * (1.0 + gate * (1.0 - sg))).astype(BF16)

    a_spec = pl.BlockSpec((None, tm, fs), lambda i, j: (j, i, 0))
    return pl.pallas_call(
        body, name="ffn_down_bwd", grid=(s // tm, N_DEV),
        in_specs=[pl.BlockSpec((tm, d), lambda i, j: (i, 0)), pl.BlockSpec((None, fs, d), lambda i, j: (j, 0, 0)),
                  a_spec, a_spec],
        out_specs=[a_spec, a_spec],
        out_shape=[_sds((N_DEV, s, fs), BF16), _sds((N_DEV, s, fs), BF16)],
        compiler_params=_params(("parallel", "arbitrary")),
    )(dff, w_down, gate, up)


def _mesh_place():
    x, y, c = lax.axis_index("x"), lax.axis_index("y"), lax.axis_index("c")
    peers = []
    for d in range(1, N_DEV):
        px = 1 - x if d & 4 else x
        py = 1 - y if d & 2 else y
        pc = 1 - c if d & 1 else c
        peers.append((d, (px, py, pc), 4 * px + 2 * py + pc))
    return 4 * x + 2 * y + c, peers


def _flat_me():
    return 4 * lax.axis_index("x") + 2 * lax.axis_index("y") + lax.axis_index("c")


def _in_hbm(a):
    return pltpu.with_memory_space_constraint(a, pltpu.HBM)


def _pair_plan():
    x, y, c = lax.axis_index("x"), lax.axis_index("y"), lax.axis_index("c")
    return [(2 * q + (1 - c), q, q, (x, y, 1 - c)) for q in range(4)]


def _chip_plan():
    x, y, c = lax.axis_index("x"), lax.axis_index("y"), lax.axis_index("c")
    plan = []
    for fx, fy in ((1, 0), (0, 1), (1, 1)):
        cx, cy = (1 - x if fx else x), (1 - y if fy else y)
        plan.append((2 * cx + cy, 2 * x + y, 2 * cx + cy, (cx, cy, c)))
    return plan


def _split_start(name, srcs, lands, plan, k):
    n = len(srcs)

    def body(*refs):
        ins, lnd = refs[:n], refs[n:2 * n]
        send, recv, token = refs[2 * n], refs[2 * n + 1], refs[-1]
        copies = plan()
        for a in range(n):
            for t, (src, dst, _, dev) in enumerate(copies):
                pltpu.make_async_remote_copy(src_ref=ins[a].at[src], dst_ref=lnd[a].at[dst], send_sem=send.at[k * a + t],
                                             recv_sem=recv.at[k * a + t], device_id=dev, device_id_type=MESH).start()
        token[...] = jnp.zeros_like(token)

    res = pl.pallas_call(
        body, name=name,
        out_shape=[pltpu.SemaphoreType.DMA((n * k,)), pltpu.SemaphoreType.DMA((n * k,))]
        + [pltpu.HBM(a.shape, a.dtype) for a in list(srcs) + list(lands)] + [_sds((8, LANES), F32)],
        in_specs=[HBM] * (2 * n), out_specs=[SEM, SEM] + [HBM] * (2 * n) + [pl.BlockSpec(memory_space=pltpu.VMEM)],
        input_output_aliases={i: 2 + i for i in range(2 * n)},
        compiler_params=pltpu.CompilerParams(has_side_effects=EFFECT),
    )(*[_in_hbm(a) for a in srcs], *[_in_hbm(a) for a in lands])
    return res[0], res[1], res[2:2 + n], res[2 + n:2 + 2 * n], res[-1]


def _split_wait(name, send, recv, srcs, lands, plan, k, after):
    n = len(srcs)

    def body(*refs):
        ins, lnd = refs[:n], refs[n:2 * n]
        send_sem, recv_sem = refs[2 * n], refs[2 * n + 1]
        copies = plan()
        for a in range(n):
            for t, (src, _, dst, dev) in enumerate(copies):
                cp = pltpu.make_async_remote_copy(src_ref=ins[a].at[src], dst_ref=lnd[a].at[dst], send_sem=send_sem.at[k * a + t],
                                                  recv_sem=recv_sem.at[k * a + t], device_id=dev, device_id_type=MESH)
                cp.wait_send()
                cp.wait_recv()

    res = pl.pallas_call(
        body, name=name,
        out_shape=[pltpu.HBM(a.shape, a.dtype) for a in list(srcs) + list(lands)],
        in_specs=[HBM] * (2 * n) + [SEM, SEM] + [ANY] * len(after), out_specs=[HBM] * (2 * n),
        input_output_aliases={i: i for i in range(2 * n)},
        compiler_params=pltpu.CompilerParams(has_side_effects=EFFECT),
    )(*srcs, *lands, send, recv, *after)
    return res[:n], res[n:]


def _pair_add(name, parts, land):
    _, r, cols = parts.shape
    tr = max(16, min(r, ((1 << 20) // (2 * cols)) // 16 * 16))
    while r % tr:
        tr -= 16

    def body(c_ref, p_ref, l_ref, o_ref):
        del c_ref
        o_ref[...] = (p_ref[...].astype(F32) + l_ref[...].astype(F32)).astype(BF16)

    blk = pl.BlockSpec((None, tr, cols), lambda q, i, c_ref: (q, i, 0))
    return pl.pallas_call(
        body, name=name,
        grid_spec=pltpu.PrefetchScalarGridSpec(
            num_scalar_prefetch=1, grid=(4, r // tr),
            in_specs=[pl.BlockSpec((None, tr, cols), lambda q, i, c_ref: (2 * q + c_ref[0], i, 0)), blk], out_specs=blk),
        out_shape=_sds((4, r, cols), BF16),
        compiler_params=_params(("parallel", "parallel")),
    )(jnp.reshape(lax.axis_index("c"), (1,)).astype(jnp.int32), parts, land)


def _scatter_pairs(tag, parts):
    lands = [lax.empty((4,) + a.shape[1:], a.dtype) for a in parts]
    return _split_start("pair_" + tag, parts, lands, _pair_plan, 4)


def _scatter_chips(tag, started, after):
    send, recv, parts, lands, _ = started
    parts, lands = _split_wait("pair_" + tag + "_wait", send, recv, parts, lands, _pair_plan, 4, [after])
    sums = [_pair_add("pair_" + tag + "_add%d" % a, p, l) for a, (p, l) in enumerate(zip(parts, lands))]
    chip = 2 * lax.axis_index("x") + lax.axis_index("y")
    final = [lax.dynamic_update_slice_in_dim(lax.empty(v.shape, v.dtype), lax.dynamic_slice_in_dim(v, chip, 1, 0), chip, 0)
             for v in sums]
    return _split_start("chips_" + tag, sums, final, _chip_plan, 3)


def _scatter_end(tag, started, after):
    send, recv, sums, final, _ = started
    return _split_wait("chips_" + tag + "_wait", send, recv, sums, final, _chip_plan, 3, after)[1]


def _gather_targets():
    x, y, c = lax.axis_index("x"), lax.axis_index("y"), lax.axis_index("c")
    chips = [(x, y), (1 - x, y), (x, 1 - y), (1 - x, 1 - y)]
    same = [((cx, cy, c), 4 * cx + 2 * cy + c) for cx, cy in chips]
    other = [((cx, cy, 1 - c), 4 * cx + 2 * cy + 1 - c) for cx, cy in chips]
    return same[0][1], [other[0]] + same[1:], [flat for _, flat in other[1:]], other[0][0]


def _gather_start(shards):
    n = len(shards)
    me = _flat_me()
    lands = [lax.dynamic_update_slice_in_dim(lax.empty((N_DEV,) + a.shape, a.dtype), a[None], me, 0) for a in shards]

    def body(*refs):
        lnd, send, recv, token = refs[:n], refs[n], refs[n + 1], refs[-1]
        mine, targets, _, _ = _gather_targets()
        for a in range(n):
            for t, (dev, _) in enumerate(targets):
                pltpu.make_async_remote_copy(src_ref=lnd[a].at[mine], dst_ref=lnd[a].at[mine], send_sem=send.at[4 * a + t],
                                             recv_sem=recv.at[4 * a + t], device_id=dev, device_id_type=MESH).start()
        token[...] = jnp.zeros_like(token)

    res = pl.pallas_call(
        body, name="gather_start",
        out_shape=[pltpu.SemaphoreType.DMA((4 * n,)), pltpu.SemaphoreType.DMA((4 * n,))]
        + [pltpu.HBM(a.shape, a.dtype) for a in lands] + [_sds((8, LANES), F32)],
        in_specs=[HBM] * n, out_specs=[SEM, SEM] + [HBM] * n + [pl.BlockSpec(memory_space=pltpu.VMEM)],
        input_output_aliases={i: 2 + i for i in range(n)},
        compiler_params=pltpu.CompilerParams(has_side_effects=EFFECT),
    )(*[_in_hbm(a) for a in lands])
    return res[0], res[1], list(res[2:2 + n]), res[-1]


def _gather_forward(name, lands, first, send, recv, after):
    n = len(lands)

    def body(*refs):
        lnd, send_sem, recv_sem = refs[:n], refs[n], refs[n + 1]
        send2, recv2, token = refs[-3], refs[-2], refs[-1]
        mine, targets, _, sibling = _gather_targets()
        for a in range(n):
            for t, (dev, flat) in enumerate(targets):
                cp = pltpu.make_async_remote_copy(src_ref=lnd[a].at[mine], dst_ref=lnd[a].at[flat],
                                                  send_sem=send_sem.at[4 * (first + a) + t],
                                                  recv_sem=recv_sem.at[4 * (first + a) + t], device_id=dev, device_id_type=MESH)
                cp.wait_send()
                if t:
                    cp.wait_recv()
                    pltpu.make_async_remote_copy(src_ref=lnd[a].at[flat], dst_ref=lnd[a].at[flat], send_sem=send2.at[3 * a + t - 1],
                                                 recv_sem=recv2.at[3 * a + t - 1], device_id=sibling, device_id_type=MESH).start()
        token[...] = jnp.zeros_like(token)

    res = pl.pallas_call(
        body, name=name,
        out_shape=[pltpu.HBM(a.shape, a.dtype) for a in lands]
        + [pltpu.SemaphoreType.DMA((3 * n,)), pltpu.SemaphoreType.DMA((3 * n,)), _sds((8, LANES), F32)],
        in_specs=[HBM] * n + [SEM, SEM] + [ANY] * len(after),
        out_specs=[HBM] * n + [SEM, SEM, pl.BlockSpec(memory_space=pltpu.VMEM)],
        input_output_aliases={i: i for i in range(n)},
        compiler_params=pltpu.CompilerParams(has_side_effects=EFFECT),
    )(*lands, send, recv, *after)
    return list(res[:n]), res[n], res[n + 1], res[-1]


def _gather_wait(name, lands, first, recv, send2, recv2, after):
    n = len(lands)

    def body(*refs):
        lnd, recv_sem, send2_sem, recv2_sem = refs[:n], refs[n], refs[n + 1], refs[n + 2]
        mine, targets, passed, sibling = _gather_targets()
        for a in range(n):
            dev, flat = targets[0]
            pltpu.make_async_remote_copy(src_ref=lnd[a].at[mine], dst_ref=lnd[a].at[flat], send_sem=send2_sem.at[3 * a],
                                         recv_sem=recv_sem.at[4 * (first + a)], device_id=dev, device_id_type=MESH).wait_recv()
            for t in range(3):
                cp = pltpu.make_async_remote_copy(src_ref=lnd[a].at[targets[t + 1][1]], dst_ref=lnd[a].at[passed[t]],
                                                  send_sem=send2_sem.at[3 * a + t], recv_sem=recv2_sem.at[3 * a + t],
                                                  device_id=sibling, device_id_type=MESH)
                cp.wait_send()
                cp.wait_recv()

    res = pl.pallas_call(
        body, name=name, out_shape=[pltpu.HBM(a.shape, a.dtype) for a in lands],
        in_specs=[HBM] * n + [SEM, SEM, SEM, ANY], out_specs=[HBM] * n,
        input_output_aliases={i: i for i in range(n)},
        compiler_params=pltpu.CompilerParams(has_side_effects=EFFECT),
    )(*lands, recv, send2, recv2, after)
    return list(res)


def _adamw_decay(w, m, v):
    return ADAM_WD * w, ADAM_B1 * m, ADAM_B2 * v


def _adamw_finish(g, wd_w, m1, v1):
    m = m1 + (1.0 - ADAM_B1) * g
    v = v1 + (1.0 - ADAM_B2) * (g * g)
    m_hat = m / (1.0 - ADAM_B1 ** ADAM_STEP)
    v_hat = v / (1.0 - ADAM_B2 ** ADAM_STEP)
    delta = -ADAM_LR * (m_hat / (jnp.sqrt(v_hat) + ADAM_EPS) + wd_w)
    return delta, m, v


def _adamw(g, w, m, v):
    return _adamw_finish(g, *_adamw_decay(w, m, v))


def _update_prep(name, w, m, v, block_bytes=1 << 20):
    _, r, c = w.shape
    tr = max(8, min(r, (block_bytes // (4 * c)) // 8 * 8))
    while r % tr:
        tr -= 8

    def body(w_ref, m_ref, v_ref, ow_ref, om_ref, ov_ref):
        ow_ref[...], om_ref[...], ov_ref[...] = _adamw_decay(w_ref[...], m_ref[...], v_ref[...])

    blk = pl.BlockSpec((None, tr, c), lambda i: (0, i, 0))
    return pl.pallas_call(
        body, name=name, grid=(r // tr,), in_specs=[blk] * 3, out_specs=[blk] * 3, out_shape=[_sds((1, r, c), F32)] * 3,
        compiler_params=_params(("parallel",)),
    )(w, m, v)


def _update(name, parts, w, m, v, layout=None, decayed=False, block_bytes=1 << 20):
    _, r, c = w.shape
    n_slots, _, cp = parts.shape
    tr = max(8, min(r, (block_bytes // (4 * cp)) // 8 * 8))
    while r % tr:
        tr -= 8

    def body(p_ref, w_ref, m_ref, v_ref, g_ref, d_ref, nm_ref, nv_ref, *scratch):
        g = p_ref[0].astype(F32)
        for p in range(1, n_slots):
            g = g + p_ref[p].astype(F32)
        if layout is not None:
            s1, s2, lg = layout.my_shifts()
            lane = lax.broadcasted_iota(jnp.int32, g.shape, 1)
            scratch[0][...] = jnp.where(lane < lg, pltpu.roll(g, cp - s1, 1), pltpu.roll(g, cp - s2, 1))
            g = scratch[0][:, 0:c]
        g_ref[...] = g
        step = _adamw_finish if decayed else _adamw
        d_ref[...], nm_ref[...], nv_ref[...] = step(g, w_ref[...], m_ref[...], v_ref[...])

    blk = pl.BlockSpec((None, tr, c), lambda i: (0, i, 0))
    return pl.pallas_call(
        body, name=name, grid=(r // tr,),
        in_specs=[pl.BlockSpec((n_slots, tr, cp), lambda i: (0, i, 0)), blk, blk, blk],
        out_specs=[blk] * 4, out_shape=[_sds((1, r, c), F32)] * 4,
        scratch_shapes=[] if layout is None else [pltpu.VMEM((tr, cp), F32)],
        compiler_params=_params(("parallel",)),
    )(parts, w, m, v)


def _small_update(part, w, m, v):
    n = part.shape[1]

    def body(p_ref, w_ref, m_ref, v_ref, g_ref, d_ref, nm_ref, nv_ref, buf, send, recv):
        me, peers = _mesh_place()
        buf[me] = p_ref[...]
        sent = []
        for d, dev, flat in peers:
            cp = pltpu.make_async_remote_copy(src_ref=p_ref, dst_ref=buf.at[me], send_sem=send.at[d],
                                              recv_sem=recv.at[d], device_id=dev, device_id_type=MESH)
            cp.start()
            sent.append(cp)
        for d, dev, flat in peers:
            pltpu.make_async_remote_copy(src_ref=p_ref, dst_ref=buf.at[flat], send_sem=send.at[d],
                                         recv_sem=recv.at[d], device_id=dev, device_id_type=MESH).wait_recv()
        for cp in sent:
            cp.wait_send()
        g = buf[0]
        for p in range(1, N_DEV):
            g = g + buf[p]
        g_ref[...] = g
        d_ref[...], nm_ref[...], nv_ref[...] = _adamw(g, w_ref[...], m_ref[...], v_ref[...])

    vm = pl.BlockSpec(memory_space=pltpu.VMEM)
    return pl.pallas_call(
        body, name="small_update", in_specs=[vm] * 4, out_specs=[vm] * 4, out_shape=[_sds((1, n), F32)] * 4,
        scratch_shapes=[pltpu.VMEM((N_DEV, 1, n), F32), pltpu.SemaphoreType.DMA((N_DEV,)),
                        pltpu.SemaphoreType.DMA((N_DEV,))],
    )(part, w, m, v)


class _WInLayout:
    def __init__(self, n8, n_f, d_sb, d_fox, d):
        assert n8 % LANES == 1 and n_f < LANES and d % (N_DEV * LANES) == 0
        self.n8, self.n_f, self.d = n8, n_f, d
        self.sp = n8 // LANES
        self.wp = (n8 + 2 * LANES - 2) // LANES * LANES
        self.n_qkv = 3 * (d_sb + d_fox)
        nq, dt, tc = self.n_qkv // LANES, d // LANES, d // N_DEV // LANES
        h_sb, h_fox = d_sb // HEAD_DIM, d_fox // HEAD_DIM
        self.sources = {}
        self.part_tile = {}
        for p in range(N_DEV):
            lg = min(max(self.n_qkv + n_f - n8 * p, 0), n8)
            s1, s2 = p, p + LANES - n_f
            spans = []
            if lg > 0:
                spans.append(("a", self.sp * p, s1 // LANES, (lg + s1 - 1) // LANES))
            if lg < n8:
                spans.append(("g", self.sp * p - 1 - nq, (lg + s2) // LANES, (n8 - 1 + s2) // LANES))
            for kind, base, first, last in spans:
                for i in range(first, last + 1):
                    assert (p, i) not in self.part_tile
                    self.part_tile[(p, i)] = (kind, base + i)
                    self.sources.setdefault((kind, base + i), []).append((p, i))
        self.cat_tiles = [("a", r * h_sb + h) for h in range(h_sb) for r in range(3)]
        self.cat_tiles += [("a", 3 * h_sb + r * h_fox + h) for h in range(h_fox) for r in range(3)]
        self.cat_tiles += [("g", which * dt + j * tc + half) for j in range(N_DEV) for which in (0, 1) for half in range(tc)]
        self.cat_tiles += [("a", nq)] + [None] * (F_PAD // LANES - 1)
        self.cat_index = {key: c for c, key in enumerate(self.cat_tiles) if key is not None}

    def my_shifts(self):
        me = _flat_me()
        return me, me + LANES - self.n_f, jnp.clip(self.n_qkv + self.n_f - self.n8 * me, 0, self.n8)


def _lane_tile(i):
    return pl.ds(i * LANES, LANES)


def _w_in_shift(w_in, lay, tr=256):
    _, d, n8 = w_in.shape

    def body(w_ref, o_ref, buf):
        buf[...] = jnp.zeros_like(buf)
        buf[:, 0:n8] = w_ref[...]
        v = buf[...]
        s1, s2, lg = lay.my_shifts()
        pos = lax.broadcasted_iota(jnp.int32, v.shape, 1)
        o_ref[...] = jnp.where(pos < lg + s1, pltpu.roll(v, s1, 1),
                               jnp.where(pos >= lg + s2, pltpu.roll(v, s2, 1), 0.0)).astype(BF16)

    return pl.pallas_call(
        body, name="w_in_shift", grid=(d // tr,),
        in_specs=[pl.BlockSpec((None, tr, n8), lambda i: (0, i, 0))],
        out_specs=pl.BlockSpec((tr, lay.wp), lambda i: (i, 0)), out_shape=_sds((d, lay.wp), BF16),
        scratch_shapes=[pltpu.VMEM((tr, lay.wp), F32)],
        compiler_params=_params(("parallel",)),
    )(w_in)


def _w_in_build(g_in, lay, tr=256):
    d = g_in.shape[1]
    width = len(lay.cat_tiles) * LANES

    def body(g_ref, o_ref):
        for c, key in enumerate(lay.cat_tiles):
            if key is None:
                o_ref[:, _lane_tile(c)] = jnp.zeros((tr, LANES), BF16)
                continue
            (p, i), *more = lay.sources[key]
            val = g_ref[p, :, _lane_tile(i)]
            for p2, i2 in more:
                val = val + g_ref[p2, :, _lane_tile(i2)]
            o_ref[:, _lane_tile(c)] = val

    return pl.pallas_call(
        body, name="w_in_build", grid=(d // tr,),
        in_specs=[pl.BlockSpec((N_DEV, tr, lay.wp), lambda i: (0, i, 0))],
        out_specs=pl.BlockSpec((tr, width), lambda i: (i, 0)), out_shape=_sds((d, width), BF16),
        compiler_params=_params(("parallel",)),
    )(g_in)


def _w_in_grad_parts(dwq, dwgf, lay, tr=256):
    d = dwq.shape[0]
    nq = lay.n_qkv // LANES

    def body(q_ref, g_ref, o_ref):
        for p in range(N_DEV):
            for i in range(lay.wp // LANES):
                key = lay.part_tile.get((p, i))
                if key is None:
                    o_ref[p, :, _lane_tile(i)] = jnp.zeros((tr, LANES), BF16)
                    continue
                c = lay.cat_index[key]
                o_ref[p, :, _lane_tile(i)] = q_ref[:, _lane_tile(c)] if c < nq else g_ref[:, _lane_tile(c - nq)]

    return pl.pallas_call(
        body, name="w_in_grad_parts", grid=(d // tr,),
        in_specs=[pl.BlockSpec((tr, dwq.shape[1]), lambda i: (i, 0)), pl.BlockSpec((tr, dwgf.shape[1]), lambda i: (i, 0))],
        out_specs=pl.BlockSpec((N_DEV, tr, lay.wp), lambda i: (0, i, 0)), out_shape=_sds((N_DEV, d, lay.wp), BF16),
        compiler_params=_params(("parallel",)),
    )(dwq, dwgf)


def kernel(x, norm_mix_pre, norm_mix_post, w_in, b_forget, w_branch_sb, w_branch_fox, w_out, norm_ffn_pre, norm_ffn_post, w_ffn_gate, w_ffn_up, w_ffn_down, loss_target, m_norm_mix_pre, m_norm_mix_post, m_w_in, m_b_forget, m_w_branch_sb, m_w_branch_fox, m_w_out, m_norm_ffn_pre, m_norm_ffn_post, m_w_ffn_gate, m_w_ffn_up, m_w_ffn_down, v_norm_mix_pre, v_norm_mix_post, v_w_in, v_b_forget, v_w_branch_sb, v_w_branch_fox, v_w_out, v_norm_ffn_pre, v_norm_ffn_post, v_w_ffn_gate, v_w_ffn_up, v_w_ffn_down):
    xs, target = x[0], loss_target[0]
    s, d = xs.shape
    d_sb, d_fox = w_branch_sb.shape[1], w_branch_fox.shape[1]
    h_sb, h_fox = d_sb // HEAD_DIM, d_fox // HEAD_DIM
    n_f = b_forget.shape[1]
    fs = w_ffn_gate.shape[2]
    cs = d // N_DEV
    n_qkv = 3 * (d_sb + d_fox)
    n_gf = 2 * d + F_PAD
    f_blk = 2 * d // LANES
    big = (w_in, w_branch_sb, w_branch_fox, w_out, w_ffn_gate, w_ffn_up, w_ffn_down)
    big_m = (m_w_in, m_w_branch_sb, m_w_branch_fox, m_w_out, m_w_ffn_gate, m_w_ffn_up, m_w_ffn_down)
    big_v = (v_w_in, v_w_branch_sb, v_w_branch_fox, v_w_out, v_w_ffn_gate, v_w_ffn_up, v_w_ffn_down)

    lay = _WInLayout(w_in.shape[2], n_f, d_sb, d_fox, d)
    send1, recv1, lands, token = _gather_start([_w_in_shift(w_in, lay)] + [w[0].astype(BF16) for w in big[1:]])
    b_pad = jnp.pad(b_forget, ((0, 0), (0, LANES - n_f)))

    u, u_t = _pre_norm(xs, norm_mix_pre, dep=token)
    weights = dict(zip(("w_in", "w_branch_sb", "w_branch_fox", "w_out", "w_ffn_gate", "w_ffn_up", "w_ffn_down"),
                       zip(big, big_m, big_v)))
    decayed = {nm: _update_prep("decay_" + nm, *weights[nm]) for nm in ("w_in", "w_ffn_gate", "w_ffn_up")}
    l_in, send2, recv2, token = _gather_forward("gather_in_forward", lands[0:1], 0, send1, recv1,
                                                [u] + [t[2] for t in decayed.values()])
    (g_in,) = _gather_wait("gather_in_wait", l_in, 0, recv1, send2, recv2, token)
    w_cat = _w_in_build(g_in, lay)
    qkv = _mm_plain("proj_qkv", "nn", u, w_cat, BF16, n=n_qkv)
    gf = _mm_plain("proj_gates", "nn", u, w_cat, F32, n_off=n_qkv, n=n_gf)
    cum_col, cum_row = _forget_fwd(gf, b_pad, f_blk)
    o_sb, o_sb_t, tot = _sb_fwd(qkv, h_sb)
    l_mid, send2, recv2, token = _gather_forward("gather_mid_forward", lands[1:4], 1, send1, recv1, [o_sb])
    o_fx, o_fx_t, o_fx32, lse = _fox_fwd(qkv, cum_col, cum_row, h_fox, h_sb, token)
    g_sb, g_fx, g_out = _gather_wait("gather_mid_wait", l_mid, 1, recv1, send2, recv2, o_fx)
    w_out_full = g_out.reshape(d, d)
    merged, merged_t, a_sb, a_fx = _branch_merge(o_sb, o_fx, g_sb, g_fx, gf, o_fx)
    l_ffn, send2, recv2, token = _gather_forward("gather_ffn_forward", lands[4:7], 4, send1, recv1, [merged])
    mix = _mm_plain("out_proj", "nn", merged, w_out_full, F32, dep=token)
    h1, u2, u2_t = _mid_norms(xs, mix, norm_mix_post, norm_ffn_pre)
    g_gate, g_up, g_down = _gather_wait("gather_ffn_wait", l_ffn, 4, recv1, send2, recv2, u2)
    gate, up, act, act_t = _ffn_up(u2, g_gate, g_up)
    tm, tn = _tile(s, 1024), _tile(d, 1024)
    ff = _matmul("ffn_down", "nn",
                 [(act, pl.BlockSpec((None, tm, fs), lambda i, j, k: (k, i, 0)),
                   g_down, pl.BlockSpec((None, fs, tn), lambda i, j, k: (k, 0, j)))],
                 (s // tm, d // tn, N_DEV), (tm, tn), _sds((s, d), F32), pl.BlockSpec((tm, tn), lambda i, j, k: (i, j)))
    loss_part, dy, dff, dg_ffn_post = _loss_head(h1, ff, target, norm_ffn_post)

    dgate, dup = _ffn_down_bwd(dff, g_down, gate, up)
    dw_down = _matmul("dw_down", "nn",
                      [(act_t, pl.BlockSpec((None, fs, s), lambda j, n, k: (j, 0, 0)),
                        dff, pl.BlockSpec((s, tn), lambda j, n, k: (0, n)))],
                      (N_DEV, d // tn, 1), (fs, tn), _sds((N_DEV, fs, d), BF16),
                      pl.BlockSpec((None, fs, tn), lambda j, n, k: (j, 0, n)))

    def dw_up(name, dact):
        return _matmul(name, "nn",
                       [(u2_t, pl.BlockSpec((tn, s), lambda j, i, k: (i, 0)),
                         dact, pl.BlockSpec((None, s, fs), lambda j, i, k: (j, 0, 0)))],
                       (N_DEV, d // tn, 1), (tn, fs), _sds((N_DEV, d, fs), BF16),
                       pl.BlockSpec((None, tn, fs), lambda j, i, k: (j, i, 0)))

    dw_gate, dw_upw = dw_up("dw_gate", dgate), dw_up("dw_up", dup)
    rs_ffn = _scatter_pairs("ffn", [dw_gate, dw_upw, dw_down])
    a_spec = pl.BlockSpec((None, tm, fs), lambda i, j, k: (k, i, 0))
    b_spec = pl.BlockSpec((None, tn, fs), lambda i, j, k: (k, j, 0))
    du2 = _matmul("du2", "nt", [(dgate, a_spec, g_gate, b_spec), (dup, a_spec, g_up, b_spec)],
                  (s // tm, d // tn, N_DEV), (tm, tn), _sds((s, d), F32), pl.BlockSpec((tm, tn), lambda i, j, k: (i, j)),
                  dep=rs_ffn[4])
    rs_ffn = _scatter_chips("ffn", rs_ffn, du2)
    dh1, dmix, dg_ffn_pre, dg_mix_post = _mid_norms_bwd(dy, du2, h1, mix, norm_ffn_pre, norm_mix_post)

    da_sb, da_fx, dgf = _merge_bwd(dmix, w_out_full, gf, a_sb, a_fx, dep=rs_ffn[4])
    dw_out = _mm_plain("dw_out", "nn", merged_t, dmix, BF16).reshape(N_DEV, cs, d)

    def branch_bwd(tag, da, w_b, o_t, width):
        tb = _tile(width, 1024)
        do = _matmul("do_" + tag, "nt",
                     [(da, pl.BlockSpec((tm, cs), lambda i, j, k: (i, k)),
                       w_b, pl.BlockSpec((None, tb, cs), lambda i, j, k: (k, j, 0)))],
                     (s // tm, width // tb, N_DEV), (tm, tb), _sds((s, width), BF16),
                     pl.BlockSpec((tm, tb), lambda i, j, k: (i, j)))
        dw = _matmul("dw_" + tag, "nn",
                     [(o_t, pl.BlockSpec((width, s), lambda j, i, k: (0, 0)),
                       da, pl.BlockSpec((s, cs), lambda j, i, k: (0, j)))],
                     (N_DEV, 1, 1), (width, cs), _sds((N_DEV, width, cs), BF16),
                     pl.BlockSpec((None, width, cs), lambda j, i, k: (j, 0, 0)))
        return do, dw

    do_sb, dw_sb = branch_bwd("sb", da_sb, g_sb, o_sb_t, d_sb)
    do_fx, dw_fx = branch_bwd("fox", da_fx, g_fx, o_fx_t, d_fox)

    rs_mid = _scatter_pairs("mid", [dw_sb, dw_fx, dw_out])

    dqkv = _sb_bwd(qkv, do_sb, tot, h_sb, rs_mid[4])
    rs_mid = _scatter_chips("mid", rs_mid, dqkv)
    dqkv, dcum = _fox_bwd(dqkv, qkv, do_fx, o_fx32, lse, cum_col, cum_row, h_fox, h_sb, rs_mid[4])
    dgf, db_part = _forget_bwd(dgf, dcum, gf, b_pad, f_blk)
    dw_in = _w_in_grad_parts(_mm_plain("dw_qkv", "nn", u_t, dqkv, BF16), _mm_plain("dw_gates", "nn", u_t, dgf, BF16), lay)
    rs_in = _scatter_pairs("in", [dw_in])
    du = _mm_plain("du_qkv", "nt", dqkv, w_cat, F32, tn=1024, dep=rs_in[4])
    rs_in = _scatter_chips("in", rs_in, du)
    du = _mm_plain("du_gates", "nt", dgf, w_cat, F32, tn=1024, k_off=n_qkv, init=du, dep=rs_in[4])
    dx, dg_mix_pre = _pre_norm_bwd(dh1, du, xs, norm_mix_pre)

    upd = {}

    def update_group(tag, rs, names, after):
        parts = _scatter_end(tag, rs, after)
        for nm, p in zip(names, parts):
            w, m, v = decayed.get(nm, weights[nm])
            upd[nm] = _update("update_" + nm, p, w, m, v, layout=lay if nm == "w_in" else None, decayed=nm in decayed)

    update_group("ffn", rs_ffn, ("w_ffn_gate", "w_ffn_up", "w_ffn_down"), [dx])
    update_group("mid", rs_mid, ("w_branch_sb", "w_branch_fox", "w_out"), [upd[nm][3] for nm in ("w_ffn_gate", "w_ffn_up", "w_ffn_down")])
    update_group("in", rs_in, ("w_in",), [upd[nm][3] for nm in ("w_branch_sb", "w_branch_fox", "w_out")])

    small = ((norm_mix_pre, m_norm_mix_pre, v_norm_mix_pre), (norm_mix_post, m_norm_mix_post, v_norm_mix_post),
             (norm_ffn_pre, m_norm_ffn_pre, v_norm_ffn_pre), (norm_ffn_post, m_norm_ffn_post, v_norm_ffn_post))
    pad_f = ((0, 0), (0, LANES - n_f))
    cat = lambda i: jnp.concatenate([t[i] for t in small] + [jnp.pad((b_forget, m_b_forget, v_b_forget)[i], pad_f)], axis=1)
    sm = _small_update(jnp.concatenate([dg_mix_pre, dg_mix_post, dg_ffn_pre, dg_ffn_post, db_part], axis=1),
                       cat(0), cat(1), cat(2))
    for i, nm in enumerate(("norm_mix_pre", "norm_mix_post", "norm_ffn_pre", "norm_ffn_post")):
        upd[nm] = [o[:, i * d:(i + 1) * d] for o in sm]
    upd["b_forget"] = [o[:, 4 * d:4 * d + n_f] for o in sm]

    loss = lax.psum(loss_part[0, 0], ("x", "y", "c"))
    order = ("norm_mix_pre", "norm_mix_post", "w_in", "b_forget", "w_branch_sb", "w_branch_fox", "w_out",
             "norm_ffn_pre", "norm_ffn_post", "w_ffn_gate", "w_ffn_up", "w_ffn_down")
    return (loss, dx[None]) + tuple(upd[nm][i] for i in range(4) for nm in order)
```

```python
import jax
import jax.numpy as jnp
from jax import lax
from jax.experimental import pallas as pl
from jax.experimental.pallas import tpu as pltpu

F32 = jnp.float32
BF16 = jnp.bfloat16
MESH = pl.DeviceIdType.MESH
ANY = pl.BlockSpec(memory_space=pl.ANY)
HBM = pl.BlockSpec(memory_space=pltpu.HBM)
SEM = pl.BlockSpec(memory_space=pltpu.SEMAPHORE)
EFFECT = pltpu.SideEffectType.DATAFLOW_SIDE_EFFECTING

N_DEV = 8
HEAD_DIM = 128
RMS_EPS = 1e-6
F_PAD = 512
LANES = 128
ATT_TQ = 256
ATT_TK = 256
ATT_HP = 4
NEG_BIG = -1e30
VMEM_LIMIT = 56 * 1024 * 1024

ADAM_LR = 0.001
ADAM_B1 = 0.9
ADAM_B2 = 0.999
ADAM_EPS = 1e-08
ADAM_WD = 0.01
ADAM_STEP = 10

_DIMS = {"nn": ((1,), (0,)), "nt": ((1,), (1,)), "tn": ((0,), (0,))}


def _params(sem):
    return pltpu.CompilerParams(dimension_semantics=sem, vmem_limit_bytes=VMEM_LIMIT)


def _dot(a, b, mode="nn"):
    return lax.dot_general(a.astype(BF16), b.astype(BF16), (_DIMS[mode], ((), ())), preferred_element_type=F32)


def _tile(n, pref):
    if n <= pref:
        return n
    t = (pref // LANES) * LANES
    while n % t:
        t -= LANES
    return t


def _split2(v):
    hi = v.astype(BF16)
    return hi, (v - hi.astype(F32)).astype(BF16)


def _split3(v):
    a = v.astype(BF16)
    r = v - a.astype(F32)
    b = r.astype(BF16)
    return a, b, (r - b.astype(F32)).astype(BF16)


def _tri(n, cmp):
    r = lax.broadcasted_iota(jnp.int32, (n, n), 0)
    c = lax.broadcasted_iota(jnp.int32, (n, n), 1)
    return jnp.where(cmp(r, c), 1.0, 0.0).astype(BF16)


def _lane_pick(v, h):
    lane = lax.broadcasted_iota(jnp.int32, v.shape, 1)
    return jnp.sum(jnp.where(lane == h, v, 0.0), axis=1, keepdims=True)


def _lane_put(ref, rows, h, col):
    old = ref[rows, :]
    lane = lax.broadcasted_iota(jnp.int32, old.shape, 1)
    ref[rows, :] = jnp.where(lane == h, col, old)


def _sigmoid(z):
    return 1.0 / (1.0 + jnp.exp(-z))


def _log_sigmoid(z):
    return jnp.minimum(z, 0.0) - jnp.log(1.0 + jnp.exp(-jnp.abs(z)))


def _sds(shape, dtype):
    return jax.ShapeDtypeStruct(shape, dtype)


def _matmul(name, mode, pairs, grid, acc_shape, out_shape, out_specs, extras=(), epilogue=None, init=None, dep=None):
    n_p, n_e = len(pairs), len(extras)
    nk = grid[-1]
    single = not isinstance(out_shape, (list, tuple))
    n_i = 0 if init is None else 1
    n_d = 0 if dep is None else 1

    one_step = nk == 1 and init is None

    def body(*refs):
        ab = refs[:2 * n_p]
        ex = refs[2 * n_p:2 * n_p + n_e]
        ini = refs[2 * n_p + n_e:2 * n_p + n_e + n_i]
        outs = refs[2 * n_p + n_e + n_i + n_d:len(refs) - (0 if one_step else 1)]

        def finish(total):
            if epilogue is None:
                outs[0][...] = total.astype(outs[0].dtype)
            else:
                epilogue(total, ex, outs)

        t = _dot(ab[0][...], ab[1][...], mode)
        for p in range(1, n_p):
            t = t + _dot(ab[2 * p][...], ab[2 * p + 1][...], mode)
        if one_step:
            finish(t)
            return
        acc = refs[-1]
        k = pl.program_id(len(grid) - 1)

        @pl.when(k == 0)
        def _():
            acc[...] = t if init is None else ini[0][...].astype(F32) + t

        @pl.when(k > 0)
        def _():
            acc[...] += t

        @pl.when(k == nk - 1)
        def _():
            finish(acc[...])

    in_specs = [s for (_, sa, _, sb) in pairs for s in (sa, sb)] + [s for (_, s) in extras]
    args = [v for (a, _, b, _) in pairs for v in (a, b)] + [e for (e, _) in extras]
    if init is not None:
        in_specs.append(init[1])
        args.append(init[0])
    if dep is not None:
        in_specs.append(ANY)
        args.append(dep)
    return pl.pallas_call(
        body, name=name, grid=grid, in_specs=in_specs,
        out_specs=out_specs if single else list(out_specs),
        out_shape=out_shape if single else list(out_shape),
        scratch_shapes=[] if one_step else [pltpu.VMEM(acc_shape, F32)],
        compiler_params=_params(("parallel",) * (len(grid) - 1) + ("arbitrary",)),
    )(*args)


def _mm_plain(name, mode, a, b, out_dtype, *, n_off=0, n=None, k_off=0, tm=1024, tn=1536, tk=2048, init=None, dep=None):
    if mode == "nn":
        (m, kk), nn_ = a.shape, b.shape[1]
    elif mode == "nt":
        (m, kk), nn_ = a.shape, b.shape[0]
    else:
        (kk, m), nn_ = a.shape, b.shape[1]
    n = nn_ if n is None else n
    tm, tn, tk = _tile(m, tm), _tile(n, tn), _tile(kk, tk)
    while n_off % tn or n % tn:
        tn -= LANES
    while k_off % tk or kk % tk:
        tk -= LANES
    off, koff = n_off // tn, k_off // tk
    a_spec = {"nn": pl.BlockSpec((tm, tk), lambda i, j, k: (i, k)),
              "nt": pl.BlockSpec((tm, tk), lambda i, j, k: (i, k)),
              "tn": pl.BlockSpec((tk, tm), lambda i, j, k: (k, i))}[mode]
    b_spec = {"nn": pl.BlockSpec((tk, tn), lambda i, j, k: (k, j + off)),
              "nt": pl.BlockSpec((tn, tk), lambda i, j, k: (j, k + koff)),
              "tn": pl.BlockSpec((tk, tn), lambda i, j, k: (k, j))}[mode]
    o_spec = pl.BlockSpec((tm, tn), lambda i, j, k: (i, j))
    if init is not None:
        init = (init, o_spec)
    return _matmul(name, mode, [(a, a_spec, b, b_spec)], (m // tm, n // tn, kk // tk), (tm, tn),
                   _sds((m, n), out_dtype), o_spec, init=init, dep=dep)


def _rows_call(name, body, ins, outs, s, tr=256, dep=None):
    def spec(v, per_row):
        if per_row == "transposed":
            return pl.BlockSpec((v.shape[0], tr), lambda i: (0, i))
        if per_row:
            return pl.BlockSpec((tr, v.shape[1]), lambda i: (i, 0))
        return pl.BlockSpec(v.shape, lambda i: (0, 0))
    n_in = len(ins)
    deps = [] if dep is None else [dep]

    def with_dep(*refs):
        body(*refs[:n_in], *refs[n_in + len(deps):])

    return pl.pallas_call(
        with_dep, name=name, grid=(s // tr,),
        in_specs=[spec(v, p) for v, p in ins] + [ANY] * len(deps), out_specs=[spec(v, p) for v, p in outs],
        out_shape=[_sds(v.shape, v.dtype) for v, _ in outs],
        compiler_params=_params(("arbitrary",)),
    )(*[v for v, _ in ins], *deps)


def _rsq(v):
    return lax.rsqrt(jnp.mean(v * v, axis=-1, keepdims=True) + RMS_EPS)


def _norm_bwd(dy, v, r, g):
    vh = v * r
    t = dy * g
    dv = r * (t - vh * jnp.mean(t * vh, axis=-1, keepdims=True))
    return dv, jnp.sum(dy * vh, axis=0, keepdims=True)


def _accum(ref, val):
    @pl.when(pl.program_id(0) == 0)
    def _():
        ref[...] = jnp.zeros_like(ref)
    ref[...] += val


def _pre_norm(x, g, dep=None):
    def body(x_ref, g_ref, u_ref, ut_ref):
        v = x_ref[...]
        u = (v * _rsq(v) * g_ref[...]).astype(BF16)
        u_ref[...] = u
        ut_ref[...] = u.T
    s, d = x.shape
    return _rows_call("pre_norm", body, [(x, True), (g, False)],
                      [(_sds((s, d), BF16), True), (_sds((d, s), BF16), "transposed")], s, dep=dep)


def _mid_norms(x, mix, g_post, g_pre):
    def body(x_ref, mix_ref, gp_ref, gn_ref, h_ref, u_ref, ut_ref):
        mv = mix_ref[...]
        h = x_ref[...] + mv * _rsq(mv) * gp_ref[...]
        h_ref[...] = h
        u = (h * _rsq(h) * gn_ref[...]).astype(BF16)
        u_ref[...] = u
        ut_ref[...] = u.T
    s, d = x.shape
    return _rows_call("mid_norms", body, [(x, True), (mix, True), (g_post, False), (g_pre, False)],
                      [(_sds((s, d), F32), True), (_sds((s, d), BF16), True), (_sds((d, s), BF16), "transposed")], s)


def _loss_head(h1, ff, target, g):
    s, d = h1.shape

    def body(h_ref, ff_ref, t_ref, g_ref, loss_ref, dy_ref, dff_ref, dg_ref):
        fv = ff_ref[...]
        r = _rsq(fv)
        err = h_ref[...] + fv * r * g_ref[...] - t_ref[...]
        part = 0.5 * jnp.sum(jnp.mean(err * err, axis=-1, keepdims=True), axis=0, keepdims=True)
        _accum(loss_ref, jnp.broadcast_to(part, loss_ref.shape))
        dy = err * (1.0 / d)
        dy_ref[...] = dy
        dff, dg = _norm_bwd(dy, fv, r, g_ref[...])
        dff_ref[...] = dff.astype(BF16)
        _accum(dg_ref, dg)

    return _rows_call("loss_head", body, [(h1, True), (ff, True), (target, True), (g, False)],
                      [(_sds((1, LANES), F32), False), (_sds((s, d), F32), True),
                       (_sds((s, d), BF16), True), (_sds((1, d), F32), False)], s)


def _mid_norms_bwd(dy, du2, h1, mix, g_pre, g_post):
    s, d = dy.shape

    def body(dy_ref, du_ref, h_ref, mix_ref, gn_ref, gp_ref, dh_ref, dmix_ref, dgn_ref, dgp_ref):
        h = h_ref[...]
        dh, dgn = _norm_bwd(du_ref[...], h, _rsq(h), gn_ref[...])
        dh = dh + dy_ref[...]
        dh_ref[...] = dh
        _accum(dgn_ref, dgn)
        mv = mix_ref[...]
        dmix, dgp = _norm_bwd(dh, mv, _rsq(mv), gp_ref[...])
        dmix_ref[...] = dmix.astype(BF16)
        _accum(dgp_ref, dgp)

    return _rows_call("mid_norms_bwd", body,
                      [(dy, True), (du2, True), (h1, True), (mix, True), (g_pre, False), (g_post, False)],
                      [(_sds((s, d), F32), True), (_sds((s, d), BF16), True),
                       (_sds((1, d), F32), False), (_sds((1, d), F32), False)], s)


def _pre_norm_bwd(dh1, du, x, g, dep=None):
    s, d = x.shape

    def body(dh_ref, du_ref, x_ref, g_ref, dx_ref, dg_ref):
        v = x_ref[...]
        dv, dg = _norm_bwd(du_ref[...], v, _rsq(v), g_ref[...])
        dx_ref[...] = dh_ref[...] + dv
        _accum(dg_ref, dg)

    return _rows_call("pre_norm_bwd", body, [(dh1, True), (du, True), (x, True), (g, False)],
                      [(_sds((s, d), F32), True), (_sds((1, d), F32), False)], s, dep=dep)


def _forget_fwd(gf, b_pad, f_blk):
    s = gf.shape[0]
    tb = ATT_TK
    nb = s // tb

    def body(f_ref, b_ref, col_ref, row_ref):
        incl = _tri(tb, lambda r, c: c <= r)
        carry = jnp.zeros((1, LANES), F32)
        for i in range(nb):
            lf = _log_sigmoid(f_ref[pl.ds(i * tb, tb), :] + b_ref[...])
            parts = _split3(lf)
            cum = carry + _dot(incl, parts[0]) + _dot(incl, parts[1]) + _dot(incl, parts[2])
            col_ref[pl.ds(i * tb, tb), :] = cum
            row_ref[i] = cum.T
            carry = carry + jnp.sum(lf, axis=0, keepdims=True)

    return pl.pallas_call(
        body, name="forget_fwd", grid=(1,),
        in_specs=[pl.BlockSpec((s, LANES), lambda i: (0, f_blk)), pl.BlockSpec((1, LANES), lambda i: (0, 0))],
        out_specs=[pl.BlockSpec((s, LANES), lambda i: (0, 0)), pl.BlockSpec((nb, LANES, tb), lambda i: (0, 0, 0))],
        out_shape=[_sds((s, LANES), F32), _sds((nb, LANES, tb), F32)],
        compiler_params=_params(("arbitrary",)),
    )(gf, b_pad)


def _forget_bwd(dgf, dcum, gf, b_pad, f_blk):
    s = gf.shape[0]
    tb = ATT_TK
    nb = s // tb
    sec = dgf.shape[1] // F_PAD - 1

    def body(dgf_hbm, dc_ref, f_ref, b_ref, out_ref, db_ref):
        del dgf_hbm
        incl = _tri(tb, lambda r, c: c >= r)
        carry = jnp.zeros((1, LANES), F32)
        db = jnp.zeros((1, LANES), F32)
        out_ref[...] = jnp.zeros_like(out_ref)
        for i in reversed(range(nb)):
            dc = dc_ref[pl.ds(i * tb, tb), :]
            parts = _split3(dc)
            dlf = carry + _dot(incl, parts[0]) + _dot(incl, parts[1]) + _dot(incl, parts[2])
            z = f_ref[pl.ds(i * tb, tb), :] + b_ref[...]
            df = dlf * _sigmoid(-z)
            out_ref[pl.ds(i * tb, tb), pl.ds(0, LANES)] = df.astype(BF16)
            db = db + jnp.sum(df, axis=0, keepdims=True)
            carry = carry + jnp.sum(dc, axis=0, keepdims=True)
        db_ref[...] = db

    return pl.pallas_call(
        body, name="forget_bwd", grid=(1,),
        in_specs=[ANY, pl.BlockSpec((s, LANES), lambda i: (0, 0)),
                  pl.BlockSpec((s, LANES), lambda i: (0, f_blk)), pl.BlockSpec((1, LANES), lambda i: (0, 0))],
        out_specs=[pl.BlockSpec((s, F_PAD), lambda i: (0, sec)), pl.BlockSpec((1, LANES), lambda i: (0, 0))],
        out_shape=[_sds(dgf.shape, BF16), _sds((1, LANES), F32)],
        input_output_aliases={0: 0},
        compiler_params=_params(("arbitrary",)),
    )(dgf, dcum, gf, b_pad)


def _diag_mask(strict):
    r = lax.broadcasted_iota(jnp.int32, (ATT_TQ, ATT_TK), 0)
    c = lax.broadcasted_iota(jnp.int32, (ATT_TQ, ATT_TK), 1)
    return c < r if strict else c <= r


def _qkv_specs(hb0, s):
    specs = []
    for j in range(ATT_HP):
        def col(g, j=j):
            return 3 * (hb0 + ATT_HP * g + j)
        specs += [pl.BlockSpec((ATT_TQ, HEAD_DIM), lambda g, i, col=col: (i, col(g))),
                  pl.BlockSpec((s, HEAD_DIM), lambda g, i, col=col: (0, col(g) + 1)),
                  pl.BlockSpec((s, HEAD_DIM), lambda g, i, col=col: (0, col(g) + 2))]
    return specs


def _head_cols(j):
    return pl.ds(j * HEAD_DIM, HEAD_DIM)


def _sb_fwd(qkv, n_heads):
    s = qkv.shape[0]
    scale = HEAD_DIM ** -0.5
    tq, tk = ATT_TQ, ATT_TK
    heads = range(ATT_HP)

    def body(*refs):
        qkv_refs, (o_ref, ot_ref, tot_ref) = refs[:3 * ATT_HP], refs[3 * ATT_HP:]
        g, i = pl.program_id(0), pl.program_id(1)

        @pl.when((g == 0) & (i == 0))
        def _():
            tot_ref[...] = jnp.zeros_like(tot_ref)

        qs = [qkv_refs[3 * j][...] for j in heads]
        upper = _tri(tk, lambda r, c: r > c)

        def tile(kj, carry, mask):
            rows = pl.ds(pl.multiple_of(kj * tk, tk), tk)
            z = [_dot(qs[j], qkv_refs[3 * j + 1][rows, :], "nt") * scale for j in heads]
            lsz = [_log_sigmoid(z[j]) for j in heads]
            lk = [lsz[j] - z[j] if mask is None else jnp.where(mask, lsz[j] - z[j], 0.0) for j in heads]
            parts = [_split2(lk[j]) for j in heads]
            above = [carry[j][0] + _dot(parts[j][0], upper) + _dot(parts[j][1], upper) for j in heads]
            w = [jnp.exp(lsz[j] + above[j]) for j in heads]
            if mask is not None:
                w = [jnp.where(mask, w[j], 0.0) for j in heads]
            return tuple((carry[j][0] + jnp.sum(lk[j], axis=1, keepdims=True),
                          carry[j][1] + _dot(w[j], qkv_refs[3 * j + 2][rows, :])) for j in heads)

        carry = tile(i, tuple((jnp.zeros((tq, 1), F32), jnp.zeros((tq, HEAD_DIM), F32)) for _ in heads), _diag_mask(True))
        carry = lax.fori_loop(0, i, lambda n, cr: tile(i - 1 - n, cr, None), carry)
        q_rows = pl.ds(pl.multiple_of(i * tq, tq), tq)
        for j in heads:
            c, acc = carry[j]
            o = acc.astype(BF16)
            o_ref[:, _head_cols(j)] = o
            ot_ref[_head_cols(j), :] = o.T
            _lane_put(tot_ref, q_rows, ATT_HP * g + j, c)

    wide = ATT_HP * HEAD_DIM
    return pl.pallas_call(
        body, name="sb_fwd", grid=(n_heads // ATT_HP, s // tq),
        in_specs=_qkv_specs(0, s),
        out_specs=[pl.BlockSpec((tq, wide), lambda g, i: (i, g)), pl.BlockSpec((wide, tq), lambda g, i: (g, i)),
                   pl.BlockSpec((s, LANES), lambda g, i: (0, 0))],
        out_shape=[_sds((s, n_heads * HEAD_DIM), BF16), _sds((n_heads * HEAD_DIM, s), BF16), _sds((s, LANES), F32)],
        compiler_params=_params(("arbitrary", "arbitrary")),
    )(*[qkv] * (3 * ATT_HP))


def _sb_bwd(qkv, do, tot, n_heads, dep):
    s = qkv.shape[0]
    scale = HEAD_DIM ** -0.5
    tq, tk = ATT_TQ, ATT_TK
    nq = s // tq
    hd = HEAD_DIM

    heads = range(ATT_HP)

    def body(*refs):
        qkv_refs = refs[:3 * ATT_HP]
        do_ref, tot_ref, _, out_ref, dk_acc, dv_acc = refs[3 * ATT_HP:]
        g, i = pl.program_id(0), pl.program_id(1)

        @pl.when(i == 0)
        def _():
            dk_acc[...] = jnp.zeros_like(dk_acc)
            dv_acc[...] = jnp.zeros_like(dv_acc)

        qs = [qkv_refs[3 * j][...] for j in heads]
        douts = [do_ref[:, _head_cols(j)] for j in heads]
        totals = [_lane_pick(tot_ref[...], ATT_HP * g + j) for j in heads]
        incl = _tri(tk, lambda r, c: r <= c)
        excl = _tri(tk, lambda r, c: r < c)

        def tile(kj, carry, mask):
            rows = pl.ds(pl.multiple_of(kj * tk, tk), tk)
            k_t = [qkv_refs[3 * j + 1][rows, :] for j in heads]
            z = [_dot(qs[j], k_t[j], "nt") * scale for j in heads]
            dw = [_dot(douts[j], qkv_refs[3 * j + 2][rows, :], "nt") for j in heads]
            lsz = [_log_sigmoid(z[j]) for j in heads]
            lk = [lsz[j] - z[j] if mask is None else jnp.where(mask, lsz[j] - z[j], 0.0) for j in heads]
            parts = [_split2(lk[j]) for j in heads]
            below = [carry[j][0] + _dot(parts[j][0], incl) + _dot(parts[j][1], incl) for j in heads]
            w = [jnp.exp(lsz[j] + (totals[j] - below[j])) for j in heads]
            if mask is not None:
                w = [jnp.where(mask, w[j], 0.0) for j in heads]
            e = [dw[j] * w[j] for j in heads]
            parts = [_split2(e[j]) for j in heads]
            e_before = [carry[j][1] + _dot(parts[j][0], excl) + _dot(parts[j][1], excl) for j in heads]
            sg = [jnp.exp(lsz[j]) for j in heads]
            dz = [e[j] * (1.0 - sg[j]) - e_before[j] * sg[j] for j in heads]
            if mask is not None:
                dz = [jnp.where(mask, dz[j], 0.0) for j in heads]
            dz = [(dz[j] * scale).astype(BF16) for j in heads]
            for j in heads:
                dk_acc[j, rows, :] += _dot(dz[j], qs[j], "tn")
                dv_acc[j, rows, :] += _dot(w[j], douts[j], "tn")
            return tuple((carry[j][0] + jnp.sum(lk[j], axis=1, keepdims=True),
                          carry[j][1] + jnp.sum(e[j], axis=1, keepdims=True),
                          carry[j][2] + _dot(dz[j], k_t[j])) for j in heads)

        zero = jnp.zeros((tq, 1), F32)
        carry = lax.fori_loop(0, i, lambda kj, cr: tile(kj, cr, None),
                              tuple((zero, zero, jnp.zeros((tq, hd), F32)) for _ in heads))
        carry = tile(i, carry, _diag_mask(True))
        for j in heads:
            out_ref[pl.ds(pl.multiple_of(i * tq, tq), tq), pl.ds(3 * j * hd, hd)] = carry[j][2].astype(BF16)

        @pl.when(i == nq - 1)
        def _():
            for j in heads:
                out_ref[:, pl.ds((3 * j + 1) * hd, hd)] = dk_acc[j].astype(BF16)
                out_ref[:, pl.ds((3 * j + 2) * hd, hd)] = dv_acc[j].astype(BF16)

    wide = ATT_HP * hd
    return pl.pallas_call(
        body, name="sb_bwd", grid=(n_heads // ATT_HP, nq),
        in_specs=_qkv_specs(0, s) + [pl.BlockSpec((tq, wide), lambda g, i: (i, g)),
                                     pl.BlockSpec((tq, LANES), lambda g, i: (i, 0)), ANY],
        out_specs=pl.BlockSpec((s, 3 * wide), lambda g, i: (0, g)),
        out_shape=_sds(qkv.shape, BF16),
        scratch_shapes=[pltpu.VMEM((ATT_HP, s, hd), F32), pltpu.VMEM((ATT_HP, s, hd), F32)],
        compiler_params=_params(("arbitrary", "arbitrary")),
    )(*[qkv] * (3 * ATT_HP), do, tot, dep)


def _fox_fwd(qkv, cum_col, cum_row, n_heads, hb0, dep):
    s = qkv.shape[0]
    scale = HEAD_DIM ** -0.5
    tq, tk = ATT_TQ, ATT_TK

    heads = range(ATT_HP)

    def body(*refs):
        qkv_refs = refs[:3 * ATT_HP]
        cc_ref, cr_ref, _, o_ref, ot_ref, o32_ref, lse_ref = refs[3 * ATT_HP:]
        g, i = pl.program_id(0), pl.program_id(1)

        @pl.when((g == 0) & (i == 0))
        def _():
            lse_ref[...] = jnp.zeros_like(lse_ref)

        qs = [qkv_refs[3 * j][...] for j in heads]
        cqs = [_lane_pick(cc_ref[...], ATT_HP * g + j) for j in heads]

        def tile(kj, carry, mask):
            rows = pl.ds(pl.multiple_of(kj * tk, tk), tk)
            sc = [_dot(qs[j], qkv_refs[3 * j + 1][rows, :], "nt") * scale + cqs[j]
                  - cr_ref[kj, pl.ds(ATT_HP * g + j, 1), :] for j in heads]
            if mask is not None:
                sc = [jnp.where(mask, sc[j], NEG_BIG) for j in heads]
            m_new = [jnp.maximum(carry[j][0], jnp.max(sc[j], axis=1, keepdims=True)) for j in heads]
            p = [jnp.exp(sc[j] - m_new[j]) for j in heads]
            alpha = [jnp.exp(carry[j][0] - m_new[j]) for j in heads]
            parts = [_split2(p[j]) for j in heads]
            v_t = [qkv_refs[3 * j + 2][rows, :] for j in heads]
            pv = [_dot(parts[j][0], v_t[j]) + _dot(parts[j][1], v_t[j]) for j in heads]
            return tuple((m_new[j], alpha[j] * carry[j][1] + jnp.sum(p[j], axis=1, keepdims=True),
                          alpha[j] * carry[j][2] + pv[j]) for j in heads)

        carry = tuple((jnp.full((tq, 1), NEG_BIG, F32), jnp.zeros((tq, 1), F32), jnp.zeros((tq, HEAD_DIM), F32))
                      for _ in heads)
        carry = lax.fori_loop(0, i, lambda kj, cr: tile(kj, cr, None), carry)
        carry = tile(i, carry, _diag_mask(False))
        q_rows = pl.ds(pl.multiple_of(i * tq, tq), tq)
        for j in heads:
            m, l, acc = carry[j]
            o = acc / l
            o_ref[:, _head_cols(j)] = o.astype(BF16)
            ot_ref[_head_cols(j), :] = o.astype(BF16).T
            o32_ref[:, _head_cols(j)] = o
            _lane_put(lse_ref, q_rows, ATT_HP * g + j, m + jnp.log(l))

    nb = cum_row.shape[0]
    wide = ATT_HP * HEAD_DIM
    return pl.pallas_call(
        body, name="fox_fwd", grid=(n_heads // ATT_HP, s // tq),
        in_specs=_qkv_specs(hb0, s) + [pl.BlockSpec((tq, LANES), lambda g, i: (i, 0)),
                                       pl.BlockSpec((nb, 8, tk), lambda g, i: (0, 0, 0)), ANY],
        out_specs=[pl.BlockSpec((tq, wide), lambda g, i: (i, g)), pl.BlockSpec((wide, tq), lambda g, i: (g, i)),
                   pl.BlockSpec((tq, wide), lambda g, i: (i, g)), pl.BlockSpec((s, LANES), lambda g, i: (0, 0))],
        out_shape=[_sds((s, n_heads * HEAD_DIM), BF16), _sds((n_heads * HEAD_DIM, s), BF16),
                   _sds((s, n_heads * HEAD_DIM), F32), _sds((s, LANES), F32)],
        compiler_params=_params(("arbitrary", "arbitrary")),
    )(*[qkv] * (3 * ATT_HP), cum_col, cum_row, dep)


def _fox_bwd(dqkv, qkv, do, o, lse, cum_col, cum_row, n_heads, hb0, dep):
    s = qkv.shape[0]
    scale = HEAD_DIM ** -0.5
    tq, tk = ATT_TQ, ATT_TK
    nq = s // tq
    hd = HEAD_DIM

    heads = range(ATT_HP)
    assert hb0 % ATT_HP == 0

    def body(*refs):
        qkv_refs = refs[1:1 + 3 * ATT_HP]
        do_ref, o_ref, lse_ref, cc_ref, cr_ref, _, out_ref, dc_ref, dk_acc, dv_acc, col_acc = refs[1 + 3 * ATT_HP:]
        g, i = pl.program_id(0), pl.program_id(1)

        @pl.when((g == 0) & (i == 0))
        def _():
            dc_ref[...] = jnp.zeros_like(dc_ref)

        @pl.when(i == 0)
        def _():
            dk_acc[...] = jnp.zeros_like(dk_acc)
            dv_acc[...] = jnp.zeros_like(dv_acc)
            col_acc[...] = jnp.zeros_like(col_acc)

        qs = [qkv_refs[3 * j][...] for j in heads]
        douts = [do_ref[:, _head_cols(j)] for j in heads]
        deltas = [jnp.sum(douts[j].astype(F32) * o_ref[:, _head_cols(j)], axis=1, keepdims=True) for j in heads]
        shifts = [_lane_pick(cc_ref[...], ATT_HP * g + j) - _lane_pick(lse_ref[...], ATT_HP * g + j) for j in heads]

        def tile(kj, carry, mask):
            rows = pl.ds(pl.multiple_of(kj * tk, tk), tk)
            k_t = [qkv_refs[3 * j + 1][rows, :] for j in heads]
            sc = [_dot(qs[j], k_t[j], "nt") * scale + shifts[j] - cr_ref[kj, pl.ds(ATT_HP * g + j, 1), :] for j in heads]
            dp = [_dot(douts[j], qkv_refs[3 * j + 2][rows, :], "nt") for j in heads]
            p = [jnp.exp(sc[j]) for j in heads]
            if mask is not None:
                p = [jnp.where(mask, p[j], 0.0) for j in heads]
            ds_f = [p[j] * (dp[j] - deltas[j]) for j in heads]
            ds = [(ds_f[j] * scale).astype(BF16) for j in heads]
            for j in heads:
                col_acc[j, kj] += jnp.broadcast_to(jnp.sum(ds_f[j], axis=0, keepdims=True), (8, tk))
                dk_acc[j, rows, :] += _dot(ds[j], qs[j], "tn")
                dv_acc[j, rows, :] += _dot(p[j], douts[j], "tn")
            return tuple((carry[j][0] + _dot(ds[j], k_t[j]), carry[j][1] + jnp.sum(ds_f[j], axis=1, keepdims=True))
                         for j in heads)

        carry = lax.fori_loop(0, i, lambda kj, cr: tile(kj, cr, None),
                              tuple((jnp.zeros((tq, hd), F32), jnp.zeros((tq, 1), F32)) for _ in heads))
        carry = tile(i, carry, _diag_mask(False))
        q_rows = pl.ds(pl.multiple_of(i * tq, tq), tq)
        for j in heads:
            out_ref[q_rows, pl.ds(3 * j * hd, hd)] = carry[j][0].astype(BF16)
            _lane_put(dc_ref, q_rows, ATT_HP * g + j, carry[j][1])

        @pl.when(i == nq - 1)
        def _():
            lane = lax.broadcasted_iota(jnp.int32, (tk, LANES), 1)
            for j in heads:
                out_ref[:, pl.ds((3 * j + 1) * hd, hd)] = dk_acc[j].astype(BF16)
                out_ref[:, pl.ds((3 * j + 2) * hd, hd)] = dv_acc[j].astype(BF16)
                for kj in range(nb):
                    col = jnp.broadcast_to(col_acc[j, kj][0:1, :], (LANES, tk)).T
                    old = dc_ref[pl.ds(kj * tk, tk), :]
                    dc_ref[pl.ds(kj * tk, tk), :] = jnp.where(lane == ATT_HP * g + j, old - col, old)

    nb = cum_row.shape[0]
    wide = ATT_HP * hd
    return pl.pallas_call(
        body, name="fox_bwd", grid=(n_heads // ATT_HP, nq),
        in_specs=[ANY] + _qkv_specs(hb0, s) + [
            pl.BlockSpec((tq, wide), lambda g, i: (i, g)), pl.BlockSpec((tq, wide), lambda g, i: (i, g)),
            pl.BlockSpec((tq, LANES), lambda g, i: (i, 0)), pl.BlockSpec((tq, LANES), lambda g, i: (i, 0)),
            pl.BlockSpec((nb, 8, tk), lambda g, i: (0, 0, 0)), ANY],
        out_specs=[pl.BlockSpec((s, 3 * wide), lambda g, i: (0, hb0 // ATT_HP + g)),
                   pl.BlockSpec((s, LANES), lambda g, i: (0, 0))],
        out_shape=[_sds(dqkv.shape, BF16), _sds((s, LANES), F32)],
        scratch_shapes=[pltpu.VMEM((ATT_HP, s, hd), F32), pltpu.VMEM((ATT_HP, s, hd), F32),
                        pltpu.VMEM((ATT_HP, s // tk, 8, tk), F32)],
        input_output_aliases={0: 0},
        compiler_params=_params(("arbitrary", "arbitrary")),
    )(dqkv, *[qkv] * (3 * ATT_HP), do, o, lse, cum_col, cum_row, dep)


def _branch_merge(o_sb, o_fx, w_sb, w_fx, gf, dep, tm=1024):
    s = o_sb.shape[0]
    cs = w_sb.shape[2]
    tm = _tile(s, tm)

    def body(osb_ref, ofx_ref, wsb_ref, wfx_ref, g_ref, dep_ref, merged_ref, mt_ref, asb_ref, afx_ref):
        del dep_ref
        a_sb = _dot(osb_ref[...], wsb_ref[...])
        a_fx = _dot(ofx_ref[...], wfx_ref[...])
        g = g_ref[...]
        merged = (_sigmoid(g[:, :cs]) * a_sb + _sigmoid(g[:, cs:]) * a_fx).astype(BF16)
        merged_ref[...] = merged
        mt_ref[...] = merged.T
        asb_ref[...] = a_sb.astype(BF16)
        afx_ref[...] = a_fx.astype(BF16)

    blk = pl.BlockSpec((tm, cs), lambda i, j: (i, j))
    out = _sds((s, N_DEV * cs), BF16)
    return pl.pallas_call(
        body, name="branch_merge", grid=(s // tm, N_DEV),
        in_specs=[pl.BlockSpec((tm, o_sb.shape[1]), lambda i, j: (i, 0)),
                  pl.BlockSpec((tm, o_fx.shape[1]), lambda i, j: (i, 0)),
                  pl.BlockSpec((None,) + w_sb.shape[1:], lambda i, j: (j, 0, 0)),
                  pl.BlockSpec((None,) + w_fx.shape[1:], lambda i, j: (j, 0, 0)),
                  pl.BlockSpec((tm, 2 * cs), lambda i, j: (i, j)), ANY],
        out_specs=[blk, pl.BlockSpec((cs, tm), lambda i, j: (j, i)), blk, blk],
        out_shape=[out, _sds((N_DEV * cs, s), BF16), out, out],
        compiler_params=_params(("parallel", "arbitrary")),
    )(o_sb, o_fx, w_sb, w_fx, gf, dep)


def _merge_bwd(dmix, w_out, gf, a_sb, a_fx, tm=1024, tk=2048, dep=None):
    s, d = dmix.shape
    cs = d // N_DEV
    tm, tk = _tile(s, tm), _tile(d, tk)

    def epilogue(acc, ex, outs):
        g, a_sb, a_fx = ex[0][...], ex[1][...].astype(F32), ex[2][...].astype(F32)
        s_sb, s_fx = _sigmoid(g[:, :cs]), _sigmoid(g[:, cs:])
        outs[0][...] = (acc * s_sb).astype(BF16)
        outs[1][...] = (acc * s_fx).astype(BF16)
        outs[2][...] = jnp.concatenate([acc * a_sb * s_sb * (1.0 - s_sb), acc * a_fx * s_fx * (1.0 - s_fx)],
                                       axis=1).astype(BF16)

    blk = pl.BlockSpec((tm, cs), lambda i, j, k: (i, j))
    wide = pl.BlockSpec((tm, 2 * cs), lambda i, j, k: (i, j))
    return _matmul(
        "merge_bwd", "nt",
        [(dmix, pl.BlockSpec((tm, tk), lambda i, j, k: (i, k)), w_out, pl.BlockSpec((cs, tk), lambda i, j, k: (j, k)))],
        (s // tm, N_DEV, d // tk), (tm, cs),
        [_sds((s, d), BF16), _sds((s, d), BF16), _sds(gf.shape, BF16)], [blk, blk, wide],
        extras=[(gf, wide), (a_sb, blk), (a_fx, blk)], epilogue=epilogue, dep=dep)


def _ffn_up(u2, w_gate, w_up, dep, tm=1024):
    s, d = u2.shape
    fs = w_gate.shape[2]
    tm = _tile(s, tm)

    def body(u_ref, wg_ref, wu_ref, dep_ref, gate_ref, up_ref, act_ref, actt_ref):
        del dep_ref
        u = u_ref[...]
        gate = _dot(u, wg_ref[...])
        up = _dot(u, wu_ref[...])
        gate_ref[...] = gate
        up_ref[...] = up
        act = (gate * _sigmoid(gate) * up).astype(BF16)
        act_ref[...] = act
        actt_ref[...] = act.T

    w_spec = pl.BlockSpec((None, d, fs), lambda i, j: (j, 0, 0))
    o_spec = pl.BlockSpec((None, tm, fs), lambda i, j: (j, i, 0))
    return pl.pallas_call(
        body, name="ffn_up", grid=(s // tm, N_DEV),
        in_specs=[pl.BlockSpec((tm, d), lambda i, j: (i, 0)), w_spec, w_spec, ANY],
        out_specs=[o_spec, o_spec, o_spec, pl.BlockSpec((None, fs, tm), lambda i, j: (j, 0, i))],
        out_shape=[_sds((N_DEV, s, fs), F32), _sds((N_DEV, s, fs), F32), _sds((N_DEV, s, fs), BF16),
                   _sds((N_DEV, fs, s), BF16)],
        compiler_params=_params(("parallel", "arbitrary")),
    )(u2, w_gate, w_up, dep)


def _ffn_down_bwd(dff, w_down, gate, up, tm=1024):
    s, d = dff.shape
    fs = w_down.shape[1]
    tm = _tile(s, tm)

    def body(dff_ref, wd_ref, gate_ref, up_ref, dgate_ref, dup_ref):
        dact = _dot(dff_ref[...], wd_ref[...], "nt")
        gate = gate_ref[...]
        sg = _sigmoid(gate)
        dup_ref[...] = (dact * gate * sg).astype(BF16)
        dgate_ref[...] = (dact * up_ref[...] * sg * (1.0 + gate * (1.0 - sg))).astype(BF16)

    a_spec = pl.BlockSpec((None, tm, fs), lambda i, j: (j, i, 0))
    return pl.pallas_call(
        body, name="ffn_down_bwd", grid=(s // tm, N_DEV),
        in_specs=[pl.BlockSpec((tm, d), lambda i, j: (i, 0)), pl.BlockSpec((None, fs, d), lambda i, j: (j, 0, 0)),
                  a_spec, a_spec],
        out_specs=[a_spec, a_spec],
        out_shape=[_sds((N_DEV, s, fs), BF16), _sds((N_DEV, s, fs), BF16)],
        compiler_params=_params(("parallel", "arbitrary")),
    )(dff, w_down, gate, up)


def _mesh_place():
    x, y, c = lax.axis_index("x"), lax.axis_index("y"), lax.axis_index("c")
    peers = []
    for d in range(1, N_DEV):
        px = 1 - x if d & 4 else x
        py = 1 - y if d & 2 else y
        pc = 1 - c if d & 1 else c
        peers.append((d, (px, py, pc), 4 * px + 2 * py + pc))
    return 4 * x + 2 * y + c, peers


def _flat_me():
    return 4 * lax.axis_index("x") + 2 * lax.axis_index("y") + lax.axis_index("c")


def _in_hbm(a):
    return pltpu.with_memory_space_constraint(a, pltpu.HBM)


def _pair_plan():
    x, y, c = lax.axis_index("x"), lax.axis_index("y"), lax.axis_index("c")
    return [(2 * q + (1 - c), q, q, (x, y, 1 - c)) for q in range(4)]


def _chip_plan():
    x, y, c = lax.axis_index("x"), lax.axis_index("y"), lax.axis_index("c")
    plan = []
    for fx, fy in ((1, 0), (0, 1), (1, 1)):
        cx, cy = (1 - x if fx else x), (1 - y if fy else y)
        plan.append((2 * cx + cy, 2 * x + y, 2 * cx + cy, (cx, cy, c)))
    return plan


def _split_start(name, srcs, lands, plan, k):
    n = len(srcs)

    def body(*refs):
        ins, lnd = refs[:n], refs[n:2 * n]
        send, recv, token = refs[2 * n], refs[2 * n + 1], refs[-1]
        copies = plan()
        for a in range(n):
            for t, (src, dst, _, dev) in enumerate(copies):
                pltpu.make_async_remote_copy(src_ref=ins[a].at[src], dst_ref=lnd[a].at[dst], send_sem=send.at[k * a + t],
                                             recv_sem=recv.at[k * a + t], device_id=dev, device_id_type=MESH).start()
        token[...] = jnp.zeros_like(token)

    res = pl.pallas_call(
        body, name=name,
        out_shape=[pltpu.SemaphoreType.DMA((n * k,)), pltpu.SemaphoreType.DMA((n * k,))]
        + [pltpu.HBM(a.shape, a.dtype) for a in list(srcs) + list(lands)] + [_sds((8, LANES), F32)],
        in_specs=[HBM] * (2 * n), out_specs=[SEM, SEM] + [HBM] * (2 * n) + [pl.BlockSpec(memory_space=pltpu.VMEM)],
        input_output_aliases={i: 2 + i for i in range(2 * n)},
        compiler_params=pltpu.CompilerParams(has_side_effects=EFFECT),
    )(*[_in_hbm(a) for a in srcs], *[_in_hbm(a) for a in lands])
    return res[0], res[1], res[2:2 + n], res[2 + n:2 + 2 * n], res[-1]


def _split_wait(name, send, recv, srcs, lands, plan, k, after):
    n = len(srcs)

    def body(*refs):
        ins, lnd = refs[:n], refs[n:2 * n]
        send_sem, recv_sem = refs[2 * n], refs[2 * n + 1]
        copies = plan()
        for a in range(n):
            for t, (src, _, dst, dev) in enumerate(copies):
                cp = pltpu.make_async_remote_copy(src_ref=ins[a].at[src], dst_ref=lnd[a].at[dst], send_sem=send_sem.at[k * a + t],
                                                  recv_sem=recv_sem.at[k * a + t], device_id=dev, device_id_type=MESH)
                cp.wait_send()
                cp.wait_recv()

    res = pl.pallas_call(
        body, name=name,
        out_shape=[pltpu.HBM(a.shape, a.dtype) for a in list(srcs) + list(lands)],
        in_specs=[HBM] * (2 * n) + [SEM, SEM] + [ANY] * len(after), out_specs=[HBM] * (2 * n),
        input_output_aliases={i: i for i in range(2 * n)},
        compiler_params=pltpu.CompilerParams(has_side_effects=EFFECT),
    )(*srcs, *lands, send, recv, *after)
    return res[:n], res[n:]


def _pair_add(name, parts, land):
    _, r, cols = parts.shape
    tr = max(16, min(r, ((1 << 20) // (2 * cols)) // 16 * 16))
    while r % tr:
        tr -= 16

    def body(c_ref, p_ref, l_ref, o_ref):
        del c_ref
        o_ref[...] = (p_ref[...].astype(F32) + l_ref[...].astype(F32)).astype(BF16)

    blk = pl.BlockSpec((None, tr, cols), lambda q, i, c_ref: (q, i, 0))
    return pl.pallas_call(
        body, name=name,
        grid_spec=pltpu.PrefetchScalarGridSpec(
            num_scalar_prefetch=1, grid=(4, r // tr),
            in_specs=[pl.BlockSpec((None, tr, cols), lambda q, i, c_ref: (2 * q + c_ref[0], i, 0)), blk], out_specs=blk),
        out_shape=_sds((4, r, cols), BF16),
        compiler_params=_params(("parallel", "parallel")),
    )(jnp.reshape(lax.axis_index("c"), (1,)).astype(jnp.int32), parts, land)


def _scatter_pairs(tag, parts):
    lands = [lax.empty((4,) + a.shape[1:], a.dtype) for a in parts]
    return _split_start("pair_" + tag, parts, lands, _pair_plan, 4)


def _scatter_chips(tag, started, after):
    send, recv, parts, lands, _ = started
    parts, lands = _split_wait("pair_" + tag + "_wait", send, recv, parts, lands, _pair_plan, 4, [after])
    sums = [_pair_add("pair_" + tag + "_add%d" % a, p, l) for a, (p, l) in enumerate(zip(parts, lands))]
    chip = 2 * lax.axis_index("x") + lax.axis_index("y")
    final = [lax.dynamic_update_slice_in_dim(lax.empty(v.shape, v.dtype), lax.dynamic_slice_in_dim(v, chip, 1, 0), chip, 0)
             for v in sums]
    return _split_start("chips_" + tag, sums, final, _chip_plan, 3)


def _scatter_end(tag, started, after):
    send, recv, sums, final, _ = started
    return _split_wait("chips_" + tag + "_wait", send, recv, sums, final, _chip_plan, 3, after)[1]


def _gather_targets():
    x, y, c = lax.axis_index("x"), lax.axis_index("y"), lax.axis_index("c")
    chips = [(x, y), (1 - x, y), (x, 1 - y), (1 - x, 1 - y)]
    same = [((cx, cy, c), 4 * cx + 2 * cy + c) for cx, cy in chips]
    other = [((cx, cy, 1 - c), 4 * cx + 2 * cy + 1 - c) for cx, cy in chips]
    return same[0][1], [other[0]] + same[1:], [flat for _, flat in other[1:]], other[0][0]


def _gather_start(shards):
    n = len(shards)
    me = _flat_me()
    lands = [lax.dynamic_update_slice_in_dim(lax.empty((N_DEV,) + a.shape, a.dtype), a[None], me, 0) for a in shards]

    def body(*refs):
        lnd, send, recv, token = refs[:n], refs[n], refs[n + 1], refs[-1]
        mine, targets, _, _ = _gather_targets()
        for a in range(n):
            for t, (dev, _) in enumerate(targets):
                pltpu.make_async_remote_copy(src_ref=lnd[a].at[mine], dst_ref=lnd[a].at[mine], send_sem=send.at[4 * a + t],
                                             recv_sem=recv.at[4 * a + t], device_id=dev, device_id_type=MESH).start()
        token[...] = jnp.zeros_like(token)

    res = pl.pallas_call(
        body, name="gather_start",
        out_shape=[pltpu.SemaphoreType.DMA((4 * n,)), pltpu.SemaphoreType.DMA((4 * n,))]
        + [pltpu.HBM(a.shape, a.dtype) for a in lands] + [_sds((8, LANES), F32)],
        in_specs=[HBM] * n, out_specs=[SEM, SEM] + [HBM] * n + [pl.BlockSpec(memory_space=pltpu.VMEM)],
        input_output_aliases={i: 2 + i for i in range(n)},
        compiler_params=pltpu.CompilerParams(has_side_effects=EFFECT),
    )(*[_in_hbm(a) for a in lands])
    return res[0], res[1], list(res[2:2 + n]), res[-1]


def _gather_forward(name, lands, first, send, recv, after):
    n = len(lands)

    def body(*refs):
        lnd, send_sem, recv_sem = refs[:n], refs[n], refs[n + 1]
        send2, recv2, token = refs[-3], refs[-2], refs[-1]
        mine, targets, _, sibling = _gather_targets()
        for a in range(n):
            for t, (dev, flat) in enumerate(targets):
                cp = pltpu.make_async_remote_copy(src_ref=lnd[a].at[mine], dst_ref=lnd[a].at[flat],
                                                  send_sem=send_sem.at[4 * (first + a) + t],
                                                  recv_sem=recv_sem.at[4 * (first + a) + t], device_id=dev, device_id_type=MESH)
                cp.wait_send()
                if t:
                    cp.wait_recv()
                    pltpu.make_async_remote_copy(src_ref=lnd[a].at[flat], dst_ref=lnd[a].at[flat], send_sem=send2.at[3 * a + t - 1],
                                                 recv_sem=recv2.at[3 * a + t - 1], device_id=sibling, device_id_type=MESH).start()
        token[...] = jnp.zeros_like(token)

    res = pl.pallas_call(
        body, name=name,
        out_shape=[pltpu.HBM(a.shape, a.dtype) for a in lands]
        + [pltpu.SemaphoreType.DMA((3 * n,)), pltpu.SemaphoreType.DMA((3 * n,)), _sds((8, LANES), F32)],
        in_specs=[HBM] * n + [SEM, SEM] + [ANY] * len(after),
        out_specs=[HBM] * n + [SEM, SEM, pl.BlockSpec(memory_space=pltpu.VMEM)],
        input_output_aliases={i: i for i in range(n)},
        compiler_params=pltpu.CompilerParams(has_side_effects=EFFECT),
    )(*lands, send, recv, *after)
    return list(res[:n]), res[n], res[n + 1], res[-1]


def _gather_wait(name, lands, first, recv, send2, recv2, after):
    n = len(lands)

    def body(*refs):
        lnd, recv_sem, send2_sem, recv2_sem = refs[:n], refs[n], refs[n + 1], refs[n + 2]
        mine, targets, passed, sibling = _gather_targets()
        for a in range(n):
            dev, flat = targets[0]
            pltpu.make_async_remote_copy(src_ref=lnd[a].at[mine], dst_ref=lnd[a].at[flat], send_sem=send2_sem.at[3 * a],
                                         recv_sem=recv_sem.at[4 * (first + a)], device_id=dev, device_id_type=MESH).wait_recv()
            for t in range(3):
                cp = pltpu.make_async_remote_copy(src_ref=lnd[a].at[targets[t + 1][1]], dst_ref=lnd[a].at[passed[t]],
                                                  send_sem=send2_sem.at[3 * a + t], recv_sem=recv2_sem.at[3 * a + t],
                                                  device_id=sibling, device_id_type=MESH)
                cp.wait_send()
                cp.wait_recv()

    res = pl.pallas_call(
        body, name=name, out_shape=[pltpu.HBM(a.shape, a.dtype) for a in lands],
        in_specs=[HBM] * n + [SEM, SEM, SEM, ANY], out_specs=[HBM] * n,
        input_output_aliases={i: i for i in range(n)},
        compiler_params=pltpu.CompilerParams(has_side_effects=EFFECT),
    )(*lands, recv, send2, recv2, after)
    return list(res)


def _adamw_decay(w, m, v):
    return ADAM_WD * w, ADAM_B1 * m, ADAM_B2 * v


def _adamw_finish(g, wd_w, m1, v1):
    m = m1 + (1.0 - ADAM_B1) * g
    v = v1 + (1.0 - ADAM_B2) * (g * g)
    m_hat = m / (1.0 - ADAM_B1 ** ADAM_STEP)
    v_hat = v / (1.0 - ADAM_B2 ** ADAM_STEP)
    delta = -ADAM_LR * (m_hat / (jnp.sqrt(v_hat) + ADAM_EPS) + wd_w)
    return delta, m, v


def _adamw(g, w, m, v):
    return _adamw_finish(g, *_adamw_decay(w, m, v))


def _update_prep(name, w, m, v, block_bytes=1 << 20):
    _, r, c = w.shape
    tr = max(8, min(r, (block_bytes // (4 * c)) // 8 * 8))
    while r % tr:
        tr -= 8

    def body(w_ref, m_ref, v_ref, ow_ref, om_ref, ov_ref):
        ow_ref[...], om_ref[...], ov_ref[...] = _adamw_decay(w_ref[...], m_ref[...], v_ref[...])

    blk = pl.BlockSpec((None, tr, c), lambda i: (0, i, 0))
    return pl.pallas_call(
        body, name=name, grid=(r // tr,), in_specs=[blk] * 3, out_specs=[blk] * 3, out_shape=[_sds((1, r, c), F32)] * 3,
        compiler_params=_params(("parallel",)),
    )(w, m, v)


def _update(name, parts, w, m, v, layout=None, decayed=False, transposed_out=False, block_bytes=1 << 20):
    _, r, c = w.shape
    n_slots, _, cp = parts.shape
    tr = max(8, min(r, (block_bytes // (4 * cp)) // 8 * 8))
    if transposed_out:
        tr = _tile(r, 256)
    while r % tr:
        tr -= 8

    def body(p_ref, w_ref, m_ref, v_ref, g_ref, d_ref, nm_ref, nv_ref, *scratch):
        g = p_ref[0].astype(F32)
        for p in range(1, n_slots):
            g = g + p_ref[p].astype(F32)
        if layout is not None:
            s1, s2, lg = layout.my_shifts()
            lane = lax.broadcasted_iota(jnp.int32, g.shape, 1)
            scratch[0][...] = jnp.where(lane < lg, pltpu.roll(g, cp - s1, 1), pltpu.roll(g, cp - s2, 1))
            g = scratch[0][:, 0:c]
        step = _adamw_finish if decayed else _adamw
        results = (g,) + step(g, w_ref[...], m_ref[...], v_ref[...])
        for ref, val in zip((g_ref, d_ref, nm_ref, nv_ref), results):
            ref[...] = val.T if transposed_out else val

    blk = pl.BlockSpec((None, tr, c), lambda i: (0, i, 0))
    out_blk = pl.BlockSpec((None, c, tr), lambda i: (0, 0, i)) if transposed_out else blk
    res = pl.pallas_call(
        body, name=name, grid=(r // tr,),
        in_specs=[pl.BlockSpec((n_slots, tr, cp), lambda i: (0, i, 0)), blk, blk, blk],
        out_specs=[out_blk] * 4, out_shape=[_sds((1, c, r) if transposed_out else (1, r, c), F32)] * 4,
        scratch_shapes=[] if layout is None else [pltpu.VMEM((tr, cp), F32)],
        compiler_params=_params(("parallel",)),
    )(parts, w, m, v)
    return [jnp.transpose(o, (0, 2, 1)) for o in res] if transposed_out else res


def _small_update(part, w, m, v):
    n = part.shape[1]

    def body(p_ref, w_ref, m_ref, v_ref, g_ref, d_ref, nm_ref, nv_ref, buf, send, recv):
        me, peers = _mesh_place()
        buf[me] = p_ref[...]
        sent = []
        for d, dev, flat in peers:
            cp = pltpu.make_async_remote_copy(src_ref=p_ref, dst_ref=buf.at[me], send_sem=send.at[d],
                                              recv_sem=recv.at[d], device_id=dev, device_id_type=MESH)
            cp.start()
            sent.append(cp)
        for d, dev, flat in peers:
            pltpu.make_async_remote_copy(src_ref=p_ref, dst_ref=buf.at[flat], send_sem=send.at[d],
                                         recv_sem=recv.at[d], device_id=dev, device_id_type=MESH).wait_recv()
        for cp in sent:
            cp.wait_send()
        g = buf[0]
        for p in range(1, N_DEV):
            g = g + buf[p]
        g_ref[...] = g
        d_ref[...], nm_ref[...], nv_ref[...] = _adamw(g, w_ref[...], m_ref[...], v_ref[...])

    vm = pl.BlockSpec(memory_space=pltpu.VMEM)
    return pl.pallas_call(
        body, name="small_update", in_specs=[vm] * 4, out_specs=[vm] * 4, out_shape=[_sds((1, n), F32)] * 4,
        scratch_shapes=[pltpu.VMEM((N_DEV, 1, n), F32), pltpu.SemaphoreType.DMA((N_DEV,)),
                        pltpu.SemaphoreType.DMA((N_DEV,))],
    )(part, w, m, v)


class _WInLayout:
    def __init__(self, n8, n_f, d_sb, d_fox, d):
        assert n8 % LANES == 1 and n_f < LANES and d % (N_DEV * LANES) == 0
        self.n8, self.n_f, self.d = n8, n_f, d
        self.sp = n8 // LANES
        self.wp = (n8 + 2 * LANES - 2) // LANES * LANES
        self.n_qkv = 3 * (d_sb + d_fox)
        nq, dt, tc = self.n_qkv // LANES, d // LANES, d // N_DEV // LANES
        h_sb, h_fox = d_sb // HEAD_DIM, d_fox // HEAD_DIM
        self.sources = {}
        self.part_tile = {}
        for p in range(N_DEV):
            lg = min(max(self.n_qkv + n_f - n8 * p, 0), n8)
            s1, s2 = p, p + LANES - n_f
            spans = []
            if lg > 0:
                spans.append(("a", self.sp * p, s1 // LANES, (lg + s1 - 1) // LANES))
            if lg < n8:
                spans.append(("g", self.sp * p - 1 - nq, (lg + s2) // LANES, (n8 - 1 + s2) // LANES))
            for kind, base, first, last in spans:
                for i in range(first, last + 1):
                    assert (p, i) not in self.part_tile
                    self.part_tile[(p, i)] = (kind, base + i)
                    self.sources.setdefault((kind, base + i), []).append((p, i))
        self.cat_tiles = [("a", r * h_sb + h) for h in range(h_sb) for r in range(3)]
        self.cat_tiles += [("a", 3 * h_sb + r * h_fox + h) for h in range(h_fox) for r in range(3)]
        self.cat_tiles += [("g", which * dt + j * tc + half) for j in range(N_DEV) for which in (0, 1) for half in range(tc)]
        self.cat_tiles += [("a", nq)] + [None] * (F_PAD // LANES - 1)
        self.cat_index = {key: c for c, key in enumerate(self.cat_tiles) if key is not None}

    def my_shifts(self):
        me = _flat_me()
        return me, me + LANES - self.n_f, jnp.clip(self.n_qkv + self.n_f - self.n8 * me, 0, self.n8)


def _lane_tile(i):
    return pl.ds(i * LANES, LANES)


def _w_in_shift(w_in, lay, tr=256):
    _, d, n8 = w_in.shape

    def body(w_ref, o_ref, buf):
        buf[...] = jnp.zeros_like(buf)
        buf[:, 0:n8] = w_ref[...]
        v = buf[...]
        s1, s2, lg = lay.my_shifts()
        pos = lax.broadcasted_iota(jnp.int32, v.shape, 1)
        o_ref[...] = jnp.where(pos < lg + s1, pltpu.roll(v, s1, 1),
                               jnp.where(pos >= lg + s2, pltpu.roll(v, s2, 1), 0.0)).astype(BF16)

    return pl.pallas_call(
        body, name="w_in_shift", grid=(d // tr,),
        in_specs=[pl.BlockSpec((None, tr, n8), lambda i: (0, i, 0))],
        out_specs=pl.BlockSpec((tr, lay.wp), lambda i: (i, 0)), out_shape=_sds((d, lay.wp), BF16),
        scratch_shapes=[pltpu.VMEM((tr, lay.wp), F32)],
        compiler_params=_params(("parallel",)),
    )(w_in)


def _w_in_build(g_in, lay, tr=256):
    d = g_in.shape[1]
    width = len(lay.cat_tiles) * LANES

    def body(g_ref, o_ref):
        for c, key in enumerate(lay.cat_tiles):
            if key is None:
                o_ref[:, _lane_tile(c)] = jnp.zeros((tr, LANES), BF16)
                continue
            (p, i), *more = lay.sources[key]
            val = g_ref[p, :, _lane_tile(i)]
            for p2, i2 in more:
                val = val + g_ref[p2, :, _lane_tile(i2)]
            o_ref[:, _lane_tile(c)] = val

    return pl.pallas_call(
        body, name="w_in_build", grid=(d // tr,),
        in_specs=[pl.BlockSpec((N_DEV, tr, lay.wp), lambda i: (0, i, 0))],
        out_specs=pl.BlockSpec((tr, width), lambda i: (i, 0)), out_shape=_sds((d, width), BF16),
        compiler_params=_params(("parallel",)),
    )(g_in)


def _w_in_grad_parts(dwq, dwgf, lay, tr=256):
    d = dwq.shape[0]
    nq = lay.n_qkv // LANES

    def body(q_ref, g_ref, o_ref):
        for p in range(N_DEV):
            for i in range(lay.wp // LANES):
                key = lay.part_tile.get((p, i))
                if key is None:
                    o_ref[p, :, _lane_tile(i)] = jnp.zeros((tr, LANES), BF16)
                    continue
                c = lay.cat_index[key]
                o_ref[p, :, _lane_tile(i)] = q_ref[:, _lane_tile(c)] if c < nq else g_ref[:, _lane_tile(c - nq)]

    return pl.pallas_call(
        body, name="w_in_grad_parts", grid=(d // tr,),
        in_specs=[pl.BlockSpec((tr, dwq.shape[1]), lambda i: (i, 0)), pl.BlockSpec((tr, dwgf.shape[1]), lambda i: (i, 0))],
        out_specs=pl.BlockSpec((N_DEV, tr, lay.wp), lambda i: (0, i, 0)), out_shape=_sds((N_DEV, d, lay.wp), BF16),
        compiler_params=_params(("parallel",)),
    )(dwq, dwgf)


def kernel(x, norm_mix_pre, norm_mix_post, w_in, b_forget, w_branch_sb, w_branch_fox, w_out, norm_ffn_pre, norm_ffn_post, w_ffn_gate, w_ffn_up, w_ffn_down, loss_target, m_norm_mix_pre, m_norm_mix_post, m_w_in, m_b_forget, m_w_branch_sb, m_w_branch_fox, m_w_out, m_norm_ffn_pre, m_norm_ffn_post, m_w_ffn_gate, m_w_ffn_up, m_w_ffn_down, v_norm_mix_pre, v_norm_mix_post, v_w_in, v_b_forget, v_w_branch_sb, v_w_branch_fox, v_w_out, v_norm_ffn_pre, v_norm_ffn_post, v_w_ffn_gate, v_w_ffn_up, v_w_ffn_down):
    xs, target = x[0], loss_target[0]
    s, d = xs.shape
    d_sb, d_fox = w_branch_sb.shape[1], w_branch_fox.shape[1]
    h_sb, h_fox = d_sb // HEAD_DIM, d_fox // HEAD_DIM
    n_f = b_forget.shape[1]
    fs = w_ffn_gate.shape[2]
    cs = d // N_DEV
    n_qkv = 3 * (d_sb + d_fox)
    n_gf = 2 * d + F_PAD
    f_blk = 2 * d // LANES
    big = (w_in, w_branch_sb, w_branch_fox, w_out, w_ffn_gate, w_ffn_up, w_ffn_down)
    big_m = (m_w_in, m_w_branch_sb, m_w_branch_fox, m_w_out, m_w_ffn_gate, m_w_ffn_up, m_w_ffn_down)
    big_v = (v_w_in, v_w_branch_sb, v_w_branch_fox, v_w_out, v_w_ffn_gate, v_w_ffn_up, v_w_ffn_down)

    lay = _WInLayout(w_in.shape[2], n_f, d_sb, d_fox, d)
    send1, recv1, lands, token = _gather_start([_w_in_shift(w_in, lay)] + [w[0].astype(BF16) for w in big[1:]])
    b_pad = jnp.pad(b_forget, ((0, 0), (0, LANES - n_f)))

    u, u_t = _pre_norm(xs, norm_mix_pre, dep=token)
    weights = dict(zip(("w_in", "w_branch_sb", "w_branch_fox", "w_out", "w_ffn_gate", "w_ffn_up", "w_ffn_down"),
                       zip(big, big_m, big_v)))
    decayed = {nm: _update_prep("decay_" + nm, *weights[nm]) for nm in ("w_in", "w_ffn_gate", "w_ffn_up")}
    l_in, send2, recv2, token = _gather_forward("gather_in_forward", lands[0:1], 0, send1, recv1,
                                                [u] + [t[2] for t in decayed.values()])
    (g_in,) = _gather_wait("gather_in_wait", l_in, 0, recv1, send2, recv2, token)
    w_cat = _w_in_build(g_in, lay)
    qkv = _mm_plain("proj_qkv", "nn", u, w_cat, BF16, n=n_qkv)
    gf = _mm_plain("proj_gates", "nn", u, w_cat, F32, n_off=n_qkv, n=n_gf)
    cum_col, cum_row = _forget_fwd(gf, b_pad, f_blk)
    o_sb, o_sb_t, tot = _sb_fwd(qkv, h_sb)
    l_mid, send2, recv2, token = _gather_forward("gather_mid_forward", lands[1:4], 1, send1, recv1, [o_sb])
    o_fx, o_fx_t, o_fx32, lse = _fox_fwd(qkv, cum_col, cum_row, h_fox, h_sb, token)
    g_sb, g_fx, g_out = _gather_wait("gather_mid_wait", l_mid, 1, recv1, send2, recv2, o_fx)
    w_out_full = g_out.reshape(d, d)
    merged, merged_t, a_sb, a_fx = _branch_merge(o_sb, o_fx, g_sb, g_fx, gf, o_fx)
    l_ffn, send2, recv2, token = _gather_forward("gather_ffn_forward", lands[4:6], 4, send1, recv1, [merged])
    mix = _mm_plain("out_proj", "nn", merged, w_out_full, F32, dep=token)
    h1, u2, u2_t = _mid_norms(xs, mix, norm_mix_post, norm_ffn_pre)
    g_gate, g_up = _gather_wait("gather_ffn_wait", l_ffn, 4, recv1, send2, recv2, u2)
    l_down, send2, recv2, token = _gather_forward("gather_down_forward", lands[6:7], 6, send1, recv1, [u2])
    gate, up, act, act_t = _ffn_up(u2, g_gate, g_up, token)
    (g_down,) = _gather_wait("gather_down_wait", l_down, 6, recv1, send2, recv2, act)
    tm, tn = _tile(s, 1024), _tile(d, 1024)
    ff = _matmul("ffn_down", "nn",
                 [(act, pl.BlockSpec((None, tm, fs), lambda i, j, k: (k, i, 0)),
                   g_down, pl.BlockSpec((None, fs, tn), lambda i, j, k: (k, 0, j)))],
                 (s // tm, d // tn, N_DEV), (tm, tn), _sds((s, d), F32), pl.BlockSpec((tm, tn), lambda i, j, k: (i, j)))
    loss_part, dy, dff, dg_ffn_post = _loss_head(h1, ff, target, norm_ffn_post)

    dgate, dup = _ffn_down_bwd(dff, g_down, gate, up)
    dw_down = _matmul("dw_down", "nn",
                      [(act_t, pl.BlockSpec((None, fs, s), lambda j, n, k: (j, 0, 0)),
                        dff, pl.BlockSpec((s, tn), lambda j, n, k: (0, n)))],
                      (N_DEV, d // tn, 1), (fs, tn), _sds((N_DEV, fs, d), BF16),
                      pl.BlockSpec((None, fs, tn), lambda j, n, k: (j, 0, n)))

    def dw_up(name, dact):
        return _matmul(name, "nn",
                       [(u2_t, pl.BlockSpec((tn, s), lambda j, i, k: (i, 0)),
                         dact, pl.BlockSpec((None, s, fs), lambda j, i, k: (j, 0, 0)))],
                       (N_DEV, d // tn, 1), (tn, fs), _sds((N_DEV, d, fs), BF16),
                       pl.BlockSpec((None, tn, fs), lambda j, i, k: (j, i, 0)))

    dw_gate, dw_upw = dw_up("dw_gate", dgate), dw_up("dw_up", dup)
    rs_ffn = _scatter_pairs("ffn", [dw_gate, dw_upw, dw_down])
    a_spec = pl.BlockSpec((None, tm, fs), lambda i, j, k: (k, i, 0))
    b_spec = pl.BlockSpec((None, tn, fs), lambda i, j, k: (k, j, 0))
    du2 = _matmul("du2", "nt", [(dgate, a_spec, g_gate, b_spec), (dup, a_spec, g_up, b_spec)],
                  (s // tm, d // tn, N_DEV), (tm, tn), _sds((s, d), F32), pl.BlockSpec((tm, tn), lambda i, j, k: (i, j)),
                  dep=rs_ffn[4])
    rs_ffn = _scatter_chips("ffn", rs_ffn, du2)
    dh1, dmix, dg_ffn_pre, dg_mix_post = _mid_norms_bwd(dy, du2, h1, mix, norm_ffn_pre, norm_mix_post)

    da_sb, da_fx, dgf = _merge_bwd(dmix, w_out_full, gf, a_sb, a_fx, dep=rs_ffn[4])
    dw_out = _mm_plain("dw_out", "nn", merged_t, dmix, BF16).reshape(N_DEV, cs, d)

    def branch_bwd(tag, da, w_b, o_t, width):
        tb = _tile(width, 1024)
        do = _matmul("do_" + tag, "nt",
                     [(da, pl.BlockSpec((tm, cs), lambda i, j, k: (i, k)),
                       w_b, pl.BlockSpec((None, tb, cs), lambda i, j, k: (k, j, 0)))],
                     (s // tm, width // tb, N_DEV), (tm, tb), _sds((s, width), BF16),
                     pl.BlockSpec((tm, tb), lambda i, j, k: (i, j)))
        dw = _matmul("dw_" + tag, "nn",
                     [(o_t, pl.BlockSpec((width, s), lambda j, i, k: (0, 0)),
                       da, pl.BlockSpec((s, cs), lambda j, i, k: (0, j)))],
                     (N_DEV, 1, 1), (width, cs), _sds((N_DEV, width, cs), BF16),
                     pl.BlockSpec((None, width, cs), lambda j, i, k: (j, 0, 0)))
        return do, dw

    do_sb, dw_sb = branch_bwd("sb", da_sb, g_sb, o_sb_t, d_sb)
    do_fx, dw_fx = branch_bwd("fox", da_fx, g_fx, o_fx_t, d_fox)

    rs_mid = _scatter_pairs("mid", [dw_sb, dw_fx, dw_out])

    dqkv = _sb_bwd(qkv, do_sb, tot, h_sb, rs_mid[4])
    rs_mid = _scatter_chips("mid", rs_mid, dqkv)
    dqkv, dcum = _fox_bwd(dqkv, qkv, do_fx, o_fx32, lse, cum_col, cum_row, h_fox, h_sb, rs_mid[4])
    dgf, db_part = _forget_bwd(dgf, dcum, gf, b_pad, f_blk)
    dw_in = _w_in_grad_parts(_mm_plain("dw_qkv", "nn", u_t, dqkv, BF16), _mm_plain("dw_gates", "nn", u_t, dgf, BF16), lay)
    rs_in = _scatter_pairs("in", [dw_in])
    du = _mm_plain("du_qkv", "nt", dqkv, w_cat, F32, tn=1024, dep=rs_in[4])
    rs_in = _scatter_chips("in", rs_in, du)
    du = _mm_plain("du_gates", "nt", dgf, w_cat, F32, tn=1024, k_off=n_qkv, init=du, dep=rs_in[4])
    dx, dg_mix_pre = _pre_norm_bwd(dh1, du, xs, norm_mix_pre)

    upd = {}

    def update_group(tag, rs, names, after):
        parts = _scatter_end(tag, rs, after)
        for nm, p in zip(names, parts):
            w, m, v = decayed.get(nm, weights[nm])
            upd[nm] = _update("update_" + nm, p, w, m, v, layout=lay if nm == "w_in" else None, decayed=nm in decayed,
                              transposed_out=nm in ("w_ffn_gate", "w_ffn_up"))

    update_group("ffn", rs_ffn, ("w_ffn_gate", "w_ffn_up", "w_ffn_down"), [dx])
    update_group("mid", rs_mid, ("w_branch_sb", "w_branch_fox", "w_out"), [upd[nm][3] for nm in ("w_ffn_gate", "w_ffn_up", "w_ffn_down")])
    update_group("in", rs_in, ("w_in",), [upd[nm][3] for nm in ("w_branch_sb", "w_branch_fox", "w_out")])

    small = ((norm_mix_pre, m_norm_mix_pre, v_norm_mix_pre), (norm_mix_post, m_norm_mix_post, v_norm_mix_post),
             (norm_ffn_pre, m_norm_ffn_pre, v_norm_ffn_pre), (norm_ffn_post, m_norm_ffn_post, v_norm_ffn_post))
    pad_f = ((0, 0), (0, LANES - n_f))
    cat = lambda i: jnp.concatenate([t[i] for t in small] + [jnp.pad((b_forget, m_b_forget, v_b_forget)[i], pad_f)], axis=1)
    sm = _small_update(jnp.concatenate([dg_mix_pre, dg_mix_post, dg_ffn_pre, dg_ffn_post, db_part], axis=1),
                       cat(0), cat(1), cat(2))
    for i, nm in enumerate(("norm_mix_pre", "norm_mix_post", "norm_ffn_pre", "norm_ffn_post")):
        upd[nm] = [o[:, i * d:(i + 1) * d] for o in sm]
    upd["b_forget"] = [o[:, 4 * d:4 * d + n_f] for o in sm]

    loss = lax.psum(loss_part[0, 0], ("x", "y", "c"))
    order = ("norm_mix_pre", "norm_mix_post", "w_in", "b_forget", "w_branch_sb", "w_branch_fox", "w_out",
             "norm_ffn_pre", "norm_ffn_post", "w_ffn_gate", "w_ffn_up", "w_ffn_down")
    return (loss, dx[None]) + tuple(upd[nm][i] for i in range(4) for nm in order)
```

```python
import jax
import jax.numpy as jnp
from jax import lax
from jax.experimental import pallas as pl
from jax.experimental.pallas import tpu as pltpu

F32 = jnp.float32
BF16 = jnp.bfloat16
MESH = pl.DeviceIdType.MESH
ANY = pl.BlockSpec(memory_space=pl.ANY)
HBM = pl.BlockSpec(memory_space=pltpu.HBM)
SEM = pl.BlockSpec(memory_space=pltpu.SEMAPHORE)
EFFECT = pltpu.SideEffectType.DATAFLOW_SIDE_EFFECTING

N_DEV = 8
HEAD_DIM = 128
RMS_EPS = 1e-6
F_PAD = 512
LANES = 128
ATT_TQ = 256
ATT_TK = 256
ATT_HP = 4
NEG_BIG = -1e30
VMEM_LIMIT = 56 * 1024 * 1024

ADAM_LR = 0.001
ADAM_B1 = 0.9
ADAM_B2 = 0.999
ADAM_EPS = 1e-08
ADAM_WD = 0.01
ADAM_STEP = 10

_DIMS = {"nn": ((1,), (0,)), "nt": ((1,), (1,)), "tn": ((0,), (0,))}


def _params(sem):
    return pltpu.CompilerParams(dimension_semantics=sem, vmem_limit_bytes=VMEM_LIMIT)


def _dot(a, b, mode="nn"):
    return lax.dot_general(a.astype(BF16), b.astype(BF16), (_DIMS[mode], ((), ())), preferred_element_type=F32)


def _tile(n, pref):
    if n <= pref:
        return n
    t = (pref // LANES) * LANES
    while n % t:
        t -= LANES
    return t


def _split2(v):
    hi = v.astype(BF16)
    return hi, (v - hi.astype(F32)).astype(BF16)


def _split3(v):
    a = v.astype(BF16)
    r = v - a.astype(F32)
    b = r.astype(BF16)
    return a, b, (r - b.astype(F32)).astype(BF16)


def _tri(n, cmp):
    r = lax.broadcasted_iota(jnp.int32, (n, n), 0)
    c = lax.broadcasted_iota(jnp.int32, (n, n), 1)
    return jnp.where(cmp(r, c), 1.0, 0.0).astype(BF16)


def _lane_pick(v, h):
    lane = lax.broadcasted_iota(jnp.int32, v.shape, 1)
    return jnp.sum(jnp.where(lane == h, v, 0.0), axis=1, keepdims=True)


def _lane_put(ref, rows, h, col):
    old = ref[rows, :]
    lane = lax.broadcasted_iota(jnp.int32, old.shape, 1)
    ref[rows, :] = jnp.where(lane == h, col, old)


def _sigmoid(z):
    return 1.0 / (1.0 + jnp.exp(-z))


def _log_sigmoid(z):
    return jnp.minimum(z, 0.0) - jnp.log(1.0 + jnp.exp(-jnp.abs(z)))


def _sds(shape, dtype):
    return jax.ShapeDtypeStruct(shape, dtype)


def _matmul(name, mode, pairs, grid, acc_shape, out_shape, out_specs, extras=(), epilogue=None, init=None, dep=None):
    n_p, n_e = len(pairs), len(extras)
    nk = grid[-1]
    single = not isinstance(out_shape, (list, tuple))
    n_i = 0 if init is None else 1
    n_d = 0 if dep is None else 1

    one_step = nk == 1 and init is None

    def body(*refs):
        ab = refs[:2 * n_p]
        ex = refs[2 * n_p:2 * n_p + n_e]
        ini = refs[2 * n_p + n_e:2 * n_p + n_e + n_i]
        outs = refs[2 * n_p + n_e + n_i + n_d:len(refs) - (0 if one_step else 1)]

        def finish(total):
            if epilogue is None:
                outs[0][...] = total.astype(outs[0].dtype)
            else:
                epilogue(total, ex, outs)

        t = _dot(ab[0][...], ab[1][...], mode)
        for p in range(1, n_p):
            t = t + _dot(ab[2 * p][...], ab[2 * p + 1][...], mode)
        if one_step:
            finish(t)
            return
        acc = refs[-1]
        k = pl.program_id(len(grid) - 1)

        @pl.when(k == 0)
        def _():
            acc[...] = t if init is None else ini[0][...].astype(F32) + t

        @pl.when(k > 0)
        def _():
            acc[...] += t

        @pl.when(k == nk - 1)
        def _():
            finish(acc[...])

    in_specs = [s for (_, sa, _, sb) in pairs for s in (sa, sb)] + [s for (_, s) in extras]
    args = [v for (a, _, b, _) in pairs for v in (a, b)] + [e for (e, _) in extras]
    if init is not None:
        in_specs.append(init[1])
        args.append(init[0])
    if dep is not None:
        in_specs.append(ANY)
        args.append(dep)
    return pl.pallas_call(
        body, name=name, grid=grid, in_specs=in_specs,
        out_specs=out_specs if single else list(out_specs),
        out_shape=out_shape if single else list(out_shape),
        scratch_shapes=[] if one_step else [pltpu.VMEM(acc_shape, F32)],
        compiler_params=_params(("parallel",) * (len(grid) - 1) + ("arbitrary",)),
    )(*args)


def _mm_plain(name, mode, a, b, out_dtype, *, n_off=0, n=None, k_off=0, tm=1024, tn=1536, tk=2048, init=None, dep=None):
    if mode == "nn":
        (m, kk), nn_ = a.shape, b.shape[1]
    elif mode == "nt":
        (m, kk), nn_ = a.shape, b.shape[0]
    else:
        (kk, m), nn_ = a.shape, b.shape[1]
    n = nn_ if n is None else n
    tm, tn, tk = _tile(m, tm), _tile(n, tn), _tile(kk, tk)
    while n_off % tn or n % tn:
        tn -= LANES
    while k_off % tk or kk % tk:
        tk -= LANES
    off, koff = n_off // tn, k_off // tk
    a_spec = {"nn": pl.BlockSpec((tm, tk), lambda i, j, k: (i, k)),
              "nt": pl.BlockSpec((tm, tk), lambda i, j, k: (i, k)),
              "tn": pl.BlockSpec((tk, tm), lambda i, j, k: (k, i))}[mode]
    b_spec = {"nn": pl.BlockSpec((tk, tn), lambda i, j, k: (k, j + off)),
              "nt": pl.BlockSpec((tn, tk), lambda i, j, k: (j, k + koff)),
              "tn": pl.BlockSpec((tk, tn), lambda i, j, k: (k, j))}[mode]
    o_spec = pl.BlockSpec((tm, tn), lambda i, j, k: (i, j))
    if init is not None:
        init = (init, o_spec)
    return _matmul(name, mode, [(a, a_spec, b, b_spec)], (m // tm, n // tn, kk // tk), (tm, tn),
                   _sds((m, n), out_dtype), o_spec, init=init, dep=dep)


def _rows_call(name, body, ins, outs, s, tr=256, dep=None):
    def spec(v, per_row):
        if per_row == "transposed":
            return pl.BlockSpec((v.shape[0], tr), lambda i: (0, i))
        if per_row:
            return pl.BlockSpec((tr, v.shape[1]), lambda i: (i, 0))
        return pl.BlockSpec(v.shape, lambda i: (0, 0))
    n_in = len(ins)
    deps = [] if dep is None else [dep]

    def with_dep(*refs):
        body(*refs[:n_in], *refs[n_in + len(deps):])

    return pl.pallas_call(
        with_dep, name=name, grid=(s // tr,),
        in_specs=[spec(v, p) for v, p in ins] + [ANY] * len(deps), out_specs=[spec(v, p) for v, p in outs],
        out_shape=[_sds(v.shape, v.dtype) for v, _ in outs],
        compiler_params=_params(("arbitrary",)),
    )(*[v for v, _ in ins], *deps)


def _rsq(v):
    return lax.rsqrt(jnp.mean(v * v, axis=-1, keepdims=True) + RMS_EPS)


def _norm_bwd(dy, v, r, g):
    vh = v * r
    t = dy * g
    dv = r * (t - vh * jnp.mean(t * vh, axis=-1, keepdims=True))
    return dv, jnp.sum(dy * vh, axis=0, keepdims=True)


def _accum(ref, val):
    @pl.when(pl.program_id(0) == 0)
    def _():
        ref[...] = jnp.zeros_like(ref)
    ref[...] += val


def _pre_norm(x, g, dep=None):
    def body(x_ref, g_ref, u_ref, ut_ref):
        v = x_ref[...]
        u = (v * _rsq(v) * g_ref[...]).astype(BF16)
        u_ref[...] = u
        ut_ref[...] = u.T
    s, d = x.shape
    return _rows_call("pre_norm", body, [(x, True), (g, False)],
                      [(_sds((s, d), BF16), True), (_sds((d, s), BF16), "transposed")], s, dep=dep)


def _mid_norms(x, mix, g_post, g_pre):
    def body(x_ref, mix_ref, gp_ref, gn_ref, h_ref, u_ref, ut_ref):
        mv = mix_ref[...]
        h = x_ref[...] + mv * _rsq(mv) * gp_ref[...]
        h_ref[...] = h
        u = (h * _rsq(h) * gn_ref[...]).astype(BF16)
        u_ref[...] = u
        ut_ref[...] = u.T
    s, d = x.shape
    return _rows_call("mid_norms", body, [(x, True), (mix, True), (g_post, False), (g_pre, False)],
                      [(_sds((s, d), F32), True), (_sds((s, d), BF16), True), (_sds((d, s), BF16), "transposed")], s)


def _loss_head(h1, ff, target, g):
    s, d = h1.shape

    def body(h_ref, ff_ref, t_ref, g_ref, loss_ref, dy_ref, dff_ref, dg_ref):
        fv = ff_ref[...]
        r = _rsq(fv)
        err = h_ref[...] + fv * r * g_ref[...] - t_ref[...]
        part = 0.5 * jnp.sum(jnp.mean(err * err, axis=-1, keepdims=True), axis=0, keepdims=True)
        _accum(loss_ref, jnp.broadcast_to(part, loss_ref.shape))
        dy = err * (1.0 / d)
        dy_ref[...] = dy
        dff, dg = _norm_bwd(dy, fv, r, g_ref[...])
        dff_ref[...] = dff.astype(BF16)
        _accum(dg_ref, dg)

    return _rows_call("loss_head", body, [(h1, True), (ff, True), (target, True), (g, False)],
                      [(_sds((1, LANES), F32), False), (_sds((s, d), F32), True),
                       (_sds((s, d), BF16), True), (_sds((1, d), F32), False)], s)


def _mid_norms_bwd(dy, du2, h1, mix, g_pre, g_post):
    s, d = dy.shape

    def body(dy_ref, du_ref, h_ref, mix_ref, gn_ref, gp_ref, dh_ref, dmix_ref, dgn_ref, dgp_ref):
        h = h_ref[...]
        dh, dgn = _norm_bwd(du_ref[...], h, _rsq(h), gn_ref[...])
        dh = dh + dy_ref[...]
        dh_ref[...] = dh
        _accum(dgn_ref, dgn)
        mv = mix_ref[...]
        dmix, dgp = _norm_bwd(dh, mv, _rsq(mv), gp_ref[...])
        dmix_ref[...] = dmix.astype(BF16)
        _accum(dgp_ref, dgp)

    return _rows_call("mid_norms_bwd", body,
                      [(dy, True), (du2, True), (h1, True), (mix, True), (g_pre, False), (g_post, False)],
                      [(_sds((s, d), F32), True), (_sds((s, d), BF16), True),
                       (_sds((1, d), F32), False), (_sds((1, d), F32), False)], s)


def _pre_norm_bwd(dh1, du, x, g, dep=None):
    s, d = x.shape

    def body(dh_ref, du_ref, x_ref, g_ref, dx_ref, dg_ref):
        v = x_ref[...]
        dv, dg = _norm_bwd(du_ref[...], v, _rsq(v), g_ref[...])
        dx_ref[...] = dh_ref[...] + dv
        _accum(dg_ref, dg)

    return _rows_call("pre_norm_bwd", body, [(dh1, True), (du, True), (x, True), (g, False)],
                      [(_sds((s, d), F32), True), (_sds((1, d), F32), False)], s, dep=dep)


def _forget_fwd(gf, b_pad, f_blk):
    s = gf.shape[0]
    tb = ATT_TK
    nb = s // tb

    def body(f_ref, b_ref, col_ref, row_ref):
        incl = _tri(tb, lambda r, c: c <= r)
        carry = jnp.zeros((1, LANES), F32)
        for i in range(nb):
            lf = _log_sigmoid(f_ref[pl.ds(i * tb, tb), :] + b_ref[...])
            parts = _split3(lf)
            cum = carry + _dot(incl, parts[0]) + _dot(incl, parts[1]) + _dot(incl, parts[2])
            col_ref[pl.ds(i * tb, tb), :] = cum
            row_ref[i] = cum.T
            carry = carry + jnp.sum(lf, axis=0, keepdims=True)

    return pl.pallas_call(
        body, name="forget_fwd", grid=(1,),
        in_specs=[pl.BlockSpec((s, LANES), lambda i: (0, f_blk)), pl.BlockSpec((1, LANES), lambda i: (0, 0))],
        out_specs=[pl.BlockSpec((s, LANES), lambda i: (0, 0)), pl.BlockSpec((nb, LANES, tb), lambda i: (0, 0, 0))],
        out_shape=[_sds((s, LANES), F32), _sds((nb, LANES, tb), F32)],
        compiler_params=_params(("arbitrary",)),
    )(gf, b_pad)


def _forget_bwd(dgf, dcum, gf, b_pad, f_blk):
    s = gf.shape[0]
    tb = ATT_TK
    nb = s // tb
    sec = dgf.shape[1] // F_PAD - 1

    def body(dgf_hbm, dc_ref, f_ref, b_ref, out_ref, db_ref):
        del dgf_hbm
        incl = _tri(tb, lambda r, c: c >= r)
        carry = jnp.zeros((1, LANES), F32)
        db = jnp.zeros((1, LANES), F32)
        out_ref[...] = jnp.zeros_like(out_ref)
        for i in reversed(range(nb)):
            dc = dc_ref[pl.ds(i * tb, tb), :]
            parts = _split3(dc)
            dlf = carry + _dot(incl, parts[0]) + _dot(incl, parts[1]) + _dot(incl, parts[2])
            z = f_ref[pl.ds(i * tb, tb), :] + b_ref[...]
            df = dlf * _sigmoid(-z)
            out_ref[pl.ds(i * tb, tb), pl.ds(0, LANES)] = df.astype(BF16)
            db = db + jnp.sum(df, axis=0, keepdims=True)
            carry = carry + jnp.sum(dc, axis=0, keepdims=True)
        db_ref[...] = db

    return pl.pallas_call(
        body, name="forget_bwd", grid=(1,),
        in_specs=[ANY, pl.BlockSpec((s, LANES), lambda i: (0, 0)),
                  pl.BlockSpec((s, LANES), lambda i: (0, f_blk)), pl.BlockSpec((1, LANES), lambda i: (0, 0))],
        out_specs=[pl.BlockSpec((s, F_PAD), lambda i: (0, sec)), pl.BlockSpec((1, LANES), lambda i: (0, 0))],
        out_shape=[_sds(dgf.shape, BF16), _sds((1, LANES), F32)],
        input_output_aliases={0: 0},
        compiler_params=_params(("arbitrary",)),
    )(dgf, dcum, gf, b_pad)


def _diag_mask(strict):
    r = lax.broadcasted_iota(jnp.int32, (ATT_TQ, ATT_TK), 0)
    c = lax.broadcasted_iota(jnp.int32, (ATT_TQ, ATT_TK), 1)
    return c < r if strict else c <= r


def _qkv_specs(hb0, s):
    specs = []
    for j in range(ATT_HP):
        def col(g, j=j):
            return 3 * (hb0 + ATT_HP * g + j)
        specs += [pl.BlockSpec((ATT_TQ, HEAD_DIM), lambda g, i, col=col: (i, col(g))),
                  pl.BlockSpec((s, HEAD_DIM), lambda g, i, col=col: (0, col(g) + 1)),
                  pl.BlockSpec((s, HEAD_DIM), lambda g, i, col=col: (0, col(g) + 2))]
    return specs


def _head_cols(j):
    return pl.ds(j * HEAD_DIM, HEAD_DIM)


def _sb_fwd(qkv, n_heads):
    s = qkv.shape[0]
    scale = HEAD_DIM ** -0.5
    tq, tk = ATT_TQ, ATT_TK
    heads = range(ATT_HP)

    def body(*refs):
        qkv_refs, (o_ref, ot_ref, tot_ref) = refs[:3 * ATT_HP], refs[3 * ATT_HP:]
        g, i = pl.program_id(0), pl.program_id(1)

        @pl.when((g == 0) & (i == 0))
        def _():
            tot_ref[...] = jnp.zeros_like(tot_ref)

        qs = [qkv_refs[3 * j][...] for j in heads]
        upper = _tri(tk, lambda r, c: r > c)

        def tile(kj, carry, mask):
            rows = pl.ds(pl.multiple_of(kj * tk, tk), tk)
            z = [_dot(qs[j], qkv_refs[3 * j + 1][rows, :], "nt") * scale for j in heads]
            lsz = [_log_sigmoid(z[j]) for j in heads]
            lk = [lsz[j] - z[j] if mask is None else jnp.where(mask, lsz[j] - z[j], 0.0) for j in heads]
            parts = [_split2(lk[j]) for j in heads]
            above = [carry[j][0] + _dot(parts[j][0], upper) + _dot(parts[j][1], upper) for j in heads]
            w = [jnp.exp(lsz[j] + above[j]) for j in heads]
            if mask is not None:
                w = [jnp.where(mask, w[j], 0.0) for j in heads]
            return tuple((carry[j][0] + jnp.sum(lk[j], axis=1, keepdims=True),
                          carry[j][1] + _dot(w[j], qkv_refs[3 * j + 2][rows, :])) for j in heads)

        carry = tile(i, tuple((jnp.zeros((tq, 1), F32), jnp.zeros((tq, HEAD_DIM), F32)) for _ in heads), _diag_mask(True))
        carry = lax.fori_loop(0, i, lambda n, cr: tile(i - 1 - n, cr, None), carry)
        q_rows = pl.ds(pl.multiple_of(i * tq, tq), tq)
        for j in heads:
            c, acc = carry[j]
            o = acc.astype(BF16)
            o_ref[:, _head_cols(j)] = o
            ot_ref[_head_cols(j), :] = o.T
            _lane_put(tot_ref, q_rows, ATT_HP * g + j, c)

    wide = ATT_HP * HEAD_DIM
    return pl.pallas_call(
        body, name="sb_fwd", grid=(n_heads // ATT_HP, s // tq),
        in_specs=_qkv_specs(0, s),
        out_specs=[pl.BlockSpec((tq, wide), lambda g, i: (i, g)), pl.BlockSpec((wide, tq), lambda g, i: (g, i)),
                   pl.BlockSpec((s, LANES), lambda g, i: (0, 0))],
        out_shape=[_sds((s, n_heads * HEAD_DIM), BF16), _sds((n_heads * HEAD_DIM, s), BF16), _sds((s, LANES), F32)],
        compiler_params=_params(("arbitrary", "arbitrary")),
    )(*[qkv] * (3 * ATT_HP))


def _sb_bwd(qkv, do, tot, n_heads, dep):
    s = qkv.shape[0]
    scale = HEAD_DIM ** -0.5
    tq, tk = ATT_TQ, ATT_TK
    nq = s // tq
    hd = HEAD_DIM

    heads = range(ATT_HP)

    def body(*refs):
        qkv_refs = refs[:3 * ATT_HP]
        do_ref, tot_ref, _, out_ref, dk_acc, dv_acc = refs[3 * ATT_HP:]
        g, i = pl.program_id(0), pl.program_id(1)

        @pl.when(i == 0)
        def _():
            dk_acc[...] = jnp.zeros_like(dk_acc)
            dv_acc[...] = jnp.zeros_like(dv_acc)

        qs = [qkv_refs[3 * j][...] for j in heads]
        douts = [do_ref[:, _head_cols(j)] for j in heads]
        totals = [_lane_pick(tot_ref[...], ATT_HP * g + j) for j in heads]
        incl = _tri(tk, lambda r, c: r <= c)
        excl = _tri(tk, lambda r, c: r < c)

        def tile(kj, carry, mask):
            rows = pl.ds(pl.multiple_of(kj * tk, tk), tk)
            k_t = [qkv_refs[3 * j + 1][rows, :] for j in heads]
            z = [_dot(qs[j], k_t[j], "nt") * scale for j in heads]
            dw = [_dot(douts[j], qkv_refs[3 * j + 2][rows, :], "nt") for j in heads]
            lsz = [_log_sigmoid(z[j]) for j in heads]
            lk = [lsz[j] - z[j] if mask is None else jnp.where(mask, lsz[j] - z[j], 0.0) for j in heads]
            parts = [_split2(lk[j]) for j in heads]
            below = [carry[j][0] + _dot(parts[j][0], incl) + _dot(parts[j][1], incl) for j in heads]
            w = [jnp.exp(lsz[j] + (totals[j] - below[j])) for j in heads]
            if mask is not None:
                w = [jnp.where(mask, w[j], 0.0) for j in heads]
            e = [dw[j] * w[j] for j in heads]
            parts = [_split2(e[j]) for j in heads]
            e_before = [carry[j][1] + _dot(parts[j][0], excl) + _dot(parts[j][1], excl) for j in heads]
            sg = [jnp.exp(lsz[j]) for j in heads]
            dz = [e[j] * (1.0 - sg[j]) - e_before[j] * sg[j] for j in heads]
            if mask is not None:
                dz = [jnp.where(mask, dz[j], 0.0) for j in heads]
            dz = [(dz[j] * scale).astype(BF16) for j in heads]
            for j in heads:
                dk_acc[j, rows, :] += _dot(dz[j], qs[j], "tn")
                dv_acc[j, rows, :] += _dot(w[j], douts[j], "tn")
            return tuple((carry[j][0] + jnp.sum(lk[j], axis=1, keepdims=True),
                          carry[j][1] + jnp.sum(e[j], axis=1, keepdims=True),
                          carry[j][2] + _dot(dz[j], k_t[j])) for j in heads)

        zero = jnp.zeros((tq, 1), F32)
        carry = lax.fori_loop(0, i, lambda kj, cr: tile(kj, cr, None),
                              tuple((zero, zero, jnp.zeros((tq, hd), F32)) for _ in heads))
        carry = tile(i, carry, _diag_mask(True))
        for j in heads:
            out_ref[pl.ds(pl.multiple_of(i * tq, tq), tq), pl.ds(3 * j * hd, hd)] = carry[j][2].astype(BF16)

        @pl.when(i == nq - 1)
        def _():
            for j in heads:
                out_ref[:, pl.ds((3 * j + 1) * hd, hd)] = dk_acc[j].astype(BF16)
                out_ref[:, pl.ds((3 * j + 2) * hd, hd)] = dv_acc[j].astype(BF16)

    wide = ATT_HP * hd
    return pl.pallas_call(
        body, name="sb_bwd", grid=(n_heads // ATT_HP, nq),
        in_specs=_qkv_specs(0, s) + [pl.BlockSpec((tq, wide), lambda g, i: (i, g)),
                                     pl.BlockSpec((tq, LANES), lambda g, i: (i, 0)), ANY],
        out_specs=pl.BlockSpec((s, 3 * wide), lambda g, i: (0, g)),
        out_shape=_sds(qkv.shape, BF16),
        scratch_shapes=[pltpu.VMEM((ATT_HP, s, hd), F32), pltpu.VMEM((ATT_HP, s, hd), F32)],
        compiler_params=_params(("arbitrary", "arbitrary")),
    )(*[qkv] * (3 * ATT_HP), do, tot, dep)


def _fox_fwd(qkv, cum_col, cum_row, n_heads, hb0, dep):
    s = qkv.shape[0]
    scale = HEAD_DIM ** -0.5
    tq, tk = ATT_TQ, ATT_TK

    heads = range(ATT_HP)

    def body(*refs):
        qkv_refs = refs[:3 * ATT_HP]
        cc_ref, cr_ref, _, o_ref, ot_ref, o32_ref, lse_ref = refs[3 * ATT_HP:]
        g, i = pl.program_id(0), pl.program_id(1)

        @pl.when((g == 0) & (i == 0))
        def _():
            lse_ref[...] = jnp.zeros_like(lse_ref)

        qs = [qkv_refs[3 * j][...] for j in heads]
        cqs = [_lane_pick(cc_ref[...], ATT_HP * g + j) for j in heads]

        def tile(kj, carry, mask):
            rows = pl.ds(pl.multiple_of(kj * tk, tk), tk)
            sc = [_dot(qs[j], qkv_refs[3 * j + 1][rows, :], "nt") * scale + cqs[j]
                  - cr_ref[kj, pl.ds(ATT_HP * g + j, 1), :] for j in heads]
            if mask is not None:
                sc = [jnp.where(mask, sc[j], NEG_BIG) for j in heads]
            m_new = [jnp.maximum(carry[j][0], jnp.max(sc[j], axis=1, keepdims=True)) for j in heads]
            p = [jnp.exp(sc[j] - m_new[j]) for j in heads]
            alpha = [jnp.exp(carry[j][0] - m_new[j]) for j in heads]
            parts = [_split2(p[j]) for j in heads]
            v_t = [qkv_refs[3 * j + 2][rows, :] for j in heads]
            pv = [_dot(parts[j][0], v_t[j]) + _dot(parts[j][1], v_t[j]) for j in heads]
            return tuple((m_new[j], alpha[j] * carry[j][1] + jnp.sum(p[j], axis=1, keepdims=True),
                          alpha[j] * carry[j][2] + pv[j]) for j in heads)

        carry = tuple((jnp.full((tq, 1), NEG_BIG, F32), jnp.zeros((tq, 1), F32), jnp.zeros((tq, HEAD_DIM), F32))
                      for _ in heads)
        carry = lax.fori_loop(0, i, lambda kj, cr: tile(kj, cr, None), carry)
        carry = tile(i, carry, _diag_mask(False))
        q_rows = pl.ds(pl.multiple_of(i * tq, tq), tq)
        for j in heads:
            m, l, acc = carry[j]
            o = acc / l
            o_ref[:, _head_cols(j)] = o.astype(BF16)
            ot_ref[_head_cols(j), :] = o.astype(BF16).T
            o32_ref[:, _head_cols(j)] = o
            _lane_put(lse_ref, q_rows, ATT_HP * g + j, m + jnp.log(l))

    nb = cum_row.shape[0]
    wide = ATT_HP * HEAD_DIM
    return pl.pallas_call(
        body, name="fox_fwd", grid=(n_heads // ATT_HP, s // tq),
        in_specs=_qkv_specs(hb0, s) + [pl.BlockSpec((tq, LANES), lambda g, i: (i, 0)),
                                       pl.BlockSpec((nb, 8, tk), lambda g, i: (0, 0, 0)), ANY],
        out_specs=[pl.BlockSpec((tq, wide), lambda g, i: (i, g)), pl.BlockSpec((wide, tq), lambda g, i: (g, i)),
                   pl.BlockSpec((tq, wide), lambda g, i: (i, g)), pl.BlockSpec((s, LANES), lambda g, i: (0, 0))],
        out_shape=[_sds((s, n_heads * HEAD_DIM), BF16), _sds((n_heads * HEAD_DIM, s), BF16),
                   _sds((s, n_heads * HEAD_DIM), F32), _sds((s, LANES), F32)],
        compiler_params=_params(("arbitrary", "arbitrary")),
    )(*[qkv] * (3 * ATT_HP), cum_col, cum_row, dep)


def _fox_bwd(dqkv, qkv, do, o, lse, cum_col, cum_row, n_heads, hb0, dep):
    s = qkv.shape[0]
    scale = HEAD_DIM ** -0.5
    tq, tk = ATT_TQ, ATT_TK
    nq = s // tq
    hd = HEAD_DIM

    heads = range(ATT_HP)
    assert hb0 % ATT_HP == 0

    def body(*refs):
        qkv_refs = refs[1:1 + 3 * ATT_HP]
        do_ref, o_ref, lse_ref, cc_ref, cr_ref, _, out_ref, dc_ref, dk_acc, dv_acc, col_acc = refs[1 + 3 * ATT_HP:]
        g, i = pl.program_id(0), pl.program_id(1)

        @pl.when((g == 0) & (i == 0))
        def _():
            dc_ref[...] = jnp.zeros_like(dc_ref)

        @pl.when(i == 0)
        def _():
            dk_acc[...] = jnp.zeros_like(dk_acc)
            dv_acc[...] = jnp.zeros_like(dv_acc)
            col_acc[...] = jnp.zeros_like(col_acc)

        qs = [qkv_refs[3 * j][...] for j in heads]
        douts = [do_ref[:, _head_cols(j)] for j in heads]
        deltas = [jnp.sum(douts[j].astype(F32) * o_ref[:, _head_cols(j)], axis=1, keepdims=True) for j in heads]
        shifts = [_lane_pick(cc_ref[...], ATT_HP * g + j) - _lane_pick(lse_ref[...], ATT_HP * g + j) for j in heads]

        def tile(kj, carry, mask):
            rows = pl.ds(pl.multiple_of(kj * tk, tk), tk)
            k_t = [qkv_refs[3 * j + 1][rows, :] for j in heads]
            sc = [_dot(qs[j], k_t[j], "nt") * scale + shifts[j] - cr_ref[kj, pl.ds(ATT_HP * g + j, 1), :] for j in heads]
            dp = [_dot(douts[j], qkv_refs[3 * j + 2][rows, :], "nt") for j in heads]
            p = [jnp.exp(sc[j]) for j in heads]
            if mask is not None:
                p = [jnp.where(mask, p[j], 0.0) for j in heads]
            ds_f = [p[j] * (dp[j] - deltas[j]) for j in heads]
            ds = [(ds_f[j] * scale).astype(BF16) for j in heads]
            for j in heads:
                col_acc[j, kj] += jnp.broadcast_to(jnp.sum(ds_f[j], axis=0, keepdims=True), (8, tk))
                dk_acc[j, rows, :] += _dot(ds[j], qs[j], "tn")
                dv_acc[j, rows, :] += _dot(p[j], douts[j], "tn")
            return tuple((carry[j][0] + _dot(ds[j], k_t[j]), carry[j][1] + jnp.sum(ds_f[j], axis=1, keepdims=True))
                         for j in heads)

        carry = lax.fori_loop(0, i, lambda kj, cr: tile(kj, cr, None),
                              tuple((jnp.zeros((tq, hd), F32), jnp.zeros((tq, 1), F32)) for _ in heads))
        carry = tile(i, carry, _diag_mask(False))
        q_rows = pl.ds(pl.multiple_of(i * tq, tq), tq)
        for j in heads:
            out_ref[q_rows, pl.ds(3 * j * hd, hd)] = carry[j][0].astype(BF16)
            _lane_put(dc_ref, q_rows, ATT_HP * g + j, carry[j][1])

        @pl.when(i == nq - 1)
        def _():
            lane = lax.broadcasted_iota(jnp.int32, (tk, LANES), 1)
            for j in heads:
                out_ref[:, pl.ds((3 * j + 1) * hd, hd)] = dk_acc[j].astype(BF16)
                out_ref[:, pl.ds((3 * j + 2) * hd, hd)] = dv_acc[j].astype(BF16)
                for kj in range(nb):
                    col = jnp.broadcast_to(col_acc[j, kj][0:1, :], (LANES, tk)).T
                    old = dc_ref[pl.ds(kj * tk, tk), :]
                    dc_ref[pl.ds(kj * tk, tk), :] = jnp.where(lane == ATT_HP * g + j, old - col, old)

    nb = cum_row.shape[0]
    wide = ATT_HP * hd
    return pl.pallas_call(
        body, name="fox_bwd", grid=(n_heads // ATT_HP, nq),
        in_specs=[ANY] + _qkv_specs(hb0, s) + [
            pl.BlockSpec((tq, wide), lambda g, i: (i, g)), pl.BlockSpec((tq, wide), lambda g, i: (i, g)),
            pl.BlockSpec((tq, LANES), lambda g, i: (i, 0)), pl.BlockSpec((tq, LANES), lambda g, i: (i, 0)),
            pl.BlockSpec((nb, 8, tk), lambda g, i: (0, 0, 0)), ANY],
        out_specs=[pl.BlockSpec((s, 3 * wide), lambda g, i: (0, hb0 // ATT_HP + g)),
                   pl.BlockSpec((s, LANES), lambda g, i: (0, 0))],
        out_shape=[_sds(dqkv.shape, BF16), _sds((s, LANES), F32)],
        scratch_shapes=[pltpu.VMEM((ATT_HP, s, hd), F32), pltpu.VMEM((ATT_HP, s, hd), F32),
                        pltpu.VMEM((ATT_HP, s // tk, 8, tk), F32)],
        input_output_aliases={0: 0},
        compiler_params=_params(("arbitrary", "arbitrary")),
    )(dqkv, *[qkv] * (3 * ATT_HP), do, o, lse, cum_col, cum_row, dep)


def _branch_merge(o_sb, o_fx, w_sb, w_fx, gf, dep, tm=1024):
    s = o_sb.shape[0]
    cs = w_sb.shape[2]
    tm = _tile(s, tm)

    def body(osb_ref, ofx_ref, wsb_ref, wfx_ref, g_ref, dep_ref, merged_ref, mt_ref, asb_ref, afx_ref):
        del dep_ref
        a_sb = _dot(osb_ref[...], wsb_ref[...])
        a_fx = _dot(ofx_ref[...], wfx_ref[...])
        g = g_ref[...]
        merged = (_sigmoid(g[:, :cs]) * a_sb + _sigmoid(g[:, cs:]) * a_fx).astype(BF16)
        merged_ref[...] = merged
        mt_ref[...] = merged.T
        asb_ref[...] = a_sb.astype(BF16)
        afx_ref[...] = a_fx.astype(BF16)

    blk = pl.BlockSpec((tm, cs), lambda i, j: (i, j))
    out = _sds((s, N_DEV * cs), BF16)
    return pl.pallas_call(
        body, name="branch_merge", grid=(s // tm, N_DEV),
        in_specs=[pl.BlockSpec((tm, o_sb.shape[1]), lambda i, j: (i, 0)),
                  pl.BlockSpec((tm, o_fx.shape[1]), lambda i, j: (i, 0)),
                  pl.BlockSpec((None,) + w_sb.shape[1:], lambda i, j: (j, 0, 0)),
                  pl.BlockSpec((None,) + w_fx.shape[1:], lambda i, j: (j, 0, 0)),
                  pl.BlockSpec((tm, 2 * cs), lambda i, j: (i, j)), ANY],
        out_specs=[blk, pl.BlockSpec((cs, tm), lambda i, j: (j, i)), blk, blk],
        out_shape=[out, _sds((N_DEV * cs, s), BF16), out, out],
        compiler_params=_params(("parallel", "arbitrary")),
    )(o_sb, o_fx, w_sb, w_fx, gf, dep)


def _merge_bwd(dmix, w_out, gf, a_sb, a_fx, tm=1024, tk=2048, dep=None):
    s, d = dmix.shape
    cs = d // N_DEV
    tm, tk = _tile(s, tm), _tile(d, tk)

    def epilogue(acc, ex, outs):
        g, a_sb, a_fx = ex[0][...], ex[1][...].astype(F32), ex[2][...].astype(F32)
        s_sb, s_fx = _sigmoid(g[:, :cs]), _sigmoid(g[:, cs:])
        outs[0][...] = (acc * s_sb).astype(BF16)
        outs[1][...] = (acc * s_fx).astype(BF16)
        outs[2][...] = jnp.concatenate([acc * a_sb * s_sb * (1.0 - s_sb), acc * a_fx * s_fx * (1.0 - s_fx)],
                                       axis=1).astype(BF16)

    blk = pl.BlockSpec((tm, cs), lambda i, j, k: (i, j))
    wide = pl.BlockSpec((tm, 2 * cs), lambda i, j, k: (i, j))
    return _matmul(
        "merge_bwd", "nt",
        [(dmix, pl.BlockSpec((tm, tk), lambda i, j, k: (i, k)), w_out, pl.BlockSpec((cs, tk), lambda i, j, k: (j, k)))],
        (s // tm, N_DEV, d // tk), (tm, cs),
        [_sds((s, d), BF16), _sds((s, d), BF16), _sds(gf.shape, BF16)], [blk, blk, wide],
        extras=[(gf, wide), (a_sb, blk), (a_fx, blk)], epilogue=epilogue, dep=dep)


def _ffn_up(u2, w_gate, w_up, dep, tm=1024):
    s, d = u2.shape
    fs = w_gate.shape[2]
    tm = _tile(s, tm)

    def body(u_ref, wg_ref, wu_ref, dep_ref, gate_ref, up_ref, act_ref, actt_ref):
        del dep_ref
        u = u_ref[...]
        gate = _dot(u, wg_ref[...])
        up = _dot(u, wu_ref[...])
        gate_ref[...] = gate
        up_ref[...] = up
        act = (gate * _sigmoid(gate) * up).astype(BF16)
        act_ref[...] = act
        actt_ref[...] = act.T

    w_spec = pl.BlockSpec((None, d, fs), lambda i, j: (j, 0, 0))
    o_spec = pl.BlockSpec((None, tm, fs), lambda i, j: (j, i, 0))
    return pl.pallas_call(
        body, name="ffn_up", grid=(s // tm, N_DEV),
        in_specs=[pl.BlockSpec((tm, d), lambda i, j: (i, 0)), w_spec, w_spec, ANY],
        out_specs=[o_spec, o_spec, o_spec, pl.BlockSpec((None, fs, tm), lambda i, j: (j, 0, i))],
        out_shape=[_sds((N_DEV, s, fs), F32), _sds((N_DEV, s, fs), F32), _sds((N_DEV, s, fs), BF16),
                   _sds((N_DEV, fs, s), BF16)],
        compiler_params=_params(("parallel", "arbitrary")),
    )(u2, w_gate, w_up, dep)


def _ffn_down_bwd(dff, w_down, gate, up, tm=1024):
    s, d = dff.shape
    fs = w_down.shape[1]
    tm = _tile(s, tm)

    def body(dff_ref, wd_ref, gate_ref, up_ref, dgate_ref, dup_ref):
        dact = _dot(dff_ref[...], wd_ref[...], "nt")
        gate = gate_ref[...]
        sg = _sigmoid(gate)
        dup_ref[...] = (dact * gate * sg).astype(BF16)
        dgate_ref[...] = (dact * up_ref[...] * sg * (1.0 + gate * (1.0 - sg))).astype(BF16)

    a_spec = pl.BlockSpec((None, tm, fs), lambda i, j: (j, i, 0))
    return pl.pallas_call(
        body, name="ffn_down_bwd", grid=(s // tm, N_DEV),
        in_specs=[pl.BlockSpec((tm, d), lambda i, j: (i, 0)), pl.BlockSpec((None, fs, d), lambda i, j: (j, 0, 0)),
                  a_spec, a_spec],
        out_specs=[a_spec, a_spec],
        out_shape=[_sds((N_DEV, s, fs), BF16), _sds((N_DEV, s, fs), BF16)],
        compiler_params=_params(("parallel", "arbitrary")),
    )(dff, w_down, gate, up)


def _mesh_place():
    x, y, c = lax.axis_index("x"), lax.axis_index("y"), lax.axis_index("c")
    peers = []
    for d in range(1, N_DEV):
        px = 1 - x if d & 4 else x
        py = 1 - y if d & 2 else y
        pc = 1 - c if d & 1 else c
        peers.append((d, (px, py, pc), 4 * px + 2 * py + pc))
    return 4 * x + 2 * y + c, peers


def _flat_me():
    return 4 * lax.axis_index("x") + 2 * lax.axis_index("y") + lax.axis_index("c")


def _in_hbm(a):
    return pltpu.with_memory_space_constraint(a, pltpu.HBM)


def _pair_plan():
    x, y, c = lax.axis_index("x"), lax.axis_index("y"), lax.axis_index("c")
    return [(2 * q + (1 - c), q, q, (x, y, 1 - c)) for q in range(4)]


def _chip_plan():
    x, y, c = lax.axis_index("x"), lax.axis_index("y"), lax.axis_index("c")
    plan = []
    for fx, fy in ((1, 0), (0, 1), (1, 1)):
        cx, cy = (1 - x if fx else x), (1 - y if fy else y)
        plan.append((2 * cx + cy, 2 * x + y, 2 * cx + cy, (cx, cy, c)))
    return plan


def _split_start(name, srcs, lands, plan, k):
    n = len(srcs)

    def body(*refs):
        ins, lnd = refs[:n], refs[n:2 * n]
        send, recv, token = refs[2 * n], refs[2 * n + 1], refs[-1]
        copies = plan()
        for a in range(n):
            for t, (src, dst, _, dev) in enumerate(copies):
                pltpu.make_async_remote_copy(src_ref=ins[a].at[src], dst_ref=lnd[a].at[dst], send_sem=send.at[k * a + t],
                                             recv_sem=recv.at[k * a + t], device_id=dev, device_id_type=MESH).start()
        token[...] = jnp.zeros_like(token)

    res = pl.pallas_call(
        body, name=name,
        out_shape=[pltpu.SemaphoreType.DMA((n * k,)), pltpu.SemaphoreType.DMA((n * k,))]
        + [pltpu.HBM(a.shape, a.dtype) for a in list(srcs) + list(lands)] + [_sds((8, LANES), F32)],
        in_specs=[HBM] * (2 * n), out_specs=[SEM, SEM] + [HBM] * (2 * n) + [pl.BlockSpec(memory_space=pltpu.VMEM)],
        input_output_aliases={i: 2 + i for i in range(2 * n)},
        compiler_params=pltpu.CompilerParams(has_side_effects=EFFECT),
    )(*[_in_hbm(a) for a in srcs], *[_in_hbm(a) for a in lands])
    return res[0], res[1], res[2:2 + n], res[2 + n:2 + 2 * n], res[-1]


def _split_wait(name, send, recv, srcs, lands, plan, k, after):
    n = len(srcs)

    def body(*refs):
        ins, lnd = refs[:n], refs[n:2 * n]
        send_sem, recv_sem = refs[2 * n], refs[2 * n + 1]
        copies = plan()
        for a in range(n):
            for t, (src, _, dst, dev) in enumerate(copies):
                cp = pltpu.make_async_remote_copy(src_ref=ins[a].at[src], dst_ref=lnd[a].at[dst], send_sem=send_sem.at[k * a + t],
                                                  recv_sem=recv_sem.at[k * a + t], device_id=dev, device_id_type=MESH)
                cp.wait_send()
                cp.wait_recv()

    res = pl.pallas_call(
        body, name=name,
        out_shape=[pltpu.HBM(a.shape, a.dtype) for a in list(srcs) + list(lands)],
        in_specs=[HBM] * (2 * n) + [SEM, SEM] + [ANY] * len(after), out_specs=[HBM] * (2 * n),
        input_output_aliases={i: i for i in range(2 * n)},
        compiler_params=pltpu.CompilerParams(has_side_effects=EFFECT),
    )(*srcs, *lands, send, recv, *after)
    return res[:n], res[n:]


def _pair_add(name, parts, land):
    _, r, cols = parts.shape
    tr = max(16, min(r, ((1 << 20) // (2 * cols)) // 16 * 16))
    while r % tr:
        tr -= 16

    def body(c_ref, p_ref, l_ref, o_ref):
        del c_ref
        o_ref[...] = (p_ref[...].astype(F32) + l_ref[...].astype(F32)).astype(BF16)

    blk = pl.BlockSpec((None, tr, cols), lambda q, i, c_ref: (q, i, 0))
    return pl.pallas_call(
        body, name=name,
        grid_spec=pltpu.PrefetchScalarGridSpec(
            num_scalar_prefetch=1, grid=(4, r // tr),
            in_specs=[pl.BlockSpec((None, tr, cols), lambda q, i, c_ref: (2 * q + c_ref[0], i, 0)), blk], out_specs=blk),
        out_shape=_sds((4, r, cols), BF16),
        compiler_params=_params(("parallel", "parallel")),
    )(jnp.reshape(lax.axis_index("c"), (1,)).astype(jnp.int32), parts, land)


def _scatter_pairs(tag, parts):
    lands = [lax.empty((4,) + a.shape[1:], a.dtype) for a in parts]
    return _split_start("pair_" + tag, parts, lands, _pair_plan, 4)


def _scatter_chips(tag, started, after):
    send, recv, parts, lands, _ = started
    parts, lands = _split_wait("pair_" + tag + "_wait", send, recv, parts, lands, _pair_plan, 4, [after])
    sums = [_pair_add("pair_" + tag + "_add%d" % a, p, l) for a, (p, l) in enumerate(zip(parts, lands))]
    chip = 2 * lax.axis_index("x") + lax.axis_index("y")
    final = [lax.dynamic_update_slice_in_dim(lax.empty(v.shape, v.dtype), lax.dynamic_slice_in_dim(v, chip, 1, 0), chip, 0)
             for v in sums]
    return _split_start("chips_" + tag, sums, final, _chip_plan, 3)


def _scatter_end(tag, started, after):
    send, recv, sums, final, _ = started
    return _split_wait("chips_" + tag + "_wait", send, recv, sums, final, _chip_plan, 3, after)[1]


def _gather_targets():
    x, y, c = lax.axis_index("x"), lax.axis_index("y"), lax.axis_index("c")
    chips = [(x, y), (1 - x, y), (x, 1 - y), (1 - x, 1 - y)]
    same = [((cx, cy, c), 4 * cx + 2 * cy + c) for cx, cy in chips]
    other = [((cx, cy, 1 - c), 4 * cx + 2 * cy + 1 - c) for cx, cy in chips]
    return same[0][1], [other[0]] + same[1:], [flat for _, flat in other[1:]], other[0][0]


def _gather_start(shards):
    n = len(shards)
    me = _flat_me()
    lands = [lax.dynamic_update_slice_in_dim(lax.empty((N_DEV,) + a.shape, a.dtype), a[None], me, 0) for a in shards]

    def body(*refs):
        lnd, send, recv, token = refs[:n], refs[n], refs[n + 1], refs[-1]
        mine, targets, _, _ = _gather_targets()
        for a in range(n):
            for t, (dev, _) in enumerate(targets):
                pltpu.make_async_remote_copy(src_ref=lnd[a].at[mine], dst_ref=lnd[a].at[mine], send_sem=send.at[4 * a + t],
                                             recv_sem=recv.at[4 * a + t], device_id=dev, device_id_type=MESH).start()
        token[...] = jnp.zeros_like(token)

    res = pl.pallas_call(
        body, name="gather_start",
        out_shape=[pltpu.SemaphoreType.DMA((4 * n,)), pltpu.SemaphoreType.DMA((4 * n,))]
        + [pltpu.HBM(a.shape, a.dtype) for a in lands] + [_sds((8, LANES), F32)],
        in_specs=[HBM] * n, out_specs=[SEM, SEM] + [HBM] * n + [pl.BlockSpec(memory_space=pltpu.VMEM)],
        input_output_aliases={i: 2 + i for i in range(n)},
        compiler_params=pltpu.CompilerParams(has_side_effects=EFFECT),
    )(*[_in_hbm(a) for a in lands])
    return res[0], res[1], list(res[2:2 + n]), res[-1]


def _gather_forward(name, lands, first, send, recv, after):
    n = len(lands)

    def body(*refs):
        lnd, send_sem, recv_sem = refs[:n], refs[n], refs[n + 1]
        send2, recv2, token = refs[-3], refs[-2], refs[-1]
        mine, targets, _, sibling = _gather_targets()
        for a in range(n):
            for t, (dev, flat) in enumerate(targets):
                cp = pltpu.make_async_remote_copy(src_ref=lnd[a].at[mine], dst_ref=lnd[a].at[flat],
                                                  send_sem=send_sem.at[4 * (first + a) + t],
                                                  recv_sem=recv_sem.at[4 * (first + a) + t], device_id=dev, device_id_type=MESH)
                cp.wait_send()
                if t:
                    cp.wait_recv()
                    pltpu.make_async_remote_copy(src_ref=lnd[a].at[flat], dst_ref=lnd[a].at[flat], send_sem=send2.at[3 * a + t - 1],
                                                 recv_sem=recv2.at[3 * a + t - 1], device_id=sibling, device_id_type=MESH).start()
        token[...] = jnp.zeros_like(token)

    res = pl.pallas_call(
        body, name=name,
        out_shape=[pltpu.HBM(a.shape, a.dtype) for a in lands]
        + [pltpu.SemaphoreType.DMA((3 * n,)), pltpu.SemaphoreType.DMA((3 * n,)), _sds((8, LANES), F32)],
        in_specs=[HBM] * n + [SEM, SEM] + [ANY] * len(after),
        out_specs=[HBM] * n + [SEM, SEM, pl.BlockSpec(memory_space=pltpu.VMEM)],
        input_output_aliases={i: i for i in range(n)},
        compiler_params=pltpu.CompilerParams(has_side_effects=EFFECT),
    )(*lands, send, recv, *after)
    return list(res[:n]), res[n], res[n + 1], res[-1]


def _gather_wait(name, lands, first, recv, send2, recv2, after):
    n = len(lands)

    def body(*refs):
        lnd, recv_sem, send2_sem, recv2_sem = refs[:n], refs[n], refs[n + 1], refs[n + 2]
        mine, targets, passed, sibling = _gather_targets()
        for a in range(n):
            dev, flat = targets[0]
            pltpu.make_async_remote_copy(src_ref=lnd[a].at[mine], dst_ref=lnd[a].at[flat], send_sem=send2_sem.at[3 * a],
                                         recv_sem=recv_sem.at[4 * (first + a)], device_id=dev, device_id_type=MESH).wait_recv()
            for t in range(3):
                cp = pltpu.make_async_remote_copy(src_ref=lnd[a].at[targets[t + 1][1]], dst_ref=lnd[a].at[passed[t]],
                                                  send_sem=send2_sem.at[3 * a + t], recv_sem=recv2_sem.at[3 * a + t],
                                                  device_id=sibling, device_id_type=MESH)
                cp.wait_send()
                cp.wait_recv()

    res = pl.pallas_call(
        body, name=name, out_shape=[pltpu.HBM(a.shape, a.dtype) for a in lands],
        in_specs=[HBM] * n + [SEM, SEM, SEM, ANY], out_specs=[HBM] * n,
        input_output_aliases={i: i for i in range(n)},
        compiler_params=pltpu.CompilerParams(has_side_effects=EFFECT),
    )(*lands, recv, send2, recv2, after)
    return list(res)


def _adamw_decay(w, m, v):
    return ADAM_WD * w, ADAM_B1 * m, ADAM_B2 * v


def _adamw_finish(g, wd_w, m1, v1):
    m = m1 + (1.0 - ADAM_B1) * g
    v = v1 + (1.0 - ADAM_B2) * (g * g)
    m_hat = m / (1.0 - ADAM_B1 ** ADAM_STEP)
    v_hat = v / (1.0 - ADAM_B2 ** ADAM_STEP)
    delta = -ADAM_LR * (m_hat / (jnp.sqrt(v_hat) + ADAM_EPS) + wd_w)
    return delta, m, v


def _adamw(g, w, m, v):
    return _adamw_finish(g, *_adamw_decay(w, m, v))


def _update_prep(name, w, m, v, dep, block_bytes=1 << 20):
    _, r, c = w.shape
    tr = max(8, min(r, (block_bytes // (4 * c)) // 8 * 8))
    while r % tr:
        tr -= 8

    def body(w_ref, m_ref, v_ref, dep_ref, ow_ref, om_ref, ov_ref):
        del dep_ref
        ow_ref[...], om_ref[...], ov_ref[...] = _adamw_decay(w_ref[...], m_ref[...], v_ref[...])

    blk = pl.BlockSpec((None, tr, c), lambda i: (0, i, 0))
    return pl.pallas_call(
        body, name=name, grid=(r // tr,), in_specs=[blk] * 3 + [ANY], out_specs=[blk] * 3,
        out_shape=[_sds((1, r, c), F32)] * 3, compiler_params=_params(("parallel",)),
    )(w, m, v, dep)


def _update(name, parts, w, m, v, layout=None, decayed=False, transposed_out=False, block_bytes=1 << 20):
    _, r, c = w.shape
    n_slots, _, cp = parts.shape
    tr = max(8, min(r, (block_bytes // (4 * cp)) // 8 * 8))
    if transposed_out:
        tr = _tile(r, 256)
    while r % tr:
        tr -= 8

    def body(p_ref, w_ref, m_ref, v_ref, g_ref, d_ref, nm_ref, nv_ref, *scratch):
        g = p_ref[0].astype(F32)
        for p in range(1, n_slots):
            g = g + p_ref[p].astype(F32)
        if layout is not None:
            s1, s2, lg = layout.my_shifts()
            lane = lax.broadcasted_iota(jnp.int32, g.shape, 1)
            scratch[0][...] = jnp.where(lane < lg, pltpu.roll(g, cp - s1, 1), pltpu.roll(g, cp - s2, 1))
            g = scratch[0][:, 0:c]
        step = _adamw_finish if decayed else _adamw
        results = (g,) + step(g, w_ref[...], m_ref[...], v_ref[...])
        for ref, val in zip((g_ref, d_ref, nm_ref, nv_ref), results):
            ref[...] = val.T if transposed_out else val

    blk = pl.BlockSpec((None, tr, c), lambda i: (0, i, 0))
    out_blk = pl.BlockSpec((None, c, tr), lambda i: (0, 0, i)) if transposed_out else blk
    res = pl.pallas_call(
        body, name=name, grid=(r // tr,),
        in_specs=[pl.BlockSpec((n_slots, tr, cp), lambda i: (0, i, 0)), blk, blk, blk],
        out_specs=[out_blk] * 4, out_shape=[_sds((1, c, r) if transposed_out else (1, r, c), F32)] * 4,
        scratch_shapes=[] if layout is None else [pltpu.VMEM((tr, cp), F32)],
        compiler_params=_params(("parallel",)),
    )(parts, w, m, v)
    return [jnp.transpose(o, (0, 2, 1)) for o in res] if transposed_out else res


def _small_update(part, w, m, v):
    n = part.shape[1]

    def body(p_ref, w_ref, m_ref, v_ref, g_ref, d_ref, nm_ref, nv_ref, buf, send, recv):
        me, peers = _mesh_place()
        buf[me] = p_ref[...]
        sent = []
        for d, dev, flat in peers:
            cp = pltpu.make_async_remote_copy(src_ref=p_ref, dst_ref=buf.at[me], send_sem=send.at[d],
                                              recv_sem=recv.at[d], device_id=dev, device_id_type=MESH)
            cp.start()
            sent.append(cp)
        for d, dev, flat in peers:
            pltpu.make_async_remote_copy(src_ref=p_ref, dst_ref=buf.at[flat], send_sem=send.at[d],
                                         recv_sem=recv.at[d], device_id=dev, device_id_type=MESH).wait_recv()
        for cp in sent:
            cp.wait_send()
        g = buf[0]
        for p in range(1, N_DEV):
            g = g + buf[p]
        g_ref[...] = g
        d_ref[...], nm_ref[...], nv_ref[...] = _adamw(g, w_ref[...], m_ref[...], v_ref[...])

    vm = pl.BlockSpec(memory_space=pltpu.VMEM)
    return pl.pallas_call(
        body, name="small_update", in_specs=[vm] * 4, out_specs=[vm] * 4, out_shape=[_sds((1, n), F32)] * 4,
        scratch_shapes=[pltpu.VMEM((N_DEV, 1, n), F32), pltpu.SemaphoreType.DMA((N_DEV,)),
                        pltpu.SemaphoreType.DMA((N_DEV,))],
    )(part, w, m, v)


class _WInLayout:
    def __init__(self, n8, n_f, d_sb, d_fox, d):
        assert n8 % LANES == 1 and n_f < LANES and d % (N_DEV * LANES) == 0
        self.n8, self.n_f, self.d = n8, n_f, d
        self.sp = n8 // LANES
        self.wp = (n8 + 2 * LANES - 2) // LANES * LANES
        self.n_qkv = 3 * (d_sb + d_fox)
        nq, dt, tc = self.n_qkv // LANES, d // LANES, d // N_DEV // LANES
        h_sb, h_fox = d_sb // HEAD_DIM, d_fox // HEAD_DIM
        self.sources = {}
        self.part_tile = {}
        for p in range(N_DEV):
            lg = min(max(self.n_qkv + n_f - n8 * p, 0), n8)
            s1, s2 = p, p + LANES - n_f
            spans = []
            if lg > 0:
                spans.append(("a", self.sp * p, s1 // LANES, (lg + s1 - 1) // LANES))
            if lg < n8:
                spans.append(("g", self.sp * p - 1 - nq, (lg + s2) // LANES, (n8 - 1 + s2) // LANES))
            for kind, base, first, last in spans:
                for i in range(first, last + 1):
                    assert (p, i) not in self.part_tile
                    self.part_tile[(p, i)] = (kind, base + i)
                    self.sources.setdefault((kind, base + i), []).append((p, i))
        self.cat_tiles = [("a", r * h_sb + h) for h in range(h_sb) for r in range(3)]
        self.cat_tiles += [("a", 3 * h_sb + r * h_fox + h) for h in range(h_fox) for r in range(3)]
        self.cat_tiles += [("g", which * dt + j * tc + half) for j in range(N_DEV) for which in (0, 1) for half in range(tc)]
        self.cat_tiles += [("a", nq)] + [None] * (F_PAD // LANES - 1)
        self.cat_index = {key: c for c, key in enumerate(self.cat_tiles) if key is not None}

    def my_shifts(self):
        me = _flat_me()
        return me, me + LANES - self.n_f, jnp.clip(self.n_qkv + self.n_f - self.n8 * me, 0, self.n8)


def _lane_tile(i):
    return pl.ds(i * LANES, LANES)


def _w_in_shift(w_in, lay, tr=256):
    _, d, n8 = w_in.shape

    def body(w_ref, o_ref, buf):
        buf[...] = jnp.zeros_like(buf)
        buf[:, 0:n8] = w_ref[...]
        v = buf[...]
        s1, s2, lg = lay.my_shifts()
        pos = lax.broadcasted_iota(jnp.int32, v.shape, 1)
        o_ref[...] = jnp.where(pos < lg + s1, pltpu.roll(v, s1, 1),
                               jnp.where(pos >= lg + s2, pltpu.roll(v, s2, 1), 0.0)).astype(BF16)

    return pl.pallas_call(
        body, name="w_in_shift", grid=(d // tr,),
        in_specs=[pl.BlockSpec((None, tr, n8), lambda i: (0, i, 0))],
        out_specs=pl.BlockSpec((tr, lay.wp), lambda i: (i, 0)), out_shape=_sds((d, lay.wp), BF16),
        scratch_shapes=[pltpu.VMEM((tr, lay.wp), F32)],
        compiler_params=_params(("parallel",)),
    )(w_in)


def _w_in_build(g_in, lay, tr=256):
    d = g_in.shape[1]
    width = len(lay.cat_tiles) * LANES

    def body(g_ref, o_ref):
        for c, key in enumerate(lay.cat_tiles):
            if key is None:
                o_ref[:, _lane_tile(c)] = jnp.zeros((tr, LANES), BF16)
                continue
            (p, i), *more = lay.sources[key]
            val = g_ref[p, :, _lane_tile(i)]
            for p2, i2 in more:
                val = val + g_ref[p2, :, _lane_tile(i2)]
            o_ref[:, _lane_tile(c)] = val

    return pl.pallas_call(
        body, name="w_in_build", grid=(d // tr,),
        in_specs=[pl.BlockSpec((N_DEV, tr, lay.wp), lambda i: (0, i, 0))],
        out_specs=pl.BlockSpec((tr, width), lambda i: (i, 0)), out_shape=_sds((d, width), BF16),
        compiler_params=_params(("parallel",)),
    )(g_in)


def _w_in_grad_parts(dwq, dwgf, lay, tr=256):
    d = dwq.shape[0]
    nq = lay.n_qkv // LANES

    def body(q_ref, g_ref, o_ref):
        for p in range(N_DEV):
            for i in range(lay.wp // LANES):
                key = lay.part_tile.get((p, i))
                if key is None:
                    o_ref[p, :, _lane_tile(i)] = jnp.zeros((tr, LANES), BF16)
                    continue
                c = lay.cat_index[key]
                o_ref[p, :, _lane_tile(i)] = q_ref[:, _lane_tile(c)] if c < nq else g_ref[:, _lane_tile(c - nq)]

    return pl.pallas_call(
        body, name="w_in_grad_parts", grid=(d // tr,),
        in_specs=[pl.BlockSpec((tr, dwq.shape[1]), lambda i: (i, 0)), pl.BlockSpec((tr, dwgf.shape[1]), lambda i: (i, 0))],
        out_specs=pl.BlockSpec((N_DEV, tr, lay.wp), lambda i: (0, i, 0)), out_shape=_sds((N_DEV, d, lay.wp), BF16),
        compiler_params=_params(("parallel",)),
    )(dwq, dwgf)


def kernel(x, norm_mix_pre, norm_mix_post, w_in, b_forget, w_branch_sb, w_branch_fox, w_out, norm_ffn_pre, norm_ffn_post, w_ffn_gate, w_ffn_up, w_ffn_down, loss_target, m_norm_mix_pre, m_norm_mix_post, m_w_in, m_b_forget, m_w_branch_sb, m_w_branch_fox, m_w_out, m_norm_ffn_pre, m_norm_ffn_post, m_w_ffn_gate, m_w_ffn_up, m_w_ffn_down, v_norm_mix_pre, v_norm_mix_post, v_w_in, v_b_forget, v_w_branch_sb, v_w_branch_fox, v_w_out, v_norm_ffn_pre, v_norm_ffn_post, v_w_ffn_gate, v_w_ffn_up, v_w_ffn_down):
    xs, target = x[0], loss_target[0]
    s, d = xs.shape
    d_sb, d_fox = w_branch_sb.shape[1], w_branch_fox.shape[1]
    h_sb, h_fox = d_sb // HEAD_DIM, d_fox // HEAD_DIM
    n_f = b_forget.shape[1]
    fs = w_ffn_gate.shape[2]
    cs = d // N_DEV
    n_qkv = 3 * (d_sb + d_fox)
    n_gf = 2 * d + F_PAD
    f_blk = 2 * d // LANES
    big = (w_in, w_branch_sb, w_branch_fox, w_out, w_ffn_gate, w_ffn_up, w_ffn_down)
    big_m = (m_w_in, m_w_branch_sb, m_w_branch_fox, m_w_out, m_w_ffn_gate, m_w_ffn_up, m_w_ffn_down)
    big_v = (v_w_in, v_w_branch_sb, v_w_branch_fox, v_w_out, v_w_ffn_gate, v_w_ffn_up, v_w_ffn_down)

    lay = _WInLayout(w_in.shape[2], n_f, d_sb, d_fox, d)
    send1, recv1, lands, token = _gather_start([_w_in_shift(w_in, lay)] + [w[0].astype(BF16) for w in big[1:]])
    b_pad = jnp.pad(b_forget, ((0, 0), (0, LANES - n_f)))

    u, u_t = _pre_norm(xs, norm_mix_pre, dep=token)
    weights = dict(zip(("w_in", "w_branch_sb", "w_branch_fox", "w_out", "w_ffn_gate", "w_ffn_up", "w_ffn_down"),
                       zip(big, big_m, big_v)))
    decayed = {nm: _update_prep("decay_" + nm, *weights[nm], u) for nm in ("w_in", "w_ffn_gate", "w_ffn_up")}
    l_in, send2, recv2, token = _gather_forward("gather_in_forward", lands[0:1], 0, send1, recv1,
                                                [u] + [t[2] for t in decayed.values()])
    (g_in,) = _gather_wait("gather_in_wait", l_in, 0, recv1, send2, recv2, token)
    w_cat = _w_in_build(g_in, lay)
    qkv = _mm_plain("proj_qkv", "nn", u, w_cat, BF16, n=n_qkv)
    gf = _mm_plain("proj_gates", "nn", u, w_cat, F32, n_off=n_qkv, n=n_gf)
    cum_col, cum_row = _forget_fwd(gf, b_pad, f_blk)
    o_sb, o_sb_t, tot = _sb_fwd(qkv, h_sb)
    l_mid, send2, recv2, token = _gather_forward("gather_mid_forward", lands[1:4], 1, send1, recv1, [o_sb])
    o_fx, o_fx_t, o_fx32, lse = _fox_fwd(qkv, cum_col, cum_row, h_fox, h_sb, token)
    g_sb, g_fx, g_out = _gather_wait("gather_mid_wait", l_mid, 1, recv1, send2, recv2, o_fx)
    w_out_full = g_out.reshape(d, d)
    merged, merged_t, a_sb, a_fx = _branch_merge(o_sb, o_fx, g_sb, g_fx, gf, o_fx)
    l_ffn, send2, recv2, token = _gather_forward("gather_ffn_forward", lands[4:6], 4, send1, recv1, [merged])
    mix = _mm_plain("out_proj", "nn", merged, w_out_full, F32, dep=token)
    h1, u2, u2_t = _mid_norms(xs, mix, norm_mix_post, norm_ffn_pre)
    g_gate, g_up = _gather_wait("gather_ffn_wait", l_ffn, 4, recv1, send2, recv2, u2)
    l_down, send2, recv2, token = _gather_forward("gather_down_forward", lands[6:7], 6, send1, recv1, [u2])
    gate, up, act, act_t = _ffn_up(u2, g_gate, g_up, token)
    (g_down,) = _gather_wait("gather_down_wait", l_down, 6, recv1, send2, recv2, act)
    tm, tn = _tile(s, 1024), _tile(d, 1024)
    ff = _matmul("ffn_down", "nn",
                 [(act, pl.BlockSpec((None, tm, fs), lambda i, j, k: (k, i, 0)),
                   g_down, pl.BlockSpec((None, fs, tn), lambda i, j, k: (k, 0, j)))],
                 (s // tm, d // tn, N_DEV), (tm, tn), _sds((s, d), F32), pl.BlockSpec((tm, tn), lambda i, j, k: (i, j)))
    loss_part, dy, dff, dg_ffn_post = _loss_head(h1, ff, target, norm_ffn_post)

    dgate, dup = _ffn_down_bwd(dff, g_down, gate, up)
    dw_down = _matmul("dw_down", "nn",
                      [(act_t, pl.BlockSpec((None, fs, s), lambda j, n, k: (j, 0, 0)),
                        dff, pl.BlockSpec((s, tn), lambda j, n, k: (0, n)))],
                      (N_DEV, d // tn, 1), (fs, tn), _sds((N_DEV, fs, d), BF16),
                      pl.BlockSpec((None, fs, tn), lambda j, n, k: (j, 0, n)))

    def dw_up(name, dact):
        return _matmul(name, "nn",
                       [(u2_t, pl.BlockSpec((tn, s), lambda j, i, k: (i, 0)),
                         dact, pl.BlockSpec((None, s, fs), lambda j, i, k: (j, 0, 0)))],
                       (N_DEV, d // tn, 1), (tn, fs), _sds((N_DEV, d, fs), BF16),
                       pl.BlockSpec((None, tn, fs), lambda j, i, k: (j, i, 0)))

    dw_gate, dw_upw = dw_up("dw_gate", dgate), dw_up("dw_up", dup)
    rs_ffn = _scatter_pairs("ffn", [dw_gate, dw_upw, dw_down])
    a_spec = pl.BlockSpec((None, tm, fs), lambda i, j, k: (k, i, 0))
    b_spec = pl.BlockSpec((None, tn, fs), lambda i, j, k: (k, j, 0))
    du2 = _matmul("du2", "nt", [(dgate, a_spec, g_gate, b_spec), (dup, a_spec, g_up, b_spec)],
                  (s // tm, d // tn, N_DEV), (tm, tn), _sds((s, d), F32), pl.BlockSpec((tm, tn), lambda i, j, k: (i, j)),
                  dep=rs_ffn[4])
    rs_ffn = _scatter_chips("ffn", rs_ffn, du2)
    dh1, dmix, dg_ffn_pre, dg_mix_post = _mid_norms_bwd(dy, du2, h1, mix, norm_ffn_pre, norm_mix_post)

    da_sb, da_fx, dgf = _merge_bwd(dmix, w_out_full, gf, a_sb, a_fx, dep=rs_ffn[4])
    dw_out = _mm_plain("dw_out", "nn", merged_t, dmix, BF16).reshape(N_DEV, cs, d)

    def branch_bwd(tag, da, w_b, o_t, width):
        tb = _tile(width, 1024)
        do = _matmul("do_" + tag, "nt",
                     [(da, pl.BlockSpec((tm, cs), lambda i, j, k: (i, k)),
                       w_b, pl.BlockSpec((None, tb, cs), lambda i, j, k: (k, j, 0)))],
                     (s // tm, width // tb, N_DEV), (tm, tb), _sds((s, width), BF16),
                     pl.BlockSpec((tm, tb), lambda i, j, k: (i, j)))
        dw = _matmul("dw_" + tag, "nn",
                     [(o_t, pl.BlockSpec((width, s), lambda j, i, k: (0, 0)),
                       da, pl.BlockSpec((s, cs), lambda j, i, k: (0, j)))],
                     (N_DEV, 1, 1), (width, cs), _sds((N_DEV, width, cs), BF16),
                     pl.BlockSpec((None, width, cs), lambda j, i, k: (j, 0, 0)))
        return do, dw

    do_sb, dw_sb = branch_bwd("sb", da_sb, g_sb, o_sb_t, d_sb)
    do_fx, dw_fx = branch_bwd("fox", da_fx, g_fx, o_fx_t, d_fox)

    rs_mid = _scatter_pairs("mid", [dw_sb, dw_fx, dw_out])

    dqkv = _sb_bwd(qkv, do_sb, tot, h_sb, rs_mid[4])
    rs_mid = _scatter_chips("mid", rs_mid, dqkv)
    dqkv, dcum = _fox_bwd(dqkv, qkv, do_fx, o_fx32, lse, cum_col, cum_row, h_fox, h_sb, rs_mid[4])
    dgf, db_part = _forget_bwd(dgf, dcum, gf, b_pad, f_blk)
    dw_in = _w_in_grad_parts(_mm_plain("dw_qkv", "nn", u_t, dqkv, BF16), _mm_plain("dw_gates", "nn", u_t, dgf, BF16), lay)
    rs_in = _scatter_pairs("in", [dw_in])
    du = _mm_plain("du_qkv", "nt", dqkv, w_cat, F32, tn=1024, dep=rs_in[4])
    rs_in = _scatter_chips("in", rs_in, du)
    du = _mm_plain("du_gates", "nt", dgf, w_cat, F32, tn=1024, k_off=n_qkv, init=du, dep=rs_in[4])
    dx, dg_mix_pre = _pre_norm_bwd(dh1, du, xs, norm_mix_pre)

    upd = {}

    def update_group(tag, rs, names, after):
        parts = _scatter_end(tag, rs, after)
        for nm, p in zip(names, parts):
            w, m, v = decayed.get(nm, weights[nm])
            upd[nm] = _update("update_" + nm, p, w, m, v, layout=lay if nm == "w_in" else None, decayed=nm in decayed,
                              transposed_out=nm in ("w_ffn_gate", "w_ffn_up"))

    update_group("ffn", rs_ffn, ("w_ffn_gate", "w_ffn_up", "w_ffn_down"), [dx])
    update_group("mid", rs_mid, ("w_branch_sb", "w_branch_fox", "w_out"), [upd[nm][3] for nm in ("w_ffn_gate", "w_ffn_up", "w_ffn_down")])
    update_group("in", rs_in, ("w_in",), [upd[nm][3] for nm in ("w_branch_sb", "w_branch_fox", "w_out")])

    small = ((norm_mix_pre, m_norm_mix_pre, v_norm_mix_pre), (norm_mix_post, m_norm_mix_post, v_norm_mix_post),
             (norm_ffn_pre, m_norm_ffn_pre, v_norm_ffn_pre), (norm_ffn_post, m_norm_ffn_post, v_norm_ffn_post))
    pad_f = ((0, 0), (0, LANES - n_f))
    cat = lambda i: jnp.concatenate([t[i] for t in small] + [jnp.pad((b_forget, m_b_forget, v_b_forget)[i], pad_f)], axis=1)
    sm = _small_update(jnp.concatenate([dg_mix_pre, dg_mix_post, dg_ffn_pre, dg_ffn_post, db_part], axis=1),
                       cat(0), cat(1), cat(2))
    for i, nm in enumerate(("norm_mix_pre", "norm_mix_post", "norm_ffn_pre", "norm_ffn_post")):
        upd[nm] = [o[:, i * d:(i + 1) * d] for o in sm]
    upd["b_forget"] = [o[:, 4 * d:4 * d + n_f] for o in sm]

    loss = lax.psum(loss_part[0, 0], ("x", "y", "c"))
    order = ("norm_mix_pre", "norm_mix_post", "w_in", "b_forget", "w_branch_sb", "w_branch_fox", "w_out",
             "norm_ffn_pre", "norm_ffn_post", "w_ffn_gate", "w_ffn_up", "w_ffn_down")
    return (loss, dx[None]) + tuple(upd[nm][i] for i in range(4) for nm in order)
```

```python
import jax
import jax.numpy as jnp
from jax import lax
from jax.experimental import pallas as pl
from jax.experimental.pallas import tpu as pltpu

F32 = jnp.float32
BF16 = jnp.bfloat16
MESH = pl.DeviceIdType.MESH
ANY = pl.BlockSpec(memory_space=pl.ANY)
HBM = pl.BlockSpec(memory_space=pltpu.HBM)
SEM = pl.BlockSpec(memory_space=pltpu.SEMAPHORE)
EFFECT = pltpu.SideEffectType.DATAFLOW_SIDE_EFFECTING

N_DEV = 8
HEAD_DIM = 128
RMS_EPS = 1e-6
F_PAD = 512
LANES = 128
ATT_TQ = 256
ATT_TK = 256
ATT_HP = 4
NEG_BIG = -1e30
VMEM_LIMIT = 56 * 1024 * 1024

ADAM_LR = 0.001
ADAM_B1 = 0.9
ADAM_B2 = 0.999
ADAM_EPS = 1e-08
ADAM_WD = 0.01
ADAM_STEP = 10

_DIMS = {"nn": ((1,), (0,)), "nt": ((1,), (1,)), "tn": ((0,), (0,))}


def _params(sem):
    return pltpu.CompilerParams(dimension_semantics=sem, vmem_limit_bytes=VMEM_LIMIT)


def _dot(a, b, mode="nn"):
    return lax.dot_general(a.astype(BF16), b.astype(BF16), (_DIMS[mode], ((), ())), preferred_element_type=F32)


def _tile(n, pref):
    if n <= pref:
        return n
    t = (pref // LANES) * LANES
    while n % t:
        t -= LANES
    return t


def _split2(v):
    hi = v.astype(BF16)
    return hi, (v - hi.astype(F32)).astype(BF16)


def _split3(v):
    a = v.astype(BF16)
    r = v - a.astype(F32)
    b = r.astype(BF16)
    return a, b, (r - b.astype(F32)).astype(BF16)


def _tri(n, cmp):
    r = lax.broadcasted_iota(jnp.int32, (n, n), 0)
    c = lax.broadcasted_iota(jnp.int32, (n, n), 1)
    return jnp.where(cmp(r, c), 1.0, 0.0).astype(BF16)


def _lane_pick(v, h):
    lane = lax.broadcasted_iota(jnp.int32, v.shape, 1)
    return jnp.sum(jnp.where(lane == h, v, 0.0), axis=1, keepdims=True)


def _lane_put(ref, rows, h, col):
    old = ref[rows, :]
    lane = lax.broadcasted_iota(jnp.int32, old.shape, 1)
    ref[rows, :] = jnp.where(lane == h, col, old)


def _sigmoid(z):
    return 1.0 / (1.0 + jnp.exp(-z))


def _log_sigmoid(z):
    return jnp.minimum(z, 0.0) - jnp.log(1.0 + jnp.exp(-jnp.abs(z)))


def _sds(shape, dtype):
    return jax.ShapeDtypeStruct(shape, dtype)


def _matmul(name, mode, pairs, grid, acc_shape, out_shape, out_specs, extras=(), epilogue=None, init=None, dep=None):
    n_p, n_e = len(pairs), len(extras)
    nk = grid[-1]
    single = not isinstance(out_shape, (list, tuple))
    n_i = 0 if init is None else 1
    n_d = 0 if dep is None else 1

    one_step = nk == 1 and init is None

    def body(*refs):
        ab = refs[:2 * n_p]
        ex = refs[2 * n_p:2 * n_p + n_e]
        ini = refs[2 * n_p + n_e:2 * n_p + n_e + n_i]
        outs = refs[2 * n_p + n_e + n_i + n_d:len(refs) - (0 if one_step else 1)]

        def finish(total):
            if epilogue is None:
                outs[0][...] = total.astype(outs[0].dtype)
            else:
                epilogue(total, ex, outs)

        t = _dot(ab[0][...], ab[1][...], mode)
        for p in range(1, n_p):
            t = t + _dot(ab[2 * p][...], ab[2 * p + 1][...], mode)
        if one_step:
            finish(t)
            return
        acc = refs[-1]
        k = pl.program_id(len(grid) - 1)

        @pl.when(k == 0)
        def _():
            acc[...] = t if init is None else ini[0][...].astype(F32) + t

        @pl.when(k > 0)
        def _():
            acc[...] += t

        @pl.when(k == nk - 1)
        def _():
            finish(acc[...])

    in_specs = [s for (_, sa, _, sb) in pairs for s in (sa, sb)] + [s for (_, s) in extras]
    args = [v for (a, _, b, _) in pairs for v in (a, b)] + [e for (e, _) in extras]
    if init is not None:
        in_specs.append(init[1])
        args.append(init[0])
    if dep is not None:
        in_specs.append(ANY)
        args.append(dep)
    return pl.pallas_call(
        body, name=name, grid=grid, in_specs=in_specs,
        out_specs=out_specs if single else list(out_specs),
        out_shape=out_shape if single else list(out_shape),
        scratch_shapes=[] if one_step else [pltpu.VMEM(acc_shape, F32)],
        compiler_params=_params(("parallel",) * (len(grid) - 1) + ("arbitrary",)),
    )(*args)


def _mm_plain(name, mode, a, b, out_dtype, *, n_off=0, n=None, k_off=0, tm=1024, tn=1536, tk=2048, init=None, dep=None):
    if mode == "nn":
        (m, kk), nn_ = a.shape, b.shape[1]
    elif mode == "nt":
        (m, kk), nn_ = a.shape, b.shape[0]
    else:
        (kk, m), nn_ = a.shape, b.shape[1]
    n = nn_ if n is None else n
    tm, tn, tk = _tile(m, tm), _tile(n, tn), _tile(kk, tk)
    while n_off % tn or n % tn:
        tn -= LANES
    while k_off % tk or kk % tk:
        tk -= LANES
    off, koff = n_off // tn, k_off // tk
    a_spec = {"nn": pl.BlockSpec((tm, tk), lambda i, j, k: (i, k)),
              "nt": pl.BlockSpec((tm, tk), lambda i, j, k: (i, k)),
              "tn": pl.BlockSpec((tk, tm), lambda i, j, k: (k, i))}[mode]
    b_spec = {"nn": pl.BlockSpec((tk, tn), lambda i, j, k: (k, j + off)),
              "nt": pl.BlockSpec((tn, tk), lambda i, j, k: (j, k + koff)),
              "tn": pl.BlockSpec((tk, tn), lambda i, j, k: (k, j))}[mode]
    o_spec = pl.BlockSpec((tm, tn), lambda i, j, k: (i, j))
    if init is not None:
        init = (init, o_spec)
    return _matmul(name, mode, [(a, a_spec, b, b_spec)], (m // tm, n // tn, kk // tk), (tm, tn),
                   _sds((m, n), out_dtype), o_spec, init=init, dep=dep)


def _rows_call(name, body, ins, outs, s, tr=256, dep=None):
    def spec(v, per_row):
        if per_row == "transposed":
            return pl.BlockSpec((v.shape[0], tr), lambda i: (0, i))
        if per_row:
            return pl.BlockSpec((tr, v.shape[1]), lambda i: (i, 0))
        return pl.BlockSpec(v.shape, lambda i: (0, 0))
    n_in = len(ins)
    deps = [] if dep is None else [dep]

    def with_dep(*refs):
        body(*refs[:n_in], *refs[n_in + len(deps):])

    return pl.pallas_call(
        with_dep, name=name, grid=(s // tr,),
        in_specs=[spec(v, p) for v, p in ins] + [ANY] * len(deps), out_specs=[spec(v, p) for v, p in outs],
        out_shape=[_sds(v.shape, v.dtype) for v, _ in outs],
        compiler_params=_params(("arbitrary",)),
    )(*[v for v, _ in ins], *deps)


def _rsq(v):
    return lax.rsqrt(jnp.mean(v * v, axis=-1, keepdims=True) + RMS_EPS)


def _norm_bwd(dy, v, r, g):
    vh = v * r
    t = dy * g
    dv = r * (t - vh * jnp.mean(t * vh, axis=-1, keepdims=True))
    return dv, jnp.sum(dy * vh, axis=0, keepdims=True)


def _accum(ref, val):
    @pl.when(pl.program_id(0) == 0)
    def _():
        ref[...] = jnp.zeros_like(ref)
    ref[...] += val


def _pre_norm(x, g, dep=None):
    def body(x_ref, g_ref, u_ref, ut_ref):
        v = x_ref[...]
        u = (v * _rsq(v) * g_ref[...]).astype(BF16)
        u_ref[...] = u
        ut_ref[...] = u.T
    s, d = x.shape
    return _rows_call("pre_norm", body, [(x, True), (g, False)],
                      [(_sds((s, d), BF16), True), (_sds((d, s), BF16), "transposed")], s, dep=dep)


def _mid_norms(x, mix, g_post, g_pre):
    def body(x_ref, mix_ref, gp_ref, gn_ref, h_ref, u_ref, ut_ref):
        mv = mix_ref[...]
        h = x_ref[...] + mv * _rsq(mv) * gp_ref[...]
        h_ref[...] = h
        u = (h * _rsq(h) * gn_ref[...]).astype(BF16)
        u_ref[...] = u
        ut_ref[...] = u.T
    s, d = x.shape
    return _rows_call("mid_norms", body, [(x, True), (mix, True), (g_post, False), (g_pre, False)],
                      [(_sds((s, d), F32), True), (_sds((s, d), BF16), True), (_sds((d, s), BF16), "transposed")], s)


def _loss_head(h1, ff, target, g):
    s, d = h1.shape

    def body(h_ref, ff_ref, t_ref, g_ref, loss_ref, dy_ref, dff_ref, dg_ref):
        fv = ff_ref[...]
        r = _rsq(fv)
        err = h_ref[...] + fv * r * g_ref[...] - t_ref[...]
        part = 0.5 * jnp.sum(jnp.mean(err * err, axis=-1, keepdims=True), axis=0, keepdims=True)
        _accum(loss_ref, jnp.broadcast_to(part, loss_ref.shape))
        dy = err * (1.0 / d)
        dy_ref[...] = dy
        dff, dg = _norm_bwd(dy, fv, r, g_ref[...])
        dff_ref[...] = dff.astype(BF16)
        _accum(dg_ref, dg)

    return _rows_call("loss_head", body, [(h1, True), (ff, True), (target, True), (g, False)],
                      [(_sds((1, LANES), F32), False), (_sds((s, d), F32), True),
                       (_sds((s, d), BF16), True), (_sds((1, d), F32), False)], s)


def _mid_norms_bwd(dy, du2, h1, mix, g_pre, g_post):
    s, d = dy.shape

    def body(dy_ref, du_ref, h_ref, mix_ref, gn_ref, gp_ref, dh_ref, dmix_ref, dgn_ref, dgp_ref):
        h = h_ref[...]
        dh, dgn = _norm_bwd(du_ref[...], h, _rsq(h), gn_ref[...])
        dh = dh + dy_ref[...]
        dh_ref[...] = dh
        _accum(dgn_ref, dgn)
        mv = mix_ref[...]
        dmix, dgp = _norm_bwd(dh, mv, _rsq(mv), gp_ref[...])
        dmix_ref[...] = dmix.astype(BF16)
        _accum(dgp_ref, dgp)

    return _rows_call("mid_norms_bwd", body,
                      [(dy, True), (du2, True), (h1, True), (mix, True), (g_pre, False), (g_post, False)],
                      [(_sds((s, d), F32), True), (_sds((s, d), BF16), True),
                       (_sds((1, d), F32), False), (_sds((1, d), F32), False)], s)


def _pre_norm_bwd(dh1, du, x, g, dep=None):
    s, d = x.shape

    def body(dh_ref, du_ref, x_ref, g_ref, dx_ref, dg_ref):
        v = x_ref[...]
        dv, dg = _norm_bwd(du_ref[...], v, _rsq(v), g_ref[...])
        dx_ref[...] = dh_ref[...] + dv
        _accum(dg_ref, dg)

    return _rows_call("pre_norm_bwd", body, [(dh1, True), (du, True), (x, True), (g, False)],
                      [(_sds((s, d), F32), True), (_sds((1, d), F32), False)], s, dep=dep)


def _forget_fwd(gf, b_pad, f_blk):
    s = gf.shape[0]
    tb = ATT_TK
    nb = s // tb

    def body(f_ref, b_ref, col_ref, row_ref):
        incl = _tri(tb, lambda r, c: c <= r)
        carry = jnp.zeros((1, LANES), F32)
        for i in range(nb):
            lf = _log_sigmoid(f_ref[pl.ds(i * tb, tb), :] + b_ref[...])
            parts = _split3(lf)
            cum = carry + _dot(incl, parts[0]) + _dot(incl, parts[1]) + _dot(incl, parts[2])
            col_ref[pl.ds(i * tb, tb), :] = cum
            row_ref[i] = cum.T
            carry = carry + jnp.sum(lf, axis=0, keepdims=True)

    return pl.pallas_call(
        body, name="forget_fwd", grid=(1,),
        in_specs=[pl.BlockSpec((s, LANES), lambda i: (0, f_blk)), pl.BlockSpec((1, LANES), lambda i: (0, 0))],
        out_specs=[pl.BlockSpec((s, LANES), lambda i: (0, 0)), pl.BlockSpec((nb, LANES, tb), lambda i: (0, 0, 0))],
        out_shape=[_sds((s, LANES), F32), _sds((nb, LANES, tb), F32)],
        compiler_params=_params(("arbitrary",)),
    )(gf, b_pad)


def _forget_bwd(dgf, dcum, gf, b_pad, f_blk):
    s = gf.shape[0]
    tb = ATT_TK
    nb = s // tb
    sec = dgf.shape[1] // F_PAD - 1

    def body(dgf_hbm, dc_ref, f_ref, b_ref, out_ref, db_ref):
        del dgf_hbm
        incl = _tri(tb, lambda r, c: c >= r)
        carry = jnp.zeros((1, LANES), F32)
        db = jnp.zeros((1, LANES), F32)
        out_ref[...] = jnp.zeros_like(out_ref)
        for i in reversed(range(nb)):
            dc = dc_ref[pl.ds(i * tb, tb), :]
            parts = _split3(dc)
            dlf = carry + _dot(incl, parts[0]) + _dot(incl, parts[1]) + _dot(incl, parts[2])
            z = f_ref[pl.ds(i * tb, tb), :] + b_ref[...]
            df = dlf * _sigmoid(-z)
            out_ref[pl.ds(i * tb, tb), pl.ds(0, LANES)] = df.astype(BF16)
            db = db + jnp.sum(df, axis=0, keepdims=True)
            carry = carry + jnp.sum(dc, axis=0, keepdims=True)
        db_ref[...] = db

    return pl.pallas_call(
        body, name="forget_bwd", grid=(1,),
        in_specs=[ANY, pl.BlockSpec((s, LANES), lambda i: (0, 0)),
                  pl.BlockSpec((s, LANES), lambda i: (0, f_blk)), pl.BlockSpec((1, LANES), lambda i: (0, 0))],
        out_specs=[pl.BlockSpec((s, F_PAD), lambda i: (0, sec)), pl.BlockSpec((1, LANES), lambda i: (0, 0))],
        out_shape=[_sds(dgf.shape, BF16), _sds((1, LANES), F32)],
        input_output_aliases={0: 0},
        compiler_params=_params(("arbitrary",)),
    )(dgf, dcum, gf, b_pad)


def _diag_mask(strict):
    r = lax.broadcasted_iota(jnp.int32, (ATT_TQ, ATT_TK), 0)
    c = lax.broadcasted_iota(jnp.int32, (ATT_TQ, ATT_TK), 1)
    return c < r if strict else c <= r


def _qkv_specs(hb0, s):
    specs = []
    for j in range(ATT_HP):
        def col(g, j=j):
            return 3 * (hb0 + ATT_HP * g + j)
        specs += [pl.BlockSpec((ATT_TQ, HEAD_DIM), lambda g, i, col=col: (i, col(g))),
                  pl.BlockSpec((s, HEAD_DIM), lambda g, i, col=col: (0, col(g) + 1)),
                  pl.BlockSpec((s, HEAD_DIM), lambda g, i, col=col: (0, col(g) + 2))]
    return specs


def _head_cols(j):
    return pl.ds(j * HEAD_DIM, HEAD_DIM)


def _sb_fwd(qkv, n_heads):
    s = qkv.shape[0]
    scale = HEAD_DIM ** -0.5
    tq, tk = ATT_TQ, ATT_TK
    heads = range(ATT_HP)

    def body(*refs):
        qkv_refs, (o_ref, ot_ref, tot_ref) = refs[:3 * ATT_HP], refs[3 * ATT_HP:]
        g, i = pl.program_id(0), pl.program_id(1)

        @pl.when((g == 0) & (i == 0))
        def _():
            tot_ref[...] = jnp.zeros_like(tot_ref)

        qs = [qkv_refs[3 * j][...] for j in heads]
        upper = _tri(tk, lambda r, c: r > c)

        def tile(kj, carry, mask):
            rows = pl.ds(pl.multiple_of(kj * tk, tk), tk)
            z = [_dot(qs[j], qkv_refs[3 * j + 1][rows, :], "nt") * scale for j in heads]
            lsz = [_log_sigmoid(z[j]) for j in heads]
            lk = [lsz[j] - z[j] if mask is None else jnp.where(mask, lsz[j] - z[j], 0.0) for j in heads]
            parts = [_split2(lk[j]) for j in heads]
            above = [carry[j][0] + _dot(parts[j][0], upper) + _dot(parts[j][1], upper) for j in heads]
            w = [jnp.exp(lsz[j] + above[j]) for j in heads]
            if mask is not None:
                w = [jnp.where(mask, w[j], 0.0) for j in heads]
            return tuple((carry[j][0] + jnp.sum(lk[j], axis=1, keepdims=True),
                          carry[j][1] + _dot(w[j], qkv_refs[3 * j + 2][rows, :])) for j in heads)

        carry = tile(i, tuple((jnp.zeros((tq, 1), F32), jnp.zeros((tq, HEAD_DIM), F32)) for _ in heads), _diag_mask(True))
        carry = lax.fori_loop(0, i, lambda n, cr: tile(i - 1 - n, cr, None), carry)
        q_rows = pl.ds(pl.multiple_of(i * tq, tq), tq)
        for j in heads:
            c, acc = carry[j]
            o = acc.astype(BF16)
            o_ref[:, _head_cols(j)] = o
            ot_ref[_head_cols(j), :] = o.T
            _lane_put(tot_ref, q_rows, ATT_HP * g + j, c)

    wide = ATT_HP * HEAD_DIM
    return pl.pallas_call(
        body, name="sb_fwd", grid=(n_heads // ATT_HP, s // tq),
        in_specs=_qkv_specs(0, s),
        out_specs=[pl.BlockSpec((tq, wide), lambda g, i: (i, g)), pl.BlockSpec((wide, tq), lambda g, i: (g, i)),
                   pl.BlockSpec((s, LANES), lambda g, i: (0, 0))],
        out_shape=[_sds((s, n_heads * HEAD_DIM), BF16), _sds((n_heads * HEAD_DIM, s), BF16), _sds((s, LANES), F32)],
        compiler_params=_params(("arbitrary", "arbitrary")),
    )(*[qkv] * (3 * ATT_HP))


def _sb_bwd(qkv, do, tot, n_heads, dep):
    s = qkv.shape[0]
    scale = HEAD_DIM ** -0.5
    tq, tk = ATT_TQ, ATT_TK
    nq = s // tq
    hd = HEAD_DIM

    heads = range(ATT_HP)

    def body(*refs):
        qkv_refs = refs[:3 * ATT_HP]
        do_ref, tot_ref, _, out_ref, dk_acc, dv_acc = refs[3 * ATT_HP:]
        g, i = pl.program_id(0), pl.program_id(1)

        @pl.when(i == 0)
        def _():
            dk_acc[...] = jnp.zeros_like(dk_acc)
            dv_acc[...] = jnp.zeros_like(dv_acc)

        qs = [qkv_refs[3 * j][...] for j in heads]
        douts = [do_ref[:, _head_cols(j)] for j in heads]
        totals = [_lane_pick(tot_ref[...], ATT_HP * g + j) for j in heads]
        incl = _tri(tk, lambda r, c: r <= c)
        excl = _tri(tk, lambda r, c: r < c)

        def tile(kj, carry, mask):
            rows = pl.ds(pl.multiple_of(kj * tk, tk), tk)
            k_t = [qkv_refs[3 * j + 1][rows, :] for j in heads]
            z = [_dot(qs[j], k_t[j], "nt") * scale for j in heads]
            dw = [_dot(douts[j], qkv_refs[3 * j + 2][rows, :], "nt") for j in heads]
            lsz = [_log_sigmoid(z[j]) for j in heads]
            lk = [lsz[j] - z[j] if mask is None else jnp.where(mask, lsz[j] - z[j], 0.0) for j in heads]
            parts = [_split2(lk[j]) for j in heads]
            below = [carry[j][0] + _dot(parts[j][0], incl) + _dot(parts[j][1], incl) for j in heads]
            w = [jnp.exp(lsz[j] + (totals[j] - below[j])) for j in heads]
            if mask is not None:
                w = [jnp.where(mask, w[j], 0.0) for j in heads]
            e = [dw[j] * w[j] for j in heads]
            parts = [_split2(e[j]) for j in heads]
            e_before = [carry[j][1] + _dot(parts[j][0], excl) + _dot(parts[j][1], excl) for j in heads]
            sg = [jnp.exp(lsz[j]) for j in heads]
            dz = [e[j] * (1.0 - sg[j]) - e_before[j] * sg[j] for j in heads]
            if mask is not None:
                dz = [jnp.where(mask, dz[j], 0.0) for j in heads]
            dz = [(dz[j] * scale).astype(BF16) for j in heads]
            for j in heads:
                dk_acc[j, rows, :] += _dot(dz[j], qs[j], "tn")
                dv_acc[j, rows, :] += _dot(w[j], douts[j], "tn")
            return tuple((carry[j][0] + jnp.sum(lk[j], axis=1, keepdims=True),
                          carry[j][1] + jnp.sum(e[j], axis=1, keepdims=True),
                          carry[j][2] + _dot(dz[j], k_t[j])) for j in heads)

        zero = jnp.zeros((tq, 1), F32)
        carry = lax.fori_loop(0, i, lambda kj, cr: tile(kj, cr, None),
                              tuple((zero, zero, jnp.zeros((tq, hd), F32)) for _ in heads))
        carry = tile(i, carry, _diag_mask(True))
        for j in heads:
            out_ref[pl.ds(pl.multiple_of(i * tq, tq), tq), pl.ds(3 * j * hd, hd)] = carry[j][2].astype(BF16)

        @pl.when(i == nq - 1)
        def _():
            for j in heads:
                out_ref[:, pl.ds((3 * j + 1) * hd, hd)] = dk_acc[j].astype(BF16)
                out_ref[:, pl.ds((3 * j + 2) * hd, hd)] = dv_acc[j].astype(BF16)

    wide = ATT_HP * hd
    return pl.pallas_call(
        body, name="sb_bwd", grid=(n_heads // ATT_HP, nq),
        in_specs=_qkv_specs(0, s) + [pl.BlockSpec((tq, wide), lambda g, i: (i, g)),
                                     pl.BlockSpec((tq, LANES), lambda g, i: (i, 0)), ANY],
        out_specs=pl.BlockSpec((s, 3 * wide), lambda g, i: (0, g)),
        out_shape=_sds(qkv.shape, BF16),
        scratch_shapes=[pltpu.VMEM((ATT_HP, s, hd), F32), pltpu.VMEM((ATT_HP, s, hd), F32)],
        compiler_params=_params(("arbitrary", "arbitrary")),
    )(*[qkv] * (3 * ATT_HP), do, tot, dep)


def _fox_fwd(qkv, cum_col, cum_row, n_heads, hb0, dep):
    s = qkv.shape[0]
    scale = HEAD_DIM ** -0.5
    tq, tk = ATT_TQ, ATT_TK

    heads = range(ATT_HP)

    def body(*refs):
        qkv_refs = refs[:3 * ATT_HP]
        cc_ref, cr_ref, _, o_ref, ot_ref, o32_ref, lse_ref = refs[3 * ATT_HP:]
        g, i = pl.program_id(0), pl.program_id(1)

        @pl.when((g == 0) & (i == 0))
        def _():
            lse_ref[...] = jnp.zeros_like(lse_ref)

        qs = [qkv_refs[3 * j][...] for j in heads]
        cqs = [_lane_pick(cc_ref[...], ATT_HP * g + j) for j in heads]

        def tile(kj, carry, mask):
            rows = pl.ds(pl.multiple_of(kj * tk, tk), tk)
            sc = [_dot(qs[j], qkv_refs[3 * j + 1][rows, :], "nt") * scale + cqs[j]
                  - cr_ref[kj, pl.ds(ATT_HP * g + j, 1), :] for j in heads]
            if mask is not None:
                sc = [jnp.where(mask, sc[j], NEG_BIG) for j in heads]
            m_new = [jnp.maximum(carry[j][0], jnp.max(sc[j], axis=1, keepdims=True)) for j in heads]
            p = [jnp.exp(sc[j] - m_new[j]) for j in heads]
            alpha = [jnp.exp(carry[j][0] - m_new[j]) for j in heads]
            parts = [_split2(p[j]) for j in heads]
            v_t = [qkv_refs[3 * j + 2][rows, :] for j in heads]
            pv = [_dot(parts[j][0], v_t[j]) + _dot(parts[j][1], v_t[j]) for j in heads]
            return tuple((m_new[j], alpha[j] * carry[j][1] + jnp.sum(p[j], axis=1, keepdims=True),
                          alpha[j] * carry[j][2] + pv[j]) for j in heads)

        carry = tuple((jnp.full((tq, 1), NEG_BIG, F32), jnp.zeros((tq, 1), F32), jnp.zeros((tq, HEAD_DIM), F32))
                      for _ in heads)
        carry = lax.fori_loop(0, i, lambda kj, cr: tile(kj, cr, None), carry)
        carry = tile(i, carry, _diag_mask(False))
        q_rows = pl.ds(pl.multiple_of(i * tq, tq), tq)
        for j in heads:
            m, l, acc = carry[j]
            o = acc / l
            o_ref[:, _head_cols(j)] = o.astype(BF16)
            ot_ref[_head_cols(j), :] = o.astype(BF16).T
            o32_ref[:, _head_cols(j)] = o
            _lane_put(lse_ref, q_rows, ATT_HP * g + j, m + jnp.log(l))

    nb = cum_row.shape[0]
    wide = ATT_HP * HEAD_DIM
    return pl.pallas_call(
        body, name="fox_fwd", grid=(n_heads // ATT_HP, s // tq),
        in_specs=_qkv_specs(hb0, s) + [pl.BlockSpec((tq, LANES), lambda g, i: (i, 0)),
                                       pl.BlockSpec((nb, 8, tk), lambda g, i: (0, 0, 0)), ANY],
        out_specs=[pl.BlockSpec((tq, wide), lambda g, i: (i, g)), pl.BlockSpec((wide, tq), lambda g, i: (g, i)),
                   pl.BlockSpec((tq, wide), lambda g, i: (i, g)), pl.BlockSpec((s, LANES), lambda g, i: (0, 0))],
        out_shape=[_sds((s, n_heads * HEAD_DIM), BF16), _sds((n_heads * HEAD_DIM, s), BF16),
                   _sds((s, n_heads * HEAD_DIM), F32), _sds((s, LANES), F32)],
        compiler_params=_params(("arbitrary", "arbitrary")),
    )(*[qkv] * (3 * ATT_HP), cum_col, cum_row, dep)


def _fox_bwd(dqkv, qkv, do, o, lse, cum_col, cum_row, n_heads, hb0, dep):
    s = qkv.shape[0]
    scale = HEAD_DIM ** -0.5
    tq, tk = ATT_TQ, ATT_TK
    nq = s // tq
    hd = HEAD_DIM

    heads = range(ATT_HP)
    assert hb0 % ATT_HP == 0

    def body(*refs):
        qkv_refs = refs[1:1 + 3 * ATT_HP]
        do_ref, o_ref, lse_ref, cc_ref, cr_ref, _, out_ref, dc_ref, dk_acc, dv_acc, col_acc = refs[1 + 3 * ATT_HP:]
        g, i = pl.program_id(0), pl.program_id(1)

        @pl.when((g == 0) & (i == 0))
        def _():
            dc_ref[...] = jnp.zeros_like(dc_ref)

        @pl.when(i == 0)
        def _():
            dk_acc[...] = jnp.zeros_like(dk_acc)
            dv_acc[...] = jnp.zeros_like(dv_acc)
            col_acc[...] = jnp.zeros_like(col_acc)

        qs = [qkv_refs[3 * j][...] for j in heads]
        douts = [do_ref[:, _head_cols(j)] for j in heads]
        deltas = [jnp.sum(douts[j].astype(F32) * o_ref[:, _head_cols(j)], axis=1, keepdims=True) for j in heads]
        shifts = [_lane_pick(cc_ref[...], ATT_HP * g + j) - _lane_pick(lse_ref[...], ATT_HP * g + j) for j in heads]

        def tile(kj, carry, mask):
            rows = pl.ds(pl.multiple_of(kj * tk, tk), tk)
            k_t = [qkv_refs[3 * j + 1][rows, :] for j in heads]
            sc = [_dot(qs[j], k_t[j], "nt") * scale + shifts[j] - cr_ref[kj, pl.ds(ATT_HP * g + j, 1), :] for j in heads]
            dp = [_dot(douts[j], qkv_refs[3 * j + 2][rows, :], "nt") for j in heads]
            p = [jnp.exp(sc[j]) for j in heads]
            if mask is not None:
                p = [jnp.where(mask, p[j], 0.0) for j in heads]
            ds_f = [p[j] * (dp[j] - deltas[j]) for j in heads]
            ds = [(ds_f[j] * scale).astype(BF16) for j in heads]
            for j in heads:
                col_acc[j, kj] += jnp.broadcast_to(jnp.sum(ds_f[j], axis=0, keepdims=True), (8, tk))
                dk_acc[j, rows, :] += _dot(ds[j], qs[j], "tn")
                dv_acc[j, rows, :] += _dot(p[j], douts[j], "tn")
            return tuple((carry[j][0] + _dot(ds[j], k_t[j]), carry[j][1] + jnp.sum(ds_f[j], axis=1, keepdims=True))
                         for j in heads)

        carry = lax.fori_loop(0, i, lambda kj, cr: tile(kj, cr, None),
                              tuple((jnp.zeros((tq, hd), F32), jnp.zeros((tq, 1), F32)) for _ in heads))
        carry = tile(i, carry, _diag_mask(False))
        q_rows = pl.ds(pl.multiple_of(i * tq, tq), tq)
        for j in heads:
            out_ref[q_rows, pl.ds(3 * j * hd, hd)] = carry[j][0].astype(BF16)
            _lane_put(dc_ref, q_rows, ATT_HP * g + j, carry[j][1])

        @pl.when(i == nq - 1)
        def _():
            lane = lax.broadcasted_iota(jnp.int32, (tk, LANES), 1)
            for j in heads:
                out_ref[:, pl.ds((3 * j + 1) * hd, hd)] = dk_acc[j].astype(BF16)
                out_ref[:, pl.ds((3 * j + 2) * hd, hd)] = dv_acc[j].astype(BF16)
                for kj in range(nb):
                    col = jnp.broadcast_to(col_acc[j, kj][0:1, :], (LANES, tk)).T
                    old = dc_ref[pl.ds(kj * tk, tk), :]
                    dc_ref[pl.ds(kj * tk, tk), :] = jnp.where(lane == ATT_HP * g + j, old - col, old)

    nb = cum_row.shape[0]
    wide = ATT_HP * hd
    return pl.pallas_call(
        body, name="fox_bwd", grid=(n_heads // ATT_HP, nq),
        in_specs=[ANY] + _qkv_specs(hb0, s) + [
            pl.BlockSpec((tq, wide), lambda g, i: (i, g)), pl.BlockSpec((tq, wide), lambda g, i: (i, g)),
            pl.BlockSpec((tq, LANES), lambda g, i: (i, 0)), pl.BlockSpec((tq, LANES), lambda g, i: (i, 0)),
            pl.BlockSpec((nb, 8, tk), lambda g, i: (0, 0, 0)), ANY],
        out_specs=[pl.BlockSpec((s, 3 * wide), lambda g, i: (0, hb0 // ATT_HP + g)),
                   pl.BlockSpec((s, LANES), lambda g, i: (0, 0))],
        out_shape=[_sds(dqkv.shape, BF16), _sds((s, LANES), F32)],
        scratch_shapes=[pltpu.VMEM((ATT_HP, s, hd), F32), pltpu.VMEM((ATT_HP, s, hd), F32),
                        pltpu.VMEM((ATT_HP, s // tk, 8, tk), F32)],
        input_output_aliases={0: 0},
        compiler_params=_params(("arbitrary", "arbitrary")),
    )(dqkv, *[qkv] * (3 * ATT_HP), do, o, lse, cum_col, cum_row, dep)


def _branch_merge(o_sb, o_fx, w_sb, w_fx, gf, dep, tm=1024):
    s = o_sb.shape[0]
    cs = w_sb.shape[2]
    tm = _tile(s, tm)

    def body(osb_ref, ofx_ref, wsb_ref, wfx_ref, g_ref, dep_ref, merged_ref, mt_ref, asb_ref, afx_ref):
        del dep_ref
        a_sb = _dot(osb_ref[...], wsb_ref[...])
        a_fx = _dot(ofx_ref[...], wfx_ref[...])
        g = g_ref[...]
        merged = (_sigmoid(g[:, :cs]) * a_sb + _sigmoid(g[:, cs:]) * a_fx).astype(BF16)
        merged_ref[...] = merged
        mt_ref[...] = merged.T
        asb_ref[...] = a_sb.astype(BF16)
        afx_ref[...] = a_fx.astype(BF16)

    blk = pl.BlockSpec((tm, cs), lambda i, j: (i, j))
    out = _sds((s, N_DEV * cs), BF16)
    return pl.pallas_call(
        body, name="branch_merge", grid=(s // tm, N_DEV),
        in_specs=[pl.BlockSpec((tm, o_sb.shape[1]), lambda i, j: (i, 0)),
                  pl.BlockSpec((tm, o_fx.shape[1]), lambda i, j: (i, 0)),
                  pl.BlockSpec((None,) + w_sb.shape[1:], lambda i, j: (j, 0, 0)),
                  pl.BlockSpec((None,) + w_fx.shape[1:], lambda i, j: (j, 0, 0)),
                  pl.BlockSpec((tm, 2 * cs), lambda i, j: (i, j)), ANY],
        out_specs=[blk, pl.BlockSpec((cs, tm), lambda i, j: (j, i)), blk, blk],
        out_shape=[out, _sds((N_DEV * cs, s), BF16), out, out],
        compiler_params=_params(("parallel", "arbitrary")),
    )(o_sb, o_fx, w_sb, w_fx, gf, dep)


def _merge_bwd(dmix, w_out, gf, a_sb, a_fx, tm=1024, tk=2048, dep=None):
    s, d = dmix.shape
    cs = d // N_DEV
    tm, tk = _tile(s, tm), _tile(d, tk)

    def epilogue(acc, ex, outs):
        g, a_sb, a_fx = ex[0][...], ex[1][...].astype(F32), ex[2][...].astype(F32)
        s_sb, s_fx = _sigmoid(g[:, :cs]), _sigmoid(g[:, cs:])
        outs[0][...] = (acc * s_sb).astype(BF16)
        outs[1][...] = (acc * s_fx).astype(BF16)
        outs[2][...] = jnp.concatenate([acc * a_sb * s_sb * (1.0 - s_sb), acc * a_fx * s_fx * (1.0 - s_fx)],
                                       axis=1).astype(BF16)

    blk = pl.BlockSpec((tm, cs), lambda i, j, k: (i, j))
    wide = pl.BlockSpec((tm, 2 * cs), lambda i, j, k: (i, j))
    return _matmul(
        "merge_bwd", "nt",
        [(dmix, pl.BlockSpec((tm, tk), lambda i, j, k: (i, k)), w_out, pl.BlockSpec((cs, tk), lambda i, j, k: (j, k)))],
        (s // tm, N_DEV, d // tk), (tm, cs),
        [_sds((s, d), BF16), _sds((s, d), BF16), _sds(gf.shape, BF16)], [blk, blk, wide],
        extras=[(gf, wide), (a_sb, blk), (a_fx, blk)], epilogue=epilogue, dep=dep)


def _ffn_up(u2, w_gate, w_up, dep, tm=1024):
    s, d = u2.shape
    fs = w_gate.shape[2]
    tm = _tile(s, tm)

    def body(u_ref, wg_ref, wu_ref, dep_ref, gate_ref, up_ref, act_ref, actt_ref):
        del dep_ref
        u = u_ref[...]
        gate = _dot(u, wg_ref[...])
        up = _dot(u, wu_ref[...])
        gate_ref[...] = gate
        up_ref[...] = up
        act = (gate * _sigmoid(gate) * up).astype(BF16)
        act_ref[...] = act
        actt_ref[...] = act.T

    w_spec = pl.BlockSpec((None, d, fs), lambda i, j: (j, 0, 0))
    o_spec = pl.BlockSpec((None, tm, fs), lambda i, j: (j, i, 0))
    return pl.pallas_call(
        body, name="ffn_up", grid=(s // tm, N_DEV),
        in_specs=[pl.BlockSpec((tm, d), lambda i, j: (i, 0)), w_spec, w_spec, ANY],
        out_specs=[o_spec, o_spec, o_spec, pl.BlockSpec((None, fs, tm), lambda i, j: (j, 0, i))],
        out_shape=[_sds((N_DEV, s, fs), F32), _sds((N_DEV, s, fs), F32), _sds((N_DEV, s, fs), BF16),
                   _sds((N_DEV, fs, s), BF16)],
        compiler_params=_params(("parallel", "arbitrary")),
    )(u2, w_gate, w_up, dep)


def _ffn_down_bwd(dff, w_down, gate, up, tm=1024):
    s, d = dff.shape
    fs = w_down.shape[1]
    tm = _tile(s, tm)

    def body(dff_ref, wd_ref, gate_ref, up_ref, dgate_ref, dup_ref):
        dact = _dot(dff_ref[...], wd_ref[...], "nt")
        gate = gate_ref[...]
        sg = _sigmoid(gate)
        dup_ref[...] = (dact * gate * sg).astype(BF16)
        dgate_ref[...] = (dact * up_ref[...] * sg * (1.0 + gate * (1.0 - sg))).astype(BF16)

    a_spec = pl.BlockSpec((None, tm, fs), lambda i, j: (j, i, 0))
    return pl.pallas_call(
        body, name="ffn_down_bwd", grid=(s // tm, N_DEV),
        in_specs=[pl.BlockSpec((tm, d), lambda i, j: (i, 0)), pl.BlockSpec((None, fs, d), lambda i, j: (j, 0, 0)),
                  a_spec, a_spec],
        out_specs=[a_spec, a_spec],
        out_shape=[_sds((N_DEV, s, fs), BF16), _sds((N_DEV, s, fs), BF16)],
        compiler_params=_params(("parallel", "arbitrary")),
    )(dff, w_down, gate, up)


def _mesh_place():
    x, y, c = lax.axis_index("x"), lax.axis_index("y"), lax.axis_index("c")
    peers = []
    for d in range(1, N_DEV):
        px = 1 - x if d & 4 else x
        py = 1 - y if d & 2 else y
        pc = 1 - c if d & 1 else c
        peers.append((d, (px, py, pc), 4 * px + 2 * py + pc))
    return 4 * x + 2 * y + c, peers


def _flat_me():
    return 4 * lax.axis_index("x") + 2 * lax.axis_index("y") + lax.axis_index("c")


def _in_hbm(a):
    return pltpu.with_memory_space_constraint(a, pltpu.HBM)


def _pair_plan():
    x, y, c = lax.axis_index("x"), lax.axis_index("y"), lax.axis_index("c")
    return [(2 * q + (1 - c), q, q, (x, y, 1 - c)) for q in range(4)]


def _chip_plan():
    x, y, c = lax.axis_index("x"), lax.axis_index("y"), lax.axis_index("c")
    plan = []
    for fx, fy in ((1, 0), (0, 1), (1, 1)):
        cx, cy = (1 - x if fx else x), (1 - y if fy else y)
        plan.append((2 * cx + cy, 2 * x + y, 2 * cx + cy, (cx, cy, c)))
    return plan


def _split_start(name, srcs, lands, plan, k):
    n = len(srcs)

    def body(*refs):
        ins, lnd = refs[:n], refs[n:2 * n]
        send, recv, token = refs[2 * n], refs[2 * n + 1], refs[-1]
        copies = plan()
        for a in range(n):
            for t, (src, dst, _, dev) in enumerate(copies):
                pltpu.make_async_remote_copy(src_ref=ins[a].at[src], dst_ref=lnd[a].at[dst], send_sem=send.at[k * a + t],
                                             recv_sem=recv.at[k * a + t], device_id=dev, device_id_type=MESH).start()
        token[...] = jnp.zeros_like(token)

    res = pl.pallas_call(
        body, name=name,
        out_shape=[pltpu.SemaphoreType.DMA((n * k,)), pltpu.SemaphoreType.DMA((n * k,))]
        + [pltpu.HBM(a.shape, a.dtype) for a in list(srcs) + list(lands)] + [_sds((8, LANES), F32)],
        in_specs=[HBM] * (2 * n), out_specs=[SEM, SEM] + [HBM] * (2 * n) + [pl.BlockSpec(memory_space=pltpu.VMEM)],
        input_output_aliases={i: 2 + i for i in range(2 * n)},
        compiler_params=pltpu.CompilerParams(has_side_effects=EFFECT),
    )(*[_in_hbm(a) for a in srcs], *[_in_hbm(a) for a in lands])
    return res[0], res[1], res[2:2 + n], res[2 + n:2 + 2 * n], res[-1]


def _split_wait(name, send, recv, srcs, lands, plan, k, after):
    n = len(srcs)

    def body(*refs):
        ins, lnd = refs[:n], refs[n:2 * n]
        send_sem, recv_sem = refs[2 * n], refs[2 * n + 1]
        copies = plan()
        for a in range(n):
            for t, (src, _, dst, dev) in enumerate(copies):
                cp = pltpu.make_async_remote_copy(src_ref=ins[a].at[src], dst_ref=lnd[a].at[dst], send_sem=send_sem.at[k * a + t],
                                                  recv_sem=recv_sem.at[k * a + t], device_id=dev, device_id_type=MESH)
                cp.wait_send()
                cp.wait_recv()

    res = pl.pallas_call(
        body, name=name,
        out_shape=[pltpu.HBM(a.shape, a.dtype) for a in list(srcs) + list(lands)],
        in_specs=[HBM] * (2 * n) + [SEM, SEM] + [ANY] * len(after), out_specs=[HBM] * (2 * n),
        input_output_aliases={i: i for i in range(2 * n)},
        compiler_params=pltpu.CompilerParams(has_side_effects=EFFECT),
    )(*srcs, *lands, send, recv, *after)
    return res[:n], res[n:]


def _pair_add(name, parts, land):
    _, r, cols = parts.shape
    tr = max(16, min(r, ((1 << 20) // (2 * cols)) // 16 * 16))
    while r % tr:
        tr -= 16

    def body(c_ref, p_ref, l_ref, o_ref):
        del c_ref
        o_ref[...] = (p_ref[...].astype(F32) + l_ref[...].astype(F32)).astype(BF16)

    blk = pl.BlockSpec((None, tr, cols), lambda q, i, c_ref: (q, i, 0))
    return pl.pallas_call(
        body, name=name,
        grid_spec=pltpu.PrefetchScalarGridSpec(
            num_scalar_prefetch=1, grid=(4, r // tr),
            in_specs=[pl.BlockSpec((None, tr, cols), lambda q, i, c_ref: (2 * q + c_ref[0], i, 0)), blk], out_specs=blk),
        out_shape=_sds((4, r, cols), BF16),
        compiler_params=_params(("parallel", "parallel")),
    )(jnp.reshape(lax.axis_index("c"), (1,)).astype(jnp.int32), parts, land)


def _scatter_pairs(tag, parts):
    lands = [lax.empty((4,) + a.shape[1:], a.dtype) for a in parts]
    return _split_start("pair_" + tag, parts, lands, _pair_plan, 4)


def _scatter_chips(tag, started, after):
    send, recv, parts, lands, _ = started
    parts, lands = _split_wait("pair_" + tag + "_wait", send, recv, parts, lands, _pair_plan, 4, [after])
    sums = [_pair_add("pair_" + tag + "_add%d" % a, p, l) for a, (p, l) in enumerate(zip(parts, lands))]
    chip = 2 * lax.axis_index("x") + lax.axis_index("y")
    final = [lax.dynamic_update_slice_in_dim(lax.empty(v.shape, v.dtype), lax.dynamic_slice_in_dim(v, chip, 1, 0), chip, 0)
             for v in sums]
    return _split_start("chips_" + tag, sums, final, _chip_plan, 3)


def _scatter_end(tag, started, after):
    send, recv, sums, final, _ = started
    return _split_wait("chips_" + tag + "_wait", send, recv, sums, final, _chip_plan, 3, after)[1]


def _gather_targets():
    x, y, c = lax.axis_index("x"), lax.axis_index("y"), lax.axis_index("c")
    chips = [(x, y), (1 - x, y), (x, 1 - y), (1 - x, 1 - y)]
    same = [((cx, cy, c), 4 * cx + 2 * cy + c) for cx, cy in chips]
    other = [((cx, cy, 1 - c), 4 * cx + 2 * cy + 1 - c) for cx, cy in chips]
    return same[0][1], [other[0]] + same[1:], [flat for _, flat in other[1:]], other[0][0]


def _gather_start(shards):
    n = len(shards)
    me = _flat_me()
    lands = [lax.dynamic_update_slice_in_dim(lax.empty((N_DEV,) + a.shape, a.dtype), a[None], me, 0) for a in shards]

    def body(*refs):
        lnd, send, recv, token = refs[:n], refs[n], refs[n + 1], refs[-1]
        mine, targets, _, _ = _gather_targets()
        for a in range(n):
            for t, (dev, _) in enumerate(targets):
                pltpu.make_async_remote_copy(src_ref=lnd[a].at[mine], dst_ref=lnd[a].at[mine], send_sem=send.at[4 * a + t],
                                             recv_sem=recv.at[4 * a + t], device_id=dev, device_id_type=MESH).start()
        token[...] = jnp.zeros_like(token)

    res = pl.pallas_call(
        body, name="gather_start",
        out_shape=[pltpu.SemaphoreType.DMA((4 * n,)), pltpu.SemaphoreType.DMA((4 * n,))]
        + [pltpu.HBM(a.shape, a.dtype) for a in lands] + [_sds((8, LANES), F32)],
        in_specs=[HBM] * n, out_specs=[SEM, SEM] + [HBM] * n + [pl.BlockSpec(memory_space=pltpu.VMEM)],
        input_output_aliases={i: 2 + i for i in range(n)},
        compiler_params=pltpu.CompilerParams(has_side_effects=EFFECT),
    )(*[_in_hbm(a) for a in lands])
    return res[0], res[1], list(res[2:2 + n]), res[-1]


def _gather_forward(name, lands, first, send, recv, after):
    n = len(lands)

    def body(*refs):
        lnd, send_sem, recv_sem = refs[:n], refs[n], refs[n + 1]
        send2, recv2, token = refs[-3], refs[-2], refs[-1]
        mine, targets, _, sibling = _gather_targets()
        for a in range(n):
            for t, (dev, flat) in enumerate(targets):
                cp = pltpu.make_async_remote_copy(src_ref=lnd[a].at[mine], dst_ref=lnd[a].at[flat],
                                                  send_sem=send_sem.at[4 * (first + a) + t],
                                                  recv_sem=recv_sem.at[4 * (first + a) + t], device_id=dev, device_id_type=MESH)
                cp.wait_send()
                if t:
                    cp.wait_recv()
                    pltpu.make_async_remote_copy(src_ref=lnd[a].at[flat], dst_ref=lnd[a].at[flat], send_sem=send2.at[3 * a + t - 1],
                                                 recv_sem=recv2.at[3 * a + t - 1], device_id=sibling, device_id_type=MESH).start()
        token[...] = jnp.zeros_like(token)

    res = pl.pallas_call(
        body, name=name,
        out_shape=[pltpu.HBM(a.shape, a.dtype) for a in lands]
        + [pltpu.SemaphoreType.DMA((3 * n,)), pltpu.SemaphoreType.DMA((3 * n,)), _sds((8, LANES), F32)],
        in_specs=[HBM] * n + [SEM, SEM] + [ANY] * len(after),
        out_specs=[HBM] * n + [SEM, SEM, pl.BlockSpec(memory_space=pltpu.VMEM)],
        input_output_aliases={i: i for i in range(n)},
        compiler_params=pltpu.CompilerParams(has_side_effects=EFFECT),
    )(*lands, send, recv, *after)
    return list(res[:n]), res[n], res[n + 1], res[-1]


def _gather_wait(name, lands, first, recv, send2, recv2, after):
    n = len(lands)

    def body(*refs):
        lnd, recv_sem, send2_sem, recv2_sem = refs[:n], refs[n], refs[n + 1], refs[n + 2]
        mine, targets, passed, sibling = _gather_targets()
        for a in range(n):
            dev, flat = targets[0]
            pltpu.make_async_remote_copy(src_ref=lnd[a].at[mine], dst_ref=lnd[a].at[flat], send_sem=send2_sem.at[3 * a],
                                         recv_sem=recv_sem.at[4 * (first + a)], device_id=dev, device_id_type=MESH).wait_recv()
            for t in range(3):
                cp = pltpu.make_async_remote_copy(src_ref=lnd[a].at[targets[t + 1][1]], dst_ref=lnd[a].at[passed[t]],
                                                  send_sem=send2_sem.at[3 * a + t], recv_sem=recv2_sem.at[3 * a + t],
                                                  device_id=sibling, device_id_type=MESH)
                cp.wait_send()
                cp.wait_recv()

    res = pl.pallas_call(
        body, name=name, out_shape=[pltpu.HBM(a.shape, a.dtype) for a in lands],
        in_specs=[HBM] * n + [SEM, SEM, SEM, ANY], out_specs=[HBM] * n,
        input_output_aliases={i: i for i in range(n)},
        compiler_params=pltpu.CompilerParams(has_side_effects=EFFECT),
    )(*lands, recv, send2, recv2, after)
    return list(res)


def _adamw_decay(w, m, v):
    return ADAM_WD * w, ADAM_B1 * m, ADAM_B2 * v


def _adamw_finish(g, wd_w, m1, v1):
    m = m1 + (1.0 - ADAM_B1) * g
    v = v1 + (1.0 - ADAM_B2) * (g * g)
    m_hat = m / (1.0 - ADAM_B1 ** ADAM_STEP)
    v_hat = v / (1.0 - ADAM_B2 ** ADAM_STEP)
    delta = -ADAM_LR * (m_hat / (jnp.sqrt(v_hat) + ADAM_EPS) + wd_w)
    return delta, m, v


def _adamw(g, w, m, v):
    return _adamw_finish(g, *_adamw_decay(w, m, v))


def _update_prep(name, w, m, v, dep, block_bytes=1 << 20):
    _, r, c = w.shape
    tr = max(8, min(r, (block_bytes // (4 * c)) // 8 * 8))
    while r % tr:
        tr -= 8

    def body(w_ref, m_ref, v_ref, dep_ref, ow_ref, om_ref, ov_ref):
        del dep_ref
        ow_ref[...], om_ref[...], ov_ref[...] = _adamw_decay(w_ref[...], m_ref[...], v_ref[...])

    blk = pl.BlockSpec((None, tr, c), lambda i: (0, i, 0))
    return pl.pallas_call(
        body, name=name, grid=(r // tr,), in_specs=[blk] * 3 + [ANY], out_specs=[blk] * 3,
        out_shape=[_sds((1, r, c), F32)] * 3, compiler_params=_params(("parallel",)),
    )(w, m, v, dep)


def _update(name, parts, w, m, v, layout=None, decayed=False, transposed_out=False, block_bytes=1 << 20):
    _, r, c = w.shape
    n_slots, _, cp = parts.shape
    tr = max(8, min(r, (block_bytes // (4 * cp)) // 8 * 8))
    if transposed_out:
        tr = _tile(r, 256)
    while r % tr:
        tr -= 8

    def body(p_ref, w_ref, m_ref, v_ref, g_ref, d_ref, nm_ref, nv_ref, *scratch):
        g = p_ref[0].astype(F32)
        for p in range(1, n_slots):
            g = g + p_ref[p].astype(F32)
        if layout is not None:
            s1, s2, lg = layout.my_shifts()
            lane = lax.broadcasted_iota(jnp.int32, g.shape, 1)
            scratch[0][...] = jnp.where(lane < lg, pltpu.roll(g, cp - s1, 1), pltpu.roll(g, cp - s2, 1))
            g = scratch[0][:, 0:c]
        step = _adamw_finish if decayed else _adamw
        results = (g,) + step(g, w_ref[...], m_ref[...], v_ref[...])
        for ref, val in zip((g_ref, d_ref, nm_ref, nv_ref), results):
            ref[...] = val.T if transposed_out else val

    blk = pl.BlockSpec((None, tr, c), lambda i: (0, i, 0))
    out_blk = pl.BlockSpec((None, c, tr), lambda i: (0, 0, i)) if transposed_out else blk
    res = pl.pallas_call(
        body, name=name, grid=(r // tr,),
        in_specs=[pl.BlockSpec((n_slots, tr, cp), lambda i: (0, i, 0)), blk, blk, blk],
        out_specs=[out_blk] * 4, out_shape=[_sds((1, c, r) if transposed_out else (1, r, c), F32)] * 4,
        scratch_shapes=[] if layout is None else [pltpu.VMEM((tr, cp), F32)],
        compiler_params=_params(("parallel",)),
    )(parts, w, m, v)
    return [jnp.transpose(o, (0, 2, 1)) for o in res] if transposed_out else res


def _small_update(part, w, m, v):
    n = part.shape[1]

    def body(p_ref, w_ref, m_ref, v_ref, g_ref, d_ref, nm_ref, nv_ref, buf, send, recv):
        me, peers = _mesh_place()
        buf[me] = p_ref[...]
        sent = []
        for d, dev, flat in peers:
            cp = pltpu.make_async_remote_copy(src_ref=p_ref, dst_ref=buf.at[me], send_sem=send.at[d],
                                              recv_sem=recv.at[d], device_id=dev, device_id_type=MESH)
            cp.start()
            sent.append(cp)
        for d, dev, flat in peers:
            pltpu.make_async_remote_copy(src_ref=p_ref, dst_ref=buf.at[flat], send_sem=send.at[d],
                                         recv_sem=recv.at[d], device_id=dev, device_id_type=MESH).wait_recv()
        for cp in sent:
            cp.wait_send()
        g = buf[0]
        for p in range(1, N_DEV):
            g = g + buf[p]
        g_ref[...] = g
        d_ref[...], nm_ref[...], nv_ref[...] = _adamw(g, w_ref[...], m_ref[...], v_ref[...])

    vm = pl.BlockSpec(memory_space=pltpu.VMEM)
    return pl.pallas_call(
        body, name="small_update", in_specs=[vm] * 4, out_specs=[vm] * 4, out_shape=[_sds((1, n), F32)] * 4,
        scratch_shapes=[pltpu.VMEM((N_DEV, 1, n), F32), pltpu.SemaphoreType.DMA((N_DEV,)),
                        pltpu.SemaphoreType.DMA((N_DEV,))],
    )(part, w, m, v)


class _WInLayout:
    def __init__(self, n8, n_f, d_sb, d_fox, d):
        assert n8 % LANES == 1 and n_f < LANES and d % (N_DEV * LANES) == 0
        self.n8, self.n_f, self.d = n8, n_f, d
        self.sp = n8 // LANES
        self.wp = (n8 + 2 * LANES - 2) // LANES * LANES
        self.n_qkv = 3 * (d_sb + d_fox)
        nq, dt, tc = self.n_qkv // LANES, d // LANES, d // N_DEV // LANES
        h_sb, h_fox = d_sb // HEAD_DIM, d_fox // HEAD_DIM
        self.sources = {}
        self.part_tile = {}
        for p in range(N_DEV):
            lg = min(max(self.n_qkv + n_f - n8 * p, 0), n8)
            s1, s2 = p, p + LANES - n_f
            spans = []
            if lg > 0:
                spans.append(("a", self.sp * p, s1 // LANES, (lg + s1 - 1) // LANES))
            if lg < n8:
                spans.append(("g", self.sp * p - 1 - nq, (lg + s2) // LANES, (n8 - 1 + s2) // LANES))
            for kind, base, first, last in spans:
                for i in range(first, last + 1):
                    assert (p, i) not in self.part_tile
                    self.part_tile[(p, i)] = (kind, base + i)
                    self.sources.setdefault((kind, base + i), []).append((p, i))
        self.cat_tiles = [("a", r * h_sb + h) for h in range(h_sb) for r in range(3)]
        self.cat_tiles += [("a", 3 * h_sb + r * h_fox + h) for h in range(h_fox) for r in range(3)]
        self.cat_tiles += [("g", which * dt + j * tc + half) for j in range(N_DEV) for which in (0, 1) for half in range(tc)]
        self.cat_tiles += [("a", nq)] + [None] * (F_PAD // LANES - 1)
        self.cat_index = {key: c for c, key in enumerate(self.cat_tiles) if key is not None}

    def my_shifts(self):
        me = _flat_me()
        return me, me + LANES - self.n_f, jnp.clip(self.n_qkv + self.n_f - self.n8 * me, 0, self.n8)


def _lane_tile(i):
    return pl.ds(i * LANES, LANES)


def _w_in_shift(w_in, lay, tr=256):
    _, d, n8 = w_in.shape

    def body(w_ref, o_ref, buf):
        buf[...] = jnp.zeros_like(buf)
        buf[:, 0:n8] = w_ref[...]
        v = buf[...]
        s1, s2, lg = lay.my_shifts()
        pos = lax.broadcasted_iota(jnp.int32, v.shape, 1)
        o_ref[...] = jnp.where(pos < lg + s1, pltpu.roll(v, s1, 1),
                               jnp.where(pos >= lg + s2, pltpu.roll(v, s2, 1), 0.0)).astype(BF16)

    return pl.pallas_call(
        body, name="w_in_shift", grid=(d // tr,),
        in_specs=[pl.BlockSpec((None, tr, n8), lambda i: (0, i, 0))],
        out_specs=pl.BlockSpec((tr, lay.wp), lambda i: (i, 0)), out_shape=_sds((d, lay.wp), BF16),
        scratch_shapes=[pltpu.VMEM((tr, lay.wp), F32)],
        compiler_params=_params(("parallel",)),
    )(w_in)


def _w_in_build(g_in, lay, tr=256):
    d = g_in.shape[1]
    width = len(lay.cat_tiles) * LANES

    def body(g_ref, o_ref):
        for c, key in enumerate(lay.cat_tiles):
            if key is None:
                o_ref[:, _lane_tile(c)] = jnp.zeros((tr, LANES), BF16)
                continue
            (p, i), *more = lay.sources[key]
            val = g_ref[p, :, _lane_tile(i)]
            for p2, i2 in more:
                val = val + g_ref[p2, :, _lane_tile(i2)]
            o_ref[:, _lane_tile(c)] = val

    return pl.pallas_call(
        body, name="w_in_build", grid=(d // tr,),
        in_specs=[pl.BlockSpec((N_DEV, tr, lay.wp), lambda i: (0, i, 0))],
        out_specs=pl.BlockSpec((tr, width), lambda i: (i, 0)), out_shape=_sds((d, width), BF16),
        compiler_params=_params(("parallel",)),
    )(g_in)


def _w_in_grad_parts(dwq, dwgf, lay, tr=256):
    d = dwq.shape[0]
    nq = lay.n_qkv // LANES

    def body(q_ref, g_ref, o_ref):
        for p in range(N_DEV):
            for i in range(lay.wp // LANES):
                key = lay.part_tile.get((p, i))
                if key is None:
                    o_ref[p, :, _lane_tile(i)] = jnp.zeros((tr, LANES), BF16)
                    continue
                c = lay.cat_index[key]
                o_ref[p, :, _lane_tile(i)] = q_ref[:, _lane_tile(c)] if c < nq else g_ref[:, _lane_tile(c - nq)]

    return pl.pallas_call(
        body, name="w_in_grad_parts", grid=(d // tr,),
        in_specs=[pl.BlockSpec((tr, dwq.shape[1]), lambda i: (i, 0)), pl.BlockSpec((tr, dwgf.shape[1]), lambda i: (i, 0))],
        out_specs=pl.BlockSpec((N_DEV, tr, lay.wp), lambda i: (0, i, 0)), out_shape=_sds((N_DEV, d, lay.wp), BF16),
        compiler_params=_params(("parallel",)),
    )(dwq, dwgf)


def kernel(x, norm_mix_pre, norm_mix_post, w_in, b_forget, w_branch_sb, w_branch_fox, w_out, norm_ffn_pre, norm_ffn_post, w_ffn_gate, w_ffn_up, w_ffn_down, loss_target, m_norm_mix_pre, m_norm_mix_post, m_w_in, m_b_forget, m_w_branch_sb, m_w_branch_fox, m_w_out, m_norm_ffn_pre, m_norm_ffn_post, m_w_ffn_gate, m_w_ffn_up, m_w_ffn_down, v_norm_mix_pre, v_norm_mix_post, v_w_in, v_b_forget, v_w_branch_sb, v_w_branch_fox, v_w_out, v_norm_ffn_pre, v_norm_ffn_post, v_w_ffn_gate, v_w_ffn_up, v_w_ffn_down):
    xs, target = x[0], loss_target[0]
    s, d = xs.shape
    d_sb, d_fox = w_branch_sb.shape[1], w_branch_fox.shape[1]
    h_sb, h_fox = d_sb // HEAD_DIM, d_fox // HEAD_DIM
    n_f = b_forget.shape[1]
    fs = w_ffn_gate.shape[2]
    cs = d // N_DEV
    n_qkv = 3 * (d_sb + d_fox)
    n_gf = 2 * d + F_PAD
    f_blk = 2 * d // LANES
    big = (w_in, w_branch_sb, w_branch_fox, w_out, w_ffn_gate, w_ffn_up, w_ffn_down)
    big_m = (m_w_in, m_w_branch_sb, m_w_branch_fox, m_w_out, m_w_ffn_gate, m_w_ffn_up, m_w_ffn_down)
    big_v = (v_w_in, v_w_branch_sb, v_w_branch_fox, v_w_out, v_w_ffn_gate, v_w_ffn_up, v_w_ffn_down)

    lay = _WInLayout(w_in.shape[2], n_f, d_sb, d_fox, d)
    send1, recv1, lands, token = _gather_start([_w_in_shift(w_in, lay)] + [w[0].astype(BF16) for w in big[1:]])
    b_pad = jnp.pad(b_forget, ((0, 0), (0, LANES - n_f)))

    started = token[0, 0]
    u, u_t = _pre_norm(xs, norm_mix_pre, dep=token)
    weights = dict(zip(("w_in", "w_branch_sb", "w_branch_fox", "w_out", "w_ffn_gate", "w_ffn_up", "w_ffn_down"),
                       zip(big, big_m, big_v)))
    decayed = {nm: _update_prep("decay_" + nm, *[t + started for t in weights[nm]], u)
               for nm in ("w_in", "w_ffn_gate", "w_ffn_up")}
    l_in, send2, recv2, token = _gather_forward("gather_in_forward", lands[0:1], 0, send1, recv1,
                                                [u] + [t[2] for t in decayed.values()])
    (g_in,) = _gather_wait("gather_in_wait", l_in, 0, recv1, send2, recv2, token)
    w_cat = _w_in_build(g_in, lay)
    qkv = _mm_plain("proj_qkv", "nn", u, w_cat, BF16, n=n_qkv)
    gf = _mm_plain("proj_gates", "nn", u, w_cat, F32, n_off=n_qkv, n=n_gf)
    cum_col, cum_row = _forget_fwd(gf, b_pad, f_blk)
    o_sb, o_sb_t, tot = _sb_fwd(qkv, h_sb)
    l_mid, send2, recv2, token = _gather_forward("gather_mid_forward", lands[1:4], 1, send1, recv1, [o_sb])
    o_fx, o_fx_t, o_fx32, lse = _fox_fwd(qkv, cum_col, cum_row, h_fox, h_sb, token)
    g_sb, g_fx, g_out = _gather_wait("gather_mid_wait", l_mid, 1, recv1, send2, recv2, o_fx)
    w_out_full = g_out.reshape(d, d)
    merged, merged_t, a_sb, a_fx = _branch_merge(o_sb, o_fx, g_sb, g_fx, gf, o_fx)
    l_ffn, send2, recv2, token = _gather_forward("gather_ffn_forward", lands[4:6], 4, send1, recv1, [merged])
    mix = _mm_plain("out_proj", "nn", merged, w_out_full, F32, dep=token)
    h1, u2, u2_t = _mid_norms(xs, mix, norm_mix_post, norm_ffn_pre)
    g_gate, g_up = _gather_wait("gather_ffn_wait", l_ffn, 4, recv1, send2, recv2, u2)
    l_down, send2, recv2, token = _gather_forward("gather_down_forward", lands[6:7], 6, send1, recv1, [u2])
    gate, up, act, act_t = _ffn_up(u2, g_gate, g_up, token)
    (g_down,) = _gather_wait("gather_down_wait", l_down, 6, recv1, send2, recv2, act)
    tm, tn = _tile(s, 1024), _tile(d, 1024)
    ff = _matmul("ffn_down", "nn",
                 [(act, pl.BlockSpec((None, tm, fs), lambda i, j, k: (k, i, 0)),
                   g_down, pl.BlockSpec((None, fs, tn), lambda i, j, k: (k, 0, j)))],
                 (s // tm, d // tn, N_DEV), (tm, tn), _sds((s, d), F32), pl.BlockSpec((tm, tn), lambda i, j, k: (i, j)))
    loss_part, dy, dff, dg_ffn_post = _loss_head(h1, ff, target, norm_ffn_post)

    dgate, dup = _ffn_down_bwd(dff, g_down, gate, up)
    dw_down = _matmul("dw_down", "nn",
                      [(act_t, pl.BlockSpec((None, fs, s), lambda j, n, k: (j, 0, 0)),
                        dff, pl.BlockSpec((s, tn), lambda j, n, k: (0, n)))],
                      (N_DEV, d // tn, 1), (fs, tn), _sds((N_DEV, fs, d), BF16),
                      pl.BlockSpec((None, fs, tn), lambda j, n, k: (j, 0, n)))

    def dw_up(name, dact):
        return _matmul(name, "nn",
                       [(u2_t, pl.BlockSpec((tn, s), lambda j, i, k: (i, 0)),
                         dact, pl.BlockSpec((None, s, fs), lambda j, i, k: (j, 0, 0)))],
                       (N_DEV, d // tn, 1), (tn, fs), _sds((N_DEV, d, fs), BF16),
                       pl.BlockSpec((None, tn, fs), lambda j, i, k: (j, i, 0)))

    dw_gate, dw_upw = dw_up("dw_gate", dgate), dw_up("dw_up", dup)
    rs_ffn = _scatter_pairs("ffn", [dw_gate, dw_upw, dw_down])
    a_spec = pl.BlockSpec((None, tm, fs), lambda i, j, k: (k, i, 0))
    b_spec = pl.BlockSpec((None, tn, fs), lambda i, j, k: (k, j, 0))
    du2 = _matmul("du2", "nt", [(dgate, a_spec, g_gate, b_spec), (dup, a_spec, g_up, b_spec)],
                  (s // tm, d // tn, N_DEV), (tm, tn), _sds((s, d), F32), pl.BlockSpec((tm, tn), lambda i, j, k: (i, j)),
                  dep=rs_ffn[4])
    rs_ffn = _scatter_chips("ffn", rs_ffn, du2)
    dh1, dmix, dg_ffn_pre, dg_mix_post = _mid_norms_bwd(dy, du2, h1, mix, norm_ffn_pre, norm_mix_post)

    da_sb, da_fx, dgf = _merge_bwd(dmix, w_out_full, gf, a_sb, a_fx, dep=rs_ffn[4])
    dw_out = _mm_plain("dw_out", "nn", merged_t, dmix, BF16).reshape(N_DEV, cs, d)

    def branch_bwd(tag, da, w_b, o_t, width):
        tb = _tile(width, 1024)
        do = _matmul("do_" + tag, "nt",
                     [(da, pl.BlockSpec((tm, cs), lambda i, j, k: (i, k)),
                       w_b, pl.BlockSpec((None, tb, cs), lambda i, j, k: (k, j, 0)))],
                     (s // tm, width // tb, N_DEV), (tm, tb), _sds((s, width), BF16),
                     pl.BlockSpec((tm, tb), lambda i, j, k: (i, j)))
        dw = _matmul("dw_" + tag, "nn",
                     [(o_t, pl.BlockSpec((width, s), lambda j, i, k: (0, 0)),
                       da, pl.BlockSpec((s, cs), lambda j, i, k: (0, j)))],
                     (N_DEV, 1, 1), (width, cs), _sds((N_DEV, width, cs), BF16),
                     pl.BlockSpec((None, width, cs), lambda j, i, k: (j, 0, 0)))
        return do, dw

    do_sb, dw_sb = branch_bwd("sb", da_sb, g_sb, o_sb_t, d_sb)
    do_fx, dw_fx = branch_bwd("fox", da_fx, g_fx, o_fx_t, d_fox)

    rs_mid = _scatter_pairs("mid", [dw_sb, dw_fx, dw_out])

    dqkv = _sb_bwd(qkv, do_sb, tot, h_sb, rs_mid[4])
    rs_mid = _scatter_chips("mid", rs_mid, dqkv)
    dqkv, dcum = _fox_bwd(dqkv, qkv, do_fx, o_fx32, lse, cum_col, cum_row, h_fox, h_sb, rs_mid[4])
    dgf, db_part = _forget_bwd(dgf, dcum, gf, b_pad, f_blk)
    dw_in = _w_in_grad_parts(_mm_plain("dw_qkv", "nn", u_t, dqkv, BF16), _mm_plain("dw_gates", "nn", u_t, dgf, BF16), lay)
    rs_in = _scatter_pairs("in", [dw_in])
    du = _mm_plain("du_qkv", "nt", dqkv, w_cat, F32, tn=1024, dep=rs_in[4])
    rs_in = _scatter_chips("in", rs_in, du)
    du = _mm_plain("du_gates", "nt", dgf, w_cat, F32, tn=1024, k_off=n_qkv, init=du, dep=rs_in[4])
    dx, dg_mix_pre = _pre_norm_bwd(dh1, du, xs, norm_mix_pre)

    upd = {}

    def update_group(tag, rs, names, after):
        parts = _scatter_end(tag, rs, after)
        for nm, p in zip(names, parts):
            w, m, v = decayed.get(nm, weights[nm])
            upd[nm] = _update("update_" + nm, p, w, m, v, layout=lay if nm == "w_in" else None, decayed=nm in decayed,
                              transposed_out=nm in ("w_ffn_gate", "w_ffn_up"))

    update_group("ffn", rs_ffn, ("w_ffn_gate", "w_ffn_up", "w_ffn_down"), [dx])
    update_group("mid", rs_mid, ("w_branch_sb", "w_branch_fox", "w_out"), [upd[nm][3] for nm in ("w_ffn_gate", "w_ffn_up", "w_ffn_down")])
    update_group("in", rs_in, ("w_in",), [upd[nm][3] for nm in ("w_branch_sb", "w_branch_fox", "w_out")])

    small = ((norm_mix_pre, m_norm_mix_pre, v_norm_mix_pre), (norm_mix_post, m_norm_mix_post, v_norm_mix_post),
             (norm_ffn_pre, m_norm_ffn_pre, v_norm_ffn_pre), (norm_ffn_post, m_norm_ffn_post, v_norm_ffn_post))
    pad_f = ((0, 0), (0, LANES - n_f))
    cat = lambda i: jnp.concatenate([t[i] for t in small] + [jnp.pad((b_forget, m_b_forget, v_b_forget)[i], pad_f)], axis=1)
    sm = _small_update(jnp.concatenate([dg_mix_pre, dg_mix_post, dg_ffn_pre, dg_ffn_post, db_part], axis=1),
                       cat(0), cat(1), cat(2))
    for i, nm in enumerate(("norm_mix_pre", "norm_mix_post", "norm_ffn_pre", "norm_ffn_post")):
        upd[nm] = [o[:, i * d:(i + 1) * d] for o in sm]
    upd["b_forget"] = [o[:, 4 * d:4 * d + n_f] for o in sm]

    loss = lax.psum(loss_part[0, 0], ("x", "y", "c"))
    order = ("norm_mix_pre", "norm_mix_post", "w_in", "b_forget", "w_branch_sb", "w_branch_fox", "w_out",
             "norm_ffn_pre", "norm_ffn_post", "w_ffn_gate", "w_ffn_up", "w_ffn_down")
    return (loss, dx[None]) + tuple(upd[nm][i] for i in range(4) for nm in order)
```

```python
import jax
import jax.numpy as jnp
from jax import lax
from jax.experimental import pallas as pl
from jax.experimental.pallas import tpu as pltpu

F32 = jnp.float32
BF16 = jnp.bfloat16
MESH = pl.DeviceIdType.MESH
ANY = pl.BlockSpec(memory_space=pl.ANY)
HBM = pl.BlockSpec(memory_space=pltpu.HBM)
SEM = pl.BlockSpec(memory_space=pltpu.SEMAPHORE)
EFFECT = pltpu.SideEffectType.DATAFLOW_SIDE_EFFECTING

N_DEV = 8
HEAD_DIM = 128
RMS_EPS = 1e-6
F_PAD = 512
LANES = 128
ATT_TQ = 256
ATT_TK = 256
ATT_HP = 4
NEG_BIG = -1e30
VMEM_LIMIT = 56 * 1024 * 1024

ADAM_LR = 0.001
ADAM_B1 = 0.9
ADAM_B2 = 0.999
ADAM_EPS = 1e-08
ADAM_WD = 0.01
ADAM_STEP = 10

_DIMS = {"nn": ((1,), (0,)), "nt": ((1,), (1,)), "tn": ((0,), (0,))}


def _params(sem):
    return pltpu.CompilerParams(dimension_semantics=sem, vmem_limit_bytes=VMEM_LIMIT)


def _dot(a, b, mode="nn"):
    return lax.dot_general(a.astype(BF16), b.astype(BF16), (_DIMS[mode], ((), ())), preferred_element_type=F32)


def _tile(n, pref):
    if n <= pref:
        return n
    t = (pref // LANES) * LANES
    while n % t:
        t -= LANES
    return t


def _split2(v):
    hi = v.astype(BF16)
    return hi, (v - hi.astype(F32)).astype(BF16)


def _split3(v):
    a = v.astype(BF16)
    r = v - a.astype(F32)
    b = r.astype(BF16)
    return a, b, (r - b.astype(F32)).astype(BF16)


def _tri(n, cmp):
    r = lax.broadcasted_iota(jnp.int32, (n, n), 0)
    c = lax.broadcasted_iota(jnp.int32, (n, n), 1)
    return jnp.where(cmp(r, c), 1.0, 0.0).astype(BF16)


def _lane_pick(v, h):
    lane = lax.broadcasted_iota(jnp.int32, v.shape, 1)
    return jnp.sum(jnp.where(lane == h, v, 0.0), axis=1, keepdims=True)


def _lane_put(ref, rows, h, col):
    old = ref[rows, :]
    lane = lax.broadcasted_iota(jnp.int32, old.shape, 1)
    ref[rows, :] = jnp.where(lane == h, col, old)


def _sigmoid(z):
    return 1.0 / (1.0 + jnp.exp(-z))


def _log_sigmoid(z):
    return jnp.minimum(z, 0.0) - jnp.log(1.0 + jnp.exp(-jnp.abs(z)))


def _sds(shape, dtype):
    return jax.ShapeDtypeStruct(shape, dtype)


def _matmul(name, mode, pairs, grid, acc_shape, out_shape, out_specs, extras=(), epilogue=None, init=None, dep=None):
    n_p, n_e = len(pairs), len(extras)
    nk = grid[-1]
    single = not isinstance(out_shape, (list, tuple))
    n_i = 0 if init is None else 1
    n_d = 0 if dep is None else 1

    one_step = nk == 1 and init is None

    def body(*refs):
        ab = refs[:2 * n_p]
        ex = refs[2 * n_p:2 * n_p + n_e]
        ini = refs[2 * n_p + n_e:2 * n_p + n_e + n_i]
        outs = refs[2 * n_p + n_e + n_i + n_d:len(refs) - (0 if one_step else 1)]

        def finish(total):
            if epilogue is None:
                outs[0][...] = total.astype(outs[0].dtype)
            else:
                epilogue(total, ex, outs)

        t = _dot(ab[0][...], ab[1][...], mode)
        for p in range(1, n_p):
            t = t + _dot(ab[2 * p][...], ab[2 * p + 1][...], mode)
        if one_step:
            finish(t)
            return
        acc = refs[-1]
        k = pl.program_id(len(grid) - 1)

        @pl.when(k == 0)
        def _():
            acc[...] = t if init is None else ini[0][...].astype(F32) + t

        @pl.when(k > 0)
        def _():
            acc[...] += t

        @pl.when(k == nk - 1)
        def _():
            finish(acc[...])

    in_specs = [s for (_, sa, _, sb) in pairs for s in (sa, sb)] + [s for (_, s) in extras]
    args = [v for (a, _, b, _) in pairs for v in (a, b)] + [e for (e, _) in extras]
    if init is not None:
        in_specs.append(init[1])
        args.append(init[0])
    if dep is not None:
        in_specs.append(ANY)
        args.append(dep)
    return pl.pallas_call(
        body, name=name, grid=grid, in_specs=in_specs,
        out_specs=out_specs if single else list(out_specs),
        out_shape=out_shape if single else list(out_shape),
        scratch_shapes=[] if one_step else [pltpu.VMEM(acc_shape, F32)],
        compiler_params=_params(("parallel",) * (len(grid) - 1) + ("arbitrary",)),
    )(*args)


def _mm_plain(name, mode, a, b, out_dtype, *, n_off=0, n=None, k_off=0, tm=1024, tn=1536, tk=2048, init=None, dep=None):
    if mode == "nn":
        (m, kk), nn_ = a.shape, b.shape[1]
    elif mode == "nt":
        (m, kk), nn_ = a.shape, b.shape[0]
    else:
        (kk, m), nn_ = a.shape, b.shape[1]
    n = nn_ if n is None else n
    tm, tn, tk = _tile(m, tm), _tile(n, tn), _tile(kk, tk)
    while n_off % tn or n % tn:
        tn -= LANES
    while k_off % tk or kk % tk:
        tk -= LANES
    off, koff = n_off // tn, k_off // tk
    a_spec = {"nn": pl.BlockSpec((tm, tk), lambda i, j, k: (i, k)),
              "nt": pl.BlockSpec((tm, tk), lambda i, j, k: (i, k)),
              "tn": pl.BlockSpec((tk, tm), lambda i, j, k: (k, i))}[mode]
    b_spec = {"nn": pl.BlockSpec((tk, tn), lambda i, j, k: (k, j + off)),
              "nt": pl.BlockSpec((tn, tk), lambda i, j, k: (j, k + koff)),
              "tn": pl.BlockSpec((tk, tn), lambda i, j, k: (k, j))}[mode]
    o_spec = pl.BlockSpec((tm, tn), lambda i, j, k: (i, j))
    if init is not None:
        init = (init, o_spec)
    return _matmul(name, mode, [(a, a_spec, b, b_spec)], (m // tm, n // tn, kk // tk), (tm, tn),
                   _sds((m, n), out_dtype), o_spec, init=init, dep=dep)


def _rows_call(name, body, ins, outs, s, tr=256, dep=None):
    def spec(v, per_row):
        if per_row == "transposed":
            return pl.BlockSpec((v.shape[0], tr), lambda i: (0, i))
        if per_row:
            return pl.BlockSpec((tr, v.shape[1]), lambda i: (i, 0))
        return pl.BlockSpec(v.shape, lambda i: (0, 0))
    n_in = len(ins)
    deps = [] if dep is None else [dep]

    def with_dep(*refs):
        body(*refs[:n_in], *refs[n_in + len(deps):])

    return pl.pallas_call(
        with_dep, name=name, grid=(s // tr,),
        in_specs=[spec(v, p) for v, p in ins] + [ANY] * len(deps), out_specs=[spec(v, p) for v, p in outs],
        out_shape=[_sds(v.shape, v.dtype) for v, _ in outs],
        compiler_params=_params(("arbitrary",)),
    )(*[v for v, _ in ins], *deps)


def _rsq(v):
    return lax.rsqrt(jnp.mean(v * v, axis=-1, keepdims=True) + RMS_EPS)


def _norm_bwd(dy, v, r, g):
    vh = v * r
    t = dy * g
    dv = r * (t - vh * jnp.mean(t * vh, axis=-1, keepdims=True))
    return dv, jnp.sum(dy * vh, axis=0, keepdims=True)


def _accum(ref, val):
    @pl.when(pl.program_id(0) == 0)
    def _():
        ref[...] = jnp.zeros_like(ref)
    ref[...] += val


def _pre_norm(x, g, dep=None):
    def body(x_ref, g_ref, u_ref, ut_ref):
        v = x_ref[...]
        u = (v * _rsq(v) * g_ref[...]).astype(BF16)
        u_ref[...] = u
        ut_ref[...] = u.T
    s, d = x.shape
    return _rows_call("pre_norm", body, [(x, True), (g, False)],
                      [(_sds((s, d), BF16), True), (_sds((d, s), BF16), "transposed")], s, dep=dep)


def _mid_norms(x, mix, g_post, g_pre):
    def body(x_ref, mix_ref, gp_ref, gn_ref, h_ref, u_ref, ut_ref):
        mv = mix_ref[...]
        h = x_ref[...] + mv * _rsq(mv) * gp_ref[...]
        h_ref[...] = h
        u = (h * _rsq(h) * gn_ref[...]).astype(BF16)
        u_ref[...] = u
        ut_ref[...] = u.T
    s, d = x.shape
    return _rows_call("mid_norms", body, [(x, True), (mix, True), (g_post, False), (g_pre, False)],
                      [(_sds((s, d), F32), True), (_sds((s, d), BF16), True), (_sds((d, s), BF16), "transposed")], s)


def _loss_head(h1, ff, target, g):
    s, d = h1.shape

    def body(h_ref, ff_ref, t_ref, g_ref, loss_ref, dy_ref, dff_ref, dg_ref):
        fv = ff_ref[...]
        r = _rsq(fv)
        err = h_ref[...] + fv * r * g_ref[...] - t_ref[...]
        part = 0.5 * jnp.sum(jnp.mean(err * err, axis=-1, keepdims=True), axis=0, keepdims=True)
        _accum(loss_ref, jnp.broadcast_to(part, loss_ref.shape))
        dy = err * (1.0 / d)
        dy_ref[...] = dy
        dff, dg = _norm_bwd(dy, fv, r, g_ref[...])
        dff_ref[...] = dff.astype(BF16)
        _accum(dg_ref, dg)

    return _rows_call("loss_head", body, [(h1, True), (ff, True), (target, True), (g, False)],
                      [(_sds((1, LANES), F32), False), (_sds((s, d), F32), True),
                       (_sds((s, d), BF16), True), (_sds((1, d), F32), False)], s)


def _mid_norms_bwd(dy, du2, h1, mix, g_pre, g_post):
    s, d = dy.shape

    def body(dy_ref, du_ref, h_ref, mix_ref, gn_ref, gp_ref, dh_ref, dmix_ref, dgn_ref, dgp_ref):
        h = h_ref[...]
        dh, dgn = _norm_bwd(du_ref[...], h, _rsq(h), gn_ref[...])
        dh = dh + dy_ref[...]
        dh_ref[...] = dh
        _accum(dgn_ref, dgn)
        mv = mix_ref[...]
        dmix, dgp = _norm_bwd(dh, mv, _rsq(mv), gp_ref[...])
        dmix_ref[...] = dmix.astype(BF16)
        _accum(dgp_ref, dgp)

    return _rows_call("mid_norms_bwd", body,
                      [(dy, True), (du2, True), (h1, True), (mix, True), (g_pre, False), (g_post, False)],
                      [(_sds((s, d), F32), True), (_sds((s, d), BF16), True),
                       (_sds((1, d), F32), False), (_sds((1, d), F32), False)], s)


def _pre_norm_bwd(dh1, du, x, g, dep=None):
    s, d = x.shape

    def body(dh_ref, du_ref, x_ref, g_ref, dx_ref, dg_ref):
        v = x_ref[...]
        dv, dg = _norm_bwd(du_ref[...], v, _rsq(v), g_ref[...])
        dx_ref[...] = dh_ref[...] + dv
        _accum(dg_ref, dg)

    return _rows_call("pre_norm_bwd", body, [(dh1, True), (du, True), (x, True), (g, False)],
                      [(_sds((s, d), F32), True), (_sds((1, d), F32), False)], s, dep=dep)


def _forget_fwd(gf, b_pad, f_blk):
    s = gf.shape[0]
    tb = ATT_TK
    nb = s // tb

    def body(f_ref, b_ref, col_ref, row_ref):
        incl = _tri(tb, lambda r, c: c <= r)
        carry = jnp.zeros((1, LANES), F32)
        for i in range(nb):
            lf = _log_sigmoid(f_ref[pl.ds(i * tb, tb), :] + b_ref[...])
            parts = _split3(lf)
            cum = carry + _dot(incl, parts[0]) + _dot(incl, parts[1]) + _dot(incl, parts[2])
            col_ref[pl.ds(i * tb, tb), :] = cum
            row_ref[i] = cum.T
            carry = carry + jnp.sum(lf, axis=0, keepdims=True)

    return pl.pallas_call(
        body, name="forget_fwd", grid=(1,),
        in_specs=[pl.BlockSpec((s, LANES), lambda i: (0, f_blk)), pl.BlockSpec((1, LANES), lambda i: (0, 0))],
        out_specs=[pl.BlockSpec((s, LANES), lambda i: (0, 0)), pl.BlockSpec((nb, LANES, tb), lambda i: (0, 0, 0))],
        out_shape=[_sds((s, LANES), F32), _sds((nb, LANES, tb), F32)],
        compiler_params=_params(("arbitrary",)),
    )(gf, b_pad)


def _forget_bwd(dgf, dcum, gf, b_pad, f_blk):
    s = gf.shape[0]
    tb = ATT_TK
    nb = s // tb
    sec = dgf.shape[1] // F_PAD - 1

    def body(dgf_hbm, dc_ref, f_ref, b_ref, out_ref, db_ref):
        del dgf_hbm
        incl = _tri(tb, lambda r, c: c >= r)
        carry = jnp.zeros((1, LANES), F32)
        db = jnp.zeros((1, LANES), F32)
        out_ref[...] = jnp.zeros_like(out_ref)
        for i in reversed(range(nb)):
            dc = dc_ref[pl.ds(i * tb, tb), :]
            parts = _split3(dc)
            dlf = carry + _dot(incl, parts[0]) + _dot(incl, parts[1]) + _dot(incl, parts[2])
            z = f_ref[pl.ds(i * tb, tb), :] + b_ref[...]
            df = dlf * _sigmoid(-z)
            out_ref[pl.ds(i * tb, tb), pl.ds(0, LANES)] = df.astype(BF16)
            db = db + jnp.sum(df, axis=0, keepdims=True)
            carry = carry + jnp.sum(dc, axis=0, keepdims=True)
        db_ref[...] = db

    return pl.pallas_call(
        body, name="forget_bwd", grid=(1,),
        in_specs=[ANY, pl.BlockSpec((s, LANES), lambda i: (0, 0)),
                  pl.BlockSpec((s, LANES), lambda i: (0, f_blk)), pl.BlockSpec((1, LANES), lambda i: (0, 0))],
        out_specs=[pl.BlockSpec((s, F_PAD), lambda i: (0, sec)), pl.BlockSpec((1, LANES), lambda i: (0, 0))],
        out_shape=[_sds(dgf.shape, BF16), _sds((1, LANES), F32)],
        input_output_aliases={0: 0},
        compiler_params=_params(("arbitrary",)),
    )(dgf, dcum, gf, b_pad)


def _diag_mask(strict):
    r = lax.broadcasted_iota(jnp.int32, (ATT_TQ, ATT_TK), 0)
    c = lax.broadcasted_iota(jnp.int32, (ATT_TQ, ATT_TK), 1)
    return c < r if strict else c <= r


def _qkv_specs(hb0, s):
    specs = []
    for j in range(ATT_HP):
        def col(g, j=j):
            return 3 * (hb0 + ATT_HP * g + j)
        specs += [pl.BlockSpec((ATT_TQ, HEAD_DIM), lambda g, i, col=col: (i, col(g))),
                  pl.BlockSpec((s, HEAD_DIM), lambda g, i, col=col: (0, col(g) + 1)),
                  pl.BlockSpec((s, HEAD_DIM), lambda g, i, col=col: (0, col(g) + 2))]
    return specs


def _head_cols(j):
    return pl.ds(j * HEAD_DIM, HEAD_DIM)


def _sb_fwd(qkv, n_heads):
    s = qkv.shape[0]
    scale = HEAD_DIM ** -0.5
    tq, tk = ATT_TQ, ATT_TK
    heads = range(ATT_HP)

    def body(*refs):
        qkv_refs, (o_ref, ot_ref, tot_ref) = refs[:3 * ATT_HP], refs[3 * ATT_HP:]
        g, i = pl.program_id(0), pl.program_id(1)

        @pl.when((g == 0) & (i == 0))
        def _():
            tot_ref[...] = jnp.zeros_like(tot_ref)

        qs = [qkv_refs[3 * j][...] for j in heads]
        upper = _tri(tk, lambda r, c: r > c)

        def tile(kj, carry, mask):
            rows = pl.ds(pl.multiple_of(kj * tk, tk), tk)
            z = [_dot(qs[j], qkv_refs[3 * j + 1][rows, :], "nt") * scale for j in heads]
            lsz = [_log_sigmoid(z[j]) for j in heads]
            lk = [lsz[j] - z[j] if mask is None else jnp.where(mask, lsz[j] - z[j], 0.0) for j in heads]
            parts = [_split2(lk[j]) for j in heads]
            above = [carry[j][0] + _dot(parts[j][0], upper) + _dot(parts[j][1], upper) for j in heads]
            w = [jnp.exp(lsz[j] + above[j]) for j in heads]
            if mask is not None:
                w = [jnp.where(mask, w[j], 0.0) for j in heads]
            return tuple((carry[j][0] + jnp.sum(lk[j], axis=1, keepdims=True),
                          carry[j][1] + _dot(w[j], qkv_refs[3 * j + 2][rows, :])) for j in heads)

        carry = tile(i, tuple((jnp.zeros((tq, 1), F32), jnp.zeros((tq, HEAD_DIM), F32)) for _ in heads), _diag_mask(True))
        carry = lax.fori_loop(0, i, lambda n, cr: tile(i - 1 - n, cr, None), carry)
        q_rows = pl.ds(pl.multiple_of(i * tq, tq), tq)
        for j in heads:
            c, acc = carry[j]
            o = acc.astype(BF16)
            o_ref[:, _head_cols(j)] = o
            ot_ref[_head_cols(j), :] = o.T
            _lane_put(tot_ref, q_rows, ATT_HP * g + j, c)

    wide = ATT_HP * HEAD_DIM
    return pl.pallas_call(
        body, name="sb_fwd", grid=(n_heads // ATT_HP, s // tq),
        in_specs=_qkv_specs(0, s),
        out_specs=[pl.BlockSpec((tq, wide), lambda g, i: (i, g)), pl.BlockSpec((wide, tq), lambda g, i: (g, i)),
                   pl.BlockSpec((s, LANES), lambda g, i: (0, 0))],
        out_shape=[_sds((s, n_heads * HEAD_DIM), BF16), _sds((n_heads * HEAD_DIM, s), BF16), _sds((s, LANES), F32)],
        compiler_params=_params(("arbitrary", "arbitrary")),
    )(*[qkv] * (3 * ATT_HP))


def _sb_bwd(qkv, do, tot, n_heads, dep):
    s = qkv.shape[0]
    scale = HEAD_DIM ** -0.5
    tq, tk = ATT_TQ, ATT_TK
    nq = s // tq
    hd = HEAD_DIM

    heads = range(ATT_HP)

    def body(*refs):
        qkv_refs = refs[:3 * ATT_HP]
        do_ref, tot_ref, _, out_ref, dk_acc, dv_acc = refs[3 * ATT_HP:]
        g, i = pl.program_id(0), pl.program_id(1)

        @pl.when(i == 0)
        def _():
            dk_acc[...] = jnp.zeros_like(dk_acc)
            dv_acc[...] = jnp.zeros_like(dv_acc)

        qs = [qkv_refs[3 * j][...] for j in heads]
        douts = [do_ref[:, _head_cols(j)] for j in heads]
        totals = [_lane_pick(tot_ref[...], ATT_HP * g + j) for j in heads]
        incl = _tri(tk, lambda r, c: r <= c)
        excl = _tri(tk, lambda r, c: r < c)

        def tile(kj, carry, mask):
            rows = pl.ds(pl.multiple_of(kj * tk, tk), tk)
            k_t = [qkv_refs[3 * j + 1][rows, :] for j in heads]
            z = [_dot(qs[j], k_t[j], "nt") * scale for j in heads]
            dw = [_dot(douts[j], qkv_refs[3 * j + 2][rows, :], "nt") for j in heads]
            lsz = [_log_sigmoid(z[j]) for j in heads]
            lk = [lsz[j] - z[j] if mask is None else jnp.where(mask, lsz[j] - z[j], 0.0) for j in heads]
            parts = [_split2(lk[j]) for j in heads]
            below = [carry[j][0] + _dot(parts[j][0], incl) + _dot(parts[j][1], incl) for j in heads]
            w = [jnp.exp(lsz[j] + (totals[j] - below[j])) for j in heads]
            if mask is not None:
                w = [jnp.where(mask, w[j], 0.0) for j in heads]
            e = [dw[j] * w[j] for j in heads]
            parts = [_split2(e[j]) for j in heads]
            e_before = [carry[j][1] + _dot(parts[j][0], excl) + _dot(parts[j][1], excl) for j in heads]
            sg = [jnp.exp(lsz[j]) for j in heads]
            dz = [e[j] * (1.0 - sg[j]) - e_before[j] * sg[j] for j in heads]
            if mask is not None:
                dz = [jnp.where(mask, dz[j], 0.0) for j in heads]
            dz = [(dz[j] * scale).astype(BF16) for j in heads]
            for j in heads:
                dk_acc[j, rows, :] += _dot(dz[j], qs[j], "tn")
                dv_acc[j, rows, :] += _dot(w[j], douts[j], "tn")
            return tuple((carry[j][0] + jnp.sum(lk[j], axis=1, keepdims=True),
                          carry[j][1] + jnp.sum(e[j], axis=1, keepdims=True),
                          carry[j][2] + _dot(dz[j], k_t[j])) for j in heads)

        zero = jnp.zeros((tq, 1), F32)
        carry = lax.fori_loop(0, i, lambda kj, cr: tile(kj, cr, None),
                              tuple((zero, zero, jnp.zeros((tq, hd), F32)) for _ in heads))
        carry = tile(i, carry, _diag_mask(True))
        for j in heads:
            out_ref[pl.ds(pl.multiple_of(i * tq, tq), tq), pl.ds(3 * j * hd, hd)] = carry[j][2].astype(BF16)

        @pl.when(i == nq - 1)
        def _():
            for j in heads:
                out_ref[:, pl.ds((3 * j + 1) * hd, hd)] = dk_acc[j].astype(BF16)
                out_ref[:, pl.ds((3 * j + 2) * hd, hd)] = dv_acc[j].astype(BF16)

    wide = ATT_HP * hd
    return pl.pallas_call(
        body, name="sb_bwd", grid=(n_heads // ATT_HP, nq),
        in_specs=_qkv_specs(0, s) + [pl.BlockSpec((tq, wide), lambda g, i: (i, g)),
                                     pl.BlockSpec((tq, LANES), lambda g, i: (i, 0)), ANY],
        out_specs=pl.BlockSpec((s, 3 * wide), lambda g, i: (0, g)),
        out_shape=_sds(qkv.shape, BF16),
        scratch_shapes=[pltpu.VMEM((ATT_HP, s, hd), F32), pltpu.VMEM((ATT_HP, s, hd), F32)],
        compiler_params=_params(("arbitrary", "arbitrary")),
    )(*[qkv] * (3 * ATT_HP), do, tot, dep)


def _fox_fwd(qkv, cum_col, cum_row, n_heads, hb0, dep):
    s = qkv.shape[0]
    scale = HEAD_DIM ** -0.5
    tq, tk = ATT_TQ, ATT_TK

    heads = range(ATT_HP)

    def body(*refs):
        qkv_refs = refs[:3 * ATT_HP]
        cc_ref, cr_ref, _, o_ref, ot_ref, o32_ref, lse_ref = refs[3 * ATT_HP:]
        g, i = pl.program_id(0), pl.program_id(1)

        @pl.when((g == 0) & (i == 0))
        def _():
            lse_ref[...] = jnp.zeros_like(lse_ref)

        qs = [qkv_refs[3 * j][...] for j in heads]
        cqs = [_lane_pick(cc_ref[...], ATT_HP * g + j) for j in heads]

        def tile(kj, carry, mask):
            rows = pl.ds(pl.multiple_of(kj * tk, tk), tk)
            sc = [_dot(qs[j], qkv_refs[3 * j + 1][rows, :], "nt") * scale + cqs[j]
                  - cr_ref[kj, pl.ds(ATT_HP * g + j, 1), :] for j in heads]
            if mask is not None:
                sc = [jnp.where(mask, sc[j], NEG_BIG) for j in heads]
            m_new = [jnp.maximum(carry[j][0], jnp.max(sc[j], axis=1, keepdims=True)) for j in heads]
            p = [jnp.exp(sc[j] - m_new[j]) for j in heads]
            alpha = [jnp.exp(carry[j][0] - m_new[j]) for j in heads]
            parts = [_split2(p[j]) for j in heads]
            v_t = [qkv_refs[3 * j + 2][rows, :] for j in heads]
            pv = [_dot(parts[j][0], v_t[j]) + _dot(parts[j][1], v_t[j]) for j in heads]
            return tuple((m_new[j], alpha[j] * carry[j][1] + jnp.sum(p[j], axis=1, keepdims=True),
                          alpha[j] * carry[j][2] + pv[j]) for j in heads)

        carry = tuple((jnp.full((tq, 1), NEG_BIG, F32), jnp.zeros((tq, 1), F32), jnp.zeros((tq, HEAD_DIM), F32))
                      for _ in heads)
        carry = lax.fori_loop(0, i, lambda kj, cr: tile(kj, cr, None), carry)
        carry = tile(i, carry, _diag_mask(False))
        q_rows = pl.ds(pl.multiple_of(i * tq, tq), tq)
        for j in heads:
            m, l, acc = carry[j]
            o = acc / l
            o_ref[:, _head_cols(j)] = o.astype(BF16)
            ot_ref[_head_cols(j), :] = o.astype(BF16).T
            o32_ref[:, _head_cols(j)] = o
            _lane_put(lse_ref, q_rows, ATT_HP * g + j, m + jnp.log(l))

    nb = cum_row.shape[0]
    wide = ATT_HP * HEAD_DIM
    return pl.pallas_call(
        body, name="fox_fwd", grid=(n_heads // ATT_HP, s // tq),
        in_specs=_qkv_specs(hb0, s) + [pl.BlockSpec((tq, LANES), lambda g, i: (i, 0)),
                                       pl.BlockSpec((nb, 8, tk), lambda g, i: (0, 0, 0)), ANY],
        out_specs=[pl.BlockSpec((tq, wide), lambda g, i: (i, g)), pl.BlockSpec((wide, tq), lambda g, i: (g, i)),
                   pl.BlockSpec((tq, wide), lambda g, i: (i, g)), pl.BlockSpec((s, LANES), lambda g, i: (0, 0))],
        out_shape=[_sds((s, n_heads * HEAD_DIM), BF16), _sds((n_heads * HEAD_DIM, s), BF16),
                   _sds((s, n_heads * HEAD_DIM), F32), _sds((s, LANES), F32)],
        compiler_params=_params(("arbitrary", "arbitrary")),
    )(*[qkv] * (3 * ATT_HP), cum_col, cum_row, dep)


def _fox_bwd(dqkv, qkv, do, o, lse, cum_col, cum_row, n_heads, hb0, dep):
    s = qkv.shape[0]
    scale = HEAD_DIM ** -0.5
    tq, tk = ATT_TQ, ATT_TK
    nq = s // tq
    hd = HEAD_DIM

    heads = range(ATT_HP)
    assert hb0 % ATT_HP == 0

    def body(*refs):
        qkv_refs = refs[1:1 + 3 * ATT_HP]
        do_ref, o_ref, lse_ref, cc_ref, cr_ref, _, out_ref, dc_ref, dk_acc, dv_acc, col_acc = refs[1 + 3 * ATT_HP:]
        g, i = pl.program_id(0), pl.program_id(1)

        @pl.when((g == 0) & (i == 0))
        def _():
            dc_ref[...] = jnp.zeros_like(dc_ref)

        @pl.when(i == 0)
        def _():
            dk_acc[...] = jnp.zeros_like(dk_acc)
            dv_acc[...] = jnp.zeros_like(dv_acc)
            col_acc[...] = jnp.zeros_like(col_acc)

        qs = [qkv_refs[3 * j][...] for j in heads]
        douts = [do_ref[:, _head_cols(j)] for j in heads]
        deltas = [jnp.sum(douts[j].astype(F32) * o_ref[:, _head_cols(j)], axis=1, keepdims=True) for j in heads]
        shifts = [_lane_pick(cc_ref[...], ATT_HP * g + j) - _lane_pick(lse_ref[...], ATT_HP * g + j) for j in heads]

        def tile(kj, carry, mask):
            rows = pl.ds(pl.multiple_of(kj * tk, tk), tk)
            k_t = [qkv_refs[3 * j + 1][rows, :] for j in heads]
            sc = [_dot(qs[j], k_t[j], "nt") * scale + shifts[j] - cr_ref[kj, pl.ds(ATT_HP * g + j, 1), :] for j in heads]
            dp = [_dot(douts[j], qkv_refs[3 * j + 2][rows, :], "nt") for j in heads]
            p = [jnp.exp(sc[j]) for j in heads]
            if mask is not None:
                p = [jnp.where(mask, p[j], 0.0) for j in heads]
            ds_f = [p[j] * (dp[j] - deltas[j]) for j in heads]
            ds = [(ds_f[j] * scale).astype(BF16) for j in heads]
            for j in heads:
                col_acc[j, kj] += jnp.broadcast_to(jnp.sum(ds_f[j], axis=0, keepdims=True), (8, tk))
                dk_acc[j, rows, :] += _dot(ds[j], qs[j], "tn")
                dv_acc[j, rows, :] += _dot(p[j], douts[j], "tn")
            return tuple((carry[j][0] + _dot(ds[j], k_t[j]), carry[j][1] + jnp.sum(ds_f[j], axis=1, keepdims=True))
                         for j in heads)

        carry = lax.fori_loop(0, i, lambda kj, cr: tile(kj, cr, None),
                              tuple((jnp.zeros((tq, hd), F32), jnp.zeros((tq, 1), F32)) for _ in heads))
        carry = tile(i, carry, _diag_mask(False))
        q_rows = pl.ds(pl.multiple_of(i * tq, tq), tq)
        for j in heads:
            out_ref[q_rows, pl.ds(3 * j * hd, hd)] = carry[j][0].astype(BF16)
            _lane_put(dc_ref, q_rows, ATT_HP * g + j, carry[j][1])

        @pl.when(i == nq - 1)
        def _():
            lane = lax.broadcasted_iota(jnp.int32, (tk, LANES), 1)
            for j in heads:
                out_ref[:, pl.ds((3 * j + 1) * hd, hd)] = dk_acc[j].astype(BF16)
                out_ref[:, pl.ds((3 * j + 2) * hd, hd)] = dv_acc[j].astype(BF16)
                for kj in range(nb):
                    col = jnp.broadcast_to(col_acc[j, kj][0:1, :], (LANES, tk)).T
                    old = dc_ref[pl.ds(kj * tk, tk), :]
                    dc_ref[pl.ds(kj * tk, tk), :] = jnp.where(lane == ATT_HP * g + j, old - col, old)

    nb = cum_row.shape[0]
    wide = ATT_HP * hd
    return pl.pallas_call(
        body, name="fox_bwd", grid=(n_heads // ATT_HP, nq),
        in_specs=[ANY] + _qkv_specs(hb0, s) + [
            pl.BlockSpec((tq, wide), lambda g, i: (i, g)), pl.BlockSpec((tq, wide), lambda g, i: (i, g)),
            pl.BlockSpec((tq, LANES), lambda g, i: (i, 0)), pl.BlockSpec((tq, LANES), lambda g, i: (i, 0)),
            pl.BlockSpec((nb, 8, tk), lambda g, i: (0, 0, 0)), ANY],
        out_specs=[pl.BlockSpec((s, 3 * wide), lambda g, i: (0, hb0 // ATT_HP + g)),
                   pl.BlockSpec((s, LANES), lambda g, i: (0, 0))],
        out_shape=[_sds(dqkv.shape, BF16), _sds((s, LANES), F32)],
        scratch_shapes=[pltpu.VMEM((ATT_HP, s, hd), F32), pltpu.VMEM((ATT_HP, s, hd), F32),
                        pltpu.VMEM((ATT_HP, s // tk, 8, tk), F32)],
        input_output_aliases={0: 0},
        compiler_params=_params(("arbitrary", "arbitrary")),
    )(dqkv, *[qkv] * (3 * ATT_HP), do, o, lse, cum_col, cum_row, dep)


def _branch_merge(o_sb, o_fx, w_sb, w_fx, gf, dep, tm=1024):
    s = o_sb.shape[0]
    cs = w_sb.shape[2]
    tm = _tile(s, tm)

    def body(osb_ref, ofx_ref, wsb_ref, wfx_ref, g_ref, dep_ref, merged_ref, mt_ref, asb_ref, afx_ref):
        del dep_ref
        a_sb = _dot(osb_ref[...], wsb_ref[...])
        a_fx = _dot(ofx_ref[...], wfx_ref[...])
        g = g_ref[...]
        merged = (_sigmoid(g[:, :cs]) * a_sb + _sigmoid(g[:, cs:]) * a_fx).astype(BF16)
        merged_ref[...] = merged
        mt_ref[...] = merged.T
        asb_ref[...] = a_sb.astype(BF16)
        afx_ref[...] = a_fx.astype(BF16)

    blk = pl.BlockSpec((tm, cs), lambda i, j: (i, j))
    out = _sds((s, N_DEV * cs), BF16)
    return pl.pallas_call(
        body, name="branch_merge", grid=(s // tm, N_DEV),
        in_specs=[pl.BlockSpec((tm, o_sb.shape[1]), lambda i, j: (i, 0)),
                  pl.BlockSpec((tm, o_fx.shape[1]), lambda i, j: (i, 0)),
                  pl.BlockSpec((None,) + w_sb.shape[1:], lambda i, j: (j, 0, 0)),
                  pl.BlockSpec((None,) + w_fx.shape[1:], lambda i, j: (j, 0, 0)),
                  pl.BlockSpec((tm, 2 * cs), lambda i, j: (i, j)), ANY],
        out_specs=[blk, pl.BlockSpec((cs, tm), lambda i, j: (j, i)), blk, blk],
        out_shape=[out, _sds((N_DEV * cs, s), BF16), out, out],
        compiler_params=_params(("parallel", "arbitrary")),
    )(o_sb, o_fx, w_sb, w_fx, gf, dep)


def _merge_bwd(dmix, w_out, gf, a_sb, a_fx, tm=1024, tk=2048, dep=None):
    s, d = dmix.shape
    cs = d // N_DEV
    tm, tk = _tile(s, tm), _tile(d, tk)

    def epilogue(acc, ex, outs):
        g, a_sb, a_fx = ex[0][...], ex[1][...].astype(F32), ex[2][...].astype(F32)
        s_sb, s_fx = _sigmoid(g[:, :cs]), _sigmoid(g[:, cs:])
        outs[0][...] = (acc * s_sb).astype(BF16)
        outs[1][...] = (acc * s_fx).astype(BF16)
        outs[2][...] = jnp.concatenate([acc * a_sb * s_sb * (1.0 - s_sb), acc * a_fx * s_fx * (1.0 - s_fx)],
                                       axis=1).astype(BF16)

    blk = pl.BlockSpec((tm, cs), lambda i, j, k: (i, j))
    wide = pl.BlockSpec((tm, 2 * cs), lambda i, j, k: (i, j))
    return _matmul(
        "merge_bwd", "nt",
        [(dmix, pl.BlockSpec((tm, tk), lambda i, j, k: (i, k)), w_out, pl.BlockSpec((cs, tk), lambda i, j, k: (j, k)))],
        (s // tm, N_DEV, d // tk), (tm, cs),
        [_sds((s, d), BF16), _sds((s, d), BF16), _sds(gf.shape, BF16)], [blk, blk, wide],
        extras=[(gf, wide), (a_sb, blk), (a_fx, blk)], epilogue=epilogue, dep=dep)


def _ffn_up(u2, w_gate, w_up, dep, tm=1024):
    s, d = u2.shape
    fs = w_gate.shape[2]
    tm = _tile(s, tm)

    def body(u_ref, wg_ref, wu_ref, dep_ref, gate_ref, up_ref, act_ref, actt_ref):
        del dep_ref
        u = u_ref[...]
        gate = _dot(u, wg_ref[...])
        up = _dot(u, wu_ref[...])
        gate_ref[...] = gate
        up_ref[...] = up
        act = (gate * _sigmoid(gate) * up).astype(BF16)
        act_ref[...] = act
        actt_ref[...] = act.T

    w_spec = pl.BlockSpec((None, d, fs), lambda i, j: (j, 0, 0))
    o_spec = pl.BlockSpec((None, tm, fs), lambda i, j: (j, i, 0))
    return pl.pallas_call(
        body, name="ffn_up", grid=(s // tm, N_DEV),
        in_specs=[pl.BlockSpec((tm, d), lambda i, j: (i, 0)), w_spec, w_spec, ANY],
        out_specs=[o_spec, o_spec, o_spec, pl.BlockSpec((None, fs, tm), lambda i, j: (j, 0, i))],
        out_shape=[_sds((N_DEV, s, fs), F32), _sds((N_DEV, s, fs), F32), _sds((N_DEV, s, fs), BF16),
                   _sds((N_DEV, fs, s), BF16)],
        compiler_params=_params(("parallel", "arbitrary")),
    )(u2, w_gate, w_up, dep)


def _ffn_down_bwd(dff, w_down, gate, up, tm=1024):
    s, d = dff.shape
    fs = w_down.shape[1]
    tm = _tile(s, tm)

    def body(dff_ref, wd_ref, gate_ref, up_ref, dgate_ref, dup_ref):
        dact = _dot(dff_ref[...], wd_ref[...], "nt")
        gate = gate_ref[...]
        sg = _sigmoid(gate)
        dup_ref[...] = (dact * gate * sg).astype(BF16)
        dgate_ref[...] = (dact * up_ref[...] * sg * (1.0 + gate * (1.0 - sg))).astype(BF16)

    a_spec = pl.BlockSpec((None, tm, fs), lambda i, j: (j, i, 0))
    return pl.pallas_call(
        body, name="ffn_down_bwd", grid=(s // tm, N_DEV),
        in_specs=[pl.BlockSpec((tm, d), lambda i, j: (i, 0)), pl.BlockSpec((None, fs, d), lambda i, j: (j, 0, 0)),
                  a_spec, a_spec],
        out_specs=[a_spec, a_spec],
        out_shape=[_sds((N_DEV, s, fs), BF16), _sds((N_DEV, s, fs), BF16)],
        compiler_params=_params(("parallel", "arbitrary")),
    )(dff, w_down, gate, up)


def _mesh_place():
    x, y, c = lax.axis_index("x"), lax.axis_index("y"), lax.axis_index("c")
    peers = []
    for d in range(1, N_DEV):
        px = 1 - x if d & 4 else x
        py = 1 - y if d & 2 else y
        pc = 1 - c if d & 1 else c
        peers.append((d, (px, py, pc), 4 * px + 2 * py + pc))
    return 4 * x + 2 * y + c, peers


def _flat_me():
    return 4 * lax.axis_index("x") + 2 * lax.axis_index("y") + lax.axis_index("c")


def _in_hbm(a):
    return pltpu.with_memory_space_constraint(a, pltpu.HBM)


def _pair_plan():
    x, y, c = lax.axis_index("x"), lax.axis_index("y"), lax.axis_index("c")
    return [(2 * q + (1 - c), q, q, (x, y, 1 - c)) for q in range(4)]


def _chip_plan():
    x, y, c = lax.axis_index("x"), lax.axis_index("y"), lax.axis_index("c")
    plan = []
    for fx, fy in ((1, 0), (0, 1), (1, 1)):
        cx, cy = (1 - x if fx else x), (1 - y if fy else y)
        plan.append((2 * cx + cy, 2 * x + y, 2 * cx + cy, (cx, cy, c)))
    return plan


def _split_start(name, srcs, lands, plan, k):
    n = len(srcs)

    def body(*refs):
        ins, lnd = refs[:n], refs[n:2 * n]
        send, recv, token = refs[2 * n], refs[2 * n + 1], refs[-1]
        copies = plan()
        for a in range(n):
            for t, (src, dst, _, dev) in enumerate(copies):
                pltpu.make_async_remote_copy(src_ref=ins[a].at[src], dst_ref=lnd[a].at[dst], send_sem=send.at[k * a + t],
                                             recv_sem=recv.at[k * a + t], device_id=dev, device_id_type=MESH).start()
        token[...] = jnp.zeros_like(token)

    res = pl.pallas_call(
        body, name=name,
        out_shape=[pltpu.SemaphoreType.DMA((n * k,)), pltpu.SemaphoreType.DMA((n * k,))]
        + [pltpu.HBM(a.shape, a.dtype) for a in list(srcs) + list(lands)] + [_sds((8, LANES), F32)],
        in_specs=[HBM] * (2 * n), out_specs=[SEM, SEM] + [HBM] * (2 * n) + [pl.BlockSpec(memory_space=pltpu.VMEM)],
        input_output_aliases={i: 2 + i for i in range(2 * n)},
        compiler_params=pltpu.CompilerParams(has_side_effects=EFFECT),
    )(*[_in_hbm(a) for a in srcs], *[_in_hbm(a) for a in lands])
    return res[0], res[1], res[2:2 + n], res[2 + n:2 + 2 * n], res[-1]


def _split_wait(name, send, recv, srcs, lands, plan, k, after):
    n = len(srcs)

    def body(*refs):
        ins, lnd = refs[:n], refs[n:2 * n]
        send_sem, recv_sem = refs[2 * n], refs[2 * n + 1]
        copies = plan()
        for a in range(n):
            for t, (src, _, dst, dev) in enumerate(copies):
                cp = pltpu.make_async_remote_copy(src_ref=ins[a].at[src], dst_ref=lnd[a].at[dst], send_sem=send_sem.at[k * a + t],
                                                  recv_sem=recv_sem.at[k * a + t], device_id=dev, device_id_type=MESH)
                cp.wait_send()
                cp.wait_recv()

    res = pl.pallas_call(
        body, name=name,
        out_shape=[pltpu.HBM(a.shape, a.dtype) for a in list(srcs) + list(lands)],
        in_specs=[HBM] * (2 * n) + [SEM, SEM] + [ANY] * len(after), out_specs=[HBM] * (2 * n),
        input_output_aliases={i: i for i in range(2 * n)},
        compiler_params=pltpu.CompilerParams(has_side_effects=EFFECT),
    )(*srcs, *lands, send, recv, *after)
    return res[:n], res[n:]


def _pair_add(name, parts, land):
    _, r, cols = parts.shape
    tr = max(16, min(r, ((1 << 20) // (2 * cols)) // 16 * 16))
    while r % tr:
        tr -= 16

    def body(c_ref, p_ref, l_ref, o_ref):
        del c_ref
        o_ref[...] = (p_ref[...].astype(F32) + l_ref[...].astype(F32)).astype(BF16)

    blk = pl.BlockSpec((None, tr, cols), lambda q, i, c_ref: (q, i, 0))
    return pl.pallas_call(
        body, name=name,
        grid_spec=pltpu.PrefetchScalarGridSpec(
            num_scalar_prefetch=1, grid=(4, r // tr),
            in_specs=[pl.BlockSpec((None, tr, cols), lambda q, i, c_ref: (2 * q + c_ref[0], i, 0)), blk], out_specs=blk),
        out_shape=_sds((4, r, cols), BF16),
        compiler_params=_params(("parallel", "parallel")),
    )(jnp.reshape(lax.axis_index("c"), (1,)).astype(jnp.int32), parts, land)


def _scatter_pairs(tag, parts):
    lands = [lax.empty((4,) + a.shape[1:], a.dtype) for a in parts]
    return _split_start("pair_" + tag, parts, lands, _pair_plan, 4)


def _scatter_chips(tag, started, after):
    send, recv, parts, lands, _ = started
    parts, lands = _split_wait("pair_" + tag + "_wait", send, recv, parts, lands, _pair_plan, 4, [after])
    sums = [_pair_add("pair_" + tag + "_add%d" % a, p, l) for a, (p, l) in enumerate(zip(parts, lands))]
    chip = 2 * lax.axis_index("x") + lax.axis_index("y")
    final = [lax.dynamic_update_slice_in_dim(lax.empty(v.shape, v.dtype), lax.dynamic_slice_in_dim(v, chip, 1, 0), chip, 0)
             for v in sums]
    return _split_start("chips_" + tag, sums, final, _chip_plan, 3)


def _scatter_end(tag, started, after):
    send, recv, sums, final, _ = started
    return _split_wait("chips_" + tag + "_wait", send, recv, sums, final, _chip_plan, 3, after)[1]


def _gather_targets():
    x, y, c = lax.axis_index("x"), lax.axis_index("y"), lax.axis_index("c")
    chips = [(x, y), (1 - x, y), (x, 1 - y), (1 - x, 1 - y)]
    same = [((cx, cy, c), 4 * cx + 2 * cy + c) for cx, cy in chips]
    other = [((cx, cy, 1 - c), 4 * cx + 2 * cy + 1 - c) for cx, cy in chips]
    return same[0][1], [other[0]] + same[1:], [flat for _, flat in other[1:]], other[0][0]


def _gather_start(shards):
    n = len(shards)
    me = _flat_me()
    lands = [lax.dynamic_update_slice_in_dim(lax.empty((N_DEV,) + a.shape, a.dtype), a[None], me, 0) for a in shards]

    def body(*refs):
        lnd, send, recv, token = refs[:n], refs[n], refs[n + 1], refs[-1]
        mine, targets, _, _ = _gather_targets()
        for a in range(n):
            for t, (dev, _) in enumerate(targets):
                pltpu.make_async_remote_copy(src_ref=lnd[a].at[mine], dst_ref=lnd[a].at[mine], send_sem=send.at[4 * a + t],
                                             recv_sem=recv.at[4 * a + t], device_id=dev, device_id_type=MESH).start()
        token[...] = jnp.zeros_like(token)

    res = pl.pallas_call(
        body, name="gather_start",
        out_shape=[pltpu.SemaphoreType.DMA((4 * n,)), pltpu.SemaphoreType.DMA((4 * n,))]
        + [pltpu.HBM(a.shape, a.dtype) for a in lands] + [_sds((8, LANES), F32)],
        in_specs=[HBM] * n, out_specs=[SEM, SEM] + [HBM] * n + [pl.BlockSpec(memory_space=pltpu.VMEM)],
        input_output_aliases={i: 2 + i for i in range(n)},
        compiler_params=pltpu.CompilerParams(has_side_effects=EFFECT),
    )(*[_in_hbm(a) for a in lands])
    return res[0], res[1], list(res[2:2 + n]), res[-1]


def _gather_forward(name, lands, first, send, recv, after):
    n = len(lands)

    def body(*refs):
        lnd, send_sem, recv_sem = refs[:n], refs[n], refs[n + 1]
        send2, recv2, token = refs[-3], refs[-2], refs[-1]
        mine, targets, _, sibling = _gather_targets()
        for a in range(n):
            for t, (dev, flat) in enumerate(targets):
                cp = pltpu.make_async_remote_copy(src_ref=lnd[a].at[mine], dst_ref=lnd[a].at[flat],
                                                  send_sem=send_sem.at[4 * (first + a) + t],
                                                  recv_sem=recv_sem.at[4 * (first + a) + t], device_id=dev, device_id_type=MESH)
                cp.wait_send()
                if t:
                    cp.wait_recv()
                    pltpu.make_async_remote_copy(src_ref=lnd[a].at[flat], dst_ref=lnd[a].at[flat], send_sem=send2.at[3 * a + t - 1],
                                                 recv_sem=recv2.at[3 * a + t - 1], device_id=sibling, device_id_type=MESH).start()
        token[...] = jnp.zeros_like(token)

    res = pl.pallas_call(
        body, name=name,
        out_shape=[pltpu.HBM(a.shape, a.dtype) for a in lands]
        + [pltpu.SemaphoreType.DMA((3 * n,)), pltpu.SemaphoreType.DMA((3 * n,)), _sds((8, LANES), F32)],
        in_specs=[HBM] * n + [SEM, SEM] + [ANY] * len(after),
        out_specs=[HBM] * n + [SEM, SEM, pl.BlockSpec(memory_space=pltpu.VMEM)],
        input_output_aliases={i: i for i in range(n)},
        compiler_params=pltpu.CompilerParams(has_side_effects=EFFECT),
    )(*lands, send, recv, *after)
    return list(res[:n]), res[n], res[n + 1], res[-1]


def _gather_wait(name, lands, first, recv, send2, recv2, after):
    n = len(lands)

    def body(*refs):
        lnd, recv_sem, send2_sem, recv2_sem = refs[:n], refs[n], refs[n + 1], refs[n + 2]
        mine, targets, passed, sibling = _gather_targets()
        for a in range(n):
            dev, flat = targets[0]
            pltpu.make_async_remote_copy(src_ref=lnd[a].at[mine], dst_ref=lnd[a].at[flat], send_sem=send2_sem.at[3 * a],
                                         recv_sem=recv_sem.at[4 * (first + a)], device_id=dev, device_id_type=MESH).wait_recv()
            for t in range(3):
                cp = pltpu.make_async_remote_copy(src_ref=lnd[a].at[targets[t + 1][1]], dst_ref=lnd[a].at[passed[t]],
                                                  send_sem=send2_sem.at[3 * a + t], recv_sem=recv2_sem.at[3 * a + t],
                                                  device_id=sibling, device_id_type=MESH)
                cp.wait_send()
                cp.wait_recv()

    res = pl.pallas_call(
        body, name=name, out_shape=[pltpu.HBM(a.shape, a.dtype) for a in lands],
        in_specs=[HBM] * n + [SEM, SEM, SEM, ANY], out_specs=[HBM] * n,
        input_output_aliases={i: i for i in range(n)},
        compiler_params=pltpu.CompilerParams(has_side_effects=EFFECT),
    )(*lands, recv, send2, recv2, after)
    return list(res)


def _adamw_decay(w, m, v):
    return ADAM_WD * w, ADAM_B1 * m, ADAM_B2 * v


def _adamw_finish(g, wd_w, m1, v1):
    m = m1 + (1.0 - ADAM_B1) * g
    v = v1 + (1.0 - ADAM_B2) * (g * g)
    m_hat = m / (1.0 - ADAM_B1 ** ADAM_STEP)
    v_hat = v / (1.0 - ADAM_B2 ** ADAM_STEP)
    delta = -ADAM_LR * (m_hat / (jnp.sqrt(v_hat) + ADAM_EPS) + wd_w)
    return delta, m, v


def _adamw(g, w, m, v):
    return _adamw_finish(g, *_adamw_decay(w, m, v))


def _update_prep(name, w, m, v, dep, block_bytes=1 << 20):
    _, r, c = w.shape
    tr = max(8, min(r, (block_bytes // (4 * c)) // 8 * 8))
    while r % tr:
        tr -= 8

    def body(w_ref, m_ref, v_ref, dep_ref, ow_ref, om_ref, ov_ref):
        del dep_ref
        ow_ref[...], om_ref[...], ov_ref[...] = _adamw_decay(w_ref[...], m_ref[...], v_ref[...])

    blk = pl.BlockSpec((None, tr, c), lambda i: (0, i, 0))
    return pl.pallas_call(
        body, name=name, grid=(r // tr,), in_specs=[blk] * 3 + [ANY], out_specs=[blk] * 3,
        out_shape=[_sds((1, r, c), F32)] * 3, compiler_params=_params(("parallel",)),
    )(w, m, v, dep)


def _update(name, parts, w, m, v, layout=None, decayed=False, transposed_out=False, block_bytes=1 << 20):
    _, r, c = w.shape
    n_slots, _, cp = parts.shape
    tr = max(8, min(r, (block_bytes // (4 * cp)) // 8 * 8))
    if transposed_out:
        tr = _tile(r, 256)
    while r % tr:
        tr -= 8

    def body(p_ref, w_ref, m_ref, v_ref, g_ref, d_ref, nm_ref, nv_ref, *scratch):
        g = p_ref[0].astype(F32)
        for p in range(1, n_slots):
            g = g + p_ref[p].astype(F32)
        if layout is not None:
            s1, s2, lg = layout.my_shifts()
            lane = lax.broadcasted_iota(jnp.int32, g.shape, 1)
            scratch[0][...] = jnp.where(lane < lg, pltpu.roll(g, cp - s1, 1), pltpu.roll(g, cp - s2, 1))
            g = scratch[0][:, 0:c]
        step = _adamw_finish if decayed else _adamw
        results = (g,) + step(g, w_ref[...], m_ref[...], v_ref[...])
        for ref, val in zip((g_ref, d_ref, nm_ref, nv_ref), results):
            ref[...] = val.T if transposed_out else val

    blk = pl.BlockSpec((None, tr, c), lambda i: (0, i, 0))
    out_blk = pl.BlockSpec((None, c, tr), lambda i: (0, 0, i)) if transposed_out else blk
    res = pl.pallas_call(
        body, name=name, grid=(r // tr,),
        in_specs=[pl.BlockSpec((n_slots, tr, cp), lambda i: (0, i, 0)), blk, blk, blk],
        out_specs=[out_blk] * 4, out_shape=[_sds((1, c, r) if transposed_out else (1, r, c), F32)] * 4,
        scratch_shapes=[] if layout is None else [pltpu.VMEM((tr, cp), F32)],
        compiler_params=_params(("parallel",)),
    )(parts, w, m, v)
    return [jnp.transpose(o, (0, 2, 1)) for o in res] if transposed_out else res


def _small_update(part, w, m, v):
    n = part.shape[1]

    def body(p_ref, w_ref, m_ref, v_ref, g_ref, d_ref, nm_ref, nv_ref, buf, send, recv):
        me, peers = _mesh_place()
        buf[me] = p_ref[...]
        sent = []
        for d, dev, flat in peers:
            cp = pltpu.make_async_remote_copy(src_ref=p_ref, dst_ref=buf.at[me], send_sem=send.at[d],
                                              recv_sem=recv.at[d], device_id=dev, device_id_type=MESH)
            cp.start()
            sent.append(cp)
        for d, dev, flat in peers:
            pltpu.make_async_remote_copy(src_ref=p_ref, dst_ref=buf.at[flat], send_sem=send.at[d],
                                         recv_sem=recv.at[d], device_id=dev, device_id_type=MESH).wait_recv()
        for cp in sent:
            cp.wait_send()
        g = buf[0]
        for p in range(1, N_DEV):
            g = g + buf[p]
        g_ref[...] = g
        d_ref[...], nm_ref[...], nv_ref[...] = _adamw(g, w_ref[...], m_ref[...], v_ref[...])

    vm = pl.BlockSpec(memory_space=pltpu.VMEM)
    return pl.pallas_call(
        body, name="small_update", in_specs=[vm] * 4, out_specs=[vm] * 4, out_shape=[_sds((1, n), F32)] * 4,
        scratch_shapes=[pltpu.VMEM((N_DEV, 1, n), F32), pltpu.SemaphoreType.DMA((N_DEV,)),
                        pltpu.SemaphoreType.DMA((N_DEV,))],
    )(part, w, m, v)


class _WInLayout:
    def __init__(self, n8, n_f, d_sb, d_fox, d):
        assert n8 % LANES == 1 and n_f < LANES and d % (N_DEV * LANES) == 0
        self.n8, self.n_f, self.d = n8, n_f, d
        self.sp = n8 // LANES
        self.wp = (n8 + 2 * LANES - 2) // LANES * LANES
        self.n_qkv = 3 * (d_sb + d_fox)
        nq, dt, tc = self.n_qkv // LANES, d // LANES, d // N_DEV // LANES
        h_sb, h_fox = d_sb // HEAD_DIM, d_fox // HEAD_DIM
        self.sources = {}
        self.part_tile = {}
        for p in range(N_DEV):
            lg = min(max(self.n_qkv + n_f - n8 * p, 0), n8)
            s1, s2 = p, p + LANES - n_f
            spans = []
            if lg > 0:
                spans.append(("a", self.sp * p, s1 // LANES, (lg + s1 - 1) // LANES))
            if lg < n8:
                spans.append(("g", self.sp * p - 1 - nq, (lg + s2) // LANES, (n8 - 1 + s2) // LANES))
            for kind, base, first, last in spans:
                for i in range(first, last + 1):
                    assert (p, i) not in self.part_tile
                    self.part_tile[(p, i)] = (kind, base + i)
                    self.sources.setdefault((kind, base + i), []).append((p, i))
        self.cat_tiles = [("a", r * h_sb + h) for h in range(h_sb) for r in range(3)]
        self.cat_tiles += [("a", 3 * h_sb + r * h_fox + h) for h in range(h_fox) for r in range(3)]
        self.cat_tiles += [("g", which * dt + j * tc + half) for j in range(N_DEV) for which in (0, 1) for half in range(tc)]
        self.cat_tiles += [("a", nq)] + [None] * (F_PAD // LANES - 1)
        self.cat_index = {key: c for c, key in enumerate(self.cat_tiles) if key is not None}

    def my_shifts(self):
        me = _flat_me()
        return me, me + LANES - self.n_f, jnp.clip(self.n_qkv + self.n_f - self.n8 * me, 0, self.n8)


def _lane_tile(i):
    return pl.ds(i * LANES, LANES)


def _w_in_shift(w_in, lay, tr=256):
    _, d, n8 = w_in.shape
    kd = d // LANES
    kt = tr // LANES
    by_col = jnp.transpose(w_in, (0, 2, 1)).reshape(n8 * kd, LANES)

    def body(w_ref, o_ref, buf):
        k0 = kt * pl.program_id(0)
        buf[...] = jnp.zeros_like(buf)
        for j in range(n8 // LANES):
            for kk in range(kt):
                piece = w_ref[pl.ds(j * LANES * kd + k0 + kk, LANES, stride=kd), :]
                buf[kk * LANES:(kk + 1) * LANES, j * LANES:(j + 1) * LANES] = piece.T
        first = lax.broadcasted_iota(jnp.int32, (8, LANES), 0) == 0
        for kk in range(kt):
            row = w_ref[pl.ds((n8 - 1) * kd + k0 + kk, 1), :]
            buf[kk * LANES:(kk + 1) * LANES, n8 - 1:n8 + 7] = jnp.where(first, jnp.broadcast_to(row, (8, LANES)), 0.0).T
        v = buf[...]
        s1, s2, lg = lay.my_shifts()
        pos = lax.broadcasted_iota(jnp.int32, v.shape, 1)
        o_ref[...] = jnp.where(pos < lg + s1, pltpu.roll(v, s1, 1),
                               jnp.where(pos >= lg + s2, pltpu.roll(v, s2, 1), 0.0)).astype(BF16)

    return pl.pallas_call(
        body, name="w_in_shift", grid=(d // tr,),
        in_specs=[pl.BlockSpec((n8 * kd, LANES), lambda i: (0, 0))],
        out_specs=pl.BlockSpec((tr, lay.wp), lambda i: (i, 0)), out_shape=_sds((d, lay.wp), BF16),
        scratch_shapes=[pltpu.VMEM((tr, lay.wp), F32)],
        compiler_params=_params(("arbitrary",)),
    )(by_col)


def _w_in_build(g_in, lay, tr=256):
    d = g_in.shape[1]
    width = len(lay.cat_tiles) * LANES

    def body(g_ref, o_ref):
        for c, key in enumerate(lay.cat_tiles):
            if key is None:
                o_ref[:, _lane_tile(c)] = jnp.zeros((tr, LANES), BF16)
                continue
            (p, i), *more = lay.sources[key]
            val = g_ref[p, :, _lane_tile(i)]
            for p2, i2 in more:
                val = val + g_ref[p2, :, _lane_tile(i2)]
            o_ref[:, _lane_tile(c)] = val

    return pl.pallas_call(
        body, name="w_in_build", grid=(d // tr,),
        in_specs=[pl.BlockSpec((N_DEV, tr, lay.wp), lambda i: (0, i, 0))],
        out_specs=pl.BlockSpec((tr, width), lambda i: (i, 0)), out_shape=_sds((d, width), BF16),
        compiler_params=_params(("parallel",)),
    )(g_in)


def _w_in_grad_parts(dwq, dwgf, lay, tr=256):
    d = dwq.shape[0]
    nq = lay.n_qkv // LANES

    def body(q_ref, g_ref, o_ref):
        for p in range(N_DEV):
            for i in range(lay.wp // LANES):
                key = lay.part_tile.get((p, i))
                if key is None:
                    o_ref[p, :, _lane_tile(i)] = jnp.zeros((tr, LANES), BF16)
                    continue
                c = lay.cat_index[key]
                o_ref[p, :, _lane_tile(i)] = q_ref[:, _lane_tile(c)] if c < nq else g_ref[:, _lane_tile(c - nq)]

    return pl.pallas_call(
        body, name="w_in_grad_parts", grid=(d // tr,),
        in_specs=[pl.BlockSpec((tr, dwq.shape[1]), lambda i: (i, 0)), pl.BlockSpec((tr, dwgf.shape[1]), lambda i: (i, 0))],
        out_specs=pl.BlockSpec((N_DEV, tr, lay.wp), lambda i: (0, i, 0)), out_shape=_sds((N_DEV, d, lay.wp), BF16),
        compiler_params=_params(("parallel",)),
    )(dwq, dwgf)


def kernel(x, norm_mix_pre, norm_mix_post, w_in, b_forget, w_branch_sb, w_branch_fox, w_out, norm_ffn_pre, norm_ffn_post, w_ffn_gate, w_ffn_up, w_ffn_down, loss_target, m_norm_mix_pre, m_norm_mix_post, m_w_in, m_b_forget, m_w_branch_sb, m_w_branch_fox, m_w_out, m_norm_ffn_pre, m_norm_ffn_post, m_w_ffn_gate, m_w_ffn_up, m_w_ffn_down, v_norm_mix_pre, v_norm_mix_post, v_w_in, v_b_forget, v_w_branch_sb, v_w_branch_fox, v_w_out, v_norm_ffn_pre, v_norm_ffn_post, v_w_ffn_gate, v_w_ffn_up, v_w_ffn_down):
    xs, target = x[0], loss_target[0]
    s, d = xs.shape
    d_sb, d_fox = w_branch_sb.shape[1], w_branch_fox.shape[1]
    h_sb, h_fox = d_sb // HEAD_DIM, d_fox // HEAD_DIM
    n_f = b_forget.shape[1]
    fs = w_ffn_gate.shape[2]
    cs = d // N_DEV
    n_qkv = 3 * (d_sb + d_fox)
    n_gf = 2 * d + F_PAD
    f_blk = 2 * d // LANES
    big = (w_in, w_branch_sb, w_branch_fox, w_out, w_ffn_gate, w_ffn_up, w_ffn_down)
    big_m = (m_w_in, m_w_branch_sb, m_w_branch_fox, m_w_out, m_w_ffn_gate, m_w_ffn_up, m_w_ffn_down)
    big_v = (v_w_in, v_w_branch_sb, v_w_branch_fox, v_w_out, v_w_ffn_gate, v_w_ffn_up, v_w_ffn_down)

    lay = _WInLayout(w_in.shape[2], n_f, d_sb, d_fox, d)
    send1, recv1, lands, token = _gather_start([_w_in_shift(w_in, lay)] + [w[0].astype(BF16) for w in big[1:]])
    b_pad = jnp.pad(b_forget, ((0, 0), (0, LANES - n_f)))

    started = token[0, 0]
    u, u_t = _pre_norm(xs, norm_mix_pre, dep=token)
    weights = dict(zip(("w_in", "w_branch_sb", "w_branch_fox", "w_out", "w_ffn_gate", "w_ffn_up", "w_ffn_down"),
                       zip(big, big_m, big_v)))
    decayed = {nm: _update_prep("decay_" + nm, *[t + started for t in weights[nm]], u)
               for nm in ("w_in", "w_ffn_gate", "w_ffn_up")}
    l_in, send2, recv2, token = _gather_forward("gather_in_forward", lands[0:1], 0, send1, recv1,
                                                [u] + [t[2] for t in decayed.values()])
    (g_in,) = _gather_wait("gather_in_wait", l_in, 0, recv1, send2, recv2, token)
    w_cat = _w_in_build(g_in, lay)
    qkv = _mm_plain("proj_qkv", "nn", u, w_cat, BF16, n=n_qkv)
    gf = _mm_plain("proj_gates", "nn", u, w_cat, F32, n_off=n_qkv, n=n_gf)
    cum_col, cum_row = _forget_fwd(gf, b_pad, f_blk)
    o_sb, o_sb_t, tot = _sb_fwd(qkv, h_sb)
    l_mid, send2, recv2, token = _gather_forward("gather_mid_forward", lands[1:4], 1, send1, recv1, [o_sb])
    o_fx, o_fx_t, o_fx32, lse = _fox_fwd(qkv, cum_col, cum_row, h_fox, h_sb, token)
    g_sb, g_fx, g_out = _gather_wait("gather_mid_wait", l_mid, 1, recv1, send2, recv2, o_fx)
    w_out_full = g_out.reshape(d, d)
    merged, merged_t, a_sb, a_fx = _branch_merge(o_sb, o_fx, g_sb, g_fx, gf, o_fx)
    l_ffn, send2, recv2, token = _gather_forward("gather_ffn_forward", lands[4:6], 4, send1, recv1, [merged])
    mix = _mm_plain("out_proj", "nn", merged, w_out_full, F32, dep=token)
    h1, u2, u2_t = _mid_norms(xs, mix, norm_mix_post, norm_ffn_pre)
    g_gate, g_up = _gather_wait("gather_ffn_wait", l_ffn, 4, recv1, send2, recv2, u2)
    l_down, send2, recv2, token = _gather_forward("gather_down_forward", lands[6:7], 6, send1, recv1, [u2])
    gate, up, act, act_t = _ffn_up(u2, g_gate, g_up, token)
    (g_down,) = _gather_wait("gather_down_wait", l_down, 6, recv1, send2, recv2, act)
    tm, tn = _tile(s, 1024), _tile(d, 1024)
    ff = _matmul("ffn_down", "nn",
                 [(act, pl.BlockSpec((None, tm, fs), lambda i, j, k: (k, i, 0)),
                   g_down, pl.BlockSpec((None, fs, tn), lambda i, j, k: (k, 0, j)))],
                 (s // tm, d // tn, N_DEV), (tm, tn), _sds((s, d), F32), pl.BlockSpec((tm, tn), lambda i, j, k: (i, j)))
    loss_part, dy, dff, dg_ffn_post = _loss_head(h1, ff, target, norm_ffn_post)

    dgate, dup = _ffn_down_bwd(dff, g_down, gate, up)
    dw_down = _matmul("dw_down", "nn",
                      [(act_t, pl.BlockSpec((None, fs, s), lambda j, n, k: (j, 0, 0)),
                        dff, pl.BlockSpec((s, tn), lambda j, n, k: (0, n)))],
                      (N_DEV, d // tn, 1), (fs, tn), _sds((N_DEV, fs, d), BF16),
                      pl.BlockSpec((None, fs, tn), lambda j, n, k: (j, 0, n)))

    def dw_up(name, dact):
        return _matmul(name, "nn",
                       [(u2_t, pl.BlockSpec((tn, s), lambda j, i, k: (i, 0)),
                         dact, pl.BlockSpec((None, s, fs), lambda j, i, k: (j, 0, 0)))],
                       (N_DEV, d // tn, 1), (tn, fs), _sds((N_DEV, d, fs), BF16),
                       pl.BlockSpec((None, tn, fs), lambda j, i, k: (j, i, 0)))

    dw_gate, dw_upw = dw_up("dw_gate", dgate), dw_up("dw_up", dup)
    rs_ffn = _scatter_pairs("ffn", [dw_gate, dw_upw, dw_down])
    a_spec = pl.BlockSpec((None, tm, fs), lambda i, j, k: (k, i, 0))
    b_spec = pl.BlockSpec((None, tn, fs), lambda i, j, k: (k, j, 0))
    du2 = _matmul("du2", "nt", [(dgate, a_spec, g_gate, b_spec), (dup, a_spec, g_up, b_spec)],
                  (s // tm, d // tn, N_DEV), (tm, tn), _sds((s, d), F32), pl.BlockSpec((tm, tn), lambda i, j, k: (i, j)),
                  dep=rs_ffn[4])
    rs_ffn = _scatter_chips("ffn", rs_ffn, du2)
    dh1, dmix, dg_ffn_pre, dg_mix_post = _mid_norms_bwd(dy, du2, h1, mix, norm_ffn_pre, norm_mix_post)

    da_sb, da_fx, dgf = _merge_bwd(dmix, w_out_full, gf, a_sb, a_fx, dep=rs_ffn[4])
    dw_out = _mm_plain("dw_out", "nn", merged_t, dmix, BF16).reshape(N_DEV, cs, d)

    def branch_bwd(tag, da, w_b, o_t, width):
        tb = _tile(width, 1024)
        do = _matmul("do_" + tag, "nt",
                     [(da, pl.BlockSpec((tm, cs), lambda i, j, k: (i, k)),
                       w_b, pl.BlockSpec((None, tb, cs), lambda i, j, k: (k, j, 0)))],
                     (s // tm, width // tb, N_DEV), (tm, tb), _sds((s, width), BF16),
                     pl.BlockSpec((tm, tb), lambda i, j, k: (i, j)))
        dw = _matmul("dw_" + tag, "nn",
                     [(o_t, pl.BlockSpec((width, s), lambda j, i, k: (0, 0)),
                       da, pl.BlockSpec((s, cs), lambda j, i, k: (0, j)))],
                     (N_DEV, 1, 1), (width, cs), _sds((N_DEV, width, cs), BF16),
                     pl.BlockSpec((None, width, cs), lambda j, i, k: (j, 0, 0)))
        return do, dw

    do_sb, dw_sb = branch_bwd("sb", da_sb, g_sb, o_sb_t, d_sb)
    do_fx, dw_fx = branch_bwd("fox", da_fx, g_fx, o_fx_t, d_fox)

    rs_mid = _scatter_pairs("mid", [dw_sb, dw_fx, dw_out])

    dqkv = _sb_bwd(qkv, do_sb, tot, h_sb, rs_mid[4])
    rs_mid = _scatter_chips("mid", rs_mid, dqkv)
    dqkv, dcum = _fox_bwd(dqkv, qkv, do_fx, o_fx32, lse, cum_col, cum_row, h_fox, h_sb, rs_mid[4])
    dgf, db_part = _forget_bwd(dgf, dcum, gf, b_pad, f_blk)
    dw_in = _w_in_grad_parts(_mm_plain("dw_qkv", "nn", u_t, dqkv, BF16), _mm_plain("dw_gates", "nn", u_t, dgf, BF16), lay)
    rs_in = _scatter_pairs("in", [dw_in])
    du = _mm_plain("du_qkv", "nt", dqkv, w_cat, F32, tn=1024, dep=rs_in[4])
    rs_in = _scatter_chips("in", rs_in, du)
    du = _mm_plain("du_gates", "nt", dgf, w_cat, F32, tn=1024, k_off=n_qkv, init=du, dep=rs_in[4])
    dx, dg_mix_pre = _pre_norm_bwd(dh1, du, xs, norm_mix_pre)

    upd = {}

    def update_group(tag, rs, names, after):
        parts = _scatter_end(tag, rs, after)
        for nm, p in zip(names, parts):
            w, m, v = decayed.get(nm, weights[nm])
            upd[nm] = _update("update_" + nm, p, w, m, v, layout=lay if nm == "w_in" else None, decayed=nm in decayed,
                              transposed_out=nm in ("w_ffn_gate", "w_ffn_up"))

    update_group("ffn", rs_ffn, ("w_ffn_gate", "w_ffn_up", "w_ffn_down"), [dx])
    update_group("mid", rs_mid, ("w_branch_sb", "w_branch_fox", "w_out"), [upd[nm][3] for nm in ("w_ffn_gate", "w_ffn_up", "w_ffn_down")])
    update_group("in", rs_in, ("w_in",), [upd[nm][3] for nm in ("w_branch_sb", "w_branch_fox", "w_out")])

    small = ((norm_mix_pre, m_norm_mix_pre, v_norm_mix_pre), (norm_mix_post, m_norm_mix_post, v_norm_mix_post),
             (norm_ffn_pre, m_norm_ffn_pre, v_norm_ffn_pre), (norm_ffn_post, m_norm_ffn_post, v_norm_ffn_post))
    pad_f = ((0, 0), (0, LANES - n_f))
    cat = lambda i: jnp.concatenate([t[i] for t in small] + [jnp.pad((b_forget, m_b_forget, v_b_forget)[i], pad_f)], axis=1)
    sm = _small_update(jnp.concatenate([dg_mix_pre, dg_mix_post, dg_ffn_pre, dg_ffn_post, db_part], axis=1),
                       cat(0), cat(1), cat(2))
    for i, nm in enumerate(("norm_mix_pre", "norm_mix_post", "norm_ffn_pre", "norm_ffn_post")):
        upd[nm] = [o[:, i * d:(i + 1) * d] for o in sm]
    upd["b_forget"] = [o[:, 4 * d:4 * d + n_f] for o in sm]

    loss = lax.psum(loss_part[0, 0], ("x", "y", "c"))
    order = ("norm_mix_pre", "norm_mix_post", "w_in", "b_forget", "w_branch_sb", "w_branch_fox", "w_out",
             "norm_ffn_pre", "norm_ffn_post", "w_ffn_gate", "w_ffn_up", "w_ffn_down")
    return (loss, dx[None]) + tuple(upd[nm][i] for i in range(4) for nm in order)
```

```python
import jax
import jax.numpy as jnp
from jax import lax
from jax.experimental import pallas as pl
from jax.experimental.pallas import tpu as pltpu

F32 = jnp.float32
BF16 = jnp.bfloat16
MESH = pl.DeviceIdType.MESH
ANY = pl.BlockSpec(memory_space=pl.ANY)
HBM = pl.BlockSpec(memory_space=pltpu.HBM)
SEM = pl.BlockSpec(memory_space=pltpu.SEMAPHORE)
EFFECT = pltpu.SideEffectType.DATAFLOW_SIDE_EFFECTING

N_DEV = 8
HEAD_DIM = 128
RMS_EPS = 1e-6
F_PAD = 512
LANES = 128
ATT_TQ = 256
ATT_TK = 256
ATT_HP = 4
NEG_BIG = -1e30
VMEM_LIMIT = 56 * 1024 * 1024

ADAM_LR = 0.001
ADAM_B1 = 0.9
ADAM_B2 = 0.999
ADAM_EPS = 1e-08
ADAM_WD = 0.01
ADAM_STEP = 10

_DIMS = {"nn": ((1,), (0,)), "nt": ((1,), (1,)), "tn": ((0,), (0,))}


def _params(sem):
    return pltpu.CompilerParams(dimension_semantics=sem, vmem_limit_bytes=VMEM_LIMIT)


def _dot(a, b, mode="nn"):
    return lax.dot_general(a.astype(BF16), b.astype(BF16), (_DIMS[mode], ((), ())), preferred_element_type=F32)


def _tile(n, pref):
    if n <= pref:
        return n
    t = (pref // LANES) * LANES
    while n % t:
        t -= LANES
    return t


def _split2(v):
    hi = v.astype(BF16)
    return hi, (v - hi.astype(F32)).astype(BF16)


def _split3(v):
    a = v.astype(BF16)
    r = v - a.astype(F32)
    b = r.astype(BF16)
    return a, b, (r - b.astype(F32)).astype(BF16)


def _tri(n, cmp):
    r = lax.broadcasted_iota(jnp.int32, (n, n), 0)
    c = lax.broadcasted_iota(jnp.int32, (n, n), 1)
    return jnp.where(cmp(r, c), 1.0, 0.0).astype(BF16)


def _lane_pick(v, h):
    lane = lax.broadcasted_iota(jnp.int32, v.shape, 1)
    return jnp.sum(jnp.where(lane == h, v, 0.0), axis=1, keepdims=True)


def _lane_put(ref, rows, h, col):
    old = ref[rows, :]
    lane = lax.broadcasted_iota(jnp.int32, old.shape, 1)
    ref[rows, :] = jnp.where(lane == h, col, old)


def _sigmoid(z):
    return 1.0 / (1.0 + jnp.exp(-z))


def _log_sigmoid(z):
    return jnp.minimum(z, 0.0) - jnp.log(1.0 + jnp.exp(-jnp.abs(z)))


def _sds(shape, dtype):
    return jax.ShapeDtypeStruct(shape, dtype)


def _matmul(name, mode, pairs, grid, acc_shape, out_shape, out_specs, extras=(), epilogue=None, init=None, dep=None):
    n_p, n_e = len(pairs), len(extras)
    nk = grid[-1]
    single = not isinstance(out_shape, (list, tuple))
    n_i = 0 if init is None else 1
    n_d = 0 if dep is None else 1

    one_step = nk == 1 and init is None

    def body(*refs):
        ab = refs[:2 * n_p]
        ex = refs[2 * n_p:2 * n_p + n_e]
        ini = refs[2 * n_p + n_e:2 * n_p + n_e + n_i]
        outs = refs[2 * n_p + n_e + n_i + n_d:len(refs) - (0 if one_step else 1)]

        def finish(total):
            if epilogue is None:
                outs[0][...] = total.astype(outs[0].dtype)
            else:
                epilogue(total, ex, outs)

        t = _dot(ab[0][...], ab[1][...], mode)
        for p in range(1, n_p):
            t = t + _dot(ab[2 * p][...], ab[2 * p + 1][...], mode)
        if one_step:
            finish(t)
            return
        acc = refs[-1]
        k = pl.program_id(len(grid) - 1)

        @pl.when(k == 0)
        def _():
            acc[...] = t if init is None else ini[0][...].astype(F32) + t

        @pl.when(k > 0)
        def _():
            acc[...] += t

        @pl.when(k == nk - 1)
        def _():
            finish(acc[...])

    in_specs = [s for (_, sa, _, sb) in pairs for s in (sa, sb)] + [s for (_, s) in extras]
    args = [v for (a, _, b, _) in pairs for v in (a, b)] + [e for (e, _) in extras]
    if init is not None:
        in_specs.append(init[1])
        args.append(init[0])
    if dep is not None:
        in_specs.append(ANY)
        args.append(dep)
    return pl.pallas_call(
        body, name=name, grid=grid, in_specs=in_specs,
        out_specs=out_specs if single else list(out_specs),
        out_shape=out_shape if single else list(out_shape),
        scratch_shapes=[] if one_step else [pltpu.VMEM(acc_shape, F32)],
        compiler_params=_params(("parallel",) * (len(grid) - 1) + ("arbitrary",)),
    )(*args)


def _mm_plain(name, mode, a, b, out_dtype, *, n_off=0, n=None, k_off=0, tm=1024, tn=1536, tk=2048, init=None, dep=None):
    if mode == "nn":
        (m, kk), nn_ = a.shape, b.shape[1]
    elif mode == "nt":
        (m, kk), nn_ = a.shape, b.shape[0]
    else:
        (kk, m), nn_ = a.shape, b.shape[1]
    n = nn_ if n is None else n
    tm, tn, tk = _tile(m, tm), _tile(n, tn), _tile(kk, tk)
    while n_off % tn or n % tn:
        tn -= LANES
    while k_off % tk or kk % tk:
        tk -= LANES
    off, koff = n_off // tn, k_off // tk
    a_spec = {"nn": pl.BlockSpec((tm, tk), lambda i, j, k: (i, k)),
              "nt": pl.BlockSpec((tm, tk), lambda i, j, k: (i, k)),
              "tn": pl.BlockSpec((tk, tm), lambda i, j, k: (k, i))}[mode]
    b_spec = {"nn": pl.BlockSpec((tk, tn), lambda i, j, k: (k, j + off)),
              "nt": pl.BlockSpec((tn, tk), lambda i, j, k: (j, k + koff)),
              "tn": pl.BlockSpec((tk, tn), lambda i, j, k: (k, j))}[mode]
    o_spec = pl.BlockSpec((tm, tn), lambda i, j, k: (i, j))
    if init is not None:
        init = (init, o_spec)
    return _matmul(name, mode, [(a, a_spec, b, b_spec)], (m // tm, n // tn, kk // tk), (tm, tn),
                   _sds((m, n), out_dtype), o_spec, init=init, dep=dep)


def _rows_call(name, body, ins, outs, s, tr=256, dep=None):
    def spec(v, per_row):
        if per_row == "transposed":
            return pl.BlockSpec((v.shape[0], tr), lambda i: (0, i))
        if per_row:
            return pl.BlockSpec((tr, v.shape[1]), lambda i: (i, 0))
        return pl.BlockSpec(v.shape, lambda i: (0, 0))
    n_in = len(ins)
    deps = [] if dep is None else [dep]

    def with_dep(*refs):
        body(*refs[:n_in], *refs[n_in + len(deps):])

    return pl.pallas_call(
        with_dep, name=name, grid=(s // tr,),
        in_specs=[spec(v, p) for v, p in ins] + [ANY] * len(deps), out_specs=[spec(v, p) for v, p in outs],
        out_shape=[_sds(v.shape, v.dtype) for v, _ in outs],
        compiler_params=_params(("arbitrary",)),
    )(*[v for v, _ in ins], *deps)


def _rsq(v):
    return lax.rsqrt(jnp.mean(v * v, axis=-1, keepdims=True) + RMS_EPS)


def _norm_bwd(dy, v, r, g):
    vh = v * r
    t = dy * g
    dv = r * (t - vh * jnp.mean(t * vh, axis=-1, keepdims=True))
    return dv, jnp.sum(dy * vh, axis=0, keepdims=True)


def _accum(ref, val):
    @pl.when(pl.program_id(0) == 0)
    def _():
        ref[...] = jnp.zeros_like(ref)
    ref[...] += val


def _pre_norm(x, g, dep=None):
    def body(x_ref, g_ref, u_ref, ut_ref):
        v = x_ref[...]
        u = (v * _rsq(v) * g_ref[...]).astype(BF16)
        u_ref[...] = u
        ut_ref[...] = u.T
    s, d = x.shape
    return _rows_call("pre_norm", body, [(x, True), (g, False)],
                      [(_sds((s, d), BF16), True), (_sds((d, s), BF16), "transposed")], s, dep=dep)


def _mid_norms(x, mix, g_post, g_pre):
    def body(x_ref, mix_ref, gp_ref, gn_ref, h_ref, u_ref, ut_ref):
        mv = mix_ref[...]
        h = x_ref[...] + mv * _rsq(mv) * gp_ref[...]
        h_ref[...] = h
        u = (h * _rsq(h) * gn_ref[...]).astype(BF16)
        u_ref[...] = u
        ut_ref[...] = u.T
    s, d = x.shape
    return _rows_call("mid_norms", body, [(x, True), (mix, True), (g_post, False), (g_pre, False)],
                      [(_sds((s, d), F32), True), (_sds((s, d), BF16), True), (_sds((d, s), BF16), "transposed")], s)


def _loss_head(h1, ff, target, g):
    s, d = h1.shape

    def body(h_ref, ff_ref, t_ref, g_ref, loss_ref, dy_ref, dff_ref, dg_ref):
        fv = ff_ref[...]
        r = _rsq(fv)
        err = h_ref[...] + fv * r * g_ref[...] - t_ref[...]
        part = 0.5 * jnp.sum(jnp.mean(err * err, axis=-1, keepdims=True), axis=0, keepdims=True)
        _accum(loss_ref, jnp.broadcast_to(part, loss_ref.shape))
        dy = err * (1.0 / d)
        dy_ref[...] = dy
        dff, dg = _norm_bwd(dy, fv, r, g_ref[...])
        dff_ref[...] = dff.astype(BF16)
        _accum(dg_ref, dg)

    return _rows_call("loss_head", body, [(h1, True), (ff, True), (target, True), (g, False)],
                      [(_sds((1, LANES), F32), False), (_sds((s, d), F32), True),
                       (_sds((s, d), BF16), True), (_sds((1, d), F32), False)], s)


def _mid_norms_bwd(dy, du2, h1, mix, g_pre, g_post):
    s, d = dy.shape

    def body(dy_ref, du_ref, h_ref, mix_ref, gn_ref, gp_ref, dh_ref, dmix_ref, dgn_ref, dgp_ref):
        h = h_ref[...]
        dh, dgn = _norm_bwd(du_ref[...], h, _rsq(h), gn_ref[...])
        dh = dh + dy_ref[...]
        dh_ref[...] = dh
        _accum(dgn_ref, dgn)
        mv = mix_ref[...]
        dmix, dgp = _norm_bwd(dh, mv, _rsq(mv), gp_ref[...])
        dmix_ref[...] = dmix.astype(BF16)
        _accum(dgp_ref, dgp)

    return _rows_call("mid_norms_bwd", body,
                      [(dy, True), (du2, True), (h1, True), (mix, True), (g_pre, False), (g_post, False)],
                      [(_sds((s, d), F32), True), (_sds((s, d), BF16), True),
                       (_sds((1, d), F32), False), (_sds((1, d), F32), False)], s)


def _pre_norm_bwd(dh1, du, x, g, dep=None):
    s, d = x.shape

    def body(dh_ref, du_ref, x_ref, g_ref, dx_ref, dg_ref):
        v = x_ref[...]
        dv, dg = _norm_bwd(du_ref[...], v, _rsq(v), g_ref[...])
        dx_ref[...] = dh_ref[...] + dv
        _accum(dg_ref, dg)

    return _rows_call("pre_norm_bwd", body, [(dh1, True), (du, True), (x, True), (g, False)],
                      [(_sds((s, d), F32), True), (_sds((1, d), F32), False)], s, dep=dep)


def _forget_fwd(gf, b_pad, f_blk):
    s = gf.shape[0]
    tb = ATT_TK
    nb = s // tb

    def body(f_ref, b_ref, col_ref, row_ref):
        incl = _tri(tb, lambda r, c: c <= r)
        carry = jnp.zeros((1, LANES), F32)
        for i in range(nb):
            lf = _log_sigmoid(f_ref[pl.ds(i * tb, tb), :] + b_ref[...])
            parts = _split3(lf)
            cum = carry + _dot(incl, parts[0]) + _dot(incl, parts[1]) + _dot(incl, parts[2])
            col_ref[pl.ds(i * tb, tb), :] = cum
            row_ref[i] = cum.T
            carry = carry + jnp.sum(lf, axis=0, keepdims=True)

    return pl.pallas_call(
        body, name="forget_fwd", grid=(1,),
        in_specs=[pl.BlockSpec((s, LANES), lambda i: (0, f_blk)), pl.BlockSpec((1, LANES), lambda i: (0, 0))],
        out_specs=[pl.BlockSpec((s, LANES), lambda i: (0, 0)), pl.BlockSpec((nb, LANES, tb), lambda i: (0, 0, 0))],
        out_shape=[_sds((s, LANES), F32), _sds((nb, LANES, tb), F32)],
        compiler_params=_params(("arbitrary",)),
    )(gf, b_pad)


def _forget_bwd(dgf, dcum, gf, b_pad, f_blk):
    s = gf.shape[0]
    tb = ATT_TK
    nb = s // tb
    sec = dgf.shape[1] // F_PAD - 1

    def body(dgf_hbm, dc_ref, f_ref, b_ref, out_ref, db_ref):
        del dgf_hbm
        incl = _tri(tb, lambda r, c: c >= r)
        carry = jnp.zeros((1, LANES), F32)
        db = jnp.zeros((1, LANES), F32)
        out_ref[...] = jnp.zeros_like(out_ref)
        for i in reversed(range(nb)):
            dc = dc_ref[pl.ds(i * tb, tb), :]
            parts = _split3(dc)
            dlf = carry + _dot(incl, parts[0]) + _dot(incl, parts[1]) + _dot(incl, parts[2])
            z = f_ref[pl.ds(i * tb, tb), :] + b_ref[...]
            df = dlf * _sigmoid(-z)
            out_ref[pl.ds(i * tb, tb), pl.ds(0, LANES)] = df.astype(BF16)
            db = db + jnp.sum(df, axis=0, keepdims=True)
            carry = carry + jnp.sum(dc, axis=0, keepdims=True)
        db_ref[...] = db

    return pl.pallas_call(
        body, name="forget_bwd", grid=(1,),
        in_specs=[ANY, pl.BlockSpec((s, LANES), lambda i: (0, 0)),
                  pl.BlockSpec((s, LANES), lambda i: (0, f_blk)), pl.BlockSpec((1, LANES), lambda i: (0, 0))],
        out_specs=[pl.BlockSpec((s, F_PAD), lambda i: (0, sec)), pl.BlockSpec((1, LANES), lambda i: (0, 0))],
        out_shape=[_sds(dgf.shape, BF16), _sds((1, LANES), F32)],
        input_output_aliases={0: 0},
        compiler_params=_params(("arbitrary",)),
    )(dgf, dcum, gf, b_pad)


def _diag_mask(strict):
    r = lax.broadcasted_iota(jnp.int32, (ATT_TQ, ATT_TK), 0)
    c = lax.broadcasted_iota(jnp.int32, (ATT_TQ, ATT_TK), 1)
    return c < r if strict else c <= r


def _qkv_specs(hb0, s):
    specs = []
    for j in range(ATT_HP):
        def col(g, j=j):
            return 3 * (hb0 + ATT_HP * g + j)
        specs += [pl.BlockSpec((ATT_TQ, HEAD_DIM), lambda g, i, col=col: (i, col(g))),
                  pl.BlockSpec((s, HEAD_DIM), lambda g, i, col=col: (0, col(g) + 1)),
                  pl.BlockSpec((s, HEAD_DIM), lambda g, i, col=col: (0, col(g) + 2))]
    return specs


def _head_cols(j):
    return pl.ds(j * HEAD_DIM, HEAD_DIM)


def _sb_fwd(qkv, n_heads):
    s = qkv.shape[0]
    scale = HEAD_DIM ** -0.5
    tq, tk = ATT_TQ, ATT_TK
    heads = range(ATT_HP)

    def body(*refs):
        qkv_refs, (o_ref, ot_ref, tot_ref) = refs[:3 * ATT_HP], refs[3 * ATT_HP:]
        g, i = pl.program_id(0), pl.program_id(1)

        @pl.when((g == 0) & (i == 0))
        def _():
            tot_ref[...] = jnp.zeros_like(tot_ref)

        qs = [qkv_refs[3 * j][...] for j in heads]
        upper = _tri(tk, lambda r, c: r > c)

        def tile(kj, carry, mask):
            rows = pl.ds(pl.multiple_of(kj * tk, tk), tk)
            z = [_dot(qs[j], qkv_refs[3 * j + 1][rows, :], "nt") * scale for j in heads]
            lsz = [_log_sigmoid(z[j]) for j in heads]
            lk = [lsz[j] - z[j] if mask is None else jnp.where(mask, lsz[j] - z[j], 0.0) for j in heads]
            parts = [_split2(lk[j]) for j in heads]
            above = [carry[j][0] + _dot(parts[j][0], upper) + _dot(parts[j][1], upper) for j in heads]
            w = [jnp.exp(lsz[j] + above[j]) for j in heads]
            if mask is not None:
                w = [jnp.where(mask, w[j], 0.0) for j in heads]
            return tuple((carry[j][0] + jnp.sum(lk[j], axis=1, keepdims=True),
                          carry[j][1] + _dot(w[j], qkv_refs[3 * j + 2][rows, :])) for j in heads)

        carry = tile(i, tuple((jnp.zeros((tq, 1), F32), jnp.zeros((tq, HEAD_DIM), F32)) for _ in heads), _diag_mask(True))
        carry = lax.fori_loop(0, i, lambda n, cr: tile(i - 1 - n, cr, None), carry)
        q_rows = pl.ds(pl.multiple_of(i * tq, tq), tq)
        for j in heads:
            c, acc = carry[j]
            o = acc.astype(BF16)
            o_ref[:, _head_cols(j)] = o
            ot_ref[_head_cols(j), :] = o.T
            _lane_put(tot_ref, q_rows, ATT_HP * g + j, c)

    wide = ATT_HP * HEAD_DIM
    return pl.pallas_call(
        body, name="sb_fwd", grid=(n_heads // ATT_HP, s // tq),
        in_specs=_qkv_specs(0, s),
        out_specs=[pl.BlockSpec((tq, wide), lambda g, i: (i, g)), pl.BlockSpec((wide, tq), lambda g, i: (g, i)),
                   pl.BlockSpec((s, LANES), lambda g, i: (0, 0))],
        out_shape=[_sds((s, n_heads * HEAD_DIM), BF16), _sds((n_heads * HEAD_DIM, s), BF16), _sds((s, LANES), F32)],
        compiler_params=_params(("arbitrary", "arbitrary")),
    )(*[qkv] * (3 * ATT_HP))


def _sb_bwd(qkv, do, tot, n_heads, dep):
    s = qkv.shape[0]
    scale = HEAD_DIM ** -0.5
    tq, tk = ATT_TQ, ATT_TK
    nq = s // tq
    hd = HEAD_DIM

    heads = range(ATT_HP)

    def body(*refs):
        qkv_refs = refs[:3 * ATT_HP]
        do_ref, tot_ref, _, out_ref, dk_acc, dv_acc = refs[3 * ATT_HP:]
        g, i = pl.program_id(0), pl.program_id(1)

        @pl.when(i == 0)
        def _():
            dk_acc[...] = jnp.zeros_like(dk_acc)
            dv_acc[...] = jnp.zeros_like(dv_acc)

        qs = [qkv_refs[3 * j][...] for j in heads]
        douts = [do_ref[:, _head_cols(j)] for j in heads]
        totals = [_lane_pick(tot_ref[...], ATT_HP * g + j) for j in heads]
        incl = _tri(tk, lambda r, c: r <= c)
        excl = _tri(tk, lambda r, c: r < c)

        def tile(kj, carry, mask):
            rows = pl.ds(pl.multiple_of(kj * tk, tk), tk)
            k_t = [qkv_refs[3 * j + 1][rows, :] for j in heads]
            z = [_dot(qs[j], k_t[j], "nt") * scale for j in heads]
            dw = [_dot(douts[j], qkv_refs[3 * j + 2][rows, :], "nt") for j in heads]
            lsz = [_log_sigmoid(z[j]) for j in heads]
            lk = [lsz[j] - z[j] if mask is None else jnp.where(mask, lsz[j] - z[j], 0.0) for j in heads]
            parts = [_split2(lk[j]) for j in heads]
            below = [carry[j][0] + _dot(parts[j][0], incl) + _dot(parts[j][1], incl) for j in heads]
            w = [jnp.exp(lsz[j] + (totals[j] - below[j])) for j in heads]
            if mask is not None:
                w = [jnp.where(mask, w[j], 0.0) for j in heads]
            e = [dw[j] * w[j] for j in heads]
            parts = [_split2(e[j]) for j in heads]
            e_before = [carry[j][1] + _dot(parts[j][0], excl) + _dot(parts[j][1], excl) for j in heads]
            sg = [jnp.exp(lsz[j]) for j in heads]
            dz = [e[j] * (1.0 - sg[j]) - e_before[j] * sg[j] for j in heads]
            if mask is not None:
                dz = [jnp.where(mask, dz[j], 0.0) for j in heads]
            dz = [(dz[j] * scale).astype(BF16) for j in heads]
            for j in heads:
                dk_acc[j, rows, :] += _dot(dz[j], qs[j], "tn")
                dv_acc[j, rows, :] += _dot(w[j], douts[j], "tn")
            return tuple((carry[j][0] + jnp.sum(lk[j], axis=1, keepdims=True),
                          carry[j][1] + jnp.sum(e[j], axis=1, keepdims=True),
                          carry[j][2] + _dot(dz[j], k_t[j])) for j in heads)

        zero = jnp.zeros((tq, 1), F32)
        carry = lax.fori_loop(0, i, lambda kj, cr: tile(kj, cr, None),
                              tuple((zero, zero, jnp.zeros((tq, hd), F32)) for _ in heads))
        carry = tile(i, carry, _diag_mask(True))
        for j in heads:
            out_ref[pl.ds(pl.multiple_of(i * tq, tq), tq), pl.ds(3 * j * hd, hd)] = carry[j][2].astype(BF16)

        @pl.when(i == nq - 1)
        def _():
            for j in heads:
                out_ref[:, pl.ds((3 * j + 1) * hd, hd)] = dk_acc[j].astype(BF16)
                out_ref[:, pl.ds((3 * j + 2) * hd, hd)] = dv_acc[j].astype(BF16)

    wide = ATT_HP * hd
    return pl.pallas_call(
        body, name="sb_bwd", grid=(n_heads // ATT_HP, nq),
        in_specs=_qkv_specs(0, s) + [pl.BlockSpec((tq, wide), lambda g, i: (i, g)),
                                     pl.BlockSpec((tq, LANES), lambda g, i: (i, 0)), ANY],
        out_specs=pl.BlockSpec((s, 3 * wide), lambda g, i: (0, g)),
        out_shape=_sds(qkv.shape, BF16),
        scratch_shapes=[pltpu.VMEM((ATT_HP, s, hd), F32), pltpu.VMEM((ATT_HP, s, hd), F32)],
        compiler_params=_params(("arbitrary", "arbitrary")),
    )(*[qkv] * (3 * ATT_HP), do, tot, dep)


def _fox_fwd(qkv, cum_col, cum_row, n_heads, hb0, dep):
    s = qkv.shape[0]
    scale = HEAD_DIM ** -0.5
    tq, tk = ATT_TQ, ATT_TK

    heads = range(ATT_HP)

    def body(*refs):
        qkv_refs = refs[:3 * ATT_HP]
        cc_ref, cr_ref, _, o_ref, ot_ref, o32_ref, lse_ref = refs[3 * ATT_HP:]
        g, i = pl.program_id(0), pl.program_id(1)

        @pl.when((g == 0) & (i == 0))
        def _():
            lse_ref[...] = jnp.zeros_like(lse_ref)

        qs = [qkv_refs[3 * j][...] for j in heads]
        cqs = [_lane_pick(cc_ref[...], ATT_HP * g + j) for j in heads]

        def tile(kj, carry, mask):
            rows = pl.ds(pl.multiple_of(kj * tk, tk), tk)
            sc = [_dot(qs[j], qkv_refs[3 * j + 1][rows, :], "nt") * scale + cqs[j]
                  - cr_ref[kj, pl.ds(ATT_HP * g + j, 1), :] for j in heads]
            if mask is not None:
                sc = [jnp.where(mask, sc[j], NEG_BIG) for j in heads]
            m_new = [jnp.maximum(carry[j][0], jnp.max(sc[j], axis=1, keepdims=True)) for j in heads]
            p = [jnp.exp(sc[j] - m_new[j]) for j in heads]
            alpha = [jnp.exp(carry[j][0] - m_new[j]) for j in heads]
            parts = [_split2(p[j]) for j in heads]
            v_t = [qkv_refs[3 * j + 2][rows, :] for j in heads]
            pv = [_dot(parts[j][0], v_t[j]) + _dot(parts[j][1], v_t[j]) for j in heads]
            return tuple((m_new[j], alpha[j] * carry[j][1] + jnp.sum(p[j], axis=1, keepdims=True),
                          alpha[j] * carry[j][2] + pv[j]) for j in heads)

        carry = tuple((jnp.full((tq, 1), NEG_BIG, F32), jnp.zeros((tq, 1), F32), jnp.zeros((tq, HEAD_DIM), F32))
                      for _ in heads)
        carry = lax.fori_loop(0, i, lambda kj, cr: tile(kj, cr, None), carry)
        carry = tile(i, carry, _diag_mask(False))
        q_rows = pl.ds(pl.multiple_of(i * tq, tq), tq)
        for j in heads:
            m, l, acc = carry[j]
            o = acc / l
            o_ref[:, _head_cols(j)] = o.astype(BF16)
            ot_ref[_head_cols(j), :] = o.astype(BF16).T
            o32_ref[:, _head_cols(j)] = o
            _lane_put(lse_ref, q_rows, ATT_HP * g + j, m + jnp.log(l))

    nb = cum_row.shape[0]
    wide = ATT_HP * HEAD_DIM
    return pl.pallas_call(
        body, name="fox_fwd", grid=(n_heads // ATT_HP, s // tq),
        in_specs=_qkv_specs(hb0, s) + [pl.BlockSpec((tq, LANES), lambda g, i: (i, 0)),
                                       pl.BlockSpec((nb, 8, tk), lambda g, i: (0, 0, 0)), ANY],
        out_specs=[pl.BlockSpec((tq, wide), lambda g, i: (i, g)), pl.BlockSpec((wide, tq), lambda g, i: (g, i)),
                   pl.BlockSpec((tq, wide), lambda g, i: (i, g)), pl.BlockSpec((s, LANES), lambda g, i: (0, 0))],
        out_shape=[_sds((s, n_heads * HEAD_DIM), BF16), _sds((n_heads * HEAD_DIM, s), BF16),
                   _sds((s, n_heads * HEAD_DIM), F32), _sds((s, LANES), F32)],
        compiler_params=_params(("arbitrary", "arbitrary")),
    )(*[qkv] * (3 * ATT_HP), cum_col, cum_row, dep)


def _fox_bwd(dqkv, qkv, do, o, lse, cum_col, cum_row, n_heads, hb0, dep):
    s = qkv.shape[0]
    scale = HEAD_DIM ** -0.5
    tq, tk = ATT_TQ, ATT_TK
    nq = s // tq
    hd = HEAD_DIM

    heads = range(ATT_HP)
    assert hb0 % ATT_HP == 0

    def body(*refs):
        qkv_refs = refs[1:1 + 3 * ATT_HP]
        do_ref, o_ref, lse_ref, cc_ref, cr_ref, _, out_ref, dc_ref, dk_acc, dv_acc, col_acc = refs[1 + 3 * ATT_HP:]
        g, i = pl.program_id(0), pl.program_id(1)

        @pl.when((g == 0) & (i == 0))
        def _():
            dc_ref[...] = jnp.zeros_like(dc_ref)

        @pl.when(i == 0)
        def _():
            dk_acc[...] = jnp.zeros_like(dk_acc)
            dv_acc[...] = jnp.zeros_like(dv_acc)
            col_acc[...] = jnp.zeros_like(col_acc)

        qs = [qkv_refs[3 * j][...] for j in heads]
        douts = [do_ref[:, _head_cols(j)] for j in heads]
        deltas = [jnp.sum(douts[j].astype(F32) * o_ref[:, _head_cols(j)], axis=1, keepdims=True) for j in heads]
        shifts = [_lane_pick(cc_ref[...], ATT_HP * g + j) - _lane_pick(lse_ref[...], ATT_HP * g + j) for j in heads]

        def tile(kj, carry, mask):
            rows = pl.ds(pl.multiple_of(kj * tk, tk), tk)
            k_t = [qkv_refs[3 * j + 1][rows, :] for j in heads]
            sc = [_dot(qs[j], k_t[j], "nt") * scale + shifts[j] - cr_ref[kj, pl.ds(ATT_HP * g + j, 1), :] for j in heads]
            dp = [_dot(douts[j], qkv_refs[3 * j + 2][rows, :], "nt") for j in heads]
            p = [jnp.exp(sc[j]) for j in heads]
            if mask is not None:
                p = [jnp.where(mask, p[j], 0.0) for j in heads]
            ds_f = [p[j] * (dp[j] - deltas[j]) for j in heads]
            ds = [(ds_f[j] * scale).astype(BF16) for j in heads]
            for j in heads:
                col_acc[j, kj] += jnp.broadcast_to(jnp.sum(ds_f[j], axis=0, keepdims=True), (8, tk))
                dk_acc[j, rows, :] += _dot(ds[j], qs[j], "tn")
                dv_acc[j, rows, :] += _dot(p[j], douts[j], "tn")
            return tuple((carry[j][0] + _dot(ds[j], k_t[j]), carry[j][1] + jnp.sum(ds_f[j], axis=1, keepdims=True))
                         for j in heads)

        carry = lax.fori_loop(0, i, lambda kj, cr: tile(kj, cr, None),
                              tuple((jnp.zeros((tq, hd), F32), jnp.zeros((tq, 1), F32)) for _ in heads))
        carry = tile(i, carry, _diag_mask(False))
        q_rows = pl.ds(pl.multiple_of(i * tq, tq), tq)
        for j in heads:
            out_ref[q_rows, pl.ds(3 * j * hd, hd)] = carry[j][0].astype(BF16)
            _lane_put(dc_ref, q_rows, ATT_HP * g + j, carry[j][1])

        @pl.when(i == nq - 1)
        def _():
            lane = lax.broadcasted_iota(jnp.int32, (tk, LANES), 1)
            for j in heads:
                out_ref[:, pl.ds((3 * j + 1) * hd, hd)] = dk_acc[j].astype(BF16)
                out_ref[:, pl.ds((3 * j + 2) * hd, hd)] = dv_acc[j].astype(BF16)
                for kj in range(nb):
                    col = jnp.broadcast_to(col_acc[j, kj][0:1, :], (LANES, tk)).T
                    old = dc_ref[pl.ds(kj * tk, tk), :]
                    dc_ref[pl.ds(kj * tk, tk), :] = jnp.where(lane == ATT_HP * g + j, old - col, old)

    nb = cum_row.shape[0]
    wide = ATT_HP * hd
    return pl.pallas_call(
        body, name="fox_bwd", grid=(n_heads // ATT_HP, nq),
        in_specs=[ANY] + _qkv_specs(hb0, s) + [
            pl.BlockSpec((tq, wide), lambda g, i: (i, g)), pl.BlockSpec((tq, wide), lambda g, i: (i, g)),
            pl.BlockSpec((tq, LANES), lambda g, i: (i, 0)), pl.BlockSpec((tq, LANES), lambda g, i: (i, 0)),
            pl.BlockSpec((nb, 8, tk), lambda g, i: (0, 0, 0)), ANY],
        out_specs=[pl.BlockSpec((s, 3 * wide), lambda g, i: (0, hb0 // ATT_HP + g)),
                   pl.BlockSpec((s, LANES), lambda g, i: (0, 0))],
        out_shape=[_sds(dqkv.shape, BF16), _sds((s, LANES), F32)],
        scratch_shapes=[pltpu.VMEM((ATT_HP, s, hd), F32), pltpu.VMEM((ATT_HP, s, hd), F32),
                        pltpu.VMEM((ATT_HP, s // tk, 8, tk), F32)],
        input_output_aliases={0: 0},
        compiler_params=_params(("arbitrary", "arbitrary")),
    )(dqkv, *[qkv] * (3 * ATT_HP), do, o, lse, cum_col, cum_row, dep)


def _branch_merge(o_sb, o_fx, w_sb, w_fx, gf, dep, tm=1024):
    s = o_sb.shape[0]
    cs = w_sb.shape[2]
    tm = _tile(s, tm)

    def body(osb_ref, ofx_ref, wsb_ref, wfx_ref, g_ref, dep_ref, merged_ref, mt_ref, asb_ref, afx_ref):
        del dep_ref
        a_sb = _dot(osb_ref[...], wsb_ref[...])
        a_fx = _dot(ofx_ref[...], wfx_ref[...])
        g = g_ref[...]
        merged = (_sigmoid(g[:, :cs]) * a_sb + _sigmoid(g[:, cs:]) * a_fx).astype(BF16)
        merged_ref[...] = merged
        mt_ref[...] = merged.T
        asb_ref[...] = a_sb.astype(BF16)
        afx_ref[...] = a_fx.astype(BF16)

    blk = pl.BlockSpec((tm, cs), lambda i, j: (i, j))
    out = _sds((s, N_DEV * cs), BF16)
    return pl.pallas_call(
        body, name="branch_merge", grid=(s // tm, N_DEV),
        in_specs=[pl.BlockSpec((tm, o_sb.shape[1]), lambda i, j: (i, 0)),
                  pl.BlockSpec((tm, o_fx.shape[1]), lambda i, j: (i, 0)),
                  pl.BlockSpec((None,) + w_sb.shape[1:], lambda i, j: (j, 0, 0)),
                  pl.BlockSpec((None,) + w_fx.shape[1:], lambda i, j: (j, 0, 0)),
                  pl.BlockSpec((tm, 2 * cs), lambda i, j: (i, j)), ANY],
        out_specs=[blk, pl.BlockSpec((cs, tm), lambda i, j: (j, i)), blk, blk],
        out_shape=[out, _sds((N_DEV * cs, s), BF16), out, out],
        compiler_params=_params(("parallel", "arbitrary")),
    )(o_sb, o_fx, w_sb, w_fx, gf, dep)


def _merge_bwd(dmix, w_out, gf, a_sb, a_fx, tm=1024, tk=2048, dep=None):
    s, d = dmix.shape
    cs = d // N_DEV
    tm, tk = _tile(s, tm), _tile(d, tk)

    def epilogue(acc, ex, outs):
        g, a_sb, a_fx = ex[0][...], ex[1][...].astype(F32), ex[2][...].astype(F32)
        s_sb, s_fx = _sigmoid(g[:, :cs]), _sigmoid(g[:, cs:])
        outs[0][...] = (acc * s_sb).astype(BF16)
        outs[1][...] = (acc * s_fx).astype(BF16)
        outs[2][...] = jnp.concatenate([acc * a_sb * s_sb * (1.0 - s_sb), acc * a_fx * s_fx * (1.0 - s_fx)],
                                       axis=1).astype(BF16)

    blk = pl.BlockSpec((tm, cs), lambda i, j, k: (i, j))
    wide = pl.BlockSpec((tm, 2 * cs), lambda i, j, k: (i, j))
    return _matmul(
        "merge_bwd", "nt",
        [(dmix, pl.BlockSpec((tm, tk), lambda i, j, k: (i, k)), w_out, pl.BlockSpec((cs, tk), lambda i, j, k: (j, k)))],
        (s // tm, N_DEV, d // tk), (tm, cs),
        [_sds((s, d), BF16), _sds((s, d), BF16), _sds(gf.shape, BF16)], [blk, blk, wide],
        extras=[(gf, wide), (a_sb, blk), (a_fx, blk)], epilogue=epilogue, dep=dep)


def _ffn_up(u2, w_gate, w_up, dep, tm=1024):
    s, d = u2.shape
    fs = w_gate.shape[2]
    tm = _tile(s, tm)

    def body(u_ref, wg_ref, wu_ref, dep_ref, gate_ref, up_ref, act_ref, actt_ref):
        del dep_ref
        u = u_ref[...]
        gate = _dot(u, wg_ref[...])
        up = _dot(u, wu_ref[...])
        gate_ref[...] = gate
        up_ref[...] = up
        act = (gate * _sigmoid(gate) * up).astype(BF16)
        act_ref[...] = act
        actt_ref[...] = act.T

    w_spec = pl.BlockSpec((None, d, fs), lambda i, j: (j, 0, 0))
    o_spec = pl.BlockSpec((None, tm, fs), lambda i, j: (j, i, 0))
    return pl.pallas_call(
        body, name="ffn_up", grid=(s // tm, N_DEV),
        in_specs=[pl.BlockSpec((tm, d), lambda i, j: (i, 0)), w_spec, w_spec, ANY],
        out_specs=[o_spec, o_spec, o_spec, pl.BlockSpec((None, fs, tm), lambda i, j: (j, 0, i))],
        out_shape=[_sds((N_DEV, s, fs), F32), _sds((N_DEV, s, fs), F32), _sds((N_DEV, s, fs), BF16),
                   _sds((N_DEV, fs, s), BF16)],
        compiler_params=_params(("parallel", "arbitrary")),
    )(u2, w_gate, w_up, dep)


def _ffn_down_bwd(dff, w_down, gate, up, tm=1024):
    s, d = dff.shape
    fs = w_down.shape[1]
    tm = _tile(s, tm)

    def body(dff_ref, wd_ref, gate_ref, up_ref, dgate_ref, dup_ref):
        dact = _dot(dff_ref[...], wd_ref[...], "nt")
        gate = gate_ref[...]
        sg = _sigmoid(gate)
        dup_ref[...] = (dact * gate * sg).astype(BF16)
        dgate_ref[...] = (dact * up_ref[...] * sg * (1.0 + gate * (1.0 - sg))).astype(BF16)

    a_spec = pl.BlockSpec((None, tm, fs), lambda i, j: (j, i, 0))
    return pl.pallas_call(
        body, name="ffn_down_bwd", grid=(s // tm, N_DEV),
        in_specs=[pl.BlockSpec((tm, d), lambda i, j: (i, 0)), pl.BlockSpec((None, fs, d), lambda i, j: (j, 0, 0)),
                  a_spec, a_spec],
        out_specs=[a_spec, a_spec],
        out_shape=[_sds((N_DEV, s, fs), BF16), _sds((N_DEV, s, fs), BF16)],
        compiler_params=_params(("parallel", "arbitrary")),
    )(dff, w_down, gate, up)


def _mesh_place():
    x, y, c = lax.axis_index("x"), lax.axis_index("y"), lax.axis_index("c")
    peers = []
    for d in range(1, N_DEV):
        px = 1 - x if d & 4 else x
        py = 1 - y if d & 2 else y
        pc = 1 - c if d & 1 else c
        peers.append((d, (px, py, pc), 4 * px + 2 * py + pc))
    return 4 * x + 2 * y + c, peers


def _flat_me():
    return 4 * lax.axis_index("x") + 2 * lax.axis_index("y") + lax.axis_index("c")


def _in_hbm(a):
    return pltpu.with_memory_space_constraint(a, pltpu.HBM)


def _pair_plan():
    x, y, c = lax.axis_index("x"), lax.axis_index("y"), lax.axis_index("c")
    return [(2 * q + (1 - c), q, q, (x, y, 1 - c)) for q in range(4)]


def _chip_plan():
    x, y, c = lax.axis_index("x"), lax.axis_index("y"), lax.axis_index("c")
    plan = []
    for fx, fy in ((1, 0), (0, 1), (1, 1)):
        cx, cy = (1 - x if fx else x), (1 - y if fy else y)
        plan.append((2 * cx + cy, 2 * x + y, 2 * cx + cy, (cx, cy, c)))
    return plan


def _split_start(name, srcs, lands, plan, k):
    n = len(srcs)

    def body(*refs):
        ins, lnd = refs[:n], refs[n:2 * n]
        send, recv, token = refs[2 * n], refs[2 * n + 1], refs[-1]
        copies = plan()
        for a in range(n):
            for t, (src, dst, _, dev) in enumerate(copies):
                pltpu.make_async_remote_copy(src_ref=ins[a].at[src], dst_ref=lnd[a].at[dst], send_sem=send.at[k * a + t],
                                             recv_sem=recv.at[k * a + t], device_id=dev, device_id_type=MESH).start()
        token[...] = jnp.zeros_like(token)

    res = pl.pallas_call(
        body, name=name,
        out_shape=[pltpu.SemaphoreType.DMA((n * k,)), pltpu.SemaphoreType.DMA((n * k,))]
        + [pltpu.HBM(a.shape, a.dtype) for a in list(srcs) + list(lands)] + [_sds((8, LANES), F32)],
        in_specs=[HBM] * (2 * n), out_specs=[SEM, SEM] + [HBM] * (2 * n) + [pl.BlockSpec(memory_space=pltpu.VMEM)],
        input_output_aliases={i: 2 + i for i in range(2 * n)},
        compiler_params=pltpu.CompilerParams(has_side_effects=EFFECT),
    )(*[_in_hbm(a) for a in srcs], *[_in_hbm(a) for a in lands])
    return res[0], res[1], res[2:2 + n], res[2 + n:2 + 2 * n], res[-1]


def _split_wait(name, send, recv, srcs, lands, plan, k, after):
    n = len(srcs)

    def body(*refs):
        ins, lnd = refs[:n], refs[n:2 * n]
        send_sem, recv_sem = refs[2 * n], refs[2 * n + 1]
        copies = plan()
        for a in range(n):
            for t, (src, _, dst, dev) in enumerate(copies):
                cp = pltpu.make_async_remote_copy(src_ref=ins[a].at[src], dst_ref=lnd[a].at[dst], send_sem=send_sem.at[k * a + t],
                                                  recv_sem=recv_sem.at[k * a + t], device_id=dev, device_id_type=MESH)
                cp.wait_send()
                cp.wait_recv()

    res = pl.pallas_call(
        body, name=name,
        out_shape=[pltpu.HBM(a.shape, a.dtype) for a in list(srcs) + list(lands)],
        in_specs=[HBM] * (2 * n) + [SEM, SEM] + [ANY] * len(after), out_specs=[HBM] * (2 * n),
        input_output_aliases={i: i for i in range(2 * n)},
        compiler_params=pltpu.CompilerParams(has_side_effects=EFFECT),
    )(*srcs, *lands, send, recv, *after)
    return res[:n], res[n:]


def _pair_add(name, parts, land):
    _, r, cols = parts.shape
    tr = max(16, min(r, ((1 << 20) // (2 * cols)) // 16 * 16))
    while r % tr:
        tr -= 16

    def body(c_ref, p_ref, l_ref, o_ref):
        del c_ref
        o_ref[...] = (p_ref[...].astype(F32) + l_ref[...].astype(F32)).astype(BF16)

    blk = pl.BlockSpec((None, tr, cols), lambda q, i, c_ref: (q, i, 0))
    return pl.pallas_call(
        body, name=name,
        grid_spec=pltpu.PrefetchScalarGridSpec(
            num_scalar_prefetch=1, grid=(4, r // tr),
            in_specs=[pl.BlockSpec((None, tr, cols), lambda q, i, c_ref: (2 * q + c_ref[0], i, 0)), blk], out_specs=blk),
        out_shape=_sds((4, r, cols), BF16),
        compiler_params=_params(("parallel", "parallel")),
    )(jnp.reshape(lax.axis_index("c"), (1,)).astype(jnp.int32), parts, land)


def _scatter_pairs(tag, parts):
    lands = [lax.empty((4,) + a.shape[1:], a.dtype) for a in parts]
    return _split_start("pair_" + tag, parts, lands, _pair_plan, 4)


def _scatter_chips(tag, started, after):
    send, recv, parts, lands, _ = started
    parts, lands = _split_wait("pair_" + tag + "_wait", send, recv, parts, lands, _pair_plan, 4, [after])
    sums = [_pair_add("pair_" + tag + "_add%d" % a, p, l) for a, (p, l) in enumerate(zip(parts, lands))]
    chip = 2 * lax.axis_index("x") + lax.axis_index("y")
    final = [lax.dynamic_update_slice_in_dim(lax.empty(v.shape, v.dtype), lax.dynamic_slice_in_dim(v, chip, 1, 0), chip, 0)
             for v in sums]
    return _split_start("chips_" + tag, sums, final, _chip_plan, 3)


def _scatter_end(tag, started, after):
    send, recv, sums, final, _ = started
    return _split_wait("chips_" + tag + "_wait", send, recv, sums, final, _chip_plan, 3, after)[1]


def _gather_targets():
    x, y, c = lax.axis_index("x"), lax.axis_index("y"), lax.axis_index("c")
    chips = [(x, y), (1 - x, y), (x, 1 - y), (1 - x, 1 - y)]
    same = [((cx, cy, c), 4 * cx + 2 * cy + c) for cx, cy in chips]
    other = [((cx, cy, 1 - c), 4 * cx + 2 * cy + 1 - c) for cx, cy in chips]
    return same[0][1], [other[0]] + same[1:], [flat for _, flat in other[1:]], other[0][0]


def _gather_start(shards):
    n = len(shards)
    me = _flat_me()
    lands = [lax.dynamic_update_slice_in_dim(lax.empty((N_DEV,) + a.shape, a.dtype), a[None], me, 0) for a in shards]

    def body(*refs):
        lnd, send, recv, token = refs[:n], refs[n], refs[n + 1], refs[-1]
        mine, targets, _, _ = _gather_targets()
        for a in range(n):
            for t, (dev, _) in enumerate(targets):
                pltpu.make_async_remote_copy(src_ref=lnd[a].at[mine], dst_ref=lnd[a].at[mine], send_sem=send.at[4 * a + t],
                                             recv_sem=recv.at[4 * a + t], device_id=dev, device_id_type=MESH).start()
        token[...] = jnp.zeros_like(token)

    res = pl.pallas_call(
        body, name="gather_start",
        out_shape=[pltpu.SemaphoreType.DMA((4 * n,)), pltpu.SemaphoreType.DMA((4 * n,))]
        + [pltpu.HBM(a.shape, a.dtype) for a in lands] + [_sds((8, LANES), F32)],
        in_specs=[HBM] * n, out_specs=[SEM, SEM] + [HBM] * n + [pl.BlockSpec(memory_space=pltpu.VMEM)],
        input_output_aliases={i: 2 + i for i in range(n)},
        compiler_params=pltpu.CompilerParams(has_side_effects=EFFECT),
    )(*[_in_hbm(a) for a in lands])
    return res[0], res[1], list(res[2:2 + n]), res[-1]


def _gather_forward(name, lands, first, send, recv, after):
    n = len(lands)

    def body(*refs):
        lnd, send_sem, recv_sem = refs[:n], refs[n], refs[n + 1]
        send2, recv2, token = refs[-3], refs[-2], refs[-1]
        mine, targets, _, sibling = _gather_targets()
        for a in range(n):
            for t, (dev, flat) in enumerate(targets):
                cp = pltpu.make_async_remote_copy(src_ref=lnd[a].at[mine], dst_ref=lnd[a].at[flat],
                                                  send_sem=send_sem.at[4 * (first + a) + t],
                                                  recv_sem=recv_sem.at[4 * (first + a) + t], device_id=dev, device_id_type=MESH)
                cp.wait_send()
                if t:
                    cp.wait_recv()
                    pltpu.make_async_remote_copy(src_ref=lnd[a].at[flat], dst_ref=lnd[a].at[flat], send_sem=send2.at[3 * a + t - 1],
                                                 recv_sem=recv2.at[3 * a + t - 1], device_id=sibling, device_id_type=MESH).start()
        token[...] = jnp.zeros_like(token)

    res = pl.pallas_call(
        body, name=name,
        out_shape=[pltpu.HBM(a.shape, a.dtype) for a in lands]
        + [pltpu.SemaphoreType.DMA((3 * n,)), pltpu.SemaphoreType.DMA((3 * n,)), _sds((8, LANES), F32)],
        in_specs=[HBM] * n + [SEM, SEM] + [ANY] * len(after),
        out_specs=[HBM] * n + [SEM, SEM, pl.BlockSpec(memory_space=pltpu.VMEM)],
        input_output_aliases={i: i for i in range(n)},
        compiler_params=pltpu.CompilerParams(has_side_effects=EFFECT),
    )(*lands, send, recv, *after)
    return list(res[:n]), res[n], res[n + 1], res[-1]


def _gather_wait(name, lands, first, recv, send2, recv2, after):
    n = len(lands)

    def body(*refs):
        lnd, recv_sem, send2_sem, recv2_sem = refs[:n], refs[n], refs[n + 1], refs[n + 2]
        mine, targets, passed, sibling = _gather_targets()
        for a in range(n):
            dev, flat = targets[0]
            pltpu.make_async_remote_copy(src_ref=lnd[a].at[mine], dst_ref=lnd[a].at[flat], send_sem=send2_sem.at[3 * a],
                                         recv_sem=recv_sem.at[4 * (first + a)], device_id=dev, device_id_type=MESH).wait_recv()
            for t in range(3):
                cp = pltpu.make_async_remote_copy(src_ref=lnd[a].at[targets[t + 1][1]], dst_ref=lnd[a].at[passed[t]],
                                                  send_sem=send2_sem.at[3 * a + t], recv_sem=recv2_sem.at[3 * a + t],
                                                  device_id=sibling, device_id_type=MESH)
                cp.wait_send()
                cp.wait_recv()

    res = pl.pallas_call(
        body, name=name, out_shape=[pltpu.HBM(a.shape, a.dtype) for a in lands],
        in_specs=[HBM] * n + [SEM, SEM, SEM, ANY], out_specs=[HBM] * n,
        input_output_aliases={i: i for i in range(n)},
        compiler_params=pltpu.CompilerParams(has_side_effects=EFFECT),
    )(*lands, recv, send2, recv2, after)
    return list(res)


def _adamw_decay(w, m, v):
    return ADAM_WD * w, ADAM_B1 * m, ADAM_B2 * v


def _adamw_finish(g, wd_w, m1, v1):
    m = m1 + (1.0 - ADAM_B1) * g
    v = v1 + (1.0 - ADAM_B2) * (g * g)
    m_hat = m / (1.0 - ADAM_B1 ** ADAM_STEP)
    v_hat = v / (1.0 - ADAM_B2 ** ADAM_STEP)
    delta = -ADAM_LR * (m_hat / (jnp.sqrt(v_hat) + ADAM_EPS) + wd_w)
    return delta, m, v


def _adamw(g, w, m, v):
    return _adamw_finish(g, *_adamw_decay(w, m, v))


def _update_prep(name, w, m, v, dep, block_bytes=1 << 20):
    _, r, c = w.shape
    tr = max(8, min(r, (block_bytes // (4 * c)) // 8 * 8))
    while r % tr:
        tr -= 8

    def body(w_ref, m_ref, v_ref, dep_ref, ow_ref, om_ref, ov_ref):
        del dep_ref
        ow_ref[...], om_ref[...], ov_ref[...] = _adamw_decay(w_ref[...], m_ref[...], v_ref[...])

    blk = pl.BlockSpec((None, tr, c), lambda i: (0, i, 0))
    return pl.pallas_call(
        body, name=name, grid=(r // tr,), in_specs=[blk] * 3 + [ANY], out_specs=[blk] * 3,
        out_shape=[_sds((1, r, c), F32)] * 3, compiler_params=_params(("parallel",)),
    )(w, m, v, dep)


def _update(name, parts, w, m, v, layout=None, decayed=False, transposed_out=False, block_bytes=1 << 20):
    _, r, c = w.shape
    n_slots, _, cp = parts.shape
    tr = max(8, min(r, (block_bytes // (4 * cp)) // 8 * 8))
    if transposed_out:
        tr = _tile(r, 256)
    while r % tr:
        tr -= 8

    def body(p_ref, w_ref, m_ref, v_ref, g_ref, d_ref, nm_ref, nv_ref, *scratch):
        g = p_ref[0].astype(F32)
        for p in range(1, n_slots):
            g = g + p_ref[p].astype(F32)
        if layout is not None:
            s1, s2, lg = layout.my_shifts()
            lane = lax.broadcasted_iota(jnp.int32, g.shape, 1)
            scratch[0][...] = jnp.where(lane < lg, pltpu.roll(g, cp - s1, 1), pltpu.roll(g, cp - s2, 1))
            g = scratch[0][:, 0:c]
        step = _adamw_finish if decayed else _adamw
        results = (g,) + step(g, w_ref[...], m_ref[...], v_ref[...])
        if ragged:
            scratch[-2][...] = jnp.zeros_like(scratch[-2])
        for ref, val in zip((g_ref, d_ref, nm_ref, nv_ref), results):
            if not transposed_out:
                ref[...] = val
            elif not ragged:
                ref[...] = val.T
            else:
                wide, tall = scratch[-2], scratch[-1]
                wide[:, 0:c] = val
                tall[...] = wide[...].T
                ref[...] = tall[0:c, :]

    ragged = transposed_out and c % 8 != 0
    c_wide = -(-c // LANES) * LANES
    blk = pl.BlockSpec((None, tr, c), lambda i: (0, i, 0))
    out_blk = pl.BlockSpec((None, c, tr), lambda i: (0, 0, i)) if transposed_out else blk
    scratch_shapes = [] if layout is None else [pltpu.VMEM((tr, cp), F32)]
    if ragged:
        scratch_shapes += [pltpu.VMEM((tr, c_wide), F32), pltpu.VMEM((c_wide, tr), F32)]
    res = pl.pallas_call(
        body, name=name, grid=(r // tr,),
        in_specs=[pl.BlockSpec((n_slots, tr, cp), lambda i: (0, i, 0)), blk, blk, blk],
        out_specs=[out_blk] * 4, out_shape=[_sds((1, c, r) if transposed_out else (1, r, c), F32)] * 4,
        scratch_shapes=scratch_shapes,
        compiler_params=_params(("parallel",)),
    )(parts, w, m, v)
    return [jnp.transpose(o, (0, 2, 1)) for o in res] if transposed_out else res


def _small_update(part, w, m, v):
    n = part.shape[1]

    def body(p_ref, w_ref, m_ref, v_ref, g_ref, d_ref, nm_ref, nv_ref, buf, send, recv):
        me, peers = _mesh_place()
        buf[me] = p_ref[...]
        sent = []
        for d, dev, flat in peers:
            cp = pltpu.make_async_remote_copy(src_ref=p_ref, dst_ref=buf.at[me], send_sem=send.at[d],
                                              recv_sem=recv.at[d], device_id=dev, device_id_type=MESH)
            cp.start()
            sent.append(cp)
        for d, dev, flat in peers:
            pltpu.make_async_remote_copy(src_ref=p_ref, dst_ref=buf.at[flat], send_sem=send.at[d],
                                         recv_sem=recv.at[d], device_id=dev, device_id_type=MESH).wait_recv()
        for cp in sent:
            cp.wait_send()
        g = buf[0]
        for p in range(1, N_DEV):
            g = g + buf[p]
        g_ref[...] = g
        d_ref[...], nm_ref[...], nv_ref[...] = _adamw(g, w_ref[...], m_ref[...], v_ref[...])

    vm = pl.BlockSpec(memory_space=pltpu.VMEM)
    return pl.pallas_call(
        body, name="small_update", in_specs=[vm] * 4, out_specs=[vm] * 4, out_shape=[_sds((1, n), F32)] * 4,
        scratch_shapes=[pltpu.VMEM((N_DEV, 1, n), F32), pltpu.SemaphoreType.DMA((N_DEV,)),
                        pltpu.SemaphoreType.DMA((N_DEV,))],
    )(part, w, m, v)


class _WInLayout:
    def __init__(self, n8, n_f, d_sb, d_fox, d):
        assert n8 % LANES == 1 and n_f < LANES and d % (N_DEV * LANES) == 0
        self.n8, self.n_f, self.d = n8, n_f, d
        self.sp = n8 // LANES
        self.wp = (n8 + 2 * LANES - 2) // LANES * LANES
        self.n_qkv = 3 * (d_sb + d_fox)
        nq, dt, tc = self.n_qkv // LANES, d // LANES, d // N_DEV // LANES
        h_sb, h_fox = d_sb // HEAD_DIM, d_fox // HEAD_DIM
        self.sources = {}
        self.part_tile = {}
        for p in range(N_DEV):
            lg = min(max(self.n_qkv + n_f - n8 * p, 0), n8)
            s1, s2 = p, p + LANES - n_f
            spans = []
            if lg > 0:
                spans.append(("a", self.sp * p, s1 // LANES, (lg + s1 - 1) // LANES))
            if lg < n8:
                spans.append(("g", self.sp * p - 1 - nq, (lg + s2) // LANES, (n8 - 1 + s2) // LANES))
            for kind, base, first, last in spans:
                for i in range(first, last + 1):
                    assert (p, i) not in self.part_tile
                    self.part_tile[(p, i)] = (kind, base + i)
                    self.sources.setdefault((kind, base + i), []).append((p, i))
        self.cat_tiles = [("a", r * h_sb + h) for h in range(h_sb) for r in range(3)]
        self.cat_tiles += [("a", 3 * h_sb + r * h_fox + h) for h in range(h_fox) for r in range(3)]
        self.cat_tiles += [("g", which * dt + j * tc + half) for j in range(N_DEV) for which in (0, 1) for half in range(tc)]
        self.cat_tiles += [("a", nq)] + [None] * (F_PAD // LANES - 1)
        self.cat_index = {key: c for c, key in enumerate(self.cat_tiles) if key is not None}

    def my_shifts(self):
        me = _flat_me()
        return me, me + LANES - self.n_f, jnp.clip(self.n_qkv + self.n_f - self.n8 * me, 0, self.n8)


def _lane_tile(i):
    return pl.ds(i * LANES, LANES)


def _w_in_shift(w_in, lay, tr=256):
    _, d, n8 = w_in.shape
    kd = d // LANES
    kt = tr // LANES
    by_col = jnp.transpose(w_in, (0, 2, 1)).reshape(n8 * kd, LANES)

    def body(w_ref, o_ref, buf):
        k0 = kt * pl.program_id(0)
        buf[...] = jnp.zeros_like(buf)
        for j in range(n8 // LANES):
            for kk in range(kt):
                piece = w_ref[pl.ds(j * LANES * kd + k0 + kk, LANES, stride=kd), :]
                buf[kk * LANES:(kk + 1) * LANES, j * LANES:(j + 1) * LANES] = piece.T
        first = lax.broadcasted_iota(jnp.int32, (8, LANES), 0) == 0
        for kk in range(kt):
            row = w_ref[pl.ds((n8 - 1) * kd + k0 + kk, 1), :]
            buf[kk * LANES:(kk + 1) * LANES, n8 - 1:n8 + 7] = jnp.where(first, jnp.broadcast_to(row, (8, LANES)), 0.0).T
        v = buf[...]
        s1, s2, lg = lay.my_shifts()
        pos = lax.broadcasted_iota(jnp.int32, v.shape, 1)
        o_ref[...] = jnp.where(pos < lg + s1, pltpu.roll(v, s1, 1),
                               jnp.where(pos >= lg + s2, pltpu.roll(v, s2, 1), 0.0)).astype(BF16)

    return pl.pallas_call(
        body, name="w_in_shift", grid=(d // tr,),
        in_specs=[pl.BlockSpec((n8 * kd, LANES), lambda i: (0, 0))],
        out_specs=pl.BlockSpec((tr, lay.wp), lambda i: (i, 0)), out_shape=_sds((d, lay.wp), BF16),
        scratch_shapes=[pltpu.VMEM((tr, lay.wp), F32)],
        compiler_params=_params(("arbitrary",)),
    )(by_col)


def _w_in_build(g_in, lay, tr=256):
    d = g_in.shape[1]
    width = len(lay.cat_tiles) * LANES

    def body(g_ref, o_ref):
        for c, key in enumerate(lay.cat_tiles):
            if key is None:
                o_ref[:, _lane_tile(c)] = jnp.zeros((tr, LANES), BF16)
                continue
            (p, i), *more = lay.sources[key]
            val = g_ref[p, :, _lane_tile(i)]
            for p2, i2 in more:
                val = val + g_ref[p2, :, _lane_tile(i2)]
            o_ref[:, _lane_tile(c)] = val

    return pl.pallas_call(
        body, name="w_in_build", grid=(d // tr,),
        in_specs=[pl.BlockSpec((N_DEV, tr, lay.wp), lambda i: (0, i, 0))],
        out_specs=pl.BlockSpec((tr, width), lambda i: (i, 0)), out_shape=_sds((d, width), BF16),
        compiler_params=_params(("parallel",)),
    )(g_in)


def _w_in_grad_parts(dwq, dwgf, lay, tr=256):
    d = dwq.shape[0]
    nq = lay.n_qkv // LANES

    def body(q_ref, g_ref, o_ref):
        for p in range(N_DEV):
            for i in range(lay.wp // LANES):
                key = lay.part_tile.get((p, i))
                if key is None:
                    o_ref[p, :, _lane_tile(i)] = jnp.zeros((tr, LANES), BF16)
                    continue
                c = lay.cat_index[key]
                o_ref[p, :, _lane_tile(i)] = q_ref[:, _lane_tile(c)] if c < nq else g_ref[:, _lane_tile(c - nq)]

    return pl.pallas_call(
        body, name="w_in_grad_parts", grid=(d // tr,),
        in_specs=[pl.BlockSpec((tr, dwq.shape[1]), lambda i: (i, 0)), pl.BlockSpec((tr, dwgf.shape[1]), lambda i: (i, 0))],
        out_specs=pl.BlockSpec((N_DEV, tr, lay.wp), lambda i: (0, i, 0)), out_shape=_sds((N_DEV, d, lay.wp), BF16),
        compiler_params=_params(("parallel",)),
    )(dwq, dwgf)


def kernel(x, norm_mix_pre, norm_mix_post, w_in, b_forget, w_branch_sb, w_branch_fox, w_out, norm_ffn_pre, norm_ffn_post, w_ffn_gate, w_ffn_up, w_ffn_down, loss_target, m_norm_mix_pre, m_norm_mix_post, m_w_in, m_b_forget, m_w_branch_sb, m_w_branch_fox, m_w_out, m_norm_ffn_pre, m_norm_ffn_post, m_w_ffn_gate, m_w_ffn_up, m_w_ffn_down, v_norm_mix_pre, v_norm_mix_post, v_w_in, v_b_forget, v_w_branch_sb, v_w_branch_fox, v_w_out, v_norm_ffn_pre, v_norm_ffn_post, v_w_ffn_gate, v_w_ffn_up, v_w_ffn_down):
    xs, target = x[0], loss_target[0]
    s, d = xs.shape
    d_sb, d_fox = w_branch_sb.shape[1], w_branch_fox.shape[1]
    h_sb, h_fox = d_sb // HEAD_DIM, d_fox // HEAD_DIM
    n_f = b_forget.shape[1]
    fs = w_ffn_gate.shape[2]
    cs = d // N_DEV
    n_qkv = 3 * (d_sb + d_fox)
    n_gf = 2 * d + F_PAD
    f_blk = 2 * d // LANES
    big = (w_in, w_branch_sb, w_branch_fox, w_out, w_ffn_gate, w_ffn_up, w_ffn_down)
    big_m = (m_w_in, m_w_branch_sb, m_w_branch_fox, m_w_out, m_w_ffn_gate, m_w_ffn_up, m_w_ffn_down)
    big_v = (v_w_in, v_w_branch_sb, v_w_branch_fox, v_w_out, v_w_ffn_gate, v_w_ffn_up, v_w_ffn_down)

    lay = _WInLayout(w_in.shape[2], n_f, d_sb, d_fox, d)
    send1, recv1, lands, token = _gather_start([_w_in_shift(w_in, lay)] + [w[0].astype(BF16) for w in big[1:]])
    b_pad = jnp.pad(b_forget, ((0, 0), (0, LANES - n_f)))

    started = token[0, 0]
    u, u_t = _pre_norm(xs, norm_mix_pre, dep=token)
    weights = dict(zip(("w_in", "w_branch_sb", "w_branch_fox", "w_out", "w_ffn_gate", "w_ffn_up", "w_ffn_down"),
                       zip(big, big_m, big_v)))
    decayed = {nm: _update_prep("decay_" + nm, *[t + started for t in weights[nm]], u)
               for nm in ("w_in", "w_ffn_gate", "w_ffn_up")}
    l_in, send2, recv2, token = _gather_forward("gather_in_forward", lands[0:1], 0, send1, recv1,
                                                [u] + [t[2] for t in decayed.values()])
    (g_in,) = _gather_wait("gather_in_wait", l_in, 0, recv1, send2, recv2, token)
    w_cat = _w_in_build(g_in, lay)
    qkv = _mm_plain("proj_qkv", "nn", u, w_cat, BF16, n=n_qkv)
    gf = _mm_plain("proj_gates", "nn", u, w_cat, F32, n_off=n_qkv, n=n_gf)
    cum_col, cum_row = _forget_fwd(gf, b_pad, f_blk)
    o_sb, o_sb_t, tot = _sb_fwd(qkv, h_sb)
    l_mid, send2, recv2, token = _gather_forward("gather_mid_forward", lands[1:4], 1, send1, recv1, [o_sb])
    o_fx, o_fx_t, o_fx32, lse = _fox_fwd(qkv, cum_col, cum_row, h_fox, h_sb, token)
    g_sb, g_fx, g_out = _gather_wait("gather_mid_wait", l_mid, 1, recv1, send2, recv2, o_fx)
    w_out_full = g_out.reshape(d, d)
    merged, merged_t, a_sb, a_fx = _branch_merge(o_sb, o_fx, g_sb, g_fx, gf, o_fx)
    l_ffn, send2, recv2, token = _gather_forward("gather_ffn_forward", lands[4:6], 4, send1, recv1, [merged])
    mix = _mm_plain("out_proj", "nn", merged, w_out_full, F32, dep=token)
    h1, u2, u2_t = _mid_norms(xs, mix, norm_mix_post, norm_ffn_pre)
    g_gate, g_up = _gather_wait("gather_ffn_wait", l_ffn, 4, recv1, send2, recv2, u2)
    l_down, send2, recv2, token = _gather_forward("gather_down_forward", lands[6:7], 6, send1, recv1, [u2])
    gate, up, act, act_t = _ffn_up(u2, g_gate, g_up, token)
    (g_down,) = _gather_wait("gather_down_wait", l_down, 6, recv1, send2, recv2, act)
    tm, tn = _tile(s, 1024), _tile(d, 1024)
    ff = _matmul("ffn_down", "nn",
                 [(act, pl.BlockSpec((None, tm, fs), lambda i, j, k: (k, i, 0)),
                   g_down, pl.BlockSpec((None, fs, tn), lambda i, j, k: (k, 0, j)))],
                 (s // tm, d // tn, N_DEV), (tm, tn), _sds((s, d), F32), pl.BlockSpec((tm, tn), lambda i, j, k: (i, j)))
    loss_part, dy, dff, dg_ffn_post = _loss_head(h1, ff, target, norm_ffn_post)

    dgate, dup = _ffn_down_bwd(dff, g_down, gate, up)
    dw_down = _matmul("dw_down", "nn",
                      [(act_t, pl.BlockSpec((None, fs, s), lambda j, n, k: (j, 0, 0)),
                        dff, pl.BlockSpec((s, tn), lambda j, n, k: (0, n)))],
                      (N_DEV, d // tn, 1), (fs, tn), _sds((N_DEV, fs, d), BF16),
                      pl.BlockSpec((None, fs, tn), lambda j, n, k: (j, 0, n)))

    def dw_up(name, dact):
        return _matmul(name, "nn",
                       [(u2_t, pl.BlockSpec((tn, s), lambda j, i, k: (i, 0)),
                         dact, pl.BlockSpec((None, s, fs), lambda j, i, k: (j, 0, 0)))],
                       (N_DEV, d // tn, 1), (tn, fs), _sds((N_DEV, d, fs), BF16),
                       pl.BlockSpec((None, tn, fs), lambda j, i, k: (j, i, 0)))

    dw_gate, dw_upw = dw_up("dw_gate", dgate), dw_up("dw_up", dup)
    rs_ffn = _scatter_pairs("ffn", [dw_gate, dw_upw, dw_down])
    a_spec = pl.BlockSpec((None, tm, fs), lambda i, j, k: (k, i, 0))
    b_spec = pl.BlockSpec((None, tn, fs), lambda i, j, k: (k, j, 0))
    du2 = _matmul("du2", "nt", [(dgate, a_spec, g_gate, b_spec), (dup, a_spec, g_up, b_spec)],
                  (s // tm, d // tn, N_DEV), (tm, tn), _sds((s, d), F32), pl.BlockSpec((tm, tn), lambda i, j, k: (i, j)),
                  dep=rs_ffn[4])
    rs_ffn = _scatter_chips("ffn", rs_ffn, du2)
    dh1, dmix, dg_ffn_pre, dg_mix_post = _mid_norms_bwd(dy, du2, h1, mix, norm_ffn_pre, norm_mix_post)

    da_sb, da_fx, dgf = _merge_bwd(dmix, w_out_full, gf, a_sb, a_fx, dep=rs_ffn[4])
    dw_out = _mm_plain("dw_out", "nn", merged_t, dmix, BF16).reshape(N_DEV, cs, d)

    def branch_bwd(tag, da, w_b, o_t, width):
        tb = _tile(width, 1024)
        do = _matmul("do_" + tag, "nt",
                     [(da, pl.BlockSpec((tm, cs), lambda i, j, k: (i, k)),
                       w_b, pl.BlockSpec((None, tb, cs), lambda i, j, k: (k, j, 0)))],
                     (s // tm, width // tb, N_DEV), (tm, tb), _sds((s, width), BF16),
                     pl.BlockSpec((tm, tb), lambda i, j, k: (i, j)))
        dw = _matmul("dw_" + tag, "nn",
                     [(o_t, pl.BlockSpec((width, s), lambda j, i, k: (0, 0)),
                       da, pl.BlockSpec((s, cs), lambda j, i, k: (0, j)))],
                     (N_DEV, 1, 1), (width, cs), _sds((N_DEV, width, cs), BF16),
                     pl.BlockSpec((None, width, cs), lambda j, i, k: (j, 0, 0)))
        return do, dw

    do_sb, dw_sb = branch_bwd("sb", da_sb, g_sb, o_sb_t, d_sb)
    do_fx, dw_fx = branch_bwd("fox", da_fx, g_fx, o_fx_t, d_fox)

    rs_mid = _scatter_pairs("mid", [dw_sb, dw_fx, dw_out])

    dqkv = _sb_bwd(qkv, do_sb, tot, h_sb, rs_mid[4])
    rs_mid = _scatter_chips("mid", rs_mid, dqkv)
    dqkv, dcum = _fox_bwd(dqkv, qkv, do_fx, o_fx32, lse, cum_col, cum_row, h_fox, h_sb, rs_mid[4])
    dgf, db_part = _forget_bwd(dgf, dcum, gf, b_pad, f_blk)
    dw_in = _w_in_grad_parts(_mm_plain("dw_qkv", "nn", u_t, dqkv, BF16), _mm_plain("dw_gates", "nn", u_t, dgf, BF16), lay)
    rs_in = _scatter_pairs("in", [dw_in])
    du = _mm_plain("du_qkv", "nt", dqkv, w_cat, F32, tn=1024, dep=rs_in[4])
    rs_in = _scatter_chips("in", rs_in, du)
    du = _mm_plain("du_gates", "nt", dgf, w_cat, F32, tn=1024, k_off=n_qkv, init=du, dep=rs_in[4])
    dx, dg_mix_pre = _pre_norm_bwd(dh1, du, xs, norm_mix_pre)

    upd = {}

    def update_group(tag, rs, names, after):
        parts = _scatter_end(tag, rs, after)
        for nm, p in zip(names, parts):
            w, m, v = decayed.get(nm, weights[nm])
            upd[nm] = _update("update_" + nm, p, w, m, v, layout=lay if nm == "w_in" else None, decayed=nm in decayed,
                              transposed_out=nm in ("w_in", "w_ffn_gate", "w_ffn_up"))

    update_group("ffn", rs_ffn, ("w_ffn_gate", "w_ffn_up", "w_ffn_down"), [dx])
    update_group("mid", rs_mid, ("w_branch_sb", "w_branch_fox", "w_out"), [upd[nm][3] for nm in ("w_ffn_gate", "w_ffn_up", "w_ffn_down")])
    update_group("in", rs_in, ("w_in",), [upd[nm][3] for nm in ("w_branch_sb", "w_branch_fox", "w_out")])

    small = ((norm_mix_pre, m_norm_mix_pre, v_norm_mix_pre), (norm_mix_post, m_norm_mix_post, v_norm_mix_post),
             (norm_ffn_pre, m_norm_ffn_pre, v_norm_ffn_pre), (norm_ffn_post, m_norm_ffn_post, v_norm_ffn_post))
    pad_f = ((0, 0), (0, LANES - n_f))
    cat = lambda i: jnp.concatenate([t[i] for t in small] + [jnp.pad((b_forget, m_b_forget, v_b_forget)[i], pad_f)], axis=1)
    sm = _small_update(jnp.concatenate([dg_mix_pre, dg_mix_post, dg_ffn_pre, dg_ffn_post, db_part], axis=1),
                       cat(0), cat(1), cat(2))
    for i, nm in enumerate(("norm_mix_pre", "norm_mix_post", "norm_ffn_pre", "norm_ffn_post")):
        upd[nm] = [o[:, i * d:(i + 1) * d] for o in sm]
    upd["b_forget"] = [o[:, 4 * d:4 * d + n_f] for o in sm]

    loss = lax.psum(loss_part[0, 0], ("x", "y", "c"))
    order = ("norm_mix_pre", "norm_mix_post", "w_in", "b_forget", "w_branch_sb", "w_branch_fox", "w_out",
             "norm_ffn_pre", "norm_ffn_post", "w_ffn_gate", "w_ffn_up", "w_ffn_down")
    return (loss, dx[None]) + tuple(upd[nm][i] for i in range(4) for nm in order)
```

```python
import jax
import jax.numpy as jnp
from jax import lax
from jax.experimental import pallas as pl
from jax.experimental.pallas import tpu as pltpu

F32 = jnp.float32
BF16 = jnp.bfloat16
MESH = pl.DeviceIdType.MESH
ANY = pl.BlockSpec(memory_space=pl.ANY)
HBM = pl.BlockSpec(memory_space=pltpu.HBM)
SEM = pl.BlockSpec(memory_space=pltpu.SEMAPHORE)
EFFECT = pltpu.SideEffectType.DATAFLOW_SIDE_EFFECTING

N_DEV = 8
HEAD_DIM = 128
RMS_EPS = 1e-6
F_PAD = 512
LANES = 128
ATT_TQ = 256
ATT_TK = 256
ATT_HP = 4
NEG_BIG = -1e30
VMEM_LIMIT = 56 * 1024 * 1024

ADAM_LR = 0.001
ADAM_B1 = 0.9
ADAM_B2 = 0.999
ADAM_EPS = 1e-08
ADAM_WD = 0.01
ADAM_STEP = 10

_DIMS = {"nn": ((1,), (0,)), "nt": ((1,), (1,)), "tn": ((0,), (0,))}


def _params(sem):
    return pltpu.CompilerParams(dimension_semantics=sem, vmem_limit_bytes=VMEM_LIMIT)


def _dot(a, b, mode="nn"):
    return lax.dot_general(a.astype(BF16), b.astype(BF16), (_DIMS[mode], ((), ())), preferred_element_type=F32)


def _tile(n, pref):
    if n <= pref:
        return n
    t = (pref // LANES) * LANES
    while n % t:
        t -= LANES
    return t


def _split2(v):
    hi = v.astype(BF16)
    return hi, (v - hi.astype(F32)).astype(BF16)


def _split3(v):
    a = v.astype(BF16)
    r = v - a.astype(F32)
    b = r.astype(BF16)
    return a, b, (r - b.astype(F32)).astype(BF16)


def _tri(n, cmp):
    r = lax.broadcasted_iota(jnp.int32, (n, n), 0)
    c = lax.broadcasted_iota(jnp.int32, (n, n), 1)
    return jnp.where(cmp(r, c), 1.0, 0.0).astype(BF16)


def _lane_pick(v, h):
    lane = lax.broadcasted_iota(jnp.int32, v.shape, 1)
    return jnp.sum(jnp.where(lane == h, v, 0.0), axis=1, keepdims=True)


def _lane_put(ref, rows, h, col):
    old = ref[rows, :]
    lane = lax.broadcasted_iota(jnp.int32, old.shape, 1)
    ref[rows, :] = jnp.where(lane == h, col, old)


def _sigmoid(z):
    return 1.0 / (1.0 + jnp.exp(-z))


def _log_sigmoid(z):
    return jnp.minimum(z, 0.0) - jnp.log(1.0 + jnp.exp(-jnp.abs(z)))


def _sds(shape, dtype):
    return jax.ShapeDtypeStruct(shape, dtype)


def _matmul(name, mode, pairs, grid, acc_shape, out_shape, out_specs, extras=(), epilogue=None, init=None, dep=None):
    n_p, n_e = len(pairs), len(extras)
    nk = grid[-1]
    single = not isinstance(out_shape, (list, tuple))
    n_i = 0 if init is None else 1
    n_d = 0 if dep is None else 1

    one_step = nk == 1 and init is None

    def body(*refs):
        ab = refs[:2 * n_p]
        ex = refs[2 * n_p:2 * n_p + n_e]
        ini = refs[2 * n_p + n_e:2 * n_p + n_e + n_i]
        outs = refs[2 * n_p + n_e + n_i + n_d:len(refs) - (0 if one_step else 1)]

        def finish(total):
            if epilogue is None:
                outs[0][...] = total.astype(outs[0].dtype)
            else:
                epilogue(total, ex, outs)

        t = _dot(ab[0][...], ab[1][...], mode)
        for p in range(1, n_p):
            t = t + _dot(ab[2 * p][...], ab[2 * p + 1][...], mode)
        if one_step:
            finish(t)
            return
        acc = refs[-1]
        k = pl.program_id(len(grid) - 1)

        @pl.when(k == 0)
        def _():
            acc[...] = t if init is None else ini[0][...].astype(F32) + t

        @pl.when(k > 0)
        def _():
            acc[...] += t

        @pl.when(k == nk - 1)
        def _():
            finish(acc[...])

    in_specs = [s for (_, sa, _, sb) in pairs for s in (sa, sb)] + [s for (_, s) in extras]
    args = [v for (a, _, b, _) in pairs for v in (a, b)] + [e for (e, _) in extras]
    if init is not None:
        in_specs.append(init[1])
        args.append(init[0])
    if dep is not None:
        in_specs.append(ANY)
        args.append(dep)
    return pl.pallas_call(
        body, name=name, grid=grid, in_specs=in_specs,
        out_specs=out_specs if single else list(out_specs),
        out_shape=out_shape if single else list(out_shape),
        scratch_shapes=[] if one_step else [pltpu.VMEM(acc_shape, F32)],
        compiler_params=_params(("parallel",) * (len(grid) - 1) + ("arbitrary",)),
    )(*args)


def _mm_plain(name, mode, a, b, out_dtype, *, n_off=0, n=None, k_off=0, tm=1024, tn=1536, tk=2048, init=None, dep=None):
    if mode == "nn":
        (m, kk), nn_ = a.shape, b.shape[1]
    elif mode == "nt":
        (m, kk), nn_ = a.shape, b.shape[0]
    else:
        (kk, m), nn_ = a.shape, b.shape[1]
    n = nn_ if n is None else n
    tm, tn, tk = _tile(m, tm), _tile(n, tn), _tile(kk, tk)
    while n_off % tn or n % tn:
        tn -= LANES
    while k_off % tk or kk % tk:
        tk -= LANES
    off, koff = n_off // tn, k_off // tk
    a_spec = {"nn": pl.BlockSpec((tm, tk), lambda i, j, k: (i, k)),
              "nt": pl.BlockSpec((tm, tk), lambda i, j, k: (i, k)),
              "tn": pl.BlockSpec((tk, tm), lambda i, j, k: (k, i))}[mode]
    b_spec = {"nn": pl.BlockSpec((tk, tn), lambda i, j, k: (k, j + off)),
              "nt": pl.BlockSpec((tn, tk), lambda i, j, k: (j, k + koff)),
              "tn": pl.BlockSpec((tk, tn), lambda i, j, k: (k, j))}[mode]
    o_spec = pl.BlockSpec((tm, tn), lambda i, j, k: (i, j))
    if init is not None:
        init = (init, o_spec)
    return _matmul(name, mode, [(a, a_spec, b, b_spec)], (m // tm, n // tn, kk // tk), (tm, tn),
                   _sds((m, n), out_dtype), o_spec, init=init, dep=dep)


def _rows_call(name, body, ins, outs, s, tr=256, dep=None):
    def spec(v, per_row):
        if per_row == "transposed":
            return pl.BlockSpec((v.shape[0], tr), lambda i: (0, i))
        if per_row:
            return pl.BlockSpec((tr, v.shape[1]), lambda i: (i, 0))
        return pl.BlockSpec(v.shape, lambda i: (0, 0))
    n_in = len(ins)
    deps = [] if dep is None else [dep]

    def with_dep(*refs):
        body(*refs[:n_in], *refs[n_in + len(deps):])

    return pl.pallas_call(
        with_dep, name=name, grid=(s // tr,),
        in_specs=[spec(v, p) for v, p in ins] + [ANY] * len(deps), out_specs=[spec(v, p) for v, p in outs],
        out_shape=[_sds(v.shape, v.dtype) for v, _ in outs],
        compiler_params=_params(("arbitrary",)),
    )(*[v for v, _ in ins], *deps)


def _rsq(v):
    return lax.rsqrt(jnp.mean(v * v, axis=-1, keepdims=True) + RMS_EPS)


def _norm_bwd(dy, v, r, g):
    vh = v * r
    t = dy * g
    dv = r * (t - vh * jnp.mean(t * vh, axis=-1, keepdims=True))
    return dv, jnp.sum(dy * vh, axis=0, keepdims=True)


def _accum(ref, val):
    @pl.when(pl.program_id(0) == 0)
    def _():
        ref[...] = jnp.zeros_like(ref)
    ref[...] += val


def _pre_norm(x, g, dep=None):
    def body(x_ref, g_ref, u_ref, ut_ref):
        v = x_ref[...]
        u = (v * _rsq(v) * g_ref[...]).astype(BF16)
        u_ref[...] = u
        ut_ref[...] = u.T
    s, d = x.shape
    return _rows_call("pre_norm", body, [(x, True), (g, False)],
                      [(_sds((s, d), BF16), True), (_sds((d, s), BF16), "transposed")], s, dep=dep)


def _mid_norms(x, mix, g_post, g_pre):
    def body(x_ref, mix_ref, gp_ref, gn_ref, h_ref, u_ref, ut_ref):
        mv = mix_ref[...]
        h = x_ref[...] + mv * _rsq(mv) * gp_ref[...]
        h_ref[...] = h
        u = (h * _rsq(h) * gn_ref[...]).astype(BF16)
        u_ref[...] = u
        ut_ref[...] = u.T
    s, d = x.shape
    return _rows_call("mid_norms", body, [(x, True), (mix, True), (g_post, False), (g_pre, False)],
                      [(_sds((s, d), F32), True), (_sds((s, d), BF16), True), (_sds((d, s), BF16), "transposed")], s)


def _loss_head(h1, ff, target, g):
    s, d = h1.shape

    def body(h_ref, ff_ref, t_ref, g_ref, loss_ref, dy_ref, dff_ref, dg_ref):
        fv = ff_ref[...]
        r = _rsq(fv)
        err = h_ref[...] + fv * r * g_ref[...] - t_ref[...]
        part = 0.5 * jnp.sum(jnp.mean(err * err, axis=-1, keepdims=True), axis=0, keepdims=True)
        _accum(loss_ref, jnp.broadcast_to(part, loss_ref.shape))
        dy = err * (1.0 / d)
        dy_ref[...] = dy
        dff, dg = _norm_bwd(dy, fv, r, g_ref[...])
        dff_ref[...] = dff.astype(BF16)
        _accum(dg_ref, dg)

    return _rows_call("loss_head", body, [(h1, True), (ff, True), (target, True), (g, False)],
                      [(_sds((1, LANES), F32), False), (_sds((s, d), F32), True),
                       (_sds((s, d), BF16), True), (_sds((1, d), F32), False)], s)


def _mid_norms_bwd(dy, du2, h1, mix, g_pre, g_post):
    s, d = dy.shape

    def body(dy_ref, du_ref, h_ref, mix_ref, gn_ref, gp_ref, dh_ref, dmix_ref, dgn_ref, dgp_ref):
        h = h_ref[...]
        dh, dgn = _norm_bwd(du_ref[...], h, _rsq(h), gn_ref[...])
        dh = dh + dy_ref[...]
        dh_ref[...] = dh
        _accum(dgn_ref, dgn)
        mv = mix_ref[...]
        dmix, dgp = _norm_bwd(dh, mv, _rsq(mv), gp_ref[...])
        dmix_ref[...] = dmix.astype(BF16)
        _accum(dgp_ref, dgp)

    return _rows_call("mid_norms_bwd", body,
                      [(dy, True), (du2, True), (h1, True), (mix, True), (g_pre, False), (g_post, False)],
                      [(_sds((s, d), F32), True), (_sds((s, d), BF16), True),
                       (_sds((1, d), F32), False), (_sds((1, d), F32), False)], s)


def _pre_norm_bwd(dh1, du, x, g, dep=None):
    s, d = x.shape

    def body(dh_ref, du_ref, x_ref, g_ref, dx_ref, dg_ref):
        v = x_ref[...]
        dv, dg = _norm_bwd(du_ref[...], v, _rsq(v), g_ref[...])
        dx_ref[...] = dh_ref[...] + dv
        _accum(dg_ref, dg)

    return _rows_call("pre_norm_bwd", body, [(dh1, True), (du, True), (x, True), (g, False)],
                      [(_sds((s, d), F32), True), (_sds((1, d), F32), False)], s, dep=dep)


def _forget_fwd(gf, b_pad, f_blk):
    s = gf.shape[0]
    tb = ATT_TK
    nb = s // tb

    def body(f_ref, b_ref, col_ref, row_ref):
        incl = _tri(tb, lambda r, c: c <= r)
        carry = jnp.zeros((1, LANES), F32)
        for i in range(nb):
            lf = _log_sigmoid(f_ref[pl.ds(i * tb, tb), :] + b_ref[...])
            parts = _split3(lf)
            cum = carry + _dot(incl, parts[0]) + _dot(incl, parts[1]) + _dot(incl, parts[2])
            col_ref[pl.ds(i * tb, tb), :] = cum
            row_ref[i] = cum.T
            carry = carry + jnp.sum(lf, axis=0, keepdims=True)

    return pl.pallas_call(
        body, name="forget_fwd", grid=(1,),
        in_specs=[pl.BlockSpec((s, LANES), lambda i: (0, f_blk)), pl.BlockSpec((1, LANES), lambda i: (0, 0))],
        out_specs=[pl.BlockSpec((s, LANES), lambda i: (0, 0)), pl.BlockSpec((nb, LANES, tb), lambda i: (0, 0, 0))],
        out_shape=[_sds((s, LANES), F32), _sds((nb, LANES, tb), F32)],
        compiler_params=_params(("arbitrary",)),
    )(gf, b_pad)


def _forget_bwd(dgf, dcum, gf, b_pad, f_blk):
    s = gf.shape[0]
    tb = ATT_TK
    nb = s // tb
    sec = dgf.shape[1] // F_PAD - 1

    def body(dgf_hbm, dc_ref, f_ref, b_ref, out_ref, db_ref):
        del dgf_hbm
        incl = _tri(tb, lambda r, c: c >= r)
        carry = jnp.zeros((1, LANES), F32)
        db = jnp.zeros((1, LANES), F32)
        out_ref[...] = jnp.zeros_like(out_ref)
        for i in reversed(range(nb)):
            dc = dc_ref[pl.ds(i * tb, tb), :]
            parts = _split3(dc)
            dlf = carry + _dot(incl, parts[0]) + _dot(incl, parts[1]) + _dot(incl, parts[2])
            z = f_ref[pl.ds(i * tb, tb), :] + b_ref[...]
            df = dlf * _sigmoid(-z)
            out_ref[pl.ds(i * tb, tb), pl.ds(0, LANES)] = df.astype(BF16)
            db = db + jnp.sum(df, axis=0, keepdims=True)
            carry = carry + jnp.sum(dc, axis=0, keepdims=True)
        db_ref[...] = db

    return pl.pallas_call(
        body, name="forget_bwd", grid=(1,),
        in_specs=[ANY, pl.BlockSpec((s, LANES), lambda i: (0, 0)),
                  pl.BlockSpec((s, LANES), lambda i: (0, f_blk)), pl.BlockSpec((1, LANES), lambda i: (0, 0))],
        out_specs=[pl.BlockSpec((s, F_PAD), lambda i: (0, sec)), pl.BlockSpec((1, LANES), lambda i: (0, 0))],
        out_shape=[_sds(dgf.shape, BF16), _sds((1, LANES), F32)],
        input_output_aliases={0: 0},
        compiler_params=_params(("arbitrary",)),
    )(dgf, dcum, gf, b_pad)


def _diag_mask(strict):
    r = lax.broadcasted_iota(jnp.int32, (ATT_TQ, ATT_TK), 0)
    c = lax.broadcasted_iota(jnp.int32, (ATT_TQ, ATT_TK), 1)
    return c < r if strict else c <= r


def _qkv_specs(hb0, s):
    specs = []
    for j in range(ATT_HP):
        def col(g, j=j):
            return 3 * (hb0 + ATT_HP * g + j)
        specs += [pl.BlockSpec((ATT_TQ, HEAD_DIM), lambda g, i, col=col: (i, col(g))),
                  pl.BlockSpec((s, HEAD_DIM), lambda g, i, col=col: (0, col(g) + 1)),
                  pl.BlockSpec((s, HEAD_DIM), lambda g, i, col=col: (0, col(g) + 2))]
    return specs


def _head_cols(j):
    return pl.ds(j * HEAD_DIM, HEAD_DIM)


def _sb_fwd(qkv, n_heads):
    s = qkv.shape[0]
    scale = HEAD_DIM ** -0.5
    tq, tk = ATT_TQ, ATT_TK
    heads = range(ATT_HP)

    def body(*refs):
        qkv_refs, (o_ref, ot_ref, tot_ref) = refs[:3 * ATT_HP], refs[3 * ATT_HP:]
        g, i = pl.program_id(0), pl.program_id(1)

        @pl.when((g == 0) & (i == 0))
        def _():
            tot_ref[...] = jnp.zeros_like(tot_ref)

        qs = [qkv_refs[3 * j][...] for j in heads]
        upper = _tri(tk, lambda r, c: r > c)

        def tile(kj, carry, mask):
            rows = pl.ds(pl.multiple_of(kj * tk, tk), tk)
            z = [_dot(qs[j], qkv_refs[3 * j + 1][rows, :], "nt") * scale for j in heads]
            lsz = [_log_sigmoid(z[j]) for j in heads]
            lk = [lsz[j] - z[j] if mask is None else jnp.where(mask, lsz[j] - z[j], 0.0) for j in heads]
            parts = [_split2(lk[j]) for j in heads]
            above = [carry[j][0] + _dot(parts[j][0], upper) + _dot(parts[j][1], upper) for j in heads]
            w = [jnp.exp(lsz[j] + above[j]) for j in heads]
            if mask is not None:
                w = [jnp.where(mask, w[j], 0.0) for j in heads]
            return tuple((carry[j][0] + jnp.sum(lk[j], axis=1, keepdims=True),
                          carry[j][1] + _dot(w[j], qkv_refs[3 * j + 2][rows, :])) for j in heads)

        carry = tile(i, tuple((jnp.zeros((tq, 1), F32), jnp.zeros((tq, HEAD_DIM), F32)) for _ in heads), _diag_mask(True))
        carry = lax.fori_loop(0, i, lambda n, cr: tile(i - 1 - n, cr, None), carry)
        q_rows = pl.ds(pl.multiple_of(i * tq, tq), tq)
        for j in heads:
            c, acc = carry[j]
            o = acc.astype(BF16)
            o_ref[:, _head_cols(j)] = o
            ot_ref[_head_cols(j), :] = o.T
            _lane_put(tot_ref, q_rows, ATT_HP * g + j, c)

    wide = ATT_HP * HEAD_DIM
    return pl.pallas_call(
        body, name="sb_fwd", grid=(n_heads // ATT_HP, s // tq),
        in_specs=_qkv_specs(0, s),
        out_specs=[pl.BlockSpec((tq, wide), lambda g, i: (i, g)), pl.BlockSpec((wide, tq), lambda g, i: (g, i)),
                   pl.BlockSpec((s, LANES), lambda g, i: (0, 0))],
        out_shape=[_sds((s, n_heads * HEAD_DIM), BF16), _sds((n_heads * HEAD_DIM, s), BF16), _sds((s, LANES), F32)],
        compiler_params=_params(("arbitrary", "arbitrary")),
    )(*[qkv] * (3 * ATT_HP))


def _sb_bwd(qkv, do, tot, n_heads, dep):
    s = qkv.shape[0]
    scale = HEAD_DIM ** -0.5
    tq, tk = ATT_TQ, ATT_TK
    nq = s // tq
    hd = HEAD_DIM

    heads = range(ATT_HP)

    def body(*refs):
        qkv_refs = refs[:3 * ATT_HP]
        do_ref, tot_ref, _, out_ref, dk_acc, dv_acc = refs[3 * ATT_HP:]
        g, i = pl.program_id(0), pl.program_id(1)

        @pl.when(i == 0)
        def _():
            dk_acc[...] = jnp.zeros_like(dk_acc)
            dv_acc[...] = jnp.zeros_like(dv_acc)

        qs = [qkv_refs[3 * j][...] for j in heads]
        douts = [do_ref[:, _head_cols(j)] for j in heads]
        totals = [_lane_pick(tot_ref[...], ATT_HP * g + j) for j in heads]
        incl = _tri(tk, lambda r, c: r <= c)
        excl = _tri(tk, lambda r, c: r < c)

        def tile(kj, carry, mask):
            rows = pl.ds(pl.multiple_of(kj * tk, tk), tk)
            k_t = [qkv_refs[3 * j + 1][rows, :] for j in heads]
            z = [_dot(qs[j], k_t[j], "nt") * scale for j in heads]
            dw = [_dot(douts[j], qkv_refs[3 * j + 2][rows, :], "nt") for j in heads]
            lsz = [_log_sigmoid(z[j]) for j in heads]
            lk = [lsz[j] - z[j] if mask is None else jnp.where(mask, lsz[j] - z[j], 0.0) for j in heads]
            parts = [_split2(lk[j]) for j in heads]
            below = [carry[j][0] + _dot(parts[j][0], incl) + _dot(parts[j][1], incl) for j in heads]
            w = [jnp.exp(lsz[j] + (totals[j] - below[j])) for j in heads]
            if mask is not None:
                w = [jnp.where(mask, w[j], 0.0) for j in heads]
            e = [dw[j] * w[j] for j in heads]
            parts = [_split2(e[j]) for j in heads]
            e_before = [carry[j][1] + _dot(parts[j][0], excl) + _dot(parts[j][1], excl) for j in heads]
            sg = [jnp.exp(lsz[j]) for j in heads]
            dz = [e[j] * (1.0 - sg[j]) - e_before[j] * sg[j] for j in heads]
            if mask is not None:
                dz = [jnp.where(mask, dz[j], 0.0) for j in heads]
            dz = [(dz[j] * scale).astype(BF16) for j in heads]
            for j in heads:
                dk_acc[j, rows, :] += _dot(dz[j], qs[j], "tn")
                dv_acc[j, rows, :] += _dot(w[j], douts[j], "tn")
            return tuple((carry[j][0] + jnp.sum(lk[j], axis=1, keepdims=True),
                          carry[j][1] + jnp.sum(e[j], axis=1, keepdims=True),
                          carry[j][2] + _dot(dz[j], k_t[j])) for j in heads)

        zero = jnp.zeros((tq, 1), F32)
        carry = lax.fori_loop(0, i, lambda kj, cr: tile(kj, cr, None),
                              tuple((zero, zero, jnp.zeros((tq, hd), F32)) for _ in heads))
        carry = tile(i, carry, _diag_mask(True))
        for j in heads:
            out_ref[pl.ds(pl.multiple_of(i * tq, tq), tq), pl.ds(3 * j * hd, hd)] = carry[j][2].astype(BF16)

        @pl.when(i == nq - 1)
        def _():
            for j in heads:
                out_ref[:, pl.ds((3 * j + 1) * hd, hd)] = dk_acc[j].astype(BF16)
                out_ref[:, pl.ds((3 * j + 2) * hd, hd)] = dv_acc[j].astype(BF16)

    wide = ATT_HP * hd
    return pl.pallas_call(
        body, name="sb_bwd", grid=(n_heads // ATT_HP, nq),
        in_specs=_qkv_specs(0, s) + [pl.BlockSpec((tq, wide), lambda g, i: (i, g)),
                                     pl.BlockSpec((tq, LANES), lambda g, i: (i, 0)), ANY],
        out_specs=pl.BlockSpec((s, 3 * wide), lambda g, i: (0, g)),
        out_shape=_sds(qkv.shape, BF16),
        scratch_shapes=[pltpu.VMEM((ATT_HP, s, hd), F32), pltpu.VMEM((ATT_HP, s, hd), F32)],
        compiler_params=_params(("arbitrary", "arbitrary")),
    )(*[qkv] * (3 * ATT_HP), do, tot, dep)


def _fox_fwd(qkv, cum_col, cum_row, n_heads, hb0, dep):
    s = qkv.shape[0]
    scale = HEAD_DIM ** -0.5
    tq, tk = ATT_TQ, ATT_TK

    heads = range(ATT_HP)

    def body(*refs):
        qkv_refs = refs[:3 * ATT_HP]
        cc_ref, cr_ref, _, o_ref, ot_ref, o32_ref, lse_ref = refs[3 * ATT_HP:]
        g, i = pl.program_id(0), pl.program_id(1)

        @pl.when((g == 0) & (i == 0))
        def _():
            lse_ref[...] = jnp.zeros_like(lse_ref)

        qs = [qkv_refs[3 * j][...] for j in heads]
        cqs = [_lane_pick(cc_ref[...], ATT_HP * g + j) for j in heads]

        def tile(kj, carry, mask):
            rows = pl.ds(pl.multiple_of(kj * tk, tk), tk)
            sc = [_dot(qs[j], qkv_refs[3 * j + 1][rows, :], "nt") * scale + cqs[j]
                  - cr_ref[kj, pl.ds(ATT_HP * g + j, 1), :] for j in heads]
            if mask is not None:
                sc = [jnp.where(mask, sc[j], NEG_BIG) for j in heads]
            m_new = [jnp.maximum(carry[j][0], jnp.max(sc[j], axis=1, keepdims=True)) for j in heads]
            p = [jnp.exp(sc[j] - m_new[j]) for j in heads]
            alpha = [jnp.exp(carry[j][0] - m_new[j]) for j in heads]
            parts = [_split2(p[j]) for j in heads]
            v_t = [qkv_refs[3 * j + 2][rows, :] for j in heads]
            pv = [_dot(parts[j][0], v_t[j]) + _dot(parts[j][1], v_t[j]) for j in heads]
            return tuple((m_new[j], alpha[j] * carry[j][1] + jnp.sum(p[j], axis=1, keepdims=True),
                          alpha[j] * carry[j][2] + pv[j]) for j in heads)

        carry = tuple((jnp.full((tq, 1), NEG_BIG, F32), jnp.zeros((tq, 1), F32), jnp.zeros((tq, HEAD_DIM), F32))
                      for _ in heads)
        carry = lax.fori_loop(0, i, lambda kj, cr: tile(kj, cr, None), carry)
        carry = tile(i, carry, _diag_mask(False))
        q_rows = pl.ds(pl.multiple_of(i * tq, tq), tq)
        for j in heads:
            m, l, acc = carry[j]
            o = acc / l
            o_ref[:, _head_cols(j)] = o.astype(BF16)
            ot_ref[_head_cols(j), :] = o.astype(BF16).T
            o32_ref[:, _head_cols(j)] = o
            _lane_put(lse_ref, q_rows, ATT_HP * g + j, m + jnp.log(l))

    nb = cum_row.shape[0]
    wide = ATT_HP * HEAD_DIM
    return pl.pallas_call(
        body, name="fox_fwd", grid=(n_heads // ATT_HP, s // tq),
        in_specs=_qkv_specs(hb0, s) + [pl.BlockSpec((tq, LANES), lambda g, i: (i, 0)),
                                       pl.BlockSpec((nb, 8, tk), lambda g, i: (0, 0, 0)), ANY],
        out_specs=[pl.BlockSpec((tq, wide), lambda g, i: (i, g)), pl.BlockSpec((wide, tq), lambda g, i: (g, i)),
                   pl.BlockSpec((tq, wide), lambda g, i: (i, g)), pl.BlockSpec((s, LANES), lambda g, i: (0, 0))],
        out_shape=[_sds((s, n_heads * HEAD_DIM), BF16), _sds((n_heads * HEAD_DIM, s), BF16),
                   _sds((s, n_heads * HEAD_DIM), F32), _sds((s, LANES), F32)],
        compiler_params=_params(("arbitrary", "arbitrary")),
    )(*[qkv] * (3 * ATT_HP), cum_col, cum_row, dep)


def _fox_bwd(dqkv, qkv, do, o, lse, cum_col, cum_row, n_heads, hb0, dep):
    s = qkv.shape[0]
    scale = HEAD_DIM ** -0.5
    tq, tk = ATT_TQ, ATT_TK
    nq = s // tq
    hd = HEAD_DIM

    heads = range(ATT_HP)
    assert hb0 % ATT_HP == 0

    def body(*refs):
        qkv_refs = refs[1:1 + 3 * ATT_HP]
        do_ref, o_ref, lse_ref, cc_ref, cr_ref, _, out_ref, dc_ref, dk_acc, dv_acc, col_acc = refs[1 + 3 * ATT_HP:]
        g, i = pl.program_id(0), pl.program_id(1)

        @pl.when((g == 0) & (i == 0))
        def _():
            dc_ref[...] = jnp.zeros_like(dc_ref)

        @pl.when(i == 0)
        def _():
            dk_acc[...] = jnp.zeros_like(dk_acc)
            dv_acc[...] = jnp.zeros_like(dv_acc)
            col_acc[...] = jnp.zeros_like(col_acc)

        qs = [qkv_refs[3 * j][...] for j in heads]
        douts = [do_ref[:, _head_cols(j)] for j in heads]
        deltas = [jnp.sum(douts[j].astype(F32) * o_ref[:, _head_cols(j)], axis=1, keepdims=True) for j in heads]
        shifts = [_lane_pick(cc_ref[...], ATT_HP * g + j) - _lane_pick(lse_ref[...], ATT_HP * g + j) for j in heads]

        def tile(kj, carry, mask):
            rows = pl.ds(pl.multiple_of(kj * tk, tk), tk)
            k_t = [qkv_refs[3 * j + 1][rows, :] for j in heads]
            sc = [_dot(qs[j], k_t[j], "nt") * scale + shifts[j] - cr_ref[kj, pl.ds(ATT_HP * g + j, 1), :] for j in heads]
            dp = [_dot(douts[j], qkv_refs[3 * j + 2][rows, :], "nt") for j in heads]
            p = [jnp.exp(sc[j]) for j in heads]
            if mask is not None:
                p = [jnp.where(mask, p[j], 0.0) for j in heads]
            ds_f = [p[j] * (dp[j] - deltas[j]) for j in heads]
            ds = [(ds_f[j] * scale).astype(BF16) for j in heads]
            for j in heads:
                col_acc[j, kj] += jnp.broadcast_to(jnp.sum(ds_f[j], axis=0, keepdims=True), (8, tk))
                dk_acc[j, rows, :] += _dot(ds[j], qs[j], "tn")
                dv_acc[j, rows, :] += _dot(p[j], douts[j], "tn")
            return tuple((carry[j][0] + _dot(ds[j], k_t[j]), carry[j][1] + jnp.sum(ds_f[j], axis=1, keepdims=True))
                         for j in heads)

        carry = lax.fori_loop(0, i, lambda kj, cr: tile(kj, cr, None),
                              tuple((jnp.zeros((tq, hd), F32), jnp.zeros((tq, 1), F32)) for _ in heads))
        carry = tile(i, carry, _diag_mask(False))
        q_rows = pl.ds(pl.multiple_of(i * tq, tq), tq)
        for j in heads:
            out_ref[q_rows, pl.ds(3 * j * hd, hd)] = carry[j][0].astype(BF16)
            _lane_put(dc_ref, q_rows, ATT_HP * g + j, carry[j][1])

        @pl.when(i == nq - 1)
        def _():
            lane = lax.broadcasted_iota(jnp.int32, (tk, LANES), 1)
            for j in heads:
                out_ref[:, pl.ds((3 * j + 1) * hd, hd)] = dk_acc[j].astype(BF16)
                out_ref[:, pl.ds((3 * j + 2) * hd, hd)] = dv_acc[j].astype(BF16)
                for kj in range(nb):
                    col = jnp.broadcast_to(col_acc[j, kj][0:1, :], (LANES, tk)).T
                    old = dc_ref[pl.ds(kj * tk, tk), :]
                    dc_ref[pl.ds(kj * tk, tk), :] = jnp.where(lane == ATT_HP * g + j, old - col, old)

    nb = cum_row.shape[0]
    wide = ATT_HP * hd
    return pl.pallas_call(
        body, name="fox_bwd", grid=(n_heads // ATT_HP, nq),
        in_specs=[ANY] + _qkv_specs(hb0, s) + [
            pl.BlockSpec((tq, wide), lambda g, i: (i, g)), pl.BlockSpec((tq, wide), lambda g, i: (i, g)),
            pl.BlockSpec((tq, LANES), lambda g, i: (i, 0)), pl.BlockSpec((tq, LANES), lambda g, i: (i, 0)),
            pl.BlockSpec((nb, 8, tk), lambda g, i: (0, 0, 0)), ANY],
        out_specs=[pl.BlockSpec((s, 3 * wide), lambda g, i: (0, hb0 // ATT_HP + g)),
                   pl.BlockSpec((s, LANES), lambda g, i: (0, 0))],
        out_shape=[_sds(dqkv.shape, BF16), _sds((s, LANES), F32)],
        scratch_shapes=[pltpu.VMEM((ATT_HP, s, hd), F32), pltpu.VMEM((ATT_HP, s, hd), F32),
                        pltpu.VMEM((ATT_HP, s // tk, 8, tk), F32)],
        input_output_aliases={0: 0},
        compiler_params=_params(("arbitrary", "arbitrary")),
    )(dqkv, *[qkv] * (3 * ATT_HP), do, o, lse, cum_col, cum_row, dep)


def _branch_merge(o_sb, o_fx, w_sb, w_fx, gf, dep, tm=1024):
    s = o_sb.shape[0]
    cs = w_sb.shape[2]
    tm = _tile(s, tm)

    def body(osb_ref, ofx_ref, wsb_ref, wfx_ref, g_ref, dep_ref, merged_ref, mt_ref, asb_ref, afx_ref):
        del dep_ref
        a_sb = _dot(osb_ref[...], wsb_ref[...])
        a_fx = _dot(ofx_ref[...], wfx_ref[...])
        g = g_ref[...]
        merged = (_sigmoid(g[:, :cs]) * a_sb + _sigmoid(g[:, cs:]) * a_fx).astype(BF16)
        merged_ref[...] = merged
        mt_ref[...] = merged.T
        asb_ref[...] = a_sb.astype(BF16)
        afx_ref[...] = a_fx.astype(BF16)

    blk = pl.BlockSpec((tm, cs), lambda i, j: (i, j))
    out = _sds((s, N_DEV * cs), BF16)
    return pl.pallas_call(
        body, name="branch_merge", grid=(s // tm, N_DEV),
        in_specs=[pl.BlockSpec((tm, o_sb.shape[1]), lambda i, j: (i, 0)),
                  pl.BlockSpec((tm, o_fx.shape[1]), lambda i, j: (i, 0)),
                  pl.BlockSpec((None,) + w_sb.shape[1:], lambda i, j: (j, 0, 0)),
                  pl.BlockSpec((None,) + w_fx.shape[1:], lambda i, j: (j, 0, 0)),
                  pl.BlockSpec((tm, 2 * cs), lambda i, j: (i, j)), ANY],
        out_specs=[blk, pl.BlockSpec((cs, tm), lambda i, j: (j, i)), blk, blk],
        out_shape=[out, _sds((N_DEV * cs, s), BF16), out, out],
        compiler_params=_params(("parallel", "arbitrary")),
    )(o_sb, o_fx, w_sb, w_fx, gf, dep)


def _merge_bwd(dmix, w_out, gf, a_sb, a_fx, tm=1024, tk=2048, dep=None):
    s, d = dmix.shape
    cs = d // N_DEV
    tm, tk = _tile(s, tm), _tile(d, tk)

    def epilogue(acc, ex, outs):
        g, a_sb, a_fx = ex[0][...], ex[1][...].astype(F32), ex[2][...].astype(F32)
        s_sb, s_fx = _sigmoid(g[:, :cs]), _sigmoid(g[:, cs:])
        outs[0][...] = (acc * s_sb).astype(BF16)
        outs[1][...] = (acc * s_fx).astype(BF16)
        outs[2][...] = jnp.concatenate([acc * a_sb * s_sb * (1.0 - s_sb), acc * a_fx * s_fx * (1.0 - s_fx)],
                                       axis=1).astype(BF16)

    blk = pl.BlockSpec((tm, cs), lambda i, j, k: (i, j))
    wide = pl.BlockSpec((tm, 2 * cs), lambda i, j, k: (i, j))
    return _matmul(
        "merge_bwd", "nt",
        [(dmix, pl.BlockSpec((tm, tk), lambda i, j, k: (i, k)), w_out, pl.BlockSpec((cs, tk), lambda i, j, k: (j, k)))],
        (s // tm, N_DEV, d // tk), (tm, cs),
        [_sds((s, d), BF16), _sds((s, d), BF16), _sds(gf.shape, BF16)], [blk, blk, wide],
        extras=[(gf, wide), (a_sb, blk), (a_fx, blk)], epilogue=epilogue, dep=dep)


def _ffn_up(u2, w_gate, w_up, dep, tm=1024):
    s, d = u2.shape
    fs = w_gate.shape[2]
    tm = _tile(s, tm)

    def body(u_ref, wg_ref, wu_ref, dep_ref, gate_ref, up_ref, act_ref, actt_ref):
        del dep_ref
        u = u_ref[...]
        gate = _dot(u, wg_ref[...])
        up = _dot(u, wu_ref[...])
        gate_ref[...] = gate
        up_ref[...] = up
        act = (gate * _sigmoid(gate) * up).astype(BF16)
        act_ref[...] = act
        actt_ref[...] = act.T

    w_spec = pl.BlockSpec((None, d, fs), lambda i, j: (j, 0, 0))
    o_spec = pl.BlockSpec((None, tm, fs), lambda i, j: (j, i, 0))
    return pl.pallas_call(
        body, name="ffn_up", grid=(s // tm, N_DEV),
        in_specs=[pl.BlockSpec((tm, d), lambda i, j: (i, 0)), w_spec, w_spec, ANY],
        out_specs=[o_spec, o_spec, o_spec, pl.BlockSpec((None, fs, tm), lambda i, j: (j, 0, i))],
        out_shape=[_sds((N_DEV, s, fs), F32), _sds((N_DEV, s, fs), F32), _sds((N_DEV, s, fs), BF16),
                   _sds((N_DEV, fs, s), BF16)],
        compiler_params=_params(("parallel", "arbitrary")),
    )(u2, w_gate, w_up, dep)


def _ffn_down_bwd(dff, w_down, gate, up, tm=1024):
    s, d = dff.shape
    fs = w_down.shape[1]
    tm = _tile(s, tm)

    def body(dff_ref, wd_ref, gate_ref, up_ref, dgate_ref, dup_ref):
        dact = _dot(dff_ref[...], wd_ref[...], "nt")
        gate = gate_ref[...]
        sg = _sigmoid(gate)
        dup_ref[...] = (dact * gate * sg).astype(BF16)
        dgate_ref[...] = (dact * up_ref[...] * sg * (1.0 + gate * (1.0 - sg))).astype(BF16)

    a_spec = pl.BlockSpec((None, tm, fs), lambda i, j: (j, i, 0))
    return pl.pallas_call(
        body, name="ffn_down_bwd", grid=(s // tm, N_DEV),
        in_specs=[pl.BlockSpec((tm, d), lambda i, j: (i, 0)), pl.BlockSpec((None, fs, d), lambda i, j: (j, 0, 0)),
                  a_spec, a_spec],
        out_specs=[a_spec, a_spec],
        out_shape=[_sds((N_DEV, s, fs), BF16), _sds((N_DEV, s, fs), BF16)],
        compiler_params=_params(("parallel", "arbitrary")),
    )(dff, w_down, gate, up)


def _mesh_place():
    x, y, c = lax.axis_index("x"), lax.axis_index("y"), lax.axis_index("c")
    peers = []
    for d in range(1, N_DEV):
        px = 1 - x if d & 4 else x
        py = 1 - y if d & 2 else y
        pc = 1 - c if d & 1 else c
        peers.append((d, (px, py, pc), 4 * px + 2 * py + pc))
    return 4 * x + 2 * y + c, peers


def _flat_me():
    return 4 * lax.axis_index("x") + 2 * lax.axis_index("y") + lax.axis_index("c")


def _in_hbm(a):
    return pltpu.with_memory_space_constraint(a, pltpu.HBM)


def _pair_plan():
    x, y, c = lax.axis_index("x"), lax.axis_index("y"), lax.axis_index("c")
    return [(2 * q + (1 - c), q, q, (x, y, 1 - c)) for q in range(4)]


def _chip_plan():
    x, y, c = lax.axis_index("x"), lax.axis_index("y"), lax.axis_index("c")
    plan = []
    for fx, fy in ((1, 0), (0, 1), (1, 1)):
        cx, cy = (1 - x if fx else x), (1 - y if fy else y)
        plan.append((2 * cx + cy, 2 * x + y, 2 * cx + cy, (cx, cy, c)))
    return plan


def _split_start(name, srcs, lands, plan, k):
    n = len(srcs)

    def body(*refs):
        ins, lnd = refs[:n], refs[n:2 * n]
        send, recv, token = refs[2 * n], refs[2 * n + 1], refs[-1]
        copies = plan()
        for a in range(n):
            for t, (src, dst, _, dev) in enumerate(copies):
                pltpu.make_async_remote_copy(src_ref=ins[a].at[src], dst_ref=lnd[a].at[dst], send_sem=send.at[k * a + t],
                                             recv_sem=recv.at[k * a + t], device_id=dev, device_id_type=MESH).start()
        token[...] = jnp.zeros_like(token)

    res = pl.pallas_call(
        body, name=name,
        out_shape=[pltpu.SemaphoreType.DMA((n * k,)), pltpu.SemaphoreType.DMA((n * k,))]
        + [pltpu.HBM(a.shape, a.dtype) for a in list(srcs) + list(lands)] + [_sds((8, LANES), F32)],
        in_specs=[HBM] * (2 * n), out_specs=[SEM, SEM] + [HBM] * (2 * n) + [pl.BlockSpec(memory_space=pltpu.VMEM)],
        input_output_aliases={i: 2 + i for i in range(2 * n)},
        compiler_params=pltpu.CompilerParams(has_side_effects=EFFECT),
    )(*[_in_hbm(a) for a in srcs], *[_in_hbm(a) for a in lands])
    return res[0], res[1], res[2:2 + n], res[2 + n:2 + 2 * n], res[-1]


def _split_wait(name, send, recv, srcs, lands, plan, k, after):
    n = len(srcs)

    def body(*refs):
        ins, lnd = refs[:n], refs[n:2 * n]
        send_sem, recv_sem = refs[2 * n], refs[2 * n + 1]
        copies = plan()
        for a in range(n):
            for t, (src, _, dst, dev) in enumerate(copies):
                cp = pltpu.make_async_remote_copy(src_ref=ins[a].at[src], dst_ref=lnd[a].at[dst], send_sem=send_sem.at[k * a + t],
                                                  recv_sem=recv_sem.at[k * a + t], device_id=dev, device_id_type=MESH)
                cp.wait_send()
                cp.wait_recv()

    res = pl.pallas_call(
        body, name=name,
        out_shape=[pltpu.HBM(a.shape, a.dtype) for a in list(srcs) + list(lands)],
        in_specs=[HBM] * (2 * n) + [SEM, SEM] + [ANY] * len(after), out_specs=[HBM] * (2 * n),
        input_output_aliases={i: i for i in range(2 * n)},
        compiler_params=pltpu.CompilerParams(has_side_effects=EFFECT),
    )(*srcs, *lands, send, recv, *after)
    return res[:n], res[n:]


def _pair_add(name, parts, land):
    _, r, cols = parts.shape
    tr = max(16, min(r, ((1 << 20) // (2 * cols)) // 16 * 16))
    while r % tr:
        tr -= 16

    def body(c_ref, p_ref, l_ref, o_ref):
        del c_ref
        o_ref[...] = (p_ref[...].astype(F32) + l_ref[...].astype(F32)).astype(BF16)

    blk = pl.BlockSpec((None, tr, cols), lambda q, i, c_ref: (q, i, 0))
    return pl.pallas_call(
        body, name=name,
        grid_spec=pltpu.PrefetchScalarGridSpec(
            num_scalar_prefetch=1, grid=(4, r // tr),
            in_specs=[pl.BlockSpec((None, tr, cols), lambda q, i, c_ref: (2 * q + c_ref[0], i, 0)), blk], out_specs=blk),
        out_shape=_sds((4, r, cols), BF16),
        compiler_params=_params(("parallel", "parallel")),
    )(jnp.reshape(lax.axis_index("c"), (1,)).astype(jnp.int32), parts, land)


def _scatter_pairs(tag, parts):
    lands = [lax.empty((4,) + a.shape[1:], a.dtype) for a in parts]
    return _split_start("pair_" + tag, parts, lands, _pair_plan, 4)


def _scatter_chips(tag, started, after):
    send, recv, parts, lands, _ = started
    parts, lands = _split_wait("pair_" + tag + "_wait", send, recv, parts, lands, _pair_plan, 4, [after])
    sums = [_pair_add("pair_" + tag + "_add%d" % a, p, l) for a, (p, l) in enumerate(zip(parts, lands))]
    chip = 2 * lax.axis_index("x") + lax.axis_index("y")
    final = [lax.dynamic_update_slice_in_dim(lax.empty(v.shape, v.dtype), lax.dynamic_slice_in_dim(v, chip, 1, 0), chip, 0)
             for v in sums]
    return _split_start("chips_" + tag, sums, final, _chip_plan, 3)


def _scatter_end(tag, started, after):
    send, recv, sums, final, _ = started
    return _split_wait("chips_" + tag + "_wait", send, recv, sums, final, _chip_plan, 3, after)[1]


def _gather_targets():
    x, y, c = lax.axis_index("x"), lax.axis_index("y"), lax.axis_index("c")
    chips = [(x, y), (1 - x, y), (x, 1 - y), (1 - x, 1 - y)]
    same = [((cx, cy, c), 4 * cx + 2 * cy + c) for cx, cy in chips]
    other = [((cx, cy, 1 - c), 4 * cx + 2 * cy + 1 - c) for cx, cy in chips]
    return same[0][1], [other[0]] + same[1:], [flat for _, flat in other[1:]], other[0][0]


def _gather_start(shards):
    n = len(shards)
    me = _flat_me()
    lands = [lax.dynamic_update_slice_in_dim(lax.empty((N_DEV,) + a.shape, a.dtype), a[None], me, 0) for a in shards]

    def body(*refs):
        lnd, send, recv, token = refs[:n], refs[n], refs[n + 1], refs[-1]
        mine, targets, _, _ = _gather_targets()
        for a in range(n):
            for t, (dev, _) in enumerate(targets):
                pltpu.make_async_remote_copy(src_ref=lnd[a].at[mine], dst_ref=lnd[a].at[mine], send_sem=send.at[4 * a + t],
                                             recv_sem=recv.at[4 * a + t], device_id=dev, device_id_type=MESH).start()
        token[...] = jnp.zeros_like(token)

    res = pl.pallas_call(
        body, name="gather_start",
        out_shape=[pltpu.SemaphoreType.DMA((4 * n,)), pltpu.SemaphoreType.DMA((4 * n,))]
        + [pltpu.HBM(a.shape, a.dtype) for a in lands] + [_sds((8, LANES), F32)],
        in_specs=[HBM] * n, out_specs=[SEM, SEM] + [HBM] * n + [pl.BlockSpec(memory_space=pltpu.VMEM)],
        input_output_aliases={i: 2 + i for i in range(n)},
        compiler_params=pltpu.CompilerParams(has_side_effects=EFFECT),
    )(*[_in_hbm(a) for a in lands])
    return res[0], res[1], list(res[2:2 + n]), res[-1]


def _gather_forward(name, lands, first, send, recv, after):
    n = len(lands)

    def body(*refs):
        lnd, send_sem, recv_sem = refs[:n], refs[n], refs[n + 1]
        send2, recv2, token = refs[-3], refs[-2], refs[-1]
        mine, targets, _, sibling = _gather_targets()
        for a in range(n):
            for t, (dev, flat) in enumerate(targets):
                cp = pltpu.make_async_remote_copy(src_ref=lnd[a].at[mine], dst_ref=lnd[a].at[flat],
                                                  send_sem=send_sem.at[4 * (first + a) + t],
                                                  recv_sem=recv_sem.at[4 * (first + a) + t], device_id=dev, device_id_type=MESH)
                cp.wait_send()
                if t:
                    cp.wait_recv()
                    pltpu.make_async_remote_copy(src_ref=lnd[a].at[flat], dst_ref=lnd[a].at[flat], send_sem=send2.at[3 * a + t - 1],
                                                 recv_sem=recv2.at[3 * a + t - 1], device_id=sibling, device_id_type=MESH).start()
        token[...] = jnp.zeros_like(token)

    res = pl.pallas_call(
        body, name=name,
        out_shape=[pltpu.HBM(a.shape, a.dtype) for a in lands]
        + [pltpu.SemaphoreType.DMA((3 * n,)), pltpu.SemaphoreType.DMA((3 * n,)), _sds((8, LANES), F32)],
        in_specs=[HBM] * n + [SEM, SEM] + [ANY] * len(after),
        out_specs=[HBM] * n + [SEM, SEM, pl.BlockSpec(memory_space=pltpu.VMEM)],
        input_output_aliases={i: i for i in range(n)},
        compiler_params=pltpu.CompilerParams(has_side_effects=EFFECT),
    )(*lands, send, recv, *after)
    return list(res[:n]), res[n], res[n + 1], res[-1]


def _gather_wait(name, lands, first, recv, send2, recv2, after):
    n = len(lands)

    def body(*refs):
        lnd, recv_sem, send2_sem, recv2_sem = refs[:n], refs[n], refs[n + 1], refs[n + 2]
        mine, targets, passed, sibling = _gather_targets()
        for a in range(n):
            dev, flat = targets[0]
            pltpu.make_async_remote_copy(src_ref=lnd[a].at[mine], dst_ref=lnd[a].at[flat], send_sem=send2_sem.at[3 * a],
                                         recv_sem=recv_sem.at[4 * (first + a)], device_id=dev, device_id_type=MESH).wait_recv()
            for t in range(3):
                cp = pltpu.make_async_remote_copy(src_ref=lnd[a].at[targets[t + 1][1]], dst_ref=lnd[a].at[passed[t]],
                                                  send_sem=send2_sem.at[3 * a + t], recv_sem=recv2_sem.at[3 * a + t],
                                                  device_id=sibling, device_id_type=MESH)
                cp.wait_send()
                cp.wait_recv()

    res = pl.pallas_call(
        body, name=name, out_shape=[pltpu.HBM(a.shape, a.dtype) for a in lands],
        in_specs=[HBM] * n + [SEM, SEM, SEM, ANY], out_specs=[HBM] * n,
        input_output_aliases={i: i for i in range(n)},
        compiler_params=pltpu.CompilerParams(has_side_effects=EFFECT),
    )(*lands, recv, send2, recv2, after)
    return list(res)


def _adamw_decay(w, m, v):
    return ADAM_WD * w, ADAM_B1 * m, ADAM_B2 * v


def _adamw_finish(g, wd_w, m1, v1):
    m = m1 + (1.0 - ADAM_B1) * g
    v = v1 + (1.0 - ADAM_B2) * (g * g)
    m_hat = m / (1.0 - ADAM_B1 ** ADAM_STEP)
    v_hat = v / (1.0 - ADAM_B2 ** ADAM_STEP)
    delta = -ADAM_LR * (m_hat / (jnp.sqrt(v_hat) + ADAM_EPS) + wd_w)
    return delta, m, v


def _adamw(g, w, m, v):
    return _adamw_finish(g, *_adamw_decay(w, m, v))


def _update_prep(name, w, m, v, dep, w_done=False, block_bytes=1 << 20):
    _, r, c = m.shape
    tr = max(8, min(r, (block_bytes // (4 * c)) // 8 * 8))
    while r % tr:
        tr -= 8
    blk = pl.BlockSpec((None, tr, c), lambda i: (0, i, 0))
    if w_done:
        def body(m_ref, v_ref, dep_ref, om_ref, ov_ref):
            del dep_ref
            om_ref[...] = ADAM_B1 * m_ref[...]
            ov_ref[...] = ADAM_B2 * v_ref[...]

        m1, v1 = pl.pallas_call(
            body, name=name, grid=(r // tr,), in_specs=[blk] * 2 + [ANY], out_specs=[blk] * 2,
            out_shape=[_sds((1, r, c), F32)] * 2, compiler_params=_params(("parallel",)),
        )(m, v, dep)
        return w, m1, v1

    def body(w_ref, m_ref, v_ref, dep_ref, ow_ref, om_ref, ov_ref):
        del dep_ref
        ow_ref[...], om_ref[...], ov_ref[...] = _adamw_decay(w_ref[...], m_ref[...], v_ref[...])

    return pl.pallas_call(
        body, name=name, grid=(r // tr,), in_specs=[blk] * 3 + [ANY], out_specs=[blk] * 3,
        out_shape=[_sds((1, r, c), F32)] * 3, compiler_params=_params(("parallel",)),
    )(w, m, v, dep)


def _update(name, parts, w, m, v, layout=None, decayed=False, transposed_out=False, block_bytes=1 << 20):
    _, r, c = w.shape
    n_slots, _, cp = parts.shape
    tr = max(8, min(r, (block_bytes // (4 * cp)) // 8 * 8))
    if transposed_out:
        tr = _tile(r, 256)
    while r % tr:
        tr -= 8

    def body(p_ref, w_ref, m_ref, v_ref, g_ref, d_ref, nm_ref, nv_ref, *scratch):
        g = p_ref[0].astype(F32)
        for p in range(1, n_slots):
            g = g + p_ref[p].astype(F32)
        if layout is not None:
            s1, s2, lg = layout.my_shifts()
            lane = lax.broadcasted_iota(jnp.int32, g.shape, 1)
            scratch[0][...] = jnp.where(lane < lg, pltpu.roll(g, cp - s1, 1), pltpu.roll(g, cp - s2, 1))
            g = scratch[0][:, 0:c]
        step = _adamw_finish if decayed else _adamw
        results = (g,) + step(g, w_ref[...], m_ref[...], v_ref[...])
        if ragged:
            scratch[-2][...] = jnp.zeros_like(scratch[-2])
        for ref, val in zip((g_ref, d_ref, nm_ref, nv_ref), results):
            if not transposed_out:
                ref[...] = val
            elif not ragged:
                ref[...] = val.T
            else:
                wide, tall = scratch[-2], scratch[-1]
                wide[:, 0:c] = val
                tall[...] = wide[...].T
                ref[...] = tall[0:c, :]

    ragged = transposed_out and c % 8 != 0
    c_wide = -(-c // LANES) * LANES
    blk = pl.BlockSpec((None, tr, c), lambda i: (0, i, 0))
    out_blk = pl.BlockSpec((None, c, tr), lambda i: (0, 0, i)) if transposed_out else blk
    scratch_shapes = [] if layout is None else [pltpu.VMEM((tr, cp), F32)]
    if ragged:
        scratch_shapes += [pltpu.VMEM((tr, c_wide), F32), pltpu.VMEM((c_wide, tr), F32)]
    res = pl.pallas_call(
        body, name=name, grid=(r // tr,),
        in_specs=[pl.BlockSpec((n_slots, tr, cp), lambda i: (0, i, 0)), blk, blk, blk],
        out_specs=[out_blk] * 4, out_shape=[_sds((1, c, r) if transposed_out else (1, r, c), F32)] * 4,
        scratch_shapes=scratch_shapes,
        compiler_params=_params(("parallel",)),
    )(parts, w, m, v)
    return [jnp.transpose(o, (0, 2, 1)) for o in res] if transposed_out else res


def _small_update(part, w, m, v):
    n = part.shape[1]

    def body(p_ref, w_ref, m_ref, v_ref, g_ref, d_ref, nm_ref, nv_ref, buf, send, recv):
        me, peers = _mesh_place()
        buf[me] = p_ref[...]
        sent = []
        for d, dev, flat in peers:
            cp = pltpu.make_async_remote_copy(src_ref=p_ref, dst_ref=buf.at[me], send_sem=send.at[d],
                                              recv_sem=recv.at[d], device_id=dev, device_id_type=MESH)
            cp.start()
            sent.append(cp)
        for d, dev, flat in peers:
            pltpu.make_async_remote_copy(src_ref=p_ref, dst_ref=buf.at[flat], send_sem=send.at[d],
                                         recv_sem=recv.at[d], device_id=dev, device_id_type=MESH).wait_recv()
        for cp in sent:
            cp.wait_send()
        g = buf[0]
        for p in range(1, N_DEV):
            g = g + buf[p]
        g_ref[...] = g
        d_ref[...], nm_ref[...], nv_ref[...] = _adamw(g, w_ref[...], m_ref[...], v_ref[...])

    vm = pl.BlockSpec(memory_space=pltpu.VMEM)
    return pl.pallas_call(
        body, name="small_update", in_specs=[vm] * 4, out_specs=[vm] * 4, out_shape=[_sds((1, n), F32)] * 4,
        scratch_shapes=[pltpu.VMEM((N_DEV, 1, n), F32), pltpu.SemaphoreType.DMA((N_DEV,)),
                        pltpu.SemaphoreType.DMA((N_DEV,))],
    )(part, w, m, v)


class _WInLayout:
    def __init__(self, n8, n_f, d_sb, d_fox, d):
        assert n8 % LANES == 1 and n_f < LANES and d % (N_DEV * LANES) == 0
        self.n8, self.n_f, self.d = n8, n_f, d
        self.sp = n8 // LANES
        self.wp = (n8 + 2 * LANES - 2) // LANES * LANES
        self.n_qkv = 3 * (d_sb + d_fox)
        nq, dt, tc = self.n_qkv // LANES, d // LANES, d // N_DEV // LANES
        h_sb, h_fox = d_sb // HEAD_DIM, d_fox // HEAD_DIM
        self.sources = {}
        self.part_tile = {}
        for p in range(N_DEV):
            lg = min(max(self.n_qkv + n_f - n8 * p, 0), n8)
            s1, s2 = p, p + LANES - n_f
            spans = []
            if lg > 0:
                spans.append(("a", self.sp * p, s1 // LANES, (lg + s1 - 1) // LANES))
            if lg < n8:
                spans.append(("g", self.sp * p - 1 - nq, (lg + s2) // LANES, (n8 - 1 + s2) // LANES))
            for kind, base, first, last in spans:
                for i in range(first, last + 1):
                    assert (p, i) not in self.part_tile
                    self.part_tile[(p, i)] = (kind, base + i)
                    self.sources.setdefault((kind, base + i), []).append((p, i))
        self.cat_tiles = [("a", r * h_sb + h) for h in range(h_sb) for r in range(3)]
        self.cat_tiles += [("a", 3 * h_sb + r * h_fox + h) for h in range(h_fox) for r in range(3)]
        self.cat_tiles += [("g", which * dt + j * tc + half) for j in range(N_DEV) for which in (0, 1) for half in range(tc)]
        self.cat_tiles += [("a", nq)] + [None] * (F_PAD // LANES - 1)
        self.cat_index = {key: c for c, key in enumerate(self.cat_tiles) if key is not None}

    def my_shifts(self):
        me = _flat_me()
        return me, me + LANES - self.n_f, jnp.clip(self.n_qkv + self.n_f - self.n8 * me, 0, self.n8)


def _lane_tile(i):
    return pl.ds(i * LANES, LANES)


def _w_in_shift(w_in, lay, tr=256):
    _, d, n8 = w_in.shape
    kd = d // LANES
    kt = tr // LANES
    by_col = jnp.transpose(w_in, (0, 2, 1)).reshape(n8 * kd, LANES)

    def body(w_ref, o_ref, wd_ref, buf):
        k0 = kt * pl.program_id(0)
        buf[...] = jnp.zeros_like(buf)
        for j in range(n8 // LANES):
            for kk in range(kt):
                piece = w_ref[pl.ds(j * LANES * kd + k0 + kk, LANES, stride=kd), :]
                buf[kk * LANES:(kk + 1) * LANES, j * LANES:(j + 1) * LANES] = piece.T
        first = lax.broadcasted_iota(jnp.int32, (8, LANES), 0) == 0
        for kk in range(kt):
            row = w_ref[pl.ds((n8 - 1) * kd + k0 + kk, 1), :]
            buf[kk * LANES:(kk + 1) * LANES, n8 - 1:n8 + 7] = jnp.where(first, jnp.broadcast_to(row, (8, LANES)), 0.0).T
        wd_ref[...] = ADAM_WD * buf[:, 0:n8]
        v = buf[...]
        s1, s2, lg = lay.my_shifts()
        pos = lax.broadcasted_iota(jnp.int32, v.shape, 1)
        o_ref[...] = jnp.where(pos < lg + s1, pltpu.roll(v, s1, 1),
                               jnp.where(pos >= lg + s2, pltpu.roll(v, s2, 1), 0.0)).astype(BF16)

    return pl.pallas_call(
        body, name="w_in_shift", grid=(d // tr,),
        in_specs=[pl.BlockSpec((n8 * kd, LANES), lambda i: (0, 0))],
        out_specs=[pl.BlockSpec((tr, lay.wp), lambda i: (i, 0)), pl.BlockSpec((None, tr, n8), lambda i: (0, i, 0))],
        out_shape=[_sds((d, lay.wp), BF16), _sds((1, d, n8), F32)],
        scratch_shapes=[pltpu.VMEM((tr, lay.wp), F32)],
        compiler_params=_params(("arbitrary",)),
    )(by_col)


def _w_in_build(g_in, lay, tr=256):
    d = g_in.shape[1]
    width = len(lay.cat_tiles) * LANES

    def body(g_ref, o_ref):
        for c, key in enumerate(lay.cat_tiles):
            if key is None:
                o_ref[:, _lane_tile(c)] = jnp.zeros((tr, LANES), BF16)
                continue
            (p, i), *more = lay.sources[key]
            val = g_ref[p, :, _lane_tile(i)]
            for p2, i2 in more:
                val = val + g_ref[p2, :, _lane_tile(i2)]
            o_ref[:, _lane_tile(c)] = val

    return pl.pallas_call(
        body, name="w_in_build", grid=(d // tr,),
        in_specs=[pl.BlockSpec((N_DEV, tr, lay.wp), lambda i: (0, i, 0))],
        out_specs=pl.BlockSpec((tr, width), lambda i: (i, 0)), out_shape=_sds((d, width), BF16),
        compiler_params=_params(("parallel",)),
    )(g_in)


def _w_in_grad_parts(dwq, dwgf, lay, tr=256):
    d = dwq.shape[0]
    nq = lay.n_qkv // LANES

    def body(q_ref, g_ref, o_ref):
        for p in range(N_DEV):
            for i in range(lay.wp // LANES):
                key = lay.part_tile.get((p, i))
                if key is None:
                    o_ref[p, :, _lane_tile(i)] = jnp.zeros((tr, LANES), BF16)
                    continue
                c = lay.cat_index[key]
                o_ref[p, :, _lane_tile(i)] = q_ref[:, _lane_tile(c)] if c < nq else g_ref[:, _lane_tile(c - nq)]

    return pl.pallas_call(
        body, name="w_in_grad_parts", grid=(d // tr,),
        in_specs=[pl.BlockSpec((tr, dwq.shape[1]), lambda i: (i, 0)), pl.BlockSpec((tr, dwgf.shape[1]), lambda i: (i, 0))],
        out_specs=pl.BlockSpec((N_DEV, tr, lay.wp), lambda i: (0, i, 0)), out_shape=_sds((N_DEV, d, lay.wp), BF16),
        compiler_params=_params(("parallel",)),
    )(dwq, dwgf)


def kernel(x, norm_mix_pre, norm_mix_post, w_in, b_forget, w_branch_sb, w_branch_fox, w_out, norm_ffn_pre, norm_ffn_post, w_ffn_gate, w_ffn_up, w_ffn_down, loss_target, m_norm_mix_pre, m_norm_mix_post, m_w_in, m_b_forget, m_w_branch_sb, m_w_branch_fox, m_w_out, m_norm_ffn_pre, m_norm_ffn_post, m_w_ffn_gate, m_w_ffn_up, m_w_ffn_down, v_norm_mix_pre, v_norm_mix_post, v_w_in, v_b_forget, v_w_branch_sb, v_w_branch_fox, v_w_out, v_norm_ffn_pre, v_norm_ffn_post, v_w_ffn_gate, v_w_ffn_up, v_w_ffn_down):
    xs, target = x[0], loss_target[0]
    s, d = xs.shape
    d_sb, d_fox = w_branch_sb.shape[1], w_branch_fox.shape[1]
    h_sb, h_fox = d_sb // HEAD_DIM, d_fox // HEAD_DIM
    n_f = b_forget.shape[1]
    fs = w_ffn_gate.shape[2]
    cs = d // N_DEV
    n_qkv = 3 * (d_sb + d_fox)
    n_gf = 2 * d + F_PAD
    f_blk = 2 * d // LANES
    big = (w_in, w_branch_sb, w_branch_fox, w_out, w_ffn_gate, w_ffn_up, w_ffn_down)
    big_m = (m_w_in, m_w_branch_sb, m_w_branch_fox, m_w_out, m_w_ffn_gate, m_w_ffn_up, m_w_ffn_down)
    big_v = (v_w_in, v_w_branch_sb, v_w_branch_fox, v_w_out, v_w_ffn_gate, v_w_ffn_up, v_w_ffn_down)

    lay = _WInLayout(w_in.shape[2], n_f, d_sb, d_fox, d)
    w_in_shifted, wd_w_in = _w_in_shift(w_in, lay)
    send1, recv1, lands, token = _gather_start([w_in_shifted] + [w[0].astype(BF16) for w in big[1:]])
    b_pad = jnp.pad(b_forget, ((0, 0), (0, LANES - n_f)))

    started = token[0, 0]
    u, u_t = _pre_norm(xs, norm_mix_pre, dep=token)
    weights = dict(zip(("w_in", "w_branch_sb", "w_branch_fox", "w_out", "w_ffn_gate", "w_ffn_up", "w_ffn_down"),
                       zip(big, big_m, big_v)))
    decayed = {nm: _update_prep("decay_" + nm, *[t + started for t in weights[nm]], u)
               for nm in ("w_ffn_gate", "w_ffn_up")}
    decayed["w_in"] = _update_prep("decay_w_in", wd_w_in, m_w_in + started, v_w_in + started, u, w_done=True)
    l_in, send2, recv2, token = _gather_forward("gather_in_forward", lands[0:1], 0, send1, recv1,
                                                [u] + [t[2] for t in decayed.values()])
    (g_in,) = _gather_wait("gather_in_wait", l_in, 0, recv1, send2, recv2, token)
    w_cat = _w_in_build(g_in, lay)
    qkv = _mm_plain("proj_qkv", "nn", u, w_cat, BF16, n=n_qkv)
    gf = _mm_plain("proj_gates", "nn", u, w_cat, F32, n_off=n_qkv, n=n_gf)
    cum_col, cum_row = _forget_fwd(gf, b_pad, f_blk)
    o_sb, o_sb_t, tot = _sb_fwd(qkv, h_sb)
    l_mid, send2, recv2, token = _gather_forward("gather_mid_forward", lands[1:4], 1, send1, recv1, [o_sb])
    o_fx, o_fx_t, o_fx32, lse = _fox_fwd(qkv, cum_col, cum_row, h_fox, h_sb, token)
    g_sb, g_fx, g_out = _gather_wait("gather_mid_wait", l_mid, 1, recv1, send2, recv2, o_fx)
    w_out_full = g_out.reshape(d, d)
    merged, merged_t, a_sb, a_fx = _branch_merge(o_sb, o_fx, g_sb, g_fx, gf, o_fx)
    l_ffn, send2, recv2, token = _gather_forward("gather_ffn_forward", lands[4:6], 4, send1, recv1, [merged])
    mix = _mm_plain("out_proj", "nn", merged, w_out_full, F32, dep=token)
    h1, u2, u2_t = _mid_norms(xs, mix, norm_mix_post, norm_ffn_pre)
    g_gate, g_up = _gather_wait("gather_ffn_wait", l_ffn, 4, recv1, send2, recv2, u2)
    l_down, send2, recv2, token = _gather_forward("gather_down_forward", lands[6:7], 6, send1, recv1, [u2])
    gate, up, act, act_t = _ffn_up(u2, g_gate, g_up, token)
    (g_down,) = _gather_wait("gather_down_wait", l_down, 6, recv1, send2, recv2, act)
    tm, tn = _tile(s, 1024), _tile(d, 1024)
    ff = _matmul("ffn_down", "nn",
                 [(act, pl.BlockSpec((None, tm, fs), lambda i, j, k: (k, i, 0)),
                   g_down, pl.BlockSpec((None, fs, tn), lambda i, j, k: (k, 0, j)))],
                 (s // tm, d // tn, N_DEV), (tm, tn), _sds((s, d), F32), pl.BlockSpec((tm, tn), lambda i, j, k: (i, j)))
    loss_part, dy, dff, dg_ffn_post = _loss_head(h1, ff, target, norm_ffn_post)

    dgate, dup = _ffn_down_bwd(dff, g_down, gate, up)
    dw_down = _matmul("dw_down", "nn",
                      [(act_t, pl.BlockSpec((None, fs, s), lambda j, n, k: (j, 0, 0)),
                        dff, pl.BlockSpec((s, tn), lambda j, n, k: (0, n)))],
                      (N_DEV, d // tn, 1), (fs, tn), _sds((N_DEV, fs, d), BF16),
                      pl.BlockSpec((None, fs, tn), lambda j, n, k: (j, 0, n)))

    def dw_up(name, dact):
        return _matmul(name, "nn",
                       [(u2_t, pl.BlockSpec((tn, s), lambda j, i, k: (i, 0)),
                         dact, pl.BlockSpec((None, s, fs), lambda j, i, k: (j, 0, 0)))],
                       (N_DEV, d // tn, 1), (tn, fs), _sds((N_DEV, d, fs), BF16),
                       pl.BlockSpec((None, tn, fs), lambda j, i, k: (j, i, 0)))

    dw_gate, dw_upw = dw_up("dw_gate", dgate), dw_up("dw_up", dup)
    rs_ffn = _scatter_pairs("ffn", [dw_gate, dw_upw, dw_down])
    a_spec = pl.BlockSpec((None, tm, fs), lambda i, j, k: (k, i, 0))
    b_spec = pl.BlockSpec((None, tn, fs), lambda i, j, k: (k, j, 0))
    du2 = _matmul("du2", "nt", [(dgate, a_spec, g_gate, b_spec), (dup, a_spec, g_up, b_spec)],
                  (s // tm, d // tn, N_DEV), (tm, tn), _sds((s, d), F32), pl.BlockSpec((tm, tn), lambda i, j, k: (i, j)),
                  dep=rs_ffn[4])
    rs_ffn = _scatter_chips("ffn", rs_ffn, du2)
    dh1, dmix, dg_ffn_pre, dg_mix_post = _mid_norms_bwd(dy, du2, h1, mix, norm_ffn_pre, norm_mix_post)

    da_sb, da_fx, dgf = _merge_bwd(dmix, w_out_full, gf, a_sb, a_fx, dep=rs_ffn[4])
    dw_out = _mm_plain("dw_out", "nn", merged_t, dmix, BF16).reshape(N_DEV, cs, d)

    def branch_bwd(tag, da, w_b, o_t, width):
        tb = _tile(width, 1024)
        do = _matmul("do_" + tag, "nt",
                     [(da, pl.BlockSpec((tm, cs), lambda i, j, k: (i, k)),
                       w_b, pl.BlockSpec((None, tb, cs), lambda i, j, k: (k, j, 0)))],
                     (s // tm, width // tb, N_DEV), (tm, tb), _sds((s, width), BF16),
                     pl.BlockSpec((tm, tb), lambda i, j, k: (i, j)))
        dw = _matmul("dw_" + tag, "nn",
                     [(o_t, pl.BlockSpec((width, s), lambda j, i, k: (0, 0)),
                       da, pl.BlockSpec((s, cs), lambda j, i, k: (0, j)))],
                     (N_DEV, 1, 1), (width, cs), _sds((N_DEV, width, cs), BF16),
                     pl.BlockSpec((None, width, cs), lambda j, i, k: (j, 0, 0)))
        return do, dw

    do_sb, dw_sb = branch_bwd("sb", da_sb, g_sb, o_sb_t, d_sb)
    do_fx, dw_fx = branch_bwd("fox", da_fx, g_fx, o_fx_t, d_fox)

    rs_mid = _scatter_pairs("mid", [dw_sb, dw_fx, dw_out])

    dqkv = _sb_bwd(qkv, do_sb, tot, h_sb, rs_mid[4])
    rs_mid = _scatter_chips("mid", rs_mid, dqkv)
    dqkv, dcum = _fox_bwd(dqkv, qkv, do_fx, o_fx32, lse, cum_col, cum_row, h_fox, h_sb, rs_mid[4])
    dgf, db_part = _forget_bwd(dgf, dcum, gf, b_pad, f_blk)
    dw_in = _w_in_grad_parts(_mm_plain("dw_qkv", "nn", u_t, dqkv, BF16), _mm_plain("dw_gates", "nn", u_t, dgf, BF16), lay)
    rs_in = _scatter_pairs("in", [dw_in])
    du = _mm_plain("du_qkv", "nt", dqkv, w_cat, F32, tn=1024, dep=rs_in[4])
    rs_in = _scatter_chips("in", rs_in, du)
    du = _mm_plain("du_gates", "nt", dgf, w_cat, F32, tn=1024, k_off=n_qkv, init=du, dep=rs_in[4])
    dx, dg_mix_pre = _pre_norm_bwd(dh1, du, xs, norm_mix_pre)

    upd = {}

    def update_group(tag, rs, names, after):
        parts = _scatter_end(tag, rs, after)
        for nm, p in zip(names, parts):
            w, m, v = decayed.get(nm, weights[nm])
            upd[nm] = _update("update_" + nm, p, w, m, v, layout=lay if nm == "w_in" else None, decayed=nm in decayed,
                              transposed_out=nm in ("w_in", "w_ffn_gate", "w_ffn_up"))

    update_group("ffn", rs_ffn, ("w_ffn_gate", "w_ffn_up", "w_ffn_down"), [dx])
    update_group("mid", rs_mid, ("w_branch_sb", "w_branch_fox", "w_out"), [upd[nm][3] for nm in ("w_ffn_gate", "w_ffn_up", "w_ffn_down")])
    update_group("in", rs_in, ("w_in",), [upd[nm][3] for nm in ("w_branch_sb", "w_branch_fox", "w_out")])

    small = ((norm_mix_pre, m_norm_mix_pre, v_norm_mix_pre), (norm_mix_post, m_norm_mix_post, v_norm_mix_post),
             (norm_ffn_pre, m_norm_ffn_pre, v_norm_ffn_pre), (norm_ffn_post, m_norm_ffn_post, v_norm_ffn_post))
    pad_f = ((0, 0), (0, LANES - n_f))
    cat = lambda i: jnp.concatenate([t[i] for t in small] + [jnp.pad((b_forget, m_b_forget, v_b_forget)[i], pad_f)], axis=1)
    sm = _small_update(jnp.concatenate([dg_mix_pre, dg_mix_post, dg_ffn_pre, dg_ffn_post, db_part], axis=1),
                       cat(0), cat(1), cat(2))
    for i, nm in enumerate(("norm_mix_pre", "norm_mix_post", "norm_ffn_pre", "norm_ffn_post")):
        upd[nm] = [o[:, i * d:(i + 1) * d] for o in sm]
    upd["b_forget"] = [o[:, 4 * d:4 * d + n_f] for o in sm]

    loss = lax.psum(loss_part[0, 0], ("x", "y", "c"))
    order = ("norm_mix_pre", "norm_mix_post", "w_in", "b_forget", "w_branch_sb", "w_branch_fox", "w_out",
             "norm_ffn_pre", "norm_ffn_post", "w_ffn_gate", "w_ffn_up", "w_ffn_down")
    return (loss, dx[None]) + tuple(upd[nm][i] for i in range(4) for nm in order)
```

```python
import jax
import jax.numpy as jnp
from jax import lax
from jax.experimental import pallas as pl
from jax.experimental.pallas import tpu as pltpu

F32 = jnp.float32
BF16 = jnp.bfloat16
MESH = pl.DeviceIdType.MESH
ANY = pl.BlockSpec(memory_space=pl.ANY)
HBM = pl.BlockSpec(memory_space=pltpu.HBM)
SEM = pl.BlockSpec(memory_space=pltpu.SEMAPHORE)
EFFECT = pltpu.SideEffectType.DATAFLOW_SIDE_EFFECTING

N_DEV = 8
HEAD_DIM = 128
RMS_EPS = 1e-6
F_PAD = 512
LANES = 128
ATT_TQ = 256
ATT_TK = 256
ATT_HP = 4
NEG_BIG = -1e30
VMEM_LIMIT = 56 * 1024 * 1024

ADAM_LR = 0.001
ADAM_B1 = 0.9
ADAM_B2 = 0.999
ADAM_EPS = 1e-08
ADAM_WD = 0.01
ADAM_STEP = 10

_DIMS = {"nn": ((1,), (0,)), "nt": ((1,), (1,)), "tn": ((0,), (0,))}


def _params(sem):
    return pltpu.CompilerParams(dimension_semantics=sem, vmem_limit_bytes=VMEM_LIMIT)


def _dot(a, b, mode="nn"):
    return lax.dot_general(a.astype(BF16), b.astype(BF16), (_DIMS[mode], ((), ())), preferred_element_type=F32)


def _tile(n, pref):
    if n <= pref:
        return n
    t = (pref // LANES) * LANES
    while n % t:
        t -= LANES
    return t


def _split2(v):
    hi = v.astype(BF16)
    return hi, (v - hi.astype(F32)).astype(BF16)


def _split3(v):
    a = v.astype(BF16)
    r = v - a.astype(F32)
    b = r.astype(BF16)
    return a, b, (r - b.astype(F32)).astype(BF16)


def _tri(n, cmp):
    r = lax.broadcasted_iota(jnp.int32, (n, n), 0)
    c = lax.broadcasted_iota(jnp.int32, (n, n), 1)
    return jnp.where(cmp(r, c), 1.0, 0.0).astype(BF16)


def _lane_pick(v, h):
    lane = lax.broadcasted_iota(jnp.int32, v.shape, 1)
    return jnp.sum(jnp.where(lane == h, v, 0.0), axis=1, keepdims=True)


def _lane_put(ref, rows, h, col):
    old = ref[rows, :]
    lane = lax.broadcasted_iota(jnp.int32, old.shape, 1)
    ref[rows, :] = jnp.where(lane == h, col, old)


def _sigmoid(z):
    return 1.0 / (1.0 + jnp.exp(-z))


def _log_sigmoid(z):
    return jnp.minimum(z, 0.0) - jnp.log(1.0 + jnp.exp(-jnp.abs(z)))


def _sds(shape, dtype):
    return jax.ShapeDtypeStruct(shape, dtype)


def _matmul(name, mode, pairs, grid, acc_shape, out_shape, out_specs, extras=(), epilogue=None, init=None, dep=None):
    n_p, n_e = len(pairs), len(extras)
    nk = grid[-1]
    single = not isinstance(out_shape, (list, tuple))
    n_i = 0 if init is None else 1
    n_d = 0 if dep is None else 1

    one_step = nk == 1 and init is None

    def body(*refs):
        ab = refs[:2 * n_p]
        ex = refs[2 * n_p:2 * n_p + n_e]
        ini = refs[2 * n_p + n_e:2 * n_p + n_e + n_i]
        outs = refs[2 * n_p + n_e + n_i + n_d:len(refs) - (0 if one_step else 1)]

        def finish(total):
            if epilogue is None:
                outs[0][...] = total.astype(outs[0].dtype)
            else:
                epilogue(total, ex, outs)

        t = _dot(ab[0][...], ab[1][...], mode)
        for p in range(1, n_p):
            t = t + _dot(ab[2 * p][...], ab[2 * p + 1][...], mode)
        if one_step:
            finish(t)
            return
        acc = refs[-1]
        k = pl.program_id(len(grid) - 1)

        @pl.when(k == 0)
        def _():
            acc[...] = t if init is None else ini[0][...].astype(F32) + t

        @pl.when(k > 0)
        def _():
            acc[...] += t

        @pl.when(k == nk - 1)
        def _():
            finish(acc[...])

    in_specs = [s for (_, sa, _, sb) in pairs for s in (sa, sb)] + [s for (_, s) in extras]
    args = [v for (a, _, b, _) in pairs for v in (a, b)] + [e for (e, _) in extras]
    if init is not None:
        in_specs.append(init[1])
        args.append(init[0])
    if dep is not None:
        in_specs.append(ANY)
        args.append(dep)
    return pl.pallas_call(
        body, name=name, grid=grid, in_specs=in_specs,
        out_specs=out_specs if single else list(out_specs),
        out_shape=out_shape if single else list(out_shape),
        scratch_shapes=[] if one_step else [pltpu.VMEM(acc_shape, F32)],
        compiler_params=_params(("parallel",) * (len(grid) - 1) + ("arbitrary",)),
    )(*args)


def _mm_plain(name, mode, a, b, out_dtype, *, n_off=0, n=None, k_off=0, tm=1024, tn=1536, tk=2048, init=None, dep=None):
    if mode == "nn":
        (m, kk), nn_ = a.shape, b.shape[1]
    elif mode == "nt":
        (m, kk), nn_ = a.shape, b.shape[0]
    else:
        (kk, m), nn_ = a.shape, b.shape[1]
    n = nn_ if n is None else n
    tm, tn, tk = _tile(m, tm), _tile(n, tn), _tile(kk, tk)
    while n_off % tn or n % tn:
        tn -= LANES
    while k_off % tk or kk % tk:
        tk -= LANES
    off, koff = n_off // tn, k_off // tk
    a_spec = {"nn": pl.BlockSpec((tm, tk), lambda i, j, k: (i, k)),
              "nt": pl.BlockSpec((tm, tk), lambda i, j, k: (i, k)),
              "tn": pl.BlockSpec((tk, tm), lambda i, j, k: (k, i))}[mode]
    b_spec = {"nn": pl.BlockSpec((tk, tn), lambda i, j, k: (k, j + off)),
              "nt": pl.BlockSpec((tn, tk), lambda i, j, k: (j, k + koff)),
              "tn": pl.BlockSpec((tk, tn), lambda i, j, k: (k, j))}[mode]
    o_spec = pl.BlockSpec((tm, tn), lambda i, j, k: (i, j))
    if init is not None:
        init = (init, o_spec)
    return _matmul(name, mode, [(a, a_spec, b, b_spec)], (m // tm, n // tn, kk // tk), (tm, tn),
                   _sds((m, n), out_dtype), o_spec, init=init, dep=dep)


def _rows_call(name, body, ins, outs, s, tr=256, dep=None):
    def spec(v, per_row):
        if per_row == "transposed":
            return pl.BlockSpec((v.shape[0], tr), lambda i: (0, i))
        if per_row:
            return pl.BlockSpec((tr, v.shape[1]), lambda i: (i, 0))
        return pl.BlockSpec(v.shape, lambda i: (0, 0))
    n_in = len(ins)
    deps = [] if dep is None else [dep]

    def with_dep(*refs):
        body(*refs[:n_in], *refs[n_in + len(deps):])

    return pl.pallas_call(
        with_dep, name=name, grid=(s // tr,),
        in_specs=[spec(v, p) for v, p in ins] + [ANY] * len(deps), out_specs=[spec(v, p) for v, p in outs],
        out_shape=[_sds(v.shape, v.dtype) for v, _ in outs],
        compiler_params=_params(("arbitrary",)),
    )(*[v for v, _ in ins], *deps)


def _rsq(v):
    return lax.rsqrt(jnp.mean(v * v, axis=-1, keepdims=True) + RMS_EPS)


def _norm_bwd(dy, v, r, g):
    vh = v * r
    t = dy * g
    dv = r * (t - vh * jnp.mean(t * vh, axis=-1, keepdims=True))
    return dv, jnp.sum(dy * vh, axis=0, keepdims=True)


def _accum(ref, val):
    @pl.when(pl.program_id(0) == 0)
    def _():
        ref[...] = jnp.zeros_like(ref)
    ref[...] += val


def _pre_norm(x, g, dep=None):
    def body(x_ref, g_ref, u_ref, ut_ref):
        v = x_ref[...]
        u = (v * _rsq(v) * g_ref[...]).astype(BF16)
        u_ref[...] = u
        ut_ref[...] = u.T
    s, d = x.shape
    return _rows_call("pre_norm", body, [(x, True), (g, False)],
                      [(_sds((s, d), BF16), True), (_sds((d, s), BF16), "transposed")], s, dep=dep)


def _mid_norms(x, mix, g_post, g_pre):
    def body(x_ref, mix_ref, gp_ref, gn_ref, h_ref, u_ref, ut_ref):
        mv = mix_ref[...]
        h = x_ref[...] + mv * _rsq(mv) * gp_ref[...]
        h_ref[...] = h
        u = (h * _rsq(h) * gn_ref[...]).astype(BF16)
        u_ref[...] = u
        ut_ref[...] = u.T
    s, d = x.shape
    return _rows_call("mid_norms", body, [(x, True), (mix, True), (g_post, False), (g_pre, False)],
                      [(_sds((s, d), F32), True), (_sds((s, d), BF16), True), (_sds((d, s), BF16), "transposed")], s)


def _loss_head(h1, ff, target, g):
    s, d = h1.shape

    def body(h_ref, ff_ref, t_ref, g_ref, loss_ref, dy_ref, dff_ref, dg_ref):
        fv = ff_ref[...]
        r = _rsq(fv)
        err = h_ref[...] + fv * r * g_ref[...] - t_ref[...]
        part = 0.5 * jnp.sum(jnp.mean(err * err, axis=-1, keepdims=True), axis=0, keepdims=True)
        _accum(loss_ref, jnp.broadcast_to(part, loss_ref.shape))
        dy = err * (1.0 / d)
        dy_ref[...] = dy
        dff, dg = _norm_bwd(dy, fv, r, g_ref[...])
        dff_ref[...] = dff.astype(BF16)
        _accum(dg_ref, dg)

    return _rows_call("loss_head", body, [(h1, True), (ff, True), (target, True), (g, False)],
                      [(_sds((1, LANES), F32), False), (_sds((s, d), F32), True),
                       (_sds((s, d), BF16), True), (_sds((1, d), F32), False)], s)


def _mid_norms_bwd(dy, du2, h1, mix, g_pre, g_post):
    s, d = dy.shape

    def body(dy_ref, du_ref, h_ref, mix_ref, gn_ref, gp_ref, dh_ref, dmix_ref, dgn_ref, dgp_ref):
        h = h_ref[...]
        dh, dgn = _norm_bwd(du_ref[...], h, _rsq(h), gn_ref[...])
        dh = dh + dy_ref[...]
        dh_ref[...] = dh
        _accum(dgn_ref, dgn)
        mv = mix_ref[...]
        dmix, dgp = _norm_bwd(dh, mv, _rsq(mv), gp_ref[...])
        dmix_ref[...] = dmix.astype(BF16)
        _accum(dgp_ref, dgp)

    return _rows_call("mid_norms_bwd", body,
                      [(dy, True), (du2, True), (h1, True), (mix, True), (g_pre, False), (g_post, False)],
                      [(_sds((s, d), F32), True), (_sds((s, d), BF16), True),
                       (_sds((1, d), F32), False), (_sds((1, d), F32), False)], s)


def _pre_norm_bwd(dh1, du, x, g, dep=None):
    s, d = x.shape

    def body(dh_ref, du_ref, x_ref, g_ref, dx_ref, dg_ref):
        v = x_ref[...]
        dv, dg = _norm_bwd(du_ref[...], v, _rsq(v), g_ref[...])
        dx_ref[...] = dh_ref[...] + dv
        _accum(dg_ref, dg)

    return _rows_call("pre_norm_bwd", body, [(dh1, True), (du, True), (x, True), (g, False)],
                      [(_sds((s, d), F32), True), (_sds((1, d), F32), False)], s, dep=dep)


def _forget_fwd(gf, b_pad, f_blk):
    s = gf.shape[0]
    tb = ATT_TK
    nb = s // tb

    def body(f_ref, b_ref, col_ref, row_ref):
        incl = _tri(tb, lambda r, c: c <= r)
        carry = jnp.zeros((1, LANES), F32)
        for i in range(nb):
            lf = _log_sigmoid(f_ref[pl.ds(i * tb, tb), :] + b_ref[...])
            parts = _split3(lf)
            cum = carry + _dot(incl, parts[0]) + _dot(incl, parts[1]) + _dot(incl, parts[2])
            col_ref[pl.ds(i * tb, tb), :] = cum
            row_ref[i] = cum.T
            carry = carry + jnp.sum(lf, axis=0, keepdims=True)

    return pl.pallas_call(
        body, name="forget_fwd", grid=(1,),
        in_specs=[pl.BlockSpec((s, LANES), lambda i: (0, f_blk)), pl.BlockSpec((1, LANES), lambda i: (0, 0))],
        out_specs=[pl.BlockSpec((s, LANES), lambda i: (0, 0)), pl.BlockSpec((nb, LANES, tb), lambda i: (0, 0, 0))],
        out_shape=[_sds((s, LANES), F32), _sds((nb, LANES, tb), F32)],
        compiler_params=_params(("arbitrary",)),
    )(gf, b_pad)


def _forget_bwd(dgf, dcum, gf, b_pad, f_blk):
    s = gf.shape[0]
    tb = ATT_TK
    nb = s // tb
    sec = dgf.shape[1] // F_PAD - 1

    def body(dgf_hbm, dc_ref, f_ref, b_ref, out_ref, db_ref):
        del dgf_hbm
        incl = _tri(tb, lambda r, c: c >= r)
        carry = jnp.zeros((1, LANES), F32)
        db = jnp.zeros((1, LANES), F32)
        out_ref[...] = jnp.zeros_like(out_ref)
        for i in reversed(range(nb)):
            dc = dc_ref[pl.ds(i * tb, tb), :]
            parts = _split3(dc)
            dlf = carry + _dot(incl, parts[0]) + _dot(incl, parts[1]) + _dot(incl, parts[2])
            z = f_ref[pl.ds(i * tb, tb), :] + b_ref[...]
            df = dlf * _sigmoid(-z)
            out_ref[pl.ds(i * tb, tb), pl.ds(0, LANES)] = df.astype(BF16)
            db = db + jnp.sum(df, axis=0, keepdims=True)
            carry = carry + jnp.sum(dc, axis=0, keepdims=True)
        db_ref[...] = db

    return pl.pallas_call(
        body, name="forget_bwd", grid=(1,),
        in_specs=[ANY, pl.BlockSpec((s, LANES), lambda i: (0, 0)),
                  pl.BlockSpec((s, LANES), lambda i: (0, f_blk)), pl.BlockSpec((1, LANES), lambda i: (0, 0))],
        out_specs=[pl.BlockSpec((s, F_PAD), lambda i: (0, sec)), pl.BlockSpec((1, LANES), lambda i: (0, 0))],
        out_shape=[_sds(dgf.shape, BF16), _sds((1, LANES), F32)],
        input_output_aliases={0: 0},
        compiler_params=_params(("arbitrary",)),
    )(dgf, dcum, gf, b_pad)


def _diag_mask(strict):
    r = lax.broadcasted_iota(jnp.int32, (ATT_TQ, ATT_TK), 0)
    c = lax.broadcasted_iota(jnp.int32, (ATT_TQ, ATT_TK), 1)
    return c < r if strict else c <= r


def _qkv_specs(hb0, s):
    specs = []
    for j in range(ATT_HP):
        def col(g, j=j):
            return 3 * (hb0 + ATT_HP * g + j)
        specs += [pl.BlockSpec((ATT_TQ, HEAD_DIM), lambda g, i, col=col: (i, col(g))),
                  pl.BlockSpec((s, HEAD_DIM), lambda g, i, col=col: (0, col(g) + 1)),
                  pl.BlockSpec((s, HEAD_DIM), lambda g, i, col=col: (0, col(g) + 2))]
    return specs


def _head_cols(j):
    return pl.ds(j * HEAD_DIM, HEAD_DIM)


def _sb_fwd(qkv, n_heads):
    s = qkv.shape[0]
    scale = HEAD_DIM ** -0.5
    tq, tk = ATT_TQ, ATT_TK
    heads = range(ATT_HP)

    def body(*refs):
        qkv_refs, (o_ref, ot_ref, tot_ref) = refs[:3 * ATT_HP], refs[3 * ATT_HP:]
        g, i = pl.program_id(0), pl.program_id(1)

        @pl.when((g == 0) & (i == 0))
        def _():
            tot_ref[...] = jnp.zeros_like(tot_ref)

        qs = [qkv_refs[3 * j][...] for j in heads]
        upper = _tri(tk, lambda r, c: r > c)

        def tile(kj, carry, mask):
            rows = pl.ds(pl.multiple_of(kj * tk, tk), tk)
            z = [_dot(qs[j], qkv_refs[3 * j + 1][rows, :], "nt") * scale for j in heads]
            lsz = [_log_sigmoid(z[j]) for j in heads]
            lk = [lsz[j] - z[j] if mask is None else jnp.where(mask, lsz[j] - z[j], 0.0) for j in heads]
            parts = [_split2(lk[j]) for j in heads]
            above = [carry[j][0] + _dot(parts[j][0], upper) + _dot(parts[j][1], upper) for j in heads]
            w = [jnp.exp(lsz[j] + above[j]) for j in heads]
            if mask is not None:
                w = [jnp.where(mask, w[j], 0.0) for j in heads]
            return tuple((carry[j][0] + jnp.sum(lk[j], axis=1, keepdims=True),
                          carry[j][1] + _dot(w[j], qkv_refs[3 * j + 2][rows, :])) for j in heads)

        carry = tile(i, tuple((jnp.zeros((tq, 1), F32), jnp.zeros((tq, HEAD_DIM), F32)) for _ in heads), _diag_mask(True))
        carry = lax.fori_loop(0, i, lambda n, cr: tile(i - 1 - n, cr, None), carry)
        q_rows = pl.ds(pl.multiple_of(i * tq, tq), tq)
        for j in heads:
            c, acc = carry[j]
            o = acc.astype(BF16)
            o_ref[:, _head_cols(j)] = o
            ot_ref[_head_cols(j), :] = o.T
            _lane_put(tot_ref, q_rows, ATT_HP * g + j, c)

    wide = ATT_HP * HEAD_DIM
    return pl.pallas_call(
        body, name="sb_fwd", grid=(n_heads // ATT_HP, s // tq),
        in_specs=_qkv_specs(0, s),
        out_specs=[pl.BlockSpec((tq, wide), lambda g, i: (i, g)), pl.BlockSpec((wide, tq), lambda g, i: (g, i)),
                   pl.BlockSpec((s, LANES), lambda g, i: (0, 0))],
        out_shape=[_sds((s, n_heads * HEAD_DIM), BF16), _sds((n_heads * HEAD_DIM, s), BF16), _sds((s, LANES), F32)],
        compiler_params=_params(("arbitrary", "arbitrary")),
    )(*[qkv] * (3 * ATT_HP))


def _sb_bwd(qkv, do, tot, n_heads, dep):
    s = qkv.shape[0]
    scale = HEAD_DIM ** -0.5
    tq, tk = ATT_TQ, ATT_TK
    nq = s // tq
    hd = HEAD_DIM

    heads = range(ATT_HP)

    def body(*refs):
        qkv_refs = refs[:3 * ATT_HP]
        do_ref, tot_ref, _, out_ref, dk_acc, dv_acc = refs[3 * ATT_HP:]
        g, i = pl.program_id(0), pl.program_id(1)

        @pl.when(i == 0)
        def _():
            dk_acc[...] = jnp.zeros_like(dk_acc)
            dv_acc[...] = jnp.zeros_like(dv_acc)

        qs = [qkv_refs[3 * j][...] for j in heads]
        douts = [do_ref[:, _head_cols(j)] for j in heads]
        totals = [_lane_pick(tot_ref[...], ATT_HP * g + j) for j in heads]
        incl = _tri(tk, lambda r, c: r <= c)
        excl = _tri(tk, lambda r, c: r < c)

        def tile(kj, carry, mask):
            rows = pl.ds(pl.multiple_of(kj * tk, tk), tk)
            k_t = [qkv_refs[3 * j + 1][rows, :] for j in heads]
            z = [_dot(qs[j], k_t[j], "nt") * scale for j in heads]
            dw = [_dot(douts[j], qkv_refs[3 * j + 2][rows, :], "nt") for j in heads]
            lsz = [_log_sigmoid(z[j]) for j in heads]
            lk = [lsz[j] - z[j] if mask is None else jnp.where(mask, lsz[j] - z[j], 0.0) for j in heads]
            parts = [_split2(lk[j]) for j in heads]
            below = [carry[j][0] + _dot(parts[j][0], incl) + _dot(parts[j][1], incl) for j in heads]
            w = [jnp.exp(lsz[j] + (totals[j] - below[j])) for j in heads]
            if mask is not None:
                w = [jnp.where(mask, w[j], 0.0) for j in heads]
            e = [dw[j] * w[j] for j in heads]
            parts = [_split2(e[j]) for j in heads]
            e_before = [carry[j][1] + _dot(parts[j][0], excl) + _dot(parts[j][1], excl) for j in heads]
            sg = [jnp.exp(lsz[j]) for j in heads]
            dz = [e[j] * (1.0 - sg[j]) - e_before[j] * sg[j] for j in heads]
            if mask is not None:
                dz = [jnp.where(mask, dz[j], 0.0) for j in heads]
            dz = [(dz[j] * scale).astype(BF16) for j in heads]
            for j in heads:
                dk_acc[j, rows, :] += _dot(dz[j], qs[j], "tn")
                dv_acc[j, rows, :] += _dot(w[j], douts[j], "tn")
            return tuple((carry[j][0] + jnp.sum(lk[j], axis=1, keepdims=True),
                          carry[j][1] + jnp.sum(e[j], axis=1, keepdims=True),
                          carry[j][2] + _dot(dz[j], k_t[j])) for j in heads)

        zero = jnp.zeros((tq, 1), F32)
        carry = lax.fori_loop(0, i, lambda kj, cr: tile(kj, cr, None),
                              tuple((zero, zero, jnp.zeros((tq, hd), F32)) for _ in heads))
        carry = tile(i, carry, _diag_mask(True))
        for j in heads:
            out_ref[pl.ds(pl.multiple_of(i * tq, tq), tq), pl.ds(3 * j * hd, hd)] = carry[j][2].astype(BF16)

        @pl.when(i == nq - 1)
        def _():
            for j in heads:
                out_ref[:, pl.ds((3 * j + 1) * hd, hd)] = dk_acc[j].astype(BF16)
                out_ref[:, pl.ds((3 * j + 2) * hd, hd)] = dv_acc[j].astype(BF16)

    wide = ATT_HP * hd
    return pl.pallas_call(
        body, name="sb_bwd", grid=(n_heads // ATT_HP, nq),
        in_specs=_qkv_specs(0, s) + [pl.BlockSpec((tq, wide), lambda g, i: (i, g)),
                                     pl.BlockSpec((tq, LANES), lambda g, i: (i, 0)), ANY],
        out_specs=pl.BlockSpec((s, 3 * wide), lambda g, i: (0, g)),
        out_shape=_sds(qkv.shape, BF16),
        scratch_shapes=[pltpu.VMEM((ATT_HP, s, hd), F32), pltpu.VMEM((ATT_HP, s, hd), F32)],
        compiler_params=_params(("arbitrary", "arbitrary")),
    )(*[qkv] * (3 * ATT_HP), do, tot, dep)


def _fox_fwd(qkv, cum_col, cum_row, n_heads, hb0, dep):
    s = qkv.shape[0]
    scale = HEAD_DIM ** -0.5
    tq, tk = ATT_TQ, ATT_TK

    heads = range(ATT_HP)

    def body(*refs):
        qkv_refs = refs[:3 * ATT_HP]
        cc_ref, cr_ref, _, o_ref, ot_ref, o32_ref, lse_ref = refs[3 * ATT_HP:]
        g, i = pl.program_id(0), pl.program_id(1)

        @pl.when((g == 0) & (i == 0))
        def _():
            lse_ref[...] = jnp.zeros_like(lse_ref)

        qs = [qkv_refs[3 * j][...] for j in heads]
        cqs = [_lane_pick(cc_ref[...], ATT_HP * g + j) for j in heads]

        def tile(kj, carry, mask):
            rows = pl.ds(pl.multiple_of(kj * tk, tk), tk)
            sc = [_dot(qs[j], qkv_refs[3 * j + 1][rows, :], "nt") * scale + cqs[j]
                  - cr_ref[kj, pl.ds(ATT_HP * g + j, 1), :] for j in heads]
            if mask is not None:
                sc = [jnp.where(mask, sc[j], NEG_BIG) for j in heads]
            m_new = [jnp.maximum(carry[j][0], jnp.max(sc[j], axis=1, keepdims=True)) for j in heads]
            p = [jnp.exp(sc[j] - m_new[j]) for j in heads]
            alpha = [jnp.exp(carry[j][0] - m_new[j]) for j in heads]
            parts = [_split2(p[j]) for j in heads]
            v_t = [qkv_refs[3 * j + 2][rows, :] for j in heads]
            pv = [_dot(parts[j][0], v_t[j]) + _dot(parts[j][1], v_t[j]) for j in heads]
            return tuple((m_new[j], alpha[j] * carry[j][1] + jnp.sum(p[j], axis=1, keepdims=True),
                          alpha[j] * carry[j][2] + pv[j]) for j in heads)

        carry = tuple((jnp.full((tq, 1), NEG_BIG, F32), jnp.zeros((tq, 1), F32), jnp.zeros((tq, HEAD_DIM), F32))
                      for _ in heads)
        carry = lax.fori_loop(0, i, lambda kj, cr: tile(kj, cr, None), carry)
        carry = tile(i, carry, _diag_mask(False))
        q_rows = pl.ds(pl.multiple_of(i * tq, tq), tq)
        for j in heads:
            m, l, acc = carry[j]
            o = acc / l
            o_ref[:, _head_cols(j)] = o.astype(BF16)
            ot_ref[_head_cols(j), :] = o.astype(BF16).T
            o32_ref[:, _head_cols(j)] = o
            _lane_put(lse_ref, q_rows, ATT_HP * g + j, m + jnp.log(l))

    nb = cum_row.shape[0]
    wide = ATT_HP * HEAD_DIM
    return pl.pallas_call(
        body, name="fox_fwd", grid=(n_heads // ATT_HP, s // tq),
        in_specs=_qkv_specs(hb0, s) + [pl.BlockSpec((tq, LANES), lambda g, i: (i, 0)),
                                       pl.BlockSpec((nb, 8, tk), lambda g, i: (0, 0, 0)), ANY],
        out_specs=[pl.BlockSpec((tq, wide), lambda g, i: (i, g)), pl.BlockSpec((wide, tq), lambda g, i: (g, i)),
                   pl.BlockSpec((tq, wide), lambda g, i: (i, g)), pl.BlockSpec((s, LANES), lambda g, i: (0, 0))],
        out_shape=[_sds((s, n_heads * HEAD_DIM), BF16), _sds((n_heads * HEAD_DIM, s), BF16),
                   _sds((s, n_heads * HEAD_DIM), F32), _sds((s, LANES), F32)],
        compiler_params=_params(("arbitrary", "arbitrary")),
    )(*[qkv] * (3 * ATT_HP), cum_col, cum_row, dep)


def _fox_bwd(dqkv, qkv, do, o, lse, cum_col, cum_row, n_heads, hb0, dep):
    s = qkv.shape[0]
    scale = HEAD_DIM ** -0.5
    tq, tk = ATT_TQ, ATT_TK
    nq = s // tq
    hd = HEAD_DIM

    heads = range(ATT_HP)
    assert hb0 % ATT_HP == 0

    def body(*refs):
        qkv_refs = refs[1:1 + 3 * ATT_HP]
        do_ref, o_ref, lse_ref, cc_ref, cr_ref, _, out_ref, dc_ref, dk_acc, dv_acc, col_acc = refs[1 + 3 * ATT_HP:]
        g, i = pl.program_id(0), pl.program_id(1)

        @pl.when((g == 0) & (i == 0))
        def _():
            dc_ref[...] = jnp.zeros_like(dc_ref)

        @pl.when(i == 0)
        def _():
            dk_acc[...] = jnp.zeros_like(dk_acc)
            dv_acc[...] = jnp.zeros_like(dv_acc)
            col_acc[...] = jnp.zeros_like(col_acc)

        qs = [qkv_refs[3 * j][...] for j in heads]
        douts = [do_ref[:, _head_cols(j)] for j in heads]
        deltas = [jnp.sum(douts[j].astype(F32) * o_ref[:, _head_cols(j)], axis=1, keepdims=True) for j in heads]
        shifts = [_lane_pick(cc_ref[...], ATT_HP * g + j) - _lane_pick(lse_ref[...], ATT_HP * g + j) for j in heads]

        def tile(kj, carry, mask):
            rows = pl.ds(pl.multiple_of(kj * tk, tk), tk)
            k_t = [qkv_refs[3 * j + 1][rows, :] for j in heads]
            sc = [_dot(qs[j], k_t[j], "nt") * scale + shifts[j] - cr_ref[kj, pl.ds(ATT_HP * g + j, 1), :] for j in heads]
            dp = [_dot(douts[j], qkv_refs[3 * j + 2][rows, :], "nt") for j in heads]
            p = [jnp.exp(sc[j]) for j in heads]
            if mask is not None:
                p = [jnp.where(mask, p[j], 0.0) for j in heads]
            ds_f = [p[j] * (dp[j] - deltas[j]) for j in heads]
            ds = [(ds_f[j] * scale).astype(BF16) for j in heads]
            for j in heads:
                col_acc[j, kj] += jnp.broadcast_to(jnp.sum(ds_f[j], axis=0, keepdims=True), (8, tk))
                dk_acc[j, rows, :] += _dot(ds[j], qs[j], "tn")
                dv_acc[j, rows, :] += _dot(p[j], douts[j], "tn")
            return tuple((carry[j][0] + _dot(ds[j], k_t[j]), carry[j][1] + jnp.sum(ds_f[j], axis=1, keepdims=True))
                         for j in heads)

        carry = lax.fori_loop(0, i, lambda kj, cr: tile(kj, cr, None),
                              tuple((jnp.zeros((tq, hd), F32), jnp.zeros((tq, 1), F32)) for _ in heads))
        carry = tile(i, carry, _diag_mask(False))
        q_rows = pl.ds(pl.multiple_of(i * tq, tq), tq)
        for j in heads:
            out_ref[q_rows, pl.ds(3 * j * hd, hd)] = carry[j][0].astype(BF16)
            _lane_put(dc_ref, q_rows, ATT_HP * g + j, carry[j][1])

        @pl.when(i == nq - 1)
        def _():
            lane = lax.broadcasted_iota(jnp.int32, (tk, LANES), 1)
            for j in heads:
                out_ref[:, pl.ds((3 * j + 1) * hd, hd)] = dk_acc[j].astype(BF16)
                out_ref[:, pl.ds((3 * j + 2) * hd, hd)] = dv_acc[j].astype(BF16)
                for kj in range(nb):
                    col = jnp.broadcast_to(col_acc[j, kj][0:1, :], (LANES, tk)).T
                    old = dc_ref[pl.ds(kj * tk, tk), :]
                    dc_ref[pl.ds(kj * tk, tk), :] = jnp.where(lane == ATT_HP * g + j, old - col, old)

    nb = cum_row.shape[0]
    wide = ATT_HP * hd
    return pl.pallas_call(
        body, name="fox_bwd", grid=(n_heads // ATT_HP, nq),
        in_specs=[ANY] + _qkv_specs(hb0, s) + [
            pl.BlockSpec((tq, wide), lambda g, i: (i, g)), pl.BlockSpec((tq, wide), lambda g, i: (i, g)),
            pl.BlockSpec((tq, LANES), lambda g, i: (i, 0)), pl.BlockSpec((tq, LANES), lambda g, i: (i, 0)),
            pl.BlockSpec((nb, 8, tk), lambda g, i: (0, 0, 0)), ANY],
        out_specs=[pl.BlockSpec((s, 3 * wide), lambda g, i: (0, hb0 // ATT_HP + g)),
                   pl.BlockSpec((s, LANES), lambda g, i: (0, 0))],
        out_shape=[_sds(dqkv.shape, BF16), _sds((s, LANES), F32)],
        scratch_shapes=[pltpu.VMEM((ATT_HP, s, hd), F32), pltpu.VMEM((ATT_HP, s, hd), F32),
                        pltpu.VMEM((ATT_HP, s // tk, 8, tk), F32)],
        input_output_aliases={0: 0},
        compiler_params=_params(("arbitrary", "arbitrary")),
    )(dqkv, *[qkv] * (3 * ATT_HP), do, o, lse, cum_col, cum_row, dep)


def _branch_merge(o_sb, o_fx, w_sb, w_fx, gf, dep, tm=1024):
    s = o_sb.shape[0]
    cs = w_sb.shape[2]
    tm = _tile(s, tm)

    def body(osb_ref, ofx_ref, wsb_ref, wfx_ref, g_ref, dep_ref, merged_ref, mt_ref, asb_ref, afx_ref):
        del dep_ref
        a_sb = _dot(osb_ref[...], wsb_ref[...])
        a_fx = _dot(ofx_ref[...], wfx_ref[...])
        g = g_ref[...]
        merged = (_sigmoid(g[:, :cs]) * a_sb + _sigmoid(g[:, cs:]) * a_fx).astype(BF16)
        merged_ref[...] = merged
        mt_ref[...] = merged.T
        asb_ref[...] = a_sb.astype(BF16)
        afx_ref[...] = a_fx.astype(BF16)

    blk = pl.BlockSpec((tm, cs), lambda i, j: (i, j))
    out = _sds((s, N_DEV * cs), BF16)
    return pl.pallas_call(
        body, name="branch_merge", grid=(s // tm, N_DEV),
        in_specs=[pl.BlockSpec((tm, o_sb.shape[1]), lambda i, j: (i, 0)),
                  pl.BlockSpec((tm, o_fx.shape[1]), lambda i, j: (i, 0)),
                  pl.BlockSpec((None,) + w_sb.shape[1:], lambda i, j: (j, 0, 0)),
                  pl.BlockSpec((None,) + w_fx.shape[1:], lambda i, j: (j, 0, 0)),
                  pl.BlockSpec((tm, 2 * cs), lambda i, j: (i, j)), ANY],
        out_specs=[blk, pl.BlockSpec((cs, tm), lambda i, j: (j, i)), blk, blk],
        out_shape=[out, _sds((N_DEV * cs, s), BF16), out, out],
        compiler_params=_params(("parallel", "arbitrary")),
    )(o_sb, o_fx, w_sb, w_fx, gf, dep)


def _merge_bwd(dmix, w_out, gf, a_sb, a_fx, tm=1024, tk=2048, dep=None):
    s, d = dmix.shape
    cs = d // N_DEV
    tm, tk = _tile(s, tm), _tile(d, tk)

    def epilogue(acc, ex, outs):
        g, a_sb, a_fx = ex[0][...], ex[1][...].astype(F32), ex[2][...].astype(F32)
        s_sb, s_fx = _sigmoid(g[:, :cs]), _sigmoid(g[:, cs:])
        outs[0][...] = (acc * s_sb).astype(BF16)
        outs[1][...] = (acc * s_fx).astype(BF16)
        outs[2][...] = jnp.concatenate([acc * a_sb * s_sb * (1.0 - s_sb), acc * a_fx * s_fx * (1.0 - s_fx)],
                                       axis=1).astype(BF16)

    blk = pl.BlockSpec((tm, cs), lambda i, j, k: (i, j))
    wide = pl.BlockSpec((tm, 2 * cs), lambda i, j, k: (i, j))
    return _matmul(
        "merge_bwd", "nt",
        [(dmix, pl.BlockSpec((tm, tk), lambda i, j, k: (i, k)), w_out, pl.BlockSpec((cs, tk), lambda i, j, k: (j, k)))],
        (s // tm, N_DEV, d // tk), (tm, cs),
        [_sds((s, d), BF16), _sds((s, d), BF16), _sds(gf.shape, BF16)], [blk, blk, wide],
        extras=[(gf, wide), (a_sb, blk), (a_fx, blk)], epilogue=epilogue, dep=dep)


def _ffn_up(u2, w_gate, w_up, dep, tm=1024):
    s, d = u2.shape
    fs = w_gate.shape[2]
    tm = _tile(s, tm)

    def body(u_ref, wg_ref, wu_ref, dep_ref, gate_ref, up_ref, act_ref, actt_ref):
        del dep_ref
        u = u_ref[...]
        gate = _dot(u, wg_ref[...])
        up = _dot(u, wu_ref[...])
        gate_ref[...] = gate
        up_ref[...] = up
        act = (gate * _sigmoid(gate) * up).astype(BF16)
        act_ref[...] = act
        actt_ref[...] = act.T

    w_spec = pl.BlockSpec((None, d, fs), lambda i, j: (j, 0, 0))
    o_spec = pl.BlockSpec((None, tm, fs), lambda i, j: (j, i, 0))
    return pl.pallas_call(
        body, name="ffn_up", grid=(s // tm, N_DEV),
        in_specs=[pl.BlockSpec((tm, d), lambda i, j: (i, 0)), w_spec, w_spec, ANY],
        out_specs=[o_spec, o_spec, o_spec, pl.BlockSpec((None, fs, tm), lambda i, j: (j, 0, i))],
        out_shape=[_sds((N_DEV, s, fs), F32), _sds((N_DEV, s, fs), F32), _sds((N_DEV, s, fs), BF16),
                   _sds((N_DEV, fs, s), BF16)],
        compiler_params=_params(("parallel", "arbitrary")),
    )(u2, w_gate, w_up, dep)


def _ffn_down_bwd(dff, w_down, gate, up, tm=1024):
    s, d = dff.shape
    fs = w_down.shape[1]
    tm = _tile(s, tm)

    def body(dff_ref, wd_ref, gate_ref, up_ref, dgate_ref, dup_ref):
        dact = _dot(dff_ref[...], wd_ref[...], "nt")
        gate = gate_ref[...]
        sg = _sigmoid(gate)
        dup_ref[...] = (dact * gate * sg).astype(BF16)
        dgate_ref[...] = (dact * up_ref[...] * sg * (1.0 + gate * (1.0 - sg))).astype(BF16)

    a_spec = pl.BlockSpec((None, tm, fs), lambda i, j: (j, i, 0))
    return pl.pallas_call(
        body, name="ffn_down_bwd", grid=(s // tm, N_DEV),
        in_specs=[pl.BlockSpec((tm, d), lambda i, j: (i, 0)), pl.BlockSpec((None, fs, d), lambda i, j: (j, 0, 0)),
                  a_spec, a_spec],
        out_specs=[a_spec, a_spec],
        out_shape=[_sds((N_DEV, s, fs), BF16), _sds((N_DEV, s, fs), BF16)],
        compiler_params=_params(("parallel", "arbitrary")),
    )(dff, w_down, gate, up)


def _mesh_place():
    x, y, c = lax.axis_index("x"), lax.axis_index("y"), lax.axis_index("c")
    peers = []
    for d in range(1, N_DEV):
        px = 1 - x if d & 4 else x
        py = 1 - y if d & 2 else y
        pc = 1 - c if d & 1 else c
        peers.append((d, (px, py, pc), 4 * px + 2 * py + pc))
    return 4 * x + 2 * y + c, peers


def _flat_me():
    return 4 * lax.axis_index("x") + 2 * lax.axis_index("y") + lax.axis_index("c")


def _in_hbm(a):
    return pltpu.with_memory_space_constraint(a, pltpu.HBM)


def _pair_plan():
    x, y, c = lax.axis_index("x"), lax.axis_index("y"), lax.axis_index("c")
    return [(2 * q + (1 - c), q, q, (x, y, 1 - c)) for q in range(4)]


def _chip_plan():
    x, y, c = lax.axis_index("x"), lax.axis_index("y"), lax.axis_index("c")
    plan = []
    for fx, fy in ((1, 0), (0, 1), (1, 1)):
        cx, cy = (1 - x if fx else x), (1 - y if fy else y)
        plan.append((2 * cx + cy, 2 * x + y, 2 * cx + cy, (cx, cy, c)))
    return plan


def _split_start(name, srcs, lands, plan, k):
    n = len(srcs)

    def body(*refs):
        ins, lnd = refs[:n], refs[n:2 * n]
        send, recv, token = refs[2 * n], refs[2 * n + 1], refs[-1]
        copies = plan()
        for a in range(n):
            for t, (src, dst, _, dev) in enumerate(copies):
                pltpu.make_async_remote_copy(src_ref=ins[a].at[src], dst_ref=lnd[a].at[dst], send_sem=send.at[k * a + t],
                                             recv_sem=recv.at[k * a + t], device_id=dev, device_id_type=MESH).start()
        token[...] = jnp.zeros_like(token)

    res = pl.pallas_call(
        body, name=name,
        out_shape=[pltpu.SemaphoreType.DMA((n * k,)), pltpu.SemaphoreType.DMA((n * k,))]
        + [pltpu.HBM(a.shape, a.dtype) for a in list(srcs) + list(lands)] + [_sds((8, LANES), F32)],
        in_specs=[HBM] * (2 * n), out_specs=[SEM, SEM] + [HBM] * (2 * n) + [pl.BlockSpec(memory_space=pltpu.VMEM)],
        input_output_aliases={i: 2 + i for i in range(2 * n)},
        compiler_params=pltpu.CompilerParams(has_side_effects=EFFECT),
    )(*[_in_hbm(a) for a in srcs], *[_in_hbm(a) for a in lands])
    return res[0], res[1], res[2:2 + n], res[2 + n:2 + 2 * n], res[-1]


def _split_wait(name, send, recv, srcs, lands, plan, k, after):
    n = len(srcs)

    def body(*refs):
        ins, lnd = refs[:n], refs[n:2 * n]
        send_sem, recv_sem = refs[2 * n], refs[2 * n + 1]
        copies = plan()
        for a in range(n):
            for t, (src, _, dst, dev) in enumerate(copies):
                cp = pltpu.make_async_remote_copy(src_ref=ins[a].at[src], dst_ref=lnd[a].at[dst], send_sem=send_sem.at[k * a + t],
                                                  recv_sem=recv_sem.at[k * a + t], device_id=dev, device_id_type=MESH)
                cp.wait_send()
                cp.wait_recv()

    res = pl.pallas_call(
        body, name=name,
        out_shape=[pltpu.HBM(a.shape, a.dtype) for a in list(srcs) + list(lands)],
        in_specs=[HBM] * (2 * n) + [SEM, SEM] + [ANY] * len(after), out_specs=[HBM] * (2 * n),
        input_output_aliases={i: i for i in range(2 * n)},
        compiler_params=pltpu.CompilerParams(has_side_effects=EFFECT),
    )(*srcs, *lands, send, recv, *after)
    return res[:n], res[n:]


def _pair_add(name, parts, land):
    _, r, cols = parts.shape
    tr = max(16, min(r, ((1 << 20) // (2 * cols)) // 16 * 16))
    while r % tr:
        tr -= 16

    def body(c_ref, p_ref, l_ref, o_ref):
        del c_ref
        o_ref[...] = (p_ref[...].astype(F32) + l_ref[...].astype(F32)).astype(BF16)

    blk = pl.BlockSpec((None, tr, cols), lambda q, i, c_ref: (q, i, 0))
    return pl.pallas_call(
        body, name=name,
        grid_spec=pltpu.PrefetchScalarGridSpec(
            num_scalar_prefetch=1, grid=(4, r // tr),
            in_specs=[pl.BlockSpec((None, tr, cols), lambda q, i, c_ref: (2 * q + c_ref[0], i, 0)), blk], out_specs=blk),
        out_shape=_sds((4, r, cols), BF16),
        compiler_params=_params(("parallel", "parallel")),
    )(jnp.reshape(lax.axis_index("c"), (1,)).astype(jnp.int32), parts, land)


def _scatter_pairs(tag, parts):
    lands = [lax.empty((4,) + a.shape[1:], a.dtype) for a in parts]
    return _split_start("pair_" + tag, parts, lands, _pair_plan, 4)


def _scatter_chips(tag, started, after):
    send, recv, parts, lands, _ = started
    parts, lands = _split_wait("pair_" + tag + "_wait", send, recv, parts, lands, _pair_plan, 4,
                               after if isinstance(after, list) else [after])
    sums = [_pair_add("pair_" + tag + "_add%d" % a, p, l) for a, (p, l) in enumerate(zip(parts, lands))]
    chip = 2 * lax.axis_index("x") + lax.axis_index("y")
    final = [lax.dynamic_update_slice_in_dim(lax.empty(v.shape, v.dtype), lax.dynamic_slice_in_dim(v, chip, 1, 0), chip, 0)
             for v in sums]
    return _split_start("chips_" + tag, sums, final, _chip_plan, 3)


def _scatter_end(tag, started, after):
    send, recv, sums, final, _ = started
    return _split_wait("chips_" + tag + "_wait", send, recv, sums, final, _chip_plan, 3, after)[1]


def _gather_targets():
    x, y, c = lax.axis_index("x"), lax.axis_index("y"), lax.axis_index("c")
    chips = [(x, y), (1 - x, y), (x, 1 - y), (1 - x, 1 - y)]
    same = [((cx, cy, c), 4 * cx + 2 * cy + c) for cx, cy in chips]
    other = [((cx, cy, 1 - c), 4 * cx + 2 * cy + 1 - c) for cx, cy in chips]
    return same[0][1], [other[0]] + same[1:], [flat for _, flat in other[1:]], other[0][0]


def _gather_start(shards):
    n = len(shards)
    me = _flat_me()
    lands = [lax.dynamic_update_slice_in_dim(lax.empty((N_DEV,) + a.shape, a.dtype), a[None], me, 0) for a in shards]

    def body(*refs):
        lnd, send, recv, token = refs[:n], refs[n], refs[n + 1], refs[-1]
        mine, targets, _, _ = _gather_targets()
        for a in range(n):
            for t, (dev, _) in enumerate(targets):
                pltpu.make_async_remote_copy(src_ref=lnd[a].at[mine], dst_ref=lnd[a].at[mine], send_sem=send.at[4 * a + t],
                                             recv_sem=recv.at[4 * a + t], device_id=dev, device_id_type=MESH).start()
        token[...] = jnp.zeros_like(token)

    res = pl.pallas_call(
        body, name="gather_start",
        out_shape=[pltpu.SemaphoreType.DMA((4 * n,)), pltpu.SemaphoreType.DMA((4 * n,))]
        + [pltpu.HBM(a.shape, a.dtype) for a in lands] + [_sds((8, LANES), F32)],
        in_specs=[HBM] * n, out_specs=[SEM, SEM] + [HBM] * n + [pl.BlockSpec(memory_space=pltpu.VMEM)],
        input_output_aliases={i: 2 + i for i in range(n)},
        compiler_params=pltpu.CompilerParams(has_side_effects=EFFECT),
    )(*[_in_hbm(a) for a in lands])
    return res[0], res[1], list(res[2:2 + n]), res[-1]


def _gather_forward(name, lands, first, send, recv, after):
    n = len(lands)

    def body(*refs):
        lnd, send_sem, recv_sem = refs[:n], refs[n], refs[n + 1]
        send2, recv2, token = refs[-3], refs[-2], refs[-1]
        mine, targets, _, sibling = _gather_targets()
        for a in range(n):
            for t, (dev, flat) in enumerate(targets):
                cp = pltpu.make_async_remote_copy(src_ref=lnd[a].at[mine], dst_ref=lnd[a].at[flat],
                                                  send_sem=send_sem.at[4 * (first + a) + t],
                                                  recv_sem=recv_sem.at[4 * (first + a) + t], device_id=dev, device_id_type=MESH)
                cp.wait_send()
                if t:
                    cp.wait_recv()
                    pltpu.make_async_remote_copy(src_ref=lnd[a].at[flat], dst_ref=lnd[a].at[flat], send_sem=send2.at[3 * a + t - 1],
                                                 recv_sem=recv2.at[3 * a + t - 1], device_id=sibling, device_id_type=MESH).start()
        token[...] = jnp.zeros_like(token)

    res = pl.pallas_call(
        body, name=name,
        out_shape=[pltpu.HBM(a.shape, a.dtype) for a in lands]
        + [pltpu.SemaphoreType.DMA((3 * n,)), pltpu.SemaphoreType.DMA((3 * n,)), _sds((8, LANES), F32)],
        in_specs=[HBM] * n + [SEM, SEM] + [ANY] * len(after),
        out_specs=[HBM] * n + [SEM, SEM, pl.BlockSpec(memory_space=pltpu.VMEM)],
        input_output_aliases={i: i for i in range(n)},
        compiler_params=pltpu.CompilerParams(has_side_effects=EFFECT),
    )(*lands, send, recv, *after)
    return list(res[:n]), res[n], res[n + 1], res[-1]


def _gather_wait(name, lands, first, recv, send2, recv2, after):
    n = len(lands)

    def body(*refs):
        lnd, recv_sem, send2_sem, recv2_sem = refs[:n], refs[n], refs[n + 1], refs[n + 2]
        mine, targets, passed, sibling = _gather_targets()
        for a in range(n):
            dev, flat = targets[0]
            pltpu.make_async_remote_copy(src_ref=lnd[a].at[mine], dst_ref=lnd[a].at[flat], send_sem=send2_sem.at[3 * a],
                                         recv_sem=recv_sem.at[4 * (first + a)], device_id=dev, device_id_type=MESH).wait_recv()
            for t in range(3):
                cp = pltpu.make_async_remote_copy(src_ref=lnd[a].at[targets[t + 1][1]], dst_ref=lnd[a].at[passed[t]],
                                                  send_sem=send2_sem.at[3 * a + t], recv_sem=recv2_sem.at[3 * a + t],
                                                  device_id=sibling, device_id_type=MESH)
                cp.wait_send()
                cp.wait_recv()

    res = pl.pallas_call(
        body, name=name, out_shape=[pltpu.HBM(a.shape, a.dtype) for a in lands],
        in_specs=[HBM] * n + [SEM, SEM, SEM, ANY], out_specs=[HBM] * n,
        input_output_aliases={i: i for i in range(n)},
        compiler_params=pltpu.CompilerParams(has_side_effects=EFFECT),
    )(*lands, recv, send2, recv2, after)
    return list(res)


def _adamw_decay(w, m, v):
    return ADAM_WD * w, ADAM_B1 * m, ADAM_B2 * v


def _adamw_finish(g, wd_w, m1, v1):
    m = m1 + (1.0 - ADAM_B1) * g
    v = v1 + (1.0 - ADAM_B2) * (g * g)
    m_hat = m / (1.0 - ADAM_B1 ** ADAM_STEP)
    v_hat = v / (1.0 - ADAM_B2 ** ADAM_STEP)
    delta = -ADAM_LR * (m_hat / (jnp.sqrt(v_hat) + ADAM_EPS) + wd_w)
    return delta, m, v


def _adamw(g, w, m, v):
    return _adamw_finish(g, *_adamw_decay(w, m, v))


def _update_prep(name, w, m, v, dep, w_done=False, block_bytes=1 << 20):
    _, r, c = m.shape
    tr = max(8, min(r, (block_bytes // (4 * c)) // 8 * 8))
    while r % tr:
        tr -= 8
    blk = pl.BlockSpec((None, tr, c), lambda i: (0, i, 0))
    if w_done:
        def body(m_ref, v_ref, dep_ref, om_ref, ov_ref):
            del dep_ref
            om_ref[...] = ADAM_B1 * m_ref[...]
            ov_ref[...] = ADAM_B2 * v_ref[...]

        m1, v1 = pl.pallas_call(
            body, name=name, grid=(r // tr,), in_specs=[blk] * 2 + [ANY], out_specs=[blk] * 2,
            out_shape=[_sds((1, r, c), F32)] * 2, compiler_params=_params(("parallel",)),
        )(m, v, dep)
        return w, m1, v1

    def body(w_ref, m_ref, v_ref, dep_ref, ow_ref, om_ref, ov_ref):
        del dep_ref
        ow_ref[...], om_ref[...], ov_ref[...] = _adamw_decay(w_ref[...], m_ref[...], v_ref[...])

    return pl.pallas_call(
        body, name=name, grid=(r // tr,), in_specs=[blk] * 3 + [ANY], out_specs=[blk] * 3,
        out_shape=[_sds((1, r, c), F32)] * 3, compiler_params=_params(("parallel",)),
    )(w, m, v, dep)


def _update(name, parts, w, m, v, layout=None, decayed=False, transposed_out=False, block_bytes=1 << 20):
    _, r, c = w.shape
    n_slots, _, cp = parts.shape
    tr = max(8, min(r, (block_bytes // (4 * cp)) // 8 * 8))
    if transposed_out:
        tr = _tile(r, 256)
    while r % tr:
        tr -= 8

    def body(p_ref, w_ref, m_ref, v_ref, g_ref, d_ref, nm_ref, nv_ref, *scratch):
        g = p_ref[0].astype(F32)
        for p in range(1, n_slots):
            g = g + p_ref[p].astype(F32)
        if layout is not None:
            s1, s2, lg = layout.my_shifts()
            lane = lax.broadcasted_iota(jnp.int32, g.shape, 1)
            scratch[0][...] = jnp.where(lane < lg, pltpu.roll(g, cp - s1, 1), pltpu.roll(g, cp - s2, 1))
            g = scratch[0][:, 0:c]
        step = _adamw_finish if decayed else _adamw
        results = (g,) + step(g, w_ref[...], m_ref[...], v_ref[...])
        if ragged:
            scratch[-2][...] = jnp.zeros_like(scratch[-2])
        for ref, val in zip((g_ref, d_ref, nm_ref, nv_ref), results):
            if not transposed_out:
                ref[...] = val
            elif not ragged:
                ref[...] = val.T
            else:
                wide, tall = scratch[-2], scratch[-1]
                wide[:, 0:c] = val
                tall[...] = wide[...].T
                ref[...] = tall[0:c, :]

    ragged = transposed_out and c % 8 != 0
    c_wide = -(-c // LANES) * LANES
    blk = pl.BlockSpec((None, tr, c), lambda i: (0, i, 0))
    out_blk = pl.BlockSpec((None, c, tr), lambda i: (0, 0, i)) if transposed_out else blk
    scratch_shapes = [] if layout is None else [pltpu.VMEM((tr, cp), F32)]
    if ragged:
        scratch_shapes += [pltpu.VMEM((tr, c_wide), F32), pltpu.VMEM((c_wide, tr), F32)]
    res = pl.pallas_call(
        body, name=name, grid=(r // tr,),
        in_specs=[pl.BlockSpec((n_slots, tr, cp), lambda i: (0, i, 0)), blk, blk, blk],
        out_specs=[out_blk] * 4, out_shape=[_sds((1, c, r) if transposed_out else (1, r, c), F32)] * 4,
        scratch_shapes=scratch_shapes,
        compiler_params=_params(("parallel",)),
    )(parts, w, m, v)
    return [jnp.transpose(o, (0, 2, 1)) for o in res] if transposed_out else res


def _small_update(part, w, m, v):
    n = part.shape[1]

    def body(p_ref, w_ref, m_ref, v_ref, g_ref, d_ref, nm_ref, nv_ref, buf, send, recv):
        me, peers = _mesh_place()
        buf[me] = p_ref[...]
        sent = []
        for d, dev, flat in peers:
            cp = pltpu.make_async_remote_copy(src_ref=p_ref, dst_ref=buf.at[me], send_sem=send.at[d],
                                              recv_sem=recv.at[d], device_id=dev, device_id_type=MESH)
            cp.start()
            sent.append(cp)
        for d, dev, flat in peers:
            pltpu.make_async_remote_copy(src_ref=p_ref, dst_ref=buf.at[flat], send_sem=send.at[d],
                                         recv_sem=recv.at[d], device_id=dev, device_id_type=MESH).wait_recv()
        for cp in sent:
            cp.wait_send()
        g = buf[0]
        for p in range(1, N_DEV):
            g = g + buf[p]
        g_ref[...] = g
        d_ref[...], nm_ref[...], nv_ref[...] = _adamw(g, w_ref[...], m_ref[...], v_ref[...])

    vm = pl.BlockSpec(memory_space=pltpu.VMEM)
    return pl.pallas_call(
        body, name="small_update", in_specs=[vm] * 4, out_specs=[vm] * 4, out_shape=[_sds((1, n), F32)] * 4,
        scratch_shapes=[pltpu.VMEM((N_DEV, 1, n), F32), pltpu.SemaphoreType.DMA((N_DEV,)),
                        pltpu.SemaphoreType.DMA((N_DEV,))],
    )(part, w, m, v)


class _WInLayout:
    def __init__(self, n8, n_f, d_sb, d_fox, d):
        assert n8 % LANES == 1 and n_f < LANES and d % (N_DEV * LANES) == 0
        self.n8, self.n_f, self.d = n8, n_f, d
        self.sp = n8 // LANES
        self.wp = (n8 + 2 * LANES - 2) // LANES * LANES
        self.n_qkv = 3 * (d_sb + d_fox)
        nq, dt, tc = self.n_qkv // LANES, d // LANES, d // N_DEV // LANES
        h_sb, h_fox = d_sb // HEAD_DIM, d_fox // HEAD_DIM
        self.sources = {}
        self.part_tile = {}
        for p in range(N_DEV):
            lg = min(max(self.n_qkv + n_f - n8 * p, 0), n8)
            s1, s2 = p, p + LANES - n_f
            spans = []
            if lg > 0:
                spans.append(("a", self.sp * p, s1 // LANES, (lg + s1 - 1) // LANES))
            if lg < n8:
                spans.append(("g", self.sp * p - 1 - nq, (lg + s2) // LANES, (n8 - 1 + s2) // LANES))
            for kind, base, first, last in spans:
                for i in range(first, last + 1):
                    assert (p, i) not in self.part_tile
                    self.part_tile[(p, i)] = (kind, base + i)
                    self.sources.setdefault((kind, base + i), []).append((p, i))
        self.cat_tiles = [("a", r * h_sb + h) for h in range(h_sb) for r in range(3)]
        self.cat_tiles += [("a", 3 * h_sb + r * h_fox + h) for h in range(h_fox) for r in range(3)]
        self.cat_tiles += [("g", which * dt + j * tc + half) for j in range(N_DEV) for which in (0, 1) for half in range(tc)]
        self.cat_tiles += [("a", nq)] + [None] * (F_PAD // LANES - 1)
        self.cat_index = {key: c for c, key in enumerate(self.cat_tiles) if key is not None}

    def my_shifts(self):
        me = _flat_me()
        return me, me + LANES - self.n_f, jnp.clip(self.n_qkv + self.n_f - self.n8 * me, 0, self.n8)


def _lane_tile(i):
    return pl.ds(i * LANES, LANES)


def _w_in_shift(w_in, lay, tr=256):
    _, d, n8 = w_in.shape
    kd = d // LANES
    kt = tr // LANES
    by_col = jnp.transpose(w_in, (0, 2, 1)).reshape(n8 * kd, LANES)

    def body(w_ref, o_ref, wd_ref, buf):
        k0 = kt * pl.program_id(0)
        buf[...] = jnp.zeros_like(buf)
        for j in range(n8 // LANES):
            for kk in range(kt):
                piece = w_ref[pl.ds(j * LANES * kd + k0 + kk, LANES, stride=kd), :]
                buf[kk * LANES:(kk + 1) * LANES, j * LANES:(j + 1) * LANES] = piece.T
        first = lax.broadcasted_iota(jnp.int32, (8, LANES), 0) == 0
        for kk in range(kt):
            row = w_ref[pl.ds((n8 - 1) * kd + k0 + kk, 1), :]
            buf[kk * LANES:(kk + 1) * LANES, n8 - 1:n8 + 7] = jnp.where(first, jnp.broadcast_to(row, (8, LANES)), 0.0).T
        wd_ref[...] = ADAM_WD * buf[:, 0:n8]
        v = buf[...]
        s1, s2, lg = lay.my_shifts()
        pos = lax.broadcasted_iota(jnp.int32, v.shape, 1)
        o_ref[...] = jnp.where(pos < lg + s1, pltpu.roll(v, s1, 1),
                               jnp.where(pos >= lg + s2, pltpu.roll(v, s2, 1), 0.0)).astype(BF16)

    return pl.pallas_call(
        body, name="w_in_shift", grid=(d // tr,),
        in_specs=[pl.BlockSpec((n8 * kd, LANES), lambda i: (0, 0))],
        out_specs=[pl.BlockSpec((tr, lay.wp), lambda i: (i, 0)), pl.BlockSpec((None, tr, n8), lambda i: (0, i, 0))],
        out_shape=[_sds((d, lay.wp), BF16), _sds((1, d, n8), F32)],
        scratch_shapes=[pltpu.VMEM((tr, lay.wp), F32)],
        compiler_params=_params(("arbitrary",)),
    )(by_col)


def _w_in_build(g_in, lay, tr=256):
    d = g_in.shape[1]
    width = len(lay.cat_tiles) * LANES

    def body(g_ref, o_ref):
        for c, key in enumerate(lay.cat_tiles):
            if key is None:
                o_ref[:, _lane_tile(c)] = jnp.zeros((tr, LANES), BF16)
                continue
            (p, i), *more = lay.sources[key]
            val = g_ref[p, :, _lane_tile(i)]
            for p2, i2 in more:
                val = val + g_ref[p2, :, _lane_tile(i2)]
            o_ref[:, _lane_tile(c)] = val

    return pl.pallas_call(
        body, name="w_in_build", grid=(d // tr,),
        in_specs=[pl.BlockSpec((N_DEV, tr, lay.wp), lambda i: (0, i, 0))],
        out_specs=pl.BlockSpec((tr, width), lambda i: (i, 0)), out_shape=_sds((d, width), BF16),
        compiler_params=_params(("parallel",)),
    )(g_in)


def _w_in_grad_parts(dwq, dwgf, lay, tr=256):
    d = dwq.shape[0]
    nq = lay.n_qkv // LANES

    def body(q_ref, g_ref, o_ref):
        for p in range(N_DEV):
            for i in range(lay.wp // LANES):
                key = lay.part_tile.get((p, i))
                if key is None:
                    o_ref[p, :, _lane_tile(i)] = jnp.zeros((tr, LANES), BF16)
                    continue
                c = lay.cat_index[key]
                o_ref[p, :, _lane_tile(i)] = q_ref[:, _lane_tile(c)] if c < nq else g_ref[:, _lane_tile(c - nq)]

    return pl.pallas_call(
        body, name="w_in_grad_parts", grid=(d // tr,),
        in_specs=[pl.BlockSpec((tr, dwq.shape[1]), lambda i: (i, 0)), pl.BlockSpec((tr, dwgf.shape[1]), lambda i: (i, 0))],
        out_specs=pl.BlockSpec((N_DEV, tr, lay.wp), lambda i: (0, i, 0)), out_shape=_sds((N_DEV, d, lay.wp), BF16),
        compiler_params=_params(("parallel",)),
    )(dwq, dwgf)


def kernel(x, norm_mix_pre, norm_mix_post, w_in, b_forget, w_branch_sb, w_branch_fox, w_out, norm_ffn_pre, norm_ffn_post, w_ffn_gate, w_ffn_up, w_ffn_down, loss_target, m_norm_mix_pre, m_norm_mix_post, m_w_in, m_b_forget, m_w_branch_sb, m_w_branch_fox, m_w_out, m_norm_ffn_pre, m_norm_ffn_post, m_w_ffn_gate, m_w_ffn_up, m_w_ffn_down, v_norm_mix_pre, v_norm_mix_post, v_w_in, v_b_forget, v_w_branch_sb, v_w_branch_fox, v_w_out, v_norm_ffn_pre, v_norm_ffn_post, v_w_ffn_gate, v_w_ffn_up, v_w_ffn_down):
    xs, target = x[0], loss_target[0]
    s, d = xs.shape
    d_sb, d_fox = w_branch_sb.shape[1], w_branch_fox.shape[1]
    h_sb, h_fox = d_sb // HEAD_DIM, d_fox // HEAD_DIM
    n_f = b_forget.shape[1]
    fs = w_ffn_gate.shape[2]
    cs = d // N_DEV
    n_qkv = 3 * (d_sb + d_fox)
    n_gf = 2 * d + F_PAD
    f_blk = 2 * d // LANES
    big = (w_in, w_branch_sb, w_branch_fox, w_out, w_ffn_gate, w_ffn_up, w_ffn_down)
    big_m = (m_w_in, m_w_branch_sb, m_w_branch_fox, m_w_out, m_w_ffn_gate, m_w_ffn_up, m_w_ffn_down)
    big_v = (v_w_in, v_w_branch_sb, v_w_branch_fox, v_w_out, v_w_ffn_gate, v_w_ffn_up, v_w_ffn_down)

    lay = _WInLayout(w_in.shape[2], n_f, d_sb, d_fox, d)
    w_in_shifted, wd_w_in = _w_in_shift(w_in, lay)
    send1, recv1, lands, token = _gather_start([w_in_shifted] + [w[0].astype(BF16) for w in big[1:]])
    b_pad = jnp.pad(b_forget, ((0, 0), (0, LANES - n_f)))

    started = token[0, 0]
    u, u_t = _pre_norm(xs, norm_mix_pre, dep=token)
    weights = dict(zip(("w_in", "w_branch_sb", "w_branch_fox", "w_out", "w_ffn_gate", "w_ffn_up", "w_ffn_down"),
                       zip(big, big_m, big_v)))
    decayed = {nm: _update_prep("decay_" + nm, *[t + started for t in weights[nm]], u)
               for nm in ("w_ffn_gate", "w_ffn_up")}
    decayed["w_in"] = _update_prep("decay_w_in", wd_w_in, m_w_in + started, v_w_in + started, u, w_done=True)
    l_in, send2, recv2, token = _gather_forward("gather_in_forward", lands[0:1], 0, send1, recv1,
                                                [u] + [t[2] for t in decayed.values()])
    (g_in,) = _gather_wait("gather_in_wait", l_in, 0, recv1, send2, recv2, token)
    w_cat = _w_in_build(g_in, lay)
    qkv = _mm_plain("proj_qkv", "nn", u, w_cat, BF16, n=n_qkv)
    gf = _mm_plain("proj_gates", "nn", u, w_cat, F32, n_off=n_qkv, n=n_gf)
    cum_col, cum_row = _forget_fwd(gf, b_pad, f_blk)
    o_sb, o_sb_t, tot = _sb_fwd(qkv, h_sb)
    l_mid, send2, recv2, token = _gather_forward("gather_mid_forward", lands[1:4], 1, send1, recv1, [o_sb])
    o_fx, o_fx_t, o_fx32, lse = _fox_fwd(qkv, cum_col, cum_row, h_fox, h_sb, token)
    g_sb, g_fx, g_out = _gather_wait("gather_mid_wait", l_mid, 1, recv1, send2, recv2, o_fx)
    w_out_full = g_out.reshape(d, d)
    merged, merged_t, a_sb, a_fx = _branch_merge(o_sb, o_fx, g_sb, g_fx, gf, o_fx)
    l_ffn, send2, recv2, token = _gather_forward("gather_ffn_forward", lands[4:6], 4, send1, recv1, [merged])
    mix = _mm_plain("out_proj", "nn", merged, w_out_full, F32, dep=token)
    h1, u2, u2_t = _mid_norms(xs, mix, norm_mix_post, norm_ffn_pre)
    g_gate, g_up = _gather_wait("gather_ffn_wait", l_ffn, 4, recv1, send2, recv2, u2)
    l_down, send2, recv2, token = _gather_forward("gather_down_forward", lands[6:7], 6, send1, recv1, [u2])
    gate, up, act, act_t = _ffn_up(u2, g_gate, g_up, token)
    (g_down,) = _gather_wait("gather_down_wait", l_down, 6, recv1, send2, recv2, act)
    tm, tn = _tile(s, 1024), _tile(d, 1024)
    ff = _matmul("ffn_down", "nn",
                 [(act, pl.BlockSpec((None, tm, fs), lambda i, j, k: (k, i, 0)),
                   g_down, pl.BlockSpec((None, fs, tn), lambda i, j, k: (k, 0, j)))],
                 (s // tm, d // tn, N_DEV), (tm, tn), _sds((s, d), F32), pl.BlockSpec((tm, tn), lambda i, j, k: (i, j)))
    loss_part, dy, dff, dg_ffn_post = _loss_head(h1, ff, target, norm_ffn_post)

    dgate, dup = _ffn_down_bwd(dff, g_down, gate, up)
    dw_down = _matmul("dw_down", "nn",
                      [(act_t, pl.BlockSpec((None, fs, s), lambda j, n, k: (j, 0, 0)),
                        dff, pl.BlockSpec((s, tn), lambda j, n, k: (0, n)))],
                      (N_DEV, d // tn, 1), (fs, tn), _sds((N_DEV, fs, d), BF16),
                      pl.BlockSpec((None, fs, tn), lambda j, n, k: (j, 0, n)))

    def dw_up(name, dact):
        return _matmul(name, "nn",
                       [(u2_t, pl.BlockSpec((tn, s), lambda j, i, k: (i, 0)),
                         dact, pl.BlockSpec((None, s, fs), lambda j, i, k: (j, 0, 0)))],
                       (N_DEV, d // tn, 1), (tn, fs), _sds((N_DEV, d, fs), BF16),
                       pl.BlockSpec((None, tn, fs), lambda j, i, k: (j, i, 0)))

    dw_gate, dw_upw = dw_up("dw_gate", dgate), dw_up("dw_up", dup)
    rs_ffn = _scatter_pairs("ffn", [dw_gate, dw_upw, dw_down])
    a_spec = pl.BlockSpec((None, tm, fs), lambda i, j, k: (k, i, 0))
    b_spec = pl.BlockSpec((None, tn, fs), lambda i, j, k: (k, j, 0))
    du2 = _matmul("du2", "nt", [(dgate, a_spec, g_gate, b_spec), (dup, a_spec, g_up, b_spec)],
                  (s // tm, d // tn, N_DEV), (tm, tn), _sds((s, d), F32), pl.BlockSpec((tm, tn), lambda i, j, k: (i, j)),
                  dep=rs_ffn[4])
    rs_ffn = _scatter_chips("ffn", rs_ffn, du2)
    dh1, dmix, dg_ffn_pre, dg_mix_post = _mid_norms_bwd(dy, du2, h1, mix, norm_ffn_pre, norm_mix_post)

    da_sb, da_fx, dgf = _merge_bwd(dmix, w_out_full, gf, a_sb, a_fx, dep=rs_ffn[4])
    dw_out = _mm_plain("dw_out", "nn", merged_t, dmix, BF16).reshape(N_DEV, cs, d)

    def branch_bwd(tag, da, w_b, o_t, width):
        tb = _tile(width, 1024)
        do = _matmul("do_" + tag, "nt",
                     [(da, pl.BlockSpec((tm, cs), lambda i, j, k: (i, k)),
                       w_b, pl.BlockSpec((None, tb, cs), lambda i, j, k: (k, j, 0)))],
                     (s // tm, width // tb, N_DEV), (tm, tb), _sds((s, width), BF16),
                     pl.BlockSpec((tm, tb), lambda i, j, k: (i, j)))
        dw = _matmul("dw_" + tag, "nn",
                     [(o_t, pl.BlockSpec((width, s), lambda j, i, k: (0, 0)),
                       da, pl.BlockSpec((s, cs), lambda j, i, k: (0, j)))],
                     (N_DEV, 1, 1), (width, cs), _sds((N_DEV, width, cs), BF16),
                     pl.BlockSpec((None, width, cs), lambda j, i, k: (j, 0, 0)))
        return do, dw

    do_sb, dw_sb = branch_bwd("sb", da_sb, g_sb, o_sb_t, d_sb)
    do_fx, dw_fx = branch_bwd("fox", da_fx, g_fx, o_fx_t, d_fox)

    rs_mid = _scatter_pairs("mid", [dw_sb, dw_fx, dw_out])

    dqkv = _sb_bwd(qkv, do_sb, tot, h_sb, rs_mid[4])
    rs_mid = _scatter_chips("mid", rs_mid, dqkv)
    dqkv, dcum = _fox_bwd(dqkv, qkv, do_fx, o_fx32, lse, cum_col, cum_row, h_fox, h_sb, rs_mid[4])
    dgf, db_part = _forget_bwd(dgf, dcum, gf, b_pad, f_blk)
    dw_in = _w_in_grad_parts(_mm_plain("dw_qkv", "nn", u_t, dqkv, BF16), _mm_plain("dw_gates", "nn", u_t, dgf, BF16), lay)
    upd = {}

    def update_group(tag, rs, names, after):
        parts = _scatter_end(tag, rs, after)
        for nm, p in zip(names, parts):
            w, m, v = decayed.get(nm, weights[nm])
            upd[nm] = _update("update_" + nm, p, w, m, v, layout=lay if nm == "w_in" else None, decayed=nm in decayed,
                              transposed_out=nm in ("w_in", "w_ffn_gate", "w_ffn_up"))

    rs_in = _scatter_pairs("in", [dw_in])
    update_group("ffn", rs_ffn, ("w_ffn_gate", "w_ffn_up", "w_ffn_down"), [rs_in[4]])
    rs_in = _scatter_chips("in", rs_in, [upd[nm][3] for nm in ("w_ffn_gate", "w_ffn_up", "w_ffn_down")])
    du = _mm_plain("du_qkv", "nt", dqkv, w_cat, F32, tn=1024, dep=rs_in[4])
    du = _mm_plain("du_gates", "nt", dgf, w_cat, F32, tn=1024, k_off=n_qkv, init=du)
    dx, dg_mix_pre = _pre_norm_bwd(dh1, du, xs, norm_mix_pre)
    update_group("mid", rs_mid, ("w_branch_sb", "w_branch_fox", "w_out"), [dx])
    update_group("in", rs_in, ("w_in",), [upd[nm][3] for nm in ("w_branch_sb", "w_branch_fox", "w_out")])

    small = ((norm_mix_pre, m_norm_mix_pre, v_norm_mix_pre), (norm_mix_post, m_norm_mix_post, v_norm_mix_post),
             (norm_ffn_pre, m_norm_ffn_pre, v_norm_ffn_pre), (norm_ffn_post, m_norm_ffn_post, v_norm_ffn_post))
    pad_f = ((0, 0), (0, LANES - n_f))
    cat = lambda i: jnp.concatenate([t[i] for t in small] + [jnp.pad((b_forget, m_b_forget, v_b_forget)[i], pad_f)], axis=1)
    sm = _small_update(jnp.concatenate([dg_mix_pre, dg_mix_post, dg_ffn_pre, dg_ffn_post, db_part], axis=1),
                       cat(0), cat(1), cat(2))
    for i, nm in enumerate(("norm_mix_pre", "norm_mix_post", "norm_ffn_pre", "norm_ffn_post")):
        upd[nm] = [o[:, i * d:(i + 1) * d] for o in sm]
    upd["b_forget"] = [o[:, 4 * d:4 * d + n_f] for o in sm]

    loss = lax.psum(loss_part[0, 0], ("x", "y", "c"))
    order = ("norm_mix_pre", "norm_mix_post", "w_in", "b_forget", "w_branch_sb", "w_branch_fox", "w_out",
             "norm_ffn_pre", "norm_ffn_post", "w_ffn_gate", "w_ffn_up", "w_ffn_down")
    return (loss, dx[None]) + tuple(upd[nm][i] for i in range(4) for nm in order)
```

```python
import jax
import jax.numpy as jnp
from jax import lax
from jax.experimental import pallas as pl
from jax.experimental.pallas import tpu as pltpu

F32 = jnp.float32
BF16 = jnp.bfloat16
MESH = pl.DeviceIdType.MESH
ANY = pl.BlockSpec(memory_space=pl.ANY)
HBM = pl.BlockSpec(memory_space=pltpu.HBM)
SEM = pl.BlockSpec(memory_space=pltpu.SEMAPHORE)
EFFECT = pltpu.SideEffectType.DATAFLOW_SIDE_EFFECTING

N_DEV = 8
HEAD_DIM = 128
RMS_EPS = 1e-6
F_PAD = 512
LANES = 128
ATT_TQ = 256
ATT_TK = 256
ATT_HP = 4
NEG_BIG = -1e30
VMEM_LIMIT = 56 * 1024 * 1024

ADAM_LR = 0.001
ADAM_B1 = 0.9
ADAM_B2 = 0.999
ADAM_EPS = 1e-08
ADAM_WD = 0.01
ADAM_STEP = 10

_DIMS = {"nn": ((1,), (0,)), "nt": ((1,), (1,)), "tn": ((0,), (0,))}


def _params(sem):
    return pltpu.CompilerParams(dimension_semantics=sem, vmem_limit_bytes=VMEM_LIMIT)


def _dot(a, b, mode="nn"):
    return lax.dot_general(a.astype(BF16), b.astype(BF16), (_DIMS[mode], ((), ())), preferred_element_type=F32)


def _tile(n, pref):
    if n <= pref:
        return n
    t = (pref // LANES) * LANES
    while n % t:
        t -= LANES
    return t


def _split2(v):
    hi = v.astype(BF16)
    return hi, (v - hi.astype(F32)).astype(BF16)


def _split3(v):
    a = v.astype(BF16)
    r = v - a.astype(F32)
    b = r.astype(BF16)
    return a, b, (r - b.astype(F32)).astype(BF16)


def _tri(n, cmp):
    r = lax.broadcasted_iota(jnp.int32, (n, n), 0)
    c = lax.broadcasted_iota(jnp.int32, (n, n), 1)
    return jnp.where(cmp(r, c), 1.0, 0.0).astype(BF16)


def _lane_pick(v, h):
    lane = lax.broadcasted_iota(jnp.int32, v.shape, 1)
    return jnp.sum(jnp.where(lane == h, v, 0.0), axis=1, keepdims=True)


def _lane_put(ref, rows, h, col):
    old = ref[rows, :]
    lane = lax.broadcasted_iota(jnp.int32, old.shape, 1)
    ref[rows, :] = jnp.where(lane == h, col, old)


def _sigmoid(z):
    return 1.0 / (1.0 + jnp.exp(-z))


def _log_sigmoid(z):
    return jnp.minimum(z, 0.0) - jnp.log(1.0 + jnp.exp(-jnp.abs(z)))


def _sds(shape, dtype):
    return jax.ShapeDtypeStruct(shape, dtype)


def _matmul(name, mode, pairs, grid, acc_shape, out_shape, out_specs, extras=(), epilogue=None, init=None, dep=None):
    n_p, n_e = len(pairs), len(extras)
    nk = grid[-1]
    single = not isinstance(out_shape, (list, tuple))
    n_i = 0 if init is None else 1
    n_d = 0 if dep is None else 1

    one_step = nk == 1 and init is None

    def body(*refs):
        ab = refs[:2 * n_p]
        ex = refs[2 * n_p:2 * n_p + n_e]
        ini = refs[2 * n_p + n_e:2 * n_p + n_e + n_i]
        outs = refs[2 * n_p + n_e + n_i + n_d:len(refs) - (0 if one_step else 1)]

        def finish(total):
            if epilogue is None:
                outs[0][...] = total.astype(outs[0].dtype)
            else:
                epilogue(total, ex, outs)

        t = _dot(ab[0][...], ab[1][...], mode)
        for p in range(1, n_p):
            t = t + _dot(ab[2 * p][...], ab[2 * p + 1][...], mode)
        if one_step:
            finish(t)
            return
        acc = refs[-1]
        k = pl.program_id(len(grid) - 1)

        @pl.when(k == 0)
        def _():
            acc[...] = t if init is None else ini[0][...].astype(F32) + t

        @pl.when(k > 0)
        def _():
            acc[...] += t

        @pl.when(k == nk - 1)
        def _():
            finish(acc[...])

    in_specs = [s for (_, sa, _, sb) in pairs for s in (sa, sb)] + [s for (_, s) in extras]
    args = [v for (a, _, b, _) in pairs for v in (a, b)] + [e for (e, _) in extras]
    if init is not None:
        in_specs.append(init[1])
        args.append(init[0])
    if dep is not None:
        in_specs.append(ANY)
        args.append(dep)
    return pl.pallas_call(
        body, name=name, grid=grid, in_specs=in_specs,
        out_specs=out_specs if single else list(out_specs),
        out_shape=out_shape if single else list(out_shape),
        scratch_shapes=[] if one_step else [pltpu.VMEM(acc_shape, F32)],
        compiler_params=_params(("parallel",) * (len(grid) - 1) + ("arbitrary",)),
    )(*args)


def _mm_plain(name, mode, a, b, out_dtype, *, n_off=0, n=None, k_off=0, tm=1024, tn=1536, tk=2048, init=None, dep=None):
    if mode == "nn":
        (m, kk), nn_ = a.shape, b.shape[1]
    elif mode == "nt":
        (m, kk), nn_ = a.shape, b.shape[0]
    else:
        (kk, m), nn_ = a.shape, b.shape[1]
    n = nn_ if n is None else n
    tm, tn, tk = _tile(m, tm), _tile(n, tn), _tile(kk, tk)
    while n_off % tn or n % tn:
        tn -= LANES
    while k_off % tk or kk % tk:
        tk -= LANES
    off, koff = n_off // tn, k_off // tk
    a_spec = {"nn": pl.BlockSpec((tm, tk), lambda i, j, k: (i, k)),
              "nt": pl.BlockSpec((tm, tk), lambda i, j, k: (i, k)),
              "tn": pl.BlockSpec((tk, tm), lambda i, j, k: (k, i))}[mode]
    b_spec = {"nn": pl.BlockSpec((tk, tn), lambda i, j, k: (k, j + off)),
              "nt": pl.BlockSpec((tn, tk), lambda i, j, k: (j, k + koff)),
              "tn": pl.BlockSpec((tk, tn), lambda i, j, k: (k, j))}[mode]
    o_spec = pl.BlockSpec((tm, tn), lambda i, j, k: (i, j))
    if init is not None:
        init = (init, o_spec)
    return _matmul(name, mode, [(a, a_spec, b, b_spec)], (m // tm, n // tn, kk // tk), (tm, tn),
                   _sds((m, n), out_dtype), o_spec, init=init, dep=dep)


def _rows_call(name, body, ins, outs, s, tr=256, dep=None):
    def spec(v, per_row):
        if per_row == "transposed":
            return pl.BlockSpec((v.shape[0], tr), lambda i: (0, i))
        if per_row:
            return pl.BlockSpec((tr, v.shape[1]), lambda i: (i, 0))
        return pl.BlockSpec(v.shape, lambda i: (0, 0))
    n_in = len(ins)
    deps = [] if dep is None else [dep]

    def with_dep(*refs):
        body(*refs[:n_in], *refs[n_in + len(deps):])

    return pl.pallas_call(
        with_dep, name=name, grid=(s // tr,),
        in_specs=[spec(v, p) for v, p in ins] + [ANY] * len(deps), out_specs=[spec(v, p) for v, p in outs],
        out_shape=[_sds(v.shape, v.dtype) for v, _ in outs],
        compiler_params=_params(("arbitrary",)),
    )(*[v for v, _ in ins], *deps)


def _rsq(v):
    return lax.rsqrt(jnp.mean(v * v, axis=-1, keepdims=True) + RMS_EPS)


def _norm_bwd(dy, v, r, g):
    vh = v * r
    t = dy * g
    dv = r * (t - vh * jnp.mean(t * vh, axis=-1, keepdims=True))
    return dv, jnp.sum(dy * vh, axis=0, keepdims=True)


def _accum(ref, val):
    @pl.when(pl.program_id(0) == 0)
    def _():
        ref[...] = jnp.zeros_like(ref)
    ref[...] += val


def _pre_norm(x, g, dep=None):
    def body(x_ref, g_ref, u_ref, ut_ref):
        v = x_ref[...]
        u = (v * _rsq(v) * g_ref[...]).astype(BF16)
        u_ref[...] = u
        ut_ref[...] = u.T
    s, d = x.shape
    return _rows_call("pre_norm", body, [(x, True), (g, False)],
                      [(_sds((s, d), BF16), True), (_sds((d, s), BF16), "transposed")], s, dep=dep)


def _mid_norms(x, mix, g_post, g_pre):
    def body(x_ref, mix_ref, gp_ref, gn_ref, h_ref, u_ref, ut_ref):
        mv = mix_ref[...]
        h = x_ref[...] + mv * _rsq(mv) * gp_ref[...]
        h_ref[...] = h
        u = (h * _rsq(h) * gn_ref[...]).astype(BF16)
        u_ref[...] = u
        ut_ref[...] = u.T
    s, d = x.shape
    return _rows_call("mid_norms", body, [(x, True), (mix, True), (g_post, False), (g_pre, False)],
                      [(_sds((s, d), F32), True), (_sds((s, d), BF16), True), (_sds((d, s), BF16), "transposed")], s)


def _loss_head(h1, ff, target, g):
    s, d = h1.shape

    def body(h_ref, ff_ref, t_ref, g_ref, loss_ref, dy_ref, dff_ref, dg_ref):
        fv = ff_ref[...]
        r = _rsq(fv)
        err = h_ref[...] + fv * r * g_ref[...] - t_ref[...]
        part = 0.5 * jnp.sum(jnp.mean(err * err, axis=-1, keepdims=True), axis=0, keepdims=True)
        _accum(loss_ref, jnp.broadcast_to(part, loss_ref.shape))
        dy = err * (1.0 / d)
        dy_ref[...] = dy
        dff, dg = _norm_bwd(dy, fv, r, g_ref[...])
        dff_ref[...] = dff.astype(BF16)
        _accum(dg_ref, dg)

    return _rows_call("loss_head", body, [(h1, True), (ff, True), (target, True), (g, False)],
                      [(_sds((1, LANES), F32), False), (_sds((s, d), F32), True),
                       (_sds((s, d), BF16), True), (_sds((1, d), F32), False)], s)


def _mid_norms_bwd(dy, du2, h1, mix, g_pre, g_post):
    s, d = dy.shape

    def body(dy_ref, du_ref, h_ref, mix_ref, gn_ref, gp_ref, dh_ref, dmix_ref, dgn_ref, dgp_ref):
        h = h_ref[...]
        dh, dgn = _norm_bwd(du_ref[...], h, _rsq(h), gn_ref[...])
        dh = dh + dy_ref[...]
        dh_ref[...] = dh
        _accum(dgn_ref, dgn)
        mv = mix_ref[...]
        dmix, dgp = _norm_bwd(dh, mv, _rsq(mv), gp_ref[...])
        dmix_ref[...] = dmix.astype(BF16)
        _accum(dgp_ref, dgp)

    return _rows_call("mid_norms_bwd", body,
                      [(dy, True), (du2, True), (h1, True), (mix, True), (g_pre, False), (g_post, False)],
                      [(_sds((s, d), F32), True), (_sds((s, d), BF16), True),
                       (_sds((1, d), F32), False), (_sds((1, d), F32), False)], s)


def _pre_norm_bwd(dh1, du, x, g, dep=None):
    s, d = x.shape

    def body(dh_ref, du_ref, x_ref, g_ref, dx_ref, dg_ref):
        v = x_ref[...]
        dv, dg = _norm_bwd(du_ref[...], v, _rsq(v), g_ref[...])
        dx_ref[...] = dh_ref[...] + dv
        _accum(dg_ref, dg)

    return _rows_call("pre_norm_bwd", body, [(dh1, True), (du, True), (x, True), (g, False)],
                      [(_sds((s, d), F32), True), (_sds((1, d), F32), False)], s, dep=dep)


def _forget_fwd(gf, b_pad, f_blk):
    s = gf.shape[0]
    tb = ATT_TK
    nb = s // tb

    def body(f_ref, b_ref, col_ref, row_ref):
        incl = _tri(tb, lambda r, c: c <= r)
        carry = jnp.zeros((1, LANES), F32)
        for i in range(nb):
            lf = _log_sigmoid(f_ref[pl.ds(i * tb, tb), :] + b_ref[...])
            parts = _split3(lf)
            cum = carry + _dot(incl, parts[0]) + _dot(incl, parts[1]) + _dot(incl, parts[2])
            col_ref[pl.ds(i * tb, tb), :] = cum
            row_ref[i] = cum.T
            carry = carry + jnp.sum(lf, axis=0, keepdims=True)

    return pl.pallas_call(
        body, name="forget_fwd", grid=(1,),
        in_specs=[pl.BlockSpec((s, LANES), lambda i: (0, f_blk)), pl.BlockSpec((1, LANES), lambda i: (0, 0))],
        out_specs=[pl.BlockSpec((s, LANES), lambda i: (0, 0)), pl.BlockSpec((nb, LANES, tb), lambda i: (0, 0, 0))],
        out_shape=[_sds((s, LANES), F32), _sds((nb, LANES, tb), F32)],
        compiler_params=_params(("arbitrary",)),
    )(gf, b_pad)


def _forget_bwd(dgf, dcum, gf, b_pad, f_blk):
    s = gf.shape[0]
    tb = ATT_TK
    nb = s // tb
    sec = dgf.shape[1] // F_PAD - 1

    def body(dgf_hbm, dc_ref, f_ref, b_ref, out_ref, db_ref):
        del dgf_hbm
        incl = _tri(tb, lambda r, c: c >= r)
        carry = jnp.zeros((1, LANES), F32)
        db = jnp.zeros((1, LANES), F32)
        out_ref[...] = jnp.zeros_like(out_ref)
        for i in reversed(range(nb)):
            dc = dc_ref[pl.ds(i * tb, tb), :]
            parts = _split3(dc)
            dlf = carry + _dot(incl, parts[0]) + _dot(incl, parts[1]) + _dot(incl, parts[2])
            z = f_ref[pl.ds(i * tb, tb), :] + b_ref[...]
            df = dlf * _sigmoid(-z)
            out_ref[pl.ds(i * tb, tb), pl.ds(0, LANES)] = df.astype(BF16)
            db = db + jnp.sum(df, axis=0, keepdims=True)
            carry = carry + jnp.sum(dc, axis=0, keepdims=True)
        db_ref[...] = db

    return pl.pallas_call(
        body, name="forget_bwd", grid=(1,),
        in_specs=[ANY, pl.BlockSpec((s, LANES), lambda i: (0, 0)),
                  pl.BlockSpec((s, LANES), lambda i: (0, f_blk)), pl.BlockSpec((1, LANES), lambda i: (0, 0))],
        out_specs=[pl.BlockSpec((s, F_PAD), lambda i: (0, sec)), pl.BlockSpec((1, LANES), lambda i: (0, 0))],
        out_shape=[_sds(dgf.shape, BF16), _sds((1, LANES), F32)],
        input_output_aliases={0: 0},
        compiler_params=_params(("arbitrary",)),
    )(dgf, dcum, gf, b_pad)


def _diag_mask(strict):
    r = lax.broadcasted_iota(jnp.int32, (ATT_TQ, ATT_TK), 0)
    c = lax.broadcasted_iota(jnp.int32, (ATT_TQ, ATT_TK), 1)
    return c < r if strict else c <= r


def _qkv_specs(hb0, s):
    specs = []
    for j in range(ATT_HP):
        def col(g, j=j):
            return 3 * (hb0 + ATT_HP * g + j)
        specs += [pl.BlockSpec((ATT_TQ, HEAD_DIM), lambda g, i, col=col: (i, col(g))),
                  pl.BlockSpec((s, HEAD_DIM), lambda g, i, col=col: (0, col(g) + 1)),
                  pl.BlockSpec((s, HEAD_DIM), lambda g, i, col=col: (0, col(g) + 2))]
    return specs


def _head_cols(j):
    return pl.ds(j * HEAD_DIM, HEAD_DIM)


def _sb_fwd(qkv, n_heads):
    s = qkv.shape[0]
    scale = HEAD_DIM ** -0.5
    tq, tk = ATT_TQ, ATT_TK
    heads = range(ATT_HP)

    def body(*refs):
        qkv_refs, (o_ref, ot_ref, tot_ref) = refs[:3 * ATT_HP], refs[3 * ATT_HP:]
        g, i = pl.program_id(0), pl.program_id(1)

        @pl.when((g == 0) & (i == 0))
        def _():
            tot_ref[...] = jnp.zeros_like(tot_ref)

        qs = [qkv_refs[3 * j][...] for j in heads]
        upper = _tri(tk, lambda r, c: r > c)

        def tile(kj, carry, mask):
            rows = pl.ds(pl.multiple_of(kj * tk, tk), tk)
            z = [_dot(qs[j], qkv_refs[3 * j + 1][rows, :], "nt") * scale for j in heads]
            lsz = [_log_sigmoid(z[j]) for j in heads]
            lk = [lsz[j] - z[j] if mask is None else jnp.where(mask, lsz[j] - z[j], 0.0) for j in heads]
            parts = [_split2(lk[j]) for j in heads]
            above = [carry[j][0] + _dot(parts[j][0], upper) + _dot(parts[j][1], upper) for j in heads]
            w = [jnp.exp(lsz[j] + above[j]) for j in heads]
            if mask is not None:
                w = [jnp.where(mask, w[j], 0.0) for j in heads]
            return tuple((carry[j][0] + jnp.sum(lk[j], axis=1, keepdims=True),
                          carry[j][1] + _dot(w[j], qkv_refs[3 * j + 2][rows, :])) for j in heads)

        carry = tile(i, tuple((jnp.zeros((tq, 1), F32), jnp.zeros((tq, HEAD_DIM), F32)) for _ in heads), _diag_mask(True))
        carry = lax.fori_loop(0, i, lambda n, cr: tile(i - 1 - n, cr, None), carry)
        q_rows = pl.ds(pl.multiple_of(i * tq, tq), tq)
        for j in heads:
            c, acc = carry[j]
            o = acc.astype(BF16)
            o_ref[:, _head_cols(j)] = o
            ot_ref[_head_cols(j), :] = o.T
            _lane_put(tot_ref, q_rows, ATT_HP * g + j, c)

    wide = ATT_HP * HEAD_DIM
    return pl.pallas_call(
        body, name="sb_fwd", grid=(n_heads // ATT_HP, s // tq),
        in_specs=_qkv_specs(0, s),
        out_specs=[pl.BlockSpec((tq, wide), lambda g, i: (i, g)), pl.BlockSpec((wide, tq), lambda g, i: (g, i)),
                   pl.BlockSpec((s, LANES), lambda g, i: (0, 0))],
        out_shape=[_sds((s, n_heads * HEAD_DIM), BF16), _sds((n_heads * HEAD_DIM, s), BF16), _sds((s, LANES), F32)],
        compiler_params=_params(("arbitrary", "arbitrary")),
    )(*[qkv] * (3 * ATT_HP))


def _sb_bwd(qkv, do, tot, n_heads, dep):
    s = qkv.shape[0]
    scale = HEAD_DIM ** -0.5
    tq, tk = ATT_TQ, ATT_TK
    nq = s // tq
    hd = HEAD_DIM

    heads = range(ATT_HP)

    def body(*refs):
        qkv_refs = refs[:3 * ATT_HP]
        do_ref, tot_ref, _, out_ref, dk_acc, dv_acc = refs[3 * ATT_HP:]
        g, i = pl.program_id(0), pl.program_id(1)

        @pl.when(i == 0)
        def _():
            dk_acc[...] = jnp.zeros_like(dk_acc)
            dv_acc[...] = jnp.zeros_like(dv_acc)

        qs = [qkv_refs[3 * j][...] for j in heads]
        douts = [do_ref[:, _head_cols(j)] for j in heads]
        totals = [_lane_pick(tot_ref[...], ATT_HP * g + j) for j in heads]
        incl = _tri(tk, lambda r, c: r <= c)
        excl = _tri(tk, lambda r, c: r < c)

        def tile(kj, carry, mask):
            rows = pl.ds(pl.multiple_of(kj * tk, tk), tk)
            k_t = [qkv_refs[3 * j + 1][rows, :] for j in heads]
            z = [_dot(qs[j], k_t[j], "nt") * scale for j in heads]
            dw = [_dot(douts[j], qkv_refs[3 * j + 2][rows, :], "nt") for j in heads]
            lsz = [_log_sigmoid(z[j]) for j in heads]
            lk = [lsz[j] - z[j] if mask is None else jnp.where(mask, lsz[j] - z[j], 0.0) for j in heads]
            parts = [_split2(lk[j]) for j in heads]
            below = [carry[j][0] + _dot(parts[j][0], incl) + _dot(parts[j][1], incl) for j in heads]
            w = [jnp.exp(lsz[j] + (totals[j] - below[j])) for j in heads]
            if mask is not None:
                w = [jnp.where(mask, w[j], 0.0) for j in heads]
            e = [dw[j] * w[j] for j in heads]
            parts = [_split2(e[j]) for j in heads]
            e_before = [carry[j][1] + _dot(parts[j][0], excl) + _dot(parts[j][1], excl) for j in heads]
            sg = [jnp.exp(lsz[j]) for j in heads]
            dz = [e[j] * (1.0 - sg[j]) - e_before[j] * sg[j] for j in heads]
            if mask is not None:
                dz = [jnp.where(mask, dz[j], 0.0) for j in heads]
            dz = [(dz[j] * scale).astype(BF16) for j in heads]
            for j in heads:
                dk_acc[j, rows, :] += _dot(dz[j], qs[j], "tn")
                dv_acc[j, rows, :] += _dot(w[j], douts[j], "tn")
            return tuple((carry[j][0] + jnp.sum(lk[j], axis=1, keepdims=True),
                          carry[j][1] + jnp.sum(e[j], axis=1, keepdims=True),
                          carry[j][2] + _dot(dz[j], k_t[j])) for j in heads)

        zero = jnp.zeros((tq, 1), F32)
        carry = lax.fori_loop(0, i, lambda kj, cr: tile(kj, cr, None),
                              tuple((zero, zero, jnp.zeros((tq, hd), F32)) for _ in heads))
        carry = tile(i, carry, _diag_mask(True))
        for j in heads:
            out_ref[pl.ds(pl.multiple_of(i * tq, tq), tq), pl.ds(3 * j * hd, hd)] = carry[j][2].astype(BF16)

        @pl.when(i == nq - 1)
        def _():
            for j in heads:
                out_ref[:, pl.ds((3 * j + 1) * hd, hd)] = dk_acc[j].astype(BF16)
                out_ref[:, pl.ds((3 * j + 2) * hd, hd)] = dv_acc[j].astype(BF16)

    wide = ATT_HP * hd
    return pl.pallas_call(
        body, name="sb_bwd", grid=(n_heads // ATT_HP, nq),
        in_specs=_qkv_specs(0, s) + [pl.BlockSpec((tq, wide), lambda g, i: (i, g)),
                                     pl.BlockSpec((tq, LANES), lambda g, i: (i, 0)), ANY],
        out_specs=pl.BlockSpec((s, 3 * wide), lambda g, i: (0, g)),
        out_shape=_sds(qkv.shape, BF16),
        scratch_shapes=[pltpu.VMEM((ATT_HP, s, hd), F32), pltpu.VMEM((ATT_HP, s, hd), F32)],
        compiler_params=_params(("arbitrary", "arbitrary")),
    )(*[qkv] * (3 * ATT_HP), do, tot, dep)


def _fox_fwd(qkv, cum_col, cum_row, n_heads, hb0, dep):
    s = qkv.shape[0]
    scale = HEAD_DIM ** -0.5
    tq, tk = ATT_TQ, ATT_TK

    heads = range(ATT_HP)

    def body(*refs):
        qkv_refs = refs[:3 * ATT_HP]
        cc_ref, cr_ref, _, o_ref, ot_ref, o32_ref, lse_ref = refs[3 * ATT_HP:]
        g, i = pl.program_id(0), pl.program_id(1)

        @pl.when((g == 0) & (i == 0))
        def _():
            lse_ref[...] = jnp.zeros_like(lse_ref)

        qs = [qkv_refs[3 * j][...] for j in heads]
        cqs = [_lane_pick(cc_ref[...], ATT_HP * g + j) for j in heads]

        def tile(kj, carry, mask):
            rows = pl.ds(pl.multiple_of(kj * tk, tk), tk)
            sc = [_dot(qs[j], qkv_refs[3 * j + 1][rows, :], "nt") * scale + cqs[j]
                  - cr_ref[kj, pl.ds(ATT_HP * g + j, 1), :] for j in heads]
            if mask is not None:
                sc = [jnp.where(mask, sc[j], NEG_BIG) for j in heads]
            m_new = [jnp.maximum(carry[j][0], jnp.max(sc[j], axis=1, keepdims=True)) for j in heads]
            p = [jnp.exp(sc[j] - m_new[j]) for j in heads]
            alpha = [jnp.exp(carry[j][0] - m_new[j]) for j in heads]
            parts = [_split2(p[j]) for j in heads]
            v_t = [qkv_refs[3 * j + 2][rows, :] for j in heads]
            pv = [_dot(parts[j][0], v_t[j]) + _dot(parts[j][1], v_t[j]) for j in heads]
            return tuple((m_new[j], alpha[j] * carry[j][1] + jnp.sum(p[j], axis=1, keepdims=True),
                          alpha[j] * carry[j][2] + pv[j]) for j in heads)

        carry = tuple((jnp.full((tq, 1), NEG_BIG, F32), jnp.zeros((tq, 1), F32), jnp.zeros((tq, HEAD_DIM), F32))
                      for _ in heads)
        carry = lax.fori_loop(0, i, lambda kj, cr: tile(kj, cr, None), carry)
        carry = tile(i, carry, _diag_mask(False))
        q_rows = pl.ds(pl.multiple_of(i * tq, tq), tq)
        for j in heads:
            m, l, acc = carry[j]
            o = acc / l
            o_ref[:, _head_cols(j)] = o.astype(BF16)
            ot_ref[_head_cols(j), :] = o.astype(BF16).T
            o32_ref[:, _head_cols(j)] = o
            _lane_put(lse_ref, q_rows, ATT_HP * g + j, m + jnp.log(l))

    nb = cum_row.shape[0]
    wide = ATT_HP * HEAD_DIM
    return pl.pallas_call(
        body, name="fox_fwd", grid=(n_heads // ATT_HP, s // tq),
        in_specs=_qkv_specs(hb0, s) + [pl.BlockSpec((tq, LANES), lambda g, i: (i, 0)),
                                       pl.BlockSpec((nb, 8, tk), lambda g, i: (0, 0, 0)), ANY],
        out_specs=[pl.BlockSpec((tq, wide), lambda g, i: (i, g)), pl.BlockSpec((wide, tq), lambda g, i: (g, i)),
                   pl.BlockSpec((tq, wide), lambda g, i: (i, g)), pl.BlockSpec((s, LANES), lambda g, i: (0, 0))],
        out_shape=[_sds((s, n_heads * HEAD_DIM), BF16), _sds((n_heads * HEAD_DIM, s), BF16),
                   _sds((s, n_heads * HEAD_DIM), F32), _sds((s, LANES), F32)],
        compiler_params=_params(("arbitrary", "arbitrary")),
    )(*[qkv] * (3 * ATT_HP), cum_col, cum_row, dep)


def _fox_bwd(dqkv, qkv, do, o, lse, cum_col, cum_row, n_heads, hb0, dep):
    s = qkv.shape[0]
    scale = HEAD_DIM ** -0.5
    tq, tk = ATT_TQ, ATT_TK
    nq = s // tq
    hd = HEAD_DIM

    heads = range(ATT_HP)
    assert hb0 % ATT_HP == 0

    def body(*refs):
        qkv_refs = refs[1:1 + 3 * ATT_HP]
        do_ref, o_ref, lse_ref, cc_ref, cr_ref, _, out_ref, dc_ref, dk_acc, dv_acc, col_acc = refs[1 + 3 * ATT_HP:]
        g, i = pl.program_id(0), pl.program_id(1)

        @pl.when((g == 0) & (i == 0))
        def _():
            dc_ref[...] = jnp.zeros_like(dc_ref)

        @pl.when(i == 0)
        def _():
            dk_acc[...] = jnp.zeros_like(dk_acc)
            dv_acc[...] = jnp.zeros_like(dv_acc)
            col_acc[...] = jnp.zeros_like(col_acc)

        qs = [qkv_refs[3 * j][...] for j in heads]
        douts = [do_ref[:, _head_cols(j)] for j in heads]
        deltas = [jnp.sum(douts[j].astype(F32) * o_ref[:, _head_cols(j)], axis=1, keepdims=True) for j in heads]
        shifts = [_lane_pick(cc_ref[...], ATT_HP * g + j) - _lane_pick(lse_ref[...], ATT_HP * g + j) for j in heads]

        def tile(kj, carry, mask):
            rows = pl.ds(pl.multiple_of(kj * tk, tk), tk)
            k_t = [qkv_refs[3 * j + 1][rows, :] for j in heads]
            sc = [_dot(qs[j], k_t[j], "nt") * scale + shifts[j] - cr_ref[kj, pl.ds(ATT_HP * g + j, 1), :] for j in heads]
            dp = [_dot(douts[j], qkv_refs[3 * j + 2][rows, :], "nt") for j in heads]
            p = [jnp.exp(sc[j]) for j in heads]
            if mask is not None:
                p = [jnp.where(mask, p[j], 0.0) for j in heads]
            ds_f = [p[j] * (dp[j] - deltas[j]) for j in heads]
            ds = [(ds_f[j] * scale).astype(BF16) for j in heads]
            for j in heads:
                col_acc[j, kj] += jnp.broadcast_to(jnp.sum(ds_f[j], axis=0, keepdims=True), (8, tk))
                dk_acc[j, rows, :] += _dot(ds[j], qs[j], "tn")
                dv_acc[j, rows, :] += _dot(p[j], douts[j], "tn")
            return tuple((carry[j][0] + _dot(ds[j], k_t[j]), carry[j][1] + jnp.sum(ds_f[j], axis=1, keepdims=True))
                         for j in heads)

        carry = lax.fori_loop(0, i, lambda kj, cr: tile(kj, cr, None),
                              tuple((jnp.zeros((tq, hd), F32), jnp.zeros((tq, 1), F32)) for _ in heads))
        carry = tile(i, carry, _diag_mask(False))
        q_rows = pl.ds(pl.multiple_of(i * tq, tq), tq)
        for j in heads:
            out_ref[q_rows, pl.ds(3 * j * hd, hd)] = carry[j][0].astype(BF16)
            _lane_put(dc_ref, q_rows, ATT_HP * g + j, carry[j][1])

        @pl.when(i == nq - 1)
        def _():
            lane = lax.broadcasted_iota(jnp.int32, (tk, LANES), 1)
            for j in heads:
                out_ref[:, pl.ds((3 * j + 1) * hd, hd)] = dk_acc[j].astype(BF16)
                out_ref[:, pl.ds((3 * j + 2) * hd, hd)] = dv_acc[j].astype(BF16)
                for kj in range(nb):
                    col = jnp.broadcast_to(col_acc[j, kj][0:1, :], (LANES, tk)).T
                    old = dc_ref[pl.ds(kj * tk, tk), :]
                    dc_ref[pl.ds(kj * tk, tk), :] = jnp.where(lane == ATT_HP * g + j, old - col, old)

    nb = cum_row.shape[0]
    wide = ATT_HP * hd
    return pl.pallas_call(
        body, name="fox_bwd", grid=(n_heads // ATT_HP, nq),
        in_specs=[ANY] + _qkv_specs(hb0, s) + [
            pl.BlockSpec((tq, wide), lambda g, i: (i, g)), pl.BlockSpec((tq, wide), lambda g, i: (i, g)),
            pl.BlockSpec((tq, LANES), lambda g, i: (i, 0)), pl.BlockSpec((tq, LANES), lambda g, i: (i, 0)),
            pl.BlockSpec((nb, 8, tk), lambda g, i: (0, 0, 0)), ANY],
        out_specs=[pl.BlockSpec((s, 3 * wide), lambda g, i: (0, hb0 // ATT_HP + g)),
                   pl.BlockSpec((s, LANES), lambda g, i: (0, 0))],
        out_shape=[_sds(dqkv.shape, BF16), _sds((s, LANES), F32)],
        scratch_shapes=[pltpu.VMEM((ATT_HP, s, hd), F32), pltpu.VMEM((ATT_HP, s, hd), F32),
                        pltpu.VMEM((ATT_HP, s // tk, 8, tk), F32)],
        input_output_aliases={0: 0},
        compiler_params=_params(("arbitrary", "arbitrary")),
    )(dqkv, *[qkv] * (3 * ATT_HP), do, o, lse, cum_col, cum_row, dep)


def _branch_merge(o_sb, o_fx, w_sb, w_fx, gf, dep, tm=1024):
    s = o_sb.shape[0]
    cs = w_sb.shape[2]
    tm = _tile(s, tm)

    def body(osb_ref, ofx_ref, wsb_ref, wfx_ref, g_ref, dep_ref, merged_ref, mt_ref, asb_ref, afx_ref):
        del dep_ref
        a_sb = _dot(osb_ref[...], wsb_ref[...])
        a_fx = _dot(ofx_ref[...], wfx_ref[...])
        g = g_ref[...]
        merged = (_sigmoid(g[:, :cs]) * a_sb + _sigmoid(g[:, cs:]) * a_fx).astype(BF16)
        merged_ref[...] = merged
        mt_ref[...] = merged.T
        asb_ref[...] = a_sb.astype(BF16)
        afx_ref[...] = a_fx.astype(BF16)

    blk = pl.BlockSpec((tm, cs), lambda i, j: (i, j))
    out = _sds((s, N_DEV * cs), BF16)
    return pl.pallas_call(
        body, name="branch_merge", grid=(s // tm, N_DEV),
        in_specs=[pl.BlockSpec((tm, o_sb.shape[1]), lambda i, j: (i, 0)),
                  pl.BlockSpec((tm, o_fx.shape[1]), lambda i, j: (i, 0)),
                  pl.BlockSpec((None,) + w_sb.shape[1:], lambda i, j: (j, 0, 0)),
                  pl.BlockSpec((None,) + w_fx.shape[1:], lambda i, j: (j, 0, 0)),
                  pl.BlockSpec((tm, 2 * cs), lambda i, j: (i, j)), ANY],
        out_specs=[blk, pl.BlockSpec((cs, tm), lambda i, j: (j, i)), blk, blk],
        out_shape=[out, _sds((N_DEV * cs, s), BF16), out, out],
        compiler_params=_params(("parallel", "arbitrary")),
    )(o_sb, o_fx, w_sb, w_fx, gf, dep)


def _merge_bwd(dmix, w_out, gf, a_sb, a_fx, tm=1024, tk=2048, dep=None):
    s, d = dmix.shape
    cs = d // N_DEV
    tm, tk = _tile(s, tm), _tile(d, tk)

    def epilogue(acc, ex, outs):
        g, a_sb, a_fx = ex[0][...], ex[1][...].astype(F32), ex[2][...].astype(F32)
        s_sb, s_fx = _sigmoid(g[:, :cs]), _sigmoid(g[:, cs:])
        outs[0][...] = (acc * s_sb).astype(BF16)
        outs[1][...] = (acc * s_fx).astype(BF16)
        outs[2][...] = jnp.concatenate([acc * a_sb * s_sb * (1.0 - s_sb), acc * a_fx * s_fx * (1.0 - s_fx)],
                                       axis=1).astype(BF16)

    blk = pl.BlockSpec((tm, cs), lambda i, j, k: (i, j))
    wide = pl.BlockSpec((tm, 2 * cs), lambda i, j, k: (i, j))
    return _matmul(
        "merge_bwd", "nt",
        [(dmix, pl.BlockSpec((tm, tk), lambda i, j, k: (i, k)), w_out, pl.BlockSpec((cs, tk), lambda i, j, k: (j, k)))],
        (s // tm, N_DEV, d // tk), (tm, cs),
        [_sds((s, d), BF16), _sds((s, d), BF16), _sds(gf.shape, BF16)], [blk, blk, wide],
        extras=[(gf, wide), (a_sb, blk), (a_fx, blk)], epilogue=epilogue, dep=dep)


def _ffn_up(u2, w_gate, w_up, dep, tm=1024):
    s, d = u2.shape
    fs = w_gate.shape[2]
    tm = _tile(s, tm)

    def body(u_ref, wg_ref, wu_ref, dep_ref, gate_ref, up_ref, act_ref, actt_ref):
        del dep_ref
        u = u_ref[...]
        gate = _dot(u, wg_ref[...])
        up = _dot(u, wu_ref[...])
        gate_ref[...] = gate
        up_ref[...] = up
        act = (gate * _sigmoid(gate) * up).astype(BF16)
        act_ref[...] = act
        actt_ref[...] = act.T

    w_spec = pl.BlockSpec((None, d, fs), lambda i, j: (j, 0, 0))
    o_spec = pl.BlockSpec((None, tm, fs), lambda i, j: (j, i, 0))
    return pl.pallas_call(
        body, name="ffn_up", grid=(s // tm, N_DEV),
        in_specs=[pl.BlockSpec((tm, d), lambda i, j: (i, 0)), w_spec, w_spec, ANY],
        out_specs=[o_spec, o_spec, o_spec, pl.BlockSpec((None, fs, tm), lambda i, j: (j, 0, i))],
        out_shape=[_sds((N_DEV, s, fs), F32), _sds((N_DEV, s, fs), F32), _sds((N_DEV, s, fs), BF16),
                   _sds((N_DEV, fs, s), BF16)],
        compiler_params=_params(("parallel", "arbitrary")),
    )(u2, w_gate, w_up, dep)


def _ffn_down_bwd(dff, w_down, gate, up, tm=1024):
    s, d = dff.shape
    fs = w_down.shape[1]
    tm = _tile(s, tm)

    def body(dff_ref, wd_ref, gate_ref, up_ref, dgate_ref, dup_ref):
        dact = _dot(dff_ref[...], wd_ref[...], "nt")
        gate = gate_ref[...]
        sg = _sigmoid(gate)
        dup_ref[...] = (dact * gate * sg).astype(BF16)
        dgate_ref[...] = (dact * up_ref[...] * sg * (1.0 + gate * (1.0 - sg))).astype(BF16)

    a_spec = pl.BlockSpec((None, tm, fs), lambda i, j: (j, i, 0))
    return pl.pallas_call(
        body, name="ffn_down_bwd", grid=(s // tm, N_DEV),
        in_specs=[pl.BlockSpec((tm, d), lambda i, j: (i, 0)), pl.BlockSpec((None, fs, d), lambda i, j: (j, 0, 0)),
                  a_spec, a_spec],
        out_specs=[a_spec, a_spec],
        out_shape=[_sds((N_DEV, s, fs), BF16), _sds((N_DEV, s, fs), BF16)],
        compiler_params=_params(("parallel", "arbitrary")),
    )(dff, w_down, gate, up)


def _mesh_place():
    x, y, c = lax.axis_index("x"), lax.axis_index("y"), lax.axis_index("c")
    peers = []
    for d in range(1, N_DEV):
        px = 1 - x if d & 4 else x
        py = 1 - y if d & 2 else y
        pc = 1 - c if d & 1 else c
        peers.append((d, (px, py, pc), 4 * px + 2 * py + pc))
    return 4 * x + 2 * y + c, peers


def _flat_me():
    return 4 * lax.axis_index("x") + 2 * lax.axis_index("y") + lax.axis_index("c")


def _in_hbm(a):
    return pltpu.with_memory_space_constraint(a, pltpu.HBM)


def _pair_plan():
    x, y, c = lax.axis_index("x"), lax.axis_index("y"), lax.axis_index("c")
    return [(2 * q + (1 - c), q, q, (x, y, 1 - c)) for q in range(4)]


def _chip_plan():
    x, y, c = lax.axis_index("x"), lax.axis_index("y"), lax.axis_index("c")
    plan = []
    for fx, fy in ((1, 0), (0, 1), (1, 1)):
        cx, cy = (1 - x if fx else x), (1 - y if fy else y)
        plan.append((2 * cx + cy, 2 * x + y, 2 * cx + cy, (cx, cy, c)))
    return plan


def _split_start(name, srcs, lands, plan, k):
    n = len(srcs)

    def body(*refs):
        ins, lnd = refs[:n], refs[n:2 * n]
        send, recv, token = refs[2 * n], refs[2 * n + 1], refs[-1]
        copies = plan()
        for a in range(n):
            for t, (src, dst, _, dev) in enumerate(copies):
                pltpu.make_async_remote_copy(src_ref=ins[a].at[src], dst_ref=lnd[a].at[dst], send_sem=send.at[k * a + t],
                                             recv_sem=recv.at[k * a + t], device_id=dev, device_id_type=MESH).start()
        token[...] = jnp.zeros_like(token)

    res = pl.pallas_call(
        body, name=name,
        out_shape=[pltpu.SemaphoreType.DMA((n * k,)), pltpu.SemaphoreType.DMA((n * k,))]
        + [pltpu.HBM(a.shape, a.dtype) for a in list(srcs) + list(lands)] + [_sds((8, LANES), F32)],
        in_specs=[HBM] * (2 * n), out_specs=[SEM, SEM] + [HBM] * (2 * n) + [pl.BlockSpec(memory_space=pltpu.VMEM)],
        input_output_aliases={i: 2 + i for i in range(2 * n)},
        compiler_params=pltpu.CompilerParams(has_side_effects=EFFECT),
    )(*[_in_hbm(a) for a in srcs], *[_in_hbm(a) for a in lands])
    return res[0], res[1], res[2:2 + n], res[2 + n:2 + 2 * n], res[-1]


def _split_wait(name, send, recv, srcs, lands, plan, k, after):
    n = len(srcs)

    def body(*refs):
        ins, lnd = refs[:n], refs[n:2 * n]
        send_sem, recv_sem = refs[2 * n], refs[2 * n + 1]
        copies = plan()
        for a in range(n):
            for t, (src, _, dst, dev) in enumerate(copies):
                cp = pltpu.make_async_remote_copy(src_ref=ins[a].at[src], dst_ref=lnd[a].at[dst], send_sem=send_sem.at[k * a + t],
                                                  recv_sem=recv_sem.at[k * a + t], device_id=dev, device_id_type=MESH)
                cp.wait_send()
                cp.wait_recv()

    res = pl.pallas_call(
        body, name=name,
        out_shape=[pltpu.HBM(a.shape, a.dtype) for a in list(srcs) + list(lands)],
        in_specs=[HBM] * (2 * n) + [SEM, SEM] + [ANY] * len(after), out_specs=[HBM] * (2 * n),
        input_output_aliases={i: i for i in range(2 * n)},
        compiler_params=pltpu.CompilerParams(has_side_effects=EFFECT),
    )(*srcs, *lands, send, recv, *after)
    return res[:n], res[n:]


def _pair_add(name, parts, land):
    _, r, cols = parts.shape
    tr = max(16, min(r, ((1 << 20) // (2 * cols)) // 16 * 16))
    while r % tr:
        tr -= 16

    def body(c_ref, p_ref, l_ref, o_ref):
        del c_ref
        o_ref[...] = (p_ref[...].astype(F32) + l_ref[...].astype(F32)).astype(BF16)

    blk = pl.BlockSpec((None, tr, cols), lambda q, i, c_ref: (q, i, 0))
    return pl.pallas_call(
        body, name=name,
        grid_spec=pltpu.PrefetchScalarGridSpec(
            num_scalar_prefetch=1, grid=(4, r // tr),
            in_specs=[pl.BlockSpec((None, tr, cols), lambda q, i, c_ref: (2 * q + c_ref[0], i, 0)), blk], out_specs=blk),
        out_shape=_sds((4, r, cols), BF16),
        compiler_params=_params(("parallel", "parallel")),
    )(jnp.reshape(lax.axis_index("c"), (1,)).astype(jnp.int32), parts, land)


def _scatter_pairs(tag, parts):
    lands = [lax.empty((4,) + a.shape[1:], a.dtype) for a in parts]
    return _split_start("pair_" + tag, parts, lands, _pair_plan, 4)


def _scatter_chips(tag, started, after):
    send, recv, parts, lands, _ = started
    parts, lands = _split_wait("pair_" + tag + "_wait", send, recv, parts, lands, _pair_plan, 4, [after])
    sums = [_pair_add("pair_" + tag + "_add%d" % a, p, l) for a, (p, l) in enumerate(zip(parts, lands))]
    chip = 2 * lax.axis_index("x") + lax.axis_index("y")
    final = [lax.dynamic_update_slice_in_dim(lax.empty(v.shape, v.dtype), lax.dynamic_slice_in_dim(v, chip, 1, 0), chip, 0)
             for v in sums]
    return _split_start("chips_" + tag, sums, final, _chip_plan, 3)


def _scatter_end(tag, started, after):
    send, recv, sums, final, _ = started
    return _split_wait("chips_" + tag + "_wait", send, recv, sums, final, _chip_plan, 3, after)[1]


def _gather_targets():
    x, y, c = lax.axis_index("x"), lax.axis_index("y"), lax.axis_index("c")
    chips = [(x, y), (1 - x, y), (x, 1 - y), (1 - x, 1 - y)]
    same = [((cx, cy, c), 4 * cx + 2 * cy + c) for cx, cy in chips]
    other = [((cx, cy, 1 - c), 4 * cx + 2 * cy + 1 - c) for cx, cy in chips]
    return same[0][1], [other[0]] + same[1:], [flat for _, flat in other[1:]], other[0][0]


def _gather_start(shards):
    n = len(shards)
    me = _flat_me()
    lands = [lax.dynamic_update_slice_in_dim(lax.empty((N_DEV,) + a.shape, a.dtype), a[None], me, 0) for a in shards]

    def body(*refs):
        lnd, send, recv, token = refs[:n], refs[n], refs[n + 1], refs[-1]
        mine, targets, _, _ = _gather_targets()
        for a in range(n):
            for t, (dev, _) in enumerate(targets):
                pltpu.make_async_remote_copy(src_ref=lnd[a].at[mine], dst_ref=lnd[a].at[mine], send_sem=send.at[4 * a + t],
                                             recv_sem=recv.at[4 * a + t], device_id=dev, device_id_type=MESH).start()
        token[...] = jnp.zeros_like(token)

    res = pl.pallas_call(
        body, name="gather_start",
        out_shape=[pltpu.SemaphoreType.DMA((4 * n,)), pltpu.SemaphoreType.DMA((4 * n,))]
        + [pltpu.HBM(a.shape, a.dtype) for a in lands] + [_sds((8, LANES), F32)],
        in_specs=[HBM] * n, out_specs=[SEM, SEM] + [HBM] * n + [pl.BlockSpec(memory_space=pltpu.VMEM)],
        input_output_aliases={i: 2 + i for i in range(n)},
        compiler_params=pltpu.CompilerParams(has_side_effects=EFFECT),
    )(*[_in_hbm(a) for a in lands])
    return res[0], res[1], list(res[2:2 + n]), res[-1]


def _gather_forward(name, lands, first, send, recv, after):
    n = len(lands)

    def body(*refs):
        lnd, send_sem, recv_sem = refs[:n], refs[n], refs[n + 1]
        send2, recv2, token = refs[-3], refs[-2], refs[-1]
        mine, targets, _, sibling = _gather_targets()
        for a in range(n):
            for t, (dev, flat) in enumerate(targets):
                cp = pltpu.make_async_remote_copy(src_ref=lnd[a].at[mine], dst_ref=lnd[a].at[flat],
                                                  send_sem=send_sem.at[4 * (first + a) + t],
                                                  recv_sem=recv_sem.at[4 * (first + a) + t], device_id=dev, device_id_type=MESH)
                cp.wait_send()
                if t:
                    cp.wait_recv()
                    pltpu.make_async_remote_copy(src_ref=lnd[a].at[flat], dst_ref=lnd[a].at[flat], send_sem=send2.at[3 * a + t - 1],
                                                 recv_sem=recv2.at[3 * a + t - 1], device_id=sibling, device_id_type=MESH).start()
        token[...] = jnp.zeros_like(token)

    res = pl.pallas_call(
        body, name=name,
        out_shape=[pltpu.HBM(a.shape, a.dtype) for a in lands]
        + [pltpu.SemaphoreType.DMA((3 * n,)), pltpu.SemaphoreType.DMA((3 * n,)), _sds((8, LANES), F32)],
        in_specs=[HBM] * n + [SEM, SEM] + [ANY] * len(after),
        out_specs=[HBM] * n + [SEM, SEM, pl.BlockSpec(memory_space=pltpu.VMEM)],
        input_output_aliases={i: i for i in range(n)},
        compiler_params=pltpu.CompilerParams(has_side_effects=EFFECT),
    )(*lands, send, recv, *after)
    return list(res[:n]), res[n], res[n + 1], res[-1]


def _gather_wait(name, lands, first, recv, send2, recv2, after):
    n = len(lands)

    def body(*refs):
        lnd, recv_sem, send2_sem, recv2_sem = refs[:n], refs[n], refs[n + 1], refs[n + 2]
        mine, targets, passed, sibling = _gather_targets()
        for a in range(n):
            dev, flat = targets[0]
            pltpu.make_async_remote_copy(src_ref=lnd[a].at[mine], dst_ref=lnd[a].at[flat], send_sem=send2_sem.at[3 * a],
                                         recv_sem=recv_sem.at[4 * (first + a)], device_id=dev, device_id_type=MESH).wait_recv()
            for t in range(3):
                cp = pltpu.make_async_remote_copy(src_ref=lnd[a].at[targets[t + 1][1]], dst_ref=lnd[a].at[passed[t]],
                                                  send_sem=send2_sem.at[3 * a + t], recv_sem=recv2_sem.at[3 * a + t],
                                                  device_id=sibling, device_id_type=MESH)
                cp.wait_send()
                cp.wait_recv()

    res = pl.pallas_call(
        body, name=name, out_shape=[pltpu.HBM(a.shape, a.dtype) for a in lands],
        in_specs=[HBM] * n + [SEM, SEM, SEM, ANY], out_specs=[HBM] * n,
        input_output_aliases={i: i for i in range(n)},
        compiler_params=pltpu.CompilerParams(has_side_effects=EFFECT),
    )(*lands, recv, send2, recv2, after)
    return list(res)


def _adamw_decay(w, m, v):
    return ADAM_WD * w, ADAM_B1 * m, ADAM_B2 * v


def _adamw_finish(g, wd_w, m1, v1):
    m = m1 + (1.0 - ADAM_B1) * g
    v = v1 + (1.0 - ADAM_B2) * (g * g)
    m_hat = m / (1.0 - ADAM_B1 ** ADAM_STEP)
    v_hat = v / (1.0 - ADAM_B2 ** ADAM_STEP)
    delta = -ADAM_LR * (m_hat / (jnp.sqrt(v_hat) + ADAM_EPS) + wd_w)
    return delta, m, v


def _adamw(g, w, m, v):
    return _adamw_finish(g, *_adamw_decay(w, m, v))


def _update_prep(name, w, m, v, dep, w_done=False, block_bytes=1 << 20):
    _, r, c = m.shape
    tr = max(8, min(r, (block_bytes // (4 * c)) // 8 * 8))
    while r % tr:
        tr -= 8
    blk = pl.BlockSpec((None, tr, c), lambda i: (0, i, 0))
    if w_done:
        def body(m_ref, v_ref, dep_ref, om_ref, ov_ref):
            del dep_ref
            om_ref[...] = ADAM_B1 * m_ref[...]
            ov_ref[...] = ADAM_B2 * v_ref[...]

        m1, v1 = pl.pallas_call(
            body, name=name, grid=(r // tr,), in_specs=[blk] * 2 + [ANY], out_specs=[blk] * 2,
            out_shape=[_sds((1, r, c), F32)] * 2, compiler_params=_params(("parallel",)),
        )(m, v, dep)
        return w, m1, v1

    def body(w_ref, m_ref, v_ref, dep_ref, ow_ref, om_ref, ov_ref):
        del dep_ref
        ow_ref[...], om_ref[...], ov_ref[...] = _adamw_decay(w_ref[...], m_ref[...], v_ref[...])

    return pl.pallas_call(
        body, name=name, grid=(r // tr,), in_specs=[blk] * 3 + [ANY], out_specs=[blk] * 3,
        out_shape=[_sds((1, r, c), F32)] * 3, compiler_params=_params(("parallel",)),
    )(w, m, v, dep)


def _update(name, parts, w, m, v, layout=None, decayed=False, transposed_out=False, block_bytes=1 << 20):
    _, r, c = w.shape
    n_slots, _, cp = parts.shape
    tr = max(8, min(r, (block_bytes // (4 * cp)) // 8 * 8))
    if transposed_out:
        tr = _tile(r, 256)
    while r % tr:
        tr -= 8

    def body(p_ref, w_ref, m_ref, v_ref, g_ref, d_ref, nm_ref, nv_ref, *scratch):
        g = p_ref[0].astype(F32)
        for p in range(1, n_slots):
            g = g + p_ref[p].astype(F32)
        if layout is not None:
            s1, s2, lg = layout.my_shifts()
            lane = lax.broadcasted_iota(jnp.int32, g.shape, 1)
            scratch[0][...] = jnp.where(lane < lg, pltpu.roll(g, cp - s1, 1), pltpu.roll(g, cp - s2, 1))
            g = scratch[0][:, 0:c]
        step = _adamw_finish if decayed else _adamw
        results = (g,) + step(g, w_ref[...], m_ref[...], v_ref[...])
        if ragged:
            scratch[-2][...] = jnp.zeros_like(scratch[-2])
        for ref, val in zip((g_ref, d_ref, nm_ref, nv_ref), results):
            if not transposed_out:
                ref[...] = val
            elif not ragged:
                ref[...] = val.T
            else:
                wide, tall = scratch[-2], scratch[-1]
                wide[:, 0:c] = val
                tall[...] = wide[...].T
                ref[...] = tall[0:c, :]

    ragged = transposed_out and c % 8 != 0
    c_wide = -(-c // LANES) * LANES
    blk = pl.BlockSpec((None, tr, c), lambda i: (0, i, 0))
    out_blk = pl.BlockSpec((None, c, tr), lambda i: (0, 0, i)) if transposed_out else blk
    scratch_shapes = [] if layout is None else [pltpu.VMEM((tr, cp), F32)]
    if ragged:
        scratch_shapes += [pltpu.VMEM((tr, c_wide), F32), pltpu.VMEM((c_wide, tr), F32)]
    res = pl.pallas_call(
        body, name=name, grid=(r // tr,),
        in_specs=[pl.BlockSpec((n_slots, tr, cp), lambda i: (0, i, 0)), blk, blk, blk],
        out_specs=[out_blk] * 4, out_shape=[_sds((1, c, r) if transposed_out else (1, r, c), F32)] * 4,
        scratch_shapes=scratch_shapes,
        compiler_params=_params(("parallel",)),
    )(parts, w, m, v)
    return [jnp.transpose(o, (0, 2, 1)) for o in res] if transposed_out else res


def _small_update(part, w, m, v):
    n = part.shape[1]

    def body(p_ref, w_ref, m_ref, v_ref, g_ref, d_ref, nm_ref, nv_ref, buf, send, recv):
        me, peers = _mesh_place()
        buf[me] = p_ref[...]
        sent = []
        for d, dev, flat in peers:
            cp = pltpu.make_async_remote_copy(src_ref=p_ref, dst_ref=buf.at[me], send_sem=send.at[d],
                                              recv_sem=recv.at[d], device_id=dev, device_id_type=MESH)
            cp.start()
            sent.append(cp)
        for d, dev, flat in peers:
            pltpu.make_async_remote_copy(src_ref=p_ref, dst_ref=buf.at[flat], send_sem=send.at[d],
                                         recv_sem=recv.at[d], device_id=dev, device_id_type=MESH).wait_recv()
        for cp in sent:
            cp.wait_send()
        g = buf[0]
        for p in range(1, N_DEV):
            g = g + buf[p]
        g_ref[...] = g
        d_ref[...], nm_ref[...], nv_ref[...] = _adamw(g, w_ref[...], m_ref[...], v_ref[...])

    vm = pl.BlockSpec(memory_space=pltpu.VMEM)
    return pl.pallas_call(
        body, name="small_update", in_specs=[vm] * 4, out_specs=[vm] * 4, out_shape=[_sds((1, n), F32)] * 4,
        scratch_shapes=[pltpu.VMEM((N_DEV, 1, n), F32), pltpu.SemaphoreType.DMA((N_DEV,)),
                        pltpu.SemaphoreType.DMA((N_DEV,))],
    )(part, w, m, v)


class _WInLayout:
    def __init__(self, n8, n_f, d_sb, d_fox, d):
        assert n8 % LANES == 1 and n_f < LANES and d % (N_DEV * LANES) == 0
        self.n8, self.n_f, self.d = n8, n_f, d
        self.sp = n8 // LANES
        self.wp = (n8 + 2 * LANES - 2) // LANES * LANES
        self.n_qkv = 3 * (d_sb + d_fox)
        nq, dt, tc = self.n_qkv // LANES, d // LANES, d // N_DEV // LANES
        h_sb, h_fox = d_sb // HEAD_DIM, d_fox // HEAD_DIM
        self.sources = {}
        self.part_tile = {}
        for p in range(N_DEV):
            lg = min(max(self.n_qkv + n_f - n8 * p, 0), n8)
            s1, s2 = p, p + LANES - n_f
            spans = []
            if lg > 0:
                spans.append(("a", self.sp * p, s1 // LANES, (lg + s1 - 1) // LANES))
            if lg < n8:
                spans.append(("g", self.sp * p - 1 - nq, (lg + s2) // LANES, (n8 - 1 + s2) // LANES))
            for kind, base, first, last in spans:
                for i in range(first, last + 1):
                    assert (p, i) not in self.part_tile
                    self.part_tile[(p, i)] = (kind, base + i)
                    self.sources.setdefault((kind, base + i), []).append((p, i))
        self.cat_tiles = [("a", r * h_sb + h) for h in range(h_sb) for r in range(3)]
        self.cat_tiles += [("a", 3 * h_sb + r * h_fox + h) for h in range(h_fox) for r in range(3)]
        self.cat_tiles += [("g", which * dt + j * tc + half) for j in range(N_DEV) for which in (0, 1) for half in range(tc)]
        self.cat_tiles += [("a", nq)] + [None] * (F_PAD // LANES - 1)
        self.cat_index = {key: c for c, key in enumerate(self.cat_tiles) if key is not None}

    def my_shifts(self):
        me = _flat_me()
        return me, me + LANES - self.n_f, jnp.clip(self.n_qkv + self.n_f - self.n8 * me, 0, self.n8)


def _lane_tile(i):
    return pl.ds(i * LANES, LANES)


def _w_in_shift(w_in, lay, tr=256):
    _, d, n8 = w_in.shape
    kd = d // LANES
    kt = tr // LANES
    by_col = jnp.transpose(w_in, (0, 2, 1)).reshape(n8 * kd, LANES)

    def body(w_ref, o_ref, wd_ref, buf):
        k0 = kt * pl.program_id(0)
        buf[...] = jnp.zeros_like(buf)
        for j in range(n8 // LANES):
            for kk in range(kt):
                piece = w_ref[pl.ds(j * LANES * kd + k0 + kk, LANES, stride=kd), :]
                buf[kk * LANES:(kk + 1) * LANES, j * LANES:(j + 1) * LANES] = piece.T
        first = lax.broadcasted_iota(jnp.int32, (8, LANES), 0) == 0
        for kk in range(kt):
            row = w_ref[pl.ds((n8 - 1) * kd + k0 + kk, 1), :]
            buf[kk * LANES:(kk + 1) * LANES, n8 - 1:n8 + 7] = jnp.where(first, jnp.broadcast_to(row, (8, LANES)), 0.0).T
        wd_ref[...] = ADAM_WD * buf[:, 0:n8]
        v = buf[...]
        s1, s2, lg = lay.my_shifts()
        pos = lax.broadcasted_iota(jnp.int32, v.shape, 1)
        o_ref[...] = jnp.where(pos < lg + s1, pltpu.roll(v, s1, 1),
                               jnp.where(pos >= lg + s2, pltpu.roll(v, s2, 1), 0.0)).astype(BF16)

    return pl.pallas_call(
        body, name="w_in_shift", grid=(d // tr,),
        in_specs=[pl.BlockSpec((n8 * kd, LANES), lambda i: (0, 0))],
        out_specs=[pl.BlockSpec((tr, lay.wp), lambda i: (i, 0)), pl.BlockSpec((None, tr, n8), lambda i: (0, i, 0))],
        out_shape=[_sds((d, lay.wp), BF16), _sds((1, d, n8), F32)],
        scratch_shapes=[pltpu.VMEM((tr, lay.wp), F32)],
        compiler_params=_params(("arbitrary",)),
    )(by_col)


def _w_in_build(g_in, lay, tr=256):
    d = g_in.shape[1]
    width = len(lay.cat_tiles) * LANES

    def body(g_ref, o_ref):
        for c, key in enumerate(lay.cat_tiles):
            if key is None:
                o_ref[:, _lane_tile(c)] = jnp.zeros((tr, LANES), BF16)
                continue
            (p, i), *more = lay.sources[key]
            val = g_ref[p, :, _lane_tile(i)]
            for p2, i2 in more:
                val = val + g_ref[p2, :, _lane_tile(i2)]
            o_ref[:, _lane_tile(c)] = val

    return pl.pallas_call(
        body, name="w_in_build", grid=(d // tr,),
        in_specs=[pl.BlockSpec((N_DEV, tr, lay.wp), lambda i: (0, i, 0))],
        out_specs=pl.BlockSpec((tr, width), lambda i: (i, 0)), out_shape=_sds((d, width), BF16),
        compiler_params=_params(("parallel",)),
    )(g_in)


def _w_in_grad_parts(dwq, dwgf, lay, tr=256):
    d = dwq.shape[0]
    nq = lay.n_qkv // LANES

    def body(q_ref, g_ref, o_ref):
        for p in range(N_DEV):
            for i in range(lay.wp // LANES):
                key = lay.part_tile.get((p, i))
                if key is None:
                    o_ref[p, :, _lane_tile(i)] = jnp.zeros((tr, LANES), BF16)
                    continue
                c = lay.cat_index[key]
                o_ref[p, :, _lane_tile(i)] = q_ref[:, _lane_tile(c)] if c < nq else g_ref[:, _lane_tile(c - nq)]

    return pl.pallas_call(
        body, name="w_in_grad_parts", grid=(d // tr,),
        in_specs=[pl.BlockSpec((tr, dwq.shape[1]), lambda i: (i, 0)), pl.BlockSpec((tr, dwgf.shape[1]), lambda i: (i, 0))],
        out_specs=pl.BlockSpec((N_DEV, tr, lay.wp), lambda i: (0, i, 0)), out_shape=_sds((N_DEV, d, lay.wp), BF16),
        compiler_params=_params(("parallel",)),
    )(dwq, dwgf)


def kernel(x, norm_mix_pre, norm_mix_post, w_in, b_forget, w_branch_sb, w_branch_fox, w_out, norm_ffn_pre, norm_ffn_post, w_ffn_gate, w_ffn_up, w_ffn_down, loss_target, m_norm_mix_pre, m_norm_mix_post, m_w_in, m_b_forget, m_w_branch_sb, m_w_branch_fox, m_w_out, m_norm_ffn_pre, m_norm_ffn_post, m_w_ffn_gate, m_w_ffn_up, m_w_ffn_down, v_norm_mix_pre, v_norm_mix_post, v_w_in, v_b_forget, v_w_branch_sb, v_w_branch_fox, v_w_out, v_norm_ffn_pre, v_norm_ffn_post, v_w_ffn_gate, v_w_ffn_up, v_w_ffn_down):
    xs, target = x[0], loss_target[0]
    s, d = xs.shape
    d_sb, d_fox = w_branch_sb.shape[1], w_branch_fox.shape[1]
    h_sb, h_fox = d_sb // HEAD_DIM, d_fox // HEAD_DIM
    n_f = b_forget.shape[1]
    fs = w_ffn_gate.shape[2]
    cs = d // N_DEV
    n_qkv = 3 * (d_sb + d_fox)
    n_gf = 2 * d + F_PAD
    f_blk = 2 * d // LANES
    big = (w_in, w_branch_sb, w_branch_fox, w_out, w_ffn_gate, w_ffn_up, w_ffn_down)
    big_m = (m_w_in, m_w_branch_sb, m_w_branch_fox, m_w_out, m_w_ffn_gate, m_w_ffn_up, m_w_ffn_down)
    big_v = (v_w_in, v_w_branch_sb, v_w_branch_fox, v_w_out, v_w_ffn_gate, v_w_ffn_up, v_w_ffn_down)

    lay = _WInLayout(w_in.shape[2], n_f, d_sb, d_fox, d)
    w_in_shifted, wd_w_in = _w_in_shift(w_in, lay)
    send1, recv1, lands, token = _gather_start([w_in_shifted] + [w[0].astype(BF16) for w in big[1:]])
    b_pad = jnp.pad(b_forget, ((0, 0), (0, LANES - n_f)))

    started = token[0, 0]
    u, u_t = _pre_norm(xs, norm_mix_pre, dep=token)
    weights = dict(zip(("w_in", "w_branch_sb", "w_branch_fox", "w_out", "w_ffn_gate", "w_ffn_up", "w_ffn_down"),
                       zip(big, big_m, big_v)))
    decayed = {nm: _update_prep("decay_" + nm, *[t + started for t in weights[nm]], u)
               for nm in ("w_ffn_gate", "w_ffn_up")}
    decayed["w_in"] = _update_prep("decay_w_in", wd_w_in, m_w_in + started, v_w_in + started, u, w_done=True)
    l_in, send2, recv2, token = _gather_forward("gather_in_forward", lands[0:1], 0, send1, recv1,
                                                [u] + [t[2] for t in decayed.values()])
    (g_in,) = _gather_wait("gather_in_wait", l_in, 0, recv1, send2, recv2, token)
    w_cat = _w_in_build(g_in, lay)
    qkv = _mm_plain("proj_qkv", "nn", u, w_cat, BF16, n=n_qkv)
    gf = _mm_plain("proj_gates", "nn", u, w_cat, F32, n_off=n_qkv, n=n_gf)
    cum_col, cum_row = _forget_fwd(gf, b_pad, f_blk)
    o_sb, o_sb_t, tot = _sb_fwd(qkv, h_sb)
    l_mid, send2, recv2, token = _gather_forward("gather_mid_forward", lands[1:4], 1, send1, recv1, [o_sb])
    o_fx, o_fx_t, o_fx32, lse = _fox_fwd(qkv, cum_col, cum_row, h_fox, h_sb, token)
    g_sb, g_fx, g_out = _gather_wait("gather_mid_wait", l_mid, 1, recv1, send2, recv2, o_fx)
    w_out_full = g_out.reshape(d, d)
    merged, merged_t, a_sb, a_fx = _branch_merge(o_sb, o_fx, g_sb, g_fx, gf, o_fx)
    l_ffn, send2, recv2, token = _gather_forward("gather_ffn_forward", lands[4:6], 4, send1, recv1, [merged])
    mix = _mm_plain("out_proj", "nn", merged, w_out_full, F32, dep=token)
    h1, u2, u2_t = _mid_norms(xs, mix, norm_mix_post, norm_ffn_pre)
    g_gate, g_up = _gather_wait("gather_ffn_wait", l_ffn, 4, recv1, send2, recv2, u2)
    l_down, send2, recv2, token = _gather_forward("gather_down_forward", lands[6:7], 6, send1, recv1, [u2])
    gate, up, act, act_t = _ffn_up(u2, g_gate, g_up, token)
    (g_down,) = _gather_wait("gather_down_wait", l_down, 6, recv1, send2, recv2, act)
    tm, tn = _tile(s, 1024), _tile(d, 1024)
    tw = _tile(d, 2048)
    ff = _matmul("ffn_down", "nn",
                 [(act, pl.BlockSpec((None, tm, fs), lambda i, j, k: (k, i, 0)),
                   g_down, pl.BlockSpec((None, fs, tw), lambda i, j, k: (k, 0, j)))],
                 (s // tm, d // tw, N_DEV), (tm, tw), _sds((s, d), F32), pl.BlockSpec((tm, tw), lambda i, j, k: (i, j)))
    loss_part, dy, dff, dg_ffn_post = _loss_head(h1, ff, target, norm_ffn_post)

    dgate, dup = _ffn_down_bwd(dff, g_down, gate, up)
    dw_down = _matmul("dw_down", "nn",
                      [(act_t, pl.BlockSpec((None, fs, s), lambda j, n, k: (j, 0, 0)),
                        dff, pl.BlockSpec((s, tn), lambda j, n, k: (0, n)))],
                      (N_DEV, d // tn, 1), (fs, tn), _sds((N_DEV, fs, d), BF16),
                      pl.BlockSpec((None, fs, tn), lambda j, n, k: (j, 0, n)))

    def dw_up(name, dact):
        return _matmul(name, "nn",
                       [(u2_t, pl.BlockSpec((tn, s), lambda j, i, k: (i, 0)),
                         dact, pl.BlockSpec((None, s, fs), lambda j, i, k: (j, 0, 0)))],
                       (N_DEV, d // tn, 1), (tn, fs), _sds((N_DEV, d, fs), BF16),
                       pl.BlockSpec((None, tn, fs), lambda j, i, k: (j, i, 0)))

    dw_gate, dw_upw = dw_up("dw_gate", dgate), dw_up("dw_up", dup)
    rs_ffn = _scatter_pairs("ffn", [dw_gate, dw_upw, dw_down])
    a_spec = pl.BlockSpec((None, tm, fs), lambda i, j, k: (k, i, 0))
    b_spec = pl.BlockSpec((None, tw, fs), lambda i, j, k: (k, j, 0))
    du2 = _matmul("du2", "nt", [(dgate, a_spec, g_gate, b_spec), (dup, a_spec, g_up, b_spec)],
                  (s // tm, d // tw, N_DEV), (tm, tw), _sds((s, d), F32), pl.BlockSpec((tm, tw), lambda i, j, k: (i, j)),
                  dep=rs_ffn[4])
    rs_ffn = _scatter_chips("ffn", rs_ffn, du2)
    dh1, dmix, dg_ffn_pre, dg_mix_post = _mid_norms_bwd(dy, du2, h1, mix, norm_ffn_pre, norm_mix_post)

    da_sb, da_fx, dgf = _merge_bwd(dmix, w_out_full, gf, a_sb, a_fx, dep=rs_ffn[4])
    dw_out = _mm_plain("dw_out", "nn", merged_t, dmix, BF16).reshape(N_DEV, cs, d)

    def branch_bwd(tag, da, w_b, o_t, width):
        tb = _tile(width, 1024)
        do = _matmul("do_" + tag, "nt",
                     [(da, pl.BlockSpec((tm, cs), lambda i, j, k: (i, k)),
                       w_b, pl.BlockSpec((None, tb, cs), lambda i, j, k: (k, j, 0)))],
                     (s // tm, width // tb, N_DEV), (tm, tb), _sds((s, width), BF16),
                     pl.BlockSpec((tm, tb), lambda i, j, k: (i, j)))
        dw = _matmul("dw_" + tag, "nn",
                     [(o_t, pl.BlockSpec((width, s), lambda j, i, k: (0, 0)),
                       da, pl.BlockSpec((s, cs), lambda j, i, k: (0, j)))],
                     (N_DEV, 1, 1), (width, cs), _sds((N_DEV, width, cs), BF16),
                     pl.BlockSpec((None, width, cs), lambda j, i, k: (j, 0, 0)))
        return do, dw

    do_sb, dw_sb = branch_bwd("sb", da_sb, g_sb, o_sb_t, d_sb)
    do_fx, dw_fx = branch_bwd("fox", da_fx, g_fx, o_fx_t, d_fox)

    rs_mid = _scatter_pairs("mid", [dw_sb, dw_fx, dw_out])

    dqkv = _sb_bwd(qkv, do_sb, tot, h_sb, rs_mid[4])
    rs_mid = _scatter_chips("mid", rs_mid, dqkv)
    dqkv, dcum = _fox_bwd(dqkv, qkv, do_fx, o_fx32, lse, cum_col, cum_row, h_fox, h_sb, rs_mid[4])
    dgf, db_part = _forget_bwd(dgf, dcum, gf, b_pad, f_blk)
    dw_in = _w_in_grad_parts(_mm_plain("dw_qkv", "nn", u_t, dqkv, BF16), _mm_plain("dw_gates", "nn", u_t, dgf, BF16), lay)
    rs_in = _scatter_pairs("in", [dw_in])
    du = _mm_plain("du_qkv", "nt", dqkv, w_cat, F32, tn=1024, dep=rs_in[4])
    rs_in = _scatter_chips("in", rs_in, du)
    du = _mm_plain("du_gates", "nt", dgf, w_cat, F32, tn=1024, k_off=n_qkv, init=du, dep=rs_in[4])
    dx, dg_mix_pre = _pre_norm_bwd(dh1, du, xs, norm_mix_pre)

    upd = {}

    def update_group(tag, rs, names, after):
        parts = _scatter_end(tag, rs, after)
        for nm, p in zip(names, parts):
            w, m, v = decayed.get(nm, weights[nm])
            upd[nm] = _update("update_" + nm, p, w, m, v, layout=lay if nm == "w_in" else None, decayed=nm in decayed,
                              transposed_out=nm in ("w_in", "w_ffn_gate", "w_ffn_up"))

    update_group("ffn", rs_ffn, ("w_ffn_gate", "w_ffn_up", "w_ffn_down"), [dx])
    update_group("mid", rs_mid, ("w_branch_sb", "w_branch_fox", "w_out"), [upd[nm][3] for nm in ("w_ffn_gate", "w_ffn_up", "w_ffn_down")])
    update_group("in", rs_in, ("w_in",), [upd[nm][3] for nm in ("w_branch_sb", "w_branch_fox", "w_out")])

    small = ((norm_mix_pre, m_norm_mix_pre, v_norm_mix_pre), (norm_mix_post, m_norm_mix_post, v_norm_mix_post),
             (norm_ffn_pre, m_norm_ffn_pre, v_norm_ffn_pre), (norm_ffn_post, m_norm_ffn_post, v_norm_ffn_post))
    pad_f = ((0, 0), (0, LANES - n_f))
    cat = lambda i: jnp.concatenate([t[i] for t in small] + [jnp.pad((b_forget, m_b_forget, v_b_forget)[i], pad_f)], axis=1)
    sm = _small_update(jnp.concatenate([dg_mix_pre, dg_mix_post, dg_ffn_pre, dg_ffn_post, db_part], axis=1),
                       cat(0), cat(1), cat(2))
    for i, nm in enumerate(("norm_mix_pre", "norm_mix_post", "norm_ffn_pre", "norm_ffn_post")):
        upd[nm] = [o[:, i * d:(i + 1) * d] for o in sm]
    upd["b_forget"] = [o[:, 4 * d:4 * d + n_f] for o in sm]

    loss = lax.psum(loss_part[0, 0], ("x", "y", "c"))
    order = ("norm_mix_pre", "norm_mix_post", "w_in", "b_forget", "w_branch_sb", "w_branch_fox", "w_out",
             "norm_ffn_pre", "norm_ffn_post", "w_ffn_gate", "w_ffn_up", "w_ffn_down")
    return (loss, dx[None]) + tuple(upd[nm][i] for i in range(4) for nm in order)
```

```python
import jax
import jax.numpy as jnp
from jax import lax
from jax.experimental import pallas as pl
from jax.experimental.pallas import tpu as pltpu

F32 = jnp.float32
BF16 = jnp.bfloat16
MESH = pl.DeviceIdType.MESH
ANY = pl.BlockSpec(memory_space=pl.ANY)
HBM = pl.BlockSpec(memory_space=pltpu.HBM)
SEM = pl.BlockSpec(memory_space=pltpu.SEMAPHORE)
EFFECT = pltpu.SideEffectType.DATAFLOW_SIDE_EFFECTING

N_DEV = 8
HEAD_DIM = 128
RMS_EPS = 1e-6
F_PAD = 512
LANES = 128
ATT_TQ = 256
ATT_TK = 256
ATT_HP = 4
NEG_BIG = -1e30
VMEM_LIMIT = 56 * 1024 * 1024

ADAM_LR = 0.001
ADAM_B1 = 0.9
ADAM_B2 = 0.999
ADAM_EPS = 1e-08
ADAM_WD = 0.01
ADAM_STEP = 10

_DIMS = {"nn": ((1,), (0,)), "nt": ((1,), (1,)), "tn": ((0,), (0,))}


def _params(sem):
    return pltpu.CompilerParams(dimension_semantics=sem, vmem_limit_bytes=VMEM_LIMIT)


def _dot(a, b, mode="nn"):
    return lax.dot_general(a.astype(BF16), b.astype(BF16), (_DIMS[mode], ((), ())), preferred_element_type=F32)


def _tile(n, pref):
    if n <= pref:
        return n
    t = (pref // LANES) * LANES
    while n % t:
        t -= LANES
    return t


def _split2(v):
    hi = v.astype(BF16)
    return hi, (v - hi.astype(F32)).astype(BF16)


def _split3(v):
    a = v.astype(BF16)
    r = v - a.astype(F32)
    b = r.astype(BF16)
    return a, b, (r - b.astype(F32)).astype(BF16)


def _tri(n, cmp):
    r = lax.broadcasted_iota(jnp.int32, (n, n), 0)
    c = lax.broadcasted_iota(jnp.int32, (n, n), 1)
    return jnp.where(cmp(r, c), 1.0, 0.0).astype(BF16)


def _lane_pick(v, h):
    lane = lax.broadcasted_iota(jnp.int32, v.shape, 1)
    return jnp.sum(jnp.where(lane == h, v, 0.0), axis=1, keepdims=True)


def _lane_put(ref, rows, h, col):
    old = ref[rows, :]
    lane = lax.broadcasted_iota(jnp.int32, old.shape, 1)
    ref[rows, :] = jnp.where(lane == h, col, old)


def _sigmoid(z):
    return 1.0 / (1.0 + jnp.exp(-z))


def _log_sigmoid(z):
    return jnp.minimum(z, 0.0) - jnp.log(1.0 + jnp.exp(-jnp.abs(z)))


def _sds(shape, dtype):
    return jax.ShapeDtypeStruct(shape, dtype)


def _matmul(name, mode, pairs, grid, acc_shape, out_shape, out_specs, extras=(), epilogue=None, init=None, dep=None):
    n_p, n_e = len(pairs), len(extras)
    nk = grid[-1]
    single = not isinstance(out_shape, (list, tuple))
    n_i = 0 if init is None else 1
    n_d = 0 if dep is None else 1

    one_step = nk == 1 and init is None

    def body(*refs):
        ab = refs[:2 * n_p]
        ex = refs[2 * n_p:2 * n_p + n_e]
        ini = refs[2 * n_p + n_e:2 * n_p + n_e + n_i]
        outs = refs[2 * n_p + n_e + n_i + n_d:len(refs) - (0 if one_step else 1)]

        def finish(total):
            if epilogue is None:
                outs[0][...] = total.astype(outs[0].dtype)
            else:
                epilogue(total, ex, outs)

        t = _dot(ab[0][...], ab[1][...], mode)
        for p in range(1, n_p):
            t = t + _dot(ab[2 * p][...], ab[2 * p + 1][...], mode)
        if one_step:
            finish(t)
            return
        acc = refs[-1]
        k = pl.program_id(len(grid) - 1)

        @pl.when(k == 0)
        def _():
            acc[...] = t if init is None else ini[0][...].astype(F32) + t

        @pl.when(k > 0)
        def _():
            acc[...] += t

        @pl.when(k == nk - 1)
        def _():
            finish(acc[...])

    in_specs = [s for (_, sa, _, sb) in pairs for s in (sa, sb)] + [s for (_, s) in extras]
    args = [v for (a, _, b, _) in pairs for v in (a, b)] + [e for (e, _) in extras]
    if init is not None:
        in_specs.append(init[1])
        args.append(init[0])
    if dep is not None:
        in_specs.append(ANY)
        args.append(dep)
    return pl.pallas_call(
        body, name=name, grid=grid, in_specs=in_specs,
        out_specs=out_specs if single else list(out_specs),
        out_shape=out_shape if single else list(out_shape),
        scratch_shapes=[] if one_step else [pltpu.VMEM(acc_shape, F32)],
        compiler_params=_params(("parallel",) * (len(grid) - 1) + ("arbitrary",)),
    )(*args)


def _mm_plain(name, mode, a, b, out_dtype, *, n_off=0, n=None, k_off=0, tm=1024, tn=1536, tk=2048, init=None, dep=None):
    if mode == "nn":
        (m, kk), nn_ = a.shape, b.shape[1]
    elif mode == "nt":
        (m, kk), nn_ = a.shape, b.shape[0]
    else:
        (kk, m), nn_ = a.shape, b.shape[1]
    n = nn_ if n is None else n
    tm, tn, tk = _tile(m, tm), _tile(n, tn), _tile(kk, tk)
    while n_off % tn or n % tn:
        tn -= LANES
    while k_off % tk or kk % tk:
        tk -= LANES
    off, koff = n_off // tn, k_off // tk
    a_spec = {"nn": pl.BlockSpec((tm, tk), lambda i, j, k: (i, k)),
              "nt": pl.BlockSpec((tm, tk), lambda i, j, k: (i, k)),
              "tn": pl.BlockSpec((tk, tm), lambda i, j, k: (k, i))}[mode]
    b_spec = {"nn": pl.BlockSpec((tk, tn), lambda i, j, k: (k, j + off)),
              "nt": pl.BlockSpec((tn, tk), lambda i, j, k: (j, k + koff)),
              "tn": pl.BlockSpec((tk, tn), lambda i, j, k: (k, j))}[mode]
    o_spec = pl.BlockSpec((tm, tn), lambda i, j, k: (i, j))
    if init is not None:
        init = (init, o_spec)
    return _matmul(name, mode, [(a, a_spec, b, b_spec)], (m // tm, n // tn, kk // tk), (tm, tn),
                   _sds((m, n), out_dtype), o_spec, init=init, dep=dep)


def _rows_call(name, body, ins, outs, s, tr=256, dep=None):
    def spec(v, per_row):
        if per_row == "transposed":
            return pl.BlockSpec((v.shape[0], tr), lambda i: (0, i))
        if per_row:
            return pl.BlockSpec((tr, v.shape[1]), lambda i: (i, 0))
        return pl.BlockSpec(v.shape, lambda i: (0, 0))
    n_in = len(ins)
    deps = [] if dep is None else [dep]

    def with_dep(*refs):
        body(*refs[:n_in], *refs[n_in + len(deps):])

    return pl.pallas_call(
        with_dep, name=name, grid=(s // tr,),
        in_specs=[spec(v, p) for v, p in ins] + [ANY] * len(deps), out_specs=[spec(v, p) for v, p in outs],
        out_shape=[_sds(v.shape, v.dtype) for v, _ in outs],
        compiler_params=_params(("arbitrary",)),
    )(*[v for v, _ in ins], *deps)


def _rsq(v):
    return lax.rsqrt(jnp.mean(v * v, axis=-1, keepdims=True) + RMS_EPS)


def _norm_bwd(dy, v, r, g):
    vh = v * r
    t = dy * g
    dv = r * (t - vh * jnp.mean(t * vh, axis=-1, keepdims=True))
    return dv, jnp.sum(dy * vh, axis=0, keepdims=True)


def _accum(ref, val):
    @pl.when(pl.program_id(0) == 0)
    def _():
        ref[...] = jnp.zeros_like(ref)
    ref[...] += val


def _pre_norm(x, g, dep=None):
    def body(x_ref, g_ref, u_ref, ut_ref):
        v = x_ref[...]
        u = (v * _rsq(v) * g_ref[...]).astype(BF16)
        u_ref[...] = u
        ut_ref[...] = u.T
    s, d = x.shape
    return _rows_call("pre_norm", body, [(x, True), (g, False)],
                      [(_sds((s, d), BF16), True), (_sds((d, s), BF16), "transposed")], s, dep=dep)


def _mid_norms(x, mix, g_post, g_pre):
    def body(x_ref, mix_ref, gp_ref, gn_ref, h_ref, u_ref, ut_ref):
        mv = mix_ref[...]
        h = x_ref[...] + mv * _rsq(mv) * gp_ref[...]
        h_ref[...] = h
        u = (h * _rsq(h) * gn_ref[...]).astype(BF16)
        u_ref[...] = u
        ut_ref[...] = u.T
    s, d = x.shape
    return _rows_call("mid_norms", body, [(x, True), (mix, True), (g_post, False), (g_pre, False)],
                      [(_sds((s, d), F32), True), (_sds((s, d), BF16), True), (_sds((d, s), BF16), "transposed")], s)


def _loss_head(h1, ff, target, g):
    s, d = h1.shape

    def body(h_ref, ff_ref, t_ref, g_ref, loss_ref, dy_ref, dff_ref, dg_ref):
        fv = ff_ref[...]
        r = _rsq(fv)
        err = h_ref[...] + fv * r * g_ref[...] - t_ref[...]
        part = 0.5 * jnp.sum(jnp.mean(err * err, axis=-1, keepdims=True), axis=0, keepdims=True)
        _accum(loss_ref, jnp.broadcast_to(part, loss_ref.shape))
        dy = err * (1.0 / d)
        dy_ref[...] = dy
        dff, dg = _norm_bwd(dy, fv, r, g_ref[...])
        dff_ref[...] = dff.astype(BF16)
        _accum(dg_ref, dg)

    return _rows_call("loss_head", body, [(h1, True), (ff, True), (target, True), (g, False)],
                      [(_sds((1, LANES), F32), False), (_sds((s, d), F32), True),
                       (_sds((s, d), BF16), True), (_sds((1, d), F32), False)], s)


def _mid_norms_bwd(dy, du2, h1, mix, g_pre, g_post):
    s, d = dy.shape

    def body(dy_ref, du_ref, h_ref, mix_ref, gn_ref, gp_ref, dh_ref, dmix_ref, dgn_ref, dgp_ref):
        h = h_ref[...]
        dh, dgn = _norm_bwd(du_ref[...], h, _rsq(h), gn_ref[...])
        dh = dh + dy_ref[...]
        dh_ref[...] = dh
        _accum(dgn_ref, dgn)
        mv = mix_ref[...]
        dmix, dgp = _norm_bwd(dh, mv, _rsq(mv), gp_ref[...])
        dmix_ref[...] = dmix.astype(BF16)
        _accum(dgp_ref, dgp)

    return _rows_call("mid_norms_bwd", body,
                      [(dy, True), (du2, True), (h1, True), (mix, True), (g_pre, False), (g_post, False)],
                      [(_sds((s, d), F32), True), (_sds((s, d), BF16), True),
                       (_sds((1, d), F32), False), (_sds((1, d), F32), False)], s)


def _pre_norm_bwd(dh1, du, x, g, dep=None):
    s, d = x.shape

    def body(dh_ref, du_ref, x_ref, g_ref, dx_ref, dg_ref):
        v = x_ref[...]
        dv, dg = _norm_bwd(du_ref[...], v, _rsq(v), g_ref[...])
        dx_ref[...] = dh_ref[...] + dv
        _accum(dg_ref, dg)

    return _rows_call("pre_norm_bwd", body, [(dh1, True), (du, True), (x, True), (g, False)],
                      [(_sds((s, d), F32), True), (_sds((1, d), F32), False)], s, dep=dep)


def _forget_fwd(gf, b_pad, f_blk):
    s = gf.shape[0]
    tb = ATT_TK
    nb = s // tb

    def body(f_ref, b_ref, col_ref, row_ref):
        incl = _tri(tb, lambda r, c: c <= r)
        carry = jnp.zeros((1, LANES), F32)
        for i in range(nb):
            lf = _log_sigmoid(f_ref[pl.ds(i * tb, tb), :] + b_ref[...])
            parts = _split3(lf)
            cum = carry + _dot(incl, parts[0]) + _dot(incl, parts[1]) + _dot(incl, parts[2])
            col_ref[pl.ds(i * tb, tb), :] = cum
            row_ref[i] = cum.T
            carry = carry + jnp.sum(lf, axis=0, keepdims=True)

    return pl.pallas_call(
        body, name="forget_fwd", grid=(1,),
        in_specs=[pl.BlockSpec((s, LANES), lambda i: (0, f_blk)), pl.BlockSpec((1, LANES), lambda i: (0, 0))],
        out_specs=[pl.BlockSpec((s, LANES), lambda i: (0, 0)), pl.BlockSpec((nb, LANES, tb), lambda i: (0, 0, 0))],
        out_shape=[_sds((s, LANES), F32), _sds((nb, LANES, tb), F32)],
        compiler_params=_params(("arbitrary",)),
    )(gf, b_pad)


def _forget_bwd(dgf, dcum, gf, b_pad, f_blk):
    s = gf.shape[0]
    tb = ATT_TK
    nb = s // tb
    sec = dgf.shape[1] // F_PAD - 1

    def body(dgf_hbm, dc_ref, f_ref, b_ref, out_ref, db_ref):
        del dgf_hbm
        incl = _tri(tb, lambda r, c: c >= r)
        carry = jnp.zeros((1, LANES), F32)
        db = jnp.zeros((1, LANES), F32)
        out_ref[...] = jnp.zeros_like(out_ref)
        for i in reversed(range(nb)):
            dc = dc_ref[pl.ds(i * tb, tb), :]
            parts = _split3(dc)
            dlf = carry + _dot(incl, parts[0]) + _dot(incl, parts[1]) + _dot(incl, parts[2])
            z = f_ref[pl.ds(i * tb, tb), :] + b_ref[...]
            df = dlf * _sigmoid(-z)
            out_ref[pl.ds(i * tb, tb), pl.ds(0, LANES)] = df.astype(BF16)
            db = db + jnp.sum(df, axis=0, keepdims=True)
            carry = carry + jnp.sum(dc, axis=0, keepdims=True)
        db_ref[...] = db

    return pl.pallas_call(
        body, name="forget_bwd", grid=(1,),
        in_specs=[ANY, pl.BlockSpec((s, LANES), lambda i: (0, 0)),
                  pl.BlockSpec((s, LANES), lambda i: (0, f_blk)), pl.BlockSpec((1, LANES), lambda i: (0, 0))],
        out_specs=[pl.BlockSpec((s, F_PAD), lambda i: (0, sec)), pl.BlockSpec((1, LANES), lambda i: (0, 0))],
        out_shape=[_sds(dgf.shape, BF16), _sds((1, LANES), F32)],
        input_output_aliases={0: 0},
        compiler_params=_params(("arbitrary",)),
    )(dgf, dcum, gf, b_pad)


def _diag_mask(strict):
    r = lax.broadcasted_iota(jnp.int32, (ATT_TQ, ATT_TK), 0)
    c = lax.broadcasted_iota(jnp.int32, (ATT_TQ, ATT_TK), 1)
    return c < r if strict else c <= r


def _qkv_specs(hb0, s):
    specs = []
    for j in range(ATT_HP):
        def col(g, j=j):
            return 3 * (hb0 + ATT_HP * g + j)
        specs += [pl.BlockSpec((ATT_TQ, HEAD_DIM), lambda g, i, col=col: (i, col(g))),
                  pl.BlockSpec((s, HEAD_DIM), lambda g, i, col=col: (0, col(g) + 1)),
                  pl.BlockSpec((s, HEAD_DIM), lambda g, i, col=col: (0, col(g) + 2))]
    return specs


def _head_cols(j):
    return pl.ds(j * HEAD_DIM, HEAD_DIM)


def _sb_fwd(qkv, n_heads):
    s = qkv.shape[0]
    scale = HEAD_DIM ** -0.5
    tq, tk = ATT_TQ, ATT_TK
    heads = range(ATT_HP)

    def body(*refs):
        qkv_refs, (o_ref, ot_ref, tot_ref) = refs[:3 * ATT_HP], refs[3 * ATT_HP:]
        g, i = pl.program_id(0), pl.program_id(1)

        @pl.when((g == 0) & (i == 0))
        def _():
            tot_ref[...] = jnp.zeros_like(tot_ref)

        qs = [qkv_refs[3 * j][...] for j in heads]
        upper = _tri(tk, lambda r, c: r > c)

        def tile(kj, carry, mask):
            rows = pl.ds(pl.multiple_of(kj * tk, tk), tk)
            z = [_dot(qs[j], qkv_refs[3 * j + 1][rows, :], "nt") * scale for j in heads]
            lsz = [_log_sigmoid(z[j]) for j in heads]
            lk = [lsz[j] - z[j] if mask is None else jnp.where(mask, lsz[j] - z[j], 0.0) for j in heads]
            parts = [_split2(lk[j]) for j in heads]
            above = [carry[j][0] + _dot(parts[j][0], upper) + _dot(parts[j][1], upper) for j in heads]
            w = [jnp.exp(lsz[j] + above[j]) for j in heads]
            if mask is not None:
                w = [jnp.where(mask, w[j], 0.0) for j in heads]
            return tuple((carry[j][0] + jnp.sum(lk[j], axis=1, keepdims=True),
                          carry[j][1] + _dot(w[j], qkv_refs[3 * j + 2][rows, :])) for j in heads)

        carry = tile(i, tuple((jnp.zeros((tq, 1), F32), jnp.zeros((tq, HEAD_DIM), F32)) for _ in heads), _diag_mask(True))
        carry = lax.fori_loop(0, i, lambda n, cr: tile(i - 1 - n, cr, None), carry)
        q_rows = pl.ds(pl.multiple_of(i * tq, tq), tq)
        for j in heads:
            c, acc = carry[j]
            o = acc.astype(BF16)
            o_ref[:, _head_cols(j)] = o
            ot_ref[_head_cols(j), :] = o.T
            _lane_put(tot_ref, q_rows, ATT_HP * g + j, c)

    wide = ATT_HP * HEAD_DIM
    return pl.pallas_call(
        body, name="sb_fwd", grid=(n_heads // ATT_HP, s // tq),
        in_specs=_qkv_specs(0, s),
        out_specs=[pl.BlockSpec((tq, wide), lambda g, i: (i, g)), pl.BlockSpec((wide, tq), lambda g, i: (g, i)),
                   pl.BlockSpec((s, LANES), lambda g, i: (0, 0))],
        out_shape=[_sds((s, n_heads * HEAD_DIM), BF16), _sds((n_heads * HEAD_DIM, s), BF16), _sds((s, LANES), F32)],
        compiler_params=_params(("arbitrary", "arbitrary")),
    )(*[qkv] * (3 * ATT_HP))


def _sb_bwd(qkv, do, tot, n_heads, dep):
    s = qkv.shape[0]
    scale = HEAD_DIM ** -0.5
    tq, tk = ATT_TQ, ATT_TK
    nq = s // tq
    hd = HEAD_DIM

    heads = range(ATT_HP)

    def body(*refs):
        qkv_refs = refs[:3 * ATT_HP]
        do_ref, tot_ref, _, out_ref, dk_acc, dv_acc = refs[3 * ATT_HP:]
        g, i = pl.program_id(0), pl.program_id(1)

        @pl.when(i == 0)
        def _():
            dk_acc[...] = jnp.zeros_like(dk_acc)
            dv_acc[...] = jnp.zeros_like(dv_acc)

        qs = [qkv_refs[3 * j][...] for j in heads]
        douts = [do_ref[:, _head_cols(j)] for j in heads]
        totals = [_lane_pick(tot_ref[...], ATT_HP * g + j) for j in heads]
        incl = _tri(tk, lambda r, c: r <= c)
        excl = _tri(tk, lambda r, c: r < c)

        def tile(kj, carry, mask):
            rows = pl.ds(pl.multiple_of(kj * tk, tk), tk)
            k_t = [qkv_refs[3 * j + 1][rows, :] for j in heads]
            z = [_dot(qs[j], k_t[j], "nt") * scale for j in heads]
            dw = [_dot(douts[j], qkv_refs[3 * j + 2][rows, :], "nt") for j in heads]
            lsz = [_log_sigmoid(z[j]) for j in heads]
            lk = [lsz[j] - z[j] if mask is None else jnp.where(mask, lsz[j] - z[j], 0.0) for j in heads]
            parts = [_split2(lk[j]) for j in heads]
            below = [carry[j][0] + _dot(parts[j][0], incl) + _dot(parts[j][1], incl) for j in heads]
            w = [jnp.exp(lsz[j] + (totals[j] - below[j])) for j in heads]
            if mask is not None:
                w = [jnp.where(mask, w[j], 0.0) for j in heads]
            e = [dw[j] * w[j] for j in heads]
            parts = [_split2(e[j]) for j in heads]
            e_before = [carry[j][1] + _dot(parts[j][0], excl) + _dot(parts[j][1], excl) for j in heads]
            sg = [jnp.exp(lsz[j]) for j in heads]
            dz = [e[j] * (1.0 - sg[j]) - e_before[j] * sg[j] for j in heads]
            if mask is not None:
                dz = [jnp.where(mask, dz[j], 0.0) for j in heads]
            dz = [(dz[j] * scale).astype(BF16) for j in heads]
            for j in heads:
                dk_acc[j, rows, :] += _dot(dz[j], qs[j], "tn")
                dv_acc[j, rows, :] += _dot(w[j], douts[j], "tn")
            return tuple((carry[j][0] + jnp.sum(lk[j], axis=1, keepdims=True),
                          carry[j][1] + jnp.sum(e[j], axis=1, keepdims=True),
                          carry[j][2] + _dot(dz[j], k_t[j])) for j in heads)

        zero = jnp.zeros((tq, 1), F32)
        carry = lax.fori_loop(0, i, lambda kj, cr: tile(kj, cr, None),
                              tuple((zero, zero, jnp.zeros((tq, hd), F32)) for _ in heads))
        carry = tile(i, carry, _diag_mask(True))
        for j in heads:
            out_ref[pl.ds(pl.multiple_of(i * tq, tq), tq), pl.ds(3 * j * hd, hd)] = carry[j][2].astype(BF16)

        @pl.when(i == nq - 1)
        def _():
            for j in heads:
                out_ref[:, pl.ds((3 * j + 1) * hd, hd)] = dk_acc[j].astype(BF16)
                out_ref[:, pl.ds((3 * j + 2) * hd, hd)] = dv_acc[j].astype(BF16)

    wide = ATT_HP * hd
    return pl.pallas_call(
        body, name="sb_bwd", grid=(n_heads // ATT_HP, nq),
        in_specs=_qkv_specs(0, s) + [pl.BlockSpec((tq, wide), lambda g, i: (i, g)),
                                     pl.BlockSpec((tq, LANES), lambda g, i: (i, 0)), ANY],
        out_specs=pl.BlockSpec((s, 3 * wide), lambda g, i: (0, g)),
        out_shape=_sds(qkv.shape, BF16),
        scratch_shapes=[pltpu.VMEM((ATT_HP, s, hd), F32), pltpu.VMEM((ATT_HP, s, hd), F32)],
        compiler_params=_params(("arbitrary", "arbitrary")),
    )(*[qkv] * (3 * ATT_HP), do, tot, dep)


def _fox_fwd(qkv, cum_col, cum_row, n_heads, hb0, dep):
    s = qkv.shape[0]
    scale = HEAD_DIM ** -0.5
    tq, tk = ATT_TQ, ATT_TK

    heads = range(ATT_HP)

    def body(*refs):
        qkv_refs = refs[:3 * ATT_HP]
        cc_ref, cr_ref, _, o_ref, ot_ref, o32_ref, lse_ref = refs[3 * ATT_HP:]
        g, i = pl.program_id(0), pl.program_id(1)

        @pl.when((g == 0) & (i == 0))
        def _():
            lse_ref[...] = jnp.zeros_like(lse_ref)

        qs = [qkv_refs[3 * j][...] for j in heads]
        cqs = [_lane_pick(cc_ref[...], ATT_HP * g + j) for j in heads]

        def tile(kj, carry, mask):
            rows = pl.ds(pl.multiple_of(kj * tk, tk), tk)
            sc = [_dot(qs[j], qkv_refs[3 * j + 1][rows, :], "nt") * scale + cqs[j]
                  - cr_ref[kj, pl.ds(ATT_HP * g + j, 1), :] for j in heads]
            if mask is not None:
                sc = [jnp.where(mask, sc[j], NEG_BIG) for j in heads]
            m_new = [jnp.maximum(carry[j][0], jnp.max(sc[j], axis=1, keepdims=True)) for j in heads]
            p = [jnp.exp(sc[j] - m_new[j]) for j in heads]
            alpha = [jnp.exp(carry[j][0] - m_new[j]) for j in heads]
            parts = [_split2(p[j]) for j in heads]
            v_t = [qkv_refs[3 * j + 2][rows, :] for j in heads]
            pv = [_dot(parts[j][0], v_t[j]) + _dot(parts[j][1], v_t[j]) for j in heads]
            return tuple((m_new[j], alpha[j] * carry[j][1] + jnp.sum(p[j], axis=1, keepdims=True),
                          alpha[j] * carry[j][2] + pv[j]) for j in heads)

        carry = tuple((jnp.full((tq, 1), NEG_BIG, F32), jnp.zeros((tq, 1), F32), jnp.zeros((tq, HEAD_DIM), F32))
                      for _ in heads)
        carry = lax.fori_loop(0, i, lambda kj, cr: tile(kj, cr, None), carry)
        carry = tile(i, carry, _diag_mask(False))
        q_rows = pl.ds(pl.multiple_of(i * tq, tq), tq)
        for j in heads:
            m, l, acc = carry[j]
            o = acc / l
            o_ref[:, _head_cols(j)] = o.astype(BF16)
            ot_ref[_head_cols(j), :] = o.astype(BF16).T
            o32_ref[:, _head_cols(j)] = o
            _lane_put(lse_ref, q_rows, ATT_HP * g + j, m + jnp.log(l))

    nb = cum_row.shape[0]
    wide = ATT_HP * HEAD_DIM
    return pl.pallas_call(
        body, name="fox_fwd", grid=(n_heads // ATT_HP, s // tq),
        in_specs=_qkv_specs(hb0, s) + [pl.BlockSpec((tq, LANES), lambda g, i: (i, 0)),
                                       pl.BlockSpec((nb, 8, tk), lambda g, i: (0, 0, 0)), ANY],
        out_specs=[pl.BlockSpec((tq, wide), lambda g, i: (i, g)), pl.BlockSpec((wide, tq), lambda g, i: (g, i)),
                   pl.BlockSpec((tq, wide), lambda g, i: (i, g)), pl.BlockSpec((s, LANES), lambda g, i: (0, 0))],
        out_shape=[_sds((s, n_heads * HEAD_DIM), BF16), _sds((n_heads * HEAD_DIM, s), BF16),
                   _sds((s, n_heads * HEAD_DIM), F32), _sds((s, LANES), F32)],
        compiler_params=_params(("arbitrary", "arbitrary")),
    )(*[qkv] * (3 * ATT_HP), cum_col, cum_row, dep)


def _fox_bwd(dqkv, qkv, do, o, lse, cum_col, cum_row, n_heads, hb0, dep):
    s = qkv.shape[0]
    scale = HEAD_DIM ** -0.5
    tq, tk = ATT_TQ, ATT_TK
    nq = s // tq
    hd = HEAD_DIM

    heads = range(ATT_HP)
    assert hb0 % ATT_HP == 0

    def body(*refs):
        qkv_refs = refs[1:1 + 3 * ATT_HP]
        do_ref, o_ref, lse_ref, cc_ref, cr_ref, _, out_ref, dc_ref, dk_acc, dv_acc, col_acc = refs[1 + 3 * ATT_HP:]
        g, i = pl.program_id(0), pl.program_id(1)

        @pl.when((g == 0) & (i == 0))
        def _():
            dc_ref[...] = jnp.zeros_like(dc_ref)

        @pl.when(i == 0)
        def _():
            dk_acc[...] = jnp.zeros_like(dk_acc)
            dv_acc[...] = jnp.zeros_like(dv_acc)
            col_acc[...] = jnp.zeros_like(col_acc)

        qs = [qkv_refs[3 * j][...] for j in heads]
        douts = [do_ref[:, _head_cols(j)] for j in heads]
        deltas = [jnp.sum(douts[j].astype(F32) * o_ref[:, _head_cols(j)], axis=1, keepdims=True) for j in heads]
        shifts = [_lane_pick(cc_ref[...], ATT_HP * g + j) - _lane_pick(lse_ref[...], ATT_HP * g + j) for j in heads]

        def tile(kj, carry, mask):
            rows = pl.ds(pl.multiple_of(kj * tk, tk), tk)
            k_t = [qkv_refs[3 * j + 1][rows, :] for j in heads]
            sc = [_dot(qs[j], k_t[j], "nt") * scale + shifts[j] - cr_ref[kj, pl.ds(ATT_HP * g + j, 1), :] for j in heads]
            dp = [_dot(douts[j], qkv_refs[3 * j + 2][rows, :], "nt") for j in heads]
            p = [jnp.exp(sc[j]) for j in heads]
            if mask is not None:
                p = [jnp.where(mask, p[j], 0.0) for j in heads]
            ds_f = [p[j] * (dp[j] - deltas[j]) for j in heads]
            ds = [(ds_f[j] * scale).astype(BF16) for j in heads]
            for j in heads:
                col_acc[j, kj] += jnp.broadcast_to(jnp.sum(ds_f[j], axis=0, keepdims=True), (8, tk))
                dk_acc[j, rows, :] += _dot(ds[j], qs[j], "tn")
                dv_acc[j, rows, :] += _dot(p[j], douts[j], "tn")
            return tuple((carry[j][0] + _dot(ds[j], k_t[j]), carry[j][1] + jnp.sum(ds_f[j], axis=1, keepdims=True))
                         for j in heads)

        carry = lax.fori_loop(0, i, lambda kj, cr: tile(kj, cr, None),
                              tuple((jnp.zeros((tq, hd), F32), jnp.zeros((tq, 1), F32)) for _ in heads))
        carry = tile(i, carry, _diag_mask(False))
        q_rows = pl.ds(pl.multiple_of(i * tq, tq), tq)
        for j in heads:
            out_ref[q_rows, pl.ds(3 * j * hd, hd)] = carry[j][0].astype(BF16)
            _lane_put(dc_ref, q_rows, ATT_HP * g + j, carry[j][1])

        @pl.when(i == nq - 1)
        def _():
            lane = lax.broadcasted_iota(jnp.int32, (tk, LANES), 1)
            for j in heads:
                out_ref[:, pl.ds((3 * j + 1) * hd, hd)] = dk_acc[j].astype(BF16)
                out_ref[:, pl.ds((3 * j + 2) * hd, hd)] = dv_acc[j].astype(BF16)
                for kj in range(nb):
                    col = jnp.broadcast_to(col_acc[j, kj][0:1, :], (LANES, tk)).T
                    old = dc_ref[pl.ds(kj * tk, tk), :]
                    dc_ref[pl.ds(kj * tk, tk), :] = jnp.where(lane == ATT_HP * g + j, old - col, old)

    nb = cum_row.shape[0]
    wide = ATT_HP * hd
    return pl.pallas_call(
        body, name="fox_bwd", grid=(n_heads // ATT_HP, nq),
        in_specs=[ANY] + _qkv_specs(hb0, s) + [
            pl.BlockSpec((tq, wide), lambda g, i: (i, g)), pl.BlockSpec((tq, wide), lambda g, i: (i, g)),
            pl.BlockSpec((tq, LANES), lambda g, i: (i, 0)), pl.BlockSpec((tq, LANES), lambda g, i: (i, 0)),
            pl.BlockSpec((nb, 8, tk), lambda g, i: (0, 0, 0)), ANY],
        out_specs=[pl.BlockSpec((s, 3 * wide), lambda g, i: (0, hb0 // ATT_HP + g)),
                   pl.BlockSpec((s, LANES), lambda g, i: (0, 0))],
        out_shape=[_sds(dqkv.shape, BF16), _sds((s, LANES), F32)],
        scratch_shapes=[pltpu.VMEM((ATT_HP, s, hd), F32), pltpu.VMEM((ATT_HP, s, hd), F32),
                        pltpu.VMEM((ATT_HP, s // tk, 8, tk), F32)],
        input_output_aliases={0: 0},
        compiler_params=_params(("arbitrary", "arbitrary")),
    )(dqkv, *[qkv] * (3 * ATT_HP), do, o, lse, cum_col, cum_row, dep)


def _branch_merge(o_sb, o_fx, w_sb, w_fx, gf, dep, tm=1024):
    s = o_sb.shape[0]
    cs = w_sb.shape[2]
    tm = _tile(s, tm)

    def body(osb_ref, ofx_ref, wsb_ref, wfx_ref, g_ref, dep_ref, merged_ref, mt_ref, asb_ref, afx_ref):
        del dep_ref
        a_sb = _dot(osb_ref[...], wsb_ref[...])
        a_fx = _dot(ofx_ref[...], wfx_ref[...])
        g = g_ref[...]
        merged = (_sigmoid(g[:, :cs]) * a_sb + _sigmoid(g[:, cs:]) * a_fx).astype(BF16)
        merged_ref[...] = merged
        mt_ref[...] = merged.T
        asb_ref[...] = a_sb.astype(BF16)
        afx_ref[...] = a_fx.astype(BF16)

    blk = pl.BlockSpec((tm, cs), lambda i, j: (i, j))
    out = _sds((s, N_DEV * cs), BF16)
    return pl.pallas_call(
        body, name="branch_merge", grid=(s // tm, N_DEV),
        in_specs=[pl.BlockSpec((tm, o_sb.shape[1]), lambda i, j: (i, 0)),
                  pl.BlockSpec((tm, o_fx.shape[1]), lambda i, j: (i, 0)),
                  pl.BlockSpec((None,) + w_sb.shape[1:], lambda i, j: (j, 0, 0)),
                  pl.BlockSpec((None,) + w_fx.shape[1:], lambda i, j: (j, 0, 0)),
                  pl.BlockSpec((tm, 2 * cs), lambda i, j: (i, j)), ANY],
        out_specs=[blk, pl.BlockSpec((cs, tm), lambda i, j: (j, i)), blk, blk],
        out_shape=[out, _sds((N_DEV * cs, s), BF16), out, out],
        compiler_params=_params(("parallel", "arbitrary")),
    )(o_sb, o_fx, w_sb, w_fx, gf, dep)


def _merge_bwd(dmix, w_out, gf, a_sb, a_fx, tm=1024, tk=2048, dep=None):
    s, d = dmix.shape
    cs = d // N_DEV
    tm, tk = _tile(s, tm), _tile(d, tk)

    def epilogue(acc, ex, outs):
        g, a_sb, a_fx = ex[0][...], ex[1][...].astype(F32), ex[2][...].astype(F32)
        s_sb, s_fx = _sigmoid(g[:, :cs]), _sigmoid(g[:, cs:])
        outs[0][...] = (acc * s_sb).astype(BF16)
        outs[1][...] = (acc * s_fx).astype(BF16)
        outs[2][...] = jnp.concatenate([acc * a_sb * s_sb * (1.0 - s_sb), acc * a_fx * s_fx * (1.0 - s_fx)],
                                       axis=1).astype(BF16)

    blk = pl.BlockSpec((tm, cs), lambda i, j, k: (i, j))
    wide = pl.BlockSpec((tm, 2 * cs), lambda i, j, k: (i, j))
    return _matmul(
        "merge_bwd", "nt",
        [(dmix, pl.BlockSpec((tm, tk), lambda i, j, k: (i, k)), w_out, pl.BlockSpec((cs, tk), lambda i, j, k: (j, k)))],
        (s // tm, N_DEV, d // tk), (tm, cs),
        [_sds((s, d), BF16), _sds((s, d), BF16), _sds(gf.shape, BF16)], [blk, blk, wide],
        extras=[(gf, wide), (a_sb, blk), (a_fx, blk)], epilogue=epilogue, dep=dep)


def _ffn_up(u2, w_gate, w_up, dep, tm=1024):
    s, d = u2.shape
    fs = w_gate.shape[2]
    tm = _tile(s, tm)

    def body(u_ref, wg_ref, wu_ref, dep_ref, gate_ref, up_ref, act_ref, actt_ref):
        del dep_ref
        halves = [pl.ds(h * (tm // 2), tm // 2) for h in range(2)]
        gates = [_dot(u_ref[r, :], wg_ref[...]) for r in halves]
        ups = [_dot(u_ref[r, :], wu_ref[...]) for r in halves]
        for r, gate, up in zip(halves, gates, ups):
            gate_ref[r, :] = gate
            up_ref[r, :] = up
            act = (gate * _sigmoid(gate) * up).astype(BF16)
            act_ref[r, :] = act
            actt_ref[:, r] = act.T

    w_spec = pl.BlockSpec((None, d, fs), lambda i, j: (j, 0, 0))
    o_spec = pl.BlockSpec((None, tm, fs), lambda i, j: (j, i, 0))
    return pl.pallas_call(
        body, name="ffn_up", grid=(s // tm, N_DEV),
        in_specs=[pl.BlockSpec((tm, d), lambda i, j: (i, 0)), w_spec, w_spec, ANY],
        out_specs=[o_spec, o_spec, o_spec, pl.BlockSpec((None, fs, tm), lambda i, j: (j, 0, i))],
        out_shape=[_sds((N_DEV, s, fs), F32), _sds((N_DEV, s, fs), F32), _sds((N_DEV, s, fs), BF16),
                   _sds((N_DEV, fs, s), BF16)],
        compiler_params=_params(("parallel", "arbitrary")),
    )(u2, w_gate, w_up, dep)


def _ffn_down_bwd(dff, w_down, gate, up, tm=1024):
    s, d = dff.shape
    fs = w_down.shape[1]
    tm = _tile(s, tm)

    def body(dff_ref, wd_ref, gate_ref, up_ref, dgate_ref, dup_ref):
        halves = [pl.ds(h * (tm // 2), tm // 2) for h in range(2)]
        dact = [_dot(dff_ref[r, :], wd_ref[...], "nt") for r in halves]
        for r, da in zip(halves, dact):
            gate = gate_ref[r, :]
            sg = _sigmoid(gate)
            dup_ref[r, :] = (da * gate * sg).astype(BF16)
            dgate_ref[r, :] = (da * up_ref[r, :] * sg * (1.0 + gate * (1.0 - sg))).astype(BF16)

    a_spec = pl.BlockSpec((None, tm, fs), lambda i, j: (j, i, 0))
    return pl.pallas_call(
        body, name="ffn_down_bwd", grid=(s // tm, N_DEV),
        in_specs=[pl.BlockSpec((tm, d), lambda i, j: (i, 0)), pl.BlockSpec((None, fs, d), lambda i, j: (j, 0, 0)),
                  a_spec, a_spec],
        out_specs=[a_spec, a_spec],
        out_shape=[_sds((N_DEV, s, fs), BF16), _sds((N_DEV, s, fs), BF16)],
        compiler_params=_params(("parallel", "arbitrary")),
    )(dff, w_down, gate, up)


def _mesh_place():
    x, y, c = lax.axis_index("x"), lax.axis_index("y"), lax.axis_index("c")
    peers = []
    for d in range(1, N_DEV):
        px = 1 - x if d & 4 else x
        py = 1 - y if d & 2 else y
        pc = 1 - c if d & 1 else c
        peers.append((d, (px, py, pc), 4 * px + 2 * py + pc))
    return 4 * x + 2 * y + c, peers


def _flat_me():
    return 4 * lax.axis_index("x") + 2 * lax.axis_index("y") + lax.axis_index("c")


def _in_hbm(a):
    return pltpu.with_memory_space_constraint(a, pltpu.HBM)


def _pair_plan():
    x, y, c = lax.axis_index("x"), lax.axis_index("y"), lax.axis_index("c")
    return [(2 * q + (1 - c), q, q, (x, y, 1 - c)) for q in range(4)]


def _chip_plan():
    x, y, c = lax.axis_index("x"), lax.axis_index("y"), lax.axis_index("c")
    plan = []
    for fx, fy in ((1, 0), (0, 1), (1, 1)):
        cx, cy = (1 - x if fx else x), (1 - y if fy else y)
        plan.append((2 * cx + cy, 2 * x + y, 2 * cx + cy, (cx, cy, c)))
    return plan


def _split_start(name, srcs, lands, plan, k):
    n = len(srcs)

    def body(*refs):
        ins, lnd = refs[:n], refs[n:2 * n]
        send, recv, token = refs[2 * n], refs[2 * n + 1], refs[-1]
        copies = plan()
        for a in range(n):
            for t, (src, dst, _, dev) in enumerate(copies):
                pltpu.make_async_remote_copy(src_ref=ins[a].at[src], dst_ref=lnd[a].at[dst], send_sem=send.at[k * a + t],
                                             recv_sem=recv.at[k * a + t], device_id=dev, device_id_type=MESH).start()
        token[...] = jnp.zeros_like(token)

    res = pl.pallas_call(
        body, name=name,
        out_shape=[pltpu.SemaphoreType.DMA((n * k,)), pltpu.SemaphoreType.DMA((n * k,))]
        + [pltpu.HBM(a.shape, a.dtype) for a in list(srcs) + list(lands)] + [_sds((8, LANES), F32)],
        in_specs=[HBM] * (2 * n), out_specs=[SEM, SEM] + [HBM] * (2 * n) + [pl.BlockSpec(memory_space=pltpu.VMEM)],
        input_output_aliases={i: 2 + i for i in range(2 * n)},
        compiler_params=pltpu.CompilerParams(has_side_effects=EFFECT),
    )(*[_in_hbm(a) for a in srcs], *[_in_hbm(a) for a in lands])
    return res[0], res[1], res[2:2 + n], res[2 + n:2 + 2 * n], res[-1]


def _split_wait(name, send, recv, srcs, lands, plan, k, after):
    n = len(srcs)

    def body(*refs):
        ins, lnd = refs[:n], refs[n:2 * n]
        send_sem, recv_sem = refs[2 * n], refs[2 * n + 1]
        copies = plan()
        for a in range(n):
            for t, (src, _, dst, dev) in enumerate(copies):
                cp = pltpu.make_async_remote_copy(src_ref=ins[a].at[src], dst_ref=lnd[a].at[dst], send_sem=send_sem.at[k * a + t],
                                                  recv_sem=recv_sem.at[k * a + t], device_id=dev, device_id_type=MESH)
                cp.wait_send()
                cp.wait_recv()

    res = pl.pallas_call(
        body, name=name,
        out_shape=[pltpu.HBM(a.shape, a.dtype) for a in list(srcs) + list(lands)],
        in_specs=[HBM] * (2 * n) + [SEM, SEM] + [ANY] * len(after), out_specs=[HBM] * (2 * n),
        input_output_aliases={i: i for i in range(2 * n)},
        compiler_params=pltpu.CompilerParams(has_side_effects=EFFECT),
    )(*srcs, *lands, send, recv, *after)
    return res[:n], res[n:]


def _pair_add(name, parts, land):
    _, r, cols = parts.shape
    tr = max(16, min(r, ((1 << 20) // (2 * cols)) // 16 * 16))
    while r % tr:
        tr -= 16

    def body(c_ref, p_ref, l_ref, o_ref):
        del c_ref
        o_ref[...] = (p_ref[...].astype(F32) + l_ref[...].astype(F32)).astype(BF16)

    blk = pl.BlockSpec((None, tr, cols), lambda q, i, c_ref: (q, i, 0))
    return pl.pallas_call(
        body, name=name,
        grid_spec=pltpu.PrefetchScalarGridSpec(
            num_scalar_prefetch=1, grid=(4, r // tr),
            in_specs=[pl.BlockSpec((None, tr, cols), lambda q, i, c_ref: (2 * q + c_ref[0], i, 0)), blk], out_specs=blk),
        out_shape=_sds((4, r, cols), BF16),
        compiler_params=_params(("parallel", "parallel")),
    )(jnp.reshape(lax.axis_index("c"), (1,)).astype(jnp.int32), parts, land)


def _scatter_pairs(tag, parts):
    lands = [lax.empty((4,) + a.shape[1:], a.dtype) for a in parts]
    return _split_start("pair_" + tag, parts, lands, _pair_plan, 4)


def _scatter_chips(tag, started, after):
    send, recv, parts, lands, _ = started
    parts, lands = _split_wait("pair_" + tag + "_wait", send, recv, parts, lands, _pair_plan, 4, [after])
    sums = [_pair_add("pair_" + tag + "_add%d" % a, p, l) for a, (p, l) in enumerate(zip(parts, lands))]
    chip = 2 * lax.axis_index("x") + lax.axis_index("y")
    final = [lax.dynamic_update_slice_in_dim(lax.empty(v.shape, v.dtype), lax.dynamic_slice_in_dim(v, chip, 1, 0), chip, 0)
             for v in sums]
    return _split_start("chips_" + tag, sums, final, _chip_plan, 3)


def _scatter_end(tag, started, after):
    send, recv, sums, final, _ = started
    return _split_wait("chips_" + tag + "_wait", send, recv, sums, final, _chip_plan, 3, after)[1]


def _gather_targets():
    x, y, c = lax.axis_index("x"), lax.axis_index("y"), lax.axis_index("c")
    chips = [(x, y), (1 - x, y), (x, 1 - y), (1 - x, 1 - y)]
    same = [((cx, cy, c), 4 * cx + 2 * cy + c) for cx, cy in chips]
    other = [((cx, cy, 1 - c), 4 * cx + 2 * cy + 1 - c) for cx, cy in chips]
    return same[0][1], [other[0]] + same[1:], [flat for _, flat in other[1:]], other[0][0]


def _gather_start(shards):
    n = len(shards)
    me = _flat_me()
    lands = [lax.dynamic_update_slice_in_dim(lax.empty((N_DEV,) + a.shape, a.dtype), a[None], me, 0) for a in shards]

    def body(*refs):
        lnd, send, recv, token = refs[:n], refs[n], refs[n + 1], refs[-1]
        mine, targets, _, _ = _gather_targets()
        for a in range(n):
            for t, (dev, _) in enumerate(targets):
                pltpu.make_async_remote_copy(src_ref=lnd[a].at[mine], dst_ref=lnd[a].at[mine], send_sem=send.at[4 * a + t],
                                             recv_sem=recv.at[4 * a + t], device_id=dev, device_id_type=MESH).start()
        token[...] = jnp.zeros_like(token)

    res = pl.pallas_call(
        body, name="gather_start",
        out_shape=[pltpu.SemaphoreType.DMA((4 * n,)), pltpu.SemaphoreType.DMA((4 * n,))]
        + [pltpu.HBM(a.shape, a.dtype) for a in lands] + [_sds((8, LANES), F32)],
        in_specs=[HBM] * n, out_specs=[SEM, SEM] + [HBM] * n + [pl.BlockSpec(memory_space=pltpu.VMEM)],
        input_output_aliases={i: 2 + i for i in range(n)},
        compiler_params=pltpu.CompilerParams(has_side_effects=EFFECT),
    )(*[_in_hbm(a) for a in lands])
    return res[0], res[1], list(res[2:2 + n]), res[-1]


def _gather_forward(name, lands, first, send, recv, after):
    n = len(lands)

    def body(*refs):
        lnd, send_sem, recv_sem = refs[:n], refs[n], refs[n + 1]
        send2, recv2, token = refs[-3], refs[-2], refs[-1]
        mine, targets, _, sibling = _gather_targets()
        for a in range(n):
            for t, (dev, flat) in enumerate(targets):
                cp = pltpu.make_async_remote_copy(src_ref=lnd[a].at[mine], dst_ref=lnd[a].at[flat],
                                                  send_sem=send_sem.at[4 * (first + a) + t],
                                                  recv_sem=recv_sem.at[4 * (first + a) + t], device_id=dev, device_id_type=MESH)
                cp.wait_send()
                if t:
                    cp.wait_recv()
                    pltpu.make_async_remote_copy(src_ref=lnd[a].at[flat], dst_ref=lnd[a].at[flat], send_sem=send2.at[3 * a + t - 1],
                                                 recv_sem=recv2.at[3 * a + t - 1], device_id=sibling, device_id_type=MESH).start()
        token[...] = jnp.zeros_like(token)

    res = pl.pallas_call(
        body, name=name,
        out_shape=[pltpu.HBM(a.shape, a.dtype) for a in lands]
        + [pltpu.SemaphoreType.DMA((3 * n,)), pltpu.SemaphoreType.DMA((3 * n,)), _sds((8, LANES), F32)],
        in_specs=[HBM] * n + [SEM, SEM] + [ANY] * len(after),
        out_specs=[HBM] * n + [SEM, SEM, pl.BlockSpec(memory_space=pltpu.VMEM)],
        input_output_aliases={i: i for i in range(n)},
        compiler_params=pltpu.CompilerParams(has_side_effects=EFFECT),
    )(*lands, send, recv, *after)
    return list(res[:n]), res[n], res[n + 1], res[-1]


def _gather_wait(name, lands, first, recv, send2, recv2, after):
    n = len(lands)

    def body(*refs):
        lnd, recv_sem, send2_sem, recv2_sem = refs[:n], refs[n], refs[n + 1], refs[n + 2]
        mine, targets, passed, sibling = _gather_targets()
        for a in range(n):
            dev, flat = targets[0]
            pltpu.make_async_remote_copy(src_ref=lnd[a].at[mine], dst_ref=lnd[a].at[flat], send_sem=send2_sem.at[3 * a],
                                         recv_sem=recv_sem.at[4 * (first + a)], device_id=dev, device_id_type=MESH).wait_recv()
            for t in range(3):
                cp = pltpu.make_async_remote_copy(src_ref=lnd[a].at[targets[t + 1][1]], dst_ref=lnd[a].at[passed[t]],
                                                  send_sem=send2_sem.at[3 * a + t], recv_sem=recv2_sem.at[3 * a + t],
                                                  device_id=sibling, device_id_type=MESH)
                cp.wait_send()
                cp.wait_recv()

    res = pl.pallas_call(
        body, name=name, out_shape=[pltpu.HBM(a.shape, a.dtype) for a in lands],
        in_specs=[HBM] * n + [SEM, SEM, SEM, ANY], out_specs=[HBM] * n,
        input_output_aliases={i: i for i in range(n)},
        compiler_params=pltpu.CompilerParams(has_side_effects=EFFECT),
    )(*lands, recv, send2, recv2, after)
    return list(res)


def _adamw_decay(w, m, v):
    return ADAM_WD * w, ADAM_B1 * m, ADAM_B2 * v


def _adamw_finish(g, wd_w, m1, v1):
    m = m1 + (1.0 - ADAM_B1) * g
    v = v1 + (1.0 - ADAM_B2) * (g * g)
    m_hat = m / (1.0 - ADAM_B1 ** ADAM_STEP)
    v_hat = v / (1.0 - ADAM_B2 ** ADAM_STEP)
    delta = -ADAM_LR * (m_hat / (jnp.sqrt(v_hat) + ADAM_EPS) + wd_w)
    return delta, m, v


def _adamw(g, w, m, v):
    return _adamw_finish(g, *_adamw_decay(w, m, v))


def _update_prep(name, w, m, v, dep, w_done=False, block_bytes=1 << 20):
    _, r, c = m.shape
    tr = max(8, min(r, (block_bytes // (4 * c)) // 8 * 8))
    while r % tr:
        tr -= 8
    blk = pl.BlockSpec((None, tr, c), lambda i: (0, i, 0))
    if w_done:
        def body(m_ref, v_ref, dep_ref, om_ref, ov_ref):
            del dep_ref
            om_ref[...] = ADAM_B1 * m_ref[...]
            ov_ref[...] = ADAM_B2 * v_ref[...]

        m1, v1 = pl.pallas_call(
            body, name=name, grid=(r // tr,), in_specs=[blk] * 2 + [ANY], out_specs=[blk] * 2,
            out_shape=[_sds((1, r, c), F32)] * 2, compiler_params=_params(("parallel",)),
        )(m, v, dep)
        return w, m1, v1

    def body(w_ref, m_ref, v_ref, dep_ref, ow_ref, om_ref, ov_ref):
        del dep_ref
        ow_ref[...], om_ref[...], ov_ref[...] = _adamw_decay(w_ref[...], m_ref[...], v_ref[...])

    return pl.pallas_call(
        body, name=name, grid=(r // tr,), in_specs=[blk] * 3 + [ANY], out_specs=[blk] * 3,
        out_shape=[_sds((1, r, c), F32)] * 3, compiler_params=_params(("parallel",)),
    )(w, m, v, dep)


def _update(name, parts, w, m, v, layout=None, decayed=False, transposed_out=False, block_bytes=1 << 20):
    _, r, c = w.shape
    n_slots, _, cp = parts.shape
    tr = max(8, min(r, (block_bytes // (4 * cp)) // 8 * 8))
    if transposed_out:
        tr = _tile(r, 256)
    while r % tr:
        tr -= 8

    def body(p_ref, w_ref, m_ref, v_ref, g_ref, d_ref, nm_ref, nv_ref, *scratch):
        g = p_ref[0].astype(F32)
        for p in range(1, n_slots):
            g = g + p_ref[p].astype(F32)
        if layout is not None:
            s1, s2, lg = layout.my_shifts()
            lane = lax.broadcasted_iota(jnp.int32, g.shape, 1)
            scratch[0][...] = jnp.where(lane < lg, pltpu.roll(g, cp - s1, 1), pltpu.roll(g, cp - s2, 1))
            g = scratch[0][:, 0:c]
        step = _adamw_finish if decayed else _adamw
        results = (g,) + step(g, w_ref[...], m_ref[...], v_ref[...])
        if ragged:
            scratch[-2][...] = jnp.zeros_like(scratch[-2])
        for ref, val in zip((g_ref, d_ref, nm_ref, nv_ref), results):
            if not transposed_out:
                ref[...] = val
            elif not ragged:
                ref[...] = val.T
            else:
                wide, tall = scratch[-2], scratch[-1]
                wide[:, 0:c] = val
                tall[...] = wide[...].T
                ref[...] = tall[0:c, :]

    ragged = transposed_out and c % 8 != 0
    c_wide = -(-c // LANES) * LANES
    blk = pl.BlockSpec((None, tr, c), lambda i: (0, i, 0))
    out_blk = pl.BlockSpec((None, c, tr), lambda i: (0, 0, i)) if transposed_out else blk
    scratch_shapes = [] if layout is None else [pltpu.VMEM((tr, cp), F32)]
    if ragged:
        scratch_shapes += [pltpu.VMEM((tr, c_wide), F32), pltpu.VMEM((c_wide, tr), F32)]
    res = pl.pallas_call(
        body, name=name, grid=(r // tr,),
        in_specs=[pl.BlockSpec((n_slots, tr, cp), lambda i: (0, i, 0)), blk, blk, blk],
        out_specs=[out_blk] * 4, out_shape=[_sds((1, c, r) if transposed_out else (1, r, c), F32)] * 4,
        scratch_shapes=scratch_shapes,
        compiler_params=_params(("parallel",)),
    )(parts, w, m, v)
    return [jnp.transpose(o, (0, 2, 1)) for o in res] if transposed_out else res


def _small_update(part, w, m, v):
    n = part.shape[1]

    def body(p_ref, w_ref, m_ref, v_ref, g_ref, d_ref, nm_ref, nv_ref, buf, send, recv):
        me, peers = _mesh_place()
        buf[me] = p_ref[...]
        sent = []
        for d, dev, flat in peers:
            cp = pltpu.make_async_remote_copy(src_ref=p_ref, dst_ref=buf.at[me], send_sem=send.at[d],
                                              recv_sem=recv.at[d], device_id=dev, device_id_type=MESH)
            cp.start()
            sent.append(cp)
        for d, dev, flat in peers:
            pltpu.make_async_remote_copy(src_ref=p_ref, dst_ref=buf.at[flat], send_sem=send.at[d],
                                         recv_sem=recv.at[d], device_id=dev, device_id_type=MESH).wait_recv()
        for cp in sent:
            cp.wait_send()
        g = buf[0]
        for p in range(1, N_DEV):
            g = g + buf[p]
        g_ref[...] = g
        d_ref[...], nm_ref[...], nv_ref[...] = _adamw(g, w_ref[...], m_ref[...], v_ref[...])

    vm = pl.BlockSpec(memory_space=pltpu.VMEM)
    return pl.pallas_call(
        body, name="small_update", in_specs=[vm] * 4, out_specs=[vm] * 4, out_shape=[_sds((1, n), F32)] * 4,
        scratch_shapes=[pltpu.VMEM((N_DEV, 1, n), F32), pltpu.SemaphoreType.DMA((N_DEV,)),
                        pltpu.SemaphoreType.DMA((N_DEV,))],
    )(part, w, m, v)


class _WInLayout:
    def __init__(self, n8, n_f, d_sb, d_fox, d):
        assert n8 % LANES == 1 and n_f < LANES and d % (N_DEV * LANES) == 0
        self.n8, self.n_f, self.d = n8, n_f, d
        self.sp = n8 // LANES
        self.wp = (n8 + 2 * LANES - 2) // LANES * LANES
        self.n_qkv = 3 * (d_sb + d_fox)
        nq, dt, tc = self.n_qkv // LANES, d // LANES, d // N_DEV // LANES
        h_sb, h_fox = d_sb // HEAD_DIM, d_fox // HEAD_DIM
        self.sources = {}
        self.part_tile = {}
        for p in range(N_DEV):
            lg = min(max(self.n_qkv + n_f - n8 * p, 0), n8)
            s1, s2 = p, p + LANES - n_f
            spans = []
            if lg > 0:
                spans.append(("a", self.sp * p, s1 // LANES, (lg + s1 - 1) // LANES))
            if lg < n8:
                spans.append(("g", self.sp * p - 1 - nq, (lg + s2) // LANES, (n8 - 1 + s2) // LANES))
            for kind, base, first, last in spans:
                for i in range(first, last + 1):
                    assert (p, i) not in self.part_tile
                    self.part_tile[(p, i)] = (kind, base + i)
                    self.sources.setdefault((kind, base + i), []).append((p, i))
        self.cat_tiles = [("a", r * h_sb + h) for h in range(h_sb) for r in range(3)]
        self.cat_tiles += [("a", 3 * h_sb + r * h_fox + h) for h in range(h_fox) for r in range(3)]
        self.cat_tiles += [("g", which * dt + j * tc + half) for j in range(N_DEV) for which in (0, 1) for half in range(tc)]
        self.cat_tiles += [("a", nq)] + [None] * (F_PAD // LANES - 1)
        self.cat_index = {key: c for c, key in enumerate(self.cat_tiles) if key is not None}

    def my_shifts(self):
        me = _flat_me()
        return me, me + LANES - self.n_f, jnp.clip(self.n_qkv + self.n_f - self.n8 * me, 0, self.n8)


def _lane_tile(i):
    return pl.ds(i * LANES, LANES)


def _w_in_shift(w_in, lay, tr=256):
    _, d, n8 = w_in.shape
    kd = d // LANES
    kt = tr // LANES
    by_col = jnp.transpose(w_in, (0, 2, 1)).reshape(n8 * kd, LANES)

    def body(w_ref, o_ref, wd_ref, buf):
        k0 = kt * pl.program_id(0)
        buf[...] = jnp.zeros_like(buf)
        for j in range(n8 // LANES):
            for kk in range(kt):
                piece = w_ref[pl.ds(j * LANES * kd + k0 + kk, LANES, stride=kd), :]
                buf[kk * LANES:(kk + 1) * LANES, j * LANES:(j + 1) * LANES] = piece.T
        first = lax.broadcasted_iota(jnp.int32, (8, LANES), 0) == 0
        for kk in range(kt):
            row = w_ref[pl.ds((n8 - 1) * kd + k0 + kk, 1), :]
            buf[kk * LANES:(kk + 1) * LANES, n8 - 1:n8 + 7] = jnp.where(first, jnp.broadcast_to(row, (8, LANES)), 0.0).T
        wd_ref[...] = ADAM_WD * buf[:, 0:n8]
        v = buf[...]
        s1, s2, lg = lay.my_shifts()
        pos = lax.broadcasted_iota(jnp.int32, v.shape, 1)
        o_ref[...] = jnp.where(pos < lg + s1, pltpu.roll(v, s1, 1),
                               jnp.where(pos >= lg + s2, pltpu.roll(v, s2, 1), 0.0)).astype(BF16)

    return pl.pallas_call(
        body, name="w_in_shift", grid=(d // tr,),
        in_specs=[pl.BlockSpec((n8 * kd, LANES), lambda i: (0, 0))],
        out_specs=[pl.BlockSpec((tr, lay.wp), lambda i: (i, 0)), pl.BlockSpec((None, tr, n8), lambda i: (0, i, 0))],
        out_shape=[_sds((d, lay.wp), BF16), _sds((1, d, n8), F32)],
        scratch_shapes=[pltpu.VMEM((tr, lay.wp), F32)],
        compiler_params=_params(("arbitrary",)),
    )(by_col)


def _w_in_build(g_in, lay, tr=256):
    d = g_in.shape[1]
    width = len(lay.cat_tiles) * LANES

    def body(g_ref, o_ref):
        for c, key in enumerate(lay.cat_tiles):
            if key is None:
                o_ref[:, _lane_tile(c)] = jnp.zeros((tr, LANES), BF16)
                continue
            (p, i), *more = lay.sources[key]
            val = g_ref[p, :, _lane_tile(i)]
            for p2, i2 in more:
                val = val + g_ref[p2, :, _lane_tile(i2)]
            o_ref[:, _lane_tile(c)] = val

    return pl.pallas_call(
        body, name="w_in_build", grid=(d // tr,),
        in_specs=[pl.BlockSpec((N_DEV, tr, lay.wp), lambda i: (0, i, 0))],
        out_specs=pl.BlockSpec((tr, width), lambda i: (i, 0)), out_shape=_sds((d, width), BF16),
        compiler_params=_params(("parallel",)),
    )(g_in)


def _w_in_grad_parts(dwq, dwgf, lay, tr=256):
    d = dwq.shape[0]
    nq = lay.n_qkv // LANES

    def body(q_ref, g_ref, o_ref):
        for p in range(N_DEV):
            for i in range(lay.wp // LANES):
                key = lay.part_tile.get((p, i))
                if key is None:
                    o_ref[p, :, _lane_tile(i)] = jnp.zeros((tr, LANES), BF16)
                    continue
                c = lay.cat_index[key]
                o_ref[p, :, _lane_tile(i)] = q_ref[:, _lane_tile(c)] if c < nq else g_ref[:, _lane_tile(c - nq)]

    return pl.pallas_call(
        body, name="w_in_grad_parts", grid=(d // tr,),
        in_specs=[pl.BlockSpec((tr, dwq.shape[1]), lambda i: (i, 0)), pl.BlockSpec((tr, dwgf.shape[1]), lambda i: (i, 0))],
        out_specs=pl.BlockSpec((N_DEV, tr, lay.wp), lambda i: (0, i, 0)), out_shape=_sds((N_DEV, d, lay.wp), BF16),
        compiler_params=_params(("parallel",)),
    )(dwq, dwgf)


def kernel(x, norm_mix_pre, norm_mix_post, w_in, b_forget, w_branch_sb, w_branch_fox, w_out, norm_ffn_pre, norm_ffn_post, w_ffn_gate, w_ffn_up, w_ffn_down, loss_target, m_norm_mix_pre, m_norm_mix_post, m_w_in, m_b_forget, m_w_branch_sb, m_w_branch_fox, m_w_out, m_norm_ffn_pre, m_norm_ffn_post, m_w_ffn_gate, m_w_ffn_up, m_w_ffn_down, v_norm_mix_pre, v_norm_mix_post, v_w_in, v_b_forget, v_w_branch_sb, v_w_branch_fox, v_w_out, v_norm_ffn_pre, v_norm_ffn_post, v_w_ffn_gate, v_w_ffn_up, v_w_ffn_down):
    xs, target = x[0], loss_target[0]
    s, d = xs.shape
    d_sb, d_fox = w_branch_sb.shape[1], w_branch_fox.shape[1]
    h_sb, h_fox = d_sb // HEAD_DIM, d_fox // HEAD_DIM
    n_f = b_forget.shape[1]
    fs = w_ffn_gate.shape[2]
    cs = d // N_DEV
    n_qkv = 3 * (d_sb + d_fox)
    n_gf = 2 * d + F_PAD
    f_blk = 2 * d // LANES
    big = (w_in, w_branch_sb, w_branch_fox, w_out, w_ffn_gate, w_ffn_up, w_ffn_down)
    big_m = (m_w_in, m_w_branch_sb, m_w_branch_fox, m_w_out, m_w_ffn_gate, m_w_ffn_up, m_w_ffn_down)
    big_v = (v_w_in, v_w_branch_sb, v_w_branch_fox, v_w_out, v_w_ffn_gate, v_w_ffn_up, v_w_ffn_down)

    lay = _WInLayout(w_in.shape[2], n_f, d_sb, d_fox, d)
    w_in_shifted, wd_w_in = _w_in_shift(w_in, lay)
    send1, recv1, lands, token = _gather_start([w_in_shifted] + [w[0].astype(BF16) for w in big[1:]])
    b_pad = jnp.pad(b_forget, ((0, 0), (0, LANES - n_f)))

    started = token[0, 0]
    u, u_t = _pre_norm(xs, norm_mix_pre, dep=token)
    weights = dict(zip(("w_in", "w_branch_sb", "w_branch_fox", "w_out", "w_ffn_gate", "w_ffn_up", "w_ffn_down"),
                       zip(big, big_m, big_v)))
    decayed = {nm: _update_prep("decay_" + nm, *[t + started for t in weights[nm]], u)
               for nm in ("w_ffn_gate", "w_ffn_up")}
    decayed["w_in"] = _update_prep("decay_w_in", wd_w_in, m_w_in + started, v_w_in + started, u, w_done=True)
    l_in, send2, recv2, token = _gather_forward("gather_in_forward", lands[0:1], 0, send1, recv1,
                                                [u] + [t[2] for t in decayed.values()])
    (g_in,) = _gather_wait("gather_in_wait", l_in, 0, recv1, send2, recv2, token)
    w_cat = _w_in_build(g_in, lay)
    qkv = _mm_plain("proj_qkv", "nn", u, w_cat, BF16, n=n_qkv)
    gf = _mm_plain("proj_gates", "nn", u, w_cat, F32, n_off=n_qkv, n=n_gf)
    cum_col, cum_row = _forget_fwd(gf, b_pad, f_blk)
    o_sb, o_sb_t, tot = _sb_fwd(qkv, h_sb)
    l_mid, send2, recv2, token = _gather_forward("gather_mid_forward", lands[1:4], 1, send1, recv1, [o_sb])
    o_fx, o_fx_t, o_fx32, lse = _fox_fwd(qkv, cum_col, cum_row, h_fox, h_sb, token)
    g_sb, g_fx, g_out = _gather_wait("gather_mid_wait", l_mid, 1, recv1, send2, recv2, o_fx)
    w_out_full = g_out.reshape(d, d)
    merged, merged_t, a_sb, a_fx = _branch_merge(o_sb, o_fx, g_sb, g_fx, gf, o_fx)
    l_ffn, send2, recv2, token = _gather_forward("gather_ffn_forward", lands[4:6], 4, send1, recv1, [merged])
    mix = _mm_plain("out_proj", "nn", merged, w_out_full, F32, dep=token)
    h1, u2, u2_t = _mid_norms(xs, mix, norm_mix_post, norm_ffn_pre)
    g_gate, g_up = _gather_wait("gather_ffn_wait", l_ffn, 4, recv1, send2, recv2, u2)
    l_down, send2, recv2, token = _gather_forward("gather_down_forward", lands[6:7], 6, send1, recv1, [u2])
    gate, up, act, act_t = _ffn_up(u2, g_gate, g_up, token)
    (g_down,) = _gather_wait("gather_down_wait", l_down, 6, recv1, send2, recv2, act)
    tm, tn = _tile(s, 1024), _tile(d, 1024)
    tw = _tile(d, 2048)
    ff = _matmul("ffn_down", "nn",
                 [(act, pl.BlockSpec((None, tm, fs), lambda i, j, k: (k, i, 0)),
                   g_down, pl.BlockSpec((None, fs, tw), lambda i, j, k: (k, 0, j)))],
                 (s // tm, d // tw, N_DEV), (tm, tw), _sds((s, d), F32), pl.BlockSpec((tm, tw), lambda i, j, k: (i, j)))
    loss_part, dy, dff, dg_ffn_post = _loss_head(h1, ff, target, norm_ffn_post)

    dgate, dup = _ffn_down_bwd(dff, g_down, gate, up)
    dw_down = _matmul("dw_down", "nn",
                      [(act_t, pl.BlockSpec((None, fs, s), lambda j, n, k: (j, 0, 0)),
                        dff, pl.BlockSpec((s, tn), lambda j, n, k: (0, n)))],
                      (N_DEV, d // tn, 1), (fs, tn), _sds((N_DEV, fs, d), BF16),
                      pl.BlockSpec((None, fs, tn), lambda j, n, k: (j, 0, n)))

    def dw_up(name, dact):
        return _matmul(name, "nn",
                       [(u2_t, pl.BlockSpec((tn, s), lambda j, i, k: (i, 0)),
                         dact, pl.BlockSpec((None, s, fs), lambda j, i, k: (j, 0, 0)))],
                       (N_DEV, d // tn, 1), (tn, fs), _sds((N_DEV, d, fs), BF16),
                       pl.BlockSpec((None, tn, fs), lambda j, i, k: (j, i, 0)))

    dw_gate, dw_upw = dw_up("dw_gate", dgate), dw_up("dw_up", dup)
    rs_ffn = _scatter_pairs("ffn", [dw_gate, dw_upw, dw_down])
    a_spec = pl.BlockSpec((None, tm, fs), lambda i, j, k: (k, i, 0))
    b_spec = pl.BlockSpec((None, tw, fs), lambda i, j, k: (k, j, 0))
    du2 = _matmul("du2", "nt", [(dgate, a_spec, g_gate, b_spec), (dup, a_spec, g_up, b_spec)],
                  (s // tm, d // tw, N_DEV), (tm, tw), _sds((s, d), F32), pl.BlockSpec((tm, tw), lambda i, j, k: (i, j)),
                  dep=rs_ffn[4])
    rs_ffn = _scatter_chips("ffn", rs_ffn, du2)
    dh1, dmix, dg_ffn_pre, dg_mix_post = _mid_norms_bwd(dy, du2, h1, mix, norm_ffn_pre, norm_mix_post)

    da_sb, da_fx, dgf = _merge_bwd(dmix, w_out_full, gf, a_sb, a_fx, dep=rs_ffn[4])
    dw_out = _mm_plain("dw_out", "nn", merged_t, dmix, BF16).reshape(N_DEV, cs, d)

    def branch_bwd(tag, da, w_b, o_t, width):
        tb = _tile(width, 1024)
        do = _matmul("do_" + tag, "nt",
                     [(da, pl.BlockSpec((tm, cs), lambda i, j, k: (i, k)),
                       w_b, pl.BlockSpec((None, tb, cs), lambda i, j, k: (k, j, 0)))],
                     (s // tm, width // tb, N_DEV), (tm, tb), _sds((s, width), BF16),
                     pl.BlockSpec((tm, tb), lambda i, j, k: (i, j)))
        dw = _matmul("dw_" + tag, "nn",
                     [(o_t, pl.BlockSpec((width, s), lambda j, i, k: (0, 0)),
                       da, pl.BlockSpec((s, cs), lambda j, i, k: (0, j)))],
                     (N_DEV, 1, 1), (width, cs), _sds((N_DEV, width, cs), BF16),
                     pl.BlockSpec((None, width, cs), lambda j, i, k: (j, 0, 0)))
        return do, dw

    do_sb, dw_sb = branch_bwd("sb", da_sb, g_sb, o_sb_t, d_sb)
    do_fx, dw_fx = branch_bwd("fox", da_fx, g_fx, o_fx_t, d_fox)

    rs_mid = _scatter_pairs("mid", [dw_sb, dw_fx, dw_out])

    dqkv = _sb_bwd(qkv, do_sb, tot, h_sb, rs_mid[4])
    rs_mid = _scatter_chips("mid", rs_mid, dqkv)
    dqkv, dcum = _fox_bwd(dqkv, qkv, do_fx, o_fx32, lse, cum_col, cum_row, h_fox, h_sb, rs_mid[4])
    dgf, db_part = _forget_bwd(dgf, dcum, gf, b_pad, f_blk)
    dw_in = _w_in_grad_parts(_mm_plain("dw_qkv", "nn", u_t, dqkv, BF16), _mm_plain("dw_gates", "nn", u_t, dgf, BF16), lay)
    rs_in = _scatter_pairs("in", [dw_in])
    du = _mm_plain("du_qkv", "nt", dqkv, w_cat, F32, tn=1024, dep=rs_in[4])
    rs_in = _scatter_chips("in", rs_in, du)
    du = _mm_plain("du_gates", "nt", dgf, w_cat, F32, tn=1024, k_off=n_qkv, init=du, dep=rs_in[4])
    dx, dg_mix_pre = _pre_norm_bwd(dh1, du, xs, norm_mix_pre)

    upd = {}

    def update_group(tag, rs, names, after):
        parts = _scatter_end(tag, rs, after)
        for nm, p in zip(names, parts):
            w, m, v = decayed.get(nm, weights[nm])
            upd[nm] = _update("update_" + nm, p, w, m, v, layout=lay if nm == "w_in" else None, decayed=nm in decayed,
                              transposed_out=nm in ("w_in", "w_ffn_gate", "w_ffn_up"))

    update_group("ffn", rs_ffn, ("w_ffn_gate", "w_ffn_up", "w_ffn_down"), [dx])
    update_group("mid", rs_mid, ("w_branch_sb", "w_branch_fox", "w_out"), [upd[nm][3] for nm in ("w_ffn_gate", "w_ffn_up", "w_ffn_down")])
    update_group("in", rs_in, ("w_in",), [upd[nm][3] for nm in ("w_branch_sb", "w_branch_fox", "w_out")])

    small = ((norm_mix_pre, m_norm_mix_pre, v_norm_mix_pre), (norm_mix_post, m_norm_mix_post, v_norm_mix_post),
             (norm_ffn_pre, m_norm_ffn_pre, v_norm_ffn_pre), (norm_ffn_post, m_norm_ffn_post, v_norm_ffn_post))
    pad_f = ((0, 0), (0, LANES - n_f))
    cat = lambda i: jnp.concatenate([t[i] for t in small] + [jnp.pad((b_forget, m_b_forget, v_b_forget)[i], pad_f)], axis=1)
    sm = _small_update(jnp.concatenate([dg_mix_pre, dg_mix_post, dg_ffn_pre, dg_ffn_post, db_part], axis=1),
                       cat(0), cat(1), cat(2))
    for i, nm in enumerate(("norm_mix_pre", "norm_mix_post", "norm_ffn_pre", "norm_ffn_post")):
        upd[nm] = [o[:, i * d:(i + 1) * d] for o in sm]
    upd["b_forget"] = [o[:, 4 * d:4 * d + n_f] for o in sm]

    loss = lax.psum(loss_part[0, 0], ("x", "y", "c"))
    order = ("norm_mix_pre", "norm_mix_post", "w_in", "b_forget", "w_branch_sb", "w_branch_fox", "w_out",
             "norm_ffn_pre", "norm_ffn_post", "w_ffn_gate", "w_ffn_up", "w_ffn_down")
    return (loss, dx[None]) + tuple(upd[nm][i] for i in range(4) for nm in order)
```

```python
import jax
import jax.numpy as jnp
from jax import lax
from jax.experimental import pallas as pl
from jax.experimental.pallas import tpu as pltpu

F32 = jnp.float32
BF16 = jnp.bfloat16
MESH = pl.DeviceIdType.MESH
ANY = pl.BlockSpec(memory_space=pl.ANY)
HBM = pl.BlockSpec(memory_space=pltpu.HBM)
SEM = pl.BlockSpec(memory_space=pltpu.SEMAPHORE)
EFFECT = pltpu.SideEffectType.DATAFLOW_SIDE_EFFECTING

N_DEV = 8
HEAD_DIM = 128
RMS_EPS = 1e-6
F_PAD = 512
LANES = 128
ATT_TQ = 256
ATT_TK = 256
ATT_HP = 4
NEG_BIG = -1e30
VMEM_LIMIT = 56 * 1024 * 1024

ADAM_LR = 0.001
ADAM_B1 = 0.9
ADAM_B2 = 0.999
ADAM_EPS = 1e-08
ADAM_WD = 0.01
ADAM_STEP = 10

_DIMS = {"nn": ((1,), (0,)), "nt": ((1,), (1,)), "tn": ((0,), (0,))}


def _params(sem):
    return pltpu.CompilerParams(dimension_semantics=sem, vmem_limit_bytes=VMEM_LIMIT)


def _dot(a, b, mode="nn"):
    return lax.dot_general(a.astype(BF16), b.astype(BF16), (_DIMS[mode], ((), ())), preferred_element_type=F32)


def _tile(n, pref):
    if n <= pref:
        return n
    t = (pref // LANES) * LANES
    while n % t:
        t -= LANES
    return t


def _split2(v):
    hi = v.astype(BF16)
    return hi, (v - hi.astype(F32)).astype(BF16)


def _split3(v):
    a = v.astype(BF16)
    r = v - a.astype(F32)
    b = r.astype(BF16)
    return a, b, (r - b.astype(F32)).astype(BF16)


def _tri(n, cmp):
    r = lax.broadcasted_iota(jnp.int32, (n, n), 0)
    c = lax.broadcasted_iota(jnp.int32, (n, n), 1)
    return jnp.where(cmp(r, c), 1.0, 0.0).astype(BF16)


def _lane_pick(v, h):
    lane = lax.broadcasted_iota(jnp.int32, v.shape, 1)
    return jnp.sum(jnp.where(lane == h, v, 0.0), axis=1, keepdims=True)


def _lane_put(ref, rows, h, col):
    old = ref[rows, :]
    lane = lax.broadcasted_iota(jnp.int32, old.shape, 1)
    ref[rows, :] = jnp.where(lane == h, col, old)


def _sigmoid(z):
    return 1.0 / (1.0 + jnp.exp(-z))


def _log_sigmoid(z):
    return jnp.minimum(z, 0.0) - jnp.log(1.0 + jnp.exp(-jnp.abs(z)))


def _sds(shape, dtype):
    return jax.ShapeDtypeStruct(shape, dtype)


def _matmul(name, mode, pairs, grid, acc_shape, out_shape, out_specs, extras=(), epilogue=None, init=None, dep=None):
    n_p, n_e = len(pairs), len(extras)
    nk = grid[-1]
    single = not isinstance(out_shape, (list, tuple))
    n_i = 0 if init is None else 1
    n_d = 0 if dep is None else 1

    one_step = nk == 1 and init is None

    def body(*refs):
        ab = refs[:2 * n_p]
        ex = refs[2 * n_p:2 * n_p + n_e]
        ini = refs[2 * n_p + n_e:2 * n_p + n_e + n_i]
        outs = refs[2 * n_p + n_e + n_i + n_d:len(refs) - (0 if one_step else 1)]

        def finish(total):
            if epilogue is None:
                outs[0][...] = total.astype(outs[0].dtype)
            else:
                epilogue(total, ex, outs)

        t = _dot(ab[0][...], ab[1][...], mode)
        for p in range(1, n_p):
            t = t + _dot(ab[2 * p][...], ab[2 * p + 1][...], mode)
        if one_step:
            finish(t)
            return
        acc = refs[-1]
        k = pl.program_id(len(grid) - 1)

        @pl.when(k == 0)
        def _():
            acc[...] = t if init is None else ini[0][...].astype(F32) + t

        @pl.when(k > 0)
        def _():
            acc[...] += t

        @pl.when(k == nk - 1)
        def _():
            finish(acc[...])

    in_specs = [s for (_, sa, _, sb) in pairs for s in (sa, sb)] + [s for (_, s) in extras]
    args = [v for (a, _, b, _) in pairs for v in (a, b)] + [e for (e, _) in extras]
    if init is not None:
        in_specs.append(init[1])
        args.append(init[0])
    if dep is not None:
        in_specs.append(ANY)
        args.append(dep)
    return pl.pallas_call(
        body, name=name, grid=grid, in_specs=in_specs,
        out_specs=out_specs if single else list(out_specs),
        out_shape=out_shape if single else list(out_shape),
        scratch_shapes=[] if one_step else [pltpu.VMEM(acc_shape, F32)],
        compiler_params=_params(("parallel",) * (len(grid) - 1) + ("arbitrary",)),
    )(*args)


def _mm_plain(name, mode, a, b, out_dtype, *, n_off=0, n=None, k_off=0, tm=1024, tn=1536, tk=2048, init=None, dep=None):
    if mode == "nn":
        (m, kk), nn_ = a.shape, b.shape[1]
    elif mode == "nt":
        (m, kk), nn_ = a.shape, b.shape[0]
    else:
        (kk, m), nn_ = a.shape, b.shape[1]
    n = nn_ if n is None else n
    tm, tn, tk = _tile(m, tm), _tile(n, tn), _tile(kk, tk)
    while n_off % tn or n % tn:
        tn -= LANES
    while k_off % tk or kk % tk:
        tk -= LANES
    off, koff = n_off // tn, k_off // tk
    a_spec = {"nn": pl.BlockSpec((tm, tk), lambda i, j, k: (i, k)),
              "nt": pl.BlockSpec((tm, tk), lambda i, j, k: (i, k)),
              "tn": pl.BlockSpec((tk, tm), lambda i, j, k: (k, i))}[mode]
    b_spec = {"nn": pl.BlockSpec((tk, tn), lambda i, j, k: (k, j + off)),
              "nt": pl.BlockSpec((tn, tk), lambda i, j, k: (j, k + koff)),
              "tn": pl.BlockSpec((tk, tn), lambda i, j, k: (k, j))}[mode]
    o_spec = pl.BlockSpec((tm, tn), lambda i, j, k: (i, j))
    if init is not None:
        init = (init, o_spec)
    return _matmul(name, mode, [(a, a_spec, b, b_spec)], (m // tm, n // tn, kk // tk), (tm, tn),
                   _sds((m, n), out_dtype), o_spec, init=init, dep=dep)


def _rows_call(name, body, ins, outs, s, tr=256, dep=None):
    def spec(v, per_row):
        if per_row == "transposed":
            return pl.BlockSpec((v.shape[0], tr), lambda i: (0, i))
        if per_row:
            return pl.BlockSpec((tr, v.shape[1]), lambda i: (i, 0))
        return pl.BlockSpec(v.shape, lambda i: (0, 0))
    n_in = len(ins)
    deps = [] if dep is None else [dep]

    def with_dep(*refs):
        body(*refs[:n_in], *refs[n_in + len(deps):])

    return pl.pallas_call(
        with_dep, name=name, grid=(s // tr,),
        in_specs=[spec(v, p) for v, p in ins] + [ANY] * len(deps), out_specs=[spec(v, p) for v, p in outs],
        out_shape=[_sds(v.shape, v.dtype) for v, _ in outs],
        compiler_params=_params(("arbitrary",)),
    )(*[v for v, _ in ins], *deps)


def _rsq(v):
    return lax.rsqrt(jnp.mean(v * v, axis=-1, keepdims=True) + RMS_EPS)


def _norm_bwd(dy, v, r, g):
    vh = v * r
    t = dy * g
    dv = r * (t - vh * jnp.mean(t * vh, axis=-1, keepdims=True))
    return dv, jnp.sum(dy * vh, axis=0, keepdims=True)


def _accum(ref, val):
    @pl.when(pl.program_id(0) == 0)
    def _():
        ref[...] = jnp.zeros_like(ref)
    ref[...] += val


def _pre_norm(x, g, dep=None):
    def body(x_ref, g_ref, u_ref, ut_ref):
        v = x_ref[...]
        u = (v * _rsq(v) * g_ref[...]).astype(BF16)
        u_ref[...] = u
        ut_ref[...] = u.T
    s, d = x.shape
    return _rows_call("pre_norm", body, [(x, True), (g, False)],
                      [(_sds((s, d), BF16), True), (_sds((d, s), BF16), "transposed")], s, dep=dep)


def _mid_norms(x, mix, g_post, g_pre):
    def body(x_ref, mix_ref, gp_ref, gn_ref, h_ref, u_ref, ut_ref):
        mv = mix_ref[...]
        h = x_ref[...] + mv * _rsq(mv) * gp_ref[...]
        h_ref[...] = h
        u = (h * _rsq(h) * gn_ref[...]).astype(BF16)
        u_ref[...] = u
        ut_ref[...] = u.T
    s, d = x.shape
    return _rows_call("mid_norms", body, [(x, True), (mix, True), (g_post, False), (g_pre, False)],
                      [(_sds((s, d), F32), True), (_sds((s, d), BF16), True), (_sds((d, s), BF16), "transposed")], s)


def _loss_head(h1, ff, target, g):
    s, d = h1.shape

    def body(h_ref, ff_ref, t_ref, g_ref, loss_ref, dy_ref, dff_ref, dg_ref):
        fv = ff_ref[...]
        r = _rsq(fv)
        err = h_ref[...] + fv * r * g_ref[...] - t_ref[...]
        part = 0.5 * jnp.sum(jnp.mean(err * err, axis=-1, keepdims=True), axis=0, keepdims=True)
        _accum(loss_ref, jnp.broadcast_to(part, loss_ref.shape))
        dy = err * (1.0 / d)
        dy_ref[...] = dy
        dff, dg = _norm_bwd(dy, fv, r, g_ref[...])
        dff_ref[...] = dff.astype(BF16)
        _accum(dg_ref, dg)

    return _rows_call("loss_head", body, [(h1, True), (ff, True), (target, True), (g, False)],
                      [(_sds((1, LANES), F32), False), (_sds((s, d), F32), True),
                       (_sds((s, d), BF16), True), (_sds((1, d), F32), False)], s)


def _mid_norms_bwd(dy, du2, h1, mix, g_pre, g_post):
    s, d = dy.shape

    def body(dy_ref, du_ref, h_ref, mix_ref, gn_ref, gp_ref, dh_ref, dmix_ref, dgn_ref, dgp_ref):
        h = h_ref[...]
        dh, dgn = _norm_bwd(du_ref[...], h, _rsq(h), gn_ref[...])
        dh = dh + dy_ref[...]
        dh_ref[...] = dh
        _accum(dgn_ref, dgn)
        mv = mix_ref[...]
        dmix, dgp = _norm_bwd(dh, mv, _rsq(mv), gp_ref[...])
        dmix_ref[...] = dmix.astype(BF16)
        _accum(dgp_ref, dgp)

    return _rows_call("mid_norms_bwd", body,
                      [(dy, True), (du2, True), (h1, True), (mix, True), (g_pre, False), (g_post, False)],
                      [(_sds((s, d), F32), True), (_sds((s, d), BF16), True),
                       (_sds((1, d), F32), False), (_sds((1, d), F32), False)], s)


def _pre_norm_bwd(dh1, du, x, g, dep=None):
    s, d = x.shape

    def body(dh_ref, du_ref, x_ref, g_ref, dx_ref, dg_ref):
        v = x_ref[...]
        dv, dg = _norm_bwd(du_ref[...], v, _rsq(v), g_ref[...])
        dx_ref[...] = dh_ref[...] + dv
        _accum(dg_ref, dg)

    return _rows_call("pre_norm_bwd", body, [(dh1, True), (du, True), (x, True), (g, False)],
                      [(_sds((s, d), F32), True), (_sds((1, d), F32), False)], s, dep=dep)


def _forget_fwd(gf, b_pad, f_blk):
    s = gf.shape[0]
    tb = ATT_TK
    nb = s // tb

    def body(f_ref, b_ref, col_ref, row_ref):
        incl = _tri(tb, lambda r, c: c <= r)
        carry = jnp.zeros((1, LANES), F32)
        for i in range(nb):
            lf = _log_sigmoid(f_ref[pl.ds(i * tb, tb), :] + b_ref[...])
            parts = _split3(lf)
            cum = carry + _dot(incl, parts[0]) + _dot(incl, parts[1]) + _dot(incl, parts[2])
            col_ref[pl.ds(i * tb, tb), :] = cum
            row_ref[i] = cum.T
            carry = carry + jnp.sum(lf, axis=0, keepdims=True)

    return pl.pallas_call(
        body, name="forget_fwd", grid=(1,),
        in_specs=[pl.BlockSpec((s, LANES), lambda i: (0, f_blk)), pl.BlockSpec((1, LANES), lambda i: (0, 0))],
        out_specs=[pl.BlockSpec((s, LANES), lambda i: (0, 0)), pl.BlockSpec((nb, LANES, tb), lambda i: (0, 0, 0))],
        out_shape=[_sds((s, LANES), F32), _sds((nb, LANES, tb), F32)],
        compiler_params=_params(("arbitrary",)),
    )(gf, b_pad)


def _forget_bwd(dgf, dcum, gf, b_pad, f_blk):
    s = gf.shape[0]
    tb = ATT_TK
    nb = s // tb
    sec = dgf.shape[1] // F_PAD - 1

    def body(dgf_hbm, dc_ref, f_ref, b_ref, out_ref, db_ref):
        del dgf_hbm
        incl = _tri(tb, lambda r, c: c >= r)
        carry = jnp.zeros((1, LANES), F32)
        db = jnp.zeros((1, LANES), F32)
        out_ref[...] = jnp.zeros_like(out_ref)
        for i in reversed(range(nb)):
            dc = dc_ref[pl.ds(i * tb, tb), :]
            parts = _split3(dc)
            dlf = carry + _dot(incl, parts[0]) + _dot(incl, parts[1]) + _dot(incl, parts[2])
            z = f_ref[pl.ds(i * tb, tb), :] + b_ref[...]
            df = dlf * _sigmoid(-z)
            out_ref[pl.ds(i * tb, tb), pl.ds(0, LANES)] = df.astype(BF16)
            db = db + jnp.sum(df, axis=0, keepdims=True)
            carry = carry + jnp.sum(dc, axis=0, keepdims=True)
        db_ref[...] = db

    return pl.pallas_call(
        body, name="forget_bwd", grid=(1,),
        in_specs=[ANY, pl.BlockSpec((s, LANES), lambda i: (0, 0)),
                  pl.BlockSpec((s, LANES), lambda i: (0, f_blk)), pl.BlockSpec((1, LANES), lambda i: (0, 0))],
        out_specs=[pl.BlockSpec((s, F_PAD), lambda i: (0, sec)), pl.BlockSpec((1, LANES), lambda i: (0, 0))],
        out_shape=[_sds(dgf.shape, BF16), _sds((1, LANES), F32)],
        input_output_aliases={0: 0},
        compiler_params=_params(("arbitrary",)),
    )(dgf, dcum, gf, b_pad)


def _diag_mask(strict):
    r = lax.broadcasted_iota(jnp.int32, (ATT_TQ, ATT_TK), 0)
    c = lax.broadcasted_iota(jnp.int32, (ATT_TQ, ATT_TK), 1)
    return c < r if strict else c <= r


def _qkv_specs(hb0, s):
    specs = []
    for j in range(ATT_HP):
        def col(g, j=j):
            return 3 * (hb0 + ATT_HP * g + j)
        specs += [pl.BlockSpec((ATT_TQ, HEAD_DIM), lambda g, i, col=col: (i, col(g))),
                  pl.BlockSpec((s, HEAD_DIM), lambda g, i, col=col: (0, col(g) + 1)),
                  pl.BlockSpec((s, HEAD_DIM), lambda g, i, col=col: (0, col(g) + 2))]
    return specs


def _head_cols(j):
    return pl.ds(j * HEAD_DIM, HEAD_DIM)


def _sb_fwd(qkv, n_heads):
    s = qkv.shape[0]
    scale = HEAD_DIM ** -0.5
    tq, tk = ATT_TQ, ATT_TK
    heads = range(ATT_HP)

    def body(*refs):
        qkv_refs, (o_ref, ot_ref, tot_ref) = refs[:3 * ATT_HP], refs[3 * ATT_HP:]
        g, i = pl.program_id(0), pl.program_id(1)

        @pl.when((g == 0) & (i == 0))
        def _():
            tot_ref[...] = jnp.zeros_like(tot_ref)

        qs = [qkv_refs[3 * j][...] for j in heads]
        upper = _tri(tk, lambda r, c: r > c)

        def tile(kj, carry, mask):
            rows = pl.ds(pl.multiple_of(kj * tk, tk), tk)
            z = [_dot(qs[j], qkv_refs[3 * j + 1][rows, :], "nt") * scale for j in heads]
            lsz = [_log_sigmoid(z[j]) for j in heads]
            lk = [lsz[j] - z[j] if mask is None else jnp.where(mask, lsz[j] - z[j], 0.0) for j in heads]
            parts = [_split2(lk[j]) for j in heads]
            above = [carry[j][0] + _dot(parts[j][0], upper) + _dot(parts[j][1], upper) for j in heads]
            w = [jnp.exp(lsz[j] + above[j]) for j in heads]
            if mask is not None:
                w = [jnp.where(mask, w[j], 0.0) for j in heads]
            return tuple((carry[j][0] + jnp.sum(lk[j], axis=1, keepdims=True),
                          carry[j][1] + _dot(w[j], qkv_refs[3 * j + 2][rows, :])) for j in heads)

        carry = tile(i, tuple((jnp.zeros((tq, 1), F32), jnp.zeros((tq, HEAD_DIM), F32)) for _ in heads), _diag_mask(True))
        carry = lax.fori_loop(0, i, lambda n, cr: tile(i - 1 - n, cr, None), carry)
        q_rows = pl.ds(pl.multiple_of(i * tq, tq), tq)
        for j in heads:
            c, acc = carry[j]
            o = acc.astype(BF16)
            o_ref[:, _head_cols(j)] = o
            ot_ref[_head_cols(j), :] = o.T
            _lane_put(tot_ref, q_rows, ATT_HP * g + j, c)

    wide = ATT_HP * HEAD_DIM
    return pl.pallas_call(
        body, name="sb_fwd", grid=(n_heads // ATT_HP, s // tq),
        in_specs=_qkv_specs(0, s),
        out_specs=[pl.BlockSpec((tq, wide), lambda g, i: (i, g)), pl.BlockSpec((wide, tq), lambda g, i: (g, i)),
                   pl.BlockSpec((s, LANES), lambda g, i: (0, 0))],
        out_shape=[_sds((s, n_heads * HEAD_DIM), BF16), _sds((n_heads * HEAD_DIM, s), BF16), _sds((s, LANES), F32)],
        compiler_params=_params(("arbitrary", "arbitrary")),
    )(*[qkv] * (3 * ATT_HP))


def _sb_bwd(qkv, do, tot, n_heads, dep):
    s = qkv.shape[0]
    scale = HEAD_DIM ** -0.5
    tq, tk = ATT_TQ, ATT_TK
    nq = s // tq
    hd = HEAD_DIM

    heads = range(ATT_HP)

    def body(*refs):
        qkv_refs = refs[:3 * ATT_HP]
        do_ref, tot_ref, _, out_ref, dk_acc, dv_acc = refs[3 * ATT_HP:]
        g, i = pl.program_id(0), pl.program_id(1)

        @pl.when(i == 0)
        def _():
            dk_acc[...] = jnp.zeros_like(dk_acc)
            dv_acc[...] = jnp.zeros_like(dv_acc)

        qs = [qkv_refs[3 * j][...] for j in heads]
        douts = [do_ref[:, _head_cols(j)] for j in heads]
        totals = [_lane_pick(tot_ref[...], ATT_HP * g + j) for j in heads]
        incl = _tri(tk, lambda r, c: r <= c)
        excl = _tri(tk, lambda r, c: r < c)

        def tile(kj, carry, mask):
            rows = pl.ds(pl.multiple_of(kj * tk, tk), tk)
            k_t = [qkv_refs[3 * j + 1][rows, :] for j in heads]
            z = [_dot(qs[j], k_t[j], "nt") * scale for j in heads]
            dw = [_dot(douts[j], qkv_refs[3 * j + 2][rows, :], "nt") for j in heads]
            lsz = [_log_sigmoid(z[j]) for j in heads]
            lk = [lsz[j] - z[j] if mask is None else jnp.where(mask, lsz[j] - z[j], 0.0) for j in heads]
            parts = [_split2(lk[j]) for j in heads]
            below = [carry[j][0] + _dot(parts[j][0], incl) + _dot(parts[j][1], incl) for j in heads]
            w = [jnp.exp(lsz[j] + (totals[j] - below[j])) for j in heads]
            if mask is not None:
                w = [jnp.where(mask, w[j], 0.0) for j in heads]
            e = [dw[j] * w[j] for j in heads]
            parts = [_split2(e[j]) for j in heads]
            e_before = [carry[j][1] + _dot(parts[j][0], excl) + _dot(parts[j][1], excl) for j in heads]
            sg = [jnp.exp(lsz[j]) for j in heads]
            dz = [e[j] * (1.0 - sg[j]) - e_before[j] * sg[j] for j in heads]
            if mask is not None:
                dz = [jnp.where(mask, dz[j], 0.0) for j in heads]
            dz = [(dz[j] * scale).astype(BF16) for j in heads]
            for j in heads:
                dk_acc[j, rows, :] += _dot(dz[j], qs[j], "tn")
                dv_acc[j, rows, :] += _dot(w[j], douts[j], "tn")
            return tuple((carry[j][0] + jnp.sum(lk[j], axis=1, keepdims=True),
                          carry[j][1] + jnp.sum(e[j], axis=1, keepdims=True),
                          carry[j][2] + _dot(dz[j], k_t[j])) for j in heads)

        zero = jnp.zeros((tq, 1), F32)
        carry = lax.fori_loop(0, i, lambda kj, cr: tile(kj, cr, None),
                              tuple((zero, zero, jnp.zeros((tq, hd), F32)) for _ in heads))
        carry = tile(i, carry, _diag_mask(True))
        for j in heads:
            out_ref[pl.ds(pl.multiple_of(i * tq, tq), tq), pl.ds(3 * j * hd, hd)] = carry[j][2].astype(BF16)

        @pl.when(i == nq - 1)
        def _():
            for j in heads:
                out_ref[:, pl.ds((3 * j + 1) * hd, hd)] = dk_acc[j].astype(BF16)
                out_ref[:, pl.ds((3 * j + 2) * hd, hd)] = dv_acc[j].astype(BF16)

    wide = ATT_HP * hd
    return pl.pallas_call(
        body, name="sb_bwd", grid=(n_heads // ATT_HP, nq),
        in_specs=_qkv_specs(0, s) + [pl.BlockSpec((tq, wide), lambda g, i: (i, g)),
                                     pl.BlockSpec((tq, LANES), lambda g, i: (i, 0)), ANY],
        out_specs=pl.BlockSpec((s, 3 * wide), lambda g, i: (0, g)),
        out_shape=_sds(qkv.shape, BF16),
        scratch_shapes=[pltpu.VMEM((ATT_HP, s, hd), F32), pltpu.VMEM((ATT_HP, s, hd), F32)],
        compiler_params=_params(("arbitrary", "arbitrary")),
    )(*[qkv] * (3 * ATT_HP), do, tot, dep)


def _fox_fwd(qkv, cum_col, cum_row, n_heads, hb0, dep):
    s = qkv.shape[0]
    scale = HEAD_DIM ** -0.5
    tq, tk = ATT_TQ, ATT_TK

    heads = range(ATT_HP)

    def body(*refs):
        qkv_refs = refs[:3 * ATT_HP]
        cc_ref, cr_ref, _, o_ref, ot_ref, o32_ref, lse_ref = refs[3 * ATT_HP:]
        g, i = pl.program_id(0), pl.program_id(1)

        @pl.when((g == 0) & (i == 0))
        def _():
            lse_ref[...] = jnp.zeros_like(lse_ref)

        qs = [qkv_refs[3 * j][...] for j in heads]
        cqs = [_lane_pick(cc_ref[...], ATT_HP * g + j) for j in heads]

        def tile(kj, carry, mask):
            rows = pl.ds(pl.multiple_of(kj * tk, tk), tk)
            sc = [_dot(qs[j], qkv_refs[3 * j + 1][rows, :], "nt") * scale + cqs[j]
                  - cr_ref[kj, pl.ds(ATT_HP * g + j, 1), :] for j in heads]
            if mask is not None:
                sc = [jnp.where(mask, sc[j], NEG_BIG) for j in heads]
            m_new = [jnp.maximum(carry[j][0], jnp.max(sc[j], axis=1, keepdims=True)) for j in heads]
            p = [jnp.exp(sc[j] - m_new[j]) for j in heads]
            alpha = [jnp.exp(carry[j][0] - m_new[j]) for j in heads]
            parts = [_split2(p[j]) for j in heads]
            v_t = [qkv_refs[3 * j + 2][rows, :] for j in heads]
            pv = [_dot(parts[j][0], v_t[j]) + _dot(parts[j][1], v_t[j]) for j in heads]
            return tuple((m_new[j], alpha[j] * carry[j][1] + jnp.sum(p[j], axis=1, keepdims=True),
                          alpha[j] * carry[j][2] + pv[j]) for j in heads)

        carry = tuple((jnp.full((tq, 1), NEG_BIG, F32), jnp.zeros((tq, 1), F32), jnp.zeros((tq, HEAD_DIM), F32))
                      for _ in heads)
        carry = lax.fori_loop(0, i, lambda kj, cr: tile(kj, cr, None), carry)
        carry = tile(i, carry, _diag_mask(False))
        q_rows = pl.ds(pl.multiple_of(i * tq, tq), tq)
        for j in heads:
            m, l, acc = carry[j]
            o = acc / l
            o_ref[:, _head_cols(j)] = o.astype(BF16)
            ot_ref[_head_cols(j), :] = o.astype(BF16).T
            o32_ref[:, _head_cols(j)] = o
            _lane_put(lse_ref, q_rows, ATT_HP * g + j, m + jnp.log(l))

    nb = cum_row.shape[0]
    wide = ATT_HP * HEAD_DIM
    return pl.pallas_call(
        body, name="fox_fwd", grid=(n_heads // ATT_HP, s // tq),
        in_specs=_qkv_specs(hb0, s) + [pl.BlockSpec((tq, LANES), lambda g, i: (i, 0)),
                                       pl.BlockSpec((nb, 8, tk), lambda g, i: (0, 0, 0)), ANY],
        out_specs=[pl.BlockSpec((tq, wide), lambda g, i: (i, g)), pl.BlockSpec((wide, tq), lambda g, i: (g, i)),
                   pl.BlockSpec((tq, wide), lambda g, i: (i, g)), pl.BlockSpec((s, LANES), lambda g, i: (0, 0))],
        out_shape=[_sds((s, n_heads * HEAD_DIM), BF16), _sds((n_heads * HEAD_DIM, s), BF16),
                   _sds((s, n_heads * HEAD_DIM), F32), _sds((s, LANES), F32)],
        compiler_params=_params(("arbitrary", "arbitrary")),
    )(*[qkv] * (3 * ATT_HP), cum_col, cum_row, dep)


def _fox_bwd(dqkv, qkv, do, o, lse, cum_col, cum_row, n_heads, hb0, dep):
    s = qkv.shape[0]
    scale = HEAD_DIM ** -0.5
    tq, tk = ATT_TQ, ATT_TK
    nq = s // tq
    hd = HEAD_DIM

    heads = range(ATT_HP)
    assert hb0 % ATT_HP == 0

    def body(*refs):
        qkv_refs = refs[1:1 + 3 * ATT_HP]
        do_ref, o_ref, lse_ref, cc_ref, cr_ref, _, out_ref, dc_ref, dk_acc, dv_acc, col_acc = refs[1 + 3 * ATT_HP:]
        g, i = pl.program_id(0), pl.program_id(1)

        @pl.when((g == 0) & (i == 0))
        def _():
            dc_ref[...] = jnp.zeros_like(dc_ref)

        @pl.when(i == 0)
        def _():
            dk_acc[...] = jnp.zeros_like(dk_acc)
            dv_acc[...] = jnp.zeros_like(dv_acc)
            col_acc[...] = jnp.zeros_like(col_acc)

        qs = [qkv_refs[3 * j][...] for j in heads]
        douts = [do_ref[:, _head_cols(j)] for j in heads]
        deltas = [jnp.sum(douts[j].astype(F32) * o_ref[:, _head_cols(j)], axis=1, keepdims=True) for j in heads]
        shifts = [_lane_pick(cc_ref[...], ATT_HP * g + j) - _lane_pick(lse_ref[...], ATT_HP * g + j) for j in heads]

        def tile(kj, carry, mask):
            rows = pl.ds(pl.multiple_of(kj * tk, tk), tk)
            k_t = [qkv_refs[3 * j + 1][rows, :] for j in heads]
            sc = [_dot(qs[j], k_t[j], "nt") * scale + shifts[j] - cr_ref[kj, pl.ds(ATT_HP * g + j, 1), :] for j in heads]
            dp = [_dot(douts[j], qkv_refs[3 * j + 2][rows, :], "nt") for j in heads]
            p = [jnp.exp(sc[j]) for j in heads]
            if mask is not None:
                p = [jnp.where(mask, p[j], 0.0) for j in heads]
            ds_f = [p[j] * (dp[j] - deltas[j]) for j in heads]
            ds = [(ds_f[j] * scale).astype(BF16) for j in heads]
            for j in heads:
                col_acc[j, kj] += jnp.broadcast_to(jnp.sum(ds_f[j], axis=0, keepdims=True), (8, tk))
                dk_acc[j, rows, :] += _dot(ds[j], qs[j], "tn")
                dv_acc[j, rows, :] += _dot(p[j], douts[j], "tn")
            return tuple((carry[j][0] + _dot(ds[j], k_t[j]), carry[j][1] + jnp.sum(ds_f[j], axis=1, keepdims=True))
                         for j in heads)

        carry = lax.fori_loop(0, i, lambda kj, cr: tile(kj, cr, None),
                              tuple((jnp.zeros((tq, hd), F32), jnp.zeros((tq, 1), F32)) for _ in heads))
        carry = tile(i, carry, _diag_mask(False))
        q_rows = pl.ds(pl.multiple_of(i * tq, tq), tq)
        for j in heads:
            out_ref[q_rows, pl.ds(3 * j * hd, hd)] = carry[j][0].astype(BF16)
            _lane_put(dc_ref, q_rows, ATT_HP * g + j, carry[j][1])

        @pl.when(i == nq - 1)
        def _():
            lane = lax.broadcasted_iota(jnp.int32, (tk, LANES), 1)
            for j in heads:
                out_ref[:, pl.ds((3 * j + 1) * hd, hd)] = dk_acc[j].astype(BF16)
                out_ref[:, pl.ds((3 * j + 2) * hd, hd)] = dv_acc[j].astype(BF16)
                for kj in range(nb):
                    col = jnp.broadcast_to(col_acc[j, kj][0:1, :], (LANES, tk)).T
                    old = dc_ref[pl.ds(kj * tk, tk), :]
                    dc_ref[pl.ds(kj * tk, tk), :] = jnp.where(lane == ATT_HP * g + j, old - col, old)

    nb = cum_row.shape[0]
    wide = ATT_HP * hd
    return pl.pallas_call(
        body, name="fox_bwd", grid=(n_heads // ATT_HP, nq),
        in_specs=[ANY] + _qkv_specs(hb0, s) + [
            pl.BlockSpec((tq, wide), lambda g, i: (i, g)), pl.BlockSpec((tq, wide), lambda g, i: (i, g)),
            pl.BlockSpec((tq, LANES), lambda g, i: (i, 0)), pl.BlockSpec((tq, LANES), lambda g, i: (i, 0)),
            pl.BlockSpec((nb, 8, tk), lambda g, i: (0, 0, 0)), ANY],
        out_specs=[pl.BlockSpec((s, 3 * wide), lambda g, i: (0, hb0 // ATT_HP + g)),
                   pl.BlockSpec((s, LANES), lambda g, i: (0, 0))],
        out_shape=[_sds(dqkv.shape, BF16), _sds((s, LANES), F32)],
        scratch_shapes=[pltpu.VMEM((ATT_HP, s, hd), F32), pltpu.VMEM((ATT_HP, s, hd), F32),
                        pltpu.VMEM((ATT_HP, s // tk, 8, tk), F32)],
        input_output_aliases={0: 0},
        compiler_params=_params(("arbitrary", "arbitrary")),
    )(dqkv, *[qkv] * (3 * ATT_HP), do, o, lse, cum_col, cum_row, dep)


def _branch_merge(o_sb, o_fx, w_sb, w_fx, gf, dep, tm=1024):
    s = o_sb.shape[0]
    cs = w_sb.shape[2]
    tm = _tile(s, tm)

    def body(osb_ref, ofx_ref, wsb_ref, wfx_ref, g_ref, dep_ref, merged_ref, mt_ref, asb_ref, afx_ref):
        del dep_ref
        a_sb = _dot(osb_ref[...], wsb_ref[...])
        a_fx = _dot(ofx_ref[...], wfx_ref[...])
        g = g_ref[...]
        merged = (_sigmoid(g[:, :cs]) * a_sb + _sigmoid(g[:, cs:]) * a_fx).astype(BF16)
        merged_ref[...] = merged
        mt_ref[...] = merged.T
        asb_ref[...] = a_sb.astype(BF16)
        afx_ref[...] = a_fx.astype(BF16)

    blk = pl.BlockSpec((tm, cs), lambda i, j: (i, j))
    out = _sds((s, N_DEV * cs), BF16)
    return pl.pallas_call(
        body, name="branch_merge", grid=(s // tm, N_DEV),
        in_specs=[pl.BlockSpec((tm, o_sb.shape[1]), lambda i, j: (i, 0)),
                  pl.BlockSpec((tm, o_fx.shape[1]), lambda i, j: (i, 0)),
                  pl.BlockSpec((None,) + w_sb.shape[1:], lambda i, j: (j, 0, 0)),
                  pl.BlockSpec((None,) + w_fx.shape[1:], lambda i, j: (j, 0, 0)),
                  pl.BlockSpec((tm, 2 * cs), lambda i, j: (i, j)), ANY],
        out_specs=[blk, pl.BlockSpec((cs, tm), lambda i, j: (j, i)), blk, blk],
        out_shape=[out, _sds((N_DEV * cs, s), BF16), out, out],
        compiler_params=_params(("parallel", "arbitrary")),
    )(o_sb, o_fx, w_sb, w_fx, gf, dep)


def _merge_bwd(dmix, w_out, gf, a_sb, a_fx, tm=1024, tk=2048, dep=None):
    s, d = dmix.shape
    cs = d // N_DEV
    tm, tk = _tile(s, tm), _tile(d, tk)

    def epilogue(acc, ex, outs):
        g, a_sb, a_fx = ex[0][...], ex[1][...].astype(F32), ex[2][...].astype(F32)
        s_sb, s_fx = _sigmoid(g[:, :cs]), _sigmoid(g[:, cs:])
        outs[0][...] = (acc * s_sb).astype(BF16)
        outs[1][...] = (acc * s_fx).astype(BF16)
        outs[2][...] = jnp.concatenate([acc * a_sb * s_sb * (1.0 - s_sb), acc * a_fx * s_fx * (1.0 - s_fx)],
                                       axis=1).astype(BF16)

    blk = pl.BlockSpec((tm, cs), lambda i, j, k: (i, j))
    wide = pl.BlockSpec((tm, 2 * cs), lambda i, j, k: (i, j))
    return _matmul(
        "merge_bwd", "nt",
        [(dmix, pl.BlockSpec((tm, tk), lambda i, j, k: (i, k)), w_out, pl.BlockSpec((cs, tk), lambda i, j, k: (j, k)))],
        (s // tm, N_DEV, d // tk), (tm, cs),
        [_sds((s, d), BF16), _sds((s, d), BF16), _sds(gf.shape, BF16)], [blk, blk, wide],
        extras=[(gf, wide), (a_sb, blk), (a_fx, blk)], epilogue=epilogue, dep=dep)


def _ffn_up(u2, w_gate, w_up, dep, tm=1024):
    s, d = u2.shape
    fs = w_gate.shape[2]
    tm = _tile(s, tm)

    def body(u_ref, wg_ref, wu_ref, dep_ref, gate_ref, up_ref, act_ref, actt_ref):
        del dep_ref
        halves = [pl.ds(h * (tm // 2), tm // 2) for h in range(2)]
        gates = [_dot(u_ref[r, :], wg_ref[...]) for r in halves]
        ups = [_dot(u_ref[r, :], wu_ref[...]) for r in halves]
        for r, gate, up in zip(halves, gates, ups):
            gate_ref[r, :] = gate
            up_ref[r, :] = up
            act = (gate * _sigmoid(gate) * up).astype(BF16)
            act_ref[r, :] = act
            actt_ref[:, r] = act.T

    w_spec = pl.BlockSpec((None, d, fs), lambda i, j: (j, 0, 0))
    o_spec = pl.BlockSpec((None, tm, fs), lambda i, j: (j, i, 0))
    return pl.pallas_call(
        body, name="ffn_up", grid=(s // tm, N_DEV),
        in_specs=[pl.BlockSpec((tm, d), lambda i, j: (i, 0)), w_spec, w_spec, ANY],
        out_specs=[o_spec, o_spec, o_spec, pl.BlockSpec((None, fs, tm), lambda i, j: (j, 0, i))],
        out_shape=[_sds((N_DEV, s, fs), F32), _sds((N_DEV, s, fs), F32), _sds((N_DEV, s, fs), BF16),
                   _sds((N_DEV, fs, s), BF16)],
        compiler_params=_params(("parallel", "arbitrary")),
    )(u2, w_gate, w_up, dep)


def _ffn_down_bwd(dff, w_down, gate, up, tm=1024):
    s, d = dff.shape
    fs = w_down.shape[1]
    tm = _tile(s, tm)

    def body(dff_ref, wd_ref, gate_ref, up_ref, dgate_ref, dup_ref):
        halves = [pl.ds(h * (tm // 2), tm // 2) for h in range(2)]
        dact = [_dot(dff_ref[r, :], wd_ref[...], "nt") for r in halves]
        for r, da in zip(halves, dact):
            gate = gate_ref[r, :]
            sg = _sigmoid(gate)
            dup_ref[r, :] = (da * gate * sg).astype(BF16)
            dgate_ref[r, :] = (da * up_ref[r, :] * sg * (1.0 + gate * (1.0 - sg))).astype(BF16)

    a_spec = pl.BlockSpec((None, tm, fs), lambda i, j: (j, i, 0))
    return pl.pallas_call(
        body, name="ffn_down_bwd", grid=(s // tm, N_DEV),
        in_specs=[pl.BlockSpec((tm, d), lambda i, j: (i, 0)), pl.BlockSpec((None, fs, d), lambda i, j: (j, 0, 0)),
                  a_spec, a_spec],
        out_specs=[a_spec, a_spec],
        out_shape=[_sds((N_DEV, s, fs), BF16), _sds((N_DEV, s, fs), BF16)],
        compiler_params=_params(("parallel", "arbitrary")),
    )(dff, w_down, gate, up)


def _mesh_place():
    x, y, c = lax.axis_index("x"), lax.axis_index("y"), lax.axis_index("c")
    peers = []
    for d in range(1, N_DEV):
        px = 1 - x if d & 4 else x
        py = 1 - y if d & 2 else y
        pc = 1 - c if d & 1 else c
        peers.append((d, (px, py, pc), 4 * px + 2 * py + pc))
    return 4 * x + 2 * y + c, peers


def _flat_me():
    return 4 * lax.axis_index("x") + 2 * lax.axis_index("y") + lax.axis_index("c")


def _in_hbm(a):
    return pltpu.with_memory_space_constraint(a, pltpu.HBM)


def _pair_plan():
    x, y, c = lax.axis_index("x"), lax.axis_index("y"), lax.axis_index("c")
    return [(2 * q + (1 - c), q, q, (x, y, 1 - c)) for q in range(4)]


def _chip_plan():
    x, y, c = lax.axis_index("x"), lax.axis_index("y"), lax.axis_index("c")
    plan = []
    for fx, fy in ((1, 0), (0, 1), (1, 1)):
        cx, cy = (1 - x if fx else x), (1 - y if fy else y)
        plan.append((2 * cx + cy, 2 * x + y, 2 * cx + cy, (cx, cy, c)))
    return plan


def _split_start(name, srcs, lands, plan, k):
    n = len(srcs)

    def body(*refs):
        ins, lnd = refs[:n], refs[n:2 * n]
        send, recv, token = refs[2 * n], refs[2 * n + 1], refs[-1]
        copies = plan()
        for a in range(n):
            for t, (src, dst, _, dev) in enumerate(copies):
                pltpu.make_async_remote_copy(src_ref=ins[a].at[src], dst_ref=lnd[a].at[dst], send_sem=send.at[k * a + t],
                                             recv_sem=recv.at[k * a + t], device_id=dev, device_id_type=MESH).start()
        token[...] = jnp.zeros_like(token)

    res = pl.pallas_call(
        body, name=name,
        out_shape=[pltpu.SemaphoreType.DMA((n * k,)), pltpu.SemaphoreType.DMA((n * k,))]
        + [pltpu.HBM(a.shape, a.dtype) for a in list(srcs) + list(lands)] + [_sds((8, LANES), F32)],
        in_specs=[HBM] * (2 * n), out_specs=[SEM, SEM] + [HBM] * (2 * n) + [pl.BlockSpec(memory_space=pltpu.VMEM)],
        input_output_aliases={i: 2 + i for i in range(2 * n)},
        compiler_params=pltpu.CompilerParams(has_side_effects=EFFECT),
    )(*[_in_hbm(a) for a in srcs], *[_in_hbm(a) for a in lands])
    return res[0], res[1], res[2:2 + n], res[2 + n:2 + 2 * n], res[-1]


def _split_wait(name, send, recv, srcs, lands, plan, k, after):
    n = len(srcs)

    def body(*refs):
        ins, lnd = refs[:n], refs[n:2 * n]
        send_sem, recv_sem = refs[2 * n], refs[2 * n + 1]
        copies = plan()
        for a in range(n):
            for t, (src, _, dst, dev) in enumerate(copies):
                cp = pltpu.make_async_remote_copy(src_ref=ins[a].at[src], dst_ref=lnd[a].at[dst], send_sem=send_sem.at[k * a + t],
                                                  recv_sem=recv_sem.at[k * a + t], device_id=dev, device_id_type=MESH)
                cp.wait_send()
                cp.wait_recv()

    res = pl.pallas_call(
        body, name=name,
        out_shape=[pltpu.HBM(a.shape, a.dtype) for a in list(srcs) + list(lands)],
        in_specs=[HBM] * (2 * n) + [SEM, SEM] + [ANY] * len(after), out_specs=[HBM] * (2 * n),
        input_output_aliases={i: i for i in range(2 * n)},
        compiler_params=pltpu.CompilerParams(has_side_effects=EFFECT),
    )(*srcs, *lands, send, recv, *after)
    return res[:n], res[n:]


def _pair_add(name, parts, land):
    _, r, cols = parts.shape
    tr = max(16, min(r, ((1 << 22) // (2 * cols)) // 16 * 16))
    while r % tr:
        tr -= 16

    def body(c_ref, p_ref, l_ref, o_ref):
        del c_ref
        o_ref[...] = (p_ref[...].astype(F32) + l_ref[...].astype(F32)).astype(BF16)

    blk = pl.BlockSpec((None, tr, cols), lambda q, i, c_ref: (q, i, 0))
    return pl.pallas_call(
        body, name=name,
        grid_spec=pltpu.PrefetchScalarGridSpec(
            num_scalar_prefetch=1, grid=(4, r // tr),
            in_specs=[pl.BlockSpec((None, tr, cols), lambda q, i, c_ref: (2 * q + c_ref[0], i, 0)), blk], out_specs=blk),
        out_shape=_sds((4, r, cols), BF16),
        compiler_params=_params(("parallel", "parallel")),
    )(jnp.reshape(lax.axis_index("c"), (1,)).astype(jnp.int32), parts, land)


def _scatter_pairs(tag, parts):
    lands = [lax.empty((4,) + a.shape[1:], a.dtype) for a in parts]
    return _split_start("pair_" + tag, parts, lands, _pair_plan, 4)


def _scatter_chips(tag, started, after):
    send, recv, parts, lands, _ = started
    parts, lands = _split_wait("pair_" + tag + "_wait", send, recv, parts, lands, _pair_plan, 4, [after])
    sums = [_pair_add("pair_" + tag + "_add%d" % a, p, l) for a, (p, l) in enumerate(zip(parts, lands))]
    chip = 2 * lax.axis_index("x") + lax.axis_index("y")
    final = [lax.dynamic_update_slice_in_dim(lax.empty(v.shape, v.dtype), lax.dynamic_slice_in_dim(v, chip, 1, 0), chip, 0)
             for v in sums]
    return _split_start("chips_" + tag, sums, final, _chip_plan, 3)


def _scatter_end(tag, started, after):
    send, recv, sums, final, _ = started
    return _split_wait("chips_" + tag + "_wait", send, recv, sums, final, _chip_plan, 3, after)[1]


def _gather_targets():
    x, y, c = lax.axis_index("x"), lax.axis_index("y"), lax.axis_index("c")
    chips = [(x, y), (1 - x, y), (x, 1 - y), (1 - x, 1 - y)]
    same = [((cx, cy, c), 4 * cx + 2 * cy + c) for cx, cy in chips]
    other = [((cx, cy, 1 - c), 4 * cx + 2 * cy + 1 - c) for cx, cy in chips]
    return same[0][1], [other[0]] + same[1:], [flat for _, flat in other[1:]], other[0][0]


def _gather_start(shards):
    n = len(shards)
    me = _flat_me()
    lands = [lax.dynamic_update_slice_in_dim(lax.empty((N_DEV,) + a.shape, a.dtype), a[None], me, 0) for a in shards]

    def body(*refs):
        lnd, send, recv, token = refs[:n], refs[n], refs[n + 1], refs[-1]
        mine, targets, _, _ = _gather_targets()
        for a in range(n):
            for t, (dev, _) in enumerate(targets):
                pltpu.make_async_remote_copy(src_ref=lnd[a].at[mine], dst_ref=lnd[a].at[mine], send_sem=send.at[4 * a + t],
                                             recv_sem=recv.at[4 * a + t], device_id=dev, device_id_type=MESH).start()
        token[...] = jnp.zeros_like(token)

    res = pl.pallas_call(
        body, name="gather_start",
        out_shape=[pltpu.SemaphoreType.DMA((4 * n,)), pltpu.SemaphoreType.DMA((4 * n,))]
        + [pltpu.HBM(a.shape, a.dtype) for a in lands] + [_sds((8, LANES), F32)],
        in_specs=[HBM] * n, out_specs=[SEM, SEM] + [HBM] * n + [pl.BlockSpec(memory_space=pltpu.VMEM)],
        input_output_aliases={i: 2 + i for i in range(n)},
        compiler_params=pltpu.CompilerParams(has_side_effects=EFFECT),
    )(*[_in_hbm(a) for a in lands])
    return res[0], res[1], list(res[2:2 + n]), res[-1]


def _gather_forward(name, lands, first, send, recv, after):
    n = len(lands)

    def body(*refs):
        lnd, send_sem, recv_sem = refs[:n], refs[n], refs[n + 1]
        send2, recv2, token = refs[-3], refs[-2], refs[-1]
        mine, targets, _, sibling = _gather_targets()
        for a in range(n):
            for t, (dev, flat) in enumerate(targets):
                cp = pltpu.make_async_remote_copy(src_ref=lnd[a].at[mine], dst_ref=lnd[a].at[flat],
                                                  send_sem=send_sem.at[4 * (first + a) + t],
                                                  recv_sem=recv_sem.at[4 * (first + a) + t], device_id=dev, device_id_type=MESH)
                cp.wait_send()
                if t:
                    cp.wait_recv()
                    pltpu.make_async_remote_copy(src_ref=lnd[a].at[flat], dst_ref=lnd[a].at[flat], send_sem=send2.at[3 * a + t - 1],
                                                 recv_sem=recv2.at[3 * a + t - 1], device_id=sibling, device_id_type=MESH).start()
        token[...] = jnp.zeros_like(token)

    res = pl.pallas_call(
        body, name=name,
        out_shape=[pltpu.HBM(a.shape, a.dtype) for a in lands]
        + [pltpu.SemaphoreType.DMA((3 * n,)), pltpu.SemaphoreType.DMA((3 * n,)), _sds((8, LANES), F32)],
        in_specs=[HBM] * n + [SEM, SEM] + [ANY] * len(after),
        out_specs=[HBM] * n + [SEM, SEM, pl.BlockSpec(memory_space=pltpu.VMEM)],
        input_output_aliases={i: i for i in range(n)},
        compiler_params=pltpu.CompilerParams(has_side_effects=EFFECT),
    )(*lands, send, recv, *after)
    return list(res[:n]), res[n], res[n + 1], res[-1]


def _gather_wait(name, lands, first, recv, send2, recv2, after):
    n = len(lands)

    def body(*refs):
        lnd, recv_sem, send2_sem, recv2_sem = refs[:n], refs[n], refs[n + 1], refs[n + 2]
        mine, targets, passed, sibling = _gather_targets()
        for a in range(n):
            dev, flat = targets[0]
            pltpu.make_async_remote_copy(src_ref=lnd[a].at[mine], dst_ref=lnd[a].at[flat], send_sem=send2_sem.at[3 * a],
                                         recv_sem=recv_sem.at[4 * (first + a)], device_id=dev, device_id_type=MESH).wait_recv()
            for t in range(3):
                cp = pltpu.make_async_remote_copy(src_ref=lnd[a].at[targets[t + 1][1]], dst_ref=lnd[a].at[passed[t]],
                                                  send_sem=send2_sem.at[3 * a + t], recv_sem=recv2_sem.at[3 * a + t],
                                                  device_id=sibling, device_id_type=MESH)
                cp.wait_send()
                cp.wait_recv()

    res = pl.pallas_call(
        body, name=name, out_shape=[pltpu.HBM(a.shape, a.dtype) for a in lands],
        in_specs=[HBM] * n + [SEM, SEM, SEM, ANY], out_specs=[HBM] * n,
        input_output_aliases={i: i for i in range(n)},
        compiler_params=pltpu.CompilerParams(has_side_effects=EFFECT),
    )(*lands, recv, send2, recv2, after)
    return list(res)


def _adamw_decay(w, m, v):
    return ADAM_WD * w, ADAM_B1 * m, ADAM_B2 * v


def _adamw_finish(g, wd_w, m1, v1):
    m = m1 + (1.0 - ADAM_B1) * g
    v = v1 + (1.0 - ADAM_B2) * (g * g)
    m_hat = m / (1.0 - ADAM_B1 ** ADAM_STEP)
    v_hat = v / (1.0 - ADAM_B2 ** ADAM_STEP)
    delta = -ADAM_LR * (m_hat / (jnp.sqrt(v_hat) + ADAM_EPS) + wd_w)
    return delta, m, v


def _adamw(g, w, m, v):
    return _adamw_finish(g, *_adamw_decay(w, m, v))


def _update_prep(name, w, m, v, dep, w_done=False, block_bytes=1 << 20):
    _, r, c = m.shape
    tr = max(8, min(r, (block_bytes // (4 * c)) // 8 * 8))
    while r % tr:
        tr -= 8
    blk = pl.BlockSpec((None, tr, c), lambda i: (0, i, 0))
    if w_done:
        def body(m_ref, v_ref, dep_ref, om_ref, ov_ref):
            del dep_ref
            om_ref[...] = ADAM_B1 * m_ref[...]
            ov_ref[...] = ADAM_B2 * v_ref[...]

        m1, v1 = pl.pallas_call(
            body, name=name, grid=(r // tr,), in_specs=[blk] * 2 + [ANY], out_specs=[blk] * 2,
            out_shape=[_sds((1, r, c), F32)] * 2, compiler_params=_params(("parallel",)),
        )(m, v, dep)
        return w, m1, v1

    def body(w_ref, m_ref, v_ref, dep_ref, ow_ref, om_ref, ov_ref):
        del dep_ref
        ow_ref[...], om_ref[...], ov_ref[...] = _adamw_decay(w_ref[...], m_ref[...], v_ref[...])

    return pl.pallas_call(
        body, name=name, grid=(r // tr,), in_specs=[blk] * 3 + [ANY], out_specs=[blk] * 3,
        out_shape=[_sds((1, r, c), F32)] * 3, compiler_params=_params(("parallel",)),
    )(w, m, v, dep)


def _update(name, parts, w, m, v, layout=None, decayed=False, transposed_out=False, block_bytes=1 << 20):
    _, r, c = w.shape
    n_slots, _, cp = parts.shape
    tr = max(8, min(r, (block_bytes // (4 * cp)) // 8 * 8))
    if transposed_out:
        tr = _tile(r, 256)
    while r % tr:
        tr -= 8

    def body(p_ref, w_ref, m_ref, v_ref, g_ref, d_ref, nm_ref, nv_ref, *scratch):
        g = p_ref[0].astype(F32)
        for p in range(1, n_slots):
            g = g + p_ref[p].astype(F32)
        if layout is not None:
            s1, s2, lg = layout.my_shifts()
            lane = lax.broadcasted_iota(jnp.int32, g.shape, 1)
            scratch[0][...] = jnp.where(lane < lg, pltpu.roll(g, cp - s1, 1), pltpu.roll(g, cp - s2, 1))
            g = scratch[0][:, 0:c]
        step = _adamw_finish if decayed else _adamw
        results = (g,) + step(g, w_ref[...], m_ref[...], v_ref[...])
        if ragged:
            scratch[-2][...] = jnp.zeros_like(scratch[-2])
        for ref, val in zip((g_ref, d_ref, nm_ref, nv_ref), results):
            if not transposed_out:
                ref[...] = val
            elif not ragged:
                ref[...] = val.T
            else:
                wide, tall = scratch[-2], scratch[-1]
                wide[:, 0:c] = val
                tall[...] = wide[...].T
                ref[...] = tall[0:c, :]

    ragged = transposed_out and c % 8 != 0
    c_wide = -(-c // LANES) * LANES
    blk = pl.BlockSpec((None, tr, c), lambda i: (0, i, 0))
    out_blk = pl.BlockSpec((None, c, tr), lambda i: (0, 0, i)) if transposed_out else blk
    scratch_shapes = [] if layout is None else [pltpu.VMEM((tr, cp), F32)]
    if ragged:
        scratch_shapes += [pltpu.VMEM((tr, c_wide), F32), pltpu.VMEM((c_wide, tr), F32)]
    res = pl.pallas_call(
        body, name=name, grid=(r // tr,),
        in_specs=[pl.BlockSpec((n_slots, tr, cp), lambda i: (0, i, 0)), blk, blk, blk],
        out_specs=[out_blk] * 4, out_shape=[_sds((1, c, r) if transposed_out else (1, r, c), F32)] * 4,
        scratch_shapes=scratch_shapes,
        compiler_params=_params(("parallel",)),
    )(parts, w, m, v)
    return [jnp.transpose(o, (0, 2, 1)) for o in res] if transposed_out else res


def _small_update(part, w, m, v):
    n = part.shape[1]

    def body(p_ref, w_ref, m_ref, v_ref, g_ref, d_ref, nm_ref, nv_ref, buf, send, recv):
        me, peers = _mesh_place()
        buf[me] = p_ref[...]
        sent = []
        for d, dev, flat in peers:
            cp = pltpu.make_async_remote_copy(src_ref=p_ref, dst_ref=buf.at[me], send_sem=send.at[d],
                                              recv_sem=recv.at[d], device_id=dev, device_id_type=MESH)
            cp.start()
            sent.append(cp)
        for d, dev, flat in peers:
            pltpu.make_async_remote_copy(src_ref=p_ref, dst_ref=buf.at[flat], send_sem=send.at[d],
                                         recv_sem=recv.at[d], device_id=dev, device_id_type=MESH).wait_recv()
        for cp in sent:
            cp.wait_send()
        g = buf[0]
        for p in range(1, N_DEV):
            g = g + buf[p]
        g_ref[...] = g
        d_ref[...], nm_ref[...], nv_ref[...] = _adamw(g, w_ref[...], m_ref[...], v_ref[...])

    vm = pl.BlockSpec(memory_space=pltpu.VMEM)
    return pl.pallas_call(
        body, name="small_update", in_specs=[vm] * 4, out_specs=[vm] * 4, out_shape=[_sds((1, n), F32)] * 4,
        scratch_shapes=[pltpu.VMEM((N_DEV, 1, n), F32), pltpu.SemaphoreType.DMA((N_DEV,)),
                        pltpu.SemaphoreType.DMA((N_DEV,))],
    )(part, w, m, v)


class _WInLayout:
    def __init__(self, n8, n_f, d_sb, d_fox, d):
        assert n8 % LANES == 1 and n_f < LANES and d % (N_DEV * LANES) == 0
        self.n8, self.n_f, self.d = n8, n_f, d
        self.sp = n8 // LANES
        self.wp = (n8 + 2 * LANES - 2) // LANES * LANES
        self.n_qkv = 3 * (d_sb + d_fox)
        nq, dt, tc = self.n_qkv // LANES, d // LANES, d // N_DEV // LANES
        h_sb, h_fox = d_sb // HEAD_DIM, d_fox // HEAD_DIM
        self.sources = {}
        self.part_tile = {}
        for p in range(N_DEV):
            lg = min(max(self.n_qkv + n_f - n8 * p, 0), n8)
            s1, s2 = p, p + LANES - n_f
            spans = []
            if lg > 0:
                spans.append(("a", self.sp * p, s1 // LANES, (lg + s1 - 1) // LANES))
            if lg < n8:
                spans.append(("g", self.sp * p - 1 - nq, (lg + s2) // LANES, (n8 - 1 + s2) // LANES))
            for kind, base, first, last in spans:
                for i in range(first, last + 1):
                    assert (p, i) not in self.part_tile
                    self.part_tile[(p, i)] = (kind, base + i)
                    self.sources.setdefault((kind, base + i), []).append((p, i))
        self.cat_tiles = [("a", r * h_sb + h) for h in range(h_sb) for r in range(3)]
        self.cat_tiles += [("a", 3 * h_sb + r * h_fox + h) for h in range(h_fox) for r in range(3)]
        self.cat_tiles += [("g", which * dt + j * tc + half) for j in range(N_DEV) for which in (0, 1) for half in range(tc)]
        self.cat_tiles += [("a", nq)] + [None] * (F_PAD // LANES - 1)
        self.cat_index = {key: c for c, key in enumerate(self.cat_tiles) if key is not None}

    def my_shifts(self):
        me = _flat_me()
        return me, me + LANES - self.n_f, jnp.clip(self.n_qkv + self.n_f - self.n8 * me, 0, self.n8)


def _lane_tile(i):
    return pl.ds(i * LANES, LANES)


def _w_in_shift(w_in, lay, tr=256):
    _, d, n8 = w_in.shape
    kd = d // LANES
    kt = tr // LANES
    by_col = jnp.transpose(w_in, (0, 2, 1)).reshape(n8 * kd, LANES)

    def body(w_ref, o_ref, wd_ref, buf):
        k0 = kt * pl.program_id(0)
        buf[...] = jnp.zeros_like(buf)
        for j in range(n8 // LANES):
            for kk in range(kt):
                piece = w_ref[pl.ds(j * LANES * kd + k0 + kk, LANES, stride=kd), :]
                buf[kk * LANES:(kk + 1) * LANES, j * LANES:(j + 1) * LANES] = piece.T
        first = lax.broadcasted_iota(jnp.int32, (8, LANES), 0) == 0
        for kk in range(kt):
            row = w_ref[pl.ds((n8 - 1) * kd + k0 + kk, 1), :]
            buf[kk * LANES:(kk + 1) * LANES, n8 - 1:n8 + 7] = jnp.where(first, jnp.broadcast_to(row, (8, LANES)), 0.0).T
        wd_ref[...] = ADAM_WD * buf[:, 0:n8]
        v = buf[...]
        s1, s2, lg = lay.my_shifts()
        pos = lax.broadcasted_iota(jnp.int32, v.shape, 1)
        o_ref[...] = jnp.where(pos < lg + s1, pltpu.roll(v, s1, 1),
                               jnp.where(pos >= lg + s2, pltpu.roll(v, s2, 1), 0.0)).astype(BF16)

    return pl.pallas_call(
        body, name="w_in_shift", grid=(d // tr,),
        in_specs=[pl.BlockSpec((n8 * kd, LANES), lambda i: (0, 0))],
        out_specs=[pl.BlockSpec((tr, lay.wp), lambda i: (i, 0)), pl.BlockSpec((None, tr, n8), lambda i: (0, i, 0))],
        out_shape=[_sds((d, lay.wp), BF16), _sds((1, d, n8), F32)],
        scratch_shapes=[pltpu.VMEM((tr, lay.wp), F32)],
        compiler_params=_params(("arbitrary",)),
    )(by_col)


def _w_in_build(g_in, lay, tr=256):
    d = g_in.shape[1]
    width = len(lay.cat_tiles) * LANES

    def body(g_ref, o_ref):
        for c, key in enumerate(lay.cat_tiles):
            if key is None:
                o_ref[:, _lane_tile(c)] = jnp.zeros((tr, LANES), BF16)
                continue
            (p, i), *more = lay.sources[key]
            val = g_ref[p, :, _lane_tile(i)]
            for p2, i2 in more:
                val = val + g_ref[p2, :, _lane_tile(i2)]
            o_ref[:, _lane_tile(c)] = val

    return pl.pallas_call(
        body, name="w_in_build", grid=(d // tr,),
        in_specs=[pl.BlockSpec((N_DEV, tr, lay.wp), lambda i: (0, i, 0))],
        out_specs=pl.BlockSpec((tr, width), lambda i: (i, 0)), out_shape=_sds((d, width), BF16),
        compiler_params=_params(("parallel",)),
    )(g_in)


def _w_in_grad_parts(dwq, dwgf, lay, tr=256):
    d = dwq.shape[0]
    nq = lay.n_qkv // LANES

    def body(q_ref, g_ref, o_ref):
        for p in range(N_DEV):
            for i in range(lay.wp // LANES):
                key = lay.part_tile.get((p, i))
                if key is None:
                    o_ref[p, :, _lane_tile(i)] = jnp.zeros((tr, LANES), BF16)
                    continue
                c = lay.cat_index[key]
                o_ref[p, :, _lane_tile(i)] = q_ref[:, _lane_tile(c)] if c < nq else g_ref[:, _lane_tile(c - nq)]

    return pl.pallas_call(
        body, name="w_in_grad_parts", grid=(d // tr,),
        in_specs=[pl.BlockSpec((tr, dwq.shape[1]), lambda i: (i, 0)), pl.BlockSpec((tr, dwgf.shape[1]), lambda i: (i, 0))],
        out_specs=pl.BlockSpec((N_DEV, tr, lay.wp), lambda i: (0, i, 0)), out_shape=_sds((N_DEV, d, lay.wp), BF16),
        compiler_params=_params(("parallel",)),
    )(dwq, dwgf)


def kernel(x, norm_mix_pre, norm_mix_post, w_in, b_forget, w_branch_sb, w_branch_fox, w_out, norm_ffn_pre, norm_ffn_post, w_ffn_gate, w_ffn_up, w_ffn_down, loss_target, m_norm_mix_pre, m_norm_mix_post, m_w_in, m_b_forget, m_w_branch_sb, m_w_branch_fox, m_w_out, m_norm_ffn_pre, m_norm_ffn_post, m_w_ffn_gate, m_w_ffn_up, m_w_ffn_down, v_norm_mix_pre, v_norm_mix_post, v_w_in, v_b_forget, v_w_branch_sb, v_w_branch_fox, v_w_out, v_norm_ffn_pre, v_norm_ffn_post, v_w_ffn_gate, v_w_ffn_up, v_w_ffn_down):
    xs, target = x[0], loss_target[0]
    s, d = xs.shape
    d_sb, d_fox = w_branch_sb.shape[1], w_branch_fox.shape[1]
    h_sb, h_fox = d_sb // HEAD_DIM, d_fox // HEAD_DIM
    n_f = b_forget.shape[1]
    fs = w_ffn_gate.shape[2]
    cs = d // N_DEV
    n_qkv = 3 * (d_sb + d_fox)
    n_gf = 2 * d + F_PAD
    f_blk = 2 * d // LANES
    big = (w_in, w_branch_sb, w_branch_fox, w_out, w_ffn_gate, w_ffn_up, w_ffn_down)
    big_m = (m_w_in, m_w_branch_sb, m_w_branch_fox, m_w_out, m_w_ffn_gate, m_w_ffn_up, m_w_ffn_down)
    big_v = (v_w_in, v_w_branch_sb, v_w_branch_fox, v_w_out, v_w_ffn_gate, v_w_ffn_up, v_w_ffn_down)

    lay = _WInLayout(w_in.shape[2], n_f, d_sb, d_fox, d)
    w_in_shifted, wd_w_in = _w_in_shift(w_in, lay)
    send1, recv1, lands, token = _gather_start([w_in_shifted] + [w[0].astype(BF16) for w in big[1:]])
    b_pad = jnp.pad(b_forget, ((0, 0), (0, LANES - n_f)))

    started = token[0, 0]
    u, u_t = _pre_norm(xs, norm_mix_pre, dep=token)
    weights = dict(zip(("w_in", "w_branch_sb", "w_branch_fox", "w_out", "w_ffn_gate", "w_ffn_up", "w_ffn_down"),
                       zip(big, big_m, big_v)))
    decayed = {nm: _update_prep("decay_" + nm, *[t + started for t in weights[nm]], u)
               for nm in ("w_ffn_gate", "w_ffn_up")}
    decayed["w_in"] = _update_prep("decay_w_in", wd_w_in, m_w_in + started, v_w_in + started, u, w_done=True)
    l_in, send2, recv2, token = _gather_forward("gather_in_forward", lands[0:1], 0, send1, recv1,
                                                [u] + [t[2] for t in decayed.values()])
    (g_in,) = _gather_wait("gather_in_wait", l_in, 0, recv1, send2, recv2, token)
    w_cat = _w_in_build(g_in, lay)
    qkv = _mm_plain("proj_qkv", "nn", u, w_cat, BF16, n=n_qkv)
    gf = _mm_plain("proj_gates", "nn", u, w_cat, F32, n_off=n_qkv, n=n_gf)
    cum_col, cum_row = _forget_fwd(gf, b_pad, f_blk)
    o_sb, o_sb_t, tot = _sb_fwd(qkv, h_sb)
    l_mid, send2, recv2, token = _gather_forward("gather_mid_forward", lands[1:4], 1, send1, recv1, [o_sb])
    o_fx, o_fx_t, o_fx32, lse = _fox_fwd(qkv, cum_col, cum_row, h_fox, h_sb, token)
    g_sb, g_fx, g_out = _gather_wait("gather_mid_wait", l_mid, 1, recv1, send2, recv2, o_fx)
    w_out_full = g_out.reshape(d, d)
    merged, merged_t, a_sb, a_fx = _branch_merge(o_sb, o_fx, g_sb, g_fx, gf, o_fx)
    l_ffn, send2, recv2, token = _gather_forward("gather_ffn_forward", lands[4:6], 4, send1, recv1, [merged])
    mix = _mm_plain("out_proj", "nn", merged, w_out_full, F32, dep=token)
    h1, u2, u2_t = _mid_norms(xs, mix, norm_mix_post, norm_ffn_pre)
    g_gate, g_up = _gather_wait("gather_ffn_wait", l_ffn, 4, recv1, send2, recv2, u2)
    l_down, send2, recv2, token = _gather_forward("gather_down_forward", lands[6:7], 6, send1, recv1, [u2])
    gate, up, act, act_t = _ffn_up(u2, g_gate, g_up, token)
    (g_down,) = _gather_wait("gather_down_wait", l_down, 6, recv1, send2, recv2, act)
    tm, tn = _tile(s, 1024), _tile(d, 1024)
    tw = _tile(d, 2048)
    ff = _matmul("ffn_down", "nn",
                 [(act, pl.BlockSpec((None, tm, fs), lambda i, j, k: (k, i, 0)),
                   g_down, pl.BlockSpec((None, fs, tw), lambda i, j, k: (k, 0, j)))],
                 (s // tm, d // tw, N_DEV), (tm, tw), _sds((s, d), F32), pl.BlockSpec((tm, tw), lambda i, j, k: (i, j)))
    loss_part, dy, dff, dg_ffn_post = _loss_head(h1, ff, target, norm_ffn_post)

    dgate, dup = _ffn_down_bwd(dff, g_down, gate, up)
    dw_down = _matmul("dw_down", "nn",
                      [(act_t, pl.BlockSpec((None, fs, s), lambda j, n, k: (j, 0, 0)),
                        dff, pl.BlockSpec((s, tn), lambda j, n, k: (0, n)))],
                      (N_DEV, d // tn, 1), (fs, tn), _sds((N_DEV, fs, d), BF16),
                      pl.BlockSpec((None, fs, tn), lambda j, n, k: (j, 0, n)))

    def dw_up(name, dact):
        return _matmul(name, "nn",
                       [(u2_t, pl.BlockSpec((tn, s), lambda j, i, k: (i, 0)),
                         dact, pl.BlockSpec((None, s, fs), lambda j, i, k: (j, 0, 0)))],
                       (N_DEV, d // tn, 1), (tn, fs), _sds((N_DEV, d, fs), BF16),
                       pl.BlockSpec((None, tn, fs), lambda j, i, k: (j, i, 0)))

    dw_gate, dw_upw = dw_up("dw_gate", dgate), dw_up("dw_up", dup)
    rs_ffn = _scatter_pairs("ffn", [dw_gate, dw_upw, dw_down])
    a_spec = pl.BlockSpec((None, tm, fs), lambda i, j, k: (k, i, 0))
    b_spec = pl.BlockSpec((None, tw, fs), lambda i, j, k: (k, j, 0))
    du2 = _matmul("du2", "nt", [(dgate, a_spec, g_gate, b_spec), (dup, a_spec, g_up, b_spec)],
                  (s // tm, d // tw, N_DEV), (tm, tw), _sds((s, d), F32), pl.BlockSpec((tm, tw), lambda i, j, k: (i, j)),
                  dep=rs_ffn[4])
    rs_ffn = _scatter_chips("ffn", rs_ffn, du2)
    dh1, dmix, dg_ffn_pre, dg_mix_post = _mid_norms_bwd(dy, du2, h1, mix, norm_ffn_pre, norm_mix_post)

    da_sb, da_fx, dgf = _merge_bwd(dmix, w_out_full, gf, a_sb, a_fx, dep=rs_ffn[4])
    dw_out = _mm_plain("dw_out", "nn", merged_t, dmix, BF16).reshape(N_DEV, cs, d)

    def branch_bwd(tag, da, w_b, o_t, width):
        tb = _tile(width, 1024)
        do = _matmul("do_" + tag, "nt",
                     [(da, pl.BlockSpec((tm, cs), lambda i, j, k: (i, k)),
                       w_b, pl.BlockSpec((None, tb, cs), lambda i, j, k: (k, j, 0)))],
                     (s // tm, width // tb, N_DEV), (tm, tb), _sds((s, width), BF16),
                     pl.BlockSpec((tm, tb), lambda i, j, k: (i, j)))
        dw = _matmul("dw_" + tag, "nn",
                     [(o_t, pl.BlockSpec((width, s), lambda j, i, k: (0, 0)),
                       da, pl.BlockSpec((s, cs), lambda j, i, k: (0, j)))],
                     (N_DEV, 1, 1), (width, cs), _sds((N_DEV, width, cs), BF16),
                     pl.BlockSpec((None, width, cs), lambda j, i, k: (j, 0, 0)))
        return do, dw

    do_sb, dw_sb = branch_bwd("sb", da_sb, g_sb, o_sb_t, d_sb)
    do_fx, dw_fx = branch_bwd("fox", da_fx, g_fx, o_fx_t, d_fox)

    rs_mid = _scatter_pairs("mid", [dw_sb, dw_fx, dw_out])

    dqkv = _sb_bwd(qkv, do_sb, tot, h_sb, rs_mid[4])
    rs_mid = _scatter_chips("mid", rs_mid, dqkv)
    dqkv, dcum = _fox_bwd(dqkv, qkv, do_fx, o_fx32, lse, cum_col, cum_row, h_fox, h_sb, rs_mid[4])
    dgf, db_part = _forget_bwd(dgf, dcum, gf, b_pad, f_blk)
    dw_in = _w_in_grad_parts(_mm_plain("dw_qkv", "nn", u_t, dqkv, BF16), _mm_plain("dw_gates", "nn", u_t, dgf, BF16), lay)
    rs_in = _scatter_pairs("in", [dw_in])
    du = _mm_plain("du_qkv", "nt", dqkv, w_cat, F32, tn=1024, dep=rs_in[4])
    rs_in = _scatter_chips("in", rs_in, du)
    du = _mm_plain("du_gates", "nt", dgf, w_cat, F32, tn=1024, k_off=n_qkv, init=du, dep=rs_in[4])
    dx, dg_mix_pre = _pre_norm_bwd(dh1, du, xs, norm_mix_pre)

    upd = {}

    def update_group(tag, rs, names, after):
        parts = _scatter_end(tag, rs, after)
        for nm, p in zip(names, parts):
            w, m, v = decayed.get(nm, weights[nm])
            upd[nm] = _update("update_" + nm, p, w, m, v, layout=lay if nm == "w_in" else None, decayed=nm in decayed,
                              transposed_out=nm in ("w_in", "w_ffn_gate", "w_ffn_up"))

    update_group("ffn", rs_ffn, ("w_ffn_gate", "w_ffn_up", "w_ffn_down"), [dx])
    update_group("mid", rs_mid, ("w_branch_sb", "w_branch_fox", "w_out"), [upd[nm][3] for nm in ("w_ffn_gate", "w_ffn_up", "w_ffn_down")])
    update_group("in", rs_in, ("w_in",), [upd[nm][3] for nm in ("w_branch_sb", "w_branch_fox", "w_out")])

    small = ((norm_mix_pre, m_norm_mix_pre, v_norm_mix_pre), (norm_mix_post, m_norm_mix_post, v_norm_mix_post),
             (norm_ffn_pre, m_norm_ffn_pre, v_norm_ffn_pre), (norm_ffn_post, m_norm_ffn_post, v_norm_ffn_post))
    pad_f = ((0, 0), (0, LANES - n_f))
    cat = lambda i: jnp.concatenate([t[i] for t in small] + [jnp.pad((b_forget, m_b_forget, v_b_forget)[i], pad_f)], axis=1)
    sm = _small_update(jnp.concatenate([dg_mix_pre, dg_mix_post, dg_ffn_pre, dg_ffn_post, db_part], axis=1),
                       cat(0), cat(1), cat(2))
    for i, nm in enumerate(("norm_mix_pre", "norm_mix_post", "norm_ffn_pre", "norm_ffn_post")):
        upd[nm] = [o[:, i * d:(i + 1) * d] for o in sm]
    upd["b_forget"] = [o[:, 4 * d:4 * d + n_f] for o in sm]

    loss = lax.psum(loss_part[0, 0], ("x", "y", "c"))
    order = ("norm_mix_pre", "norm_mix_post", "w_in", "b_forget", "w_branch_sb", "w_branch_fox", "w_out",
             "norm_ffn_pre", "norm_ffn_post", "w_ffn_gate", "w_ffn_up", "w_ffn_down")
    return (loss, dx[None]) + tuple(upd[nm][i] for i in range(4) for nm in order)
```

```python
import jax
import jax.numpy as jnp
from jax import lax
from jax.experimental import pallas as pl
from jax.experimental.pallas import tpu as pltpu

F32 = jnp.float32
BF16 = jnp.bfloat16
MESH = pl.DeviceIdType.MESH
ANY = pl.BlockSpec(memory_space=pl.ANY)
HBM = pl.BlockSpec(memory_space=pltpu.HBM)
SEM = pl.BlockSpec(memory_space=pltpu.SEMAPHORE)
EFFECT = pltpu.SideEffectType.DATAFLOW_SIDE_EFFECTING

N_DEV = 8
HEAD_DIM = 128
RMS_EPS = 1e-6
F_PAD = 512
LANES = 128
ATT_TQ = 256
ATT_TK = 256
ATT_HP = 4
NEG_BIG = -1e30
VMEM_LIMIT = 56 * 1024 * 1024

ADAM_LR = 0.001
ADAM_B1 = 0.9
ADAM_B2 = 0.999
ADAM_EPS = 1e-08
ADAM_WD = 0.01
ADAM_STEP = 10

_DIMS = {"nn": ((1,), (0,)), "nt": ((1,), (1,)), "tn": ((0,), (0,))}


def _params(sem):
    return pltpu.CompilerParams(dimension_semantics=sem, vmem_limit_bytes=VMEM_LIMIT)


def _dot(a, b, mode="nn"):
    return lax.dot_general(a.astype(BF16), b.astype(BF16), (_DIMS[mode], ((), ())), preferred_element_type=F32)


def _tile(n, pref):
    if n <= pref:
        return n
    t = (pref // LANES) * LANES
    while n % t:
        t -= LANES
    return t


def _split2(v):
    hi = v.astype(BF16)
    return hi, (v - hi.astype(F32)).astype(BF16)


def _split3(v):
    a = v.astype(BF16)
    r = v - a.astype(F32)
    b = r.astype(BF16)
    return a, b, (r - b.astype(F32)).astype(BF16)


def _tri(n, cmp):
    r = lax.broadcasted_iota(jnp.int32, (n, n), 0)
    c = lax.broadcasted_iota(jnp.int32, (n, n), 1)
    return jnp.where(cmp(r, c), 1.0, 0.0).astype(BF16)


def _lane_pick(v, h):
    lane = lax.broadcasted_iota(jnp.int32, v.shape, 1)
    return jnp.sum(jnp.where(lane == h, v, 0.0), axis=1, keepdims=True)


def _lane_put(ref, rows, h, col):
    old = ref[rows, :]
    lane = lax.broadcasted_iota(jnp.int32, old.shape, 1)
    ref[rows, :] = jnp.where(lane == h, col, old)


def _sigmoid(z):
    return 1.0 / (1.0 + jnp.exp(-z))


def _log_sigmoid(z):
    return jnp.minimum(z, 0.0) - jnp.log(1.0 + jnp.exp(-jnp.abs(z)))


def _sds(shape, dtype):
    return jax.ShapeDtypeStruct(shape, dtype)


def _matmul(name, mode, pairs, grid, acc_shape, out_shape, out_specs, extras=(), epilogue=None, init=None, dep=None):
    n_p, n_e = len(pairs), len(extras)
    nk = grid[-1]
    single = not isinstance(out_shape, (list, tuple))
    n_i = 0 if init is None else 1
    n_d = 0 if dep is None else 1

    one_step = nk == 1 and init is None

    def body(*refs):
        ab = refs[:2 * n_p]
        ex = refs[2 * n_p:2 * n_p + n_e]
        ini = refs[2 * n_p + n_e:2 * n_p + n_e + n_i]
        outs = refs[2 * n_p + n_e + n_i + n_d:len(refs) - (0 if one_step else 1)]

        def finish(total):
            if epilogue is None:
                outs[0][...] = total.astype(outs[0].dtype)
            else:
                epilogue(total, ex, outs)

        t = _dot(ab[0][...], ab[1][...], mode)
        for p in range(1, n_p):
            t = t + _dot(ab[2 * p][...], ab[2 * p + 1][...], mode)
        if one_step:
            finish(t)
            return
        acc = refs[-1]
        k = pl.program_id(len(grid) - 1)

        @pl.when(k == 0)
        def _():
            acc[...] = t if init is None else ini[0][...].astype(F32) + t

        @pl.when(k > 0)
        def _():
            acc[...] += t

        @pl.when(k == nk - 1)
        def _():
            finish(acc[...])

    in_specs = [s for (_, sa, _, sb) in pairs for s in (sa, sb)] + [s for (_, s) in extras]
    args = [v for (a, _, b, _) in pairs for v in (a, b)] + [e for (e, _) in extras]
    if init is not None:
        in_specs.append(init[1])
        args.append(init[0])
    if dep is not None:
        in_specs.append(ANY)
        args.append(dep)
    return pl.pallas_call(
        body, name=name, grid=grid, in_specs=in_specs,
        out_specs=out_specs if single else list(out_specs),
        out_shape=out_shape if single else list(out_shape),
        scratch_shapes=[] if one_step else [pltpu.VMEM(acc_shape, F32)],
        compiler_params=_params(("parallel",) * (len(grid) - 1) + ("arbitrary",)),
    )(*args)


def _mm_plain(name, mode, a, b, out_dtype, *, n_off=0, n=None, k_off=0, tm=1024, tn=1536, tk=2048, init=None, dep=None):
    if mode == "nn":
        (m, kk), nn_ = a.shape, b.shape[1]
    elif mode == "nt":
        (m, kk), nn_ = a.shape, b.shape[0]
    else:
        (kk, m), nn_ = a.shape, b.shape[1]
    n = nn_ if n is None else n
    tm, tn, tk = _tile(m, tm), _tile(n, tn), _tile(kk, tk)
    while n_off % tn or n % tn:
        tn -= LANES
    while k_off % tk or kk % tk:
        tk -= LANES
    off, koff = n_off // tn, k_off // tk
    a_spec = {"nn": pl.BlockSpec((tm, tk), lambda i, j, k: (i, k)),
              "nt": pl.BlockSpec((tm, tk), lambda i, j, k: (i, k)),
              "tn": pl.BlockSpec((tk, tm), lambda i, j, k: (k, i))}[mode]
    b_spec = {"nn": pl.BlockSpec((tk, tn), lambda i, j, k: (k, j + off)),
              "nt": pl.BlockSpec((tn, tk), lambda i, j, k: (j, k + koff)),
              "tn": pl.BlockSpec((tk, tn), lambda i, j, k: (k, j))}[mode]
    o_spec = pl.BlockSpec((tm, tn), lambda i, j, k: (i, j))
    if init is not None:
        init = (init, o_spec)
    return _matmul(name, mode, [(a, a_spec, b, b_spec)], (m // tm, n // tn, kk // tk), (tm, tn),
                   _sds((m, n), out_dtype), o_spec, init=init, dep=dep)


def _rows_call(name, body, ins, outs, s, tr=256, dep=None):
    def spec(v, per_row):
        if per_row == "transposed":
            return pl.BlockSpec((v.shape[0], tr), lambda i: (0, i))
        if per_row:
            return pl.BlockSpec((tr, v.shape[1]), lambda i: (i, 0))
        return pl.BlockSpec(v.shape, lambda i: (0, 0))
    n_in = len(ins)
    deps = [] if dep is None else [dep]

    def with_dep(*refs):
        body(*refs[:n_in], *refs[n_in + len(deps):])

    return pl.pallas_call(
        with_dep, name=name, grid=(s // tr,),
        in_specs=[spec(v, p) for v, p in ins] + [ANY] * len(deps), out_specs=[spec(v, p) for v, p in outs],
        out_shape=[_sds(v.shape, v.dtype) for v, _ in outs],
        compiler_params=_params(("arbitrary",)),
    )(*[v for v, _ in ins], *deps)


def _rsq(v):
    return lax.rsqrt(jnp.mean(v * v, axis=-1, keepdims=True) + RMS_EPS)


def _norm_bwd(dy, v, r, g):
    vh = v * r
    t = dy * g
    dv = r * (t - vh * jnp.mean(t * vh, axis=-1, keepdims=True))
    return dv, jnp.sum(dy * vh, axis=0, keepdims=True)


def _accum(ref, val):
    @pl.when(pl.program_id(0) == 0)
    def _():
        ref[...] = jnp.zeros_like(ref)
    ref[...] += val


def _pre_norm(x, g, dep=None):
    def body(x_ref, g_ref, u_ref, ut_ref):
        v = x_ref[...]
        u = (v * _rsq(v) * g_ref[...]).astype(BF16)
        u_ref[...] = u
        ut_ref[...] = u.T
    s, d = x.shape
    return _rows_call("pre_norm", body, [(x, True), (g, False)],
                      [(_sds((s, d), BF16), True), (_sds((d, s), BF16), "transposed")], s, dep=dep)


def _mid_norms(x, mix, g_post, g_pre):
    def body(x_ref, mix_ref, gp_ref, gn_ref, h_ref, u_ref, ut_ref):
        mv = mix_ref[...]
        h = x_ref[...] + mv * _rsq(mv) * gp_ref[...]
        h_ref[...] = h
        u = (h * _rsq(h) * gn_ref[...]).astype(BF16)
        u_ref[...] = u
        ut_ref[...] = u.T
    s, d = x.shape
    return _rows_call("mid_norms", body, [(x, True), (mix, True), (g_post, False), (g_pre, False)],
                      [(_sds((s, d), F32), True), (_sds((s, d), BF16), True), (_sds((d, s), BF16), "transposed")], s)


def _loss_head(h1, ff, target, g):
    s, d = h1.shape

    def body(h_ref, ff_ref, t_ref, g_ref, loss_ref, dy_ref, dff_ref, dg_ref):
        fv = ff_ref[...]
        r = _rsq(fv)
        err = h_ref[...] + fv * r * g_ref[...] - t_ref[...]
        part = 0.5 * jnp.sum(jnp.mean(err * err, axis=-1, keepdims=True), axis=0, keepdims=True)
        _accum(loss_ref, jnp.broadcast_to(part, loss_ref.shape))
        dy = err * (1.0 / d)
        dy_ref[...] = dy
        dff, dg = _norm_bwd(dy, fv, r, g_ref[...])
        dff_ref[...] = dff.astype(BF16)
        _accum(dg_ref, dg)

    return _rows_call("loss_head", body, [(h1, True), (ff, True), (target, True), (g, False)],
                      [(_sds((1, LANES), F32), False), (_sds((s, d), F32), True),
                       (_sds((s, d), BF16), True), (_sds((1, d), F32), False)], s)


def _mid_norms_bwd(dy, du2, h1, mix, g_pre, g_post):
    s, d = dy.shape

    def body(dy_ref, du_ref, h_ref, mix_ref, gn_ref, gp_ref, dh_ref, dmix_ref, dgn_ref, dgp_ref):
        h = h_ref[...]
        dh, dgn = _norm_bwd(du_ref[...], h, _rsq(h), gn_ref[...])
        dh = dh + dy_ref[...]
        dh_ref[...] = dh
        _accum(dgn_ref, dgn)
        mv = mix_ref[...]
        dmix, dgp = _norm_bwd(dh, mv, _rsq(mv), gp_ref[...])
        dmix_ref[...] = dmix.astype(BF16)
        _accum(dgp_ref, dgp)

    return _rows_call("mid_norms_bwd", body,
                      [(dy, True), (du2, True), (h1, True), (mix, True), (g_pre, False), (g_post, False)],
                      [(_sds((s, d), F32), True), (_sds((s, d), BF16), True),
                       (_sds((1, d), F32), False), (_sds((1, d), F32), False)], s)


def _pre_norm_bwd(dh1, du, x, g, dep=None):
    s, d = x.shape

    def body(dh_ref, du_ref, x_ref, g_ref, dx_ref, dg_ref):
        v = x_ref[...]
        dv, dg = _norm_bwd(du_ref[...], v, _rsq(v), g_ref[...])
        dx_ref[...] = dh_ref[...] + dv
        _accum(dg_ref, dg)

    return _rows_call("pre_norm_bwd", body, [(dh1, True), (du, True), (x, True), (g, False)],
                      [(_sds((s, d), F32), True), (_sds((1, d), F32), False)], s, dep=dep)


def _forget_fwd(gf, b_pad, f_blk):
    s = gf.shape[0]
    tb = ATT_TK
    nb = s // tb

    def body(f_ref, b_ref, col_ref, row_ref):
        incl = _tri(tb, lambda r, c: c <= r)
        carry = jnp.zeros((1, LANES), F32)
        for i in range(nb):
            lf = _log_sigmoid(f_ref[pl.ds(i * tb, tb), :] + b_ref[...])
            parts = _split3(lf)
            cum = carry + _dot(incl, parts[0]) + _dot(incl, parts[1]) + _dot(incl, parts[2])
            col_ref[pl.ds(i * tb, tb), :] = cum
            row_ref[i] = cum.T
            carry = carry + jnp.sum(lf, axis=0, keepdims=True)

    return pl.pallas_call(
        body, name="forget_fwd", grid=(1,),
        in_specs=[pl.BlockSpec((s, LANES), lambda i: (0, f_blk)), pl.BlockSpec((1, LANES), lambda i: (0, 0))],
        out_specs=[pl.BlockSpec((s, LANES), lambda i: (0, 0)), pl.BlockSpec((nb, LANES, tb), lambda i: (0, 0, 0))],
        out_shape=[_sds((s, LANES), F32), _sds((nb, LANES, tb), F32)],
        compiler_params=_params(("arbitrary",)),
    )(gf, b_pad)


def _forget_bwd(dgf, dcum, gf, b_pad, f_blk):
    s = gf.shape[0]
    tb = ATT_TK
    nb = s // tb
    sec = dgf.shape[1] // F_PAD - 1

    def body(dgf_hbm, dc_ref, f_ref, b_ref, out_ref, db_ref):
        del dgf_hbm
        incl = _tri(tb, lambda r, c: c >= r)
        carry = jnp.zeros((1, LANES), F32)
        db = jnp.zeros((1, LANES), F32)
        out_ref[...] = jnp.zeros_like(out_ref)
        for i in reversed(range(nb)):
            dc = dc_ref[pl.ds(i * tb, tb), :]
            parts = _split3(dc)
            dlf = carry + _dot(incl, parts[0]) + _dot(incl, parts[1]) + _dot(incl, parts[2])
            z = f_ref[pl.ds(i * tb, tb), :] + b_ref[...]
            df = dlf * _sigmoid(-z)
            out_ref[pl.ds(i * tb, tb), pl.ds(0, LANES)] = df.astype(BF16)
            db = db + jnp.sum(df, axis=0, keepdims=True)
            carry = carry + jnp.sum(dc, axis=0, keepdims=True)
        db_ref[...] = db

    return pl.pallas_call(
        body, name="forget_bwd", grid=(1,),
        in_specs=[ANY, pl.BlockSpec((s, LANES), lambda i: (0, 0)),
                  pl.BlockSpec((s, LANES), lambda i: (0, f_blk)), pl.BlockSpec((1, LANES), lambda i: (0, 0))],
        out_specs=[pl.BlockSpec((s, F_PAD), lambda i: (0, sec)), pl.BlockSpec((1, LANES), lambda i: (0, 0))],
        out_shape=[_sds(dgf.shape, BF16), _sds((1, LANES), F32)],
        input_output_aliases={0: 0},
        compiler_params=_params(("arbitrary",)),
    )(dgf, dcum, gf, b_pad)


def _diag_mask(strict):
    r = lax.broadcasted_iota(jnp.int32, (ATT_TQ, ATT_TK), 0)
    c = lax.broadcasted_iota(jnp.int32, (ATT_TQ, ATT_TK), 1)
    return c < r if strict else c <= r


def _qkv_specs(hb0, s):
    specs = []
    for j in range(ATT_HP):
        def col(g, j=j):
            return 3 * (hb0 + ATT_HP * g + j)
        specs += [pl.BlockSpec((ATT_TQ, HEAD_DIM), lambda g, i, col=col: (i, col(g))),
                  pl.BlockSpec((s, HEAD_DIM), lambda g, i, col=col: (0, col(g) + 1)),
                  pl.BlockSpec((s, HEAD_DIM), lambda g, i, col=col: (0, col(g) + 2))]
    return specs


def _head_cols(j):
    return pl.ds(j * HEAD_DIM, HEAD_DIM)


def _sb_fwd(qkv, n_heads):
    s = qkv.shape[0]
    scale = HEAD_DIM ** -0.5
    tq, tk = ATT_TQ, ATT_TK
    heads = range(ATT_HP)

    def body(*refs):
        qkv_refs, (o_ref, ot_ref, tot_ref) = refs[:3 * ATT_HP], refs[3 * ATT_HP:]
        g, i = pl.program_id(0), pl.program_id(1)

        @pl.when((g == 0) & (i == 0))
        def _():
            tot_ref[...] = jnp.zeros_like(tot_ref)

        qs = [qkv_refs[3 * j][...] for j in heads]
        upper = _tri(tk, lambda r, c: r > c)

        def tile(kj, carry, mask):
            rows = pl.ds(pl.multiple_of(kj * tk, tk), tk)
            z = [_dot(qs[j], qkv_refs[3 * j + 1][rows, :], "nt") * scale for j in heads]
            lsz = [_log_sigmoid(z[j]) for j in heads]
            lk = [lsz[j] - z[j] if mask is None else jnp.where(mask, lsz[j] - z[j], 0.0) for j in heads]
            parts = [_split2(lk[j]) for j in heads]
            above = [carry[j][0] + _dot(parts[j][0], upper) + _dot(parts[j][1], upper) for j in heads]
            w = [jnp.exp(lsz[j] + above[j]) for j in heads]
            if mask is not None:
                w = [jnp.where(mask, w[j], 0.0) for j in heads]
            return tuple((carry[j][0] + jnp.sum(lk[j], axis=1, keepdims=True),
                          carry[j][1] + _dot(w[j], qkv_refs[3 * j + 2][rows, :])) for j in heads)

        carry = tile(i, tuple((jnp.zeros((tq, 1), F32), jnp.zeros((tq, HEAD_DIM), F32)) for _ in heads), _diag_mask(True))
        carry = lax.fori_loop(0, i, lambda n, cr: tile(i - 1 - n, cr, None), carry)
        q_rows = pl.ds(pl.multiple_of(i * tq, tq), tq)
        for j in heads:
            c, acc = carry[j]
            o = acc.astype(BF16)
            o_ref[:, _head_cols(j)] = o
            ot_ref[_head_cols(j), :] = o.T
            _lane_put(tot_ref, q_rows, ATT_HP * g + j, c)

    wide = ATT_HP * HEAD_DIM
    return pl.pallas_call(
        body, name="sb_fwd", grid=(n_heads // ATT_HP, s // tq),
        in_specs=_qkv_specs(0, s),
        out_specs=[pl.BlockSpec((tq, wide), lambda g, i: (i, g)), pl.BlockSpec((wide, tq), lambda g, i: (g, i)),
                   pl.BlockSpec((s, LANES), lambda g, i: (0, 0))],
        out_shape=[_sds((s, n_heads * HEAD_DIM), BF16), _sds((n_heads * HEAD_DIM, s), BF16), _sds((s, LANES), F32)],
        compiler_params=_params(("arbitrary", "arbitrary")),
    )(*[qkv] * (3 * ATT_HP))


def _sb_bwd(qkv, do, tot, n_heads, dep):
    s = qkv.shape[0]
    scale = HEAD_DIM ** -0.5
    tq, tk = ATT_TQ, ATT_TK
    nq = s // tq
    hd = HEAD_DIM

    heads = range(ATT_HP)

    def body(*refs):
        qkv_refs = refs[:3 * ATT_HP]
        do_ref, tot_ref, _, out_ref, dk_acc, dv_acc = refs[3 * ATT_HP:]
        g, i = pl.program_id(0), pl.program_id(1)

        @pl.when(i == 0)
        def _():
            dk_acc[...] = jnp.zeros_like(dk_acc)
            dv_acc[...] = jnp.zeros_like(dv_acc)

        qs = [qkv_refs[3 * j][...] for j in heads]
        douts = [do_ref[:, _head_cols(j)] for j in heads]
        totals = [_lane_pick(tot_ref[...], ATT_HP * g + j) for j in heads]
        incl = _tri(tk, lambda r, c: r <= c)
        excl = _tri(tk, lambda r, c: r < c)

        def tile(kj, carry, mask):
            rows = pl.ds(pl.multiple_of(kj * tk, tk), tk)
            k_t = [qkv_refs[3 * j + 1][rows, :] for j in heads]
            z = [_dot(qs[j], k_t[j], "nt") * scale for j in heads]
            dw = [_dot(douts[j], qkv_refs[3 * j + 2][rows, :], "nt") for j in heads]
            lsz = [_log_sigmoid(z[j]) for j in heads]
            lk = [lsz[j] - z[j] if mask is None else jnp.where(mask, lsz[j] - z[j], 0.0) for j in heads]
            parts = [_split2(lk[j]) for j in heads]
            below = [carry[j][0] + _dot(parts[j][0], incl) + _dot(parts[j][1], incl) for j in heads]
            w = [jnp.exp(lsz[j] + (totals[j] - below[j])) for j in heads]
            if mask is not None:
                w = [jnp.where(mask, w[j], 0.0) for j in heads]
            e = [dw[j] * w[j] for j in heads]
            e_before = [carry[j][1] + _dot(e[j], excl) for j in heads]
            sg = [jnp.exp(lsz[j]) for j in heads]
            dz = [e[j] * (1.0 - sg[j]) - e_before[j] * sg[j] for j in heads]
            if mask is not None:
                dz = [jnp.where(mask, dz[j], 0.0) for j in heads]
            dz = [(dz[j] * scale).astype(BF16) for j in heads]
            for j in heads:
                dk_acc[j, rows, :] += _dot(dz[j], qs[j], "tn")
                dv_acc[j, rows, :] += _dot(w[j], douts[j], "tn")
            return tuple((carry[j][0] + jnp.sum(lk[j], axis=1, keepdims=True),
                          carry[j][1] + jnp.sum(e[j], axis=1, keepdims=True),
                          carry[j][2] + _dot(dz[j], k_t[j])) for j in heads)

        zero = jnp.zeros((tq, 1), F32)
        carry = lax.fori_loop(0, i, lambda kj, cr: tile(kj, cr, None),
                              tuple((zero, zero, jnp.zeros((tq, hd), F32)) for _ in heads))
        carry = tile(i, carry, _diag_mask(True))
        for j in heads:
            out_ref[pl.ds(pl.multiple_of(i * tq, tq), tq), pl.ds(3 * j * hd, hd)] = carry[j][2].astype(BF16)

        @pl.when(i == nq - 1)
        def _():
            for j in heads:
                out_ref[:, pl.ds((3 * j + 1) * hd, hd)] = dk_acc[j].astype(BF16)
                out_ref[:, pl.ds((3 * j + 2) * hd, hd)] = dv_acc[j].astype(BF16)

    wide = ATT_HP * hd
    return pl.pallas_call(
        body, name="sb_bwd", grid=(n_heads // ATT_HP, nq),
        in_specs=_qkv_specs(0, s) + [pl.BlockSpec((tq, wide), lambda g, i: (i, g)),
                                     pl.BlockSpec((tq, LANES), lambda g, i: (i, 0)), ANY],
        out_specs=pl.BlockSpec((s, 3 * wide), lambda g, i: (0, g)),
        out_shape=_sds(qkv.shape, BF16),
        scratch_shapes=[pltpu.VMEM((ATT_HP, s, hd), F32), pltpu.VMEM((ATT_HP, s, hd), F32)],
        compiler_params=_params(("arbitrary", "arbitrary")),
    )(*[qkv] * (3 * ATT_HP), do, tot, dep)


def _fox_fwd(qkv, cum_col, cum_row, n_heads, hb0, dep):
    s = qkv.shape[0]
    scale = HEAD_DIM ** -0.5
    tq, tk = ATT_TQ, ATT_TK

    heads = range(ATT_HP)

    def body(*refs):
        qkv_refs = refs[:3 * ATT_HP]
        cc_ref, cr_ref, _, o_ref, ot_ref, o32_ref, lse_ref = refs[3 * ATT_HP:]
        g, i = pl.program_id(0), pl.program_id(1)

        @pl.when((g == 0) & (i == 0))
        def _():
            lse_ref[...] = jnp.zeros_like(lse_ref)

        qs = [qkv_refs[3 * j][...] for j in heads]
        cqs = [_lane_pick(cc_ref[...], ATT_HP * g + j) for j in heads]

        def tile(kj, carry, mask):
            rows = pl.ds(pl.multiple_of(kj * tk, tk), tk)
            sc = [_dot(qs[j], qkv_refs[3 * j + 1][rows, :], "nt") * scale + cqs[j]
                  - cr_ref[kj, pl.ds(ATT_HP * g + j, 1), :] for j in heads]
            if mask is not None:
                sc = [jnp.where(mask, sc[j], NEG_BIG) for j in heads]
            m_new = [jnp.maximum(carry[j][0], jnp.max(sc[j], axis=1, keepdims=True)) for j in heads]
            p = [jnp.exp(sc[j] - m_new[j]) for j in heads]
            alpha = [jnp.exp(carry[j][0] - m_new[j]) for j in heads]
            parts = [_split2(p[j]) for j in heads]
            v_t = [qkv_refs[3 * j + 2][rows, :] for j in heads]
            pv = [_dot(parts[j][0], v_t[j]) + _dot(parts[j][1], v_t[j]) for j in heads]
            return tuple((m_new[j], alpha[j] * carry[j][1] + jnp.sum(p[j], axis=1, keepdims=True),
                          alpha[j] * carry[j][2] + pv[j]) for j in heads)

        carry = tuple((jnp.full((tq, 1), NEG_BIG, F32), jnp.zeros((tq, 1), F32), jnp.zeros((tq, HEAD_DIM), F32))
                      for _ in heads)
        carry = lax.fori_loop(0, i, lambda kj, cr: tile(kj, cr, None), carry)
        carry = tile(i, carry, _diag_mask(False))
        q_rows = pl.ds(pl.multiple_of(i * tq, tq), tq)
        for j in heads:
            m, l, acc = carry[j]
            o = acc / l
            o_ref[:, _head_cols(j)] = o.astype(BF16)
            ot_ref[_head_cols(j), :] = o.astype(BF16).T
            o32_ref[:, _head_cols(j)] = o
            _lane_put(lse_ref, q_rows, ATT_HP * g + j, m + jnp.log(l))

    nb = cum_row.shape[0]
    wide = ATT_HP * HEAD_DIM
    return pl.pallas_call(
        body, name="fox_fwd", grid=(n_heads // ATT_HP, s // tq),
        in_specs=_qkv_specs(hb0, s) + [pl.BlockSpec((tq, LANES), lambda g, i: (i, 0)),
                                       pl.BlockSpec((nb, 8, tk), lambda g, i: (0, 0, 0)), ANY],
        out_specs=[pl.BlockSpec((tq, wide), lambda g, i: (i, g)), pl.BlockSpec((wide, tq), lambda g, i: (g, i)),
                   pl.BlockSpec((tq, wide), lambda g, i: (i, g)), pl.BlockSpec((s, LANES), lambda g, i: (0, 0))],
        out_shape=[_sds((s, n_heads * HEAD_DIM), BF16), _sds((n_heads * HEAD_DIM, s), BF16),
                   _sds((s, n_heads * HEAD_DIM), F32), _sds((s, LANES), F32)],
        compiler_params=_params(("arbitrary", "arbitrary")),
    )(*[qkv] * (3 * ATT_HP), cum_col, cum_row, dep)


def _fox_bwd(dqkv, qkv, do, o, lse, cum_col, cum_row, n_heads, hb0, dep):
    s = qkv.shape[0]
    scale = HEAD_DIM ** -0.5
    tq, tk = ATT_TQ, ATT_TK
    nq = s // tq
    hd = HEAD_DIM

    heads = range(ATT_HP)
    assert hb0 % ATT_HP == 0

    def body(*refs):
        qkv_refs = refs[1:1 + 3 * ATT_HP]
        do_ref, o_ref, lse_ref, cc_ref, cr_ref, _, out_ref, dc_ref, dk_acc, dv_acc, col_acc = refs[1 + 3 * ATT_HP:]
        g, i = pl.program_id(0), pl.program_id(1)

        @pl.when((g == 0) & (i == 0))
        def _():
            dc_ref[...] = jnp.zeros_like(dc_ref)

        @pl.when(i == 0)
        def _():
            dk_acc[...] = jnp.zeros_like(dk_acc)
            dv_acc[...] = jnp.zeros_like(dv_acc)
            col_acc[...] = jnp.zeros_like(col_acc)

        qs = [qkv_refs[3 * j][...] for j in heads]
        douts = [do_ref[:, _head_cols(j)] for j in heads]
        deltas = [jnp.sum(douts[j].astype(F32) * o_ref[:, _head_cols(j)], axis=1, keepdims=True) for j in heads]
        shifts = [_lane_pick(cc_ref[...], ATT_HP * g + j) - _lane_pick(lse_ref[...], ATT_HP * g + j) for j in heads]

        def tile(kj, carry, mask):
            rows = pl.ds(pl.multiple_of(kj * tk, tk), tk)
            k_t = [qkv_refs[3 * j + 1][rows, :] for j in heads]
            sc = [_dot(qs[j], k_t[j], "nt") * scale + shifts[j] - cr_ref[kj, pl.ds(ATT_HP * g + j, 1), :] for j in heads]
            dp = [_dot(douts[j], qkv_refs[3 * j + 2][rows, :], "nt") for j in heads]
            p = [jnp.exp(sc[j]) for j in heads]
            if mask is not None:
                p = [jnp.where(mask, p[j], 0.0) for j in heads]
            ds_f = [p[j] * (dp[j] - deltas[j]) for j in heads]
            ds = [(ds_f[j] * scale).astype(BF16) for j in heads]
            for j in heads:
                col_acc[j, kj] += jnp.broadcast_to(jnp.sum(ds_f[j], axis=0, keepdims=True), (8, tk))
                dk_acc[j, rows, :] += _dot(ds[j], qs[j], "tn")
                dv_acc[j, rows, :] += _dot(p[j], douts[j], "tn")
            return tuple((carry[j][0] + _dot(ds[j], k_t[j]), carry[j][1] + jnp.sum(ds_f[j], axis=1, keepdims=True))
                         for j in heads)

        carry = lax.fori_loop(0, i, lambda kj, cr: tile(kj, cr, None),
                              tuple((jnp.zeros((tq, hd), F32), jnp.zeros((tq, 1), F32)) for _ in heads))
        carry = tile(i, carry, _diag_mask(False))
        q_rows = pl.ds(pl.multiple_of(i * tq, tq), tq)
        for j in heads:
            out_ref[q_rows, pl.ds(3 * j * hd, hd)] = carry[j][0].astype(BF16)
            _lane_put(dc_ref, q_rows, ATT_HP * g + j, carry[j][1])

        @pl.when(i == nq - 1)
        def _():
            lane = lax.broadcasted_iota(jnp.int32, (tk, LANES), 1)
            for j in heads:
                out_ref[:, pl.ds((3 * j + 1) * hd, hd)] = dk_acc[j].astype(BF16)
                out_ref[:, pl.ds((3 * j + 2) * hd, hd)] = dv_acc[j].astype(BF16)
                for kj in range(nb):
                    col = jnp.broadcast_to(col_acc[j, kj][0:1, :], (LANES, tk)).T
                    old = dc_ref[pl.ds(kj * tk, tk), :]
                    dc_ref[pl.ds(kj * tk, tk), :] = jnp.where(lane == ATT_HP * g + j, old - col, old)

    nb = cum_row.shape[0]
    wide = ATT_HP * hd
    return pl.pallas_call(
        body, name="fox_bwd", grid=(n_heads // ATT_HP, nq),
        in_specs=[ANY] + _qkv_specs(hb0, s) + [
            pl.BlockSpec((tq, wide), lambda g, i: (i, g)), pl.BlockSpec((tq, wide), lambda g, i: (i, g)),
            pl.BlockSpec((tq, LANES), lambda g, i: (i, 0)), pl.BlockSpec((tq, LANES), lambda g, i: (i, 0)),
            pl.BlockSpec((nb, 8, tk), lambda g, i: (0, 0, 0)), ANY],
        out_specs=[pl.BlockSpec((s, 3 * wide), lambda g, i: (0, hb0 // ATT_HP + g)),
                   pl.BlockSpec((s, LANES), lambda g, i: (0, 0))],
        out_shape=[_sds(dqkv.shape, BF16), _sds((s, LANES), F32)],
        scratch_shapes=[pltpu.VMEM((ATT_HP, s, hd), F32), pltpu.VMEM((ATT_HP, s, hd), F32),
                        pltpu.VMEM((ATT_HP, s // tk, 8, tk), F32)],
        input_output_aliases={0: 0},
        compiler_params=_params(("arbitrary", "arbitrary")),
    )(dqkv, *[qkv] * (3 * ATT_HP), do, o, lse, cum_col, cum_row, dep)


def _branch_merge(o_sb, o_fx, w_sb, w_fx, gf, dep, tm=1024):
    s = o_sb.shape[0]
    cs = w_sb.shape[2]
    tm = _tile(s, tm)

    def body(osb_ref, ofx_ref, wsb_ref, wfx_ref, g_ref, dep_ref, merged_ref, mt_ref, asb_ref, afx_ref):
        del dep_ref
        a_sb = _dot(osb_ref[...], wsb_ref[...])
        a_fx = _dot(ofx_ref[...], wfx_ref[...])
        g = g_ref[...]
        merged = (_sigmoid(g[:, :cs]) * a_sb + _sigmoid(g[:, cs:]) * a_fx).astype(BF16)
        merged_ref[...] = merged
        mt_ref[...] = merged.T
        asb_ref[...] = a_sb.astype(BF16)
        afx_ref[...] = a_fx.astype(BF16)

    blk = pl.BlockSpec((tm, cs), lambda i, j: (i, j))
    out = _sds((s, N_DEV * cs), BF16)
    return pl.pallas_call(
        body, name="branch_merge", grid=(s // tm, N_DEV),
        in_specs=[pl.BlockSpec((tm, o_sb.shape[1]), lambda i, j: (i, 0)),
                  pl.BlockSpec((tm, o_fx.shape[1]), lambda i, j: (i, 0)),
                  pl.BlockSpec((None,) + w_sb.shape[1:], lambda i, j: (j, 0, 0)),
                  pl.BlockSpec((None,) + w_fx.shape[1:], lambda i, j: (j, 0, 0)),
                  pl.BlockSpec((tm, 2 * cs), lambda i, j: (i, j)), ANY],
        out_specs=[blk, pl.BlockSpec((cs, tm), lambda i, j: (j, i)), blk, blk],
        out_shape=[out, _sds((N_DEV * cs, s), BF16), out, out],
        compiler_params=_params(("parallel", "arbitrary")),
    )(o_sb, o_fx, w_sb, w_fx, gf, dep)


def _merge_bwd(dmix, w_out, gf, a_sb, a_fx, tm=1024, tk=2048, dep=None):
    s, d = dmix.shape
    cs = d // N_DEV
    tm, tk = _tile(s, tm), _tile(d, tk)

    def epilogue(acc, ex, outs):
        g, a_sb, a_fx = ex[0][...], ex[1][...].astype(F32), ex[2][...].astype(F32)
        s_sb, s_fx = _sigmoid(g[:, :cs]), _sigmoid(g[:, cs:])
        outs[0][...] = (acc * s_sb).astype(BF16)
        outs[1][...] = (acc * s_fx).astype(BF16)
        outs[2][...] = jnp.concatenate([acc * a_sb * s_sb * (1.0 - s_sb), acc * a_fx * s_fx * (1.0 - s_fx)],
                                       axis=1).astype(BF16)

    blk = pl.BlockSpec((tm, cs), lambda i, j, k: (i, j))
    wide = pl.BlockSpec((tm, 2 * cs), lambda i, j, k: (i, j))
    return _matmul(
        "merge_bwd", "nt",
        [(dmix, pl.BlockSpec((tm, tk), lambda i, j, k: (i, k)), w_out, pl.BlockSpec((cs, tk), lambda i, j, k: (j, k)))],
        (s // tm, N_DEV, d // tk), (tm, cs),
        [_sds((s, d), BF16), _sds((s, d), BF16), _sds(gf.shape, BF16)], [blk, blk, wide],
        extras=[(gf, wide), (a_sb, blk), (a_fx, blk)], epilogue=epilogue, dep=dep)


def _ffn_up(u2, w_gate, w_up, dep, tm=1024):
    s, d = u2.shape
    fs = w_gate.shape[2]
    tm = _tile(s, tm)

    def body(u_ref, wg_ref, wu_ref, dep_ref, gate_ref, up_ref, act_ref, actt_ref):
        del dep_ref
        halves = [pl.ds(h * (tm // 2), tm // 2) for h in range(2)]
        gates = [_dot(u_ref[r, :], wg_ref[...]) for r in halves]
        ups = [_dot(u_ref[r, :], wu_ref[...]) for r in halves]
        for r, gate, up in zip(halves, gates, ups):
            gate_ref[r, :] = gate
            up_ref[r, :] = up
            act = (gate * _sigmoid(gate) * up).astype(BF16)
            act_ref[r, :] = act
            actt_ref[:, r] = act.T

    w_spec = pl.BlockSpec((None, d, fs), lambda i, j: (j, 0, 0))
    o_spec = pl.BlockSpec((None, tm, fs), lambda i, j: (j, i, 0))
    return pl.pallas_call(
        body, name="ffn_up", grid=(s // tm, N_DEV),
        in_specs=[pl.BlockSpec((tm, d), lambda i, j: (i, 0)), w_spec, w_spec, ANY],
        out_specs=[o_spec, o_spec, o_spec, pl.BlockSpec((None, fs, tm), lambda i, j: (j, 0, i))],
        out_shape=[_sds((N_DEV, s, fs), F32), _sds((N_DEV, s, fs), F32), _sds((N_DEV, s, fs), BF16),
                   _sds((N_DEV, fs, s), BF16)],
        compiler_params=_params(("parallel", "arbitrary")),
    )(u2, w_gate, w_up, dep)


def _ffn_down_bwd(dff, w_down, gate, up, tm=1024):
    s, d = dff.shape
    fs = w_down.shape[1]
    tm = _tile(s, tm)

    def body(dff_ref, wd_ref, gate_ref, up_ref, dgate_ref, dup_ref):
        halves = [pl.ds(h * (tm // 2), tm // 2) for h in range(2)]
        dact = [_dot(dff_ref[r, :], wd_ref[...], "nt") for r in halves]
        for r, da in zip(halves, dact):
            gate = gate_ref[r, :]
            sg = _sigmoid(gate)
            dup_ref[r, :] = (da * gate * sg).astype(BF16)
            dgate_ref[r, :] = (da * up_ref[r, :] * sg * (1.0 + gate * (1.0 - sg))).astype(BF16)

    a_spec = pl.BlockSpec((None, tm, fs), lambda i, j: (j, i, 0))
    return pl.pallas_call(
        body, name="ffn_down_bwd", grid=(s // tm, N_DEV),
        in_specs=[pl.BlockSpec((tm, d), lambda i, j: (i, 0)), pl.BlockSpec((None, fs, d), lambda i, j: (j, 0, 0)),
                  a_spec, a_spec],
        out_specs=[a_spec, a_spec],
        out_shape=[_sds((N_DEV, s, fs), BF16), _sds((N_DEV, s, fs), BF16)],
        compiler_params=_params(("parallel", "arbitrary")),
    )(dff, w_down, gate, up)


def _mesh_place():
    x, y, c = lax.axis_index("x"), lax.axis_index("y"), lax.axis_index("c")
    peers = []
    for d in range(1, N_DEV):
        px = 1 - x if d & 4 else x
        py = 1 - y if d & 2 else y
        pc = 1 - c if d & 1 else c
        peers.append((d, (px, py, pc), 4 * px + 2 * py + pc))
    return 4 * x + 2 * y + c, peers


def _flat_me():
    return 4 * lax.axis_index("x") + 2 * lax.axis_index("y") + lax.axis_index("c")


def _in_hbm(a):
    return pltpu.with_memory_space_constraint(a, pltpu.HBM)


def _pair_plan():
    x, y, c = lax.axis_index("x"), lax.axis_index("y"), lax.axis_index("c")
    return [(2 * q + (1 - c), q, q, (x, y, 1 - c)) for q in range(4)]


def _chip_plan():
    x, y, c = lax.axis_index("x"), lax.axis_index("y"), lax.axis_index("c")
    plan = []
    for fx, fy in ((1, 0), (0, 1), (1, 1)):
        cx, cy = (1 - x if fx else x), (1 - y if fy else y)
        plan.append((2 * cx + cy, 2 * x + y, 2 * cx + cy, (cx, cy, c)))
    return plan


def _split_start(name, srcs, lands, plan, k):
    n = len(srcs)

    def body(*refs):
        ins, lnd = refs[:n], refs[n:2 * n]
        send, recv, token = refs[2 * n], refs[2 * n + 1], refs[-1]
        copies = plan()
        for a in range(n):
            for t, (src, dst, _, dev) in enumerate(copies):
                pltpu.make_async_remote_copy(src_ref=ins[a].at[src], dst_ref=lnd[a].at[dst], send_sem=send.at[k * a + t],
                                             recv_sem=recv.at[k * a + t], device_id=dev, device_id_type=MESH).start()
        token[...] = jnp.zeros_like(token)

    res = pl.pallas_call(
        body, name=name,
        out_shape=[pltpu.SemaphoreType.DMA((n * k,)), pltpu.SemaphoreType.DMA((n * k,))]
        + [pltpu.HBM(a.shape, a.dtype) for a in list(srcs) + list(lands)] + [_sds((8, LANES), F32)],
        in_specs=[HBM] * (2 * n), out_specs=[SEM, SEM] + [HBM] * (2 * n) + [pl.BlockSpec(memory_space=pltpu.VMEM)],
        input_output_aliases={i: 2 + i for i in range(2 * n)},
        compiler_params=pltpu.CompilerParams(has_side_effects=EFFECT),
    )(*[_in_hbm(a) for a in srcs], *[_in_hbm(a) for a in lands])
    return res[0], res[1], res[2:2 + n], res[2 + n:2 + 2 * n], res[-1]


def _split_wait(name, send, recv, srcs, lands, plan, k, after):
    n = len(srcs)

    def body(*refs):
        ins, lnd = refs[:n], refs[n:2 * n]
        send_sem, recv_sem = refs[2 * n], refs[2 * n + 1]
        copies = plan()
        for a in range(n):
            for t, (src, _, dst, dev) in enumerate(copies):
                cp = pltpu.make_async_remote_copy(src_ref=ins[a].at[src], dst_ref=lnd[a].at[dst], send_sem=send_sem.at[k * a + t],
                                                  recv_sem=recv_sem.at[k * a + t], device_id=dev, device_id_type=MESH)
                cp.wait_send()
                cp.wait_recv()

    res = pl.pallas_call(
        body, name=name,
        out_shape=[pltpu.HBM(a.shape, a.dtype) for a in list(srcs) + list(lands)],
        in_specs=[HBM] * (2 * n) + [SEM, SEM] + [ANY] * len(after), out_specs=[HBM] * (2 * n),
        input_output_aliases={i: i for i in range(2 * n)},
        compiler_params=pltpu.CompilerParams(has_side_effects=EFFECT),
    )(*srcs, *lands, send, recv, *after)
    return res[:n], res[n:]


def _pair_add(name, parts, land):
    _, r, cols = parts.shape
    tr = max(16, min(r, ((1 << 22) // (2 * cols)) // 16 * 16))
    while r % tr:
        tr -= 16

    def body(c_ref, p_ref, l_ref, o_ref):
        del c_ref
        o_ref[...] = (p_ref[...].astype(F32) + l_ref[...].astype(F32)).astype(BF16)

    blk = pl.BlockSpec((None, tr, cols), lambda q, i, c_ref: (q, i, 0))
    return pl.pallas_call(
        body, name=name,
        grid_spec=pltpu.PrefetchScalarGridSpec(
            num_scalar_prefetch=1, grid=(4, r // tr),
            in_specs=[pl.BlockSpec((None, tr, cols), lambda q, i, c_ref: (2 * q + c_ref[0], i, 0)), blk], out_specs=blk),
        out_shape=_sds((4, r, cols), BF16),
        compiler_params=_params(("parallel", "parallel")),
    )(jnp.reshape(lax.axis_index("c"), (1,)).astype(jnp.int32), parts, land)


def _scatter_pairs(tag, parts):
    lands = [lax.empty((4,) + a.shape[1:], a.dtype) for a in parts]
    return _split_start("pair_" + tag, parts, lands, _pair_plan, 4)


def _scatter_chips(tag, started, after):
    send, recv, parts, lands, _ = started
    parts, lands = _split_wait("pair_" + tag + "_wait", send, recv, parts, lands, _pair_plan, 4, [after])
    sums = [_pair_add("pair_" + tag + "_add%d" % a, p, l) for a, (p, l) in enumerate(zip(parts, lands))]
    chip = 2 * lax.axis_index("x") + lax.axis_index("y")
    final = [lax.dynamic_update_slice_in_dim(lax.empty(v.shape, v.dtype), lax.dynamic_slice_in_dim(v, chip, 1, 0), chip, 0)
             for v in sums]
    return _split_start("chips_" + tag, sums, final, _chip_plan, 3)


def _scatter_end(tag, started, after):
    send, recv, sums, final, _ = started
    return _split_wait("chips_" + tag + "_wait", send, recv, sums, final, _chip_plan, 3, after)[1]


def _gather_targets():
    x, y, c = lax.axis_index("x"), lax.axis_index("y"), lax.axis_index("c")
    chips = [(x, y), (1 - x, y), (x, 1 - y), (1 - x, 1 - y)]
    same = [((cx, cy, c), 4 * cx + 2 * cy + c) for cx, cy in chips]
    other = [((cx, cy, 1 - c), 4 * cx + 2 * cy + 1 - c) for cx, cy in chips]
    return same[0][1], [other[0]] + same[1:], [flat for _, flat in other[1:]], other[0][0]


def _gather_start(shards):
    n = len(shards)
    me = _flat_me()
    lands = [lax.dynamic_update_slice_in_dim(lax.empty((N_DEV,) + a.shape, a.dtype), a[None], me, 0) for a in shards]

    def body(*refs):
        lnd, send, recv, token = refs[:n], refs[n], refs[n + 1], refs[-1]
        mine, targets, _, _ = _gather_targets()
        for a in range(n):
            for t, (dev, _) in enumerate(targets):
                pltpu.make_async_remote_copy(src_ref=lnd[a].at[mine], dst_ref=lnd[a].at[mine], send_sem=send.at[4 * a + t],
                                             recv_sem=recv.at[4 * a + t], device_id=dev, device_id_type=MESH).start()
        token[...] = jnp.zeros_like(token)

    res = pl.pallas_call(
        body, name="gather_start",
        out_shape=[pltpu.SemaphoreType.DMA((4 * n,)), pltpu.SemaphoreType.DMA((4 * n,))]
        + [pltpu.HBM(a.shape, a.dtype) for a in lands] + [_sds((8, LANES), F32)],
        in_specs=[HBM] * n, out_specs=[SEM, SEM] + [HBM] * n + [pl.BlockSpec(memory_space=pltpu.VMEM)],
        input_output_aliases={i: 2 + i for i in range(n)},
        compiler_params=pltpu.CompilerParams(has_side_effects=EFFECT),
    )(*[_in_hbm(a) for a in lands])
    return res[0], res[1], list(res[2:2 + n]), res[-1]


def _gather_forward(name, lands, first, send, recv, after):
    n = len(lands)

    def body(*refs):
        lnd, send_sem, recv_sem = refs[:n], refs[n], refs[n + 1]
        send2, recv2, token = refs[-3], refs[-2], refs[-1]
        mine, targets, _, sibling = _gather_targets()
        for a in range(n):
            for t, (dev, flat) in enumerate(targets):
                cp = pltpu.make_async_remote_copy(src_ref=lnd[a].at[mine], dst_ref=lnd[a].at[flat],
                                                  send_sem=send_sem.at[4 * (first + a) + t],
                                                  recv_sem=recv_sem.at[4 * (first + a) + t], device_id=dev, device_id_type=MESH)
                cp.wait_send()
                if t:
                    cp.wait_recv()
                    pltpu.make_async_remote_copy(src_ref=lnd[a].at[flat], dst_ref=lnd[a].at[flat], send_sem=send2.at[3 * a + t - 1],
                                                 recv_sem=recv2.at[3 * a + t - 1], device_id=sibling, device_id_type=MESH).start()
        token[...] = jnp.zeros_like(token)

    res = pl.pallas_call(
        body, name=name,
        out_shape=[pltpu.HBM(a.shape, a.dtype) for a in lands]
        + [pltpu.SemaphoreType.DMA((3 * n,)), pltpu.SemaphoreType.DMA((3 * n,)), _sds((8, LANES), F32)],
        in_specs=[HBM] * n + [SEM, SEM] + [ANY] * len(after),
        out_specs=[HBM] * n + [SEM, SEM, pl.BlockSpec(memory_space=pltpu.VMEM)],
        input_output_aliases={i: i for i in range(n)},
        compiler_params=pltpu.CompilerParams(has_side_effects=EFFECT),
    )(*lands, send, recv, *after)
    return list(res[:n]), res[n], res[n + 1], res[-1]


def _gather_wait(name, lands, first, recv, send2, recv2, after):
    n = len(lands)

    def body(*refs):
        lnd, recv_sem, send2_sem, recv2_sem = refs[:n], refs[n], refs[n + 1], refs[n + 2]
        mine, targets, passed, sibling = _gather_targets()
        for a in range(n):
            dev, flat = targets[0]
            pltpu.make_async_remote_copy(src_ref=lnd[a].at[mine], dst_ref=lnd[a].at[flat], send_sem=send2_sem.at[3 * a],
                                         recv_sem=recv_sem.at[4 * (first + a)], device_id=dev, device_id_type=MESH).wait_recv()
            for t in range(3):
                cp = pltpu.make_async_remote_copy(src_ref=lnd[a].at[targets[t + 1][1]], dst_ref=lnd[a].at[passed[t]],
                                                  send_sem=send2_sem.at[3 * a + t], recv_sem=recv2_sem.at[3 * a + t],
                                                  device_id=sibling, device_id_type=MESH)
                cp.wait_send()
                cp.wait_recv()

    res = pl.pallas_call(
        body, name=name, out_shape=[pltpu.HBM(a.shape, a.dtype) for a in lands],
        in_specs=[HBM] * n + [SEM, SEM, SEM, ANY], out_specs=[HBM] * n,
        input_output_aliases={i: i for i in range(n)},
        compiler_params=pltpu.CompilerParams(has_side_effects=EFFECT),
    )(*lands, recv, send2, recv2, after)
    return list(res)


def _adamw_decay(w, m, v):
    return ADAM_WD * w, ADAM_B1 * m, ADAM_B2 * v


def _adamw_finish(g, wd_w, m1, v1):
    m = m1 + (1.0 - ADAM_B1) * g
    v = v1 + (1.0 - ADAM_B2) * (g * g)
    m_hat = m / (1.0 - ADAM_B1 ** ADAM_STEP)
    v_hat = v / (1.0 - ADAM_B2 ** ADAM_STEP)
    delta = -ADAM_LR * (m_hat / (jnp.sqrt(v_hat) + ADAM_EPS) + wd_w)
    return delta, m, v


def _adamw(g, w, m, v):
    return _adamw_finish(g, *_adamw_decay(w, m, v))


def _update_prep(name, w, m, v, dep, w_done=False, block_bytes=1 << 20):
    _, r, c = m.shape
    tr = max(8, min(r, (block_bytes // (4 * c)) // 8 * 8))
    while r % tr:
        tr -= 8
    blk = pl.BlockSpec((None, tr, c), lambda i: (0, i, 0))
    if w_done:
        def body(m_ref, v_ref, dep_ref, om_ref, ov_ref):
            del dep_ref
            om_ref[...] = ADAM_B1 * m_ref[...]
            ov_ref[...] = ADAM_B2 * v_ref[...]

        m1, v1 = pl.pallas_call(
            body, name=name, grid=(r // tr,), in_specs=[blk] * 2 + [ANY], out_specs=[blk] * 2,
            out_shape=[_sds((1, r, c), F32)] * 2, compiler_params=_params(("parallel",)),
        )(m, v, dep)
        return w, m1, v1

    def body(w_ref, m_ref, v_ref, dep_ref, ow_ref, om_ref, ov_ref):
        del dep_ref
        ow_ref[...], om_ref[...], ov_ref[...] = _adamw_decay(w_ref[...], m_ref[...], v_ref[...])

    return pl.pallas_call(
        body, name=name, grid=(r // tr,), in_specs=[blk] * 3 + [ANY], out_specs=[blk] * 3,
        out_shape=[_sds((1, r, c), F32)] * 3, compiler_params=_params(("parallel",)),
    )(w, m, v, dep)


def _update(name, parts, w, m, v, layout=None, decayed=False, transposed_out=False, block_bytes=1 << 20):
    _, r, c = w.shape
    n_slots, _, cp = parts.shape
    tr = max(8, min(r, (block_bytes // (4 * cp)) // 8 * 8))
    if transposed_out:
        tr = _tile(r, 256)
    while r % tr:
        tr -= 8

    def body(p_ref, w_ref, m_ref, v_ref, g_ref, d_ref, nm_ref, nv_ref, *scratch):
        g = p_ref[0].astype(F32)
        for p in range(1, n_slots):
            g = g + p_ref[p].astype(F32)
        if layout is not None:
            s1, s2, lg = layout.my_shifts()
            lane = lax.broadcasted_iota(jnp.int32, g.shape, 1)
            scratch[0][...] = jnp.where(lane < lg, pltpu.roll(g, cp - s1, 1), pltpu.roll(g, cp - s2, 1))
            g = scratch[0][:, 0:c]
        step = _adamw_finish if decayed else _adamw
        results = (g,) + step(g, w_ref[...], m_ref[...], v_ref[...])
        if ragged:
            scratch[-2][...] = jnp.zeros_like(scratch[-2])
        for ref, val in zip((g_ref, d_ref, nm_ref, nv_ref), results):
            if not transposed_out:
                ref[...] = val
            elif not ragged:
                ref[...] = val.T
            else:
                wide, tall = scratch[-2], scratch[-1]
                wide[:, 0:c] = val
                tall[...] = wide[...].T
                ref[...] = tall[0:c, :]

    ragged = transposed_out and c % 8 != 0
    c_wide = -(-c // LANES) * LANES
    blk = pl.BlockSpec((None, tr, c), lambda i: (0, i, 0))
    out_blk = pl.BlockSpec((None, c, tr), lambda i: (0, 0, i)) if transposed_out else blk
    scratch_shapes = [] if layout is None else [pltpu.VMEM((tr, cp), F32)]
    if ragged:
        scratch_shapes += [pltpu.VMEM((tr, c_wide), F32), pltpu.VMEM((c_wide, tr), F32)]
    res = pl.pallas_call(
        body, name=name, grid=(r // tr,),
        in_specs=[pl.BlockSpec((n_slots, tr, cp), lambda i: (0, i, 0)), blk, blk, blk],
        out_specs=[out_blk] * 4, out_shape=[_sds((1, c, r) if transposed_out else (1, r, c), F32)] * 4,
        scratch_shapes=scratch_shapes,
        compiler_params=_params(("parallel",)),
    )(parts, w, m, v)
    return [jnp.transpose(o, (0, 2, 1)) for o in res] if transposed_out else res


def _small_update(part, w, m, v):
    n = part.shape[1]

    def body(p_ref, w_ref, m_ref, v_ref, g_ref, d_ref, nm_ref, nv_ref, buf, send, recv):
        me, peers = _mesh_place()
        buf[me] = p_ref[...]
        sent = []
        for d, dev, flat in peers:
            cp = pltpu.make_async_remote_copy(src_ref=p_ref, dst_ref=buf.at[me], send_sem=send.at[d],
                                              recv_sem=recv.at[d], device_id=dev, device_id_type=MESH)
            cp.start()
            sent.append(cp)
        for d, dev, flat in peers:
            pltpu.make_async_remote_copy(src_ref=p_ref, dst_ref=buf.at[flat], send_sem=send.at[d],
                                         recv_sem=recv.at[d], device_id=dev, device_id_type=MESH).wait_recv()
        for cp in sent:
            cp.wait_send()
        g = buf[0]
        for p in range(1, N_DEV):
            g = g + buf[p]
        g_ref[...] = g
        d_ref[...], nm_ref[...], nv_ref[...] = _adamw(g, w_ref[...], m_ref[...], v_ref[...])

    vm = pl.BlockSpec(memory_space=pltpu.VMEM)
    return pl.pallas_call(
        body, name="small_update", in_specs=[vm] * 4, out_specs=[vm] * 4, out_shape=[_sds((1, n), F32)] * 4,
        scratch_shapes=[pltpu.VMEM((N_DEV, 1, n), F32), pltpu.SemaphoreType.DMA((N_DEV,)),
                        pltpu.SemaphoreType.DMA((N_DEV,))],
    )(part, w, m, v)


class _WInLayout:
    def __init__(self, n8, n_f, d_sb, d_fox, d):
        assert n8 % LANES == 1 and n_f < LANES and d % (N_DEV * LANES) == 0
        self.n8, self.n_f, self.d = n8, n_f, d
        self.sp = n8 // LANES
        self.wp = (n8 + 2 * LANES - 2) // LANES * LANES
        self.n_qkv = 3 * (d_sb + d_fox)
        nq, dt, tc = self.n_qkv // LANES, d // LANES, d // N_DEV // LANES
        h_sb, h_fox = d_sb // HEAD_DIM, d_fox // HEAD_DIM
        self.sources = {}
        self.part_tile = {}
        for p in range(N_DEV):
            lg = min(max(self.n_qkv + n_f - n8 * p, 0), n8)
            s1, s2 = p, p + LANES - n_f
            spans = []
            if lg > 0:
                spans.append(("a", self.sp * p, s1 // LANES, (lg + s1 - 1) // LANES))
            if lg < n8:
                spans.append(("g", self.sp * p - 1 - nq, (lg + s2) // LANES, (n8 - 1 + s2) // LANES))
            for kind, base, first, last in spans:
                for i in range(first, last + 1):
                    assert (p, i) not in self.part_tile
                    self.part_tile[(p, i)] = (kind, base + i)
                    self.sources.setdefault((kind, base + i), []).append((p, i))
        self.cat_tiles = [("a", r * h_sb + h) for h in range(h_sb) for r in range(3)]
        self.cat_tiles += [("a", 3 * h_sb + r * h_fox + h) for h in range(h_fox) for r in range(3)]
        self.cat_tiles += [("g", which * dt + j * tc + half) for j in range(N_DEV) for which in (0, 1) for half in range(tc)]
        self.cat_tiles += [("a", nq)] + [None] * (F_PAD // LANES - 1)
        self.cat_index = {key: c for c, key in enumerate(self.cat_tiles) if key is not None}

    def my_shifts(self):
        me = _flat_me()
        return me, me + LANES - self.n_f, jnp.clip(self.n_qkv + self.n_f - self.n8 * me, 0, self.n8)


def _lane_tile(i):
    return pl.ds(i * LANES, LANES)


def _w_in_shift(w_in, lay, tr=256):
    _, d, n8 = w_in.shape
    kd = d // LANES
    kt = tr // LANES
    by_col = jnp.transpose(w_in, (0, 2, 1)).reshape(n8 * kd, LANES)

    def body(w_ref, o_ref, wd_ref, buf):
        k0 = kt * pl.program_id(0)
        buf[...] = jnp.zeros_like(buf)
        for j in range(n8 // LANES):
            for kk in range(kt):
                piece = w_ref[pl.ds(j * LANES * kd + k0 + kk, LANES, stride=kd), :]
                buf[kk * LANES:(kk + 1) * LANES, j * LANES:(j + 1) * LANES] = piece.T
        first = lax.broadcasted_iota(jnp.int32, (8, LANES), 0) == 0
        for kk in range(kt):
            row = w_ref[pl.ds((n8 - 1) * kd + k0 + kk, 1), :]
            buf[kk * LANES:(kk + 1) * LANES, n8 - 1:n8 + 7] = jnp.where(first, jnp.broadcast_to(row, (8, LANES)), 0.0).T
        wd_ref[...] = ADAM_WD * buf[:, 0:n8]
        v = buf[...]
        s1, s2, lg = lay.my_shifts()
        pos = lax.broadcasted_iota(jnp.int32, v.shape, 1)
        o_ref[...] = jnp.where(pos < lg + s1, pltpu.roll(v, s1, 1),
                               jnp.where(pos >= lg + s2, pltpu.roll(v, s2, 1), 0.0)).astype(BF16)

    return pl.pallas_call(
        body, name="w_in_shift", grid=(d // tr,),
        in_specs=[pl.BlockSpec((n8 * kd, LANES), lambda i: (0, 0))],
        out_specs=[pl.BlockSpec((tr, lay.wp), lambda i: (i, 0)), pl.BlockSpec((None, tr, n8), lambda i: (0, i, 0))],
        out_shape=[_sds((d, lay.wp), BF16), _sds((1, d, n8), F32)],
        scratch_shapes=[pltpu.VMEM((tr, lay.wp), F32)],
        compiler_params=_params(("arbitrary",)),
    )(by_col)


def _w_in_build(g_in, lay, tr=256):
    d = g_in.shape[1]
    width = len(lay.cat_tiles) * LANES

    def body(g_ref, o_ref):
        for c, key in enumerate(lay.cat_tiles):
            if key is None:
                o_ref[:, _lane_tile(c)] = jnp.zeros((tr, LANES), BF16)
                continue
            (p, i), *more = lay.sources[key]
            val = g_ref[p, :, _lane_tile(i)]
            for p2, i2 in more:
                val = val + g_ref[p2, :, _lane_tile(i2)]
            o_ref[:, _lane_tile(c)] = val

    return pl.pallas_call(
        body, name="w_in_build", grid=(d // tr,),
        in_specs=[pl.BlockSpec((N_DEV, tr, lay.wp), lambda i: (0, i, 0))],
        out_specs=pl.BlockSpec((tr, width), lambda i: (i, 0)), out_shape=_sds((d, width), BF16),
        compiler_params=_params(("parallel",)),
    )(g_in)


def _w_in_grad_parts(dwq, dwgf, lay, tr=256):
    d = dwq.shape[0]
    nq = lay.n_qkv // LANES

    def body(q_ref, g_ref, o_ref):
        for p in range(N_DEV):
            for i in range(lay.wp // LANES):
                key = lay.part_tile.get((p, i))
                if key is None:
                    o_ref[p, :, _lane_tile(i)] = jnp.zeros((tr, LANES), BF16)
                    continue
                c = lay.cat_index[key]
                o_ref[p, :, _lane_tile(i)] = q_ref[:, _lane_tile(c)] if c < nq else g_ref[:, _lane_tile(c - nq)]

    return pl.pallas_call(
        body, name="w_in_grad_parts", grid=(d // tr,),
        in_specs=[pl.BlockSpec((tr, dwq.shape[1]), lambda i: (i, 0)), pl.BlockSpec((tr, dwgf.shape[1]), lambda i: (i, 0))],
        out_specs=pl.BlockSpec((N_DEV, tr, lay.wp), lambda i: (0, i, 0)), out_shape=_sds((N_DEV, d, lay.wp), BF16),
        compiler_params=_params(("parallel",)),
    )(dwq, dwgf)


def kernel(x, norm_mix_pre, norm_mix_post, w_in, b_forget, w_branch_sb, w_branch_fox, w_out, norm_ffn_pre, norm_ffn_post, w_ffn_gate, w_ffn_up, w_ffn_down, loss_target, m_norm_mix_pre, m_norm_mix_post, m_w_in, m_b_forget, m_w_branch_sb, m_w_branch_fox, m_w_out, m_norm_ffn_pre, m_norm_ffn_post, m_w_ffn_gate, m_w_ffn_up, m_w_ffn_down, v_norm_mix_pre, v_norm_mix_post, v_w_in, v_b_forget, v_w_branch_sb, v_w_branch_fox, v_w_out, v_norm_ffn_pre, v_norm_ffn_post, v_w_ffn_gate, v_w_ffn_up, v_w_ffn_down):
    xs, target = x[0], loss_target[0]
    s, d = xs.shape
    d_sb, d_fox = w_branch_sb.shape[1], w_branch_fox.shape[1]
    h_sb, h_fox = d_sb // HEAD_DIM, d_fox // HEAD_DIM
    n_f = b_forget.shape[1]
    fs = w_ffn_gate.shape[2]
    cs = d // N_DEV
    n_qkv = 3 * (d_sb + d_fox)
    n_gf = 2 * d + F_PAD
    f_blk = 2 * d // LANES
    big = (w_in, w_branch_sb, w_branch_fox, w_out, w_ffn_gate, w_ffn_up, w_ffn_down)
    big_m = (m_w_in, m_w_branch_sb, m_w_branch_fox, m_w_out, m_w_ffn_gate, m_w_ffn_up, m_w_ffn_down)
    big_v = (v_w_in, v_w_branch_sb, v_w_branch_fox, v_w_out, v_w_ffn_gate, v_w_ffn_up, v_w_ffn_down)

    lay = _WInLayout(w_in.shape[2], n_f, d_sb, d_fox, d)
    w_in_shifted, wd_w_in = _w_in_shift(w_in, lay)
    send1, recv1, lands, token = _gather_start([w_in_shifted] + [w[0].astype(BF16) for w in big[1:]])
    b_pad = jnp.pad(b_forget, ((0, 0), (0, LANES - n_f)))

    started = token[0, 0]
    u, u_t = _pre_norm(xs, norm_mix_pre, dep=token)
    weights = dict(zip(("w_in", "w_branch_sb", "w_branch_fox", "w_out", "w_ffn_gate", "w_ffn_up", "w_ffn_down"),
                       zip(big, big_m, big_v)))
    decayed = {nm: _update_prep("decay_" + nm, *[t + started for t in weights[nm]], u)
               for nm in ("w_ffn_gate", "w_ffn_up")}
    decayed["w_in"] = _update_prep("decay_w_in", wd_w_in, m_w_in + started, v_w_in + started, u, w_done=True)
    l_in, send2, recv2, token = _gather_forward("gather_in_forward", lands[0:1], 0, send1, recv1,
                                                [u] + [t[2] for t in decayed.values()])
    (g_in,) = _gather_wait("gather_in_wait", l_in, 0, recv1, send2, recv2, token)
    w_cat = _w_in_build(g_in, lay)
    qkv = _mm_plain("proj_qkv", "nn", u, w_cat, BF16, n=n_qkv)
    gf = _mm_plain("proj_gates", "nn", u, w_cat, F32, n_off=n_qkv, n=n_gf)
    cum_col, cum_row = _forget_fwd(gf, b_pad, f_blk)
    o_sb, o_sb_t, tot = _sb_fwd(qkv, h_sb)
    l_mid, send2, recv2, token = _gather_forward("gather_mid_forward", lands[1:4], 1, send1, recv1, [o_sb])
    o_fx, o_fx_t, o_fx32, lse = _fox_fwd(qkv, cum_col, cum_row, h_fox, h_sb, token)
    g_sb, g_fx, g_out = _gather_wait("gather_mid_wait", l_mid, 1, recv1, send2, recv2, o_fx)
    w_out_full = g_out.reshape(d, d)
    merged, merged_t, a_sb, a_fx = _branch_merge(o_sb, o_fx, g_sb, g_fx, gf, o_fx)
    l_ffn, send2, recv2, token = _gather_forward("gather_ffn_forward", lands[4:6], 4, send1, recv1, [merged])
    mix = _mm_plain("out_proj", "nn", merged, w_out_full, F32, dep=token)
    h1, u2, u2_t = _mid_norms(xs, mix, norm_mix_post, norm_ffn_pre)
    g_gate, g_up = _gather_wait("gather_ffn_wait", l_ffn, 4, recv1, send2, recv2, u2)
    l_down, send2, recv2, token = _gather_forward("gather_down_forward", lands[6:7], 6, send1, recv1, [u2])
    gate, up, act, act_t = _ffn_up(u2, g_gate, g_up, token)
    (g_down,) = _gather_wait("gather_down_wait", l_down, 6, recv1, send2, recv2, act)
    tm, tn = _tile(s, 1024), _tile(d, 1024)
    tw = _tile(d, 2048)
    ff = _matmul("ffn_down", "nn",
                 [(act, pl.BlockSpec((None, tm, fs), lambda i, j, k: (k, i, 0)),
                   g_down, pl.BlockSpec((None, fs, tw), lambda i, j, k: (k, 0, j)))],
                 (s // tm, d // tw, N_DEV), (tm, tw), _sds((s, d), F32), pl.BlockSpec((tm, tw), lambda i, j, k: (i, j)))
    loss_part, dy, dff, dg_ffn_post = _loss_head(h1, ff, target, norm_ffn_post)

    dgate, dup = _ffn_down_bwd(dff, g_down, gate, up)
    dw_down = _matmul("dw_down", "nn",
                      [(act_t, pl.BlockSpec((None, fs, s), lambda j, n, k: (j, 0, 0)),
                        dff, pl.BlockSpec((s, tn), lambda j, n, k: (0, n)))],
                      (N_DEV, d // tn, 1), (fs, tn), _sds((N_DEV, fs, d), BF16),
                      pl.BlockSpec((None, fs, tn), lambda j, n, k: (j, 0, n)))

    def dw_up(name, dact):
        return _matmul(name, "nn",
                       [(u2_t, pl.BlockSpec((tn, s), lambda j, i, k: (i, 0)),
                         dact, pl.BlockSpec((None, s, fs), lambda j, i, k: (j, 0, 0)))],
                       (N_DEV, d // tn, 1), (tn, fs), _sds((N_DEV, d, fs), BF16),
                       pl.BlockSpec((None, tn, fs), lambda j, i, k: (j, i, 0)))

    dw_gate, dw_upw = dw_up("dw_gate", dgate), dw_up("dw_up", dup)
    rs_ffn = _scatter_pairs("ffn", [dw_gate, dw_upw, dw_down])
    a_spec = pl.BlockSpec((None, tm, fs), lambda i, j, k: (k, i, 0))
    b_spec = pl.BlockSpec((None, tw, fs), lambda i, j, k: (k, j, 0))
    du2 = _matmul("du2", "nt", [(dgate, a_spec, g_gate, b_spec), (dup, a_spec, g_up, b_spec)],
                  (s // tm, d // tw, N_DEV), (tm, tw), _sds((s, d), F32), pl.BlockSpec((tm, tw), lambda i, j, k: (i, j)),
                  dep=rs_ffn[4])
    rs_ffn = _scatter_chips("ffn", rs_ffn, du2)
    dh1, dmix, dg_ffn_pre, dg_mix_post = _mid_norms_bwd(dy, du2, h1, mix, norm_ffn_pre, norm_mix_post)

    da_sb, da_fx, dgf = _merge_bwd(dmix, w_out_full, gf, a_sb, a_fx, dep=rs_ffn[4])
    dw_out = _mm_plain("dw_out", "nn", merged_t, dmix, BF16).reshape(N_DEV, cs, d)

    def branch_bwd(tag, da, w_b, o_t, width):
        tb = _tile(width, 1024)
        do = _matmul("do_" + tag, "nt",
                     [(da, pl.BlockSpec((tm, cs), lambda i, j, k: (i, k)),
                       w_b, pl.BlockSpec((None, tb, cs), lambda i, j, k: (k, j, 0)))],
                     (s // tm, width // tb, N_DEV), (tm, tb), _sds((s, width), BF16),
                     pl.BlockSpec((tm, tb), lambda i, j, k: (i, j)))
        dw = _matmul("dw_" + tag, "nn",
                     [(o_t, pl.BlockSpec((width, s), lambda j, i, k: (0, 0)),
                       da, pl.BlockSpec((s, cs), lambda j, i, k: (0, j)))],
                     (N_DEV, 1, 1), (width, cs), _sds((N_DEV, width, cs), BF16),
                     pl.BlockSpec((None, width, cs), lambda j, i, k: (j, 0, 0)))
        return do, dw

    do_sb, dw_sb = branch_bwd("sb", da_sb, g_sb, o_sb_t, d_sb)
    do_fx, dw_fx = branch_bwd("fox", da_fx, g_fx, o_fx_t, d_fox)

    rs_mid = _scatter_pairs("mid", [dw_sb, dw_fx, dw_out])

    dqkv = _sb_bwd(qkv, do_sb, tot, h_sb, rs_mid[4])
    rs_mid = _scatter_chips("mid", rs_mid, dqkv)
    dqkv, dcum = _fox_bwd(dqkv, qkv, do_fx, o_fx32, lse, cum_col, cum_row, h_fox, h_sb, rs_mid[4])
    dgf, db_part = _forget_bwd(dgf, dcum, gf, b_pad, f_blk)
    dw_in = _w_in_grad_parts(_mm_plain("dw_qkv", "nn", u_t, dqkv, BF16), _mm_plain("dw_gates", "nn", u_t, dgf, BF16), lay)
    rs_in = _scatter_pairs("in", [dw_in])
    du = _mm_plain("du_qkv", "nt", dqkv, w_cat, F32, tn=1024, dep=rs_in[4])
    rs_in = _scatter_chips("in", rs_in, du)
    du = _mm_plain("du_gates", "nt", dgf, w_cat, F32, tn=1024, k_off=n_qkv, init=du, dep=rs_in[4])
    dx, dg_mix_pre = _pre_norm_bwd(dh1, du, xs, norm_mix_pre)

    upd = {}

    def update_group(tag, rs, names, after):
        parts = _scatter_end(tag, rs, after)
        for nm, p in zip(names, parts):
            w, m, v = decayed.get(nm, weights[nm])
            upd[nm] = _update("update_" + nm, p, w, m, v, layout=lay if nm == "w_in" else None, decayed=nm in decayed,
                              transposed_out=nm in ("w_in", "w_ffn_gate", "w_ffn_up"))

    update_group("ffn", rs_ffn, ("w_ffn_gate", "w_ffn_up", "w_ffn_down"), [dx])
    update_group("mid", rs_mid, ("w_branch_sb", "w_branch_fox", "w_out"), [upd[nm][3] for nm in ("w_ffn_gate", "w_ffn_up", "w_ffn_down")])
    update_group("in", rs_in, ("w_in",), [upd[nm][3] for nm in ("w_branch_sb", "w_branch_fox", "w_out")])

    small = ((norm_mix_pre, m_norm_mix_pre, v_norm_mix_pre), (norm_mix_post, m_norm_mix_post, v_norm_mix_post),
             (norm_ffn_pre, m_norm_ffn_pre, v_norm_ffn_pre), (norm_ffn_post, m_norm_ffn_post, v_norm_ffn_post))
    pad_f = ((0, 0), (0, LANES - n_f))
    cat = lambda i: jnp.concatenate([t[i] for t in small] + [jnp.pad((b_forget, m_b_forget, v_b_forget)[i], pad_f)], axis=1)
    sm = _small_update(jnp.concatenate([dg_mix_pre, dg_mix_post, dg_ffn_pre, dg_ffn_post, db_part], axis=1),
                       cat(0), cat(1), cat(2))
    for i, nm in enumerate(("norm_mix_pre", "norm_mix_post", "norm_ffn_pre", "norm_ffn_post")):
        upd[nm] = [o[:, i * d:(i + 1) * d] for o in sm]
    upd["b_forget"] = [o[:, 4 * d:4 * d + n_f] for o in sm]

    loss = lax.psum(loss_part[0, 0], ("x", "y", "c"))
    order = ("norm_mix_pre", "norm_mix_post", "w_in", "b_forget", "w_branch_sb", "w_branch_fox", "w_out",
             "norm_ffn_pre", "norm_ffn_post", "w_ffn_gate", "w_ffn_up", "w_ffn_down")
    return (loss, dx[None]) + tuple(upd[nm][i] for i in range(4) for nm in order)
```

```python
import jax
import jax.numpy as jnp
from jax import lax
from jax.experimental import pallas as pl
from jax.experimental.pallas import tpu as pltpu

F32 = jnp.float32
BF16 = jnp.bfloat16
MESH = pl.DeviceIdType.MESH
ANY = pl.BlockSpec(memory_space=pl.ANY)
HBM = pl.BlockSpec(memory_space=pltpu.HBM)
SEM = pl.BlockSpec(memory_space=pltpu.SEMAPHORE)
EFFECT = pltpu.SideEffectType.DATAFLOW_SIDE_EFFECTING

N_DEV = 8
HEAD_DIM = 128
RMS_EPS = 1e-6
F_PAD = 512
LANES = 128
ATT_TQ = 256
ATT_TK = 256
ATT_HP = 4
NEG_BIG = -1e30
VMEM_LIMIT = 56 * 1024 * 1024

ADAM_LR = 0.001
ADAM_B1 = 0.9
ADAM_B2 = 0.999
ADAM_EPS = 1e-08
ADAM_WD = 0.01
ADAM_STEP = 10

_DIMS = {"nn": ((1,), (0,)), "nt": ((1,), (1,)), "tn": ((0,), (0,))}


def _params(sem):
    return pltpu.CompilerParams(dimension_semantics=sem, vmem_limit_bytes=VMEM_LIMIT)


def _dot(a, b, mode="nn"):
    return lax.dot_general(a.astype(BF16), b.astype(BF16), (_DIMS[mode], ((), ())), preferred_element_type=F32)


def _tile(n, pref):
    if n <= pref:
        return n
    t = (pref // LANES) * LANES
    while n % t:
        t -= LANES
    return t


def _split2(v):
    hi = v.astype(BF16)
    return hi, (v - hi.astype(F32)).astype(BF16)


def _split3(v):
    a = v.astype(BF16)
    r = v - a.astype(F32)
    b = r.astype(BF16)
    return a, b, (r - b.astype(F32)).astype(BF16)


def _tri(n, cmp):
    r = lax.broadcasted_iota(jnp.int32, (n, n), 0)
    c = lax.broadcasted_iota(jnp.int32, (n, n), 1)
    return jnp.where(cmp(r, c), 1.0, 0.0).astype(BF16)


def _lane_pick(v, h):
    lane = lax.broadcasted_iota(jnp.int32, v.shape, 1)
    return jnp.sum(jnp.where(lane == h, v, 0.0), axis=1, keepdims=True)


def _lane_put(ref, rows, h, col):
    old = ref[rows, :]
    lane = lax.broadcasted_iota(jnp.int32, old.shape, 1)
    ref[rows, :] = jnp.where(lane == h, col, old)


def _sigmoid(z):
    return 1.0 / (1.0 + jnp.exp(-z))


def _log_sigmoid(z):
    return jnp.minimum(z, 0.0) - jnp.log(1.0 + jnp.exp(-jnp.abs(z)))


def _sds(shape, dtype):
    return jax.ShapeDtypeStruct(shape, dtype)


def _matmul(name, mode, pairs, grid, acc_shape, out_shape, out_specs, extras=(), epilogue=None, init=None, dep=None):
    n_p, n_e = len(pairs), len(extras)
    nk = grid[-1]
    single = not isinstance(out_shape, (list, tuple))
    n_i = 0 if init is None else 1
    n_d = 0 if dep is None else 1

    one_step = nk == 1 and init is None

    def body(*refs):
        ab = refs[:2 * n_p]
        ex = refs[2 * n_p:2 * n_p + n_e]
        ini = refs[2 * n_p + n_e:2 * n_p + n_e + n_i]
        outs = refs[2 * n_p + n_e + n_i + n_d:len(refs) - (0 if one_step else 1)]

        def finish(total):
            if epilogue is None:
                outs[0][...] = total.astype(outs[0].dtype)
            else:
                epilogue(total, ex, outs)

        t = _dot(ab[0][...], ab[1][...], mode)
        for p in range(1, n_p):
            t = t + _dot(ab[2 * p][...], ab[2 * p + 1][...], mode)
        if one_step:
            finish(t)
            return
        acc = refs[-1]
        k = pl.program_id(len(grid) - 1)

        @pl.when(k == 0)
        def _():
            acc[...] = t if init is None else ini[0][...].astype(F32) + t

        @pl.when(k > 0)
        def _():
            acc[...] += t

        @pl.when(k == nk - 1)
        def _():
            finish(acc[...])

    in_specs = [s for (_, sa, _, sb) in pairs for s in (sa, sb)] + [s for (_, s) in extras]
    args = [v for (a, _, b, _) in pairs for v in (a, b)] + [e for (e, _) in extras]
    if init is not None:
        in_specs.append(init[1])
        args.append(init[0])
    if dep is not None:
        in_specs.append(ANY)
        args.append(dep)
    return pl.pallas_call(
        body, name=name, grid=grid, in_specs=in_specs,
        out_specs=out_specs if single else list(out_specs),
        out_shape=out_shape if single else list(out_shape),
        scratch_shapes=[] if one_step else [pltpu.VMEM(acc_shape, F32)],
        compiler_params=_params(("parallel",) * (len(grid) - 1) + ("arbitrary",)),
    )(*args)


def _mm_plain(name, mode, a, b, out_dtype, *, n_off=0, n=None, k_off=0, tm=1024, tn=1536, tk=2048, init=None, dep=None):
    if mode == "nn":
        (m, kk), nn_ = a.shape, b.shape[1]
    elif mode == "nt":
        (m, kk), nn_ = a.shape, b.shape[0]
    else:
        (kk, m), nn_ = a.shape, b.shape[1]
    n = nn_ if n is None else n
    tm, tn, tk = _tile(m, tm), _tile(n, tn), _tile(kk, tk)
    while n_off % tn or n % tn:
        tn -= LANES
    while k_off % tk or kk % tk:
        tk -= LANES
    off, koff = n_off // tn, k_off // tk
    a_spec = {"nn": pl.BlockSpec((tm, tk), lambda i, j, k: (i, k)),
              "nt": pl.BlockSpec((tm, tk), lambda i, j, k: (i, k)),
              "tn": pl.BlockSpec((tk, tm), lambda i, j, k: (k, i))}[mode]
    b_spec = {"nn": pl.BlockSpec((tk, tn), lambda i, j, k: (k, j + off)),
              "nt": pl.BlockSpec((tn, tk), lambda i, j, k: (j, k + koff)),
              "tn": pl.BlockSpec((tk, tn), lambda i, j, k: (k, j))}[mode]
    o_spec = pl.BlockSpec((tm, tn), lambda i, j, k: (i, j))
    if init is not None:
        init = (init, o_spec)
    return _matmul(name, mode, [(a, a_spec, b, b_spec)], (m // tm, n // tn, kk // tk), (tm, tn),
                   _sds((m, n), out_dtype), o_spec, init=init, dep=dep)


def _rows_call(name, body, ins, outs, s, tr=256, dep=None):
    def spec(v, per_row):
        if per_row == "transposed":
            return pl.BlockSpec((v.shape[0], tr), lambda i: (0, i))
        if per_row:
            return pl.BlockSpec((tr, v.shape[1]), lambda i: (i, 0))
        return pl.BlockSpec(v.shape, lambda i: (0, 0))
    n_in = len(ins)
    deps = [] if dep is None else [dep]

    def with_dep(*refs):
        body(*refs[:n_in], *refs[n_in + len(deps):])

    return pl.pallas_call(
        with_dep, name=name, grid=(s // tr,),
        in_specs=[spec(v, p) for v, p in ins] + [ANY] * len(deps), out_specs=[spec(v, p) for v, p in outs],
        out_shape=[_sds(v.shape, v.dtype) for v, _ in outs],
        compiler_params=_params(("arbitrary",)),
    )(*[v for v, _ in ins], *deps)


def _rsq(v):
    return lax.rsqrt(jnp.mean(v * v, axis=-1, keepdims=True) + RMS_EPS)


def _norm_bwd(dy, v, r, g):
    vh = v * r
    t = dy * g
    dv = r * (t - vh * jnp.mean(t * vh, axis=-1, keepdims=True))
    return dv, jnp.sum(dy * vh, axis=0, keepdims=True)


def _accum(ref, val):
    @pl.when(pl.program_id(0) == 0)
    def _():
        ref[...] = jnp.zeros_like(ref)
    ref[...] += val


def _pre_norm(x, g, dep=None):
    def body(x_ref, g_ref, u_ref, ut_ref):
        v = x_ref[...]
        u = (v * _rsq(v) * g_ref[...]).astype(BF16)
        u_ref[...] = u
        ut_ref[...] = u.T
    s, d = x.shape
    return _rows_call("pre_norm", body, [(x, True), (g, False)],
                      [(_sds((s, d), BF16), True), (_sds((d, s), BF16), "transposed")], s, dep=dep)


def _mid_norms(x, mix, g_post, g_pre):
    def body(x_ref, mix_ref, gp_ref, gn_ref, h_ref, u_ref, ut_ref):
        mv = mix_ref[...]
        h = x_ref[...] + mv * _rsq(mv) * gp_ref[...]
        h_ref[...] = h
        u = (h * _rsq(h) * gn_ref[...]).astype(BF16)
        u_ref[...] = u
        ut_ref[...] = u.T
    s, d = x.shape
    return _rows_call("mid_norms", body, [(x, True), (mix, True), (g_post, False), (g_pre, False)],
                      [(_sds((s, d), F32), True), (_sds((s, d), BF16), True), (_sds((d, s), BF16), "transposed")], s)


def _loss_head(h1, ff, target, g):
    s, d = h1.shape

    def body(h_ref, ff_ref, t_ref, g_ref, loss_ref, dy_ref, dff_ref, dg_ref):
        fv = ff_ref[...]
        r = _rsq(fv)
        err = h_ref[...] + fv * r * g_ref[...] - t_ref[...]
        part = 0.5 * jnp.sum(jnp.mean(err * err, axis=-1, keepdims=True), axis=0, keepdims=True)
        _accum(loss_ref, jnp.broadcast_to(part, loss_ref.shape))
        dy = err * (1.0 / d)
        dy_ref[...] = dy
        dff, dg = _norm_bwd(dy, fv, r, g_ref[...])
        dff_ref[...] = dff.astype(BF16)
        _accum(dg_ref, dg)

    return _rows_call("loss_head", body, [(h1, True), (ff, True), (target, True), (g, False)],
                      [(_sds((1, LANES), F32), False), (_sds((s, d), F32), True),
                       (_sds((s, d), BF16), True), (_sds((1, d), F32), False)], s)


def _mid_norms_bwd(dy, du2, h1, mix, g_pre, g_post):
    s, d = dy.shape

    def body(dy_ref, du_ref, h_ref, mix_ref, gn_ref, gp_ref, dh_ref, dmix_ref, dgn_ref, dgp_ref):
        h = h_ref[...]
        dh, dgn = _norm_bwd(du_ref[...], h, _rsq(h), gn_ref[...])
        dh = dh + dy_ref[...]
        dh_ref[...] = dh
        _accum(dgn_ref, dgn)
        mv = mix_ref[...]
        dmix, dgp = _norm_bwd(dh, mv, _rsq(mv), gp_ref[...])
        dmix_ref[...] = dmix.astype(BF16)
        _accum(dgp_ref, dgp)

    return _rows_call("mid_norms_bwd", body,
                      [(dy, True), (du2, True), (h1, True), (mix, True), (g_pre, False), (g_post, False)],
                      [(_sds((s, d), F32), True), (_sds((s, d), BF16), True),
                       (_sds((1, d), F32), False), (_sds((1, d), F32), False)], s)


def _pre_norm_bwd(dh1, du, x, g, dep=None):
    s, d = x.shape

    def body(dh_ref, du_ref, x_ref, g_ref, dx_ref, dg_ref):
        v = x_ref[...]
        dv, dg = _norm_bwd(du_ref[...], v, _rsq(v), g_ref[...])
        dx_ref[...] = dh_ref[...] + dv
        _accum(dg_ref, dg)

    return _rows_call("pre_norm_bwd", body, [(dh1, True), (du, True), (x, True), (g, False)],
                      [(_sds((s, d), F32), True), (_sds((1, d), F32), False)], s, dep=dep)


def _forget_fwd(gf, b_pad, f_blk):
    s = gf.shape[0]
    tb = ATT_TK
    nb = s // tb

    def body(f_ref, b_ref, col_ref, row_ref):
        incl = _tri(tb, lambda r, c: c <= r)
        carry = jnp.zeros((1, LANES), F32)
        for i in range(nb):
            lf = _log_sigmoid(f_ref[pl.ds(i * tb, tb), :] + b_ref[...])
            parts = _split3(lf)
            cum = carry + _dot(incl, parts[0]) + _dot(incl, parts[1]) + _dot(incl, parts[2])
            col_ref[pl.ds(i * tb, tb), :] = cum
            row_ref[i] = cum.T
            carry = carry + jnp.sum(lf, axis=0, keepdims=True)

    return pl.pallas_call(
        body, name="forget_fwd", grid=(1,),
        in_specs=[pl.BlockSpec((s, LANES), lambda i: (0, f_blk)), pl.BlockSpec((1, LANES), lambda i: (0, 0))],
        out_specs=[pl.BlockSpec((s, LANES), lambda i: (0, 0)), pl.BlockSpec((nb, LANES, tb), lambda i: (0, 0, 0))],
        out_shape=[_sds((s, LANES), F32), _sds((nb, LANES, tb), F32)],
        compiler_params=_params(("arbitrary",)),
    )(gf, b_pad)


def _forget_bwd(dgf, dcum, gf, b_pad, f_blk):
    s = gf.shape[0]
    tb = ATT_TK
    nb = s // tb
    sec = dgf.shape[1] // F_PAD - 1

    def body(dgf_hbm, dc_ref, f_ref, b_ref, out_ref, db_ref):
        del dgf_hbm
        incl = _tri(tb, lambda r, c: c >= r)
        carry = jnp.zeros((1, LANES), F32)
        db = jnp.zeros((1, LANES), F32)
        out_ref[...] = jnp.zeros_like(out_ref)
        for i in reversed(range(nb)):
            dc = dc_ref[pl.ds(i * tb, tb), :]
            parts = _split3(dc)
            dlf = carry + _dot(incl, parts[0]) + _dot(incl, parts[1]) + _dot(incl, parts[2])
            z = f_ref[pl.ds(i * tb, tb), :] + b_ref[...]
            df = dlf * _sigmoid(-z)
            out_ref[pl.ds(i * tb, tb), pl.ds(0, LANES)] = df.astype(BF16)
            db = db + jnp.sum(df, axis=0, keepdims=True)
            carry = carry + jnp.sum(dc, axis=0, keepdims=True)
        db_ref[...] = db

    return pl.pallas_call(
        body, name="forget_bwd", grid=(1,),
        in_specs=[ANY, pl.BlockSpec((s, LANES), lambda i: (0, 0)),
                  pl.BlockSpec((s, LANES), lambda i: (0, f_blk)), pl.BlockSpec((1, LANES), lambda i: (0, 0))],
        out_specs=[pl.BlockSpec((s, F_PAD), lambda i: (0, sec)), pl.BlockSpec((1, LANES), lambda i: (0, 0))],
        out_shape=[_sds(dgf.shape, BF16), _sds((1, LANES), F32)],
        input_output_aliases={0: 0},
        compiler_params=_params(("arbitrary",)),
    )(dgf, dcum, gf, b_pad)


def _diag_mask(strict):
    r = lax.broadcasted_iota(jnp.int32, (ATT_TQ, ATT_TK), 0)
    c = lax.broadcasted_iota(jnp.int32, (ATT_TQ, ATT_TK), 1)
    return c < r if strict else c <= r


def _qkv_specs(hb0, s):
    specs = []
    for j in range(ATT_HP):
        def col(g, j=j):
            return 3 * (hb0 + ATT_HP * g + j)
        specs += [pl.BlockSpec((ATT_TQ, HEAD_DIM), lambda g, i, col=col: (i, col(g))),
                  pl.BlockSpec((s, HEAD_DIM), lambda g, i, col=col: (0, col(g) + 1)),
                  pl.BlockSpec((s, HEAD_DIM), lambda g, i, col=col: (0, col(g) + 2))]
    return specs


def _head_cols(j):
    return pl.ds(j * HEAD_DIM, HEAD_DIM)


def _sb_fwd(qkv, n_heads):
    s = qkv.shape[0]
    scale = HEAD_DIM ** -0.5
    tq, tk = ATT_TQ, ATT_TK
    heads = range(ATT_HP)

    def body(*refs):
        qkv_refs, (o_ref, ot_ref, tot_ref) = refs[:3 * ATT_HP], refs[3 * ATT_HP:]
        g, i = pl.program_id(0), pl.program_id(1)

        @pl.when((g == 0) & (i == 0))
        def _():
            tot_ref[...] = jnp.zeros_like(tot_ref)

        qs = [qkv_refs[3 * j][...] for j in heads]
        upper = _tri(tk, lambda r, c: r > c)

        def tile(kj, carry, mask):
            rows = pl.ds(pl.multiple_of(kj * tk, tk), tk)
            z = [_dot(qs[j], qkv_refs[3 * j + 1][rows, :], "nt") * scale for j in heads]
            lsz = [_log_sigmoid(z[j]) for j in heads]
            lk = [lsz[j] - z[j] if mask is None else jnp.where(mask, lsz[j] - z[j], 0.0) for j in heads]
            parts = [_split2(lk[j]) for j in heads]
            above = [carry[j][0] + _dot(parts[j][0], upper) + _dot(parts[j][1], upper) for j in heads]
            w = [jnp.exp(lsz[j] + above[j]) for j in heads]
            if mask is not None:
                w = [jnp.where(mask, w[j], 0.0) for j in heads]
            return tuple((carry[j][0] + jnp.sum(lk[j], axis=1, keepdims=True),
                          carry[j][1] + _dot(w[j], qkv_refs[3 * j + 2][rows, :])) for j in heads)

        carry = tile(i, tuple((jnp.zeros((tq, 1), F32), jnp.zeros((tq, HEAD_DIM), F32)) for _ in heads), _diag_mask(True))
        carry = lax.fori_loop(0, i, lambda n, cr: tile(i - 1 - n, cr, None), carry)
        q_rows = pl.ds(pl.multiple_of(i * tq, tq), tq)
        for j in heads:
            c, acc = carry[j]
            o = acc.astype(BF16)
            o_ref[:, _head_cols(j)] = o
            ot_ref[_head_cols(j), :] = o.T
            _lane_put(tot_ref, q_rows, ATT_HP * g + j, c)

    wide = ATT_HP * HEAD_DIM
    return pl.pallas_call(
        body, name="sb_fwd", grid=(n_heads // ATT_HP, s // tq),
        in_specs=_qkv_specs(0, s),
        out_specs=[pl.BlockSpec((tq, wide), lambda g, i: (i, g)), pl.BlockSpec((wide, tq), lambda g, i: (g, i)),
                   pl.BlockSpec((s, LANES), lambda g, i: (0, 0))],
        out_shape=[_sds((s, n_heads * HEAD_DIM), BF16), _sds((n_heads * HEAD_DIM, s), BF16), _sds((s, LANES), F32)],
        compiler_params=_params(("arbitrary", "arbitrary")),
    )(*[qkv] * (3 * ATT_HP))


def _sb_bwd(qkv, do, tot, n_heads, dep):
    s = qkv.shape[0]
    scale = HEAD_DIM ** -0.5
    tq, tk = ATT_TQ, ATT_TK
    nq = s // tq
    hd = HEAD_DIM

    heads = range(ATT_HP)

    def body(*refs):
        qkv_refs = refs[:3 * ATT_HP]
        do_ref, tot_ref, _, out_ref, dk_acc, dv_acc = refs[3 * ATT_HP:]
        g, i = pl.program_id(0), pl.program_id(1)

        @pl.when(i == 0)
        def _():
            dk_acc[...] = jnp.zeros_like(dk_acc)
            dv_acc[...] = jnp.zeros_like(dv_acc)

        qs = [qkv_refs[3 * j][...] for j in heads]
        douts = [do_ref[:, _head_cols(j)] for j in heads]
        totals = [_lane_pick(tot_ref[...], ATT_HP * g + j) for j in heads]
        incl = _tri(tk, lambda r, c: r <= c)
        excl = _tri(tk, lambda r, c: r < c)

        def tile(kj, carry, mask):
            rows = pl.ds(pl.multiple_of(kj * tk, tk), tk)
            k_t = [qkv_refs[3 * j + 1][rows, :] for j in heads]
            z = [_dot(qs[j], k_t[j], "nt") * scale for j in heads]
            dw = [_dot(douts[j], qkv_refs[3 * j + 2][rows, :], "nt") for j in heads]
            lsz = [_log_sigmoid(z[j]) for j in heads]
            lk = [lsz[j] - z[j] if mask is None else jnp.where(mask, lsz[j] - z[j], 0.0) for j in heads]
            parts = [_split2(lk[j]) for j in heads]
            below = [carry[j][0] + _dot(parts[j][0], incl) + _dot(parts[j][1], incl) for j in heads]
            w = [jnp.exp(lsz[j] + (totals[j] - below[j])) for j in heads]
            if mask is not None:
                w = [jnp.where(mask, w[j], 0.0) for j in heads]
            e = [dw[j] * w[j] for j in heads]
            e_before = [carry[j][1] + _dot(e[j], excl) for j in heads]
            sg = [jnp.exp(lsz[j]) for j in heads]
            dz = [e[j] * (1.0 - sg[j]) - e_before[j] * sg[j] for j in heads]
            if mask is not None:
                dz = [jnp.where(mask, dz[j], 0.0) for j in heads]
            dz = [(dz[j] * scale).astype(BF16) for j in heads]
            for j in heads:
                dk_acc[j, rows, :] += _dot(dz[j], qs[j], "tn")
                dv_acc[j, rows, :] += _dot(w[j], douts[j], "tn")
            return tuple((carry[j][0] + jnp.sum(lk[j], axis=1, keepdims=True),
                          carry[j][1] + jnp.sum(e[j], axis=1, keepdims=True),
                          carry[j][2] + _dot(dz[j], k_t[j])) for j in heads)

        zero = jnp.zeros((tq, 1), F32)
        carry = lax.fori_loop(0, i, lambda kj, cr: tile(kj, cr, None),
                              tuple((zero, zero, jnp.zeros((tq, hd), F32)) for _ in heads))
        carry = tile(i, carry, _diag_mask(True))
        for j in heads:
            out_ref[pl.ds(pl.multiple_of(i * tq, tq), tq), pl.ds(3 * j * hd, hd)] = carry[j][2].astype(BF16)

        @pl.when(i == nq - 1)
        def _():
            for j in heads:
                out_ref[:, pl.ds((3 * j + 1) * hd, hd)] = dk_acc[j].astype(BF16)
                out_ref[:, pl.ds((3 * j + 2) * hd, hd)] = dv_acc[j].astype(BF16)

    wide = ATT_HP * hd
    return pl.pallas_call(
        body, name="sb_bwd", grid=(n_heads // ATT_HP, nq),
        in_specs=_qkv_specs(0, s) + [pl.BlockSpec((tq, wide), lambda g, i: (i, g)),
                                     pl.BlockSpec((tq, LANES), lambda g, i: (i, 0)), ANY],
        out_specs=pl.BlockSpec((s, 3 * wide), lambda g, i: (0, g)),
        out_shape=_sds(qkv.shape, BF16),
        scratch_shapes=[pltpu.VMEM((ATT_HP, s, hd), F32), pltpu.VMEM((ATT_HP, s, hd), F32)],
        compiler_params=_params(("arbitrary", "arbitrary")),
    )(*[qkv] * (3 * ATT_HP), do, tot, dep)


def _fox_fwd(qkv, cum_col, cum_row, n_heads, hb0, dep):
    s = qkv.shape[0]
    scale = HEAD_DIM ** -0.5
    tq, tk = ATT_TQ, ATT_TK

    heads = range(ATT_HP)

    def body(*refs):
        qkv_refs = refs[:3 * ATT_HP]
        cc_ref, cr_ref, _, o_ref, ot_ref, o32_ref, lse_ref = refs[3 * ATT_HP:]
        g, i = pl.program_id(0), pl.program_id(1)

        @pl.when((g == 0) & (i == 0))
        def _():
            lse_ref[...] = jnp.zeros_like(lse_ref)

        qs = [qkv_refs[3 * j][...] for j in heads]
        cqs = [_lane_pick(cc_ref[...], ATT_HP * g + j) for j in heads]

        def tile(kj, carry, mask):
            rows = pl.ds(pl.multiple_of(kj * tk, tk), tk)
            sc = [_dot(qs[j], qkv_refs[3 * j + 1][rows, :], "nt") * scale + cqs[j]
                  - cr_ref[kj, pl.ds(ATT_HP * g + j, 1), :] for j in heads]
            if mask is not None:
                sc = [jnp.where(mask, sc[j], NEG_BIG) for j in heads]
            m_new = [jnp.maximum(carry[j][0], jnp.max(sc[j], axis=1, keepdims=True)) for j in heads]
            p = [jnp.exp(sc[j] - m_new[j]) for j in heads]
            alpha = [jnp.exp(carry[j][0] - m_new[j]) for j in heads]
            parts = [_split2(p[j]) for j in heads]
            v_t = [qkv_refs[3 * j + 2][rows, :] for j in heads]
            pv = [_dot(parts[j][0], v_t[j]) + _dot(parts[j][1], v_t[j]) for j in heads]
            return tuple((m_new[j], alpha[j] * carry[j][1] + jnp.sum(p[j], axis=1, keepdims=True),
                          alpha[j] * carry[j][2] + pv[j]) for j in heads)

        carry = tuple((jnp.full((tq, 1), NEG_BIG, F32), jnp.zeros((tq, 1), F32), jnp.zeros((tq, HEAD_DIM), F32))
                      for _ in heads)
        carry = lax.fori_loop(0, i, lambda kj, cr: tile(kj, cr, None), carry)
        carry = tile(i, carry, _diag_mask(False))
        q_rows = pl.ds(pl.multiple_of(i * tq, tq), tq)
        for j in heads:
            m, l, acc = carry[j]
            o = acc / l
            o_ref[:, _head_cols(j)] = o.astype(BF16)
            ot_ref[_head_cols(j), :] = o.astype(BF16).T
            o32_ref[:, _head_cols(j)] = o
            _lane_put(lse_ref, q_rows, ATT_HP * g + j, m + jnp.log(l))

    nb = cum_row.shape[0]
    wide = ATT_HP * HEAD_DIM
    return pl.pallas_call(
        body, name="fox_fwd", grid=(n_heads // ATT_HP, s // tq),
        in_specs=_qkv_specs(hb0, s) + [pl.BlockSpec((tq, LANES), lambda g, i: (i, 0)),
                                       pl.BlockSpec((nb, 8, tk), lambda g, i: (0, 0, 0)), ANY],
        out_specs=[pl.BlockSpec((tq, wide), lambda g, i: (i, g)), pl.BlockSpec((wide, tq), lambda g, i: (g, i)),
                   pl.BlockSpec((tq, wide), lambda g, i: (i, g)), pl.BlockSpec((s, LANES), lambda g, i: (0, 0))],
        out_shape=[_sds((s, n_heads * HEAD_DIM), BF16), _sds((n_heads * HEAD_DIM, s), BF16),
                   _sds((s, n_heads * HEAD_DIM), F32), _sds((s, LANES), F32)],
        compiler_params=_params(("arbitrary", "arbitrary")),
    )(*[qkv] * (3 * ATT_HP), cum_col, cum_row, dep)


def _fox_bwd(dqkv, qkv, do, o, lse, cum_col, cum_row, n_heads, hb0, dep):
    s = qkv.shape[0]
    scale = HEAD_DIM ** -0.5
    tq, tk = ATT_TQ, ATT_TK
    nq = s // tq
    hd = HEAD_DIM

    heads = range(ATT_HP)
    assert hb0 % ATT_HP == 0

    def body(*refs):
        qkv_refs = refs[1:1 + 3 * ATT_HP]
        do_ref, o_ref, lse_ref, cc_ref, cr_ref, _, out_ref, dc_ref, dk_acc, dv_acc, col_acc = refs[1 + 3 * ATT_HP:]
        g, i = pl.program_id(0), pl.program_id(1)

        @pl.when((g == 0) & (i == 0))
        def _():
            dc_ref[...] = jnp.zeros_like(dc_ref)

        @pl.when(i == 0)
        def _():
            dk_acc[...] = jnp.zeros_like(dk_acc)
            dv_acc[...] = jnp.zeros_like(dv_acc)
            col_acc[...] = jnp.zeros_like(col_acc)

        qs = [qkv_refs[3 * j][...] for j in heads]
        douts = [do_ref[:, _head_cols(j)] for j in heads]
        deltas = [jnp.sum(douts[j].astype(F32) * o_ref[:, _head_cols(j)], axis=1, keepdims=True) for j in heads]
        shifts = [_lane_pick(cc_ref[...], ATT_HP * g + j) - _lane_pick(lse_ref[...], ATT_HP * g + j) for j in heads]

        def tile(kj, carry, mask):
            rows = pl.ds(pl.multiple_of(kj * tk, tk), tk)
            k_t = [qkv_refs[3 * j + 1][rows, :] for j in heads]
            sc = [_dot(qs[j], k_t[j], "nt") * scale + shifts[j] - cr_ref[kj, pl.ds(ATT_HP * g + j, 1), :] for j in heads]
            dp = [_dot(douts[j], qkv_refs[3 * j + 2][rows, :], "nt") for j in heads]
            p = [jnp.exp(sc[j]) for j in heads]
            if mask is not None:
                p = [jnp.where(mask, p[j], 0.0) for j in heads]
            ds_f = [p[j] * (dp[j] - deltas[j]) for j in heads]
            ds = [(ds_f[j] * scale).astype(BF16) for j in heads]
            for j in heads:
                col_acc[j, kj] += jnp.broadcast_to(jnp.sum(ds_f[j], axis=0, keepdims=True), (8, tk))
                dk_acc[j, rows, :] += _dot(ds[j], qs[j], "tn")
                dv_acc[j, rows, :] += _dot(p[j], douts[j], "tn")
            return tuple((carry[j][0] + _dot(ds[j], k_t[j]), carry[j][1] + jnp.sum(ds_f[j], axis=1, keepdims=True))
                         for j in heads)

        carry = lax.fori_loop(0, i, lambda kj, cr: tile(kj, cr, None),
                              tuple((jnp.zeros((tq, hd), F32), jnp.zeros((tq, 1), F32)) for _ in heads))
        carry = tile(i, carry, _diag_mask(False))
        q_rows = pl.ds(pl.multiple_of(i * tq, tq), tq)
        for j in heads:
            out_ref[q_rows, pl.ds(3 * j * hd, hd)] = carry[j][0].astype(BF16)
            _lane_put(dc_ref, q_rows, ATT_HP * g + j, carry[j][1])

        @pl.when(i == nq - 1)
        def _():
            lane = lax.broadcasted_iota(jnp.int32, (tk, LANES), 1)
            for j in heads:
                out_ref[:, pl.ds((3 * j + 1) * hd, hd)] = dk_acc[j].astype(BF16)
                out_ref[:, pl.ds((3 * j + 2) * hd, hd)] = dv_acc[j].astype(BF16)
                for kj in range(nb):
                    col = jnp.broadcast_to(col_acc[j, kj][0:1, :], (LANES, tk)).T
                    old = dc_ref[pl.ds(kj * tk, tk), :]
                    dc_ref[pl.ds(kj * tk, tk), :] = jnp.where(lane == ATT_HP * g + j, old - col, old)

    nb = cum_row.shape[0]
    wide = ATT_HP * hd
    return pl.pallas_call(
        body, name="fox_bwd", grid=(n_heads // ATT_HP, nq),
        in_specs=[ANY] + _qkv_specs(hb0, s) + [
            pl.BlockSpec((tq, wide), lambda g, i: (i, g)), pl.BlockSpec((tq, wide), lambda g, i: (i, g)),
            pl.BlockSpec((tq, LANES), lambda g, i: (i, 0)), pl.BlockSpec((tq, LANES), lambda g, i: (i, 0)),
            pl.BlockSpec((nb, 8, tk), lambda g, i: (0, 0, 0)), ANY],
        out_specs=[pl.BlockSpec((s, 3 * wide), lambda g, i: (0, hb0 // ATT_HP + g)),
                   pl.BlockSpec((s, LANES), lambda g, i: (0, 0))],
        out_shape=[_sds(dqkv.shape, BF16), _sds((s, LANES), F32)],
        scratch_shapes=[pltpu.VMEM((ATT_HP, s, hd), F32), pltpu.VMEM((ATT_HP, s, hd), F32),
                        pltpu.VMEM((ATT_HP, s // tk, 8, tk), F32)],
        input_output_aliases={0: 0},
        compiler_params=_params(("arbitrary", "arbitrary")),
    )(dqkv, *[qkv] * (3 * ATT_HP), do, o, lse, cum_col, cum_row, dep)


def _branch_merge(o_sb, o_fx, w_sb, w_fx, gf, dep, tm=1024):
    s = o_sb.shape[0]
    cs = w_sb.shape[2]
    tm = _tile(s, tm)

    def body(osb_ref, ofx_ref, wsb_ref, wfx_ref, g_ref, dep_ref, merged_ref, mt_ref, asb_ref, afx_ref):
        del dep_ref
        a_sb = _dot(osb_ref[...], wsb_ref[...])
        a_fx = _dot(ofx_ref[...], wfx_ref[...])
        g = g_ref[...]
        merged = (_sigmoid(g[:, :cs]) * a_sb + _sigmoid(g[:, cs:]) * a_fx).astype(BF16)
        merged_ref[...] = merged
        mt_ref[...] = merged.T
        asb_ref[...] = a_sb.astype(BF16)
        afx_ref[...] = a_fx.astype(BF16)

    blk = pl.BlockSpec((tm, cs), lambda i, j: (i, j))
    out = _sds((s, N_DEV * cs), BF16)
    return pl.pallas_call(
        body, name="branch_merge", grid=(s // tm, N_DEV),
        in_specs=[pl.BlockSpec((tm, o_sb.shape[1]), lambda i, j: (i, 0)),
                  pl.BlockSpec((tm, o_fx.shape[1]), lambda i, j: (i, 0)),
                  pl.BlockSpec((None,) + w_sb.shape[1:], lambda i, j: (j, 0, 0)),
                  pl.BlockSpec((None,) + w_fx.shape[1:], lambda i, j: (j, 0, 0)),
                  pl.BlockSpec((tm, 2 * cs), lambda i, j: (i, j)), ANY],
        out_specs=[blk, pl.BlockSpec((cs, tm), lambda i, j: (j, i)), blk, blk],
        out_shape=[out, _sds((N_DEV * cs, s), BF16), out, out],
        compiler_params=_params(("parallel", "arbitrary")),
    )(o_sb, o_fx, w_sb, w_fx, gf, dep)


def _merge_bwd(dmix, w_out, gf, a_sb, a_fx, tm=1024, tk=2048, dep=None):
    s, d = dmix.shape
    cs = d // N_DEV
    tm, tk = _tile(s, tm), _tile(d, tk)

    def epilogue(acc, ex, outs):
        g, a_sb, a_fx = ex[0][...], ex[1][...].astype(F32), ex[2][...].astype(F32)
        s_sb, s_fx = _sigmoid(g[:, :cs]), _sigmoid(g[:, cs:])
        outs[0][...] = (acc * s_sb).astype(BF16)
        outs[1][...] = (acc * s_fx).astype(BF16)
        outs[2][...] = jnp.concatenate([acc * a_sb * s_sb * (1.0 - s_sb), acc * a_fx * s_fx * (1.0 - s_fx)],
                                       axis=1).astype(BF16)

    blk = pl.BlockSpec((tm, cs), lambda i, j, k: (i, j))
    wide = pl.BlockSpec((tm, 2 * cs), lambda i, j, k: (i, j))
    return _matmul(
        "merge_bwd", "nt",
        [(dmix, pl.BlockSpec((tm, tk), lambda i, j, k: (i, k)), w_out, pl.BlockSpec((cs, tk), lambda i, j, k: (j, k)))],
        (s // tm, N_DEV, d // tk), (tm, cs),
        [_sds((s, d), BF16), _sds((s, d), BF16), _sds(gf.shape, BF16)], [blk, blk, wide],
        extras=[(gf, wide), (a_sb, blk), (a_fx, blk)], epilogue=epilogue, dep=dep)


def _ffn_up(u2, w_gate, w_up, dep, tm=1024):
    s, d = u2.shape
    fs = w_gate.shape[2]
    tm = _tile(s, tm)

    def body(u_ref, wg_ref, wu_ref, dep_ref, gate_ref, up_ref, act_ref, actt_ref):
        del dep_ref
        halves = [pl.ds(h * (tm // 2), tm // 2) for h in range(2)]
        gates = [_dot(u_ref[r, :], wg_ref[...]) for r in halves]
        ups = [_dot(u_ref[r, :], wu_ref[...]) for r in halves]
        for r, gate, up in zip(halves, gates, ups):
            gate_ref[r, :] = gate.astype(BF16)
            up_ref[r, :] = up.astype(BF16)
            act = (gate * _sigmoid(gate) * up).astype(BF16)
            act_ref[r, :] = act
            actt_ref[:, r] = act.T

    w_spec = pl.BlockSpec((None, d, fs), lambda i, j: (j, 0, 0))
    o_spec = pl.BlockSpec((None, tm, fs), lambda i, j: (j, i, 0))
    return pl.pallas_call(
        body, name="ffn_up", grid=(s // tm, N_DEV),
        in_specs=[pl.BlockSpec((tm, d), lambda i, j: (i, 0)), w_spec, w_spec, ANY],
        out_specs=[o_spec, o_spec, o_spec, pl.BlockSpec((None, fs, tm), lambda i, j: (j, 0, i))],
        out_shape=[_sds((N_DEV, s, fs), BF16), _sds((N_DEV, s, fs), BF16), _sds((N_DEV, s, fs), BF16),
                   _sds((N_DEV, fs, s), BF16)],
        compiler_params=_params(("parallel", "arbitrary")),
    )(u2, w_gate, w_up, dep)


def _ffn_down_bwd(dff, w_down, gate, up, tm=1024):
    s, d = dff.shape
    fs = w_down.shape[1]
    tm = _tile(s, tm)

    def body(dff_ref, wd_ref, gate_ref, up_ref, dgate_ref, dup_ref):
        halves = [pl.ds(h * (tm // 2), tm // 2) for h in range(2)]
        dact = [_dot(dff_ref[r, :], wd_ref[...], "nt") for r in halves]
        for r, da in zip(halves, dact):
            gate = gate_ref[r, :].astype(F32)
            sg = _sigmoid(gate)
            dup_ref[r, :] = (da * gate * sg).astype(BF16)
            dgate_ref[r, :] = (da * up_ref[r, :].astype(F32) * sg * (1.0 + gate * (1.0 - sg))).astype(BF16)

    a_spec = pl.BlockSpec((None, tm, fs), lambda i, j: (j, i, 0))
    return pl.pallas_call(
        body, name="ffn_down_bwd", grid=(s // tm, N_DEV),
        in_specs=[pl.BlockSpec((tm, d), lambda i, j: (i, 0)), pl.BlockSpec((None, fs, d), lambda i, j: (j, 0, 0)),
                  a_spec, a_spec],
        out_specs=[a_spec, a_spec],
        out_shape=[_sds((N_DEV, s, fs), BF16), _sds((N_DEV, s, fs), BF16)],
        compiler_params=_params(("parallel", "arbitrary")),
    )(dff, w_down, gate, up)


def _mesh_place():
    x, y, c = lax.axis_index("x"), lax.axis_index("y"), lax.axis_index("c")
    peers = []
    for d in range(1, N_DEV):
        px = 1 - x if d & 4 else x
        py = 1 - y if d & 2 else y
        pc = 1 - c if d & 1 else c
        peers.append((d, (px, py, pc), 4 * px + 2 * py + pc))
    return 4 * x + 2 * y + c, peers


def _flat_me():
    return 4 * lax.axis_index("x") + 2 * lax.axis_index("y") + lax.axis_index("c")


def _in_hbm(a):
    return pltpu.with_memory_space_constraint(a, pltpu.HBM)


def _pair_plan():
    x, y, c = lax.axis_index("x"), lax.axis_index("y"), lax.axis_index("c")
    return [(2 * q + (1 - c), q, q, (x, y, 1 - c)) for q in range(4)]


def _chip_plan():
    x, y, c = lax.axis_index("x"), lax.axis_index("y"), lax.axis_index("c")
    plan = []
    for fx, fy in ((1, 0), (0, 1), (1, 1)):
        cx, cy = (1 - x if fx else x), (1 - y if fy else y)
        plan.append((2 * cx + cy, 2 * x + y, 2 * cx + cy, (cx, cy, c)))
    return plan


def _split_start(name, srcs, lands, plan, k):
    n = len(srcs)

    def body(*refs):
        ins, lnd = refs[:n], refs[n:2 * n]
        send, recv, token = refs[2 * n], refs[2 * n + 1], refs[-1]
        copies = plan()
        for a in range(n):
            for t, (src, dst, _, dev) in enumerate(copies):
                pltpu.make_async_remote_copy(src_ref=ins[a].at[src], dst_ref=lnd[a].at[dst], send_sem=send.at[k * a + t],
                                             recv_sem=recv.at[k * a + t], device_id=dev, device_id_type=MESH).start()
        token[...] = jnp.zeros_like(token)

    res = pl.pallas_call(
        body, name=name,
        out_shape=[pltpu.SemaphoreType.DMA((n * k,)), pltpu.SemaphoreType.DMA((n * k,))]
        + [pltpu.HBM(a.shape, a.dtype) for a in list(srcs) + list(lands)] + [_sds((8, LANES), F32)],
        in_specs=[HBM] * (2 * n), out_specs=[SEM, SEM] + [HBM] * (2 * n) + [pl.BlockSpec(memory_space=pltpu.VMEM)],
        input_output_aliases={i: 2 + i for i in range(2 * n)},
        compiler_params=pltpu.CompilerParams(has_side_effects=EFFECT),
    )(*[_in_hbm(a) for a in srcs], *[_in_hbm(a) for a in lands])
    return res[0], res[1], res[2:2 + n], res[2 + n:2 + 2 * n], res[-1]


def _split_wait(name, send, recv, srcs, lands, plan, k, after):
    n = len(srcs)

    def body(*refs):
        ins, lnd = refs[:n], refs[n:2 * n]
        send_sem, recv_sem = refs[2 * n], refs[2 * n + 1]
        copies = plan()
        for a in range(n):
            for t, (src, _, dst, dev) in enumerate(copies):
                cp = pltpu.make_async_remote_copy(src_ref=ins[a].at[src], dst_ref=lnd[a].at[dst], send_sem=send_sem.at[k * a + t],
                                                  recv_sem=recv_sem.at[k * a + t], device_id=dev, device_id_type=MESH)
                cp.wait_send()
                cp.wait_recv()

    res = pl.pallas_call(
        body, name=name,
        out_shape=[pltpu.HBM(a.shape, a.dtype) for a in list(srcs) + list(lands)],
        in_specs=[HBM] * (2 * n) + [SEM, SEM] + [ANY] * len(after), out_specs=[HBM] * (2 * n),
        input_output_aliases={i: i for i in range(2 * n)},
        compiler_params=pltpu.CompilerParams(has_side_effects=EFFECT),
    )(*srcs, *lands, send, recv, *after)
    return res[:n], res[n:]


def _pair_add(name, parts, land):
    _, r, cols = parts.shape
    tr = max(16, min(r, ((1 << 22) // (2 * cols)) // 16 * 16))
    while r % tr:
        tr -= 16

    def body(c_ref, p_ref, l_ref, o_ref):
        del c_ref
        o_ref[...] = (p_ref[...].astype(F32) + l_ref[...].astype(F32)).astype(BF16)

    blk = pl.BlockSpec((None, tr, cols), lambda q, i, c_ref: (q, i, 0))
    return pl.pallas_call(
        body, name=name,
        grid_spec=pltpu.PrefetchScalarGridSpec(
            num_scalar_prefetch=1, grid=(4, r // tr),
            in_specs=[pl.BlockSpec((None, tr, cols), lambda q, i, c_ref: (2 * q + c_ref[0], i, 0)), blk], out_specs=blk),
        out_shape=_sds((4, r, cols), BF16),
        compiler_params=_params(("parallel", "parallel")),
    )(jnp.reshape(lax.axis_index("c"), (1,)).astype(jnp.int32), parts, land)


def _scatter_pairs(tag, parts):
    lands = [lax.empty((4,) + a.shape[1:], a.dtype) for a in parts]
    return _split_start("pair_" + tag, parts, lands, _pair_plan, 4)


def _scatter_chips(tag, started, after):
    send, recv, parts, lands, _ = started
    parts, lands = _split_wait("pair_" + tag + "_wait", send, recv, parts, lands, _pair_plan, 4, [after])
    sums = [_pair_add("pair_" + tag + "_add%d" % a, p, l) for a, (p, l) in enumerate(zip(parts, lands))]
    chip = 2 * lax.axis_index("x") + lax.axis_index("y")
    final = [lax.dynamic_update_slice_in_dim(lax.empty(v.shape, v.dtype), lax.dynamic_slice_in_dim(v, chip, 1, 0), chip, 0)
             for v in sums]
    return _split_start("chips_" + tag, sums, final, _chip_plan, 3)


def _scatter_end(tag, started, after):
    send, recv, sums, final, _ = started
    return _split_wait("chips_" + tag + "_wait", send, recv, sums, final, _chip_plan, 3, after)[1]


def _gather_targets():
    x, y, c = lax.axis_index("x"), lax.axis_index("y"), lax.axis_index("c")
    chips = [(x, y), (1 - x, y), (x, 1 - y), (1 - x, 1 - y)]
    same = [((cx, cy, c), 4 * cx + 2 * cy + c) for cx, cy in chips]
    other = [((cx, cy, 1 - c), 4 * cx + 2 * cy + 1 - c) for cx, cy in chips]
    return same[0][1], [other[0]] + same[1:], [flat for _, flat in other[1:]], other[0][0]


def _gather_start(shards):
    n = len(shards)
    me = _flat_me()
    lands = [lax.dynamic_update_slice_in_dim(lax.empty((N_DEV,) + a.shape, a.dtype), a[None], me, 0) for a in shards]

    def body(*refs):
        lnd, send, recv, token = refs[:n], refs[n], refs[n + 1], refs[-1]
        mine, targets, _, _ = _gather_targets()
        for a in range(n):
            for t, (dev, _) in enumerate(targets):
                pltpu.make_async_remote_copy(src_ref=lnd[a].at[mine], dst_ref=lnd[a].at[mine], send_sem=send.at[4 * a + t],
                                             recv_sem=recv.at[4 * a + t], device_id=dev, device_id_type=MESH).start()
        token[...] = jnp.zeros_like(token)

    res = pl.pallas_call(
        body, name="gather_start",
        out_shape=[pltpu.SemaphoreType.DMA((4 * n,)), pltpu.SemaphoreType.DMA((4 * n,))]
        + [pltpu.HBM(a.shape, a.dtype) for a in lands] + [_sds((8, LANES), F32)],
        in_specs=[HBM] * n, out_specs=[SEM, SEM] + [HBM] * n + [pl.BlockSpec(memory_space=pltpu.VMEM)],
        input_output_aliases={i: 2 + i for i in range(n)},
        compiler_params=pltpu.CompilerParams(has_side_effects=EFFECT),
    )(*[_in_hbm(a) for a in lands])
    return res[0], res[1], list(res[2:2 + n]), res[-1]


def _gather_forward(name, lands, first, send, recv, after):
    n = len(lands)

    def body(*refs):
        lnd, send_sem, recv_sem = refs[:n], refs[n], refs[n + 1]
        send2, recv2, token = refs[-3], refs[-2], refs[-1]
        mine, targets, _, sibling = _gather_targets()
        for a in range(n):
            for t, (dev, flat) in enumerate(targets):
                cp = pltpu.make_async_remote_copy(src_ref=lnd[a].at[mine], dst_ref=lnd[a].at[flat],
                                                  send_sem=send_sem.at[4 * (first + a) + t],
                                                  recv_sem=recv_sem.at[4 * (first + a) + t], device_id=dev, device_id_type=MESH)
                cp.wait_send()
                if t:
                    cp.wait_recv()
                    pltpu.make_async_remote_copy(src_ref=lnd[a].at[flat], dst_ref=lnd[a].at[flat], send_sem=send2.at[3 * a + t - 1],
                                                 recv_sem=recv2.at[3 * a + t - 1], device_id=sibling, device_id_type=MESH).start()
        token[...] = jnp.zeros_like(token)

    res = pl.pallas_call(
        body, name=name,
        out_shape=[pltpu.HBM(a.shape, a.dtype) for a in lands]
        + [pltpu.SemaphoreType.DMA((3 * n,)), pltpu.SemaphoreType.DMA((3 * n,)), _sds((8, LANES), F32)],
        in_specs=[HBM] * n + [SEM, SEM] + [ANY] * len(after),
        out_specs=[HBM] * n + [SEM, SEM, pl.BlockSpec(memory_space=pltpu.VMEM)],
        input_output_aliases={i: i for i in range(n)},
        compiler_params=pltpu.CompilerParams(has_side_effects=EFFECT),
    )(*lands, send, recv, *after)
    return list(res[:n]), res[n], res[n + 1], res[-1]


def _gather_wait(name, lands, first, recv, send2, recv2, after):
    n = len(lands)

    def body(*refs):
        lnd, recv_sem, send2_sem, recv2_sem = refs[:n], refs[n], refs[n + 1], refs[n + 2]
        mine, targets, passed, sibling = _gather_targets()
        for a in range(n):
            dev, flat = targets[0]
            pltpu.make_async_remote_copy(src_ref=lnd[a].at[mine], dst_ref=lnd[a].at[flat], send_sem=send2_sem.at[3 * a],
                                         recv_sem=recv_sem.at[4 * (first + a)], device_id=dev, device_id_type=MESH).wait_recv()
            for t in range(3):
                cp = pltpu.make_async_remote_copy(src_ref=lnd[a].at[targets[t + 1][1]], dst_ref=lnd[a].at[passed[t]],
                                                  send_sem=send2_sem.at[3 * a + t], recv_sem=recv2_sem.at[3 * a + t],
                                                  device_id=sibling, device_id_type=MESH)
                cp.wait_send()
                cp.wait_recv()

    res = pl.pallas_call(
        body, name=name, out_shape=[pltpu.HBM(a.shape, a.dtype) for a in lands],
        in_specs=[HBM] * n + [SEM, SEM, SEM, ANY], out_specs=[HBM] * n,
        input_output_aliases={i: i for i in range(n)},
        compiler_params=pltpu.CompilerParams(has_side_effects=EFFECT),
    )(*lands, recv, send2, recv2, after)
    return list(res)


def _adamw_decay(w, m, v):
    return ADAM_WD * w, ADAM_B1 * m, ADAM_B2 * v


def _adamw_finish(g, wd_w, m1, v1):
    m = m1 + (1.0 - ADAM_B1) * g
    v = v1 + (1.0 - ADAM_B2) * (g * g)
    m_hat = m / (1.0 - ADAM_B1 ** ADAM_STEP)
    v_hat = v / (1.0 - ADAM_B2 ** ADAM_STEP)
    delta = -ADAM_LR * (m_hat / (jnp.sqrt(v_hat) + ADAM_EPS) + wd_w)
    return delta, m, v


def _adamw(g, w, m, v):
    return _adamw_finish(g, *_adamw_decay(w, m, v))


def _update_prep(name, w, m, v, dep, w_done=False, block_bytes=1 << 20):
    _, r, c = m.shape
    tr = max(8, min(r, (block_bytes // (4 * c)) // 8 * 8))
    while r % tr:
        tr -= 8
    blk = pl.BlockSpec((None, tr, c), lambda i: (0, i, 0))
    if w_done:
        def body(m_ref, v_ref, dep_ref, om_ref, ov_ref):
            del dep_ref
            om_ref[...] = ADAM_B1 * m_ref[...]
            ov_ref[...] = ADAM_B2 * v_ref[...]

        m1, v1 = pl.pallas_call(
            body, name=name, grid=(r // tr,), in_specs=[blk] * 2 + [ANY], out_specs=[blk] * 2,
            out_shape=[_sds((1, r, c), F32)] * 2, compiler_params=_params(("parallel",)),
        )(m, v, dep)
        return w, m1, v1

    def body(w_ref, m_ref, v_ref, dep_ref, ow_ref, om_ref, ov_ref):
        del dep_ref
        ow_ref[...], om_ref[...], ov_ref[...] = _adamw_decay(w_ref[...], m_ref[...], v_ref[...])

    return pl.pallas_call(
        body, name=name, grid=(r // tr,), in_specs=[blk] * 3 + [ANY], out_specs=[blk] * 3,
        out_shape=[_sds((1, r, c), F32)] * 3, compiler_params=_params(("parallel",)),
    )(w, m, v, dep)


def _update(name, parts, w, m, v, layout=None, decayed=False, transposed_out=False, block_bytes=1 << 20):
    _, r, c = w.shape
    n_slots, _, cp = parts.shape
    tr = max(8, min(r, (block_bytes // (4 * cp)) // 8 * 8))
    if transposed_out:
        tr = _tile(r, 256)
    while r % tr:
        tr -= 8

    def body(p_ref, w_ref, m_ref, v_ref, g_ref, d_ref, nm_ref, nv_ref, *scratch):
        g = p_ref[0].astype(F32)
        for p in range(1, n_slots):
            g = g + p_ref[p].astype(F32)
        if layout is not None:
            s1, s2, lg = layout.my_shifts()
            lane = lax.broadcasted_iota(jnp.int32, g.shape, 1)
            scratch[0][...] = jnp.where(lane < lg, pltpu.roll(g, cp - s1, 1), pltpu.roll(g, cp - s2, 1))
            g = scratch[0][:, 0:c]
        step = _adamw_finish if decayed else _adamw
        results = (g,) + step(g, w_ref[...], m_ref[...], v_ref[...])
        if ragged:
            scratch[-2][...] = jnp.zeros_like(scratch[-2])
        for ref, val in zip((g_ref, d_ref, nm_ref, nv_ref), results):
            if not transposed_out:
                ref[...] = val
            elif not ragged:
                ref[...] = val.T
            else:
                wide, tall = scratch[-2], scratch[-1]
                wide[:, 0:c] = val
                tall[...] = wide[...].T
                ref[...] = tall[0:c, :]

    ragged = transposed_out and c % 8 != 0
    c_wide = -(-c // LANES) * LANES
    blk = pl.BlockSpec((None, tr, c), lambda i: (0, i, 0))
    out_blk = pl.BlockSpec((None, c, tr), lambda i: (0, 0, i)) if transposed_out else blk
    scratch_shapes = [] if layout is None else [pltpu.VMEM((tr, cp), F32)]
    if ragged:
        scratch_shapes += [pltpu.VMEM((tr, c_wide), F32), pltpu.VMEM((c_wide, tr), F32)]
    res = pl.pallas_call(
        body, name=name, grid=(r // tr,),
        in_specs=[pl.BlockSpec((n_slots, tr, cp), lambda i: (0, i, 0)), blk, blk, blk],
        out_specs=[out_blk] * 4, out_shape=[_sds((1, c, r) if transposed_out else (1, r, c), F32)] * 4,
        scratch_shapes=scratch_shapes,
        compiler_params=_params(("parallel",)),
    )(parts, w, m, v)
    return [jnp.transpose(o, (0, 2, 1)) for o in res] if transposed_out else res


def _small_update(part, w, m, v):
    n = part.shape[1]

    def body(p_ref, w_ref, m_ref, v_ref, g_ref, d_ref, nm_ref, nv_ref, buf, send, recv):
        me, peers = _mesh_place()
        buf[me] = p_ref[...]
        sent = []
        for d, dev, flat in peers:
            cp = pltpu.make_async_remote_copy(src_ref=p_ref, dst_ref=buf.at[me], send_sem=send.at[d],
                                              recv_sem=recv.at[d], device_id=dev, device_id_type=MESH)
            cp.start()
            sent.append(cp)
        for d, dev, flat in peers:
            pltpu.make_async_remote_copy(src_ref=p_ref, dst_ref=buf.at[flat], send_sem=send.at[d],
                                         recv_sem=recv.at[d], device_id=dev, device_id_type=MESH).wait_recv()
        for cp in sent:
            cp.wait_send()
        g = buf[0]
        for p in range(1, N_DEV):
            g = g + buf[p]
        g_ref[...] = g
        d_ref[...], nm_ref[...], nv_ref[...] = _adamw(g, w_ref[...], m_ref[...], v_ref[...])

    vm = pl.BlockSpec(memory_space=pltpu.VMEM)
    return pl.pallas_call(
        body, name="small_update", in_specs=[vm] * 4, out_specs=[vm] * 4, out_shape=[_sds((1, n), F32)] * 4,
        scratch_shapes=[pltpu.VMEM((N_DEV, 1, n), F32), pltpu.SemaphoreType.DMA((N_DEV,)),
                        pltpu.SemaphoreType.DMA((N_DEV,))],
    )(part, w, m, v)


class _WInLayout:
    def __init__(self, n8, n_f, d_sb, d_fox, d):
        assert n8 % LANES == 1 and n_f < LANES and d % (N_DEV * LANES) == 0
        self.n8, self.n_f, self.d = n8, n_f, d
        self.sp = n8 // LANES
        self.wp = (n8 + 2 * LANES - 2) // LANES * LANES
        self.n_qkv = 3 * (d_sb + d_fox)
        nq, dt, tc = self.n_qkv // LANES, d // LANES, d // N_DEV // LANES
        h_sb, h_fox = d_sb // HEAD_DIM, d_fox // HEAD_DIM
        self.sources = {}
        self.part_tile = {}
        for p in range(N_DEV):
            lg = min(max(self.n_qkv + n_f - n8 * p, 0), n8)
            s1, s2 = p, p + LANES - n_f
            spans = []
            if lg > 0:
                spans.append(("a", self.sp * p, s1 // LANES, (lg + s1 - 1) // LANES))
            if lg < n8:
                spans.append(("g", self.sp * p - 1 - nq, (lg + s2) // LANES, (n8 - 1 + s2) // LANES))
            for kind, base, first, last in spans:
                for i in range(first, last + 1):
                    assert (p, i) not in self.part_tile
                    self.part_tile[(p, i)] = (kind, base + i)
                    self.sources.setdefault((kind, base + i), []).append((p, i))
        self.cat_tiles = [("a", r * h_sb + h) for h in range(h_sb) for r in range(3)]
        self.cat_tiles += [("a", 3 * h_sb + r * h_fox + h) for h in range(h_fox) for r in range(3)]
        self.cat_tiles += [("g", which * dt + j * tc + half) for j in range(N_DEV) for which in (0, 1) for half in range(tc)]
        self.cat_tiles += [("a", nq)] + [None] * (F_PAD // LANES - 1)
        self.cat_index = {key: c for c, key in enumerate(self.cat_tiles) if key is not None}

    def my_shifts(self):
        me = _flat_me()
        return me, me + LANES - self.n_f, jnp.clip(self.n_qkv + self.n_f - self.n8 * me, 0, self.n8)


def _lane_tile(i):
    return pl.ds(i * LANES, LANES)


def _w_in_shift(w_in, lay, tr=256):
    _, d, n8 = w_in.shape
    kd = d // LANES
    kt = tr // LANES
    by_col = jnp.transpose(w_in, (0, 2, 1)).reshape(n8 * kd, LANES)

    def body(w_ref, o_ref, wd_ref, buf):
        k0 = kt * pl.program_id(0)
        buf[...] = jnp.zeros_like(buf)
        for j in range(n8 // LANES):
            for kk in range(kt):
                piece = w_ref[pl.ds(j * LANES * kd + k0 + kk, LANES, stride=kd), :]
                buf[kk * LANES:(kk + 1) * LANES, j * LANES:(j + 1) * LANES] = piece.T
        first = lax.broadcasted_iota(jnp.int32, (8, LANES), 0) == 0
        for kk in range(kt):
            row = w_ref[pl.ds((n8 - 1) * kd + k0 + kk, 1), :]
            buf[kk * LANES:(kk + 1) * LANES, n8 - 1:n8 + 7] = jnp.where(first, jnp.broadcast_to(row, (8, LANES)), 0.0).T
        wd_ref[...] = ADAM_WD * buf[:, 0:n8]
        v = buf[...]
        s1, s2, lg = lay.my_shifts()
        pos = lax.broadcasted_iota(jnp.int32, v.shape, 1)
        o_ref[...] = jnp.where(pos < lg + s1, pltpu.roll(v, s1, 1),
                               jnp.where(pos >= lg + s2, pltpu.roll(v, s2, 1), 0.0)).astype(BF16)

    return pl.pallas_call(
        body, name="w_in_shift", grid=(d // tr,),
        in_specs=[pl.BlockSpec((n8 * kd, LANES), lambda i: (0, 0))],
        out_specs=[pl.BlockSpec((tr, lay.wp), lambda i: (i, 0)), pl.BlockSpec((None, tr, n8), lambda i: (0, i, 0))],
        out_shape=[_sds((d, lay.wp), BF16), _sds((1, d, n8), F32)],
        scratch_shapes=[pltpu.VMEM((tr, lay.wp), F32)],
        compiler_params=_params(("arbitrary",)),
    )(by_col)


def _w_in_build(g_in, lay, tr=256):
    d = g_in.shape[1]
    width = len(lay.cat_tiles) * LANES

    def body(g_ref, o_ref):
        for c, key in enumerate(lay.cat_tiles):
            if key is None:
                o_ref[:, _lane_tile(c)] = jnp.zeros((tr, LANES), BF16)
                continue
            (p, i), *more = lay.sources[key]
            val = g_ref[p, :, _lane_tile(i)]
            for p2, i2 in more:
                val = val + g_ref[p2, :, _lane_tile(i2)]
            o_ref[:, _lane_tile(c)] = val

    return pl.pallas_call(
        body, name="w_in_build", grid=(d // tr,),
        in_specs=[pl.BlockSpec((N_DEV, tr, lay.wp), lambda i: (0, i, 0))],
        out_specs=pl.BlockSpec((tr, width), lambda i: (i, 0)), out_shape=_sds((d, width), BF16),
        compiler_params=_params(("parallel",)),
    )(g_in)


def _w_in_grad_parts(dwq, dwgf, lay, tr=256):
    d = dwq.shape[0]
    nq = lay.n_qkv // LANES

    def body(q_ref, g_ref, o_ref):
        for p in range(N_DEV):
            for i in range(lay.wp // LANES):
                key = lay.part_tile.get((p, i))
                if key is None:
                    o_ref[p, :, _lane_tile(i)] = jnp.zeros((tr, LANES), BF16)
                    continue
                c = lay.cat_index[key]
                o_ref[p, :, _lane_tile(i)] = q_ref[:, _lane_tile(c)] if c < nq else g_ref[:, _lane_tile(c - nq)]

    return pl.pallas_call(
        body, name="w_in_grad_parts", grid=(d // tr,),
        in_specs=[pl.BlockSpec((tr, dwq.shape[1]), lambda i: (i, 0)), pl.BlockSpec((tr, dwgf.shape[1]), lambda i: (i, 0))],
        out_specs=pl.BlockSpec((N_DEV, tr, lay.wp), lambda i: (0, i, 0)), out_shape=_sds((N_DEV, d, lay.wp), BF16),
        compiler_params=_params(("parallel",)),
    )(dwq, dwgf)


def kernel(x, norm_mix_pre, norm_mix_post, w_in, b_forget, w_branch_sb, w_branch_fox, w_out, norm_ffn_pre, norm_ffn_post, w_ffn_gate, w_ffn_up, w_ffn_down, loss_target, m_norm_mix_pre, m_norm_mix_post, m_w_in, m_b_forget, m_w_branch_sb, m_w_branch_fox, m_w_out, m_norm_ffn_pre, m_norm_ffn_post, m_w_ffn_gate, m_w_ffn_up, m_w_ffn_down, v_norm_mix_pre, v_norm_mix_post, v_w_in, v_b_forget, v_w_branch_sb, v_w_branch_fox, v_w_out, v_norm_ffn_pre, v_norm_ffn_post, v_w_ffn_gate, v_w_ffn_up, v_w_ffn_down):
    xs, target = x[0], loss_target[0]
    s, d = xs.shape
    d_sb, d_fox = w_branch_sb.shape[1], w_branch_fox.shape[1]
    h_sb, h_fox = d_sb // HEAD_DIM, d_fox // HEAD_DIM
    n_f = b_forget.shape[1]
    fs = w_ffn_gate.shape[2]
    cs = d // N_DEV
    n_qkv = 3 * (d_sb + d_fox)
    n_gf = 2 * d + F_PAD
    f_blk = 2 * d // LANES
    big = (w_in, w_branch_sb, w_branch_fox, w_out, w_ffn_gate, w_ffn_up, w_ffn_down)
    big_m = (m_w_in, m_w_branch_sb, m_w_branch_fox, m_w_out, m_w_ffn_gate, m_w_ffn_up, m_w_ffn_down)
    big_v = (v_w_in, v_w_branch_sb, v_w_branch_fox, v_w_out, v_w_ffn_gate, v_w_ffn_up, v_w_ffn_down)

    lay = _WInLayout(w_in.shape[2], n_f, d_sb, d_fox, d)
    w_in_shifted, wd_w_in = _w_in_shift(w_in, lay)
    send1, recv1, lands, token = _gather_start([w_in_shifted] + [w[0].astype(BF16) for w in big[1:]])
    b_pad = jnp.pad(b_forget, ((0, 0), (0, LANES - n_f)))

    started = token[0, 0]
    u, u_t = _pre_norm(xs, norm_mix_pre, dep=token)
    weights = dict(zip(("w_in", "w_branch_sb", "w_branch_fox", "w_out", "w_ffn_gate", "w_ffn_up", "w_ffn_down"),
                       zip(big, big_m, big_v)))
    decayed = {nm: _update_prep("decay_" + nm, *[t + started for t in weights[nm]], u)
               for nm in ("w_ffn_gate", "w_ffn_up")}
    decayed["w_in"] = _update_prep("decay_w_in", wd_w_in, m_w_in + started, v_w_in + started, u, w_done=True)
    l_in, send2, recv2, token = _gather_forward("gather_in_forward", lands[0:1], 0, send1, recv1,
                                                [u] + [t[2] for t in decayed.values()])
    (g_in,) = _gather_wait("gather_in_wait", l_in, 0, recv1, send2, recv2, token)
    w_cat = _w_in_build(g_in, lay)
    qkv = _mm_plain("proj_qkv", "nn", u, w_cat, BF16, n=n_qkv)
    gf = _mm_plain("proj_gates", "nn", u, w_cat, F32, n_off=n_qkv, n=n_gf)
    cum_col, cum_row = _forget_fwd(gf, b_pad, f_blk)
    o_sb, o_sb_t, tot = _sb_fwd(qkv, h_sb)
    l_mid, send2, recv2, token = _gather_forward("gather_mid_forward", lands[1:4], 1, send1, recv1, [o_sb])
    o_fx, o_fx_t, o_fx32, lse = _fox_fwd(qkv, cum_col, cum_row, h_fox, h_sb, token)
    g_sb, g_fx, g_out = _gather_wait("gather_mid_wait", l_mid, 1, recv1, send2, recv2, o_fx)
    w_out_full = g_out.reshape(d, d)
    merged, merged_t, a_sb, a_fx = _branch_merge(o_sb, o_fx, g_sb, g_fx, gf, o_fx)
    l_ffn, send2, recv2, token = _gather_forward("gather_ffn_forward", lands[4:6], 4, send1, recv1, [merged])
    mix = _mm_plain("out_proj", "nn", merged, w_out_full, F32, dep=token)
    h1, u2, u2_t = _mid_norms(xs, mix, norm_mix_post, norm_ffn_pre)
    g_gate, g_up = _gather_wait("gather_ffn_wait", l_ffn, 4, recv1, send2, recv2, u2)
    l_down, send2, recv2, token = _gather_forward("gather_down_forward", lands[6:7], 6, send1, recv1, [u2])
    gate, up, act, act_t = _ffn_up(u2, g_gate, g_up, token)
    (g_down,) = _gather_wait("gather_down_wait", l_down, 6, recv1, send2, recv2, act)
    tm, tn = _tile(s, 1024), _tile(d, 1024)
    tw = _tile(d, 2048)
    ff = _matmul("ffn_down", "nn",
                 [(act, pl.BlockSpec((None, tm, fs), lambda i, j, k: (k, i, 0)),
                   g_down, pl.BlockSpec((None, fs, tw), lambda i, j, k: (k, 0, j)))],
                 (s // tm, d // tw, N_DEV), (tm, tw), _sds((s, d), F32), pl.BlockSpec((tm, tw), lambda i, j, k: (i, j)))
    loss_part, dy, dff, dg_ffn_post = _loss_head(h1, ff, target, norm_ffn_post)

    dgate, dup = _ffn_down_bwd(dff, g_down, gate, up)
    dw_down = _matmul("dw_down", "nn",
                      [(act_t, pl.BlockSpec((None, fs, s), lambda j, n, k: (j, 0, 0)),
                        dff, pl.BlockSpec((s, tn), lambda j, n, k: (0, n)))],
                      (N_DEV, d // tn, 1), (fs, tn), _sds((N_DEV, fs, d), BF16),
                      pl.BlockSpec((None, fs, tn), lambda j, n, k: (j, 0, n)))

    def dw_up(name, dact):
        return _matmul(name, "nn",
                       [(u2_t, pl.BlockSpec((tn, s), lambda j, i, k: (i, 0)),
                         dact, pl.BlockSpec((None, s, fs), lambda j, i, k: (j, 0, 0)))],
                       (N_DEV, d // tn, 1), (tn, fs), _sds((N_DEV, d, fs), BF16),
                       pl.BlockSpec((None, tn, fs), lambda j, i, k: (j, i, 0)))

    dw_gate, dw_upw = dw_up("dw_gate", dgate), dw_up("dw_up", dup)
    rs_ffn = _scatter_pairs("ffn", [dw_gate, dw_upw, dw_down])
    a_spec = pl.BlockSpec((None, tm, fs), lambda i, j, k: (k, i, 0))
    b_spec = pl.BlockSpec((None, tw, fs), lambda i, j, k: (k, j, 0))
    du2 = _matmul("du2", "nt", [(dgate, a_spec, g_gate, b_spec), (dup, a_spec, g_up, b_spec)],
                  (s // tm, d // tw, N_DEV), (tm, tw), _sds((s, d), F32), pl.BlockSpec((tm, tw), lambda i, j, k: (i, j)),
                  dep=rs_ffn[4])
    rs_ffn = _scatter_chips("ffn", rs_ffn, du2)
    dh1, dmix, dg_ffn_pre, dg_mix_post = _mid_norms_bwd(dy, du2, h1, mix, norm_ffn_pre, norm_mix_post)

    da_sb, da_fx, dgf = _merge_bwd(dmix, w_out_full, gf, a_sb, a_fx, dep=rs_ffn[4])
    dw_out = _mm_plain("dw_out", "nn", merged_t, dmix, BF16).reshape(N_DEV, cs, d)

    def branch_bwd(tag, da, w_b, o_t, width):
        tb = _tile(width, 1024)
        do = _matmul("do_" + tag, "nt",
                     [(da, pl.BlockSpec((tm, cs), lambda i, j, k: (i, k)),
                       w_b, pl.BlockSpec((None, tb, cs), lambda i, j, k: (k, j, 0)))],
                     (s // tm, width // tb, N_DEV), (tm, tb), _sds((s, width), BF16),
                     pl.BlockSpec((tm, tb), lambda i, j, k: (i, j)))
        dw = _matmul("dw_" + tag, "nn",
                     [(o_t, pl.BlockSpec((width, s), lambda j, i, k: (0, 0)),
                       da, pl.BlockSpec((s, cs), lambda j, i, k: (0, j)))],
                     (N_DEV, 1, 1), (width, cs), _sds((N_DEV, width, cs), BF16),
                     pl.BlockSpec((None, width, cs), lambda j, i, k: (j, 0, 0)))
        return do, dw

    do_sb, dw_sb = branch_bwd("sb", da_sb, g_sb, o_sb_t, d_sb)
    do_fx, dw_fx = branch_bwd("fox", da_fx, g_fx, o_fx_t, d_fox)

    rs_mid = _scatter_pairs("mid", [dw_sb, dw_fx, dw_out])

    dqkv = _sb_bwd(qkv, do_sb, tot, h_sb, rs_mid[4])
    rs_mid = _scatter_chips("mid", rs_mid, dqkv)
    dqkv, dcum = _fox_bwd(dqkv, qkv, do_fx, o_fx32, lse, cum_col, cum_row, h_fox, h_sb, rs_mid[4])
    dgf, db_part = _forget_bwd(dgf, dcum, gf, b_pad, f_blk)
    dw_in = _w_in_grad_parts(_mm_plain("dw_qkv", "nn", u_t, dqkv, BF16), _mm_plain("dw_gates", "nn", u_t, dgf, BF16), lay)
    rs_in = _scatter_pairs("in", [dw_in])
    du = _mm_plain("du_qkv", "nt", dqkv, w_cat, F32, tn=1024, dep=rs_in[4])
    rs_in = _scatter_chips("in", rs_in, du)
    du = _mm_plain("du_gates", "nt", dgf, w_cat, F32, tn=1024, k_off=n_qkv, init=du, dep=rs_in[4])
    dx, dg_mix_pre = _pre_norm_bwd(dh1, du, xs, norm_mix_pre)

    upd = {}

    def update_group(tag, rs, names, after):
        parts = _scatter_end(tag, rs, after)
        for nm, p in zip(names, parts):
            w, m, v = decayed.get(nm, weights[nm])
            upd[nm] = _update("update_" + nm, p, w, m, v, layout=lay if nm == "w_in" else None, decayed=nm in decayed,
                              transposed_out=nm in ("w_in", "w_ffn_gate", "w_ffn_up"))

    update_group("ffn", rs_ffn, ("w_ffn_gate", "w_ffn_up", "w_ffn_down"), [dx])
    update_group("mid", rs_mid, ("w_branch_sb", "w_branch_fox", "w_out"), [upd[nm][3] for nm in ("w_ffn_gate", "w_ffn_up", "w_ffn_down")])
    update_group("in", rs_in, ("w_in",), [upd[nm][3] for nm in ("w_branch_sb", "w_branch_fox", "w_out")])

    small = ((norm_mix_pre, m_norm_mix_pre, v_norm_mix_pre), (norm_mix_post, m_norm_mix_post, v_norm_mix_post),
             (norm_ffn_pre, m_norm_ffn_pre, v_norm_ffn_pre), (norm_ffn_post, m_norm_ffn_post, v_norm_ffn_post))
    pad_f = ((0, 0), (0, LANES - n_f))
    cat = lambda i: jnp.concatenate([t[i] for t in small] + [jnp.pad((b_forget, m_b_forget, v_b_forget)[i], pad_f)], axis=1)
    sm = _small_update(jnp.concatenate([dg_mix_pre, dg_mix_post, dg_ffn_pre, dg_ffn_post, db_part], axis=1),
                       cat(0), cat(1), cat(2))
    for i, nm in enumerate(("norm_mix_pre", "norm_mix_post", "norm_ffn_pre", "norm_ffn_post")):
        upd[nm] = [o[:, i * d:(i + 1) * d] for o in sm]
    upd["b_forget"] = [o[:, 4 * d:4 * d + n_f] for o in sm]

    loss = lax.psum(loss_part[0, 0], ("x", "y", "c"))
    order = ("norm_mix_pre", "norm_mix_post", "w_in", "b_forget", "w_branch_sb", "w_branch_fox", "w_out",
             "norm_ffn_pre", "norm_ffn_post", "w_ffn_gate", "w_ffn_up", "w_ffn_down")
    return (loss, dx[None]) + tuple(upd[nm][i] for i in range(4) for nm in order)
```
